```python
import jax, jax.numpy as jnp
from jax import lax
import numpy as np

D_MODEL = 1024
BATCH = 8
SEQ = 4096
DEPTH = 1

CHUNK = 64
D_MIX = D_MODEL
DN_HEAD_DIM = 128
DN_WIDTH = D_MIX // 2
DN_HEADS = DN_WIDTH // DN_HEAD_DIM
CONV_K = 4
SG_WIDTH = D_MIX - DN_WIDTH
SG_GROUPS = 4
SG_DIM = SG_WIDTH // SG_GROUPS
SG_BLOCK = 128
D_FF = 2816
FFN_CONV = 3
EPS = 1e-6
PROJ_COLS = 4 * DN_WIDTH + 2 * SG_WIDTH + 2 * DN_HEADS

kernel_name = "hybrid_gdn_gmlp_convffn_block"


def rmsnorm(x, g):
    xf = x.astype(jnp.float32)
    y = xf * lax.rsqrt(jnp.mean(xf * xf, axis=-1, keepdims=True) + EPS)
    return (y * g.astype(jnp.float32)).astype(x.dtype)


def l2norm(x):
    return x * lax.rsqrt(jnp.sum(x * x, axis=-1, keepdims=True) + EPS)


def causal_dwconv(x, w):
    K = w.shape[0]
    T = x.shape[1]
    xp = jnp.pad(x, ((0, 0), (K - 1, 0), (0, 0)))
    out = xp[:, 0:T] * w[0]
    for k in range(1, K):
        out = out + xp[:, k:k + T] * w[k]
    return out


def gated_delta_chunked(q, k, v, g, beta):
    B, T, H, D = q.shape
    N = T // CHUNK
    C = CHUNK
    ch = lambda a: a.reshape(B, N, C, H, D).transpose(0, 3, 1, 2, 4)
    q, k, v = ch(q), ch(k), ch(v)
    g = g.reshape(B, N, C, H).transpose(0, 3, 1, 2)
    beta = beta.reshape(B, N, C, H).transpose(0, 3, 1, 2)
    Gc = jnp.cumsum(g, axis=-1)
    incl = jnp.tril(jnp.ones((C, C), dtype=bool))
    strict = jnp.tril(jnp.ones((C, C), dtype=bool), -1)
    decay = jnp.exp(jnp.where(incl, Gc[..., :, None] - Gc[..., None, :], -jnp.inf))
    k_beta = k * beta[..., None]
    v_beta = v * beta[..., None]
    L = jnp.where(strict, jnp.einsum('bhnid,bhnjd->bhnij', k_beta, k) * decay, 0.0)
    eye = jnp.eye(C, dtype=jnp.float32)
    rhs = jnp.concatenate([v_beta, k_beta * jnp.exp(Gc)[..., None]], axis=-1)
    sol = lax.linalg.triangular_solve(eye + L, rhs, left_side=True, lower=True,
                                      transpose_a=False, conjugate_a=False, unit_diagonal=True)
    value, k_cumdecay = sol[..., :D], sol[..., D:]
    attn_intra = jnp.einsum('bhnid,bhnjd->bhnij', q, k) * decay
    q_decay = q * jnp.exp(Gc)[..., None]
    k_tail = k * jnp.exp(Gc[..., -1:] - Gc)[..., None]
    chunk_decay = jnp.exp(Gc[..., -1])

    def step(S, inp):
        a_i, val_i, kcd_i, qd_i, kt_i, cd_i = inp
        v_new = val_i - jnp.einsum('bhcd,bhde->bhce', kcd_i, S)
        o = jnp.einsum('bhcd,bhde->bhce', qd_i, S) + jnp.einsum('bhij,bhje->bhie', a_i, v_new)
        S = S * cd_i[..., None, None] + jnp.einsum('bhcd,bhce->bhde', kt_i, v_new)
        return S, o

    mv = lambda a: jnp.moveaxis(a, 2, 0)
    S0 = jnp.zeros((B, H, D, D), jnp.float32)
    _, o = lax.scan(step, S0, (mv(attn_intra), mv(value), mv(k_cumdecay), mv(q_decay),
                               mv(k_tail), mv(chunk_decay)))
    return o.transpose(1, 0, 3, 2, 4).reshape(B, T, H, D)


def deltanet_group(q_raw, k_raw, v_raw, gate, b_raw, a_raw, conv_w, a_log, dt_bias, norm_g):
    B, T, _ = q_raw.shape
    qkv = jax.nn.silu(causal_dwconv(jnp.concatenate([q_raw, k_raw, v_raw], axis=-1), conv_w))
    qkv = qkv.astype(jnp.float32).reshape(B, T, 3, DN_HEADS, DN_HEAD_DIM)
    q = l2norm(qkv[:, :, 0]) * (DN_HEAD_DIM ** -0.5)
    k = l2norm(qkv[:, :, 1])
    v = qkv[:, :, 2]
    beta = jax.nn.sigmoid(b_raw.astype(jnp.float32))
    g = -jnp.exp(a_log.astype(jnp.float32)) * jax.nn.softplus(a_raw.astype(jnp.float32) + dt_bias.astype(jnp.float32))
    o = gated_delta_chunked(q, k, v, g, beta)
    o = rmsnorm(o, norm_g).reshape(B, T, DN_WIDTH)
    return (o * jax.nn.silu(gate.astype(jnp.float32))).astype(q_raw.dtype)


def gmlp_group(u_raw, v_raw, norm_g, w_s, b_s):
    B, T, _ = u_raw.shape
    u = jax.nn.gelu(u_raw)
    v = jax.nn.gelu(v_raw).reshape(B, T, SG_GROUPS, SG_DIM)
    v = rmsnorm(v, norm_g.reshape(SG_GROUPS, SG_DIM))
    v = v.reshape(B, T // SG_BLOCK, SG_BLOCK, SG_GROUPS, SG_DIM)
    pos_chunk = jnp.arange(SG_BLOCK) // CHUNK
    mask = pos_chunk[None, :] <= pos_chunk[:, None]
    w_m = jnp.where(mask, w_s, 0.0).astype(v.dtype)
    s = jnp.einsum('gij,bnjgd->bnigd', w_m, v) + b_s.T[None, None, :, :, None]
    return u * s.reshape(B, T, SG_WIDTH)


def _fwd_setup_inputs(seed: int = 0) -> dict:
    key = jax.random.key(seed)
    ks = jax.random.split(key, 20)
    nrm = lambda k, shape, s: jax.random.normal(k, shape, jnp.float32) * s
    dt = jnp.exp(jax.random.uniform(ks[5], (DEPTH, DN_HEADS), jnp.float32,
                                    np.log(1e-3), np.log(1e-1)))
    return {
        "x": nrm(ks[0], (BATCH, SEQ, D_MODEL), 1.0),
        "attn_norm_g": 1.0 + nrm(ks[1], (DEPTH, D_MODEL), 0.02),
        "w_in": nrm(ks[2], (DEPTH, D_MODEL, PROJ_COLS), D_MODEL ** -0.5),
        "dn_conv_w": nrm(ks[3], (DEPTH, CONV_K, 3 * DN_WIDTH), CONV_K ** -0.5),
        "dn_a_log": jnp.log(jax.random.uniform(ks[4], (DEPTH, DN_HEADS), jnp.float32, 1.0, 16.0)),
        "dn_dt_bias": dt + jnp.log(-jnp.expm1(-dt)),
        "dn_out_norm_g": 1.0 + nrm(ks[6], (DEPTH, DN_HEAD_DIM), 0.02),
        "sg_norm_g": 1.0 + nrm(ks[7], (DEPTH, SG_WIDTH), 0.02),
        "sg_w": nrm(ks[8], (DEPTH, SG_GROUPS, SG_BLOCK, SG_BLOCK), SG_BLOCK ** -0.5),
        "sg_b": 1.0 + nrm(ks[9], (DEPTH, SG_GROUPS, SG_BLOCK), 0.01),
        "w_out": nrm(ks[10], (DEPTH, D_MIX, D_MODEL), D_MIX ** -0.5),
        "ffn_norm_g": 1.0 + nrm(ks[11], (DEPTH, D_MODEL), 0.02),
        "w_up": nrm(ks[12], (DEPTH, D_MODEL, 2 * D_FF), D_MODEL ** -0.5),
        "ffn_conv_w": nrm(ks[13], (DEPTH, FFN_CONV, 2 * D_FF), FFN_CONV ** -0.5),
        "ffn_conv_b": nrm(ks[14], (DEPTH, 2 * D_FF), 0.01),
        "w_down": nrm(ks[15], (DEPTH, D_FF, D_MODEL), D_FF ** -0.5),
        "final_norm_g": 1.0 + nrm(ks[16], (D_MODEL,), 0.02),
    }


def _fwd_reference(x, attn_norm_g, w_in, dn_conv_w, dn_a_log, dn_dt_bias, dn_out_norm_g,
              sg_norm_g, sg_w, sg_b, w_out, ffn_norm_g, w_up, ffn_conv_w, ffn_conv_b,
              w_down, final_norm_g):
    split_at = np.cumsum([DN_WIDTH] * 4 + [SG_WIDTH] * 2 + [DN_HEADS]).tolist()
    for l in range(DEPTH):
        h = rmsnorm(x, attn_norm_g[l])
        p = jnp.einsum('btd,dc->btc', h, w_in[l])
        q_raw, k_raw, v_raw, gate, u_raw, vg_raw, b_raw, a_raw = jnp.split(p, split_at, axis=-1)
        o_dn = deltanet_group(q_raw, k_raw, v_raw, gate, b_raw, a_raw, dn_conv_w[l],
                              dn_a_log[l], dn_dt_bias[l], dn_out_norm_g[l])
        o_sg = gmlp_group(u_raw, vg_raw, sg_norm_g[l], sg_w[l], sg_b[l])
        mix = jnp.concatenate([o_dn, o_sg], axis=-1)
        x = x + jnp.einsum('btc,cd->btd', mix, w_out[l])
        h = rmsnorm(x, ffn_norm_g[l])
        up = jnp.einsum('btd,df->btf', h, w_up[l])
        up = causal_dwconv(up, ffn_conv_w[l]) + ffn_conv_b[l]
        g_ff, v_ff = jnp.split(up, 2, axis=-1)
        x = x + jnp.einsum('btf,fd->btd', jax.nn.silu(g_ff) * v_ff, w_down[l])
    return rmsnorm(x, final_norm_g)


import jax as _jax
import jax.numpy as _jnp

TWIN_FORMAT = 'train_step'
FWD_PARAMS = ['x', 'attn_norm_g', 'w_in', 'dn_conv_w', 'dn_a_log', 'dn_dt_bias', 'dn_out_norm_g', 'sg_norm_g', 'sg_w', 'sg_b', 'w_out', 'ffn_norm_g', 'w_up', 'ffn_conv_w', 'ffn_conv_b', 'w_down', 'final_norm_g']
TWIN_WEIGHTS = ['attn_norm_g', 'w_in', 'dn_conv_w', 'dn_a_log', 'dn_dt_bias', 'dn_out_norm_g', 'sg_norm_g', 'sg_w', 'sg_b', 'w_out', 'ffn_norm_g', 'w_up', 'ffn_conv_w', 'ffn_conv_b', 'w_down', 'final_norm_g']
TWIN_DIFF_INPUT = 'x'
TWIN_INPUTS = ['x', 'attn_norm_g', 'w_in', 'dn_conv_w', 'dn_a_log', 'dn_dt_bias', 'dn_out_norm_g', 'sg_norm_g', 'sg_w', 'sg_b', 'w_out', 'ffn_norm_g', 'w_up', 'ffn_conv_w', 'ffn_conv_b', 'w_down', 'final_norm_g', 'loss_target', 'm_attn_norm_g', 'm_w_in', 'm_dn_conv_w', 'm_dn_a_log', 'm_dn_dt_bias', 'm_dn_out_norm_g', 'm_sg_norm_g', 'm_sg_w', 'm_sg_b', 'm_w_out', 'm_ffn_norm_g', 'm_w_up', 'm_ffn_conv_w', 'm_ffn_conv_b', 'm_w_down', 'm_final_norm_g', 'v_attn_norm_g', 'v_w_in', 'v_dn_conv_w', 'v_dn_a_log', 'v_dn_dt_bias', 'v_dn_out_norm_g', 'v_sg_norm_g', 'v_sg_w', 'v_sg_b', 'v_w_out', 'v_ffn_norm_g', 'v_w_up', 'v_ffn_conv_w', 'v_ffn_conv_b', 'v_w_down', 'v_final_norm_g']
TWIN_OUTPUTS = ['loss', 'grad_x', 'grad_attn_norm_g', 'grad_w_in', 'grad_dn_conv_w', 'grad_dn_a_log', 'grad_dn_dt_bias', 'grad_dn_out_norm_g', 'grad_sg_norm_g', 'grad_sg_w', 'grad_sg_b', 'grad_w_out', 'grad_ffn_norm_g', 'grad_w_up', 'grad_ffn_conv_w', 'grad_ffn_conv_b', 'grad_w_down', 'grad_final_norm_g', 'delta_attn_norm_g', 'delta_w_in', 'delta_dn_conv_w', 'delta_dn_a_log', 'delta_dn_dt_bias', 'delta_dn_out_norm_g', 'delta_sg_norm_g', 'delta_sg_w', 'delta_sg_b', 'delta_w_out', 'delta_ffn_norm_g', 'delta_w_up', 'delta_ffn_conv_w', 'delta_ffn_conv_b', 'delta_w_down', 'delta_final_norm_g', 'new_m_attn_norm_g', 'new_m_w_in', 'new_m_dn_conv_w', 'new_m_dn_a_log', 'new_m_dn_dt_bias', 'new_m_dn_out_norm_g', 'new_m_sg_norm_g', 'new_m_sg_w', 'new_m_sg_b', 'new_m_w_out', 'new_m_ffn_norm_g', 'new_m_w_up', 'new_m_ffn_conv_w', 'new_m_ffn_conv_b', 'new_m_w_down', 'new_m_final_norm_g', 'new_v_attn_norm_g', 'new_v_w_in', 'new_v_dn_conv_w', 'new_v_dn_a_log', 'new_v_dn_dt_bias', 'new_v_dn_out_norm_g', 'new_v_sg_norm_g', 'new_v_sg_w', 'new_v_sg_b', 'new_v_w_out', 'new_v_ffn_norm_g', 'new_v_w_up', 'new_v_ffn_conv_w', 'new_v_ffn_conv_b', 'new_v_w_down', 'new_v_final_norm_g']
TWIN_LEAF_KINDS = {'loss': 'loss', 'grad_x': 'grad_x', 'grad_attn_norm_g': 'grad_w', 'grad_w_in': 'grad_w', 'grad_dn_conv_w': 'grad_w', 'grad_dn_a_log': 'grad_w', 'grad_dn_dt_bias': 'grad_w', 'grad_dn_out_norm_g': 'grad_w', 'grad_sg_norm_g': 'grad_w', 'grad_sg_w': 'grad_w', 'grad_sg_b': 'grad_w', 'grad_w_out': 'grad_w', 'grad_ffn_norm_g': 'grad_w', 'grad_w_up': 'grad_w', 'grad_ffn_conv_w': 'grad_w', 'grad_ffn_conv_b': 'grad_w', 'grad_w_down': 'grad_w', 'grad_final_norm_g': 'grad_w', 'delta_attn_norm_g': 'delta_w', 'delta_w_in': 'delta_w', 'delta_dn_conv_w': 'delta_w', 'delta_dn_a_log': 'delta_w', 'delta_dn_dt_bias': 'delta_w', 'delta_dn_out_norm_g': 'delta_w', 'delta_sg_norm_g': 'delta_w', 'delta_sg_w': 'delta_w', 'delta_sg_b': 'delta_w', 'delta_w_out': 'delta_w', 'delta_ffn_norm_g': 'delta_w', 'delta_w_up': 'delta_w', 'delta_ffn_conv_w': 'delta_w', 'delta_ffn_conv_b': 'delta_w', 'delta_w_down': 'delta_w', 'delta_final_norm_g': 'delta_w', 'new_m_attn_norm_g': 'new_m', 'new_m_w_in': 'new_m', 'new_m_dn_conv_w': 'new_m', 'new_m_dn_a_log': 'new_m', 'new_m_dn_dt_bias': 'new_m', 'new_m_dn_out_norm_g': 'new_m', 'new_m_sg_norm_g': 'new_m', 'new_m_sg_w': 'new_m', 'new_m_sg_b': 'new_m', 'new_m_w_out': 'new_m', 'new_m_ffn_norm_g': 'new_m', 'new_m_w_up': 'new_m', 'new_m_ffn_conv_w': 'new_m', 'new_m_ffn_conv_b': 'new_m', 'new_m_w_down': 'new_m', 'new_m_final_norm_g': 'new_m', 'new_v_attn_norm_g': 'new_v', 'new_v_w_in': 'new_v', 'new_v_dn_conv_w': 'new_v', 'new_v_dn_a_log': 'new_v', 'new_v_dn_dt_bias': 'new_v', 'new_v_dn_out_norm_g': 'new_v', 'new_v_sg_norm_g': 'new_v', 'new_v_sg_w': 'new_v', 'new_v_sg_b': 'new_v', 'new_v_w_out': 'new_v', 'new_v_ffn_norm_g': 'new_v', 'new_v_w_up': 'new_v', 'new_v_ffn_conv_w': 'new_v', 'new_v_ffn_conv_b': 'new_v', 'new_v_w_down': 'new_v', 'new_v_final_norm_g': 'new_v'}


def _forward(args):
    return _fwd_reference(*[args[k] for k in FWD_PARAMS])


def _output_shape():
    def fwd():
        inp = _fwd_setup_inputs(0)
        return _fwd_reference(*[inp[k] for k in FWD_PARAMS])
    out = _jax.eval_shape(fwd)
    return out.shape, out.dtype

N_MICROBATCH = 1
ADAM_LR = 0.001
ADAM_B1 = 0.9
ADAM_B2 = 0.999
ADAM_EPS = 1e-08
ADAM_WD = 0.01
ADAM_STEP = 10
PER_EXAMPLE_BATCH_AXIS = {'x': 0, 'loss_target': 0}
SHARED_INPUTS = []
_WEIGHT_DTYPES = {'attn_norm_g': _jnp.float32, 'w_in': _jnp.float32, 'dn_conv_w': _jnp.float32, 'dn_a_log': _jnp.float32, 'dn_dt_bias': _jnp.float32, 'dn_out_norm_g': _jnp.float32, 'sg_norm_g': _jnp.float32, 'sg_w': _jnp.float32, 'sg_b': _jnp.float32, 'w_out': _jnp.float32, 'ffn_norm_g': _jnp.float32, 'w_up': _jnp.float32, 'ffn_conv_w': _jnp.float32, 'ffn_conv_b': _jnp.float32, 'w_down': _jnp.float32, 'final_norm_g': _jnp.float32}
MOMENT_SCALE = {'attn_norm_g': 1.652387e-01, 'w_in': 9.344056e-02, 'dn_conv_w': 6.937990e-02, 'dn_a_log': 5.916385e-01, 'dn_dt_bias': 5.671997e-01, 'dn_out_norm_g': 1.882234e-01, 'sg_norm_g': 9.428583e-02, 'sg_w': 8.824608e-02, 'sg_b': 1.007074e-01, 'w_out': 1.181142e-01, 'ffn_norm_g': 1.186214e-01, 'w_up': 4.916990e-02, 'ffn_conv_w': 5.033130e-02, 'ffn_conv_b': 5.101650e-02, 'w_down': 8.011413e-02, 'final_norm_g': 3.203276e+01}


def _to_microbatches(a, axis):
    t = _jnp.moveaxis(a, axis, 0)
    t = t.reshape((N_MICROBATCH, t.shape[0] // N_MICROBATCH) + t.shape[1:])
    return _jnp.moveaxis(t, 1, axis + 1)


def setup_inputs(seed: int = 0) -> dict:
    inp = _fwd_setup_inputs(seed)
    key = _jax.random.fold_in(_jax.random.key(seed), 7919)
    shape, _ = _output_shape()
    out = dict(inp)
    out["loss_target"] = _jax.random.normal(_jax.random.fold_in(key, 0), shape, _jnp.float32)
    for i, name in enumerate(TWIN_WEIGHTS):
        w = inp[name].astype(_jnp.float32)
        if MOMENT_SCALE is None:
            s = _jnp.sqrt(_jnp.mean(_jnp.square(w)) + 1e-30)
        else:
            s = MOMENT_SCALE[name]
        km, kv = _jax.random.split(_jax.random.fold_in(key, i + 1))
        out[name] = w
        out["m_" + name] = s * _jax.random.normal(km, w.shape, _jnp.float32)
        out["v_" + name] = (s * s) * _jax.random.uniform(kv, w.shape, _jnp.float32, 0.5, 1.5)
    if N_MICROBATCH > 1:
        for name, axis in PER_EXAMPLE_BATCH_AXIS.items():
            out[name] = _to_microbatches(out[name], axis)
    return {'x': out['x'], 'attn_norm_g': out['attn_norm_g'], 'w_in': out['w_in'], 'dn_conv_w': out['dn_conv_w'], 'dn_a_log': out['dn_a_log'], 'dn_dt_bias': out['dn_dt_bias'], 'dn_out_norm_g': out['dn_out_norm_g'], 'sg_norm_g': out['sg_norm_g'], 'sg_w': out['sg_w'], 'sg_b': out['sg_b'], 'w_out': out['w_out'], 'ffn_norm_g': out['ffn_norm_g'], 'w_up': out['w_up'], 'ffn_conv_w': out['ffn_conv_w'], 'ffn_conv_b': out['ffn_conv_b'], 'w_down': out['w_down'], 'final_norm_g': out['final_norm_g'], 'loss_target': out['loss_target'], 'm_attn_norm_g': out['m_attn_norm_g'], 'm_w_in': out['m_w_in'], 'm_dn_conv_w': out['m_dn_conv_w'], 'm_dn_a_log': out['m_dn_a_log'], 'm_dn_dt_bias': out['m_dn_dt_bias'], 'm_dn_out_norm_g': out['m_dn_out_norm_g'], 'm_sg_norm_g': out['m_sg_norm_g'], 'm_sg_w': out['m_sg_w'], 'm_sg_b': out['m_sg_b'], 'm_w_out': out['m_w_out'], 'm_ffn_norm_g': out['m_ffn_norm_g'], 'm_w_up': out['m_w_up'], 'm_ffn_conv_w': out['m_ffn_conv_w'], 'm_ffn_conv_b': out['m_ffn_conv_b'], 'm_w_down': out['m_w_down'], 'm_final_norm_g': out['m_final_norm_g'], 'v_attn_norm_g': out['v_attn_norm_g'], 'v_w_in': out['v_w_in'], 'v_dn_conv_w': out['v_dn_conv_w'], 'v_dn_a_log': out['v_dn_a_log'], 'v_dn_dt_bias': out['v_dn_dt_bias'], 'v_dn_out_norm_g': out['v_dn_out_norm_g'], 'v_sg_norm_g': out['v_sg_norm_g'], 'v_sg_w': out['v_sg_w'], 'v_sg_b': out['v_sg_b'], 'v_w_out': out['v_w_out'], 'v_ffn_norm_g': out['v_ffn_norm_g'], 'v_w_up': out['v_w_up'], 'v_ffn_conv_w': out['v_ffn_conv_w'], 'v_ffn_conv_b': out['v_ffn_conv_b'], 'v_w_down': out['v_w_down'], 'v_final_norm_g': out['v_final_norm_g']}


def _loss(weights, diff, rest, loss_target):
    with _jax.named_scope("forward"):
        args = {**rest, TWIN_DIFF_INPUT: diff, **{k: w.astype(_WEIGHT_DTYPES[k]) for k, w in weights.items()}}
        y = _forward(args)
    with _jax.named_scope("loss_head"):
        err = _jnp.square(y.astype(_jnp.float32) - loss_target)
        return 0.5 * _jnp.sum(_jnp.mean(err, axis=-1)) if err.ndim else 0.5 * err


def _adamw(w, g, m, v):
    m = ADAM_B1 * m + (1.0 - ADAM_B1) * g
    v = ADAM_B2 * v + (1.0 - ADAM_B2) * _jnp.square(g)
    m_hat = m / (1.0 - ADAM_B1 ** ADAM_STEP)
    v_hat = v / (1.0 - ADAM_B2 ** ADAM_STEP)
    delta = -ADAM_LR * (m_hat / (_jnp.sqrt(v_hat) + ADAM_EPS) + ADAM_WD * w)
    return delta, m, v


def reference(x, attn_norm_g, w_in, dn_conv_w, dn_a_log, dn_dt_bias, dn_out_norm_g, sg_norm_g, sg_w, sg_b, w_out, ffn_norm_g, w_up, ffn_conv_w, ffn_conv_b, w_down, final_norm_g, loss_target, m_attn_norm_g, m_w_in, m_dn_conv_w, m_dn_a_log, m_dn_dt_bias, m_dn_out_norm_g, m_sg_norm_g, m_sg_w, m_sg_b, m_w_out, m_ffn_norm_g, m_w_up, m_ffn_conv_w, m_ffn_conv_b, m_w_down, m_final_norm_g, v_attn_norm_g, v_w_in, v_dn_conv_w, v_dn_a_log, v_dn_dt_bias, v_dn_out_norm_g, v_sg_norm_g, v_sg_w, v_sg_b, v_w_out, v_ffn_norm_g, v_w_up, v_ffn_conv_w, v_ffn_conv_b, v_w_down, v_final_norm_g):
    given = dict(x=x, attn_norm_g=attn_norm_g, w_in=w_in, dn_conv_w=dn_conv_w, dn_a_log=dn_a_log, dn_dt_bias=dn_dt_bias, dn_out_norm_g=dn_out_norm_g, sg_norm_g=sg_norm_g, sg_w=sg_w, sg_b=sg_b, w_out=w_out, ffn_norm_g=ffn_norm_g, w_up=w_up, ffn_conv_w=ffn_conv_w, ffn_conv_b=ffn_conv_b, w_down=w_down, final_norm_g=final_norm_g, loss_target=loss_target, m_attn_norm_g=m_attn_norm_g, m_w_in=m_w_in, m_dn_conv_w=m_dn_conv_w, m_dn_a_log=m_dn_a_log, m_dn_dt_bias=m_dn_dt_bias, m_dn_out_norm_g=m_dn_out_norm_g, m_sg_norm_g=m_sg_norm_g, m_sg_w=m_sg_w, m_sg_b=m_sg_b, m_w_out=m_w_out, m_ffn_norm_g=m_ffn_norm_g, m_w_up=m_w_up, m_ffn_conv_w=m_ffn_conv_w, m_ffn_conv_b=m_ffn_conv_b, m_w_down=m_w_down, m_final_norm_g=m_final_norm_g, v_attn_norm_g=v_attn_norm_g, v_w_in=v_w_in, v_dn_conv_w=v_dn_conv_w, v_dn_a_log=v_dn_a_log, v_dn_dt_bias=v_dn_dt_bias, v_dn_out_norm_g=v_dn_out_norm_g, v_sg_norm_g=v_sg_norm_g, v_sg_w=v_sg_w, v_sg_b=v_sg_b, v_w_out=v_w_out, v_ffn_norm_g=v_ffn_norm_g, v_w_up=v_w_up, v_ffn_conv_w=v_ffn_conv_w, v_ffn_conv_b=v_ffn_conv_b, v_w_down=v_w_down, v_final_norm_g=v_final_norm_g)
    weights = {n: given[n] for n in TWIN_WEIGHTS}
    shared = {n: given[n] for n in SHARED_INPUTS}
    per_example = {n: given[n] for n in ['x']}
    grad_fn = _jax.value_and_grad(_loss, argnums=(0, 1))

    def one_microbatch(ex, loss_target):
        ex = dict(ex)
        diff = ex.pop(TWIN_DIFF_INPUT)
        return grad_fn(weights, diff, {**shared, **ex}, loss_target)

    if N_MICROBATCH == 1:
        loss, (grad_w, grad_x) = one_microbatch(per_example, given["loss_target"])
    else:
        def body(carry, xs):
            loss_sum, grad_sum = carry
            l_k, (gw_k, gx_k) = one_microbatch(xs[0], xs[1])
            with _jax.named_scope("update"):
                return (loss_sum + l_k, _jax.tree.map(_jnp.add, grad_sum, gw_k)), gx_k

        init = (_jnp.zeros((), _jnp.float32), _jax.tree.map(_jnp.zeros_like, weights))
        (loss, grad_w), grad_x = _jax.lax.scan(body, init, (per_example, given["loss_target"]))
    with _jax.named_scope("update"):
        delta_w, new_m, new_v = {}, {}, {}
        for n in TWIN_WEIGHTS:
            delta_w[n], new_m[n], new_v[n] = _adamw(weights[n], grad_w[n], given["m_" + n], given["v_" + n])
    return (loss, grad_x, *[grad_w[n] for n in TWIN_WEIGHTS], *[delta_w[n] for n in TWIN_WEIGHTS],
            *[new_m[n] for n in TWIN_WEIGHTS], *[new_v[n] for n in TWIN_WEIGHTS])
```

```python
import functools
import math

import jax
import jax.numpy as jnp
from jax import lax
from jax.experimental import pallas as pl
from jax.experimental.pallas import tpu as pltpu

F32 = jnp.float32
BF16 = jnp.bfloat16

D_MODEL = 1024
CHUNK = 64
HEAD_DIM = 128
N_HEADS = 4
DN_WIDTH = 512
SG_WIDTH = 512
SG_GROUPS = 4
SG_BLOCK = 128
D_FF = 2816
PROJ_COLS = 3080
PROJ_PAD = 3200
BA_COL = 3072
EPS = 1e-6
NEG = -1e30
VMEM_LIMIT = 56 * 1024 * 1024

ADAM_LR = 0.001
ADAM_B1 = 0.9
ADAM_B2 = 0.999
ADAM_EPS = 1e-08
ADAM_WD = 0.01
ADAM_STEP = 10

MESH = pl.DeviceIdType.MESH
ANY = pl.BlockSpec(memory_space=pl.ANY)


def _cp(*sem):
    return pltpu.CompilerParams(dimension_semantics=sem, vmem_limit_bytes=VMEM_LIMIT)


def _bf(a):
    return a.astype(BF16)


def _nn(a, b):
    return jnp.dot(_bf(a), _bf(b), preferred_element_type=F32)


def _nt(a, b):
    return lax.dot_general(_bf(a), _bf(b), (((1,), (1,)), ((), ())), preferred_element_type=F32)


def _tn(a, b):
    return lax.dot_general(_bf(a), _bf(b), (((0,), (0,)), ((), ())), preferred_element_type=F32)


def _split(a):
    hi = _bf(a)
    return hi, _bf(a - hi.astype(F32))


def _sigmoid(x):
    return 1.0 / (1.0 + jnp.exp(-x))


def _silu(x):
    return x * _sigmoid(x)


def _dsilu(x):
    s = _sigmoid(x)
    return s * (1.0 + x * (1.0 - s))


_GELU_C = math.sqrt(2.0 / math.pi)
_GELU_A = 0.044715


def _gelu(x):
    return 0.5 * x * (1.0 + jnp.tanh(_GELU_C * (x + _GELU_A * x * x * x)))


def _dgelu(x):
    t = jnp.tanh(_GELU_C * (x + _GELU_A * x * x * x))
    return 0.5 * (1.0 + t) + 0.5 * x * (1.0 - t * t) * _GELU_C * (1.0 + 3.0 * _GELU_A * x * x)


def _softplus(x):
    return jnp.maximum(x, 0.0) + jnp.log(1.0 + jnp.exp(-jnp.abs(x)))


def _mm_nn(name, a, b, out_dtype, tm, tn, res=None):
    M, K = a.shape
    N = b.shape[1]
    tm, tn = min(tm, M), min(tn, N)

    def body(*refs):
        a_ref, b_ref = refs[0], refs[1]
        o_ref = refs[-1]
        acc = _nn(a_ref[...], b_ref[...])
        if res is not None:
            acc = acc + refs[2][...]
        o_ref[...] = acc.astype(o_ref.dtype)

    in_specs = [pl.BlockSpec((tm, K), lambda i, j: (i, 0)), pl.BlockSpec((K, tn), lambda i, j: (0, j))]
    args = [a, b]
    if res is not None:
        in_specs.append(pl.BlockSpec((tm, tn), lambda i, j: (i, j)))
        args.append(res)
    return pl.pallas_call(
        body, name=name, grid=(M // tm, N // tn), in_specs=in_specs,
        out_specs=pl.BlockSpec((tm, tn), lambda i, j: (i, j)),
        out_shape=jax.ShapeDtypeStruct((M, N), out_dtype),
        compiler_params=_cp("parallel", "parallel"))(*args)


def _mm_nt(name, a, b, out_dtype, tm, tn):
    M, K = a.shape
    N = b.shape[0]
    tm, tn = min(tm, M), min(tn, N)

    def body(a_ref, b_ref, o_ref):
        o_ref[...] = _nt(a_ref[...], b_ref[...]).astype(o_ref.dtype)

    return pl.pallas_call(
        body, name=name, grid=(M // tm, N // tn),
        in_specs=[pl.BlockSpec((tm, K), lambda i, j: (i, 0)), pl.BlockSpec((tn, K), lambda i, j: (j, 0))],
        out_specs=pl.BlockSpec((tm, tn), lambda i, j: (i, j)),
        out_shape=jax.ShapeDtypeStruct((M, N), out_dtype),
        compiler_params=_cp("parallel", "parallel"))(a, b)


def _mm_tn(name, a, b, tm, tn, tk):
    T, M = a.shape
    N = b.shape[1]
    tm, tn, tk = min(tm, M), min(tn, N), min(tk, T)

    def body(a_ref, b_ref, o_ref):
        @pl.when(pl.program_id(2) == 0)
        def _():
            o_ref[...] = jnp.zeros_like(o_ref)
        o_ref[...] += _tn(a_ref[...], b_ref[...])

    return pl.pallas_call(
        body, name=name, grid=(M // tm, N // tn, T // tk),
        in_specs=[pl.BlockSpec((tk, tm), lambda i, j, k: (k, i)), pl.BlockSpec((tk, tn), lambda i, j, k: (k, j))],
        out_specs=pl.BlockSpec((tm, tn), lambda i, j, k: (i, j)),
        out_shape=jax.ShapeDtypeStruct((M, N), F32),
        compiler_params=_cp("parallel", "parallel", "arbitrary"))(a, b)


def _rms_fwd(name, x, g, rb=512):
    T, Dm = x.shape
    rb = min(rb, T)

    def body(x_ref, g_ref, h_ref):
        xv = x_ref[...]
        r = lax.rsqrt(jnp.mean(xv * xv, axis=-1, keepdims=True) + EPS)
        h_ref[...] = (xv * r * g_ref[...]).astype(BF16)

    return pl.pallas_call(
        body, name=name, grid=(T // rb,),
        in_specs=[pl.BlockSpec((rb, Dm), lambda i: (i, 0)), pl.BlockSpec((1, Dm), lambda i: (0, 0))],
        out_specs=pl.BlockSpec((rb, Dm), lambda i: (i, 0)),
        out_shape=jax.ShapeDtypeStruct((T, Dm), BF16), compiler_params=_cp("parallel"))(x, g)


def _rms_bwd(name, dh, x, g, dres, rb=512):
    T, Dm = x.shape
    rb = min(rb, T)

    def body(dh_ref, x_ref, g_ref, dres_ref, dx_ref, gg_ref):
        @pl.when(pl.program_id(0) == 0)
        def _():
            gg_ref[...] = jnp.zeros_like(gg_ref)
        xv = x_ref[...]
        r = lax.rsqrt(jnp.mean(xv * xv, axis=-1, keepdims=True) + EPS)
        xh = xv * r
        dhv = dh_ref[...]
        gg_ref[...] += jnp.sum(dhv * xh, axis=0, keepdims=True)
        dxh = dhv * g_ref[...]
        dx_ref[...] = dres_ref[...] + r * (dxh - xh * jnp.mean(dxh * xh, axis=-1, keepdims=True))

    row = pl.BlockSpec((rb, Dm), lambda i: (i, 0))
    vec = pl.BlockSpec((1, Dm), lambda i: (0, 0))
    return pl.pallas_call(
        body, name=name, grid=(T // rb,), in_specs=[row, row, vec, row], out_specs=[row, vec],
        out_shape=[jax.ShapeDtypeStruct((T, Dm), F32), jax.ShapeDtypeStruct((1, Dm), F32)],
        compiler_params=_cp("arbitrary"))(dh, x, g, dres)


def _loss_head(x3, tgt, g, rb=512):
    T, Dm = x3.shape
    rb = min(rb, T)

    def body(x_ref, t_ref, g_ref, loss_ref, dx_ref, gg_ref):
        @pl.when(pl.program_id(0) == 0)
        def _():
            gg_ref[...] = jnp.zeros_like(gg_ref)
            loss_ref[...] = jnp.zeros_like(loss_ref)
        xv = x_ref[...]
        r = lax.rsqrt(jnp.mean(xv * xv, axis=-1, keepdims=True) + EPS)
        xh = xv * r
        e = xh * g_ref[...] - t_ref[...]
        loss_ref[...] += jnp.zeros_like(loss_ref) + (0.5 / Dm) * jnp.sum(e * e)
        dy = e * (1.0 / Dm)
        gg_ref[...] += jnp.sum(dy * xh, axis=0, keepdims=True)
        dxh = dy * g_ref[...]
        dx_ref[...] = r * (dxh - xh * jnp.mean(dxh * xh, axis=-1, keepdims=True))

    row = pl.BlockSpec((rb, Dm), lambda i: (i, 0))
    vec = pl.BlockSpec((1, Dm), lambda i: (0, 0))
    return pl.pallas_call(
        body, name="loss_head", grid=(T // rb,), in_specs=[row, row, vec],
        out_specs=[pl.BlockSpec((1, 128), lambda i: (0, 0)), row, vec],
        out_shape=[jax.ShapeDtypeStruct((1, 128), F32), jax.ShapeDtypeStruct((T, Dm), F32),
                   jax.ShapeDtypeStruct((1, Dm), F32)],
        compiler_params=_cp("arbitrary"))(x3, tgt, g)


def _prev_rows(cur, halo, i):
    return jnp.concatenate([jnp.where(i > 0, halo, 0.0), cur], axis=0)


def _shift_down(ext, k, rb):
    if k == 0:
        return ext[8:8 + rb]
    return pltpu.roll(ext, k, 0)[8:8 + rb]


def _conv_fwd_vals(ext, w, rb):
    K = w.shape[0]
    out = _shift_down(ext, K - 1, rb) * w[0:1]
    for k in range(1, K):
        out = out + _shift_down(ext, K - 1 - k, rb) * w[k:k + 1]
    return out


def _halo_prev_spec(rb, width, col=0):
    return pl.BlockSpec((8, width), lambda i: (jnp.maximum(i * (rb // 8) - 1, 0), col))


def _conv_bwd_in(name, dc, w, rb):
    T, W = dc.shape
    K = w.shape[0]
    rb = min(rb, T)
    nb = T // rb

    def body(dc_ref, nx_ref, w_ref, o_ref):
        i = pl.program_id(0)
        ext = jnp.concatenate([dc_ref[...], jnp.where(i < nb - 1, nx_ref[...], 0.0)], axis=0)
        wv = w_ref[...]
        out = ext[0:rb] * wv[K - 1:K]
        for k in range(K - 1):
            s = K - 1 - k
            out = out + pltpu.roll(ext, rb + 8 - s, 0)[0:rb] * wv[k:k + 1]
        o_ref[...] = out.astype(BF16)

    return pl.pallas_call(
        body, name=name, grid=(nb,),
        in_specs=[pl.BlockSpec((rb, W), lambda i: (i, 0)),
                  pl.BlockSpec((8, W), lambda i: (jnp.minimum((i + 1) * (rb // 8), T // 8 - 1), 0)),
                  pl.BlockSpec((K, W), lambda i: (0, 0))],
        out_specs=pl.BlockSpec((rb, W), lambda i: (i, 0)),
        out_shape=jax.ShapeDtypeStruct((T, W), BF16), compiler_params=_cp("parallel"))(dc, dc, w)


def _ffn_act(up, w, b, rb=256):
    T, W = up.shape
    rb = min(rb, T)

    def body(up_ref, halo_ref, w_ref, b_ref, act_ref):
        ext = _prev_rows(up_ref[...], halo_ref[...], pl.program_id(0))
        c = _conv_fwd_vals(ext, w_ref[...], rb) + b_ref[...]
        act_ref[...] = (_silu(c[:, :D_FF]) * c[:, D_FF:]).astype(BF16)

    return pl.pallas_call(
        body, name="ffn_act", grid=(T // rb,),
        in_specs=[pl.BlockSpec((rb, W), lambda i: (i, 0)), _halo_prev_spec(rb, W),
                  pl.BlockSpec((3, W), lambda i: (0, 0)), pl.BlockSpec((1, W), lambda i: (0, 0))],
        out_specs=pl.BlockSpec((rb, D_FF), lambda i: (i, 0)),
        out_shape=jax.ShapeDtypeStruct((T, D_FF), BF16), compiler_params=_cp("parallel"))(up, up, w, b)


def _ffn_act_bwd(up, dact, w, b, rb=256):
    T, W = up.shape
    rb = min(rb, T)

    def body(up_ref, halo_ref, da_ref, w_ref, b_ref, dc_ref, gw_ref, gb_ref):
        @pl.when(pl.program_id(0) == 0)
        def _():
            gw_ref[...] = jnp.zeros_like(gw_ref)
            gb_ref[...] = jnp.zeros_like(gb_ref)
        ext = _prev_rows(up_ref[...], halo_ref[...], pl.program_id(0))
        c = _conv_fwd_vals(ext, w_ref[...], rb) + b_ref[...]
        gpre, vv = c[:, :D_FF], c[:, D_FF:]
        da = da_ref[...]
        dc = jnp.concatenate([da * vv * _dsilu(gpre), da * _silu(gpre)], axis=1)
        dc_ref[...] = dc
        gb_ref[...] += jnp.sum(dc, axis=0, keepdims=True)
        for k in range(3):
            gw_ref[k:k + 1, :] += jnp.sum(_shift_down(ext, 2 - k, rb) * dc, axis=0, keepdims=True)

    return pl.pallas_call(
        body, name="ffn_act_bwd", grid=(T // rb,),
        in_specs=[pl.BlockSpec((rb, W), lambda i: (i, 0)), _halo_prev_spec(rb, W),
                  pl.BlockSpec((rb, D_FF), lambda i: (i, 0)),
                  pl.BlockSpec((3, W), lambda i: (0, 0)), pl.BlockSpec((1, W), lambda i: (0, 0))],
        out_specs=[pl.BlockSpec((rb, W), lambda i: (i, 0)), pl.BlockSpec((3, W), lambda i: (0, 0)),
                   pl.BlockSpec((1, W), lambda i: (0, 0))],
        out_shape=[jax.ShapeDtypeStruct((T, W), F32), jax.ShapeDtypeStruct((3, W), F32),
                   jax.ShapeDtypeStruct((1, W), F32)],
        compiler_params=_cp("arbitrary"))(up, up, dact, w, b)


def _lane_iota(shape):
    return lax.broadcasted_iota(jnp.int32, shape, len(shape) - 1)


def _dn_act(p, conv_w, alog_row, dtb_row, rb=256):
    T = p.shape[0]
    rb = min(rb, T)
    W3 = 3 * DN_WIDTH

    def body(p_ref, halo_ref, ba_ref, w_ref, al_ref, dt_ref, q_ref, k_ref, v_ref, bg_ref):
        ext = _prev_rows(p_ref[...], halo_ref[...], pl.program_id(0))
        s = _silu(_conv_fwd_vals(ext, w_ref[...], rb))
        for h in range(N_HEADS):
            lo = h * HEAD_DIM
            sq = s[:, lo:lo + HEAD_DIM]
            n = lax.rsqrt(jnp.sum(sq * sq, axis=-1, keepdims=True) + EPS)
            q_ref[:, lo:lo + HEAD_DIM] = sq * n * (HEAD_DIM ** -0.5)
            sk = s[:, DN_WIDTH + lo:DN_WIDTH + lo + HEAD_DIM]
            n = lax.rsqrt(jnp.sum(sk * sk, axis=-1, keepdims=True) + EPS)
            k_ref[:, lo:lo + HEAD_DIM] = sk * n
        v_ref[...] = s[:, 2 * DN_WIDTH:]
        ba = ba_ref[...]
        lane = _lane_iota(ba.shape)
        beta = _sigmoid(ba)
        g = -jnp.exp(al_ref[...]) * _softplus(ba + dt_ref[...])
        bg_ref[...] = jnp.where(lane < N_HEADS, beta, jnp.where(lane < 2 * N_HEADS, g, 0.0))

    row512 = pl.BlockSpec((rb, DN_WIDTH), lambda i: (i, 0))
    row128 = pl.BlockSpec((rb, 128), lambda i: (i, 0))
    vec128 = pl.BlockSpec((1, 128), lambda i: (0, 0))
    return pl.pallas_call(
        body, name="dn_act", grid=(T // rb,),
        in_specs=[pl.BlockSpec((rb, W3), lambda i: (i, 0)), _halo_prev_spec(rb, W3),
                  pl.BlockSpec((rb, 128), lambda i: (i, BA_COL // 128)),
                  pl.BlockSpec((4, W3), lambda i: (0, 0)), vec128, vec128],
        out_specs=[row512, row512, row512, row128],
        out_shape=[jax.ShapeDtypeStruct((T, DN_WIDTH), F32)] * 3 + [jax.ShapeDtypeStruct((T, 128), F32)],
        compiler_params=_cp("parallel"))(p, p, p, conv_w, alog_row, dtb_row)


def _dn_act_bwd(p, conv_w, alog_row, dtb_row, dq, dk, dv, dbg, rb=256):
    T = p.shape[0]
    rb = min(rb, T)
    W3 = 3 * DN_WIDTH

    def body(p_ref, halo_ref, ba_ref, w_ref, al_ref, dt_ref, dq_ref, dk_ref, dv_ref, dbg_ref,
             dc_ref, dba_ref, gw_ref, gad_ref):
        @pl.when(pl.program_id(0) == 0)
        def _():
            gw_ref[...] = jnp.zeros_like(gw_ref)
            gad_ref[...] = jnp.zeros_like(gad_ref)
        ext = _prev_rows(p_ref[...], halo_ref[...], pl.program_id(0))
        c = _conv_fwd_vals(ext, w_ref[...], rb)
        s = _silu(c)
        ds = _dsilu(c)
        for h in range(N_HEADS):
            lo = h * HEAD_DIM
            for (off, d_ref, scale) in ((0, dq_ref, HEAD_DIM ** -0.5), (DN_WIDTH, dk_ref, 1.0)):
                sv = s[:, off + lo:off + lo + HEAD_DIM]
                n = lax.rsqrt(jnp.sum(sv * sv, axis=-1, keepdims=True) + EPS)
                hat = sv * n
                dvv = d_ref[:, lo:lo + HEAD_DIM]
                dsv = (n * scale) * (dvv - hat * jnp.sum(hat * dvv, axis=-1, keepdims=True))
                dc_ref[:, off + lo:off + lo + HEAD_DIM] = dsv * ds[:, off + lo:off + lo + HEAD_DIM]
        dc_ref[:, 2 * DN_WIDTH:] = dv_ref[...] * ds[:, 2 * DN_WIDTH:]
        dc = dc_ref[...]
        for k in range(4):
            gw_ref[k:k + 1, :] += jnp.sum(_shift_down(ext, 3 - k, rb) * dc, axis=0, keepdims=True)
        ba = ba_ref[...]
        dbg = dbg_ref[...]
        lane = _lane_iota(ba.shape)
        beta = _sigmoid(ba)
        ea = jnp.exp(al_ref[...])
        z = ba + dt_ref[...]
        d_a = dbg * (-ea) * _sigmoid(z)
        dba = jnp.where(lane < N_HEADS, dbg * beta * (1.0 - beta), jnp.where(lane < 2 * N_HEADS, d_a, 0.0))
        dba_ref[...] = dba.astype(BF16)
        isg = (lane >= N_HEADS) & (lane < 2 * N_HEADS)
        g = -ea * _softplus(z)
        gad_ref[0:1, :] += jnp.sum(jnp.where(isg, dbg * g, 0.0), axis=0, keepdims=True)
        gad_ref[1:2, :] += jnp.sum(jnp.where(isg, d_a, 0.0), axis=0, keepdims=True)

    row512 = pl.BlockSpec((rb, DN_WIDTH), lambda i: (i, 0))
    row128 = pl.BlockSpec((rb, 128), lambda i: (i, 0))
    vec128 = pl.BlockSpec((1, 128), lambda i: (0, 0))
    return pl.pallas_call(
        body, name="dn_act_bwd", grid=(T // rb,),
        in_specs=[pl.BlockSpec((rb, W3), lambda i: (i, 0)), _halo_prev_spec(rb, W3),
                  pl.BlockSpec((rb, 128), lambda i: (i, BA_COL // 128)),
                  pl.BlockSpec((4, W3), lambda i: (0, 0)), vec128, vec128,
                  row512, row512, row512, row128],
        out_specs=[pl.BlockSpec((rb, W3), lambda i: (i, 0)), row128,
                   pl.BlockSpec((4, W3), lambda i: (0, 0)), pl.BlockSpec((2, 128), lambda i: (0, 0))],
        out_shape=[jax.ShapeDtypeStruct((T, W3), F32), jax.ShapeDtypeStruct((T, 128), BF16),
                   jax.ShapeDtypeStruct((4, W3), F32), jax.ShapeDtypeStruct((2, 128), F32)],
        compiler_params=_cp("arbitrary"))(p, p, p, conv_w, alog_row, dtb_row, dq, dk, dv, dbg)


def _tri(incl):
    ii = lax.broadcasted_iota(jnp.int32, (CHUNK, CHUNK), 0)
    jj = lax.broadcasted_iota(jnp.int32, (CHUNK, CHUNK), 1)
    return ii, jj, ((ii >= jj) if incl else (ii > jj))


def _dn_chunk(k, bg):
    T = k.shape[0]
    N = T // CHUNK

    def body(k_ref, bg_ref, gc_ref, gct_ref, l_ref):
        ii, jj, incl = _tri(True)
        bgv = bg_ref[...]
        gc = jnp.dot(incl.astype(F32), bgv, precision=lax.Precision.HIGHEST, preferred_element_type=F32)
        gc_ref[...] = gc
        gct = gc.T
        gct_ref[0] = gct[0:8]
        for h in range(N_HEADS):
            kh = k_ref[:, h * HEAD_DIM:(h + 1) * HEAD_DIM]
            beta = bgv[:, h:h + 1]
            gcol = gc[:, N_HEADS + h:N_HEADS + h + 1]
            grow = gct[N_HEADS + h:N_HEADS + h + 1, :]
            gam = jnp.exp(jnp.where(ii > jj, gcol - grow, NEG))
            l_ref[0, h] = _nt(kh * beta, kh) * gam

    return pl.pallas_call(
        body, name="dn_chunk", grid=(N,),
        in_specs=[pl.BlockSpec((CHUNK, DN_WIDTH), lambda n: (n, 0)), pl.BlockSpec((CHUNK, 128), lambda n: (n, 0))],
        out_specs=[pl.BlockSpec((CHUNK, 128), lambda n: (n, 0)), pl.BlockSpec((1, 8, CHUNK), lambda n: (n, 0, 0)),
                   pl.BlockSpec((1, N_HEADS, CHUNK, CHUNK), lambda n: (n, 0, 0, 0))],
        out_shape=[jax.ShapeDtypeStruct((T, 128), F32), jax.ShapeDtypeStruct((N, 8, CHUNK), F32),
                   jax.ShapeDtypeStruct((N, N_HEADS, CHUNK, CHUNK), F32)],
        compiler_params=_cp("parallel"))(k, bg)


def _tri_inv(lt):
    S = lt.shape[1]

    def body(l_ref, a_ref):
        col = lax.broadcasted_iota(jnp.int32, (CHUNK, S), 0)
        for i in range(CHUNK):
            def step(j, acc):
                return acc - l_ref[pl.ds(i * CHUNK + j, 1), :] * a_ref[j]
            a_ref[i] = lax.fori_loop(0, i, step, (col == i).astype(F32))

    return pl.pallas_call(
        body, name="tri_inv", out_shape=jax.ShapeDtypeStruct((CHUNK, CHUNK, S), F32),
        compiler_params=pltpu.CompilerParams(vmem_limit_bytes=VMEM_LIMIT))(lt)


def _dn_head_terms(qh, kh, vh, beta, gcol, grow):
    ii, jj, incl = _tri(True)
    gam = jnp.exp(jnp.where(incl, gcol - grow, NEG))
    glast = grow[:, CHUNK - 1:CHUNK]
    E = jnp.exp(gcol)
    Fd = jnp.exp(glast - gcol)
    cd = jnp.exp(glast)
    kb = kh * beta
    return dict(ii=ii, jj=jj, gam=gam, E=E, F=Fd, cd=cd, kb=kb, vb=vh * beta, W=kb * E, qE=qh * E, kt=kh * Fd)


def _apply_a(a, u):
    hi, lo = _split(a)
    ub = _bf(u)
    return jnp.dot(hi, ub, preferred_element_type=F32) + jnp.dot(lo, ub, preferred_element_type=F32)


def _apply_at(a, u):
    hi, lo = _split(a)
    ub = _bf(u)
    dn = (((0,), (0,)), ((), ()))
    return (lax.dot_general(hi, ub, dn, preferred_element_type=F32)
            + lax.dot_general(lo, ub, dn, preferred_element_type=F32))


def _dn_scan(q, k, v, bg, gc, gct, a):
    T = q.shape[0]
    N = T // CHUNK

    def body(q_ref, k_ref, v_ref, bg_ref, gc_ref, gct_ref, a_ref, o_ref, sall_ref, s_ref):
        @pl.when(pl.program_id(0) == 0)
        def _():
            s_ref[...] = jnp.zeros_like(s_ref)
        bgv, gcv, gctv = bg_ref[...], gc_ref[...], gct_ref[0]
        for h in range(N_HEADS):
            sl = slice(h * HEAD_DIM, (h + 1) * HEAD_DIM)
            qh, kh, vh = q_ref[:, sl], k_ref[:, sl], v_ref[:, sl]
            t = _dn_head_terms(qh, kh, vh, bgv[:, h:h + 1], gcv[:, N_HEADS + h:N_HEADS + h + 1],
                               gctv[N_HEADS + h:N_HEADS + h + 1, :])
            S = s_ref[h]
            sall_ref[0, h] = S
            vn = _apply_a(a_ref[0, h], t["vb"] - _nn(t["W"], S))
            P = _nt(qh, kh) * t["gam"]
            o_ref[:, sl] = _nn(t["qE"], S) + _nn(P, vn)
            s_ref[h] = t["cd"] * S + _tn(t["kt"], vn)

    row512 = pl.BlockSpec((CHUNK, DN_WIDTH), lambda n: (n, 0))
    row128 = pl.BlockSpec((CHUNK, 128), lambda n: (n, 0))
    return pl.pallas_call(
        body, name="dn_scan", grid=(N,),
        in_specs=[row512, row512, row512, row128, row128, pl.BlockSpec((1, 8, CHUNK), lambda n: (n, 0, 0)),
                  pl.BlockSpec((1, N_HEADS, CHUNK, CHUNK), lambda n: (n, 0, 0, 0))],
        out_specs=[row512, pl.BlockSpec((1, N_HEADS, HEAD_DIM, HEAD_DIM), lambda n: (n, 0, 0, 0))],
        out_shape=[jax.ShapeDtypeStruct((T, DN_WIDTH), F32),
                   jax.ShapeDtypeStruct((N, N_HEADS, HEAD_DIM, HEAD_DIM), F32)],
        scratch_shapes=[pltpu.VMEM((N_HEADS, HEAD_DIM, HEAD_DIM), F32)],
        compiler_params=_cp("arbitrary"))(q, k, v, bg, gc, gct, a)


def _dn_scan_bwd(q, k, v, bg, gc, gct, a, sall, do):
    T = q.shape[0]
    N = T // CHUNK

    def body(q_ref, k_ref, v_ref, bg_ref, gc_ref, gct_ref, a_ref, sall_ref, do_ref,
             dq_ref, dk_ref, dv_ref, dbg_ref, ds_ref):
        @pl.when(pl.program_id(0) == 0)
        def _():
            ds_ref[...] = jnp.zeros_like(ds_ref)
        bgv, gcv, gctv = bg_ref[...], gc_ref[...], gct_ref[0]
        lane = _lane_iota((CHUNK, 128))
        rowi = lax.broadcasted_iota(jnp.int32, (CHUNK, 1), 0)
        dbeta_arr = jnp.zeros((CHUNK, 128), F32)
        dgc_arr = jnp.zeros((CHUNK, 128), F32)
        for h in range(N_HEADS):
            sl = slice(h * HEAD_DIM, (h + 1) * HEAD_DIM)
            qh, kh, vh, dO = q_ref[:, sl], k_ref[:, sl], v_ref[:, sl], do_ref[:, sl]
            beta = bgv[:, h:h + 1]
            t = _dn_head_terms(qh, kh, vh, beta, gcv[:, N_HEADS + h:N_HEADS + h + 1],
                               gctv[N_HEADS + h:N_HEADS + h + 1, :])
            ii, jj, gam, E, Fd, cd, kb = t["ii"], t["jj"], t["gam"], t["E"], t["F"], t["cd"], t["kb"]
            S = sall_ref[0, h]
            dSn = ds_ref[h]
            A = a_ref[0, h]
            KK = _nt(kb, kh)
            QK = _nt(qh, kh)
            P = QK * gam
            U = t["vb"] - _nn(t["W"], S)
            vn = _apply_a(A, U)
            d_vn = _tn(P, dO) + _nn(t["kt"], dSn)
            d_kt = _nt(vn, dSn)
            d_cd = jnp.sum(S * dSn)
            d_qE = _nt(dO, S)
            dP = jnp.where(ii >= jj, _nt(dO, vn), 0.0)
            dU = _apply_at(A, d_vn)
            dL = jnp.where(ii > jj, -_nt(dU, vn), 0.0)
            dW = -_nt(dU, S)
            ds_ref[h] = cd * dSn + _tn(t["qE"], dO) - _tn(t["W"], dU)
            dQK = dP * gam
            dKK = dL * gam
            Z = dQK * QK + dKK * KK
            dq_ref[:, sl] = _nn(dQK, kh) + d_qE * E
            d_kb = _nn(dKK, kh) + dW * E
            dk_ref[:, sl] = _tn(dQK, qh) + _tn(dKK, kb) + d_kb * beta + d_kt * Fd
            dv_ref[:, sl] = dU * beta
            dbeta = jnp.sum(dU * vh + d_kb * kh, axis=-1, keepdims=True)
            dE = jnp.sum(dW * kb + d_qE * qh, axis=-1, keepdims=True)
            dFF = jnp.sum(d_kt * kh, axis=-1, keepdims=True) * Fd
            dgc = (dE * E - dFF + jnp.sum(Z, axis=-1, keepdims=True) - jnp.sum(Z.T, axis=-1, keepdims=True)
                   + jnp.where(rowi == CHUNK - 1, jnp.sum(dFF) + d_cd * cd, 0.0))
            dbeta_arr = dbeta_arr + jnp.where(lane == h, dbeta, 0.0)
            dgc_arr = dgc_arr + jnp.where(lane == N_HEADS + h, dgc, 0.0)
        ii, jj, _ = _tri(True)
        rev = (jj >= ii).astype(F32)
        dbg_ref[...] = dbeta_arr + jnp.dot(rev, dgc_arr, precision=lax.Precision.HIGHEST,
                                           preferred_element_type=F32)

    row512 = pl.BlockSpec((CHUNK, DN_WIDTH), lambda n: (N - 1 - n, 0))
    row128 = pl.BlockSpec((CHUNK, 128), lambda n: (N - 1 - n, 0))
    return pl.pallas_call(
        body, name="dn_scan_bwd", grid=(N,),
        in_specs=[row512, row512, row512, row128, row128,
                  pl.BlockSpec((1, 8, CHUNK), lambda n: (N - 1 - n, 0, 0)),
                  pl.BlockSpec((1, N_HEADS, CHUNK, CHUNK), lambda n: (N - 1 - n, 0, 0, 0)),
                  pl.BlockSpec((1, N_HEADS, HEAD_DIM, HEAD_DIM), lambda n: (N - 1 - n, 0, 0, 0)), row512],
        out_specs=[row512, row512, row512, row128],
        out_shape=[jax.ShapeDtypeStruct((T, DN_WIDTH), F32)] * 3 + [jax.ShapeDtypeStruct((T, 128), F32)],
        scratch_shapes=[pltpu.VMEM((N_HEADS, HEAD_DIM, HEAD_DIM), F32)],
        compiler_params=_cp("arbitrary"))(q, k, v, bg, gc, gct, a, sall, do)


def _sg_mask():
    ii = lax.broadcasted_iota(jnp.int32, (SG_BLOCK, SG_BLOCK), 0) // CHUNK
    jj = lax.broadcasted_iota(jnp.int32, (SG_BLOCK, SG_BLOCK), 1) // CHUNK
    return jj <= ii


def _mix_fwd(o, p, ong, sgn, sgw, sgbt):
    T = o.shape[0]
    rb = SG_BLOCK

    def body(o_ref, gate_ref, u_ref, vg_ref, ong_ref, sgn_ref, sgw_ref, sgbt_ref, mix_ref):
        mask = _sg_mask()
        gate = gate_ref[...]
        for h in range(N_HEADS):
            sl = slice(h * HEAD_DIM, (h + 1) * HEAD_DIM)
            oh = o_ref[:, sl]
            r = lax.rsqrt(jnp.mean(oh * oh, axis=-1, keepdims=True) + EPS)
            mix_ref[:, sl] = (oh * r * ong_ref[...] * _silu(gate[:, sl])).astype(BF16)
        for gi in range(SG_GROUPS):
            sl = slice(gi * SG_BLOCK, (gi + 1) * SG_BLOCK)
            gv = _gelu(vg_ref[:, sl])
            r = lax.rsqrt(jnp.mean(gv * gv, axis=-1, keepdims=True) + EPS)
            vh = gv * r * sgn_ref[:, sl]
            s = _nn(jnp.where(mask, sgw_ref[gi], 0.0), vh) + sgbt_ref[:, gi:gi + 1]
            mix_ref[:, DN_WIDTH + gi * SG_BLOCK:DN_WIDTH + (gi + 1) * SG_BLOCK] = (_gelu(u_ref[:, sl]) * s).astype(BF16)

    def col(c):
        return pl.BlockSpec((rb, 512), lambda i: (i, c))
    return pl.pallas_call(
        body, name="mix_fwd", grid=(T // rb,),
        in_specs=[pl.BlockSpec((rb, DN_WIDTH), lambda i: (i, 0)), col(3), col(4), col(5),
                  pl.BlockSpec((1, 128), lambda i: (0, 0)), pl.BlockSpec((1, SG_WIDTH), lambda i: (0, 0)),
                  pl.BlockSpec((SG_GROUPS, SG_BLOCK, SG_BLOCK), lambda i: (0, 0, 0)),
                  pl.BlockSpec((SG_BLOCK, 128), lambda i: (0, 0))],
        out_specs=pl.BlockSpec((rb, D_MODEL), lambda i: (i, 0)),
        out_shape=jax.ShapeDtypeStruct((T, D_MODEL), BF16),
        compiler_params=_cp("parallel"))(o, p, p, p, ong, sgn, sgw, sgbt)


def _mix_bwd(o, p, ong, sgn, sgw, sgbt, dmix):
    T = o.shape[0]
    rb = SG_BLOCK

    def body(o_ref, gate_ref, u_ref, vg_ref, ong_ref, sgn_ref, sgw_ref, sgbt_ref, dmix_ref,
             do_ref, dp_ref, gong_ref, gsgn_ref, gsgw_ref, gsgbt_ref):
        @pl.when(pl.program_id(0) == 0)
        def _():
            gong_ref[...] = jnp.zeros_like(gong_ref)
            gsgn_ref[...] = jnp.zeros_like(gsgn_ref)
            gsgw_ref[...] = jnp.zeros_like(gsgw_ref)
            gsgbt_ref[...] = jnp.zeros_like(gsgbt_ref)
        mask = _sg_mask()
        gate = gate_ref[...]
        lane = _lane_iota((SG_BLOCK, 128))
        for h in range(N_HEADS):
            sl = slice(h * HEAD_DIM, (h + 1) * HEAD_DIM)
            oh = o_ref[:, sl]
            dm = dmix_ref[:, sl]
            r = lax.rsqrt(jnp.mean(oh * oh, axis=-1, keepdims=True) + EPS)
            oh_hat = oh * r
            gt = gate[:, sl]
            sg = _silu(gt)
            dp_ref[:, sl] = (dm * oh_hat * ong_ref[...] * _dsilu(gt)).astype(BF16)
            dn_ = dm * sg
            gong_ref[...] += jnp.sum(dn_ * oh_hat, axis=0, keepdims=True)
            dhat = dn_ * ong_ref[...]
            do_ref[:, sl] = r * (dhat - oh_hat * jnp.mean(dhat * oh_hat, axis=-1, keepdims=True))
        for gi in range(SG_GROUPS):
            sl = slice(gi * SG_BLOCK, (gi + 1) * SG_BLOCK)
            vraw = vg_ref[:, sl]
            gv = _gelu(vraw)
            r = lax.rsqrt(jnp.mean(gv * gv, axis=-1, keepdims=True) + EPS)
            vhat = gv * r
            vn = vhat * sgn_ref[:, sl]
            wm = jnp.where(mask, sgw_ref[gi], 0.0)
            s = _nn(wm, vn) + sgbt_ref[:, gi:gi + 1]
            uraw = u_ref[:, sl]
            dm = dmix_ref[:, DN_WIDTH + gi * SG_BLOCK:DN_WIDTH + (gi + 1) * SG_BLOCK]
            dp_ref[:, DN_WIDTH + gi * SG_BLOCK:DN_WIDTH + (gi + 1) * SG_BLOCK] = (dm * s * _dgelu(uraw)).astype(BF16)
            ds = dm * _gelu(uraw)
            gsgbt_ref[...] += jnp.where(lane == gi, jnp.sum(ds, axis=-1, keepdims=True), 0.0)
            gsgw_ref[gi] += jnp.where(mask, _nt(ds, vn), 0.0)
            dvn = _tn(wm, ds)
            gsgn_ref[:, sl] += jnp.sum(dvn * vhat, axis=0, keepdims=True)
            dhat = dvn * sgn_ref[:, sl]
            dgv = r * (dhat - vhat * jnp.mean(dhat * vhat, axis=-1, keepdims=True))
            dp_ref[:, 2 * DN_WIDTH + gi * SG_BLOCK:2 * DN_WIDTH + (gi + 1) * SG_BLOCK] = (dgv * _dgelu(vraw)).astype(BF16)

    def col(c):
        return pl.BlockSpec((rb, 512), lambda i: (i, c))
    full = lambda *s: pl.BlockSpec(s, lambda i: (0,) * len(s))
    return pl.pallas_call(
        body, name="mix_bwd", grid=(T // rb,),
        in_specs=[pl.BlockSpec((rb, DN_WIDTH), lambda i: (i, 0)), col(3), col(4), col(5),
                  full(1, 128), full(1, SG_WIDTH), full(SG_GROUPS, SG_BLOCK, SG_BLOCK), full(SG_BLOCK, 128),
                  pl.BlockSpec((rb, D_MODEL), lambda i: (i, 0))],
        out_specs=[pl.BlockSpec((rb, DN_WIDTH), lambda i: (i, 0)), pl.BlockSpec((rb, 3 * 512), lambda i: (i, 0)),
                   full(1, 128), full(1, SG_WIDTH), full(SG_GROUPS, SG_BLOCK, SG_BLOCK), full(SG_BLOCK, 128)],
        out_shape=[jax.ShapeDtypeStruct((T, DN_WIDTH), F32), jax.ShapeDtypeStruct((T, 3 * 512), BF16),
                   jax.ShapeDtypeStruct((1, 128), F32), jax.ShapeDtypeStruct((1, SG_WIDTH), F32),
                   jax.ShapeDtypeStruct((SG_GROUPS, SG_BLOCK, SG_BLOCK), F32),
                   jax.ShapeDtypeStruct((SG_BLOCK, 128), F32)],
        compiler_params=_cp("arbitrary"))(o, p, p, p, ong, sgn, sgw, sgbt, dmix)


def _pad_lanes(row, offset=0):
    n = row.shape[1]
    return jnp.pad(row, ((0, 0), (offset, 128 - n - offset)))


def _local_step(x, tgt, w):
    T = x.shape[0]
    N = T // CHUNK
    alog_row = _pad_lanes(w["dn_a_log"], N_HEADS)
    dtb_row = _pad_lanes(w["dn_dt_bias"], N_HEADS)
    sgbt = jnp.pad(w["sg_b"].T, ((0, 0), (0, 128 - SG_GROUPS)))

    h1 = _rms_fwd("rms_attn", x, w["attn_norm_g"])
    p = _mm_nn("in_proj", h1, w["w_in"], F32, 512, 640)
    q, k, v, bg = _dn_act(p, w["dn_conv_w"], alog_row, dtb_row)
    gc, gct, lmat = _dn_chunk(k, bg)
    lt = lmat.reshape(N * N_HEADS, CHUNK * CHUNK).T
    at = _tri_inv(lt)
    a = at.reshape(CHUNK * CHUNK, N * N_HEADS).T.reshape(N, N_HEADS, CHUNK, CHUNK)
    o, sall = _dn_scan(q, k, v, bg, gc, gct, a)
    mix = _mix_fwd(o, p, w["dn_out_norm_g"], w["sg_norm_g"], w["sg_w"], sgbt)
    x2 = _mm_nn("out_proj", mix, w["w_out"], F32, 512, 1024, res=x)
    h2 = _rms_fwd("rms_ffn", x2, w["ffn_norm_g"])
    up = _mm_nn("up_proj", h2, w["w_up"], F32, 512, 1408)
    act = _ffn_act(up, w["ffn_conv_w"], w["ffn_conv_b"])
    x3 = _mm_nn("down_proj", act, w["w_down"], F32, 512, 1024, res=x2)
    loss, dx3, g_final = _loss_head(x3, tgt, w["final_norm_g"])

    dact = _mm_nt("d_act", dx3, w["w_down"], F32, 512, 1408)
    g_w_down = _mm_tn("g_w_down", act, dx3, 1408, 1024, 1024)
    dc, g_ffn_conv_w, g_ffn_conv_b = _ffn_act_bwd(up, dact, w["ffn_conv_w"], w["ffn_conv_b"])
    dup = _conv_bwd_in("d_up", dc, w["ffn_conv_w"], 256)
    g_w_up = _mm_tn("g_w_up", h2, dup, 512, 1408, 1024)
    dh2 = _mm_nt("d_h2", dup, w["w_up"], F32, 512, 1024)
    dx2, g_ffn_norm = _rms_bwd("rms_ffn_bwd", dh2, x2, w["ffn_norm_g"], dx3)
    dmix = _mm_nt("d_mix", dx2, w["w_out"], F32, 512, 1024)
    g_w_out = _mm_tn("g_w_out", mix, dx2, 1024, 1024, 1024)
    do, dp_mid, g_ong, g_sgn, g_sgw, g_sgbt = _mix_bwd(o, p, w["dn_out_norm_g"], w["sg_norm_g"], w["sg_w"], sgbt, dmix)
    dq, dk, dv, dbg = _dn_scan_bwd(q, k, v, bg, gc, gct, a, sall, do)
    dcq, dba, g_dn_conv_w, g_ad = _dn_act_bwd(p, w["dn_conv_w"], alog_row, dtb_row, dq, dk, dv, dbg)
    dp_qkv = _conv_bwd_in("d_qkv", dcq, w["dn_conv_w"], 256)
    dp = jnp.concatenate([dp_qkv, dp_mid, dba], axis=1)
    g_w_in = _mm_tn("g_w_in", h1, dp, 512, 640, 1024)
    dh1 = _mm_nt("d_h1", dp, w["w_in"], F32, 512, 1024)
    grad_x, g_attn_norm = _rms_bwd("rms_attn_bwd", dh1, x, w["attn_norm_g"], dx2)

    grads = dict(
        attn_norm_g=g_attn_norm, w_in=g_w_in[:, :PROJ_COLS], dn_conv_w=g_dn_conv_w,
        dn_a_log=g_ad[0:1, N_HEADS:2 * N_HEADS], dn_dt_bias=g_ad[1:2, N_HEADS:2 * N_HEADS],
        dn_out_norm_g=g_ong, sg_norm_g=g_sgn, sg_w=g_sgw, sg_b=g_sgbt[:, :SG_GROUPS].T,
        w_out=g_w_out, ffn_norm_g=g_ffn_norm, w_up=g_w_up, ffn_conv_w=g_ffn_conv_w,
        ffn_conv_b=g_ffn_conv_b, w_down=g_w_down, final_norm_g=g_final)
    return loss, grad_x, grads


def _me():
    return lax.axis_index("x"), lax.axis_index("y"), lax.axis_index("c")


def _peer(rel):
    x, y, c = _me()
    return {"x": (1 - x, y, c), "y": (x, 1 - y, c), "xy": (1 - x, 1 - y, c), "c": (x, y, 1 - c)}[rel]


def _chip_of(dev):
    return 2 * dev[0] + dev[1]


CHIP_RELS = ("x", "y", "xy")


def _run_copies(copies, sends, recvs):
    for cp in copies:
        cp.start()
    for cp in recvs:
        cp.wait_recv()
    for cp in sends:
        cp.wait_send()


def _gather_shards(shards):
    n = len(shards)

    def body(*refs):
        src, out = refs[:n], refs[n:2 * n]
        send_sems, recv_sems, local_sems = refs[2 * n:]
        me = _chip_of(_me())
        local = [pltpu.make_async_copy(src[i], out[i].at[me], local_sems.at[i]) for i in range(n)]
        for cp in local:
            cp.start()
        sends, recvs = [], []
        for i in range(n):
            for r, rel in enumerate(CHIP_RELS):
                k = 3 * i + r
                peer = _peer(rel)
                sends.append(pltpu.make_async_remote_copy(
                    src_ref=src[i], dst_ref=out[i].at[me], send_sem=send_sems.at[k], recv_sem=recv_sems.at[k],
                    device_id=peer, device_id_type=MESH))
                recvs.append(pltpu.make_async_remote_copy(
                    src_ref=src[i], dst_ref=out[i].at[_chip_of(peer)], send_sem=send_sems.at[k],
                    recv_sem=recv_sems.at[k], device_id=peer, device_id_type=MESH))
        _run_copies(sends, sends, recvs)
        for cp in local:
            cp.wait()

    return pl.pallas_call(
        body, name="gather_weights", in_specs=[ANY] * n, out_specs=[ANY] * n,
        out_shape=[jax.ShapeDtypeStruct((4,) + s.shape, s.dtype) for s in shards],
        scratch_shapes=[pltpu.SemaphoreType.DMA((3 * n,)), pltpu.SemaphoreType.DMA((3 * n,)),
                        pltpu.SemaphoreType.DMA((n,))])(*shards)


def _pair_halves_and_small(grads4, small):
    n = len(grads4)

    def body(*refs):
        src, small_ref = refs[:n], refs[n]
        out, small_out = refs[n + 1:2 * n + 1], refs[2 * n + 1]
        send_sems, recv_sems, ssend, srecv, local_sem = refs[2 * n + 2:]
        x, y, c = _me()
        sib = _peer("c")
        sends, recvs = [], []
        for i in range(n):
            r2 = src[i].shape[1] // 2
            start = pl.multiple_of((1 - c) * r2, 8)
            cp = pltpu.make_async_remote_copy(
                src_ref=src[i].at[:, pl.ds(start, r2), :], dst_ref=out[i], send_sem=send_sems.at[i],
                recv_sem=recv_sems.at[i], device_id=sib, device_id_type=MESH)
            sends.append(cp)
            recvs.append(cp)
        my_slot = 4 * x + 2 * y + c
        local = pltpu.make_async_copy(small_ref, small_out.at[my_slot], local_sem)
        local.start()
        k = 0
        for fx in (0, 1):
            for fy in (0, 1):
                for fc in (0, 1):
                    if (fx, fy, fc) == (0, 0, 0):
                        continue
                    peer = (x ^ fx, y ^ fy, c ^ fc)
                    sends.append(pltpu.make_async_remote_copy(
                        src_ref=small_ref, dst_ref=small_out.at[my_slot], send_sem=ssend.at[k], recv_sem=srecv.at[k],
                        device_id=peer, device_id_type=MESH))
                    recvs.append(pltpu.make_async_remote_copy(
                        src_ref=small_ref, dst_ref=small_out.at[4 * peer[0] + 2 * peer[1] + peer[2]],
                        send_sem=ssend.at[k], recv_sem=srecv.at[k], device_id=peer, device_id_type=MESH))
                    k += 1
        _run_copies(sends, sends, recvs)
        local.wait()

    return pl.pallas_call(
        body, name="reduce_pair", in_specs=[ANY] * (n + 1), out_specs=[ANY] * (n + 1),
        out_shape=[jax.ShapeDtypeStruct((4, g.shape[1] // 2, g.shape[2]), g.dtype) for g in grads4]
        + [jax.ShapeDtypeStruct((8,) + small.shape, small.dtype)],
        scratch_shapes=[pltpu.SemaphoreType.DMA((n,)), pltpu.SemaphoreType.DMA((n,)),
                        pltpu.SemaphoreType.DMA((7,)), pltpu.SemaphoreType.DMA((7,)),
                        pltpu.SemaphoreType.DMA])(*grads4, small)


def _chips_exchange(parts4):
    n = len(parts4)

    def body(*refs):
        src, out = refs[:n], refs[n:2 * n]
        send_sems, recv_sems = refs[2 * n:]
        sends, recvs = [], []
        for i in range(n):
            for r, rel in enumerate(CHIP_RELS):
                k = 3 * i + r
                peer = _peer(rel)
                cp = pltpu.make_async_remote_copy(
                    src_ref=src[i].at[_chip_of(peer)], dst_ref=out[i].at[r], send_sem=send_sems.at[k],
                    recv_sem=recv_sems.at[k], device_id=peer, device_id_type=MESH)
                sends.append(cp)
                recvs.append(cp)
        _run_copies(sends, sends, recvs)

    return pl.pallas_call(
        body, name="reduce_chips", in_specs=[ANY] * n, out_specs=[ANY] * n,
        out_shape=[jax.ShapeDtypeStruct((3,) + p.shape[1:], p.dtype) for p in parts4],
        scratch_shapes=[pltpu.SemaphoreType.DMA((3 * n,)), pltpu.SemaphoreType.DMA((3 * n,))])(*parts4)


def _pair_allgather(halves):
    n = len(halves)

    def body(*refs):
        src, out = refs[:n], refs[n:2 * n]
        send_sems, recv_sems, local_sems = refs[2 * n:]
        x, y, c = _me()
        sib = _peer("c")
        local, sends, recvs = [], [], []
        for i in range(n):
            r2 = src[i].shape[0]
            mine = out[i].at[pl.ds(pl.multiple_of(c * r2, 8), r2), :]
            theirs = out[i].at[pl.ds(pl.multiple_of((1 - c) * r2, 8), r2), :]
            local.append(pltpu.make_async_copy(src[i], mine, local_sems.at[i]))
            sends.append(pltpu.make_async_remote_copy(
                src_ref=src[i], dst_ref=mine, send_sem=send_sems.at[i], recv_sem=recv_sems.at[i],
                device_id=sib, device_id_type=MESH))
            recvs.append(pltpu.make_async_remote_copy(
                src_ref=src[i], dst_ref=theirs, send_sem=send_sems.at[i], recv_sem=recv_sems.at[i],
                device_id=sib, device_id_type=MESH))
        for cp in local:
            cp.start()
        _run_copies(sends, sends, recvs)
        for cp in local:
            cp.wait()

    return pl.pallas_call(
        body, name="reduce_pair_gather", in_specs=[ANY] * n, out_specs=[ANY] * n,
        out_shape=[jax.ShapeDtypeStruct((2 * h.shape[0], h.shape[1]), h.dtype) for h in halves],
        scratch_shapes=[pltpu.SemaphoreType.DMA((n,)), pltpu.SemaphoreType.DMA((n,)),
                        pltpu.SemaphoreType.DMA((n,))])(*halves)


def _row_block(rows, cols, budget=2 * 1024 * 1024):
    rb = max(8, (budget // (4 * cols)) // 8 * 8)
    while rows % rb:
        rb -= 8
    return rb if rb > 0 else rows


def _sum_slots(name, first, rest):
    R, Cc = first.shape
    K = rest.shape[0]
    rb = _row_block(R, Cc)

    def body(f_ref, r_ref, o_ref):
        acc = f_ref[...]
        for j in range(K):
            acc = acc + r_ref[j]
        o_ref[...] = acc

    return pl.pallas_call(
        body, name=name, grid=(R // rb,),
        in_specs=[pl.BlockSpec((rb, Cc), lambda i: (i, 0)), pl.BlockSpec((K, rb, Cc), lambda i: (0, i, 0))],
        out_specs=pl.BlockSpec((rb, Cc), lambda i: (i, 0)),
        out_shape=jax.ShapeDtypeStruct((R, Cc), F32), compiler_params=_cp("parallel"))(first, rest)


def _adamw(name, w, g, m, v):
    R, Cc = w.shape
    rb = _row_block(R, Cc, 1024 * 1024)

    def body(w_ref, g_ref, m_ref, v_ref, d_ref, mo_ref, vo_ref):
        gv = g_ref[...]
        mn = ADAM_B1 * m_ref[...] + (1.0 - ADAM_B1) * gv
        vn = ADAM_B2 * v_ref[...] + (1.0 - ADAM_B2) * (gv * gv)
        m_hat = mn / (1.0 - ADAM_B1 ** ADAM_STEP)
        v_hat = vn / (1.0 - ADAM_B2 ** ADAM_STEP)
        d_ref[...] = -ADAM_LR * (m_hat / (jnp.sqrt(v_hat) + ADAM_EPS) + ADAM_WD * w_ref[...])
        mo_ref[...] = mn
        vo_ref[...] = vn

    blk = pl.BlockSpec((rb, Cc), lambda i: (i, 0))
    return pl.pallas_call(
        body, name=name, grid=(R // rb,), in_specs=[blk] * 4, out_specs=[blk] * 3,
        out_shape=[jax.ShapeDtypeStruct((R, Cc), F32)] * 3, compiler_params=_cp("parallel"))(w, g, m, v)


def _pack(arrs):
    rows = []
    for a in arrs:
        flat = a.reshape(-1)
        pad = (-flat.shape[0]) % 128
        rows.append(jnp.pad(flat, (0, pad)).reshape(-1, 128))
    buf = jnp.concatenate(rows, axis=0)
    return jnp.pad(buf, ((0, (-buf.shape[0]) % 8), (0, 0)))


def _unpack(buf, shapes):
    out, r = [], 0
    for s in shapes:
        n = math.prod(s)
        nr = -(-n // 128)
        out.append(buf[r:r + nr].reshape(-1)[:n].reshape(s))
        r += nr
    return out


BIG = ("w_in", "w_out", "w_up", "w_down")
CONV = ("dn_conv_w", "ffn_conv_w")
REPL = ("attn_norm_g", "dn_a_log", "dn_dt_bias", "dn_out_norm_g", "sg_norm_g", "sg_w", "sg_b",
        "ffn_norm_g", "ffn_conv_b", "final_norm_g")
ORDER = ("attn_norm_g", "w_in", "dn_conv_w", "dn_a_log", "dn_dt_bias", "dn_out_norm_g", "sg_norm_g", "sg_w",
         "sg_b", "w_out", "ffn_norm_g", "w_up", "ffn_conv_w", "ffn_conv_b", "w_down", "final_norm_g")


def kernel(x, attn_norm_g, w_in, dn_conv_w, dn_a_log, dn_dt_bias, dn_out_norm_g, sg_norm_g, sg_w, sg_b, w_out, ffn_norm_g, w_up, ffn_conv_w, ffn_conv_b, w_down, final_norm_g, loss_target, m_attn_norm_g, m_w_in, m_dn_conv_w, m_dn_a_log, m_dn_dt_bias, m_dn_out_norm_g, m_sg_norm_g, m_sg_w, m_sg_b, m_w_out, m_ffn_norm_g, m_w_up, m_ffn_conv_w, m_ffn_conv_b, m_w_down, m_final_norm_g, v_attn_norm_g, v_w_in, v_dn_conv_w, v_dn_a_log, v_dn_dt_bias, v_dn_out_norm_g, v_sg_norm_g, v_sg_w, v_sg_b, v_w_out, v_ffn_norm_g, v_w_up, v_ffn_conv_w, v_ffn_conv_b, v_w_down, v_final_norm_g):
    W = dict(attn_norm_g=attn_norm_g, w_in=w_in, dn_conv_w=dn_conv_w, dn_a_log=dn_a_log, dn_dt_bias=dn_dt_bias,
             dn_out_norm_g=dn_out_norm_g, sg_norm_g=sg_norm_g, sg_w=sg_w, sg_b=sg_b, w_out=w_out,
             ffn_norm_g=ffn_norm_g, w_up=w_up, ffn_conv_w=ffn_conv_w, ffn_conv_b=ffn_conv_b, w_down=w_down,
             final_norm_g=final_norm_g)
    Mo = dict(attn_norm_g=m_attn_norm_g, w_in=m_w_in, dn_conv_w=m_dn_conv_w, dn_a_log=m_dn_a_log,
              dn_dt_bias=m_dn_dt_bias, dn_out_norm_g=m_dn_out_norm_g, sg_norm_g=m_sg_norm_g, sg_w=m_sg_w,
              sg_b=m_sg_b, w_out=m_w_out, ffn_norm_g=m_ffn_norm_g, w_up=m_w_up, ffn_conv_w=m_ffn_conv_w,
              ffn_conv_b=m_ffn_conv_b, w_down=m_w_down, final_norm_g=m_final_norm_g)
    Vo = dict(attn_norm_g=v_attn_norm_g, w_in=v_w_in, dn_conv_w=v_dn_conv_w, dn_a_log=v_dn_a_log,
              dn_dt_bias=v_dn_dt_bias, dn_out_norm_g=v_dn_out_norm_g, sg_norm_g=v_sg_norm_g, sg_w=v_sg_w,
              sg_b=v_sg_b, w_out=v_w_out, ffn_norm_g=v_ffn_norm_g, w_up=v_w_up, ffn_conv_w=v_ffn_conv_w,
              ffn_conv_b=v_ffn_conv_b, w_down=v_w_down, final_norm_g=v_final_norm_g)
    xi, yi, ci = lax.axis_index("x"), lax.axis_index("y"), lax.axis_index("c")
    chip = 2 * xi + yi

    shards = [W[n][0].astype(BF16) for n in BIG] + [W[n][0] for n in CONV]
    g_in, g_out, g_up, g_down, g_dnc, g_ffc = _gather_shards(shards)
    full = dict(
        w_in=jnp.pad(g_in.transpose(1, 0, 2).reshape(D_MODEL, PROJ_COLS), ((0, 0), (0, PROJ_PAD - PROJ_COLS))),
        w_out=g_out.reshape(D_MODEL, D_MODEL),
        w_up=g_up.transpose(1, 0, 2).reshape(D_MODEL, 2 * D_FF),
        w_down=g_down.reshape(D_FF, D_MODEL),
        dn_conv_w=g_dnc.transpose(1, 0, 2).reshape(4, 3 * DN_WIDTH),
        ffn_conv_w=g_ffc.transpose(1, 0, 2).reshape(3, 2 * D_FF),
        attn_norm_g=attn_norm_g, dn_a_log=dn_a_log, dn_dt_bias=dn_dt_bias, dn_out_norm_g=dn_out_norm_g,
        sg_norm_g=sg_norm_g, sg_w=sg_w[0], sg_b=sg_b[0], ffn_norm_g=ffn_norm_g, ffn_conv_b=ffn_conv_b,
        final_norm_g=final_norm_g[None])

    loss_row, grad_x, g = _local_step(x[0], loss_target[0], full)
    loss = lax.psum(loss_row[0, 0], ("x", "y", "c"))

    g4 = [g["w_in"].reshape(D_MODEL, 4, PROJ_COLS // 4).transpose(1, 0, 2),
          g["w_out"].reshape(4, D_MODEL // 4, D_MODEL),
          g["w_up"].reshape(D_MODEL, 4, 2 * D_FF // 4).transpose(1, 0, 2),
          g["w_down"].reshape(4, D_FF // 4, D_MODEL)]
    small_names = REPL + CONV
    small_shapes = [g[n].shape for n in small_names]
    small = _pack([g[n] for n in small_names])
    *recv0, small_all = _pair_halves_and_small(g4, small)
    q4 = []
    for i, (gi, ri) in enumerate(zip(g4, recv0)):
        r2 = gi.shape[1] // 2
        mine = lax.dynamic_slice_in_dim(gi, ci * r2, r2, axis=1)
        q4.append(_sum_slots(f"sum_pair_{i}", mine.reshape(4 * r2, -1), ri.reshape(1, 4 * r2, -1)).reshape(ri.shape))
    recv1 = _chips_exchange(q4)
    halves = []
    for i, (qi, ri) in enumerate(zip(q4, recv1)):
        own = lax.dynamic_index_in_dim(qi, chip, axis=0, keepdims=False)
        halves.append(_sum_slots(f"sum_chips_{i}", own, ri))
    big_grads = dict(zip(BIG, _pair_allgather(halves)))
    small_sum = _sum_slots("sum_small", small_all[0], small_all[1:])
    sg = dict(zip(small_names, _unpack(small_sum, small_shapes)))
    sg["dn_conv_w"] = lax.dynamic_slice_in_dim(sg["dn_conv_w"], chip * (3 * DN_WIDTH // 4), 3 * DN_WIDTH // 4, axis=1)
    sg["ffn_conv_w"] = lax.dynamic_slice_in_dim(sg["ffn_conv_w"], chip * (2 * D_FF // 4), 2 * D_FF // 4, axis=1)

    grads, delta, new_m, new_v = {}, {}, {}, {}
    for n in BIG:
        shp = W[n].shape
        grads[n] = big_grads[n].reshape(shp)
        d, mn, vn = _adamw(f"adamw_{n}", W[n][0], big_grads[n], Mo[n][0], Vo[n][0])
        delta[n], new_m[n], new_v[n] = d.reshape(shp), mn.reshape(shp), vn.reshape(shp)
    shapes = [W[n].shape for n in small_names]
    for n in small_names:
        grads[n] = sg[n].reshape(W[n].shape)
    d, mn, vn = _adamw("adamw_small", _pack([W[n] for n in small_names]), _pack([grads[n] for n in small_names]),
                       _pack([Mo[n] for n in small_names]), _pack([Vo[n] for n in small_names]))
    for dst, buf in ((delta, d), (new_m, mn), (new_v, vn)):
        dst.update(zip(small_names, _unpack(buf, shapes)))

    return (loss, grad_x[None], *[grads[n] for n in ORDER], *[delta[n] for n in ORDER],
            *[new_m[n] for n in ORDER], *[new_v[n] for n in ORDER])
```

```python
import functools
import math

import jax
import jax.numpy as jnp
from jax import lax
from jax.experimental import pallas as pl
from jax.experimental.pallas import tpu as pltpu

F32 = jnp.float32
BF16 = jnp.bfloat16

D_MODEL = 1024
CHUNK = 64
HEAD_DIM = 128
N_HEADS = 4
DN_WIDTH = 512
SG_WIDTH = 512
SG_GROUPS = 4
SG_BLOCK = 128
D_FF = 2816
PROJ_COLS = 3080
PROJ_PAD = 3200
BA_COL = 3072
EPS = 1e-6
NEG = -1e30
VMEM_LIMIT = 56 * 1024 * 1024

ADAM_LR = 0.001
ADAM_B1 = 0.9
ADAM_B2 = 0.999
ADAM_EPS = 1e-08
ADAM_WD = 0.01
ADAM_STEP = 10

MESH = pl.DeviceIdType.MESH
ANY = pl.BlockSpec(memory_space=pl.ANY)


def _cp(*sem):
    return pltpu.CompilerParams(dimension_semantics=sem, vmem_limit_bytes=VMEM_LIMIT)


def _bf(a):
    return a.astype(BF16)


def _nn(a, b):
    return jnp.dot(_bf(a), _bf(b), preferred_element_type=F32)


def _nt(a, b):
    return lax.dot_general(_bf(a), _bf(b), (((1,), (1,)), ((), ())), preferred_element_type=F32)


def _tn(a, b):
    return lax.dot_general(_bf(a), _bf(b), (((0,), (0,)), ((), ())), preferred_element_type=F32)


def _split(a):
    hi = _bf(a)
    return hi, _bf(a - hi.astype(F32))


def _sigmoid(x):
    return 1.0 / (1.0 + jnp.exp(-x))


def _silu(x):
    return x * _sigmoid(x)


def _dsilu(x):
    s = _sigmoid(x)
    return s * (1.0 + x * (1.0 - s))


_GELU_C = math.sqrt(2.0 / math.pi)
_GELU_A = 0.044715


def _gelu(x):
    return 0.5 * x * (1.0 + jnp.tanh(_GELU_C * (x + _GELU_A * x * x * x)))


def _dgelu(x):
    t = jnp.tanh(_GELU_C * (x + _GELU_A * x * x * x))
    return 0.5 * (1.0 + t) + 0.5 * x * (1.0 - t * t) * _GELU_C * (1.0 + 3.0 * _GELU_A * x * x)


def _softplus(x):
    return jnp.maximum(x, 0.0) + jnp.log(1.0 + jnp.exp(-jnp.abs(x)))


def _mm_nn(name, a, b, out_dtype, tm, tn, res=None):
    M, K = a.shape
    N = b.shape[1]
    tm, tn = min(tm, M), min(tn, N)

    def body(*refs):
        a_ref, b_ref = refs[0], refs[1]
        o_ref = refs[-1]
        acc = _nn(a_ref[...], b_ref[...])
        if res is not None:
            acc = acc + refs[2][...]
        o_ref[...] = acc.astype(o_ref.dtype)

    in_specs = [pl.BlockSpec((tm, K), lambda i, j: (i, 0)), pl.BlockSpec((K, tn), lambda i, j: (0, j))]
    args = [a, b]
    if res is not None:
        in_specs.append(pl.BlockSpec((tm, tn), lambda i, j: (i, j)))
        args.append(res)
    return pl.pallas_call(
        body, name=name, grid=(M // tm, N // tn), in_specs=in_specs,
        out_specs=pl.BlockSpec((tm, tn), lambda i, j: (i, j)),
        out_shape=jax.ShapeDtypeStruct((M, N), out_dtype),
        compiler_params=_cp("parallel", "parallel"))(*args)


def _with_dep(in_specs, args, dep):
    if dep is None:
        return in_specs, args
    return in_specs + [ANY], args + [dep]


def _mm_nt(name, a, b, out_dtype, tm, tn, dep=None):
    M, K = a.shape
    N = b.shape[0]
    tm, tn = min(tm, M), min(tn, N)

    def body(a_ref, b_ref, *rest):
        o_ref = rest[-1]
        o_ref[...] = _nt(a_ref[...], b_ref[...]).astype(o_ref.dtype)

    in_specs, args = _with_dep(
        [pl.BlockSpec((tm, K), lambda i, j: (i, 0)), pl.BlockSpec((tn, K), lambda i, j: (j, 0))], [a, b], dep)
    return pl.pallas_call(
        body, name=name, grid=(M // tm, N // tn), in_specs=in_specs,
        out_specs=pl.BlockSpec((tm, tn), lambda i, j: (i, j)),
        out_shape=jax.ShapeDtypeStruct((M, N), out_dtype),
        compiler_params=_cp("parallel", "parallel"))(*args)


def _mm_tn(name, a, b, tm, tn, tk):
    T, M = a.shape
    N = b.shape[1]
    tm, tn, tk = min(tm, M), min(tn, N), min(tk, T)

    def body(a_ref, b_ref, o_ref):
        @pl.when(pl.program_id(2) == 0)
        def _():
            o_ref[...] = jnp.zeros_like(o_ref)
        o_ref[...] += _tn(a_ref[...], b_ref[...])

    return pl.pallas_call(
        body, name=name, grid=(M // tm, N // tn, T // tk),
        in_specs=[pl.BlockSpec((tk, tm), lambda i, j, k: (k, i)), pl.BlockSpec((tk, tn), lambda i, j, k: (k, j))],
        out_specs=pl.BlockSpec((tm, tn), lambda i, j, k: (i, j)),
        out_shape=jax.ShapeDtypeStruct((M, N), F32),
        compiler_params=_cp("parallel", "parallel", "arbitrary"))(a, b)


def _rms_fwd(name, x, g, rb=512, dep=None):
    T, Dm = x.shape
    rb = min(rb, T)

    def body(x_ref, g_ref, *rest):
        h_ref = rest[-1]
        xv = x_ref[...]
        r = lax.rsqrt(jnp.mean(xv * xv, axis=-1, keepdims=True) + EPS)
        h_ref[...] = (xv * r * g_ref[...]).astype(BF16)

    in_specs, args = _with_dep(
        [pl.BlockSpec((rb, Dm), lambda i: (i, 0)), pl.BlockSpec((1, Dm), lambda i: (0, 0))], [x, g], dep)
    return pl.pallas_call(
        body, name=name, grid=(T // rb,), in_specs=in_specs,
        out_specs=pl.BlockSpec((rb, Dm), lambda i: (i, 0)),
        out_shape=jax.ShapeDtypeStruct((T, Dm), BF16), compiler_params=_cp("parallel"))(*args)


def _rms_bwd(name, dh, x, g, dres, rb=512):
    T, Dm = x.shape
    rb = min(rb, T)

    def body(dh_ref, x_ref, g_ref, dres_ref, dx_ref, gg_ref):
        @pl.when(pl.program_id(0) == 0)
        def _():
            gg_ref[...] = jnp.zeros_like(gg_ref)
        xv = x_ref[...]
        r = lax.rsqrt(jnp.mean(xv * xv, axis=-1, keepdims=True) + EPS)
        xh = xv * r
        dhv = dh_ref[...]
        gg_ref[...] += jnp.sum(dhv * xh, axis=0, keepdims=True)
        dxh = dhv * g_ref[...]
        dx_ref[...] = dres_ref[...] + r * (dxh - xh * jnp.mean(dxh * xh, axis=-1, keepdims=True))

    row = pl.BlockSpec((rb, Dm), lambda i: (i, 0))
    vec = pl.BlockSpec((1, Dm), lambda i: (0, 0))
    return pl.pallas_call(
        body, name=name, grid=(T // rb,), in_specs=[row, row, vec, row], out_specs=[row, vec],
        out_shape=[jax.ShapeDtypeStruct((T, Dm), F32), jax.ShapeDtypeStruct((1, Dm), F32)],
        compiler_params=_cp("arbitrary"))(dh, x, g, dres)


def _loss_head(x3, tgt, g, rb=512):
    T, Dm = x3.shape
    rb = min(rb, T)

    def body(x_ref, t_ref, g_ref, loss_ref, dx_ref, gg_ref):
        @pl.when(pl.program_id(0) == 0)
        def _():
            gg_ref[...] = jnp.zeros_like(gg_ref)
            loss_ref[...] = jnp.zeros_like(loss_ref)
        xv = x_ref[...]
        r = lax.rsqrt(jnp.mean(xv * xv, axis=-1, keepdims=True) + EPS)
        xh = xv * r
        e = xh * g_ref[...] - t_ref[...]
        loss_ref[...] += jnp.zeros_like(loss_ref) + (0.5 / Dm) * jnp.sum(e * e)
        dy = e * (1.0 / Dm)
        gg_ref[...] += jnp.sum(dy * xh, axis=0, keepdims=True)
        dxh = dy * g_ref[...]
        dx_ref[...] = r * (dxh - xh * jnp.mean(dxh * xh, axis=-1, keepdims=True))

    row = pl.BlockSpec((rb, Dm), lambda i: (i, 0))
    vec = pl.BlockSpec((1, Dm), lambda i: (0, 0))
    return pl.pallas_call(
        body, name="loss_head", grid=(T // rb,), in_specs=[row, row, vec],
        out_specs=[pl.BlockSpec((1, 128), lambda i: (0, 0)), row, vec],
        out_shape=[jax.ShapeDtypeStruct((1, 128), F32), jax.ShapeDtypeStruct((T, Dm), F32),
                   jax.ShapeDtypeStruct((1, Dm), F32)],
        compiler_params=_cp("arbitrary"))(x3, tgt, g)


def _prev_rows(cur, halo, i):
    return jnp.concatenate([jnp.where(i > 0, halo, 0.0), cur], axis=0)


def _shift_down(ext, k, rb):
    if k == 0:
        return ext[8:8 + rb]
    return pltpu.roll(ext, k, 0)[8:8 + rb]


def _conv_fwd_vals(ext, w, rb):
    K = w.shape[0]
    out = _shift_down(ext, K - 1, rb) * w[0:1]
    for k in range(1, K):
        out = out + _shift_down(ext, K - 1 - k, rb) * w[k:k + 1]
    return out


def _halo_prev_spec(rb, width, col=0):
    return pl.BlockSpec((8, width), lambda i: (jnp.maximum(i * (rb // 8) - 1, 0), col))


def _conv_bwd_in(name, dc, w, rb):
    T, W = dc.shape
    K = w.shape[0]
    rb = min(rb, T)
    nb = T // rb

    def body(dc_ref, nx_ref, w_ref, o_ref):
        i = pl.program_id(0)
        ext = jnp.concatenate([dc_ref[...], jnp.where(i < nb - 1, nx_ref[...], 0.0)], axis=0)
        wv = w_ref[...]
        out = ext[0:rb] * wv[K - 1:K]
        for k in range(K - 1):
            s = K - 1 - k
            out = out + pltpu.roll(ext, rb + 8 - s, 0)[0:rb] * wv[k:k + 1]
        o_ref[...] = out.astype(BF16)

    return pl.pallas_call(
        body, name=name, grid=(nb,),
        in_specs=[pl.BlockSpec((rb, W), lambda i: (i, 0)),
                  pl.BlockSpec((8, W), lambda i: (jnp.minimum((i + 1) * (rb // 8), T // 8 - 1), 0)),
                  pl.BlockSpec((K, W), lambda i: (0, 0))],
        out_specs=pl.BlockSpec((rb, W), lambda i: (i, 0)),
        out_shape=jax.ShapeDtypeStruct((T, W), BF16), compiler_params=_cp("parallel"))(dc, dc, w)


def _ffn_act(up, w, b, rb=256):
    T, W = up.shape
    rb = min(rb, T)

    def body(up_ref, halo_ref, w_ref, b_ref, act_ref):
        ext = _prev_rows(up_ref[...], halo_ref[...], pl.program_id(0))
        c = _conv_fwd_vals(ext, w_ref[...], rb) + b_ref[...]
        act_ref[...] = (_silu(c[:, :D_FF]) * c[:, D_FF:]).astype(BF16)

    return pl.pallas_call(
        body, name="ffn_act", grid=(T // rb,),
        in_specs=[pl.BlockSpec((rb, W), lambda i: (i, 0)), _halo_prev_spec(rb, W),
                  pl.BlockSpec((3, W), lambda i: (0, 0)), pl.BlockSpec((1, W), lambda i: (0, 0))],
        out_specs=pl.BlockSpec((rb, D_FF), lambda i: (i, 0)),
        out_shape=jax.ShapeDtypeStruct((T, D_FF), BF16), compiler_params=_cp("parallel"))(up, up, w, b)


def _ffn_act_bwd(up, dact, w, b, rb=256, dep=None):
    T, W = up.shape
    rb = min(rb, T)

    def body(up_ref, halo_ref, da_ref, w_ref, b_ref, *rest):
        dc_ref, gw_ref, gb_ref = rest[-3:]
        @pl.when(pl.program_id(0) == 0)
        def _():
            gw_ref[...] = jnp.zeros_like(gw_ref)
            gb_ref[...] = jnp.zeros_like(gb_ref)
        ext = _prev_rows(up_ref[...], halo_ref[...], pl.program_id(0))
        c = _conv_fwd_vals(ext, w_ref[...], rb) + b_ref[...]
        gpre, vv = c[:, :D_FF], c[:, D_FF:]
        da = da_ref[...]
        dc = jnp.concatenate([da * vv * _dsilu(gpre), da * _silu(gpre)], axis=1)
        dc_ref[...] = dc
        gb_ref[...] += jnp.sum(dc, axis=0, keepdims=True)
        for k in range(3):
            gw_ref[k:k + 1, :] += jnp.sum(_shift_down(ext, 2 - k, rb) * dc, axis=0, keepdims=True)

    in_specs, args = _with_dep(
        [pl.BlockSpec((rb, W), lambda i: (i, 0)), _halo_prev_spec(rb, W),
         pl.BlockSpec((rb, D_FF), lambda i: (i, 0)),
         pl.BlockSpec((3, W), lambda i: (0, 0)), pl.BlockSpec((1, W), lambda i: (0, 0))],
        [up, up, dact, w, b], dep)
    return pl.pallas_call(
        body, name="ffn_act_bwd", grid=(T // rb,), in_specs=in_specs,
        out_specs=[pl.BlockSpec((rb, W), lambda i: (i, 0)), pl.BlockSpec((3, W), lambda i: (0, 0)),
                   pl.BlockSpec((1, W), lambda i: (0, 0))],
        out_shape=[jax.ShapeDtypeStruct((T, W), F32), jax.ShapeDtypeStruct((3, W), F32),
                   jax.ShapeDtypeStruct((1, W), F32)],
        compiler_params=_cp("arbitrary"))(*args)


def _lane_iota(shape):
    return lax.broadcasted_iota(jnp.int32, shape, len(shape) - 1)


def _dn_act(p, conv_w, alog_row, dtb_row, rb=256):
    T = p.shape[0]
    rb = min(rb, T)
    W3 = 3 * DN_WIDTH

    def body(p_ref, halo_ref, ba_ref, w_ref, al_ref, dt_ref, q_ref, k_ref, v_ref, bg_ref):
        ext = _prev_rows(p_ref[...], halo_ref[...], pl.program_id(0))
        s = _silu(_conv_fwd_vals(ext, w_ref[...], rb))
        for h in range(N_HEADS):
            lo = h * HEAD_DIM
            sq = s[:, lo:lo + HEAD_DIM]
            n = lax.rsqrt(jnp.sum(sq * sq, axis=-1, keepdims=True) + EPS)
            q_ref[:, lo:lo + HEAD_DIM] = sq * n * (HEAD_DIM ** -0.5)
            sk = s[:, DN_WIDTH + lo:DN_WIDTH + lo + HEAD_DIM]
            n = lax.rsqrt(jnp.sum(sk * sk, axis=-1, keepdims=True) + EPS)
            k_ref[:, lo:lo + HEAD_DIM] = sk * n
        v_ref[...] = s[:, 2 * DN_WIDTH:]
        ba = ba_ref[...]
        lane = _lane_iota(ba.shape)
        beta = _sigmoid(ba)
        g = -jnp.exp(al_ref[...]) * _softplus(ba + dt_ref[...])
        bg_ref[...] = jnp.where(lane < N_HEADS, beta, jnp.where(lane < 2 * N_HEADS, g, 0.0))

    row512 = pl.BlockSpec((rb, DN_WIDTH), lambda i: (i, 0))
    row128 = pl.BlockSpec((rb, 128), lambda i: (i, 0))
    vec128 = pl.BlockSpec((1, 128), lambda i: (0, 0))
    return pl.pallas_call(
        body, name="dn_act", grid=(T // rb,),
        in_specs=[pl.BlockSpec((rb, W3), lambda i: (i, 0)), _halo_prev_spec(rb, W3),
                  pl.BlockSpec((rb, 128), lambda i: (i, BA_COL // 128)),
                  pl.BlockSpec((4, W3), lambda i: (0, 0)), vec128, vec128],
        out_specs=[row512, row512, row512, row128],
        out_shape=[jax.ShapeDtypeStruct((T, DN_WIDTH), F32)] * 3 + [jax.ShapeDtypeStruct((T, 128), F32)],
        compiler_params=_cp("parallel"))(p, p, p, conv_w, alog_row, dtb_row)


def _dn_act_bwd(p, conv_w, alog_row, dtb_row, dq, dk, dv, dbg, rb=256):
    T = p.shape[0]
    rb = min(rb, T)
    W3 = 3 * DN_WIDTH

    def body(p_ref, halo_ref, ba_ref, w_ref, al_ref, dt_ref, dq_ref, dk_ref, dv_ref, dbg_ref,
             dc_ref, dba_ref, gw_ref, gad_ref):
        @pl.when(pl.program_id(0) == 0)
        def _():
            gw_ref[...] = jnp.zeros_like(gw_ref)
            gad_ref[...] = jnp.zeros_like(gad_ref)
        ext = _prev_rows(p_ref[...], halo_ref[...], pl.program_id(0))
        c = _conv_fwd_vals(ext, w_ref[...], rb)
        s = _silu(c)
        ds = _dsilu(c)
        for h in range(N_HEADS):
            lo = h * HEAD_DIM
            for (off, d_ref, scale) in ((0, dq_ref, HEAD_DIM ** -0.5), (DN_WIDTH, dk_ref, 1.0)):
                sv = s[:, off + lo:off + lo + HEAD_DIM]
                n = lax.rsqrt(jnp.sum(sv * sv, axis=-1, keepdims=True) + EPS)
                hat = sv * n
                dvv = d_ref[:, lo:lo + HEAD_DIM]
                dsv = (n * scale) * (dvv - hat * jnp.sum(hat * dvv, axis=-1, keepdims=True))
                dc_ref[:, off + lo:off + lo + HEAD_DIM] = dsv * ds[:, off + lo:off + lo + HEAD_DIM]
        dc_ref[:, 2 * DN_WIDTH:] = dv_ref[...] * ds[:, 2 * DN_WIDTH:]
        dc = dc_ref[...]
        for k in range(4):
            gw_ref[k:k + 1, :] += jnp.sum(_shift_down(ext, 3 - k, rb) * dc, axis=0, keepdims=True)
        ba = ba_ref[...]
        dbg = dbg_ref[...]
        lane = _lane_iota(ba.shape)
        beta = _sigmoid(ba)
        ea = jnp.exp(al_ref[...])
        z = ba + dt_ref[...]
        d_a = dbg * (-ea) * _sigmoid(z)
        dba = jnp.where(lane < N_HEADS, dbg * beta * (1.0 - beta), jnp.where(lane < 2 * N_HEADS, d_a, 0.0))
        dba_ref[...] = dba.astype(BF16)
        isg = (lane >= N_HEADS) & (lane < 2 * N_HEADS)
        g = -ea * _softplus(z)
        gad_ref[0:1, :] += jnp.sum(jnp.where(isg, dbg * g, 0.0), axis=0, keepdims=True)
        gad_ref[1:2, :] += jnp.sum(jnp.where(isg, d_a, 0.0), axis=0, keepdims=True)

    row512 = pl.BlockSpec((rb, DN_WIDTH), lambda i: (i, 0))
    row128 = pl.BlockSpec((rb, 128), lambda i: (i, 0))
    vec128 = pl.BlockSpec((1, 128), lambda i: (0, 0))
    return pl.pallas_call(
        body, name="dn_act_bwd", grid=(T // rb,),
        in_specs=[pl.BlockSpec((rb, W3), lambda i: (i, 0)), _halo_prev_spec(rb, W3),
                  pl.BlockSpec((rb, 128), lambda i: (i, BA_COL // 128)),
                  pl.BlockSpec((4, W3), lambda i: (0, 0)), vec128, vec128,
                  row512, row512, row512, row128],
        out_specs=[pl.BlockSpec((rb, W3), lambda i: (i, 0)), row128,
                   pl.BlockSpec((4, W3), lambda i: (0, 0)), pl.BlockSpec((2, 128), lambda i: (0, 0))],
        out_shape=[jax.ShapeDtypeStruct((T, W3), F32), jax.ShapeDtypeStruct((T, 128), BF16),
                   jax.ShapeDtypeStruct((4, W3), F32), jax.ShapeDtypeStruct((2, 128), F32)],
        compiler_params=_cp("arbitrary"))(p, p, p, conv_w, alog_row, dtb_row, dq, dk, dv, dbg)


def _tri(incl):
    ii = lax.broadcasted_iota(jnp.int32, (CHUNK, CHUNK), 0)
    jj = lax.broadcasted_iota(jnp.int32, (CHUNK, CHUNK), 1)
    return ii, jj, ((ii >= jj) if incl else (ii > jj))


def _dn_chunk(k, bg):
    T = k.shape[0]
    N = T // CHUNK

    def body(k_ref, bg_ref, gc_ref, gct_ref, l_ref):
        ii, jj, incl = _tri(True)
        bgv = bg_ref[...]
        gc = jnp.dot(incl.astype(F32), bgv, precision=lax.Precision.HIGHEST, preferred_element_type=F32)
        gc_ref[...] = gc
        gct = gc.T
        gct_ref[0] = gct[0:8]
        for h in range(N_HEADS):
            kh = k_ref[:, h * HEAD_DIM:(h + 1) * HEAD_DIM]
            beta = bgv[:, h:h + 1]
            gcol = gc[:, N_HEADS + h:N_HEADS + h + 1]
            grow = gct[N_HEADS + h:N_HEADS + h + 1, :]
            gam = jnp.exp(jnp.where(ii > jj, gcol - grow, NEG))
            l_ref[0, h] = _nt(kh * beta, kh) * gam

    return pl.pallas_call(
        body, name="dn_chunk", grid=(N,),
        in_specs=[pl.BlockSpec((CHUNK, DN_WIDTH), lambda n: (n, 0)), pl.BlockSpec((CHUNK, 128), lambda n: (n, 0))],
        out_specs=[pl.BlockSpec((CHUNK, 128), lambda n: (n, 0)), pl.BlockSpec((1, 8, CHUNK), lambda n: (n, 0, 0)),
                   pl.BlockSpec((1, N_HEADS, CHUNK, CHUNK), lambda n: (n, 0, 0, 0))],
        out_shape=[jax.ShapeDtypeStruct((T, 128), F32), jax.ShapeDtypeStruct((N, 8, CHUNK), F32),
                   jax.ShapeDtypeStruct((N, N_HEADS, CHUNK, CHUNK), F32)],
        compiler_params=_cp("parallel"))(k, bg)


def _tri_inv(lt):
    S = lt.shape[1]

    def body(l_ref, a_ref):
        col = lax.broadcasted_iota(jnp.int32, (CHUNK, S), 0)
        for i in range(CHUNK):
            def step(j, acc):
                return acc - l_ref[pl.ds(i * CHUNK + j, 1), :] * a_ref[j]
            a_ref[i] = lax.fori_loop(0, i, step, (col == i).astype(F32))

    return pl.pallas_call(
        body, name="tri_inv", out_shape=jax.ShapeDtypeStruct((CHUNK, CHUNK, S), F32),
        compiler_params=pltpu.CompilerParams(vmem_limit_bytes=VMEM_LIMIT))(lt)


def _dn_head_terms(qh, kh, vh, beta, gcol, grow):
    ii, jj, incl = _tri(True)
    gam = jnp.exp(jnp.where(incl, gcol - grow, NEG))
    glast = grow[:, CHUNK - 1:CHUNK]
    E = jnp.exp(gcol)
    Fd = jnp.exp(glast - gcol)
    cd = jnp.exp(glast)
    kb = kh * beta
    return dict(ii=ii, jj=jj, gam=gam, E=E, F=Fd, cd=cd, kb=kb, vb=vh * beta, W=kb * E, qE=qh * E, kt=kh * Fd)


def _apply_a(a, u):
    hi, lo = _split(a)
    ub = _bf(u)
    return jnp.dot(hi, ub, preferred_element_type=F32) + jnp.dot(lo, ub, preferred_element_type=F32)


def _apply_at(a, u):
    hi, lo = _split(a)
    ub = _bf(u)
    dn = (((0,), (0,)), ((), ()))
    return (lax.dot_general(hi, ub, dn, preferred_element_type=F32)
            + lax.dot_general(lo, ub, dn, preferred_element_type=F32))


def _dn_scan(q, k, v, bg, gc, gct, a):
    T = q.shape[0]
    N = T // CHUNK

    def body(q_ref, k_ref, v_ref, bg_ref, gc_ref, gct_ref, a_ref, o_ref, sall_ref, s_ref):
        @pl.when(pl.program_id(0) == 0)
        def _():
            s_ref[...] = jnp.zeros_like(s_ref)
        bgv, gcv, gctv = bg_ref[...], gc_ref[...], gct_ref[0]
        for h in range(N_HEADS):
            sl = slice(h * HEAD_DIM, (h + 1) * HEAD_DIM)
            qh, kh, vh = q_ref[:, sl], k_ref[:, sl], v_ref[:, sl]
            t = _dn_head_terms(qh, kh, vh, bgv[:, h:h + 1], gcv[:, N_HEADS + h:N_HEADS + h + 1],
                               gctv[N_HEADS + h:N_HEADS + h + 1, :])
            S = s_ref[h]
            sall_ref[0, h] = S
            vn = _apply_a(a_ref[0, h], t["vb"] - _nn(t["W"], S))
            P = _nt(qh, kh) * t["gam"]
            o_ref[:, sl] = _nn(t["qE"], S) + _nn(P, vn)
            s_ref[h] = t["cd"] * S + _tn(t["kt"], vn)

    row512 = pl.BlockSpec((CHUNK, DN_WIDTH), lambda n: (n, 0))
    row128 = pl.BlockSpec((CHUNK, 128), lambda n: (n, 0))
    return pl.pallas_call(
        body, name="dn_scan", grid=(N,),
        in_specs=[row512, row512, row512, row128, row128, pl.BlockSpec((1, 8, CHUNK), lambda n: (n, 0, 0)),
                  pl.BlockSpec((1, N_HEADS, CHUNK, CHUNK), lambda n: (n, 0, 0, 0))],
        out_specs=[row512, pl.BlockSpec((1, N_HEADS, HEAD_DIM, HEAD_DIM), lambda n: (n, 0, 0, 0))],
        out_shape=[jax.ShapeDtypeStruct((T, DN_WIDTH), F32),
                   jax.ShapeDtypeStruct((N, N_HEADS, HEAD_DIM, HEAD_DIM), F32)],
        scratch_shapes=[pltpu.VMEM((N_HEADS, HEAD_DIM, HEAD_DIM), F32)],
        compiler_params=_cp("arbitrary"))(q, k, v, bg, gc, gct, a)


def _dn_scan_bwd(q, k, v, bg, gc, gct, a, sall, do):
    T = q.shape[0]
    N = T // CHUNK

    def body(q_ref, k_ref, v_ref, bg_ref, gc_ref, gct_ref, a_ref, sall_ref, do_ref,
             dq_ref, dk_ref, dv_ref, dbg_ref, ds_ref):
        @pl.when(pl.program_id(0) == 0)
        def _():
            ds_ref[...] = jnp.zeros_like(ds_ref)
        bgv, gcv, gctv = bg_ref[...], gc_ref[...], gct_ref[0]
        lane = _lane_iota((CHUNK, 128))
        rowi = lax.broadcasted_iota(jnp.int32, (CHUNK, 1), 0)
        dbeta_arr = jnp.zeros((CHUNK, 128), F32)
        dgc_arr = jnp.zeros((CHUNK, 128), F32)
        for h in range(N_HEADS):
            sl = slice(h * HEAD_DIM, (h + 1) * HEAD_DIM)
            qh, kh, vh, dO = q_ref[:, sl], k_ref[:, sl], v_ref[:, sl], do_ref[:, sl]
            beta = bgv[:, h:h + 1]
            t = _dn_head_terms(qh, kh, vh, beta, gcv[:, N_HEADS + h:N_HEADS + h + 1],
                               gctv[N_HEADS + h:N_HEADS + h + 1, :])
            ii, jj, gam, E, Fd, cd, kb = t["ii"], t["jj"], t["gam"], t["E"], t["F"], t["cd"], t["kb"]
            S = sall_ref[0, h]
            dSn = ds_ref[h]
            A = a_ref[0, h]
            KK = _nt(kb, kh)
            QK = _nt(qh, kh)
            P = QK * gam
            U = t["vb"] - _nn(t["W"], S)
            vn = _apply_a(A, U)
            d_vn = _tn(P, dO) + _nn(t["kt"], dSn)
            d_kt = _nt(vn, dSn)
            d_cd = jnp.sum(S * dSn)
            d_qE = _nt(dO, S)
            dP = jnp.where(ii >= jj, _nt(dO, vn), 0.0)
            dU = _apply_at(A, d_vn)
            dL = jnp.where(ii > jj, -_nt(dU, vn), 0.0)
            dW = -_nt(dU, S)
            ds_ref[h] = cd * dSn + _tn(t["qE"], dO) - _tn(t["W"], dU)
            dQK = dP * gam
            dKK = dL * gam
            Z = dQK * QK + dKK * KK
            dq_ref[:, sl] = _nn(dQK, kh) + d_qE * E
            d_kb = _nn(dKK, kh) + dW * E
            dk_ref[:, sl] = _tn(dQK, qh) + _tn(dKK, kb) + d_kb * beta + d_kt * Fd
            dv_ref[:, sl] = dU * beta
            dbeta = jnp.sum(dU * vh + d_kb * kh, axis=-1, keepdims=True)
            dE = jnp.sum(dW * kb + d_qE * qh, axis=-1, keepdims=True)
            dFF = jnp.sum(d_kt * kh, axis=-1, keepdims=True) * Fd
            dgc = (dE * E - dFF + jnp.sum(Z, axis=-1, keepdims=True) - jnp.sum(Z.T, axis=-1, keepdims=True)
                   + jnp.where(rowi == CHUNK - 1, jnp.sum(dFF) + d_cd * cd, 0.0))
            dbeta_arr = dbeta_arr + jnp.where(lane == h, dbeta, 0.0)
            dgc_arr = dgc_arr + jnp.where(lane == N_HEADS + h, dgc, 0.0)
        ii, jj, _ = _tri(True)
        rev = (jj >= ii).astype(F32)
        dbg_ref[...] = dbeta_arr + jnp.dot(rev, dgc_arr, precision=lax.Precision.HIGHEST,
                                           preferred_element_type=F32)

    row512 = pl.BlockSpec((CHUNK, DN_WIDTH), lambda n: (N - 1 - n, 0))
    row128 = pl.BlockSpec((CHUNK, 128), lambda n: (N - 1 - n, 0))
    return pl.pallas_call(
        body, name="dn_scan_bwd", grid=(N,),
        in_specs=[row512, row512, row512, row128, row128,
                  pl.BlockSpec((1, 8, CHUNK), lambda n: (N - 1 - n, 0, 0)),
                  pl.BlockSpec((1, N_HEADS, CHUNK, CHUNK), lambda n: (N - 1 - n, 0, 0, 0)),
                  pl.BlockSpec((1, N_HEADS, HEAD_DIM, HEAD_DIM), lambda n: (N - 1 - n, 0, 0, 0)), row512],
        out_specs=[row512, row512, row512, row128],
        out_shape=[jax.ShapeDtypeStruct((T, DN_WIDTH), F32)] * 3 + [jax.ShapeDtypeStruct((T, 128), F32)],
        scratch_shapes=[pltpu.VMEM((N_HEADS, HEAD_DIM, HEAD_DIM), F32)],
        compiler_params=_cp("arbitrary"))(q, k, v, bg, gc, gct, a, sall, do)


def _sg_mask():
    ii = lax.broadcasted_iota(jnp.int32, (SG_BLOCK, SG_BLOCK), 0) // CHUNK
    jj = lax.broadcasted_iota(jnp.int32, (SG_BLOCK, SG_BLOCK), 1) // CHUNK
    return jj <= ii


def _mix_fwd(o, p, ong, sgn, sgw, sgbt):
    T = o.shape[0]
    rb = SG_BLOCK

    def body(o_ref, gate_ref, u_ref, vg_ref, ong_ref, sgn_ref, sgw_ref, sgbt_ref, mix_ref):
        mask = _sg_mask()
        gate = gate_ref[...]
        for h in range(N_HEADS):
            sl = slice(h * HEAD_DIM, (h + 1) * HEAD_DIM)
            oh = o_ref[:, sl]
            r = lax.rsqrt(jnp.mean(oh * oh, axis=-1, keepdims=True) + EPS)
            mix_ref[:, sl] = (oh * r * ong_ref[...] * _silu(gate[:, sl])).astype(BF16)
        for gi in range(SG_GROUPS):
            sl = slice(gi * SG_BLOCK, (gi + 1) * SG_BLOCK)
            gv = _gelu(vg_ref[:, sl])
            r = lax.rsqrt(jnp.mean(gv * gv, axis=-1, keepdims=True) + EPS)
            vh = gv * r * sgn_ref[:, sl]
            s = _nn(jnp.where(mask, sgw_ref[gi], 0.0), vh) + sgbt_ref[:, gi:gi + 1]
            mix_ref[:, DN_WIDTH + gi * SG_BLOCK:DN_WIDTH + (gi + 1) * SG_BLOCK] = (_gelu(u_ref[:, sl]) * s).astype(BF16)

    def col(c):
        return pl.BlockSpec((rb, 512), lambda i: (i, c))
    return pl.pallas_call(
        body, name="mix_fwd", grid=(T // rb,),
        in_specs=[pl.BlockSpec((rb, DN_WIDTH), lambda i: (i, 0)), col(3), col(4), col(5),
                  pl.BlockSpec((1, 128), lambda i: (0, 0)), pl.BlockSpec((1, SG_WIDTH), lambda i: (0, 0)),
                  pl.BlockSpec((SG_GROUPS, SG_BLOCK, SG_BLOCK), lambda i: (0, 0, 0)),
                  pl.BlockSpec((SG_BLOCK, 128), lambda i: (0, 0))],
        out_specs=pl.BlockSpec((rb, D_MODEL), lambda i: (i, 0)),
        out_shape=jax.ShapeDtypeStruct((T, D_MODEL), BF16),
        compiler_params=_cp("parallel"))(o, p, p, p, ong, sgn, sgw, sgbt)


def _mix_bwd(o, p, ong, sgn, sgw, sgbt, dmix, dep=None):
    T = o.shape[0]
    rb = SG_BLOCK

    def body(o_ref, gate_ref, u_ref, vg_ref, ong_ref, sgn_ref, sgw_ref, sgbt_ref, dmix_ref, *rest):
        do_ref, dp_ref, gong_ref, gsgn_ref, gsgw_ref, gsgbt_ref = rest[-6:]
        @pl.when(pl.program_id(0) == 0)
        def _():
            gong_ref[...] = jnp.zeros_like(gong_ref)
            gsgn_ref[...] = jnp.zeros_like(gsgn_ref)
            gsgw_ref[...] = jnp.zeros_like(gsgw_ref)
            gsgbt_ref[...] = jnp.zeros_like(gsgbt_ref)
        mask = _sg_mask()
        gate = gate_ref[...]
        lane = _lane_iota((SG_BLOCK, 128))
        for h in range(N_HEADS):
            sl = slice(h * HEAD_DIM, (h + 1) * HEAD_DIM)
            oh = o_ref[:, sl]
            dm = dmix_ref[:, sl]
            r = lax.rsqrt(jnp.mean(oh * oh, axis=-1, keepdims=True) + EPS)
            oh_hat = oh * r
            gt = gate[:, sl]
            sg = _silu(gt)
            dp_ref[:, sl] = (dm * oh_hat * ong_ref[...] * _dsilu(gt)).astype(BF16)
            dn_ = dm * sg
            gong_ref[...] += jnp.sum(dn_ * oh_hat, axis=0, keepdims=True)
            dhat = dn_ * ong_ref[...]
            do_ref[:, sl] = r * (dhat - oh_hat * jnp.mean(dhat * oh_hat, axis=-1, keepdims=True))
        for gi in range(SG_GROUPS):
            sl = slice(gi * SG_BLOCK, (gi + 1) * SG_BLOCK)
            vraw = vg_ref[:, sl]
            gv = _gelu(vraw)
            r = lax.rsqrt(jnp.mean(gv * gv, axis=-1, keepdims=True) + EPS)
            vhat = gv * r
            vn = vhat * sgn_ref[:, sl]
            wm = jnp.where(mask, sgw_ref[gi], 0.0)
            s = _nn(wm, vn) + sgbt_ref[:, gi:gi + 1]
            uraw = u_ref[:, sl]
            dm = dmix_ref[:, DN_WIDTH + gi * SG_BLOCK:DN_WIDTH + (gi + 1) * SG_BLOCK]
            dp_ref[:, DN_WIDTH + gi * SG_BLOCK:DN_WIDTH + (gi + 1) * SG_BLOCK] = (dm * s * _dgelu(uraw)).astype(BF16)
            ds = dm * _gelu(uraw)
            gsgbt_ref[...] += jnp.where(lane == gi, jnp.sum(ds, axis=-1, keepdims=True), 0.0)
            gsgw_ref[gi] += jnp.where(mask, _nt(ds, vn), 0.0)
            dvn = _tn(wm, ds)
            gsgn_ref[:, sl] += jnp.sum(dvn * vhat, axis=0, keepdims=True)
            dhat = dvn * sgn_ref[:, sl]
            dgv = r * (dhat - vhat * jnp.mean(dhat * vhat, axis=-1, keepdims=True))
            dp_ref[:, 2 * DN_WIDTH + gi * SG_BLOCK:2 * DN_WIDTH + (gi + 1) * SG_BLOCK] = (dgv * _dgelu(vraw)).astype(BF16)

    def col(c):
        return pl.BlockSpec((rb, 512), lambda i: (i, c))
    full = lambda *s: pl.BlockSpec(s, lambda i: (0,) * len(s))
    in_specs, args = _with_dep(
        [pl.BlockSpec((rb, DN_WIDTH), lambda i: (i, 0)), col(3), col(4), col(5),
         full(1, 128), full(1, SG_WIDTH), full(SG_GROUPS, SG_BLOCK, SG_BLOCK), full(SG_BLOCK, 128),
         pl.BlockSpec((rb, D_MODEL), lambda i: (i, 0))],
        [o, p, p, p, ong, sgn, sgw, sgbt, dmix], dep)
    return pl.pallas_call(
        body, name="mix_bwd", grid=(T // rb,), in_specs=in_specs,
        out_specs=[pl.BlockSpec((rb, DN_WIDTH), lambda i: (i, 0)), pl.BlockSpec((rb, 3 * 512), lambda i: (i, 0)),
                   full(1, 128), full(1, SG_WIDTH), full(SG_GROUPS, SG_BLOCK, SG_BLOCK), full(SG_BLOCK, 128)],
        out_shape=[jax.ShapeDtypeStruct((T, DN_WIDTH), F32), jax.ShapeDtypeStruct((T, 3 * 512), BF16),
                   jax.ShapeDtypeStruct((1, 128), F32), jax.ShapeDtypeStruct((1, SG_WIDTH), F32),
                   jax.ShapeDtypeStruct((SG_GROUPS, SG_BLOCK, SG_BLOCK), F32),
                   jax.ShapeDtypeStruct((SG_BLOCK, 128), F32)],
        compiler_params=_cp("arbitrary"))(*args)


def _pad_lanes(row, offset=0):
    n = row.shape[1]
    return jnp.pad(row, ((0, 0), (offset, 128 - n - offset)))


def _local_step(x, tgt, w, dep=None, late_weights=None, on_grad=None):
    T = x.shape[0]
    N = T // CHUNK
    on_grad = on_grad or (lambda name, g: None)
    alog_row = _pad_lanes(w["dn_a_log"], N_HEADS)
    dtb_row = _pad_lanes(w["dn_dt_bias"], N_HEADS)
    sgbt = jnp.pad(w["sg_b"].T, ((0, 0), (0, 128 - SG_GROUPS)))

    h1 = _rms_fwd("rms_attn", x, w["attn_norm_g"], dep=dep)
    p = _mm_nn("in_proj", h1, w["w_in"], F32, 512, 640)
    q, k, v, bg = _dn_act(p, w["dn_conv_w"], alog_row, dtb_row)
    gc, gct, lmat = _dn_chunk(k, bg)
    lt = lmat.reshape(N * N_HEADS, CHUNK * CHUNK).T
    at = _tri_inv(lt)
    a = at.reshape(CHUNK * CHUNK, N * N_HEADS).T.reshape(N, N_HEADS, CHUNK, CHUNK)
    o, sall = _dn_scan(q, k, v, bg, gc, gct, a)
    mix = _mix_fwd(o, p, w["dn_out_norm_g"], w["sg_norm_g"], w["sg_w"], sgbt)
    if late_weights is not None:
        w = {**w, **late_weights(mix)}
    x2 = _mm_nn("out_proj", mix, w["w_out"], F32, 512, 1024, res=x)
    h2 = _rms_fwd("rms_ffn", x2, w["ffn_norm_g"])
    up = _mm_nn("up_proj", h2, w["w_up"], F32, 512, 1408)
    act = _ffn_act(up, w["ffn_conv_w"], w["ffn_conv_b"])
    x3 = _mm_nn("down_proj", act, w["w_down"], F32, 512, 1024, res=x2)
    loss, dx3, g_final = _loss_head(x3, tgt, w["final_norm_g"])

    dact = _mm_nt("d_act", dx3, w["w_down"], F32, 512, 1408)
    g_w_down = _mm_tn("g_w_down", act, dx3, 1408, 1024, 1024)
    tok = on_grad("w_down", g_w_down)
    dc, g_ffn_conv_w, g_ffn_conv_b = _ffn_act_bwd(up, dact, w["ffn_conv_w"], w["ffn_conv_b"], dep=tok)
    dup = _conv_bwd_in("d_up", dc, w["ffn_conv_w"], 256)
    g_w_up = _mm_tn("g_w_up", h2, dup, 512, 1408, 1024)
    tok = on_grad("w_up", g_w_up)
    dh2 = _mm_nt("d_h2", dup, w["w_up"], F32, 512, 1024, dep=tok)
    dx2, g_ffn_norm = _rms_bwd("rms_ffn_bwd", dh2, x2, w["ffn_norm_g"], dx3)
    dmix = _mm_nt("d_mix", dx2, w["w_out"], F32, 512, 1024)
    g_w_out = _mm_tn("g_w_out", mix, dx2, 1024, 1024, 1024)
    tok = on_grad("w_out", g_w_out)
    do, dp_mid, g_ong, g_sgn, g_sgw, g_sgbt = _mix_bwd(o, p, w["dn_out_norm_g"], w["sg_norm_g"], w["sg_w"], sgbt,
                                                      dmix, dep=tok)
    dq, dk, dv, dbg = _dn_scan_bwd(q, k, v, bg, gc, gct, a, sall, do)
    dcq, dba, g_dn_conv_w, g_ad = _dn_act_bwd(p, w["dn_conv_w"], alog_row, dtb_row, dq, dk, dv, dbg)
    dp_qkv = _conv_bwd_in("d_qkv", dcq, w["dn_conv_w"], 256)
    dp = jnp.concatenate([dp_qkv, dp_mid, dba], axis=1)
    g_w_in = _mm_tn("g_w_in", h1, dp, 512, 640, 1024)[:, :PROJ_COLS]
    tok = on_grad("w_in", g_w_in)
    dh1 = _mm_nt("d_h1", dp, w["w_in"], F32, 512, 1024, dep=tok)
    grad_x, g_attn_norm = _rms_bwd("rms_attn_bwd", dh1, x, w["attn_norm_g"], dx2)

    grads = dict(
        attn_norm_g=g_attn_norm, w_in=g_w_in, dn_conv_w=g_dn_conv_w,
        dn_a_log=g_ad[0:1, N_HEADS:2 * N_HEADS], dn_dt_bias=g_ad[1:2, N_HEADS:2 * N_HEADS],
        dn_out_norm_g=g_ong, sg_norm_g=g_sgn, sg_w=g_sgw, sg_b=g_sgbt[:, :SG_GROUPS].T,
        w_out=g_w_out, ffn_norm_g=g_ffn_norm, w_up=g_w_up, ffn_conv_w=g_ffn_conv_w,
        ffn_conv_b=g_ffn_conv_b, w_down=g_w_down, final_norm_g=g_final)
    return loss, grad_x, grads


def _me():
    return lax.axis_index("x"), lax.axis_index("y"), lax.axis_index("c")


def _peer(rel):
    x, y, c = _me()
    return {"x": (1 - x, y, c), "y": (x, 1 - y, c), "xy": (1 - x, 1 - y, c), "c": (x, y, 1 - c)}[rel]


def _chip_of(dev):
    return 2 * dev[0] + dev[1]


CHIP_RELS = ("x", "y", "xy")


def _run_copies(copies, sends, recvs):
    for cp in copies:
        cp.start()
    for cp in recvs:
        cp.wait_recv()
    for cp in sends:
        cp.wait_send()


def _gather_shards(shards):
    n = len(shards)

    def body(*refs):
        src, out = refs[:n], refs[n:2 * n]
        send_sems, recv_sems, local_sems = refs[2 * n:]
        me = _chip_of(_me())
        local = [pltpu.make_async_copy(src[i], out[i].at[me], local_sems.at[i]) for i in range(n)]
        for cp in local:
            cp.start()
        sends, recvs = [], []
        for i in range(n):
            for r, rel in enumerate(CHIP_RELS):
                k = 3 * i + r
                peer = _peer(rel)
                sends.append(pltpu.make_async_remote_copy(
                    src_ref=src[i], dst_ref=out[i].at[me], send_sem=send_sems.at[k], recv_sem=recv_sems.at[k],
                    device_id=peer, device_id_type=MESH))
                recvs.append(pltpu.make_async_remote_copy(
                    src_ref=src[i], dst_ref=out[i].at[_chip_of(peer)], send_sem=send_sems.at[k],
                    recv_sem=recv_sems.at[k], device_id=peer, device_id_type=MESH))
        _run_copies(sends, sends, recvs)
        for cp in local:
            cp.wait()

    return pl.pallas_call(
        body, name="gather_weights", in_specs=[ANY] * n, out_specs=[ANY] * n,
        out_shape=[jax.ShapeDtypeStruct((4,) + s.shape, s.dtype) for s in shards],
        scratch_shapes=[pltpu.SemaphoreType.DMA((3 * n,)), pltpu.SemaphoreType.DMA((3 * n,)),
                        pltpu.SemaphoreType.DMA((n,))])(*shards)


OTHERS = tuple((fx, fy, fc) for fx in (0, 1) for fy in (0, 1) for fc in (0, 1) if (fx, fy, fc) != (0, 0, 0))


def _other(flip):
    x, y, c = _me()
    return (x ^ flip[0], y ^ flip[1], c ^ flip[2])


def _linear(dev):
    return 4 * dev[0] + 2 * dev[1] + dev[2]


def _exchange_small(small):
    def body(small_ref, out_ref, send_sems, recv_sems, local_sem):
        my_slot = _linear(_me())
        local = pltpu.make_async_copy(small_ref, out_ref.at[my_slot], local_sem)
        local.start()
        sends, recvs = [], []
        for k, flip in enumerate(OTHERS):
            peer = _other(flip)
            sends.append(pltpu.make_async_remote_copy(
                src_ref=small_ref, dst_ref=out_ref.at[my_slot], send_sem=send_sems.at[k], recv_sem=recv_sems.at[k],
                device_id=peer, device_id_type=MESH))
            recvs.append(pltpu.make_async_remote_copy(
                src_ref=small_ref, dst_ref=out_ref.at[_linear(peer)], send_sem=send_sems.at[k],
                recv_sem=recv_sems.at[k], device_id=peer, device_id_type=MESH))
        _run_copies(sends, sends, recvs)
        local.wait()

    return pl.pallas_call(
        body, name="exchange_small", in_specs=[ANY], out_specs=ANY,
        out_shape=jax.ShapeDtypeStruct((8,) + small.shape, small.dtype),
        scratch_shapes=[pltpu.SemaphoreType.DMA((7,)), pltpu.SemaphoreType.DMA((7,)), pltpu.SemaphoreType.DMA])(small)


def _pair_swap(halves):
    n = len(halves)

    def body(*refs):
        src, out = refs[:n], refs[n:2 * n]
        send_sems, recv_sems = refs[2 * n:]
        sib = _peer("c")
        copies = [pltpu.make_async_remote_copy(
            src_ref=src[i], dst_ref=out[i], send_sem=send_sems.at[i], recv_sem=recv_sems.at[i],
            device_id=sib, device_id_type=MESH) for i in range(n)]
        _run_copies(copies, copies, copies)

    return pl.pallas_call(
        body, name="pair_swap", in_specs=[ANY] * n, out_specs=[ANY] * n,
        out_shape=[jax.ShapeDtypeStruct(h.shape, h.dtype) for h in halves],
        scratch_shapes=[pltpu.SemaphoreType.DMA((n,)), pltpu.SemaphoreType.DMA((n,))])(*halves)


HBM = pl.BlockSpec(memory_space=pltpu.HBM)
SEM = pl.BlockSpec(memory_space=pltpu.SEMAPHORE)
EFFECT = pltpu.SideEffectType.DATAFLOW_SIDE_EFFECTING


def _hbm(a):
    return pltpu.with_memory_space_constraint(a, pltpu.HBM)


def _transfer_start(name, srcs, lands, n_copies, make_copies, after=None):
    n, m = len(srcs), len(lands)

    def body(*refs):
        src, land = refs[:n], refs[n:n + m]
        outs = refs[n + m + (after is not None):]
        send_sems, recv_sems, token = outs[0], outs[1], outs[-1]
        for cp in make_copies(src, land, send_sems, recv_sems):
            cp.start()
        token[...] = jnp.zeros_like(token)

    arrs = list(srcs) + list(lands)
    in_specs, args = _with_dep([HBM] * (n + m), [_hbm(a) for a in arrs], after)
    out = pl.pallas_call(
        body, name=name,
        out_shape=(pltpu.SemaphoreType.DMA((n_copies,)), pltpu.SemaphoreType.DMA((n_copies,)),
                   *[pltpu.HBM(a.shape, a.dtype) for a in arrs], jax.ShapeDtypeStruct((8, 128), F32)),
        in_specs=in_specs,
        out_specs=(SEM, SEM, *[HBM] * (n + m), pl.BlockSpec(memory_space=pltpu.VMEM)),
        input_output_aliases={i: 2 + i for i in range(n + m)},
        compiler_params=pltpu.CompilerParams(has_side_effects=EFFECT))(*args)
    return out[0], out[1], list(out[2:2 + n]), list(out[2 + n:2 + n + m]), out[-1]


def _transfer_wait(name, send_sems, recv_sems, srcs, lands, make_copies, after):
    n, m = len(srcs), len(lands)

    def body(*refs):
        src, land = refs[:n], refs[n:n + m]
        s_sems, r_sems = refs[n + m], refs[n + m + 1]
        for cp in make_copies(src, land, s_sems, r_sems):
            cp.wait_send()
            cp.wait_recv()

    arrs = list(srcs) + list(lands)
    out = pl.pallas_call(
        body, name=name, out_shape=tuple(pltpu.HBM(a.shape, a.dtype) for a in arrs),
        in_specs=[HBM] * (n + m) + [SEM, SEM, ANY], out_specs=tuple([HBM] * (n + m)),
        input_output_aliases={i: i for i in range(n + m)},
        compiler_params=pltpu.CompilerParams(has_side_effects=EFFECT))(*arrs, send_sems, recv_sems, after)
    return list(out[:n]), list(out[n:])


def _gather_copies(src, land, send_sems, recv_sems):
    me = _chip_of(_me())
    copies = []
    for i in range(len(src)):
        for r, rel in enumerate(CHIP_RELS):
            k = 3 * i + r
            copies.append(pltpu.make_async_remote_copy(
                src_ref=src[i], dst_ref=land[i].at[me], send_sem=send_sems.at[k], recv_sem=recv_sems.at[k],
                device_id=_peer(rel), device_id_type=MESH))
    return copies


def _pieces_copies(src, land, send_sems, recv_sems):
    copies = []
    for k, flip in enumerate(OTHERS):
        peer = _other(flip)
        copies.append(pltpu.make_async_remote_copy(
            src_ref=src[0].at[_linear(peer)], dst_ref=land[0].at[k], send_sem=send_sems.at[k],
            recv_sem=recv_sems.at[k], device_id=peer, device_id_type=MESH))
    return copies


def _row_block(rows, cols, budget=2 * 1024 * 1024):
    rb = max(8, (budget // (4 * cols)) // 8 * 8)
    while rows % rb:
        rb -= 8
    return rb if rb > 0 else rows


def _sum_slots(name, first, rest):
    R, Cc = first.shape
    K = rest.shape[0]
    rb = _row_block(R, Cc)

    def body(f_ref, r_ref, o_ref):
        acc = f_ref[...].astype(F32)
        for j in range(K):
            acc = acc + r_ref[j].astype(F32)
        o_ref[...] = acc

    return pl.pallas_call(
        body, name=name, grid=(R // rb,),
        in_specs=[pl.BlockSpec((rb, Cc), lambda i: (i, 0)), pl.BlockSpec((K, rb, Cc), lambda i: (0, i, 0))],
        out_specs=pl.BlockSpec((rb, Cc), lambda i: (i, 0)),
        out_shape=jax.ShapeDtypeStruct((R, Cc), F32), compiler_params=_cp("parallel"))(first, rest)


def _adamw_math(w, gv, m, v):
    mn = ADAM_B1 * m + (1.0 - ADAM_B1) * gv
    vn = ADAM_B2 * v + (1.0 - ADAM_B2) * (gv * gv)
    m_hat = mn / (1.0 - ADAM_B1 ** ADAM_STEP)
    v_hat = vn / (1.0 - ADAM_B2 ** ADAM_STEP)
    return -ADAM_LR * (m_hat / (jnp.sqrt(v_hat) + ADAM_EPS) + ADAM_WD * w), mn, vn


def _adamw_halves(name, w, mine, theirs, m, v, core):
    R, Cc = w.shape
    r2 = R // 2
    rb = _row_block(r2, Cc, 1024 * 1024)
    nb2 = r2 // rb

    def body(c_ref, w_ref, mine_ref, theirs_ref, m_ref, v_ref, g_ref, d_ref, mo_ref, vo_ref):
        is_mine = (pl.program_id(0) // nb2) == c_ref[0]
        gv = jnp.where(is_mine, mine_ref[...], theirs_ref[...])
        g_ref[...] = gv
        d_ref[...], mo_ref[...], vo_ref[...] = _adamw_math(w_ref[...], gv, m_ref[...], v_ref[...])

    blk = pl.BlockSpec((rb, Cc), lambda i, c: (i, 0))
    half = lambda own: pl.BlockSpec(
        (rb, Cc), lambda i, c: (jnp.clip(i - (c[0] if own else 1 - c[0]) * nb2, 0, nb2 - 1), 0))
    return pl.pallas_call(
        body, name=name,
        grid_spec=pltpu.PrefetchScalarGridSpec(
            num_scalar_prefetch=1, grid=(2 * nb2,), in_specs=[blk, half(True), half(False), blk, blk],
            out_specs=[blk] * 4),
        out_shape=[jax.ShapeDtypeStruct((R, Cc), F32)] * 4, compiler_params=_cp("parallel"))(core, w, mine, theirs, m, v)


def _adamw(name, w, g, m, v):
    R, Cc = w.shape
    rb = _row_block(R, Cc, 1024 * 1024)

    def body(w_ref, g_ref, m_ref, v_ref, d_ref, mo_ref, vo_ref):
        d_ref[...], mo_ref[...], vo_ref[...] = _adamw_math(w_ref[...], g_ref[...], m_ref[...], v_ref[...])

    blk = pl.BlockSpec((rb, Cc), lambda i: (i, 0))
    return pl.pallas_call(
        body, name=name, grid=(R // rb,), in_specs=[blk] * 4, out_specs=[blk] * 3,
        out_shape=[jax.ShapeDtypeStruct((R, Cc), F32)] * 3, compiler_params=_cp("parallel"))(w, g, m, v)


def _pack(arrs):
    rows = []
    for a in arrs:
        flat = a.reshape(-1)
        pad = (-flat.shape[0]) % 128
        rows.append(jnp.pad(flat, (0, pad)).reshape(-1, 128))
    buf = jnp.concatenate(rows, axis=0)
    return jnp.pad(buf, ((0, (-buf.shape[0]) % 8), (0, 0)))


def _unpack(buf, shapes):
    out, r = [], 0
    for s in shapes:
        n = math.prod(s)
        nr = -(-n // 128)
        out.append(buf[r:r + nr].reshape(-1)[:n].reshape(s))
        r += nr
    return out


BIG = ("w_in", "w_out", "w_up", "w_down")
CONV = ("dn_conv_w", "ffn_conv_w")
REPL = ("attn_norm_g", "dn_a_log", "dn_dt_bias", "dn_out_norm_g", "sg_norm_g", "sg_w", "sg_b",
        "ffn_norm_g", "ffn_conv_b", "final_norm_g")
ORDER = ("attn_norm_g", "w_in", "dn_conv_w", "dn_a_log", "dn_dt_bias", "dn_out_norm_g", "sg_norm_g", "sg_w",
         "sg_b", "w_out", "ffn_norm_g", "w_up", "ffn_conv_w", "ffn_conv_b", "w_down", "final_norm_g")


def kernel(x, attn_norm_g, w_in, dn_conv_w, dn_a_log, dn_dt_bias, dn_out_norm_g, sg_norm_g, sg_w, sg_b, w_out, ffn_norm_g, w_up, ffn_conv_w, ffn_conv_b, w_down, final_norm_g, loss_target, m_attn_norm_g, m_w_in, m_dn_conv_w, m_dn_a_log, m_dn_dt_bias, m_dn_out_norm_g, m_sg_norm_g, m_sg_w, m_sg_b, m_w_out, m_ffn_norm_g, m_w_up, m_ffn_conv_w, m_ffn_conv_b, m_w_down, m_final_norm_g, v_attn_norm_g, v_w_in, v_dn_conv_w, v_dn_a_log, v_dn_dt_bias, v_dn_out_norm_g, v_sg_norm_g, v_sg_w, v_sg_b, v_w_out, v_ffn_norm_g, v_w_up, v_ffn_conv_w, v_ffn_conv_b, v_w_down, v_final_norm_g):
    W = dict(attn_norm_g=attn_norm_g, w_in=w_in, dn_conv_w=dn_conv_w, dn_a_log=dn_a_log, dn_dt_bias=dn_dt_bias,
             dn_out_norm_g=dn_out_norm_g, sg_norm_g=sg_norm_g, sg_w=sg_w, sg_b=sg_b, w_out=w_out,
             ffn_norm_g=ffn_norm_g, w_up=w_up, ffn_conv_w=ffn_conv_w, ffn_conv_b=ffn_conv_b, w_down=w_down,
             final_norm_g=final_norm_g)
    Mo = dict(attn_norm_g=m_attn_norm_g, w_in=m_w_in, dn_conv_w=m_dn_conv_w, dn_a_log=m_dn_a_log,
              dn_dt_bias=m_dn_dt_bias, dn_out_norm_g=m_dn_out_norm_g, sg_norm_g=m_sg_norm_g, sg_w=m_sg_w,
              sg_b=m_sg_b, w_out=m_w_out, ffn_norm_g=m_ffn_norm_g, w_up=m_w_up, ffn_conv_w=m_ffn_conv_w,
              ffn_conv_b=m_ffn_conv_b, w_down=m_w_down, final_norm_g=m_final_norm_g)
    Vo = dict(attn_norm_g=v_attn_norm_g, w_in=v_w_in, dn_conv_w=v_dn_conv_w, dn_a_log=v_dn_a_log,
              dn_dt_bias=v_dn_dt_bias, dn_out_norm_g=v_dn_out_norm_g, sg_norm_g=v_sg_norm_g, sg_w=v_sg_w,
              sg_b=v_sg_b, w_out=v_w_out, ffn_norm_g=v_ffn_norm_g, w_up=v_w_up, ffn_conv_w=v_ffn_conv_w,
              ffn_conv_b=v_ffn_conv_b, w_down=v_w_down, final_norm_g=v_final_norm_g)
    xi, yi, ci = lax.axis_index("x"), lax.axis_index("y"), lax.axis_index("c")
    chip = 2 * xi + yi

    me_lin = 4 * xi + 2 * yi + ci

    g_in, g_dnc = _gather_shards([w_in[0].astype(BF16), dn_conv_w[0]])
    late = ("w_out", "w_up", "w_down", "ffn_conv_w")
    late_shards = [W[n][0].astype(BF16) for n in late[:3]] + [ffn_conv_w[0]]
    late_lands = [lax.dynamic_update_index_in_dim(lax.empty((4,) + s.shape, s.dtype), s, chip, 0) for s in late_shards]
    n_late = 3 * len(late_shards)
    ssem, rsem, late_src, late_lands, token = _transfer_start("gather_rest_start", late_shards, late_lands,
                                                              n_late, _gather_copies, after=g_in)

    def late_weights(after):
        _, (g_out, g_up, g_down, g_ffc) = _transfer_wait("gather_rest_wait", ssem, rsem, late_src, late_lands,
                                                         _gather_copies, after)
        return dict(w_out=g_out.reshape(D_MODEL, D_MODEL), w_up=g_up.transpose(1, 0, 2).reshape(D_MODEL, 2 * D_FF),
                    w_down=g_down.reshape(D_FF, D_MODEL), ffn_conv_w=g_ffc.transpose(1, 0, 2).reshape(3, 2 * D_FF))

    full = dict(
        w_in=jnp.pad(g_in.transpose(1, 0, 2).reshape(D_MODEL, PROJ_COLS), ((0, 0), (0, PROJ_PAD - PROJ_COLS))),
        dn_conv_w=g_dnc.transpose(1, 0, 2).reshape(4, 3 * DN_WIDTH),
        attn_norm_g=attn_norm_g, dn_a_log=dn_a_log, dn_dt_bias=dn_dt_bias, dn_out_norm_g=dn_out_norm_g,
        sg_norm_g=sg_norm_g, sg_w=sg_w[0], sg_b=sg_b[0], ffn_norm_g=ffn_norm_g, ffn_conv_b=ffn_conv_b,
        final_norm_g=final_norm_g[None])

    pending = {}

    def on_grad(name, gw):
        if name in ("w_in", "w_up"):
            g8 = gw.astype(BF16).reshape(D_MODEL, 4, -1).transpose(1, 0, 2)
        else:
            g8 = gw.astype(BF16)
        g8 = g8.reshape(8, -1, g8.shape[-1])
        land = lax.empty((7,) + g8.shape[1:], BF16)
        s_sem, r_sem, src, lands, tok = _transfer_start(f"reduce_{name}_start", [g8], [land], 7, _pieces_copies)
        pending[name] = (s_sem, r_sem, src, lands)
        return tok

    loss_row, grad_x, g = _local_step(x[0], loss_target[0], full, dep=token, late_weights=late_weights,
                                      on_grad=on_grad)
    loss = lax.psum(loss_row[0, 0], ("x", "y", "c"))

    small_names = REPL + CONV
    small_shapes = [g[n].shape for n in small_names]
    small_all = _exchange_small(_pack([g[n] for n in small_names]))
    small_sum = _sum_slots("sum_small", small_all[0], small_all[1:])
    sg = dict(zip(small_names, _unpack(small_sum, small_shapes)))
    sg["dn_conv_w"] = lax.dynamic_slice_in_dim(sg["dn_conv_w"], chip * (3 * DN_WIDTH // 4), 3 * DN_WIDTH // 4, axis=1)
    sg["ffn_conv_w"] = lax.dynamic_slice_in_dim(sg["ffn_conv_w"], chip * (2 * D_FF // 4), 2 * D_FF // 4, axis=1)

    halves = []
    for n in ("w_down", "w_up", "w_out", "w_in"):
        s_sem, r_sem, src, lands = pending[n]
        sent, got = _transfer_wait(f"reduce_{n}_wait", s_sem, r_sem, src, lands, _pieces_copies, grad_x)
        own = lax.dynamic_index_in_dim(sent[0], me_lin, axis=0, keepdims=False)
        halves.append(_sum_slots(f"sum_{n}", own, got[0]))
    theirs = _pair_swap(halves)
    core = ci.astype(jnp.int32).reshape(1)
    grads, delta, new_m, new_v = {}, {}, {}, {}
    for n, mine_h, their_h in zip(("w_down", "w_up", "w_out", "w_in"), halves, theirs):
        shp = W[n].shape
        gr, d, mn, vn = _adamw_halves(f"adamw_{n}", W[n][0], mine_h, their_h, Mo[n][0], Vo[n][0], core)
        grads[n], delta[n], new_m[n], new_v[n] = gr.reshape(shp), d.reshape(shp), mn.reshape(shp), vn.reshape(shp)
    shapes = [W[n].shape for n in small_names]
    for n in small_names:
        grads[n] = sg[n].reshape(W[n].shape)
    d, mn, vn = _adamw("adamw_small", _pack([W[n] for n in small_names]), _pack([grads[n] for n in small_names]),
                       _pack([Mo[n] for n in small_names]), _pack([Vo[n] for n in small_names]))
    for dst, buf in ((delta, d), (new_m, mn), (new_v, vn)):
        dst.update(zip(small_names, _unpack(buf, shapes)))

    return (loss, grad_x[None], *[grads[n] for n in ORDER], *[delta[n] for n in ORDER],
            *[new_m[n] for n in ORDER], *[new_v[n] for n in ORDER])
```

```python
import functools
import math

import jax
import jax.numpy as jnp
from jax import lax
from jax.experimental import pallas as pl
from jax.experimental.pallas import tpu as pltpu

F32 = jnp.float32
BF16 = jnp.bfloat16

D_MODEL = 1024
CHUNK = 64
HEAD_DIM = 128
N_HEADS = 4
DN_WIDTH = 512
SG_WIDTH = 512
SG_GROUPS = 4
SG_BLOCK = 128
D_FF = 2816
PROJ_COLS = 3080
PROJ_PAD = 3200
BA_COL = 3072
EPS = 1e-6
NEG = -1e30
VMEM_LIMIT = 56 * 1024 * 1024

ADAM_LR = 0.001
ADAM_B1 = 0.9
ADAM_B2 = 0.999
ADAM_EPS = 1e-08
ADAM_WD = 0.01
ADAM_STEP = 10

MESH = pl.DeviceIdType.MESH
ANY = pl.BlockSpec(memory_space=pl.ANY)


def _cp(*sem):
    return pltpu.CompilerParams(dimension_semantics=sem, vmem_limit_bytes=VMEM_LIMIT)


def _bf(a):
    return a.astype(BF16)


def _nn(a, b):
    return jnp.dot(_bf(a), _bf(b), preferred_element_type=F32)


def _nt(a, b):
    return lax.dot_general(_bf(a), _bf(b), (((1,), (1,)), ((), ())), preferred_element_type=F32)


def _tn(a, b):
    return lax.dot_general(_bf(a), _bf(b), (((0,), (0,)), ((), ())), preferred_element_type=F32)


def _split(a):
    hi = _bf(a)
    return hi, _bf(a - hi.astype(F32))


def _sigmoid(x):
    return 0.5 * jnp.tanh(0.5 * x) + 0.5


def _silu(x):
    return x * _sigmoid(x)


def _dsilu(x):
    s = _sigmoid(x)
    return s * (1.0 + x * (1.0 - s))


_GELU_C = math.sqrt(2.0 / math.pi)
_GELU_A = 0.044715


def _gelu(x):
    return 0.5 * x * (1.0 + jnp.tanh(_GELU_C * (x + _GELU_A * x * x * x)))


def _dgelu(x):
    t = jnp.tanh(_GELU_C * (x + _GELU_A * x * x * x))
    return 0.5 * (1.0 + t) + 0.5 * x * (1.0 - t * t) * _GELU_C * (1.0 + 3.0 * _GELU_A * x * x)


def _softplus(x):
    return jnp.maximum(x, 0.0) + jnp.log(1.0 + jnp.exp(-jnp.abs(x)))


def _mm_nn(name, a, b, out_dtype, tm, tn, res=None):
    M, K = a.shape
    N = b.shape[1]
    tm, tn = min(tm, M), min(tn, N)

    def body(*refs):
        a_ref, b_ref = refs[0], refs[1]
        o_ref = refs[-1]
        acc = _nn(a_ref[...], b_ref[...])
        if res is not None:
            acc = acc + refs[2][...]
        o_ref[...] = acc.astype(o_ref.dtype)

    in_specs = [pl.BlockSpec((tm, K), lambda i, j: (i, 0)), pl.BlockSpec((K, tn), lambda i, j: (0, j))]
    args = [a, b]
    if res is not None:
        in_specs.append(pl.BlockSpec((tm, tn), lambda i, j: (i, j)))
        args.append(res)
    return pl.pallas_call(
        body, name=name, grid=(M // tm, N // tn), in_specs=in_specs,
        out_specs=pl.BlockSpec((tm, tn), lambda i, j: (i, j)),
        out_shape=jax.ShapeDtypeStruct((M, N), out_dtype),
        compiler_params=_cp("parallel", "parallel"))(*args)


def _with_dep(in_specs, args, dep):
    if dep is None:
        return in_specs, args
    return in_specs + [ANY], args + [dep]


def _mm_nt(name, a, b, out_dtype, tm, tn, dep=None):
    M, K = a.shape
    N = b.shape[0]
    tm, tn = min(tm, M), min(tn, N)

    def body(a_ref, b_ref, *rest):
        o_ref = rest[-1]
        o_ref[...] = _nt(a_ref[...], b_ref[...]).astype(o_ref.dtype)

    in_specs, args = _with_dep(
        [pl.BlockSpec((tm, K), lambda i, j: (i, 0)), pl.BlockSpec((tn, K), lambda i, j: (j, 0))], [a, b], dep)
    return pl.pallas_call(
        body, name=name, grid=(M // tm, N // tn), in_specs=in_specs,
        out_specs=pl.BlockSpec((tm, tn), lambda i, j: (i, j)),
        out_shape=jax.ShapeDtypeStruct((M, N), out_dtype),
        compiler_params=_cp("parallel", "parallel"))(*args)


def _mm_tn(name, a, b, tm, tn, tk):
    T, M = a.shape
    N = b.shape[1]
    tm, tn, tk = min(tm, M), min(tn, N), min(tk, T)

    def body(a_ref, b_ref, o_ref):
        @pl.when(pl.program_id(2) == 0)
        def _():
            o_ref[...] = jnp.zeros_like(o_ref)
        o_ref[...] += _tn(a_ref[...], b_ref[...])

    return pl.pallas_call(
        body, name=name, grid=(M // tm, N // tn, T // tk),
        in_specs=[pl.BlockSpec((tk, tm), lambda i, j, k: (k, i)), pl.BlockSpec((tk, tn), lambda i, j, k: (k, j))],
        out_specs=pl.BlockSpec((tm, tn), lambda i, j, k: (i, j)),
        out_shape=jax.ShapeDtypeStruct((M, N), F32),
        compiler_params=_cp("parallel", "parallel", "arbitrary"))(a, b)


def _rms_fwd(name, x, g, rb=512, dep=None):
    T, Dm = x.shape
    rb = min(rb, T)

    def body(x_ref, g_ref, *rest):
        h_ref = rest[-1]
        xv = x_ref[...]
        r = lax.rsqrt(jnp.mean(xv * xv, axis=-1, keepdims=True) + EPS)
        h_ref[...] = (xv * r * g_ref[...]).astype(BF16)

    in_specs, args = _with_dep(
        [pl.BlockSpec((rb, Dm), lambda i: (i, 0)), pl.BlockSpec((1, Dm), lambda i: (0, 0))], [x, g], dep)
    return pl.pallas_call(
        body, name=name, grid=(T // rb,), in_specs=in_specs,
        out_specs=pl.BlockSpec((rb, Dm), lambda i: (i, 0)),
        out_shape=jax.ShapeDtypeStruct((T, Dm), BF16), compiler_params=_cp("parallel"))(*args)


def _rms_bwd(name, dh, x, g, dres, rb=512):
    T, Dm = x.shape
    rb = min(rb, T)

    def body(dh_ref, x_ref, g_ref, dres_ref, dx_ref, gg_ref):
        @pl.when(pl.program_id(0) == 0)
        def _():
            gg_ref[...] = jnp.zeros_like(gg_ref)
        xv = x_ref[...]
        r = lax.rsqrt(jnp.mean(xv * xv, axis=-1, keepdims=True) + EPS)
        xh = xv * r
        dhv = dh_ref[...]
        gg_ref[...] += jnp.sum(dhv * xh, axis=0, keepdims=True)
        dxh = dhv * g_ref[...]
        dx_ref[...] = dres_ref[...] + r * (dxh - xh * jnp.mean(dxh * xh, axis=-1, keepdims=True))

    row = pl.BlockSpec((rb, Dm), lambda i: (i, 0))
    vec = pl.BlockSpec((1, Dm), lambda i: (0, 0))
    return pl.pallas_call(
        body, name=name, grid=(T // rb,), in_specs=[row, row, vec, row], out_specs=[row, vec],
        out_shape=[jax.ShapeDtypeStruct((T, Dm), F32), jax.ShapeDtypeStruct((1, Dm), F32)],
        compiler_params=_cp("arbitrary"))(dh, x, g, dres)


def _loss_head(x3, tgt, g, rb=512):
    T, Dm = x3.shape
    rb = min(rb, T)

    def body(x_ref, t_ref, g_ref, loss_ref, dx_ref, gg_ref):
        @pl.when(pl.program_id(0) == 0)
        def _():
            gg_ref[...] = jnp.zeros_like(gg_ref)
            loss_ref[...] = jnp.zeros_like(loss_ref)
        xv = x_ref[...]
        r = lax.rsqrt(jnp.mean(xv * xv, axis=-1, keepdims=True) + EPS)
        xh = xv * r
        e = xh * g_ref[...] - t_ref[...]
        loss_ref[...] += jnp.zeros_like(loss_ref) + (0.5 / Dm) * jnp.sum(e * e)
        dy = e * (1.0 / Dm)
        gg_ref[...] += jnp.sum(dy * xh, axis=0, keepdims=True)
        dxh = dy * g_ref[...]
        dx_ref[...] = r * (dxh - xh * jnp.mean(dxh * xh, axis=-1, keepdims=True))

    row = pl.BlockSpec((rb, Dm), lambda i: (i, 0))
    vec = pl.BlockSpec((1, Dm), lambda i: (0, 0))
    return pl.pallas_call(
        body, name="loss_head", grid=(T // rb,), in_specs=[row, row, vec],
        out_specs=[pl.BlockSpec((1, 128), lambda i: (0, 0)), row, vec],
        out_shape=[jax.ShapeDtypeStruct((1, 128), F32), jax.ShapeDtypeStruct((T, Dm), F32),
                   jax.ShapeDtypeStruct((1, Dm), F32)],
        compiler_params=_cp("arbitrary"))(x3, tgt, g)


def _prev_rows(cur, halo, i):
    return jnp.concatenate([jnp.where(i > 0, halo, 0.0), cur], axis=0)


def _shift_down(ext, k, rb):
    if k == 0:
        return ext[8:8 + rb]
    return pltpu.roll(ext, k, 0)[8:8 + rb]


def _conv_fwd_vals(ext, w, rb):
    K = w.shape[0]
    out = _shift_down(ext, K - 1, rb) * w[0:1]
    for k in range(1, K):
        out = out + _shift_down(ext, K - 1 - k, rb) * w[k:k + 1]
    return out


def _halo_prev_spec(rb, width, col=0):
    return pl.BlockSpec((8, width), lambda i: (jnp.maximum(i * (rb // 8) - 1, 0), col))


def _conv_bwd_in(name, dc, w, rb):
    T, W = dc.shape
    K = w.shape[0]
    rb = min(rb, T)
    nb = T // rb

    def body(dc_ref, nx_ref, w_ref, o_ref):
        i = pl.program_id(0)
        ext = jnp.concatenate([dc_ref[...], jnp.where(i < nb - 1, nx_ref[...], 0.0)], axis=0)
        wv = w_ref[...]
        out = ext[0:rb] * wv[K - 1:K]
        for k in range(K - 1):
            s = K - 1 - k
            out = out + pltpu.roll(ext, rb + 8 - s, 0)[0:rb] * wv[k:k + 1]
        o_ref[...] = out.astype(BF16)

    return pl.pallas_call(
        body, name=name, grid=(nb,),
        in_specs=[pl.BlockSpec((rb, W), lambda i: (i, 0)),
                  pl.BlockSpec((8, W), lambda i: (jnp.minimum((i + 1) * (rb // 8), T // 8 - 1), 0)),
                  pl.BlockSpec((K, W), lambda i: (0, 0))],
        out_specs=pl.BlockSpec((rb, W), lambda i: (i, 0)),
        out_shape=jax.ShapeDtypeStruct((T, W), BF16), compiler_params=_cp("parallel"))(dc, dc, w)


def _ffn_act(up, w, b, rb=256):
    T, W = up.shape
    rb = min(rb, T)

    def body(up_ref, halo_ref, w_ref, b_ref, act_ref):
        ext = _prev_rows(up_ref[...], halo_ref[...], pl.program_id(0))
        c = _conv_fwd_vals(ext, w_ref[...], rb) + b_ref[...]
        act_ref[...] = (_silu(c[:, :D_FF]) * c[:, D_FF:]).astype(BF16)

    return pl.pallas_call(
        body, name="ffn_act", grid=(T // rb,),
        in_specs=[pl.BlockSpec((rb, W), lambda i: (i, 0)), _halo_prev_spec(rb, W),
                  pl.BlockSpec((3, W), lambda i: (0, 0)), pl.BlockSpec((1, W), lambda i: (0, 0))],
        out_specs=pl.BlockSpec((rb, D_FF), lambda i: (i, 0)),
        out_shape=jax.ShapeDtypeStruct((T, D_FF), BF16), compiler_params=_cp("parallel"))(up, up, w, b)


def _ffn_act_bwd(up, dact, w, b, rb=128, dep=None):
    T, W = up.shape
    rb = min(rb, T)
    nb = T // rb
    re = rb + 8

    def body(up_ref, prev_ref, next_ref, da_ref, danext_ref, w_ref, b_ref, *rest):
        dup_ref, gw_ref, gb_ref = rest[-3:]
        i = pl.program_id(0)

        @pl.when(i == 0)
        def _():
            gw_ref[...] = jnp.zeros_like(gw_ref)
            gb_ref[...] = jnp.zeros_like(gb_ref)
        last = i == nb - 1
        wv = w_ref[...]
        ext = jnp.concatenate([jnp.where(i > 0, prev_ref[...], 0.0), up_ref[...], next_ref[...]], axis=0)
        taps = [ext[8:8 + re] if k == 2 else pltpu.roll(ext, 2 - k, 0)[8:8 + re] for k in range(3)]
        c = taps[0] * wv[0:1] + taps[1] * wv[1:2] + taps[2] * wv[2:3] + b_ref[...]
        gpre, vv = c[:, :D_FF], c[:, D_FF:]
        row = lax.broadcasted_iota(jnp.int32, (re, 1), 0)
        da = jnp.concatenate([da_ref[...], danext_ref[...]], axis=0)
        da = jnp.where((row < rb) | jnp.logical_not(last), da, 0.0)
        s = _sigmoid(gpre)
        gs = gpre * s
        dc = jnp.concatenate([da * vv * (s + gs * (1.0 - s)), da * gs], axis=1)
        dup = dc[0:rb] * wv[2:3]
        for k in range(2):
            dup = dup + pltpu.roll(dc, re - (2 - k), 0)[0:rb] * wv[k:k + 1]
        dup_ref[...] = dup.astype(BF16)
        dcc = dc[0:rb]
        gb_ref[...] += jnp.sum(dcc, axis=0, keepdims=True)
        for k in range(3):
            gw_ref[k:k + 1, :] += jnp.sum(taps[k][0:rb] * dcc, axis=0, keepdims=True)

    nxt = lambda i: jnp.minimum((i + 1) * (rb // 8), T // 8 - 1)
    in_specs, args = _with_dep(
        [pl.BlockSpec((rb, W), lambda i: (i, 0)), _halo_prev_spec(rb, W), pl.BlockSpec((8, W), lambda i: (nxt(i), 0)),
         pl.BlockSpec((rb, D_FF), lambda i: (i, 0)), pl.BlockSpec((8, D_FF), lambda i: (nxt(i), 0)),
         pl.BlockSpec((3, W), lambda i: (0, 0)), pl.BlockSpec((1, W), lambda i: (0, 0))],
        [up, up, up, dact, dact, w, b], dep)
    return pl.pallas_call(
        body, name="ffn_act_bwd", grid=(nb,), in_specs=in_specs,
        out_specs=[pl.BlockSpec((rb, W), lambda i: (i, 0)), pl.BlockSpec((3, W), lambda i: (0, 0)),
                   pl.BlockSpec((1, W), lambda i: (0, 0))],
        out_shape=[jax.ShapeDtypeStruct((T, W), BF16), jax.ShapeDtypeStruct((3, W), F32),
                   jax.ShapeDtypeStruct((1, W), F32)],
        compiler_params=_cp("arbitrary"))(*args)


def _lane_iota(shape):
    return lax.broadcasted_iota(jnp.int32, shape, len(shape) - 1)


def _dn_act(p, conv_w, alog_row, dtb_row, rb=256):
    T = p.shape[0]
    rb = min(rb, T)
    W3 = 3 * DN_WIDTH

    def body(p_ref, halo_ref, ba_ref, w_ref, al_ref, dt_ref, q_ref, k_ref, v_ref, bg_ref):
        ext = _prev_rows(p_ref[...], halo_ref[...], pl.program_id(0))
        s = _silu(_conv_fwd_vals(ext, w_ref[...], rb))
        for h in range(N_HEADS):
            lo = h * HEAD_DIM
            sq = s[:, lo:lo + HEAD_DIM]
            n = lax.rsqrt(jnp.sum(sq * sq, axis=-1, keepdims=True) + EPS)
            q_ref[:, lo:lo + HEAD_DIM] = sq * n * (HEAD_DIM ** -0.5)
            sk = s[:, DN_WIDTH + lo:DN_WIDTH + lo + HEAD_DIM]
            n = lax.rsqrt(jnp.sum(sk * sk, axis=-1, keepdims=True) + EPS)
            k_ref[:, lo:lo + HEAD_DIM] = sk * n
        v_ref[...] = s[:, 2 * DN_WIDTH:]
        ba = ba_ref[...]
        lane = _lane_iota(ba.shape)
        beta = _sigmoid(ba)
        g = -jnp.exp(al_ref[...]) * _softplus(ba + dt_ref[...])
        bg_ref[...] = jnp.where(lane < N_HEADS, beta, jnp.where(lane < 2 * N_HEADS, g, 0.0))

    row512 = pl.BlockSpec((rb, DN_WIDTH), lambda i: (i, 0))
    row128 = pl.BlockSpec((rb, 128), lambda i: (i, 0))
    vec128 = pl.BlockSpec((1, 128), lambda i: (0, 0))
    return pl.pallas_call(
        body, name="dn_act", grid=(T // rb,),
        in_specs=[pl.BlockSpec((rb, W3), lambda i: (i, 0)), _halo_prev_spec(rb, W3),
                  pl.BlockSpec((rb, 128), lambda i: (i, BA_COL // 128)),
                  pl.BlockSpec((4, W3), lambda i: (0, 0)), vec128, vec128],
        out_specs=[row512, row512, row512, row128],
        out_shape=[jax.ShapeDtypeStruct((T, DN_WIDTH), F32)] * 3 + [jax.ShapeDtypeStruct((T, 128), F32)],
        compiler_params=_cp("parallel"))(p, p, p, conv_w, alog_row, dtb_row)


def _dn_act_bwd(p, conv_w, alog_row, dtb_row, dq, dk, dv, dbg, rb=256):
    T = p.shape[0]
    rb = min(rb, T)
    W3 = 3 * DN_WIDTH

    def body(p_ref, halo_ref, ba_ref, w_ref, al_ref, dt_ref, dq_ref, dk_ref, dv_ref, dbg_ref,
             dc_ref, dba_ref, gw_ref, gad_ref):
        @pl.when(pl.program_id(0) == 0)
        def _():
            gw_ref[...] = jnp.zeros_like(gw_ref)
            gad_ref[...] = jnp.zeros_like(gad_ref)
        ext = _prev_rows(p_ref[...], halo_ref[...], pl.program_id(0))
        c = _conv_fwd_vals(ext, w_ref[...], rb)
        s = _silu(c)
        ds = _dsilu(c)
        for h in range(N_HEADS):
            lo = h * HEAD_DIM
            for (off, d_ref, scale) in ((0, dq_ref, HEAD_DIM ** -0.5), (DN_WIDTH, dk_ref, 1.0)):
                sv = s[:, off + lo:off + lo + HEAD_DIM]
                n = lax.rsqrt(jnp.sum(sv * sv, axis=-1, keepdims=True) + EPS)
                hat = sv * n
                dvv = d_ref[:, lo:lo + HEAD_DIM]
                dsv = (n * scale) * (dvv - hat * jnp.sum(hat * dvv, axis=-1, keepdims=True))
                dc_ref[:, off + lo:off + lo + HEAD_DIM] = dsv * ds[:, off + lo:off + lo + HEAD_DIM]
        dc_ref[:, 2 * DN_WIDTH:] = dv_ref[...] * ds[:, 2 * DN_WIDTH:]
        dc = dc_ref[...]
        for k in range(4):
            gw_ref[k:k + 1, :] += jnp.sum(_shift_down(ext, 3 - k, rb) * dc, axis=0, keepdims=True)
        ba = ba_ref[...]
        dbg = dbg_ref[...]
        lane = _lane_iota(ba.shape)
        beta = _sigmoid(ba)
        ea = jnp.exp(al_ref[...])
        z = ba + dt_ref[...]
        d_a = dbg * (-ea) * _sigmoid(z)
        dba = jnp.where(lane < N_HEADS, dbg * beta * (1.0 - beta), jnp.where(lane < 2 * N_HEADS, d_a, 0.0))
        dba_ref[...] = dba.astype(BF16)
        isg = (lane >= N_HEADS) & (lane < 2 * N_HEADS)
        g = -ea * _softplus(z)
        gad_ref[0:1, :] += jnp.sum(jnp.where(isg, dbg * g, 0.0), axis=0, keepdims=True)
        gad_ref[1:2, :] += jnp.sum(jnp.where(isg, d_a, 0.0), axis=0, keepdims=True)

    row512 = pl.BlockSpec((rb, DN_WIDTH), lambda i: (i, 0))
    row128 = pl.BlockSpec((rb, 128), lambda i: (i, 0))
    vec128 = pl.BlockSpec((1, 128), lambda i: (0, 0))
    return pl.pallas_call(
        body, name="dn_act_bwd", grid=(T // rb,),
        in_specs=[pl.BlockSpec((rb, W3), lambda i: (i, 0)), _halo_prev_spec(rb, W3),
                  pl.BlockSpec((rb, 128), lambda i: (i, BA_COL // 128)),
                  pl.BlockSpec((4, W3), lambda i: (0, 0)), vec128, vec128,
                  row512, row512, row512, row128],
        out_specs=[pl.BlockSpec((rb, W3), lambda i: (i, 0)), row128,
                   pl.BlockSpec((4, W3), lambda i: (0, 0)), pl.BlockSpec((2, 128), lambda i: (0, 0))],
        out_shape=[jax.ShapeDtypeStruct((T, W3), F32), jax.ShapeDtypeStruct((T, 128), BF16),
                   jax.ShapeDtypeStruct((4, W3), F32), jax.ShapeDtypeStruct((2, 128), F32)],
        compiler_params=_cp("arbitrary"))(p, p, p, conv_w, alog_row, dtb_row, dq, dk, dv, dbg)


def _tri(incl):
    ii = lax.broadcasted_iota(jnp.int32, (CHUNK, CHUNK), 0)
    jj = lax.broadcasted_iota(jnp.int32, (CHUNK, CHUNK), 1)
    return ii, jj, ((ii >= jj) if incl else (ii > jj))


def _dn_chunk(k, bg):
    T = k.shape[0]
    N = T // CHUNK

    def body(k_ref, bg_ref, gc_ref, gct_ref, l_ref):
        ii, jj, incl = _tri(True)
        bgv = bg_ref[...]
        gc = jnp.dot(incl.astype(F32), bgv, precision=lax.Precision.HIGHEST, preferred_element_type=F32)
        gc_ref[...] = gc
        gct = gc.T
        gct_ref[0] = gct[0:8]
        for h in range(N_HEADS):
            kh = k_ref[:, h * HEAD_DIM:(h + 1) * HEAD_DIM]
            beta = bgv[:, h:h + 1]
            gcol = gc[:, N_HEADS + h:N_HEADS + h + 1]
            grow = gct[N_HEADS + h:N_HEADS + h + 1, :]
            gam = jnp.exp(jnp.where(ii > jj, gcol - grow, NEG))
            l_ref[0, h] = _nt(kh * beta, kh) * gam

    return pl.pallas_call(
        body, name="dn_chunk", grid=(N,),
        in_specs=[pl.BlockSpec((CHUNK, DN_WIDTH), lambda n: (n, 0)), pl.BlockSpec((CHUNK, 128), lambda n: (n, 0))],
        out_specs=[pl.BlockSpec((CHUNK, 128), lambda n: (n, 0)), pl.BlockSpec((1, 8, CHUNK), lambda n: (n, 0, 0)),
                   pl.BlockSpec((1, N_HEADS, CHUNK, CHUNK), lambda n: (n, 0, 0, 0))],
        out_shape=[jax.ShapeDtypeStruct((T, 128), F32), jax.ShapeDtypeStruct((N, 8, CHUNK), F32),
                   jax.ShapeDtypeStruct((N, N_HEADS, CHUNK, CHUNK), F32)],
        compiler_params=_cp("parallel"))(k, bg)


def _tri_inv(lt):
    S = lt.shape[1]

    def body(l_ref, a_ref):
        col = lax.broadcasted_iota(jnp.int32, (CHUNK, S), 0)
        for i in range(CHUNK):
            def step(j, acc):
                return acc - l_ref[pl.ds(i * CHUNK + j, 1), :] * a_ref[j]
            a_ref[i] = lax.fori_loop(0, i, step, (col == i).astype(F32))

    return pl.pallas_call(
        body, name="tri_inv", out_shape=jax.ShapeDtypeStruct((CHUNK, CHUNK, S), F32),
        compiler_params=pltpu.CompilerParams(vmem_limit_bytes=VMEM_LIMIT))(lt)


def _dn_head_terms(qh, kh, vh, beta, gcol, grow):
    ii, jj, incl = _tri(True)
    gam = jnp.exp(jnp.where(incl, gcol - grow, NEG))
    glast = grow[:, CHUNK - 1:CHUNK]
    E = jnp.exp(gcol)
    Fd = jnp.exp(glast - gcol)
    cd = jnp.exp(glast)
    kb = kh * beta
    return dict(ii=ii, jj=jj, gam=gam, E=E, F=Fd, cd=cd, kb=kb, vb=vh * beta, W=kb * E, qE=qh * E, kt=kh * Fd)


def _apply_a(a, u):
    hi, lo = _split(a)
    ub = _bf(u)
    return jnp.dot(hi, ub, preferred_element_type=F32) + jnp.dot(lo, ub, preferred_element_type=F32)


def _apply_at(a, u):
    hi, lo = _split(a)
    ub = _bf(u)
    dn = (((0,), (0,)), ((), ()))
    return (lax.dot_general(hi, ub, dn, preferred_element_type=F32)
            + lax.dot_general(lo, ub, dn, preferred_element_type=F32))


def _dn_scan(q, k, v, bg, gc, gct, a):
    T = q.shape[0]
    N = T // CHUNK

    def body(q_ref, k_ref, v_ref, bg_ref, gc_ref, gct_ref, a_ref, o_ref, sall_ref, s_ref):
        @pl.when(pl.program_id(0) == 0)
        def _():
            s_ref[...] = jnp.zeros_like(s_ref)
        bgv, gcv, gctv = bg_ref[...], gc_ref[...], gct_ref[0]
        H = range(N_HEADS)
        sl = [slice(h * HEAD_DIM, (h + 1) * HEAD_DIM) for h in H]
        q_, k_ = [q_ref[:, s] for s in sl], [k_ref[:, s] for s in sl]
        t = [_dn_head_terms(q_[h], k_[h], v_ref[:, sl[h]], bgv[:, h:h + 1],
                            gcv[:, N_HEADS + h:N_HEADS + h + 1], gctv[N_HEADS + h:N_HEADS + h + 1, :]) for h in H]
        S = [s_ref[h] for h in H]
        for h in H:
            sall_ref[0, h] = S[h]
        WS = [_nn(t[h]["W"], S[h]) for h in H]
        QK = [_nt(q_[h], k_[h]) for h in H]
        qS = [_nn(t[h]["qE"], S[h]) for h in H]
        vn = [_apply_a(a_ref[0, h], t[h]["vb"] - WS[h]) for h in H]
        Pv = [_nn(QK[h] * t[h]["gam"], vn[h]) for h in H]
        kv = [_tn(t[h]["kt"], vn[h]) for h in H]
        for h in H:
            o_ref[:, sl[h]] = qS[h] + Pv[h]
            s_ref[h] = t[h]["cd"] * S[h] + kv[h]

    row512 = pl.BlockSpec((CHUNK, DN_WIDTH), lambda n: (n, 0))
    row128 = pl.BlockSpec((CHUNK, 128), lambda n: (n, 0))
    return pl.pallas_call(
        body, name="dn_scan", grid=(N,),
        in_specs=[row512, row512, row512, row128, row128, pl.BlockSpec((1, 8, CHUNK), lambda n: (n, 0, 0)),
                  pl.BlockSpec((1, N_HEADS, CHUNK, CHUNK), lambda n: (n, 0, 0, 0))],
        out_specs=[row512, pl.BlockSpec((1, N_HEADS, HEAD_DIM, HEAD_DIM), lambda n: (n, 0, 0, 0))],
        out_shape=[jax.ShapeDtypeStruct((T, DN_WIDTH), F32),
                   jax.ShapeDtypeStruct((N, N_HEADS, HEAD_DIM, HEAD_DIM), F32)],
        scratch_shapes=[pltpu.VMEM((N_HEADS, HEAD_DIM, HEAD_DIM), F32)],
        compiler_params=_cp("arbitrary"))(q, k, v, bg, gc, gct, a)


def _dn_scan_bwd(q, k, v, bg, gc, gct, a, sall, do, dep=None):
    T = q.shape[0]
    N = T // CHUNK

    def body(q_ref, k_ref, v_ref, bg_ref, gc_ref, gct_ref, a_ref, sall_ref, do_ref, *rest):
        dq_ref, dk_ref, dv_ref, dbg_ref, ds_ref = rest[-5:]
        @pl.when(pl.program_id(0) == 0)
        def _():
            ds_ref[...] = jnp.zeros_like(ds_ref)
        bgv, gcv, gctv = bg_ref[...], gc_ref[...], gct_ref[0]
        lane = _lane_iota((CHUNK, 128))
        rowi = lax.broadcasted_iota(jnp.int32, (CHUNK, 1), 0)
        H = range(N_HEADS)
        sl = [slice(h * HEAD_DIM, (h + 1) * HEAD_DIM) for h in H]
        q_, k_, v_ = [q_ref[:, s] for s in sl], [k_ref[:, s] for s in sl], [v_ref[:, s] for s in sl]
        dO = [do_ref[:, s] for s in sl]
        beta = [bgv[:, h:h + 1] for h in H]
        t = [_dn_head_terms(q_[h], k_[h], v_[h], beta[h], gcv[:, N_HEADS + h:N_HEADS + h + 1],
                            gctv[N_HEADS + h:N_HEADS + h + 1, :]) for h in H]
        ii, jj = t[0]["ii"], t[0]["jj"]
        gam, E, Fd, cd, kb = ([t[h][n] for h in H] for n in ("gam", "E", "F", "cd", "kb"))
        S = [sall_ref[0, h] for h in H]
        dSn = [ds_ref[h] for h in H]
        A = [a_ref[0, h] for h in H]
        WS = [_nn(t[h]["W"], S[h]) for h in H]
        KK = [_nt(kb[h], k_[h]) for h in H]
        QK = [_nt(q_[h], k_[h]) for h in H]
        ktdS = [_nn(t[h]["kt"], dSn[h]) for h in H]
        d_qE = [_nt(dO[h], S[h]) for h in H]
        vn = [_apply_a(A[h], t[h]["vb"] - WS[h]) for h in H]
        PtdO = [_tn(QK[h] * gam[h], dO[h]) for h in H]
        qEdO = [_tn(t[h]["qE"], dO[h]) for h in H]
        dU = [_apply_at(A[h], PtdO[h] + ktdS[h]) for h in H]
        d_kt = [_nt(vn[h], dSn[h]) for h in H]
        dOvn = [_nt(dO[h], vn[h]) for h in H]
        dUvn = [_nt(dU[h], vn[h]) for h in H]
        dUS = [_nt(dU[h], S[h]) for h in H]
        WdU = [_tn(t[h]["W"], dU[h]) for h in H]
        for h in H:
            ds_ref[h] = cd[h] * dSn[h] + qEdO[h] - WdU[h]
        dQK = [jnp.where(ii >= jj, dOvn[h], 0.0) * gam[h] for h in H]
        dKK = [jnp.where(ii > jj, -dUvn[h], 0.0) * gam[h] for h in H]
        dQKk = [_nn(dQK[h], k_[h]) for h in H]
        dKKk = [_nn(dKK[h], k_[h]) for h in H]
        dQKq = [_tn(dQK[h], q_[h]) for h in H]
        dKKkb = [_tn(dKK[h], kb[h]) for h in H]
        dbeta_arr = jnp.zeros((CHUNK, 128), F32)
        dgc_arr = jnp.zeros((CHUNK, 128), F32)
        for h in H:
            dW = -dUS[h]
            dq_ref[:, sl[h]] = dQKk[h] + d_qE[h] * E[h]
            d_kb = dKKk[h] + dW * E[h]
            dk_ref[:, sl[h]] = dQKq[h] + dKKkb[h] + d_kb * beta[h] + d_kt[h] * Fd[h]
            dv_ref[:, sl[h]] = dU[h] * beta[h]
            Z = dQK[h] * QK[h] + dKK[h] * KK[h]
            d_cd = jnp.sum(S[h] * dSn[h])
            dbeta = jnp.sum(dU[h] * v_[h] + d_kb * k_[h], axis=-1, keepdims=True)
            dE = jnp.sum(dW * kb[h] + d_qE[h] * q_[h], axis=-1, keepdims=True)
            dFF = jnp.sum(d_kt[h] * k_[h], axis=-1, keepdims=True) * Fd[h]
            dgc = (dE * E[h] - dFF + jnp.sum(Z, axis=-1, keepdims=True) - jnp.sum(Z.T, axis=-1, keepdims=True)
                   + jnp.where(rowi == CHUNK - 1, jnp.sum(dFF) + d_cd * cd[h], 0.0))
            dbeta_arr = dbeta_arr + jnp.where(lane == h, dbeta, 0.0)
            dgc_arr = dgc_arr + jnp.where(lane == N_HEADS + h, dgc, 0.0)
        ii, jj, _ = _tri(True)
        rev = (jj >= ii).astype(F32)
        dbg_ref[...] = dbeta_arr + jnp.dot(rev, dgc_arr, precision=lax.Precision.HIGHEST,
                                           preferred_element_type=F32)

    row512 = pl.BlockSpec((CHUNK, DN_WIDTH), lambda n: (N - 1 - n, 0))
    row128 = pl.BlockSpec((CHUNK, 128), lambda n: (N - 1 - n, 0))
    in_specs, args = _with_dep(
        [row512, row512, row512, row128, row128,
         pl.BlockSpec((1, 8, CHUNK), lambda n: (N - 1 - n, 0, 0)),
         pl.BlockSpec((1, N_HEADS, CHUNK, CHUNK), lambda n: (N - 1 - n, 0, 0, 0)),
         pl.BlockSpec((1, N_HEADS, HEAD_DIM, HEAD_DIM), lambda n: (N - 1 - n, 0, 0, 0)), row512],
        [q, k, v, bg, gc, gct, a, sall, do], dep)
    return pl.pallas_call(
        body, name="dn_scan_bwd", grid=(N,), in_specs=in_specs,
        out_specs=[row512, row512, row512, row128],
        out_shape=[jax.ShapeDtypeStruct((T, DN_WIDTH), F32)] * 3 + [jax.ShapeDtypeStruct((T, 128), F32)],
        scratch_shapes=[pltpu.VMEM((N_HEADS, HEAD_DIM, HEAD_DIM), F32)],
        compiler_params=_cp("arbitrary"))(*args)


def _sg_mask():
    ii = lax.broadcasted_iota(jnp.int32, (SG_BLOCK, SG_BLOCK), 0) // CHUNK
    jj = lax.broadcasted_iota(jnp.int32, (SG_BLOCK, SG_BLOCK), 1) // CHUNK
    return jj <= ii


def _mix_fwd(o, p, ong, sgn, sgw, sgbt):
    T = o.shape[0]
    rb = SG_BLOCK

    def body(o_ref, gate_ref, u_ref, vg_ref, ong_ref, sgn_ref, sgw_ref, sgbt_ref, mix_ref):
        mask = _sg_mask()
        gate = gate_ref[...]
        for h in range(N_HEADS):
            sl = slice(h * HEAD_DIM, (h + 1) * HEAD_DIM)
            oh = o_ref[:, sl]
            r = lax.rsqrt(jnp.mean(oh * oh, axis=-1, keepdims=True) + EPS)
            mix_ref[:, sl] = (oh * r * ong_ref[...] * _silu(gate[:, sl])).astype(BF16)
        for gi in range(SG_GROUPS):
            sl = slice(gi * SG_BLOCK, (gi + 1) * SG_BLOCK)
            gv = _gelu(vg_ref[:, sl])
            r = lax.rsqrt(jnp.mean(gv * gv, axis=-1, keepdims=True) + EPS)
            vh = gv * r * sgn_ref[:, sl]
            s = _nn(jnp.where(mask, sgw_ref[gi], 0.0), vh) + sgbt_ref[:, gi:gi + 1]
            mix_ref[:, DN_WIDTH + gi * SG_BLOCK:DN_WIDTH + (gi + 1) * SG_BLOCK] = (_gelu(u_ref[:, sl]) * s).astype(BF16)

    def col(c):
        return pl.BlockSpec((rb, 512), lambda i: (i, c))
    return pl.pallas_call(
        body, name="mix_fwd", grid=(T // rb,),
        in_specs=[pl.BlockSpec((rb, DN_WIDTH), lambda i: (i, 0)), col(3), col(4), col(5),
                  pl.BlockSpec((1, 128), lambda i: (0, 0)), pl.BlockSpec((1, SG_WIDTH), lambda i: (0, 0)),
                  pl.BlockSpec((SG_GROUPS, SG_BLOCK, SG_BLOCK), lambda i: (0, 0, 0)),
                  pl.BlockSpec((SG_BLOCK, 128), lambda i: (0, 0))],
        out_specs=pl.BlockSpec((rb, D_MODEL), lambda i: (i, 0)),
        out_shape=jax.ShapeDtypeStruct((T, D_MODEL), BF16),
        compiler_params=_cp("parallel"))(o, p, p, p, ong, sgn, sgw, sgbt)


def _mix_bwd(o, p, ong, sgn, sgw, sgbt, dmix, dep=None):
    T = o.shape[0]
    rb = SG_BLOCK

    def body(o_ref, gate_ref, u_ref, vg_ref, ong_ref, sgn_ref, sgw_ref, sgbt_ref, dmix_ref, *rest):
        do_ref, dp_ref, gong_ref, gsgn_ref, gsgw_ref, gsgbt_ref = rest[-6:]
        @pl.when(pl.program_id(0) == 0)
        def _():
            gong_ref[...] = jnp.zeros_like(gong_ref)
            gsgn_ref[...] = jnp.zeros_like(gsgn_ref)
            gsgw_ref[...] = jnp.zeros_like(gsgw_ref)
            gsgbt_ref[...] = jnp.zeros_like(gsgbt_ref)
        mask = _sg_mask()
        gate = gate_ref[...]
        lane = _lane_iota((SG_BLOCK, 128))
        for h in range(N_HEADS):
            sl = slice(h * HEAD_DIM, (h + 1) * HEAD_DIM)
            oh = o_ref[:, sl]
            dm = dmix_ref[:, sl]
            r = lax.rsqrt(jnp.mean(oh * oh, axis=-1, keepdims=True) + EPS)
            oh_hat = oh * r
            gt = gate[:, sl]
            sg = _silu(gt)
            dp_ref[:, sl] = (dm * oh_hat * ong_ref[...] * _dsilu(gt)).astype(BF16)
            dn_ = dm * sg
            gong_ref[...] += jnp.sum(dn_ * oh_hat, axis=0, keepdims=True)
            dhat = dn_ * ong_ref[...]
            do_ref[:, sl] = r * (dhat - oh_hat * jnp.mean(dhat * oh_hat, axis=-1, keepdims=True))
        for gi in range(SG_GROUPS):
            sl = slice(gi * SG_BLOCK, (gi + 1) * SG_BLOCK)
            vraw = vg_ref[:, sl]
            gv = _gelu(vraw)
            r = lax.rsqrt(jnp.mean(gv * gv, axis=-1, keepdims=True) + EPS)
            vhat = gv * r
            vn = vhat * sgn_ref[:, sl]
            wm = jnp.where(mask, sgw_ref[gi], 0.0)
            s = _nn(wm, vn) + sgbt_ref[:, gi:gi + 1]
            uraw = u_ref[:, sl]
            dm = dmix_ref[:, DN_WIDTH + gi * SG_BLOCK:DN_WIDTH + (gi + 1) * SG_BLOCK]
            dp_ref[:, DN_WIDTH + gi * SG_BLOCK:DN_WIDTH + (gi + 1) * SG_BLOCK] = (dm * s * _dgelu(uraw)).astype(BF16)
            ds = dm * _gelu(uraw)
            gsgbt_ref[...] += jnp.where(lane == gi, jnp.sum(ds, axis=-1, keepdims=True), 0.0)
            gsgw_ref[gi] += jnp.where(mask, _nt(ds, vn), 0.0)
            dvn = _tn(wm, ds)
            gsgn_ref[:, sl] += jnp.sum(dvn * vhat, axis=0, keepdims=True)
            dhat = dvn * sgn_ref[:, sl]
            dgv = r * (dhat - vhat * jnp.mean(dhat * vhat, axis=-1, keepdims=True))
            dp_ref[:, 2 * DN_WIDTH + gi * SG_BLOCK:2 * DN_WIDTH + (gi + 1) * SG_BLOCK] = (dgv * _dgelu(vraw)).astype(BF16)

    def col(c):
        return pl.BlockSpec((rb, 512), lambda i: (i, c))
    full = lambda *s: pl.BlockSpec(s, lambda i: (0,) * len(s))
    in_specs, args = _with_dep(
        [pl.BlockSpec((rb, DN_WIDTH), lambda i: (i, 0)), col(3), col(4), col(5),
         full(1, 128), full(1, SG_WIDTH), full(SG_GROUPS, SG_BLOCK, SG_BLOCK), full(SG_BLOCK, 128),
         pl.BlockSpec((rb, D_MODEL), lambda i: (i, 0))],
        [o, p, p, p, ong, sgn, sgw, sgbt, dmix], dep)
    return pl.pallas_call(
        body, name="mix_bwd", grid=(T // rb,), in_specs=in_specs,
        out_specs=[pl.BlockSpec((rb, DN_WIDTH), lambda i: (i, 0)), pl.BlockSpec((rb, 3 * 512), lambda i: (i, 0)),
                   full(1, 128), full(1, SG_WIDTH), full(SG_GROUPS, SG_BLOCK, SG_BLOCK), full(SG_BLOCK, 128)],
        out_shape=[jax.ShapeDtypeStruct((T, DN_WIDTH), F32), jax.ShapeDtypeStruct((T, 3 * 512), BF16),
                   jax.ShapeDtypeStruct((1, 128), F32), jax.ShapeDtypeStruct((1, SG_WIDTH), F32),
                   jax.ShapeDtypeStruct((SG_GROUPS, SG_BLOCK, SG_BLOCK), F32),
                   jax.ShapeDtypeStruct((SG_BLOCK, 128), F32)],
        compiler_params=_cp("arbitrary"))(*args)


def _pad_lanes(row, offset=0):
    n = row.shape[1]
    return jnp.pad(row, ((0, 0), (offset, 128 - n - offset)))


def _local_step(x, tgt, w, dep=None, late_weights=None, on_grad=None):
    T = x.shape[0]
    N = T // CHUNK
    on_grad = on_grad or (lambda name, g: None)
    alog_row = _pad_lanes(w["dn_a_log"], N_HEADS)
    dtb_row = _pad_lanes(w["dn_dt_bias"], N_HEADS)
    sgbt = jnp.pad(w["sg_b"].T, ((0, 0), (0, 128 - SG_GROUPS)))

    h1 = _rms_fwd("rms_attn", x, w["attn_norm_g"], dep=dep)
    p = _mm_nn("in_proj", h1, w["w_in"], F32, 512, 640)
    q, k, v, bg = _dn_act(p, w["dn_conv_w"], alog_row, dtb_row)
    gc, gct, lmat = _dn_chunk(k, bg)
    lt = lmat.reshape(N * N_HEADS, CHUNK * CHUNK).T
    at = _tri_inv(lt)
    a = at.reshape(CHUNK * CHUNK, N * N_HEADS).T.reshape(N, N_HEADS, CHUNK, CHUNK)
    o, sall = _dn_scan(q, k, v, bg, gc, gct, a)
    mix = _mix_fwd(o, p, w["dn_out_norm_g"], w["sg_norm_g"], w["sg_w"], sgbt)
    if late_weights is not None:
        w = {**w, **late_weights(mix)}
    x2 = _mm_nn("out_proj", mix, w["w_out"], F32, 512, 1024, res=x)
    h2 = _rms_fwd("rms_ffn", x2, w["ffn_norm_g"])
    up = _mm_nn("up_proj", h2, w["w_up"], F32, 512, 1408)
    act = _ffn_act(up, w["ffn_conv_w"], w["ffn_conv_b"])
    x3 = _mm_nn("down_proj", act, w["w_down"], F32, 512, 1024, res=x2)
    loss, dx3, g_final = _loss_head(x3, tgt, w["final_norm_g"])

    dact = _mm_nt("d_act", dx3, w["w_down"], F32, 512, 1408)
    g_w_down = _mm_tn("g_w_down", act, dx3, 1408, 1024, 1024)
    tok = on_grad("w_down", g_w_down)
    dup, g_ffn_conv_w, g_ffn_conv_b = _ffn_act_bwd(up, dact, w["ffn_conv_w"], w["ffn_conv_b"], dep=tok)
    g_w_up = _mm_tn("g_w_up", h2, dup, 512, 1408, 1024)
    tok = on_grad("w_up", g_w_up)
    dh2 = _mm_nt("d_h2", dup, w["w_up"], F32, 512, 1024, dep=tok)
    dx2, g_ffn_norm = _rms_bwd("rms_ffn_bwd", dh2, x2, w["ffn_norm_g"], dx3)
    dmix = _mm_nt("d_mix", dx2, w["w_out"], F32, 512, 1024)
    g_w_out = _mm_tn("g_w_out", mix, dx2, 1024, 1024, 1024)
    tok = on_grad("w_out", g_w_out)
    do, dp_mid, g_ong, g_sgn, g_sgw, g_sgbt = _mix_bwd(o, p, w["dn_out_norm_g"], w["sg_norm_g"], w["sg_w"], sgbt,
                                                      dmix, dep=tok)
    early = dict(dn_out_norm_g=g_ong, sg_norm_g=g_sgn, sg_w=g_sgw, sg_b=g_sgbt[:, :SG_GROUPS].T,
                 ffn_norm_g=g_ffn_norm, ffn_conv_w=g_ffn_conv_w, ffn_conv_b=g_ffn_conv_b, final_norm_g=g_final)
    tok = on_grad("small_early", early)
    dq, dk, dv, dbg = _dn_scan_bwd(q, k, v, bg, gc, gct, a, sall, do, dep=tok)
    dcq, dba, g_dn_conv_w, g_ad = _dn_act_bwd(p, w["dn_conv_w"], alog_row, dtb_row, dq, dk, dv, dbg)
    dp_qkv = _conv_bwd_in("d_qkv", dcq, w["dn_conv_w"], 256)
    dp = jnp.concatenate([dp_qkv, dp_mid, dba], axis=1)
    g_w_in = _mm_tn("g_w_in", h1, dp, 512, 640, 1024)[:, :PROJ_COLS]
    tok = on_grad("w_in", g_w_in)
    dh1 = _mm_nt("d_h1", dp, w["w_in"], F32, 512, 1024, dep=tok)
    grad_x, g_attn_norm = _rms_bwd("rms_attn_bwd", dh1, x, w["attn_norm_g"], dx2)

    grads = dict(
        attn_norm_g=g_attn_norm, w_in=g_w_in, dn_conv_w=g_dn_conv_w,
        dn_a_log=g_ad[0:1, N_HEADS:2 * N_HEADS], dn_dt_bias=g_ad[1:2, N_HEADS:2 * N_HEADS],
        w_out=g_w_out, w_up=g_w_up, w_down=g_w_down, **early)
    return loss, grad_x, grads


def _me():
    return lax.axis_index("x"), lax.axis_index("y"), lax.axis_index("c")


def _peer(rel):
    x, y, c = _me()
    return {"x": (1 - x, y, c), "y": (x, 1 - y, c), "xy": (1 - x, 1 - y, c), "c": (x, y, 1 - c)}[rel]


def _chip_of(dev):
    return 2 * dev[0] + dev[1]


CHIP_RELS = ("x", "y", "xy")


def _run_copies(copies, sends, recvs):
    for cp in copies:
        cp.start()
    for cp in recvs:
        cp.wait_recv()
    for cp in sends:
        cp.wait_send()


def _gather_shards(shards):
    n = len(shards)

    def body(*refs):
        src, out = refs[:n], refs[n:2 * n]
        send_sems, recv_sems, local_sems = refs[2 * n:]
        me = _chip_of(_me())
        local = [pltpu.make_async_copy(src[i], out[i].at[me], local_sems.at[i]) for i in range(n)]
        for cp in local:
            cp.start()
        sends, recvs = [], []
        for i in range(n):
            for r, rel in enumerate(CHIP_RELS):
                k = 3 * i + r
                peer = _peer(rel)
                sends.append(pltpu.make_async_remote_copy(
                    src_ref=src[i], dst_ref=out[i].at[me], send_sem=send_sems.at[k], recv_sem=recv_sems.at[k],
                    device_id=peer, device_id_type=MESH))
                recvs.append(pltpu.make_async_remote_copy(
                    src_ref=src[i], dst_ref=out[i].at[_chip_of(peer)], send_sem=send_sems.at[k],
                    recv_sem=recv_sems.at[k], device_id=peer, device_id_type=MESH))
        _run_copies(sends, sends, recvs)
        for cp in local:
            cp.wait()

    return pl.pallas_call(
        body, name="gather_weights", in_specs=[ANY] * n, out_specs=[ANY] * n,
        out_shape=[jax.ShapeDtypeStruct((4,) + s.shape, s.dtype) for s in shards],
        scratch_shapes=[pltpu.SemaphoreType.DMA((3 * n,)), pltpu.SemaphoreType.DMA((3 * n,)),
                        pltpu.SemaphoreType.DMA((n,))])(*shards)


OTHERS = tuple((fx, fy, fc) for fx in (0, 1) for fy in (0, 1) for fc in (0, 1) if (fx, fy, fc) != (0, 0, 0))


def _other(flip):
    x, y, c = _me()
    return (x ^ flip[0], y ^ flip[1], c ^ flip[2])


def _linear(dev):
    return 4 * dev[0] + 2 * dev[1] + dev[2]


def _exchange_small(small):
    def body(small_ref, out_ref, send_sems, recv_sems, local_sem):
        my_slot = _linear(_me())
        local = pltpu.make_async_copy(small_ref, out_ref.at[my_slot], local_sem)
        local.start()
        sends, recvs = [], []
        for k, flip in enumerate(OTHERS):
            peer = _other(flip)
            sends.append(pltpu.make_async_remote_copy(
                src_ref=small_ref, dst_ref=out_ref.at[my_slot], send_sem=send_sems.at[k], recv_sem=recv_sems.at[k],
                device_id=peer, device_id_type=MESH))
            recvs.append(pltpu.make_async_remote_copy(
                src_ref=small_ref, dst_ref=out_ref.at[_linear(peer)], send_sem=send_sems.at[k],
                recv_sem=recv_sems.at[k], device_id=peer, device_id_type=MESH))
        _run_copies(sends, sends, recvs)
        local.wait()

    return pl.pallas_call(
        body, name="exchange_small", in_specs=[ANY], out_specs=ANY,
        out_shape=jax.ShapeDtypeStruct((8,) + small.shape, small.dtype),
        scratch_shapes=[pltpu.SemaphoreType.DMA((7,)), pltpu.SemaphoreType.DMA((7,)), pltpu.SemaphoreType.DMA])(small)


def _pair_swap(halves):
    n = len(halves)

    def body(*refs):
        src, out = refs[:n], refs[n:2 * n]
        send_sems, recv_sems = refs[2 * n:]
        sib = _peer("c")
        copies = [pltpu.make_async_remote_copy(
            src_ref=src[i], dst_ref=out[i], send_sem=send_sems.at[i], recv_sem=recv_sems.at[i],
            device_id=sib, device_id_type=MESH) for i in range(n)]
        _run_copies(copies, copies, copies)

    return pl.pallas_call(
        body, name="pair_swap", in_specs=[ANY] * n, out_specs=[ANY] * n,
        out_shape=[jax.ShapeDtypeStruct(h.shape, h.dtype) for h in halves],
        scratch_shapes=[pltpu.SemaphoreType.DMA((n,)), pltpu.SemaphoreType.DMA((n,))])(*halves)


HBM = pl.BlockSpec(memory_space=pltpu.HBM)
SEM = pl.BlockSpec(memory_space=pltpu.SEMAPHORE)
EFFECT = pltpu.SideEffectType.DATAFLOW_SIDE_EFFECTING


def _hbm(a):
    return pltpu.with_memory_space_constraint(a, pltpu.HBM)


def _transfer_start(name, srcs, lands, n_copies, make_copies, after=None):
    n, m = len(srcs), len(lands)

    def body(*refs):
        src, land = refs[:n], refs[n:n + m]
        outs = refs[n + m + (after is not None):]
        send_sems, recv_sems, token = outs[0], outs[1], outs[-1]
        for cp in make_copies(src, land, send_sems, recv_sems):
            cp.start()
        token[...] = jnp.zeros_like(token)

    arrs = list(srcs) + list(lands)
    in_specs, args = _with_dep([HBM] * (n + m), [_hbm(a) for a in arrs], after)
    out = pl.pallas_call(
        body, name=name,
        out_shape=(pltpu.SemaphoreType.DMA((n_copies,)), pltpu.SemaphoreType.DMA((n_copies,)),
                   *[pltpu.HBM(a.shape, a.dtype) for a in arrs], jax.ShapeDtypeStruct((8, 128), F32)),
        in_specs=in_specs,
        out_specs=(SEM, SEM, *[HBM] * (n + m), pl.BlockSpec(memory_space=pltpu.VMEM)),
        input_output_aliases={i: 2 + i for i in range(n + m)},
        compiler_params=pltpu.CompilerParams(has_side_effects=EFFECT))(*args)
    return out[0], out[1], list(out[2:2 + n]), list(out[2 + n:2 + n + m]), out[-1]


def _transfer_wait(name, send_sems, recv_sems, srcs, lands, make_copies, after):
    n, m = len(srcs), len(lands)

    def body(*refs):
        src, land = refs[:n], refs[n:n + m]
        s_sems, r_sems = refs[n + m], refs[n + m + 1]
        for cp in make_copies(src, land, s_sems, r_sems):
            cp.wait_send()
            cp.wait_recv()

    arrs = list(srcs) + list(lands)
    out = pl.pallas_call(
        body, name=name, out_shape=tuple(pltpu.HBM(a.shape, a.dtype) for a in arrs),
        in_specs=[HBM] * (n + m) + [SEM, SEM, ANY], out_specs=tuple([HBM] * (n + m)),
        input_output_aliases={i: i for i in range(n + m)},
        compiler_params=pltpu.CompilerParams(has_side_effects=EFFECT))(*arrs, send_sems, recv_sems, after)
    return list(out[:n]), list(out[n:])


def _gather_copies(src, land, send_sems, recv_sems):
    me = _chip_of(_me())
    copies = []
    for i in range(len(src)):
        for r, rel in enumerate(CHIP_RELS):
            k = 3 * i + r
            copies.append(pltpu.make_async_remote_copy(
                src_ref=src[i], dst_ref=land[i].at[me], send_sem=send_sems.at[k], recv_sem=recv_sems.at[k],
                device_id=_peer(rel), device_id_type=MESH))
    return copies


def _small_copies(src, land, send_sems, recv_sems):
    my_slot = _linear(_me())
    return [pltpu.make_async_remote_copy(
        src_ref=src[0], dst_ref=land[0].at[my_slot], send_sem=send_sems.at[k], recv_sem=recv_sems.at[k],
        device_id=_other(flip), device_id_type=MESH) for k, flip in enumerate(OTHERS)]


def _pieces_copies(src, land, send_sems, recv_sems):
    copies = []
    for k, flip in enumerate(OTHERS):
        peer = _other(flip)
        copies.append(pltpu.make_async_remote_copy(
            src_ref=src[0].at[_linear(peer)], dst_ref=land[0].at[k], send_sem=send_sems.at[k],
            recv_sem=recv_sems.at[k], device_id=peer, device_id_type=MESH))
    return copies


def _row_block(rows, cols, budget=2 * 1024 * 1024):
    rb = max(8, (budget // (4 * cols)) // 8 * 8)
    while rows % rb:
        rb -= 8
    return rb if rb > 0 else rows


def _sum_slots(name, first, rest):
    R, Cc = first.shape
    K = rest.shape[0]
    rb = _row_block(R, Cc)

    def body(f_ref, r_ref, o_ref):
        acc = f_ref[...].astype(F32)
        for j in range(K):
            acc = acc + r_ref[j].astype(F32)
        o_ref[...] = acc

    return pl.pallas_call(
        body, name=name, grid=(R // rb,),
        in_specs=[pl.BlockSpec((rb, Cc), lambda i: (i, 0)), pl.BlockSpec((K, rb, Cc), lambda i: (0, i, 0))],
        out_specs=pl.BlockSpec((rb, Cc), lambda i: (i, 0)),
        out_shape=jax.ShapeDtypeStruct((R, Cc), F32), compiler_params=_cp("parallel"))(first, rest)


def _adamw_math(w, gv, m, v):
    mn = ADAM_B1 * m + (1.0 - ADAM_B1) * gv
    vn = ADAM_B2 * v + (1.0 - ADAM_B2) * (gv * gv)
    m_hat = mn / (1.0 - ADAM_B1 ** ADAM_STEP)
    v_hat = vn / (1.0 - ADAM_B2 ** ADAM_STEP)
    return -ADAM_LR * (m_hat / (jnp.sqrt(v_hat) + ADAM_EPS) + ADAM_WD * w), mn, vn


def _adamw_halves(name, w, mine, theirs, m, v, core):
    R, Cc = w.shape
    r2 = R // 2
    rb = _row_block(r2, Cc, 1024 * 1024)
    nb2 = r2 // rb

    def body(c_ref, w_ref, mine_ref, theirs_ref, m_ref, v_ref, g_ref, d_ref, mo_ref, vo_ref):
        is_mine = (pl.program_id(0) // nb2) == c_ref[0]
        gv = jnp.where(is_mine, mine_ref[...], theirs_ref[...])
        g_ref[...] = gv
        d_ref[...], mo_ref[...], vo_ref[...] = _adamw_math(w_ref[...], gv, m_ref[...], v_ref[...])

    blk = pl.BlockSpec((rb, Cc), lambda i, c: (i, 0))
    half = lambda own: pl.BlockSpec(
        (rb, Cc), lambda i, c: (jnp.clip(i - (c[0] if own else 1 - c[0]) * nb2, 0, nb2 - 1), 0))
    return pl.pallas_call(
        body, name=name,
        grid_spec=pltpu.PrefetchScalarGridSpec(
            num_scalar_prefetch=1, grid=(2 * nb2,), in_specs=[blk, half(True), half(False), blk, blk],
            out_specs=[blk] * 4),
        out_shape=[jax.ShapeDtypeStruct((R, Cc), F32)] * 4, compiler_params=_cp("parallel"))(core, w, mine, theirs, m, v)


def _adamw(name, w, g, m, v):
    R, Cc = w.shape
    rb = _row_block(R, Cc, 1024 * 1024)

    def body(w_ref, g_ref, m_ref, v_ref, d_ref, mo_ref, vo_ref):
        d_ref[...], mo_ref[...], vo_ref[...] = _adamw_math(w_ref[...], g_ref[...], m_ref[...], v_ref[...])

    blk = pl.BlockSpec((rb, Cc), lambda i: (i, 0))
    return pl.pallas_call(
        body, name=name, grid=(R // rb,), in_specs=[blk] * 4, out_specs=[blk] * 3,
        out_shape=[jax.ShapeDtypeStruct((R, Cc), F32)] * 3, compiler_params=_cp("parallel"))(w, g, m, v)


def _pack(arrs):
    rows = []
    for a in arrs:
        flat = a.reshape(-1)
        pad = (-flat.shape[0]) % 128
        rows.append(jnp.pad(flat, (0, pad)).reshape(-1, 128))
    buf = jnp.concatenate(rows, axis=0)
    return jnp.pad(buf, ((0, (-buf.shape[0]) % 8), (0, 0)))


def _unpack(buf, shapes):
    out, r = [], 0
    for s in shapes:
        n = math.prod(s)
        nr = -(-n // 128)
        out.append(buf[r:r + nr].reshape(-1)[:n].reshape(s))
        r += nr
    return out


BIG = ("w_in", "w_out", "w_up", "w_down")
CONV = ("dn_conv_w", "ffn_conv_w")
REPL = ("attn_norm_g", "dn_a_log", "dn_dt_bias", "dn_out_norm_g", "sg_norm_g", "sg_w", "sg_b",
        "ffn_norm_g", "ffn_conv_b", "final_norm_g")
ORDER = ("attn_norm_g", "w_in", "dn_conv_w", "dn_a_log", "dn_dt_bias", "dn_out_norm_g", "sg_norm_g", "sg_w",
         "sg_b", "w_out", "ffn_norm_g", "w_up", "ffn_conv_w", "ffn_conv_b", "w_down", "final_norm_g")


def kernel(x, attn_norm_g, w_in, dn_conv_w, dn_a_log, dn_dt_bias, dn_out_norm_g, sg_norm_g, sg_w, sg_b, w_out, ffn_norm_g, w_up, ffn_conv_w, ffn_conv_b, w_down, final_norm_g, loss_target, m_attn_norm_g, m_w_in, m_dn_conv_w, m_dn_a_log, m_dn_dt_bias, m_dn_out_norm_g, m_sg_norm_g, m_sg_w, m_sg_b, m_w_out, m_ffn_norm_g, m_w_up, m_ffn_conv_w, m_ffn_conv_b, m_w_down, m_final_norm_g, v_attn_norm_g, v_w_in, v_dn_conv_w, v_dn_a_log, v_dn_dt_bias, v_dn_out_norm_g, v_sg_norm_g, v_sg_w, v_sg_b, v_w_out, v_ffn_norm_g, v_w_up, v_ffn_conv_w, v_ffn_conv_b, v_w_down, v_final_norm_g):
    W = dict(attn_norm_g=attn_norm_g, w_in=w_in, dn_conv_w=dn_conv_w, dn_a_log=dn_a_log, dn_dt_bias=dn_dt_bias,
             dn_out_norm_g=dn_out_norm_g, sg_norm_g=sg_norm_g, sg_w=sg_w, sg_b=sg_b, w_out=w_out,
             ffn_norm_g=ffn_norm_g, w_up=w_up, ffn_conv_w=ffn_conv_w, ffn_conv_b=ffn_conv_b, w_down=w_down,
             final_norm_g=final_norm_g)
    Mo = dict(attn_norm_g=m_attn_norm_g, w_in=m_w_in, dn_conv_w=m_dn_conv_w, dn_a_log=m_dn_a_log,
              dn_dt_bias=m_dn_dt_bias, dn_out_norm_g=m_dn_out_norm_g, sg_norm_g=m_sg_norm_g, sg_w=m_sg_w,
              sg_b=m_sg_b, w_out=m_w_out, ffn_norm_g=m_ffn_norm_g, w_up=m_w_up, ffn_conv_w=m_ffn_conv_w,
              ffn_conv_b=m_ffn_conv_b, w_down=m_w_down, final_norm_g=m_final_norm_g)
    Vo = dict(attn_norm_g=v_attn_norm_g, w_in=v_w_in, dn_conv_w=v_dn_conv_w, dn_a_log=v_dn_a_log,
              dn_dt_bias=v_dn_dt_bias, dn_out_norm_g=v_dn_out_norm_g, sg_norm_g=v_sg_norm_g, sg_w=v_sg_w,
              sg_b=v_sg_b, w_out=v_w_out, ffn_norm_g=v_ffn_norm_g, w_up=v_w_up, ffn_conv_w=v_ffn_conv_w,
              ffn_conv_b=v_ffn_conv_b, w_down=v_w_down, final_norm_g=v_final_norm_g)
    xi, yi, ci = lax.axis_index("x"), lax.axis_index("y"), lax.axis_index("c")
    chip = 2 * xi + yi

    me_lin = 4 * xi + 2 * yi + ci

    g_in, g_dnc = _gather_shards([w_in[0].astype(BF16), dn_conv_w[0]])
    late = ("w_out", "w_up", "w_down", "ffn_conv_w")
    late_shards = [W[n][0].astype(BF16) for n in late[:3]] + [ffn_conv_w[0]]
    late_lands = [lax.dynamic_update_index_in_dim(lax.empty((4,) + s.shape, s.dtype), s, chip, 0) for s in late_shards]
    n_late = 3 * len(late_shards)
    ssem, rsem, late_src, late_lands, token = _transfer_start("gather_rest_start", late_shards, late_lands,
                                                              n_late, _gather_copies, after=g_in)

    def late_weights(after):
        _, (g_out, g_up, g_down, g_ffc) = _transfer_wait("gather_rest_wait", ssem, rsem, late_src, late_lands,
                                                         _gather_copies, after)
        return dict(w_out=g_out.reshape(D_MODEL, D_MODEL), w_up=g_up.transpose(1, 0, 2).reshape(D_MODEL, 2 * D_FF),
                    w_down=g_down.reshape(D_FF, D_MODEL), ffn_conv_w=g_ffc.transpose(1, 0, 2).reshape(3, 2 * D_FF))

    full = dict(
        w_in=jnp.pad(g_in.transpose(1, 0, 2).reshape(D_MODEL, PROJ_COLS), ((0, 0), (0, PROJ_PAD - PROJ_COLS))),
        dn_conv_w=g_dnc.transpose(1, 0, 2).reshape(4, 3 * DN_WIDTH),
        attn_norm_g=attn_norm_g, dn_a_log=dn_a_log, dn_dt_bias=dn_dt_bias, dn_out_norm_g=dn_out_norm_g,
        sg_norm_g=sg_norm_g, sg_w=sg_w[0], sg_b=sg_b[0], ffn_norm_g=ffn_norm_g, ffn_conv_b=ffn_conv_b,
        final_norm_g=final_norm_g[None])

    pending = {}
    early_names = ("dn_out_norm_g", "sg_norm_g", "sg_w", "sg_b", "ffn_norm_g", "ffn_conv_w", "ffn_conv_b",
                   "final_norm_g")
    late_names = ("attn_norm_g", "dn_a_log", "dn_dt_bias", "dn_conv_w")

    def on_grad(name, gw):
        if name == "small_early":
            buf = _pack([gw[n] for n in early_names])
            land = lax.dynamic_update_index_in_dim(lax.empty((8,) + buf.shape, F32), buf, me_lin, 0)
            s_sem, r_sem, src, lands, tok = _transfer_start("small_early_start", [buf], [land], 7, _small_copies)
            pending[name] = (s_sem, r_sem, src, lands)
            return tok
        if name in ("w_in", "w_up"):
            g8 = gw.astype(BF16).reshape(D_MODEL, 4, -1).transpose(1, 0, 2)
        else:
            g8 = gw.astype(BF16)
        g8 = g8.reshape(8, -1, g8.shape[-1])
        land = lax.empty((7,) + g8.shape[1:], BF16)
        s_sem, r_sem, src, lands, tok = _transfer_start(f"reduce_{name}_start", [g8], [land], 7, _pieces_copies)
        pending[name] = (s_sem, r_sem, src, lands)
        return tok

    loss_row, grad_x, g = _local_step(x[0], loss_target[0], full, dep=token, late_weights=late_weights,
                                      on_grad=on_grad)
    loss = lax.psum(loss_row[0, 0], ("x", "y", "c"))

    small_names = REPL + CONV
    late_all = _exchange_small(_pack([g[n] for n in late_names]))
    late_sum = _sum_slots("sum_small_late", late_all[0], late_all[1:])
    s_sem, r_sem, src, lands = pending["small_early"]
    _, (early_all,) = _transfer_wait("small_early_wait", s_sem, r_sem, src, lands, _small_copies, grad_x)
    early_sum = _sum_slots("sum_small_early", early_all[0], early_all[1:])
    sg = dict(zip(late_names, _unpack(late_sum, [g[n].shape for n in late_names])))
    sg.update(zip(early_names, _unpack(early_sum, [g[n].shape for n in early_names])))
    sg["dn_conv_w"] = lax.dynamic_slice_in_dim(sg["dn_conv_w"], chip * (3 * DN_WIDTH // 4), 3 * DN_WIDTH // 4, axis=1)
    sg["ffn_conv_w"] = lax.dynamic_slice_in_dim(sg["ffn_conv_w"], chip * (2 * D_FF // 4), 2 * D_FF // 4, axis=1)

    halves = []
    for n in ("w_down", "w_up", "w_out", "w_in"):
        s_sem, r_sem, src, lands = pending[n]
        sent, got = _transfer_wait(f"reduce_{n}_wait", s_sem, r_sem, src, lands, _pieces_copies, grad_x)
        own = lax.dynamic_index_in_dim(sent[0], me_lin, axis=0, keepdims=False)
        halves.append(_sum_slots(f"sum_{n}", own, got[0]))
    theirs = _pair_swap(halves)
    core = ci.astype(jnp.int32).reshape(1)
    grads, delta, new_m, new_v = {}, {}, {}, {}
    for n, mine_h, their_h in zip(("w_down", "w_up", "w_out", "w_in"), halves, theirs):
        shp = W[n].shape
        gr, d, mn, vn = _adamw_halves(f"adamw_{n}", W[n][0], mine_h, their_h, Mo[n][0], Vo[n][0], core)
        grads[n], delta[n], new_m[n], new_v[n] = gr.reshape(shp), d.reshape(shp), mn.reshape(shp), vn.reshape(shp)
    shapes = [W[n].shape for n in small_names]
    for n in small_names:
        grads[n] = sg[n].reshape(W[n].shape)
    d, mn, vn = _adamw("adamw_small", _pack([W[n] for n in small_names]), _pack([grads[n] for n in small_names]),
                       _pack([Mo[n] for n in small_names]), _pack([Vo[n] for n in small_names]))
    for dst, buf in ((delta, d), (new_m, mn), (new_v, vn)):
        dst.update(zip(small_names, _unpack(buf, shapes)))

    return (loss, grad_x[None], *[grads[n] for n in ORDER], *[delta[n] for n in ORDER],
            *[new_m[n] for n in ORDER], *[new_v[n] for n in ORDER])
```

```python
import functools
import math

import jax
import jax.numpy as jnp
from jax import lax
from jax.experimental import pallas as pl
from jax.experimental.pallas import tpu as pltpu

F32 = jnp.float32
BF16 = jnp.bfloat16

D_MODEL = 1024
CHUNK = 64
HEAD_DIM = 128
N_HEADS = 4
DN_WIDTH = 512
SG_WIDTH = 512
SG_GROUPS = 4
SG_BLOCK = 128
D_FF = 2816
PROJ_COLS = 3080
PROJ_PAD = 3200
BA_COL = 3072
EPS = 1e-6
NEG = -1e30
VMEM_LIMIT = 56 * 1024 * 1024

ADAM_LR = 0.001
ADAM_B1 = 0.9
ADAM_B2 = 0.999
ADAM_EPS = 1e-08
ADAM_WD = 0.01
ADAM_STEP = 10

MESH = pl.DeviceIdType.MESH
ANY = pl.BlockSpec(memory_space=pl.ANY)


def _cp(*sem):
    return pltpu.CompilerParams(dimension_semantics=sem, vmem_limit_bytes=VMEM_LIMIT)


def _bf(a):
    return a.astype(BF16)


def _nn(a, b):
    return jnp.dot(_bf(a), _bf(b), preferred_element_type=F32)


def _nt(a, b):
    return lax.dot_general(_bf(a), _bf(b), (((1,), (1,)), ((), ())), preferred_element_type=F32)


def _tn(a, b):
    return lax.dot_general(_bf(a), _bf(b), (((0,), (0,)), ((), ())), preferred_element_type=F32)


def _split(a):
    hi = _bf(a)
    return hi, _bf(a - hi.astype(F32))


def _sigmoid(x):
    return 0.5 * jnp.tanh(0.5 * x) + 0.5


def _silu(x):
    return x * _sigmoid(x)


def _dsilu(x):
    s = _sigmoid(x)
    return s * (1.0 + x * (1.0 - s))


_GELU_C = math.sqrt(2.0 / math.pi)
_GELU_A = 0.044715


def _gelu(x):
    return 0.5 * x * (1.0 + jnp.tanh(_GELU_C * (x + _GELU_A * x * x * x)))


def _dgelu(x):
    t = jnp.tanh(_GELU_C * (x + _GELU_A * x * x * x))
    return 0.5 * (1.0 + t) + 0.5 * x * (1.0 - t * t) * _GELU_C * (1.0 + 3.0 * _GELU_A * x * x)


def _softplus(x):
    return jnp.maximum(x, 0.0) + jnp.log(1.0 + jnp.exp(-jnp.abs(x)))


def _mm_nn(name, a, b, out_dtype, tm, tn, res=None):
    M, K = a.shape
    N = b.shape[1]
    tm, tn = min(tm, M), min(tn, N)

    def body(*refs):
        a_ref, b_ref = refs[0], refs[1]
        o_ref = refs[-1]
        acc = _nn(a_ref[...], b_ref[...])
        if res is not None:
            acc = acc + refs[2][...]
        o_ref[...] = acc.astype(o_ref.dtype)

    in_specs = [pl.BlockSpec((tm, K), lambda j, i: (i, 0)), pl.BlockSpec((K, tn), lambda j, i: (0, j))]
    args = [a, b]
    if res is not None:
        in_specs.append(pl.BlockSpec((tm, tn), lambda j, i: (i, j)))
        args.append(res)
    return pl.pallas_call(
        body, name=name, grid=(N // tn, M // tm), in_specs=in_specs,
        out_specs=pl.BlockSpec((tm, tn), lambda j, i: (i, j)),
        out_shape=jax.ShapeDtypeStruct((M, N), out_dtype),
        compiler_params=_cp("parallel", "parallel"))(*args)


def _with_dep(in_specs, args, dep):
    if dep is None:
        return in_specs, args
    return in_specs + [ANY], args + [dep]


def _mm_nt(name, a, b, out_dtype, tm, tn, dep=None):
    M, K = a.shape
    N = b.shape[0]
    tm, tn = min(tm, M), min(tn, N)

    def body(a_ref, b_ref, *rest):
        o_ref = rest[-1]
        o_ref[...] = _nt(a_ref[...], b_ref[...]).astype(o_ref.dtype)

    in_specs, args = _with_dep(
        [pl.BlockSpec((tm, K), lambda i, j: (i, 0)), pl.BlockSpec((tn, K), lambda i, j: (j, 0))], [a, b], dep)
    return pl.pallas_call(
        body, name=name, grid=(M // tm, N // tn), in_specs=in_specs,
        out_specs=pl.BlockSpec((tm, tn), lambda i, j: (i, j)),
        out_shape=jax.ShapeDtypeStruct((M, N), out_dtype),
        compiler_params=_cp("parallel", "parallel"))(*args)


def _mm_tn(name, a, b, tm, tn, tk):
    T, M = a.shape
    N = b.shape[1]
    tm, tn, tk = min(tm, M), min(tn, N), min(tk, T)

    def body(a_ref, b_ref, o_ref):
        @pl.when(pl.program_id(2) == 0)
        def _():
            o_ref[...] = jnp.zeros_like(o_ref)
        o_ref[...] += _tn(a_ref[...], b_ref[...])

    return pl.pallas_call(
        body, name=name, grid=(M // tm, N // tn, T // tk),
        in_specs=[pl.BlockSpec((tk, tm), lambda i, j, k: (k, i)), pl.BlockSpec((tk, tn), lambda i, j, k: (k, j))],
        out_specs=pl.BlockSpec((tm, tn), lambda i, j, k: (i, j)),
        out_shape=jax.ShapeDtypeStruct((M, N), F32),
        compiler_params=_cp("parallel", "parallel", "arbitrary"))(a, b)


def _rms_fwd(name, x, g, rb=512, dep=None):
    T, Dm = x.shape
    rb = min(rb, T)

    def body(x_ref, g_ref, *rest):
        h_ref = rest[-1]
        xv = x_ref[...]
        r = lax.rsqrt(jnp.mean(xv * xv, axis=-1, keepdims=True) + EPS)
        h_ref[...] = (xv * r * g_ref[...]).astype(BF16)

    in_specs, args = _with_dep(
        [pl.BlockSpec((rb, Dm), lambda i: (i, 0)), pl.BlockSpec((1, Dm), lambda i: (0, 0))], [x, g], dep)
    return pl.pallas_call(
        body, name=name, grid=(T // rb,), in_specs=in_specs,
        out_specs=pl.BlockSpec((rb, Dm), lambda i: (i, 0)),
        out_shape=jax.ShapeDtypeStruct((T, Dm), BF16), compiler_params=_cp("parallel"))(*args)


def _rms_bwd(name, dh, x, g, dres, rb=512):
    T, Dm = x.shape
    rb = min(rb, T)

    def body(dh_ref, x_ref, g_ref, dres_ref, dx_ref, gg_ref):
        @pl.when(pl.program_id(0) == 0)
        def _():
            gg_ref[...] = jnp.zeros_like(gg_ref)
        xv = x_ref[...]
        r = lax.rsqrt(jnp.mean(xv * xv, axis=-1, keepdims=True) + EPS)
        xh = xv * r
        dhv = dh_ref[...]
        gg_ref[...] += jnp.sum(dhv * xh, axis=0, keepdims=True)
        dxh = dhv * g_ref[...]
        dx_ref[...] = dres_ref[...] + r * (dxh - xh * jnp.mean(dxh * xh, axis=-1, keepdims=True))

    row = pl.BlockSpec((rb, Dm), lambda i: (i, 0))
    vec = pl.BlockSpec((1, Dm), lambda i: (0, 0))
    return pl.pallas_call(
        body, name=name, grid=(T // rb,), in_specs=[row, row, vec, row], out_specs=[row, vec],
        out_shape=[jax.ShapeDtypeStruct((T, Dm), F32), jax.ShapeDtypeStruct((1, Dm), F32)],
        compiler_params=_cp("arbitrary"))(dh, x, g, dres)


def _loss_head(x3, tgt, g, rb=512):
    T, Dm = x3.shape
    rb = min(rb, T)

    def body(x_ref, t_ref, g_ref, loss_ref, dx_ref, gg_ref):
        @pl.when(pl.program_id(0) == 0)
        def _():
            gg_ref[...] = jnp.zeros_like(gg_ref)
            loss_ref[...] = jnp.zeros_like(loss_ref)
        xv = x_ref[...]
        r = lax.rsqrt(jnp.mean(xv * xv, axis=-1, keepdims=True) + EPS)
        xh = xv * r
        e = xh * g_ref[...] - t_ref[...]
        loss_ref[...] += jnp.zeros_like(loss_ref) + (0.5 / Dm) * jnp.sum(e * e)
        dy = e * (1.0 / Dm)
        gg_ref[...] += jnp.sum(dy * xh, axis=0, keepdims=True)
        dxh = dy * g_ref[...]
        dx_ref[...] = r * (dxh - xh * jnp.mean(dxh * xh, axis=-1, keepdims=True))

    row = pl.BlockSpec((rb, Dm), lambda i: (i, 0))
    vec = pl.BlockSpec((1, Dm), lambda i: (0, 0))
    return pl.pallas_call(
        body, name="loss_head", grid=(T // rb,), in_specs=[row, row, vec],
        out_specs=[pl.BlockSpec((1, 128), lambda i: (0, 0)), row, vec],
        out_shape=[jax.ShapeDtypeStruct((1, 128), F32), jax.ShapeDtypeStruct((T, Dm), F32),
                   jax.ShapeDtypeStruct((1, Dm), F32)],
        compiler_params=_cp("arbitrary"))(x3, tgt, g)


def _prev_rows(cur, halo, i):
    return jnp.concatenate([jnp.where(i > 0, halo, 0.0), cur], axis=0)


def _shift_down(ext, k, rb):
    if k == 0:
        return ext[8:8 + rb]
    return pltpu.roll(ext, k, 0)[8:8 + rb]


def _conv_fwd_vals(ext, w, rb):
    K = w.shape[0]
    out = _shift_down(ext, K - 1, rb) * w[0:1]
    for k in range(1, K):
        out = out + _shift_down(ext, K - 1 - k, rb) * w[k:k + 1]
    return out


def _halo_prev_spec(rb, width, col=0):
    return pl.BlockSpec((8, width), lambda i: (jnp.maximum(i * (rb // 8) - 1, 0), col))


def _conv_bwd_in(name, dc, w, rb):
    T, W = dc.shape
    K = w.shape[0]
    rb = min(rb, T)
    nb = T // rb

    def body(dc_ref, nx_ref, w_ref, o_ref):
        i = pl.program_id(0)
        ext = jnp.concatenate([dc_ref[...], jnp.where(i < nb - 1, nx_ref[...], 0.0)], axis=0)
        wv = w_ref[...]
        out = ext[0:rb] * wv[K - 1:K]
        for k in range(K - 1):
            s = K - 1 - k
            out = out + pltpu.roll(ext, rb + 8 - s, 0)[0:rb] * wv[k:k + 1]
        o_ref[...] = out.astype(BF16)

    return pl.pallas_call(
        body, name=name, grid=(nb,),
        in_specs=[pl.BlockSpec((rb, W), lambda i: (i, 0)),
                  pl.BlockSpec((8, W), lambda i: (jnp.minimum((i + 1) * (rb // 8), T // 8 - 1), 0)),
                  pl.BlockSpec((K, W), lambda i: (0, 0))],
        out_specs=pl.BlockSpec((rb, W), lambda i: (i, 0)),
        out_shape=jax.ShapeDtypeStruct((T, W), BF16), compiler_params=_cp("parallel"))(dc, dc, w)


LANES = 128
FF_STRIPS = D_FF // LANES


def _strip(j, base=0):
    return pl.ds(pl.multiple_of(base + j * LANES, LANES), LANES)


def _ffn_act(up, w, b, rb=256):
    T, W = up.shape
    rb = min(rb, T)

    def body(up_ref, halo_ref, w_ref, b_ref, act_ref):
        first = pl.program_id(0) == 0

        def strip(j, carry):
            c = []
            for cols in (_strip(j), _strip(j, D_FF)):
                ext = jnp.concatenate([jnp.where(first, 0.0, halo_ref[:, cols]), up_ref[:, cols]], axis=0)
                c.append(_conv_fwd_vals(ext, w_ref[:, cols], rb) + b_ref[:, cols])
            act_ref[:, _strip(j)] = (_silu(c[0]) * c[1]).astype(BF16)
            return carry

        lax.fori_loop(0, FF_STRIPS, strip, 0)

    return pl.pallas_call(
        body, name="ffn_act", grid=(T // rb,),
        in_specs=[pl.BlockSpec((rb, W), lambda i: (i, 0)), _halo_prev_spec(rb, W),
                  pl.BlockSpec((3, W), lambda i: (0, 0)), pl.BlockSpec((1, W), lambda i: (0, 0))],
        out_specs=pl.BlockSpec((rb, D_FF), lambda i: (i, 0)),
        out_shape=jax.ShapeDtypeStruct((T, D_FF), BF16), compiler_params=_cp("parallel"))(up, up, w, b)


def _ffn_act_bwd(up, dact, w, b, rb=128, dep=None):
    T, W = up.shape
    rb = min(rb, T)
    nb = T // rb
    re = rb + 8

    def body(up_ref, prev_ref, next_ref, da_ref, danext_ref, w_ref, b_ref, *rest):
        dup_ref, gw_ref, gb_ref, ext_scr, dc_scr = rest[-5:]
        i = pl.program_id(0)

        @pl.when(i == 0)
        def _():
            gw_ref[...] = jnp.zeros_like(gw_ref)
            gb_ref[...] = jnp.zeros_like(gb_ref)
        last = i == nb - 1
        row = lax.broadcasted_iota(jnp.int32, (re, 1), 0)
        live = (row < rb) | jnp.logical_not(last)

        def strip(j, carry):
            halves = (_strip(j), _strip(j, D_FF))
            taps, c = [], []
            for h, cols in enumerate(halves):
                ext_scr[h, 0:8] = jnp.where(i > 0, prev_ref[:, cols], 0.0)
                ext_scr[h, 8:8 + rb] = up_ref[:, cols]
                ext_scr[h, 8 + rb:] = next_ref[:, cols]
                tp = [ext_scr[h, 6 + k:6 + k + re] for k in range(3)]
                wv = w_ref[:, cols]
                c.append(tp[0] * wv[0:1] + tp[1] * wv[1:2] + tp[2] * wv[2:3] + b_ref[:, cols])
                taps.append(tp)
            da = jnp.where(live, jnp.concatenate([da_ref[:, halves[0]], danext_ref[:, halves[0]]], axis=0), 0.0)
            s = _sigmoid(c[0])
            gs = c[0] * s
            dcs = (da * c[1] * (s + gs * (1.0 - s)), da * gs)
            for h, (cols, tp, dc) in enumerate(zip(halves, taps, dcs)):
                wv = w_ref[:, cols]
                dc_scr[h] = dc
                dcc = dc[0:rb]
                dup = dcc * wv[2:3] + dc_scr[h, 1:1 + rb] * wv[1:2] + dc_scr[h, 2:2 + rb] * wv[0:1]
                dup_ref[:, cols] = dup.astype(BF16)
                gb_ref[:, cols] += jnp.sum(dcc, axis=0, keepdims=True)
                for k in range(3):
                    gw_ref[k:k + 1, cols] += jnp.sum(tp[k][0:rb] * dcc, axis=0, keepdims=True)
            return carry

        lax.fori_loop(0, FF_STRIPS, strip, 0)

    nxt = lambda i: jnp.minimum((i + 1) * (rb // 8), T // 8 - 1)
    in_specs, args = _with_dep(
        [pl.BlockSpec((rb, W), lambda i: (i, 0)), _halo_prev_spec(rb, W), pl.BlockSpec((8, W), lambda i: (nxt(i), 0)),
         pl.BlockSpec((rb, D_FF), lambda i: (i, 0)), pl.BlockSpec((8, D_FF), lambda i: (nxt(i), 0)),
         pl.BlockSpec((3, W), lambda i: (0, 0)), pl.BlockSpec((1, W), lambda i: (0, 0))],
        [up, up, up, dact, dact, w, b], dep)
    return pl.pallas_call(
        body, name="ffn_act_bwd", grid=(nb,), in_specs=in_specs,
        out_specs=[pl.BlockSpec((rb, W), lambda i: (i, 0)), pl.BlockSpec((3, W), lambda i: (0, 0)),
                   pl.BlockSpec((1, W), lambda i: (0, 0))],
        out_shape=[jax.ShapeDtypeStruct((T, W), BF16), jax.ShapeDtypeStruct((3, W), F32),
                   jax.ShapeDtypeStruct((1, W), F32)],
        scratch_shapes=[pltpu.VMEM((2, rb + 16, LANES), F32), pltpu.VMEM((2, re, LANES), F32)],
        compiler_params=_cp("arbitrary"))(*args)


def _lane_iota(shape):
    return lax.broadcasted_iota(jnp.int32, shape, len(shape) - 1)


def _dn_act(p, conv_w, alog_row, dtb_row, rb=256):
    T = p.shape[0]
    rb = min(rb, T)
    W3 = 3 * DN_WIDTH

    def body(p_ref, halo_ref, ba_ref, w_ref, al_ref, dt_ref, q_ref, k_ref, v_ref, bg_ref):
        ext = _prev_rows(p_ref[...], halo_ref[...], pl.program_id(0))
        s = _silu(_conv_fwd_vals(ext, w_ref[...], rb))
        for h in range(N_HEADS):
            lo = h * HEAD_DIM
            sq = s[:, lo:lo + HEAD_DIM]
            n = lax.rsqrt(jnp.sum(sq * sq, axis=-1, keepdims=True) + EPS)
            q_ref[:, lo:lo + HEAD_DIM] = sq * n * (HEAD_DIM ** -0.5)
            sk = s[:, DN_WIDTH + lo:DN_WIDTH + lo + HEAD_DIM]
            n = lax.rsqrt(jnp.sum(sk * sk, axis=-1, keepdims=True) + EPS)
            k_ref[:, lo:lo + HEAD_DIM] = sk * n
        v_ref[...] = s[:, 2 * DN_WIDTH:]
        ba = ba_ref[...]
        lane = _lane_iota(ba.shape)
        beta = _sigmoid(ba)
        g = -jnp.exp(al_ref[...]) * _softplus(ba + dt_ref[...])
        bg_ref[...] = jnp.where(lane < N_HEADS, beta, jnp.where(lane < 2 * N_HEADS, g, 0.0))

    row512 = pl.BlockSpec((rb, DN_WIDTH), lambda i: (i, 0))
    row128 = pl.BlockSpec((rb, 128), lambda i: (i, 0))
    vec128 = pl.BlockSpec((1, 128), lambda i: (0, 0))
    return pl.pallas_call(
        body, name="dn_act", grid=(T // rb,),
        in_specs=[pl.BlockSpec((rb, W3), lambda i: (i, 0)), _halo_prev_spec(rb, W3),
                  pl.BlockSpec((rb, 128), lambda i: (i, BA_COL // 128)),
                  pl.BlockSpec((4, W3), lambda i: (0, 0)), vec128, vec128],
        out_specs=[row512, row512, row512, row128],
        out_shape=[jax.ShapeDtypeStruct((T, DN_WIDTH), F32)] * 3 + [jax.ShapeDtypeStruct((T, 128), F32)],
        compiler_params=_cp("parallel"))(p, p, p, conv_w, alog_row, dtb_row)


def _dn_act_bwd(p, conv_w, alog_row, dtb_row, dq, dk, dv, dbg, rb=256):
    T = p.shape[0]
    rb = min(rb, T)
    W3 = 3 * DN_WIDTH

    def body(p_ref, halo_ref, ba_ref, w_ref, al_ref, dt_ref, dq_ref, dk_ref, dv_ref, dbg_ref,
             dc_ref, dba_ref, gw_ref, gad_ref):
        @pl.when(pl.program_id(0) == 0)
        def _():
            gw_ref[...] = jnp.zeros_like(gw_ref)
            gad_ref[...] = jnp.zeros_like(gad_ref)
        ext = _prev_rows(p_ref[...], halo_ref[...], pl.program_id(0))
        c = _conv_fwd_vals(ext, w_ref[...], rb)
        s = _silu(c)
        ds = _dsilu(c)
        for h in range(N_HEADS):
            lo = h * HEAD_DIM
            for (off, d_ref, scale) in ((0, dq_ref, HEAD_DIM ** -0.5), (DN_WIDTH, dk_ref, 1.0)):
                sv = s[:, off + lo:off + lo + HEAD_DIM]
                n = lax.rsqrt(jnp.sum(sv * sv, axis=-1, keepdims=True) + EPS)
                hat = sv * n
                dvv = d_ref[:, lo:lo + HEAD_DIM]
                dsv = (n * scale) * (dvv - hat * jnp.sum(hat * dvv, axis=-1, keepdims=True))
                dc_ref[:, off + lo:off + lo + HEAD_DIM] = dsv * ds[:, off + lo:off + lo + HEAD_DIM]
        dc_ref[:, 2 * DN_WIDTH:] = dv_ref[...] * ds[:, 2 * DN_WIDTH:]
        dc = dc_ref[...]
        for k in range(4):
            gw_ref[k:k + 1, :] += jnp.sum(_shift_down(ext, 3 - k, rb) * dc, axis=0, keepdims=True)
        ba = ba_ref[...]
        dbg = dbg_ref[...]
        lane = _lane_iota(ba.shape)
        beta = _sigmoid(ba)
        ea = jnp.exp(al_ref[...])
        z = ba + dt_ref[...]
        d_a = dbg * (-ea) * _sigmoid(z)
        dba = jnp.where(lane < N_HEADS, dbg * beta * (1.0 - beta), jnp.where(lane < 2 * N_HEADS, d_a, 0.0))
        dba_ref[...] = dba.astype(BF16)
        isg = (lane >= N_HEADS) & (lane < 2 * N_HEADS)
        g = -ea * _softplus(z)
        gad_ref[0:1, :] += jnp.sum(jnp.where(isg, dbg * g, 0.0), axis=0, keepdims=True)
        gad_ref[1:2, :] += jnp.sum(jnp.where(isg, d_a, 0.0), axis=0, keepdims=True)

    row512 = pl.BlockSpec((rb, DN_WIDTH), lambda i: (i, 0))
    row128 = pl.BlockSpec((rb, 128), lambda i: (i, 0))
    vec128 = pl.BlockSpec((1, 128), lambda i: (0, 0))
    return pl.pallas_call(
        body, name="dn_act_bwd", grid=(T // rb,),
        in_specs=[pl.BlockSpec((rb, W3), lambda i: (i, 0)), _halo_prev_spec(rb, W3),
                  pl.BlockSpec((rb, 128), lambda i: (i, BA_COL // 128)),
                  pl.BlockSpec((4, W3), lambda i: (0, 0)), vec128, vec128,
                  row512, row512, row512, row128],
        out_specs=[pl.BlockSpec((rb, W3), lambda i: (i, 0)), row128,
                   pl.BlockSpec((4, W3), lambda i: (0, 0)), pl.BlockSpec((2, 128), lambda i: (0, 0))],
        out_shape=[jax.ShapeDtypeStruct((T, W3), F32), jax.ShapeDtypeStruct((T, 128), BF16),
                   jax.ShapeDtypeStruct((4, W3), F32), jax.ShapeDtypeStruct((2, 128), F32)],
        compiler_params=_cp("arbitrary"))(p, p, p, conv_w, alog_row, dtb_row, dq, dk, dv, dbg)


def _tri(incl):
    ii = lax.broadcasted_iota(jnp.int32, (CHUNK, CHUNK), 0)
    jj = lax.broadcasted_iota(jnp.int32, (CHUNK, CHUNK), 1)
    return ii, jj, ((ii >= jj) if incl else (ii > jj))


def _dn_chunk(k, bg):
    T = k.shape[0]
    N = T // CHUNK

    def body(k_ref, bg_ref, gc_ref, gct_ref, l_ref):
        ii, jj, incl = _tri(True)
        bgv = bg_ref[...]
        gc = jnp.dot(incl.astype(F32), bgv, precision=lax.Precision.HIGHEST, preferred_element_type=F32)
        gc_ref[...] = gc
        gct = gc.T
        gct_ref[0] = gct[0:8]
        for h in range(N_HEADS):
            kh = k_ref[:, h * HEAD_DIM:(h + 1) * HEAD_DIM]
            beta = bgv[:, h:h + 1]
            gcol = gc[:, N_HEADS + h:N_HEADS + h + 1]
            grow = gct[N_HEADS + h:N_HEADS + h + 1, :]
            gam = jnp.exp(jnp.where(ii > jj, gcol - grow, NEG))
            l_ref[0, h] = _nt(kh * beta, kh) * gam

    return pl.pallas_call(
        body, name="dn_chunk", grid=(N,),
        in_specs=[pl.BlockSpec((CHUNK, DN_WIDTH), lambda n: (n, 0)), pl.BlockSpec((CHUNK, 128), lambda n: (n, 0))],
        out_specs=[pl.BlockSpec((CHUNK, 128), lambda n: (n, 0)), pl.BlockSpec((1, 8, CHUNK), lambda n: (n, 0, 0)),
                   pl.BlockSpec((1, N_HEADS, CHUNK, CHUNK), lambda n: (n, 0, 0, 0))],
        out_shape=[jax.ShapeDtypeStruct((T, 128), F32), jax.ShapeDtypeStruct((N, 8, CHUNK), F32),
                   jax.ShapeDtypeStruct((N, N_HEADS, CHUNK, CHUNK), F32)],
        compiler_params=_cp("parallel"))(k, bg)


def _tri_inv(lt):
    S = lt.shape[1]

    def body(l_ref, a_ref):
        col = lax.broadcasted_iota(jnp.int32, (CHUNK, S), 0)
        for i in range(CHUNK):
            def step(j, acc):
                return acc - l_ref[pl.ds(i * CHUNK + j, 1), :] * a_ref[j]
            a_ref[i] = lax.fori_loop(0, i, step, (col == i).astype(F32))

    return pl.pallas_call(
        body, name="tri_inv", out_shape=jax.ShapeDtypeStruct((CHUNK, CHUNK, S), F32),
        compiler_params=pltpu.CompilerParams(vmem_limit_bytes=VMEM_LIMIT))(lt)


def _dn_head_terms(qh, kh, vh, beta, gcol, grow):
    ii, jj, incl = _tri(True)
    gam = jnp.exp(jnp.where(incl, gcol - grow, NEG))
    glast = grow[:, CHUNK - 1:CHUNK]
    E = jnp.exp(gcol)
    Fd = jnp.exp(glast - gcol)
    cd = jnp.exp(glast)
    kb = kh * beta
    return dict(ii=ii, jj=jj, gam=gam, E=E, F=Fd, cd=cd, kb=kb, vb=vh * beta, W=kb * E, qE=qh * E, kt=kh * Fd)


def _apply_a(a, u):
    hi, lo = _split(a)
    ub = _bf(u)
    return jnp.dot(hi, ub, preferred_element_type=F32) + jnp.dot(lo, ub, preferred_element_type=F32)


def _apply_at(a, u):
    hi, lo = _split(a)
    ub = _bf(u)
    dn = (((0,), (0,)), ((), ()))
    return (lax.dot_general(hi, ub, dn, preferred_element_type=F32)
            + lax.dot_general(lo, ub, dn, preferred_element_type=F32))


def _dn_scan(q, k, v, bg, gc, gct, a):
    T = q.shape[0]
    N = T // CHUNK

    def body(q_ref, k_ref, v_ref, bg_ref, gc_ref, gct_ref, a_ref, o_ref, sall_ref, s_ref):
        @pl.when(pl.program_id(0) == 0)
        def _():
            s_ref[...] = jnp.zeros_like(s_ref)
        bgv, gcv, gctv = bg_ref[...], gc_ref[...], gct_ref[0]
        H = range(N_HEADS)
        sl = [slice(h * HEAD_DIM, (h + 1) * HEAD_DIM) for h in H]
        q_, k_ = [q_ref[:, s] for s in sl], [k_ref[:, s] for s in sl]
        t = [_dn_head_terms(q_[h], k_[h], v_ref[:, sl[h]], bgv[:, h:h + 1],
                            gcv[:, N_HEADS + h:N_HEADS + h + 1], gctv[N_HEADS + h:N_HEADS + h + 1, :]) for h in H]
        S = [s_ref[h] for h in H]
        for h in H:
            sall_ref[0, h] = S[h]
        WS = [_nn(t[h]["W"], S[h]) for h in H]
        QK = [_nt(q_[h], k_[h]) for h in H]
        qS = [_nn(t[h]["qE"], S[h]) for h in H]
        vn = [_apply_a(a_ref[0, h], t[h]["vb"] - WS[h]) for h in H]
        Pv = [_nn(QK[h] * t[h]["gam"], vn[h]) for h in H]
        kv = [_tn(t[h]["kt"], vn[h]) for h in H]
        for h in H:
            o_ref[:, sl[h]] = qS[h] + Pv[h]
            s_ref[h] = t[h]["cd"] * S[h] + kv[h]

    row512 = pl.BlockSpec((CHUNK, DN_WIDTH), lambda n: (n, 0))
    row128 = pl.BlockSpec((CHUNK, 128), lambda n: (n, 0))
    return pl.pallas_call(
        body, name="dn_scan", grid=(N,),
        in_specs=[row512, row512, row512, row128, row128, pl.BlockSpec((1, 8, CHUNK), lambda n: (n, 0, 0)),
                  pl.BlockSpec((1, N_HEADS, CHUNK, CHUNK), lambda n: (n, 0, 0, 0))],
        out_specs=[row512, pl.BlockSpec((1, N_HEADS, HEAD_DIM, HEAD_DIM), lambda n: (n, 0, 0, 0))],
        out_shape=[jax.ShapeDtypeStruct((T, DN_WIDTH), F32),
                   jax.ShapeDtypeStruct((N, N_HEADS, HEAD_DIM, HEAD_DIM), F32)],
        scratch_shapes=[pltpu.VMEM((N_HEADS, HEAD_DIM, HEAD_DIM), F32)],
        compiler_params=_cp("arbitrary"))(q, k, v, bg, gc, gct, a)


def _dn_scan_bwd(q, k, v, bg, gc, gct, a, sall, do, dep=None):
    T = q.shape[0]
    N = T // CHUNK

    def body(q_ref, k_ref, v_ref, bg_ref, gc_ref, gct_ref, a_ref, sall_ref, do_ref, *rest):
        dq_ref, dk_ref, dv_ref, dbg_ref, ds_ref = rest[-5:]
        @pl.when(pl.program_id(0) == 0)
        def _():
            ds_ref[...] = jnp.zeros_like(ds_ref)
        bgv, gcv, gctv = bg_ref[...], gc_ref[...], gct_ref[0]
        lane = _lane_iota((CHUNK, 128))
        rowi = lax.broadcasted_iota(jnp.int32, (CHUNK, 1), 0)
        H = range(N_HEADS)
        sl = [slice(h * HEAD_DIM, (h + 1) * HEAD_DIM) for h in H]
        q_, k_, v_ = [q_ref[:, s] for s in sl], [k_ref[:, s] for s in sl], [v_ref[:, s] for s in sl]
        dO = [do_ref[:, s] for s in sl]
        beta = [bgv[:, h:h + 1] for h in H]
        t = [_dn_head_terms(q_[h], k_[h], v_[h], beta[h], gcv[:, N_HEADS + h:N_HEADS + h + 1],
                            gctv[N_HEADS + h:N_HEADS + h + 1, :]) for h in H]
        ii, jj = t[0]["ii"], t[0]["jj"]
        gam, E, Fd, cd, kb = ([t[h][n] for h in H] for n in ("gam", "E", "F", "cd", "kb"))
        S = [sall_ref[0, h] for h in H]
        dSn = [ds_ref[h] for h in H]
        A = [a_ref[0, h] for h in H]
        WS = [_nn(t[h]["W"], S[h]) for h in H]
        KK = [_nt(kb[h], k_[h]) for h in H]
        QK = [_nt(q_[h], k_[h]) for h in H]
        ktdS = [_nn(t[h]["kt"], dSn[h]) for h in H]
        d_qE = [_nt(dO[h], S[h]) for h in H]
        vn = [_apply_a(A[h], t[h]["vb"] - WS[h]) for h in H]
        PtdO = [_tn(QK[h] * gam[h], dO[h]) for h in H]
        qEdO = [_tn(t[h]["qE"], dO[h]) for h in H]
        dU = [_apply_at(A[h], PtdO[h] + ktdS[h]) for h in H]
        d_kt = [_nt(vn[h], dSn[h]) for h in H]
        dOvn = [_nt(dO[h], vn[h]) for h in H]
        dUvn = [_nt(dU[h], vn[h]) for h in H]
        dUS = [_nt(dU[h], S[h]) for h in H]
        WdU = [_tn(t[h]["W"], dU[h]) for h in H]
        for h in H:
            ds_ref[h] = cd[h] * dSn[h] + qEdO[h] - WdU[h]
        dQK = [jnp.where(ii >= jj, dOvn[h], 0.0) * gam[h] for h in H]
        dKK = [jnp.where(ii > jj, -dUvn[h], 0.0) * gam[h] for h in H]
        dQKk = [_nn(dQK[h], k_[h]) for h in H]
        dKKk = [_nn(dKK[h], k_[h]) for h in H]
        dQKq = [_tn(dQK[h], q_[h]) for h in H]
        dKKkb = [_tn(dKK[h], kb[h]) for h in H]
        dbeta_arr = jnp.zeros((CHUNK, 128), F32)
        dgc_arr = jnp.zeros((CHUNK, 128), F32)
        for h in H:
            dW = -dUS[h]
            dq_ref[:, sl[h]] = dQKk[h] + d_qE[h] * E[h]
            d_kb = dKKk[h] + dW * E[h]
            dk_ref[:, sl[h]] = dQKq[h] + dKKkb[h] + d_kb * beta[h] + d_kt[h] * Fd[h]
            dv_ref[:, sl[h]] = dU[h] * beta[h]
            Z = dQK[h] * QK[h] + dKK[h] * KK[h]
            d_cd = jnp.sum(S[h] * dSn[h])
            dbeta = jnp.sum(dU[h] * v_[h] + d_kb * k_[h], axis=-1, keepdims=True)
            dE = jnp.sum(dW * kb[h] + d_qE[h] * q_[h], axis=-1, keepdims=True)
            dFF = jnp.sum(d_kt[h] * k_[h], axis=-1, keepdims=True) * Fd[h]
            dgc = (dE * E[h] - dFF + jnp.sum(Z, axis=-1, keepdims=True) - jnp.sum(Z.T, axis=-1, keepdims=True)
                   + jnp.where(rowi == CHUNK - 1, jnp.sum(dFF) + d_cd * cd[h], 0.0))
            dbeta_arr = dbeta_arr + jnp.where(lane == h, dbeta, 0.0)
            dgc_arr = dgc_arr + jnp.where(lane == N_HEADS + h, dgc, 0.0)
        ii, jj, _ = _tri(True)
        rev = (jj >= ii).astype(F32)
        dbg_ref[...] = dbeta_arr + jnp.dot(rev, dgc_arr, precision=lax.Precision.HIGHEST,
                                           preferred_element_type=F32)

    row512 = pl.BlockSpec((CHUNK, DN_WIDTH), lambda n: (N - 1 - n, 0))
    row128 = pl.BlockSpec((CHUNK, 128), lambda n: (N - 1 - n, 0))
    in_specs, args = _with_dep(
        [row512, row512, row512, row128, row128,
         pl.BlockSpec((1, 8, CHUNK), lambda n: (N - 1 - n, 0, 0)),
         pl.BlockSpec((1, N_HEADS, CHUNK, CHUNK), lambda n: (N - 1 - n, 0, 0, 0)),
         pl.BlockSpec((1, N_HEADS, HEAD_DIM, HEAD_DIM), lambda n: (N - 1 - n, 0, 0, 0)), row512],
        [q, k, v, bg, gc, gct, a, sall, do], dep)
    return pl.pallas_call(
        body, name="dn_scan_bwd", grid=(N,), in_specs=in_specs,
        out_specs=[row512, row512, row512, row128],
        out_shape=[jax.ShapeDtypeStruct((T, DN_WIDTH), F32)] * 3 + [jax.ShapeDtypeStruct((T, 128), F32)],
        scratch_shapes=[pltpu.VMEM((N_HEADS, HEAD_DIM, HEAD_DIM), F32)],
        compiler_params=_cp("arbitrary"))(*args)


def _sg_mask():
    ii = lax.broadcasted_iota(jnp.int32, (SG_BLOCK, SG_BLOCK), 0) // CHUNK
    jj = lax.broadcasted_iota(jnp.int32, (SG_BLOCK, SG_BLOCK), 1) // CHUNK
    return jj <= ii


def _mix_fwd(o, p, ong, sgn, sgw, sgbt):
    T = o.shape[0]
    rb = SG_BLOCK

    def body(o_ref, gate_ref, u_ref, vg_ref, ong_ref, sgn_ref, sgw_ref, sgbt_ref, mix_ref):
        mask = _sg_mask()
        gate = gate_ref[...]
        for h in range(N_HEADS):
            sl = slice(h * HEAD_DIM, (h + 1) * HEAD_DIM)
            oh = o_ref[:, sl]
            r = lax.rsqrt(jnp.mean(oh * oh, axis=-1, keepdims=True) + EPS)
            mix_ref[:, sl] = (oh * r * ong_ref[...] * _silu(gate[:, sl])).astype(BF16)
        for gi in range(SG_GROUPS):
            sl = slice(gi * SG_BLOCK, (gi + 1) * SG_BLOCK)
            gv = _gelu(vg_ref[:, sl])
            r = lax.rsqrt(jnp.mean(gv * gv, axis=-1, keepdims=True) + EPS)
            vh = gv * r * sgn_ref[:, sl]
            s = _nn(jnp.where(mask, sgw_ref[gi], 0.0), vh) + sgbt_ref[:, gi:gi + 1]
            mix_ref[:, DN_WIDTH + gi * SG_BLOCK:DN_WIDTH + (gi + 1) * SG_BLOCK] = (_gelu(u_ref[:, sl]) * s).astype(BF16)

    def col(c):
        return pl.BlockSpec((rb, 512), lambda i: (i, c))
    return pl.pallas_call(
        body, name="mix_fwd", grid=(T // rb,),
        in_specs=[pl.BlockSpec((rb, DN_WIDTH), lambda i: (i, 0)), col(3), col(4), col(5),
                  pl.BlockSpec((1, 128), lambda i: (0, 0)), pl.BlockSpec((1, SG_WIDTH), lambda i: (0, 0)),
                  pl.BlockSpec((SG_GROUPS, SG_BLOCK, SG_BLOCK), lambda i: (0, 0, 0)),
                  pl.BlockSpec((SG_BLOCK, 128), lambda i: (0, 0))],
        out_specs=pl.BlockSpec((rb, D_MODEL), lambda i: (i, 0)),
        out_shape=jax.ShapeDtypeStruct((T, D_MODEL), BF16),
        compiler_params=_cp("parallel"))(o, p, p, p, ong, sgn, sgw, sgbt)


def _mix_bwd(o, p, ong, sgn, sgw, sgbt, dmix, dep=None):
    T = o.shape[0]
    rb = SG_BLOCK

    def body(o_ref, gate_ref, u_ref, vg_ref, ong_ref, sgn_ref, sgw_ref, sgbt_ref, dmix_ref, *rest):
        do_ref, dp_ref, gong_ref, gsgn_ref, gsgw_ref, gsgbt_ref = rest[-6:]
        @pl.when(pl.program_id(0) == 0)
        def _():
            gong_ref[...] = jnp.zeros_like(gong_ref)
            gsgn_ref[...] = jnp.zeros_like(gsgn_ref)
            gsgw_ref[...] = jnp.zeros_like(gsgw_ref)
            gsgbt_ref[...] = jnp.zeros_like(gsgbt_ref)
        mask = _sg_mask()
        gate = gate_ref[...]
        lane = _lane_iota((SG_BLOCK, 128))
        for h in range(N_HEADS):
            sl = slice(h * HEAD_DIM, (h + 1) * HEAD_DIM)
            oh = o_ref[:, sl]
            dm = dmix_ref[:, sl]
            r = lax.rsqrt(jnp.mean(oh * oh, axis=-1, keepdims=True) + EPS)
            oh_hat = oh * r
            gt = gate[:, sl]
            sg = _silu(gt)
            dp_ref[:, sl] = (dm * oh_hat * ong_ref[...] * _dsilu(gt)).astype(BF16)
            dn_ = dm * sg
            gong_ref[...] += jnp.sum(dn_ * oh_hat, axis=0, keepdims=True)
            dhat = dn_ * ong_ref[...]
            do_ref[:, sl] = r * (dhat - oh_hat * jnp.mean(dhat * oh_hat, axis=-1, keepdims=True))
        for gi in range(SG_GROUPS):
            sl = slice(gi * SG_BLOCK, (gi + 1) * SG_BLOCK)
            vraw = vg_ref[:, sl]
            gv = _gelu(vraw)
            r = lax.rsqrt(jnp.mean(gv * gv, axis=-1, keepdims=True) + EPS)
            vhat = gv * r
            vn = vhat * sgn_ref[:, sl]
            wm = jnp.where(mask, sgw_ref[gi], 0.0)
            s = _nn(wm, vn) + sgbt_ref[:, gi:gi + 1]
            uraw = u_ref[:, sl]
            dm = dmix_ref[:, DN_WIDTH + gi * SG_BLOCK:DN_WIDTH + (gi + 1) * SG_BLOCK]
            dp_ref[:, DN_WIDTH + gi * SG_BLOCK:DN_WIDTH + (gi + 1) * SG_BLOCK] = (dm * s * _dgelu(uraw)).astype(BF16)
            ds = dm * _gelu(uraw)
            gsgbt_ref[...] += jnp.where(lane == gi, jnp.sum(ds, axis=-1, keepdims=True), 0.0)
            gsgw_ref[gi] += jnp.where(mask, _nt(ds, vn), 0.0)
            dvn = _tn(wm, ds)
            gsgn_ref[:, sl] += jnp.sum(dvn * vhat, axis=0, keepdims=True)
            dhat = dvn * sgn_ref[:, sl]
            dgv = r * (dhat - vhat * jnp.mean(dhat * vhat, axis=-1, keepdims=True))
            dp_ref[:, 2 * DN_WIDTH + gi * SG_BLOCK:2 * DN_WIDTH + (gi + 1) * SG_BLOCK] = (dgv * _dgelu(vraw)).astype(BF16)

    def col(c):
        return pl.BlockSpec((rb, 512), lambda i: (i, c))
    full = lambda *s: pl.BlockSpec(s, lambda i: (0,) * len(s))
    in_specs, args = _with_dep(
        [pl.BlockSpec((rb, DN_WIDTH), lambda i: (i, 0)), col(3), col(4), col(5),
         full(1, 128), full(1, SG_WIDTH), full(SG_GROUPS, SG_BLOCK, SG_BLOCK), full(SG_BLOCK, 128),
         pl.BlockSpec((rb, D_MODEL), lambda i: (i, 0))],
        [o, p, p, p, ong, sgn, sgw, sgbt, dmix], dep)
    return pl.pallas_call(
        body, name="mix_bwd", grid=(T // rb,), in_specs=in_specs,
        out_specs=[pl.BlockSpec((rb, DN_WIDTH), lambda i: (i, 0)), pl.BlockSpec((rb, 3 * 512), lambda i: (i, 0)),
                   full(1, 128), full(1, SG_WIDTH), full(SG_GROUPS, SG_BLOCK, SG_BLOCK), full(SG_BLOCK, 128)],
        out_shape=[jax.ShapeDtypeStruct((T, DN_WIDTH), F32), jax.ShapeDtypeStruct((T, 3 * 512), BF16),
                   jax.ShapeDtypeStruct((1, 128), F32), jax.ShapeDtypeStruct((1, SG_WIDTH), F32),
                   jax.ShapeDtypeStruct((SG_GROUPS, SG_BLOCK, SG_BLOCK), F32),
                   jax.ShapeDtypeStruct((SG_BLOCK, 128), F32)],
        compiler_params=_cp("arbitrary"))(*args)


def _pad_lanes(row, offset=0):
    n = row.shape[1]
    return jnp.pad(row, ((0, 0), (offset, 128 - n - offset)))


def _local_step(x, tgt, w, dep=None, late_weights=None, on_grad=None):
    T = x.shape[0]
    N = T // CHUNK
    on_grad = on_grad or (lambda name, g: None)
    alog_row = _pad_lanes(w["dn_a_log"], N_HEADS)
    dtb_row = _pad_lanes(w["dn_dt_bias"], N_HEADS)
    sgbt = jnp.pad(w["sg_b"].T, ((0, 0), (0, 128 - SG_GROUPS)))

    h1 = _rms_fwd("rms_attn", x, w["attn_norm_g"], dep=dep)
    p = _mm_nn("in_proj", h1, w["w_in"], F32, 512, PROJ_PAD)
    q, k, v, bg = _dn_act(p, w["dn_conv_w"], alog_row, dtb_row)
    gc, gct, lmat = _dn_chunk(k, bg)
    lt = lmat.reshape(N * N_HEADS, CHUNK * CHUNK).T
    at = _tri_inv(lt)
    a = at.reshape(CHUNK * CHUNK, N * N_HEADS).T.reshape(N, N_HEADS, CHUNK, CHUNK)
    o, sall = _dn_scan(q, k, v, bg, gc, gct, a)
    mix = _mix_fwd(o, p, w["dn_out_norm_g"], w["sg_norm_g"], w["sg_w"], sgbt)
    if late_weights is not None:
        w = {**w, **late_weights(mix)}
    x2 = _mm_nn("out_proj", mix, w["w_out"], F32, 512, 1024, res=x)
    h2 = _rms_fwd("rms_ffn", x2, w["ffn_norm_g"])
    up = _mm_nn("up_proj", h2, w["w_up"], F32, 512, D_FF)
    act = _ffn_act(up, w["ffn_conv_w"], w["ffn_conv_b"])
    x3 = _mm_nn("down_proj", act, w["w_down"], F32, 512, 1024, res=x2)
    loss, dx3, g_final = _loss_head(x3, tgt, w["final_norm_g"])

    dact = _mm_nt("d_act", dx3, w["w_down"], F32, 512, D_FF)
    g_w_down = _mm_tn("g_w_down", act, dx3, D_FF, 1024, 512)
    tok = on_grad("w_down", g_w_down)
    dup, g_ffn_conv_w, g_ffn_conv_b = _ffn_act_bwd(up, dact, w["ffn_conv_w"], w["ffn_conv_b"], dep=tok)
    g_w_up = _mm_tn("g_w_up", h2, dup, 1024, 1408, 512)
    tok = on_grad("w_up", g_w_up)
    dh2 = _mm_nt("d_h2", dup, w["w_up"], F32, 512, 1024, dep=tok)
    dx2, g_ffn_norm = _rms_bwd("rms_ffn_bwd", dh2, x2, w["ffn_norm_g"], dx3)
    dmix = _mm_nt("d_mix", dx2, w["w_out"], F32, 512, 1024)
    g_w_out = _mm_tn("g_w_out", mix, dx2, 1024, 1024, 1024)
    tok = on_grad("w_out", g_w_out)
    do, dp_mid, g_ong, g_sgn, g_sgw, g_sgbt = _mix_bwd(o, p, w["dn_out_norm_g"], w["sg_norm_g"], w["sg_w"], sgbt,
                                                      dmix, dep=tok)
    early = dict(dn_out_norm_g=g_ong, sg_norm_g=g_sgn, sg_w=g_sgw, sg_b=g_sgbt[:, :SG_GROUPS].T,
                 ffn_norm_g=g_ffn_norm, ffn_conv_w=g_ffn_conv_w, ffn_conv_b=g_ffn_conv_b, final_norm_g=g_final)
    tok = on_grad("small_early", early)
    dq, dk, dv, dbg = _dn_scan_bwd(q, k, v, bg, gc, gct, a, sall, do, dep=tok)
    dcq, dba, g_dn_conv_w, g_ad = _dn_act_bwd(p, w["dn_conv_w"], alog_row, dtb_row, dq, dk, dv, dbg)
    dp_qkv = _conv_bwd_in("d_qkv", dcq, w["dn_conv_w"], 256)
    dp = jnp.concatenate([dp_qkv, dp_mid, dba], axis=1)
    g_w_in = _mm_tn("g_w_in", h1, dp, 1024, PROJ_PAD, 512)[:, :PROJ_COLS]
    tok = on_grad("w_in", g_w_in)
    dh1 = _mm_nt("d_h1", dp, w["w_in"], F32, 512, 1024, dep=tok)
    grad_x, g_attn_norm = _rms_bwd("rms_attn_bwd", dh1, x, w["attn_norm_g"], dx2)

    grads = dict(
        attn_norm_g=g_attn_norm, w_in=g_w_in, dn_conv_w=g_dn_conv_w,
        dn_a_log=g_ad[0:1, N_HEADS:2 * N_HEADS], dn_dt_bias=g_ad[1:2, N_HEADS:2 * N_HEADS],
        w_out=g_w_out, w_up=g_w_up, w_down=g_w_down, **early)
    return loss, grad_x, grads


def _me():
    return lax.axis_index("x"), lax.axis_index("y"), lax.axis_index("c")


def _peer(rel):
    x, y, c = _me()
    return {"x": (1 - x, y, c), "y": (x, 1 - y, c), "xy": (1 - x, 1 - y, c), "c": (x, y, 1 - c)}[rel]


def _chip_of(dev):
    return 2 * dev[0] + dev[1]


CHIP_RELS = ("x", "y", "xy")


def _run_copies(copies, sends, recvs):
    for cp in copies:
        cp.start()
    for cp in recvs:
        cp.wait_recv()
    for cp in sends:
        cp.wait_send()


def _gather_shards(shards):
    n = len(shards)

    def body(*refs):
        src, out = refs[:n], refs[n:2 * n]
        send_sems, recv_sems, local_sems = refs[2 * n:]
        me = _chip_of(_me())
        local = [pltpu.make_async_copy(src[i], out[i].at[me], local_sems.at[i]) for i in range(n)]
        for cp in local:
            cp.start()
        sends, recvs = [], []
        for i in range(n):
            for r, rel in enumerate(CHIP_RELS):
                k = 3 * i + r
                peer = _peer(rel)
                sends.append(pltpu.make_async_remote_copy(
                    src_ref=src[i], dst_ref=out[i].at[me], send_sem=send_sems.at[k], recv_sem=recv_sems.at[k],
                    device_id=peer, device_id_type=MESH))
                recvs.append(pltpu.make_async_remote_copy(
                    src_ref=src[i], dst_ref=out[i].at[_chip_of(peer)], send_sem=send_sems.at[k],
                    recv_sem=recv_sems.at[k], device_id=peer, device_id_type=MESH))
        _run_copies(sends, sends, recvs)
        for cp in local:
            cp.wait()

    return pl.pallas_call(
        body, name="gather_weights", in_specs=[ANY] * n, out_specs=[ANY] * n,
        out_shape=[jax.ShapeDtypeStruct((4,) + s.shape, s.dtype) for s in shards],
        scratch_shapes=[pltpu.SemaphoreType.DMA((3 * n,)), pltpu.SemaphoreType.DMA((3 * n,)),
                        pltpu.SemaphoreType.DMA((n,))])(*shards)


OTHERS = tuple((fx, fy, fc) for fx in (0, 1) for fy in (0, 1) for fc in (0, 1) if (fx, fy, fc) != (0, 0, 0))


def _other(flip):
    x, y, c = _me()
    return (x ^ flip[0], y ^ flip[1], c ^ flip[2])


def _linear(dev):
    return 4 * dev[0] + 2 * dev[1] + dev[2]


def _exchange_small(small):
    def body(small_ref, out_ref, send_sems, recv_sems, local_sem):
        my_slot = _linear(_me())
        local = pltpu.make_async_copy(small_ref, out_ref.at[my_slot], local_sem)
        local.start()
        sends, recvs = [], []
        for k, flip in enumerate(OTHERS):
            peer = _other(flip)
            sends.append(pltpu.make_async_remote_copy(
                src_ref=small_ref, dst_ref=out_ref.at[my_slot], send_sem=send_sems.at[k], recv_sem=recv_sems.at[k],
                device_id=peer, device_id_type=MESH))
            recvs.append(pltpu.make_async_remote_copy(
                src_ref=small_ref, dst_ref=out_ref.at[_linear(peer)], send_sem=send_sems.at[k],
                recv_sem=recv_sems.at[k], device_id=peer, device_id_type=MESH))
        _run_copies(sends, sends, recvs)
        local.wait()

    return pl.pallas_call(
        body, name="exchange_small", in_specs=[ANY], out_specs=ANY,
        out_shape=jax.ShapeDtypeStruct((8,) + small.shape, small.dtype),
        scratch_shapes=[pltpu.SemaphoreType.DMA((7,)), pltpu.SemaphoreType.DMA((7,)), pltpu.SemaphoreType.DMA])(small)


def _pair_swap(halves):
    n = len(halves)

    def body(*refs):
        src, out = refs[:n], refs[n:2 * n]
        send_sems, recv_sems = refs[2 * n:]
        sib = _peer("c")
        copies = [pltpu.make_async_remote_copy(
            src_ref=src[i], dst_ref=out[i], send_sem=send_sems.at[i], recv_sem=recv_sems.at[i],
            device_id=sib, device_id_type=MESH) for i in range(n)]
        _run_copies(copies, copies, copies)

    return pl.pallas_call(
        body, name="pair_swap", in_specs=[ANY] * n, out_specs=[ANY] * n,
        out_shape=[jax.ShapeDtypeStruct(h.shape, h.dtype) for h in halves],
        scratch_shapes=[pltpu.SemaphoreType.DMA((n,)), pltpu.SemaphoreType.DMA((n,))])(*halves)


HBM = pl.BlockSpec(memory_space=pltpu.HBM)
SEM = pl.BlockSpec(memory_space=pltpu.SEMAPHORE)
EFFECT = pltpu.SideEffectType.DATAFLOW_SIDE_EFFECTING


def _hbm(a):
    return pltpu.with_memory_space_constraint(a, pltpu.HBM)


def _transfer_start(name, srcs, lands, n_copies, make_copies, after=None):
    n, m = len(srcs), len(lands)

    def body(*refs):
        src, land = refs[:n], refs[n:n + m]
        outs = refs[n + m + (after is not None):]
        send_sems, recv_sems, token = outs[0], outs[1], outs[-1]
        for cp in make_copies(src, land, send_sems, recv_sems):
            cp.start()
        token[...] = jnp.zeros_like(token)

    arrs = list(srcs) + list(lands)
    in_specs, args = _with_dep([HBM] * (n + m), [_hbm(a) for a in arrs], after)
    out = pl.pallas_call(
        body, name=name,
        out_shape=(pltpu.SemaphoreType.DMA((n_copies,)), pltpu.SemaphoreType.DMA((n_copies,)),
                   *[pltpu.HBM(a.shape, a.dtype) for a in arrs], jax.ShapeDtypeStruct((8, 128), F32)),
        in_specs=in_specs,
        out_specs=(SEM, SEM, *[HBM] * (n + m), pl.BlockSpec(memory_space=pltpu.VMEM)),
        input_output_aliases={i: 2 + i for i in range(n + m)},
        compiler_params=pltpu.CompilerParams(has_side_effects=EFFECT))(*args)
    return out[0], out[1], list(out[2:2 + n]), list(out[2 + n:2 + n + m]), out[-1]


def _transfer_wait(name, send_sems, recv_sems, srcs, lands, make_copies, after):
    n, m = len(srcs), len(lands)

    def body(*refs):
        src, land = refs[:n], refs[n:n + m]
        s_sems, r_sems = refs[n + m], refs[n + m + 1]
        for cp in make_copies(src, land, s_sems, r_sems):
            cp.wait_send()
            cp.wait_recv()

    arrs = list(srcs) + list(lands)
    out = pl.pallas_call(
        body, name=name, out_shape=tuple(pltpu.HBM(a.shape, a.dtype) for a in arrs),
        in_specs=[HBM] * (n + m) + [SEM, SEM, ANY], out_specs=tuple([HBM] * (n + m)),
        input_output_aliases={i: i for i in range(n + m)},
        compiler_params=pltpu.CompilerParams(has_side_effects=EFFECT))(*arrs, send_sems, recv_sems, after)
    return list(out[:n]), list(out[n:])


def _gather_copies(src, land, send_sems, recv_sems):
    me = _chip_of(_me())
    copies = []
    for i in range(len(src)):
        for r, rel in enumerate(CHIP_RELS):
            k = 3 * i + r
            copies.append(pltpu.make_async_remote_copy(
                src_ref=src[i], dst_ref=land[i].at[me], send_sem=send_sems.at[k], recv_sem=recv_sems.at[k],
                device_id=_peer(rel), device_id_type=MESH))
    return copies


def _small_copies(src, land, send_sems, recv_sems):
    my_slot = _linear(_me())
    return [pltpu.make_async_remote_copy(
        src_ref=src[0], dst_ref=land[0].at[my_slot], send_sem=send_sems.at[k], recv_sem=recv_sems.at[k],
        device_id=_other(flip), device_id_type=MESH) for k, flip in enumerate(OTHERS)]


def _pieces_copies(src, land, send_sems, recv_sems):
    copies = []
    for k, flip in enumerate(OTHERS):
        peer = _other(flip)
        copies.append(pltpu.make_async_remote_copy(
            src_ref=src[0].at[_linear(peer)], dst_ref=land[0].at[k], send_sem=send_sems.at[k],
            recv_sem=recv_sems.at[k], device_id=peer, device_id_type=MESH))
    return copies


def _row_block(rows, cols, budget=2 * 1024 * 1024):
    rb = max(8, (budget // (4 * cols)) // 8 * 8)
    while rows % rb:
        rb -= 8
    return rb if rb > 0 else rows


def _sum_slots(name, first, rest):
    R, Cc = first.shape
    K = rest.shape[0]
    rb = _row_block(R, Cc)

    def body(f_ref, r_ref, o_ref):
        acc = f_ref[...].astype(F32)
        for j in range(K):
            acc = acc + r_ref[j].astype(F32)
        o_ref[...] = acc

    return pl.pallas_call(
        body, name=name, grid=(R // rb,),
        in_specs=[pl.BlockSpec((rb, Cc), lambda i: (i, 0)), pl.BlockSpec((K, rb, Cc), lambda i: (0, i, 0))],
        out_specs=pl.BlockSpec((rb, Cc), lambda i: (i, 0)),
        out_shape=jax.ShapeDtypeStruct((R, Cc), F32), compiler_params=_cp("parallel"))(first, rest)


def _adamw_math(w, gv, m, v):
    mn = ADAM_B1 * m + (1.0 - ADAM_B1) * gv
    vn = ADAM_B2 * v + (1.0 - ADAM_B2) * (gv * gv)
    m_hat = mn / (1.0 - ADAM_B1 ** ADAM_STEP)
    v_hat = vn / (1.0 - ADAM_B2 ** ADAM_STEP)
    return -ADAM_LR * (m_hat / (jnp.sqrt(v_hat) + ADAM_EPS) + ADAM_WD * w), mn, vn


def _adamw_halves(name, w, mine, theirs, m, v, core):
    R, Cc = w.shape
    r2 = R // 2
    rb = _row_block(r2, Cc, 1024 * 1024)
    nb2 = r2 // rb

    def body(c_ref, w_ref, mine_ref, theirs_ref, m_ref, v_ref, g_ref, d_ref, mo_ref, vo_ref):
        is_mine = (pl.program_id(0) // nb2) == c_ref[0]
        gv = jnp.where(is_mine, mine_ref[...], theirs_ref[...])
        g_ref[...] = gv
        d_ref[...], mo_ref[...], vo_ref[...] = _adamw_math(w_ref[...], gv, m_ref[...], v_ref[...])

    blk = pl.BlockSpec((rb, Cc), lambda i, c: (i, 0))
    half = lambda own: pl.BlockSpec(
        (rb, Cc), lambda i, c: (jnp.clip(i - (c[0] if own else 1 - c[0]) * nb2, 0, nb2 - 1), 0))
    return pl.pallas_call(
        body, name=name,
        grid_spec=pltpu.PrefetchScalarGridSpec(
            num_scalar_prefetch=1, grid=(2 * nb2,), in_specs=[blk, half(True), half(False), blk, blk],
            out_specs=[blk] * 4),
        out_shape=[jax.ShapeDtypeStruct((R, Cc), F32)] * 4, compiler_params=_cp("parallel"))(core, w, mine, theirs, m, v)


def _adamw(name, w, g, m, v):
    R, Cc = w.shape
    rb = _row_block(R, Cc, 1024 * 1024)

    def body(w_ref, g_ref, m_ref, v_ref, d_ref, mo_ref, vo_ref):
        d_ref[...], mo_ref[...], vo_ref[...] = _adamw_math(w_ref[...], g_ref[...], m_ref[...], v_ref[...])

    blk = pl.BlockSpec((rb, Cc), lambda i: (i, 0))
    return pl.pallas_call(
        body, name=name, grid=(R // rb,), in_specs=[blk] * 4, out_specs=[blk] * 3,
        out_shape=[jax.ShapeDtypeStruct((R, Cc), F32)] * 3, compiler_params=_cp("parallel"))(w, g, m, v)


def _pack(arrs):
    rows = []
    for a in arrs:
        flat = a.reshape(-1)
        pad = (-flat.shape[0]) % 128
        rows.append(jnp.pad(flat, (0, pad)).reshape(-1, 128))
    buf = jnp.concatenate(rows, axis=0)
    return jnp.pad(buf, ((0, (-buf.shape[0]) % 8), (0, 0)))


def _unpack(buf, shapes):
    out, r = [], 0
    for s in shapes:
        n = math.prod(s)
        nr = -(-n // 128)
        out.append(buf[r:r + nr].reshape(-1)[:n].reshape(s))
        r += nr
    return out


BIG = ("w_in", "w_out", "w_up", "w_down")
CONV = ("dn_conv_w", "ffn_conv_w")
REPL = ("attn_norm_g", "dn_a_log", "dn_dt_bias", "dn_out_norm_g", "sg_norm_g", "sg_w", "sg_b",
        "ffn_norm_g", "ffn_conv_b", "final_norm_g")
ORDER = ("attn_norm_g", "w_in", "dn_conv_w", "dn_a_log", "dn_dt_bias", "dn_out_norm_g", "sg_norm_g", "sg_w",
         "sg_b", "w_out", "ffn_norm_g", "w_up", "ffn_conv_w", "ffn_conv_b", "w_down", "final_norm_g")


def kernel(x, attn_norm_g, w_in, dn_conv_w, dn_a_log, dn_dt_bias, dn_out_norm_g, sg_norm_g, sg_w, sg_b, w_out, ffn_norm_g, w_up, ffn_conv_w, ffn_conv_b, w_down, final_norm_g, loss_target, m_attn_norm_g, m_w_in, m_dn_conv_w, m_dn_a_log, m_dn_dt_bias, m_dn_out_norm_g, m_sg_norm_g, m_sg_w, m_sg_b, m_w_out, m_ffn_norm_g, m_w_up, m_ffn_conv_w, m_ffn_conv_b, m_w_down, m_final_norm_g, v_attn_norm_g, v_w_in, v_dn_conv_w, v_dn_a_log, v_dn_dt_bias, v_dn_out_norm_g, v_sg_norm_g, v_sg_w, v_sg_b, v_w_out, v_ffn_norm_g, v_w_up, v_ffn_conv_w, v_ffn_conv_b, v_w_down, v_final_norm_g):
    W = dict(attn_norm_g=attn_norm_g, w_in=w_in, dn_conv_w=dn_conv_w, dn_a_log=dn_a_log, dn_dt_bias=dn_dt_bias,
             dn_out_norm_g=dn_out_norm_g, sg_norm_g=sg_norm_g, sg_w=sg_w, sg_b=sg_b, w_out=w_out,
             ffn_norm_g=ffn_norm_g, w_up=w_up, ffn_conv_w=ffn_conv_w, ffn_conv_b=ffn_conv_b, w_down=w_down,
             final_norm_g=final_norm_g)
    Mo = dict(attn_norm_g=m_attn_norm_g, w_in=m_w_in, dn_conv_w=m_dn_conv_w, dn_a_log=m_dn_a_log,
              dn_dt_bias=m_dn_dt_bias, dn_out_norm_g=m_dn_out_norm_g, sg_norm_g=m_sg_norm_g, sg_w=m_sg_w,
              sg_b=m_sg_b, w_out=m_w_out, ffn_norm_g=m_ffn_norm_g, w_up=m_w_up, ffn_conv_w=m_ffn_conv_w,
              ffn_conv_b=m_ffn_conv_b, w_down=m_w_down, final_norm_g=m_final_norm_g)
    Vo = dict(attn_norm_g=v_attn_norm_g, w_in=v_w_in, dn_conv_w=v_dn_conv_w, dn_a_log=v_dn_a_log,
              dn_dt_bias=v_dn_dt_bias, dn_out_norm_g=v_dn_out_norm_g, sg_norm_g=v_sg_norm_g, sg_w=v_sg_w,
              sg_b=v_sg_b, w_out=v_w_out, ffn_norm_g=v_ffn_norm_g, w_up=v_w_up, ffn_conv_w=v_ffn_conv_w,
              ffn_conv_b=v_ffn_conv_b, w_down=v_w_down, final_norm_g=v_final_norm_g)
    xi, yi, ci = lax.axis_index("x"), lax.axis_index("y"), lax.axis_index("c")
    chip = 2 * xi + yi

    me_lin = 4 * xi + 2 * yi + ci

    g_in, g_dnc = _gather_shards([w_in[0].astype(BF16), dn_conv_w[0]])
    late = ("w_out", "w_up", "w_down", "ffn_conv_w")
    late_shards = [W[n][0].astype(BF16) for n in late[:3]] + [ffn_conv_w[0]]
    late_lands = [lax.dynamic_update_index_in_dim(lax.empty((4,) + s.shape, s.dtype), s, chip, 0) for s in late_shards]
    n_late = 3 * len(late_shards)
    ssem, rsem, late_src, late_lands, token = _transfer_start("gather_rest_start", late_shards, late_lands,
                                                              n_late, _gather_copies, after=g_in)

    def late_weights(after):
        _, (g_out, g_up, g_down, g_ffc) = _transfer_wait("gather_rest_wait", ssem, rsem, late_src, late_lands,
                                                         _gather_copies, after)
        return dict(w_out=g_out.reshape(D_MODEL, D_MODEL), w_up=g_up.transpose(1, 0, 2).reshape(D_MODEL, 2 * D_FF),
                    w_down=g_down.reshape(D_FF, D_MODEL), ffn_conv_w=g_ffc.transpose(1, 0, 2).reshape(3, 2 * D_FF))

    full = dict(
        w_in=jnp.pad(g_in.transpose(1, 0, 2).reshape(D_MODEL, PROJ_COLS), ((0, 0), (0, PROJ_PAD - PROJ_COLS))),
        dn_conv_w=g_dnc.transpose(1, 0, 2).reshape(4, 3 * DN_WIDTH),
        attn_norm_g=attn_norm_g, dn_a_log=dn_a_log, dn_dt_bias=dn_dt_bias, dn_out_norm_g=dn_out_norm_g,
        sg_norm_g=sg_norm_g, sg_w=sg_w[0], sg_b=sg_b[0], ffn_norm_g=ffn_norm_g, ffn_conv_b=ffn_conv_b,
        final_norm_g=final_norm_g[None])

    pending = {}
    early_names = ("dn_out_norm_g", "sg_norm_g", "sg_w", "sg_b", "ffn_norm_g", "ffn_conv_w", "ffn_conv_b",
                   "final_norm_g")
    late_names = ("attn_norm_g", "dn_a_log", "dn_dt_bias", "dn_conv_w")

    def on_grad(name, gw):
        if name == "small_early":
            buf = _pack([gw[n] for n in early_names])
            land = lax.dynamic_update_index_in_dim(lax.empty((8,) + buf.shape, F32), buf, me_lin, 0)
            s_sem, r_sem, src, lands, tok = _transfer_start("small_early_start", [buf], [land], 7, _small_copies)
            pending[name] = (s_sem, r_sem, src, lands)
            return tok
        if name in ("w_in", "w_up"):
            g8 = gw.astype(BF16).reshape(D_MODEL, 4, -1).transpose(1, 0, 2)
        else:
            g8 = gw.astype(BF16)
        g8 = g8.reshape(8, -1, g8.shape[-1])
        land = lax.empty((7,) + g8.shape[1:], BF16)
        s_sem, r_sem, src, lands, tok = _transfer_start(f"reduce_{name}_start", [g8], [land], 7, _pieces_copies)
        pending[name] = (s_sem, r_sem, src, lands)
        return tok

    loss_row, grad_x, g = _local_step(x[0], loss_target[0], full, dep=token, late_weights=late_weights,
                                      on_grad=on_grad)
    loss = lax.psum(loss_row[0, 0], ("x", "y", "c"))

    small_names = REPL + CONV
    late_all = _exchange_small(_pack([g[n] for n in late_names]))
    late_sum = _sum_slots("sum_small_late", late_all[0], late_all[1:])
    s_sem, r_sem, src, lands = pending["small_early"]
    _, (early_all,) = _transfer_wait("small_early_wait", s_sem, r_sem, src, lands, _small_copies, grad_x)
    early_sum = _sum_slots("sum_small_early", early_all[0], early_all[1:])
    sg = dict(zip(late_names, _unpack(late_sum, [g[n].shape for n in late_names])))
    sg.update(zip(early_names, _unpack(early_sum, [g[n].shape for n in early_names])))
    sg["dn_conv_w"] = lax.dynamic_slice_in_dim(sg["dn_conv_w"], chip * (3 * DN_WIDTH // 4), 3 * DN_WIDTH // 4, axis=1)
    sg["ffn_conv_w"] = lax.dynamic_slice_in_dim(sg["ffn_conv_w"], chip * (2 * D_FF // 4), 2 * D_FF // 4, axis=1)

    halves = []
    for n in ("w_down", "w_up", "w_out", "w_in"):
        s_sem, r_sem, src, lands = pending[n]
        sent, got = _transfer_wait(f"reduce_{n}_wait", s_sem, r_sem, src, lands, _pieces_copies, grad_x)
        own = lax.dynamic_index_in_dim(sent[0], me_lin, axis=0, keepdims=False)
        halves.append(_sum_slots(f"sum_{n}", own, got[0]))
    theirs = _pair_swap(halves)
    core = ci.astype(jnp.int32).reshape(1)
    grads, delta, new_m, new_v = {}, {}, {}, {}
    for n, mine_h, their_h in zip(("w_down", "w_up", "w_out", "w_in"), halves, theirs):
        shp = W[n].shape
        gr, d, mn, vn = _adamw_halves(f"adamw_{n}", W[n][0], mine_h, their_h, Mo[n][0], Vo[n][0], core)
        grads[n], delta[n], new_m[n], new_v[n] = gr.reshape(shp), d.reshape(shp), mn.reshape(shp), vn.reshape(shp)
    shapes = [W[n].shape for n in small_names]
    for n in small_names:
        grads[n] = sg[n].reshape(W[n].shape)
    d, mn, vn = _adamw("adamw_small", _pack([W[n] for n in small_names]), _pack([grads[n] for n in small_names]),
                       _pack([Mo[n] for n in small_names]), _pack([Vo[n] for n in small_names]))
    for dst, buf in ((delta, d), (new_m, mn), (new_v, vn)):
        dst.update(zip(small_names, _unpack(buf, shapes)))

    return (loss, grad_x[None], *[grads[n] for n in ORDER], *[delta[n] for n in ORDER],
            *[new_m[n] for n in ORDER], *[new_v[n] for n in ORDER])
```

```python
import functools
import math

import jax
import jax.numpy as jnp
from jax import lax
from jax.experimental import pallas as pl
from jax.experimental.pallas import tpu as pltpu

F32 = jnp.float32
BF16 = jnp.bfloat16

D_MODEL = 1024
CHUNK = 64
HEAD_DIM = 128
N_HEADS = 4
DN_WIDTH = 512
SG_WIDTH = 512
SG_GROUPS = 4
SG_BLOCK = 128
D_FF = 2816
PROJ_COLS = 3080
PROJ_PAD = 3200
BA_COL = 3072
EPS = 1e-6
NEG = -1e30
VMEM_LIMIT = 56 * 1024 * 1024

ADAM_LR = 0.001
ADAM_B1 = 0.9
ADAM_B2 = 0.999
ADAM_EPS = 1e-08
ADAM_WD = 0.01
ADAM_STEP = 10

MESH = pl.DeviceIdType.MESH
ANY = pl.BlockSpec(memory_space=pl.ANY)


def _cp(*sem):
    return pltpu.CompilerParams(dimension_semantics=sem, vmem_limit_bytes=VMEM_LIMIT)


def _bf(a):
    return a.astype(BF16)


def _nn(a, b):
    return jnp.dot(_bf(a), _bf(b), preferred_element_type=F32)


def _nt(a, b):
    return lax.dot_general(_bf(a), _bf(b), (((1,), (1,)), ((), ())), preferred_element_type=F32)


def _tn(a, b):
    return lax.dot_general(_bf(a), _bf(b), (((0,), (0,)), ((), ())), preferred_element_type=F32)


def _split(a):
    hi = _bf(a)
    return hi, _bf(a - hi.astype(F32))


def _sigmoid(x):
    return 0.5 * jnp.tanh(0.5 * x) + 0.5


def _silu(x):
    return x * _sigmoid(x)


def _dsilu(x):
    s = _sigmoid(x)
    return s * (1.0 + x * (1.0 - s))


_GELU_C = math.sqrt(2.0 / math.pi)
_GELU_A = 0.044715


def _gelu(x):
    return 0.5 * x * (1.0 + jnp.tanh(_GELU_C * (x + _GELU_A * x * x * x)))


def _dgelu(x):
    t = jnp.tanh(_GELU_C * (x + _GELU_A * x * x * x))
    return 0.5 * (1.0 + t) + 0.5 * x * (1.0 - t * t) * _GELU_C * (1.0 + 3.0 * _GELU_A * x * x)


def _softplus(x):
    return jnp.maximum(x, 0.0) + jnp.log(1.0 + jnp.exp(-jnp.abs(x)))


def _mm_nn(name, a, b, out_dtype, tm, tn, res=None):
    M, K = a.shape
    N = b.shape[1]
    tm, tn = min(tm, M), min(tn, N)

    def body(*refs):
        a_ref, b_ref = refs[0], refs[1]
        o_ref = refs[-1]
        acc = _nn(a_ref[...], b_ref[...])
        if res is not None:
            acc = acc + refs[2][...]
        o_ref[...] = acc.astype(o_ref.dtype)

    in_specs = [pl.BlockSpec((tm, K), lambda j, i: (i, 0)), pl.BlockSpec((K, tn), lambda j, i: (0, j))]
    args = [a, b]
    if res is not None:
        in_specs.append(pl.BlockSpec((tm, tn), lambda j, i: (i, j)))
        args.append(res)
    return pl.pallas_call(
        body, name=name, grid=(N // tn, M // tm), in_specs=in_specs,
        out_specs=pl.BlockSpec((tm, tn), lambda j, i: (i, j)),
        out_shape=jax.ShapeDtypeStruct((M, N), out_dtype),
        compiler_params=_cp("parallel", "parallel"))(*args)


def _with_dep(in_specs, args, dep):
    if dep is None:
        return in_specs, args
    return in_specs + [ANY], args + [dep]


def _mm_nt(name, a, b, out_dtype, tm, tn, dep=None):
    M, K = a.shape
    N = b.shape[0]
    tm, tn = min(tm, M), min(tn, N)

    def body(a_ref, b_ref, *rest):
        o_ref = rest[-1]
        o_ref[...] = _nt(a_ref[...], b_ref[...]).astype(o_ref.dtype)

    in_specs, args = _with_dep(
        [pl.BlockSpec((tm, K), lambda i, j: (i, 0)), pl.BlockSpec((tn, K), lambda i, j: (j, 0))], [a, b], dep)
    return pl.pallas_call(
        body, name=name, grid=(M // tm, N // tn), in_specs=in_specs,
        out_specs=pl.BlockSpec((tm, tn), lambda i, j: (i, j)),
        out_shape=jax.ShapeDtypeStruct((M, N), out_dtype),
        compiler_params=_cp("parallel", "parallel"))(*args)


def _mm_tn(name, a, b, tm, tn, tk, col_major_tiles=False):
    T, M = a.shape
    N = b.shape[1]
    tm, tn, tk = min(tm, M), min(tn, N), min(tk, T)
    nk = T // tk

    def body(a_ref, b_ref, o_ref, acc_ref):
        k = pl.program_id(2)

        @pl.when(k == 0)
        def _():
            acc_ref[...] = jnp.zeros_like(acc_ref)
        acc_ref[...] += _tn(a_ref[...], b_ref[...])

        @pl.when(k == nk - 1)
        def _():
            o_ref[...] = acc_ref[...].astype(BF16).reshape(o_ref.shape)

    if col_major_tiles:
        assert tm == M
        out_spec = pl.BlockSpec((1, tm, tn), lambda i, j, k: (j, 0, 0))
        out_shape = jax.ShapeDtypeStruct((N // tn, M, tn), BF16)
    else:
        out_spec = pl.BlockSpec((tm, tn), lambda i, j, k: (i, j))
        out_shape = jax.ShapeDtypeStruct((M, N), BF16)
    return pl.pallas_call(
        body, name=name, grid=(M // tm, N // tn, nk),
        in_specs=[pl.BlockSpec((tk, tm), lambda i, j, k: (k, i)), pl.BlockSpec((tk, tn), lambda i, j, k: (k, j))],
        out_specs=out_spec, out_shape=out_shape, scratch_shapes=[pltpu.VMEM((tm, tn), F32)],
        compiler_params=_cp("parallel", "parallel", "arbitrary"))(a, b)


def _rms_fwd(name, x, g, rb=512, dep=None):
    T, Dm = x.shape
    rb = min(rb, T)

    def body(x_ref, g_ref, *rest):
        h_ref = rest[-1]
        xv = x_ref[...]
        r = lax.rsqrt(jnp.mean(xv * xv, axis=-1, keepdims=True) + EPS)
        h_ref[...] = (xv * r * g_ref[...]).astype(BF16)

    in_specs, args = _with_dep(
        [pl.BlockSpec((rb, Dm), lambda i: (i, 0)), pl.BlockSpec((1, Dm), lambda i: (0, 0))], [x, g], dep)
    return pl.pallas_call(
        body, name=name, grid=(T // rb,), in_specs=in_specs,
        out_specs=pl.BlockSpec((rb, Dm), lambda i: (i, 0)),
        out_shape=jax.ShapeDtypeStruct((T, Dm), BF16), compiler_params=_cp("parallel"))(*args)


def _rms_bwd(name, dh, x, g, dres, rb=512):
    T, Dm = x.shape
    rb = min(rb, T)

    def body(dh_ref, x_ref, g_ref, dres_ref, dx_ref, gg_ref):
        @pl.when(pl.program_id(0) == 0)
        def _():
            gg_ref[...] = jnp.zeros_like(gg_ref)
        xv = x_ref[...]
        r = lax.rsqrt(jnp.mean(xv * xv, axis=-1, keepdims=True) + EPS)
        xh = xv * r
        dhv = dh_ref[...]
        gg_ref[...] += jnp.sum(dhv * xh, axis=0, keepdims=True)
        dxh = dhv * g_ref[...]
        dx_ref[...] = dres_ref[...] + r * (dxh - xh * jnp.mean(dxh * xh, axis=-1, keepdims=True))

    row = pl.BlockSpec((rb, Dm), lambda i: (i, 0))
    vec = pl.BlockSpec((1, Dm), lambda i: (0, 0))
    return pl.pallas_call(
        body, name=name, grid=(T // rb,), in_specs=[row, row, vec, row], out_specs=[row, vec],
        out_shape=[jax.ShapeDtypeStruct((T, Dm), F32), jax.ShapeDtypeStruct((1, Dm), F32)],
        compiler_params=_cp("arbitrary"))(dh, x, g, dres)


def _loss_head(x3, tgt, g, rb=512):
    T, Dm = x3.shape
    rb = min(rb, T)

    def body(x_ref, t_ref, g_ref, loss_ref, dx_ref, gg_ref):
        @pl.when(pl.program_id(0) == 0)
        def _():
            gg_ref[...] = jnp.zeros_like(gg_ref)
            loss_ref[...] = jnp.zeros_like(loss_ref)
        xv = x_ref[...]
        r = lax.rsqrt(jnp.mean(xv * xv, axis=-1, keepdims=True) + EPS)
        xh = xv * r
        e = xh * g_ref[...] - t_ref[...]
        loss_ref[...] += jnp.zeros_like(loss_ref) + (0.5 / Dm) * jnp.sum(e * e)
        dy = e * (1.0 / Dm)
        gg_ref[...] += jnp.sum(dy * xh, axis=0, keepdims=True)
        dxh = dy * g_ref[...]
        dx_ref[...] = r * (dxh - xh * jnp.mean(dxh * xh, axis=-1, keepdims=True))

    row = pl.BlockSpec((rb, Dm), lambda i: (i, 0))
    vec = pl.BlockSpec((1, Dm), lambda i: (0, 0))
    return pl.pallas_call(
        body, name="loss_head", grid=(T // rb,), in_specs=[row, row, vec],
        out_specs=[pl.BlockSpec((1, 128), lambda i: (0, 0)), row, vec],
        out_shape=[jax.ShapeDtypeStruct((1, 128), F32), jax.ShapeDtypeStruct((T, Dm), F32),
                   jax.ShapeDtypeStruct((1, Dm), F32)],
        compiler_params=_cp("arbitrary"))(x3, tgt, g)


def _halo_prev_spec(rb, width):
    return pl.BlockSpec((8, width), lambda i: (jnp.maximum(i * (rb // 8) - 1, 0), 0))


def _halo_next_spec(rb, width, T):
    return pl.BlockSpec((8, width), lambda i: (jnp.minimum((i + 1) * (rb // 8), T // 8 - 1), 0))


LANES = 128
FF_STRIPS = D_FF // LANES


def _strip(j, base=0):
    return pl.ds(pl.multiple_of(base + j * LANES, LANES), LANES)


def _ffn_act(up, w, b, rb=256):
    T, W = up.shape
    rb = min(rb, T)

    def body(up_ref, halo_ref, w_ref, b_ref, act_ref, ext_scr):
        first = pl.program_id(0) == 0

        def strip(j, carry):
            c = []
            for h, cols in enumerate((_strip(j), _strip(j, D_FF))):
                cur = up_ref[:, cols]
                ext_scr[h, 0:8] = jnp.where(first, 0.0, halo_ref[:, cols])
                ext_scr[h, 8:] = cur
                wv = w_ref[:, cols]
                c.append(ext_scr[h, 6:6 + rb] * wv[0:1] + ext_scr[h, 7:7 + rb] * wv[1:2] + cur * wv[2:3]
                         + b_ref[:, cols])
            act_ref[:, _strip(j)] = (_silu(c[0]) * c[1]).astype(BF16)
            return carry

        lax.fori_loop(0, FF_STRIPS, strip, 0)

    return pl.pallas_call(
        body, name="ffn_act", grid=(T // rb,),
        in_specs=[pl.BlockSpec((rb, W), lambda i: (i, 0)), _halo_prev_spec(rb, W),
                  pl.BlockSpec((3, W), lambda i: (0, 0)), pl.BlockSpec((1, W), lambda i: (0, 0))],
        out_specs=pl.BlockSpec((rb, D_FF), lambda i: (i, 0)),
        out_shape=jax.ShapeDtypeStruct((T, D_FF), BF16),
        scratch_shapes=[pltpu.VMEM((2, rb + 8, LANES), F32)], compiler_params=_cp("parallel"))(up, up, w, b)


def _ffn_act_bwd(up, dact, w, b, rb=128, dep=None):
    T, W = up.shape
    rb = min(rb, T)
    nb = T // rb
    re = rb + 8

    def body(up_ref, prev_ref, next_ref, da_ref, danext_ref, w_ref, b_ref, *rest):
        dup_ref, gw_ref, gb_ref, ext_scr, dc_scr = rest[-5:]
        i = pl.program_id(0)

        @pl.when(i == 0)
        def _():
            gw_ref[...] = jnp.zeros_like(gw_ref)
            gb_ref[...] = jnp.zeros_like(gb_ref)
        last = i == nb - 1
        row = lax.broadcasted_iota(jnp.int32, (re, 1), 0)
        live = (row < rb) | jnp.logical_not(last)

        def strip(j, carry):
            halves = (_strip(j), _strip(j, D_FF))
            taps, c = [], []
            for h, cols in enumerate(halves):
                ext_scr[h, 0:8] = jnp.where(i > 0, prev_ref[:, cols], 0.0)
                ext_scr[h, 8:8 + rb] = up_ref[:, cols]
                ext_scr[h, 8 + rb:] = next_ref[:, cols]
                tp = [ext_scr[h, 6 + k:6 + k + re] for k in range(3)]
                wv = w_ref[:, cols]
                c.append(tp[0] * wv[0:1] + tp[1] * wv[1:2] + tp[2] * wv[2:3] + b_ref[:, cols])
                taps.append(tp)
            da = jnp.where(live, jnp.concatenate([da_ref[:, halves[0]], danext_ref[:, halves[0]]], axis=0), 0.0)
            s = _sigmoid(c[0])
            gs = c[0] * s
            dcs = (da * c[1] * (s + gs * (1.0 - s)), da * gs)
            for h, (cols, tp, dc) in enumerate(zip(halves, taps, dcs)):
                wv = w_ref[:, cols]
                dc_scr[h] = dc
                dcc = dc[0:rb]
                dup = dcc * wv[2:3] + dc_scr[h, 1:1 + rb] * wv[1:2] + dc_scr[h, 2:2 + rb] * wv[0:1]
                dup_ref[:, cols] = dup.astype(BF16)
                gb_ref[:, cols] += jnp.sum(dcc, axis=0, keepdims=True)
                for k in range(3):
                    gw_ref[k:k + 1, cols] += jnp.sum(tp[k][0:rb] * dcc, axis=0, keepdims=True)
            return carry

        lax.fori_loop(0, FF_STRIPS, strip, 0)

    in_specs, args = _with_dep(
        [pl.BlockSpec((rb, W), lambda i: (i, 0)), _halo_prev_spec(rb, W), _halo_next_spec(rb, W, T),
         pl.BlockSpec((rb, D_FF), lambda i: (i, 0)), _halo_next_spec(rb, D_FF, T),
         pl.BlockSpec((3, W), lambda i: (0, 0)), pl.BlockSpec((1, W), lambda i: (0, 0))],
        [up, up, up, dact, dact, w, b], dep)
    return pl.pallas_call(
        body, name="ffn_act_bwd", grid=(nb,), in_specs=in_specs,
        out_specs=[pl.BlockSpec((rb, W), lambda i: (i, 0)), pl.BlockSpec((3, W), lambda i: (0, 0)),
                   pl.BlockSpec((1, W), lambda i: (0, 0))],
        out_shape=[jax.ShapeDtypeStruct((T, W), BF16), jax.ShapeDtypeStruct((3, W), F32),
                   jax.ShapeDtypeStruct((1, W), F32)],
        scratch_shapes=[pltpu.VMEM((2, rb + 16, LANES), F32), pltpu.VMEM((2, re, LANES), F32)],
        compiler_params=_cp("arbitrary"))(*args)


def _lane_iota(shape):
    return lax.broadcasted_iota(jnp.int32, shape, len(shape) - 1)


def _dn_act(p, conv_w, alog_row, dtb_row, rb=256):
    T = p.shape[0]
    rb = min(rb, T)
    W3 = 3 * DN_WIDTH

    def body(p_ref, halo_ref, ba_ref, w_ref, al_ref, dt_ref, q_ref, k_ref, v_ref, bg_ref, ext_scr):
        first = pl.program_id(0) == 0
        outs = (q_ref, k_ref, v_ref)
        for j in range(3 * N_HEADS):
            kind, h = divmod(j, N_HEADS)
            cols = slice(j * HEAD_DIM, (j + 1) * HEAD_DIM)
            cur = p_ref[:, cols]
            ext_scr[j, 0:8] = jnp.where(first, 0.0, halo_ref[:, cols])
            ext_scr[j, 8:] = cur
            wv = w_ref[:, cols]
            s = _silu(ext_scr[j, 5:5 + rb] * wv[0:1] + ext_scr[j, 6:6 + rb] * wv[1:2]
                      + ext_scr[j, 7:7 + rb] * wv[2:3] + cur * wv[3:4])
            if kind < 2:
                scale = HEAD_DIM ** -0.5 if kind == 0 else 1.0
                s = s * (lax.rsqrt(jnp.sum(s * s, axis=-1, keepdims=True) + EPS) * scale)
            outs[kind][:, h * HEAD_DIM:(h + 1) * HEAD_DIM] = s
        ba = ba_ref[...]
        lane = _lane_iota(ba.shape)
        beta = _sigmoid(ba)
        g = -jnp.exp(al_ref[...]) * _softplus(ba + dt_ref[...])
        bg_ref[...] = jnp.where(lane < N_HEADS, beta, jnp.where(lane < 2 * N_HEADS, g, 0.0))

    row512 = pl.BlockSpec((rb, DN_WIDTH), lambda i: (i, 0))
    row128 = pl.BlockSpec((rb, 128), lambda i: (i, 0))
    vec128 = pl.BlockSpec((1, 128), lambda i: (0, 0))
    return pl.pallas_call(
        body, name="dn_act", grid=(T // rb,),
        in_specs=[pl.BlockSpec((rb, W3), lambda i: (i, 0)), _halo_prev_spec(rb, W3),
                  pl.BlockSpec((rb, 128), lambda i: (i, BA_COL // 128)),
                  pl.BlockSpec((4, W3), lambda i: (0, 0)), vec128, vec128],
        out_specs=[row512, row512, row512, row128],
        out_shape=[jax.ShapeDtypeStruct((T, DN_WIDTH), F32)] * 3 + [jax.ShapeDtypeStruct((T, 128), F32)],
        scratch_shapes=[pltpu.VMEM((3 * N_HEADS, rb + 8, HEAD_DIM), F32)],
        compiler_params=_cp("parallel"))(p, p, p, conv_w, alog_row, dtb_row)


def _dn_act_bwd(p, conv_w, alog_row, dtb_row, dq, dk, dv, dbg, rb=256):
    T = p.shape[0]
    rb = min(rb, T)
    nb = T // rb
    re = rb + 8
    W3 = 3 * DN_WIDTH

    def body(p_ref, prev_ref, next_ref, ba_ref, w_ref, al_ref, dt_ref, dq_ref, dqn_ref, dk_ref, dkn_ref,
             dv_ref, dvn_ref, dbg_ref, draw_ref, dba_ref, gw_ref, gad_ref, ext_scr, dc_scr):
        i = pl.program_id(0)

        @pl.when(i == 0)
        def _():
            gw_ref[...] = jnp.zeros_like(gw_ref)
            gad_ref[...] = jnp.zeros_like(gad_ref)
        row = lax.broadcasted_iota(jnp.int32, (re, 1), 0)
        live = (row < rb) | (i < nb - 1)
        d_refs = ((dq_ref, dqn_ref), (dk_ref, dkn_ref), (dv_ref, dvn_ref))
        for j in range(3 * N_HEADS):
            kind, h = divmod(j, N_HEADS)
            cols = slice(j * HEAD_DIM, (j + 1) * HEAD_DIM)
            hcols = slice(h * HEAD_DIM, (h + 1) * HEAD_DIM)
            ext_scr[j, 0:8] = jnp.where(i > 0, prev_ref[:, cols], 0.0)
            ext_scr[j, 8:8 + rb] = p_ref[:, cols]
            ext_scr[j, 8 + rb:] = next_ref[:, cols]
            tp = [ext_scr[j, 5 + k:5 + k + re] for k in range(4)]
            wv = w_ref[:, cols]
            c = tp[0] * wv[0:1] + tp[1] * wv[1:2] + tp[2] * wv[2:3] + tp[3] * wv[3:4]
            sg = _sigmoid(c)
            s = c * sg
            d_in = jnp.where(live, jnp.concatenate([d_refs[kind][0][:, hcols], d_refs[kind][1][:, hcols]], axis=0), 0.0)
            if kind < 2:
                scale = HEAD_DIM ** -0.5 if kind == 0 else 1.0
                n = lax.rsqrt(jnp.sum(s * s, axis=-1, keepdims=True) + EPS)
                hat = s * n
                d_in = (n * scale) * (d_in - hat * jnp.sum(hat * d_in, axis=-1, keepdims=True))
            dc = d_in * (sg + s * (1.0 - sg))
            dc_scr[j] = dc
            dcc = dc[0:rb]
            draw = (dcc * wv[3:4] + dc_scr[j, 1:1 + rb] * wv[2:3] + dc_scr[j, 2:2 + rb] * wv[1:2]
                    + dc_scr[j, 3:3 + rb] * wv[0:1])
            draw_ref[:, cols] = draw.astype(BF16)
            for k in range(4):
                gw_ref[k:k + 1, cols] += jnp.sum(tp[k][0:rb] * dcc, axis=0, keepdims=True)
        ba = ba_ref[...]
        dbg = dbg_ref[...]
        lane = _lane_iota(ba.shape)
        beta = _sigmoid(ba)
        ea = jnp.exp(al_ref[...])
        z = ba + dt_ref[...]
        d_a = dbg * (-ea) * _sigmoid(z)
        dba = jnp.where(lane < N_HEADS, dbg * beta * (1.0 - beta), jnp.where(lane < 2 * N_HEADS, d_a, 0.0))
        dba_ref[...] = dba.astype(BF16)
        isg = (lane >= N_HEADS) & (lane < 2 * N_HEADS)
        g = -ea * _softplus(z)
        gad_ref[0:1, :] += jnp.sum(jnp.where(isg, dbg * g, 0.0), axis=0, keepdims=True)
        gad_ref[1:2, :] += jnp.sum(jnp.where(isg, d_a, 0.0), axis=0, keepdims=True)

    row512 = pl.BlockSpec((rb, DN_WIDTH), lambda i: (i, 0))
    row128 = pl.BlockSpec((rb, 128), lambda i: (i, 0))
    vec128 = pl.BlockSpec((1, 128), lambda i: (0, 0))
    next512 = _halo_next_spec(rb, DN_WIDTH, T)
    return pl.pallas_call(
        body, name="dn_act_bwd", grid=(nb,),
        in_specs=[pl.BlockSpec((rb, W3), lambda i: (i, 0)), _halo_prev_spec(rb, W3), _halo_next_spec(rb, W3, T),
                  pl.BlockSpec((rb, 128), lambda i: (i, BA_COL // 128)),
                  pl.BlockSpec((4, W3), lambda i: (0, 0)), vec128, vec128,
                  row512, next512, row512, next512, row512, next512, row128],
        out_specs=[pl.BlockSpec((rb, W3), lambda i: (i, 0)), row128,
                   pl.BlockSpec((4, W3), lambda i: (0, 0)), pl.BlockSpec((2, 128), lambda i: (0, 0))],
        out_shape=[jax.ShapeDtypeStruct((T, W3), BF16), jax.ShapeDtypeStruct((T, 128), BF16),
                   jax.ShapeDtypeStruct((4, W3), F32), jax.ShapeDtypeStruct((2, 128), F32)],
        scratch_shapes=[pltpu.VMEM((3 * N_HEADS, rb + 16, HEAD_DIM), F32), pltpu.VMEM((3 * N_HEADS, re, HEAD_DIM), F32)],
        compiler_params=_cp("arbitrary"))(p, p, p, p, conv_w, alog_row, dtb_row, dq, dq, dk, dk, dv, dv, dbg)


def _tri(incl):
    ii = lax.broadcasted_iota(jnp.int32, (CHUNK, CHUNK), 0)
    jj = lax.broadcasted_iota(jnp.int32, (CHUNK, CHUNK), 1)
    return ii, jj, ((ii >= jj) if incl else (ii > jj))


def _dn_chunk(k, bg, cb=4):
    T = k.shape[0]
    N = T // CHUNK
    cb = min(cb, N)

    def body(k_ref, bg_ref, gc_ref, gct_ref, l_ref):
        ii, jj, incl = _tri(True)
        tri = incl.astype(F32)
        U = range(cb)
        bgv = [bg_ref[u * CHUNK:(u + 1) * CHUNK, :] for u in U]
        gc = [jnp.dot(tri, bgv[u], precision=lax.Precision.HIGHEST, preferred_element_type=F32) for u in U]
        gct = [gc[u].T for u in U]
        kk = [[None] * N_HEADS for _ in U]
        for u in U:
            gc_ref[u * CHUNK:(u + 1) * CHUNK, :] = gc[u]
            gct_ref[u] = gct[u][0:8]
            for h in range(N_HEADS):
                kh = k_ref[u * CHUNK:(u + 1) * CHUNK, h * HEAD_DIM:(h + 1) * HEAD_DIM]
                kk[u][h] = _nt(kh * bgv[u][:, h:h + 1], kh)
        for u in U:
            for h in range(N_HEADS):
                gcol = gc[u][:, N_HEADS + h:N_HEADS + h + 1]
                grow = gct[u][N_HEADS + h:N_HEADS + h + 1, :]
                l_ref[u, h] = kk[u][h] * jnp.exp(jnp.where(ii > jj, gcol - grow, NEG))

    rows = cb * CHUNK
    return pl.pallas_call(
        body, name="dn_chunk", grid=(N // cb,),
        in_specs=[pl.BlockSpec((rows, DN_WIDTH), lambda n: (n, 0)), pl.BlockSpec((rows, 128), lambda n: (n, 0))],
        out_specs=[pl.BlockSpec((rows, 128), lambda n: (n, 0)), pl.BlockSpec((cb, 8, CHUNK), lambda n: (n, 0, 0)),
                   pl.BlockSpec((cb, N_HEADS, CHUNK, CHUNK), lambda n: (n, 0, 0, 0))],
        out_shape=[jax.ShapeDtypeStruct((T, 128), F32), jax.ShapeDtypeStruct((N, 8, CHUNK), F32),
                   jax.ShapeDtypeStruct((N, N_HEADS, CHUNK, CHUNK), F32)],
        compiler_params=_cp("parallel"))(k, bg)


def _tri_inv(lt):
    S = lt.shape[1]

    def body(l_ref, a_ref):
        col = lax.broadcasted_iota(jnp.int32, (CHUNK, S), 0)
        for i in range(CHUNK):
            def step(j, acc):
                return acc - l_ref[pl.ds(i * CHUNK + j, 1), :] * a_ref[j]
            a_ref[i] = lax.fori_loop(0, i, step, (col == i).astype(F32))

    return pl.pallas_call(
        body, name="tri_inv", out_shape=jax.ShapeDtypeStruct((CHUNK, CHUNK, S), F32),
        compiler_params=pltpu.CompilerParams(vmem_limit_bytes=VMEM_LIMIT))(lt)


def _dn_head_terms(qh, kh, vh, beta, gcol, grow):
    ii, jj, incl = _tri(True)
    gam = jnp.exp(jnp.where(incl, gcol - grow, NEG))
    glast = grow[:, CHUNK - 1:CHUNK]
    E = jnp.exp(gcol)
    Fd = jnp.exp(glast - gcol)
    cd = jnp.exp(glast)
    kb = kh * beta
    return dict(ii=ii, jj=jj, gam=gam, E=E, F=Fd, cd=cd, kb=kb, vb=vh * beta, W=kb * E, qE=qh * E, kt=kh * Fd)


def _apply_a(a, u):
    hi, lo = _split(a)
    ub = _bf(u)
    return jnp.dot(hi, ub, preferred_element_type=F32) + jnp.dot(lo, ub, preferred_element_type=F32)


def _apply_at(a, u):
    hi, lo = _split(a)
    ub = _bf(u)
    dn = (((0,), (0,)), ((), ()))
    return (lax.dot_general(hi, ub, dn, preferred_element_type=F32)
            + lax.dot_general(lo, ub, dn, preferred_element_type=F32))


def _dn_scan(q, k, v, bg, gc, gct, a):
    T = q.shape[0]
    N = T // CHUNK

    def body(q_ref, k_ref, v_ref, bg_ref, gc_ref, gct_ref, a_ref, o_ref, sall_ref, s_ref):
        @pl.when(pl.program_id(0) == 0)
        def _():
            s_ref[...] = jnp.zeros_like(s_ref)
        bgv, gcv, gctv = bg_ref[...], gc_ref[...], gct_ref[0]
        H = range(N_HEADS)
        sl = [slice(h * HEAD_DIM, (h + 1) * HEAD_DIM) for h in H]
        q_, k_ = [q_ref[:, s] for s in sl], [k_ref[:, s] for s in sl]
        t = [_dn_head_terms(q_[h], k_[h], v_ref[:, sl[h]], bgv[:, h:h + 1],
                            gcv[:, N_HEADS + h:N_HEADS + h + 1], gctv[N_HEADS + h:N_HEADS + h + 1, :]) for h in H]
        S = [s_ref[h] for h in H]
        for h in H:
            sall_ref[0, h] = S[h]
        WS = [_nn(t[h]["W"], S[h]) for h in H]
        QK = [_nt(q_[h], k_[h]) for h in H]
        qS = [_nn(t[h]["qE"], S[h]) for h in H]
        vn = [_apply_a(a_ref[0, h], t[h]["vb"] - WS[h]) for h in H]
        Pv = [_nn(QK[h] * t[h]["gam"], vn[h]) for h in H]
        kv = [_tn(t[h]["kt"], vn[h]) for h in H]
        for h in H:
            o_ref[:, sl[h]] = qS[h] + Pv[h]
            s_ref[h] = t[h]["cd"] * S[h] + kv[h]

    row512 = pl.BlockSpec((CHUNK, DN_WIDTH), lambda n: (n, 0))
    row128 = pl.BlockSpec((CHUNK, 128), lambda n: (n, 0))
    return pl.pallas_call(
        body, name="dn_scan", grid=(N,),
        in_specs=[row512, row512, row512, row128, row128, pl.BlockSpec((1, 8, CHUNK), lambda n: (n, 0, 0)),
                  pl.BlockSpec((1, N_HEADS, CHUNK, CHUNK), lambda n: (n, 0, 0, 0))],
        out_specs=[row512, pl.BlockSpec((1, N_HEADS, HEAD_DIM, HEAD_DIM), lambda n: (n, 0, 0, 0))],
        out_shape=[jax.ShapeDtypeStruct((T, DN_WIDTH), F32),
                   jax.ShapeDtypeStruct((N, N_HEADS, HEAD_DIM, HEAD_DIM), F32)],
        scratch_shapes=[pltpu.VMEM((N_HEADS, HEAD_DIM, HEAD_DIM), F32)],
        compiler_params=_cp("arbitrary"))(q, k, v, bg, gc, gct, a)


def _dn_scan_bwd(q, k, v, bg, gc, gct, a, sall, do, dep=None):
    T = q.shape[0]
    N = T // CHUNK

    def body(q_ref, k_ref, v_ref, bg_ref, gc_ref, gct_ref, a_ref, sall_ref, do_ref, *rest):
        dq_ref, dk_ref, dv_ref, dbg_ref, ds_ref = rest[-5:]
        @pl.when(pl.program_id(0) == 0)
        def _():
            ds_ref[...] = jnp.zeros_like(ds_ref)
        bgv, gcv, gctv = bg_ref[...], gc_ref[...], gct_ref[0]
        lane = _lane_iota((CHUNK, 128))
        rowi = lax.broadcasted_iota(jnp.int32, (CHUNK, 1), 0)
        H = range(N_HEADS)
        sl = [slice(h * HEAD_DIM, (h + 1) * HEAD_DIM) for h in H]
        q_, k_, v_ = [q_ref[:, s] for s in sl], [k_ref[:, s] for s in sl], [v_ref[:, s] for s in sl]
        dO = [do_ref[:, s] for s in sl]
        beta = [bgv[:, h:h + 1] for h in H]
        t = [_dn_head_terms(q_[h], k_[h], v_[h], beta[h], gcv[:, N_HEADS + h:N_HEADS + h + 1],
                            gctv[N_HEADS + h:N_HEADS + h + 1, :]) for h in H]
        ii, jj = t[0]["ii"], t[0]["jj"]
        gam, E, Fd, cd, kb = ([t[h][n] for h in H] for n in ("gam", "E", "F", "cd", "kb"))
        S = [sall_ref[0, h] for h in H]
        dSn = [ds_ref[h] for h in H]
        A = [a_ref[0, h] for h in H]
        WS = [_nn(t[h]["W"], S[h]) for h in H]
        KK = [_nt(kb[h], k_[h]) for h in H]
        QK = [_nt(q_[h], k_[h]) for h in H]
        ktdS = [_nn(t[h]["kt"], dSn[h]) for h in H]
        d_qE = [_nt(dO[h], S[h]) for h in H]
        vn = [_apply_a(A[h], t[h]["vb"] - WS[h]) for h in H]
        PtdO = [_tn(QK[h] * gam[h], dO[h]) for h in H]
        qEdO = [_tn(t[h]["qE"], dO[h]) for h in H]
        dU = [_apply_at(A[h], PtdO[h] + ktdS[h]) for h in H]
        d_kt = [_nt(vn[h], dSn[h]) for h in H]
        dOvn = [_nt(dO[h], vn[h]) for h in H]
        dUvn = [_nt(dU[h], vn[h]) for h in H]
        dUS = [_nt(dU[h], S[h]) for h in H]
        WdU = [_tn(t[h]["W"], dU[h]) for h in H]
        for h in H:
            ds_ref[h] = cd[h] * dSn[h] + qEdO[h] - WdU[h]
        dQK = [jnp.where(ii >= jj, dOvn[h], 0.0) * gam[h] for h in H]
        dKK = [jnp.where(ii > jj, -dUvn[h], 0.0) * gam[h] for h in H]
        dQKk = [_nn(dQK[h], k_[h]) for h in H]
        dKKk = [_nn(dKK[h], k_[h]) for h in H]
        dQKq = [_tn(dQK[h], q_[h]) for h in H]
        dKKkb = [_tn(dKK[h], kb[h]) for h in H]
        dbeta_arr = jnp.zeros((CHUNK, 128), F32)
        dgc_arr = jnp.zeros((CHUNK, 128), F32)
        for h in H:
            dW = -dUS[h]
            dq_ref[:, sl[h]] = dQKk[h] + d_qE[h] * E[h]
            d_kb = dKKk[h] + dW * E[h]
            dk_ref[:, sl[h]] = dQKq[h] + dKKkb[h] + d_kb * beta[h] + d_kt[h] * Fd[h]
            dv_ref[:, sl[h]] = dU[h] * beta[h]
            Z = dQK[h] * QK[h] + dKK[h] * KK[h]
            d_cd = jnp.sum(S[h] * dSn[h])
            dbeta = jnp.sum(dU[h] * v_[h] + d_kb * k_[h], axis=-1, keepdims=True)
            dE = jnp.sum(dW * kb[h] + d_qE[h] * q_[h], axis=-1, keepdims=True)
            dFF = jnp.sum(d_kt[h] * k_[h], axis=-1, keepdims=True) * Fd[h]
            dgc = (dE * E[h] - dFF + jnp.sum(Z, axis=-1, keepdims=True) - jnp.sum(Z.T, axis=-1, keepdims=True)
                   + jnp.where(rowi == CHUNK - 1, jnp.sum(dFF) + d_cd * cd[h], 0.0))
            dbeta_arr = dbeta_arr + jnp.where(lane == h, dbeta, 0.0)
            dgc_arr = dgc_arr + jnp.where(lane == N_HEADS + h, dgc, 0.0)
        ii, jj, _ = _tri(True)
        rev = (jj >= ii).astype(F32)
        dbg_ref[...] = dbeta_arr + jnp.dot(rev, dgc_arr, precision=lax.Precision.HIGHEST,
                                           preferred_element_type=F32)

    row512 = pl.BlockSpec((CHUNK, DN_WIDTH), lambda n: (N - 1 - n, 0))
    row128 = pl.BlockSpec((CHUNK, 128), lambda n: (N - 1 - n, 0))
    in_specs, args = _with_dep(
        [row512, row512, row512, row128, row128,
         pl.BlockSpec((1, 8, CHUNK), lambda n: (N - 1 - n, 0, 0)),
         pl.BlockSpec((1, N_HEADS, CHUNK, CHUNK), lambda n: (N - 1 - n, 0, 0, 0)),
         pl.BlockSpec((1, N_HEADS, HEAD_DIM, HEAD_DIM), lambda n: (N - 1 - n, 0, 0, 0)), row512],
        [q, k, v, bg, gc, gct, a, sall, do], dep)
    return pl.pallas_call(
        body, name="dn_scan_bwd", grid=(N,), in_specs=in_specs,
        out_specs=[row512, row512, row512, row128],
        out_shape=[jax.ShapeDtypeStruct((T, DN_WIDTH), F32)] * 3 + [jax.ShapeDtypeStruct((T, 128), F32)],
        scratch_shapes=[pltpu.VMEM((N_HEADS, HEAD_DIM, HEAD_DIM), F32)],
        compiler_params=_cp("arbitrary"))(*args)


def _sg_mask():
    ii = lax.broadcasted_iota(jnp.int32, (SG_BLOCK, SG_BLOCK), 0) // CHUNK
    jj = lax.broadcasted_iota(jnp.int32, (SG_BLOCK, SG_BLOCK), 1) // CHUNK
    return jj <= ii


def _mix_fwd(o, p, ong, sgn, sgw, sgbt):
    T = o.shape[0]
    rb = SG_BLOCK

    def body(o_ref, gate_ref, u_ref, vg_ref, ong_ref, sgn_ref, sgw_ref, sgbt_ref, mix_ref):
        mask = _sg_mask()
        gate = gate_ref[...]
        for h in range(N_HEADS):
            sl = slice(h * HEAD_DIM, (h + 1) * HEAD_DIM)
            oh = o_ref[:, sl]
            r = lax.rsqrt(jnp.mean(oh * oh, axis=-1, keepdims=True) + EPS)
            mix_ref[:, sl] = (oh * r * ong_ref[...] * _silu(gate[:, sl])).astype(BF16)
        for gi in range(SG_GROUPS):
            sl = slice(gi * SG_BLOCK, (gi + 1) * SG_BLOCK)
            gv = _gelu(vg_ref[:, sl])
            r = lax.rsqrt(jnp.mean(gv * gv, axis=-1, keepdims=True) + EPS)
            vh = gv * r * sgn_ref[:, sl]
            s = _nn(jnp.where(mask, sgw_ref[gi], 0.0), vh) + sgbt_ref[:, gi:gi + 1]
            mix_ref[:, DN_WIDTH + gi * SG_BLOCK:DN_WIDTH + (gi + 1) * SG_BLOCK] = (_gelu(u_ref[:, sl]) * s).astype(BF16)

    def col(c):
        return pl.BlockSpec((rb, 512), lambda i: (i, c))
    return pl.pallas_call(
        body, name="mix_fwd", grid=(T // rb,),
        in_specs=[pl.BlockSpec((rb, DN_WIDTH), lambda i: (i, 0)), col(3), col(4), col(5),
                  pl.BlockSpec((1, 128), lambda i: (0, 0)), pl.BlockSpec((1, SG_WIDTH), lambda i: (0, 0)),
                  pl.BlockSpec((SG_GROUPS, SG_BLOCK, SG_BLOCK), lambda i: (0, 0, 0)),
                  pl.BlockSpec((SG_BLOCK, 128), lambda i: (0, 0))],
        out_specs=pl.BlockSpec((rb, D_MODEL), lambda i: (i, 0)),
        out_shape=jax.ShapeDtypeStruct((T, D_MODEL), BF16),
        compiler_params=_cp("parallel"))(o, p, p, p, ong, sgn, sgw, sgbt)


def _mix_bwd(o, p, ong, sgn, sgw, sgbt, dmix, dep=None):
    T = o.shape[0]
    rb = SG_BLOCK

    def body(o_ref, gate_ref, u_ref, vg_ref, ong_ref, sgn_ref, sgw_ref, sgbt_ref, dmix_ref, *rest):
        do_ref, dp_ref, gong_ref, gsgn_ref, gsgw_ref, gsgbt_ref = rest[-6:]
        @pl.when(pl.program_id(0) == 0)
        def _():
            gong_ref[...] = jnp.zeros_like(gong_ref)
            gsgn_ref[...] = jnp.zeros_like(gsgn_ref)
            gsgw_ref[...] = jnp.zeros_like(gsgw_ref)
            gsgbt_ref[...] = jnp.zeros_like(gsgbt_ref)
        mask = _sg_mask()
        gate = gate_ref[...]
        lane = _lane_iota((SG_BLOCK, 128))
        for h in range(N_HEADS):
            sl = slice(h * HEAD_DIM, (h + 1) * HEAD_DIM)
            oh = o_ref[:, sl]
            dm = dmix_ref[:, sl]
            r = lax.rsqrt(jnp.mean(oh * oh, axis=-1, keepdims=True) + EPS)
            oh_hat = oh * r
            gt = gate[:, sl]
            sg = _silu(gt)
            dp_ref[:, sl] = (dm * oh_hat * ong_ref[...] * _dsilu(gt)).astype(BF16)
            dn_ = dm * sg
            gong_ref[...] += jnp.sum(dn_ * oh_hat, axis=0, keepdims=True)
            dhat = dn_ * ong_ref[...]
            do_ref[:, sl] = r * (dhat - oh_hat * jnp.mean(dhat * oh_hat, axis=-1, keepdims=True))
        for gi in range(SG_GROUPS):
            sl = slice(gi * SG_BLOCK, (gi + 1) * SG_BLOCK)
            vraw = vg_ref[:, sl]
            gv = _gelu(vraw)
            r = lax.rsqrt(jnp.mean(gv * gv, axis=-1, keepdims=True) + EPS)
            vhat = gv * r
            vn = vhat * sgn_ref[:, sl]
            wm = jnp.where(mask, sgw_ref[gi], 0.0)
            s = _nn(wm, vn) + sgbt_ref[:, gi:gi + 1]
            uraw = u_ref[:, sl]
            dm = dmix_ref[:, DN_WIDTH + gi * SG_BLOCK:DN_WIDTH + (gi + 1) * SG_BLOCK]
            dp_ref[:, DN_WIDTH + gi * SG_BLOCK:DN_WIDTH + (gi + 1) * SG_BLOCK] = (dm * s * _dgelu(uraw)).astype(BF16)
            ds = dm * _gelu(uraw)
            gsgbt_ref[...] += jnp.where(lane == gi, jnp.sum(ds, axis=-1, keepdims=True), 0.0)
            gsgw_ref[gi] += jnp.where(mask, _nt(ds, vn), 0.0)
            dvn = _tn(wm, ds)
            gsgn_ref[:, sl] += jnp.sum(dvn * vhat, axis=0, keepdims=True)
            dhat = dvn * sgn_ref[:, sl]
            dgv = r * (dhat - vhat * jnp.mean(dhat * vhat, axis=-1, keepdims=True))
            dp_ref[:, 2 * DN_WIDTH + gi * SG_BLOCK:2 * DN_WIDTH + (gi + 1) * SG_BLOCK] = (dgv * _dgelu(vraw)).astype(BF16)

    def col(c):
        return pl.BlockSpec((rb, 512), lambda i: (i, c))
    full = lambda *s: pl.BlockSpec(s, lambda i: (0,) * len(s))
    in_specs, args = _with_dep(
        [pl.BlockSpec((rb, DN_WIDTH), lambda i: (i, 0)), col(3), col(4), col(5),
         full(1, 128), full(1, SG_WIDTH), full(SG_GROUPS, SG_BLOCK, SG_BLOCK), full(SG_BLOCK, 128),
         pl.BlockSpec((rb, D_MODEL), lambda i: (i, 0))],
        [o, p, p, p, ong, sgn, sgw, sgbt, dmix], dep)
    return pl.pallas_call(
        body, name="mix_bwd", grid=(T // rb,), in_specs=in_specs,
        out_specs=[pl.BlockSpec((rb, DN_WIDTH), lambda i: (i, 0)), pl.BlockSpec((rb, 3 * 512), lambda i: (i, 0)),
                   full(1, 128), full(1, SG_WIDTH), full(SG_GROUPS, SG_BLOCK, SG_BLOCK), full(SG_BLOCK, 128)],
        out_shape=[jax.ShapeDtypeStruct((T, DN_WIDTH), F32), jax.ShapeDtypeStruct((T, 3 * 512), BF16),
                   jax.ShapeDtypeStruct((1, 128), F32), jax.ShapeDtypeStruct((1, SG_WIDTH), F32),
                   jax.ShapeDtypeStruct((SG_GROUPS, SG_BLOCK, SG_BLOCK), F32),
                   jax.ShapeDtypeStruct((SG_BLOCK, 128), F32)],
        compiler_params=_cp("arbitrary"))(*args)


def _pad_lanes(row, offset=0):
    n = row.shape[1]
    return jnp.pad(row, ((0, 0), (offset, 128 - n - offset)))


def _local_step(x, tgt, w, dep=None, late_weights=None, on_grad=None):
    T = x.shape[0]
    N = T // CHUNK
    on_grad = on_grad or (lambda name, g: None)
    alog_row = _pad_lanes(w["dn_a_log"], N_HEADS)
    dtb_row = _pad_lanes(w["dn_dt_bias"], N_HEADS)
    sgbt = jnp.pad(w["sg_b"].T, ((0, 0), (0, 128 - SG_GROUPS)))

    h1 = _rms_fwd("rms_attn", x, w["attn_norm_g"], dep=dep)
    p = _mm_nn("in_proj", h1, w["w_in"], F32, 512, PROJ_PAD)
    q, k, v, bg = _dn_act(p, w["dn_conv_w"], alog_row, dtb_row)
    gc, gct, lmat = _dn_chunk(k, bg)
    lt = lmat.reshape(N * N_HEADS, CHUNK * CHUNK).T
    at = _tri_inv(lt)
    a = at.reshape(CHUNK * CHUNK, N * N_HEADS).T.reshape(N, N_HEADS, CHUNK, CHUNK)
    o, sall = _dn_scan(q, k, v, bg, gc, gct, a)
    mix = _mix_fwd(o, p, w["dn_out_norm_g"], w["sg_norm_g"], w["sg_w"], sgbt)
    if late_weights is not None:
        w = {**w, **late_weights(mix)}
    x2 = _mm_nn("out_proj", mix, w["w_out"], F32, 512, 1024, res=x)
    h2 = _rms_fwd("rms_ffn", x2, w["ffn_norm_g"])
    up = _mm_nn("up_proj", h2, w["w_up"], F32, 512, D_FF)
    act = _ffn_act(up, w["ffn_conv_w"], w["ffn_conv_b"])
    x3 = _mm_nn("down_proj", act, w["w_down"], F32, 512, 1024, res=x2)
    loss, dx3, g_final = _loss_head(x3, tgt, w["final_norm_g"])

    dact = _mm_nt("d_act", dx3, w["w_down"], F32, 512, D_FF)
    g_w_down = _mm_tn("g_w_down", act, dx3, D_FF, 1024, 512)
    tok = on_grad("w_down", g_w_down)
    dup, g_ffn_conv_w, g_ffn_conv_b = _ffn_act_bwd(up, dact, w["ffn_conv_w"], w["ffn_conv_b"], dep=tok)
    g_w_up = _mm_tn("g_w_up", h2, dup, 1024, 2 * D_FF // 4, 512, col_major_tiles=True)
    tok = on_grad("w_up", g_w_up)
    dh2 = _mm_nt("d_h2", dup, w["w_up"], F32, 512, 1024, dep=tok)
    dx2, g_ffn_norm = _rms_bwd("rms_ffn_bwd", dh2, x2, w["ffn_norm_g"], dx3)
    dmix = _mm_nt("d_mix", dx2, w["w_out"], F32, 512, 1024)
    g_w_out = _mm_tn("g_w_out", mix, dx2, 1024, 1024, 1024)
    tok = on_grad("w_out", g_w_out)
    do, dp_mid, g_ong, g_sgn, g_sgw, g_sgbt = _mix_bwd(o, p, w["dn_out_norm_g"], w["sg_norm_g"], w["sg_w"], sgbt,
                                                      dmix, dep=tok)
    early = dict(dn_out_norm_g=g_ong, sg_norm_g=g_sgn, sg_w=g_sgw, sg_b=g_sgbt[:, :SG_GROUPS].T,
                 ffn_norm_g=g_ffn_norm, ffn_conv_w=g_ffn_conv_w, ffn_conv_b=g_ffn_conv_b, final_norm_g=g_final)
    tok = on_grad("small_early", early)
    dq, dk, dv, dbg = _dn_scan_bwd(q, k, v, bg, gc, gct, a, sall, do, dep=tok)
    dp_qkv, dba, g_dn_conv_w, g_ad = _dn_act_bwd(p, w["dn_conv_w"], alog_row, dtb_row, dq, dk, dv, dbg)
    dp =jnp.concatenate([dp_qkv, dp_mid, dba], axis=1)
    g_w_in = _mm_tn("g_w_in", h1, dp, 1024, PROJ_PAD, 512)[:, :PROJ_COLS]
    tok = on_grad("w_in", g_w_in)
    dh1 = _mm_nt("d_h1", dp, w["w_in"], F32, 512, 1024, dep=tok)
    grad_x, g_attn_norm = _rms_bwd("rms_attn_bwd", dh1, x, w["attn_norm_g"], dx2)

    grads = dict(
        attn_norm_g=g_attn_norm, w_in=g_w_in, dn_conv_w=g_dn_conv_w,
        dn_a_log=g_ad[0:1, N_HEADS:2 * N_HEADS], dn_dt_bias=g_ad[1:2, N_HEADS:2 * N_HEADS],
        w_out=g_w_out, w_up=g_w_up, w_down=g_w_down, **early)
    return loss, grad_x, grads


def _me():
    return lax.axis_index("x"), lax.axis_index("y"), lax.axis_index("c")


def _peer(rel):
    x, y, c = _me()
    return {"x": (1 - x, y, c), "y": (x, 1 - y, c), "xy": (1 - x, 1 - y, c), "c": (x, y, 1 - c)}[rel]


def _chip_of(dev):
    return 2 * dev[0] + dev[1]


CHIP_RELS = ("x", "y", "xy")


def _run_copies(copies, sends, recvs):
    for cp in copies:
        cp.start()
    for cp in recvs:
        cp.wait_recv()
    for cp in sends:
        cp.wait_send()


def _gather_shards(shards):
    n = len(shards)

    def body(*refs):
        src, out = refs[:n], refs[n:2 * n]
        send_sems, recv_sems, local_sems = refs[2 * n:]
        me = _chip_of(_me())
        local = [pltpu.make_async_copy(src[i], out[i].at[me], local_sems.at[i]) for i in range(n)]
        for cp in local:
            cp.start()
        sends, recvs = [], []
        for i in range(n):
            for r, rel in enumerate(CHIP_RELS):
                k = 3 * i + r
                peer = _peer(rel)
                sends.append(pltpu.make_async_remote_copy(
                    src_ref=src[i], dst_ref=out[i].at[me], send_sem=send_sems.at[k], recv_sem=recv_sems.at[k],
                    device_id=peer, device_id_type=MESH))
                recvs.append(pltpu.make_async_remote_copy(
                    src_ref=src[i], dst_ref=out[i].at[_chip_of(peer)], send_sem=send_sems.at[k],
                    recv_sem=recv_sems.at[k], device_id=peer, device_id_type=MESH))
        _run_copies(sends, sends, recvs)
        for cp in local:
            cp.wait()

    return pl.pallas_call(
        body, name="gather_weights", in_specs=[ANY] * n, out_specs=[ANY] * n,
        out_shape=[jax.ShapeDtypeStruct((4,) + s.shape, s.dtype) for s in shards],
        scratch_shapes=[pltpu.SemaphoreType.DMA((3 * n,)), pltpu.SemaphoreType.DMA((3 * n,)),
                        pltpu.SemaphoreType.DMA((n,))])(*shards)


OTHERS = tuple((fx, fy, fc) for fx in (0, 1) for fy in (0, 1) for fc in (0, 1) if (fx, fy, fc) != (0, 0, 0))


def _other(flip):
    x, y, c = _me()
    return (x ^ flip[0], y ^ flip[1], c ^ flip[2])


def _linear(dev):
    return 4 * dev[0] + 2 * dev[1] + dev[2]


def _exchange_small(small):
    def body(small_ref, out_ref, send_sems, recv_sems, local_sem):
        my_slot = _linear(_me())
        local = pltpu.make_async_copy(small_ref, out_ref.at[my_slot], local_sem)
        local.start()
        sends, recvs = [], []
        for k, flip in enumerate(OTHERS):
            peer = _other(flip)
            sends.append(pltpu.make_async_remote_copy(
                src_ref=small_ref, dst_ref=out_ref.at[my_slot], send_sem=send_sems.at[k], recv_sem=recv_sems.at[k],
                device_id=peer, device_id_type=MESH))
            recvs.append(pltpu.make_async_remote_copy(
                src_ref=small_ref, dst_ref=out_ref.at[_linear(peer)], send_sem=send_sems.at[k],
                recv_sem=recv_sems.at[k], device_id=peer, device_id_type=MESH))
        _run_copies(sends, sends, recvs)
        local.wait()

    return pl.pallas_call(
        body, name="exchange_small", in_specs=[ANY], out_specs=ANY,
        out_shape=jax.ShapeDtypeStruct((8,) + small.shape, small.dtype),
        scratch_shapes=[pltpu.SemaphoreType.DMA((7,)), pltpu.SemaphoreType.DMA((7,)), pltpu.SemaphoreType.DMA])(small)


def _pair_swap(halves):
    n = len(halves)

    def body(*refs):
        src, out = refs[:n], refs[n:2 * n]
        send_sems, recv_sems = refs[2 * n:]
        sib = _peer("c")
        copies = [pltpu.make_async_remote_copy(
            src_ref=src[i], dst_ref=out[i], send_sem=send_sems.at[i], recv_sem=recv_sems.at[i],
            device_id=sib, device_id_type=MESH) for i in range(n)]
        _run_copies(copies, copies, copies)

    return pl.pallas_call(
        body, name="pair_swap", in_specs=[ANY] * n, out_specs=[ANY] * n,
        out_shape=[jax.ShapeDtypeStruct(h.shape, h.dtype) for h in halves],
        scratch_shapes=[pltpu.SemaphoreType.DMA((n,)), pltpu.SemaphoreType.DMA((n,))])(*halves)


HBM = pl.BlockSpec(memory_space=pltpu.HBM)
SEM = pl.BlockSpec(memory_space=pltpu.SEMAPHORE)
EFFECT = pltpu.SideEffectType.DATAFLOW_SIDE_EFFECTING


def _hbm(a):
    return pltpu.with_memory_space_constraint(a, pltpu.HBM)


def _transfer_start(name, srcs, lands, n_copies, make_copies, after=None):
    n, m = len(srcs), len(lands)

    def body(*refs):
        src, land = refs[:n], refs[n:n + m]
        outs = refs[n + m + (after is not None):]
        send_sems, recv_sems, token = outs[0], outs[1], outs[-1]
        for cp in make_copies(src, land, send_sems, recv_sems):
            cp.start()
        token[...] = jnp.zeros_like(token)

    arrs = list(srcs) + list(lands)
    in_specs, args = _with_dep([HBM] * (n + m), [_hbm(a) for a in arrs], after)
    out = pl.pallas_call(
        body, name=name,
        out_shape=(pltpu.SemaphoreType.DMA((n_copies,)), pltpu.SemaphoreType.DMA((n_copies,)),
                   *[pltpu.HBM(a.shape, a.dtype) for a in arrs], jax.ShapeDtypeStruct((8, 128), F32)),
        in_specs=in_specs,
        out_specs=(SEM, SEM, *[HBM] * (n + m), pl.BlockSpec(memory_space=pltpu.VMEM)),
        input_output_aliases={i: 2 + i for i in range(n + m)},
        compiler_params=pltpu.CompilerParams(has_side_effects=EFFECT))(*args)
    return out[0], out[1], list(out[2:2 + n]), list(out[2 + n:2 + n + m]), out[-1]


def _transfer_wait(name, send_sems, recv_sems, srcs, lands, make_copies, after):
    n, m = len(srcs), len(lands)

    def body(*refs):
        src, land = refs[:n], refs[n:n + m]
        s_sems, r_sems = refs[n + m], refs[n + m + 1]
        for cp in make_copies(src, land, s_sems, r_sems):
            cp.wait_send()
            cp.wait_recv()

    arrs = list(srcs) + list(lands)
    out = pl.pallas_call(
        body, name=name, out_shape=tuple(pltpu.HBM(a.shape, a.dtype) for a in arrs),
        in_specs=[HBM] * (n + m) + [SEM, SEM, ANY], out_specs=tuple([HBM] * (n + m)),
        input_output_aliases={i: i for i in range(n + m)},
        compiler_params=pltpu.CompilerParams(has_side_effects=EFFECT))(*arrs, send_sems, recv_sems, after)
    return list(out[:n]), list(out[n:])


def _gather_copies(src, land, send_sems, recv_sems):
    me = _chip_of(_me())
    copies = []
    for i in range(len(src)):
        for r, rel in enumerate(CHIP_RELS):
            k = 3 * i + r
            copies.append(pltpu.make_async_remote_copy(
                src_ref=src[i], dst_ref=land[i].at[me], send_sem=send_sems.at[k], recv_sem=recv_sems.at[k],
                device_id=_peer(rel), device_id_type=MESH))
    return copies


def _small_copies(src, land, send_sems, recv_sems):
    my_slot = _linear(_me())
    return [pltpu.make_async_remote_copy(
        src_ref=src[0], dst_ref=land[0].at[my_slot], send_sem=send_sems.at[k], recv_sem=recv_sems.at[k],
        device_id=_other(flip), device_id_type=MESH) for k, flip in enumerate(OTHERS)]


def _pieces_copies(src, land, send_sems, recv_sems):
    copies = []
    for k, flip in enumerate(OTHERS):
        peer = _other(flip)
        copies.append(pltpu.make_async_remote_copy(
            src_ref=src[0].at[_linear(peer)], dst_ref=land[0].at[k], send_sem=send_sems.at[k],
            recv_sem=recv_sems.at[k], device_id=peer, device_id_type=MESH))
    return copies


def _row_block(rows, cols, budget=2 * 1024 * 1024):
    rb = max(8, (budget // (4 * cols)) // 8 * 8)
    while rows % rb:
        rb -= 8
    return rb if rb > 0 else rows


def _sum_slots(name, first, rest):
    R, Cc = first.shape
    K = rest.shape[0]
    rb = _row_block(R, Cc)

    def body(f_ref, r_ref, o_ref):
        acc = f_ref[...].astype(F32)
        for j in range(K):
            acc = acc + r_ref[j].astype(F32)
        o_ref[...] = acc

    return pl.pallas_call(
        body, name=name, grid=(R // rb,),
        in_specs=[pl.BlockSpec((rb, Cc), lambda i: (i, 0)), pl.BlockSpec((K, rb, Cc), lambda i: (0, i, 0))],
        out_specs=pl.BlockSpec((rb, Cc), lambda i: (i, 0)),
        out_shape=jax.ShapeDtypeStruct((R, Cc), F32), compiler_params=_cp("parallel"))(first, rest)


def _adamw_math(w, gv, m, v):
    mn = ADAM_B1 * m + (1.0 - ADAM_B1) * gv
    vn = ADAM_B2 * v + (1.0 - ADAM_B2) * (gv * gv)
    m_hat = mn / (1.0 - ADAM_B1 ** ADAM_STEP)
    v_hat = vn / (1.0 - ADAM_B2 ** ADAM_STEP)
    return -ADAM_LR * (m_hat / (jnp.sqrt(v_hat) + ADAM_EPS) + ADAM_WD * w), mn, vn


def _adamw_halves(name, w, mine, theirs, m, v, core):
    R, Cc = w.shape
    r2 = R // 2
    rb = _row_block(r2, Cc, 1024 * 1024)
    nb2 = r2 // rb

    def body(c_ref, w_ref, mine_ref, theirs_ref, m_ref, v_ref, g_ref, d_ref, mo_ref, vo_ref):
        is_mine = (pl.program_id(0) // nb2) == c_ref[0]
        gv = jnp.where(is_mine, mine_ref[...], theirs_ref[...])
        g_ref[...] = gv
        d_ref[...], mo_ref[...], vo_ref[...] = _adamw_math(w_ref[...], gv, m_ref[...], v_ref[...])

    blk = pl.BlockSpec((rb, Cc), lambda i, c: (i, 0))
    half = lambda own: pl.BlockSpec(
        (rb, Cc), lambda i, c: (jnp.clip(i - (c[0] if own else 1 - c[0]) * nb2, 0, nb2 - 1), 0))
    return pl.pallas_call(
        body, name=name,
        grid_spec=pltpu.PrefetchScalarGridSpec(
            num_scalar_prefetch=1, grid=(2 * nb2,), in_specs=[blk, half(True), half(False), blk, blk],
            out_specs=[blk] * 4),
        out_shape=[jax.ShapeDtypeStruct((R, Cc), F32)] * 4, compiler_params=_cp("parallel"))(core, w, mine, theirs, m, v)


def _adamw(name, w, g, m, v):
    R, Cc = w.shape
    rb = _row_block(R, Cc, 1024 * 1024)

    def body(w_ref, g_ref, m_ref, v_ref, d_ref, mo_ref, vo_ref):
        d_ref[...], mo_ref[...], vo_ref[...] = _adamw_math(w_ref[...], g_ref[...], m_ref[...], v_ref[...])

    blk = pl.BlockSpec((rb, Cc), lambda i: (i, 0))
    return pl.pallas_call(
        body, name=name, grid=(R // rb,), in_specs=[blk] * 4, out_specs=[blk] * 3,
        out_shape=[jax.ShapeDtypeStruct((R, Cc), F32)] * 3, compiler_params=_cp("parallel"))(w, g, m, v)


def _pack(arrs):
    rows = []
    for a in arrs:
        flat = a.reshape(-1)
        pad = (-flat.shape[0]) % 128
        rows.append(jnp.pad(flat, (0, pad)).reshape(-1, 128))
    buf = jnp.concatenate(rows, axis=0)
    return jnp.pad(buf, ((0, (-buf.shape[0]) % 8), (0, 0)))


def _unpack(buf, shapes):
    out, r = [], 0
    for s in shapes:
        n = math.prod(s)
        nr = -(-n // 128)
        out.append(buf[r:r + nr].reshape(-1)[:n].reshape(s))
        r += nr
    return out


BIG = ("w_in", "w_out", "w_up", "w_down")
CONV = ("dn_conv_w", "ffn_conv_w")
REPL = ("attn_norm_g", "dn_a_log", "dn_dt_bias", "dn_out_norm_g", "sg_norm_g", "sg_w", "sg_b",
        "ffn_norm_g", "ffn_conv_b", "final_norm_g")
ORDER = ("attn_norm_g", "w_in", "dn_conv_w", "dn_a_log", "dn_dt_bias", "dn_out_norm_g", "sg_norm_g", "sg_w",
         "sg_b", "w_out", "ffn_norm_g", "w_up", "ffn_conv_w", "ffn_conv_b", "w_down", "final_norm_g")


def kernel(x, attn_norm_g, w_in, dn_conv_w, dn_a_log, dn_dt_bias, dn_out_norm_g, sg_norm_g, sg_w, sg_b, w_out, ffn_norm_g, w_up, ffn_conv_w, ffn_conv_b, w_down, final_norm_g, loss_target, m_attn_norm_g, m_w_in, m_dn_conv_w, m_dn_a_log, m_dn_dt_bias, m_dn_out_norm_g, m_sg_norm_g, m_sg_w, m_sg_b, m_w_out, m_ffn_norm_g, m_w_up, m_ffn_conv_w, m_ffn_conv_b, m_w_down, m_final_norm_g, v_attn_norm_g, v_w_in, v_dn_conv_w, v_dn_a_log, v_dn_dt_bias, v_dn_out_norm_g, v_sg_norm_g, v_sg_w, v_sg_b, v_w_out, v_ffn_norm_g, v_w_up, v_ffn_conv_w, v_ffn_conv_b, v_w_down, v_final_norm_g):
    W = dict(attn_norm_g=attn_norm_g, w_in=w_in, dn_conv_w=dn_conv_w, dn_a_log=dn_a_log, dn_dt_bias=dn_dt_bias,
             dn_out_norm_g=dn_out_norm_g, sg_norm_g=sg_norm_g, sg_w=sg_w, sg_b=sg_b, w_out=w_out,
             ffn_norm_g=ffn_norm_g, w_up=w_up, ffn_conv_w=ffn_conv_w, ffn_conv_b=ffn_conv_b, w_down=w_down,
             final_norm_g=final_norm_g)
    Mo = dict(attn_norm_g=m_attn_norm_g, w_in=m_w_in, dn_conv_w=m_dn_conv_w, dn_a_log=m_dn_a_log,
              dn_dt_bias=m_dn_dt_bias, dn_out_norm_g=m_dn_out_norm_g, sg_norm_g=m_sg_norm_g, sg_w=m_sg_w,
              sg_b=m_sg_b, w_out=m_w_out, ffn_norm_g=m_ffn_norm_g, w_up=m_w_up, ffn_conv_w=m_ffn_conv_w,
              ffn_conv_b=m_ffn_conv_b, w_down=m_w_down, final_norm_g=m_final_norm_g)
    Vo = dict(attn_norm_g=v_attn_norm_g, w_in=v_w_in, dn_conv_w=v_dn_conv_w, dn_a_log=v_dn_a_log,
              dn_dt_bias=v_dn_dt_bias, dn_out_norm_g=v_dn_out_norm_g, sg_norm_g=v_sg_norm_g, sg_w=v_sg_w,
              sg_b=v_sg_b, w_out=v_w_out, ffn_norm_g=v_ffn_norm_g, w_up=v_w_up, ffn_conv_w=v_ffn_conv_w,
              ffn_conv_b=v_ffn_conv_b, w_down=v_w_down, final_norm_g=v_final_norm_g)
    xi, yi, ci = lax.axis_index("x"), lax.axis_index("y"), lax.axis_index("c")
    chip = 2 * xi + yi

    me_lin = 4 * xi + 2 * yi + ci

    g_in, g_dnc = _gather_shards([w_in[0].astype(BF16), dn_conv_w[0]])
    late = ("w_out", "w_up", "w_down", "ffn_conv_w")
    late_shards = [W[n][0].astype(BF16) for n in late[:3]] + [ffn_conv_w[0]]
    late_lands = [lax.dynamic_update_index_in_dim(lax.empty((4,) + s.shape, s.dtype), s, chip, 0) for s in late_shards]
    n_late = 3 * len(late_shards)
    ssem, rsem, late_src, late_lands, token = _transfer_start("gather_rest_start", late_shards, late_lands,
                                                              n_late, _gather_copies, after=g_in)

    def late_weights(after):
        _, (g_out, g_up, g_down, g_ffc) = _transfer_wait("gather_rest_wait", ssem, rsem, late_src, late_lands,
                                                         _gather_copies, after)
        return dict(w_out=g_out.reshape(D_MODEL, D_MODEL), w_up=g_up.transpose(1, 0, 2).reshape(D_MODEL, 2 * D_FF),
                    w_down=g_down.reshape(D_FF, D_MODEL), ffn_conv_w=g_ffc.transpose(1, 0, 2).reshape(3, 2 * D_FF))

    full = dict(
        w_in=jnp.pad(g_in.transpose(1, 0, 2).reshape(D_MODEL, PROJ_COLS), ((0, 0), (0, PROJ_PAD - PROJ_COLS))),
        dn_conv_w=g_dnc.transpose(1, 0, 2).reshape(4, 3 * DN_WIDTH),
        attn_norm_g=attn_norm_g, dn_a_log=dn_a_log, dn_dt_bias=dn_dt_bias, dn_out_norm_g=dn_out_norm_g,
        sg_norm_g=sg_norm_g, sg_w=sg_w[0], sg_b=sg_b[0], ffn_norm_g=ffn_norm_g, ffn_conv_b=ffn_conv_b,
        final_norm_g=final_norm_g[None])

    pending = {}
    early_names = ("dn_out_norm_g", "sg_norm_g", "sg_w", "sg_b", "ffn_norm_g", "ffn_conv_w", "ffn_conv_b",
                   "final_norm_g")
    late_names = ("attn_norm_g", "dn_a_log", "dn_dt_bias", "dn_conv_w")

    def on_grad(name, gw):
        if name == "small_early":
            buf = _pack([gw[n] for n in early_names])
            land = lax.dynamic_update_index_in_dim(lax.empty((8,) + buf.shape, F32), buf, me_lin, 0)
            s_sem, r_sem, src, lands, tok = _transfer_start("small_early_start", [buf], [land], 7, _small_copies)
            pending[name] = (s_sem, r_sem, src, lands)
            return tok
        g8 = gw.reshape(D_MODEL, 4, -1).transpose(1, 0, 2) if name == "w_in" else gw
        g8 = g8.reshape(8, -1, g8.shape[-1])
        land = lax.empty((7,) + g8.shape[1:], BF16)
        s_sem, r_sem, src, lands, tok = _transfer_start(f"reduce_{name}_start", [g8], [land], 7, _pieces_copies)
        pending[name] = (s_sem, r_sem, src, lands)
        return tok

    loss_row, grad_x, g = _local_step(x[0], loss_target[0], full, dep=token, late_weights=late_weights,
                                      on_grad=on_grad)

    small_names = REPL + CONV
    late_all = _exchange_small(_pack([g[n] for n in late_names] + [loss_row]))
    late_sum = _sum_slots("sum_small_late", late_all[0], late_all[1:])
    s_sem, r_sem, src, lands = pending["small_early"]
    _, (early_all,) = _transfer_wait("small_early_wait", s_sem, r_sem, src, lands, _small_copies, grad_x)
    early_sum = _sum_slots("sum_small_early", early_all[0], early_all[1:])
    *late_vals, loss_sum = _unpack(late_sum, [g[n].shape for n in late_names] + [loss_row.shape])
    loss = loss_sum[0, 0]
    sg = dict(zip(late_names, late_vals))
    sg.update(zip(early_names, _unpack(early_sum, [g[n].shape for n in early_names])))
    sg["dn_conv_w"] = lax.dynamic_slice_in_dim(sg["dn_conv_w"], chip * (3 * DN_WIDTH // 4), 3 * DN_WIDTH // 4, axis=1)
    sg["ffn_conv_w"] = lax.dynamic_slice_in_dim(sg["ffn_conv_w"], chip * (2 * D_FF // 4), 2 * D_FF // 4, axis=1)

    halves = []
    for n in ("w_down", "w_up", "w_out", "w_in"):
        s_sem, r_sem, src, lands = pending[n]
        sent, got = _transfer_wait(f"reduce_{n}_wait", s_sem, r_sem, src, lands, _pieces_copies, grad_x)
        own = lax.dynamic_index_in_dim(sent[0], me_lin, axis=0, keepdims=False)
        halves.append(_sum_slots(f"sum_{n}", own, got[0]))
    theirs = _pair_swap(halves)
    core = ci.astype(jnp.int32).reshape(1)
    grads, delta, new_m, new_v = {}, {}, {}, {}
    for n, mine_h, their_h in zip(("w_down", "w_up", "w_out", "w_in"), halves, theirs):
        shp = W[n].shape
        gr, d, mn, vn = _adamw_halves(f"adamw_{n}", W[n][0], mine_h, their_h, Mo[n][0], Vo[n][0], core)
        grads[n], delta[n], new_m[n], new_v[n] = gr.reshape(shp), d.reshape(shp), mn.reshape(shp), vn.reshape(shp)
    shapes = [W[n].shape for n in small_names]
    for n in small_names:
        grads[n] = sg[n].reshape(W[n].shape)
    d, mn, vn = _adamw("adamw_small", _pack([W[n] for n in small_names]), _pack([grads[n] for n in small_names]),
                       _pack([Mo[n] for n in small_names]), _pack([Vo[n] for n in small_names]))
    for dst, buf in ((delta, d), (new_m, mn), (new_v, vn)):
        dst.update(zip(small_names, _unpack(buf, shapes)))

    return (loss, grad_x[None], *[grads[n] for n in ORDER], *[delta[n] for n in ORDER],
            *[new_m[n] for n in ORDER], *[new_v[n] for n in ORDER])
```

```python
import functools
import math

import jax
import jax.numpy as jnp
from jax import lax
from jax.experimental import pallas as pl
from jax.experimental.pallas import tpu as pltpu

F32 = jnp.float32
BF16 = jnp.bfloat16

D_MODEL = 1024
CHUNK = 64
HEAD_DIM = 128
N_HEADS = 4
DN_WIDTH = 512
SG_WIDTH = 512
SG_GROUPS = 4
SG_BLOCK = 128
D_FF = 2816
PROJ_COLS = 3080
PROJ_PAD = 3200
BA_COL = 3072
EPS = 1e-6
NEG = -1e30
VMEM_LIMIT = 56 * 1024 * 1024

ADAM_LR = 0.001
ADAM_B1 = 0.9
ADAM_B2 = 0.999
ADAM_EPS = 1e-08
ADAM_WD = 0.01
ADAM_STEP = 10

MESH = pl.DeviceIdType.MESH
ANY = pl.BlockSpec(memory_space=pl.ANY)


def _cp(*sem):
    return pltpu.CompilerParams(dimension_semantics=sem, vmem_limit_bytes=VMEM_LIMIT)


def _bf(a):
    return a.astype(BF16)


def _nn(a, b):
    return jnp.dot(_bf(a), _bf(b), preferred_element_type=F32)


def _nt(a, b):
    return lax.dot_general(_bf(a), _bf(b), (((1,), (1,)), ((), ())), preferred_element_type=F32)


def _tn(a, b):
    return lax.dot_general(_bf(a), _bf(b), (((0,), (0,)), ((), ())), preferred_element_type=F32)


def _split(a):
    hi = _bf(a)
    return hi, _bf(a - hi.astype(F32))


def _sigmoid(x):
    return 0.5 * jnp.tanh(0.5 * x) + 0.5


def _silu(x):
    return x * _sigmoid(x)


def _dsilu(x):
    s = _sigmoid(x)
    return s * (1.0 + x * (1.0 - s))


_GELU_C = math.sqrt(2.0 / math.pi)
_GELU_A = 0.044715


def _gelu(x):
    return 0.5 * x * (1.0 + jnp.tanh(_GELU_C * (x + _GELU_A * x * x * x)))


def _dgelu(x):
    t = jnp.tanh(_GELU_C * (x + _GELU_A * x * x * x))
    return 0.5 * (1.0 + t) + 0.5 * x * (1.0 - t * t) * _GELU_C * (1.0 + 3.0 * _GELU_A * x * x)


def _softplus(x):
    return jnp.maximum(x, 0.0) + jnp.log(1.0 + jnp.exp(-jnp.abs(x)))


def _mm_nn(name, a, b, out_dtype, tm, tn, res=None):
    M, K = a.shape
    N = b.shape[1]
    tm, tn = min(tm, M), min(tn, N)

    def body(*refs):
        a_ref, b_ref = refs[0], refs[1]
        o_ref = refs[-1]
        acc = _nn(a_ref[...], b_ref[...])
        if res is not None:
            acc = acc + refs[2][...]
        o_ref[...] = acc.astype(o_ref.dtype)

    in_specs = [pl.BlockSpec((tm, K), lambda j, i: (i, 0)), pl.BlockSpec((K, tn), lambda j, i: (0, j))]
    args = [a, b]
    if res is not None:
        in_specs.append(pl.BlockSpec((tm, tn), lambda j, i: (i, j)))
        args.append(res)
    return pl.pallas_call(
        body, name=name, grid=(N // tn, M // tm), in_specs=in_specs,
        out_specs=pl.BlockSpec((tm, tn), lambda j, i: (i, j)),
        out_shape=jax.ShapeDtypeStruct((M, N), out_dtype),
        compiler_params=_cp("parallel", "parallel"))(*args)


def _with_dep(in_specs, args, dep):
    if dep is None:
        return in_specs, args
    return in_specs + [ANY], args + [dep]


def _mm_nt(name, a, b, out_dtype, tm, tn, dep=None):
    M, K = a.shape
    N = b.shape[0]
    tm, tn = min(tm, M), min(tn, N)

    def body(a_ref, b_ref, *rest):
        o_ref = rest[-1]
        o_ref[...] = _nt(a_ref[...], b_ref[...]).astype(o_ref.dtype)

    in_specs, args = _with_dep(
        [pl.BlockSpec((tm, K), lambda i, j: (i, 0)), pl.BlockSpec((tn, K), lambda i, j: (j, 0))], [a, b], dep)
    return pl.pallas_call(
        body, name=name, grid=(M // tm, N // tn), in_specs=in_specs,
        out_specs=pl.BlockSpec((tm, tn), lambda i, j: (i, j)),
        out_shape=jax.ShapeDtypeStruct((M, N), out_dtype),
        compiler_params=_cp("parallel", "parallel"))(*args)


def _mm_tn(name, a, b, tm, tn, tk, col_major_tiles=False):
    T, M = a.shape
    N = b.shape[1]
    tm, tn, tk = min(tm, M), min(tn, N), min(tk, T)
    nk = T // tk

    def body(a_ref, b_ref, o_ref, acc_ref):
        k = pl.program_id(2)

        @pl.when(k == 0)
        def _():
            acc_ref[...] = jnp.zeros_like(acc_ref)
        acc_ref[...] += _tn(a_ref[...], b_ref[...])

        @pl.when(k == nk - 1)
        def _():
            o_ref[...] = acc_ref[...].astype(BF16).reshape(o_ref.shape)

    if col_major_tiles:
        assert tm == M
        out_spec = pl.BlockSpec((1, tm, tn), lambda i, j, k: (j, 0, 0))
        out_shape = jax.ShapeDtypeStruct((N // tn, M, tn), BF16)
    else:
        out_spec = pl.BlockSpec((tm, tn), lambda i, j, k: (i, j))
        out_shape = jax.ShapeDtypeStruct((M, N), BF16)
    return pl.pallas_call(
        body, name=name, grid=(M // tm, N // tn, nk),
        in_specs=[pl.BlockSpec((tk, tm), lambda i, j, k: (k, i)), pl.BlockSpec((tk, tn), lambda i, j, k: (k, j))],
        out_specs=out_spec, out_shape=out_shape, scratch_shapes=[pltpu.VMEM((tm, tn), F32)],
        compiler_params=_cp("parallel", "parallel", "arbitrary"))(a, b)


def _rms_fwd(name, x, g, rb=512, dep=None):
    T, Dm = x.shape
    rb = min(rb, T)

    def body(x_ref, g_ref, *rest):
        h_ref = rest[-1]
        xv = x_ref[...]
        r = lax.rsqrt(jnp.mean(xv * xv, axis=-1, keepdims=True) + EPS)
        h_ref[...] = (xv * r * g_ref[...]).astype(BF16)

    in_specs, args = _with_dep(
        [pl.BlockSpec((rb, Dm), lambda i: (i, 0)), pl.BlockSpec((1, Dm), lambda i: (0, 0))], [x, g], dep)
    return pl.pallas_call(
        body, name=name, grid=(T // rb,), in_specs=in_specs,
        out_specs=pl.BlockSpec((rb, Dm), lambda i: (i, 0)),
        out_shape=jax.ShapeDtypeStruct((T, Dm), BF16), compiler_params=_cp("parallel"))(*args)


def _rms_bwd(name, dh, x, g, dres, rb=512):
    T, Dm = x.shape
    rb = min(rb, T)

    def body(dh_ref, x_ref, g_ref, dres_ref, dx_ref, gg_ref):
        @pl.when(pl.program_id(0) == 0)
        def _():
            gg_ref[...] = jnp.zeros_like(gg_ref)
        xv = x_ref[...]
        r = lax.rsqrt(jnp.mean(xv * xv, axis=-1, keepdims=True) + EPS)
        xh = xv * r
        dhv = dh_ref[...]
        gg_ref[...] += jnp.sum(dhv * xh, axis=0, keepdims=True)
        dxh = dhv * g_ref[...]
        dx_ref[...] = dres_ref[...] + r * (dxh - xh * jnp.mean(dxh * xh, axis=-1, keepdims=True))

    row = pl.BlockSpec((rb, Dm), lambda i: (i, 0))
    vec = pl.BlockSpec((1, Dm), lambda i: (0, 0))
    return pl.pallas_call(
        body, name=name, grid=(T // rb,), in_specs=[row, row, vec, row], out_specs=[row, vec],
        out_shape=[jax.ShapeDtypeStruct((T, Dm), F32), jax.ShapeDtypeStruct((1, Dm), F32)],
        compiler_params=_cp("arbitrary"))(dh, x, g, dres)


def _loss_head(x3, tgt, g, rb=512):
    T, Dm = x3.shape
    rb = min(rb, T)

    def body(x_ref, t_ref, g_ref, loss_ref, dx_ref, gg_ref):
        @pl.when(pl.program_id(0) == 0)
        def _():
            gg_ref[...] = jnp.zeros_like(gg_ref)
            loss_ref[...] = jnp.zeros_like(loss_ref)
        xv = x_ref[...]
        r = lax.rsqrt(jnp.mean(xv * xv, axis=-1, keepdims=True) + EPS)
        xh = xv * r
        e = xh * g_ref[...] - t_ref[...]
        loss_ref[...] += jnp.zeros_like(loss_ref) + (0.5 / Dm) * jnp.sum(e * e)
        dy = e * (1.0 / Dm)
        gg_ref[...] += jnp.sum(dy * xh, axis=0, keepdims=True)
        dxh = dy * g_ref[...]
        dx_ref[...] = r * (dxh - xh * jnp.mean(dxh * xh, axis=-1, keepdims=True))

    row = pl.BlockSpec((rb, Dm), lambda i: (i, 0))
    vec = pl.BlockSpec((1, Dm), lambda i: (0, 0))
    return pl.pallas_call(
        body, name="loss_head", grid=(T // rb,), in_specs=[row, row, vec],
        out_specs=[pl.BlockSpec((1, 128), lambda i: (0, 0)), row, vec],
        out_shape=[jax.ShapeDtypeStruct((1, 128), F32), jax.ShapeDtypeStruct((T, Dm), F32),
                   jax.ShapeDtypeStruct((1, Dm), F32)],
        compiler_params=_cp("arbitrary"))(x3, tgt, g)


def _halo_prev_spec(rb, width):
    return pl.BlockSpec((8, width), lambda i: (jnp.maximum(i * (rb // 8) - 1, 0), 0))


def _halo_next_spec(rb, width, T):
    return pl.BlockSpec((8, width), lambda i: (jnp.minimum((i + 1) * (rb // 8), T // 8 - 1), 0))


LANES = 128
FF_STRIPS = D_FF // LANES
ROW_CHUNK = 32


def _strip(j, base=0):
    return pl.ds(pl.multiple_of(base + j * LANES, LANES), LANES)


def _ffn_act(up, w, b, rb=256):
    T, W = up.shape
    rb = min(rb, T)

    def body(up_ref, halo_ref, w_ref, b_ref, act_ref, ext_scr):
        first = pl.program_id(0) == 0

        def strip(j, slot):
            halves = (_strip(j), _strip(j, D_FF))
            wv = [w_ref[:, cols] for cols in halves]
            bv = [b_ref[:, cols] for cols in halves]
            for h, cols in enumerate(halves):
                ext_scr[slot, h,0:8] = jnp.where(first, 0.0, halo_ref[:, cols])
                ext_scr[slot, h,8:] = up_ref[:, cols]
            for r0 in range(0, rb, ROW_CHUNK):
                n = min(ROW_CHUNK, rb - r0)
                c = [ext_scr[slot, h,6 + r0:6 + r0 + n] * wv[h][0:1] + ext_scr[slot, h,7 + r0:7 + r0 + n] * wv[h][1:2]
                     + ext_scr[slot, h,8 + r0:8 + r0 + n] * wv[h][2:3] + bv[h] for h in range(2)]
                act_ref[r0:r0 + n, halves[0]] = (_silu(c[0]) * c[1]).astype(BF16)

        def pair(jj, carry):
            strip(2 * jj, 0)
            strip(2 * jj + 1, 1)
            return carry

        lax.fori_loop(0, FF_STRIPS // 2, pair, 0)

    return pl.pallas_call(
        body, name="ffn_act", grid=(T // rb,),
        in_specs=[pl.BlockSpec((rb, W), lambda i: (i, 0)), _halo_prev_spec(rb, W),
                  pl.BlockSpec((3, W), lambda i: (0, 0)), pl.BlockSpec((1, W), lambda i: (0, 0))],
        out_specs=pl.BlockSpec((rb, D_FF), lambda i: (i, 0)),
        out_shape=jax.ShapeDtypeStruct((T, D_FF), BF16),
        scratch_shapes=[pltpu.VMEM((2, 2, rb + 8, LANES), F32)], compiler_params=_cp("parallel"))(up, up, w, b)


def _ffn_act_bwd(up, dact, w, b, rb=128, dep=None):
    T, W = up.shape
    rb = min(rb, T)
    nb = T // rb
    re = rb + 8

    def body(up_ref, prev_ref, next_ref, da_ref, danext_ref, w_ref, b_ref, *rest):
        dup_ref, gw_ref, gb_ref, ext_scr, dc_scr = rest[-5:]
        i = pl.program_id(0)

        @pl.when(i == 0)
        def _():
            gw_ref[...] = jnp.zeros_like(gw_ref)
            gb_ref[...] = jnp.zeros_like(gb_ref)
        last = i == nb - 1

        def fold8(a):
            return jnp.sum(a.reshape(a.shape[0] // 8, 8, LANES), axis=0)

        def strip(j, slot):
            halves = (_strip(j), _strip(j, D_FF))
            wv = [w_ref[:, cols] for cols in halves]
            bv = [b_ref[:, cols] for cols in halves]
            for h, cols in enumerate(halves):
                ext_scr[slot, h,0:8] = jnp.where(i > 0, prev_ref[:, cols], 0.0)
                ext_scr[slot, h,8:8 + rb] = up_ref[:, cols]
                ext_scr[slot, h,8 + rb:] = next_ref[:, cols]
            gb = [jnp.zeros((8, LANES), F32) for _ in range(2)]
            gw = [[jnp.zeros((8, LANES), F32) for _ in range(3)] for _ in range(2)]
            for r0 in range(0, re, ROW_CHUNK):
                n = min(ROW_CHUNK, re - r0)
                tp = [[ext_scr[slot, h,6 + k + r0:6 + k + r0 + n] for k in range(3)] for h in range(2)]
                c = [tp[h][0] * wv[h][0:1] + tp[h][1] * wv[h][1:2] + tp[h][2] * wv[h][2:3] + bv[h] for h in range(2)]
                if r0 < rb:
                    da = da_ref[r0:r0 + n, halves[0]]
                else:
                    da = jnp.where(last, 0.0, danext_ref[:, halves[0]])
                s = _sigmoid(c[0])
                gs = c[0] * s
                dcs = (da * c[1] * (s + gs * (1.0 - s)), da * gs)
                for h in range(2):
                    dc_scr[slot, h,r0:r0 + n] = dcs[h]
                    if r0 < rb:
                        gb[h] = gb[h] + fold8(dcs[h])
                        for k in range(3):
                            gw[h][k] = gw[h][k] + fold8(tp[h][k] * dcs[h])
            for r0 in range(0, rb, ROW_CHUNK):
                n = min(ROW_CHUNK, rb - r0)
                for h, cols in enumerate(halves):
                    dup = (dc_scr[slot, h,r0:r0 + n] * wv[h][2:3] + dc_scr[slot, h,r0 + 1:r0 + 1 + n] * wv[h][1:2]
                           + dc_scr[slot, h,r0 + 2:r0 + 2 + n] * wv[h][0:1])
                    dup_ref[r0:r0 + n, cols] = dup.astype(BF16)
            for h, cols in enumerate(halves):
                gb_ref[:, cols] += jnp.sum(gb[h], axis=0, keepdims=True)
                for k in range(3):
                    gw_ref[k:k + 1, cols] += jnp.sum(gw[h][k], axis=0, keepdims=True)

        def pair(jj, carry):
            strip(2 * jj, 0)
            strip(2 * jj + 1, 1)
            return carry

        lax.fori_loop(0, FF_STRIPS // 2, pair, 0)

    in_specs, args = _with_dep(
        [pl.BlockSpec((rb, W), lambda i: (i, 0)), _halo_prev_spec(rb, W), _halo_next_spec(rb, W, T),
         pl.BlockSpec((rb, D_FF), lambda i: (i, 0)), _halo_next_spec(rb, D_FF, T),
         pl.BlockSpec((3, W), lambda i: (0, 0)), pl.BlockSpec((1, W), lambda i: (0, 0))],
        [up, up, up, dact, dact, w, b], dep)
    return pl.pallas_call(
        body, name="ffn_act_bwd", grid=(nb,), in_specs=in_specs,
        out_specs=[pl.BlockSpec((rb, W), lambda i: (i, 0)), pl.BlockSpec((3, W), lambda i: (0, 0)),
                   pl.BlockSpec((1, W), lambda i: (0, 0))],
        out_shape=[jax.ShapeDtypeStruct((T, W), BF16), jax.ShapeDtypeStruct((3, W), F32),
                   jax.ShapeDtypeStruct((1, W), F32)],
        scratch_shapes=[pltpu.VMEM((2, 2, rb + 16, LANES), F32), pltpu.VMEM((2, 2, re, LANES), F32)],
        compiler_params=_cp("arbitrary"))(*args)


def _lane_iota(shape):
    return lax.broadcasted_iota(jnp.int32, shape, len(shape) - 1)


def _dn_act(p, conv_w, alog_row, dtb_row, rb=256):
    T = p.shape[0]
    rb = min(rb, T)
    W3 = 3 * DN_WIDTH

    def body(p_ref, halo_ref, ba_ref, w_ref, al_ref, dt_ref, q_ref, k_ref, v_ref, bg_ref, ext_scr):
        first = pl.program_id(0) == 0
        outs = (q_ref, k_ref, v_ref)
        for j in range(3 * N_HEADS):
            kind, h = divmod(j, N_HEADS)
            cols = slice(j * HEAD_DIM, (j + 1) * HEAD_DIM)
            cur = p_ref[:, cols]
            ext_scr[j, 0:8] = jnp.where(first, 0.0, halo_ref[:, cols])
            ext_scr[j, 8:] = cur
            wv = w_ref[:, cols]
            s = _silu(ext_scr[j, 5:5 + rb] * wv[0:1] + ext_scr[j, 6:6 + rb] * wv[1:2]
                      + ext_scr[j, 7:7 + rb] * wv[2:3] + cur * wv[3:4])
            if kind < 2:
                scale = HEAD_DIM ** -0.5 if kind == 0 else 1.0
                s = s * (lax.rsqrt(jnp.sum(s * s, axis=-1, keepdims=True) + EPS) * scale)
            outs[kind][:, h * HEAD_DIM:(h + 1) * HEAD_DIM] = s
        ba = ba_ref[...]
        lane = _lane_iota(ba.shape)
        beta = _sigmoid(ba)
        g = -jnp.exp(al_ref[...]) * _softplus(ba + dt_ref[...])
        bg_ref[...] = jnp.where(lane < N_HEADS, beta, jnp.where(lane < 2 * N_HEADS, g, 0.0))

    row512 = pl.BlockSpec((rb, DN_WIDTH), lambda i: (i, 0))
    row128 = pl.BlockSpec((rb, 128), lambda i: (i, 0))
    vec128 = pl.BlockSpec((1, 128), lambda i: (0, 0))
    return pl.pallas_call(
        body, name="dn_act", grid=(T // rb,),
        in_specs=[pl.BlockSpec((rb, W3), lambda i: (i, 0)), _halo_prev_spec(rb, W3),
                  pl.BlockSpec((rb, 128), lambda i: (i, BA_COL // 128)),
                  pl.BlockSpec((4, W3), lambda i: (0, 0)), vec128, vec128],
        out_specs=[row512, row512, row512, row128],
        out_shape=[jax.ShapeDtypeStruct((T, DN_WIDTH), F32)] * 3 + [jax.ShapeDtypeStruct((T, 128), F32)],
        scratch_shapes=[pltpu.VMEM((3 * N_HEADS, rb + 8, HEAD_DIM), F32)],
        compiler_params=_cp("parallel"))(p, p, p, conv_w, alog_row, dtb_row)


def _dn_act_bwd(p, conv_w, alog_row, dtb_row, dq, dk, dv, dbg, rb=256):
    T = p.shape[0]
    rb = min(rb, T)
    nb = T // rb
    re = rb + 8
    W3 = 3 * DN_WIDTH

    def body(p_ref, prev_ref, next_ref, ba_ref, w_ref, al_ref, dt_ref, dq_ref, dqn_ref, dk_ref, dkn_ref,
             dv_ref, dvn_ref, dbg_ref, draw_ref, dba_ref, gw_ref, gad_ref, ext_scr, dc_scr):
        i = pl.program_id(0)

        @pl.when(i == 0)
        def _():
            gw_ref[...] = jnp.zeros_like(gw_ref)
            gad_ref[...] = jnp.zeros_like(gad_ref)
        row = lax.broadcasted_iota(jnp.int32, (re, 1), 0)
        live = (row < rb) | (i < nb - 1)
        d_refs = ((dq_ref, dqn_ref), (dk_ref, dkn_ref), (dv_ref, dvn_ref))
        for j in range(3 * N_HEADS):
            kind, h = divmod(j, N_HEADS)
            cols = slice(j * HEAD_DIM, (j + 1) * HEAD_DIM)
            hcols = slice(h * HEAD_DIM, (h + 1) * HEAD_DIM)
            ext_scr[j, 0:8] = jnp.where(i > 0, prev_ref[:, cols], 0.0)
            ext_scr[j, 8:8 + rb] = p_ref[:, cols]
            ext_scr[j, 8 + rb:] = next_ref[:, cols]
            tp = [ext_scr[j, 5 + k:5 + k + re] for k in range(4)]
            wv = w_ref[:, cols]
            c = tp[0] * wv[0:1] + tp[1] * wv[1:2] + tp[2] * wv[2:3] + tp[3] * wv[3:4]
            sg = _sigmoid(c)
            s = c * sg
            d_in = jnp.where(live, jnp.concatenate([d_refs[kind][0][:, hcols], d_refs[kind][1][:, hcols]], axis=0), 0.0)
            if kind < 2:
                scale = HEAD_DIM ** -0.5 if kind == 0 else 1.0
                n = lax.rsqrt(jnp.sum(s * s, axis=-1, keepdims=True) + EPS)
                hat = s * n
                d_in = (n * scale) * (d_in - hat * jnp.sum(hat * d_in, axis=-1, keepdims=True))
            dc = d_in * (sg + s * (1.0 - sg))
            dc_scr[j] = dc
            dcc = dc[0:rb]
            draw = (dcc * wv[3:4] + dc_scr[j, 1:1 + rb] * wv[2:3] + dc_scr[j, 2:2 + rb] * wv[1:2]
                    + dc_scr[j, 3:3 + rb] * wv[0:1])
            draw_ref[:, cols] = draw.astype(BF16)
            for k in range(4):
                gw_ref[k:k + 1, cols] += jnp.sum(tp[k][0:rb] * dcc, axis=0, keepdims=True)
        ba = ba_ref[...]
        dbg = dbg_ref[...]
        lane = _lane_iota(ba.shape)
        beta = _sigmoid(ba)
        ea = jnp.exp(al_ref[...])
        z = ba + dt_ref[...]
        d_a = dbg * (-ea) * _sigmoid(z)
        dba = jnp.where(lane < N_HEADS, dbg * beta * (1.0 - beta), jnp.where(lane < 2 * N_HEADS, d_a, 0.0))
        dba_ref[...] = dba.astype(BF16)
        isg = (lane >= N_HEADS) & (lane < 2 * N_HEADS)
        g = -ea * _softplus(z)
        gad_ref[0:1, :] += jnp.sum(jnp.where(isg, dbg * g, 0.0), axis=0, keepdims=True)
        gad_ref[1:2, :] += jnp.sum(jnp.where(isg, d_a, 0.0), axis=0, keepdims=True)

    row512 = pl.BlockSpec((rb, DN_WIDTH), lambda i: (i, 0))
    row128 = pl.BlockSpec((rb, 128), lambda i: (i, 0))
    vec128 = pl.BlockSpec((1, 128), lambda i: (0, 0))
    next512 = _halo_next_spec(rb, DN_WIDTH, T)
    return pl.pallas_call(
        body, name="dn_act_bwd", grid=(nb,),
        in_specs=[pl.BlockSpec((rb, W3), lambda i: (i, 0)), _halo_prev_spec(rb, W3), _halo_next_spec(rb, W3, T),
                  pl.BlockSpec((rb, 128), lambda i: (i, BA_COL // 128)),
                  pl.BlockSpec((4, W3), lambda i: (0, 0)), vec128, vec128,
                  row512, next512, row512, next512, row512, next512, row128],
        out_specs=[pl.BlockSpec((rb, W3), lambda i: (i, 0)), row128,
                   pl.BlockSpec((4, W3), lambda i: (0, 0)), pl.BlockSpec((2, 128), lambda i: (0, 0))],
        out_shape=[jax.ShapeDtypeStruct((T, W3), BF16), jax.ShapeDtypeStruct((T, 128), BF16),
                   jax.ShapeDtypeStruct((4, W3), F32), jax.ShapeDtypeStruct((2, 128), F32)],
        scratch_shapes=[pltpu.VMEM((3 * N_HEADS, rb + 16, HEAD_DIM), F32), pltpu.VMEM((3 * N_HEADS, re, HEAD_DIM), F32)],
        compiler_params=_cp("arbitrary"))(p, p, p, p, conv_w, alog_row, dtb_row, dq, dq, dk, dk, dv, dv, dbg)


def _tri(incl):
    ii = lax.broadcasted_iota(jnp.int32, (CHUNK, CHUNK), 0)
    jj = lax.broadcasted_iota(jnp.int32, (CHUNK, CHUNK), 1)
    return ii, jj, ((ii >= jj) if incl else (ii > jj))


def _dn_chunk(k, bg, cb=4):
    T = k.shape[0]
    N = T // CHUNK
    cb = min(cb, N)

    def body(k_ref, bg_ref, gc_ref, gct_ref, l_ref):
        ii, jj, incl = _tri(True)
        tri = incl.astype(F32)
        U = range(cb)
        bgv = [bg_ref[u * CHUNK:(u + 1) * CHUNK, :] for u in U]
        gc = [jnp.dot(tri, bgv[u], precision=lax.Precision.HIGHEST, preferred_element_type=F32) for u in U]
        gct = [gc[u].T for u in U]
        kk = [[None] * N_HEADS for _ in U]
        for u in U:
            gc_ref[u * CHUNK:(u + 1) * CHUNK, :] = gc[u]
            gct_ref[u] = gct[u][0:8]
            for h in range(N_HEADS):
                kh = k_ref[u * CHUNK:(u + 1) * CHUNK, h * HEAD_DIM:(h + 1) * HEAD_DIM]
                kk[u][h] = _nt(kh * bgv[u][:, h:h + 1], kh)
        for u in U:
            for h in range(N_HEADS):
                gcol = gc[u][:, N_HEADS + h:N_HEADS + h + 1]
                grow = gct[u][N_HEADS + h:N_HEADS + h + 1, :]
                l_ref[u, h] = kk[u][h] * jnp.exp(jnp.where(ii > jj, gcol - grow, NEG))

    rows = cb * CHUNK
    return pl.pallas_call(
        body, name="dn_chunk", grid=(N // cb,),
        in_specs=[pl.BlockSpec((rows, DN_WIDTH), lambda n: (n, 0)), pl.BlockSpec((rows, 128), lambda n: (n, 0))],
        out_specs=[pl.BlockSpec((rows, 128), lambda n: (n, 0)), pl.BlockSpec((cb, 8, CHUNK), lambda n: (n, 0, 0)),
                   pl.BlockSpec((cb, N_HEADS, CHUNK, CHUNK), lambda n: (n, 0, 0, 0))],
        out_shape=[jax.ShapeDtypeStruct((T, 128), F32), jax.ShapeDtypeStruct((N, 8, CHUNK), F32),
                   jax.ShapeDtypeStruct((N, N_HEADS, CHUNK, CHUNK), F32)],
        compiler_params=_cp("parallel"))(k, bg)


def _tri_inv(lt):
    S = lt.shape[1]

    def body(l_ref, a_ref):
        col = lax.broadcasted_iota(jnp.int32, (CHUNK, S), 0)
        for i in range(CHUNK):
            def step(j, acc):
                return acc - l_ref[pl.ds(i * CHUNK + j, 1), :] * a_ref[j]
            a_ref[i] = lax.fori_loop(0, i, step, (col == i).astype(F32))

    return pl.pallas_call(
        body, name="tri_inv", out_shape=jax.ShapeDtypeStruct((CHUNK, CHUNK, S), F32),
        compiler_params=pltpu.CompilerParams(vmem_limit_bytes=VMEM_LIMIT))(lt)


def _dn_head_terms(qh, kh, vh, beta, gcol, grow):
    ii, jj, incl = _tri(True)
    gam = jnp.exp(jnp.where(incl, gcol - grow, NEG))
    glast = grow[:, CHUNK - 1:CHUNK]
    E = jnp.exp(gcol)
    Fd = jnp.exp(glast - gcol)
    cd = jnp.exp(glast)
    kb = kh * beta
    return dict(ii=ii, jj=jj, gam=gam, E=E, F=Fd, cd=cd, kb=kb, vb=vh * beta, W=kb * E, qE=qh * E, kt=kh * Fd)


def _apply_a(a, u):
    hi, lo = _split(a)
    ub = _bf(u)
    return jnp.dot(hi, ub, preferred_element_type=F32) + jnp.dot(lo, ub, preferred_element_type=F32)


def _apply_at(a, u):
    hi, lo = _split(a)
    ub = _bf(u)
    dn = (((0,), (0,)), ((), ()))
    return (lax.dot_general(hi, ub, dn, preferred_element_type=F32)
            + lax.dot_general(lo, ub, dn, preferred_element_type=F32))


def _dn_scan(q, k, v, bg, gc, gct, a):
    T = q.shape[0]
    N = T // CHUNK

    def body(q_ref, k_ref, v_ref, bg_ref, gc_ref, gct_ref, a_ref, o_ref, sall_ref, s_ref):
        @pl.when(pl.program_id(0) == 0)
        def _():
            s_ref[...] = jnp.zeros_like(s_ref)
        bgv, gcv, gctv = bg_ref[...], gc_ref[...], gct_ref[0]
        H = range(N_HEADS)
        sl = [slice(h * HEAD_DIM, (h + 1) * HEAD_DIM) for h in H]
        q_, k_ = [q_ref[:, s] for s in sl], [k_ref[:, s] for s in sl]
        t = [_dn_head_terms(q_[h], k_[h], v_ref[:, sl[h]], bgv[:, h:h + 1],
                            gcv[:, N_HEADS + h:N_HEADS + h + 1], gctv[N_HEADS + h:N_HEADS + h + 1, :]) for h in H]
        S = [s_ref[h] for h in H]
        for h in H:
            sall_ref[0, h] = S[h]
        WS = [_nn(t[h]["W"], S[h]) for h in H]
        QK = [_nt(q_[h], k_[h]) for h in H]
        qS = [_nn(t[h]["qE"], S[h]) for h in H]
        vn = [_apply_a(a_ref[0, h], t[h]["vb"] - WS[h]) for h in H]
        Pv = [_nn(QK[h] * t[h]["gam"], vn[h]) for h in H]
        kv = [_tn(t[h]["kt"], vn[h]) for h in H]
        for h in H:
            o_ref[:, sl[h]] = qS[h] + Pv[h]
            s_ref[h] = t[h]["cd"] * S[h] + kv[h]

    row512 = pl.BlockSpec((CHUNK, DN_WIDTH), lambda n: (n, 0))
    row128 = pl.BlockSpec((CHUNK, 128), lambda n: (n, 0))
    return pl.pallas_call(
        body, name="dn_scan", grid=(N,),
        in_specs=[row512, row512, row512, row128, row128, pl.BlockSpec((1, 8, CHUNK), lambda n: (n, 0, 0)),
                  pl.BlockSpec((1, N_HEADS, CHUNK, CHUNK), lambda n: (n, 0, 0, 0))],
        out_specs=[row512, pl.BlockSpec((1, N_HEADS, HEAD_DIM, HEAD_DIM), lambda n: (n, 0, 0, 0))],
        out_shape=[jax.ShapeDtypeStruct((T, DN_WIDTH), F32),
                   jax.ShapeDtypeStruct((N, N_HEADS, HEAD_DIM, HEAD_DIM), F32)],
        scratch_shapes=[pltpu.VMEM((N_HEADS, HEAD_DIM, HEAD_DIM), F32)],
        compiler_params=_cp("arbitrary"))(q, k, v, bg, gc, gct, a)


def _dn_scan_bwd(q, k, v, bg, gc, gct, a, sall, do, dep=None):
    T = q.shape[0]
    N = T // CHUNK

    def body(q_ref, k_ref, v_ref, bg_ref, gc_ref, gct_ref, a_ref, sall_ref, do_ref, *rest):
        dq_ref, dk_ref, dv_ref, dbg_ref, ds_ref = rest[-5:]
        @pl.when(pl.program_id(0) == 0)
        def _():
            ds_ref[...] = jnp.zeros_like(ds_ref)
        bgv, gcv, gctv = bg_ref[...], gc_ref[...], gct_ref[0]
        lane = _lane_iota((CHUNK, 128))
        rowi = lax.broadcasted_iota(jnp.int32, (CHUNK, 1), 0)
        H = range(N_HEADS)
        sl = [slice(h * HEAD_DIM, (h + 1) * HEAD_DIM) for h in H]
        q_, k_, v_ = [q_ref[:, s] for s in sl], [k_ref[:, s] for s in sl], [v_ref[:, s] for s in sl]
        dO = [do_ref[:, s] for s in sl]
        beta = [bgv[:, h:h + 1] for h in H]
        t = [_dn_head_terms(q_[h], k_[h], v_[h], beta[h], gcv[:, N_HEADS + h:N_HEADS + h + 1],
                            gctv[N_HEADS + h:N_HEADS + h + 1, :]) for h in H]
        ii, jj = t[0]["ii"], t[0]["jj"]
        gam, E, Fd, cd, kb = ([t[h][n] for h in H] for n in ("gam", "E", "F", "cd", "kb"))
        S = [sall_ref[0, h] for h in H]
        dSn = [ds_ref[h] for h in H]
        A = [a_ref[0, h] for h in H]
        WS = [_nn(t[h]["W"], S[h]) for h in H]
        KK = [_nt(kb[h], k_[h]) for h in H]
        QK = [_nt(q_[h], k_[h]) for h in H]
        ktdS = [_nn(t[h]["kt"], dSn[h]) for h in H]
        d_qE = [_nt(dO[h], S[h]) for h in H]
        vn = [_apply_a(A[h], t[h]["vb"] - WS[h]) for h in H]
        PtdO = [_tn(QK[h] * gam[h], dO[h]) for h in H]
        qEdO = [_tn(t[h]["qE"], dO[h]) for h in H]
        dU = [_apply_at(A[h], PtdO[h] + ktdS[h]) for h in H]
        d_kt = [_nt(vn[h], dSn[h]) for h in H]
        dOvn = [_nt(dO[h], vn[h]) for h in H]
        dUvn = [_nt(dU[h], vn[h]) for h in H]
        dUS = [_nt(dU[h], S[h]) for h in H]
        WdU = [_tn(t[h]["W"], dU[h]) for h in H]
        for h in H:
            ds_ref[h] = cd[h] * dSn[h] + qEdO[h] - WdU[h]
        dQK = [jnp.where(ii >= jj, dOvn[h], 0.0) * gam[h] for h in H]
        dKK = [jnp.where(ii > jj, -dUvn[h], 0.0) * gam[h] for h in H]
        dQKk = [_nn(dQK[h], k_[h]) for h in H]
        dKKk = [_nn(dKK[h], k_[h]) for h in H]
        dQKq = [_tn(dQK[h], q_[h]) for h in H]
        dKKkb = [_tn(dKK[h], kb[h]) for h in H]
        dbeta_arr = jnp.zeros((CHUNK, 128), F32)
        dgc_arr = jnp.zeros((CHUNK, 128), F32)
        for h in H:
            dW = -dUS[h]
            dq_ref[:, sl[h]] = dQKk[h] + d_qE[h] * E[h]
            d_kb = dKKk[h] + dW * E[h]
            dk_ref[:, sl[h]] = dQKq[h] + dKKkb[h] + d_kb * beta[h] + d_kt[h] * Fd[h]
            dv_ref[:, sl[h]] = dU[h] * beta[h]
            Z = dQK[h] * QK[h] + dKK[h] * KK[h]
            d_cd = jnp.sum(S[h] * dSn[h])
            dbeta = jnp.sum(dU[h] * v_[h] + d_kb * k_[h], axis=-1, keepdims=True)
            dE = jnp.sum(dW * kb[h] + d_qE[h] * q_[h], axis=-1, keepdims=True)
            dFF = jnp.sum(d_kt[h] * k_[h], axis=-1, keepdims=True) * Fd[h]
            dgc = (dE * E[h] - dFF + jnp.sum(Z, axis=-1, keepdims=True) - jnp.sum(Z.T, axis=-1, keepdims=True)
                   + jnp.where(rowi == CHUNK - 1, jnp.sum(dFF) + d_cd * cd[h], 0.0))
            dbeta_arr = dbeta_arr + jnp.where(lane == h, dbeta, 0.0)
            dgc_arr = dgc_arr + jnp.where(lane == N_HEADS + h, dgc, 0.0)
        ii, jj, _ = _tri(True)
        rev = (jj >= ii).astype(F32)
        dbg_ref[...] = dbeta_arr + jnp.dot(rev, dgc_arr, precision=lax.Precision.HIGHEST,
                                           preferred_element_type=F32)

    row512 = pl.BlockSpec((CHUNK, DN_WIDTH), lambda n: (N - 1 - n, 0))
    row128 = pl.BlockSpec((CHUNK, 128), lambda n: (N - 1 - n, 0))
    in_specs, args = _with_dep(
        [row512, row512, row512, row128, row128,
         pl.BlockSpec((1, 8, CHUNK), lambda n: (N - 1 - n, 0, 0)),
         pl.BlockSpec((1, N_HEADS, CHUNK, CHUNK), lambda n: (N - 1 - n, 0, 0, 0)),
         pl.BlockSpec((1, N_HEADS, HEAD_DIM, HEAD_DIM), lambda n: (N - 1 - n, 0, 0, 0)), row512],
        [q, k, v, bg, gc, gct, a, sall, do], dep)
    return pl.pallas_call(
        body, name="dn_scan_bwd", grid=(N,), in_specs=in_specs,
        out_specs=[row512, row512, row512, row128],
        out_shape=[jax.ShapeDtypeStruct((T, DN_WIDTH), F32)] * 3 + [jax.ShapeDtypeStruct((T, 128), F32)],
        scratch_shapes=[pltpu.VMEM((N_HEADS, HEAD_DIM, HEAD_DIM), F32)],
        compiler_params=_cp("arbitrary"))(*args)


def _sg_mask():
    ii = lax.broadcasted_iota(jnp.int32, (SG_BLOCK, SG_BLOCK), 0) // CHUNK
    jj = lax.broadcasted_iota(jnp.int32, (SG_BLOCK, SG_BLOCK), 1) // CHUNK
    return jj <= ii


def _mix_fwd(o, p, ong, sgn, sgw, sgbt):
    T = o.shape[0]
    rb = SG_BLOCK

    def body(o_ref, gate_ref, u_ref, vg_ref, ong_ref, sgn_ref, sgw_ref, sgbt_ref, mix_ref):
        mask = _sg_mask()
        gate = gate_ref[...]
        for h in range(N_HEADS):
            sl = slice(h * HEAD_DIM, (h + 1) * HEAD_DIM)
            oh = o_ref[:, sl]
            r = lax.rsqrt(jnp.mean(oh * oh, axis=-1, keepdims=True) + EPS)
            mix_ref[:, sl] = (oh * r * ong_ref[...] * _silu(gate[:, sl])).astype(BF16)
        for gi in range(SG_GROUPS):
            sl = slice(gi * SG_BLOCK, (gi + 1) * SG_BLOCK)
            gv = _gelu(vg_ref[:, sl])
            r = lax.rsqrt(jnp.mean(gv * gv, axis=-1, keepdims=True) + EPS)
            vh = gv * r * sgn_ref[:, sl]
            s = _nn(jnp.where(mask, sgw_ref[gi], 0.0), vh) + sgbt_ref[:, gi:gi + 1]
            mix_ref[:, DN_WIDTH + gi * SG_BLOCK:DN_WIDTH + (gi + 1) * SG_BLOCK] = (_gelu(u_ref[:, sl]) * s).astype(BF16)

    def col(c):
        return pl.BlockSpec((rb, 512), lambda i: (i, c))
    return pl.pallas_call(
        body, name="mix_fwd", grid=(T // rb,),
        in_specs=[pl.BlockSpec((rb, DN_WIDTH), lambda i: (i, 0)), col(3), col(4), col(5),
                  pl.BlockSpec((1, 128), lambda i: (0, 0)), pl.BlockSpec((1, SG_WIDTH), lambda i: (0, 0)),
                  pl.BlockSpec((SG_GROUPS, SG_BLOCK, SG_BLOCK), lambda i: (0, 0, 0)),
                  pl.BlockSpec((SG_BLOCK, 128), lambda i: (0, 0))],
        out_specs=pl.BlockSpec((rb, D_MODEL), lambda i: (i, 0)),
        out_shape=jax.ShapeDtypeStruct((T, D_MODEL), BF16),
        compiler_params=_cp("parallel"))(o, p, p, p, ong, sgn, sgw, sgbt)


def _mix_bwd(o, p, ong, sgn, sgw, sgbt, dmix, dep=None):
    T = o.shape[0]
    rb = SG_BLOCK

    def body(o_ref, gate_ref, u_ref, vg_ref, ong_ref, sgn_ref, sgw_ref, sgbt_ref, dmix_ref, *rest):
        do_ref, dp_ref, gong_ref, gsgn_ref, gsgw_ref, gsgbt_ref = rest[-6:]
        @pl.when(pl.program_id(0) == 0)
        def _():
            gong_ref[...] = jnp.zeros_like(gong_ref)
            gsgn_ref[...] = jnp.zeros_like(gsgn_ref)
            gsgw_ref[...] = jnp.zeros_like(gsgw_ref)
            gsgbt_ref[...] = jnp.zeros_like(gsgbt_ref)
        mask = _sg_mask()
        gate = gate_ref[...]
        lane = _lane_iota((SG_BLOCK, 128))
        for h in range(N_HEADS):
            sl = slice(h * HEAD_DIM, (h + 1) * HEAD_DIM)
            oh = o_ref[:, sl]
            dm = dmix_ref[:, sl]
            r = lax.rsqrt(jnp.mean(oh * oh, axis=-1, keepdims=True) + EPS)
            oh_hat = oh * r
            gt = gate[:, sl]
            sg = _silu(gt)
            dp_ref[:, sl] = (dm * oh_hat * ong_ref[...] * _dsilu(gt)).astype(BF16)
            dn_ = dm * sg
            gong_ref[...] += jnp.sum(dn_ * oh_hat, axis=0, keepdims=True)
            dhat = dn_ * ong_ref[...]
            do_ref[:, sl] = r * (dhat - oh_hat * jnp.mean(dhat * oh_hat, axis=-1, keepdims=True))
        for gi in range(SG_GROUPS):
            sl = slice(gi * SG_BLOCK, (gi + 1) * SG_BLOCK)
            vraw = vg_ref[:, sl]
            gv = _gelu(vraw)
            r = lax.rsqrt(jnp.mean(gv * gv, axis=-1, keepdims=True) + EPS)
            vhat = gv * r
            vn = vhat * sgn_ref[:, sl]
            wm = jnp.where(mask, sgw_ref[gi], 0.0)
            s = _nn(wm, vn) + sgbt_ref[:, gi:gi + 1]
            uraw = u_ref[:, sl]
            dm = dmix_ref[:, DN_WIDTH + gi * SG_BLOCK:DN_WIDTH + (gi + 1) * SG_BLOCK]
            dp_ref[:, DN_WIDTH + gi * SG_BLOCK:DN_WIDTH + (gi + 1) * SG_BLOCK] = (dm * s * _dgelu(uraw)).astype(BF16)
            ds = dm * _gelu(uraw)
            gsgbt_ref[...] += jnp.where(lane == gi, jnp.sum(ds, axis=-1, keepdims=True), 0.0)
            gsgw_ref[gi] += jnp.where(mask, _nt(ds, vn), 0.0)
            dvn = _tn(wm, ds)
            gsgn_ref[:, sl] += jnp.sum(dvn * vhat, axis=0, keepdims=True)
            dhat = dvn * sgn_ref[:, sl]
            dgv = r * (dhat - vhat * jnp.mean(dhat * vhat, axis=-1, keepdims=True))
            dp_ref[:, 2 * DN_WIDTH + gi * SG_BLOCK:2 * DN_WIDTH + (gi + 1) * SG_BLOCK] = (dgv * _dgelu(vraw)).astype(BF16)

    def col(c):
        return pl.BlockSpec((rb, 512), lambda i: (i, c))
    full = lambda *s: pl.BlockSpec(s, lambda i: (0,) * len(s))
    in_specs, args = _with_dep(
        [pl.BlockSpec((rb, DN_WIDTH), lambda i: (i, 0)), col(3), col(4), col(5),
         full(1, 128), full(1, SG_WIDTH), full(SG_GROUPS, SG_BLOCK, SG_BLOCK), full(SG_BLOCK, 128),
         pl.BlockSpec((rb, D_MODEL), lambda i: (i, 0))],
        [o, p, p, p, ong, sgn, sgw, sgbt, dmix], dep)
    return pl.pallas_call(
        body, name="mix_bwd", grid=(T // rb,), in_specs=in_specs,
        out_specs=[pl.BlockSpec((rb, DN_WIDTH), lambda i: (i, 0)), pl.BlockSpec((rb, 3 * 512), lambda i: (i, 0)),
                   full(1, 128), full(1, SG_WIDTH), full(SG_GROUPS, SG_BLOCK, SG_BLOCK), full(SG_BLOCK, 128)],
        out_shape=[jax.ShapeDtypeStruct((T, DN_WIDTH), F32), jax.ShapeDtypeStruct((T, 3 * 512), BF16),
                   jax.ShapeDtypeStruct((1, 128), F32), jax.ShapeDtypeStruct((1, SG_WIDTH), F32),
                   jax.ShapeDtypeStruct((SG_GROUPS, SG_BLOCK, SG_BLOCK), F32),
                   jax.ShapeDtypeStruct((SG_BLOCK, 128), F32)],
        compiler_params=_cp("arbitrary"))(*args)


def _pad_lanes(row, offset=0):
    n = row.shape[1]
    return jnp.pad(row, ((0, 0), (offset, 128 - n - offset)))


def _local_step(x, tgt, w, dep=None, late_weights=None, on_grad=None):
    T = x.shape[0]
    N = T // CHUNK
    on_grad = on_grad or (lambda name, g: None)
    alog_row = _pad_lanes(w["dn_a_log"], N_HEADS)
    dtb_row = _pad_lanes(w["dn_dt_bias"], N_HEADS)
    sgbt = jnp.pad(w["sg_b"].T, ((0, 0), (0, 128 - SG_GROUPS)))

    h1 = _rms_fwd("rms_attn", x, w["attn_norm_g"], dep=dep)
    p = _mm_nn("in_proj", h1, w["w_in"], F32, 512, PROJ_PAD)
    q, k, v, bg = _dn_act(p, w["dn_conv_w"], alog_row, dtb_row)
    gc, gct, lmat = _dn_chunk(k, bg)
    lt = lmat.reshape(N * N_HEADS, CHUNK * CHUNK).T
    at = _tri_inv(lt)
    a = at.reshape(CHUNK * CHUNK, N * N_HEADS).T.reshape(N, N_HEADS, CHUNK, CHUNK)
    o, sall = _dn_scan(q, k, v, bg, gc, gct, a)
    mix = _mix_fwd(o, p, w["dn_out_norm_g"], w["sg_norm_g"], w["sg_w"], sgbt)
    if late_weights is not None:
        w = {**w, **late_weights(mix)}
    x2 = _mm_nn("out_proj", mix, w["w_out"], F32, 512, 1024, res=x)
    h2 = _rms_fwd("rms_ffn", x2, w["ffn_norm_g"])
    up = _mm_nn("up_proj", h2, w["w_up"], F32, 512, D_FF)
    act = _ffn_act(up, w["ffn_conv_w"], w["ffn_conv_b"])
    x3 = _mm_nn("down_proj", act, w["w_down"], F32, 512, 1024, res=x2)
    loss, dx3, g_final = _loss_head(x3, tgt, w["final_norm_g"])

    dact = _mm_nt("d_act", dx3, w["w_down"], F32, 512, D_FF)
    g_w_down = _mm_tn("g_w_down", act, dx3, D_FF, 1024, 512)
    tok = on_grad("w_down", g_w_down)
    dup, g_ffn_conv_w, g_ffn_conv_b = _ffn_act_bwd(up, dact, w["ffn_conv_w"], w["ffn_conv_b"], dep=tok)
    g_w_up = _mm_tn("g_w_up", h2, dup, 1024, 2 * D_FF // 4, 512, col_major_tiles=True)
    tok = on_grad("w_up", g_w_up)
    dh2 = _mm_nt("d_h2", dup, w["w_up"], F32, 512, 1024, dep=tok)
    dx2, g_ffn_norm = _rms_bwd("rms_ffn_bwd", dh2, x2, w["ffn_norm_g"], dx3)
    dmix = _mm_nt("d_mix", dx2, w["w_out"], F32, 512, 1024)
    g_w_out = _mm_tn("g_w_out", mix, dx2, 1024, 1024, 1024)
    tok = on_grad("w_out", g_w_out)
    do, dp_mid, g_ong, g_sgn, g_sgw, g_sgbt = _mix_bwd(o, p, w["dn_out_norm_g"], w["sg_norm_g"], w["sg_w"], sgbt,
                                                      dmix, dep=tok)
    early = dict(dn_out_norm_g=g_ong, sg_norm_g=g_sgn, sg_w=g_sgw, sg_b=g_sgbt[:, :SG_GROUPS].T,
                 ffn_norm_g=g_ffn_norm, ffn_conv_w=g_ffn_conv_w, ffn_conv_b=g_ffn_conv_b, final_norm_g=g_final)
    tok = on_grad("small_early", early)
    dq, dk, dv, dbg = _dn_scan_bwd(q, k, v, bg, gc, gct, a, sall, do, dep=tok)
    dp_qkv, dba, g_dn_conv_w, g_ad = _dn_act_bwd(p, w["dn_conv_w"], alog_row, dtb_row, dq, dk, dv, dbg)
    dp =jnp.concatenate([dp_qkv, dp_mid, dba], axis=1)
    g_w_in = _mm_tn("g_w_in", h1, dp, 1024, PROJ_PAD, 512)[:, :PROJ_COLS]
    tok = on_grad("w_in", g_w_in)
    dh1 = _mm_nt("d_h1", dp, w["w_in"], F32, 512, 1024, dep=tok)
    grad_x, g_attn_norm = _rms_bwd("rms_attn_bwd", dh1, x, w["attn_norm_g"], dx2)

    grads = dict(
        attn_norm_g=g_attn_norm, w_in=g_w_in, dn_conv_w=g_dn_conv_w,
        dn_a_log=g_ad[0:1, N_HEADS:2 * N_HEADS], dn_dt_bias=g_ad[1:2, N_HEADS:2 * N_HEADS],
        w_out=g_w_out, w_up=g_w_up, w_down=g_w_down, **early)
    return loss, grad_x, grads


def _me():
    return lax.axis_index("x"), lax.axis_index("y"), lax.axis_index("c")


def _peer(rel):
    x, y, c = _me()
    return {"x": (1 - x, y, c), "y": (x, 1 - y, c), "xy": (1 - x, 1 - y, c), "c": (x, y, 1 - c)}[rel]


def _chip_of(dev):
    return 2 * dev[0] + dev[1]


CHIP_RELS = ("x", "y", "xy")


def _run_copies(copies, sends, recvs):
    for cp in copies:
        cp.start()
    for cp in recvs:
        cp.wait_recv()
    for cp in sends:
        cp.wait_send()


def _gather_first(w_shard, small_shard):
    R = w_shard.shape[0]
    r2 = R // 2

    def body(w_ref, s_ref, w_out, s_out, send_sems, recv_sems, local_sems):
        x, y, c = _me()
        me = _chip_of((x, y))
        sib = _peer("c")

        def half(chip, core):
            return w_out.at[chip, pl.ds(pl.multiple_of(core * r2, 8), r2), :]

        def copy(k, src, dst, to):
            return pltpu.make_async_remote_copy(src_ref=src, dst_ref=dst, send_sem=send_sems.at[k],
                                                recv_sem=recv_sems.at[k], device_id=to, device_id_type=MESH)

        own_rows = w_ref.at[pl.ds(pl.multiple_of(c * r2, 8), r2), :]
        local = [pltpu.make_async_copy(w_ref, w_out.at[me], local_sems.at[0]),
                 pltpu.make_async_copy(s_ref, s_out.at[me], local_sems.at[1])]
        for cp in local:
            cp.start()
        first = [copy(r, own_rows, half(me, c), _peer(rel)) for r, rel in enumerate(CHIP_RELS)]
        first += [copy(3 + r, s_ref, s_out.at[me], _peer(rel)) for r, rel in enumerate(CHIP_RELS)]
        for cp in first:
            cp.start()
        passed = []
        for r, rel in enumerate(CHIP_RELS):
            their = _chip_of(_peer(rel))
            copy(r, own_rows, half(their, c), _peer(rel)).wait_recv()
            fwd = copy(6 + r, half(their, c), half(their, c), sib)
            fwd.start()
            passed.append(fwd)
        for r, rel in enumerate(CHIP_RELS):
            their = _chip_of(_peer(rel))
            copy(3 + r, s_ref, s_out.at[their], _peer(rel)).wait_recv()
            copy(6 + r, own_rows, half(their, 1 - c), sib).wait_recv()
        for cp in first + passed:
            cp.wait_send()
        for cp in local:
            cp.wait()

    return pl.pallas_call(
        body, name="gather_first", in_specs=[ANY, ANY], out_specs=[ANY, ANY],
        out_shape=[jax.ShapeDtypeStruct((4,) + w_shard.shape, w_shard.dtype),
                   jax.ShapeDtypeStruct((4,) + small_shard.shape, small_shard.dtype)],
        scratch_shapes=[pltpu.SemaphoreType.DMA((9,)), pltpu.SemaphoreType.DMA((9,)),
                        pltpu.SemaphoreType.DMA((2,))])(w_shard, small_shard)


def _place_own(name, shards, slots, index):
    n = len(shards)

    def body(idx_ref, *refs):
        src, out, sems = refs[:n], refs[n:2 * n], refs[2 * n]
        copies = [pltpu.make_async_copy(src[i], out[i].at[idx_ref[0]], sems.at[i]) for i in range(n)]
        for cp in copies:
            cp.start()
        for cp in copies:
            cp.wait()

    return pl.pallas_call(
        body, name=name,
        grid_spec=pltpu.PrefetchScalarGridSpec(num_scalar_prefetch=1, grid=(1,), in_specs=[ANY] * n,
                                               out_specs=[ANY] * n,
                                               scratch_shapes=[pltpu.SemaphoreType.DMA((n,))]),
        out_shape=[jax.ShapeDtypeStruct((slots,) + s.shape, s.dtype) for s in shards])(index, *shards)


OTHERS = tuple((fx, fy, fc) for fx in (0, 1) for fy in (0, 1) for fc in (0, 1) if (fx, fy, fc) != (0, 0, 0))


def _other(flip):
    x, y, c = _me()
    return (x ^ flip[0], y ^ flip[1], c ^ flip[2])


def _linear(dev):
    return 4 * dev[0] + 2 * dev[1] + dev[2]


def _exchange_small(small):
    def body(small_ref, out_ref, send_sems, recv_sems, local_sem):
        my_slot = _linear(_me())
        local = pltpu.make_async_copy(small_ref, out_ref.at[my_slot], local_sem)
        local.start()
        sends, recvs = [], []
        for k, flip in enumerate(OTHERS):
            peer = _other(flip)
            sends.append(pltpu.make_async_remote_copy(
                src_ref=small_ref, dst_ref=out_ref.at[my_slot], send_sem=send_sems.at[k], recv_sem=recv_sems.at[k],
                device_id=peer, device_id_type=MESH))
            recvs.append(pltpu.make_async_remote_copy(
                src_ref=small_ref, dst_ref=out_ref.at[_linear(peer)], send_sem=send_sems.at[k],
                recv_sem=recv_sems.at[k], device_id=peer, device_id_type=MESH))
        _run_copies(sends, sends, recvs)
        local.wait()

    return pl.pallas_call(
        body, name="exchange_small", in_specs=[ANY], out_specs=ANY,
        out_shape=jax.ShapeDtypeStruct((8,) + small.shape, small.dtype),
        scratch_shapes=[pltpu.SemaphoreType.DMA((7,)), pltpu.SemaphoreType.DMA((7,)), pltpu.SemaphoreType.DMA])(small)


def _pair_swap(halves):
    n = len(halves)

    def body(*refs):
        src, out = refs[:n], refs[n:2 * n]
        send_sems, recv_sems = refs[2 * n:]
        sib = _peer("c")
        copies = [pltpu.make_async_remote_copy(
            src_ref=src[i], dst_ref=out[i], send_sem=send_sems.at[i], recv_sem=recv_sems.at[i],
            device_id=sib, device_id_type=MESH) for i in range(n)]
        _run_copies(copies, copies, copies)

    return pl.pallas_call(
        body, name="pair_swap", in_specs=[ANY] * n, out_specs=[ANY] * n,
        out_shape=[jax.ShapeDtypeStruct(h.shape, h.dtype) for h in halves],
        scratch_shapes=[pltpu.SemaphoreType.DMA((n,)), pltpu.SemaphoreType.DMA((n,))])(*halves)


HBM = pl.BlockSpec(memory_space=pltpu.HBM)
SEM = pl.BlockSpec(memory_space=pltpu.SEMAPHORE)
EFFECT = pltpu.SideEffectType.DATAFLOW_SIDE_EFFECTING


def _hbm(a):
    return pltpu.with_memory_space_constraint(a, pltpu.HBM)


def _transfer_start(name, srcs, lands, n_copies, make_copies, after=None):
    n, m = len(srcs), len(lands)

    def body(*refs):
        src, land = refs[:n], refs[n:n + m]
        outs = refs[n + m + (after is not None):]
        send_sems, recv_sems, token = outs[0], outs[1], outs[-1]
        for cp in make_copies(src, land, send_sems, recv_sems):
            cp.start()
        token[...] = jnp.zeros_like(token)

    arrs = list(srcs) + list(lands)
    in_specs, args = _with_dep([HBM] * (n + m), [_hbm(a) for a in arrs], after)
    out = pl.pallas_call(
        body, name=name,
        out_shape=(pltpu.SemaphoreType.DMA((n_copies,)), pltpu.SemaphoreType.DMA((n_copies,)),
                   *[pltpu.HBM(a.shape, a.dtype) for a in arrs], jax.ShapeDtypeStruct((8, 128), F32)),
        in_specs=in_specs,
        out_specs=(SEM, SEM, *[HBM] * (n + m), pl.BlockSpec(memory_space=pltpu.VMEM)),
        input_output_aliases={i: 2 + i for i in range(n + m)},
        compiler_params=pltpu.CompilerParams(has_side_effects=EFFECT))(*args)
    return out[0], out[1], list(out[2:2 + n]), list(out[2 + n:2 + n + m]), out[-1]


def _transfer_wait(name, send_sems, recv_sems, srcs, lands, make_copies, after):
    n, m = len(srcs), len(lands)

    def body(*refs):
        src, land = refs[:n], refs[n:n + m]
        s_sems, r_sems = refs[n + m], refs[n + m + 1]
        for cp in make_copies(src, land, s_sems, r_sems):
            cp.wait_send()
            cp.wait_recv()

    arrs = list(srcs) + list(lands)
    out = pl.pallas_call(
        body, name=name, out_shape=tuple(pltpu.HBM(a.shape, a.dtype) for a in arrs),
        in_specs=[HBM] * (n + m) + [SEM, SEM, ANY], out_specs=tuple([HBM] * (n + m)),
        input_output_aliases={i: i for i in range(n + m)},
        compiler_params=pltpu.CompilerParams(has_side_effects=EFFECT))(*arrs, send_sems, recv_sems, after)
    return list(out[:n]), list(out[n:])


def _gather_copies(src, land, send_sems, recv_sems):
    me = _chip_of(_me())
    copies = []
    for i in range(len(src)):
        for r, rel in enumerate(CHIP_RELS):
            k = 3 * i + r
            copies.append(pltpu.make_async_remote_copy(
                src_ref=src[i], dst_ref=land[i].at[me], send_sem=send_sems.at[k], recv_sem=recv_sems.at[k],
                device_id=_peer(rel), device_id_type=MESH))
    return copies


def _small_copies(src, land, send_sems, recv_sems):
    my_slot = _linear(_me())
    return [pltpu.make_async_remote_copy(
        src_ref=src[0], dst_ref=land[0].at[my_slot], send_sem=send_sems.at[k], recv_sem=recv_sems.at[k],
        device_id=_other(flip), device_id_type=MESH) for k, flip in enumerate(OTHERS)]


def _pieces_copies(src, land, send_sems, recv_sems):
    copies = []
    for k, flip in enumerate(OTHERS):
        peer = _other(flip)
        copies.append(pltpu.make_async_remote_copy(
            src_ref=src[0].at[_linear(peer)], dst_ref=land[0].at[k], send_sem=send_sems.at[k],
            recv_sem=recv_sems.at[k], device_id=peer, device_id_type=MESH))
    return copies


def _row_block(rows, cols, budget=2 * 1024 * 1024):
    rb = max(8, (budget // (4 * cols)) // 8 * 8)
    while rows % rb:
        rb -= 8
    return rb if rb > 0 else rows


def _sum_slots(name, first, rest):
    R, Cc = first.shape
    K = rest.shape[0]
    rb = _row_block(R, Cc)

    def body(f_ref, r_ref, o_ref):
        acc = f_ref[...].astype(F32)
        for j in range(K):
            acc = acc + r_ref[j].astype(F32)
        o_ref[...] = acc

    return pl.pallas_call(
        body, name=name, grid=(R // rb,),
        in_specs=[pl.BlockSpec((rb, Cc), lambda i: (i, 0)), pl.BlockSpec((K, rb, Cc), lambda i: (0, i, 0))],
        out_specs=pl.BlockSpec((rb, Cc), lambda i: (i, 0)),
        out_shape=jax.ShapeDtypeStruct((R, Cc), F32), compiler_params=_cp("parallel"))(first, rest)


def _adamw_math(w, gv, m, v):
    mn = ADAM_B1 * m + (1.0 - ADAM_B1) * gv
    vn = ADAM_B2 * v + (1.0 - ADAM_B2) * (gv * gv)
    m_hat = mn / (1.0 - ADAM_B1 ** ADAM_STEP)
    v_hat = vn / (1.0 - ADAM_B2 ** ADAM_STEP)
    return -ADAM_LR * (m_hat / (jnp.sqrt(v_hat) + ADAM_EPS) + ADAM_WD * w), mn, vn


def _adamw_halves(name, w, mine, theirs, m, v, core):
    R, Cc = w.shape
    r2 = R // 2
    rb = _row_block(r2, Cc, 1024 * 1024)
    nb2 = r2 // rb

    def body(c_ref, w_ref, mine_ref, theirs_ref, m_ref, v_ref, g_ref, d_ref, mo_ref, vo_ref):
        is_mine = (pl.program_id(0) // nb2) == c_ref[0]
        gv = jnp.where(is_mine, mine_ref[...], theirs_ref[...])
        g_ref[...] = gv
        d_ref[...], mo_ref[...], vo_ref[...] = _adamw_math(w_ref[...], gv, m_ref[...], v_ref[...])

    blk = pl.BlockSpec((rb, Cc), lambda i, c: (i, 0))
    half = lambda own: pl.BlockSpec(
        (rb, Cc), lambda i, c: (jnp.clip(i - (c[0] if own else 1 - c[0]) * nb2, 0, nb2 - 1), 0))
    return pl.pallas_call(
        body, name=name,
        grid_spec=pltpu.PrefetchScalarGridSpec(
            num_scalar_prefetch=1, grid=(2 * nb2,), in_specs=[blk, half(True), half(False), blk, blk],
            out_specs=[blk] * 4),
        out_shape=[jax.ShapeDtypeStruct((R, Cc), F32)] * 4, compiler_params=_cp("parallel"))(core, w, mine, theirs, m, v)


def _adamw(name, w, g, m, v):
    R, Cc = w.shape
    rb = _row_block(R, Cc, 1024 * 1024)

    def body(w_ref, g_ref, m_ref, v_ref, d_ref, mo_ref, vo_ref):
        d_ref[...], mo_ref[...], vo_ref[...] = _adamw_math(w_ref[...], g_ref[...], m_ref[...], v_ref[...])

    blk = pl.BlockSpec((rb, Cc), lambda i: (i, 0))
    return pl.pallas_call(
        body, name=name, grid=(R // rb,), in_specs=[blk] * 4, out_specs=[blk] * 3,
        out_shape=[jax.ShapeDtypeStruct((R, Cc), F32)] * 3, compiler_params=_cp("parallel"))(w, g, m, v)


def _pack(arrs):
    rows = []
    for a in arrs:
        flat = a.reshape(-1)
        pad = (-flat.shape[0]) % 128
        rows.append(jnp.pad(flat, (0, pad)).reshape(-1, 128))
    buf = jnp.concatenate(rows, axis=0)
    return jnp.pad(buf, ((0, (-buf.shape[0]) % 8), (0, 0)))


def _unpack(buf, shapes):
    out, r = [], 0
    for s in shapes:
        n = math.prod(s)
        nr = -(-n // 128)
        out.append(buf[r:r + nr].reshape(-1)[:n].reshape(s))
        r += nr
    return out


BIG = ("w_in", "w_out", "w_up", "w_down")
CONV = ("dn_conv_w", "ffn_conv_w")
REPL = ("attn_norm_g", "dn_a_log", "dn_dt_bias", "dn_out_norm_g", "sg_norm_g", "sg_w", "sg_b",
        "ffn_norm_g", "ffn_conv_b", "final_norm_g")
ORDER = ("attn_norm_g", "w_in", "dn_conv_w", "dn_a_log", "dn_dt_bias", "dn_out_norm_g", "sg_norm_g", "sg_w",
         "sg_b", "w_out", "ffn_norm_g", "w_up", "ffn_conv_w", "ffn_conv_b", "w_down", "final_norm_g")


def kernel(x, attn_norm_g, w_in, dn_conv_w, dn_a_log, dn_dt_bias, dn_out_norm_g, sg_norm_g, sg_w, sg_b, w_out, ffn_norm_g, w_up, ffn_conv_w, ffn_conv_b, w_down, final_norm_g, loss_target, m_attn_norm_g, m_w_in, m_dn_conv_w, m_dn_a_log, m_dn_dt_bias, m_dn_out_norm_g, m_sg_norm_g, m_sg_w, m_sg_b, m_w_out, m_ffn_norm_g, m_w_up, m_ffn_conv_w, m_ffn_conv_b, m_w_down, m_final_norm_g, v_attn_norm_g, v_w_in, v_dn_conv_w, v_dn_a_log, v_dn_dt_bias, v_dn_out_norm_g, v_sg_norm_g, v_sg_w, v_sg_b, v_w_out, v_ffn_norm_g, v_w_up, v_ffn_conv_w, v_ffn_conv_b, v_w_down, v_final_norm_g):
    W = dict(attn_norm_g=attn_norm_g, w_in=w_in, dn_conv_w=dn_conv_w, dn_a_log=dn_a_log, dn_dt_bias=dn_dt_bias,
             dn_out_norm_g=dn_out_norm_g, sg_norm_g=sg_norm_g, sg_w=sg_w, sg_b=sg_b, w_out=w_out,
             ffn_norm_g=ffn_norm_g, w_up=w_up, ffn_conv_w=ffn_conv_w, ffn_conv_b=ffn_conv_b, w_down=w_down,
             final_norm_g=final_norm_g)
    Mo = dict(attn_norm_g=m_attn_norm_g, w_in=m_w_in, dn_conv_w=m_dn_conv_w, dn_a_log=m_dn_a_log,
              dn_dt_bias=m_dn_dt_bias, dn_out_norm_g=m_dn_out_norm_g, sg_norm_g=m_sg_norm_g, sg_w=m_sg_w,
              sg_b=m_sg_b, w_out=m_w_out, ffn_norm_g=m_ffn_norm_g, w_up=m_w_up, ffn_conv_w=m_ffn_conv_w,
              ffn_conv_b=m_ffn_conv_b, w_down=m_w_down, final_norm_g=m_final_norm_g)
    Vo = dict(attn_norm_g=v_attn_norm_g, w_in=v_w_in, dn_conv_w=v_dn_conv_w, dn_a_log=v_dn_a_log,
              dn_dt_bias=v_dn_dt_bias, dn_out_norm_g=v_dn_out_norm_g, sg_norm_g=v_sg_norm_g, sg_w=v_sg_w,
              sg_b=v_sg_b, w_out=v_w_out, ffn_norm_g=v_ffn_norm_g, w_up=v_w_up, ffn_conv_w=v_ffn_conv_w,
              ffn_conv_b=v_ffn_conv_b, w_down=v_w_down, final_norm_g=v_final_norm_g)
    xi, yi, ci = lax.axis_index("x"), lax.axis_index("y"), lax.axis_index("c")
    chip = 2 * xi + yi

    me_lin = 4 * xi + 2 * yi + ci

    g_in, g_dnc = _gather_first(w_in[0].astype(BF16), dn_conv_w[0])
    late = ("w_out", "w_up", "w_down", "ffn_conv_w")
    late_shards = [W[n][0].astype(BF16) for n in late[:3]] + [ffn_conv_w[0]]
    late_lands = _place_own("place_late_shards", late_shards, 4, chip.astype(jnp.int32).reshape(1))
    n_late = 3 * len(late_shards)
    ssem, rsem, late_src, late_lands, token = _transfer_start("gather_rest_start", late_shards, late_lands,
                                                              n_late, _gather_copies, after=g_in)

    def late_weights(after):
        _, (g_out, g_up, g_down, g_ffc) = _transfer_wait("gather_rest_wait", ssem, rsem, late_src, late_lands,
                                                         _gather_copies, after)
        return dict(w_out=g_out.reshape(D_MODEL, D_MODEL), w_up=g_up.transpose(1, 0, 2).reshape(D_MODEL, 2 * D_FF),
                    w_down=g_down.reshape(D_FF, D_MODEL), ffn_conv_w=g_ffc.transpose(1, 0, 2).reshape(3, 2 * D_FF))

    full = dict(
        w_in=jnp.pad(g_in.transpose(1, 0, 2).reshape(D_MODEL, PROJ_COLS), ((0, 0), (0, PROJ_PAD - PROJ_COLS))),
        dn_conv_w=g_dnc.transpose(1, 0, 2).reshape(4, 3 * DN_WIDTH),
        attn_norm_g=attn_norm_g, dn_a_log=dn_a_log, dn_dt_bias=dn_dt_bias, dn_out_norm_g=dn_out_norm_g,
        sg_norm_g=sg_norm_g, sg_w=sg_w[0], sg_b=sg_b[0], ffn_norm_g=ffn_norm_g, ffn_conv_b=ffn_conv_b,
        final_norm_g=final_norm_g[None])

    pending = {}
    early_names = ("dn_out_norm_g", "sg_norm_g", "sg_w", "sg_b", "ffn_norm_g", "ffn_conv_w", "ffn_conv_b",
                   "final_norm_g")
    late_names = ("attn_norm_g", "dn_a_log", "dn_dt_bias", "dn_conv_w")

    def on_grad(name, gw):
        if name == "small_early":
            buf = _pack([gw[n] for n in early_names])
            land = _place_own("place_small_early", [buf], 8, me_lin.astype(jnp.int32).reshape(1))[0]
            s_sem, r_sem, src, lands, tok = _transfer_start("small_early_start", [buf], [land], 7, _small_copies)
            pending[name] = (s_sem, r_sem, src, lands)
            return tok
        g8 = gw.reshape(D_MODEL, 4, -1).transpose(1, 0, 2) if name == "w_in" else gw
        g8 = g8.reshape(8, -1, g8.shape[-1])
        land = lax.empty((7,) + g8.shape[1:], BF16)
        s_sem, r_sem, src, lands, tok = _transfer_start(f"reduce_{name}_start", [g8], [land], 7, _pieces_copies)
        pending[name] = (s_sem, r_sem, src, lands)
        return tok

    loss_row, grad_x, g = _local_step(x[0], loss_target[0], full, dep=token, late_weights=late_weights,
                                      on_grad=on_grad)

    small_names = REPL + CONV
    late_all = _exchange_small(_pack([g[n] for n in late_names] + [loss_row]))
    late_sum = _sum_slots("sum_small_late", late_all[0], late_all[1:])
    s_sem, r_sem, src, lands = pending["small_early"]
    _, (early_all,) = _transfer_wait("small_early_wait", s_sem, r_sem, src, lands, _small_copies, grad_x)
    early_sum = _sum_slots("sum_small_early", early_all[0], early_all[1:])
    *late_vals, loss_sum = _unpack(late_sum, [g[n].shape for n in late_names] + [loss_row.shape])
    loss = loss_sum[0, 0]
    sg = dict(zip(late_names, late_vals))
    sg.update(zip(early_names, _unpack(early_sum, [g[n].shape for n in early_names])))
    sg["dn_conv_w"] = lax.dynamic_slice_in_dim(sg["dn_conv_w"], chip * (3 * DN_WIDTH // 4), 3 * DN_WIDTH // 4, axis=1)
    sg["ffn_conv_w"] = lax.dynamic_slice_in_dim(sg["ffn_conv_w"], chip * (2 * D_FF // 4), 2 * D_FF // 4, axis=1)

    halves = []
    for n in ("w_down", "w_up", "w_out", "w_in"):
        s_sem, r_sem, src, lands = pending[n]
        sent, got = _transfer_wait(f"reduce_{n}_wait", s_sem, r_sem, src, lands, _pieces_copies, grad_x)
        own = lax.dynamic_index_in_dim(sent[0], me_lin, axis=0, keepdims=False)
        halves.append(_sum_slots(f"sum_{n}", own, got[0]))
    theirs = _pair_swap(halves)
    core = ci.astype(jnp.int32).reshape(1)
    grads, delta, new_m, new_v = {}, {}, {}, {}
    for n, mine_h, their_h in zip(("w_down", "w_up", "w_out", "w_in"), halves, theirs):
        shp = W[n].shape
        gr, d, mn, vn = _adamw_halves(f"adamw_{n}", W[n][0], mine_h, their_h, Mo[n][0], Vo[n][0], core)
        grads[n], delta[n], new_m[n], new_v[n] = gr.reshape(shp), d.reshape(shp), mn.reshape(shp), vn.reshape(shp)
    shapes = [W[n].shape for n in small_names]
    for n in small_names:
        grads[n] = sg[n].reshape(W[n].shape)
    d, mn, vn = _adamw("adamw_small", _pack([W[n] for n in small_names]), _pack([grads[n] for n in small_names]),
                       _pack([Mo[n] for n in small_names]), _pack([Vo[n] for n in small_names]))
    for dst, buf in ((delta, d), (new_m, mn), (new_v, vn)):
        dst.update(zip(small_names, _unpack(buf, shapes)))

    return (loss, grad_x[None], *[grads[n] for n in ORDER], *[delta[n] for n in ORDER],
            *[new_m[n] for n in ORDER], *[new_v[n] for n in ORDER])
```

```python
import functools
import math

import jax
import jax.numpy as jnp
from jax import lax
from jax.experimental import pallas as pl
from jax.experimental.pallas import tpu as pltpu

F32 = jnp.float32
BF16 = jnp.bfloat16

D_MODEL = 1024
CHUNK = 64
HEAD_DIM = 128
N_HEADS = 4
DN_WIDTH = 512
SG_WIDTH = 512
SG_GROUPS = 4
SG_BLOCK = 128
D_FF = 2816
PROJ_COLS = 3080
PROJ_PAD = 3200
BA_COL = 3072
EPS = 1e-6
NEG = -1e30
VMEM_LIMIT = 56 * 1024 * 1024

ADAM_LR = 0.001
ADAM_B1 = 0.9
ADAM_B2 = 0.999
ADAM_EPS = 1e-08
ADAM_WD = 0.01
ADAM_STEP = 10

MESH = pl.DeviceIdType.MESH
ANY = pl.BlockSpec(memory_space=pl.ANY)


def _cp(*sem):
    return pltpu.CompilerParams(dimension_semantics=sem, vmem_limit_bytes=VMEM_LIMIT)


def _bf(a):
    return a.astype(BF16)


def _nn(a, b):
    return jnp.dot(_bf(a), _bf(b), preferred_element_type=F32)


def _nt(a, b):
    return lax.dot_general(_bf(a), _bf(b), (((1,), (1,)), ((), ())), preferred_element_type=F32)


def _tn(a, b):
    return lax.dot_general(_bf(a), _bf(b), (((0,), (0,)), ((), ())), preferred_element_type=F32)


def _split(a):
    hi = _bf(a)
    return hi, _bf(a - hi.astype(F32))


def _sigmoid(x):
    return 0.5 * jnp.tanh(0.5 * x) + 0.5


def _silu(x):
    return x * _sigmoid(x)


def _dsilu(x):
    s = _sigmoid(x)
    return s * (1.0 + x * (1.0 - s))


_GELU_C = math.sqrt(2.0 / math.pi)
_GELU_A = 0.044715


def _gelu(x):
    return 0.5 * x * (1.0 + jnp.tanh(_GELU_C * (x + _GELU_A * x * x * x)))


def _dgelu(x):
    t = jnp.tanh(_GELU_C * (x + _GELU_A * x * x * x))
    return 0.5 * (1.0 + t) + 0.5 * x * (1.0 - t * t) * _GELU_C * (1.0 + 3.0 * _GELU_A * x * x)


def _softplus(x):
    return jnp.maximum(x, 0.0) + jnp.log(1.0 + jnp.exp(-jnp.abs(x)))


def _mm_nn(name, a, b, out_dtype, tm, tn, res=None):
    M, K = a.shape
    N = b.shape[1]
    tm, tn = min(tm, M), min(tn, N)

    def body(*refs):
        a_ref, b_ref = refs[0], refs[1]
        o_ref = refs[-1]
        acc = _nn(a_ref[...], b_ref[...])
        if res is not None:
            acc = acc + refs[2][...]
        o_ref[...] = acc.astype(o_ref.dtype)

    in_specs = [pl.BlockSpec((tm, K), lambda j, i: (i, 0)), pl.BlockSpec((K, tn), lambda j, i: (0, j))]
    args = [a, b]
    if res is not None:
        in_specs.append(pl.BlockSpec((tm, tn), lambda j, i: (i, j)))
        args.append(res)
    return pl.pallas_call(
        body, name=name, grid=(N // tn, M // tm), in_specs=in_specs,
        out_specs=pl.BlockSpec((tm, tn), lambda j, i: (i, j)),
        out_shape=jax.ShapeDtypeStruct((M, N), out_dtype),
        compiler_params=_cp("parallel", "parallel"))(*args)


def _with_dep(in_specs, args, dep):
    if dep is None:
        return in_specs, args
    return in_specs + [ANY], args + [dep]


def _mm_nt(name, a, b, out_dtype, tm, tn, dep=None):
    M, K = a.shape
    N = b.shape[0]
    tm, tn = min(tm, M), min(tn, N)

    def body(a_ref, b_ref, *rest):
        o_ref = rest[-1]
        o_ref[...] = _nt(a_ref[...], b_ref[...]).astype(o_ref.dtype)

    in_specs, args = _with_dep(
        [pl.BlockSpec((tm, K), lambda i, j: (i, 0)), pl.BlockSpec((tn, K), lambda i, j: (j, 0))], [a, b], dep)
    return pl.pallas_call(
        body, name=name, grid=(M // tm, N // tn), in_specs=in_specs,
        out_specs=pl.BlockSpec((tm, tn), lambda i, j: (i, j)),
        out_shape=jax.ShapeDtypeStruct((M, N), out_dtype),
        compiler_params=_cp("parallel", "parallel"))(*args)


def _mm_tn(name, a, b, tm, tn, tk, col_major_tiles=False):
    T, M = a.shape
    N = b.shape[1]
    tm, tn, tk = min(tm, M), min(tn, N), min(tk, T)
    nk = T // tk

    def body(a_ref, b_ref, o_ref, acc_ref):
        k = pl.program_id(2)

        @pl.when(k == 0)
        def _():
            acc_ref[...] = jnp.zeros_like(acc_ref)
        acc_ref[...] += _tn(a_ref[...], b_ref[...])

        @pl.when(k == nk - 1)
        def _():
            o_ref[...] = acc_ref[...].astype(BF16).reshape(o_ref.shape)

    if col_major_tiles:
        assert tm == M
        out_spec = pl.BlockSpec((1, tm, tn), lambda i, j, k: (j, 0, 0))
        out_shape = jax.ShapeDtypeStruct((N // tn, M, tn), BF16)
    else:
        out_spec = pl.BlockSpec((tm, tn), lambda i, j, k: (i, j))
        out_shape = jax.ShapeDtypeStruct((M, N), BF16)
    return pl.pallas_call(
        body, name=name, grid=(M // tm, N // tn, nk),
        in_specs=[pl.BlockSpec((tk, tm), lambda i, j, k: (k, i)), pl.BlockSpec((tk, tn), lambda i, j, k: (k, j))],
        out_specs=out_spec, out_shape=out_shape, scratch_shapes=[pltpu.VMEM((tm, tn), F32)],
        compiler_params=_cp("parallel", "parallel", "arbitrary"))(a, b)


def _rms_fwd(name, x, g, rb=512, dep=None):
    T, Dm = x.shape
    rb = min(rb, T)

    def body(x_ref, g_ref, *rest):
        h_ref = rest[-1]
        xv = x_ref[...]
        r = lax.rsqrt(jnp.mean(xv * xv, axis=-1, keepdims=True) + EPS)
        h_ref[...] = (xv * r * g_ref[...]).astype(BF16)

    in_specs, args = _with_dep(
        [pl.BlockSpec((rb, Dm), lambda i: (i, 0)), pl.BlockSpec((1, Dm), lambda i: (0, 0))], [x, g], dep)
    return pl.pallas_call(
        body, name=name, grid=(T // rb,), in_specs=in_specs,
        out_specs=pl.BlockSpec((rb, Dm), lambda i: (i, 0)),
        out_shape=jax.ShapeDtypeStruct((T, Dm), BF16), compiler_params=_cp("parallel"))(*args)


def _rms_bwd(name, dh, x, g, dres, rb=512):
    T, Dm = x.shape
    rb = min(rb, T)

    def body(dh_ref, x_ref, g_ref, dres_ref, dx_ref, gg_ref):
        @pl.when(pl.program_id(0) == 0)
        def _():
            gg_ref[...] = jnp.zeros_like(gg_ref)
        xv = x_ref[...]
        r = lax.rsqrt(jnp.mean(xv * xv, axis=-1, keepdims=True) + EPS)
        xh = xv * r
        dhv = dh_ref[...]
        gg_ref[...] += jnp.sum(dhv * xh, axis=0, keepdims=True)
        dxh = dhv * g_ref[...]
        dx_ref[...] = dres_ref[...] + r * (dxh - xh * jnp.mean(dxh * xh, axis=-1, keepdims=True))

    row = pl.BlockSpec((rb, Dm), lambda i: (i, 0))
    vec = pl.BlockSpec((1, Dm), lambda i: (0, 0))
    return pl.pallas_call(
        body, name=name, grid=(T // rb,), in_specs=[row, row, vec, row], out_specs=[row, vec],
        out_shape=[jax.ShapeDtypeStruct((T, Dm), F32), jax.ShapeDtypeStruct((1, Dm), F32)],
        compiler_params=_cp("arbitrary"))(dh, x, g, dres)


def _loss_head(x3, tgt, g, rb=512):
    T, Dm = x3.shape
    rb = min(rb, T)

    def body(x_ref, t_ref, g_ref, loss_ref, dx_ref, gg_ref):
        @pl.when(pl.program_id(0) == 0)
        def _():
            gg_ref[...] = jnp.zeros_like(gg_ref)
            loss_ref[...] = jnp.zeros_like(loss_ref)
        xv = x_ref[...]
        r = lax.rsqrt(jnp.mean(xv * xv, axis=-1, keepdims=True) + EPS)
        xh = xv * r
        e = xh * g_ref[...] - t_ref[...]
        loss_ref[...] += jnp.zeros_like(loss_ref) + (0.5 / Dm) * jnp.sum(e * e)
        dy = e * (1.0 / Dm)
        gg_ref[...] += jnp.sum(dy * xh, axis=0, keepdims=True)
        dxh = dy * g_ref[...]
        dx_ref[...] = r * (dxh - xh * jnp.mean(dxh * xh, axis=-1, keepdims=True))

    row = pl.BlockSpec((rb, Dm), lambda i: (i, 0))
    vec = pl.BlockSpec((1, Dm), lambda i: (0, 0))
    return pl.pallas_call(
        body, name="loss_head", grid=(T // rb,), in_specs=[row, row, vec],
        out_specs=[pl.BlockSpec((1, 128), lambda i: (0, 0)), row, vec],
        out_shape=[jax.ShapeDtypeStruct((1, 128), F32), jax.ShapeDtypeStruct((T, Dm), F32),
                   jax.ShapeDtypeStruct((1, Dm), F32)],
        compiler_params=_cp("arbitrary"))(x3, tgt, g)


def _halo_prev_spec(rb, width):
    return pl.BlockSpec((8, width), lambda i: (jnp.maximum(i * (rb // 8) - 1, 0), 0))


def _halo_next_spec(rb, width, T):
    return pl.BlockSpec((8, width), lambda i: (jnp.minimum((i + 1) * (rb // 8), T // 8 - 1), 0))


LANES = 128
FF_STRIPS = D_FF // LANES
ROW_CHUNK = 32


def _strip(j, base=0):
    return pl.ds(pl.multiple_of(base + j * LANES, LANES), LANES)


def _ffn_act(up, w, b, rb=256):
    T, W = up.shape
    rb = min(rb, T)

    def body(up_ref, halo_ref, w_ref, b_ref, act_ref, ext_scr):
        first = pl.program_id(0) == 0

        def strip(j, slot):
            halves = (_strip(j), _strip(j, D_FF))
            wv = [w_ref[:, cols] for cols in halves]
            bv = [b_ref[:, cols] for cols in halves]
            for h, cols in enumerate(halves):
                ext_scr[slot, h,0:8] = jnp.where(first, 0.0, halo_ref[:, cols])
                ext_scr[slot, h,8:] = up_ref[:, cols]
            for r0 in range(0, rb, ROW_CHUNK):
                n = min(ROW_CHUNK, rb - r0)
                c = [ext_scr[slot, h,6 + r0:6 + r0 + n] * wv[h][0:1] + ext_scr[slot, h,7 + r0:7 + r0 + n] * wv[h][1:2]
                     + ext_scr[slot, h,8 + r0:8 + r0 + n] * wv[h][2:3] + bv[h] for h in range(2)]
                act_ref[r0:r0 + n, halves[0]] = (_silu(c[0]) * c[1]).astype(BF16)

        def pair(jj, carry):
            strip(2 * jj, 0)
            strip(2 * jj + 1, 1)
            return carry

        lax.fori_loop(0, FF_STRIPS // 2, pair, 0)

    return pl.pallas_call(
        body, name="ffn_act", grid=(T // rb,),
        in_specs=[pl.BlockSpec((rb, W), lambda i: (i, 0)), _halo_prev_spec(rb, W),
                  pl.BlockSpec((3, W), lambda i: (0, 0)), pl.BlockSpec((1, W), lambda i: (0, 0))],
        out_specs=pl.BlockSpec((rb, D_FF), lambda i: (i, 0)),
        out_shape=jax.ShapeDtypeStruct((T, D_FF), BF16),
        scratch_shapes=[pltpu.VMEM((2, 2, rb + 8, LANES), F32)], compiler_params=_cp("parallel"))(up, up, w, b)


def _ffn_act_bwd(up, dact, w, b, rb=128, dep=None):
    T, W = up.shape
    rb = min(rb, T)
    nb = T // rb
    re = rb + 8

    def body(up_ref, prev_ref, next_ref, da_ref, danext_ref, w_ref, b_ref, *rest):
        dup_ref, gw_ref, gb_ref, ext_scr, dc_scr = rest[-5:]
        i = pl.program_id(0)

        @pl.when(i == 0)
        def _():
            gw_ref[...] = jnp.zeros_like(gw_ref)
            gb_ref[...] = jnp.zeros_like(gb_ref)
        last = i == nb - 1

        def fold8(a):
            return jnp.sum(a.reshape(a.shape[0] // 8, 8, LANES), axis=0)

        def strip(j, slot):
            halves = (_strip(j), _strip(j, D_FF))
            wv = [w_ref[:, cols] for cols in halves]
            bv = [b_ref[:, cols] for cols in halves]
            for h, cols in enumerate(halves):
                ext_scr[slot, h,0:8] = jnp.where(i > 0, prev_ref[:, cols], 0.0)
                ext_scr[slot, h,8:8 + rb] = up_ref[:, cols]
                ext_scr[slot, h,8 + rb:] = next_ref[:, cols]
            gb = [jnp.zeros((8, LANES), F32) for _ in range(2)]
            gw = [[jnp.zeros((8, LANES), F32) for _ in range(3)] for _ in range(2)]
            for r0 in range(0, re, ROW_CHUNK):
                n = min(ROW_CHUNK, re - r0)
                tp = [[ext_scr[slot, h,6 + k + r0:6 + k + r0 + n] for k in range(3)] for h in range(2)]
                c = [tp[h][0] * wv[h][0:1] + tp[h][1] * wv[h][1:2] + tp[h][2] * wv[h][2:3] + bv[h] for h in range(2)]
                if r0 < rb:
                    da = da_ref[r0:r0 + n, halves[0]]
                else:
                    da = jnp.where(last, 0.0, danext_ref[:, halves[0]])
                s = _sigmoid(c[0])
                gs = c[0] * s
                dcs = (da * c[1] * (s + gs * (1.0 - s)), da * gs)
                for h in range(2):
                    dc_scr[slot, h,r0:r0 + n] = dcs[h]
                    if r0 < rb:
                        gb[h] = gb[h] + fold8(dcs[h])
                        for k in range(3):
                            gw[h][k] = gw[h][k] + fold8(tp[h][k] * dcs[h])
            for r0 in range(0, rb, ROW_CHUNK):
                n = min(ROW_CHUNK, rb - r0)
                for h, cols in enumerate(halves):
                    dup = (dc_scr[slot, h,r0:r0 + n] * wv[h][2:3] + dc_scr[slot, h,r0 + 1:r0 + 1 + n] * wv[h][1:2]
                           + dc_scr[slot, h,r0 + 2:r0 + 2 + n] * wv[h][0:1])
                    dup_ref[r0:r0 + n, cols] = dup.astype(BF16)
            for h, cols in enumerate(halves):
                gb_ref[:, cols] += jnp.sum(gb[h], axis=0, keepdims=True)
                for k in range(3):
                    gw_ref[k:k + 1, cols] += jnp.sum(gw[h][k], axis=0, keepdims=True)

        def pair(jj, carry):
            strip(2 * jj, 0)
            strip(2 * jj + 1, 1)
            return carry

        lax.fori_loop(0, FF_STRIPS // 2, pair, 0)

    in_specs, args = _with_dep(
        [pl.BlockSpec((rb, W), lambda i: (i, 0)), _halo_prev_spec(rb, W), _halo_next_spec(rb, W, T),
         pl.BlockSpec((rb, D_FF), lambda i: (i, 0)), _halo_next_spec(rb, D_FF, T),
         pl.BlockSpec((3, W), lambda i: (0, 0)), pl.BlockSpec((1, W), lambda i: (0, 0))],
        [up, up, up, dact, dact, w, b], dep)
    return pl.pallas_call(
        body, name="ffn_act_bwd", grid=(nb,), in_specs=in_specs,
        out_specs=[pl.BlockSpec((rb, W), lambda i: (i, 0)), pl.BlockSpec((3, W), lambda i: (0, 0)),
                   pl.BlockSpec((1, W), lambda i: (0, 0))],
        out_shape=[jax.ShapeDtypeStruct((T, W), BF16), jax.ShapeDtypeStruct((3, W), F32),
                   jax.ShapeDtypeStruct((1, W), F32)],
        scratch_shapes=[pltpu.VMEM((2, 2, rb + 16, LANES), F32), pltpu.VMEM((2, 2, re, LANES), F32)],
        compiler_params=_cp("arbitrary"))(*args)


def _lane_iota(shape):
    return lax.broadcasted_iota(jnp.int32, shape, len(shape) - 1)


def _dn_act(p, conv_w, alog_row, dtb_row, rb=256):
    T = p.shape[0]
    rb = min(rb, T)
    W3 = 3 * DN_WIDTH

    def body(p_ref, halo_ref, ba_ref, w_ref, al_ref, dt_ref, q_ref, k_ref, v_ref, bg_ref, ext_scr):
        first = pl.program_id(0) == 0
        outs = (q_ref, k_ref, v_ref)
        for j in range(3 * N_HEADS):
            kind, h = divmod(j, N_HEADS)
            cols = slice(j * HEAD_DIM, (j + 1) * HEAD_DIM)
            cur = p_ref[:, cols]
            ext_scr[j, 0:8] = jnp.where(first, 0.0, halo_ref[:, cols])
            ext_scr[j, 8:] = cur
            wv = w_ref[:, cols]
            s = _silu(ext_scr[j, 5:5 + rb] * wv[0:1] + ext_scr[j, 6:6 + rb] * wv[1:2]
                      + ext_scr[j, 7:7 + rb] * wv[2:3] + cur * wv[3:4])
            if kind < 2:
                scale = HEAD_DIM ** -0.5 if kind == 0 else 1.0
                s = s * (lax.rsqrt(jnp.sum(s * s, axis=-1, keepdims=True) + EPS) * scale)
            outs[kind][:, h * HEAD_DIM:(h + 1) * HEAD_DIM] = s
        ba = ba_ref[...]
        lane = _lane_iota(ba.shape)
        beta = _sigmoid(ba)
        g = -jnp.exp(al_ref[...]) * _softplus(ba + dt_ref[...])
        bg_ref[...] = jnp.where(lane < N_HEADS, beta, jnp.where(lane < 2 * N_HEADS, g, 0.0))

    row512 = pl.BlockSpec((rb, DN_WIDTH), lambda i: (i, 0))
    row128 = pl.BlockSpec((rb, 128), lambda i: (i, 0))
    vec128 = pl.BlockSpec((1, 128), lambda i: (0, 0))
    return pl.pallas_call(
        body, name="dn_act", grid=(T // rb,),
        in_specs=[pl.BlockSpec((rb, W3), lambda i: (i, 0)), _halo_prev_spec(rb, W3),
                  pl.BlockSpec((rb, 128), lambda i: (i, BA_COL // 128)),
                  pl.BlockSpec((4, W3), lambda i: (0, 0)), vec128, vec128],
        out_specs=[row512, row512, row512, row128],
        out_shape=[jax.ShapeDtypeStruct((T, DN_WIDTH), F32)] * 3 + [jax.ShapeDtypeStruct((T, 128), F32)],
        scratch_shapes=[pltpu.VMEM((3 * N_HEADS, rb + 8, HEAD_DIM), F32)],
        compiler_params=_cp("parallel"))(p, p, p, conv_w, alog_row, dtb_row)


def _dn_act_bwd(p, conv_w, alog_row, dtb_row, dq, dk, dv, dbg, dp_mid, rb=256):
    T = p.shape[0]
    rb = min(rb, T)
    nb = T // rb
    re = rb + 8
    W3 = 3 * DN_WIDTH

    def body(p_ref, prev_ref, next_ref, ba_ref, w_ref, al_ref, dt_ref, dq_ref, dqn_ref, dk_ref, dkn_ref,
             dv_ref, dvn_ref, dbg_ref, mid_ref, draw_ref, gw_ref, gad_ref, ext_scr, dc_scr):
        i = pl.program_id(0)
        draw_ref[:, W3:2 * W3] = mid_ref[...]

        @pl.when(i == 0)
        def _():
            gw_ref[...] = jnp.zeros_like(gw_ref)
            gad_ref[...] = jnp.zeros_like(gad_ref)
        row = lax.broadcasted_iota(jnp.int32, (re, 1), 0)
        live = (row < rb) | (i < nb - 1)
        d_refs = ((dq_ref, dqn_ref), (dk_ref, dkn_ref), (dv_ref, dvn_ref))
        for j in range(3 * N_HEADS):
            kind, h = divmod(j, N_HEADS)
            cols = slice(j * HEAD_DIM, (j + 1) * HEAD_DIM)
            hcols = slice(h * HEAD_DIM, (h + 1) * HEAD_DIM)
            ext_scr[j, 0:8] = jnp.where(i > 0, prev_ref[:, cols], 0.0)
            ext_scr[j, 8:8 + rb] = p_ref[:, cols]
            ext_scr[j, 8 + rb:] = next_ref[:, cols]
            tp = [ext_scr[j, 5 + k:5 + k + re] for k in range(4)]
            wv = w_ref[:, cols]
            c = tp[0] * wv[0:1] + tp[1] * wv[1:2] + tp[2] * wv[2:3] + tp[3] * wv[3:4]
            sg = _sigmoid(c)
            s = c * sg
            d_in = jnp.where(live, jnp.concatenate([d_refs[kind][0][:, hcols], d_refs[kind][1][:, hcols]], axis=0), 0.0)
            if kind < 2:
                scale = HEAD_DIM ** -0.5 if kind == 0 else 1.0
                n = lax.rsqrt(jnp.sum(s * s, axis=-1, keepdims=True) + EPS)
                hat = s * n
                d_in = (n * scale) * (d_in - hat * jnp.sum(hat * d_in, axis=-1, keepdims=True))
            dc = d_in * (sg + s * (1.0 - sg))
            dc_scr[j] = dc
            dcc = dc[0:rb]
            draw = (dcc * wv[3:4] + dc_scr[j, 1:1 + rb] * wv[2:3] + dc_scr[j, 2:2 + rb] * wv[1:2]
                    + dc_scr[j, 3:3 + rb] * wv[0:1])
            draw_ref[:, cols] = draw.astype(BF16)
            for k in range(4):
                gw_ref[k:k + 1, cols] += jnp.sum(tp[k][0:rb] * dcc, axis=0, keepdims=True)
        ba = ba_ref[...]
        dbg = dbg_ref[...]
        lane = _lane_iota(ba.shape)
        beta = _sigmoid(ba)
        ea = jnp.exp(al_ref[...])
        z = ba + dt_ref[...]
        d_a = dbg * (-ea) * _sigmoid(z)
        dba = jnp.where(lane < N_HEADS, dbg * beta * (1.0 - beta), jnp.where(lane < 2 * N_HEADS, d_a, 0.0))
        draw_ref[:, BA_COL:] = dba.astype(BF16)
        isg = (lane >= N_HEADS) & (lane < 2 * N_HEADS)
        g = -ea * _softplus(z)
        gad_ref[0:1, :] += jnp.sum(jnp.where(isg, dbg * g, 0.0), axis=0, keepdims=True)
        gad_ref[1:2, :] += jnp.sum(jnp.where(isg, d_a, 0.0), axis=0, keepdims=True)

    row512 = pl.BlockSpec((rb, DN_WIDTH), lambda i: (i, 0))
    row128 = pl.BlockSpec((rb, 128), lambda i: (i, 0))
    vec128 = pl.BlockSpec((1, 128), lambda i: (0, 0))
    next512 = _halo_next_spec(rb, DN_WIDTH, T)
    return pl.pallas_call(
        body, name="dn_act_bwd", grid=(nb,),
        in_specs=[pl.BlockSpec((rb, W3), lambda i: (i, 0)), _halo_prev_spec(rb, W3), _halo_next_spec(rb, W3, T),
                  pl.BlockSpec((rb, 128), lambda i: (i, BA_COL // 128)),
                  pl.BlockSpec((4, W3), lambda i: (0, 0)), vec128, vec128,
                  row512, next512, row512, next512, row512, next512, row128,
                  pl.BlockSpec((rb, W3), lambda i: (i, 0))],
        out_specs=[pl.BlockSpec((rb, PROJ_PAD), lambda i: (i, 0)),
                   pl.BlockSpec((4, W3), lambda i: (0, 0)), pl.BlockSpec((2, 128), lambda i: (0, 0))],
        out_shape=[jax.ShapeDtypeStruct((T, PROJ_PAD), BF16),
                   jax.ShapeDtypeStruct((4, W3), F32), jax.ShapeDtypeStruct((2, 128), F32)],
        scratch_shapes=[pltpu.VMEM((3 * N_HEADS, rb + 16, HEAD_DIM), F32), pltpu.VMEM((3 * N_HEADS, re, HEAD_DIM), F32)],
        compiler_params=_cp("arbitrary"))(p, p, p, p, conv_w, alog_row, dtb_row, dq, dq, dk, dk, dv, dv, dbg, dp_mid)


def _tri(incl):
    ii = lax.broadcasted_iota(jnp.int32, (CHUNK, CHUNK), 0)
    jj = lax.broadcasted_iota(jnp.int32, (CHUNK, CHUNK), 1)
    return ii, jj, ((ii >= jj) if incl else (ii > jj))


def _dn_chunk(k, bg, cb=4):
    T = k.shape[0]
    N = T // CHUNK
    cb = min(cb, N)

    def body(k_ref, bg_ref, gc_ref, gct_ref, l_ref):
        ii, jj, incl = _tri(True)
        tri = incl.astype(F32)
        U = range(cb)
        bgv = [bg_ref[u * CHUNK:(u + 1) * CHUNK, :] for u in U]
        gc = [jnp.dot(tri, bgv[u], precision=lax.Precision.HIGHEST, preferred_element_type=F32) for u in U]
        gct = [gc[u].T for u in U]
        kk = [[None] * N_HEADS for _ in U]
        for u in U:
            gc_ref[u * CHUNK:(u + 1) * CHUNK, :] = gc[u]
            gct_ref[u] = gct[u][0:8]
            for h in range(N_HEADS):
                kh = k_ref[u * CHUNK:(u + 1) * CHUNK, h * HEAD_DIM:(h + 1) * HEAD_DIM]
                kk[u][h] = _nt(kh * bgv[u][:, h:h + 1], kh)
        for u in U:
            for h in range(N_HEADS):
                gcol = gc[u][:, N_HEADS + h:N_HEADS + h + 1]
                grow = gct[u][N_HEADS + h:N_HEADS + h + 1, :]
                l_ref[u, h] = kk[u][h] * jnp.exp(jnp.where(ii > jj, gcol - grow, NEG))

    rows = cb * CHUNK
    return pl.pallas_call(
        body, name="dn_chunk", grid=(N // cb,),
        in_specs=[pl.BlockSpec((rows, DN_WIDTH), lambda n: (n, 0)), pl.BlockSpec((rows, 128), lambda n: (n, 0))],
        out_specs=[pl.BlockSpec((rows, 128), lambda n: (n, 0)), pl.BlockSpec((cb, 8, CHUNK), lambda n: (n, 0, 0)),
                   pl.BlockSpec((cb, N_HEADS, CHUNK, CHUNK), lambda n: (n, 0, 0, 0))],
        out_shape=[jax.ShapeDtypeStruct((T, 128), F32), jax.ShapeDtypeStruct((N, 8, CHUNK), F32),
                   jax.ShapeDtypeStruct((N, N_HEADS, CHUNK, CHUNK), F32)],
        compiler_params=_cp("parallel"))(k, bg)


def _tri_inv(lt):
    S = lt.shape[1]

    def body(l_ref, a_ref):
        col = lax.broadcasted_iota(jnp.int32, (CHUNK, S), 0)
        for i in range(CHUNK):
            def step(j, acc):
                return acc - l_ref[pl.ds(i * CHUNK + j, 1), :] * a_ref[j]
            a_ref[i] = lax.fori_loop(0, i, step, (col == i).astype(F32))

    return pl.pallas_call(
        body, name="tri_inv", out_shape=jax.ShapeDtypeStruct((CHUNK, CHUNK, S), F32),
        compiler_params=pltpu.CompilerParams(vmem_limit_bytes=VMEM_LIMIT))(lt)


def _dn_head_terms(qh, kh, vh, beta, gcol, grow):
    ii, jj, incl = _tri(True)
    gam = jnp.exp(jnp.where(incl, gcol - grow, NEG))
    glast = grow[:, CHUNK - 1:CHUNK]
    E = jnp.exp(gcol)
    Fd = jnp.exp(glast - gcol)
    cd = jnp.exp(glast)
    kb = kh * beta
    return dict(ii=ii, jj=jj, gam=gam, E=E, F=Fd, cd=cd, kb=kb, vb=vh * beta, W=kb * E, qE=qh * E, kt=kh * Fd)


def _apply_a(a, u):
    hi, lo = _split(a)
    ub = _bf(u)
    return jnp.dot(hi, ub, preferred_element_type=F32) + jnp.dot(lo, ub, preferred_element_type=F32)


def _apply_at(a, u):
    hi, lo = _split(a)
    ub = _bf(u)
    dn = (((0,), (0,)), ((), ()))
    return (lax.dot_general(hi, ub, dn, preferred_element_type=F32)
            + lax.dot_general(lo, ub, dn, preferred_element_type=F32))


def _dn_scan(q, k, v, bg, gc, gct, a):
    T = q.shape[0]
    N = T // CHUNK

    def body(q_ref, k_ref, v_ref, bg_ref, gc_ref, gct_ref, a_ref, o_ref, sall_ref, s_ref):
        @pl.when(pl.program_id(0) == 0)
        def _():
            s_ref[...] = jnp.zeros_like(s_ref)
        bgv, gcv, gctv = bg_ref[...], gc_ref[...], gct_ref[0]
        H = range(N_HEADS)
        sl = [slice(h * HEAD_DIM, (h + 1) * HEAD_DIM) for h in H]
        q_, k_ = [q_ref[:, s] for s in sl], [k_ref[:, s] for s in sl]
        t = [_dn_head_terms(q_[h], k_[h], v_ref[:, sl[h]], bgv[:, h:h + 1],
                            gcv[:, N_HEADS + h:N_HEADS + h + 1], gctv[N_HEADS + h:N_HEADS + h + 1, :]) for h in H]
        S = [s_ref[h] for h in H]
        for h in H:
            sall_ref[0, h] = S[h]
        WS = [_nn(t[h]["W"], S[h]) for h in H]
        QK = [_nt(q_[h], k_[h]) for h in H]
        qS = [_nn(t[h]["qE"], S[h]) for h in H]
        vn = [_apply_a(a_ref[0, h], t[h]["vb"] - WS[h]) for h in H]
        Pv = [_nn(QK[h] * t[h]["gam"], vn[h]) for h in H]
        kv = [_tn(t[h]["kt"], vn[h]) for h in H]
        for h in H:
            o_ref[:, sl[h]] = qS[h] + Pv[h]
            s_ref[h] = t[h]["cd"] * S[h] + kv[h]

    row512 = pl.BlockSpec((CHUNK, DN_WIDTH), lambda n: (n, 0))
    row128 = pl.BlockSpec((CHUNK, 128), lambda n: (n, 0))
    return pl.pallas_call(
        body, name="dn_scan", grid=(N,),
        in_specs=[row512, row512, row512, row128, row128, pl.BlockSpec((1, 8, CHUNK), lambda n: (n, 0, 0)),
                  pl.BlockSpec((1, N_HEADS, CHUNK, CHUNK), lambda n: (n, 0, 0, 0))],
        out_specs=[row512, pl.BlockSpec((1, N_HEADS, HEAD_DIM, HEAD_DIM), lambda n: (n, 0, 0, 0))],
        out_shape=[jax.ShapeDtypeStruct((T, DN_WIDTH), F32),
                   jax.ShapeDtypeStruct((N, N_HEADS, HEAD_DIM, HEAD_DIM), F32)],
        scratch_shapes=[pltpu.VMEM((N_HEADS, HEAD_DIM, HEAD_DIM), F32)],
        compiler_params=_cp("arbitrary"))(q, k, v, bg, gc, gct, a)


def _dn_scan_bwd(q, k, v, bg, gc, gct, a, sall, do, dep=None):
    T = q.shape[0]
    N = T // CHUNK

    def body(q_ref, k_ref, v_ref, bg_ref, gc_ref, gct_ref, a_ref, sall_ref, do_ref, *rest):
        dq_ref, dk_ref, dv_ref, dbg_ref, ds_ref = rest[-5:]
        @pl.when(pl.program_id(0) == 0)
        def _():
            ds_ref[...] = jnp.zeros_like(ds_ref)
        bgv, gcv, gctv = bg_ref[...], gc_ref[...], gct_ref[0]
        lane = _lane_iota((CHUNK, 128))
        rowi = lax.broadcasted_iota(jnp.int32, (CHUNK, 1), 0)
        H = range(N_HEADS)
        sl = [slice(h * HEAD_DIM, (h + 1) * HEAD_DIM) for h in H]
        q_, k_, v_ = [q_ref[:, s] for s in sl], [k_ref[:, s] for s in sl], [v_ref[:, s] for s in sl]
        dO = [do_ref[:, s] for s in sl]
        beta = [bgv[:, h:h + 1] for h in H]
        t = [_dn_head_terms(q_[h], k_[h], v_[h], beta[h], gcv[:, N_HEADS + h:N_HEADS + h + 1],
                            gctv[N_HEADS + h:N_HEADS + h + 1, :]) for h in H]
        ii, jj = t[0]["ii"], t[0]["jj"]
        gam, E, Fd, cd, kb = ([t[h][n] for h in H] for n in ("gam", "E", "F", "cd", "kb"))
        S = [sall_ref[0, h] for h in H]
        dSn = [ds_ref[h] for h in H]
        A = [a_ref[0, h] for h in H]
        WS = [_nn(t[h]["W"], S[h]) for h in H]
        KK = [_nt(kb[h], k_[h]) for h in H]
        QK = [_nt(q_[h], k_[h]) for h in H]
        ktdS = [_nn(t[h]["kt"], dSn[h]) for h in H]
        d_qE = [_nt(dO[h], S[h]) for h in H]
        vn = [_apply_a(A[h], t[h]["vb"] - WS[h]) for h in H]
        PtdO = [_tn(QK[h] * gam[h], dO[h]) for h in H]
        qEdO = [_tn(t[h]["qE"], dO[h]) for h in H]
        dU = [_apply_at(A[h], PtdO[h] + ktdS[h]) for h in H]
        d_kt = [_nt(vn[h], dSn[h]) for h in H]
        dOvn = [_nt(dO[h], vn[h]) for h in H]
        dUvn = [_nt(dU[h], vn[h]) for h in H]
        dUS = [_nt(dU[h], S[h]) for h in H]
        WdU = [_tn(t[h]["W"], dU[h]) for h in H]
        for h in H:
            ds_ref[h] = cd[h] * dSn[h] + qEdO[h] - WdU[h]
        dQK = [jnp.where(ii >= jj, dOvn[h], 0.0) * gam[h] for h in H]
        dKK = [jnp.where(ii > jj, -dUvn[h], 0.0) * gam[h] for h in H]
        dQKk = [_nn(dQK[h], k_[h]) for h in H]
        dKKk = [_nn(dKK[h], k_[h]) for h in H]
        dQKq = [_tn(dQK[h], q_[h]) for h in H]
        dKKkb = [_tn(dKK[h], kb[h]) for h in H]
        dbeta_arr = jnp.zeros((CHUNK, 128), F32)
        dgc_arr = jnp.zeros((CHUNK, 128), F32)
        for h in H:
            dW = -dUS[h]
            dq_ref[:, sl[h]] = dQKk[h] + d_qE[h] * E[h]
            d_kb = dKKk[h] + dW * E[h]
            dk_ref[:, sl[h]] = dQKq[h] + dKKkb[h] + d_kb * beta[h] + d_kt[h] * Fd[h]
            dv_ref[:, sl[h]] = dU[h] * beta[h]
            Z = dQK[h] * QK[h] + dKK[h] * KK[h]
            d_cd = jnp.sum(S[h] * dSn[h])
            dbeta = jnp.sum(dU[h] * v_[h] + d_kb * k_[h], axis=-1, keepdims=True)
            dE = jnp.sum(dW * kb[h] + d_qE[h] * q_[h], axis=-1, keepdims=True)
            dFF = jnp.sum(d_kt[h] * k_[h], axis=-1, keepdims=True) * Fd[h]
            dgc = (dE * E[h] - dFF + jnp.sum(Z, axis=-1, keepdims=True) - jnp.sum(Z.T, axis=-1, keepdims=True)
                   + jnp.where(rowi == CHUNK - 1, jnp.sum(dFF) + d_cd * cd[h], 0.0))
            dbeta_arr = dbeta_arr + jnp.where(lane == h, dbeta, 0.0)
            dgc_arr = dgc_arr + jnp.where(lane == N_HEADS + h, dgc, 0.0)
        ii, jj, _ = _tri(True)
        rev = (jj >= ii).astype(F32)
        dbg_ref[...] = dbeta_arr + jnp.dot(rev, dgc_arr, precision=lax.Precision.HIGHEST,
                                           preferred_element_type=F32)

    row512 = pl.BlockSpec((CHUNK, DN_WIDTH), lambda n: (N - 1 - n, 0))
    row128 = pl.BlockSpec((CHUNK, 128), lambda n: (N - 1 - n, 0))
    in_specs, args = _with_dep(
        [row512, row512, row512, row128, row128,
         pl.BlockSpec((1, 8, CHUNK), lambda n: (N - 1 - n, 0, 0)),
         pl.BlockSpec((1, N_HEADS, CHUNK, CHUNK), lambda n: (N - 1 - n, 0, 0, 0)),
         pl.BlockSpec((1, N_HEADS, HEAD_DIM, HEAD_DIM), lambda n: (N - 1 - n, 0, 0, 0)), row512],
        [q, k, v, bg, gc, gct, a, sall, do], dep)
    return pl.pallas_call(
        body, name="dn_scan_bwd", grid=(N,), in_specs=in_specs,
        out_specs=[row512, row512, row512, row128],
        out_shape=[jax.ShapeDtypeStruct((T, DN_WIDTH), F32)] * 3 + [jax.ShapeDtypeStruct((T, 128), F32)],
        scratch_shapes=[pltpu.VMEM((N_HEADS, HEAD_DIM, HEAD_DIM), F32)],
        compiler_params=_cp("arbitrary"))(*args)


def _sg_mask():
    ii = lax.broadcasted_iota(jnp.int32, (SG_BLOCK, SG_BLOCK), 0) // CHUNK
    jj = lax.broadcasted_iota(jnp.int32, (SG_BLOCK, SG_BLOCK), 1) // CHUNK
    return jj <= ii


def _mix_fwd(o, p, ong, sgn, sgw, sgbt):
    T = o.shape[0]
    rb = SG_BLOCK

    def body(o_ref, gate_ref, u_ref, vg_ref, ong_ref, sgn_ref, sgw_ref, sgbt_ref, mix_ref):
        mask = _sg_mask()
        gate = gate_ref[...]
        for h in range(N_HEADS):
            sl = slice(h * HEAD_DIM, (h + 1) * HEAD_DIM)
            oh = o_ref[:, sl]
            r = lax.rsqrt(jnp.mean(oh * oh, axis=-1, keepdims=True) + EPS)
            mix_ref[:, sl] = (oh * r * ong_ref[...] * _silu(gate[:, sl])).astype(BF16)
        for gi in range(SG_GROUPS):
            sl = slice(gi * SG_BLOCK, (gi + 1) * SG_BLOCK)
            gv = _gelu(vg_ref[:, sl])
            r = lax.rsqrt(jnp.mean(gv * gv, axis=-1, keepdims=True) + EPS)
            vh = gv * r * sgn_ref[:, sl]
            s = _nn(jnp.where(mask, sgw_ref[gi], 0.0), vh) + sgbt_ref[:, gi:gi + 1]
            mix_ref[:, DN_WIDTH + gi * SG_BLOCK:DN_WIDTH + (gi + 1) * SG_BLOCK] = (_gelu(u_ref[:, sl]) * s).astype(BF16)

    def col(c):
        return pl.BlockSpec((rb, 512), lambda i: (i, c))
    return pl.pallas_call(
        body, name="mix_fwd", grid=(T // rb,),
        in_specs=[pl.BlockSpec((rb, DN_WIDTH), lambda i: (i, 0)), col(3), col(4), col(5),
                  pl.BlockSpec((1, 128), lambda i: (0, 0)), pl.BlockSpec((1, SG_WIDTH), lambda i: (0, 0)),
                  pl.BlockSpec((SG_GROUPS, SG_BLOCK, SG_BLOCK), lambda i: (0, 0, 0)),
                  pl.BlockSpec((SG_BLOCK, 128), lambda i: (0, 0))],
        out_specs=pl.BlockSpec((rb, D_MODEL), lambda i: (i, 0)),
        out_shape=jax.ShapeDtypeStruct((T, D_MODEL), BF16),
        compiler_params=_cp("parallel"))(o, p, p, p, ong, sgn, sgw, sgbt)


def _mix_bwd(o, p, ong, sgn, sgw, sgbt, dmix, dep=None):
    T = o.shape[0]
    rb = SG_BLOCK

    def body(o_ref, gate_ref, u_ref, vg_ref, ong_ref, sgn_ref, sgw_ref, sgbt_ref, dmix_ref, *rest):
        do_ref, dp_ref, gong_ref, gsgn_ref, gsgw_ref, gsgbt_ref = rest[-6:]
        @pl.when(pl.program_id(0) == 0)
        def _():
            gong_ref[...] = jnp.zeros_like(gong_ref)
            gsgn_ref[...] = jnp.zeros_like(gsgn_ref)
            gsgw_ref[...] = jnp.zeros_like(gsgw_ref)
            gsgbt_ref[...] = jnp.zeros_like(gsgbt_ref)
        mask = _sg_mask()
        gate = gate_ref[...]
        lane = _lane_iota((SG_BLOCK, 128))
        for h in range(N_HEADS):
            sl = slice(h * HEAD_DIM, (h + 1) * HEAD_DIM)
            oh = o_ref[:, sl]
            dm = dmix_ref[:, sl]
            r = lax.rsqrt(jnp.mean(oh * oh, axis=-1, keepdims=True) + EPS)
            oh_hat = oh * r
            gt = gate[:, sl]
            sg = _silu(gt)
            dp_ref[:, sl] = (dm * oh_hat * ong_ref[...] * _dsilu(gt)).astype(BF16)
            dn_ = dm * sg
            gong_ref[...] += jnp.sum(dn_ * oh_hat, axis=0, keepdims=True)
            dhat = dn_ * ong_ref[...]
            do_ref[:, sl] = r * (dhat - oh_hat * jnp.mean(dhat * oh_hat, axis=-1, keepdims=True))
        for gi in range(SG_GROUPS):
            sl = slice(gi * SG_BLOCK, (gi + 1) * SG_BLOCK)
            vraw = vg_ref[:, sl]
            gv = _gelu(vraw)
            r = lax.rsqrt(jnp.mean(gv * gv, axis=-1, keepdims=True) + EPS)
            vhat = gv * r
            vn = vhat * sgn_ref[:, sl]
            wm = jnp.where(mask, sgw_ref[gi], 0.0)
            s = _nn(wm, vn) + sgbt_ref[:, gi:gi + 1]
            uraw = u_ref[:, sl]
            dm = dmix_ref[:, DN_WIDTH + gi * SG_BLOCK:DN_WIDTH + (gi + 1) * SG_BLOCK]
            dp_ref[:, DN_WIDTH + gi * SG_BLOCK:DN_WIDTH + (gi + 1) * SG_BLOCK] = (dm * s * _dgelu(uraw)).astype(BF16)
            ds = dm * _gelu(uraw)
            gsgbt_ref[...] += jnp.where(lane == gi, jnp.sum(ds, axis=-1, keepdims=True), 0.0)
            gsgw_ref[gi] += jnp.where(mask, _nt(ds, vn), 0.0)
            dvn = _tn(wm, ds)
            gsgn_ref[:, sl] += jnp.sum(dvn * vhat, axis=0, keepdims=True)
            dhat = dvn * sgn_ref[:, sl]
            dgv = r * (dhat - vhat * jnp.mean(dhat * vhat, axis=-1, keepdims=True))
            dp_ref[:, 2 * DN_WIDTH + gi * SG_BLOCK:2 * DN_WIDTH + (gi + 1) * SG_BLOCK] = (dgv * _dgelu(vraw)).astype(BF16)

    def col(c):
        return pl.BlockSpec((rb, 512), lambda i: (i, c))
    full = lambda *s: pl.BlockSpec(s, lambda i: (0,) * len(s))
    in_specs, args = _with_dep(
        [pl.BlockSpec((rb, DN_WIDTH), lambda i: (i, 0)), col(3), col(4), col(5),
         full(1, 128), full(1, SG_WIDTH), full(SG_GROUPS, SG_BLOCK, SG_BLOCK), full(SG_BLOCK, 128),
         pl.BlockSpec((rb, D_MODEL), lambda i: (i, 0))],
        [o, p, p, p, ong, sgn, sgw, sgbt, dmix], dep)
    return pl.pallas_call(
        body, name="mix_bwd", grid=(T // rb,), in_specs=in_specs,
        out_specs=[pl.BlockSpec((rb, DN_WIDTH), lambda i: (i, 0)), pl.BlockSpec((rb, 3 * 512), lambda i: (i, 0)),
                   full(1, 128), full(1, SG_WIDTH), full(SG_GROUPS, SG_BLOCK, SG_BLOCK), full(SG_BLOCK, 128)],
        out_shape=[jax.ShapeDtypeStruct((T, DN_WIDTH), F32), jax.ShapeDtypeStruct((T, 3 * 512), BF16),
                   jax.ShapeDtypeStruct((1, 128), F32), jax.ShapeDtypeStruct((1, SG_WIDTH), F32),
                   jax.ShapeDtypeStruct((SG_GROUPS, SG_BLOCK, SG_BLOCK), F32),
                   jax.ShapeDtypeStruct((SG_BLOCK, 128), F32)],
        compiler_params=_cp("arbitrary"))(*args)


def _pad_lanes(row, offset=0):
    n = row.shape[1]
    return jnp.pad(row, ((0, 0), (offset, 128 - n - offset)))


def _local_step(x, tgt, w, dep=None, late_weights=None, on_grad=None):
    T = x.shape[0]
    N = T // CHUNK
    on_grad = on_grad or (lambda name, g: None)
    alog_row = _pad_lanes(w["dn_a_log"], N_HEADS)
    dtb_row = _pad_lanes(w["dn_dt_bias"], N_HEADS)
    sgbt = jnp.pad(w["sg_b"].T, ((0, 0), (0, 128 - SG_GROUPS)))

    h1 = _rms_fwd("rms_attn", x, w["attn_norm_g"], dep=dep)
    p = _mm_nn("in_proj", h1, w["w_in"], F32, 512, PROJ_PAD)
    q, k, v, bg = _dn_act(p, w["dn_conv_w"], alog_row, dtb_row)
    gc, gct, lmat = _dn_chunk(k, bg)
    lt = lmat.reshape(N * N_HEADS, CHUNK * CHUNK).T
    at = _tri_inv(lt)
    a = at.reshape(CHUNK * CHUNK, N * N_HEADS).T.reshape(N, N_HEADS, CHUNK, CHUNK)
    o, sall = _dn_scan(q, k, v, bg, gc, gct, a)
    mix = _mix_fwd(o, p, w["dn_out_norm_g"], w["sg_norm_g"], w["sg_w"], sgbt)
    if late_weights is not None:
        w = {**w, **late_weights(mix)}
    x2 = _mm_nn("out_proj", mix, w["w_out"], F32, 512, 1024, res=x)
    h2 = _rms_fwd("rms_ffn", x2, w["ffn_norm_g"])
    up = _mm_nn("up_proj", h2, w["w_up"], F32, 512, D_FF)
    act = _ffn_act(up, w["ffn_conv_w"], w["ffn_conv_b"])
    x3 = _mm_nn("down_proj", act, w["w_down"], F32, 512, 1024, res=x2)
    loss, dx3, g_final = _loss_head(x3, tgt, w["final_norm_g"])

    dact = _mm_nt("d_act", dx3, w["w_down"], F32, 512, D_FF)
    g_w_down = _mm_tn("g_w_down", act, dx3, D_FF, 1024, 512)
    tok = on_grad("w_down", g_w_down)
    dup, g_ffn_conv_w, g_ffn_conv_b = _ffn_act_bwd(up, dact, w["ffn_conv_w"], w["ffn_conv_b"], dep=tok)
    g_w_up = _mm_tn("g_w_up", h2, dup, 1024, 2 * D_FF // 4, 512, col_major_tiles=True)
    tok = on_grad("w_up", g_w_up)
    dh2 = _mm_nt("d_h2", dup, w["w_up"], F32, 512, 1024, dep=tok)
    dx2, g_ffn_norm = _rms_bwd("rms_ffn_bwd", dh2, x2, w["ffn_norm_g"], dx3)
    dmix = _mm_nt("d_mix", dx2, w["w_out"], F32, 512, 1024)
    g_w_out = _mm_tn("g_w_out", mix, dx2, 1024, 1024, 1024)
    tok = on_grad("w_out", g_w_out)
    do, dp_mid, g_ong, g_sgn, g_sgw, g_sgbt = _mix_bwd(o, p, w["dn_out_norm_g"], w["sg_norm_g"], w["sg_w"], sgbt,
                                                      dmix, dep=tok)
    early = dict(dn_out_norm_g=g_ong, sg_norm_g=g_sgn, sg_w=g_sgw, sg_b=g_sgbt[:, :SG_GROUPS].T,
                 ffn_norm_g=g_ffn_norm, ffn_conv_w=g_ffn_conv_w, ffn_conv_b=g_ffn_conv_b, final_norm_g=g_final)
    tok = on_grad("small_early", early)
    dq, dk, dv, dbg = _dn_scan_bwd(q, k, v, bg, gc, gct, a, sall, do, dep=tok)
    dp, g_dn_conv_w, g_ad = _dn_act_bwd(p, w["dn_conv_w"], alog_row, dtb_row, dq, dk, dv, dbg, dp_mid)
    g_w_in = _mm_tn("g_w_in", h1, dp, 1024, PROJ_PAD, 512)[:, :PROJ_COLS]
    tok = on_grad("w_in", g_w_in)
    dh1 = _mm_nt("d_h1", dp, w["w_in"], F32, 512, 1024, dep=tok)
    grad_x, g_attn_norm = _rms_bwd("rms_attn_bwd", dh1, x, w["attn_norm_g"], dx2)

    grads = dict(
        attn_norm_g=g_attn_norm, w_in=g_w_in, dn_conv_w=g_dn_conv_w,
        dn_a_log=g_ad[0:1, N_HEADS:2 * N_HEADS], dn_dt_bias=g_ad[1:2, N_HEADS:2 * N_HEADS],
        w_out=g_w_out, w_up=g_w_up, w_down=g_w_down, **early)
    return loss, grad_x, grads


def _me():
    return lax.axis_index("x"), lax.axis_index("y"), lax.axis_index("c")


def _peer(rel):
    x, y, c = _me()
    return {"x": (1 - x, y, c), "y": (x, 1 - y, c), "xy": (1 - x, 1 - y, c), "c": (x, y, 1 - c)}[rel]


def _chip_of(dev):
    return 2 * dev[0] + dev[1]


CHIP_RELS = ("x", "y", "xy")


def _run_copies(copies, sends, recvs):
    for cp in copies:
        cp.start()
    for cp in recvs:
        cp.wait_recv()
    for cp in sends:
        cp.wait_send()


def _gather_first(w_shard, small_shard):
    R = w_shard.shape[0]
    r2 = R // 2

    def body(w_ref, s_ref, w_out, s_out, send_sems, recv_sems):
        x, y, c = _me()
        me = _chip_of((x, y))
        sib = _peer("c")

        def half(chip, core):
            return w_out.at[chip, pl.ds(pl.multiple_of(core * r2, 8), r2), :]

        def copy(k, src, dst, to):
            return pltpu.make_async_remote_copy(src_ref=src, dst_ref=dst, send_sem=send_sems.at[k],
                                                recv_sem=recv_sems.at[k], device_id=to, device_id_type=MESH)

        own_rows = w_ref.at[pl.ds(pl.multiple_of(c * r2, 8), r2), :]
        first = [copy(r, own_rows, half(me, c), _peer(rel)) for r, rel in enumerate(CHIP_RELS)]
        first += [copy(3 + r, s_ref, s_out.at[me], _peer(rel)) for r, rel in enumerate(CHIP_RELS)]
        for cp in first:
            cp.start()
        passed = []
        for r, rel in enumerate(CHIP_RELS):
            their = _chip_of(_peer(rel))
            copy(r, own_rows, half(their, c), _peer(rel)).wait_recv()
            fwd = copy(6 + r, half(their, c), half(their, c), sib)
            fwd.start()
            passed.append(fwd)
        for r, rel in enumerate(CHIP_RELS):
            their = _chip_of(_peer(rel))
            copy(3 + r, s_ref, s_out.at[their], _peer(rel)).wait_recv()
            copy(6 + r, own_rows, half(their, 1 - c), sib).wait_recv()
        for cp in first + passed:
            cp.wait_send()

    w_all, s_all = pl.pallas_call(
        body, name="gather_first", in_specs=[ANY, ANY], out_specs=[ANY, ANY],
        out_shape=[jax.ShapeDtypeStruct((4,) + w_shard.shape, w_shard.dtype),
                   jax.ShapeDtypeStruct((4,) + small_shard.shape, small_shard.dtype)],
        scratch_shapes=[pltpu.SemaphoreType.DMA((9,)), pltpu.SemaphoreType.DMA((9,))])(w_shard, small_shard)
    me = _chip_of(_me())
    return (lax.dynamic_update_index_in_dim(w_all, w_shard, me, 0),
            lax.dynamic_update_index_in_dim(s_all, small_shard, me, 0))


OTHERS = tuple((fx, fy, fc) for fx in (0, 1) for fy in (0, 1) for fc in (0, 1) if (fx, fy, fc) != (0, 0, 0))


def _other(flip):
    x, y, c = _me()
    return (x ^ flip[0], y ^ flip[1], c ^ flip[2])


def _linear(dev):
    return 4 * dev[0] + 2 * dev[1] + dev[2]


def _exchange_small(small):
    def body(small_ref, out_ref, send_sems, recv_sems):
        my_slot = _linear(_me())
        sends, recvs = [], []
        for k, flip in enumerate(OTHERS):
            peer = _other(flip)
            sends.append(pltpu.make_async_remote_copy(
                src_ref=small_ref, dst_ref=out_ref.at[my_slot], send_sem=send_sems.at[k], recv_sem=recv_sems.at[k],
                device_id=peer, device_id_type=MESH))
            recvs.append(pltpu.make_async_remote_copy(
                src_ref=small_ref, dst_ref=out_ref.at[_linear(peer)], send_sem=send_sems.at[k],
                recv_sem=recv_sems.at[k], device_id=peer, device_id_type=MESH))
        _run_copies(sends, sends, recvs)

    out = pl.pallas_call(
        body, name="exchange_small", in_specs=[ANY], out_specs=ANY,
        out_shape=jax.ShapeDtypeStruct((8,) + small.shape, small.dtype),
        scratch_shapes=[pltpu.SemaphoreType.DMA((7,)), pltpu.SemaphoreType.DMA((7,))])(small)
    return lax.dynamic_update_index_in_dim(out, small, _linear(_me()), 0)


def _pair_swap(halves):
    n = len(halves)

    def body(*refs):
        src, out = refs[:n], refs[n:2 * n]
        send_sems, recv_sems = refs[2 * n:]
        sib = _peer("c")
        copies = [pltpu.make_async_remote_copy(
            src_ref=src[i], dst_ref=out[i], send_sem=send_sems.at[i], recv_sem=recv_sems.at[i],
            device_id=sib, device_id_type=MESH) for i in range(n)]
        _run_copies(copies, copies, copies)

    return pl.pallas_call(
        body, name="pair_swap", in_specs=[ANY] * n, out_specs=[ANY] * n,
        out_shape=[jax.ShapeDtypeStruct(h.shape, h.dtype) for h in halves],
        scratch_shapes=[pltpu.SemaphoreType.DMA((n,)), pltpu.SemaphoreType.DMA((n,))])(*halves)


HBM = pl.BlockSpec(memory_space=pltpu.HBM)
SEM = pl.BlockSpec(memory_space=pltpu.SEMAPHORE)
EFFECT = pltpu.SideEffectType.DATAFLOW_SIDE_EFFECTING


def _hbm(a):
    return pltpu.with_memory_space_constraint(a, pltpu.HBM)


def _transfer_start(name, srcs, lands, n_copies, make_copies, after=None):
    n, m = len(srcs), len(lands)

    def body(*refs):
        src, land = refs[:n], refs[n:n + m]
        outs = refs[n + m + (after is not None):]
        send_sems, recv_sems, token = outs[0], outs[1], outs[-1]
        for cp in make_copies(src, land, send_sems, recv_sems):
            cp.start()
        token[...] = jnp.zeros_like(token)

    arrs = list(srcs) + list(lands)
    in_specs, args = _with_dep([HBM] * (n + m), [_hbm(a) for a in arrs], after)
    out = pl.pallas_call(
        body, name=name,
        out_shape=(pltpu.SemaphoreType.DMA((n_copies,)), pltpu.SemaphoreType.DMA((n_copies,)),
                   *[pltpu.HBM(a.shape, a.dtype) for a in arrs], jax.ShapeDtypeStruct((8, 128), F32)),
        in_specs=in_specs,
        out_specs=(SEM, SEM, *[HBM] * (n + m), pl.BlockSpec(memory_space=pltpu.VMEM)),
        input_output_aliases={i: 2 + i for i in range(n + m)},
        compiler_params=pltpu.CompilerParams(has_side_effects=EFFECT))(*args)
    return out[0], out[1], list(out[2:2 + n]), list(out[2 + n:2 + n + m]), out[-1]


def _transfer_wait(name, send_sems, recv_sems, srcs, lands, make_copies, after):
    n, m = len(srcs), len(lands)

    def body(*refs):
        src, land = refs[:n], refs[n:n + m]
        s_sems, r_sems = refs[n + m], refs[n + m + 1]
        for cp in make_copies(src, land, s_sems, r_sems):
            cp.wait_send()
            cp.wait_recv()

    arrs = list(srcs) + list(lands)
    out = pl.pallas_call(
        body, name=name, out_shape=tuple(pltpu.HBM(a.shape, a.dtype) for a in arrs),
        in_specs=[HBM] * (n + m) + [SEM, SEM, ANY], out_specs=tuple([HBM] * (n + m)),
        input_output_aliases={i: i for i in range(n + m)},
        compiler_params=pltpu.CompilerParams(has_side_effects=EFFECT))(*arrs, send_sems, recv_sems, after)
    return list(out[:n]), list(out[n:])


def _gather_copies(src, land, send_sems, recv_sems):
    me = _chip_of(_me())
    copies = []
    for i in range(len(src)):
        for r, rel in enumerate(CHIP_RELS):
            k = 3 * i + r
            copies.append(pltpu.make_async_remote_copy(
                src_ref=src[i], dst_ref=land[i].at[me], send_sem=send_sems.at[k], recv_sem=recv_sems.at[k],
                device_id=_peer(rel), device_id_type=MESH))
    return copies


def _small_copies(src, land, send_sems, recv_sems):
    my_slot = _linear(_me())
    return [pltpu.make_async_remote_copy(
        src_ref=src[0], dst_ref=land[0].at[my_slot], send_sem=send_sems.at[k], recv_sem=recv_sems.at[k],
        device_id=_other(flip), device_id_type=MESH) for k, flip in enumerate(OTHERS)]


def _pieces_copies(src, land, send_sems, recv_sems):
    copies = []
    for k, flip in enumerate(OTHERS):
        peer = _other(flip)
        copies.append(pltpu.make_async_remote_copy(
            src_ref=src[0].at[_linear(peer)], dst_ref=land[0].at[k], send_sem=send_sems.at[k],
            recv_sem=recv_sems.at[k], device_id=peer, device_id_type=MESH))
    return copies


def _row_block(rows, cols, budget=2 * 1024 * 1024):
    rb = max(8, (budget // (4 * cols)) // 8 * 8)
    while rows % rb:
        rb -= 8
    return rb if rb > 0 else rows


def _sum_slots(name, first, rest):
    R, Cc = first.shape
    K = rest.shape[0]
    rb = _row_block(R, Cc)

    def body(f_ref, r_ref, o_ref):
        acc = f_ref[...].astype(F32)
        for j in range(K):
            acc = acc + r_ref[j].astype(F32)
        o_ref[...] = acc

    return pl.pallas_call(
        body, name=name, grid=(R // rb,),
        in_specs=[pl.BlockSpec((rb, Cc), lambda i: (i, 0)), pl.BlockSpec((K, rb, Cc), lambda i: (0, i, 0))],
        out_specs=pl.BlockSpec((rb, Cc), lambda i: (i, 0)),
        out_shape=jax.ShapeDtypeStruct((R, Cc), F32), compiler_params=_cp("parallel"))(first, rest)


def _adamw_math(w, gv, m, v):
    mn = ADAM_B1 * m + (1.0 - ADAM_B1) * gv
    vn = ADAM_B2 * v + (1.0 - ADAM_B2) * (gv * gv)
    m_hat = mn / (1.0 - ADAM_B1 ** ADAM_STEP)
    v_hat = vn / (1.0 - ADAM_B2 ** ADAM_STEP)
    return -ADAM_LR * (m_hat / (jnp.sqrt(v_hat) + ADAM_EPS) + ADAM_WD * w), mn, vn


def _adamw_halves(name, w, mine, theirs, m, v, core):
    R, Cc = w.shape
    r2 = R // 2
    rb = _row_block(r2, Cc, 1024 * 1024)
    nb2 = r2 // rb

    def body(c_ref, w_ref, mine_ref, theirs_ref, m_ref, v_ref, g_ref, d_ref, mo_ref, vo_ref):
        is_mine = (pl.program_id(0) // nb2) == c_ref[0]
        gv = jnp.where(is_mine, mine_ref[...], theirs_ref[...])
        g_ref[...] = gv
        d_ref[...], mo_ref[...], vo_ref[...] = _adamw_math(w_ref[...], gv, m_ref[...], v_ref[...])

    blk = pl.BlockSpec((rb, Cc), lambda i, c: (i, 0))
    half = lambda own: pl.BlockSpec(
        (rb, Cc), lambda i, c: (jnp.clip(i - (c[0] if own else 1 - c[0]) * nb2, 0, nb2 - 1), 0))
    return pl.pallas_call(
        body, name=name,
        grid_spec=pltpu.PrefetchScalarGridSpec(
            num_scalar_prefetch=1, grid=(2 * nb2,), in_specs=[blk, half(True), half(False), blk, blk],
            out_specs=[blk] * 4),
        out_shape=[jax.ShapeDtypeStruct((R, Cc), F32)] * 4, compiler_params=_cp("parallel"))(core, w, mine, theirs, m, v)


def _adamw(name, w, g, m, v):
    R, Cc = w.shape
    rb = _row_block(R, Cc, 1024 * 1024)

    def body(w_ref, g_ref, m_ref, v_ref, d_ref, mo_ref, vo_ref):
        d_ref[...], mo_ref[...], vo_ref[...] = _adamw_math(w_ref[...], g_ref[...], m_ref[...], v_ref[...])

    blk = pl.BlockSpec((rb, Cc), lambda i: (i, 0))
    return pl.pallas_call(
        body, name=name, grid=(R // rb,), in_specs=[blk] * 4, out_specs=[blk] * 3,
        out_shape=[jax.ShapeDtypeStruct((R, Cc), F32)] * 3, compiler_params=_cp("parallel"))(w, g, m, v)


def _pack(arrs):
    rows = []
    for a in arrs:
        flat = a.reshape(-1)
        pad = (-flat.shape[0]) % 128
        rows.append(jnp.pad(flat, (0, pad)).reshape(-1, 128))
    buf = jnp.concatenate(rows, axis=0)
    return jnp.pad(buf, ((0, (-buf.shape[0]) % 8), (0, 0)))


def _unpack(buf, shapes):
    out, r = [], 0
    for s in shapes:
        n = math.prod(s)
        nr = -(-n // 128)
        out.append(buf[r:r + nr].reshape(-1)[:n].reshape(s))
        r += nr
    return out


BIG = ("w_in", "w_out", "w_up", "w_down")
CONV = ("dn_conv_w", "ffn_conv_w")
REPL = ("attn_norm_g", "dn_a_log", "dn_dt_bias", "dn_out_norm_g", "sg_norm_g", "sg_w", "sg_b",
        "ffn_norm_g", "ffn_conv_b", "final_norm_g")
ORDER = ("attn_norm_g", "w_in", "dn_conv_w", "dn_a_log", "dn_dt_bias", "dn_out_norm_g", "sg_norm_g", "sg_w",
         "sg_b", "w_out", "ffn_norm_g", "w_up", "ffn_conv_w", "ffn_conv_b", "w_down", "final_norm_g")


def kernel(x, attn_norm_g, w_in, dn_conv_w, dn_a_log, dn_dt_bias, dn_out_norm_g, sg_norm_g, sg_w, sg_b, w_out, ffn_norm_g, w_up, ffn_conv_w, ffn_conv_b, w_down, final_norm_g, loss_target, m_attn_norm_g, m_w_in, m_dn_conv_w, m_dn_a_log, m_dn_dt_bias, m_dn_out_norm_g, m_sg_norm_g, m_sg_w, m_sg_b, m_w_out, m_ffn_norm_g, m_w_up, m_ffn_conv_w, m_ffn_conv_b, m_w_down, m_final_norm_g, v_attn_norm_g, v_w_in, v_dn_conv_w, v_dn_a_log, v_dn_dt_bias, v_dn_out_norm_g, v_sg_norm_g, v_sg_w, v_sg_b, v_w_out, v_ffn_norm_g, v_w_up, v_ffn_conv_w, v_ffn_conv_b, v_w_down, v_final_norm_g):
    W = dict(attn_norm_g=attn_norm_g, w_in=w_in, dn_conv_w=dn_conv_w, dn_a_log=dn_a_log, dn_dt_bias=dn_dt_bias,
             dn_out_norm_g=dn_out_norm_g, sg_norm_g=sg_norm_g, sg_w=sg_w, sg_b=sg_b, w_out=w_out,
             ffn_norm_g=ffn_norm_g, w_up=w_up, ffn_conv_w=ffn_conv_w, ffn_conv_b=ffn_conv_b, w_down=w_down,
             final_norm_g=final_norm_g)
    Mo = dict(attn_norm_g=m_attn_norm_g, w_in=m_w_in, dn_conv_w=m_dn_conv_w, dn_a_log=m_dn_a_log,
              dn_dt_bias=m_dn_dt_bias, dn_out_norm_g=m_dn_out_norm_g, sg_norm_g=m_sg_norm_g, sg_w=m_sg_w,
              sg_b=m_sg_b, w_out=m_w_out, ffn_norm_g=m_ffn_norm_g, w_up=m_w_up, ffn_conv_w=m_ffn_conv_w,
              ffn_conv_b=m_ffn_conv_b, w_down=m_w_down, final_norm_g=m_final_norm_g)
    Vo = dict(attn_norm_g=v_attn_norm_g, w_in=v_w_in, dn_conv_w=v_dn_conv_w, dn_a_log=v_dn_a_log,
              dn_dt_bias=v_dn_dt_bias, dn_out_norm_g=v_dn_out_norm_g, sg_norm_g=v_sg_norm_g, sg_w=v_sg_w,
              sg_b=v_sg_b, w_out=v_w_out, ffn_norm_g=v_ffn_norm_g, w_up=v_w_up, ffn_conv_w=v_ffn_conv_w,
              ffn_conv_b=v_ffn_conv_b, w_down=v_w_down, final_norm_g=v_final_norm_g)
    xi, yi, ci = lax.axis_index("x"), lax.axis_index("y"), lax.axis_index("c")
    chip = 2 * xi + yi

    me_lin = 4 * xi + 2 * yi + ci

    g_in, g_dnc = _gather_first(w_in[0].astype(BF16), dn_conv_w[0])
    late = ("w_out", "w_up", "w_down", "ffn_conv_w")
    late_shards = [W[n][0].astype(BF16) for n in late[:3]] + [ffn_conv_w[0]]
    late_lands = [lax.dynamic_update_index_in_dim(lax.empty((4,) + s.shape, s.dtype), s, chip, 0) for s in late_shards]
    n_late = 3 * len(late_shards)
    ssem, rsem, late_src, late_lands, token = _transfer_start("gather_rest_start", late_shards, late_lands,
                                                              n_late, _gather_copies, after=g_in)

    def late_weights(after):
        _, (g_out, g_up, g_down, g_ffc) = _transfer_wait("gather_rest_wait", ssem, rsem, late_src, late_lands,
                                                         _gather_copies, after)
        return dict(w_out=g_out.reshape(D_MODEL, D_MODEL), w_up=g_up.transpose(1, 0, 2).reshape(D_MODEL, 2 * D_FF),
                    w_down=g_down.reshape(D_FF, D_MODEL), ffn_conv_w=g_ffc.transpose(1, 0, 2).reshape(3, 2 * D_FF))

    full = dict(
        w_in=jnp.pad(g_in.transpose(1, 0, 2).reshape(D_MODEL, PROJ_COLS), ((0, 0), (0, PROJ_PAD - PROJ_COLS))),
        dn_conv_w=g_dnc.transpose(1, 0, 2).reshape(4, 3 * DN_WIDTH),
        attn_norm_g=attn_norm_g, dn_a_log=dn_a_log, dn_dt_bias=dn_dt_bias, dn_out_norm_g=dn_out_norm_g,
        sg_norm_g=sg_norm_g, sg_w=sg_w[0], sg_b=sg_b[0], ffn_norm_g=ffn_norm_g, ffn_conv_b=ffn_conv_b,
        final_norm_g=final_norm_g[None])

    pending = {}
    early_names = ("dn_out_norm_g", "sg_norm_g", "sg_w", "sg_b", "ffn_norm_g", "ffn_conv_w", "ffn_conv_b",
                   "final_norm_g")
    late_names = ("attn_norm_g", "dn_a_log", "dn_dt_bias", "dn_conv_w")

    def on_grad(name, gw):
        if name == "small_early":
            buf = _pack([gw[n] for n in early_names])
            land = lax.dynamic_update_index_in_dim(lax.empty((8,) + buf.shape, F32), buf, me_lin, 0)
            s_sem, r_sem, src, lands, tok = _transfer_start("small_early_start", [buf], [land], 7, _small_copies)
            pending[name] = (s_sem, r_sem, src, lands)
            return tok
        g8 = gw.reshape(D_MODEL, 4, -1).transpose(1, 0, 2) if name == "w_in" else gw
        g8 = g8.reshape(8, -1, g8.shape[-1])
        land = lax.empty((7,) + g8.shape[1:], BF16)
        s_sem, r_sem, src, lands, tok = _transfer_start(f"reduce_{name}_start", [g8], [land], 7, _pieces_copies)
        pending[name] = (s_sem, r_sem, src, lands)
        return tok

    loss_row, grad_x, g = _local_step(x[0], loss_target[0], full, dep=token, late_weights=late_weights,
                                      on_grad=on_grad)

    small_names = REPL + CONV
    late_all = _exchange_small(_pack([g[n] for n in late_names] + [loss_row]))
    late_sum = _sum_slots("sum_small_late", late_all[0], late_all[1:])
    s_sem, r_sem, src, lands = pending["small_early"]
    _, (early_all,) = _transfer_wait("small_early_wait", s_sem, r_sem, src, lands, _small_copies, grad_x)
    early_sum = _sum_slots("sum_small_early", early_all[0], early_all[1:])
    *late_vals, loss_sum = _unpack(late_sum, [g[n].shape for n in late_names] + [loss_row.shape])
    loss = loss_sum[0, 0]
    sg = dict(zip(late_names, late_vals))
    sg.update(zip(early_names, _unpack(early_sum, [g[n].shape for n in early_names])))
    sg["dn_conv_w"] = lax.dynamic_slice_in_dim(sg["dn_conv_w"], chip * (3 * DN_WIDTH // 4), 3 * DN_WIDTH // 4, axis=1)
    sg["ffn_conv_w"] = lax.dynamic_slice_in_dim(sg["ffn_conv_w"], chip * (2 * D_FF // 4), 2 * D_FF // 4, axis=1)

    halves = []
    for n in ("w_down", "w_up", "w_out", "w_in"):
        s_sem, r_sem, src, lands = pending[n]
        sent, got = _transfer_wait(f"reduce_{n}_wait", s_sem, r_sem, src, lands, _pieces_copies, grad_x)
        own = lax.dynamic_index_in_dim(sent[0], me_lin, axis=0, keepdims=False)
        halves.append(_sum_slots(f"sum_{n}", own, got[0]))
    theirs = _pair_swap(halves)
    core = ci.astype(jnp.int32).reshape(1)
    grads, delta, new_m, new_v = {}, {}, {}, {}
    for n, mine_h, their_h in zip(("w_down", "w_up", "w_out", "w_in"), halves, theirs):
        shp = W[n].shape
        gr, d, mn, vn = _adamw_halves(f"adamw_{n}", W[n][0], mine_h, their_h, Mo[n][0], Vo[n][0], core)
        grads[n], delta[n], new_m[n], new_v[n] = gr.reshape(shp), d.reshape(shp), mn.reshape(shp), vn.reshape(shp)
    shapes = [W[n].shape for n in small_names]
    for n in small_names:
        grads[n] = sg[n].reshape(W[n].shape)
    d, mn, vn = _adamw("adamw_small", _pack([W[n] for n in small_names]), _pack([grads[n] for n in small_names]),
                       _pack([Mo[n] for n in small_names]), _pack([Vo[n] for n in small_names]))
    for dst, buf in ((delta, d), (new_m, mn), (new_v, vn)):
        dst.update(zip(small_names, _unpack(buf, shapes)))

    return (loss, grad_x[None], *[grads[n] for n in ORDER], *[delta[n] for n in ORDER],
            *[new_m[n] for n in ORDER], *[new_v[n] for n in ORDER])
```

```python
import functools
import math

import jax
import jax.numpy as jnp
from jax import lax
from jax.experimental import pallas as pl
from jax.experimental.pallas import tpu as pltpu

F32 = jnp.float32
BF16 = jnp.bfloat16

D_MODEL = 1024
CHUNK = 64
SCAN_CHUNKS = 2
HEAD_DIM = 128
N_HEADS = 4
DN_WIDTH = 512
SG_WIDTH = 512
SG_GROUPS = 4
SG_BLOCK = 128
D_FF = 2816
PROJ_COLS = 3080
PROJ_PAD = 3200
BA_COL = 3072
EPS = 1e-6
NEG = -1e30
VMEM_LIMIT = 56 * 1024 * 1024

ADAM_LR = 0.001
ADAM_B1 = 0.9
ADAM_B2 = 0.999
ADAM_EPS = 1e-08
ADAM_WD = 0.01
ADAM_STEP = 10

MESH = pl.DeviceIdType.MESH
ANY = pl.BlockSpec(memory_space=pl.ANY)


def _cp(*sem):
    return pltpu.CompilerParams(dimension_semantics=sem, vmem_limit_bytes=VMEM_LIMIT)


def _bf(a):
    return a.astype(BF16)


def _nn(a, b):
    return jnp.dot(_bf(a), _bf(b), preferred_element_type=F32)


def _nt(a, b):
    return lax.dot_general(_bf(a), _bf(b), (((1,), (1,)), ((), ())), preferred_element_type=F32)


def _tn(a, b):
    return lax.dot_general(_bf(a), _bf(b), (((0,), (0,)), ((), ())), preferred_element_type=F32)


def _split(a):
    hi = _bf(a)
    return hi, _bf(a - hi.astype(F32))


def _sigmoid(x):
    return 0.5 * jnp.tanh(0.5 * x) + 0.5


def _silu(x):
    return x * _sigmoid(x)


def _dsilu(x):
    s = _sigmoid(x)
    return s * (1.0 + x * (1.0 - s))


_GELU_C = math.sqrt(2.0 / math.pi)
_GELU_A = 0.044715


def _gelu(x):
    return 0.5 * x * (1.0 + jnp.tanh(_GELU_C * (x + _GELU_A * x * x * x)))


def _dgelu(x):
    t = jnp.tanh(_GELU_C * (x + _GELU_A * x * x * x))
    return 0.5 * (1.0 + t) + 0.5 * x * (1.0 - t * t) * _GELU_C * (1.0 + 3.0 * _GELU_A * x * x)


def _softplus(x):
    return jnp.maximum(x, 0.0) + jnp.log(1.0 + jnp.exp(-jnp.abs(x)))


def _mm_nn(name, a, b, out_dtype, tm, tn, res=None):
    M, K = a.shape
    N = b.shape[1]
    tm, tn = min(tm, M), min(tn, N)

    def body(*refs):
        a_ref, b_ref = refs[0], refs[1]
        o_ref = refs[-1]
        acc = _nn(a_ref[...], b_ref[...])
        if res is not None:
            acc = acc + refs[2][...]
        o_ref[...] = acc.astype(o_ref.dtype)

    in_specs = [pl.BlockSpec((tm, K), lambda j, i: (i, 0)), pl.BlockSpec((K, tn), lambda j, i: (0, j))]
    args = [a, b]
    if res is not None:
        in_specs.append(pl.BlockSpec((tm, tn), lambda j, i: (i, j)))
        args.append(res)
    return pl.pallas_call(
        body, name=name, grid=(N // tn, M // tm), in_specs=in_specs,
        out_specs=pl.BlockSpec((tm, tn), lambda j, i: (i, j)),
        out_shape=jax.ShapeDtypeStruct((M, N), out_dtype),
        compiler_params=_cp("parallel", "parallel"))(*args)


def _with_dep(in_specs, args, dep):
    if dep is None:
        return in_specs, args
    return in_specs + [ANY], args + [dep]


def _mm_nt(name, a, b, out_dtype, tm, tn, dep=None):
    M, K = a.shape
    N = b.shape[0]
    tm, tn = min(tm, M), min(tn, N)

    def body(a_ref, b_ref, *rest):
        o_ref = rest[-1]
        o_ref[...] = _nt(a_ref[...], b_ref[...]).astype(o_ref.dtype)

    in_specs, args = _with_dep(
        [pl.BlockSpec((tm, K), lambda i, j: (i, 0)), pl.BlockSpec((tn, K), lambda i, j: (j, 0))], [a, b], dep)
    return pl.pallas_call(
        body, name=name, grid=(M // tm, N // tn), in_specs=in_specs,
        out_specs=pl.BlockSpec((tm, tn), lambda i, j: (i, j)),
        out_shape=jax.ShapeDtypeStruct((M, N), out_dtype),
        compiler_params=_cp("parallel", "parallel"))(*args)


def _mm_tn(name, a, b, tm, tn, tk, col_major_tiles=False):
    T, M = a.shape
    N = b.shape[1]
    tm, tn, tk = min(tm, M), min(tn, N), min(tk, T)
    nk = T // tk

    def body(a_ref, b_ref, o_ref, acc_ref):
        k = pl.program_id(2)

        @pl.when(k == 0)
        def _():
            acc_ref[...] = jnp.zeros_like(acc_ref)
        acc_ref[...] += _tn(a_ref[...], b_ref[...])

        @pl.when(k == nk - 1)
        def _():
            o_ref[...] = acc_ref[...].astype(BF16).reshape(o_ref.shape)

    if col_major_tiles:
        assert tm == M
        out_spec = pl.BlockSpec((1, tm, tn), lambda i, j, k: (j, 0, 0))
        out_shape = jax.ShapeDtypeStruct((N // tn, M, tn), BF16)
    else:
        out_spec = pl.BlockSpec((tm, tn), lambda i, j, k: (i, j))
        out_shape = jax.ShapeDtypeStruct((M, N), BF16)
    return pl.pallas_call(
        body, name=name, grid=(M // tm, N // tn, nk),
        in_specs=[pl.BlockSpec((tk, tm), lambda i, j, k: (k, i)), pl.BlockSpec((tk, tn), lambda i, j, k: (k, j))],
        out_specs=out_spec, out_shape=out_shape, scratch_shapes=[pltpu.VMEM((tm, tn), F32)],
        compiler_params=_cp("parallel", "parallel", "arbitrary"))(a, b)


def _rms_fwd(name, x, g, rb=512, dep=None):
    T, Dm = x.shape
    rb = min(rb, T)

    def body(x_ref, g_ref, *rest):
        h_ref = rest[-1]
        xv = x_ref[...]
        r = lax.rsqrt(jnp.mean(xv * xv, axis=-1, keepdims=True) + EPS)
        h_ref[...] = (xv * r * g_ref[...]).astype(BF16)

    in_specs, args = _with_dep(
        [pl.BlockSpec((rb, Dm), lambda i: (i, 0)), pl.BlockSpec((1, Dm), lambda i: (0, 0))], [x, g], dep)
    return pl.pallas_call(
        body, name=name, grid=(T // rb,), in_specs=in_specs,
        out_specs=pl.BlockSpec((rb, Dm), lambda i: (i, 0)),
        out_shape=jax.ShapeDtypeStruct((T, Dm), BF16), compiler_params=_cp("parallel"))(*args)


def _rms_bwd(name, dh, x, g, dres, rb=512):
    T, Dm = x.shape
    rb = min(rb, T)

    def body(dh_ref, x_ref, g_ref, dres_ref, dx_ref, gg_ref):
        @pl.when(pl.program_id(0) == 0)
        def _():
            gg_ref[...] = jnp.zeros_like(gg_ref)
        xv = x_ref[...]
        r = lax.rsqrt(jnp.mean(xv * xv, axis=-1, keepdims=True) + EPS)
        xh = xv * r
        dhv = dh_ref[...]
        gg_ref[...] += jnp.sum(dhv * xh, axis=0, keepdims=True)
        dxh = dhv * g_ref[...]
        dx_ref[...] = dres_ref[...] + r * (dxh - xh * jnp.mean(dxh * xh, axis=-1, keepdims=True))

    row = pl.BlockSpec((rb, Dm), lambda i: (i, 0))
    vec = pl.BlockSpec((1, Dm), lambda i: (0, 0))
    return pl.pallas_call(
        body, name=name, grid=(T // rb,), in_specs=[row, row, vec, row], out_specs=[row, vec],
        out_shape=[jax.ShapeDtypeStruct((T, Dm), F32), jax.ShapeDtypeStruct((1, Dm), F32)],
        compiler_params=_cp("arbitrary"))(dh, x, g, dres)


def _loss_head(x3, tgt, g, rb=512):
    T, Dm = x3.shape
    rb = min(rb, T)

    def body(x_ref, t_ref, g_ref, loss_ref, dx_ref, gg_ref):
        @pl.when(pl.program_id(0) == 0)
        def _():
            gg_ref[...] = jnp.zeros_like(gg_ref)
            loss_ref[...] = jnp.zeros_like(loss_ref)
        xv = x_ref[...]
        r = lax.rsqrt(jnp.mean(xv * xv, axis=-1, keepdims=True) + EPS)
        xh = xv * r
        e = xh * g_ref[...] - t_ref[...]
        loss_ref[...] += jnp.zeros_like(loss_ref) + (0.5 / Dm) * jnp.sum(e * e)
        dy = e * (1.0 / Dm)
        gg_ref[...] += jnp.sum(dy * xh, axis=0, keepdims=True)
        dxh = dy * g_ref[...]
        dx_ref[...] = r * (dxh - xh * jnp.mean(dxh * xh, axis=-1, keepdims=True))

    row = pl.BlockSpec((rb, Dm), lambda i: (i, 0))
    vec = pl.BlockSpec((1, Dm), lambda i: (0, 0))
    return pl.pallas_call(
        body, name="loss_head", grid=(T // rb,), in_specs=[row, row, vec],
        out_specs=[pl.BlockSpec((1, 128), lambda i: (0, 0)), row, vec],
        out_shape=[jax.ShapeDtypeStruct((1, 128), F32), jax.ShapeDtypeStruct((T, Dm), F32),
                   jax.ShapeDtypeStruct((1, Dm), F32)],
        compiler_params=_cp("arbitrary"))(x3, tgt, g)


def _halo_prev_spec(rb, width):
    return pl.BlockSpec((8, width), lambda i: (jnp.maximum(i * (rb // 8) - 1, 0), 0))


def _halo_next_spec(rb, width, T):
    return pl.BlockSpec((8, width), lambda i: (jnp.minimum((i + 1) * (rb // 8), T // 8 - 1), 0))


LANES = 128
FF_STRIPS = D_FF // LANES
ROW_CHUNK = 32


def _strip(j, base=0):
    return pl.ds(pl.multiple_of(base + j * LANES, LANES), LANES)


def _ffn_act(up, w, b, rb=256):
    T, W = up.shape
    rb = min(rb, T)

    def body(up_ref, halo_ref, w_ref, b_ref, act_ref, ext_scr):
        first = pl.program_id(0) == 0

        def strip(j, slot):
            halves = (_strip(j), _strip(j, D_FF))
            wv = [w_ref[:, cols] for cols in halves]
            bv = [b_ref[:, cols] for cols in halves]
            for h, cols in enumerate(halves):
                ext_scr[slot, h,0:8] = jnp.where(first, 0.0, halo_ref[:, cols])
                ext_scr[slot, h,8:] = up_ref[:, cols]
            for r0 in range(0, rb, ROW_CHUNK):
                n = min(ROW_CHUNK, rb - r0)
                c = [ext_scr[slot, h,6 + r0:6 + r0 + n] * wv[h][0:1] + ext_scr[slot, h,7 + r0:7 + r0 + n] * wv[h][1:2]
                     + ext_scr[slot, h,8 + r0:8 + r0 + n] * wv[h][2:3] + bv[h] for h in range(2)]
                act_ref[r0:r0 + n, halves[0]] = (_silu(c[0]) * c[1]).astype(BF16)

        def pair(jj, carry):
            strip(2 * jj, 0)
            strip(2 * jj + 1, 1)
            return carry

        lax.fori_loop(0, FF_STRIPS // 2, pair, 0)

    return pl.pallas_call(
        body, name="ffn_act", grid=(T // rb,),
        in_specs=[pl.BlockSpec((rb, W), lambda i: (i, 0)), _halo_prev_spec(rb, W),
                  pl.BlockSpec((3, W), lambda i: (0, 0)), pl.BlockSpec((1, W), lambda i: (0, 0))],
        out_specs=pl.BlockSpec((rb, D_FF), lambda i: (i, 0)),
        out_shape=jax.ShapeDtypeStruct((T, D_FF), BF16),
        scratch_shapes=[pltpu.VMEM((2, 2, rb + 8, LANES), F32)], compiler_params=_cp("parallel"))(up, up, w, b)


def _ffn_act_bwd(up, dact, w, b, rb=128, dep=None):
    T, W = up.shape
    rb = min(rb, T)
    nb = T // rb
    re = rb + 8

    def body(up_ref, prev_ref, next_ref, da_ref, danext_ref, w_ref, b_ref, *rest):
        dup_ref, gw_ref, gb_ref, ext_scr, dc_scr = rest[-5:]
        i = pl.program_id(0)

        @pl.when(i == 0)
        def _():
            gw_ref[...] = jnp.zeros_like(gw_ref)
            gb_ref[...] = jnp.zeros_like(gb_ref)
        last = i == nb - 1

        def fold8(a):
            return jnp.sum(a.reshape(a.shape[0] // 8, 8, LANES), axis=0)

        def strip(j, slot):
            halves = (_strip(j), _strip(j, D_FF))
            wv = [w_ref[:, cols] for cols in halves]
            bv = [b_ref[:, cols] for cols in halves]
            for h, cols in enumerate(halves):
                ext_scr[slot, h,0:8] = jnp.where(i > 0, prev_ref[:, cols], 0.0)
                ext_scr[slot, h,8:8 + rb] = up_ref[:, cols]
                ext_scr[slot, h,8 + rb:] = next_ref[:, cols]
            gb = [jnp.zeros((8, LANES), F32) for _ in range(2)]
            gw = [[jnp.zeros((8, LANES), F32) for _ in range(3)] for _ in range(2)]
            for r0 in range(0, re, ROW_CHUNK):
                n = min(ROW_CHUNK, re - r0)
                tp = [[ext_scr[slot, h,6 + k + r0:6 + k + r0 + n] for k in range(3)] for h in range(2)]
                c = [tp[h][0] * wv[h][0:1] + tp[h][1] * wv[h][1:2] + tp[h][2] * wv[h][2:3] + bv[h] for h in range(2)]
                if r0 < rb:
                    da = da_ref[r0:r0 + n, halves[0]]
                else:
                    da = jnp.where(last, 0.0, danext_ref[:, halves[0]])
                s = _sigmoid(c[0])
                gs = c[0] * s
                dcs = (da * c[1] * (s + gs * (1.0 - s)), da * gs)
                for h in range(2):
                    dc_scr[slot, h,r0:r0 + n] = dcs[h]
                    if r0 < rb:
                        gb[h] = gb[h] + fold8(dcs[h])
                        for k in range(3):
                            gw[h][k] = gw[h][k] + fold8(tp[h][k] * dcs[h])
            for r0 in range(0, rb, ROW_CHUNK):
                n = min(ROW_CHUNK, rb - r0)
                for h, cols in enumerate(halves):
                    dup = (dc_scr[slot, h,r0:r0 + n] * wv[h][2:3] + dc_scr[slot, h,r0 + 1:r0 + 1 + n] * wv[h][1:2]
                           + dc_scr[slot, h,r0 + 2:r0 + 2 + n] * wv[h][0:1])
                    dup_ref[r0:r0 + n, cols] = dup.astype(BF16)
            for h, cols in enumerate(halves):
                gb_ref[:, cols] += jnp.sum(gb[h], axis=0, keepdims=True)
                for k in range(3):
                    gw_ref[k:k + 1, cols] += jnp.sum(gw[h][k], axis=0, keepdims=True)

        def pair(jj, carry):
            strip(2 * jj, 0)
            strip(2 * jj + 1, 1)
            return carry

        lax.fori_loop(0, FF_STRIPS // 2, pair, 0)

    in_specs, args = _with_dep(
        [pl.BlockSpec((rb, W), lambda i: (i, 0)), _halo_prev_spec(rb, W), _halo_next_spec(rb, W, T),
         pl.BlockSpec((rb, D_FF), lambda i: (i, 0)), _halo_next_spec(rb, D_FF, T),
         pl.BlockSpec((3, W), lambda i: (0, 0)), pl.BlockSpec((1, W), lambda i: (0, 0))],
        [up, up, up, dact, dact, w, b], dep)
    return pl.pallas_call(
        body, name="ffn_act_bwd", grid=(nb,), in_specs=in_specs,
        out_specs=[pl.BlockSpec((rb, W), lambda i: (i, 0)), pl.BlockSpec((3, W), lambda i: (0, 0)),
                   pl.BlockSpec((1, W), lambda i: (0, 0))],
        out_shape=[jax.ShapeDtypeStruct((T, W), BF16), jax.ShapeDtypeStruct((3, W), F32),
                   jax.ShapeDtypeStruct((1, W), F32)],
        scratch_shapes=[pltpu.VMEM((2, 2, rb + 16, LANES), F32), pltpu.VMEM((2, 2, re, LANES), F32)],
        compiler_params=_cp("arbitrary"))(*args)


def _lane_iota(shape):
    return lax.broadcasted_iota(jnp.int32, shape, len(shape) - 1)


def _dn_act(p, conv_w, alog_row, dtb_row, rb=256):
    T = p.shape[0]
    rb = min(rb, T)
    W3 = 3 * DN_WIDTH

    def body(p_ref, halo_ref, ba_ref, w_ref, al_ref, dt_ref, q_ref, k_ref, v_ref, bg_ref, ext_scr):
        first = pl.program_id(0) == 0
        outs = (q_ref, k_ref, v_ref)
        for j in range(3 * N_HEADS):
            kind, h = divmod(j, N_HEADS)
            cols = slice(j * HEAD_DIM, (j + 1) * HEAD_DIM)
            cur = p_ref[:, cols]
            ext_scr[j, 0:8] = jnp.where(first, 0.0, halo_ref[:, cols])
            ext_scr[j, 8:] = cur
            wv = w_ref[:, cols]
            s = _silu(ext_scr[j, 5:5 + rb] * wv[0:1] + ext_scr[j, 6:6 + rb] * wv[1:2]
                      + ext_scr[j, 7:7 + rb] * wv[2:3] + cur * wv[3:4])
            if kind < 2:
                scale = HEAD_DIM ** -0.5 if kind == 0 else 1.0
                s = s * (lax.rsqrt(jnp.sum(s * s, axis=-1, keepdims=True) + EPS) * scale)
            outs[kind][:, h * HEAD_DIM:(h + 1) * HEAD_DIM] = s
        ba = ba_ref[...]
        lane = _lane_iota(ba.shape)
        beta = _sigmoid(ba)
        g = -jnp.exp(al_ref[...]) * _softplus(ba + dt_ref[...])
        bg_ref[...] = jnp.where(lane < N_HEADS, beta, jnp.where(lane < 2 * N_HEADS, g, 0.0))

    row512 = pl.BlockSpec((rb, DN_WIDTH), lambda i: (i, 0))
    row128 = pl.BlockSpec((rb, 128), lambda i: (i, 0))
    vec128 = pl.BlockSpec((1, 128), lambda i: (0, 0))
    return pl.pallas_call(
        body, name="dn_act", grid=(T // rb,),
        in_specs=[pl.BlockSpec((rb, W3), lambda i: (i, 0)), _halo_prev_spec(rb, W3),
                  pl.BlockSpec((rb, 128), lambda i: (i, BA_COL // 128)),
                  pl.BlockSpec((4, W3), lambda i: (0, 0)), vec128, vec128],
        out_specs=[row512, row512, row512, row128],
        out_shape=[jax.ShapeDtypeStruct((T, DN_WIDTH), F32)] * 3 + [jax.ShapeDtypeStruct((T, 128), F32)],
        scratch_shapes=[pltpu.VMEM((3 * N_HEADS, rb + 8, HEAD_DIM), F32)],
        compiler_params=_cp("parallel"))(p, p, p, conv_w, alog_row, dtb_row)


def _dn_act_bwd(p, conv_w, alog_row, dtb_row, dq, dk, dv, dbg, dp_mid, rb=256):
    T = p.shape[0]
    rb = min(rb, T)
    nb = T // rb
    re = rb + 8
    W3 = 3 * DN_WIDTH

    def body(p_ref, prev_ref, next_ref, ba_ref, w_ref, al_ref, dt_ref, dq_ref, dqn_ref, dk_ref, dkn_ref,
             dv_ref, dvn_ref, dbg_ref, mid_ref, draw_ref, gw_ref, gad_ref, ext_scr, dc_scr):
        i = pl.program_id(0)
        draw_ref[:, W3:2 * W3] = mid_ref[...]

        @pl.when(i == 0)
        def _():
            gw_ref[...] = jnp.zeros_like(gw_ref)
            gad_ref[...] = jnp.zeros_like(gad_ref)
        row = lax.broadcasted_iota(jnp.int32, (re, 1), 0)
        live = (row < rb) | (i < nb - 1)
        d_refs = ((dq_ref, dqn_ref), (dk_ref, dkn_ref), (dv_ref, dvn_ref))
        for j in range(3 * N_HEADS):
            kind, h = divmod(j, N_HEADS)
            cols = slice(j * HEAD_DIM, (j + 1) * HEAD_DIM)
            hcols = slice(h * HEAD_DIM, (h + 1) * HEAD_DIM)
            ext_scr[j, 0:8] = jnp.where(i > 0, prev_ref[:, cols], 0.0)
            ext_scr[j, 8:8 + rb] = p_ref[:, cols]
            ext_scr[j, 8 + rb:] = next_ref[:, cols]
            tp = [ext_scr[j, 5 + k:5 + k + re] for k in range(4)]
            wv = w_ref[:, cols]
            c = tp[0] * wv[0:1] + tp[1] * wv[1:2] + tp[2] * wv[2:3] + tp[3] * wv[3:4]
            sg = _sigmoid(c)
            s = c * sg
            d_in = jnp.where(live, jnp.concatenate([d_refs[kind][0][:, hcols], d_refs[kind][1][:, hcols]], axis=0), 0.0)
            if kind < 2:
                scale = HEAD_DIM ** -0.5 if kind == 0 else 1.0
                n = lax.rsqrt(jnp.sum(s * s, axis=-1, keepdims=True) + EPS)
                hat = s * n
                d_in = (n * scale) * (d_in - hat * jnp.sum(hat * d_in, axis=-1, keepdims=True))
            dc = d_in * (sg + s * (1.0 - sg))
            dc_scr[j] = dc
            dcc = dc[0:rb]
            draw = (dcc * wv[3:4] + dc_scr[j, 1:1 + rb] * wv[2:3] + dc_scr[j, 2:2 + rb] * wv[1:2]
                    + dc_scr[j, 3:3 + rb] * wv[0:1])
            draw_ref[:, cols] = draw.astype(BF16)
            for k in range(4):
                gw_ref[k:k + 1, cols] += jnp.sum(tp[k][0:rb] * dcc, axis=0, keepdims=True)
        ba = ba_ref[...]
        dbg = dbg_ref[...]
        lane = _lane_iota(ba.shape)
        beta = _sigmoid(ba)
        ea = jnp.exp(al_ref[...])
        z = ba + dt_ref[...]
        d_a = dbg * (-ea) * _sigmoid(z)
        dba = jnp.where(lane < N_HEADS, dbg * beta * (1.0 - beta), jnp.where(lane < 2 * N_HEADS, d_a, 0.0))
        draw_ref[:, BA_COL:] = dba.astype(BF16)
        isg = (lane >= N_HEADS) & (lane < 2 * N_HEADS)
        g = -ea * _softplus(z)
        gad_ref[0:1, :] += jnp.sum(jnp.where(isg, dbg * g, 0.0), axis=0, keepdims=True)
        gad_ref[1:2, :] += jnp.sum(jnp.where(isg, d_a, 0.0), axis=0, keepdims=True)

    row512 = pl.BlockSpec((rb, DN_WIDTH), lambda i: (i, 0))
    row128 = pl.BlockSpec((rb, 128), lambda i: (i, 0))
    vec128 = pl.BlockSpec((1, 128), lambda i: (0, 0))
    next512 = _halo_next_spec(rb, DN_WIDTH, T)
    return pl.pallas_call(
        body, name="dn_act_bwd", grid=(nb,),
        in_specs=[pl.BlockSpec((rb, W3), lambda i: (i, 0)), _halo_prev_spec(rb, W3), _halo_next_spec(rb, W3, T),
                  pl.BlockSpec((rb, 128), lambda i: (i, BA_COL // 128)),
                  pl.BlockSpec((4, W3), lambda i: (0, 0)), vec128, vec128,
                  row512, next512, row512, next512, row512, next512, row128,
                  pl.BlockSpec((rb, W3), lambda i: (i, 0))],
        out_specs=[pl.BlockSpec((rb, PROJ_PAD), lambda i: (i, 0)),
                   pl.BlockSpec((4, W3), lambda i: (0, 0)), pl.BlockSpec((2, 128), lambda i: (0, 0))],
        out_shape=[jax.ShapeDtypeStruct((T, PROJ_PAD), BF16),
                   jax.ShapeDtypeStruct((4, W3), F32), jax.ShapeDtypeStruct((2, 128), F32)],
        scratch_shapes=[pltpu.VMEM((3 * N_HEADS, rb + 16, HEAD_DIM), F32), pltpu.VMEM((3 * N_HEADS, re, HEAD_DIM), F32)],
        compiler_params=_cp("arbitrary"))(p, p, p, p, conv_w, alog_row, dtb_row, dq, dq, dk, dk, dv, dv, dbg, dp_mid)


def _tri(incl):
    ii = lax.broadcasted_iota(jnp.int32, (CHUNK, CHUNK), 0)
    jj = lax.broadcasted_iota(jnp.int32, (CHUNK, CHUNK), 1)
    return ii, jj, ((ii >= jj) if incl else (ii > jj))


def _dn_chunk(k, bg, cb=4):
    T = k.shape[0]
    N = T // CHUNK
    cb = min(cb, N)

    def body(k_ref, bg_ref, gc_ref, gct_ref, l_ref):
        ii, jj, incl = _tri(True)
        tri = incl.astype(F32)
        U = range(cb)
        bgv = [bg_ref[u * CHUNK:(u + 1) * CHUNK, :] for u in U]
        gc = [jnp.dot(tri, bgv[u], precision=lax.Precision.HIGHEST, preferred_element_type=F32) for u in U]
        gct = [gc[u].T for u in U]
        kk = [[None] * N_HEADS for _ in U]
        for u in U:
            gc_ref[u * CHUNK:(u + 1) * CHUNK, :] = gc[u]
            gct_ref[u] = gct[u][0:8]
            for h in range(N_HEADS):
                kh = k_ref[u * CHUNK:(u + 1) * CHUNK, h * HEAD_DIM:(h + 1) * HEAD_DIM]
                kk[u][h] = _nt(kh * bgv[u][:, h:h + 1], kh)
        for u in U:
            for h in range(N_HEADS):
                gcol = gc[u][:, N_HEADS + h:N_HEADS + h + 1]
                grow = gct[u][N_HEADS + h:N_HEADS + h + 1, :]
                l_ref[u, h] = kk[u][h] * jnp.exp(jnp.where(ii > jj, gcol - grow, NEG))

    rows = cb * CHUNK
    return pl.pallas_call(
        body, name="dn_chunk", grid=(N // cb,),
        in_specs=[pl.BlockSpec((rows, DN_WIDTH), lambda n: (n, 0)), pl.BlockSpec((rows, 128), lambda n: (n, 0))],
        out_specs=[pl.BlockSpec((rows, 128), lambda n: (n, 0)), pl.BlockSpec((cb, 8, CHUNK), lambda n: (n, 0, 0)),
                   pl.BlockSpec((cb, N_HEADS, CHUNK, CHUNK), lambda n: (n, 0, 0, 0))],
        out_shape=[jax.ShapeDtypeStruct((T, 128), F32), jax.ShapeDtypeStruct((N, 8, CHUNK), F32),
                   jax.ShapeDtypeStruct((N, N_HEADS, CHUNK, CHUNK), F32)],
        compiler_params=_cp("parallel"))(k, bg)


def _tri_inv(lt):
    S = lt.shape[1]

    def body(l_ref, a_ref):
        col = lax.broadcasted_iota(jnp.int32, (CHUNK, S), 0)
        for i in range(CHUNK):
            def step(j, acc):
                return acc - l_ref[pl.ds(i * CHUNK + j, 1), :] * a_ref[j]
            a_ref[i] = lax.fori_loop(0, i, step, (col == i).astype(F32))

    return pl.pallas_call(
        body, name="tri_inv", out_shape=jax.ShapeDtypeStruct((CHUNK, CHUNK, S), F32),
        compiler_params=pltpu.CompilerParams(vmem_limit_bytes=VMEM_LIMIT))(lt)


def _dn_head_terms(qh, kh, vh, beta, gcol, grow):
    ii, jj, incl = _tri(True)
    gam = jnp.exp(jnp.where(incl, gcol - grow, NEG))
    glast = grow[:, CHUNK - 1:CHUNK]
    E = jnp.exp(gcol)
    Fd = jnp.exp(glast - gcol)
    cd = jnp.exp(glast)
    kb = kh * beta
    return dict(ii=ii, jj=jj, gam=gam, E=E, F=Fd, cd=cd, kb=kb, vb=vh * beta, W=kb * E, qE=qh * E, kt=kh * Fd)


def _apply_a(a, u):
    hi, lo = _split(a)
    ub = _bf(u)
    return jnp.dot(hi, ub, preferred_element_type=F32) + jnp.dot(lo, ub, preferred_element_type=F32)


def _dn_scan(q, k, v, bg, gc, gct, a):
    T = q.shape[0]
    N = T // CHUNK

    cb = min(SCAN_CHUNKS, N)

    def body(q_ref, k_ref, v_ref, bg_ref, gc_ref, gct_ref, a_ref, o_ref, sall_ref, s_ref):
        @pl.when(pl.program_id(0) == 0)
        def _():
            s_ref[...] = jnp.zeros_like(s_ref)
        H = range(N_HEADS)
        sl = [slice(h * HEAD_DIM, (h + 1) * HEAD_DIM) for h in H]
        pre = []
        for u in range(cb):
            r = slice(u * CHUNK, (u + 1) * CHUNK)
            bgv, gcv, gctv = bg_ref[r, :], gc_ref[r, :], gct_ref[u]
            q_, k_ = [q_ref[r, s] for s in sl], [k_ref[r, s] for s in sl]
            t = [_dn_head_terms(q_[h], k_[h], v_ref[r, sl[h]], bgv[:, h:h + 1],
                                gcv[:, N_HEADS + h:N_HEADS + h + 1], gctv[N_HEADS + h:N_HEADS + h + 1, :]) for h in H]
            P = [_nt(q_[h], k_[h]) * t[h]["gam"] for h in H]
            pre.append((r, t, P))
        S = [s_ref[h] for h in H]
        for u in range(cb):
            r, t, P = pre[u]
            for h in H:
                sall_ref[u, h] = S[h]
            WS = [_nn(t[h]["W"], S[h]) for h in H]
            qS = [_nn(t[h]["qE"], S[h]) for h in H]
            vn = [_apply_a(a_ref[u, h], t[h]["vb"] - WS[h]) for h in H]
            Pv = [_nn(P[h], vn[h]) for h in H]
            kv = [_tn(t[h]["kt"], vn[h]) for h in H]
            for h in H:
                o_ref[r, sl[h]] = qS[h] + Pv[h]
            S = [t[h]["cd"] * S[h] + kv[h] for h in H]
        for h in H:
            s_ref[h] = S[h]

    row512 = pl.BlockSpec((cb * CHUNK, DN_WIDTH), lambda n: (n, 0))
    row128 = pl.BlockSpec((cb * CHUNK, 128), lambda n: (n, 0))
    return pl.pallas_call(
        body, name="dn_scan", grid=(N // cb,),
        in_specs=[row512, row512, row512, row128, row128, pl.BlockSpec((cb, 8, CHUNK), lambda n: (n, 0, 0)),
                  pl.BlockSpec((cb, N_HEADS, CHUNK, CHUNK), lambda n: (n, 0, 0, 0))],
        out_specs=[row512, pl.BlockSpec((cb, N_HEADS, HEAD_DIM, HEAD_DIM), lambda n: (n, 0, 0, 0))],
        out_shape=[jax.ShapeDtypeStruct((T, DN_WIDTH), F32),
                   jax.ShapeDtypeStruct((N, N_HEADS, HEAD_DIM, HEAD_DIM), F32)],
        scratch_shapes=[pltpu.VMEM((N_HEADS, HEAD_DIM, HEAD_DIM), F32)],
        compiler_params=_cp("arbitrary"))(q, k, v, bg, gc, gct, a)


def _dn_scan_bwd(q, k, v, bg, gc, gct, a, a_t, sall, do, dep=None):
    T = q.shape[0]
    N = T // CHUNK

    cb = min(SCAN_CHUNKS, N)
    nb = N // cb

    def body(q_ref, k_ref, v_ref, bg_ref, gc_ref, gct_ref, a_ref, at_ref, sall_ref, do_ref, *rest):
        dq_ref, dk_ref, dv_ref, dbg_ref, ds_ref = rest[-5:]
        @pl.when(pl.program_id(0) == 0)
        def _():
            ds_ref[...] = jnp.zeros_like(ds_ref)
        lane = _lane_iota((CHUNK, 128))
        rowi = lax.broadcasted_iota(jnp.int32, (CHUNK, 1), 0)
        ii, jj, _ = _tri(True)
        rev = (jj >= ii).astype(F32)
        H = range(N_HEADS)
        sl = [slice(h * HEAD_DIM, (h + 1) * HEAD_DIM) for h in H]
        pre = {}
        for u in reversed(range(cb)):
            r = slice(u * CHUNK, (u + 1) * CHUNK)
            bgv, gcv, gctv = bg_ref[r, :], gc_ref[r, :], gct_ref[u]
            q_, k_, v_ = [q_ref[r, s] for s in sl], [k_ref[r, s] for s in sl], [v_ref[r, s] for s in sl]
            dO = [do_ref[r, s] for s in sl]
            beta = [bgv[:, h:h + 1] for h in H]
            t = [_dn_head_terms(q_[h], k_[h], v_[h], beta[h], gcv[:, N_HEADS + h:N_HEADS + h + 1],
                                gctv[N_HEADS + h:N_HEADS + h + 1, :]) for h in H]
            S = [sall_ref[u, h] for h in H]
            A = [a_ref[u, h] for h in H]
            WS = [_nn(t[h]["W"], S[h]) for h in H]
            KK = [_nt(t[h]["kb"], k_[h]) for h in H]
            QK = [_nt(q_[h], k_[h]) for h in H]
            d_qE = [_nt(dO[h], S[h]) for h in H]
            vn = [_apply_a(A[h], t[h]["vb"] - WS[h]) for h in H]
            PtdO = [_tn(QK[h] * t[h]["gam"], dO[h]) for h in H]
            qEdO = [_tn(t[h]["qE"], dO[h]) for h in H]
            dOvn = [_nt(dO[h], vn[h]) for h in H]
            dQK = [jnp.where(ii >= jj, dOvn[h], 0.0) * t[h]["gam"] for h in H]
            dQKk = [_nn(dQK[h], k_[h]) for h in H]
            dQKq = [_tn(dQK[h], q_[h]) for h in H]
            pre[u] = (r, q_, k_, v_, beta, t, S, A, KK, QK, d_qE, vn, PtdO, qEdO, dQK, dQKk, dQKq)
        dSn = [ds_ref[h] for h in H]
        for u in reversed(range(cb)):
            r, q_, k_, v_, beta, t, S, A, KK, QK, d_qE, vn, PtdO, qEdO, dQK, dQKk, dQKq = pre[u]
            gam, E, Fd, cd, kb = ([t[h][n] for h in H] for n in ("gam", "E", "F", "cd", "kb"))
            ktdS = [_nn(t[h]["kt"], dSn[h]) for h in H]
            dU = [_apply_a(at_ref[u, h], PtdO[h] + ktdS[h]) for h in H]
            d_kt = [_nt(vn[h], dSn[h]) for h in H]
            dUvn = [_nt(dU[h], vn[h]) for h in H]
            dUS = [_nt(dU[h], S[h]) for h in H]
            WdU = [_tn(t[h]["W"], dU[h]) for h in H]
            d_cd = [jnp.sum(S[h] * dSn[h]) for h in H]
            dSn = [cd[h] * dSn[h] + qEdO[h] - WdU[h] for h in H]
            dKK = [jnp.where(ii > jj, -dUvn[h], 0.0) * gam[h] for h in H]
            dKKk = [_nn(dKK[h], k_[h]) for h in H]
            dKKkb = [_tn(dKK[h], kb[h]) for h in H]
            dbeta_arr = jnp.zeros((CHUNK, 128), F32)
            dgc_arr = jnp.zeros((CHUNK, 128), F32)
            for h in H:
                dW = -dUS[h]
                dq_ref[r, sl[h]] = dQKk[h] + d_qE[h] * E[h]
                d_kb = dKKk[h] + dW * E[h]
                dk_ref[r, sl[h]] = dQKq[h] + dKKkb[h] + d_kb * beta[h] + d_kt[h] * Fd[h]
                dv_ref[r, sl[h]] = dU[h] * beta[h]
                Z = dQK[h] * QK[h] + dKK[h] * KK[h]
                dbeta = jnp.sum(dU[h] * v_[h] + d_kb * k_[h], axis=-1, keepdims=True)
                dE = jnp.sum(dW * kb[h] + d_qE[h] * q_[h], axis=-1, keepdims=True)
                dFF = jnp.sum(d_kt[h] * k_[h], axis=-1, keepdims=True) * Fd[h]
                dgc = (dE * E[h] - dFF + jnp.sum(Z, axis=-1, keepdims=True) - jnp.sum(Z.T, axis=-1, keepdims=True)
                       + jnp.where(rowi == CHUNK - 1, jnp.sum(dFF) + d_cd[h] * cd[h], 0.0))
                dbeta_arr = dbeta_arr + jnp.where(lane == h, dbeta, 0.0)
                dgc_arr = dgc_arr + jnp.where(lane == N_HEADS + h, dgc, 0.0)
            dbg_ref[r, :] = dbeta_arr + jnp.dot(rev, dgc_arr, precision=lax.Precision.HIGHEST,
                                                preferred_element_type=F32)
        for h in H:
            ds_ref[h] = dSn[h]

    row512 = pl.BlockSpec((cb * CHUNK, DN_WIDTH), lambda n: (nb - 1 - n, 0))
    row128 = pl.BlockSpec((cb * CHUNK, 128), lambda n: (nb - 1 - n, 0))
    in_specs, args = _with_dep(
        [row512, row512, row512, row128, row128,
         pl.BlockSpec((cb, 8, CHUNK), lambda n: (nb - 1 - n, 0, 0)),
         pl.BlockSpec((cb, N_HEADS, CHUNK, CHUNK), lambda n: (nb - 1 - n, 0, 0, 0)),
         pl.BlockSpec((cb, N_HEADS, CHUNK, CHUNK), lambda n: (nb - 1 - n, 0, 0, 0)),
         pl.BlockSpec((cb, N_HEADS, HEAD_DIM, HEAD_DIM), lambda n: (nb - 1 - n, 0, 0, 0)), row512],
        [q, k, v, bg, gc, gct, a, a_t, sall, do], dep)
    return pl.pallas_call(
        body, name="dn_scan_bwd", grid=(nb,), in_specs=in_specs,
        out_specs=[row512, row512, row512, row128],
        out_shape=[jax.ShapeDtypeStruct((T, DN_WIDTH), F32)] * 3 + [jax.ShapeDtypeStruct((T, 128), F32)],
        scratch_shapes=[pltpu.VMEM((N_HEADS, HEAD_DIM, HEAD_DIM), F32)],
        compiler_params=_cp("arbitrary"))(*args)


def _sg_mask():
    ii = lax.broadcasted_iota(jnp.int32, (SG_BLOCK, SG_BLOCK), 0) // CHUNK
    jj = lax.broadcasted_iota(jnp.int32, (SG_BLOCK, SG_BLOCK), 1) // CHUNK
    return jj <= ii


def _mix_fwd(o, p, ong, sgn, sgw, sgbt):
    T = o.shape[0]
    rb = SG_BLOCK

    def body(o_ref, gate_ref, u_ref, vg_ref, ong_ref, sgn_ref, sgw_ref, sgbt_ref, mix_ref):
        mask = _sg_mask()
        gate = gate_ref[...]
        for h in range(N_HEADS):
            sl = slice(h * HEAD_DIM, (h + 1) * HEAD_DIM)
            oh = o_ref[:, sl]
            r = lax.rsqrt(jnp.mean(oh * oh, axis=-1, keepdims=True) + EPS)
            mix_ref[:, sl] = (oh * r * ong_ref[...] * _silu(gate[:, sl])).astype(BF16)
        for gi in range(SG_GROUPS):
            sl = slice(gi * SG_BLOCK, (gi + 1) * SG_BLOCK)
            gv = _gelu(vg_ref[:, sl])
            r = lax.rsqrt(jnp.mean(gv * gv, axis=-1, keepdims=True) + EPS)
            vh = gv * r * sgn_ref[:, sl]
            s = _nn(jnp.where(mask, sgw_ref[gi], 0.0), vh) + sgbt_ref[:, gi:gi + 1]
            mix_ref[:, DN_WIDTH + gi * SG_BLOCK:DN_WIDTH + (gi + 1) * SG_BLOCK] = (_gelu(u_ref[:, sl]) * s).astype(BF16)

    def col(c):
        return pl.BlockSpec((rb, 512), lambda i: (i, c))
    return pl.pallas_call(
        body, name="mix_fwd", grid=(T // rb,),
        in_specs=[pl.BlockSpec((rb, DN_WIDTH), lambda i: (i, 0)), col(3), col(4), col(5),
                  pl.BlockSpec((1, 128), lambda i: (0, 0)), pl.BlockSpec((1, SG_WIDTH), lambda i: (0, 0)),
                  pl.BlockSpec((SG_GROUPS, SG_BLOCK, SG_BLOCK), lambda i: (0, 0, 0)),
                  pl.BlockSpec((SG_BLOCK, 128), lambda i: (0, 0))],
        out_specs=pl.BlockSpec((rb, D_MODEL), lambda i: (i, 0)),
        out_shape=jax.ShapeDtypeStruct((T, D_MODEL), BF16),
        compiler_params=_cp("parallel"))(o, p, p, p, ong, sgn, sgw, sgbt)


def _mix_bwd(o, p, ong, sgn, sgw, sgbt, dmix, dep=None):
    T = o.shape[0]
    rb = SG_BLOCK

    def body(o_ref, gate_ref, u_ref, vg_ref, ong_ref, sgn_ref, sgw_ref, sgbt_ref, dmix_ref, *rest):
        do_ref, dp_ref, gong_ref, gsgn_ref, gsgw_ref, gsgbt_ref = rest[-6:]
        @pl.when(pl.program_id(0) == 0)
        def _():
            gong_ref[...] = jnp.zeros_like(gong_ref)
            gsgn_ref[...] = jnp.zeros_like(gsgn_ref)
            gsgw_ref[...] = jnp.zeros_like(gsgw_ref)
            gsgbt_ref[...] = jnp.zeros_like(gsgbt_ref)
        mask = _sg_mask()
        gate = gate_ref[...]
        lane = _lane_iota((SG_BLOCK, 128))
        for h in range(N_HEADS):
            sl = slice(h * HEAD_DIM, (h + 1) * HEAD_DIM)
            oh = o_ref[:, sl]
            dm = dmix_ref[:, sl]
            r = lax.rsqrt(jnp.mean(oh * oh, axis=-1, keepdims=True) + EPS)
            oh_hat = oh * r
            gt = gate[:, sl]
            sg = _silu(gt)
            dp_ref[:, sl] = (dm * oh_hat * ong_ref[...] * _dsilu(gt)).astype(BF16)
            dn_ = dm * sg
            gong_ref[...] += jnp.sum(dn_ * oh_hat, axis=0, keepdims=True)
            dhat = dn_ * ong_ref[...]
            do_ref[:, sl] = r * (dhat - oh_hat * jnp.mean(dhat * oh_hat, axis=-1, keepdims=True))
        for gi in range(SG_GROUPS):
            sl = slice(gi * SG_BLOCK, (gi + 1) * SG_BLOCK)
            vraw = vg_ref[:, sl]
            gv = _gelu(vraw)
            r = lax.rsqrt(jnp.mean(gv * gv, axis=-1, keepdims=True) + EPS)
            vhat = gv * r
            vn = vhat * sgn_ref[:, sl]
            wm = jnp.where(mask, sgw_ref[gi], 0.0)
            s = _nn(wm, vn) + sgbt_ref[:, gi:gi + 1]
            uraw = u_ref[:, sl]
            dm = dmix_ref[:, DN_WIDTH + gi * SG_BLOCK:DN_WIDTH + (gi + 1) * SG_BLOCK]
            dp_ref[:, DN_WIDTH + gi * SG_BLOCK:DN_WIDTH + (gi + 1) * SG_BLOCK] = (dm * s * _dgelu(uraw)).astype(BF16)
            ds = dm * _gelu(uraw)
            gsgbt_ref[...] += jnp.where(lane == gi, jnp.sum(ds, axis=-1, keepdims=True), 0.0)
            gsgw_ref[gi] += jnp.where(mask, _nt(ds, vn), 0.0)
            dvn = _tn(wm, ds)
            gsgn_ref[:, sl] += jnp.sum(dvn * vhat, axis=0, keepdims=True)
            dhat = dvn * sgn_ref[:, sl]
            dgv = r * (dhat - vhat * jnp.mean(dhat * vhat, axis=-1, keepdims=True))
            dp_ref[:, 2 * DN_WIDTH + gi * SG_BLOCK:2 * DN_WIDTH + (gi + 1) * SG_BLOCK] = (dgv * _dgelu(vraw)).astype(BF16)

    def col(c):
        return pl.BlockSpec((rb, 512), lambda i: (i, c))
    full = lambda *s: pl.BlockSpec(s, lambda i: (0,) * len(s))
    in_specs, args = _with_dep(
        [pl.BlockSpec((rb, DN_WIDTH), lambda i: (i, 0)), col(3), col(4), col(5),
         full(1, 128), full(1, SG_WIDTH), full(SG_GROUPS, SG_BLOCK, SG_BLOCK), full(SG_BLOCK, 128),
         pl.BlockSpec((rb, D_MODEL), lambda i: (i, 0))],
        [o, p, p, p, ong, sgn, sgw, sgbt, dmix], dep)
    return pl.pallas_call(
        body, name="mix_bwd", grid=(T // rb,), in_specs=in_specs,
        out_specs=[pl.BlockSpec((rb, DN_WIDTH), lambda i: (i, 0)), pl.BlockSpec((rb, 3 * 512), lambda i: (i, 0)),
                   full(1, 128), full(1, SG_WIDTH), full(SG_GROUPS, SG_BLOCK, SG_BLOCK), full(SG_BLOCK, 128)],
        out_shape=[jax.ShapeDtypeStruct((T, DN_WIDTH), F32), jax.ShapeDtypeStruct((T, 3 * 512), BF16),
                   jax.ShapeDtypeStruct((1, 128), F32), jax.ShapeDtypeStruct((1, SG_WIDTH), F32),
                   jax.ShapeDtypeStruct((SG_GROUPS, SG_BLOCK, SG_BLOCK), F32),
                   jax.ShapeDtypeStruct((SG_BLOCK, 128), F32)],
        compiler_params=_cp("arbitrary"))(*args)


def _pad_lanes(row, offset=0):
    n = row.shape[1]
    return jnp.pad(row, ((0, 0), (offset, 128 - n - offset)))


def _local_step(x, tgt, w, dep=None, late_weights=None, on_grad=None):
    T = x.shape[0]
    N = T // CHUNK
    on_grad = on_grad or (lambda name, g: None)
    alog_row = _pad_lanes(w["dn_a_log"], N_HEADS)
    dtb_row = _pad_lanes(w["dn_dt_bias"], N_HEADS)
    sgbt = jnp.pad(w["sg_b"].T, ((0, 0), (0, 128 - SG_GROUPS)))

    h1 = _rms_fwd("rms_attn", x, w["attn_norm_g"], dep=dep)
    p = _mm_nn("in_proj", h1, w["w_in"], F32, 512, PROJ_PAD)
    q, k, v, bg = _dn_act(p, w["dn_conv_w"], alog_row, dtb_row)
    gc, gct, lmat = _dn_chunk(k, bg)
    lt = lmat.reshape(N * N_HEADS, CHUNK * CHUNK).T
    at = _tri_inv(lt)
    a = at.reshape(CHUNK * CHUNK, N * N_HEADS).T.reshape(N, N_HEADS, CHUNK, CHUNK)
    a_t = at.transpose(1, 0, 2).reshape(CHUNK * CHUNK, N * N_HEADS).T.reshape(N, N_HEADS, CHUNK, CHUNK)
    o, sall = _dn_scan(q, k, v, bg, gc, gct, a)
    mix = _mix_fwd(o, p, w["dn_out_norm_g"], w["sg_norm_g"], w["sg_w"], sgbt)
    if late_weights is not None:
        w = {**w, **late_weights(mix)}
    x2 = _mm_nn("out_proj", mix, w["w_out"], F32, 512, 1024, res=x)
    h2 = _rms_fwd("rms_ffn", x2, w["ffn_norm_g"])
    up = _mm_nn("up_proj", h2, w["w_up"], F32, 512, D_FF)
    act = _ffn_act(up, w["ffn_conv_w"], w["ffn_conv_b"])
    x3 = _mm_nn("down_proj", act, w["w_down"], F32, 512, 1024, res=x2)
    loss, dx3, g_final = _loss_head(x3, tgt, w["final_norm_g"])

    dact = _mm_nt("d_act", dx3, w["w_down"], F32, 512, D_FF)
    g_w_down = _mm_tn("g_w_down", act, dx3, D_FF, 1024, 512)
    tok = on_grad("w_down", g_w_down)
    dup, g_ffn_conv_w, g_ffn_conv_b = _ffn_act_bwd(up, dact, w["ffn_conv_w"], w["ffn_conv_b"], dep=tok)
    g_w_up = _mm_tn("g_w_up", h2, dup, 1024, 2 * D_FF // 4, 512, col_major_tiles=True)
    tok = on_grad("w_up", g_w_up)
    dh2 = _mm_nt("d_h2", dup, w["w_up"], F32, 512, 1024, dep=tok)
    dx2, g_ffn_norm = _rms_bwd("rms_ffn_bwd", dh2, x2, w["ffn_norm_g"], dx3)
    dmix = _mm_nt("d_mix", dx2, w["w_out"], F32, 512, 1024)
    g_w_out = _mm_tn("g_w_out", mix, dx2, 1024, 1024, 1024)
    tok = on_grad("w_out", g_w_out)
    do, dp_mid, g_ong, g_sgn, g_sgw, g_sgbt = _mix_bwd(o, p, w["dn_out_norm_g"], w["sg_norm_g"], w["sg_w"], sgbt,
                                                      dmix, dep=tok)
    early = dict(dn_out_norm_g=g_ong, sg_norm_g=g_sgn, sg_w=g_sgw, sg_b=g_sgbt[:, :SG_GROUPS].T,
                 ffn_norm_g=g_ffn_norm, ffn_conv_w=g_ffn_conv_w, ffn_conv_b=g_ffn_conv_b, final_norm_g=g_final)
    tok = on_grad("small_early", early)
    dq, dk, dv, dbg = _dn_scan_bwd(q, k, v, bg, gc, gct, a, a_t, sall, do, dep=tok)
    dp, g_dn_conv_w, g_ad = _dn_act_bwd(p, w["dn_conv_w"], alog_row, dtb_row, dq, dk, dv, dbg, dp_mid)
    g_w_in = _mm_tn("g_w_in", h1, dp, 1024, PROJ_PAD, 512)[:, :PROJ_COLS]
    tok = on_grad("w_in", g_w_in)
    dh1 = _mm_nt("d_h1", dp, w["w_in"], F32, 512, 1024, dep=tok)
    grad_x, g_attn_norm = _rms_bwd("rms_attn_bwd", dh1, x, w["attn_norm_g"], dx2)

    grads = dict(
        attn_norm_g=g_attn_norm, w_in=g_w_in, dn_conv_w=g_dn_conv_w,
        dn_a_log=g_ad[0:1, N_HEADS:2 * N_HEADS], dn_dt_bias=g_ad[1:2, N_HEADS:2 * N_HEADS],
        w_out=g_w_out, w_up=g_w_up, w_down=g_w_down, **early)
    return loss, grad_x, grads


def _me():
    return lax.axis_index("x"), lax.axis_index("y"), lax.axis_index("c")


def _peer(rel):
    x, y, c = _me()
    return {"x": (1 - x, y, c), "y": (x, 1 - y, c), "xy": (1 - x, 1 - y, c), "c": (x, y, 1 - c)}[rel]


def _chip_of(dev):
    return 2 * dev[0] + dev[1]


CHIP_RELS = ("x", "y", "xy")


def _run_copies(copies, sends, recvs):
    for cp in copies:
        cp.start()
    for cp in recvs:
        cp.wait_recv()
    for cp in sends:
        cp.wait_send()


def _gather_first(w_shard, small_shard):
    R = w_shard.shape[0]
    r2 = R // 2

    def body(w_ref, s_ref, w_out, s_out, send_sems, recv_sems):
        x, y, c = _me()
        me = _chip_of((x, y))
        sib = _peer("c")

        def half(chip, core):
            return w_out.at[chip, pl.ds(pl.multiple_of(core * r2, 8), r2), :]

        def copy(k, src, dst, to):
            return pltpu.make_async_remote_copy(src_ref=src, dst_ref=dst, send_sem=send_sems.at[k],
                                                recv_sem=recv_sems.at[k], device_id=to, device_id_type=MESH)

        own_rows = w_ref.at[pl.ds(pl.multiple_of(c * r2, 8), r2), :]
        first = [copy(r, own_rows, half(me, c), _peer(rel)) for r, rel in enumerate(CHIP_RELS)]
        first += [copy(3 + r, s_ref, s_out.at[me], _peer(rel)) for r, rel in enumerate(CHIP_RELS)]
        for cp in first:
            cp.start()
        passed = []
        for r, rel in enumerate(CHIP_RELS):
            their = _chip_of(_peer(rel))
            copy(r, own_rows, half(their, c), _peer(rel)).wait_recv()
            fwd = copy(6 + r, half(their, c), half(their, c), sib)
            fwd.start()
            passed.append(fwd)
        for r, rel in enumerate(CHIP_RELS):
            their = _chip_of(_peer(rel))
            copy(3 + r, s_ref, s_out.at[their], _peer(rel)).wait_recv()
            copy(6 + r, own_rows, half(their, 1 - c), sib).wait_recv()
        for cp in first + passed:
            cp.wait_send()

    w_all, s_all = pl.pallas_call(
        body, name="gather_first", in_specs=[ANY, ANY], out_specs=[ANY, ANY],
        out_shape=[jax.ShapeDtypeStruct((4,) + w_shard.shape, w_shard.dtype),
                   jax.ShapeDtypeStruct((4,) + small_shard.shape, small_shard.dtype)],
        scratch_shapes=[pltpu.SemaphoreType.DMA((9,)), pltpu.SemaphoreType.DMA((9,))])(w_shard, small_shard)
    me = _chip_of(_me())
    return (lax.dynamic_update_index_in_dim(w_all, w_shard, me, 0),
            lax.dynamic_update_index_in_dim(s_all, small_shard, me, 0))


OTHERS = tuple((fx, fy, fc) for fx in (0, 1) for fy in (0, 1) for fc in (0, 1) if (fx, fy, fc) != (0, 0, 0))


def _other(flip):
    x, y, c = _me()
    return (x ^ flip[0], y ^ flip[1], c ^ flip[2])


def _linear(dev):
    return 4 * dev[0] + 2 * dev[1] + dev[2]


def _exchange_small(small):
    def body(small_ref, out_ref, send_sems, recv_sems):
        my_slot = _linear(_me())
        sends, recvs = [], []
        for k, flip in enumerate(OTHERS):
            peer = _other(flip)
            sends.append(pltpu.make_async_remote_copy(
                src_ref=small_ref, dst_ref=out_ref.at[my_slot], send_sem=send_sems.at[k], recv_sem=recv_sems.at[k],
                device_id=peer, device_id_type=MESH))
            recvs.append(pltpu.make_async_remote_copy(
                src_ref=small_ref, dst_ref=out_ref.at[_linear(peer)], send_sem=send_sems.at[k],
                recv_sem=recv_sems.at[k], device_id=peer, device_id_type=MESH))
        _run_copies(sends, sends, recvs)

    out = pl.pallas_call(
        body, name="exchange_small", in_specs=[ANY], out_specs=ANY,
        out_shape=jax.ShapeDtypeStruct((8,) + small.shape, small.dtype),
        scratch_shapes=[pltpu.SemaphoreType.DMA((7,)), pltpu.SemaphoreType.DMA((7,))])(small)
    return lax.dynamic_update_index_in_dim(out, small, _linear(_me()), 0)


def _pair_swap(halves):
    n = len(halves)

    def body(*refs):
        src, out = refs[:n], refs[n:2 * n]
        send_sems, recv_sems = refs[2 * n:]
        sib = _peer("c")
        copies = [pltpu.make_async_remote_copy(
            src_ref=src[i], dst_ref=out[i], send_sem=send_sems.at[i], recv_sem=recv_sems.at[i],
            device_id=sib, device_id_type=MESH) for i in range(n)]
        _run_copies(copies, copies, copies)

    return pl.pallas_call(
        body, name="pair_swap", in_specs=[ANY] * n, out_specs=[ANY] * n,
        out_shape=[jax.ShapeDtypeStruct(h.shape, h.dtype) for h in halves],
        scratch_shapes=[pltpu.SemaphoreType.DMA((n,)), pltpu.SemaphoreType.DMA((n,))])(*halves)


HBM = pl.BlockSpec(memory_space=pltpu.HBM)
SEM = pl.BlockSpec(memory_space=pltpu.SEMAPHORE)
EFFECT = pltpu.SideEffectType.DATAFLOW_SIDE_EFFECTING


def _hbm(a):
    return pltpu.with_memory_space_constraint(a, pltpu.HBM)


def _transfer_start(name, srcs, lands, n_copies, make_copies, after=None):
    n, m = len(srcs), len(lands)

    def body(*refs):
        src, land = refs[:n], refs[n:n + m]
        outs = refs[n + m + (after is not None):]
        send_sems, recv_sems, token = outs[0], outs[1], outs[-1]
        for cp in make_copies(src, land, send_sems, recv_sems):
            cp.start()
        token[...] = jnp.zeros_like(token)

    arrs = list(srcs) + list(lands)
    in_specs, args = _with_dep([HBM] * (n + m), [_hbm(a) for a in arrs], after)
    out = pl.pallas_call(
        body, name=name,
        out_shape=(pltpu.SemaphoreType.DMA((n_copies,)), pltpu.SemaphoreType.DMA((n_copies,)),
                   *[pltpu.HBM(a.shape, a.dtype) for a in arrs], jax.ShapeDtypeStruct((8, 128), F32)),
        in_specs=in_specs,
        out_specs=(SEM, SEM, *[HBM] * (n + m), pl.BlockSpec(memory_space=pltpu.VMEM)),
        input_output_aliases={i: 2 + i for i in range(n + m)},
        compiler_params=pltpu.CompilerParams(has_side_effects=EFFECT))(*args)
    return out[0], out[1], list(out[2:2 + n]), list(out[2 + n:2 + n + m]), out[-1]


def _transfer_wait(name, send_sems, recv_sems, srcs, lands, make_copies, after):
    n, m = len(srcs), len(lands)

    def body(*refs):
        src, land = refs[:n], refs[n:n + m]
        s_sems, r_sems = refs[n + m], refs[n + m + 1]
        for cp in make_copies(src, land, s_sems, r_sems):
            cp.wait_send()
            cp.wait_recv()

    arrs = list(srcs) + list(lands)
    out = pl.pallas_call(
        body, name=name, out_shape=tuple(pltpu.HBM(a.shape, a.dtype) for a in arrs),
        in_specs=[HBM] * (n + m) + [SEM, SEM, ANY], out_specs=tuple([HBM] * (n + m)),
        input_output_aliases={i: i for i in range(n + m)},
        compiler_params=pltpu.CompilerParams(has_side_effects=EFFECT))(*arrs, send_sems, recv_sems, after)
    return list(out[:n]), list(out[n:])


def _gather_copies(src, land, send_sems, recv_sems):
    me = _chip_of(_me())
    copies = []
    for i in range(len(src)):
        for r, rel in enumerate(CHIP_RELS):
            k = 3 * i + r
            copies.append(pltpu.make_async_remote_copy(
                src_ref=src[i], dst_ref=land[i].at[me], send_sem=send_sems.at[k], recv_sem=recv_sems.at[k],
                device_id=_peer(rel), device_id_type=MESH))
    return copies


def _small_copies(src, land, send_sems, recv_sems):
    my_slot = _linear(_me())
    return [pltpu.make_async_remote_copy(
        src_ref=src[0], dst_ref=land[0].at[my_slot], send_sem=send_sems.at[k], recv_sem=recv_sems.at[k],
        device_id=_other(flip), device_id_type=MESH) for k, flip in enumerate(OTHERS)]


def _pieces_copies(src, land, send_sems, recv_sems):
    copies = []
    for k, flip in enumerate(OTHERS):
        peer = _other(flip)
        copies.append(pltpu.make_async_remote_copy(
            src_ref=src[0].at[_linear(peer)], dst_ref=land[0].at[k], send_sem=send_sems.at[k],
            recv_sem=recv_sems.at[k], device_id=peer, device_id_type=MESH))
    return copies


def _row_block(rows, cols, budget=2 * 1024 * 1024):
    rb = max(8, (budget // (4 * cols)) // 8 * 8)
    while rows % rb:
        rb -= 8
    return rb if rb > 0 else rows


def _sum_slots(name, first, rest):
    R, Cc = first.shape
    K = rest.shape[0]
    rb = _row_block(R, Cc)

    def body(f_ref, r_ref, o_ref):
        acc = f_ref[...].astype(F32)
        for j in range(K):
            acc = acc + r_ref[j].astype(F32)
        o_ref[...] = acc

    return pl.pallas_call(
        body, name=name, grid=(R // rb,),
        in_specs=[pl.BlockSpec((rb, Cc), lambda i: (i, 0)), pl.BlockSpec((K, rb, Cc), lambda i: (0, i, 0))],
        out_specs=pl.BlockSpec((rb, Cc), lambda i: (i, 0)),
        out_shape=jax.ShapeDtypeStruct((R, Cc), F32), compiler_params=_cp("parallel"))(first, rest)


def _adamw_math(w, gv, m, v):
    mn = ADAM_B1 * m + (1.0 - ADAM_B1) * gv
    vn = ADAM_B2 * v + (1.0 - ADAM_B2) * (gv * gv)
    m_hat = mn / (1.0 - ADAM_B1 ** ADAM_STEP)
    v_hat = vn / (1.0 - ADAM_B2 ** ADAM_STEP)
    return -ADAM_LR * (m_hat / (jnp.sqrt(v_hat) + ADAM_EPS) + ADAM_WD * w), mn, vn


def _adamw_halves(name, w, mine, theirs, m, v, core):
    R, Cc = w.shape
    r2 = R // 2
    rb = _row_block(r2, Cc, 1024 * 1024)
    nb2 = r2 // rb

    def body(c_ref, w_ref, mine_ref, theirs_ref, m_ref, v_ref, g_ref, d_ref, mo_ref, vo_ref):
        is_mine = (pl.program_id(0) // nb2) == c_ref[0]
        gv = jnp.where(is_mine, mine_ref[...], theirs_ref[...])
        g_ref[...] = gv
        d_ref[...], mo_ref[...], vo_ref[...] = _adamw_math(w_ref[...], gv, m_ref[...], v_ref[...])

    blk = pl.BlockSpec((rb, Cc), lambda i, c: (i, 0))
    half = lambda own: pl.BlockSpec(
        (rb, Cc), lambda i, c: (jnp.clip(i - (c[0] if own else 1 - c[0]) * nb2, 0, nb2 - 1), 0))
    return pl.pallas_call(
        body, name=name,
        grid_spec=pltpu.PrefetchScalarGridSpec(
            num_scalar_prefetch=1, grid=(2 * nb2,), in_specs=[blk, half(True), half(False), blk, blk],
            out_specs=[blk] * 4),
        out_shape=[jax.ShapeDtypeStruct((R, Cc), F32)] * 4, compiler_params=_cp("parallel"))(core, w, mine, theirs, m, v)


def _adamw(name, w, g, m, v):
    R, Cc = w.shape
    rb = _row_block(R, Cc, 1024 * 1024)

    def body(w_ref, g_ref, m_ref, v_ref, d_ref, mo_ref, vo_ref):
        d_ref[...], mo_ref[...], vo_ref[...] = _adamw_math(w_ref[...], g_ref[...], m_ref[...], v_ref[...])

    blk = pl.BlockSpec((rb, Cc), lambda i: (i, 0))
    return pl.pallas_call(
        body, name=name, grid=(R // rb,), in_specs=[blk] * 4, out_specs=[blk] * 3,
        out_shape=[jax.ShapeDtypeStruct((R, Cc), F32)] * 3, compiler_params=_cp("parallel"))(w, g, m, v)


def _pack(arrs):
    rows = []
    for a in arrs:
        flat = a.reshape(-1)
        pad = (-flat.shape[0]) % 128
        rows.append(jnp.pad(flat, (0, pad)).reshape(-1, 128))
    buf = jnp.concatenate(rows, axis=0)
    return jnp.pad(buf, ((0, (-buf.shape[0]) % 8), (0, 0)))


def _unpack(buf, shapes):
    out, r = [], 0
    for s in shapes:
        n = math.prod(s)
        nr = -(-n // 128)
        out.append(buf[r:r + nr].reshape(-1)[:n].reshape(s))
        r += nr
    return out


BIG = ("w_in", "w_out", "w_up", "w_down")
CONV = ("dn_conv_w", "ffn_conv_w")
REPL = ("attn_norm_g", "dn_a_log", "dn_dt_bias", "dn_out_norm_g", "sg_norm_g", "sg_w", "sg_b",
        "ffn_norm_g", "ffn_conv_b", "final_norm_g")
ORDER = ("attn_norm_g", "w_in", "dn_conv_w", "dn_a_log", "dn_dt_bias", "dn_out_norm_g", "sg_norm_g", "sg_w",
         "sg_b", "w_out", "ffn_norm_g", "w_up", "ffn_conv_w", "ffn_conv_b", "w_down", "final_norm_g")


def kernel(x, attn_norm_g, w_in, dn_conv_w, dn_a_log, dn_dt_bias, dn_out_norm_g, sg_norm_g, sg_w, sg_b, w_out, ffn_norm_g, w_up, ffn_conv_w, ffn_conv_b, w_down, final_norm_g, loss_target, m_attn_norm_g, m_w_in, m_dn_conv_w, m_dn_a_log, m_dn_dt_bias, m_dn_out_norm_g, m_sg_norm_g, m_sg_w, m_sg_b, m_w_out, m_ffn_norm_g, m_w_up, m_ffn_conv_w, m_ffn_conv_b, m_w_down, m_final_norm_g, v_attn_norm_g, v_w_in, v_dn_conv_w, v_dn_a_log, v_dn_dt_bias, v_dn_out_norm_g, v_sg_norm_g, v_sg_w, v_sg_b, v_w_out, v_ffn_norm_g, v_w_up, v_ffn_conv_w, v_ffn_conv_b, v_w_down, v_final_norm_g):
    W = dict(attn_norm_g=attn_norm_g, w_in=w_in, dn_conv_w=dn_conv_w, dn_a_log=dn_a_log, dn_dt_bias=dn_dt_bias,
             dn_out_norm_g=dn_out_norm_g, sg_norm_g=sg_norm_g, sg_w=sg_w, sg_b=sg_b, w_out=w_out,
             ffn_norm_g=ffn_norm_g, w_up=w_up, ffn_conv_w=ffn_conv_w, ffn_conv_b=ffn_conv_b, w_down=w_down,
             final_norm_g=final_norm_g)
    Mo = dict(attn_norm_g=m_attn_norm_g, w_in=m_w_in, dn_conv_w=m_dn_conv_w, dn_a_log=m_dn_a_log,
              dn_dt_bias=m_dn_dt_bias, dn_out_norm_g=m_dn_out_norm_g, sg_norm_g=m_sg_norm_g, sg_w=m_sg_w,
              sg_b=m_sg_b, w_out=m_w_out, ffn_norm_g=m_ffn_norm_g, w_up=m_w_up, ffn_conv_w=m_ffn_conv_w,
              ffn_conv_b=m_ffn_conv_b, w_down=m_w_down, final_norm_g=m_final_norm_g)
    Vo = dict(attn_norm_g=v_attn_norm_g, w_in=v_w_in, dn_conv_w=v_dn_conv_w, dn_a_log=v_dn_a_log,
              dn_dt_bias=v_dn_dt_bias, dn_out_norm_g=v_dn_out_norm_g, sg_norm_g=v_sg_norm_g, sg_w=v_sg_w,
              sg_b=v_sg_b, w_out=v_w_out, ffn_norm_g=v_ffn_norm_g, w_up=v_w_up, ffn_conv_w=v_ffn_conv_w,
              ffn_conv_b=v_ffn_conv_b, w_down=v_w_down, final_norm_g=v_final_norm_g)
    xi, yi, ci = lax.axis_index("x"), lax.axis_index("y"), lax.axis_index("c")
    chip = 2 * xi + yi

    me_lin = 4 * xi + 2 * yi + ci

    g_in, g_dnc = _gather_first(w_in[0].astype(BF16), dn_conv_w[0])
    late = ("w_out", "w_up", "w_down", "ffn_conv_w")
    late_shards = [W[n][0].astype(BF16) for n in late[:3]] + [ffn_conv_w[0]]
    late_lands = [lax.dynamic_update_index_in_dim(lax.empty((4,) + s.shape, s.dtype), s, chip, 0) for s in late_shards]
    n_late = 3 * len(late_shards)
    ssem, rsem, late_src, late_lands, token = _transfer_start("gather_rest_start", late_shards, late_lands,
                                                              n_late, _gather_copies, after=g_in)

    def late_weights(after):
        _, (g_out, g_up, g_down, g_ffc) = _transfer_wait("gather_rest_wait", ssem, rsem, late_src, late_lands,
                                                         _gather_copies, after)
        return dict(w_out=g_out.reshape(D_MODEL, D_MODEL), w_up=g_up.transpose(1, 0, 2).reshape(D_MODEL, 2 * D_FF),
                    w_down=g_down.reshape(D_FF, D_MODEL), ffn_conv_w=g_ffc.transpose(1, 0, 2).reshape(3, 2 * D_FF))

    full = dict(
        w_in=jnp.pad(g_in.transpose(1, 0, 2).reshape(D_MODEL, PROJ_COLS), ((0, 0), (0, PROJ_PAD - PROJ_COLS))),
        dn_conv_w=g_dnc.transpose(1, 0, 2).reshape(4, 3 * DN_WIDTH),
        attn_norm_g=attn_norm_g, dn_a_log=dn_a_log, dn_dt_bias=dn_dt_bias, dn_out_norm_g=dn_out_norm_g,
        sg_norm_g=sg_norm_g, sg_w=sg_w[0], sg_b=sg_b[0], ffn_norm_g=ffn_norm_g, ffn_conv_b=ffn_conv_b,
        final_norm_g=final_norm_g[None])

    pending = {}
    early_names = ("dn_out_norm_g", "sg_norm_g", "sg_w", "sg_b", "ffn_norm_g", "ffn_conv_w", "ffn_conv_b",
                   "final_norm_g")
    late_names = ("attn_norm_g", "dn_a_log", "dn_dt_bias", "dn_conv_w")

    def on_grad(name, gw):
        if name == "small_early":
            buf = _pack([gw[n] for n in early_names])
            land = lax.dynamic_update_index_in_dim(lax.empty((8,) + buf.shape, F32), buf, me_lin, 0)
            s_sem, r_sem, src, lands, tok = _transfer_start("small_early_start", [buf], [land], 7, _small_copies)
            pending[name] = (s_sem, r_sem, src, lands)
            return tok
        g8 = gw.reshape(D_MODEL, 4, -1).transpose(1, 0, 2) if name == "w_in" else gw
        g8 = g8.reshape(8, -1, g8.shape[-1])
        land = lax.empty((7,) + g8.shape[1:], BF16)
        s_sem, r_sem, src, lands, tok = _transfer_start(f"reduce_{name}_start", [g8], [land], 7, _pieces_copies)
        pending[name] = (s_sem, r_sem, src, lands)
        return tok

    loss_row, grad_x, g = _local_step(x[0], loss_target[0], full, dep=token, late_weights=late_weights,
                                      on_grad=on_grad)

    small_names = REPL + CONV
    late_all = _exchange_small(_pack([g[n] for n in late_names] + [loss_row]))
    late_sum = _sum_slots("sum_small_late", late_all[0], late_all[1:])
    s_sem, r_sem, src, lands = pending["small_early"]
    _, (early_all,) = _transfer_wait("small_early_wait", s_sem, r_sem, src, lands, _small_copies, grad_x)
    early_sum = _sum_slots("sum_small_early", early_all[0], early_all[1:])
    *late_vals, loss_sum = _unpack(late_sum, [g[n].shape for n in late_names] + [loss_row.shape])
    loss = loss_sum[0, 0]
    sg = dict(zip(late_names, late_vals))
    sg.update(zip(early_names, _unpack(early_sum, [g[n].shape for n in early_names])))
    sg["dn_conv_w"] = lax.dynamic_slice_in_dim(sg["dn_conv_w"], chip * (3 * DN_WIDTH // 4), 3 * DN_WIDTH // 4, axis=1)
    sg["ffn_conv_w"] = lax.dynamic_slice_in_dim(sg["ffn_conv_w"], chip * (2 * D_FF // 4), 2 * D_FF // 4, axis=1)

    halves = []
    for n in ("w_down", "w_up", "w_out", "w_in"):
        s_sem, r_sem, src, lands = pending[n]
        sent, got = _transfer_wait(f"reduce_{n}_wait", s_sem, r_sem, src, lands, _pieces_copies, grad_x)
        own = lax.dynamic_index_in_dim(sent[0], me_lin, axis=0, keepdims=False)
        halves.append(_sum_slots(f"sum_{n}", own, got[0]))
    theirs = _pair_swap(halves)
    core = ci.astype(jnp.int32).reshape(1)
    grads, delta, new_m, new_v = {}, {}, {}, {}
    for n, mine_h, their_h in zip(("w_down", "w_up", "w_out", "w_in"), halves, theirs):
        shp = W[n].shape
        gr, d, mn, vn = _adamw_halves(f"adamw_{n}", W[n][0], mine_h, their_h, Mo[n][0], Vo[n][0], core)
        grads[n], delta[n], new_m[n], new_v[n] = gr.reshape(shp), d.reshape(shp), mn.reshape(shp), vn.reshape(shp)
    shapes = [W[n].shape for n in small_names]
    for n in small_names:
        grads[n] = sg[n].reshape(W[n].shape)
    d, mn, vn = _adamw("adamw_small", _pack([W[n] for n in small_names]), _pack([grads[n] for n in small_names]),
                       _pack([Mo[n] for n in small_names]), _pack([Vo[n] for n in small_names]))
    for dst, buf in ((delta, d), (new_m, mn), (new_v, vn)):
        dst.update(zip(small_names, _unpack(buf, shapes)))

    return (loss, grad_x[None], *[grads[n] for n in ORDER], *[delta[n] for n in ORDER],
            *[new_m[n] for n in ORDER], *[new_v[n] for n in ORDER])
```

```python
import functools
import math

import jax
import jax.numpy as jnp
from jax import lax
from jax.experimental import pallas as pl
from jax.experimental.pallas import tpu as pltpu

F32 = jnp.float32
BF16 = jnp.bfloat16

D_MODEL = 1024
CHUNK = 64
SCAN_CHUNKS = 2
HEAD_DIM = 128
N_HEADS = 4
DN_WIDTH = 512
SG_WIDTH = 512
SG_GROUPS = 4
SG_BLOCK = 128
D_FF = 2816
PROJ_COLS = 3080
PROJ_PAD = 3200
BA_COL = 3072
EPS = 1e-6
NEG = -1e30
VMEM_LIMIT = 56 * 1024 * 1024

ADAM_LR = 0.001
ADAM_B1 = 0.9
ADAM_B2 = 0.999
ADAM_EPS = 1e-08
ADAM_WD = 0.01
ADAM_STEP = 10

MESH = pl.DeviceIdType.MESH
ANY = pl.BlockSpec(memory_space=pl.ANY)


def _cp(*sem):
    return pltpu.CompilerParams(dimension_semantics=sem, vmem_limit_bytes=VMEM_LIMIT)


def _bf(a):
    return a.astype(BF16)


def _nn(a, b):
    return jnp.dot(_bf(a), _bf(b), preferred_element_type=F32)


def _nt(a, b):
    return lax.dot_general(_bf(a), _bf(b), (((1,), (1,)), ((), ())), preferred_element_type=F32)


def _tn(a, b):
    return lax.dot_general(_bf(a), _bf(b), (((0,), (0,)), ((), ())), preferred_element_type=F32)


def _split(a):
    hi = _bf(a)
    return hi, _bf(a - hi.astype(F32))


def _sigmoid(x):
    return 0.5 * jnp.tanh(0.5 * x) + 0.5


def _silu(x):
    return x * _sigmoid(x)


def _dsilu(x):
    s = _sigmoid(x)
    return s * (1.0 + x * (1.0 - s))


_GELU_C = math.sqrt(2.0 / math.pi)
_GELU_A = 0.044715


def _gelu(x):
    return 0.5 * x * (1.0 + jnp.tanh(_GELU_C * (x + _GELU_A * x * x * x)))


def _dgelu(x):
    t = jnp.tanh(_GELU_C * (x + _GELU_A * x * x * x))
    return 0.5 * (1.0 + t) + 0.5 * x * (1.0 - t * t) * _GELU_C * (1.0 + 3.0 * _GELU_A * x * x)


def _softplus(x):
    return jnp.maximum(x, 0.0) + jnp.log(1.0 + jnp.exp(-jnp.abs(x)))


def _mm_nn(name, a, b, out_dtype, tm, tn, res=None):
    M, K = a.shape
    N = b.shape[1]
    tm, tn = min(tm, M), min(tn, N)

    def body(*refs):
        a_ref, b_ref = refs[0], refs[1]
        o_ref = refs[-1]
        acc = _nn(a_ref[...], b_ref[...])
        if res is not None:
            acc = acc + refs[2][...]
        o_ref[...] = acc.astype(o_ref.dtype)

    in_specs = [pl.BlockSpec((tm, K), lambda j, i: (i, 0)), pl.BlockSpec((K, tn), lambda j, i: (0, j))]
    args = [a, b]
    if res is not None:
        in_specs.append(pl.BlockSpec((tm, tn), lambda j, i: (i, j)))
        args.append(res)
    return pl.pallas_call(
        body, name=name, grid=(N // tn, M // tm), in_specs=in_specs,
        out_specs=pl.BlockSpec((tm, tn), lambda j, i: (i, j)),
        out_shape=jax.ShapeDtypeStruct((M, N), out_dtype),
        compiler_params=_cp("parallel", "parallel"))(*args)


def _in_proj(h, w4, tm=512):
    T, K = h.shape
    g, _, wc = w4.shape
    tm = min(tm, T)

    def body(h_ref, w4_ref, p_ref, w_ref):
        @pl.when(pl.program_id(0) == 0)
        def _():
            w_ref[:, g * wc:] = jnp.zeros((K, PROJ_PAD - g * wc), BF16)
            for j in range(g):
                w_ref[:, j * wc:(j + 1) * wc] = w4_ref[j]
        p_ref[...] = _nn(h_ref[...], w_ref[...])

    return pl.pallas_call(
        body, name="in_proj", grid=(T // tm,),
        in_specs=[pl.BlockSpec((tm, K), lambda i: (i, 0)), pl.BlockSpec((g, K, wc), lambda i: (0, 0, 0))],
        out_specs=[pl.BlockSpec((tm, PROJ_PAD), lambda i: (i, 0)), pl.BlockSpec((K, PROJ_PAD), lambda i: (0, 0))],
        out_shape=[jax.ShapeDtypeStruct((T, PROJ_PAD), F32), jax.ShapeDtypeStruct((K, PROJ_PAD), BF16)],
        compiler_params=_cp("arbitrary"))(h, w4)


def _with_dep(in_specs, args, dep):
    if dep is None:
        return in_specs, args
    return in_specs + [ANY], args + [dep]


def _mm_nt(name, a, b, out_dtype, tm, tn, dep=None):
    M, K = a.shape
    N = b.shape[0]
    tm, tn = min(tm, M), min(tn, N)

    def body(a_ref, b_ref, *rest):
        o_ref = rest[-1]
        o_ref[...] = _nt(a_ref[...], b_ref[...]).astype(o_ref.dtype)

    in_specs, args = _with_dep(
        [pl.BlockSpec((tm, K), lambda i, j: (i, 0)), pl.BlockSpec((tn, K), lambda i, j: (j, 0))], [a, b], dep)
    return pl.pallas_call(
        body, name=name, grid=(M // tm, N // tn), in_specs=in_specs,
        out_specs=pl.BlockSpec((tm, tn), lambda i, j: (i, j)),
        out_shape=jax.ShapeDtypeStruct((M, N), out_dtype),
        compiler_params=_cp("parallel", "parallel"))(*args)


def _mm_tn(name, a, b, tm, tn, tk, col_major_tiles=False, col_groups=None):
    T, M = a.shape
    N = b.shape[1]
    tm, tn, tk = min(tm, M), min(tn, N), min(tk, T)
    nk = T // tk

    def body(a_ref, b_ref, o_ref, acc_ref):
        k = pl.program_id(2)

        @pl.when(k == 0)
        def _():
            acc_ref[...] = jnp.zeros_like(acc_ref)
        acc_ref[...] += _tn(a_ref[...], b_ref[...])

        @pl.when(k == nk - 1)
        def _():
            if col_groups:
                for j in range(col_groups[0]):
                    o_ref[j] = acc_ref[:, j * col_groups[1]:(j + 1) * col_groups[1]].astype(BF16)
            else:
                o_ref[...] = acc_ref[...].astype(BF16).reshape(o_ref.shape)

    if col_groups:
        assert tm == M and tn == N and col_groups[0] * col_groups[1] <= N
        out_spec = pl.BlockSpec((col_groups[0], M, col_groups[1]), lambda i, j, k: (0, 0, 0))
        out_shape = jax.ShapeDtypeStruct((col_groups[0], M, col_groups[1]), BF16)
    elif col_major_tiles:
        assert tm == M
        out_spec = pl.BlockSpec((1, tm, tn), lambda i, j, k: (j, 0, 0))
        out_shape = jax.ShapeDtypeStruct((N // tn, M, tn), BF16)
    else:
        out_spec = pl.BlockSpec((tm, tn), lambda i, j, k: (i, j))
        out_shape = jax.ShapeDtypeStruct((M, N), BF16)
    return pl.pallas_call(
        body, name=name, grid=(M // tm, N // tn, nk),
        in_specs=[pl.BlockSpec((tk, tm), lambda i, j, k: (k, i)), pl.BlockSpec((tk, tn), lambda i, j, k: (k, j))],
        out_specs=out_spec, out_shape=out_shape, scratch_shapes=[pltpu.VMEM((tm, tn), F32)],
        compiler_params=_cp("parallel", "parallel", "arbitrary"))(a, b)


def _rms_fwd(name, x, g, rb=512, dep=None):
    T, Dm = x.shape
    rb = min(rb, T)

    def body(x_ref, g_ref, *rest):
        h_ref = rest[-1]
        xv = x_ref[...]
        r = lax.rsqrt(jnp.mean(xv * xv, axis=-1, keepdims=True) + EPS)
        h_ref[...] = (xv * r * g_ref[...]).astype(BF16)

    in_specs, args = _with_dep(
        [pl.BlockSpec((rb, Dm), lambda i: (i, 0)), pl.BlockSpec((1, Dm), lambda i: (0, 0))], [x, g], dep)
    return pl.pallas_call(
        body, name=name, grid=(T // rb,), in_specs=in_specs,
        out_specs=pl.BlockSpec((rb, Dm), lambda i: (i, 0)),
        out_shape=jax.ShapeDtypeStruct((T, Dm), BF16), compiler_params=_cp("parallel"))(*args)


def _rms_bwd(name, dh, x, g, dres, rb=512):
    T, Dm = x.shape
    rb = min(rb, T)

    def body(dh_ref, x_ref, g_ref, dres_ref, dx_ref, gg_ref):
        @pl.when(pl.program_id(0) == 0)
        def _():
            gg_ref[...] = jnp.zeros_like(gg_ref)
        xv = x_ref[...]
        r = lax.rsqrt(jnp.mean(xv * xv, axis=-1, keepdims=True) + EPS)
        xh = xv * r
        dhv = dh_ref[...]
        gg_ref[...] += jnp.sum(dhv * xh, axis=0, keepdims=True)
        dxh = dhv * g_ref[...]
        dx_ref[...] = dres_ref[...] + r * (dxh - xh * jnp.mean(dxh * xh, axis=-1, keepdims=True))

    row = pl.BlockSpec((rb, Dm), lambda i: (i, 0))
    vec = pl.BlockSpec((1, Dm), lambda i: (0, 0))
    return pl.pallas_call(
        body, name=name, grid=(T // rb,), in_specs=[row, row, vec, row], out_specs=[row, vec],
        out_shape=[jax.ShapeDtypeStruct((T, Dm), F32), jax.ShapeDtypeStruct((1, Dm), F32)],
        compiler_params=_cp("arbitrary"))(dh, x, g, dres)


def _loss_head(x3, tgt, g, rb=512):
    T, Dm = x3.shape
    rb = min(rb, T)

    def body(x_ref, t_ref, g_ref, loss_ref, dx_ref, gg_ref):
        @pl.when(pl.program_id(0) == 0)
        def _():
            gg_ref[...] = jnp.zeros_like(gg_ref)
            loss_ref[...] = jnp.zeros_like(loss_ref)
        xv = x_ref[...]
        r = lax.rsqrt(jnp.mean(xv * xv, axis=-1, keepdims=True) + EPS)
        xh = xv * r
        e = xh * g_ref[...] - t_ref[...]
        loss_ref[...] += jnp.zeros_like(loss_ref) + (0.5 / Dm) * jnp.sum(e * e)
        dy = e * (1.0 / Dm)
        gg_ref[...] += jnp.sum(dy * xh, axis=0, keepdims=True)
        dxh = dy * g_ref[...]
        dx_ref[...] = r * (dxh - xh * jnp.mean(dxh * xh, axis=-1, keepdims=True))

    row = pl.BlockSpec((rb, Dm), lambda i: (i, 0))
    vec = pl.BlockSpec((1, Dm), lambda i: (0, 0))
    return pl.pallas_call(
        body, name="loss_head", grid=(T // rb,), in_specs=[row, row, vec],
        out_specs=[pl.BlockSpec((1, 128), lambda i: (0, 0)), row, vec],
        out_shape=[jax.ShapeDtypeStruct((1, 128), F32), jax.ShapeDtypeStruct((T, Dm), F32),
                   jax.ShapeDtypeStruct((1, Dm), F32)],
        compiler_params=_cp("arbitrary"))(x3, tgt, g)


def _halo_prev_spec(rb, width):
    return pl.BlockSpec((8, width), lambda i: (jnp.maximum(i * (rb // 8) - 1, 0), 0))


def _halo_next_spec(rb, width, T):
    return pl.BlockSpec((8, width), lambda i: (jnp.minimum((i + 1) * (rb // 8), T // 8 - 1), 0))


LANES = 128
FF_STRIPS = D_FF // LANES
ROW_CHUNK = 32


def _strip(j, base=0):
    return pl.ds(pl.multiple_of(base + j * LANES, LANES), LANES)


def _ffn_act(up, w, b, rb=256):
    T, W = up.shape
    rb = min(rb, T)

    def body(up_ref, halo_ref, w_ref, b_ref, act_ref, ext_scr):
        first = pl.program_id(0) == 0

        def strip(j, slot):
            halves = (_strip(j), _strip(j, D_FF))
            wv = [w_ref[:, cols] for cols in halves]
            bv = [b_ref[:, cols] for cols in halves]
            for h, cols in enumerate(halves):
                ext_scr[slot, h,0:8] = jnp.where(first, 0.0, halo_ref[:, cols])
                ext_scr[slot, h,8:] = up_ref[:, cols]
            for r0 in range(0, rb, ROW_CHUNK):
                n = min(ROW_CHUNK, rb - r0)
                c = [ext_scr[slot, h,6 + r0:6 + r0 + n] * wv[h][0:1] + ext_scr[slot, h,7 + r0:7 + r0 + n] * wv[h][1:2]
                     + ext_scr[slot, h,8 + r0:8 + r0 + n] * wv[h][2:3] + bv[h] for h in range(2)]
                act_ref[r0:r0 + n, halves[0]] = (_silu(c[0]) * c[1]).astype(BF16)

        def pair(jj, carry):
            strip(2 * jj, 0)
            strip(2 * jj + 1, 1)
            return carry

        lax.fori_loop(0, FF_STRIPS // 2, pair, 0)

    return pl.pallas_call(
        body, name="ffn_act", grid=(T // rb,),
        in_specs=[pl.BlockSpec((rb, W), lambda i: (i, 0)), _halo_prev_spec(rb, W),
                  pl.BlockSpec((3, W), lambda i: (0, 0)), pl.BlockSpec((1, W), lambda i: (0, 0))],
        out_specs=pl.BlockSpec((rb, D_FF), lambda i: (i, 0)),
        out_shape=jax.ShapeDtypeStruct((T, D_FF), BF16),
        scratch_shapes=[pltpu.VMEM((2, 2, rb + 8, LANES), F32)], compiler_params=_cp("parallel"))(up, up, w, b)


def _ffn_act_bwd(up, dact, w, b, rb=128, dep=None):
    T, W = up.shape
    rb = min(rb, T)
    nb = T // rb
    re = rb + 8

    def body(up_ref, prev_ref, next_ref, da_ref, danext_ref, w_ref, b_ref, *rest):
        dup_ref, gw_ref, gb_ref, ext_scr, dc_scr = rest[-5:]
        i = pl.program_id(0)

        @pl.when(i == 0)
        def _():
            gw_ref[...] = jnp.zeros_like(gw_ref)
            gb_ref[...] = jnp.zeros_like(gb_ref)
        last = i == nb - 1

        def fold8(a):
            return jnp.sum(a.reshape(a.shape[0] // 8, 8, LANES), axis=0)

        def strip(j, slot):
            halves = (_strip(j), _strip(j, D_FF))
            wv = [w_ref[:, cols] for cols in halves]
            bv = [b_ref[:, cols] for cols in halves]
            for h, cols in enumerate(halves):
                ext_scr[slot, h,0:8] = jnp.where(i > 0, prev_ref[:, cols], 0.0)
                ext_scr[slot, h,8:8 + rb] = up_ref[:, cols]
                ext_scr[slot, h,8 + rb:] = next_ref[:, cols]
            gb = [jnp.zeros((8, LANES), F32) for _ in range(2)]
            gw = [[jnp.zeros((8, LANES), F32) for _ in range(3)] for _ in range(2)]
            for r0 in range(0, re, ROW_CHUNK):
                n = min(ROW_CHUNK, re - r0)
                tp = [[ext_scr[slot, h,6 + k + r0:6 + k + r0 + n] for k in range(3)] for h in range(2)]
                c = [tp[h][0] * wv[h][0:1] + tp[h][1] * wv[h][1:2] + tp[h][2] * wv[h][2:3] + bv[h] for h in range(2)]
                if r0 < rb:
                    da = da_ref[r0:r0 + n, halves[0]]
                else:
                    da = jnp.where(last, 0.0, danext_ref[:, halves[0]])
                s = _sigmoid(c[0])
                gs = c[0] * s
                dcs = (da * c[1] * (s + gs * (1.0 - s)), da * gs)
                for h in range(2):
                    dc_scr[slot, h,r0:r0 + n] = dcs[h]
                    if r0 < rb:
                        gb[h] = gb[h] + fold8(dcs[h])
                        for k in range(3):
                            gw[h][k] = gw[h][k] + fold8(tp[h][k] * dcs[h])
            for r0 in range(0, rb, ROW_CHUNK):
                n = min(ROW_CHUNK, rb - r0)
                for h, cols in enumerate(halves):
                    dup = (dc_scr[slot, h,r0:r0 + n] * wv[h][2:3] + dc_scr[slot, h,r0 + 1:r0 + 1 + n] * wv[h][1:2]
                           + dc_scr[slot, h,r0 + 2:r0 + 2 + n] * wv[h][0:1])
                    dup_ref[r0:r0 + n, cols] = dup.astype(BF16)
            for h, cols in enumerate(halves):
                gb_ref[:, cols] += jnp.sum(gb[h], axis=0, keepdims=True)
                for k in range(3):
                    gw_ref[k:k + 1, cols] += jnp.sum(gw[h][k], axis=0, keepdims=True)

        def pair(jj, carry):
            strip(2 * jj, 0)
            strip(2 * jj + 1, 1)
            return carry

        lax.fori_loop(0, FF_STRIPS // 2, pair, 0)

    in_specs, args = _with_dep(
        [pl.BlockSpec((rb, W), lambda i: (i, 0)), _halo_prev_spec(rb, W), _halo_next_spec(rb, W, T),
         pl.BlockSpec((rb, D_FF), lambda i: (i, 0)), _halo_next_spec(rb, D_FF, T),
         pl.BlockSpec((3, W), lambda i: (0, 0)), pl.BlockSpec((1, W), lambda i: (0, 0))],
        [up, up, up, dact, dact, w, b], dep)
    return pl.pallas_call(
        body, name="ffn_act_bwd", grid=(nb,), in_specs=in_specs,
        out_specs=[pl.BlockSpec((rb, W), lambda i: (i, 0)), pl.BlockSpec((3, W), lambda i: (0, 0)),
                   pl.BlockSpec((1, W), lambda i: (0, 0))],
        out_shape=[jax.ShapeDtypeStruct((T, W), BF16), jax.ShapeDtypeStruct((3, W), F32),
                   jax.ShapeDtypeStruct((1, W), F32)],
        scratch_shapes=[pltpu.VMEM((2, 2, rb + 16, LANES), F32), pltpu.VMEM((2, 2, re, LANES), F32)],
        compiler_params=_cp("arbitrary"))(*args)


def _lane_iota(shape):
    return lax.broadcasted_iota(jnp.int32, shape, len(shape) - 1)


def _dn_act(p, conv_w, alog_row, dtb_row, rb=256):
    T = p.shape[0]
    rb = min(rb, T)
    W3 = 3 * DN_WIDTH

    def body(p_ref, halo_ref, ba_ref, w_ref, al_ref, dt_ref, q_ref, k_ref, v_ref, bg_ref, ext_scr):
        first = pl.program_id(0) == 0
        outs = (q_ref, k_ref, v_ref)
        for j in range(3 * N_HEADS):
            kind, h = divmod(j, N_HEADS)
            cols = slice(j * HEAD_DIM, (j + 1) * HEAD_DIM)
            cur = p_ref[:, cols]
            ext_scr[j, 0:8] = jnp.where(first, 0.0, halo_ref[:, cols])
            ext_scr[j, 8:] = cur
            wv = w_ref[:, cols]
            s = _silu(ext_scr[j, 5:5 + rb] * wv[0:1] + ext_scr[j, 6:6 + rb] * wv[1:2]
                      + ext_scr[j, 7:7 + rb] * wv[2:3] + cur * wv[3:4])
            if kind < 2:
                scale = HEAD_DIM ** -0.5 if kind == 0 else 1.0
                s = s * (lax.rsqrt(jnp.sum(s * s, axis=-1, keepdims=True) + EPS) * scale)
            outs[kind][:, h * HEAD_DIM:(h + 1) * HEAD_DIM] = s
        ba = ba_ref[...]
        lane = _lane_iota(ba.shape)
        beta = _sigmoid(ba)
        g = -jnp.exp(al_ref[...]) * _softplus(ba + dt_ref[...])
        bg_ref[...] = jnp.where(lane < N_HEADS, beta, jnp.where(lane < 2 * N_HEADS, g, 0.0))

    row512 = pl.BlockSpec((rb, DN_WIDTH), lambda i: (i, 0))
    row128 = pl.BlockSpec((rb, 128), lambda i: (i, 0))
    vec128 = pl.BlockSpec((1, 128), lambda i: (0, 0))
    return pl.pallas_call(
        body, name="dn_act", grid=(T // rb,),
        in_specs=[pl.BlockSpec((rb, W3), lambda i: (i, 0)), _halo_prev_spec(rb, W3),
                  pl.BlockSpec((rb, 128), lambda i: (i, BA_COL // 128)),
                  pl.BlockSpec((4, W3), lambda i: (0, 0)), vec128, vec128],
        out_specs=[row512, row512, row512, row128],
        out_shape=[jax.ShapeDtypeStruct((T, DN_WIDTH), F32)] * 3 + [jax.ShapeDtypeStruct((T, 128), F32)],
        scratch_shapes=[pltpu.VMEM((3 * N_HEADS, rb + 8, HEAD_DIM), F32)],
        compiler_params=_cp("parallel"))(p, p, p, conv_w, alog_row, dtb_row)


def _dn_act_bwd(p, conv_w, alog_row, dtb_row, dq, dk, dv, dbg, dp_mid, rb=256):
    T = p.shape[0]
    rb = min(rb, T)
    nb = T // rb
    re = rb + 8
    W3 = 3 * DN_WIDTH

    def body(p_ref, prev_ref, next_ref, ba_ref, w_ref, al_ref, dt_ref, dq_ref, dqn_ref, dk_ref, dkn_ref,
             dv_ref, dvn_ref, dbg_ref, mid_ref, draw_ref, gw_ref, gad_ref, ext_scr, dc_scr):
        i = pl.program_id(0)
        draw_ref[:, W3:2 * W3] = mid_ref[...]

        @pl.when(i == 0)
        def _():
            gw_ref[...] = jnp.zeros_like(gw_ref)
            gad_ref[...] = jnp.zeros_like(gad_ref)
        row = lax.broadcasted_iota(jnp.int32, (re, 1), 0)
        live = (row < rb) | (i < nb - 1)
        d_refs = ((dq_ref, dqn_ref), (dk_ref, dkn_ref), (dv_ref, dvn_ref))
        for j in range(3 * N_HEADS):
            kind, h = divmod(j, N_HEADS)
            cols = slice(j * HEAD_DIM, (j + 1) * HEAD_DIM)
            hcols = slice(h * HEAD_DIM, (h + 1) * HEAD_DIM)
            ext_scr[j, 0:8] = jnp.where(i > 0, prev_ref[:, cols], 0.0)
            ext_scr[j, 8:8 + rb] = p_ref[:, cols]
            ext_scr[j, 8 + rb:] = next_ref[:, cols]
            tp = [ext_scr[j, 5 + k:5 + k + re] for k in range(4)]
            wv = w_ref[:, cols]
            c = tp[0] * wv[0:1] + tp[1] * wv[1:2] + tp[2] * wv[2:3] + tp[3] * wv[3:4]
            sg = _sigmoid(c)
            s = c * sg
            d_in = jnp.where(live, jnp.concatenate([d_refs[kind][0][:, hcols], d_refs[kind][1][:, hcols]], axis=0), 0.0)
            if kind < 2:
                scale = HEAD_DIM ** -0.5 if kind == 0 else 1.0
                n = lax.rsqrt(jnp.sum(s * s, axis=-1, keepdims=True) + EPS)
                hat = s * n
                d_in = (n * scale) * (d_in - hat * jnp.sum(hat * d_in, axis=-1, keepdims=True))
            dc = d_in * (sg + s * (1.0 - sg))
            dc_scr[j] = dc
            dcc = dc[0:rb]
            draw = (dcc * wv[3:4] + dc_scr[j, 1:1 + rb] * wv[2:3] + dc_scr[j, 2:2 + rb] * wv[1:2]
                    + dc_scr[j, 3:3 + rb] * wv[0:1])
            draw_ref[:, cols] = draw.astype(BF16)
            for k in range(4):
                gw_ref[k:k + 1, cols] += jnp.sum(tp[k][0:rb] * dcc, axis=0, keepdims=True)
        ba = ba_ref[...]
        dbg = dbg_ref[...]
        lane = _lane_iota(ba.shape)
        beta = _sigmoid(ba)
        ea = jnp.exp(al_ref[...])
        z = ba + dt_ref[...]
        d_a = dbg * (-ea) * _sigmoid(z)
        dba = jnp.where(lane < N_HEADS, dbg * beta * (1.0 - beta), jnp.where(lane < 2 * N_HEADS, d_a, 0.0))
        draw_ref[:, BA_COL:] = dba.astype(BF16)
        isg = (lane >= N_HEADS) & (lane < 2 * N_HEADS)
        g = -ea * _softplus(z)
        gad_ref[0:1, :] += jnp.sum(jnp.where(isg, dbg * g, 0.0), axis=0, keepdims=True)
        gad_ref[1:2, :] += jnp.sum(jnp.where(isg, d_a, 0.0), axis=0, keepdims=True)

    row512 = pl.BlockSpec((rb, DN_WIDTH), lambda i: (i, 0))
    row128 = pl.BlockSpec((rb, 128), lambda i: (i, 0))
    vec128 = pl.BlockSpec((1, 128), lambda i: (0, 0))
    next512 = _halo_next_spec(rb, DN_WIDTH, T)
    return pl.pallas_call(
        body, name="dn_act_bwd", grid=(nb,),
        in_specs=[pl.BlockSpec((rb, W3), lambda i: (i, 0)), _halo_prev_spec(rb, W3), _halo_next_spec(rb, W3, T),
                  pl.BlockSpec((rb, 128), lambda i: (i, BA_COL // 128)),
                  pl.BlockSpec((4, W3), lambda i: (0, 0)), vec128, vec128,
                  row512, next512, row512, next512, row512, next512, row128,
                  pl.BlockSpec((rb, W3), lambda i: (i, 0))],
        out_specs=[pl.BlockSpec((rb, PROJ_PAD), lambda i: (i, 0)),
                   pl.BlockSpec((4, W3), lambda i: (0, 0)), pl.BlockSpec((2, 128), lambda i: (0, 0))],
        out_shape=[jax.ShapeDtypeStruct((T, PROJ_PAD), BF16),
                   jax.ShapeDtypeStruct((4, W3), F32), jax.ShapeDtypeStruct((2, 128), F32)],
        scratch_shapes=[pltpu.VMEM((3 * N_HEADS, rb + 16, HEAD_DIM), F32), pltpu.VMEM((3 * N_HEADS, re, HEAD_DIM), F32)],
        compiler_params=_cp("arbitrary"))(p, p, p, p, conv_w, alog_row, dtb_row, dq, dq, dk, dk, dv, dv, dbg, dp_mid)


def _tri(incl):
    ii = lax.broadcasted_iota(jnp.int32, (CHUNK, CHUNK), 0)
    jj = lax.broadcasted_iota(jnp.int32, (CHUNK, CHUNK), 1)
    return ii, jj, ((ii >= jj) if incl else (ii > jj))


def _dn_chunk(k, bg, cb=4):
    T = k.shape[0]
    N = T // CHUNK
    cb = min(cb, N)

    def body(k_ref, bg_ref, gc_ref, gct_ref, l_ref):
        ii, jj, incl = _tri(True)
        tri = incl.astype(F32)
        U = range(cb)
        bgv = [bg_ref[u * CHUNK:(u + 1) * CHUNK, :] for u in U]
        gc = [jnp.dot(tri, bgv[u], precision=lax.Precision.HIGHEST, preferred_element_type=F32) for u in U]
        gct = [gc[u].T for u in U]
        kk = [[None] * N_HEADS for _ in U]
        for u in U:
            gc_ref[u * CHUNK:(u + 1) * CHUNK, :] = gc[u]
            gct_ref[u] = gct[u][0:8]
            for h in range(N_HEADS):
                kh = k_ref[u * CHUNK:(u + 1) * CHUNK, h * HEAD_DIM:(h + 1) * HEAD_DIM]
                kk[u][h] = _nt(kh * bgv[u][:, h:h + 1], kh)
        for u in U:
            for h in range(N_HEADS):
                gcol = gc[u][:, N_HEADS + h:N_HEADS + h + 1]
                grow = gct[u][N_HEADS + h:N_HEADS + h + 1, :]
                l_ref[u, h] = kk[u][h] * jnp.exp(jnp.where(ii > jj, gcol - grow, NEG))

    rows = cb * CHUNK
    return pl.pallas_call(
        body, name="dn_chunk", grid=(N // cb,),
        in_specs=[pl.BlockSpec((rows, DN_WIDTH), lambda n: (n, 0)), pl.BlockSpec((rows, 128), lambda n: (n, 0))],
        out_specs=[pl.BlockSpec((rows, 128), lambda n: (n, 0)), pl.BlockSpec((cb, 8, CHUNK), lambda n: (n, 0, 0)),
                   pl.BlockSpec((cb, N_HEADS, CHUNK, CHUNK), lambda n: (n, 0, 0, 0))],
        out_shape=[jax.ShapeDtypeStruct((T, 128), F32), jax.ShapeDtypeStruct((N, 8, CHUNK), F32),
                   jax.ShapeDtypeStruct((N, N_HEADS, CHUNK, CHUNK), F32)],
        compiler_params=_cp("parallel"))(k, bg)


def _tri_inv(lt):
    S = lt.shape[1]

    def body(l_ref, a_ref):
        col = lax.broadcasted_iota(jnp.int32, (CHUNK, S), 0)
        for i in range(CHUNK):
            def step(j, acc):
                return acc - l_ref[pl.ds(i * CHUNK + j, 1), :] * a_ref[j]
            a_ref[i] = lax.fori_loop(0, i, step, (col == i).astype(F32))

    return pl.pallas_call(
        body, name="tri_inv", out_shape=jax.ShapeDtypeStruct((CHUNK, CHUNK, S), F32),
        compiler_params=pltpu.CompilerParams(vmem_limit_bytes=VMEM_LIMIT))(lt)


def _dn_head_terms(qh, kh, vh, beta, gcol, grow):
    ii, jj, incl = _tri(True)
    gam = jnp.exp(jnp.where(incl, gcol - grow, NEG))
    glast = grow[:, CHUNK - 1:CHUNK]
    cd = jnp.exp(glast)
    shape = (CHUNK, HEAD_DIM)
    E = jnp.broadcast_to(jnp.exp(gcol), shape)
    Fd = jnp.broadcast_to(jnp.exp(glast - gcol), shape)
    beta = jnp.broadcast_to(beta, shape)
    kb = kh * beta
    return dict(ii=ii, jj=jj, gam=gam, E=E, F=Fd, beta=beta, cd=cd, kb=kb, vb=vh * beta, W=kb * E, qE=qh * E,
                kt=kh * Fd)


def _apply_a(a, u):
    hi, lo = _split(a)
    ub = _bf(u)
    return jnp.dot(hi, ub, preferred_element_type=F32) + jnp.dot(lo, ub, preferred_element_type=F32)


def _dn_scan(q, k, v, bg, gc, gct, a):
    T = q.shape[0]
    N = T // CHUNK

    cb = min(SCAN_CHUNKS, N)

    def body(q_ref, k_ref, v_ref, bg_ref, gc_ref, gct_ref, a_ref, o_ref, sall_ref, s_ref):
        @pl.when(pl.program_id(0) == 0)
        def _():
            s_ref[...] = jnp.zeros_like(s_ref)
        H = range(N_HEADS)
        sl = [slice(h * HEAD_DIM, (h + 1) * HEAD_DIM) for h in H]
        pre = []
        for u in range(cb):
            r = slice(u * CHUNK, (u + 1) * CHUNK)
            bgv, gcv, gctv = bg_ref[r, :], gc_ref[r, :], gct_ref[u]
            q_, k_ = [q_ref[r, s] for s in sl], [k_ref[r, s] for s in sl]
            t = [_dn_head_terms(q_[h], k_[h], v_ref[r, sl[h]], bgv[:, h:h + 1],
                                gcv[:, N_HEADS + h:N_HEADS + h + 1], gctv[N_HEADS + h:N_HEADS + h + 1, :]) for h in H]
            P = [_nt(q_[h], k_[h]) * t[h]["gam"] for h in H]
            pre.append((r, t, P))
        S = [s_ref[h] for h in H]
        for u in range(cb):
            r, t, P = pre[u]
            for h in H:
                sall_ref[u, h] = S[h]
            WS = [_nn(t[h]["W"], S[h]) for h in H]
            qS = [_nn(t[h]["qE"], S[h]) for h in H]
            vn = [_apply_a(a_ref[u, h], t[h]["vb"] - WS[h]) for h in H]
            Pv = [_nn(P[h], vn[h]) for h in H]
            kv = [_tn(t[h]["kt"], vn[h]) for h in H]
            for h in H:
                o_ref[r, sl[h]] = qS[h] + Pv[h]
            S = [t[h]["cd"] * S[h] + kv[h] for h in H]
        for h in H:
            s_ref[h] = S[h]

    row512 = pl.BlockSpec((cb * CHUNK, DN_WIDTH), lambda n: (n, 0))
    row128 = pl.BlockSpec((cb * CHUNK, 128), lambda n: (n, 0))
    return pl.pallas_call(
        body, name="dn_scan", grid=(N // cb,),
        in_specs=[row512, row512, row512, row128, row128, pl.BlockSpec((cb, 8, CHUNK), lambda n: (n, 0, 0)),
                  pl.BlockSpec((cb, N_HEADS, CHUNK, CHUNK), lambda n: (n, 0, 0, 0))],
        out_specs=[row512, pl.BlockSpec((cb, N_HEADS, HEAD_DIM, HEAD_DIM), lambda n: (n, 0, 0, 0))],
        out_shape=[jax.ShapeDtypeStruct((T, DN_WIDTH), F32),
                   jax.ShapeDtypeStruct((N, N_HEADS, HEAD_DIM, HEAD_DIM), F32)],
        scratch_shapes=[pltpu.VMEM((N_HEADS, HEAD_DIM, HEAD_DIM), F32)],
        compiler_params=_cp("arbitrary"))(q, k, v, bg, gc, gct, a)


def _dn_scan_bwd(q, k, v, bg, gc, gct, a, a_t, sall, do, dep=None):
    T = q.shape[0]
    N = T // CHUNK

    cb = min(SCAN_CHUNKS, N)
    nb = N // cb

    def body(q_ref, k_ref, v_ref, bg_ref, gc_ref, gct_ref, a_ref, at_ref, sall_ref, do_ref, *rest):
        dq_ref, dk_ref, dv_ref, dbg_ref, ds_ref = rest[-5:]
        @pl.when(pl.program_id(0) == 0)
        def _():
            ds_ref[...] = jnp.zeros_like(ds_ref)
        lane = _lane_iota((CHUNK, 128))
        rowi = lax.broadcasted_iota(jnp.int32, (CHUNK, 1), 0)
        ii, jj, _ = _tri(True)
        rev = (jj >= ii).astype(F32)
        H = range(N_HEADS)
        sl = [slice(h * HEAD_DIM, (h + 1) * HEAD_DIM) for h in H]
        pre = {}
        for u in reversed(range(cb)):
            r = slice(u * CHUNK, (u + 1) * CHUNK)
            bgv, gcv, gctv = bg_ref[r, :], gc_ref[r, :], gct_ref[u]
            q_, k_, v_ = [q_ref[r, s] for s in sl], [k_ref[r, s] for s in sl], [v_ref[r, s] for s in sl]
            dO = [do_ref[r, s] for s in sl]
            t = [_dn_head_terms(q_[h], k_[h], v_[h], bgv[:, h:h + 1], gcv[:, N_HEADS + h:N_HEADS + h + 1],
                                gctv[N_HEADS + h:N_HEADS + h + 1, :]) for h in H]
            beta = [t[h]["beta"] for h in H]
            S = [sall_ref[u, h] for h in H]
            A = [a_ref[u, h] for h in H]
            WS = [_nn(t[h]["W"], S[h]) for h in H]
            KK = [_nt(t[h]["kb"], k_[h]) for h in H]
            QK = [_nt(q_[h], k_[h]) for h in H]
            d_qE = [_nt(dO[h], S[h]) for h in H]
            vn = [_apply_a(A[h], t[h]["vb"] - WS[h]) for h in H]
            PtdO = [_tn(QK[h] * t[h]["gam"], dO[h]) for h in H]
            qEdO = [_tn(t[h]["qE"], dO[h]) for h in H]
            dOvn = [_nt(dO[h], vn[h]) for h in H]
            dQK = [jnp.where(ii >= jj, dOvn[h], 0.0) * t[h]["gam"] for h in H]
            dQKk = [_nn(dQK[h], k_[h]) for h in H]
            dQKq = [_tn(dQK[h], q_[h]) for h in H]
            pre[u] = (r, q_, k_, v_, beta, t, S, A, KK, QK, d_qE, vn, PtdO, qEdO, dQK, dQKk, dQKq)
        dSn = [ds_ref[h] for h in H]
        for u in reversed(range(cb)):
            r, q_, k_, v_, beta, t, S, A, KK, QK, d_qE, vn, PtdO, qEdO, dQK, dQKk, dQKq = pre[u]
            gam, E, Fd, cd, kb = ([t[h][n] for h in H] for n in ("gam", "E", "F", "cd", "kb"))
            ktdS = [_nn(t[h]["kt"], dSn[h]) for h in H]
            dU = [_apply_a(at_ref[u, h], PtdO[h] + ktdS[h]) for h in H]
            d_kt = [_nt(vn[h], dSn[h]) for h in H]
            dUvn = [_nt(dU[h], vn[h]) for h in H]
            dUS = [_nt(dU[h], S[h]) for h in H]
            WdU = [_tn(t[h]["W"], dU[h]) for h in H]
            d_cd = [jnp.sum(S[h] * dSn[h]) for h in H]
            dSn = [cd[h] * dSn[h] + qEdO[h] - WdU[h] for h in H]
            dKK = [jnp.where(ii > jj, -dUvn[h], 0.0) * gam[h] for h in H]
            dKKk = [_nn(dKK[h], k_[h]) for h in H]
            dKKkb = [_tn(dKK[h], kb[h]) for h in H]
            dbeta_arr = jnp.zeros((CHUNK, 128), F32)
            dgc_arr = jnp.zeros((CHUNK, 128), F32)
            for h in H:
                dW = -dUS[h]
                dq_ref[r, sl[h]] = dQKk[h] + d_qE[h] * E[h]
                d_kb = dKKk[h] + dW * E[h]
                dk_ref[r, sl[h]] = dQKq[h] + dKKkb[h] + d_kb * beta[h] + d_kt[h] * Fd[h]
                dv_ref[r, sl[h]] = dU[h] * beta[h]
                Z = dQK[h] * QK[h] + dKK[h] * KK[h]
                dbeta = jnp.sum(dU[h] * v_[h] + d_kb * k_[h], axis=-1, keepdims=True)
                m_e = (dW * kb[h] + d_qE[h] * q_[h]) * E[h]
                m_f = d_kt[h] * k_[h] * Fd[h]
                zdiag = jnp.where(ii == jj, jnp.sum(Z, axis=0, keepdims=True), 0.0)
                dgc = (jnp.sum(m_e - m_f, axis=-1, keepdims=True) + jnp.sum(Z - zdiag, axis=-1, keepdims=True)
                       + jnp.where(rowi == CHUNK - 1, jnp.sum(m_f) + d_cd[h] * cd[h], 0.0))
                dbeta_arr = dbeta_arr + jnp.where(lane == h, dbeta, 0.0)
                dgc_arr = dgc_arr + jnp.where(lane == N_HEADS + h, dgc, 0.0)
            dbg_ref[r, :] = dbeta_arr + jnp.dot(rev, dgc_arr, precision=lax.Precision.HIGHEST,
                                                preferred_element_type=F32)
        for h in H:
            ds_ref[h] = dSn[h]

    row512 = pl.BlockSpec((cb * CHUNK, DN_WIDTH), lambda n: (nb - 1 - n, 0))
    row128 = pl.BlockSpec((cb * CHUNK, 128), lambda n: (nb - 1 - n, 0))
    in_specs, args = _with_dep(
        [row512, row512, row512, row128, row128,
         pl.BlockSpec((cb, 8, CHUNK), lambda n: (nb - 1 - n, 0, 0)),
         pl.BlockSpec((cb, N_HEADS, CHUNK, CHUNK), lambda n: (nb - 1 - n, 0, 0, 0)),
         pl.BlockSpec((cb, N_HEADS, CHUNK, CHUNK), lambda n: (nb - 1 - n, 0, 0, 0)),
         pl.BlockSpec((cb, N_HEADS, HEAD_DIM, HEAD_DIM), lambda n: (nb - 1 - n, 0, 0, 0)), row512],
        [q, k, v, bg, gc, gct, a, a_t, sall, do], dep)
    return pl.pallas_call(
        body, name="dn_scan_bwd", grid=(nb,), in_specs=in_specs,
        out_specs=[row512, row512, row512, row128],
        out_shape=[jax.ShapeDtypeStruct((T, DN_WIDTH), F32)] * 3 + [jax.ShapeDtypeStruct((T, 128), F32)],
        scratch_shapes=[pltpu.VMEM((N_HEADS, HEAD_DIM, HEAD_DIM), F32)],
        compiler_params=_cp("arbitrary"))(*args)


def _sg_mask():
    ii = lax.broadcasted_iota(jnp.int32, (SG_BLOCK, SG_BLOCK), 0) // CHUNK
    jj = lax.broadcasted_iota(jnp.int32, (SG_BLOCK, SG_BLOCK), 1) // CHUNK
    return jj <= ii


def _mix_fwd(o, p, ong, sgn, sgw, sgbt):
    T = o.shape[0]
    rb = SG_BLOCK

    def body(o_ref, gate_ref, u_ref, vg_ref, ong_ref, sgn_ref, sgw_ref, sgbt_ref, mix_ref):
        mask = _sg_mask()
        gate = gate_ref[...]
        for h in range(N_HEADS):
            sl = slice(h * HEAD_DIM, (h + 1) * HEAD_DIM)
            oh = o_ref[:, sl]
            r = lax.rsqrt(jnp.mean(oh * oh, axis=-1, keepdims=True) + EPS)
            mix_ref[:, sl] = (oh * r * ong_ref[...] * _silu(gate[:, sl])).astype(BF16)
        for gi in range(SG_GROUPS):
            sl = slice(gi * SG_BLOCK, (gi + 1) * SG_BLOCK)
            gv = _gelu(vg_ref[:, sl])
            r = lax.rsqrt(jnp.mean(gv * gv, axis=-1, keepdims=True) + EPS)
            vh = gv * r * sgn_ref[:, sl]
            s = _nn(jnp.where(mask, sgw_ref[gi], 0.0), vh) + sgbt_ref[:, gi:gi + 1]
            mix_ref[:, DN_WIDTH + gi * SG_BLOCK:DN_WIDTH + (gi + 1) * SG_BLOCK] = (_gelu(u_ref[:, sl]) * s).astype(BF16)

    def col(c):
        return pl.BlockSpec((rb, 512), lambda i: (i, c))
    return pl.pallas_call(
        body, name="mix_fwd", grid=(T // rb,),
        in_specs=[pl.BlockSpec((rb, DN_WIDTH), lambda i: (i, 0)), col(3), col(4), col(5),
                  pl.BlockSpec((1, 128), lambda i: (0, 0)), pl.BlockSpec((1, SG_WIDTH), lambda i: (0, 0)),
                  pl.BlockSpec((SG_GROUPS, SG_BLOCK, SG_BLOCK), lambda i: (0, 0, 0)),
                  pl.BlockSpec((SG_BLOCK, 128), lambda i: (0, 0))],
        out_specs=pl.BlockSpec((rb, D_MODEL), lambda i: (i, 0)),
        out_shape=jax.ShapeDtypeStruct((T, D_MODEL), BF16),
        compiler_params=_cp("parallel"))(o, p, p, p, ong, sgn, sgw, sgbt)


def _mix_bwd(o, p, ong, sgn, sgw, sgbt, dmix, dep=None):
    T = o.shape[0]
    rb = SG_BLOCK

    def body(o_ref, gate_ref, u_ref, vg_ref, ong_ref, sgn_ref, sgw_ref, sgbt_ref, dmix_ref, *rest):
        do_ref, dp_ref, gong_ref, gsgn_ref, gsgw_ref, gsgbt_ref = rest[-6:]
        @pl.when(pl.program_id(0) == 0)
        def _():
            gong_ref[...] = jnp.zeros_like(gong_ref)
            gsgn_ref[...] = jnp.zeros_like(gsgn_ref)
            gsgw_ref[...] = jnp.zeros_like(gsgw_ref)
            gsgbt_ref[...] = jnp.zeros_like(gsgbt_ref)
        mask = _sg_mask()
        gate = gate_ref[...]
        lane = _lane_iota((SG_BLOCK, 128))
        for h in range(N_HEADS):
            sl = slice(h * HEAD_DIM, (h + 1) * HEAD_DIM)
            oh = o_ref[:, sl]
            dm = dmix_ref[:, sl]
            r = lax.rsqrt(jnp.mean(oh * oh, axis=-1, keepdims=True) + EPS)
            oh_hat = oh * r
            gt = gate[:, sl]
            sg = _silu(gt)
            dp_ref[:, sl] = (dm * oh_hat * ong_ref[...] * _dsilu(gt)).astype(BF16)
            dn_ = dm * sg
            gong_ref[...] += jnp.sum(dn_ * oh_hat, axis=0, keepdims=True)
            dhat = dn_ * ong_ref[...]
            do_ref[:, sl] = r * (dhat - oh_hat * jnp.mean(dhat * oh_hat, axis=-1, keepdims=True))
        for gi in range(SG_GROUPS):
            sl = slice(gi * SG_BLOCK, (gi + 1) * SG_BLOCK)
            vraw = vg_ref[:, sl]
            gv = _gelu(vraw)
            r = lax.rsqrt(jnp.mean(gv * gv, axis=-1, keepdims=True) + EPS)
            vhat = gv * r
            vn = vhat * sgn_ref[:, sl]
            wm = jnp.where(mask, sgw_ref[gi], 0.0)
            s = _nn(wm, vn) + sgbt_ref[:, gi:gi + 1]
            uraw = u_ref[:, sl]
            dm = dmix_ref[:, DN_WIDTH + gi * SG_BLOCK:DN_WIDTH + (gi + 1) * SG_BLOCK]
            dp_ref[:, DN_WIDTH + gi * SG_BLOCK:DN_WIDTH + (gi + 1) * SG_BLOCK] = (dm * s * _dgelu(uraw)).astype(BF16)
            ds = dm * _gelu(uraw)
            gsgbt_ref[...] += jnp.where(lane == gi, jnp.sum(ds, axis=-1, keepdims=True), 0.0)
            gsgw_ref[gi] += jnp.where(mask, _nt(ds, vn), 0.0)
            dvn = _tn(wm, ds)
            gsgn_ref[:, sl] += jnp.sum(dvn * vhat, axis=0, keepdims=True)
            dhat = dvn * sgn_ref[:, sl]
            dgv = r * (dhat - vhat * jnp.mean(dhat * vhat, axis=-1, keepdims=True))
            dp_ref[:, 2 * DN_WIDTH + gi * SG_BLOCK:2 * DN_WIDTH + (gi + 1) * SG_BLOCK] = (dgv * _dgelu(vraw)).astype(BF16)

    def col(c):
        return pl.BlockSpec((rb, 512), lambda i: (i, c))
    full = lambda *s: pl.BlockSpec(s, lambda i: (0,) * len(s))
    in_specs, args = _with_dep(
        [pl.BlockSpec((rb, DN_WIDTH), lambda i: (i, 0)), col(3), col(4), col(5),
         full(1, 128), full(1, SG_WIDTH), full(SG_GROUPS, SG_BLOCK, SG_BLOCK), full(SG_BLOCK, 128),
         pl.BlockSpec((rb, D_MODEL), lambda i: (i, 0))],
        [o, p, p, p, ong, sgn, sgw, sgbt, dmix], dep)
    return pl.pallas_call(
        body, name="mix_bwd", grid=(T // rb,), in_specs=in_specs,
        out_specs=[pl.BlockSpec((rb, DN_WIDTH), lambda i: (i, 0)), pl.BlockSpec((rb, 3 * 512), lambda i: (i, 0)),
                   full(1, 128), full(1, SG_WIDTH), full(SG_GROUPS, SG_BLOCK, SG_BLOCK), full(SG_BLOCK, 128)],
        out_shape=[jax.ShapeDtypeStruct((T, DN_WIDTH), F32), jax.ShapeDtypeStruct((T, 3 * 512), BF16),
                   jax.ShapeDtypeStruct((1, 128), F32), jax.ShapeDtypeStruct((1, SG_WIDTH), F32),
                   jax.ShapeDtypeStruct((SG_GROUPS, SG_BLOCK, SG_BLOCK), F32),
                   jax.ShapeDtypeStruct((SG_BLOCK, 128), F32)],
        compiler_params=_cp("arbitrary"))(*args)


def _pad_lanes(row, offset=0):
    n = row.shape[1]
    return jnp.pad(row, ((0, 0), (offset, 128 - n - offset)))


def _local_step(x, tgt, w, dep=None, late_weights=None, on_grad=None):
    T = x.shape[0]
    N = T // CHUNK
    on_grad = on_grad or (lambda name, g: None)
    alog_row = _pad_lanes(w["dn_a_log"], N_HEADS)
    dtb_row = _pad_lanes(w["dn_dt_bias"], N_HEADS)
    sgbt = jnp.pad(w["sg_b"].T, ((0, 0), (0, 128 - SG_GROUPS)))

    h1 = _rms_fwd("rms_attn", x, w["attn_norm_g"], dep=dep)
    p, w_in_pad = _in_proj(h1, w["w_in"])
    q, k, v, bg = _dn_act(p, w["dn_conv_w"], alog_row, dtb_row)
    gc, gct, lmat = _dn_chunk(k, bg)
    lt = lmat.reshape(N * N_HEADS, CHUNK * CHUNK).T
    at = _tri_inv(lt)
    a = at.reshape(CHUNK * CHUNK, N * N_HEADS).T.reshape(N, N_HEADS, CHUNK, CHUNK)
    a_t = at.transpose(1, 0, 2).reshape(CHUNK * CHUNK, N * N_HEADS).T.reshape(N, N_HEADS, CHUNK, CHUNK)
    o, sall = _dn_scan(q, k, v, bg, gc, gct, a)
    mix = _mix_fwd(o, p, w["dn_out_norm_g"], w["sg_norm_g"], w["sg_w"], sgbt)
    if late_weights is not None:
        w = {**w, **late_weights(mix)}
    x2 = _mm_nn("out_proj", mix, w["w_out"], F32, 512, 1024, res=x)
    h2 = _rms_fwd("rms_ffn", x2, w["ffn_norm_g"])
    up = _mm_nn("up_proj", h2, w["w_up"], F32, 512, D_FF)
    act = _ffn_act(up, w["ffn_conv_w"], w["ffn_conv_b"])
    x3 = _mm_nn("down_proj", act, w["w_down"], F32, 512, 1024, res=x2)
    loss, dx3, g_final = _loss_head(x3, tgt, w["final_norm_g"])

    dact = _mm_nt("d_act", dx3, w["w_down"], F32, 512, D_FF)
    g_w_down = _mm_tn("g_w_down", act, dx3, D_FF, 1024, 512)
    tok = on_grad("w_down", g_w_down)
    dup, g_ffn_conv_w, g_ffn_conv_b = _ffn_act_bwd(up, dact, w["ffn_conv_w"], w["ffn_conv_b"], dep=tok)
    g_w_up = _mm_tn("g_w_up", h2, dup, 1024, 2 * D_FF // 4, 512, col_major_tiles=True)
    tok = on_grad("w_up", g_w_up)
    dh2 = _mm_nt("d_h2", dup, w["w_up"], F32, 512, 1024, dep=tok)
    dx2, g_ffn_norm = _rms_bwd("rms_ffn_bwd", dh2, x2, w["ffn_norm_g"], dx3)
    dmix = _mm_nt("d_mix", dx2, w["w_out"], F32, 512, 1024)
    g_w_out = _mm_tn("g_w_out", mix, dx2, 1024, 1024, 1024)
    tok = on_grad("w_out", g_w_out)
    do, dp_mid, g_ong, g_sgn, g_sgw, g_sgbt = _mix_bwd(o, p, w["dn_out_norm_g"], w["sg_norm_g"], w["sg_w"], sgbt,
                                                      dmix, dep=tok)
    early = dict(dn_out_norm_g=g_ong, sg_norm_g=g_sgn, sg_w=g_sgw, sg_b=g_sgbt[:, :SG_GROUPS].T,
                 ffn_norm_g=g_ffn_norm, ffn_conv_w=g_ffn_conv_w, ffn_conv_b=g_ffn_conv_b, final_norm_g=g_final)
    tok = on_grad("small_early", early)
    dq, dk, dv, dbg = _dn_scan_bwd(q, k, v, bg, gc, gct, a, a_t, sall, do, dep=tok)
    dp, g_dn_conv_w, g_ad = _dn_act_bwd(p, w["dn_conv_w"], alog_row, dtb_row, dq, dk, dv, dbg, dp_mid)
    g_w_in = _mm_tn("g_w_in", h1, dp, 1024, PROJ_PAD, 512, col_groups=(4, PROJ_COLS // 4))
    tok = on_grad("w_in", g_w_in)
    dh1 = _mm_nt("d_h1", dp, w_in_pad, F32, 512, 1024, dep=tok)
    grad_x, g_attn_norm = _rms_bwd("rms_attn_bwd", dh1, x, w["attn_norm_g"], dx2)

    grads = dict(
        attn_norm_g=g_attn_norm, w_in=g_w_in, dn_conv_w=g_dn_conv_w,
        dn_a_log=g_ad[0:1, N_HEADS:2 * N_HEADS], dn_dt_bias=g_ad[1:2, N_HEADS:2 * N_HEADS],
        w_out=g_w_out, w_up=g_w_up, w_down=g_w_down, **early)
    return loss, grad_x, grads


def _me():
    return lax.axis_index("x"), lax.axis_index("y"), lax.axis_index("c")


def _peer(rel):
    x, y, c = _me()
    return {"x": (1 - x, y, c), "y": (x, 1 - y, c), "xy": (1 - x, 1 - y, c), "c": (x, y, 1 - c)}[rel]


def _chip_of(dev):
    return 2 * dev[0] + dev[1]


CHIP_RELS = ("x", "y", "xy")


def _run_copies(copies, sends, recvs):
    for cp in copies:
        cp.start()
    for cp in recvs:
        cp.wait_recv()
    for cp in sends:
        cp.wait_send()


def _gather_first(w_shard, small_shard):
    R = w_shard.shape[0]
    r2 = R // 2

    def body(w_ref, s_ref, w_out, s_out, send_sems, recv_sems):
        x, y, c = _me()
        me = _chip_of((x, y))
        sib = _peer("c")

        def half(chip, core):
            return w_out.at[chip, pl.ds(pl.multiple_of(core * r2, 8), r2), :]

        def copy(k, src, dst, to):
            return pltpu.make_async_remote_copy(src_ref=src, dst_ref=dst, send_sem=send_sems.at[k],
                                                recv_sem=recv_sems.at[k], device_id=to, device_id_type=MESH)

        own_rows = w_ref.at[pl.ds(pl.multiple_of(c * r2, 8), r2), :]
        first = [copy(r, own_rows, half(me, c), _peer(rel)) for r, rel in enumerate(CHIP_RELS)]
        first += [copy(3 + r, s_ref, s_out.at[me], _peer(rel)) for r, rel in enumerate(CHIP_RELS)]
        for cp in first:
            cp.start()
        passed = []
        for r, rel in enumerate(CHIP_RELS):
            their = _chip_of(_peer(rel))
            copy(r, own_rows, half(their, c), _peer(rel)).wait_recv()
            fwd = copy(6 + r, half(their, c), half(their, c), sib)
            fwd.start()
            passed.append(fwd)
        for r, rel in enumerate(CHIP_RELS):
            their = _chip_of(_peer(rel))
            copy(3 + r, s_ref, s_out.at[their], _peer(rel)).wait_recv()
            copy(6 + r, own_rows, half(their, 1 - c), sib).wait_recv()
        for cp in first + passed:
            cp.wait_send()

    w_all, s_all = pl.pallas_call(
        body, name="gather_first", in_specs=[ANY, ANY], out_specs=[ANY, ANY],
        out_shape=[jax.ShapeDtypeStruct((4,) + w_shard.shape, w_shard.dtype),
                   jax.ShapeDtypeStruct((4,) + small_shard.shape, small_shard.dtype)],
        scratch_shapes=[pltpu.SemaphoreType.DMA((9,)), pltpu.SemaphoreType.DMA((9,))])(w_shard, small_shard)
    me = _chip_of(_me())
    return (lax.dynamic_update_index_in_dim(w_all, w_shard, me, 0),
            lax.dynamic_update_index_in_dim(s_all, small_shard, me, 0))


OTHERS = tuple((fx, fy, fc) for fx in (0, 1) for fy in (0, 1) for fc in (0, 1) if (fx, fy, fc) != (0, 0, 0))


def _other(flip):
    x, y, c = _me()
    return (x ^ flip[0], y ^ flip[1], c ^ flip[2])


def _linear(dev):
    return 4 * dev[0] + 2 * dev[1] + dev[2]


def _exchange_small(small):
    def body(small_ref, out_ref, send_sems, recv_sems):
        my_slot = _linear(_me())
        sends, recvs = [], []
        for k, flip in enumerate(OTHERS):
            peer = _other(flip)
            sends.append(pltpu.make_async_remote_copy(
                src_ref=small_ref, dst_ref=out_ref.at[my_slot], send_sem=send_sems.at[k], recv_sem=recv_sems.at[k],
                device_id=peer, device_id_type=MESH))
            recvs.append(pltpu.make_async_remote_copy(
                src_ref=small_ref, dst_ref=out_ref.at[_linear(peer)], send_sem=send_sems.at[k],
                recv_sem=recv_sems.at[k], device_id=peer, device_id_type=MESH))
        _run_copies(sends, sends, recvs)

    out = pl.pallas_call(
        body, name="exchange_small", in_specs=[ANY], out_specs=ANY,
        out_shape=jax.ShapeDtypeStruct((8,) + small.shape, small.dtype),
        scratch_shapes=[pltpu.SemaphoreType.DMA((7,)), pltpu.SemaphoreType.DMA((7,))])(small)
    return lax.dynamic_update_index_in_dim(out, small, _linear(_me()), 0)


def _pair_swap(halves):
    n = len(halves)

    def body(*refs):
        src, out = refs[:n], refs[n:2 * n]
        send_sems, recv_sems = refs[2 * n:]
        sib = _peer("c")
        copies = [pltpu.make_async_remote_copy(
            src_ref=src[i], dst_ref=out[i], send_sem=send_sems.at[i], recv_sem=recv_sems.at[i],
            device_id=sib, device_id_type=MESH) for i in range(n)]
        _run_copies(copies, copies, copies)

    return pl.pallas_call(
        body, name="pair_swap", in_specs=[ANY] * n, out_specs=[ANY] * n,
        out_shape=[jax.ShapeDtypeStruct(h.shape, h.dtype) for h in halves],
        scratch_shapes=[pltpu.SemaphoreType.DMA((n,)), pltpu.SemaphoreType.DMA((n,))])(*halves)


HBM = pl.BlockSpec(memory_space=pltpu.HBM)
SEM = pl.BlockSpec(memory_space=pltpu.SEMAPHORE)
EFFECT = pltpu.SideEffectType.DATAFLOW_SIDE_EFFECTING


def _hbm(a):
    return pltpu.with_memory_space_constraint(a, pltpu.HBM)


def _transfer_start(name, srcs, lands, n_copies, make_copies, after=None):
    n, m = len(srcs), len(lands)

    def body(*refs):
        src, land = refs[:n], refs[n:n + m]
        outs = refs[n + m + (after is not None):]
        send_sems, recv_sems, token = outs[0], outs[1], outs[-1]
        for cp in make_copies(src, land, send_sems, recv_sems):
            cp.start()
        token[...] = jnp.zeros_like(token)

    arrs = list(srcs) + list(lands)
    in_specs, args = _with_dep([HBM] * (n + m), [_hbm(a) for a in arrs], after)
    out = pl.pallas_call(
        body, name=name,
        out_shape=(pltpu.SemaphoreType.DMA((n_copies,)), pltpu.SemaphoreType.DMA((n_copies,)),
                   *[pltpu.HBM(a.shape, a.dtype) for a in arrs], jax.ShapeDtypeStruct((8, 128), F32)),
        in_specs=in_specs,
        out_specs=(SEM, SEM, *[HBM] * (n + m), pl.BlockSpec(memory_space=pltpu.VMEM)),
        input_output_aliases={i: 2 + i for i in range(n + m)},
        compiler_params=pltpu.CompilerParams(has_side_effects=EFFECT))(*args)
    return out[0], out[1], list(out[2:2 + n]), list(out[2 + n:2 + n + m]), out[-1]


def _transfer_wait(name, send_sems, recv_sems, srcs, lands, make_copies, after):
    n, m = len(srcs), len(lands)

    def body(*refs):
        src, land = refs[:n], refs[n:n + m]
        s_sems, r_sems = refs[n + m], refs[n + m + 1]
        for cp in make_copies(src, land, s_sems, r_sems):
            cp.wait_send()
            cp.wait_recv()

    arrs = list(srcs) + list(lands)
    out = pl.pallas_call(
        body, name=name, out_shape=tuple(pltpu.HBM(a.shape, a.dtype) for a in arrs),
        in_specs=[HBM] * (n + m) + [SEM, SEM, ANY], out_specs=tuple([HBM] * (n + m)),
        input_output_aliases={i: i for i in range(n + m)},
        compiler_params=pltpu.CompilerParams(has_side_effects=EFFECT))(*arrs, send_sems, recv_sems, after)
    return list(out[:n]), list(out[n:])


def _gather_copies(src, land, send_sems, recv_sems):
    me = _chip_of(_me())
    copies = []
    for i in range(len(src)):
        for r, rel in enumerate(CHIP_RELS):
            k = 3 * i + r
            copies.append(pltpu.make_async_remote_copy(
                src_ref=src[i], dst_ref=land[i].at[me], send_sem=send_sems.at[k], recv_sem=recv_sems.at[k],
                device_id=_peer(rel), device_id_type=MESH))
    return copies


def _small_copies(src, land, send_sems, recv_sems):
    my_slot = _linear(_me())
    return [pltpu.make_async_remote_copy(
        src_ref=src[0], dst_ref=land[0].at[my_slot], send_sem=send_sems.at[k], recv_sem=recv_sems.at[k],
        device_id=_other(flip), device_id_type=MESH) for k, flip in enumerate(OTHERS)]


def _pieces_copies(src, land, send_sems, recv_sems):
    copies = []
    for k, flip in enumerate(OTHERS):
        peer = _other(flip)
        copies.append(pltpu.make_async_remote_copy(
            src_ref=src[0].at[_linear(peer)], dst_ref=land[0].at[k], send_sem=send_sems.at[k],
            recv_sem=recv_sems.at[k], device_id=peer, device_id_type=MESH))
    return copies


def _row_block(rows, cols, budget=2 * 1024 * 1024):
    rb = max(8, (budget // (4 * cols)) // 8 * 8)
    while rows % rb:
        rb -= 8
    return rb if rb > 0 else rows


def _sum_slots(name, first, rest):
    R, Cc = first.shape
    K = rest.shape[0]
    rb = _row_block(R, Cc)

    def body(f_ref, r_ref, o_ref):
        acc = f_ref[...].astype(F32)
        for j in range(K):
            acc = acc + r_ref[j].astype(F32)
        o_ref[...] = acc

    return pl.pallas_call(
        body, name=name, grid=(R // rb,),
        in_specs=[pl.BlockSpec((rb, Cc), lambda i: (i, 0)), pl.BlockSpec((K, rb, Cc), lambda i: (0, i, 0))],
        out_specs=pl.BlockSpec((rb, Cc), lambda i: (i, 0)),
        out_shape=jax.ShapeDtypeStruct((R, Cc), F32), compiler_params=_cp("parallel"))(first, rest)


def _adamw_math(w, gv, m, v):
    mn = ADAM_B1 * m + (1.0 - ADAM_B1) * gv
    vn = ADAM_B2 * v + (1.0 - ADAM_B2) * (gv * gv)
    m_hat = mn / (1.0 - ADAM_B1 ** ADAM_STEP)
    v_hat = vn / (1.0 - ADAM_B2 ** ADAM_STEP)
    return -ADAM_LR * (m_hat / (jnp.sqrt(v_hat) + ADAM_EPS) + ADAM_WD * w), mn, vn


def _adamw_halves(name, w, mine, theirs, m, v, core):
    R, Cc = w.shape
    r2 = R // 2
    rb = _row_block(r2, Cc, 1024 * 1024)
    nb2 = r2 // rb

    def body(c_ref, w_ref, mine_ref, theirs_ref, m_ref, v_ref, g_ref, d_ref, mo_ref, vo_ref):
        is_mine = (pl.program_id(0) // nb2) == c_ref[0]
        gv = jnp.where(is_mine, mine_ref[...], theirs_ref[...])
        g_ref[...] = gv
        d_ref[...], mo_ref[...], vo_ref[...] = _adamw_math(w_ref[...], gv, m_ref[...], v_ref[...])

    blk = pl.BlockSpec((rb, Cc), lambda i, c: (i, 0))
    half = lambda own: pl.BlockSpec(
        (rb, Cc), lambda i, c: (jnp.clip(i - (c[0] if own else 1 - c[0]) * nb2, 0, nb2 - 1), 0))
    return pl.pallas_call(
        body, name=name,
        grid_spec=pltpu.PrefetchScalarGridSpec(
            num_scalar_prefetch=1, grid=(2 * nb2,), in_specs=[blk, half(True), half(False), blk, blk],
            out_specs=[blk] * 4),
        out_shape=[jax.ShapeDtypeStruct((R, Cc), F32)] * 4, compiler_params=_cp("parallel"))(core, w, mine, theirs, m, v)


def _adamw(name, w, g, m, v):
    R, Cc = w.shape
    rb = _row_block(R, Cc, 1024 * 1024)

    def body(w_ref, g_ref, m_ref, v_ref, d_ref, mo_ref, vo_ref):
        d_ref[...], mo_ref[...], vo_ref[...] = _adamw_math(w_ref[...], g_ref[...], m_ref[...], v_ref[...])

    blk = pl.BlockSpec((rb, Cc), lambda i: (i, 0))
    return pl.pallas_call(
        body, name=name, grid=(R // rb,), in_specs=[blk] * 4, out_specs=[blk] * 3,
        out_shape=[jax.ShapeDtypeStruct((R, Cc), F32)] * 3, compiler_params=_cp("parallel"))(w, g, m, v)


def _pack(arrs):
    rows = []
    for a in arrs:
        flat = a.reshape(-1)
        pad = (-flat.shape[0]) % 128
        rows.append(jnp.pad(flat, (0, pad)).reshape(-1, 128))
    buf = jnp.concatenate(rows, axis=0)
    return jnp.pad(buf, ((0, (-buf.shape[0]) % 8), (0, 0)))


def _unpack(buf, shapes):
    out, r = [], 0
    for s in shapes:
        n = math.prod(s)
        nr = -(-n // 128)
        out.append(buf[r:r + nr].reshape(-1)[:n].reshape(s))
        r += nr
    return out


BIG = ("w_in", "w_out", "w_up", "w_down")
CONV = ("dn_conv_w", "ffn_conv_w")
REPL = ("attn_norm_g", "dn_a_log", "dn_dt_bias", "dn_out_norm_g", "sg_norm_g", "sg_w", "sg_b",
        "ffn_norm_g", "ffn_conv_b", "final_norm_g")
ORDER = ("attn_norm_g", "w_in", "dn_conv_w", "dn_a_log", "dn_dt_bias", "dn_out_norm_g", "sg_norm_g", "sg_w",
         "sg_b", "w_out", "ffn_norm_g", "w_up", "ffn_conv_w", "ffn_conv_b", "w_down", "final_norm_g")


def kernel(x, attn_norm_g, w_in, dn_conv_w, dn_a_log, dn_dt_bias, dn_out_norm_g, sg_norm_g, sg_w, sg_b, w_out, ffn_norm_g, w_up, ffn_conv_w, ffn_conv_b, w_down, final_norm_g, loss_target, m_attn_norm_g, m_w_in, m_dn_conv_w, m_dn_a_log, m_dn_dt_bias, m_dn_out_norm_g, m_sg_norm_g, m_sg_w, m_sg_b, m_w_out, m_ffn_norm_g, m_w_up, m_ffn_conv_w, m_ffn_conv_b, m_w_down, m_final_norm_g, v_attn_norm_g, v_w_in, v_dn_conv_w, v_dn_a_log, v_dn_dt_bias, v_dn_out_norm_g, v_sg_norm_g, v_sg_w, v_sg_b, v_w_out, v_ffn_norm_g, v_w_up, v_ffn_conv_w, v_ffn_conv_b, v_w_down, v_final_norm_g):
    W = dict(attn_norm_g=attn_norm_g, w_in=w_in, dn_conv_w=dn_conv_w, dn_a_log=dn_a_log, dn_dt_bias=dn_dt_bias,
             dn_out_norm_g=dn_out_norm_g, sg_norm_g=sg_norm_g, sg_w=sg_w, sg_b=sg_b, w_out=w_out,
             ffn_norm_g=ffn_norm_g, w_up=w_up, ffn_conv_w=ffn_conv_w, ffn_conv_b=ffn_conv_b, w_down=w_down,
             final_norm_g=final_norm_g)
    Mo = dict(attn_norm_g=m_attn_norm_g, w_in=m_w_in, dn_conv_w=m_dn_conv_w, dn_a_log=m_dn_a_log,
              dn_dt_bias=m_dn_dt_bias, dn_out_norm_g=m_dn_out_norm_g, sg_norm_g=m_sg_norm_g, sg_w=m_sg_w,
              sg_b=m_sg_b, w_out=m_w_out, ffn_norm_g=m_ffn_norm_g, w_up=m_w_up, ffn_conv_w=m_ffn_conv_w,
              ffn_conv_b=m_ffn_conv_b, w_down=m_w_down, final_norm_g=m_final_norm_g)
    Vo = dict(attn_norm_g=v_attn_norm_g, w_in=v_w_in, dn_conv_w=v_dn_conv_w, dn_a_log=v_dn_a_log,
              dn_dt_bias=v_dn_dt_bias, dn_out_norm_g=v_dn_out_norm_g, sg_norm_g=v_sg_norm_g, sg_w=v_sg_w,
              sg_b=v_sg_b, w_out=v_w_out, ffn_norm_g=v_ffn_norm_g, w_up=v_w_up, ffn_conv_w=v_ffn_conv_w,
              ffn_conv_b=v_ffn_conv_b, w_down=v_w_down, final_norm_g=v_final_norm_g)
    xi, yi, ci = lax.axis_index("x"), lax.axis_index("y"), lax.axis_index("c")
    chip = 2 * xi + yi

    me_lin = 4 * xi + 2 * yi + ci

    g_in, g_dnc = _gather_first(w_in[0].astype(BF16), dn_conv_w[0])
    late = ("w_out", "w_up", "w_down", "ffn_conv_w")
    late_shards = [W[n][0].astype(BF16) for n in late[:3]] + [ffn_conv_w[0]]
    late_lands = [lax.dynamic_update_index_in_dim(lax.empty((4,) + s.shape, s.dtype), s, chip, 0) for s in late_shards]
    n_late = 3 * len(late_shards)
    ssem, rsem, late_src, late_lands, token = _transfer_start("gather_rest_start", late_shards, late_lands,
                                                              n_late, _gather_copies, after=g_in)

    def late_weights(after):
        _, (g_out, g_up, g_down, g_ffc) = _transfer_wait("gather_rest_wait", ssem, rsem, late_src, late_lands,
                                                         _gather_copies, after)
        return dict(w_out=g_out.reshape(D_MODEL, D_MODEL), w_up=g_up.transpose(1, 0, 2).reshape(D_MODEL, 2 * D_FF),
                    w_down=g_down.reshape(D_FF, D_MODEL), ffn_conv_w=g_ffc.transpose(1, 0, 2).reshape(3, 2 * D_FF))

    full = dict(
        w_in=g_in,
        dn_conv_w=g_dnc.transpose(1, 0, 2).reshape(4, 3 * DN_WIDTH),
        attn_norm_g=attn_norm_g, dn_a_log=dn_a_log, dn_dt_bias=dn_dt_bias, dn_out_norm_g=dn_out_norm_g,
        sg_norm_g=sg_norm_g, sg_w=sg_w[0], sg_b=sg_b[0], ffn_norm_g=ffn_norm_g, ffn_conv_b=ffn_conv_b,
        final_norm_g=final_norm_g[None])

    pending = {}
    early_names = ("dn_out_norm_g", "sg_norm_g", "sg_w", "sg_b", "ffn_norm_g", "ffn_conv_w", "ffn_conv_b",
                   "final_norm_g")
    late_names = ("attn_norm_g", "dn_a_log", "dn_dt_bias", "dn_conv_w")

    def on_grad(name, gw):
        if name == "small_early":
            buf = _pack([gw[n] for n in early_names])
            land = lax.dynamic_update_index_in_dim(lax.empty((8,) + buf.shape, F32), buf, me_lin, 0)
            s_sem, r_sem, src, lands, tok = _transfer_start("small_early_start", [buf], [land], 7, _small_copies)
            pending[name] = (s_sem, r_sem, src, lands)
            return tok
        g8 = gw.reshape(8, -1, gw.shape[-1])
        land = lax.empty((7,) + g8.shape[1:], BF16)
        s_sem, r_sem, src, lands, tok = _transfer_start(f"reduce_{name}_start", [g8], [land], 7, _pieces_copies)
        pending[name] = (s_sem, r_sem, src, lands)
        return tok

    loss_row, grad_x, g = _local_step(x[0], loss_target[0], full, dep=token, late_weights=late_weights,
                                      on_grad=on_grad)

    small_names = REPL + CONV
    late_all = _exchange_small(_pack([g[n] for n in late_names] + [loss_row]))
    late_sum = _sum_slots("sum_small_late", late_all[0], late_all[1:])
    s_sem, r_sem, src, lands = pending["small_early"]
    _, (early_all,) = _transfer_wait("small_early_wait", s_sem, r_sem, src, lands, _small_copies, grad_x)
    early_sum = _sum_slots("sum_small_early", early_all[0], early_all[1:])
    *late_vals, loss_sum = _unpack(late_sum, [g[n].shape for n in late_names] + [loss_row.shape])
    loss = loss_sum[0, 0]
    sg = dict(zip(late_names, late_vals))
    sg.update(zip(early_names, _unpack(early_sum, [g[n].shape for n in early_names])))
    sg["dn_conv_w"] = lax.dynamic_slice_in_dim(sg["dn_conv_w"], chip * (3 * DN_WIDTH // 4), 3 * DN_WIDTH // 4, axis=1)
    sg["ffn_conv_w"] = lax.dynamic_slice_in_dim(sg["ffn_conv_w"], chip * (2 * D_FF // 4), 2 * D_FF // 4, axis=1)

    halves = []
    for n in ("w_down", "w_up", "w_out", "w_in"):
        s_sem, r_sem, src, lands = pending[n]
        sent, got = _transfer_wait(f"reduce_{n}_wait", s_sem, r_sem, src, lands, _pieces_copies, grad_x)
        own = lax.dynamic_index_in_dim(sent[0], me_lin, axis=0, keepdims=False)
        halves.append(_sum_slots(f"sum_{n}", own, got[0]))
    theirs = _pair_swap(halves)
    core = ci.astype(jnp.int32).reshape(1)
    grads, delta, new_m, new_v = {}, {}, {}, {}
    for n, mine_h, their_h in zip(("w_down", "w_up", "w_out", "w_in"), halves, theirs):
        shp = W[n].shape
        gr, d, mn, vn = _adamw_halves(f"adamw_{n}", W[n][0], mine_h, their_h, Mo[n][0], Vo[n][0], core)
        grads[n], delta[n], new_m[n], new_v[n] = gr.reshape(shp), d.reshape(shp), mn.reshape(shp), vn.reshape(shp)
    shapes = [W[n].shape for n in small_names]
    for n in small_names:
        grads[n] = sg[n].reshape(W[n].shape)
    d, mn, vn = _adamw("adamw_small", _pack([W[n] for n in small_names]), _pack([grads[n] for n in small_names]),
                       _pack([Mo[n] for n in small_names]), _pack([Vo[n] for n in small_names]))
    for dst, buf in ((delta, d), (new_m, mn), (new_v, vn)):
        dst.update(zip(small_names, _unpack(buf, shapes)))

    return (loss, grad_x[None], *[grads[n] for n in ORDER], *[delta[n] for n in ORDER],
            *[new_m[n] for n in ORDER], *[new_v[n] for n in ORDER])
```

```python
import functools
import math

import jax
import jax.numpy as jnp
from jax import lax
from jax.experimental import pallas as pl
from jax.experimental.pallas import tpu as pltpu

F32 = jnp.float32
BF16 = jnp.bfloat16

D_MODEL = 1024
CHUNK = 64
SCAN_CHUNKS = 2
HEAD_DIM = 128
N_HEADS = 4
DN_WIDTH = 512
SG_WIDTH = 512
SG_GROUPS = 4
SG_BLOCK = 128
D_FF = 2816
PROJ_COLS = 3080
PROJ_PAD = 3200
BA_COL = 3072
EPS = 1e-6
NEG = -1e30
VMEM_LIMIT = 56 * 1024 * 1024

ADAM_LR = 0.001
ADAM_B1 = 0.9
ADAM_B2 = 0.999
ADAM_EPS = 1e-08
ADAM_WD = 0.01
ADAM_STEP = 10

MESH = pl.DeviceIdType.MESH
ANY = pl.BlockSpec(memory_space=pl.ANY)


def _cp(*sem):
    return pltpu.CompilerParams(dimension_semantics=sem, vmem_limit_bytes=VMEM_LIMIT)


def _bf(a):
    return a.astype(BF16)


def _nn(a, b):
    return jnp.dot(_bf(a), _bf(b), preferred_element_type=F32)


def _nt(a, b):
    return lax.dot_general(_bf(a), _bf(b), (((1,), (1,)), ((), ())), preferred_element_type=F32)


def _tn(a, b):
    return lax.dot_general(_bf(a), _bf(b), (((0,), (0,)), ((), ())), preferred_element_type=F32)


def _split(a):
    hi = _bf(a)
    return hi, _bf(a - hi.astype(F32))


def _sigmoid(x):
    return 0.5 * jnp.tanh(0.5 * x) + 0.5


def _silu(x):
    return x * _sigmoid(x)


def _dsilu(x):
    s = _sigmoid(x)
    return s * (1.0 + x * (1.0 - s))


_GELU_C = math.sqrt(2.0 / math.pi)
_GELU_A = 0.044715


def _gelu(x):
    return 0.5 * x * (1.0 + jnp.tanh(_GELU_C * (x + _GELU_A * x * x * x)))


def _dgelu(x):
    t = jnp.tanh(_GELU_C * (x + _GELU_A * x * x * x))
    return 0.5 * (1.0 + t) + 0.5 * x * (1.0 - t * t) * _GELU_C * (1.0 + 3.0 * _GELU_A * x * x)


def _softplus(x):
    return jnp.maximum(x, 0.0) + jnp.log(1.0 + jnp.exp(-jnp.abs(x)))


def _mm_nn(name, a, b, out_dtype, tm, tn, res=None):
    M, K = a.shape
    N = b.shape[1]
    tm, tn = min(tm, M), min(tn, N)

    def body(*refs):
        a_ref, b_ref = refs[0], refs[1]
        o_ref = refs[-1]
        acc = _nn(a_ref[...], b_ref[...])
        if res is not None:
            acc = acc + refs[2][...]
        o_ref[...] = acc.astype(o_ref.dtype)

    in_specs = [pl.BlockSpec((tm, K), lambda j, i: (i, 0)), pl.BlockSpec((K, tn), lambda j, i: (0, j))]
    args = [a, b]
    if res is not None:
        in_specs.append(pl.BlockSpec((tm, tn), lambda j, i: (i, j)))
        args.append(res)
    return pl.pallas_call(
        body, name=name, grid=(N // tn, M // tm), in_specs=in_specs,
        out_specs=pl.BlockSpec((tm, tn), lambda j, i: (i, j)),
        out_shape=jax.ShapeDtypeStruct((M, N), out_dtype),
        compiler_params=_cp("parallel", "parallel"))(*args)


def _with_dep(in_specs, args, dep):
    if dep is None:
        return in_specs, args
    return in_specs + [ANY], args + [dep]


SUB_ROWS = 128


def _sub_blocks(tm):
    return [slice(r0, min(r0 + SUB_ROWS, tm)) for r0 in range(0, tm, SUB_ROWS)]


def _rms_hat(xv):
    r = lax.rsqrt(jnp.mean(xv * xv, axis=-1, keepdims=True) + EPS)
    return xv * r, r


def _rms_bwd_vals(dh, xh, r, g):
    dxh = dh * g
    return r * (dxh - xh * jnp.mean(dxh * xh, axis=-1, keepdims=True)), jnp.sum(dh * xh, axis=0, keepdims=True)


def _in_proj(x, g, w4, tm=512, dep=None):
    T, K = x.shape
    ng, _, wc = w4.shape
    tm = min(tm, T)

    def body(x_ref, g_ref, w4_ref, *rest):
        p_ref, h_ref, w_ref = rest[-3:]

        @pl.when(pl.program_id(0) == 0)
        def _():
            w_ref[:, ng * wc:] = jnp.zeros((K, PROJ_PAD - ng * wc), BF16)
            for j in range(ng):
                w_ref[:, j * wc:(j + 1) * wc] = w4_ref[j]
        for r in _sub_blocks(tm):
            xh, _ = _rms_hat(x_ref[r, :])
            h = (xh * g_ref[...]).astype(BF16)
            h_ref[r, :] = h
            p_ref[r, :] = jnp.dot(h, w_ref[...], preferred_element_type=F32)

    in_specs, args = _with_dep(
        [pl.BlockSpec((tm, K), lambda i: (i, 0)), pl.BlockSpec((1, K), lambda i: (0, 0)),
         pl.BlockSpec((ng, K, wc), lambda i: (0, 0, 0))], [x, g, w4], dep)
    return pl.pallas_call(
        body, name="in_proj", grid=(T // tm,), in_specs=in_specs,
        out_specs=[pl.BlockSpec((tm, PROJ_PAD), lambda i: (i, 0)), pl.BlockSpec((tm, K), lambda i: (i, 0)),
                   pl.BlockSpec((K, PROJ_PAD), lambda i: (0, 0))],
        out_shape=[jax.ShapeDtypeStruct((T, PROJ_PAD), F32), jax.ShapeDtypeStruct((T, K), BF16),
                   jax.ShapeDtypeStruct((K, PROJ_PAD), BF16)],
        compiler_params=_cp("arbitrary"))(*args)


def _out_proj(mix, w, x, g, tm=512):
    T, K = mix.shape
    Dm = w.shape[1]
    tm = min(tm, T)

    def body(a_ref, w_ref, x_ref, g_ref, x2_ref, h_ref):
        for r in _sub_blocks(tm):
            x2 = _nn(a_ref[r, :], w_ref[...]) + x_ref[r, :]
            x2_ref[r, :] = x2
            xh, _ = _rms_hat(x2)
            h_ref[r, :] = (xh * g_ref[...]).astype(BF16)

    row = lambda width: pl.BlockSpec((tm, width), lambda i: (i, 0))
    return pl.pallas_call(
        body, name="out_proj", grid=(T // tm,),
        in_specs=[row(K), pl.BlockSpec((K, Dm), lambda i: (0, 0)), row(Dm), pl.BlockSpec((1, Dm), lambda i: (0, 0))],
        out_specs=[row(Dm), row(Dm)],
        out_shape=[jax.ShapeDtypeStruct((T, Dm), F32), jax.ShapeDtypeStruct((T, Dm), BF16)],
        compiler_params=_cp("parallel"))(mix, w, x, g)


def _down_proj_loss(act, w, x2, tgt, g, tm=512):
    T, K = act.shape
    Dm = w.shape[1]
    tm = min(tm, T)

    def body(a_ref, w_ref, x_ref, t_ref, g_ref, loss_ref, dx_ref, gg_ref):
        @pl.when(pl.program_id(0) == 0)
        def _():
            gg_ref[...] = jnp.zeros_like(gg_ref)
            loss_ref[...] = jnp.zeros_like(loss_ref)
        for r in _sub_blocks(tm):
            xh, rr = _rms_hat(_nn(a_ref[r, :], w_ref[...]) + x_ref[r, :])
            e = xh * g_ref[...] - t_ref[r, :]
            loss_ref[...] += jnp.zeros_like(loss_ref) + (0.5 / Dm) * jnp.sum(e * e)
            dx, gg = _rms_bwd_vals(e * (1.0 / Dm), xh, rr, g_ref[...])
            dx_ref[r, :] = dx
            gg_ref[...] += gg

    row = lambda width: pl.BlockSpec((tm, width), lambda i: (i, 0))
    vec = pl.BlockSpec((1, Dm), lambda i: (0, 0))
    return pl.pallas_call(
        body, name="down_proj_loss", grid=(T // tm,),
        in_specs=[row(K), pl.BlockSpec((K, Dm), lambda i: (0, 0)), row(Dm), row(Dm), vec],
        out_specs=[pl.BlockSpec((1, 128), lambda i: (0, 0)), row(Dm), vec],
        out_shape=[jax.ShapeDtypeStruct((1, 128), F32), jax.ShapeDtypeStruct((T, Dm), F32),
                   jax.ShapeDtypeStruct((1, Dm), F32)],
        compiler_params=_cp("arbitrary"))(act, w, x2, tgt, g)


def _mm_nt_rms_bwd(name, a, b, x, g, dres, tm=512, dep=None):
    M, K = a.shape
    Dm = b.shape[0]
    tm = min(tm, M)

    def body(a_ref, b_ref, x_ref, g_ref, dres_ref, *rest):
        dx_ref, gg_ref = rest[-2:]

        @pl.when(pl.program_id(0) == 0)
        def _():
            gg_ref[...] = jnp.zeros_like(gg_ref)
        for r in _sub_blocks(tm):
            xh, rr = _rms_hat(x_ref[r, :])
            dx, gg = _rms_bwd_vals(_nt(a_ref[r, :], b_ref[...]), xh, rr, g_ref[...])
            dx_ref[r, :] = dres_ref[r, :] + dx
            gg_ref[...] += gg

    row = lambda width: pl.BlockSpec((tm, width), lambda i: (i, 0))
    vec = pl.BlockSpec((1, Dm), lambda i: (0, 0))
    in_specs, args = _with_dep([row(K), pl.BlockSpec((Dm, K), lambda i: (0, 0)), row(Dm), vec, row(Dm)],
                               [a, b, x, g, dres], dep)
    return pl.pallas_call(
        body, name=name, grid=(M // tm,), in_specs=in_specs, out_specs=[row(Dm), vec],
        out_shape=[jax.ShapeDtypeStruct((M, Dm), F32), jax.ShapeDtypeStruct((1, Dm), F32)],
        compiler_params=_cp("arbitrary"))(*args)


def _mm_nt(name, a, b, out_dtype, tm, tn, dep=None):
    M, K = a.shape
    N = b.shape[0]
    tm, tn = min(tm, M), min(tn, N)

    def body(a_ref, b_ref, *rest):
        o_ref = rest[-1]
        o_ref[...] = _nt(a_ref[...], b_ref[...]).astype(o_ref.dtype)

    in_specs, args = _with_dep(
        [pl.BlockSpec((tm, K), lambda i, j: (i, 0)), pl.BlockSpec((tn, K), lambda i, j: (j, 0))], [a, b], dep)
    return pl.pallas_call(
        body, name=name, grid=(M // tm, N // tn), in_specs=in_specs,
        out_specs=pl.BlockSpec((tm, tn), lambda i, j: (i, j)),
        out_shape=jax.ShapeDtypeStruct((M, N), out_dtype),
        compiler_params=_cp("parallel", "parallel"))(*args)


def _mm_tn(name, a, b, tm, tn, tk, col_major_tiles=False, col_groups=None):
    T, M = a.shape
    N = b.shape[1]
    tm, tn, tk = min(tm, M), min(tn, N), min(tk, T)
    nk = T // tk

    def body(a_ref, b_ref, o_ref, acc_ref):
        k = pl.program_id(2)

        @pl.when(k == 0)
        def _():
            acc_ref[...] = jnp.zeros_like(acc_ref)
        acc_ref[...] += _tn(a_ref[...], b_ref[...])

        @pl.when(k == nk - 1)
        def _():
            if col_groups:
                for j in range(col_groups[0]):
                    o_ref[j] = acc_ref[:, j * col_groups[1]:(j + 1) * col_groups[1]].astype(BF16)
            else:
                o_ref[...] = acc_ref[...].astype(BF16).reshape(o_ref.shape)

    if col_groups:
        assert tm == M and tn == N and col_groups[0] * col_groups[1] <= N
        out_spec = pl.BlockSpec((col_groups[0], M, col_groups[1]), lambda i, j, k: (0, 0, 0))
        out_shape = jax.ShapeDtypeStruct((col_groups[0], M, col_groups[1]), BF16)
    elif col_major_tiles:
        assert tm == M
        out_spec = pl.BlockSpec((1, tm, tn), lambda i, j, k: (j, 0, 0))
        out_shape = jax.ShapeDtypeStruct((N // tn, M, tn), BF16)
    else:
        out_spec = pl.BlockSpec((tm, tn), lambda i, j, k: (i, j))
        out_shape = jax.ShapeDtypeStruct((M, N), BF16)
    return pl.pallas_call(
        body, name=name, grid=(M // tm, N // tn, nk),
        in_specs=[pl.BlockSpec((tk, tm), lambda i, j, k: (k, i)), pl.BlockSpec((tk, tn), lambda i, j, k: (k, j))],
        out_specs=out_spec, out_shape=out_shape, scratch_shapes=[pltpu.VMEM((tm, tn), F32)],
        compiler_params=_cp("parallel", "parallel", "arbitrary"))(a, b)


def _halo_prev_spec(rb, width):
    return pl.BlockSpec((8, width), lambda i: (jnp.maximum(i * (rb // 8) - 1, 0), 0))


def _halo_next_spec(rb, width, T):
    return pl.BlockSpec((8, width), lambda i: (jnp.minimum((i + 1) * (rb // 8), T // 8 - 1), 0))


LANES = 128
FF_STRIPS = D_FF // LANES
ROW_CHUNK = 32


def _strip(j, base=0):
    return pl.ds(pl.multiple_of(base + j * LANES, LANES), LANES)


def _ffn_act(up, w, b, rb=256):
    T, W = up.shape
    rb = min(rb, T)

    def body(up_ref, halo_ref, w_ref, b_ref, act_ref, ext_scr):
        first = pl.program_id(0) == 0

        def strip(j, slot):
            halves = (_strip(j), _strip(j, D_FF))
            wv = [w_ref[:, cols] for cols in halves]
            bv = [b_ref[:, cols] for cols in halves]
            for h, cols in enumerate(halves):
                ext_scr[slot, h,0:8] = jnp.where(first, 0.0, halo_ref[:, cols])
                ext_scr[slot, h,8:] = up_ref[:, cols]
            for r0 in range(0, rb, ROW_CHUNK):
                n = min(ROW_CHUNK, rb - r0)
                c = [ext_scr[slot, h,6 + r0:6 + r0 + n] * wv[h][0:1] + ext_scr[slot, h,7 + r0:7 + r0 + n] * wv[h][1:2]
                     + ext_scr[slot, h,8 + r0:8 + r0 + n] * wv[h][2:3] + bv[h] for h in range(2)]
                act_ref[r0:r0 + n, halves[0]] = (_silu(c[0]) * c[1]).astype(BF16)

        def pair(jj, carry):
            strip(2 * jj, 0)
            strip(2 * jj + 1, 1)
            return carry

        lax.fori_loop(0, FF_STRIPS // 2, pair, 0)

    return pl.pallas_call(
        body, name="ffn_act", grid=(T // rb,),
        in_specs=[pl.BlockSpec((rb, W), lambda i: (i, 0)), _halo_prev_spec(rb, W),
                  pl.BlockSpec((3, W), lambda i: (0, 0)), pl.BlockSpec((1, W), lambda i: (0, 0))],
        out_specs=pl.BlockSpec((rb, D_FF), lambda i: (i, 0)),
        out_shape=jax.ShapeDtypeStruct((T, D_FF), BF16),
        scratch_shapes=[pltpu.VMEM((2, 2, rb + 8, LANES), F32)], compiler_params=_cp("parallel"))(up, up, w, b)


def _ffn_act_bwd(up, dact, w, b, rb=128, dep=None):
    T, W = up.shape
    rb = min(rb, T)
    nb = T // rb
    re = rb + 8

    def body(up_ref, prev_ref, next_ref, da_ref, danext_ref, w_ref, b_ref, *rest):
        dup_ref, gw_ref, gb_ref, ext_scr, dc_scr = rest[-5:]
        i = pl.program_id(0)

        @pl.when(i == 0)
        def _():
            gw_ref[...] = jnp.zeros_like(gw_ref)
            gb_ref[...] = jnp.zeros_like(gb_ref)
        last = i == nb - 1

        def fold8(a):
            return jnp.sum(a.reshape(a.shape[0] // 8, 8, LANES), axis=0)

        def strip(j, slot):
            halves = (_strip(j), _strip(j, D_FF))
            wv = [w_ref[:, cols] for cols in halves]
            bv = [b_ref[:, cols] for cols in halves]
            for h, cols in enumerate(halves):
                ext_scr[slot, h,0:8] = jnp.where(i > 0, prev_ref[:, cols], 0.0)
                ext_scr[slot, h,8:8 + rb] = up_ref[:, cols]
                ext_scr[slot, h,8 + rb:] = next_ref[:, cols]
            gb = [jnp.zeros((8, LANES), F32) for _ in range(2)]
            gw = [[jnp.zeros((8, LANES), F32) for _ in range(3)] for _ in range(2)]
            for r0 in range(0, re, ROW_CHUNK):
                n = min(ROW_CHUNK, re - r0)
                tp = [[ext_scr[slot, h,6 + k + r0:6 + k + r0 + n] for k in range(3)] for h in range(2)]
                c = [tp[h][0] * wv[h][0:1] + tp[h][1] * wv[h][1:2] + tp[h][2] * wv[h][2:3] + bv[h] for h in range(2)]
                if r0 < rb:
                    da = da_ref[r0:r0 + n, halves[0]]
                else:
                    da = jnp.where(last, 0.0, danext_ref[:, halves[0]])
                s = _sigmoid(c[0])
                gs = c[0] * s
                dcs = (da * c[1] * (s + gs * (1.0 - s)), da * gs)
                for h in range(2):
                    dc_scr[slot, h,r0:r0 + n] = dcs[h]
                    if r0 < rb:
                        gb[h] = gb[h] + fold8(dcs[h])
                        for k in range(3):
                            gw[h][k] = gw[h][k] + fold8(tp[h][k] * dcs[h])
            for r0 in range(0, rb, ROW_CHUNK):
                n = min(ROW_CHUNK, rb - r0)
                for h, cols in enumerate(halves):
                    dup = (dc_scr[slot, h,r0:r0 + n] * wv[h][2:3] + dc_scr[slot, h,r0 + 1:r0 + 1 + n] * wv[h][1:2]
                           + dc_scr[slot, h,r0 + 2:r0 + 2 + n] * wv[h][0:1])
                    dup_ref[r0:r0 + n, cols] = dup.astype(BF16)
            for h, cols in enumerate(halves):
                gb_ref[:, cols] += jnp.sum(gb[h], axis=0, keepdims=True)
                for k in range(3):
                    gw_ref[k:k + 1, cols] += jnp.sum(gw[h][k], axis=0, keepdims=True)

        def pair(jj, carry):
            strip(2 * jj, 0)
            strip(2 * jj + 1, 1)
            return carry

        lax.fori_loop(0, FF_STRIPS // 2, pair, 0)

    in_specs, args = _with_dep(
        [pl.BlockSpec((rb, W), lambda i: (i, 0)), _halo_prev_spec(rb, W), _halo_next_spec(rb, W, T),
         pl.BlockSpec((rb, D_FF), lambda i: (i, 0)), _halo_next_spec(rb, D_FF, T),
         pl.BlockSpec((3, W), lambda i: (0, 0)), pl.BlockSpec((1, W), lambda i: (0, 0))],
        [up, up, up, dact, dact, w, b], dep)
    return pl.pallas_call(
        body, name="ffn_act_bwd", grid=(nb,), in_specs=in_specs,
        out_specs=[pl.BlockSpec((rb, W), lambda i: (i, 0)), pl.BlockSpec((3, W), lambda i: (0, 0)),
                   pl.BlockSpec((1, W), lambda i: (0, 0))],
        out_shape=[jax.ShapeDtypeStruct((T, W), BF16), jax.ShapeDtypeStruct((3, W), F32),
                   jax.ShapeDtypeStruct((1, W), F32)],
        scratch_shapes=[pltpu.VMEM((2, 2, rb + 16, LANES), F32), pltpu.VMEM((2, 2, re, LANES), F32)],
        compiler_params=_cp("arbitrary"))(*args)


def _lane_iota(shape):
    return lax.broadcasted_iota(jnp.int32, shape, len(shape) - 1)


def _dn_act(p, conv_w, alog_row, dtb_row, rb=256):
    T = p.shape[0]
    rb = min(rb, T)
    W3 = 3 * DN_WIDTH

    def body(p_ref, halo_ref, ba_ref, w_ref, al_ref, dt_ref, q_ref, k_ref, v_ref, bg_ref, ext_scr):
        first = pl.program_id(0) == 0
        outs = (q_ref, k_ref, v_ref)
        for j in range(3 * N_HEADS):
            kind, h = divmod(j, N_HEADS)
            cols = slice(j * HEAD_DIM, (j + 1) * HEAD_DIM)
            cur = p_ref[:, cols]
            ext_scr[j, 0:8] = jnp.where(first, 0.0, halo_ref[:, cols])
            ext_scr[j, 8:] = cur
            wv = w_ref[:, cols]
            s = _silu(ext_scr[j, 5:5 + rb] * wv[0:1] + ext_scr[j, 6:6 + rb] * wv[1:2]
                      + ext_scr[j, 7:7 + rb] * wv[2:3] + cur * wv[3:4])
            if kind < 2:
                scale = HEAD_DIM ** -0.5 if kind == 0 else 1.0
                s = s * (lax.rsqrt(jnp.sum(s * s, axis=-1, keepdims=True) + EPS) * scale)
            outs[kind][:, h * HEAD_DIM:(h + 1) * HEAD_DIM] = s
        ba = ba_ref[...]
        lane = _lane_iota(ba.shape)
        beta = _sigmoid(ba)
        g = -jnp.exp(al_ref[...]) * _softplus(ba + dt_ref[...])
        bg_ref[...] = jnp.where(lane < N_HEADS, beta, jnp.where(lane < 2 * N_HEADS, g, 0.0))

    row512 = pl.BlockSpec((rb, DN_WIDTH), lambda i: (i, 0))
    row128 = pl.BlockSpec((rb, 128), lambda i: (i, 0))
    vec128 = pl.BlockSpec((1, 128), lambda i: (0, 0))
    return pl.pallas_call(
        body, name="dn_act", grid=(T // rb,),
        in_specs=[pl.BlockSpec((rb, W3), lambda i: (i, 0)), _halo_prev_spec(rb, W3),
                  pl.BlockSpec((rb, 128), lambda i: (i, BA_COL // 128)),
                  pl.BlockSpec((4, W3), lambda i: (0, 0)), vec128, vec128],
        out_specs=[row512, row512, row512, row128],
        out_shape=[jax.ShapeDtypeStruct((T, DN_WIDTH), F32)] * 3 + [jax.ShapeDtypeStruct((T, 128), F32)],
        scratch_shapes=[pltpu.VMEM((3 * N_HEADS, rb + 8, HEAD_DIM), F32)],
        compiler_params=_cp("parallel"))(p, p, p, conv_w, alog_row, dtb_row)


def _dn_act_bwd(p, conv_w, alog_row, dtb_row, dq, dk, dv, dbg, dp_mid, rb=256):
    T = p.shape[0]
    rb = min(rb, T)
    nb = T // rb
    re = rb + 8
    W3 = 3 * DN_WIDTH

    def body(p_ref, prev_ref, next_ref, ba_ref, w_ref, al_ref, dt_ref, dq_ref, dqn_ref, dk_ref, dkn_ref,
             dv_ref, dvn_ref, dbg_ref, mid_ref, draw_ref, gw_ref, gad_ref, ext_scr, dc_scr):
        i = pl.program_id(0)
        draw_ref[:, W3:2 * W3] = mid_ref[...]

        @pl.when(i == 0)
        def _():
            gw_ref[...] = jnp.zeros_like(gw_ref)
            gad_ref[...] = jnp.zeros_like(gad_ref)
        row = lax.broadcasted_iota(jnp.int32, (re, 1), 0)
        live = (row < rb) | (i < nb - 1)
        d_refs = ((dq_ref, dqn_ref), (dk_ref, dkn_ref), (dv_ref, dvn_ref))
        for j in range(3 * N_HEADS):
            kind, h = divmod(j, N_HEADS)
            cols = slice(j * HEAD_DIM, (j + 1) * HEAD_DIM)
            hcols = slice(h * HEAD_DIM, (h + 1) * HEAD_DIM)
            ext_scr[j, 0:8] = jnp.where(i > 0, prev_ref[:, cols], 0.0)
            ext_scr[j, 8:8 + rb] = p_ref[:, cols]
            ext_scr[j, 8 + rb:] = next_ref[:, cols]
            tp = [ext_scr[j, 5 + k:5 + k + re] for k in range(4)]
            wv = w_ref[:, cols]
            c = tp[0] * wv[0:1] + tp[1] * wv[1:2] + tp[2] * wv[2:3] + tp[3] * wv[3:4]
            sg = _sigmoid(c)
            s = c * sg
            d_in = jnp.where(live, jnp.concatenate([d_refs[kind][0][:, hcols], d_refs[kind][1][:, hcols]], axis=0), 0.0)
            if kind < 2:
                scale = HEAD_DIM ** -0.5 if kind == 0 else 1.0
                n = lax.rsqrt(jnp.sum(s * s, axis=-1, keepdims=True) + EPS)
                hat = s * n
                d_in = (n * scale) * (d_in - hat * jnp.sum(hat * d_in, axis=-1, keepdims=True))
            dc = d_in * (sg + s * (1.0 - sg))
            dc_scr[j] = dc
            dcc = dc[0:rb]
            draw = (dcc * wv[3:4] + dc_scr[j, 1:1 + rb] * wv[2:3] + dc_scr[j, 2:2 + rb] * wv[1:2]
                    + dc_scr[j, 3:3 + rb] * wv[0:1])
            draw_ref[:, cols] = draw.astype(BF16)
            for k in range(4):
                gw_ref[k:k + 1, cols] += jnp.sum(tp[k][0:rb] * dcc, axis=0, keepdims=True)
        ba = ba_ref[...]
        dbg = dbg_ref[...]
        lane = _lane_iota(ba.shape)
        beta = _sigmoid(ba)
        ea = jnp.exp(al_ref[...])
        z = ba + dt_ref[...]
        d_a = dbg * (-ea) * _sigmoid(z)
        dba = jnp.where(lane < N_HEADS, dbg * beta * (1.0 - beta), jnp.where(lane < 2 * N_HEADS, d_a, 0.0))
        draw_ref[:, BA_COL:] = dba.astype(BF16)
        isg = (lane >= N_HEADS) & (lane < 2 * N_HEADS)
        g = -ea * _softplus(z)
        gad_ref[0:1, :] += jnp.sum(jnp.where(isg, dbg * g, 0.0), axis=0, keepdims=True)
        gad_ref[1:2, :] += jnp.sum(jnp.where(isg, d_a, 0.0), axis=0, keepdims=True)

    row512 = pl.BlockSpec((rb, DN_WIDTH), lambda i: (i, 0))
    row128 = pl.BlockSpec((rb, 128), lambda i: (i, 0))
    vec128 = pl.BlockSpec((1, 128), lambda i: (0, 0))
    next512 = _halo_next_spec(rb, DN_WIDTH, T)
    return pl.pallas_call(
        body, name="dn_act_bwd", grid=(nb,),
        in_specs=[pl.BlockSpec((rb, W3), lambda i: (i, 0)), _halo_prev_spec(rb, W3), _halo_next_spec(rb, W3, T),
                  pl.BlockSpec((rb, 128), lambda i: (i, BA_COL // 128)),
                  pl.BlockSpec((4, W3), lambda i: (0, 0)), vec128, vec128,
                  row512, next512, row512, next512, row512, next512, row128,
                  pl.BlockSpec((rb, W3), lambda i: (i, 0))],
        out_specs=[pl.BlockSpec((rb, PROJ_PAD), lambda i: (i, 0)),
                   pl.BlockSpec((4, W3), lambda i: (0, 0)), pl.BlockSpec((2, 128), lambda i: (0, 0))],
        out_shape=[jax.ShapeDtypeStruct((T, PROJ_PAD), BF16),
                   jax.ShapeDtypeStruct((4, W3), F32), jax.ShapeDtypeStruct((2, 128), F32)],
        scratch_shapes=[pltpu.VMEM((3 * N_HEADS, rb + 16, HEAD_DIM), F32), pltpu.VMEM((3 * N_HEADS, re, HEAD_DIM), F32)],
        compiler_params=_cp("arbitrary"))(p, p, p, p, conv_w, alog_row, dtb_row, dq, dq, dk, dk, dv, dv, dbg, dp_mid)


def _tri(incl):
    ii = lax.broadcasted_iota(jnp.int32, (CHUNK, CHUNK), 0)
    jj = lax.broadcasted_iota(jnp.int32, (CHUNK, CHUNK), 1)
    return ii, jj, ((ii >= jj) if incl else (ii > jj))


def _dn_chunk(k, bg, cb=4):
    T = k.shape[0]
    N = T // CHUNK
    cb = min(cb, N)

    def body(k_ref, bg_ref, gc_ref, gct_ref, l_ref):
        ii, jj, incl = _tri(True)
        tri = incl.astype(F32)
        U = range(cb)
        bgv = [bg_ref[u * CHUNK:(u + 1) * CHUNK, :] for u in U]
        gc = [jnp.dot(tri, bgv[u], precision=lax.Precision.HIGHEST, preferred_element_type=F32) for u in U]
        gct = [gc[u].T for u in U]
        kk = [[None] * N_HEADS for _ in U]
        for u in U:
            gc_ref[u * CHUNK:(u + 1) * CHUNK, :] = gc[u]
            gct_ref[u] = gct[u][0:8]
            for h in range(N_HEADS):
                kh = k_ref[u * CHUNK:(u + 1) * CHUNK, h * HEAD_DIM:(h + 1) * HEAD_DIM]
                kk[u][h] = _nt(kh * bgv[u][:, h:h + 1], kh)
        for u in U:
            for h in range(N_HEADS):
                gcol = gc[u][:, N_HEADS + h:N_HEADS + h + 1]
                grow = gct[u][N_HEADS + h:N_HEADS + h + 1, :]
                l_ref[u, h] = kk[u][h] * jnp.exp(jnp.where(ii > jj, gcol - grow, NEG))

    rows = cb * CHUNK
    return pl.pallas_call(
        body, name="dn_chunk", grid=(N // cb,),
        in_specs=[pl.BlockSpec((rows, DN_WIDTH), lambda n: (n, 0)), pl.BlockSpec((rows, 128), lambda n: (n, 0))],
        out_specs=[pl.BlockSpec((rows, 128), lambda n: (n, 0)), pl.BlockSpec((cb, 8, CHUNK), lambda n: (n, 0, 0)),
                   pl.BlockSpec((cb, N_HEADS, CHUNK, CHUNK), lambda n: (n, 0, 0, 0))],
        out_shape=[jax.ShapeDtypeStruct((T, 128), F32), jax.ShapeDtypeStruct((N, 8, CHUNK), F32),
                   jax.ShapeDtypeStruct((N, N_HEADS, CHUNK, CHUNK), F32)],
        compiler_params=_cp("parallel"))(k, bg)


def _tri_inv(lt):
    S = lt.shape[1]

    def body(l_ref, a_ref):
        col = lax.broadcasted_iota(jnp.int32, (CHUNK, S), 0)
        for i in range(CHUNK):
            def step(j, acc):
                return acc - l_ref[pl.ds(i * CHUNK + j, 1), :] * a_ref[j]
            a_ref[i] = lax.fori_loop(0, i, step, (col == i).astype(F32))

    return pl.pallas_call(
        body, name="tri_inv", out_shape=jax.ShapeDtypeStruct((CHUNK, CHUNK, S), F32),
        compiler_params=pltpu.CompilerParams(vmem_limit_bytes=VMEM_LIMIT))(lt)


def _dn_head_terms(qh, kh, vh, beta, gcol, grow):
    ii, jj, incl = _tri(True)
    gam = jnp.exp(jnp.where(incl, gcol - grow, NEG))
    glast = grow[:, CHUNK - 1:CHUNK]
    cd = jnp.exp(glast)
    shape = (CHUNK, HEAD_DIM)
    E = jnp.broadcast_to(jnp.exp(gcol), shape)
    Fd = jnp.broadcast_to(jnp.exp(glast - gcol), shape)
    beta = jnp.broadcast_to(beta, shape)
    kb = kh * beta
    return dict(ii=ii, jj=jj, gam=gam, E=E, F=Fd, beta=beta, cd=cd, kb=kb, vb=vh * beta, W=kb * E, qE=qh * E,
                kt=kh * Fd)


def _apply_a(a, u):
    hi, lo = _split(a)
    ub = _bf(u)
    return jnp.dot(hi, ub, preferred_element_type=F32) + jnp.dot(lo, ub, preferred_element_type=F32)


def _dn_scan(q, k, v, bg, gc, gct, a):
    T = q.shape[0]
    N = T // CHUNK

    cb = min(SCAN_CHUNKS, N)

    def body(q_ref, k_ref, v_ref, bg_ref, gc_ref, gct_ref, a_ref, o_ref, sall_ref, s_ref):
        @pl.when(pl.program_id(0) == 0)
        def _():
            s_ref[...] = jnp.zeros_like(s_ref)
        H = range(N_HEADS)
        sl = [slice(h * HEAD_DIM, (h + 1) * HEAD_DIM) for h in H]
        pre = []
        for u in range(cb):
            r = slice(u * CHUNK, (u + 1) * CHUNK)
            bgv, gcv, gctv = bg_ref[r, :], gc_ref[r, :], gct_ref[u]
            q_, k_ = [q_ref[r, s] for s in sl], [k_ref[r, s] for s in sl]
            t = [_dn_head_terms(q_[h], k_[h], v_ref[r, sl[h]], bgv[:, h:h + 1],
                                gcv[:, N_HEADS + h:N_HEADS + h + 1], gctv[N_HEADS + h:N_HEADS + h + 1, :]) for h in H]
            P = [_nt(q_[h], k_[h]) * t[h]["gam"] for h in H]
            pre.append((r, t, P))
        S = [s_ref[h] for h in H]
        for u in range(cb):
            r, t, P = pre[u]
            for h in H:
                sall_ref[u, h] = S[h]
            WS = [_nn(t[h]["W"], S[h]) for h in H]
            qS = [_nn(t[h]["qE"], S[h]) for h in H]
            vn = [_apply_a(a_ref[u, h], t[h]["vb"] - WS[h]) for h in H]
            Pv = [_nn(P[h], vn[h]) for h in H]
            kv = [_tn(t[h]["kt"], vn[h]) for h in H]
            for h in H:
                o_ref[r, sl[h]] = qS[h] + Pv[h]
            S = [t[h]["cd"] * S[h] + kv[h] for h in H]
        for h in H:
            s_ref[h] = S[h]

    row512 = pl.BlockSpec((cb * CHUNK, DN_WIDTH), lambda n: (n, 0))
    row128 = pl.BlockSpec((cb * CHUNK, 128), lambda n: (n, 0))
    return pl.pallas_call(
        body, name="dn_scan", grid=(N // cb,),
        in_specs=[row512, row512, row512, row128, row128, pl.BlockSpec((cb, 8, CHUNK), lambda n: (n, 0, 0)),
                  pl.BlockSpec((cb, N_HEADS, CHUNK, CHUNK), lambda n: (n, 0, 0, 0))],
        out_specs=[row512, pl.BlockSpec((cb, N_HEADS, HEAD_DIM, HEAD_DIM), lambda n: (n, 0, 0, 0))],
        out_shape=[jax.ShapeDtypeStruct((T, DN_WIDTH), F32),
                   jax.ShapeDtypeStruct((N, N_HEADS, HEAD_DIM, HEAD_DIM), F32)],
        scratch_shapes=[pltpu.VMEM((N_HEADS, HEAD_DIM, HEAD_DIM), F32)],
        compiler_params=_cp("arbitrary"))(q, k, v, bg, gc, gct, a)


def _dn_scan_bwd(q, k, v, bg, gc, gct, a, a_t, sall, do, dep=None):
    T = q.shape[0]
    N = T // CHUNK

    cb = min(SCAN_CHUNKS, N)
    nb = N // cb

    def body(q_ref, k_ref, v_ref, bg_ref, gc_ref, gct_ref, a_ref, at_ref, sall_ref, do_ref, *rest):
        dq_ref, dk_ref, dv_ref, dbg_ref, ds_ref = rest[-5:]
        @pl.when(pl.program_id(0) == 0)
        def _():
            ds_ref[...] = jnp.zeros_like(ds_ref)
        lane = _lane_iota((CHUNK, 128))
        rowi = lax.broadcasted_iota(jnp.int32, (CHUNK, 1), 0)
        ii, jj, _ = _tri(True)
        rev = (jj >= ii).astype(F32)
        H = range(N_HEADS)
        sl = [slice(h * HEAD_DIM, (h + 1) * HEAD_DIM) for h in H]
        pre = {}
        for u in reversed(range(cb)):
            r = slice(u * CHUNK, (u + 1) * CHUNK)
            bgv, gcv, gctv = bg_ref[r, :], gc_ref[r, :], gct_ref[u]
            q_, k_, v_ = [q_ref[r, s] for s in sl], [k_ref[r, s] for s in sl], [v_ref[r, s] for s in sl]
            dO = [do_ref[r, s] for s in sl]
            t = [_dn_head_terms(q_[h], k_[h], v_[h], bgv[:, h:h + 1], gcv[:, N_HEADS + h:N_HEADS + h + 1],
                                gctv[N_HEADS + h:N_HEADS + h + 1, :]) for h in H]
            beta = [t[h]["beta"] for h in H]
            S = [sall_ref[u, h] for h in H]
            A = [a_ref[u, h] for h in H]
            WS = [_nn(t[h]["W"], S[h]) for h in H]
            KK = [_nt(t[h]["kb"], k_[h]) for h in H]
            QK = [_nt(q_[h], k_[h]) for h in H]
            d_qE = [_nt(dO[h], S[h]) for h in H]
            vn = [_apply_a(A[h], t[h]["vb"] - WS[h]) for h in H]
            PtdO = [_tn(QK[h] * t[h]["gam"], dO[h]) for h in H]
            qEdO = [_tn(t[h]["qE"], dO[h]) for h in H]
            dOvn = [_nt(dO[h], vn[h]) for h in H]
            dQK = [jnp.where(ii >= jj, dOvn[h], 0.0) * t[h]["gam"] for h in H]
            dQKk = [_nn(dQK[h], k_[h]) for h in H]
            dQKq = [_tn(dQK[h], q_[h]) for h in H]
            pre[u] = (r, q_, k_, v_, beta, t, S, A, KK, QK, d_qE, vn, PtdO, qEdO, dQK, dQKk, dQKq)
        dSn = [ds_ref[h] for h in H]
        for u in reversed(range(cb)):
            r, q_, k_, v_, beta, t, S, A, KK, QK, d_qE, vn, PtdO, qEdO, dQK, dQKk, dQKq = pre[u]
            gam, E, Fd, cd, kb = ([t[h][n] for h in H] for n in ("gam", "E", "F", "cd", "kb"))
            ktdS = [_nn(t[h]["kt"], dSn[h]) for h in H]
            dU = [_apply_a(at_ref[u, h], PtdO[h] + ktdS[h]) for h in H]
            d_kt = [_nt(vn[h], dSn[h]) for h in H]
            dUvn = [_nt(dU[h], vn[h]) for h in H]
            dUS = [_nt(dU[h], S[h]) for h in H]
            WdU = [_tn(t[h]["W"], dU[h]) for h in H]
            d_cd = [jnp.sum(S[h] * dSn[h]) for h in H]
            dSn = [cd[h] * dSn[h] + qEdO[h] - WdU[h] for h in H]
            dKK = [jnp.where(ii > jj, -dUvn[h], 0.0) * gam[h] for h in H]
            dKKk = [_nn(dKK[h], k_[h]) for h in H]
            dKKkb = [_tn(dKK[h], kb[h]) for h in H]
            dbeta_arr = jnp.zeros((CHUNK, 128), F32)
            dgc_arr = jnp.zeros((CHUNK, 128), F32)
            for h in H:
                dW = -dUS[h]
                dq_ref[r, sl[h]] = dQKk[h] + d_qE[h] * E[h]
                d_kb = dKKk[h] + dW * E[h]
                dk_ref[r, sl[h]] = dQKq[h] + dKKkb[h] + d_kb * beta[h] + d_kt[h] * Fd[h]
                dv_ref[r, sl[h]] = dU[h] * beta[h]
                Z = dQK[h] * QK[h] + dKK[h] * KK[h]
                dbeta = jnp.sum(dU[h] * v_[h] + d_kb * k_[h], axis=-1, keepdims=True)
                m_e = (dW * kb[h] + d_qE[h] * q_[h]) * E[h]
                m_f = d_kt[h] * k_[h] * Fd[h]
                zdiag = jnp.where(ii == jj, jnp.sum(Z, axis=0, keepdims=True), 0.0)
                dgc = (jnp.sum(m_e - m_f, axis=-1, keepdims=True) + jnp.sum(Z - zdiag, axis=-1, keepdims=True)
                       + jnp.where(rowi == CHUNK - 1, jnp.sum(m_f) + d_cd[h] * cd[h], 0.0))
                dbeta_arr = dbeta_arr + jnp.where(lane == h, dbeta, 0.0)
                dgc_arr = dgc_arr + jnp.where(lane == N_HEADS + h, dgc, 0.0)
            dbg_ref[r, :] = dbeta_arr + jnp.dot(rev, dgc_arr, precision=lax.Precision.HIGHEST,
                                                preferred_element_type=F32)
        for h in H:
            ds_ref[h] = dSn[h]

    row512 = pl.BlockSpec((cb * CHUNK, DN_WIDTH), lambda n: (nb - 1 - n, 0))
    row128 = pl.BlockSpec((cb * CHUNK, 128), lambda n: (nb - 1 - n, 0))
    in_specs, args = _with_dep(
        [row512, row512, row512, row128, row128,
         pl.BlockSpec((cb, 8, CHUNK), lambda n: (nb - 1 - n, 0, 0)),
         pl.BlockSpec((cb, N_HEADS, CHUNK, CHUNK), lambda n: (nb - 1 - n, 0, 0, 0)),
         pl.BlockSpec((cb, N_HEADS, CHUNK, CHUNK), lambda n: (nb - 1 - n, 0, 0, 0)),
         pl.BlockSpec((cb, N_HEADS, HEAD_DIM, HEAD_DIM), lambda n: (nb - 1 - n, 0, 0, 0)), row512],
        [q, k, v, bg, gc, gct, a, a_t, sall, do], dep)
    return pl.pallas_call(
        body, name="dn_scan_bwd", grid=(nb,), in_specs=in_specs,
        out_specs=[row512, row512, row512, row128],
        out_shape=[jax.ShapeDtypeStruct((T, DN_WIDTH), F32)] * 3 + [jax.ShapeDtypeStruct((T, 128), F32)],
        scratch_shapes=[pltpu.VMEM((N_HEADS, HEAD_DIM, HEAD_DIM), F32)],
        compiler_params=_cp("arbitrary"))(*args)


def _sg_mask():
    ii = lax.broadcasted_iota(jnp.int32, (SG_BLOCK, SG_BLOCK), 0) // CHUNK
    jj = lax.broadcasted_iota(jnp.int32, (SG_BLOCK, SG_BLOCK), 1) // CHUNK
    return jj <= ii


def _mix_fwd(o, p, ong, sgn, sgw, sgbt):
    T = o.shape[0]
    rb = SG_BLOCK

    def body(o_ref, gate_ref, u_ref, vg_ref, ong_ref, sgn_ref, sgw_ref, sgbt_ref, mix_ref):
        mask = _sg_mask()
        gate = gate_ref[...]
        for h in range(N_HEADS):
            sl = slice(h * HEAD_DIM, (h + 1) * HEAD_DIM)
            oh = o_ref[:, sl]
            r = lax.rsqrt(jnp.mean(oh * oh, axis=-1, keepdims=True) + EPS)
            mix_ref[:, sl] = (oh * r * ong_ref[...] * _silu(gate[:, sl])).astype(BF16)
        for gi in range(SG_GROUPS):
            sl = slice(gi * SG_BLOCK, (gi + 1) * SG_BLOCK)
            gv = _gelu(vg_ref[:, sl])
            r = lax.rsqrt(jnp.mean(gv * gv, axis=-1, keepdims=True) + EPS)
            vh = gv * r * sgn_ref[:, sl]
            s = _nn(jnp.where(mask, sgw_ref[gi], 0.0), vh) + sgbt_ref[:, gi:gi + 1]
            mix_ref[:, DN_WIDTH + gi * SG_BLOCK:DN_WIDTH + (gi + 1) * SG_BLOCK] = (_gelu(u_ref[:, sl]) * s).astype(BF16)

    def col(c):
        return pl.BlockSpec((rb, 512), lambda i: (i, c))
    return pl.pallas_call(
        body, name="mix_fwd", grid=(T // rb,),
        in_specs=[pl.BlockSpec((rb, DN_WIDTH), lambda i: (i, 0)), col(3), col(4), col(5),
                  pl.BlockSpec((1, 128), lambda i: (0, 0)), pl.BlockSpec((1, SG_WIDTH), lambda i: (0, 0)),
                  pl.BlockSpec((SG_GROUPS, SG_BLOCK, SG_BLOCK), lambda i: (0, 0, 0)),
                  pl.BlockSpec((SG_BLOCK, 128), lambda i: (0, 0))],
        out_specs=pl.BlockSpec((rb, D_MODEL), lambda i: (i, 0)),
        out_shape=jax.ShapeDtypeStruct((T, D_MODEL), BF16),
        compiler_params=_cp("parallel"))(o, p, p, p, ong, sgn, sgw, sgbt)


def _mix_bwd(o, p, ong, sgn, sgw, sgbt, dmix, dep=None):
    T = o.shape[0]
    rb = SG_BLOCK

    def body(o_ref, gate_ref, u_ref, vg_ref, ong_ref, sgn_ref, sgw_ref, sgbt_ref, dmix_ref, *rest):
        do_ref, dp_ref, gong_ref, gsgn_ref, gsgw_ref, gsgbt_ref = rest[-6:]
        @pl.when(pl.program_id(0) == 0)
        def _():
            gong_ref[...] = jnp.zeros_like(gong_ref)
            gsgn_ref[...] = jnp.zeros_like(gsgn_ref)
            gsgw_ref[...] = jnp.zeros_like(gsgw_ref)
            gsgbt_ref[...] = jnp.zeros_like(gsgbt_ref)
        mask = _sg_mask()
        gate = gate_ref[...]
        lane = _lane_iota((SG_BLOCK, 128))
        for h in range(N_HEADS):
            sl = slice(h * HEAD_DIM, (h + 1) * HEAD_DIM)
            oh = o_ref[:, sl]
            dm = dmix_ref[:, sl]
            r = lax.rsqrt(jnp.mean(oh * oh, axis=-1, keepdims=True) + EPS)
            oh_hat = oh * r
            gt = gate[:, sl]
            sg = _silu(gt)
            dp_ref[:, sl] = (dm * oh_hat * ong_ref[...] * _dsilu(gt)).astype(BF16)
            dn_ = dm * sg
            gong_ref[...] += jnp.sum(dn_ * oh_hat, axis=0, keepdims=True)
            dhat = dn_ * ong_ref[...]
            do_ref[:, sl] = r * (dhat - oh_hat * jnp.mean(dhat * oh_hat, axis=-1, keepdims=True))
        for gi in range(SG_GROUPS):
            sl = slice(gi * SG_BLOCK, (gi + 1) * SG_BLOCK)
            vraw = vg_ref[:, sl]
            gv = _gelu(vraw)
            r = lax.rsqrt(jnp.mean(gv * gv, axis=-1, keepdims=True) + EPS)
            vhat = gv * r
            vn = vhat * sgn_ref[:, sl]
            wm = jnp.where(mask, sgw_ref[gi], 0.0)
            s = _nn(wm, vn) + sgbt_ref[:, gi:gi + 1]
            uraw = u_ref[:, sl]
            dm = dmix_ref[:, DN_WIDTH + gi * SG_BLOCK:DN_WIDTH + (gi + 1) * SG_BLOCK]
            dp_ref[:, DN_WIDTH + gi * SG_BLOCK:DN_WIDTH + (gi + 1) * SG_BLOCK] = (dm * s * _dgelu(uraw)).astype(BF16)
            ds = dm * _gelu(uraw)
            gsgbt_ref[...] += jnp.where(lane == gi, jnp.sum(ds, axis=-1, keepdims=True), 0.0)
            gsgw_ref[gi] += jnp.where(mask, _nt(ds, vn), 0.0)
            dvn = _tn(wm, ds)
            gsgn_ref[:, sl] += jnp.sum(dvn * vhat, axis=0, keepdims=True)
            dhat = dvn * sgn_ref[:, sl]
            dgv = r * (dhat - vhat * jnp.mean(dhat * vhat, axis=-1, keepdims=True))
            dp_ref[:, 2 * DN_WIDTH + gi * SG_BLOCK:2 * DN_WIDTH + (gi + 1) * SG_BLOCK] = (dgv * _dgelu(vraw)).astype(BF16)

    def col(c):
        return pl.BlockSpec((rb, 512), lambda i: (i, c))
    full = lambda *s: pl.BlockSpec(s, lambda i: (0,) * len(s))
    in_specs, args = _with_dep(
        [pl.BlockSpec((rb, DN_WIDTH), lambda i: (i, 0)), col(3), col(4), col(5),
         full(1, 128), full(1, SG_WIDTH), full(SG_GROUPS, SG_BLOCK, SG_BLOCK), full(SG_BLOCK, 128),
         pl.BlockSpec((rb, D_MODEL), lambda i: (i, 0))],
        [o, p, p, p, ong, sgn, sgw, sgbt, dmix], dep)
    return pl.pallas_call(
        body, name="mix_bwd", grid=(T // rb,), in_specs=in_specs,
        out_specs=[pl.BlockSpec((rb, DN_WIDTH), lambda i: (i, 0)), pl.BlockSpec((rb, 3 * 512), lambda i: (i, 0)),
                   full(1, 128), full(1, SG_WIDTH), full(SG_GROUPS, SG_BLOCK, SG_BLOCK), full(SG_BLOCK, 128)],
        out_shape=[jax.ShapeDtypeStruct((T, DN_WIDTH), F32), jax.ShapeDtypeStruct((T, 3 * 512), BF16),
                   jax.ShapeDtypeStruct((1, 128), F32), jax.ShapeDtypeStruct((1, SG_WIDTH), F32),
                   jax.ShapeDtypeStruct((SG_GROUPS, SG_BLOCK, SG_BLOCK), F32),
                   jax.ShapeDtypeStruct((SG_BLOCK, 128), F32)],
        compiler_params=_cp("arbitrary"))(*args)


def _pad_lanes(row, offset=0):
    n = row.shape[1]
    return jnp.pad(row, ((0, 0), (offset, 128 - n - offset)))


def _local_step(x, tgt, w, dep=None, late_weights=None, on_grad=None):
    T = x.shape[0]
    N = T // CHUNK
    on_grad = on_grad or (lambda name, g: None)
    alog_row = _pad_lanes(w["dn_a_log"], N_HEADS)
    dtb_row = _pad_lanes(w["dn_dt_bias"], N_HEADS)
    sgbt = jnp.pad(w["sg_b"].T, ((0, 0), (0, 128 - SG_GROUPS)))

    p, h1, w_in_pad = _in_proj(x, w["attn_norm_g"], w["w_in"], dep=dep)
    q, k, v, bg = _dn_act(p, w["dn_conv_w"], alog_row, dtb_row)
    gc, gct, lmat = _dn_chunk(k, bg)
    lt = lmat.reshape(N * N_HEADS, CHUNK * CHUNK).T
    at = _tri_inv(lt)
    a = at.reshape(CHUNK * CHUNK, N * N_HEADS).T.reshape(N, N_HEADS, CHUNK, CHUNK)
    a_t = at.transpose(1, 0, 2).reshape(CHUNK * CHUNK, N * N_HEADS).T.reshape(N, N_HEADS, CHUNK, CHUNK)
    o, sall = _dn_scan(q, k, v, bg, gc, gct, a)
    mix = _mix_fwd(o, p, w["dn_out_norm_g"], w["sg_norm_g"], w["sg_w"], sgbt)
    if late_weights is not None:
        w = {**w, **late_weights(mix)}
    x2, h2 = _out_proj(mix, w["w_out"], x, w["ffn_norm_g"])
    up = _mm_nn("up_proj", h2, w["w_up"], F32, 512, D_FF)
    act = _ffn_act(up, w["ffn_conv_w"], w["ffn_conv_b"])
    loss, dx3, g_final = _down_proj_loss(act, w["w_down"], x2, tgt, w["final_norm_g"])

    dact = _mm_nt("d_act", dx3, w["w_down"], F32, 512, D_FF)
    g_w_down = _mm_tn("g_w_down", act, dx3, D_FF, 1024, 512)
    tok = on_grad("w_down", g_w_down)
    dup, g_ffn_conv_w, g_ffn_conv_b = _ffn_act_bwd(up, dact, w["ffn_conv_w"], w["ffn_conv_b"], dep=tok)
    g_w_up = _mm_tn("g_w_up", h2, dup, 1024, 2 * D_FF // 4, 512, col_major_tiles=True)
    tok = on_grad("w_up", g_w_up)
    dx2, g_ffn_norm = _mm_nt_rms_bwd("d_h2", dup, w["w_up"], x2, w["ffn_norm_g"], dx3, dep=tok)
    dmix = _mm_nt("d_mix", dx2, w["w_out"], F32, 512, 1024)
    g_w_out = _mm_tn("g_w_out", mix, dx2, 1024, 1024, 1024)
    tok = on_grad("w_out", g_w_out)
    do, dp_mid, g_ong, g_sgn, g_sgw, g_sgbt = _mix_bwd(o, p, w["dn_out_norm_g"], w["sg_norm_g"], w["sg_w"], sgbt,
                                                      dmix, dep=tok)
    early = dict(dn_out_norm_g=g_ong, sg_norm_g=g_sgn, sg_w=g_sgw, sg_b=g_sgbt[:, :SG_GROUPS].T,
                 ffn_norm_g=g_ffn_norm, ffn_conv_w=g_ffn_conv_w, ffn_conv_b=g_ffn_conv_b, final_norm_g=g_final)
    tok = on_grad("small_early", early)
    dq, dk, dv, dbg = _dn_scan_bwd(q, k, v, bg, gc, gct, a, a_t, sall, do, dep=tok)
    dp, g_dn_conv_w, g_ad = _dn_act_bwd(p, w["dn_conv_w"], alog_row, dtb_row, dq, dk, dv, dbg, dp_mid)
    g_w_in = _mm_tn("g_w_in", h1, dp, 1024, PROJ_PAD, 512, col_groups=(4, PROJ_COLS // 4))
    tok = on_grad("w_in", g_w_in)
    grad_x, g_attn_norm = _mm_nt_rms_bwd("d_h1", dp, w_in_pad, x, w["attn_norm_g"], dx2, dep=tok)

    grads = dict(
        attn_norm_g=g_attn_norm, w_in=g_w_in, dn_conv_w=g_dn_conv_w,
        dn_a_log=g_ad[0:1, N_HEADS:2 * N_HEADS], dn_dt_bias=g_ad[1:2, N_HEADS:2 * N_HEADS],
        w_out=g_w_out, w_up=g_w_up, w_down=g_w_down, **early)
    return loss, grad_x, grads


def _me():
    return lax.axis_index("x"), lax.axis_index("y"), lax.axis_index("c")


def _peer(rel):
    x, y, c = _me()
    return {"x": (1 - x, y, c), "y": (x, 1 - y, c), "xy": (1 - x, 1 - y, c), "c": (x, y, 1 - c)}[rel]


def _chip_of(dev):
    return 2 * dev[0] + dev[1]


CHIP_RELS = ("x", "y", "xy")


def _run_copies(copies, sends, recvs):
    for cp in copies:
        cp.start()
    for cp in recvs:
        cp.wait_recv()
    for cp in sends:
        cp.wait_send()


def _gather_first(w_shard, small_shard):
    R = w_shard.shape[0]
    r2 = R // 2

    def body(w_ref, s_ref, w_out, s_out, send_sems, recv_sems):
        x, y, c = _me()
        me = _chip_of((x, y))
        sib = _peer("c")

        def half(chip, core):
            return w_out.at[chip, pl.ds(pl.multiple_of(core * r2, 8), r2), :]

        def copy(k, src, dst, to):
            return pltpu.make_async_remote_copy(src_ref=src, dst_ref=dst, send_sem=send_sems.at[k],
                                                recv_sem=recv_sems.at[k], device_id=to, device_id_type=MESH)

        own_rows = w_ref.at[pl.ds(pl.multiple_of(c * r2, 8), r2), :]
        first = [copy(r, own_rows, half(me, c), _peer(rel)) for r, rel in enumerate(CHIP_RELS)]
        first += [copy(3 + r, s_ref, s_out.at[me], _peer(rel)) for r, rel in enumerate(CHIP_RELS)]
        for cp in first:
            cp.start()
        passed = []
        for r, rel in enumerate(CHIP_RELS):
            their = _chip_of(_peer(rel))
            copy(r, own_rows, half(their, c), _peer(rel)).wait_recv()
            fwd = copy(6 + r, half(their, c), half(their, c), sib)
            fwd.start()
            passed.append(fwd)
        for r, rel in enumerate(CHIP_RELS):
            their = _chip_of(_peer(rel))
            copy(3 + r, s_ref, s_out.at[their], _peer(rel)).wait_recv()
            copy(6 + r, own_rows, half(their, 1 - c), sib).wait_recv()
        for cp in first + passed:
            cp.wait_send()

    w_all, s_all = pl.pallas_call(
        body, name="gather_first", in_specs=[ANY, ANY], out_specs=[ANY, ANY],
        out_shape=[jax.ShapeDtypeStruct((4,) + w_shard.shape, w_shard.dtype),
                   jax.ShapeDtypeStruct((4,) + small_shard.shape, small_shard.dtype)],
        scratch_shapes=[pltpu.SemaphoreType.DMA((9,)), pltpu.SemaphoreType.DMA((9,))])(w_shard, small_shard)
    me = _chip_of(_me())
    return (lax.dynamic_update_index_in_dim(w_all, w_shard, me, 0),
            lax.dynamic_update_index_in_dim(s_all, small_shard, me, 0))


OTHERS = tuple((fx, fy, fc) for fx in (0, 1) for fy in (0, 1) for fc in (0, 1) if (fx, fy, fc) != (0, 0, 0))


def _other(flip):
    x, y, c = _me()
    return (x ^ flip[0], y ^ flip[1], c ^ flip[2])


def _linear(dev):
    return 4 * dev[0] + 2 * dev[1] + dev[2]


def _exchange_small(small):
    def body(small_ref, out_ref, send_sems, recv_sems):
        my_slot = _linear(_me())
        sends, recvs = [], []
        for k, flip in enumerate(OTHERS):
            peer = _other(flip)
            sends.append(pltpu.make_async_remote_copy(
                src_ref=small_ref, dst_ref=out_ref.at[my_slot], send_sem=send_sems.at[k], recv_sem=recv_sems.at[k],
                device_id=peer, device_id_type=MESH))
            recvs.append(pltpu.make_async_remote_copy(
                src_ref=small_ref, dst_ref=out_ref.at[_linear(peer)], send_sem=send_sems.at[k],
                recv_sem=recv_sems.at[k], device_id=peer, device_id_type=MESH))
        _run_copies(sends, sends, recvs)

    out = pl.pallas_call(
        body, name="exchange_small", in_specs=[ANY], out_specs=ANY,
        out_shape=jax.ShapeDtypeStruct((8,) + small.shape, small.dtype),
        scratch_shapes=[pltpu.SemaphoreType.DMA((7,)), pltpu.SemaphoreType.DMA((7,))])(small)
    return lax.dynamic_update_index_in_dim(out, small, _linear(_me()), 0)


def _pair_swap(halves):
    n = len(halves)

    def body(*refs):
        src, out = refs[:n], refs[n:2 * n]
        send_sems, recv_sems = refs[2 * n:]
        sib = _peer("c")
        copies = [pltpu.make_async_remote_copy(
            src_ref=src[i], dst_ref=out[i], send_sem=send_sems.at[i], recv_sem=recv_sems.at[i],
            device_id=sib, device_id_type=MESH) for i in range(n)]
        _run_copies(copies, copies, copies)

    return pl.pallas_call(
        body, name="pair_swap", in_specs=[ANY] * n, out_specs=[ANY] * n,
        out_shape=[jax.ShapeDtypeStruct(h.shape, h.dtype) for h in halves],
        scratch_shapes=[pltpu.SemaphoreType.DMA((n,)), pltpu.SemaphoreType.DMA((n,))])(*halves)


HBM = pl.BlockSpec(memory_space=pltpu.HBM)
SEM = pl.BlockSpec(memory_space=pltpu.SEMAPHORE)
EFFECT = pltpu.SideEffectType.DATAFLOW_SIDE_EFFECTING


def _hbm(a):
    return pltpu.with_memory_space_constraint(a, pltpu.HBM)


def _transfer_start(name, srcs, lands, n_copies, make_copies, after=None):
    n, m = len(srcs), len(lands)

    def body(*refs):
        src, land = refs[:n], refs[n:n + m]
        outs = refs[n + m + (after is not None):]
        send_sems, recv_sems, token = outs[0], outs[1], outs[-1]
        for cp in make_copies(src, land, send_sems, recv_sems):
            cp.start()
        token[...] = jnp.zeros_like(token)

    arrs = list(srcs) + list(lands)
    in_specs, args = _with_dep([HBM] * (n + m), [_hbm(a) for a in arrs], after)
    out = pl.pallas_call(
        body, name=name,
        out_shape=(pltpu.SemaphoreType.DMA((n_copies,)), pltpu.SemaphoreType.DMA((n_copies,)),
                   *[pltpu.HBM(a.shape, a.dtype) for a in arrs], jax.ShapeDtypeStruct((8, 128), F32)),
        in_specs=in_specs,
        out_specs=(SEM, SEM, *[HBM] * (n + m), pl.BlockSpec(memory_space=pltpu.VMEM)),
        input_output_aliases={i: 2 + i for i in range(n + m)},
        compiler_params=pltpu.CompilerParams(has_side_effects=EFFECT))(*args)
    return out[0], out[1], list(out[2:2 + n]), list(out[2 + n:2 + n + m]), out[-1]


def _transfer_wait(name, send_sems, recv_sems, srcs, lands, make_copies, after):
    n, m = len(srcs), len(lands)

    def body(*refs):
        src, land = refs[:n], refs[n:n + m]
        s_sems, r_sems = refs[n + m], refs[n + m + 1]
        for cp in make_copies(src, land, s_sems, r_sems):
            cp.wait_send()
            cp.wait_recv()

    arrs = list(srcs) + list(lands)
    out = pl.pallas_call(
        body, name=name, out_shape=tuple(pltpu.HBM(a.shape, a.dtype) for a in arrs),
        in_specs=[HBM] * (n + m) + [SEM, SEM, ANY], out_specs=tuple([HBM] * (n + m)),
        input_output_aliases={i: i for i in range(n + m)},
        compiler_params=pltpu.CompilerParams(has_side_effects=EFFECT))(*arrs, send_sems, recv_sems, after)
    return list(out[:n]), list(out[n:])


def _gather_copies(src, land, send_sems, recv_sems):
    me = _chip_of(_me())
    copies = []
    for i in range(len(src)):
        for r, rel in enumerate(CHIP_RELS):
            k = 3 * i + r
            copies.append(pltpu.make_async_remote_copy(
                src_ref=src[i], dst_ref=land[i].at[me], send_sem=send_sems.at[k], recv_sem=recv_sems.at[k],
                device_id=_peer(rel), device_id_type=MESH))
    return copies


def _small_copies(src, land, send_sems, recv_sems):
    my_slot = _linear(_me())
    return [pltpu.make_async_remote_copy(
        src_ref=src[0], dst_ref=land[0].at[my_slot], send_sem=send_sems.at[k], recv_sem=recv_sems.at[k],
        device_id=_other(flip), device_id_type=MESH) for k, flip in enumerate(OTHERS)]


def _pieces_copies(src, land, send_sems, recv_sems):
    copies = []
    for k, flip in enumerate(OTHERS):
        peer = _other(flip)
        copies.append(pltpu.make_async_remote_copy(
            src_ref=src[0].at[_linear(peer)], dst_ref=land[0].at[k], send_sem=send_sems.at[k],
            recv_sem=recv_sems.at[k], device_id=peer, device_id_type=MESH))
    return copies


def _row_block(rows, cols, budget=2 * 1024 * 1024):
    rb = max(8, (budget // (4 * cols)) // 8 * 8)
    while rows % rb:
        rb -= 8
    return rb if rb > 0 else rows


def _sum_slots(name, first, rest):
    R, Cc = first.shape
    K = rest.shape[0]
    rb = _row_block(R, Cc)

    def body(f_ref, r_ref, o_ref):
        acc = f_ref[...].astype(F32)
        for j in range(K):
            acc = acc + r_ref[j].astype(F32)
        o_ref[...] = acc

    return pl.pallas_call(
        body, name=name, grid=(R // rb,),
        in_specs=[pl.BlockSpec((rb, Cc), lambda i: (i, 0)), pl.BlockSpec((K, rb, Cc), lambda i: (0, i, 0))],
        out_specs=pl.BlockSpec((rb, Cc), lambda i: (i, 0)),
        out_shape=jax.ShapeDtypeStruct((R, Cc), F32), compiler_params=_cp("parallel"))(first, rest)


def _adamw_math(w, gv, m, v):
    mn = ADAM_B1 * m + (1.0 - ADAM_B1) * gv
    vn = ADAM_B2 * v + (1.0 - ADAM_B2) * (gv * gv)
    m_hat = mn / (1.0 - ADAM_B1 ** ADAM_STEP)
    v_hat = vn / (1.0 - ADAM_B2 ** ADAM_STEP)
    return -ADAM_LR * (m_hat / (jnp.sqrt(v_hat) + ADAM_EPS) + ADAM_WD * w), mn, vn


def _adamw_halves(name, w, mine, theirs, m, v, core):
    R, Cc = w.shape
    r2 = R // 2
    rb = _row_block(r2, Cc, 1024 * 1024)
    nb2 = r2 // rb

    def body(c_ref, w_ref, mine_ref, theirs_ref, m_ref, v_ref, g_ref, d_ref, mo_ref, vo_ref):
        is_mine = (pl.program_id(0) // nb2) == c_ref[0]
        gv = jnp.where(is_mine, mine_ref[...], theirs_ref[...])
        g_ref[...] = gv
        d_ref[...], mo_ref[...], vo_ref[...] = _adamw_math(w_ref[...], gv, m_ref[...], v_ref[...])

    blk = pl.BlockSpec((rb, Cc), lambda i, c: (i, 0))
    half = lambda own: pl.BlockSpec(
        (rb, Cc), lambda i, c: (jnp.clip(i - (c[0] if own else 1 - c[0]) * nb2, 0, nb2 - 1), 0))
    return pl.pallas_call(
        body, name=name,
        grid_spec=pltpu.PrefetchScalarGridSpec(
            num_scalar_prefetch=1, grid=(2 * nb2,), in_specs=[blk, half(True), half(False), blk, blk],
            out_specs=[blk] * 4),
        out_shape=[jax.ShapeDtypeStruct((R, Cc), F32)] * 4, compiler_params=_cp("parallel"))(core, w, mine, theirs, m, v)


def _adamw(name, w, g, m, v):
    R, Cc = w.shape
    rb = _row_block(R, Cc, 1024 * 1024)

    def body(w_ref, g_ref, m_ref, v_ref, d_ref, mo_ref, vo_ref):
        d_ref[...], mo_ref[...], vo_ref[...] = _adamw_math(w_ref[...], g_ref[...], m_ref[...], v_ref[...])

    blk = pl.BlockSpec((rb, Cc), lambda i: (i, 0))
    return pl.pallas_call(
        body, name=name, grid=(R // rb,), in_specs=[blk] * 4, out_specs=[blk] * 3,
        out_shape=[jax.ShapeDtypeStruct((R, Cc), F32)] * 3, compiler_params=_cp("parallel"))(w, g, m, v)


def _pack(arrs):
    rows = []
    for a in arrs:
        flat = a.reshape(-1)
        pad = (-flat.shape[0]) % 128
        rows.append(jnp.pad(flat, (0, pad)).reshape(-1, 128))
    buf = jnp.concatenate(rows, axis=0)
    return jnp.pad(buf, ((0, (-buf.shape[0]) % 8), (0, 0)))


def _unpack(buf, shapes):
    out, r = [], 0
    for s in shapes:
        n = math.prod(s)
        nr = -(-n // 128)
        out.append(buf[r:r + nr].reshape(-1)[:n].reshape(s))
        r += nr
    return out


BIG = ("w_in", "w_out", "w_up", "w_down")
CONV = ("dn_conv_w", "ffn_conv_w")
REPL = ("attn_norm_g", "dn_a_log", "dn_dt_bias", "dn_out_norm_g", "sg_norm_g", "sg_w", "sg_b",
        "ffn_norm_g", "ffn_conv_b", "final_norm_g")
ORDER = ("attn_norm_g", "w_in", "dn_conv_w", "dn_a_log", "dn_dt_bias", "dn_out_norm_g", "sg_norm_g", "sg_w",
         "sg_b", "w_out", "ffn_norm_g", "w_up", "ffn_conv_w", "ffn_conv_b", "w_down", "final_norm_g")


def kernel(x, attn_norm_g, w_in, dn_conv_w, dn_a_log, dn_dt_bias, dn_out_norm_g, sg_norm_g, sg_w, sg_b, w_out, ffn_norm_g, w_up, ffn_conv_w, ffn_conv_b, w_down, final_norm_g, loss_target, m_attn_norm_g, m_w_in, m_dn_conv_w, m_dn_a_log, m_dn_dt_bias, m_dn_out_norm_g, m_sg_norm_g, m_sg_w, m_sg_b, m_w_out, m_ffn_norm_g, m_w_up, m_ffn_conv_w, m_ffn_conv_b, m_w_down, m_final_norm_g, v_attn_norm_g, v_w_in, v_dn_conv_w, v_dn_a_log, v_dn_dt_bias, v_dn_out_norm_g, v_sg_norm_g, v_sg_w, v_sg_b, v_w_out, v_ffn_norm_g, v_w_up, v_ffn_conv_w, v_ffn_conv_b, v_w_down, v_final_norm_g):
    W = dict(attn_norm_g=attn_norm_g, w_in=w_in, dn_conv_w=dn_conv_w, dn_a_log=dn_a_log, dn_dt_bias=dn_dt_bias,
             dn_out_norm_g=dn_out_norm_g, sg_norm_g=sg_norm_g, sg_w=sg_w, sg_b=sg_b, w_out=w_out,
             ffn_norm_g=ffn_norm_g, w_up=w_up, ffn_conv_w=ffn_conv_w, ffn_conv_b=ffn_conv_b, w_down=w_down,
             final_norm_g=final_norm_g)
    Mo = dict(attn_norm_g=m_attn_norm_g, w_in=m_w_in, dn_conv_w=m_dn_conv_w, dn_a_log=m_dn_a_log,
              dn_dt_bias=m_dn_dt_bias, dn_out_norm_g=m_dn_out_norm_g, sg_norm_g=m_sg_norm_g, sg_w=m_sg_w,
              sg_b=m_sg_b, w_out=m_w_out, ffn_norm_g=m_ffn_norm_g, w_up=m_w_up, ffn_conv_w=m_ffn_conv_w,
              ffn_conv_b=m_ffn_conv_b, w_down=m_w_down, final_norm_g=m_final_norm_g)
    Vo = dict(attn_norm_g=v_attn_norm_g, w_in=v_w_in, dn_conv_w=v_dn_conv_w, dn_a_log=v_dn_a_log,
              dn_dt_bias=v_dn_dt_bias, dn_out_norm_g=v_dn_out_norm_g, sg_norm_g=v_sg_norm_g, sg_w=v_sg_w,
              sg_b=v_sg_b, w_out=v_w_out, ffn_norm_g=v_ffn_norm_g, w_up=v_w_up, ffn_conv_w=v_ffn_conv_w,
              ffn_conv_b=v_ffn_conv_b, w_down=v_w_down, final_norm_g=v_final_norm_g)
    xi, yi, ci = lax.axis_index("x"), lax.axis_index("y"), lax.axis_index("c")
    chip = 2 * xi + yi

    me_lin = 4 * xi + 2 * yi + ci

    g_in, g_dnc = _gather_first(w_in[0].astype(BF16), dn_conv_w[0])
    late = ("w_out", "w_up", "w_down", "ffn_conv_w")
    late_shards = [W[n][0].astype(BF16) for n in late[:3]] + [ffn_conv_w[0]]
    late_lands = [lax.dynamic_update_index_in_dim(lax.empty((4,) + s.shape, s.dtype), s, chip, 0) for s in late_shards]
    n_late = 3 * len(late_shards)
    ssem, rsem, late_src, late_lands, token = _transfer_start("gather_rest_start", late_shards, late_lands,
                                                              n_late, _gather_copies, after=g_in)

    def late_weights(after):
        _, (g_out, g_up, g_down, g_ffc) = _transfer_wait("gather_rest_wait", ssem, rsem, late_src, late_lands,
                                                         _gather_copies, after)
        return dict(w_out=g_out.reshape(D_MODEL, D_MODEL), w_up=g_up.transpose(1, 0, 2).reshape(D_MODEL, 2 * D_FF),
                    w_down=g_down.reshape(D_FF, D_MODEL), ffn_conv_w=g_ffc.transpose(1, 0, 2).reshape(3, 2 * D_FF))

    full = dict(
        w_in=g_in,
        dn_conv_w=g_dnc.transpose(1, 0, 2).reshape(4, 3 * DN_WIDTH),
        attn_norm_g=attn_norm_g, dn_a_log=dn_a_log, dn_dt_bias=dn_dt_bias, dn_out_norm_g=dn_out_norm_g,
        sg_norm_g=sg_norm_g, sg_w=sg_w[0], sg_b=sg_b[0], ffn_norm_g=ffn_norm_g, ffn_conv_b=ffn_conv_b,
        final_norm_g=final_norm_g[None])

    pending = {}
    early_names = ("dn_out_norm_g", "sg_norm_g", "sg_w", "sg_b", "ffn_norm_g", "ffn_conv_w", "ffn_conv_b",
                   "final_norm_g")
    late_names = ("attn_norm_g", "dn_a_log", "dn_dt_bias", "dn_conv_w")

    def on_grad(name, gw):
        if name == "small_early":
            buf = _pack([gw[n] for n in early_names])
            land = lax.dynamic_update_index_in_dim(lax.empty((8,) + buf.shape, F32), buf, me_lin, 0)
            s_sem, r_sem, src, lands, tok = _transfer_start("small_early_start", [buf], [land], 7, _small_copies)
            pending[name] = (s_sem, r_sem, src, lands)
            return tok
        g8 = gw.reshape(8, -1, gw.shape[-1])
        land = lax.empty((7,) + g8.shape[1:], BF16)
        s_sem, r_sem, src, lands, tok = _transfer_start(f"reduce_{name}_start", [g8], [land], 7, _pieces_copies)
        pending[name] = (s_sem, r_sem, src, lands)
        return tok

    loss_row, grad_x, g = _local_step(x[0], loss_target[0], full, dep=token, late_weights=late_weights,
                                      on_grad=on_grad)

    small_names = REPL + CONV
    late_all = _exchange_small(_pack([g[n] for n in late_names] + [loss_row]))
    late_sum = _sum_slots("sum_small_late", late_all[0], late_all[1:])
    s_sem, r_sem, src, lands = pending["small_early"]
    _, (early_all,) = _transfer_wait("small_early_wait", s_sem, r_sem, src, lands, _small_copies, grad_x)
    early_sum = _sum_slots("sum_small_early", early_all[0], early_all[1:])
    *late_vals, loss_sum = _unpack(late_sum, [g[n].shape for n in late_names] + [loss_row.shape])
    loss = loss_sum[0, 0]
    sg = dict(zip(late_names, late_vals))
    sg.update(zip(early_names, _unpack(early_sum, [g[n].shape for n in early_names])))
    sg["dn_conv_w"] = lax.dynamic_slice_in_dim(sg["dn_conv_w"], chip * (3 * DN_WIDTH // 4), 3 * DN_WIDTH // 4, axis=1)
    sg["ffn_conv_w"] = lax.dynamic_slice_in_dim(sg["ffn_conv_w"], chip * (2 * D_FF // 4), 2 * D_FF // 4, axis=1)

    halves = []
    for n in ("w_down", "w_up", "w_out", "w_in"):
        s_sem, r_sem, src, lands = pending[n]
        sent, got = _transfer_wait(f"reduce_{n}_wait", s_sem, r_sem, src, lands, _pieces_copies, grad_x)
        own = lax.dynamic_index_in_dim(sent[0], me_lin, axis=0, keepdims=False)
        halves.append(_sum_slots(f"sum_{n}", own, got[0]))
    theirs = _pair_swap(halves)
    core = ci.astype(jnp.int32).reshape(1)
    grads, delta, new_m, new_v = {}, {}, {}, {}
    for n, mine_h, their_h in zip(("w_down", "w_up", "w_out", "w_in"), halves, theirs):
        shp = W[n].shape
        gr, d, mn, vn = _adamw_halves(f"adamw_{n}", W[n][0], mine_h, their_h, Mo[n][0], Vo[n][0], core)
        grads[n], delta[n], new_m[n], new_v[n] = gr.reshape(shp), d.reshape(shp), mn.reshape(shp), vn.reshape(shp)
    shapes = [W[n].shape for n in small_names]
    for n in small_names:
        grads[n] = sg[n].reshape(W[n].shape)
    d, mn, vn = _adamw("adamw_small", _pack([W[n] for n in small_names]), _pack([grads[n] for n in small_names]),
                       _pack([Mo[n] for n in small_names]), _pack([Vo[n] for n in small_names]))
    for dst, buf in ((delta, d), (new_m, mn), (new_v, vn)):
        dst.update(zip(small_names, _unpack(buf, shapes)))

    return (loss, grad_x[None], *[grads[n] for n in ORDER], *[delta[n] for n in ORDER],
            *[new_m[n] for n in ORDER], *[new_v[n] for n in ORDER])
```

```python
import functools
import math

import jax
import jax.numpy as jnp
from jax import lax
from jax.experimental import pallas as pl
from jax.experimental.pallas import tpu as pltpu

F32 = jnp.float32
BF16 = jnp.bfloat16

D_MODEL = 1024
CHUNK = 64
SCAN_CHUNKS = 2
HEAD_DIM = 128
N_HEADS = 4
DN_WIDTH = 512
SG_WIDTH = 512
SG_GROUPS = 4
SG_BLOCK = 128
D_FF = 2816
PROJ_COLS = 3080
PROJ_PAD = 3200
BA_COL = 3072
EPS = 1e-6
NEG = -1e30
VMEM_LIMIT = 56 * 1024 * 1024

ADAM_LR = 0.001
ADAM_B1 = 0.9
ADAM_B2 = 0.999
ADAM_EPS = 1e-08
ADAM_WD = 0.01
ADAM_STEP = 10

MESH = pl.DeviceIdType.MESH
ANY = pl.BlockSpec(memory_space=pl.ANY)


def _cp(*sem):
    return pltpu.CompilerParams(dimension_semantics=sem, vmem_limit_bytes=VMEM_LIMIT)


def _bf(a):
    return a.astype(BF16)


def _nn(a, b):
    return jnp.dot(_bf(a), _bf(b), preferred_element_type=F32)


def _nt(a, b):
    return lax.dot_general(_bf(a), _bf(b), (((1,), (1,)), ((), ())), preferred_element_type=F32)


def _tn(a, b):
    return lax.dot_general(_bf(a), _bf(b), (((0,), (0,)), ((), ())), preferred_element_type=F32)


def _split(a):
    hi = _bf(a)
    return hi, _bf(a - hi.astype(F32))


def _sigmoid(x):
    return 0.5 * jnp.tanh(0.5 * x) + 0.5


def _silu(x):
    return x * _sigmoid(x)


def _dsilu(x):
    s = _sigmoid(x)
    return s * (1.0 + x * (1.0 - s))


_GELU_C = math.sqrt(2.0 / math.pi)
_GELU_A = 0.044715


def _gelu(x):
    return 0.5 * x * (1.0 + jnp.tanh(_GELU_C * (x + _GELU_A * x * x * x)))


def _dgelu(x):
    t = jnp.tanh(_GELU_C * (x + _GELU_A * x * x * x))
    return 0.5 * (1.0 + t) + 0.5 * x * (1.0 - t * t) * _GELU_C * (1.0 + 3.0 * _GELU_A * x * x)


def _softplus(x):
    return jnp.maximum(x, 0.0) + jnp.log(1.0 + jnp.exp(-jnp.abs(x)))


def _mm_nn(name, a, b, out_dtype, tm, tn, res=None):
    M, K = a.shape
    N = b.shape[1]
    tm, tn = min(tm, M), min(tn, N)

    def body(*refs):
        a_ref, b_ref = refs[0], refs[1]
        o_ref = refs[-1]
        acc = _nn(a_ref[...], b_ref[...])
        if res is not None:
            acc = acc + refs[2][...]
        o_ref[...] = acc.astype(o_ref.dtype)

    in_specs = [pl.BlockSpec((tm, K), lambda j, i: (i, 0)), pl.BlockSpec((K, tn), lambda j, i: (0, j))]
    args = [a, b]
    if res is not None:
        in_specs.append(pl.BlockSpec((tm, tn), lambda j, i: (i, j)))
        args.append(res)
    return pl.pallas_call(
        body, name=name, grid=(N // tn, M // tm), in_specs=in_specs,
        out_specs=pl.BlockSpec((tm, tn), lambda j, i: (i, j)),
        out_shape=jax.ShapeDtypeStruct((M, N), out_dtype),
        compiler_params=_cp("parallel", "parallel"))(*args)


def _with_dep(in_specs, args, dep):
    if dep is None:
        return in_specs, args
    return in_specs + [ANY], args + [dep]


SUB_ROWS = 128


def _sub_blocks(tm):
    return [slice(r0, min(r0 + SUB_ROWS, tm)) for r0 in range(0, tm, SUB_ROWS)]


def _rms_hat(xv):
    r = lax.rsqrt(jnp.mean(xv * xv, axis=-1, keepdims=True) + EPS)
    return xv * r, r


def _rms_bwd_vals(dh, xh, r, g):
    dxh = dh * g
    return r * (dxh - xh * jnp.mean(dxh * xh, axis=-1, keepdims=True)), jnp.sum(dh * xh, axis=0, keepdims=True)


def _in_proj(x, g, w4, tm=512, dep=None):
    T, K = x.shape
    ng, _, wc = w4.shape
    tm = min(tm, T)

    def body(x_ref, g_ref, w4_ref, *rest):
        p_ref, h_ref, w_ref = rest[-3:]

        @pl.when(pl.program_id(0) == 0)
        def _():
            w_ref[:, ng * wc:] = jnp.zeros((K, PROJ_PAD - ng * wc), BF16)
            for j in range(ng):
                w_ref[:, j * wc:(j + 1) * wc] = w4_ref[j]
        for r in _sub_blocks(tm):
            xh, _ = _rms_hat(x_ref[r, :])
            h_ref[r, :] = (xh * g_ref[...]).astype(BF16)
        p_ref[...] = jnp.dot(h_ref[...], w_ref[...], preferred_element_type=F32)

    in_specs, args = _with_dep(
        [pl.BlockSpec((tm, K), lambda i: (i, 0)), pl.BlockSpec((1, K), lambda i: (0, 0)),
         pl.BlockSpec((ng, K, wc), lambda i: (0, 0, 0))], [x, g, w4], dep)
    return pl.pallas_call(
        body, name="in_proj", grid=(T // tm,), in_specs=in_specs,
        out_specs=[pl.BlockSpec((tm, PROJ_PAD), lambda i: (i, 0)), pl.BlockSpec((tm, K), lambda i: (i, 0)),
                   pl.BlockSpec((K, PROJ_PAD), lambda i: (0, 0))],
        out_shape=[jax.ShapeDtypeStruct((T, PROJ_PAD), F32), jax.ShapeDtypeStruct((T, K), BF16),
                   jax.ShapeDtypeStruct((K, PROJ_PAD), BF16)],
        compiler_params=_cp("arbitrary"))(*args)


def _out_proj(mix, w, x, g, tm=512):
    T, K = mix.shape
    Dm = w.shape[1]
    tm = min(tm, T)

    def body(a_ref, w_ref, x_ref, g_ref, x2_ref, h_ref):
        x2_ref[...] = _nn(a_ref[...], w_ref[...]) + x_ref[...]
        for r in _sub_blocks(tm):
            xh, _ = _rms_hat(x2_ref[r, :])
            h_ref[r, :] = (xh * g_ref[...]).astype(BF16)

    row = lambda width: pl.BlockSpec((tm, width), lambda i: (i, 0))
    return pl.pallas_call(
        body, name="out_proj", grid=(T // tm,),
        in_specs=[row(K), pl.BlockSpec((K, Dm), lambda i: (0, 0)), row(Dm), pl.BlockSpec((1, Dm), lambda i: (0, 0))],
        out_specs=[row(Dm), row(Dm)],
        out_shape=[jax.ShapeDtypeStruct((T, Dm), F32), jax.ShapeDtypeStruct((T, Dm), BF16)],
        compiler_params=_cp("parallel"))(mix, w, x, g)


def _down_proj_loss(act, w, x2, tgt, g, tm=512):
    T, K = act.shape
    Dm = w.shape[1]
    tm = min(tm, T)

    def body(a_ref, w_ref, x_ref, t_ref, g_ref, loss_ref, dx_ref, gg_ref):
        @pl.when(pl.program_id(0) == 0)
        def _():
            gg_ref[...] = jnp.zeros_like(gg_ref)
            loss_ref[...] = jnp.zeros_like(loss_ref)
        dx_ref[...] = _nn(a_ref[...], w_ref[...]) + x_ref[...]
        for r in _sub_blocks(tm):
            xh, rr = _rms_hat(dx_ref[r, :])
            e = xh * g_ref[...] - t_ref[r, :]
            loss_ref[...] += jnp.zeros_like(loss_ref) + (0.5 / Dm) * jnp.sum(e * e)
            dx, gg = _rms_bwd_vals(e * (1.0 / Dm), xh, rr, g_ref[...])
            dx_ref[r, :] = dx
            gg_ref[...] += gg

    row = lambda width: pl.BlockSpec((tm, width), lambda i: (i, 0))
    vec = pl.BlockSpec((1, Dm), lambda i: (0, 0))
    return pl.pallas_call(
        body, name="down_proj_loss", grid=(T // tm,),
        in_specs=[row(K), pl.BlockSpec((K, Dm), lambda i: (0, 0)), row(Dm), row(Dm), vec],
        out_specs=[pl.BlockSpec((1, 128), lambda i: (0, 0)), row(Dm), vec],
        out_shape=[jax.ShapeDtypeStruct((1, 128), F32), jax.ShapeDtypeStruct((T, Dm), F32),
                   jax.ShapeDtypeStruct((1, Dm), F32)],
        compiler_params=_cp("arbitrary"))(act, w, x2, tgt, g)


def _mm_nt_rms_bwd(name, a, b, x, g, dres, tm=512, dep=None):
    M, K = a.shape
    Dm = b.shape[0]
    tm = min(tm, M)

    def body(a_ref, b_ref, x_ref, g_ref, dres_ref, *rest):
        dx_ref, gg_ref = rest[-2:]

        @pl.when(pl.program_id(0) == 0)
        def _():
            gg_ref[...] = jnp.zeros_like(gg_ref)
        dx_ref[...] = _nt(a_ref[...], b_ref[...])
        for r in _sub_blocks(tm):
            xh, rr = _rms_hat(x_ref[r, :])
            dx, gg = _rms_bwd_vals(dx_ref[r, :], xh, rr, g_ref[...])
            dx_ref[r, :] = dres_ref[r, :] + dx
            gg_ref[...] += gg

    row = lambda width: pl.BlockSpec((tm, width), lambda i: (i, 0))
    vec = pl.BlockSpec((1, Dm), lambda i: (0, 0))
    in_specs, args = _with_dep([row(K), pl.BlockSpec((Dm, K), lambda i: (0, 0)), row(Dm), vec, row(Dm)],
                               [a, b, x, g, dres], dep)
    return pl.pallas_call(
        body, name=name, grid=(M // tm,), in_specs=in_specs, out_specs=[row(Dm), vec],
        out_shape=[jax.ShapeDtypeStruct((M, Dm), F32), jax.ShapeDtypeStruct((1, Dm), F32)],
        compiler_params=_cp("arbitrary"))(*args)


def _mm_nt(name, a, b, out_dtype, tm, tn, dep=None):
    M, K = a.shape
    N = b.shape[0]
    tm, tn = min(tm, M), min(tn, N)

    def body(a_ref, b_ref, *rest):
        o_ref = rest[-1]
        o_ref[...] = _nt(a_ref[...], b_ref[...]).astype(o_ref.dtype)

    in_specs, args = _with_dep(
        [pl.BlockSpec((tm, K), lambda i, j: (i, 0)), pl.BlockSpec((tn, K), lambda i, j: (j, 0))], [a, b], dep)
    return pl.pallas_call(
        body, name=name, grid=(M // tm, N // tn), in_specs=in_specs,
        out_specs=pl.BlockSpec((tm, tn), lambda i, j: (i, j)),
        out_shape=jax.ShapeDtypeStruct((M, N), out_dtype),
        compiler_params=_cp("parallel", "parallel"))(*args)


def _mm_tn(name, a, b, tm, tn, tk, col_major_tiles=False, col_groups=None):
    T, M = a.shape
    N = b.shape[1]
    tm, tn, tk = min(tm, M), min(tn, N), min(tk, T)
    nk = T // tk

    def body(a_ref, b_ref, o_ref, acc_ref):
        k = pl.program_id(2)

        @pl.when(k == 0)
        def _():
            acc_ref[...] = jnp.zeros_like(acc_ref)
        acc_ref[...] += _tn(a_ref[...], b_ref[...])

        @pl.when(k == nk - 1)
        def _():
            if col_groups:
                for j in range(col_groups[0]):
                    o_ref[j] = acc_ref[:, j * col_groups[1]:(j + 1) * col_groups[1]].astype(BF16)
            else:
                o_ref[...] = acc_ref[...].astype(BF16).reshape(o_ref.shape)

    if col_groups:
        assert tm == M and tn == N and col_groups[0] * col_groups[1] <= N
        out_spec = pl.BlockSpec((col_groups[0], M, col_groups[1]), lambda i, j, k: (0, 0, 0))
        out_shape = jax.ShapeDtypeStruct((col_groups[0], M, col_groups[1]), BF16)
    elif col_major_tiles:
        assert tm == M
        out_spec = pl.BlockSpec((1, tm, tn), lambda i, j, k: (j, 0, 0))
        out_shape = jax.ShapeDtypeStruct((N // tn, M, tn), BF16)
    else:
        out_spec = pl.BlockSpec((tm, tn), lambda i, j, k: (i, j))
        out_shape = jax.ShapeDtypeStruct((M, N), BF16)
    return pl.pallas_call(
        body, name=name, grid=(M // tm, N // tn, nk),
        in_specs=[pl.BlockSpec((tk, tm), lambda i, j, k: (k, i)), pl.BlockSpec((tk, tn), lambda i, j, k: (k, j))],
        out_specs=out_spec, out_shape=out_shape, scratch_shapes=[pltpu.VMEM((tm, tn), F32)],
        compiler_params=_cp("parallel", "parallel", "arbitrary"))(a, b)


def _halo_prev_spec(rb, width):
    return pl.BlockSpec((8, width), lambda i: (jnp.maximum(i * (rb // 8) - 1, 0), 0))


def _halo_next_spec(rb, width, T):
    return pl.BlockSpec((8, width), lambda i: (jnp.minimum((i + 1) * (rb // 8), T // 8 - 1), 0))


LANES = 128
FF_STRIPS = D_FF // LANES
ROW_CHUNK = 32


def _strip(j, base=0):
    return pl.ds(pl.multiple_of(base + j * LANES, LANES), LANES)


def _ffn_act(up, w, b, rb=256):
    T, W = up.shape
    rb = min(rb, T)

    def body(up_ref, halo_ref, w_ref, b_ref, act_ref, ext_scr):
        first = pl.program_id(0) == 0

        def strip(j, slot):
            halves = (_strip(j), _strip(j, D_FF))
            wv = [w_ref[:, cols] for cols in halves]
            bv = [b_ref[:, cols] for cols in halves]
            for h, cols in enumerate(halves):
                ext_scr[slot, h,0:8] = jnp.where(first, 0.0, halo_ref[:, cols])
                ext_scr[slot, h,8:] = up_ref[:, cols]
            for r0 in range(0, rb, ROW_CHUNK):
                n = min(ROW_CHUNK, rb - r0)
                c = [ext_scr[slot, h,6 + r0:6 + r0 + n] * wv[h][0:1] + ext_scr[slot, h,7 + r0:7 + r0 + n] * wv[h][1:2]
                     + ext_scr[slot, h,8 + r0:8 + r0 + n] * wv[h][2:3] + bv[h] for h in range(2)]
                act_ref[r0:r0 + n, halves[0]] = (_silu(c[0]) * c[1]).astype(BF16)

        def pair(jj, carry):
            strip(2 * jj, 0)
            strip(2 * jj + 1, 1)
            return carry

        lax.fori_loop(0, FF_STRIPS // 2, pair, 0)

    return pl.pallas_call(
        body, name="ffn_act", grid=(T // rb,),
        in_specs=[pl.BlockSpec((rb, W), lambda i: (i, 0)), _halo_prev_spec(rb, W),
                  pl.BlockSpec((3, W), lambda i: (0, 0)), pl.BlockSpec((1, W), lambda i: (0, 0))],
        out_specs=pl.BlockSpec((rb, D_FF), lambda i: (i, 0)),
        out_shape=jax.ShapeDtypeStruct((T, D_FF), BF16),
        scratch_shapes=[pltpu.VMEM((2, 2, rb + 8, LANES), F32)], compiler_params=_cp("parallel"))(up, up, w, b)


def _ffn_act_bwd(up, dact, w, b, rb=128, dep=None):
    T, W = up.shape
    rb = min(rb, T)
    nb = T // rb
    re = rb + 8

    def body(up_ref, prev_ref, next_ref, da_ref, danext_ref, w_ref, b_ref, *rest):
        dup_ref, gw_ref, gb_ref, ext_scr, dc_scr = rest[-5:]
        i = pl.program_id(0)

        @pl.when(i == 0)
        def _():
            gw_ref[...] = jnp.zeros_like(gw_ref)
            gb_ref[...] = jnp.zeros_like(gb_ref)
        last = i == nb - 1

        def fold8(a):
            return jnp.sum(a.reshape(a.shape[0] // 8, 8, LANES), axis=0)

        def strip(j, slot):
            halves = (_strip(j), _strip(j, D_FF))
            wv = [w_ref[:, cols] for cols in halves]
            bv = [b_ref[:, cols] for cols in halves]
            for h, cols in enumerate(halves):
                ext_scr[slot, h,0:8] = jnp.where(i > 0, prev_ref[:, cols], 0.0)
                ext_scr[slot, h,8:8 + rb] = up_ref[:, cols]
                ext_scr[slot, h,8 + rb:] = next_ref[:, cols]
            gb = [jnp.zeros((8, LANES), F32) for _ in range(2)]
            gw = [[jnp.zeros((8, LANES), F32) for _ in range(3)] for _ in range(2)]
            for r0 in range(0, re, ROW_CHUNK):
                n = min(ROW_CHUNK, re - r0)
                tp = [[ext_scr[slot, h,6 + k + r0:6 + k + r0 + n] for k in range(3)] for h in range(2)]
                c = [tp[h][0] * wv[h][0:1] + tp[h][1] * wv[h][1:2] + tp[h][2] * wv[h][2:3] + bv[h] for h in range(2)]
                if r0 < rb:
                    da = da_ref[r0:r0 + n, halves[0]]
                else:
                    da = jnp.where(last, 0.0, danext_ref[:, halves[0]])
                s = _sigmoid(c[0])
                gs = c[0] * s
                dcs = (da * c[1] * (s + gs * (1.0 - s)), da * gs)
                for h in range(2):
                    dc_scr[slot, h,r0:r0 + n] = dcs[h]
                    if r0 < rb:
                        gb[h] = gb[h] + fold8(dcs[h])
                        for k in range(3):
                            gw[h][k] = gw[h][k] + fold8(tp[h][k] * dcs[h])
            for r0 in range(0, rb, ROW_CHUNK):
                n = min(ROW_CHUNK, rb - r0)
                for h, cols in enumerate(halves):
                    dup = (dc_scr[slot, h,r0:r0 + n] * wv[h][2:3] + dc_scr[slot, h,r0 + 1:r0 + 1 + n] * wv[h][1:2]
                           + dc_scr[slot, h,r0 + 2:r0 + 2 + n] * wv[h][0:1])
                    dup_ref[r0:r0 + n, cols] = dup.astype(BF16)
            for h, cols in enumerate(halves):
                gb_ref[:, cols] += jnp.sum(gb[h], axis=0, keepdims=True)
                for k in range(3):
                    gw_ref[k:k + 1, cols] += jnp.sum(gw[h][k], axis=0, keepdims=True)

        def pair(jj, carry):
            strip(2 * jj, 0)
            strip(2 * jj + 1, 1)
            return carry

        lax.fori_loop(0, FF_STRIPS // 2, pair, 0)

    in_specs, args = _with_dep(
        [pl.BlockSpec((rb, W), lambda i: (i, 0)), _halo_prev_spec(rb, W), _halo_next_spec(rb, W, T),
         pl.BlockSpec((rb, D_FF), lambda i: (i, 0)), _halo_next_spec(rb, D_FF, T),
         pl.BlockSpec((3, W), lambda i: (0, 0)), pl.BlockSpec((1, W), lambda i: (0, 0))],
        [up, up, up, dact, dact, w, b], dep)
    return pl.pallas_call(
        body, name="ffn_act_bwd", grid=(nb,), in_specs=in_specs,
        out_specs=[pl.BlockSpec((rb, W), lambda i: (i, 0)), pl.BlockSpec((3, W), lambda i: (0, 0)),
                   pl.BlockSpec((1, W), lambda i: (0, 0))],
        out_shape=[jax.ShapeDtypeStruct((T, W), BF16), jax.ShapeDtypeStruct((3, W), F32),
                   jax.ShapeDtypeStruct((1, W), F32)],
        scratch_shapes=[pltpu.VMEM((2, 2, rb + 16, LANES), F32), pltpu.VMEM((2, 2, re, LANES), F32)],
        compiler_params=_cp("arbitrary"))(*args)


def _lane_iota(shape):
    return lax.broadcasted_iota(jnp.int32, shape, len(shape) - 1)


def _dn_act(p, conv_w, alog_row, dtb_row, rb=256):
    T = p.shape[0]
    rb = min(rb, T)
    W3 = 3 * DN_WIDTH

    def body(p_ref, halo_ref, ba_ref, w_ref, al_ref, dt_ref, q_ref, k_ref, v_ref, bg_ref, ext_scr):
        first = pl.program_id(0) == 0
        outs = (q_ref, k_ref, v_ref)
        for j in range(3 * N_HEADS):
            kind, h = divmod(j, N_HEADS)
            cols = slice(j * HEAD_DIM, (j + 1) * HEAD_DIM)
            cur = p_ref[:, cols]
            ext_scr[j, 0:8] = jnp.where(first, 0.0, halo_ref[:, cols])
            ext_scr[j, 8:] = cur
            wv = w_ref[:, cols]
            s = _silu(ext_scr[j, 5:5 + rb] * wv[0:1] + ext_scr[j, 6:6 + rb] * wv[1:2]
                      + ext_scr[j, 7:7 + rb] * wv[2:3] + cur * wv[3:4])
            if kind < 2:
                scale = HEAD_DIM ** -0.5 if kind == 0 else 1.0
                s = s * (lax.rsqrt(jnp.sum(s * s, axis=-1, keepdims=True) + EPS) * scale)
            outs[kind][:, h * HEAD_DIM:(h + 1) * HEAD_DIM] = s
        ba = ba_ref[...]
        lane = _lane_iota(ba.shape)
        beta = _sigmoid(ba)
        g = -jnp.exp(al_ref[...]) * _softplus(ba + dt_ref[...])
        bg_ref[...] = jnp.where(lane < N_HEADS, beta, jnp.where(lane < 2 * N_HEADS, g, 0.0))

    row512 = pl.BlockSpec((rb, DN_WIDTH), lambda i: (i, 0))
    row128 = pl.BlockSpec((rb, 128), lambda i: (i, 0))
    vec128 = pl.BlockSpec((1, 128), lambda i: (0, 0))
    return pl.pallas_call(
        body, name="dn_act", grid=(T // rb,),
        in_specs=[pl.BlockSpec((rb, W3), lambda i: (i, 0)), _halo_prev_spec(rb, W3),
                  pl.BlockSpec((rb, 128), lambda i: (i, BA_COL // 128)),
                  pl.BlockSpec((4, W3), lambda i: (0, 0)), vec128, vec128],
        out_specs=[row512, row512, row512, row128],
        out_shape=[jax.ShapeDtypeStruct((T, DN_WIDTH), F32)] * 3 + [jax.ShapeDtypeStruct((T, 128), F32)],
        scratch_shapes=[pltpu.VMEM((3 * N_HEADS, rb + 8, HEAD_DIM), F32)],
        compiler_params=_cp("parallel"))(p, p, p, conv_w, alog_row, dtb_row)


def _dn_act_bwd(p, conv_w, alog_row, dtb_row, dq, dk, dv, dbg, dp_mid, rb=256):
    T = p.shape[0]
    rb = min(rb, T)
    nb = T // rb
    re = rb + 8
    W3 = 3 * DN_WIDTH

    def body(p_ref, prev_ref, next_ref, ba_ref, w_ref, al_ref, dt_ref, dq_ref, dqn_ref, dk_ref, dkn_ref,
             dv_ref, dvn_ref, dbg_ref, mid_ref, draw_ref, gw_ref, gad_ref, ext_scr, dc_scr):
        i = pl.program_id(0)
        draw_ref[:, W3:2 * W3] = mid_ref[...]

        @pl.when(i == 0)
        def _():
            gw_ref[...] = jnp.zeros_like(gw_ref)
            gad_ref[...] = jnp.zeros_like(gad_ref)
        row = lax.broadcasted_iota(jnp.int32, (re, 1), 0)
        live = (row < rb) | (i < nb - 1)
        d_refs = ((dq_ref, dqn_ref), (dk_ref, dkn_ref), (dv_ref, dvn_ref))
        for j in range(3 * N_HEADS):
            kind, h = divmod(j, N_HEADS)
            cols = slice(j * HEAD_DIM, (j + 1) * HEAD_DIM)
            hcols = slice(h * HEAD_DIM, (h + 1) * HEAD_DIM)
            ext_scr[j, 0:8] = jnp.where(i > 0, prev_ref[:, cols], 0.0)
            ext_scr[j, 8:8 + rb] = p_ref[:, cols]
            ext_scr[j, 8 + rb:] = next_ref[:, cols]
            tp = [ext_scr[j, 5 + k:5 + k + re] for k in range(4)]
            wv = w_ref[:, cols]
            c = tp[0] * wv[0:1] + tp[1] * wv[1:2] + tp[2] * wv[2:3] + tp[3] * wv[3:4]
            sg = _sigmoid(c)
            s = c * sg
            d_in = jnp.where(live, jnp.concatenate([d_refs[kind][0][:, hcols], d_refs[kind][1][:, hcols]], axis=0), 0.0)
            if kind < 2:
                scale = HEAD_DIM ** -0.5 if kind == 0 else 1.0
                n = lax.rsqrt(jnp.sum(s * s, axis=-1, keepdims=True) + EPS)
                hat = s * n
                d_in = (n * scale) * (d_in - hat * jnp.sum(hat * d_in, axis=-1, keepdims=True))
            dc = d_in * (sg + s * (1.0 - sg))
            dc_scr[j] = dc
            dcc = dc[0:rb]
            draw = (dcc * wv[3:4] + dc_scr[j, 1:1 + rb] * wv[2:3] + dc_scr[j, 2:2 + rb] * wv[1:2]
                    + dc_scr[j, 3:3 + rb] * wv[0:1])
            draw_ref[:, cols] = draw.astype(BF16)
            for k in range(4):
                gw_ref[k:k + 1, cols] += jnp.sum(tp[k][0:rb] * dcc, axis=0, keepdims=True)
        ba = ba_ref[...]
        dbg = dbg_ref[...]
        lane = _lane_iota(ba.shape)
        beta = _sigmoid(ba)
        ea = jnp.exp(al_ref[...])
        z = ba + dt_ref[...]
        d_a = dbg * (-ea) * _sigmoid(z)
        dba = jnp.where(lane < N_HEADS, dbg * beta * (1.0 - beta), jnp.where(lane < 2 * N_HEADS, d_a, 0.0))
        draw_ref[:, BA_COL:] = dba.astype(BF16)
        isg = (lane >= N_HEADS) & (lane < 2 * N_HEADS)
        g = -ea * _softplus(z)
        gad_ref[0:1, :] += jnp.sum(jnp.where(isg, dbg * g, 0.0), axis=0, keepdims=True)
        gad_ref[1:2, :] += jnp.sum(jnp.where(isg, d_a, 0.0), axis=0, keepdims=True)

    row512 = pl.BlockSpec((rb, DN_WIDTH), lambda i: (i, 0))
    row128 = pl.BlockSpec((rb, 128), lambda i: (i, 0))
    vec128 = pl.BlockSpec((1, 128), lambda i: (0, 0))
    next512 = _halo_next_spec(rb, DN_WIDTH, T)
    return pl.pallas_call(
        body, name="dn_act_bwd", grid=(nb,),
        in_specs=[pl.BlockSpec((rb, W3), lambda i: (i, 0)), _halo_prev_spec(rb, W3), _halo_next_spec(rb, W3, T),
                  pl.BlockSpec((rb, 128), lambda i: (i, BA_COL // 128)),
                  pl.BlockSpec((4, W3), lambda i: (0, 0)), vec128, vec128,
                  row512, next512, row512, next512, row512, next512, row128,
                  pl.BlockSpec((rb, W3), lambda i: (i, 0))],
        out_specs=[pl.BlockSpec((rb, PROJ_PAD), lambda i: (i, 0)),
                   pl.BlockSpec((4, W3), lambda i: (0, 0)), pl.BlockSpec((2, 128), lambda i: (0, 0))],
        out_shape=[jax.ShapeDtypeStruct((T, PROJ_PAD), BF16),
                   jax.ShapeDtypeStruct((4, W3), F32), jax.ShapeDtypeStruct((2, 128), F32)],
        scratch_shapes=[pltpu.VMEM((3 * N_HEADS, rb + 16, HEAD_DIM), F32), pltpu.VMEM((3 * N_HEADS, re, HEAD_DIM), F32)],
        compiler_params=_cp("arbitrary"))(p, p, p, p, conv_w, alog_row, dtb_row, dq, dq, dk, dk, dv, dv, dbg, dp_mid)


def _tri(incl):
    ii = lax.broadcasted_iota(jnp.int32, (CHUNK, CHUNK), 0)
    jj = lax.broadcasted_iota(jnp.int32, (CHUNK, CHUNK), 1)
    return ii, jj, ((ii >= jj) if incl else (ii > jj))


def _dn_chunk(k, bg, cb=4):
    T = k.shape[0]
    N = T // CHUNK
    cb = min(cb, N)

    def body(k_ref, bg_ref, gc_ref, gct_ref, l_ref):
        ii, jj, incl = _tri(True)
        tri = incl.astype(F32)
        U = range(cb)
        bgv = [bg_ref[u * CHUNK:(u + 1) * CHUNK, :] for u in U]
        gc = [jnp.dot(tri, bgv[u], precision=lax.Precision.HIGHEST, preferred_element_type=F32) for u in U]
        gct = [gc[u].T for u in U]
        kk = [[None] * N_HEADS for _ in U]
        for u in U:
            gc_ref[u * CHUNK:(u + 1) * CHUNK, :] = gc[u]
            gct_ref[u] = gct[u][0:8]
            for h in range(N_HEADS):
                kh = k_ref[u * CHUNK:(u + 1) * CHUNK, h * HEAD_DIM:(h + 1) * HEAD_DIM]
                kk[u][h] = _nt(kh * bgv[u][:, h:h + 1], kh)
        for u in U:
            for h in range(N_HEADS):
                gcol = gc[u][:, N_HEADS + h:N_HEADS + h + 1]
                grow = gct[u][N_HEADS + h:N_HEADS + h + 1, :]
                l_ref[u, h] = kk[u][h] * jnp.exp(jnp.where(ii > jj, gcol - grow, NEG))

    rows = cb * CHUNK
    return pl.pallas_call(
        body, name="dn_chunk", grid=(N // cb,),
        in_specs=[pl.BlockSpec((rows, DN_WIDTH), lambda n: (n, 0)), pl.BlockSpec((rows, 128), lambda n: (n, 0))],
        out_specs=[pl.BlockSpec((rows, 128), lambda n: (n, 0)), pl.BlockSpec((cb, 8, CHUNK), lambda n: (n, 0, 0)),
                   pl.BlockSpec((cb, N_HEADS, CHUNK, CHUNK), lambda n: (n, 0, 0, 0))],
        out_shape=[jax.ShapeDtypeStruct((T, 128), F32), jax.ShapeDtypeStruct((N, 8, CHUNK), F32),
                   jax.ShapeDtypeStruct((N, N_HEADS, CHUNK, CHUNK), F32)],
        compiler_params=_cp("parallel"))(k, bg)


def _tri_inv(lt):
    S = lt.shape[1]

    def body(l_ref, a_ref):
        col = lax.broadcasted_iota(jnp.int32, (CHUNK, S), 0)
        for i in range(CHUNK):
            def step(j, acc):
                return acc - l_ref[pl.ds(i * CHUNK + j, 1), :] * a_ref[j]
            a_ref[i] = lax.fori_loop(0, i, step, (col == i).astype(F32))

    return pl.pallas_call(
        body, name="tri_inv", out_shape=jax.ShapeDtypeStruct((CHUNK, CHUNK, S), F32),
        compiler_params=pltpu.CompilerParams(vmem_limit_bytes=VMEM_LIMIT))(lt)


def _dn_head_terms(qh, kh, vh, beta, gcol, grow):
    ii, jj, incl = _tri(True)
    gam = jnp.exp(jnp.where(incl, gcol - grow, NEG))
    glast = grow[:, CHUNK - 1:CHUNK]
    cd = jnp.exp(glast)
    shape = (CHUNK, HEAD_DIM)
    E = jnp.broadcast_to(jnp.exp(gcol), shape)
    Fd = jnp.broadcast_to(jnp.exp(glast - gcol), shape)
    beta = jnp.broadcast_to(beta, shape)
    kb = kh * beta
    return dict(ii=ii, jj=jj, gam=gam, E=E, F=Fd, beta=beta, cd=cd, kb=kb, vb=vh * beta, W=kb * E, qE=qh * E,
                kt=kh * Fd)


def _apply_a(a, u):
    hi, lo = _split(a)
    ub = _bf(u)
    return jnp.dot(hi, ub, preferred_element_type=F32) + jnp.dot(lo, ub, preferred_element_type=F32)


def _dn_scan(q, k, v, bg, gc, gct, a):
    T = q.shape[0]
    N = T // CHUNK

    cb = min(SCAN_CHUNKS, N)

    def body(q_ref, k_ref, v_ref, bg_ref, gc_ref, gct_ref, a_ref, o_ref, sall_ref, s_ref):
        @pl.when(pl.program_id(0) == 0)
        def _():
            s_ref[...] = jnp.zeros_like(s_ref)
        H = range(N_HEADS)
        sl = [slice(h * HEAD_DIM, (h + 1) * HEAD_DIM) for h in H]
        pre = []
        for u in range(cb):
            r = slice(u * CHUNK, (u + 1) * CHUNK)
            bgv, gcv, gctv = bg_ref[r, :], gc_ref[r, :], gct_ref[u]
            q_, k_ = [q_ref[r, s] for s in sl], [k_ref[r, s] for s in sl]
            t = [_dn_head_terms(q_[h], k_[h], v_ref[r, sl[h]], bgv[:, h:h + 1],
                                gcv[:, N_HEADS + h:N_HEADS + h + 1], gctv[N_HEADS + h:N_HEADS + h + 1, :]) for h in H]
            P = [_nt(q_[h], k_[h]) * t[h]["gam"] for h in H]
            pre.append((r, t, P))
        S = [s_ref[h] for h in H]
        for u in range(cb):
            r, t, P = pre[u]
            for h in H:
                sall_ref[u, h] = S[h]
            WS = [_nn(t[h]["W"], S[h]) for h in H]
            qS = [_nn(t[h]["qE"], S[h]) for h in H]
            vn = [_apply_a(a_ref[u, h], t[h]["vb"] - WS[h]) for h in H]
            Pv = [_nn(P[h], vn[h]) for h in H]
            kv = [_tn(t[h]["kt"], vn[h]) for h in H]
            for h in H:
                o_ref[r, sl[h]] = qS[h] + Pv[h]
            S = [t[h]["cd"] * S[h] + kv[h] for h in H]
        for h in H:
            s_ref[h] = S[h]

    row512 = pl.BlockSpec((cb * CHUNK, DN_WIDTH), lambda n: (n, 0))
    row128 = pl.BlockSpec((cb * CHUNK, 128), lambda n: (n, 0))
    return pl.pallas_call(
        body, name="dn_scan", grid=(N // cb,),
        in_specs=[row512, row512, row512, row128, row128, pl.BlockSpec((cb, 8, CHUNK), lambda n: (n, 0, 0)),
                  pl.BlockSpec((cb, N_HEADS, CHUNK, CHUNK), lambda n: (n, 0, 0, 0))],
        out_specs=[row512, pl.BlockSpec((cb, N_HEADS, HEAD_DIM, HEAD_DIM), lambda n: (n, 0, 0, 0))],
        out_shape=[jax.ShapeDtypeStruct((T, DN_WIDTH), F32),
                   jax.ShapeDtypeStruct((N, N_HEADS, HEAD_DIM, HEAD_DIM), F32)],
        scratch_shapes=[pltpu.VMEM((N_HEADS, HEAD_DIM, HEAD_DIM), F32)],
        compiler_params=_cp("arbitrary"))(q, k, v, bg, gc, gct, a)


def _dn_scan_bwd(q, k, v, bg, gc, gct, a, a_t, sall, do, dep=None):
    T = q.shape[0]
    N = T // CHUNK

    cb = min(SCAN_CHUNKS, N)
    nb = N // cb

    def body(q_ref, k_ref, v_ref, bg_ref, gc_ref, gct_ref, a_ref, at_ref, sall_ref, do_ref, *rest):
        dq_ref, dk_ref, dv_ref, dbg_ref, ds_ref = rest[-5:]
        @pl.when(pl.program_id(0) == 0)
        def _():
            ds_ref[...] = jnp.zeros_like(ds_ref)
        lane = _lane_iota((CHUNK, 128))
        rowi = lax.broadcasted_iota(jnp.int32, (CHUNK, 1), 0)
        ii, jj, _ = _tri(True)
        rev = (jj >= ii).astype(F32)
        H = range(N_HEADS)
        sl = [slice(h * HEAD_DIM, (h + 1) * HEAD_DIM) for h in H]
        pre = {}
        for u in reversed(range(cb)):
            r = slice(u * CHUNK, (u + 1) * CHUNK)
            bgv, gcv, gctv = bg_ref[r, :], gc_ref[r, :], gct_ref[u]
            q_, k_, v_ = [q_ref[r, s] for s in sl], [k_ref[r, s] for s in sl], [v_ref[r, s] for s in sl]
            dO = [do_ref[r, s] for s in sl]
            t = [_dn_head_terms(q_[h], k_[h], v_[h], bgv[:, h:h + 1], gcv[:, N_HEADS + h:N_HEADS + h + 1],
                                gctv[N_HEADS + h:N_HEADS + h + 1, :]) for h in H]
            beta = [t[h]["beta"] for h in H]
            S = [sall_ref[u, h] for h in H]
            A = [a_ref[u, h] for h in H]
            WS = [_nn(t[h]["W"], S[h]) for h in H]
            KK = [_nt(t[h]["kb"], k_[h]) for h in H]
            QK = [_nt(q_[h], k_[h]) for h in H]
            d_qE = [_nt(dO[h], S[h]) for h in H]
            vn = [_apply_a(A[h], t[h]["vb"] - WS[h]) for h in H]
            PtdO = [_tn(QK[h] * t[h]["gam"], dO[h]) for h in H]
            qEdO = [_tn(t[h]["qE"], dO[h]) for h in H]
            dOvn = [_nt(dO[h], vn[h]) for h in H]
            dQK = [jnp.where(ii >= jj, dOvn[h], 0.0) * t[h]["gam"] for h in H]
            dQKk = [_nn(dQK[h], k_[h]) for h in H]
            dQKq = [_tn(dQK[h], q_[h]) for h in H]
            pre[u] = (r, q_, k_, v_, beta, t, S, A, KK, QK, d_qE, vn, PtdO, qEdO, dQK, dQKk, dQKq)
        dSn = [ds_ref[h] for h in H]
        for u in reversed(range(cb)):
            r, q_, k_, v_, beta, t, S, A, KK, QK, d_qE, vn, PtdO, qEdO, dQK, dQKk, dQKq = pre[u]
            gam, E, Fd, cd, kb = ([t[h][n] for h in H] for n in ("gam", "E", "F", "cd", "kb"))
            ktdS = [_nn(t[h]["kt"], dSn[h]) for h in H]
            dU = [_apply_a(at_ref[u, h], PtdO[h] + ktdS[h]) for h in H]
            d_kt = [_nt(vn[h], dSn[h]) for h in H]
            dUvn = [_nt(dU[h], vn[h]) for h in H]
            dUS = [_nt(dU[h], S[h]) for h in H]
            WdU = [_tn(t[h]["W"], dU[h]) for h in H]
            d_cd = [jnp.sum(S[h] * dSn[h]) for h in H]
            dSn = [cd[h] * dSn[h] + qEdO[h] - WdU[h] for h in H]
            dKK = [jnp.where(ii > jj, -dUvn[h], 0.0) * gam[h] for h in H]
            dKKk = [_nn(dKK[h], k_[h]) for h in H]
            dKKkb = [_tn(dKK[h], kb[h]) for h in H]
            dbeta_arr = jnp.zeros((CHUNK, 128), F32)
            dgc_arr = jnp.zeros((CHUNK, 128), F32)
            for h in H:
                dW = -dUS[h]
                dq_ref[r, sl[h]] = dQKk[h] + d_qE[h] * E[h]
                d_kb = dKKk[h] + dW * E[h]
                dk_ref[r, sl[h]] = dQKq[h] + dKKkb[h] + d_kb * beta[h] + d_kt[h] * Fd[h]
                dv_ref[r, sl[h]] = dU[h] * beta[h]
                Z = dQK[h] * QK[h] + dKK[h] * KK[h]
                dbeta = jnp.sum(dU[h] * v_[h] + d_kb * k_[h], axis=-1, keepdims=True)
                m_e = (dW * kb[h] + d_qE[h] * q_[h]) * E[h]
                m_f = d_kt[h] * k_[h] * Fd[h]
                zdiag = jnp.where(ii == jj, jnp.sum(Z, axis=0, keepdims=True), 0.0)
                dgc = (jnp.sum(m_e - m_f, axis=-1, keepdims=True) + jnp.sum(Z - zdiag, axis=-1, keepdims=True)
                       + jnp.where(rowi == CHUNK - 1, jnp.sum(m_f) + d_cd[h] * cd[h], 0.0))
                dbeta_arr = dbeta_arr + jnp.where(lane == h, dbeta, 0.0)
                dgc_arr = dgc_arr + jnp.where(lane == N_HEADS + h, dgc, 0.0)
            dbg_ref[r, :] = dbeta_arr + jnp.dot(rev, dgc_arr, precision=lax.Precision.HIGHEST,
                                                preferred_element_type=F32)
        for h in H:
            ds_ref[h] = dSn[h]

    row512 = pl.BlockSpec((cb * CHUNK, DN_WIDTH), lambda n: (nb - 1 - n, 0))
    row128 = pl.BlockSpec((cb * CHUNK, 128), lambda n: (nb - 1 - n, 0))
    in_specs, args = _with_dep(
        [row512, row512, row512, row128, row128,
         pl.BlockSpec((cb, 8, CHUNK), lambda n: (nb - 1 - n, 0, 0)),
         pl.BlockSpec((cb, N_HEADS, CHUNK, CHUNK), lambda n: (nb - 1 - n, 0, 0, 0)),
         pl.BlockSpec((cb, N_HEADS, CHUNK, CHUNK), lambda n: (nb - 1 - n, 0, 0, 0)),
         pl.BlockSpec((cb, N_HEADS, HEAD_DIM, HEAD_DIM), lambda n: (nb - 1 - n, 0, 0, 0)), row512],
        [q, k, v, bg, gc, gct, a, a_t, sall, do], dep)
    return pl.pallas_call(
        body, name="dn_scan_bwd", grid=(nb,), in_specs=in_specs,
        out_specs=[row512, row512, row512, row128],
        out_shape=[jax.ShapeDtypeStruct((T, DN_WIDTH), F32)] * 3 + [jax.ShapeDtypeStruct((T, 128), F32)],
        scratch_shapes=[pltpu.VMEM((N_HEADS, HEAD_DIM, HEAD_DIM), F32)],
        compiler_params=_cp("arbitrary"))(*args)


def _sg_mask():
    ii = lax.broadcasted_iota(jnp.int32, (SG_BLOCK, SG_BLOCK), 0) // CHUNK
    jj = lax.broadcasted_iota(jnp.int32, (SG_BLOCK, SG_BLOCK), 1) // CHUNK
    return jj <= ii


def _mix_fwd(o, p, ong, sgn, sgw, sgbt):
    T = o.shape[0]
    rb = SG_BLOCK

    def body(o_ref, gate_ref, u_ref, vg_ref, ong_ref, sgn_ref, sgw_ref, sgbt_ref, mix_ref):
        mask = _sg_mask()
        gate = gate_ref[...]
        for h in range(N_HEADS):
            sl = slice(h * HEAD_DIM, (h + 1) * HEAD_DIM)
            oh = o_ref[:, sl]
            r = lax.rsqrt(jnp.mean(oh * oh, axis=-1, keepdims=True) + EPS)
            mix_ref[:, sl] = (oh * r * ong_ref[...] * _silu(gate[:, sl])).astype(BF16)
        for gi in range(SG_GROUPS):
            sl = slice(gi * SG_BLOCK, (gi + 1) * SG_BLOCK)
            gv = _gelu(vg_ref[:, sl])
            r = lax.rsqrt(jnp.mean(gv * gv, axis=-1, keepdims=True) + EPS)
            vh = gv * r * sgn_ref[:, sl]
            s = _nn(jnp.where(mask, sgw_ref[gi], 0.0), vh) + sgbt_ref[:, gi:gi + 1]
            mix_ref[:, DN_WIDTH + gi * SG_BLOCK:DN_WIDTH + (gi + 1) * SG_BLOCK] = (_gelu(u_ref[:, sl]) * s).astype(BF16)

    def col(c):
        return pl.BlockSpec((rb, 512), lambda i: (i, c))
    return pl.pallas_call(
        body, name="mix_fwd", grid=(T // rb,),
        in_specs=[pl.BlockSpec((rb, DN_WIDTH), lambda i: (i, 0)), col(3), col(4), col(5),
                  pl.BlockSpec((1, 128), lambda i: (0, 0)), pl.BlockSpec((1, SG_WIDTH), lambda i: (0, 0)),
                  pl.BlockSpec((SG_GROUPS, SG_BLOCK, SG_BLOCK), lambda i: (0, 0, 0)),
                  pl.BlockSpec((SG_BLOCK, 128), lambda i: (0, 0))],
        out_specs=pl.BlockSpec((rb, D_MODEL), lambda i: (i, 0)),
        out_shape=jax.ShapeDtypeStruct((T, D_MODEL), BF16),
        compiler_params=_cp("parallel"))(o, p, p, p, ong, sgn, sgw, sgbt)


def _mix_bwd(o, p, ong, sgn, sgw, sgbt, dmix, dep=None):
    T = o.shape[0]
    rb = SG_BLOCK

    def body(o_ref, gate_ref, u_ref, vg_ref, ong_ref, sgn_ref, sgw_ref, sgbt_ref, dmix_ref, *rest):
        do_ref, dp_ref, gong_ref, gsgn_ref, gsgw_ref, gsgbt_ref = rest[-6:]
        @pl.when(pl.program_id(0) == 0)
        def _():
            gong_ref[...] = jnp.zeros_like(gong_ref)
            gsgn_ref[...] = jnp.zeros_like(gsgn_ref)
            gsgw_ref[...] = jnp.zeros_like(gsgw_ref)
            gsgbt_ref[...] = jnp.zeros_like(gsgbt_ref)
        mask = _sg_mask()
        gate = gate_ref[...]
        lane = _lane_iota((SG_BLOCK, 128))
        for h in range(N_HEADS):
            sl = slice(h * HEAD_DIM, (h + 1) * HEAD_DIM)
            oh = o_ref[:, sl]
            dm = dmix_ref[:, sl]
            r = lax.rsqrt(jnp.mean(oh * oh, axis=-1, keepdims=True) + EPS)
            oh_hat = oh * r
            gt = gate[:, sl]
            sg = _silu(gt)
            dp_ref[:, sl] = (dm * oh_hat * ong_ref[...] * _dsilu(gt)).astype(BF16)
            dn_ = dm * sg
            gong_ref[...] += jnp.sum(dn_ * oh_hat, axis=0, keepdims=True)
            dhat = dn_ * ong_ref[...]
            do_ref[:, sl] = r * (dhat - oh_hat * jnp.mean(dhat * oh_hat, axis=-1, keepdims=True))
        for gi in range(SG_GROUPS):
            sl = slice(gi * SG_BLOCK, (gi + 1) * SG_BLOCK)
            vraw = vg_ref[:, sl]
            gv = _gelu(vraw)
            r = lax.rsqrt(jnp.mean(gv * gv, axis=-1, keepdims=True) + EPS)
            vhat = gv * r
            vn = vhat * sgn_ref[:, sl]
            wm = jnp.where(mask, sgw_ref[gi], 0.0)
            s = _nn(wm, vn) + sgbt_ref[:, gi:gi + 1]
            uraw = u_ref[:, sl]
            dm = dmix_ref[:, DN_WIDTH + gi * SG_BLOCK:DN_WIDTH + (gi + 1) * SG_BLOCK]
            dp_ref[:, DN_WIDTH + gi * SG_BLOCK:DN_WIDTH + (gi + 1) * SG_BLOCK] = (dm * s * _dgelu(uraw)).astype(BF16)
            ds = dm * _gelu(uraw)
            gsgbt_ref[...] += jnp.where(lane == gi, jnp.sum(ds, axis=-1, keepdims=True), 0.0)
            gsgw_ref[gi] += jnp.where(mask, _nt(ds, vn), 0.0)
            dvn = _tn(wm, ds)
            gsgn_ref[:, sl] += jnp.sum(dvn * vhat, axis=0, keepdims=True)
            dhat = dvn * sgn_ref[:, sl]
            dgv = r * (dhat - vhat * jnp.mean(dhat * vhat, axis=-1, keepdims=True))
            dp_ref[:, 2 * DN_WIDTH + gi * SG_BLOCK:2 * DN_WIDTH + (gi + 1) * SG_BLOCK] = (dgv * _dgelu(vraw)).astype(BF16)

    def col(c):
        return pl.BlockSpec((rb, 512), lambda i: (i, c))
    full = lambda *s: pl.BlockSpec(s, lambda i: (0,) * len(s))
    in_specs, args = _with_dep(
        [pl.BlockSpec((rb, DN_WIDTH), lambda i: (i, 0)), col(3), col(4), col(5),
         full(1, 128), full(1, SG_WIDTH), full(SG_GROUPS, SG_BLOCK, SG_BLOCK), full(SG_BLOCK, 128),
         pl.BlockSpec((rb, D_MODEL), lambda i: (i, 0))],
        [o, p, p, p, ong, sgn, sgw, sgbt, dmix], dep)
    return pl.pallas_call(
        body, name="mix_bwd", grid=(T // rb,), in_specs=in_specs,
        out_specs=[pl.BlockSpec((rb, DN_WIDTH), lambda i: (i, 0)), pl.BlockSpec((rb, 3 * 512), lambda i: (i, 0)),
                   full(1, 128), full(1, SG_WIDTH), full(SG_GROUPS, SG_BLOCK, SG_BLOCK), full(SG_BLOCK, 128)],
        out_shape=[jax.ShapeDtypeStruct((T, DN_WIDTH), F32), jax.ShapeDtypeStruct((T, 3 * 512), BF16),
                   jax.ShapeDtypeStruct((1, 128), F32), jax.ShapeDtypeStruct((1, SG_WIDTH), F32),
                   jax.ShapeDtypeStruct((SG_GROUPS, SG_BLOCK, SG_BLOCK), F32),
                   jax.ShapeDtypeStruct((SG_BLOCK, 128), F32)],
        compiler_params=_cp("arbitrary"))(*args)


def _pad_lanes(row, offset=0):
    n = row.shape[1]
    return jnp.pad(row, ((0, 0), (offset, 128 - n - offset)))


def _local_step(x, tgt, w, dep=None, late_weights=None, on_grad=None):
    T = x.shape[0]
    N = T // CHUNK
    on_grad = on_grad or (lambda name, g: None)
    alog_row = _pad_lanes(w["dn_a_log"], N_HEADS)
    dtb_row = _pad_lanes(w["dn_dt_bias"], N_HEADS)
    sgbt = jnp.pad(w["sg_b"].T, ((0, 0), (0, 128 - SG_GROUPS)))

    p, h1, w_in_pad = _in_proj(x, w["attn_norm_g"], w["w_in"], dep=dep)
    q, k, v, bg = _dn_act(p, w["dn_conv_w"], alog_row, dtb_row)
    gc, gct, lmat = _dn_chunk(k, bg)
    lt = lmat.reshape(N * N_HEADS, CHUNK * CHUNK).T
    at = _tri_inv(lt)
    a = at.reshape(CHUNK * CHUNK, N * N_HEADS).T.reshape(N, N_HEADS, CHUNK, CHUNK)
    a_t = at.transpose(1, 0, 2).reshape(CHUNK * CHUNK, N * N_HEADS).T.reshape(N, N_HEADS, CHUNK, CHUNK)
    o, sall = _dn_scan(q, k, v, bg, gc, gct, a)
    mix = _mix_fwd(o, p, w["dn_out_norm_g"], w["sg_norm_g"], w["sg_w"], sgbt)
    if late_weights is not None:
        w = {**w, **late_weights(mix)}
    x2, h2 = _out_proj(mix, w["w_out"], x, w["ffn_norm_g"])
    up = _mm_nn("up_proj", h2, w["w_up"], F32, 512, D_FF)
    act = _ffn_act(up, w["ffn_conv_w"], w["ffn_conv_b"])
    loss, dx3, g_final = _down_proj_loss(act, w["w_down"], x2, tgt, w["final_norm_g"])

    dact = _mm_nt("d_act", dx3, w["w_down"], F32, 512, D_FF)
    g_w_down = _mm_tn("g_w_down", act, dx3, D_FF, 1024, 512)
    tok = on_grad("w_down", g_w_down)
    dup, g_ffn_conv_w, g_ffn_conv_b = _ffn_act_bwd(up, dact, w["ffn_conv_w"], w["ffn_conv_b"], dep=tok)
    g_w_up = _mm_tn("g_w_up", h2, dup, 1024, 2 * D_FF // 4, 512, col_major_tiles=True)
    tok = on_grad("w_up", g_w_up)
    dx2, g_ffn_norm = _mm_nt_rms_bwd("d_h2", dup, w["w_up"], x2, w["ffn_norm_g"], dx3, dep=tok)
    dmix = _mm_nt("d_mix", dx2, w["w_out"], F32, 512, 1024)
    g_w_out = _mm_tn("g_w_out", mix, dx2, 1024, 1024, 1024)
    tok = on_grad("w_out", g_w_out)
    do, dp_mid, g_ong, g_sgn, g_sgw, g_sgbt = _mix_bwd(o, p, w["dn_out_norm_g"], w["sg_norm_g"], w["sg_w"], sgbt,
                                                      dmix, dep=tok)
    early = dict(dn_out_norm_g=g_ong, sg_norm_g=g_sgn, sg_w=g_sgw, sg_b=g_sgbt[:, :SG_GROUPS].T,
                 ffn_norm_g=g_ffn_norm, ffn_conv_w=g_ffn_conv_w, ffn_conv_b=g_ffn_conv_b, final_norm_g=g_final)
    tok = on_grad("small_early", early)
    dq, dk, dv, dbg = _dn_scan_bwd(q, k, v, bg, gc, gct, a, a_t, sall, do, dep=tok)
    dp, g_dn_conv_w, g_ad = _dn_act_bwd(p, w["dn_conv_w"], alog_row, dtb_row, dq, dk, dv, dbg, dp_mid)
    g_w_in = _mm_tn("g_w_in", h1, dp, 1024, PROJ_PAD, 512, col_groups=(4, PROJ_COLS // 4))
    tok = on_grad("w_in", g_w_in)
    grad_x, g_attn_norm = _mm_nt_rms_bwd("d_h1", dp, w_in_pad, x, w["attn_norm_g"], dx2, dep=tok)

    grads = dict(
        attn_norm_g=g_attn_norm, w_in=g_w_in, dn_conv_w=g_dn_conv_w,
        dn_a_log=g_ad[0:1, N_HEADS:2 * N_HEADS], dn_dt_bias=g_ad[1:2, N_HEADS:2 * N_HEADS],
        w_out=g_w_out, w_up=g_w_up, w_down=g_w_down, **early)
    return loss, grad_x, grads


def _me():
    return lax.axis_index("x"), lax.axis_index("y"), lax.axis_index("c")


def _peer(rel):
    x, y, c = _me()
    return {"x": (1 - x, y, c), "y": (x, 1 - y, c), "xy": (1 - x, 1 - y, c), "c": (x, y, 1 - c)}[rel]


def _chip_of(dev):
    return 2 * dev[0] + dev[1]


CHIP_RELS = ("x", "y", "xy")


def _run_copies(copies, sends, recvs):
    for cp in copies:
        cp.start()
    for cp in recvs:
        cp.wait_recv()
    for cp in sends:
        cp.wait_send()


def _gather_first(w_shard, small_shard):
    R = w_shard.shape[0]
    r2 = R // 2

    def body(w_ref, s_ref, w_out, s_out, send_sems, recv_sems):
        x, y, c = _me()
        me = _chip_of((x, y))
        sib = _peer("c")

        def half(chip, core):
            return w_out.at[chip, pl.ds(pl.multiple_of(core * r2, 8), r2), :]

        def copy(k, src, dst, to):
            return pltpu.make_async_remote_copy(src_ref=src, dst_ref=dst, send_sem=send_sems.at[k],
                                                recv_sem=recv_sems.at[k], device_id=to, device_id_type=MESH)

        own_rows = w_ref.at[pl.ds(pl.multiple_of(c * r2, 8), r2), :]
        first = [copy(r, own_rows, half(me, c), _peer(rel)) for r, rel in enumerate(CHIP_RELS)]
        first += [copy(3 + r, s_ref, s_out.at[me], _peer(rel)) for r, rel in enumerate(CHIP_RELS)]
        for cp in first:
            cp.start()
        passed = []
        for r, rel in enumerate(CHIP_RELS):
            their = _chip_of(_peer(rel))
            copy(r, own_rows, half(their, c), _peer(rel)).wait_recv()
            fwd = copy(6 + r, half(their, c), half(their, c), sib)
            fwd.start()
            passed.append(fwd)
        for r, rel in enumerate(CHIP_RELS):
            their = _chip_of(_peer(rel))
            copy(3 + r, s_ref, s_out.at[their], _peer(rel)).wait_recv()
            copy(6 + r, own_rows, half(their, 1 - c), sib).wait_recv()
        for cp in first + passed:
            cp.wait_send()

    w_all, s_all = pl.pallas_call(
        body, name="gather_first", in_specs=[ANY, ANY], out_specs=[ANY, ANY],
        out_shape=[jax.ShapeDtypeStruct((4,) + w_shard.shape, w_shard.dtype),
                   jax.ShapeDtypeStruct((4,) + small_shard.shape, small_shard.dtype)],
        scratch_shapes=[pltpu.SemaphoreType.DMA((9,)), pltpu.SemaphoreType.DMA((9,))])(w_shard, small_shard)
    me = _chip_of(_me())
    return (lax.dynamic_update_index_in_dim(w_all, w_shard, me, 0),
            lax.dynamic_update_index_in_dim(s_all, small_shard, me, 0))


OTHERS = tuple((fx, fy, fc) for fx in (0, 1) for fy in (0, 1) for fc in (0, 1) if (fx, fy, fc) != (0, 0, 0))


def _other(flip):
    x, y, c = _me()
    return (x ^ flip[0], y ^ flip[1], c ^ flip[2])


def _linear(dev):
    return 4 * dev[0] + 2 * dev[1] + dev[2]


def _exchange_small(small):
    def body(small_ref, out_ref, send_sems, recv_sems):
        my_slot = _linear(_me())
        sends, recvs = [], []
        for k, flip in enumerate(OTHERS):
            peer = _other(flip)
            sends.append(pltpu.make_async_remote_copy(
                src_ref=small_ref, dst_ref=out_ref.at[my_slot], send_sem=send_sems.at[k], recv_sem=recv_sems.at[k],
                device_id=peer, device_id_type=MESH))
            recvs.append(pltpu.make_async_remote_copy(
                src_ref=small_ref, dst_ref=out_ref.at[_linear(peer)], send_sem=send_sems.at[k],
                recv_sem=recv_sems.at[k], device_id=peer, device_id_type=MESH))
        _run_copies(sends, sends, recvs)

    out = pl.pallas_call(
        body, name="exchange_small", in_specs=[ANY], out_specs=ANY,
        out_shape=jax.ShapeDtypeStruct((8,) + small.shape, small.dtype),
        scratch_shapes=[pltpu.SemaphoreType.DMA((7,)), pltpu.SemaphoreType.DMA((7,))])(small)
    return lax.dynamic_update_index_in_dim(out, small, _linear(_me()), 0)


def _pair_swap(halves):
    n = len(halves)

    def body(*refs):
        src, out = refs[:n], refs[n:2 * n]
        send_sems, recv_sems = refs[2 * n:]
        sib = _peer("c")
        copies = [pltpu.make_async_remote_copy(
            src_ref=src[i], dst_ref=out[i], send_sem=send_sems.at[i], recv_sem=recv_sems.at[i],
            device_id=sib, device_id_type=MESH) for i in range(n)]
        _run_copies(copies, copies, copies)

    return pl.pallas_call(
        body, name="pair_swap", in_specs=[ANY] * n, out_specs=[ANY] * n,
        out_shape=[jax.ShapeDtypeStruct(h.shape, h.dtype) for h in halves],
        scratch_shapes=[pltpu.SemaphoreType.DMA((n,)), pltpu.SemaphoreType.DMA((n,))])(*halves)


HBM = pl.BlockSpec(memory_space=pltpu.HBM)
SEM = pl.BlockSpec(memory_space=pltpu.SEMAPHORE)
EFFECT = pltpu.SideEffectType.DATAFLOW_SIDE_EFFECTING


def _hbm(a):
    return pltpu.with_memory_space_constraint(a, pltpu.HBM)


def _transfer_start(name, srcs, lands, n_copies, make_copies, after=None):
    n, m = len(srcs), len(lands)

    def body(*refs):
        src, land = refs[:n], refs[n:n + m]
        outs = refs[n + m + (after is not None):]
        send_sems, recv_sems, token = outs[0], outs[1], outs[-1]
        for cp in make_copies(src, land, send_sems, recv_sems):
            cp.start()
        token[...] = jnp.zeros_like(token)

    arrs = list(srcs) + list(lands)
    in_specs, args = _with_dep([HBM] * (n + m), [_hbm(a) for a in arrs], after)
    out = pl.pallas_call(
        body, name=name,
        out_shape=(pltpu.SemaphoreType.DMA((n_copies,)), pltpu.SemaphoreType.DMA((n_copies,)),
                   *[pltpu.HBM(a.shape, a.dtype) for a in arrs], jax.ShapeDtypeStruct((8, 128), F32)),
        in_specs=in_specs,
        out_specs=(SEM, SEM, *[HBM] * (n + m), pl.BlockSpec(memory_space=pltpu.VMEM)),
        input_output_aliases={i: 2 + i for i in range(n + m)},
        compiler_params=pltpu.CompilerParams(has_side_effects=EFFECT))(*args)
    return out[0], out[1], list(out[2:2 + n]), list(out[2 + n:2 + n + m]), out[-1]


def _transfer_wait(name, send_sems, recv_sems, srcs, lands, make_copies, after):
    n, m = len(srcs), len(lands)

    def body(*refs):
        src, land = refs[:n], refs[n:n + m]
        s_sems, r_sems = refs[n + m], refs[n + m + 1]
        for cp in make_copies(src, land, s_sems, r_sems):
            cp.wait_send()
            cp.wait_recv()

    arrs = list(srcs) + list(lands)
    out = pl.pallas_call(
        body, name=name, out_shape=tuple(pltpu.HBM(a.shape, a.dtype) for a in arrs),
        in_specs=[HBM] * (n + m) + [SEM, SEM, ANY], out_specs=tuple([HBM] * (n + m)),
        input_output_aliases={i: i for i in range(n + m)},
        compiler_params=pltpu.CompilerParams(has_side_effects=EFFECT))(*arrs, send_sems, recv_sems, after)
    return list(out[:n]), list(out[n:])


def _gather_copies(src, land, send_sems, recv_sems):
    me = _chip_of(_me())
    copies = []
    for i in range(len(src)):
        for r, rel in enumerate(CHIP_RELS):
            k = 3 * i + r
            copies.append(pltpu.make_async_remote_copy(
                src_ref=src[i], dst_ref=land[i].at[me], send_sem=send_sems.at[k], recv_sem=recv_sems.at[k],
                device_id=_peer(rel), device_id_type=MESH))
    return copies


def _small_copies(src, land, send_sems, recv_sems):
    my_slot = _linear(_me())
    return [pltpu.make_async_remote_copy(
        src_ref=src[0], dst_ref=land[0].at[my_slot], send_sem=send_sems.at[k], recv_sem=recv_sems.at[k],
        device_id=_other(flip), device_id_type=MESH) for k, flip in enumerate(OTHERS)]


def _pieces_copies(src, land, send_sems, recv_sems):
    copies = []
    for k, flip in enumerate(OTHERS):
        peer = _other(flip)
        copies.append(pltpu.make_async_remote_copy(
            src_ref=src[0].at[_linear(peer)], dst_ref=land[0].at[k], send_sem=send_sems.at[k],
            recv_sem=recv_sems.at[k], device_id=peer, device_id_type=MESH))
    return copies


def _row_block(rows, cols, budget=2 * 1024 * 1024):
    rb = max(8, (budget // (4 * cols)) // 8 * 8)
    while rows % rb:
        rb -= 8
    return rb if rb > 0 else rows


def _sum_slots(name, first, rest):
    R, Cc = first.shape
    K = rest.shape[0]
    rb = _row_block(R, Cc)

    def body(f_ref, r_ref, o_ref):
        acc = f_ref[...].astype(F32)
        for j in range(K):
            acc = acc + r_ref[j].astype(F32)
        o_ref[...] = acc

    return pl.pallas_call(
        body, name=name, grid=(R // rb,),
        in_specs=[pl.BlockSpec((rb, Cc), lambda i: (i, 0)), pl.BlockSpec((K, rb, Cc), lambda i: (0, i, 0))],
        out_specs=pl.BlockSpec((rb, Cc), lambda i: (i, 0)),
        out_shape=jax.ShapeDtypeStruct((R, Cc), F32), compiler_params=_cp("parallel"))(first, rest)


def _adamw_math(w, gv, m, v):
    mn = ADAM_B1 * m + (1.0 - ADAM_B1) * gv
    vn = ADAM_B2 * v + (1.0 - ADAM_B2) * (gv * gv)
    m_hat = mn / (1.0 - ADAM_B1 ** ADAM_STEP)
    v_hat = vn / (1.0 - ADAM_B2 ** ADAM_STEP)
    return -ADAM_LR * (m_hat / (jnp.sqrt(v_hat) + ADAM_EPS) + ADAM_WD * w), mn, vn


def _adamw_halves(name, w, mine, theirs, m, v, core):
    R, Cc = w.shape
    r2 = R // 2
    rb = _row_block(r2, Cc, 1024 * 1024)
    nb2 = r2 // rb

    def body(c_ref, w_ref, mine_ref, theirs_ref, m_ref, v_ref, g_ref, d_ref, mo_ref, vo_ref):
        is_mine = (pl.program_id(0) // nb2) == c_ref[0]
        gv = jnp.where(is_mine, mine_ref[...], theirs_ref[...])
        g_ref[...] = gv
        d_ref[...], mo_ref[...], vo_ref[...] = _adamw_math(w_ref[...], gv, m_ref[...], v_ref[...])

    blk = pl.BlockSpec((rb, Cc), lambda i, c: (i, 0))
    half = lambda own: pl.BlockSpec(
        (rb, Cc), lambda i, c: (jnp.clip(i - (c[0] if own else 1 - c[0]) * nb2, 0, nb2 - 1), 0))
    return pl.pallas_call(
        body, name=name,
        grid_spec=pltpu.PrefetchScalarGridSpec(
            num_scalar_prefetch=1, grid=(2 * nb2,), in_specs=[blk, half(True), half(False), blk, blk],
            out_specs=[blk] * 4),
        out_shape=[jax.ShapeDtypeStruct((R, Cc), F32)] * 4, compiler_params=_cp("parallel"))(core, w, mine, theirs, m, v)


def _adamw(name, w, g, m, v):
    R, Cc = w.shape
    rb = _row_block(R, Cc, 1024 * 1024)

    def body(w_ref, g_ref, m_ref, v_ref, d_ref, mo_ref, vo_ref):
        d_ref[...], mo_ref[...], vo_ref[...] = _adamw_math(w_ref[...], g_ref[...], m_ref[...], v_ref[...])

    blk = pl.BlockSpec((rb, Cc), lambda i: (i, 0))
    return pl.pallas_call(
        body, name=name, grid=(R // rb,), in_specs=[blk] * 4, out_specs=[blk] * 3,
        out_shape=[jax.ShapeDtypeStruct((R, Cc), F32)] * 3, compiler_params=_cp("parallel"))(w, g, m, v)


def _pack(arrs):
    rows = []
    for a in arrs:
        flat = a.reshape(-1)
        pad = (-flat.shape[0]) % 128
        rows.append(jnp.pad(flat, (0, pad)).reshape(-1, 128))
    buf = jnp.concatenate(rows, axis=0)
    return jnp.pad(buf, ((0, (-buf.shape[0]) % 8), (0, 0)))


def _unpack(buf, shapes):
    out, r = [], 0
    for s in shapes:
        n = math.prod(s)
        nr = -(-n // 128)
        out.append(buf[r:r + nr].reshape(-1)[:n].reshape(s))
        r += nr
    return out


BIG = ("w_in", "w_out", "w_up", "w_down")
CONV = ("dn_conv_w", "ffn_conv_w")
REPL = ("attn_norm_g", "dn_a_log", "dn_dt_bias", "dn_out_norm_g", "sg_norm_g", "sg_w", "sg_b",
        "ffn_norm_g", "ffn_conv_b", "final_norm_g")
ORDER = ("attn_norm_g", "w_in", "dn_conv_w", "dn_a_log", "dn_dt_bias", "dn_out_norm_g", "sg_norm_g", "sg_w",
         "sg_b", "w_out", "ffn_norm_g", "w_up", "ffn_conv_w", "ffn_conv_b", "w_down", "final_norm_g")


def kernel(x, attn_norm_g, w_in, dn_conv_w, dn_a_log, dn_dt_bias, dn_out_norm_g, sg_norm_g, sg_w, sg_b, w_out, ffn_norm_g, w_up, ffn_conv_w, ffn_conv_b, w_down, final_norm_g, loss_target, m_attn_norm_g, m_w_in, m_dn_conv_w, m_dn_a_log, m_dn_dt_bias, m_dn_out_norm_g, m_sg_norm_g, m_sg_w, m_sg_b, m_w_out, m_ffn_norm_g, m_w_up, m_ffn_conv_w, m_ffn_conv_b, m_w_down, m_final_norm_g, v_attn_norm_g, v_w_in, v_dn_conv_w, v_dn_a_log, v_dn_dt_bias, v_dn_out_norm_g, v_sg_norm_g, v_sg_w, v_sg_b, v_w_out, v_ffn_norm_g, v_w_up, v_ffn_conv_w, v_ffn_conv_b, v_w_down, v_final_norm_g):
    W = dict(attn_norm_g=attn_norm_g, w_in=w_in, dn_conv_w=dn_conv_w, dn_a_log=dn_a_log, dn_dt_bias=dn_dt_bias,
             dn_out_norm_g=dn_out_norm_g, sg_norm_g=sg_norm_g, sg_w=sg_w, sg_b=sg_b, w_out=w_out,
             ffn_norm_g=ffn_norm_g, w_up=w_up, ffn_conv_w=ffn_conv_w, ffn_conv_b=ffn_conv_b, w_down=w_down,
             final_norm_g=final_norm_g)
    Mo = dict(attn_norm_g=m_attn_norm_g, w_in=m_w_in, dn_conv_w=m_dn_conv_w, dn_a_log=m_dn_a_log,
              dn_dt_bias=m_dn_dt_bias, dn_out_norm_g=m_dn_out_norm_g, sg_norm_g=m_sg_norm_g, sg_w=m_sg_w,
              sg_b=m_sg_b, w_out=m_w_out, ffn_norm_g=m_ffn_norm_g, w_up=m_w_up, ffn_conv_w=m_ffn_conv_w,
              ffn_conv_b=m_ffn_conv_b, w_down=m_w_down, final_norm_g=m_final_norm_g)
    Vo = dict(attn_norm_g=v_attn_norm_g, w_in=v_w_in, dn_conv_w=v_dn_conv_w, dn_a_log=v_dn_a_log,
              dn_dt_bias=v_dn_dt_bias, dn_out_norm_g=v_dn_out_norm_g, sg_norm_g=v_sg_norm_g, sg_w=v_sg_w,
              sg_b=v_sg_b, w_out=v_w_out, ffn_norm_g=v_ffn_norm_g, w_up=v_w_up, ffn_conv_w=v_ffn_conv_w,
              ffn_conv_b=v_ffn_conv_b, w_down=v_w_down, final_norm_g=v_final_norm_g)
    xi, yi, ci = lax.axis_index("x"), lax.axis_index("y"), lax.axis_index("c")
    chip = 2 * xi + yi

    me_lin = 4 * xi + 2 * yi + ci

    g_in, g_dnc = _gather_first(w_in[0].astype(BF16), dn_conv_w[0])
    late = ("w_out", "w_up", "w_down", "ffn_conv_w")
    late_shards = [W[n][0].astype(BF16) for n in late[:3]] + [ffn_conv_w[0]]
    late_lands = [lax.dynamic_update_index_in_dim(lax.empty((4,) + s.shape, s.dtype), s, chip, 0) for s in late_shards]
    n_late = 3 * len(late_shards)
    ssem, rsem, late_src, late_lands, token = _transfer_start("gather_rest_start", late_shards, late_lands,
                                                              n_late, _gather_copies, after=g_in)

    def late_weights(after):
        _, (g_out, g_up, g_down, g_ffc) = _transfer_wait("gather_rest_wait", ssem, rsem, late_src, late_lands,
                                                         _gather_copies, after)
        return dict(w_out=g_out.reshape(D_MODEL, D_MODEL), w_up=g_up.transpose(1, 0, 2).reshape(D_MODEL, 2 * D_FF),
                    w_down=g_down.reshape(D_FF, D_MODEL), ffn_conv_w=g_ffc.transpose(1, 0, 2).reshape(3, 2 * D_FF))

    full = dict(
        w_in=g_in,
        dn_conv_w=g_dnc.transpose(1, 0, 2).reshape(4, 3 * DN_WIDTH),
        attn_norm_g=attn_norm_g, dn_a_log=dn_a_log, dn_dt_bias=dn_dt_bias, dn_out_norm_g=dn_out_norm_g,
        sg_norm_g=sg_norm_g, sg_w=sg_w[0], sg_b=sg_b[0], ffn_norm_g=ffn_norm_g, ffn_conv_b=ffn_conv_b,
        final_norm_g=final_norm_g[None])

    pending = {}
    early_names = ("dn_out_norm_g", "sg_norm_g", "sg_w", "sg_b", "ffn_norm_g", "ffn_conv_w", "ffn_conv_b",
                   "final_norm_g")
    late_names = ("attn_norm_g", "dn_a_log", "dn_dt_bias", "dn_conv_w")

    def on_grad(name, gw):
        if name == "small_early":
            buf = _pack([gw[n] for n in early_names])
            land = lax.dynamic_update_index_in_dim(lax.empty((8,) + buf.shape, F32), buf, me_lin, 0)
            s_sem, r_sem, src, lands, tok = _transfer_start("small_early_start", [buf], [land], 7, _small_copies)
            pending[name] = (s_sem, r_sem, src, lands)
            return tok
        g8 = gw.reshape(8, -1, gw.shape[-1])
        land = lax.empty((7,) + g8.shape[1:], BF16)
        s_sem, r_sem, src, lands, tok = _transfer_start(f"reduce_{name}_start", [g8], [land], 7, _pieces_copies)
        pending[name] = (s_sem, r_sem, src, lands)
        return tok

    loss_row, grad_x, g = _local_step(x[0], loss_target[0], full, dep=token, late_weights=late_weights,
                                      on_grad=on_grad)

    small_names = REPL + CONV
    late_all = _exchange_small(_pack([g[n] for n in late_names] + [loss_row]))
    late_sum = _sum_slots("sum_small_late", late_all[0], late_all[1:])
    s_sem, r_sem, src, lands = pending["small_early"]
    _, (early_all,) = _transfer_wait("small_early_wait", s_sem, r_sem, src, lands, _small_copies, grad_x)
    early_sum = _sum_slots("sum_small_early", early_all[0], early_all[1:])
    *late_vals, loss_sum = _unpack(late_sum, [g[n].shape for n in late_names] + [loss_row.shape])
    loss = loss_sum[0, 0]
    sg = dict(zip(late_names, late_vals))
    sg.update(zip(early_names, _unpack(early_sum, [g[n].shape for n in early_names])))
    sg["dn_conv_w"] = lax.dynamic_slice_in_dim(sg["dn_conv_w"], chip * (3 * DN_WIDTH // 4), 3 * DN_WIDTH // 4, axis=1)
    sg["ffn_conv_w"] = lax.dynamic_slice_in_dim(sg["ffn_conv_w"], chip * (2 * D_FF // 4), 2 * D_FF // 4, axis=1)

    halves = []
    for n in ("w_down", "w_up", "w_out", "w_in"):
        s_sem, r_sem, src, lands = pending[n]
        sent, got = _transfer_wait(f"reduce_{n}_wait", s_sem, r_sem, src, lands, _pieces_copies, grad_x)
        own = lax.dynamic_index_in_dim(sent[0], me_lin, axis=0, keepdims=False)
        halves.append(_sum_slots(f"sum_{n}", own, got[0]))
    theirs = _pair_swap(halves)
    core = ci.astype(jnp.int32).reshape(1)
    grads, delta, new_m, new_v = {}, {}, {}, {}
    for n, mine_h, their_h in zip(("w_down", "w_up", "w_out", "w_in"), halves, theirs):
        shp = W[n].shape
        gr, d, mn, vn = _adamw_halves(f"adamw_{n}", W[n][0], mine_h, their_h, Mo[n][0], Vo[n][0], core)
        grads[n], delta[n], new_m[n], new_v[n] = gr.reshape(shp), d.reshape(shp), mn.reshape(shp), vn.reshape(shp)
    shapes = [W[n].shape for n in small_names]
    for n in small_names:
        grads[n] = sg[n].reshape(W[n].shape)
    d, mn, vn = _adamw("adamw_small", _pack([W[n] for n in small_names]), _pack([grads[n] for n in small_names]),
                       _pack([Mo[n] for n in small_names]), _pack([Vo[n] for n in small_names]))
    for dst, buf in ((delta, d), (new_m, mn), (new_v, vn)):
        dst.update(zip(small_names, _unpack(buf, shapes)))

    return (loss, grad_x[None], *[grads[n] for n in ORDER], *[delta[n] for n in ORDER],
            *[new_m[n] for n in ORDER], *[new_v[n] for n in ORDER])
```

```python
import functools
import math

import jax
import jax.numpy as jnp
from jax import lax
from jax.experimental import pallas as pl
from jax.experimental.pallas import tpu as pltpu

F32 = jnp.float32
BF16 = jnp.bfloat16

D_MODEL = 1024
CHUNK = 64
SCAN_CHUNKS = 2
HEAD_DIM = 128
N_HEADS = 4
DN_WIDTH = 512
SG_WIDTH = 512
SG_GROUPS = 4
SG_BLOCK = 128
D_FF = 2816
PROJ_COLS = 3080
PROJ_PAD = 3200
BA_COL = 3072
EPS = 1e-6
NEG = -1e30
VMEM_LIMIT = 56 * 1024 * 1024

ADAM_LR = 0.001
ADAM_B1 = 0.9
ADAM_B2 = 0.999
ADAM_EPS = 1e-08
ADAM_WD = 0.01
ADAM_STEP = 10

MESH = pl.DeviceIdType.MESH
ANY = pl.BlockSpec(memory_space=pl.ANY)


def _cp(*sem):
    return pltpu.CompilerParams(dimension_semantics=sem, vmem_limit_bytes=VMEM_LIMIT)


def _bf(a):
    return a.astype(BF16)


def _nn(a, b):
    return jnp.dot(_bf(a), _bf(b), preferred_element_type=F32)


def _nt(a, b):
    return lax.dot_general(_bf(a), _bf(b), (((1,), (1,)), ((), ())), preferred_element_type=F32)


def _tn(a, b):
    return lax.dot_general(_bf(a), _bf(b), (((0,), (0,)), ((), ())), preferred_element_type=F32)


def _split(a):
    hi = _bf(a)
    return hi, _bf(a - hi.astype(F32))


def _sigmoid(x):
    return 0.5 * jnp.tanh(0.5 * x) + 0.5


def _silu(x):
    return x * _sigmoid(x)


def _dsilu(x):
    s = _sigmoid(x)
    return s * (1.0 + x * (1.0 - s))


_GELU_C = math.sqrt(2.0 / math.pi)
_GELU_A = 0.044715


def _gelu(x):
    return 0.5 * x * (1.0 + jnp.tanh(_GELU_C * (x + _GELU_A * x * x * x)))


def _dgelu(x):
    t = jnp.tanh(_GELU_C * (x + _GELU_A * x * x * x))
    return 0.5 * (1.0 + t) + 0.5 * x * (1.0 - t * t) * _GELU_C * (1.0 + 3.0 * _GELU_A * x * x)


def _softplus(x):
    return jnp.maximum(x, 0.0) + jnp.log(1.0 + jnp.exp(-jnp.abs(x)))


def _mm_nn(name, a, b, out_dtype, tm, tn, res=None):
    M, K = a.shape
    N = b.shape[1]
    tm, tn = min(tm, M), min(tn, N)

    def body(*refs):
        a_ref, b_ref = refs[0], refs[1]
        o_ref = refs[-1]
        acc = _nn(a_ref[...], b_ref[...])
        if res is not None:
            acc = acc + refs[2][...]
        o_ref[...] = acc.astype(o_ref.dtype)

    in_specs = [pl.BlockSpec((tm, K), lambda j, i: (i, 0)), pl.BlockSpec((K, tn), lambda j, i: (0, j))]
    args = [a, b]
    if res is not None:
        in_specs.append(pl.BlockSpec((tm, tn), lambda j, i: (i, j)))
        args.append(res)
    return pl.pallas_call(
        body, name=name, grid=(N // tn, M // tm), in_specs=in_specs,
        out_specs=pl.BlockSpec((tm, tn), lambda j, i: (i, j)),
        out_shape=jax.ShapeDtypeStruct((M, N), out_dtype),
        compiler_params=_cp("parallel", "parallel"))(*args)


def _with_dep(in_specs, args, dep):
    if dep is None:
        return in_specs, args
    return in_specs + [ANY], args + [dep]


SUB_ROWS = 128


def _sub_blocks(tm):
    return [slice(r0, min(r0 + SUB_ROWS, tm)) for r0 in range(0, tm, SUB_ROWS)]


def _rms_hat(xv):
    r = lax.rsqrt(jnp.mean(xv * xv, axis=-1, keepdims=True) + EPS)
    return xv * r, r


def _rms_bwd_vals(dh, xh, r, g):
    dxh = dh * g
    return r * (dxh - xh * jnp.mean(dxh * xh, axis=-1, keepdims=True)), jnp.sum(dh * xh, axis=0, keepdims=True)


def _in_proj(x, g, w4, tm=512, dep=None):
    T, K = x.shape
    ng, _, wc = w4.shape
    tm = min(tm, T)

    def body(x_ref, g_ref, w4_ref, *rest):
        p_ref, h_ref, w_ref = rest[-3:]

        @pl.when(pl.program_id(0) == 0)
        def _():
            w_ref[:, ng * wc:] = jnp.zeros((K, PROJ_PAD - ng * wc), BF16)
            for j in range(ng):
                w_ref[:, j * wc:(j + 1) * wc] = w4_ref[j]
        for r in _sub_blocks(tm):
            xh, _ = _rms_hat(x_ref[r, :])
            h_ref[r, :] = (xh * g_ref[...]).astype(BF16)
        p_ref[...] = jnp.dot(h_ref[...], w_ref[...], preferred_element_type=F32)

    in_specs, args = _with_dep(
        [pl.BlockSpec((tm, K), lambda i: (i, 0)), pl.BlockSpec((1, K), lambda i: (0, 0)),
         pl.BlockSpec((ng, K, wc), lambda i: (0, 0, 0))], [x, g, w4], dep)
    return pl.pallas_call(
        body, name="in_proj", grid=(T // tm,), in_specs=in_specs,
        out_specs=[pl.BlockSpec((tm, PROJ_PAD), lambda i: (i, 0)), pl.BlockSpec((tm, K), lambda i: (i, 0)),
                   pl.BlockSpec((K, PROJ_PAD), lambda i: (0, 0))],
        out_shape=[jax.ShapeDtypeStruct((T, PROJ_PAD), F32), jax.ShapeDtypeStruct((T, K), BF16),
                   jax.ShapeDtypeStruct((K, PROJ_PAD), BF16)],
        compiler_params=_cp("arbitrary"))(*args)


def _out_proj(mix, w, x, g, tm=512):
    T, K = mix.shape
    Dm = w.shape[1]
    tm = min(tm, T)

    def body(a_ref, w_ref, x_ref, g_ref, x2_ref, h_ref):
        x2_ref[...] = _nn(a_ref[...], w_ref[...]) + x_ref[...]
        for r in _sub_blocks(tm):
            xh, _ = _rms_hat(x2_ref[r, :])
            h_ref[r, :] = (xh * g_ref[...]).astype(BF16)

    row = lambda width: pl.BlockSpec((tm, width), lambda i: (i, 0))
    return pl.pallas_call(
        body, name="out_proj", grid=(T // tm,),
        in_specs=[row(K), pl.BlockSpec((K, Dm), lambda i: (0, 0)), row(Dm), pl.BlockSpec((1, Dm), lambda i: (0, 0))],
        out_specs=[row(Dm), row(Dm)],
        out_shape=[jax.ShapeDtypeStruct((T, Dm), F32), jax.ShapeDtypeStruct((T, Dm), BF16)],
        compiler_params=_cp("parallel"))(mix, w, x, g)


def _down_proj_loss(act, w, x2, tgt, g, tm=512):
    T, K = act.shape
    Dm = w.shape[1]
    tm = min(tm, T)

    def body(a_ref, w_ref, x_ref, t_ref, g_ref, loss_ref, dx_ref, gg_ref):
        @pl.when(pl.program_id(0) == 0)
        def _():
            gg_ref[...] = jnp.zeros_like(gg_ref)
            loss_ref[...] = jnp.zeros_like(loss_ref)
        dx_ref[...] = _nn(a_ref[...], w_ref[...]) + x_ref[...]
        for r in _sub_blocks(tm):
            xh, rr = _rms_hat(dx_ref[r, :])
            e = xh * g_ref[...] - t_ref[r, :]
            loss_ref[...] += jnp.zeros_like(loss_ref) + (0.5 / Dm) * jnp.sum(e * e)
            dx, gg = _rms_bwd_vals(e * (1.0 / Dm), xh, rr, g_ref[...])
            dx_ref[r, :] = dx
            gg_ref[...] += gg

    row = lambda width: pl.BlockSpec((tm, width), lambda i: (i, 0))
    vec = pl.BlockSpec((1, Dm), lambda i: (0, 0))
    return pl.pallas_call(
        body, name="down_proj_loss", grid=(T // tm,),
        in_specs=[row(K), pl.BlockSpec((K, Dm), lambda i: (0, 0)), row(Dm), row(Dm), vec],
        out_specs=[pl.BlockSpec((1, 128), lambda i: (0, 0)), row(Dm), vec],
        out_shape=[jax.ShapeDtypeStruct((1, 128), F32), jax.ShapeDtypeStruct((T, Dm), F32),
                   jax.ShapeDtypeStruct((1, Dm), F32)],
        compiler_params=_cp("arbitrary"))(act, w, x2, tgt, g)


def _mm_nt_rms_bwd(name, a, b, x, g, dres, tm=512, dep=None):
    M, K = a.shape
    Dm = b.shape[0]
    tm = min(tm, M)

    def body(a_ref, b_ref, x_ref, g_ref, dres_ref, *rest):
        dx_ref, gg_ref = rest[-2:]

        @pl.when(pl.program_id(0) == 0)
        def _():
            gg_ref[...] = jnp.zeros_like(gg_ref)
        dx_ref[...] = _nt(a_ref[...], b_ref[...])
        for r in _sub_blocks(tm):
            xh, rr = _rms_hat(x_ref[r, :])
            dx, gg = _rms_bwd_vals(dx_ref[r, :], xh, rr, g_ref[...])
            dx_ref[r, :] = dres_ref[r, :] + dx
            gg_ref[...] += gg

    row = lambda width: pl.BlockSpec((tm, width), lambda i: (i, 0))
    vec = pl.BlockSpec((1, Dm), lambda i: (0, 0))
    in_specs, args = _with_dep([row(K), pl.BlockSpec((Dm, K), lambda i: (0, 0)), row(Dm), vec, row(Dm)],
                               [a, b, x, g, dres], dep)
    return pl.pallas_call(
        body, name=name, grid=(M // tm,), in_specs=in_specs, out_specs=[row(Dm), vec],
        out_shape=[jax.ShapeDtypeStruct((M, Dm), F32), jax.ShapeDtypeStruct((1, Dm), F32)],
        compiler_params=_cp("arbitrary"))(*args)


def _mm_nt(name, a, b, out_dtype, tm, tn, dep=None):
    M, K = a.shape
    N = b.shape[0]
    tm, tn = min(tm, M), min(tn, N)

    def body(a_ref, b_ref, *rest):
        o_ref = rest[-1]
        o_ref[...] = _nt(a_ref[...], b_ref[...]).astype(o_ref.dtype)

    in_specs, args = _with_dep(
        [pl.BlockSpec((tm, K), lambda i, j: (i, 0)), pl.BlockSpec((tn, K), lambda i, j: (j, 0))], [a, b], dep)
    return pl.pallas_call(
        body, name=name, grid=(M // tm, N // tn), in_specs=in_specs,
        out_specs=pl.BlockSpec((tm, tn), lambda i, j: (i, j)),
        out_shape=jax.ShapeDtypeStruct((M, N), out_dtype),
        compiler_params=_cp("parallel", "parallel"))(*args)


def _mm_tn(name, a, b, tm, tn, tk, col_major_tiles=False, col_groups=None):
    T, M = a.shape
    N = b.shape[1]
    tm, tn, tk = min(tm, M), min(tn, N), min(tk, T)
    nk = T // tk

    def body(a_ref, b_ref, o_ref, acc_ref):
        k = pl.program_id(2)

        @pl.when(k == 0)
        def _():
            acc_ref[...] = jnp.zeros_like(acc_ref)
        acc_ref[...] += _tn(a_ref[...], b_ref[...])

        @pl.when(k == nk - 1)
        def _():
            if col_groups:
                for j in range(col_groups[0]):
                    o_ref[j] = acc_ref[:, j * col_groups[1]:(j + 1) * col_groups[1]].astype(BF16)
            else:
                o_ref[...] = acc_ref[...].astype(BF16).reshape(o_ref.shape)

    if col_groups:
        assert tm == M and tn == N and col_groups[0] * col_groups[1] <= N
        out_spec = pl.BlockSpec((col_groups[0], M, col_groups[1]), lambda i, j, k: (0, 0, 0))
        out_shape = jax.ShapeDtypeStruct((col_groups[0], M, col_groups[1]), BF16)
    elif col_major_tiles:
        assert tm == M
        out_spec = pl.BlockSpec((1, tm, tn), lambda i, j, k: (j, 0, 0))
        out_shape = jax.ShapeDtypeStruct((N // tn, M, tn), BF16)
    else:
        out_spec = pl.BlockSpec((tm, tn), lambda i, j, k: (i, j))
        out_shape = jax.ShapeDtypeStruct((M, N), BF16)
    return pl.pallas_call(
        body, name=name, grid=(M // tm, N // tn, nk),
        in_specs=[pl.BlockSpec((tk, tm), lambda i, j, k: (k, i)), pl.BlockSpec((tk, tn), lambda i, j, k: (k, j))],
        out_specs=out_spec, out_shape=out_shape, scratch_shapes=[pltpu.VMEM((tm, tn), F32)],
        compiler_params=_cp("parallel", "parallel", "arbitrary"))(a, b)


def _halo_prev_spec(rb, width):
    return pl.BlockSpec((8, width), lambda i: (jnp.maximum(i * (rb // 8) - 1, 0), 0))


def _halo_next_spec(rb, width, T):
    return pl.BlockSpec((8, width), lambda i: (jnp.minimum((i + 1) * (rb // 8), T // 8 - 1), 0))


LANES = 128
FF_STRIPS = D_FF // LANES
ROW_CHUNK = 32


def _strip(j, base=0):
    return pl.ds(pl.multiple_of(base + j * LANES, LANES), LANES)


def _ffn_act(up, w, b, rb=256):
    T, W = up.shape
    rb = min(rb, T)

    def body(up_ref, halo_ref, w_ref, b_ref, act_ref, ext_scr):
        first = pl.program_id(0) == 0

        def strip(j, slot):
            halves = (_strip(j), _strip(j, D_FF))
            wv = [w_ref[:, cols] for cols in halves]
            bv = [b_ref[:, cols] for cols in halves]
            for h, cols in enumerate(halves):
                ext_scr[slot, h,0:8] = jnp.where(first, 0.0, halo_ref[:, cols])
                ext_scr[slot, h,8:] = up_ref[:, cols]
            for r0 in range(0, rb, ROW_CHUNK):
                n = min(ROW_CHUNK, rb - r0)
                c = [ext_scr[slot, h,6 + r0:6 + r0 + n] * wv[h][0:1] + ext_scr[slot, h,7 + r0:7 + r0 + n] * wv[h][1:2]
                     + ext_scr[slot, h,8 + r0:8 + r0 + n] * wv[h][2:3] + bv[h] for h in range(2)]
                act_ref[r0:r0 + n, halves[0]] = (_silu(c[0]) * c[1]).astype(BF16)

        def pair(jj, carry):
            strip(2 * jj, 0)
            strip(2 * jj + 1, 1)
            return carry

        lax.fori_loop(0, FF_STRIPS // 2, pair, 0)

    return pl.pallas_call(
        body, name="ffn_act", grid=(T // rb,),
        in_specs=[pl.BlockSpec((rb, W), lambda i: (i, 0)), _halo_prev_spec(rb, W),
                  pl.BlockSpec((3, W), lambda i: (0, 0)), pl.BlockSpec((1, W), lambda i: (0, 0))],
        out_specs=pl.BlockSpec((rb, D_FF), lambda i: (i, 0)),
        out_shape=jax.ShapeDtypeStruct((T, D_FF), BF16),
        scratch_shapes=[pltpu.VMEM((2, 2, rb + 8, LANES), F32)], compiler_params=_cp("parallel"))(up, up, w, b)


def _ffn_act_bwd(up, dact, w, b, rb=128, dep=None):
    T, W = up.shape
    rb = min(rb, T)
    nb = T // rb
    re = rb + 8

    def body(up_ref, prev_ref, next_ref, da_ref, danext_ref, w_ref, b_ref, *rest):
        dup_ref, gw_ref, gb_ref, ext_scr, dc_scr = rest[-5:]
        i = pl.program_id(0)

        @pl.when(i == 0)
        def _():
            gw_ref[...] = jnp.zeros_like(gw_ref)
            gb_ref[...] = jnp.zeros_like(gb_ref)
        last = i == nb - 1

        def fold8(a):
            return jnp.sum(a.reshape(a.shape[0] // 8, 8, LANES), axis=0)

        def strip(j, slot):
            halves = (_strip(j), _strip(j, D_FF))
            wv = [w_ref[:, cols] for cols in halves]
            bv = [b_ref[:, cols] for cols in halves]
            for h, cols in enumerate(halves):
                ext_scr[slot, h,0:8] = jnp.where(i > 0, prev_ref[:, cols], 0.0)
                ext_scr[slot, h,8:8 + rb] = up_ref[:, cols]
                ext_scr[slot, h,8 + rb:] = next_ref[:, cols]
            gb = [jnp.zeros((8, LANES), F32) for _ in range(2)]
            gw = [[jnp.zeros((8, LANES), F32) for _ in range(3)] for _ in range(2)]
            for r0 in range(0, re, ROW_CHUNK):
                n = min(ROW_CHUNK, re - r0)
                tp = [[ext_scr[slot, h,6 + k + r0:6 + k + r0 + n] for k in range(3)] for h in range(2)]
                c = [tp[h][0] * wv[h][0:1] + tp[h][1] * wv[h][1:2] + tp[h][2] * wv[h][2:3] + bv[h] for h in range(2)]
                if r0 < rb:
                    da = da_ref[r0:r0 + n, halves[0]]
                else:
                    da = jnp.where(last, 0.0, danext_ref[:, halves[0]])
                s = _sigmoid(c[0])
                gs = c[0] * s
                dcs = (da * c[1] * (s + gs * (1.0 - s)), da * gs)
                for h in range(2):
                    dc_scr[slot, h,r0:r0 + n] = dcs[h]
                    if r0 < rb:
                        gb[h] = gb[h] + fold8(dcs[h])
                        for k in range(3):
                            gw[h][k] = gw[h][k] + fold8(tp[h][k] * dcs[h])
            for r0 in range(0, rb, ROW_CHUNK):
                n = min(ROW_CHUNK, rb - r0)
                for h, cols in enumerate(halves):
                    dup = (dc_scr[slot, h,r0:r0 + n] * wv[h][2:3] + dc_scr[slot, h,r0 + 1:r0 + 1 + n] * wv[h][1:2]
                           + dc_scr[slot, h,r0 + 2:r0 + 2 + n] * wv[h][0:1])
                    dup_ref[r0:r0 + n, cols] = dup.astype(BF16)
            for h, cols in enumerate(halves):
                gb_ref[:, cols] += jnp.sum(gb[h], axis=0, keepdims=True)
                for k in range(3):
                    gw_ref[k:k + 1, cols] += jnp.sum(gw[h][k], axis=0, keepdims=True)

        def pair(jj, carry):
            strip(2 * jj, 0)
            strip(2 * jj + 1, 1)
            return carry

        lax.fori_loop(0, FF_STRIPS // 2, pair, 0)

    in_specs, args = _with_dep(
        [pl.BlockSpec((rb, W), lambda i: (i, 0)), _halo_prev_spec(rb, W), _halo_next_spec(rb, W, T),
         pl.BlockSpec((rb, D_FF), lambda i: (i, 0)), _halo_next_spec(rb, D_FF, T),
         pl.BlockSpec((3, W), lambda i: (0, 0)), pl.BlockSpec((1, W), lambda i: (0, 0))],
        [up, up, up, dact, dact, w, b], dep)
    return pl.pallas_call(
        body, name="ffn_act_bwd", grid=(nb,), in_specs=in_specs,
        out_specs=[pl.BlockSpec((rb, W), lambda i: (i, 0)), pl.BlockSpec((3, W), lambda i: (0, 0)),
                   pl.BlockSpec((1, W), lambda i: (0, 0))],
        out_shape=[jax.ShapeDtypeStruct((T, W), BF16), jax.ShapeDtypeStruct((3, W), F32),
                   jax.ShapeDtypeStruct((1, W), F32)],
        scratch_shapes=[pltpu.VMEM((2, 2, rb + 16, LANES), F32), pltpu.VMEM((2, 2, re, LANES), F32)],
        compiler_params=_cp("arbitrary"))(*args)


def _lane_iota(shape):
    return lax.broadcasted_iota(jnp.int32, shape, len(shape) - 1)


def _dn_act(p, conv_w, alog_row, dtb_row, rb=256):
    T = p.shape[0]
    rb = min(rb, T)
    W3 = 3 * DN_WIDTH

    def body(p_ref, halo_ref, ba_ref, w_ref, al_ref, dt_ref, q_ref, k_ref, v_ref, bg_ref, ext_scr):
        first = pl.program_id(0) == 0
        outs = (q_ref, k_ref, v_ref)
        for j in range(3 * N_HEADS):
            kind, h = divmod(j, N_HEADS)
            cols = slice(j * HEAD_DIM, (j + 1) * HEAD_DIM)
            cur = p_ref[:, cols]
            ext_scr[j, 0:8] = jnp.where(first, 0.0, halo_ref[:, cols])
            ext_scr[j, 8:] = cur
            wv = w_ref[:, cols]
            s = _silu(ext_scr[j, 5:5 + rb] * wv[0:1] + ext_scr[j, 6:6 + rb] * wv[1:2]
                      + ext_scr[j, 7:7 + rb] * wv[2:3] + cur * wv[3:4])
            if kind < 2:
                scale = HEAD_DIM ** -0.5 if kind == 0 else 1.0
                s = s * (lax.rsqrt(jnp.sum(s * s, axis=-1, keepdims=True) + EPS) * scale)
            outs[kind][:, h * HEAD_DIM:(h + 1) * HEAD_DIM] = s
        ba = ba_ref[...]
        lane = _lane_iota(ba.shape)
        beta = _sigmoid(ba)
        g = -jnp.exp(al_ref[...]) * _softplus(ba + dt_ref[...])
        bg_ref[...] = jnp.where(lane < N_HEADS, beta, jnp.where(lane < 2 * N_HEADS, g, 0.0))

    row512 = pl.BlockSpec((rb, DN_WIDTH), lambda i: (i, 0))
    row128 = pl.BlockSpec((rb, 128), lambda i: (i, 0))
    vec128 = pl.BlockSpec((1, 128), lambda i: (0, 0))
    return pl.pallas_call(
        body, name="dn_act", grid=(T // rb,),
        in_specs=[pl.BlockSpec((rb, W3), lambda i: (i, 0)), _halo_prev_spec(rb, W3),
                  pl.BlockSpec((rb, 128), lambda i: (i, BA_COL // 128)),
                  pl.BlockSpec((4, W3), lambda i: (0, 0)), vec128, vec128],
        out_specs=[row512, row512, row512, row128],
        out_shape=[jax.ShapeDtypeStruct((T, DN_WIDTH), F32)] * 3 + [jax.ShapeDtypeStruct((T, 128), F32)],
        scratch_shapes=[pltpu.VMEM((3 * N_HEADS, rb + 8, HEAD_DIM), F32)],
        compiler_params=_cp("parallel"))(p, p, p, conv_w, alog_row, dtb_row)


def _dn_act_bwd(p, conv_w, alog_row, dtb_row, dq, dk, dv, dbg, dp_mid, rb=256):
    T = p.shape[0]
    rb = min(rb, T)
    nb = T // rb
    re = rb + 8
    W3 = 3 * DN_WIDTH

    def body(p_ref, prev_ref, next_ref, ba_ref, w_ref, al_ref, dt_ref, dq_ref, dqn_ref, dk_ref, dkn_ref,
             dv_ref, dvn_ref, dbg_ref, mid_ref, draw_ref, gw_ref, gad_ref, ext_scr, dc_scr):
        i = pl.program_id(0)
        draw_ref[:, W3:2 * W3] = mid_ref[...]

        @pl.when(i == 0)
        def _():
            gw_ref[...] = jnp.zeros_like(gw_ref)
            gad_ref[...] = jnp.zeros_like(gad_ref)
        row = lax.broadcasted_iota(jnp.int32, (re, 1), 0)
        live = (row < rb) | (i < nb - 1)
        d_refs = ((dq_ref, dqn_ref), (dk_ref, dkn_ref), (dv_ref, dvn_ref))
        for j in range(3 * N_HEADS):
            kind, h = divmod(j, N_HEADS)
            cols = slice(j * HEAD_DIM, (j + 1) * HEAD_DIM)
            hcols = slice(h * HEAD_DIM, (h + 1) * HEAD_DIM)
            ext_scr[j, 0:8] = jnp.where(i > 0, prev_ref[:, cols], 0.0)
            ext_scr[j, 8:8 + rb] = p_ref[:, cols]
            ext_scr[j, 8 + rb:] = next_ref[:, cols]
            tp = [ext_scr[j, 5 + k:5 + k + re] for k in range(4)]
            wv = w_ref[:, cols]
            c = tp[0] * wv[0:1] + tp[1] * wv[1:2] + tp[2] * wv[2:3] + tp[3] * wv[3:4]
            sg = _sigmoid(c)
            s = c * sg
            d_in = jnp.where(live, jnp.concatenate([d_refs[kind][0][:, hcols], d_refs[kind][1][:, hcols]], axis=0), 0.0)
            if kind < 2:
                scale = HEAD_DIM ** -0.5 if kind == 0 else 1.0
                n = lax.rsqrt(jnp.sum(s * s, axis=-1, keepdims=True) + EPS)
                hat = s * n
                d_in = (n * scale) * (d_in - hat * jnp.sum(hat * d_in, axis=-1, keepdims=True))
            dc = d_in * (sg + s * (1.0 - sg))
            dc_scr[j] = dc
            dcc = dc[0:rb]
            draw = (dcc * wv[3:4] + dc_scr[j, 1:1 + rb] * wv[2:3] + dc_scr[j, 2:2 + rb] * wv[1:2]
                    + dc_scr[j, 3:3 + rb] * wv[0:1])
            draw_ref[:, cols] = draw.astype(BF16)
            for k in range(4):
                gw_ref[k:k + 1, cols] += jnp.sum(tp[k][0:rb] * dcc, axis=0, keepdims=True)
        ba = ba_ref[...]
        dbg = dbg_ref[...]
        lane = _lane_iota(ba.shape)
        beta = _sigmoid(ba)
        ea = jnp.exp(al_ref[...])
        z = ba + dt_ref[...]
        d_a = dbg * (-ea) * _sigmoid(z)
        dba = jnp.where(lane < N_HEADS, dbg * beta * (1.0 - beta), jnp.where(lane < 2 * N_HEADS, d_a, 0.0))
        draw_ref[:, BA_COL:] = dba.astype(BF16)
        isg = (lane >= N_HEADS) & (lane < 2 * N_HEADS)
        g = -ea * _softplus(z)
        gad_ref[0:1, :] += jnp.sum(jnp.where(isg, dbg * g, 0.0), axis=0, keepdims=True)
        gad_ref[1:2, :] += jnp.sum(jnp.where(isg, d_a, 0.0), axis=0, keepdims=True)

    row512 = pl.BlockSpec((rb, DN_WIDTH), lambda i: (i, 0))
    row128 = pl.BlockSpec((rb, 128), lambda i: (i, 0))
    vec128 = pl.BlockSpec((1, 128), lambda i: (0, 0))
    next512 = _halo_next_spec(rb, DN_WIDTH, T)
    return pl.pallas_call(
        body, name="dn_act_bwd", grid=(nb,),
        in_specs=[pl.BlockSpec((rb, W3), lambda i: (i, 0)), _halo_prev_spec(rb, W3), _halo_next_spec(rb, W3, T),
                  pl.BlockSpec((rb, 128), lambda i: (i, BA_COL // 128)),
                  pl.BlockSpec((4, W3), lambda i: (0, 0)), vec128, vec128,
                  row512, next512, row512, next512, row512, next512, row128,
                  pl.BlockSpec((rb, W3), lambda i: (i, 0))],
        out_specs=[pl.BlockSpec((rb, PROJ_PAD), lambda i: (i, 0)),
                   pl.BlockSpec((4, W3), lambda i: (0, 0)), pl.BlockSpec((2, 128), lambda i: (0, 0))],
        out_shape=[jax.ShapeDtypeStruct((T, PROJ_PAD), BF16),
                   jax.ShapeDtypeStruct((4, W3), F32), jax.ShapeDtypeStruct((2, 128), F32)],
        scratch_shapes=[pltpu.VMEM((3 * N_HEADS, rb + 16, HEAD_DIM), F32), pltpu.VMEM((3 * N_HEADS, re, HEAD_DIM), F32)],
        compiler_params=_cp("arbitrary"))(p, p, p, p, conv_w, alog_row, dtb_row, dq, dq, dk, dk, dv, dv, dbg, dp_mid)


def _tri(incl):
    ii = lax.broadcasted_iota(jnp.int32, (CHUNK, CHUNK), 0)
    jj = lax.broadcasted_iota(jnp.int32, (CHUNK, CHUNK), 1)
    return ii, jj, ((ii >= jj) if incl else (ii > jj))


def _dn_chunk(k, bg, cb=4):
    T = k.shape[0]
    N = T // CHUNK
    cb = min(cb, N)

    def body(k_ref, bg_ref, gc_ref, gct_ref, l_ref):
        ii, jj, incl = _tri(True)
        tri = incl.astype(F32)
        U = range(cb)
        bgv = [bg_ref[u * CHUNK:(u + 1) * CHUNK, :] for u in U]
        gc = [jnp.dot(tri, bgv[u], precision=lax.Precision.HIGHEST, preferred_element_type=F32) for u in U]
        gct = [gc[u].T for u in U]
        kk = [[None] * N_HEADS for _ in U]
        for u in U:
            gc_ref[u * CHUNK:(u + 1) * CHUNK, :] = gc[u]
            gct_ref[u] = gct[u][0:8]
            for h in range(N_HEADS):
                kh = k_ref[u * CHUNK:(u + 1) * CHUNK, h * HEAD_DIM:(h + 1) * HEAD_DIM]
                kk[u][h] = _nt(kh * bgv[u][:, h:h + 1], kh)
        for u in U:
            for h in range(N_HEADS):
                gcol = gc[u][:, N_HEADS + h:N_HEADS + h + 1]
                grow = gct[u][N_HEADS + h:N_HEADS + h + 1, :]
                l_ref[u, h] = kk[u][h] * jnp.exp(jnp.where(ii > jj, gcol - grow, NEG))

    rows = cb * CHUNK
    return pl.pallas_call(
        body, name="dn_chunk", grid=(N // cb,),
        in_specs=[pl.BlockSpec((rows, DN_WIDTH), lambda n: (n, 0)), pl.BlockSpec((rows, 128), lambda n: (n, 0))],
        out_specs=[pl.BlockSpec((rows, 128), lambda n: (n, 0)), pl.BlockSpec((cb, 8, CHUNK), lambda n: (n, 0, 0)),
                   pl.BlockSpec((cb, N_HEADS, CHUNK, CHUNK), lambda n: (n, 0, 0, 0))],
        out_shape=[jax.ShapeDtypeStruct((T, 128), F32), jax.ShapeDtypeStruct((N, 8, CHUNK), F32),
                   jax.ShapeDtypeStruct((N, N_HEADS, CHUNK, CHUNK), F32)],
        compiler_params=_cp("parallel"))(k, bg)


def _tri_inv(lt):
    S = lt.shape[1]

    def body(l_ref, a_ref):
        sub = lax.broadcasted_iota(jnp.int32, (8, S), 0)
        groups = CHUNK // 8
        for i in range(CHUNK):
            acc = [((sub + 8 * k) == i).astype(F32) for k in range(groups)]
            for jb in range((i + 7) // 8):
                nk = jb + 1

                def step(j, carry, nk=nk, i=i):
                    lrow = l_ref[pl.ds(i * CHUNK + j, 1), :]
                    return tuple(carry[k] - lrow * a_ref[j, 8 * k:8 * k + 8, :] for k in range(nk))

                acc[:nk] = list(lax.fori_loop(8 * jb, min(8 * jb + 8, i), step, tuple(acc[:nk])))
            for k in range(groups):
                a_ref[i, 8 * k:8 * k + 8, :] = acc[k]

    return pl.pallas_call(
        body, name="tri_inv", out_shape=jax.ShapeDtypeStruct((CHUNK, CHUNK, S), F32),
        compiler_params=pltpu.CompilerParams(vmem_limit_bytes=VMEM_LIMIT))(lt)


def _dn_head_terms(qh, kh, vh, beta, gcol, grow):
    ii, jj, incl = _tri(True)
    gam = jnp.exp(jnp.where(incl, gcol - grow, NEG))
    glast = grow[:, CHUNK - 1:CHUNK]
    cd = jnp.exp(glast)
    shape = (CHUNK, HEAD_DIM)
    E = jnp.broadcast_to(jnp.exp(gcol), shape)
    Fd = jnp.broadcast_to(jnp.exp(glast - gcol), shape)
    beta = jnp.broadcast_to(beta, shape)
    kb = kh * beta
    return dict(ii=ii, jj=jj, gam=gam, E=E, F=Fd, beta=beta, cd=cd, kb=kb, vb=vh * beta, W=kb * E, qE=qh * E,
                kt=kh * Fd)


def _apply_a(a, u):
    hi, lo = _split(a)
    ub = _bf(u)
    return jnp.dot(hi, ub, preferred_element_type=F32) + jnp.dot(lo, ub, preferred_element_type=F32)


def _dn_scan(q, k, v, bg, gc, gct, a):
    T = q.shape[0]
    N = T // CHUNK

    cb = min(SCAN_CHUNKS, N)

    def body(q_ref, k_ref, v_ref, bg_ref, gc_ref, gct_ref, a_ref, o_ref, sall_ref, s_ref):
        @pl.when(pl.program_id(0) == 0)
        def _():
            s_ref[...] = jnp.zeros_like(s_ref)
        H = range(N_HEADS)
        sl = [slice(h * HEAD_DIM, (h + 1) * HEAD_DIM) for h in H]
        pre = []
        for u in range(cb):
            r = slice(u * CHUNK, (u + 1) * CHUNK)
            bgv, gcv, gctv = bg_ref[r, :], gc_ref[r, :], gct_ref[u]
            q_, k_ = [q_ref[r, s] for s in sl], [k_ref[r, s] for s in sl]
            t = [_dn_head_terms(q_[h], k_[h], v_ref[r, sl[h]], bgv[:, h:h + 1],
                                gcv[:, N_HEADS + h:N_HEADS + h + 1], gctv[N_HEADS + h:N_HEADS + h + 1, :]) for h in H]
            P = [_nt(q_[h], k_[h]) * t[h]["gam"] for h in H]
            pre.append((r, t, P))
        S = [s_ref[h] for h in H]
        for u in range(cb):
            r, t, P = pre[u]
            for h in H:
                sall_ref[u, h] = S[h]
            WS = [_nn(t[h]["W"], S[h]) for h in H]
            qS = [_nn(t[h]["qE"], S[h]) for h in H]
            vn = [_apply_a(a_ref[u, h], t[h]["vb"] - WS[h]) for h in H]
            Pv = [_nn(P[h], vn[h]) for h in H]
            kv = [_tn(t[h]["kt"], vn[h]) for h in H]
            for h in H:
                o_ref[r, sl[h]] = qS[h] + Pv[h]
            S = [t[h]["cd"] * S[h] + kv[h] for h in H]
        for h in H:
            s_ref[h] = S[h]

    row512 = pl.BlockSpec((cb * CHUNK, DN_WIDTH), lambda n: (n, 0))
    row128 = pl.BlockSpec((cb * CHUNK, 128), lambda n: (n, 0))
    return pl.pallas_call(
        body, name="dn_scan", grid=(N // cb,),
        in_specs=[row512, row512, row512, row128, row128, pl.BlockSpec((cb, 8, CHUNK), lambda n: (n, 0, 0)),
                  pl.BlockSpec((cb, N_HEADS, CHUNK, CHUNK), lambda n: (n, 0, 0, 0))],
        out_specs=[row512, pl.BlockSpec((cb, N_HEADS, HEAD_DIM, HEAD_DIM), lambda n: (n, 0, 0, 0))],
        out_shape=[jax.ShapeDtypeStruct((T, DN_WIDTH), F32),
                   jax.ShapeDtypeStruct((N, N_HEADS, HEAD_DIM, HEAD_DIM), F32)],
        scratch_shapes=[pltpu.VMEM((N_HEADS, HEAD_DIM, HEAD_DIM), F32)],
        compiler_params=_cp("arbitrary"))(q, k, v, bg, gc, gct, a)


def _dn_scan_bwd(q, k, v, bg, gc, gct, a, a_t, sall, do, dep=None):
    T = q.shape[0]
    N = T // CHUNK

    cb = min(SCAN_CHUNKS, N)
    nb = N // cb

    def body(q_ref, k_ref, v_ref, bg_ref, gc_ref, gct_ref, a_ref, at_ref, sall_ref, do_ref, *rest):
        dq_ref, dk_ref, dv_ref, dbg_ref, ds_ref = rest[-5:]
        @pl.when(pl.program_id(0) == 0)
        def _():
            ds_ref[...] = jnp.zeros_like(ds_ref)
        lane = _lane_iota((CHUNK, 128))
        rowi = lax.broadcasted_iota(jnp.int32, (CHUNK, 1), 0)
        ii, jj, _ = _tri(True)
        rev = (jj >= ii).astype(F32)
        H = range(N_HEADS)
        sl = [slice(h * HEAD_DIM, (h + 1) * HEAD_DIM) for h in H]
        pre = {}
        for u in reversed(range(cb)):
            r = slice(u * CHUNK, (u + 1) * CHUNK)
            bgv, gcv, gctv = bg_ref[r, :], gc_ref[r, :], gct_ref[u]
            q_, k_, v_ = [q_ref[r, s] for s in sl], [k_ref[r, s] for s in sl], [v_ref[r, s] for s in sl]
            dO = [do_ref[r, s] for s in sl]
            t = [_dn_head_terms(q_[h], k_[h], v_[h], bgv[:, h:h + 1], gcv[:, N_HEADS + h:N_HEADS + h + 1],
                                gctv[N_HEADS + h:N_HEADS + h + 1, :]) for h in H]
            beta = [t[h]["beta"] for h in H]
            S = [sall_ref[u, h] for h in H]
            A = [a_ref[u, h] for h in H]
            WS = [_nn(t[h]["W"], S[h]) for h in H]
            KK = [_nt(t[h]["kb"], k_[h]) for h in H]
            QK = [_nt(q_[h], k_[h]) for h in H]
            d_qE = [_nt(dO[h], S[h]) for h in H]
            vn = [_apply_a(A[h], t[h]["vb"] - WS[h]) for h in H]
            PtdO = [_tn(QK[h] * t[h]["gam"], dO[h]) for h in H]
            qEdO = [_tn(t[h]["qE"], dO[h]) for h in H]
            dOvn = [_nt(dO[h], vn[h]) for h in H]
            dQK = [jnp.where(ii >= jj, dOvn[h], 0.0) * t[h]["gam"] for h in H]
            dQKk = [_nn(dQK[h], k_[h]) for h in H]
            dQKq = [_tn(dQK[h], q_[h]) for h in H]
            pre[u] = (r, q_, k_, v_, beta, t, S, A, KK, QK, d_qE, vn, PtdO, qEdO, dQK, dQKk, dQKq)
        dSn = [ds_ref[h] for h in H]
        for u in reversed(range(cb)):
            r, q_, k_, v_, beta, t, S, A, KK, QK, d_qE, vn, PtdO, qEdO, dQK, dQKk, dQKq = pre[u]
            gam, E, Fd, cd, kb = ([t[h][n] for h in H] for n in ("gam", "E", "F", "cd", "kb"))
            ktdS = [_nn(t[h]["kt"], dSn[h]) for h in H]
            dU = [_apply_a(at_ref[u, h], PtdO[h] + ktdS[h]) for h in H]
            d_kt = [_nt(vn[h], dSn[h]) for h in H]
            dUvn = [_nt(dU[h], vn[h]) for h in H]
            dUS = [_nt(dU[h], S[h]) for h in H]
            WdU = [_tn(t[h]["W"], dU[h]) for h in H]
            d_cd = [jnp.sum(S[h] * dSn[h]) for h in H]
            dSn = [cd[h] * dSn[h] + qEdO[h] - WdU[h] for h in H]
            dKK = [jnp.where(ii > jj, -dUvn[h], 0.0) * gam[h] for h in H]
            dKKk = [_nn(dKK[h], k_[h]) for h in H]
            dKKkb = [_tn(dKK[h], kb[h]) for h in H]
            dbeta_arr = jnp.zeros((CHUNK, 128), F32)
            dgc_arr = jnp.zeros((CHUNK, 128), F32)
            for h in H:
                dW = -dUS[h]
                dq_ref[r, sl[h]] = dQKk[h] + d_qE[h] * E[h]
                d_kb = dKKk[h] + dW * E[h]
                dk_ref[r, sl[h]] = dQKq[h] + dKKkb[h] + d_kb * beta[h] + d_kt[h] * Fd[h]
                dv_ref[r, sl[h]] = dU[h] * beta[h]
                Z = dQK[h] * QK[h] + dKK[h] * KK[h]
                dbeta = jnp.sum(dU[h] * v_[h] + d_kb * k_[h], axis=-1, keepdims=True)
                m_e = (dW * kb[h] + d_qE[h] * q_[h]) * E[h]
                m_f = d_kt[h] * k_[h] * Fd[h]
                zdiag = jnp.where(ii == jj, jnp.sum(Z, axis=0, keepdims=True), 0.0)
                dgc = (jnp.sum(m_e - m_f, axis=-1, keepdims=True) + jnp.sum(Z - zdiag, axis=-1, keepdims=True)
                       + jnp.where(rowi == CHUNK - 1, jnp.sum(m_f) + d_cd[h] * cd[h], 0.0))
                dbeta_arr = dbeta_arr + jnp.where(lane == h, dbeta, 0.0)
                dgc_arr = dgc_arr + jnp.where(lane == N_HEADS + h, dgc, 0.0)
            dbg_ref[r, :] = dbeta_arr + jnp.dot(rev, dgc_arr, precision=lax.Precision.HIGHEST,
                                                preferred_element_type=F32)
        for h in H:
            ds_ref[h] = dSn[h]

    row512 = pl.BlockSpec((cb * CHUNK, DN_WIDTH), lambda n: (nb - 1 - n, 0))
    row128 = pl.BlockSpec((cb * CHUNK, 128), lambda n: (nb - 1 - n, 0))
    in_specs, args = _with_dep(
        [row512, row512, row512, row128, row128,
         pl.BlockSpec((cb, 8, CHUNK), lambda n: (nb - 1 - n, 0, 0)),
         pl.BlockSpec((cb, N_HEADS, CHUNK, CHUNK), lambda n: (nb - 1 - n, 0, 0, 0)),
         pl.BlockSpec((cb, N_HEADS, CHUNK, CHUNK), lambda n: (nb - 1 - n, 0, 0, 0)),
         pl.BlockSpec((cb, N_HEADS, HEAD_DIM, HEAD_DIM), lambda n: (nb - 1 - n, 0, 0, 0)), row512],
        [q, k, v, bg, gc, gct, a, a_t, sall, do], dep)
    return pl.pallas_call(
        body, name="dn_scan_bwd", grid=(nb,), in_specs=in_specs,
        out_specs=[row512, row512, row512, row128],
        out_shape=[jax.ShapeDtypeStruct((T, DN_WIDTH), F32)] * 3 + [jax.ShapeDtypeStruct((T, 128), F32)],
        scratch_shapes=[pltpu.VMEM((N_HEADS, HEAD_DIM, HEAD_DIM), F32)],
        compiler_params=_cp("arbitrary"))(*args)


def _sg_mask():
    ii = lax.broadcasted_iota(jnp.int32, (SG_BLOCK, SG_BLOCK), 0) // CHUNK
    jj = lax.broadcasted_iota(jnp.int32, (SG_BLOCK, SG_BLOCK), 1) // CHUNK
    return jj <= ii


def _mix_fwd(o, p, ong, sgn, sgw, sgbt):
    T = o.shape[0]
    rb = SG_BLOCK

    def body(o_ref, gate_ref, u_ref, vg_ref, ong_ref, sgn_ref, sgw_ref, sgbt_ref, mix_ref):
        mask = _sg_mask()
        gate = gate_ref[...]
        for h in range(N_HEADS):
            sl = slice(h * HEAD_DIM, (h + 1) * HEAD_DIM)
            oh = o_ref[:, sl]
            r = lax.rsqrt(jnp.mean(oh * oh, axis=-1, keepdims=True) + EPS)
            mix_ref[:, sl] = (oh * r * ong_ref[...] * _silu(gate[:, sl])).astype(BF16)
        for gi in range(SG_GROUPS):
            sl = slice(gi * SG_BLOCK, (gi + 1) * SG_BLOCK)
            gv = _gelu(vg_ref[:, sl])
            r = lax.rsqrt(jnp.mean(gv * gv, axis=-1, keepdims=True) + EPS)
            vh = gv * r * sgn_ref[:, sl]
            s = _nn(jnp.where(mask, sgw_ref[gi], 0.0), vh) + sgbt_ref[:, gi:gi + 1]
            mix_ref[:, DN_WIDTH + gi * SG_BLOCK:DN_WIDTH + (gi + 1) * SG_BLOCK] = (_gelu(u_ref[:, sl]) * s).astype(BF16)

    def col(c):
        return pl.BlockSpec((rb, 512), lambda i: (i, c))
    return pl.pallas_call(
        body, name="mix_fwd", grid=(T // rb,),
        in_specs=[pl.BlockSpec((rb, DN_WIDTH), lambda i: (i, 0)), col(3), col(4), col(5),
                  pl.BlockSpec((1, 128), lambda i: (0, 0)), pl.BlockSpec((1, SG_WIDTH), lambda i: (0, 0)),
                  pl.BlockSpec((SG_GROUPS, SG_BLOCK, SG_BLOCK), lambda i: (0, 0, 0)),
                  pl.BlockSpec((SG_BLOCK, 128), lambda i: (0, 0))],
        out_specs=pl.BlockSpec((rb, D_MODEL), lambda i: (i, 0)),
        out_shape=jax.ShapeDtypeStruct((T, D_MODEL), BF16),
        compiler_params=_cp("parallel"))(o, p, p, p, ong, sgn, sgw, sgbt)


def _mix_bwd(o, p, ong, sgn, sgw, sgbt, dmix, dep=None):
    T = o.shape[0]
    rb = SG_BLOCK

    def body(o_ref, gate_ref, u_ref, vg_ref, ong_ref, sgn_ref, sgw_ref, sgbt_ref, dmix_ref, *rest):
        do_ref, dp_ref, gong_ref, gsgn_ref, gsgw_ref, gsgbt_ref = rest[-6:]
        @pl.when(pl.program_id(0) == 0)
        def _():
            gong_ref[...] = jnp.zeros_like(gong_ref)
            gsgn_ref[...] = jnp.zeros_like(gsgn_ref)
            gsgw_ref[...] = jnp.zeros_like(gsgw_ref)
            gsgbt_ref[...] = jnp.zeros_like(gsgbt_ref)
        mask = _sg_mask()
        gate = gate_ref[...]
        lane = _lane_iota((SG_BLOCK, 128))
        for h in range(N_HEADS):
            sl = slice(h * HEAD_DIM, (h + 1) * HEAD_DIM)
            oh = o_ref[:, sl]
            dm = dmix_ref[:, sl]
            r = lax.rsqrt(jnp.mean(oh * oh, axis=-1, keepdims=True) + EPS)
            oh_hat = oh * r
            gt = gate[:, sl]
            sg = _silu(gt)
            dp_ref[:, sl] = (dm * oh_hat * ong_ref[...] * _dsilu(gt)).astype(BF16)
            dn_ = dm * sg
            gong_ref[...] += jnp.sum(dn_ * oh_hat, axis=0, keepdims=True)
            dhat = dn_ * ong_ref[...]
            do_ref[:, sl] = r * (dhat - oh_hat * jnp.mean(dhat * oh_hat, axis=-1, keepdims=True))
        for gi in range(SG_GROUPS):
            sl = slice(gi * SG_BLOCK, (gi + 1) * SG_BLOCK)
            vraw = vg_ref[:, sl]
            gv = _gelu(vraw)
            r = lax.rsqrt(jnp.mean(gv * gv, axis=-1, keepdims=True) + EPS)
            vhat = gv * r
            vn = vhat * sgn_ref[:, sl]
            wm = jnp.where(mask, sgw_ref[gi], 0.0)
            s = _nn(wm, vn) + sgbt_ref[:, gi:gi + 1]
            uraw = u_ref[:, sl]
            dm = dmix_ref[:, DN_WIDTH + gi * SG_BLOCK:DN_WIDTH + (gi + 1) * SG_BLOCK]
            dp_ref[:, DN_WIDTH + gi * SG_BLOCK:DN_WIDTH + (gi + 1) * SG_BLOCK] = (dm * s * _dgelu(uraw)).astype(BF16)
            ds = dm * _gelu(uraw)
            gsgbt_ref[...] += jnp.where(lane == gi, jnp.sum(ds, axis=-1, keepdims=True), 0.0)
            gsgw_ref[gi] += jnp.where(mask, _nt(ds, vn), 0.0)
            dvn = _tn(wm, ds)
            gsgn_ref[:, sl] += jnp.sum(dvn * vhat, axis=0, keepdims=True)
            dhat = dvn * sgn_ref[:, sl]
            dgv = r * (dhat - vhat * jnp.mean(dhat * vhat, axis=-1, keepdims=True))
            dp_ref[:, 2 * DN_WIDTH + gi * SG_BLOCK:2 * DN_WIDTH + (gi + 1) * SG_BLOCK] = (dgv * _dgelu(vraw)).astype(BF16)

    def col(c):
        return pl.BlockSpec((rb, 512), lambda i: (i, c))
    full = lambda *s: pl.BlockSpec(s, lambda i: (0,) * len(s))
    in_specs, args = _with_dep(
        [pl.BlockSpec((rb, DN_WIDTH), lambda i: (i, 0)), col(3), col(4), col(5),
         full(1, 128), full(1, SG_WIDTH), full(SG_GROUPS, SG_BLOCK, SG_BLOCK), full(SG_BLOCK, 128),
         pl.BlockSpec((rb, D_MODEL), lambda i: (i, 0))],
        [o, p, p, p, ong, sgn, sgw, sgbt, dmix], dep)
    return pl.pallas_call(
        body, name="mix_bwd", grid=(T // rb,), in_specs=in_specs,
        out_specs=[pl.BlockSpec((rb, DN_WIDTH), lambda i: (i, 0)), pl.BlockSpec((rb, 3 * 512), lambda i: (i, 0)),
                   full(1, 128), full(1, SG_WIDTH), full(SG_GROUPS, SG_BLOCK, SG_BLOCK), full(SG_BLOCK, 128)],
        out_shape=[jax.ShapeDtypeStruct((T, DN_WIDTH), F32), jax.ShapeDtypeStruct((T, 3 * 512), BF16),
                   jax.ShapeDtypeStruct((1, 128), F32), jax.ShapeDtypeStruct((1, SG_WIDTH), F32),
                   jax.ShapeDtypeStruct((SG_GROUPS, SG_BLOCK, SG_BLOCK), F32),
                   jax.ShapeDtypeStruct((SG_BLOCK, 128), F32)],
        compiler_params=_cp("arbitrary"))(*args)


def _pad_lanes(row, offset=0):
    n = row.shape[1]
    return jnp.pad(row, ((0, 0), (offset, 128 - n - offset)))


def _local_step(x, tgt, w, dep=None, late_weights=None, on_grad=None):
    T = x.shape[0]
    N = T // CHUNK
    on_grad = on_grad or (lambda name, g: None)
    alog_row = _pad_lanes(w["dn_a_log"], N_HEADS)
    dtb_row = _pad_lanes(w["dn_dt_bias"], N_HEADS)
    sgbt = jnp.pad(w["sg_b"].T, ((0, 0), (0, 128 - SG_GROUPS)))

    p, h1, w_in_pad = _in_proj(x, w["attn_norm_g"], w["w_in"], dep=dep)
    q, k, v, bg = _dn_act(p, w["dn_conv_w"], alog_row, dtb_row)
    gc, gct, lmat = _dn_chunk(k, bg)
    lt = lmat.reshape(N * N_HEADS, CHUNK * CHUNK).T
    at = _tri_inv(lt)
    a = at.reshape(CHUNK * CHUNK, N * N_HEADS).T.reshape(N, N_HEADS, CHUNK, CHUNK)
    a_t = at.transpose(1, 0, 2).reshape(CHUNK * CHUNK, N * N_HEADS).T.reshape(N, N_HEADS, CHUNK, CHUNK)
    o, sall = _dn_scan(q, k, v, bg, gc, gct, a)
    mix = _mix_fwd(o, p, w["dn_out_norm_g"], w["sg_norm_g"], w["sg_w"], sgbt)
    if late_weights is not None:
        w = {**w, **late_weights(mix)}
    x2, h2 = _out_proj(mix, w["w_out"], x, w["ffn_norm_g"])
    up = _mm_nn("up_proj", h2, w["w_up"], F32, 512, D_FF)
    act = _ffn_act(up, w["ffn_conv_w"], w["ffn_conv_b"])
    loss, dx3, g_final = _down_proj_loss(act, w["w_down"], x2, tgt, w["final_norm_g"])

    dact = _mm_nt("d_act", dx3, w["w_down"], F32, 512, D_FF)
    g_w_down = _mm_tn("g_w_down", act, dx3, D_FF, 1024, 1024)
    tok = on_grad("w_down", g_w_down)
    dup, g_ffn_conv_w, g_ffn_conv_b = _ffn_act_bwd(up, dact, w["ffn_conv_w"], w["ffn_conv_b"], dep=tok)
    g_w_up = _mm_tn("g_w_up", h2, dup, 1024, 2 * D_FF // 4, 2048, col_major_tiles=True)
    tok = on_grad("w_up", g_w_up)
    dx2, g_ffn_norm = _mm_nt_rms_bwd("d_h2", dup, w["w_up"], x2, w["ffn_norm_g"], dx3, dep=tok)
    dmix = _mm_nt("d_mix", dx2, w["w_out"], F32, 512, 1024)
    g_w_out = _mm_tn("g_w_out", mix, dx2, 1024, 1024, 1024)
    tok = on_grad("w_out", g_w_out)
    do, dp_mid, g_ong, g_sgn, g_sgw, g_sgbt = _mix_bwd(o, p, w["dn_out_norm_g"], w["sg_norm_g"], w["sg_w"], sgbt,
                                                      dmix, dep=tok)
    early = dict(dn_out_norm_g=g_ong, sg_norm_g=g_sgn, sg_w=g_sgw, sg_b=g_sgbt[:, :SG_GROUPS].T,
                 ffn_norm_g=g_ffn_norm, ffn_conv_w=g_ffn_conv_w, ffn_conv_b=g_ffn_conv_b, final_norm_g=g_final)
    tok = on_grad("small_early", early)
    dq, dk, dv, dbg = _dn_scan_bwd(q, k, v, bg, gc, gct, a, a_t, sall, do, dep=tok)
    dp, g_dn_conv_w, g_ad = _dn_act_bwd(p, w["dn_conv_w"], alog_row, dtb_row, dq, dk, dv, dbg, dp_mid)
    g_w_in = _mm_tn("g_w_in", h1, dp, 1024, PROJ_PAD, 1024, col_groups=(4, PROJ_COLS // 4))
    tok = on_grad("w_in", g_w_in)
    grad_x, g_attn_norm = _mm_nt_rms_bwd("d_h1", dp, w_in_pad, x, w["attn_norm_g"], dx2, dep=tok)

    grads = dict(
        attn_norm_g=g_attn_norm, w_in=g_w_in, dn_conv_w=g_dn_conv_w,
        dn_a_log=g_ad[0:1, N_HEADS:2 * N_HEADS], dn_dt_bias=g_ad[1:2, N_HEADS:2 * N_HEADS],
        w_out=g_w_out, w_up=g_w_up, w_down=g_w_down, **early)
    return loss, grad_x, grads


def _me():
    return lax.axis_index("x"), lax.axis_index("y"), lax.axis_index("c")


def _peer(rel):
    x, y, c = _me()
    return {"x": (1 - x, y, c), "y": (x, 1 - y, c), "xy": (1 - x, 1 - y, c), "c": (x, y, 1 - c)}[rel]


def _chip_of(dev):
    return 2 * dev[0] + dev[1]


CHIP_RELS = ("x", "y", "xy")


def _run_copies(copies, sends, recvs):
    for cp in copies:
        cp.start()
    for cp in recvs:
        cp.wait_recv()
    for cp in sends:
        cp.wait_send()


def _gather_first(w_shard, small_shard):
    R = w_shard.shape[0]
    r2 = R // 2

    def body(w_ref, s_ref, w_out, s_out, send_sems, recv_sems):
        x, y, c = _me()
        me = _chip_of((x, y))
        sib = _peer("c")

        def half(chip, core):
            return w_out.at[chip, pl.ds(pl.multiple_of(core * r2, 8), r2), :]

        def copy(k, src, dst, to):
            return pltpu.make_async_remote_copy(src_ref=src, dst_ref=dst, send_sem=send_sems.at[k],
                                                recv_sem=recv_sems.at[k], device_id=to, device_id_type=MESH)

        own_rows = w_ref.at[pl.ds(pl.multiple_of(c * r2, 8), r2), :]
        first = [copy(r, own_rows, half(me, c), _peer(rel)) for r, rel in enumerate(CHIP_RELS)]
        first += [copy(3 + r, s_ref, s_out.at[me], _peer(rel)) for r, rel in enumerate(CHIP_RELS)]
        for cp in first:
            cp.start()
        passed = []
        for r, rel in enumerate(CHIP_RELS):
            their = _chip_of(_peer(rel))
            copy(r, own_rows, half(their, c), _peer(rel)).wait_recv()
            fwd = copy(6 + r, half(their, c), half(their, c), sib)
            fwd.start()
            passed.append(fwd)
        for r, rel in enumerate(CHIP_RELS):
            their = _chip_of(_peer(rel))
            copy(3 + r, s_ref, s_out.at[their], _peer(rel)).wait_recv()
            copy(6 + r, own_rows, half(their, 1 - c), sib).wait_recv()
        for cp in first + passed:
            cp.wait_send()

    w_all, s_all = pl.pallas_call(
        body, name="gather_first", in_specs=[ANY, ANY], out_specs=[ANY, ANY],
        out_shape=[jax.ShapeDtypeStruct((4,) + w_shard.shape, w_shard.dtype),
                   jax.ShapeDtypeStruct((4,) + small_shard.shape, small_shard.dtype)],
        scratch_shapes=[pltpu.SemaphoreType.DMA((9,)), pltpu.SemaphoreType.DMA((9,))])(w_shard, small_shard)
    me = _chip_of(_me())
    return (lax.dynamic_update_index_in_dim(w_all, w_shard, me, 0),
            lax.dynamic_update_index_in_dim(s_all, small_shard, me, 0))


OTHERS = tuple((fx, fy, fc) for fx in (0, 1) for fy in (0, 1) for fc in (0, 1) if (fx, fy, fc) != (0, 0, 0))


def _other(flip):
    x, y, c = _me()
    return (x ^ flip[0], y ^ flip[1], c ^ flip[2])


def _linear(dev):
    return 4 * dev[0] + 2 * dev[1] + dev[2]


def _exchange_small(small):
    def body(small_ref, out_ref, send_sems, recv_sems):
        my_slot = _linear(_me())
        sends, recvs = [], []
        for k, flip in enumerate(OTHERS):
            peer = _other(flip)
            sends.append(pltpu.make_async_remote_copy(
                src_ref=small_ref, dst_ref=out_ref.at[my_slot], send_sem=send_sems.at[k], recv_sem=recv_sems.at[k],
                device_id=peer, device_id_type=MESH))
            recvs.append(pltpu.make_async_remote_copy(
                src_ref=small_ref, dst_ref=out_ref.at[_linear(peer)], send_sem=send_sems.at[k],
                recv_sem=recv_sems.at[k], device_id=peer, device_id_type=MESH))
        _run_copies(sends, sends, recvs)

    out = pl.pallas_call(
        body, name="exchange_small", in_specs=[ANY], out_specs=ANY,
        out_shape=jax.ShapeDtypeStruct((8,) + small.shape, small.dtype),
        scratch_shapes=[pltpu.SemaphoreType.DMA((7,)), pltpu.SemaphoreType.DMA((7,))])(small)
    return lax.dynamic_update_index_in_dim(out, small, _linear(_me()), 0)


def _pair_swap(halves):
    n = len(halves)

    def body(*refs):
        src, out = refs[:n], refs[n:2 * n]
        send_sems, recv_sems = refs[2 * n:]
        sib = _peer("c")
        copies = [pltpu.make_async_remote_copy(
            src_ref=src[i], dst_ref=out[i], send_sem=send_sems.at[i], recv_sem=recv_sems.at[i],
            device_id=sib, device_id_type=MESH) for i in range(n)]
        _run_copies(copies, copies, copies)

    return pl.pallas_call(
        body, name="pair_swap", in_specs=[ANY] * n, out_specs=[ANY] * n,
        out_shape=[jax.ShapeDtypeStruct(h.shape, h.dtype) for h in halves],
        scratch_shapes=[pltpu.SemaphoreType.DMA((n,)), pltpu.SemaphoreType.DMA((n,))])(*halves)


HBM = pl.BlockSpec(memory_space=pltpu.HBM)
SEM = pl.BlockSpec(memory_space=pltpu.SEMAPHORE)
EFFECT = pltpu.SideEffectType.DATAFLOW_SIDE_EFFECTING


def _hbm(a):
    return pltpu.with_memory_space_constraint(a, pltpu.HBM)


def _transfer_start(name, srcs, lands, n_copies, make_copies, after=None):
    n, m = len(srcs), len(lands)

    def body(*refs):
        src, land = refs[:n], refs[n:n + m]
        outs = refs[n + m + (after is not None):]
        send_sems, recv_sems, token = outs[0], outs[1], outs[-1]
        for cp in make_copies(src, land, send_sems, recv_sems):
            cp.start()
        token[...] = jnp.zeros_like(token)

    arrs = list(srcs) + list(lands)
    in_specs, args = _with_dep([HBM] * (n + m), [_hbm(a) for a in arrs], after)
    out = pl.pallas_call(
        body, name=name,
        out_shape=(pltpu.SemaphoreType.DMA((n_copies,)), pltpu.SemaphoreType.DMA((n_copies,)),
                   *[pltpu.HBM(a.shape, a.dtype) for a in arrs], jax.ShapeDtypeStruct((8, 128), F32)),
        in_specs=in_specs,
        out_specs=(SEM, SEM, *[HBM] * (n + m), pl.BlockSpec(memory_space=pltpu.VMEM)),
        input_output_aliases={i: 2 + i for i in range(n + m)},
        compiler_params=pltpu.CompilerParams(has_side_effects=EFFECT))(*args)
    return out[0], out[1], list(out[2:2 + n]), list(out[2 + n:2 + n + m]), out[-1]


def _transfer_wait(name, send_sems, recv_sems, srcs, lands, make_copies, after):
    n, m = len(srcs), len(lands)

    def body(*refs):
        src, land = refs[:n], refs[n:n + m]
        s_sems, r_sems = refs[n + m], refs[n + m + 1]
        for cp in make_copies(src, land, s_sems, r_sems):
            cp.wait_send()
            cp.wait_recv()

    arrs = list(srcs) + list(lands)
    out = pl.pallas_call(
        body, name=name, out_shape=tuple(pltpu.HBM(a.shape, a.dtype) for a in arrs),
        in_specs=[HBM] * (n + m) + [SEM, SEM, ANY], out_specs=tuple([HBM] * (n + m)),
        input_output_aliases={i: i for i in range(n + m)},
        compiler_params=pltpu.CompilerParams(has_side_effects=EFFECT))(*arrs, send_sems, recv_sems, after)
    return list(out[:n]), list(out[n:])


def _gather_copies(src, land, send_sems, recv_sems):
    me = _chip_of(_me())
    copies = []
    for i in range(len(src)):
        for r, rel in enumerate(CHIP_RELS):
            k = 3 * i + r
            copies.append(pltpu.make_async_remote_copy(
                src_ref=src[i], dst_ref=land[i].at[me], send_sem=send_sems.at[k], recv_sem=recv_sems.at[k],
                device_id=_peer(rel), device_id_type=MESH))
    return copies


def _small_copies(src, land, send_sems, recv_sems):
    my_slot = _linear(_me())
    return [pltpu.make_async_remote_copy(
        src_ref=src[0], dst_ref=land[0].at[my_slot], send_sem=send_sems.at[k], recv_sem=recv_sems.at[k],
        device_id=_other(flip), device_id_type=MESH) for k, flip in enumerate(OTHERS)]


def _pieces_copies(src, land, send_sems, recv_sems):
    copies = []
    for k, flip in enumerate(OTHERS):
        peer = _other(flip)
        copies.append(pltpu.make_async_remote_copy(
            src_ref=src[0].at[_linear(peer)], dst_ref=land[0].at[k], send_sem=send_sems.at[k],
            recv_sem=recv_sems.at[k], device_id=peer, device_id_type=MESH))
    return copies


def _row_block(rows, cols, budget=2 * 1024 * 1024):
    rb = max(8, (budget // (4 * cols)) // 8 * 8)
    while rows % rb:
        rb -= 8
    return rb if rb > 0 else rows


def _sum_slots(name, first, rest):
    R, Cc = first.shape
    K = rest.shape[0]
    rb = _row_block(R, Cc)

    def body(f_ref, r_ref, o_ref):
        acc = f_ref[...].astype(F32)
        for j in range(K):
            acc = acc + r_ref[j].astype(F32)
        o_ref[...] = acc

    return pl.pallas_call(
        body, name=name, grid=(R // rb,),
        in_specs=[pl.BlockSpec((rb, Cc), lambda i: (i, 0)), pl.BlockSpec((K, rb, Cc), lambda i: (0, i, 0))],
        out_specs=pl.BlockSpec((rb, Cc), lambda i: (i, 0)),
        out_shape=jax.ShapeDtypeStruct((R, Cc), F32), compiler_params=_cp("parallel"))(first, rest)


def _adamw_math(w, gv, m, v):
    mn = ADAM_B1 * m + (1.0 - ADAM_B1) * gv
    vn = ADAM_B2 * v + (1.0 - ADAM_B2) * (gv * gv)
    m_hat = mn / (1.0 - ADAM_B1 ** ADAM_STEP)
    v_hat = vn / (1.0 - ADAM_B2 ** ADAM_STEP)
    return -ADAM_LR * (m_hat / (jnp.sqrt(v_hat) + ADAM_EPS) + ADAM_WD * w), mn, vn


def _adamw_halves(name, w, mine, theirs, m, v, core):
    R, Cc = w.shape
    r2 = R // 2
    rb = _row_block(r2, Cc, 1024 * 1024)
    nb2 = r2 // rb

    def body(c_ref, w_ref, mine_ref, theirs_ref, m_ref, v_ref, g_ref, d_ref, mo_ref, vo_ref):
        is_mine = (pl.program_id(0) // nb2) == c_ref[0]
        gv = jnp.where(is_mine, mine_ref[...], theirs_ref[...])
        g_ref[...] = gv
        d_ref[...], mo_ref[...], vo_ref[...] = _adamw_math(w_ref[...], gv, m_ref[...], v_ref[...])

    blk = pl.BlockSpec((rb, Cc), lambda i, c: (i, 0))
    half = lambda own: pl.BlockSpec(
        (rb, Cc), lambda i, c: (jnp.clip(i - (c[0] if own else 1 - c[0]) * nb2, 0, nb2 - 1), 0))
    return pl.pallas_call(
        body, name=name,
        grid_spec=pltpu.PrefetchScalarGridSpec(
            num_scalar_prefetch=1, grid=(2 * nb2,), in_specs=[blk, half(True), half(False), blk, blk],
            out_specs=[blk] * 4),
        out_shape=[jax.ShapeDtypeStruct((R, Cc), F32)] * 4, compiler_params=_cp("parallel"))(core, w, mine, theirs, m, v)


def _adamw(name, w, g, m, v):
    R, Cc = w.shape
    rb = _row_block(R, Cc, 1024 * 1024)

    def body(w_ref, g_ref, m_ref, v_ref, d_ref, mo_ref, vo_ref):
        d_ref[...], mo_ref[...], vo_ref[...] = _adamw_math(w_ref[...], g_ref[...], m_ref[...], v_ref[...])

    blk = pl.BlockSpec((rb, Cc), lambda i: (i, 0))
    return pl.pallas_call(
        body, name=name, grid=(R // rb,), in_specs=[blk] * 4, out_specs=[blk] * 3,
        out_shape=[jax.ShapeDtypeStruct((R, Cc), F32)] * 3, compiler_params=_cp("parallel"))(w, g, m, v)


def _pack(arrs):
    rows = []
    for a in arrs:
        flat = a.reshape(-1)
        pad = (-flat.shape[0]) % 128
        rows.append(jnp.pad(flat, (0, pad)).reshape(-1, 128))
    buf = jnp.concatenate(rows, axis=0)
    return jnp.pad(buf, ((0, (-buf.shape[0]) % 8), (0, 0)))


def _unpack(buf, shapes):
    out, r = [], 0
    for s in shapes:
        n = math.prod(s)
        nr = -(-n // 128)
        out.append(buf[r:r + nr].reshape(-1)[:n].reshape(s))
        r += nr
    return out


BIG = ("w_in", "w_out", "w_up", "w_down")
CONV = ("dn_conv_w", "ffn_conv_w")
REPL = ("attn_norm_g", "dn_a_log", "dn_dt_bias", "dn_out_norm_g", "sg_norm_g", "sg_w", "sg_b",
        "ffn_norm_g", "ffn_conv_b", "final_norm_g")
ORDER = ("attn_norm_g", "w_in", "dn_conv_w", "dn_a_log", "dn_dt_bias", "dn_out_norm_g", "sg_norm_g", "sg_w",
         "sg_b", "w_out", "ffn_norm_g", "w_up", "ffn_conv_w", "ffn_conv_b", "w_down", "final_norm_g")


def kernel(x, attn_norm_g, w_in, dn_conv_w, dn_a_log, dn_dt_bias, dn_out_norm_g, sg_norm_g, sg_w, sg_b, w_out, ffn_norm_g, w_up, ffn_conv_w, ffn_conv_b, w_down, final_norm_g, loss_target, m_attn_norm_g, m_w_in, m_dn_conv_w, m_dn_a_log, m_dn_dt_bias, m_dn_out_norm_g, m_sg_norm_g, m_sg_w, m_sg_b, m_w_out, m_ffn_norm_g, m_w_up, m_ffn_conv_w, m_ffn_conv_b, m_w_down, m_final_norm_g, v_attn_norm_g, v_w_in, v_dn_conv_w, v_dn_a_log, v_dn_dt_bias, v_dn_out_norm_g, v_sg_norm_g, v_sg_w, v_sg_b, v_w_out, v_ffn_norm_g, v_w_up, v_ffn_conv_w, v_ffn_conv_b, v_w_down, v_final_norm_g):
    W = dict(attn_norm_g=attn_norm_g, w_in=w_in, dn_conv_w=dn_conv_w, dn_a_log=dn_a_log, dn_dt_bias=dn_dt_bias,
             dn_out_norm_g=dn_out_norm_g, sg_norm_g=sg_norm_g, sg_w=sg_w, sg_b=sg_b, w_out=w_out,
             ffn_norm_g=ffn_norm_g, w_up=w_up, ffn_conv_w=ffn_conv_w, ffn_conv_b=ffn_conv_b, w_down=w_down,
             final_norm_g=final_norm_g)
    Mo = dict(attn_norm_g=m_attn_norm_g, w_in=m_w_in, dn_conv_w=m_dn_conv_w, dn_a_log=m_dn_a_log,
              dn_dt_bias=m_dn_dt_bias, dn_out_norm_g=m_dn_out_norm_g, sg_norm_g=m_sg_norm_g, sg_w=m_sg_w,
              sg_b=m_sg_b, w_out=m_w_out, ffn_norm_g=m_ffn_norm_g, w_up=m_w_up, ffn_conv_w=m_ffn_conv_w,
              ffn_conv_b=m_ffn_conv_b, w_down=m_w_down, final_norm_g=m_final_norm_g)
    Vo = dict(attn_norm_g=v_attn_norm_g, w_in=v_w_in, dn_conv_w=v_dn_conv_w, dn_a_log=v_dn_a_log,
              dn_dt_bias=v_dn_dt_bias, dn_out_norm_g=v_dn_out_norm_g, sg_norm_g=v_sg_norm_g, sg_w=v_sg_w,
              sg_b=v_sg_b, w_out=v_w_out, ffn_norm_g=v_ffn_norm_g, w_up=v_w_up, ffn_conv_w=v_ffn_conv_w,
              ffn_conv_b=v_ffn_conv_b, w_down=v_w_down, final_norm_g=v_final_norm_g)
    xi, yi, ci = lax.axis_index("x"), lax.axis_index("y"), lax.axis_index("c")
    chip = 2 * xi + yi

    me_lin = 4 * xi + 2 * yi + ci

    g_in, g_dnc = _gather_first(w_in[0].astype(BF16), dn_conv_w[0])
    late = ("w_out", "w_up", "w_down", "ffn_conv_w")
    late_shards = [W[n][0].astype(BF16) for n in late[:3]] + [ffn_conv_w[0]]
    late_lands = [lax.dynamic_update_index_in_dim(lax.empty((4,) + s.shape, s.dtype), s, chip, 0) for s in late_shards]
    n_late = 3 * len(late_shards)
    ssem, rsem, late_src, late_lands, token = _transfer_start("gather_rest_start", late_shards, late_lands,
                                                              n_late, _gather_copies, after=g_in)

    def late_weights(after):
        _, (g_out, g_up, g_down, g_ffc) = _transfer_wait("gather_rest_wait", ssem, rsem, late_src, late_lands,
                                                         _gather_copies, after)
        return dict(w_out=g_out.reshape(D_MODEL, D_MODEL), w_up=g_up.transpose(1, 0, 2).reshape(D_MODEL, 2 * D_FF),
                    w_down=g_down.reshape(D_FF, D_MODEL), ffn_conv_w=g_ffc.transpose(1, 0, 2).reshape(3, 2 * D_FF))

    full = dict(
        w_in=g_in,
        dn_conv_w=g_dnc.transpose(1, 0, 2).reshape(4, 3 * DN_WIDTH),
        attn_norm_g=attn_norm_g, dn_a_log=dn_a_log, dn_dt_bias=dn_dt_bias, dn_out_norm_g=dn_out_norm_g,
        sg_norm_g=sg_norm_g, sg_w=sg_w[0], sg_b=sg_b[0], ffn_norm_g=ffn_norm_g, ffn_conv_b=ffn_conv_b,
        final_norm_g=final_norm_g[None])

    pending = {}
    early_names = ("dn_out_norm_g", "sg_norm_g", "sg_w", "sg_b", "ffn_norm_g", "ffn_conv_w", "ffn_conv_b",
                   "final_norm_g")
    late_names = ("attn_norm_g", "dn_a_log", "dn_dt_bias", "dn_conv_w")

    def on_grad(name, gw):
        if name == "small_early":
            buf = _pack([gw[n] for n in early_names])
            land = lax.dynamic_update_index_in_dim(lax.empty((8,) + buf.shape, F32), buf, me_lin, 0)
            s_sem, r_sem, src, lands, tok = _transfer_start("small_early_start", [buf], [land], 7, _small_copies)
            pending[name] = (s_sem, r_sem, src, lands)
            return tok
        g8 = gw.reshape(8, -1, gw.shape[-1])
        land = lax.empty((7,) + g8.shape[1:], BF16)
        s_sem, r_sem, src, lands, tok = _transfer_start(f"reduce_{name}_start", [g8], [land], 7, _pieces_copies)
        pending[name] = (s_sem, r_sem, src, lands)
        return tok

    loss_row, grad_x, g = _local_step(x[0], loss_target[0], full, dep=token, late_weights=late_weights,
                                      on_grad=on_grad)

    small_names = REPL + CONV
    late_all = _exchange_small(_pack([g[n] for n in late_names] + [loss_row]))
    late_sum = _sum_slots("sum_small_late", late_all[0], late_all[1:])
    s_sem, r_sem, src, lands = pending["small_early"]
    _, (early_all,) = _transfer_wait("small_early_wait", s_sem, r_sem, src, lands, _small_copies, grad_x)
    early_sum = _sum_slots("sum_small_early", early_all[0], early_all[1:])
    *late_vals, loss_sum = _unpack(late_sum, [g[n].shape for n in late_names] + [loss_row.shape])
    loss = loss_sum[0, 0]
    sg = dict(zip(late_names, late_vals))
    sg.update(zip(early_names, _unpack(early_sum, [g[n].shape for n in early_names])))
    sg["dn_conv_w"] = lax.dynamic_slice_in_dim(sg["dn_conv_w"], chip * (3 * DN_WIDTH // 4), 3 * DN_WIDTH // 4, axis=1)
    sg["ffn_conv_w"] = lax.dynamic_slice_in_dim(sg["ffn_conv_w"], chip * (2 * D_FF // 4), 2 * D_FF // 4, axis=1)

    halves = []
    for n in ("w_down", "w_up", "w_out", "w_in"):
        s_sem, r_sem, src, lands = pending[n]
        sent, got = _transfer_wait(f"reduce_{n}_wait", s_sem, r_sem, src, lands, _pieces_copies, grad_x)
        own = lax.dynamic_index_in_dim(sent[0], me_lin, axis=0, keepdims=False)
        halves.append(_sum_slots(f"sum_{n}", own, got[0]))
    theirs = _pair_swap(halves)
    core = ci.astype(jnp.int32).reshape(1)
    grads, delta, new_m, new_v = {}, {}, {}, {}
    for n, mine_h, their_h in zip(("w_down", "w_up", "w_out", "w_in"), halves, theirs):
        shp = W[n].shape
        gr, d, mn, vn = _adamw_halves(f"adamw_{n}", W[n][0], mine_h, their_h, Mo[n][0], Vo[n][0], core)
        grads[n], delta[n], new_m[n], new_v[n] = gr.reshape(shp), d.reshape(shp), mn.reshape(shp), vn.reshape(shp)
    shapes = [W[n].shape for n in small_names]
    for n in small_names:
        grads[n] = sg[n].reshape(W[n].shape)
    d, mn, vn = _adamw("adamw_small", _pack([W[n] for n in small_names]), _pack([grads[n] for n in small_names]),
                       _pack([Mo[n] for n in small_names]), _pack([Vo[n] for n in small_names]))
    for dst, buf in ((delta, d), (new_m, mn), (new_v, vn)):
        dst.update(zip(small_names, _unpack(buf, shapes)))

    return (loss, grad_x[None], *[grads[n] for n in ORDER], *[delta[n] for n in ORDER],
            *[new_m[n] for n in ORDER], *[new_v[n] for n in ORDER])
```

```python
import functools
import math

import jax
import jax.numpy as jnp
from jax import lax
from jax.experimental import pallas as pl
from jax.experimental.pallas import tpu as pltpu

F32 = jnp.float32
BF16 = jnp.bfloat16

D_MODEL = 1024
CHUNK = 64
SCAN_CHUNKS = 2
HEAD_DIM = 128
N_HEADS = 4
DN_WIDTH = 512
SG_WIDTH = 512
SG_GROUPS = 4
SG_BLOCK = 128
D_FF = 2816
PROJ_COLS = 3080
PROJ_PAD = 3200
BA_COL = 3072
EPS = 1e-6
NEG = -1e30
VMEM_LIMIT = 56 * 1024 * 1024

ADAM_LR = 0.001
ADAM_B1 = 0.9
ADAM_B2 = 0.999
ADAM_EPS = 1e-08
ADAM_WD = 0.01
ADAM_STEP = 10

MESH = pl.DeviceIdType.MESH
ANY = pl.BlockSpec(memory_space=pl.ANY)


def _cp(*sem):
    return pltpu.CompilerParams(dimension_semantics=sem, vmem_limit_bytes=VMEM_LIMIT)


def _bf(a):
    return a.astype(BF16)


def _nn(a, b):
    return jnp.dot(_bf(a), _bf(b), preferred_element_type=F32)


def _nt(a, b):
    return lax.dot_general(_bf(a), _bf(b), (((1,), (1,)), ((), ())), preferred_element_type=F32)


def _tn(a, b):
    return lax.dot_general(_bf(a), _bf(b), (((0,), (0,)), ((), ())), preferred_element_type=F32)


def _split(a):
    hi = _bf(a)
    return hi, _bf(a - hi.astype(F32))


def _sigmoid(x):
    return 0.5 * jnp.tanh(0.5 * x) + 0.5


def _silu(x):
    return x * _sigmoid(x)


def _dsilu(x):
    s = _sigmoid(x)
    return s * (1.0 + x * (1.0 - s))


_GELU_C = math.sqrt(2.0 / math.pi)
_GELU_A = 0.044715


def _gelu(x):
    return 0.5 * x * (1.0 + jnp.tanh(_GELU_C * (x + _GELU_A * x * x * x)))


def _dgelu(x):
    t = jnp.tanh(_GELU_C * (x + _GELU_A * x * x * x))
    return 0.5 * (1.0 + t) + 0.5 * x * (1.0 - t * t) * _GELU_C * (1.0 + 3.0 * _GELU_A * x * x)


def _softplus(x):
    return jnp.maximum(x, 0.0) + jnp.log(1.0 + jnp.exp(-jnp.abs(x)))


def _mm_nn(name, a, b, out_dtype, tm, tn, res=None):
    M, K = a.shape
    N = b.shape[1]
    tm, tn = min(tm, M), min(tn, N)

    def body(*refs):
        a_ref, b_ref = refs[0], refs[1]
        o_ref = refs[-1]
        acc = _nn(a_ref[...], b_ref[...])
        if res is not None:
            acc = acc + refs[2][...]
        o_ref[...] = acc.astype(o_ref.dtype)

    in_specs = [pl.BlockSpec((tm, K), lambda j, i: (i, 0)), pl.BlockSpec((K, tn), lambda j, i: (0, j))]
    args = [a, b]
    if res is not None:
        in_specs.append(pl.BlockSpec((tm, tn), lambda j, i: (i, j)))
        args.append(res)
    return pl.pallas_call(
        body, name=name, grid=(N // tn, M // tm), in_specs=in_specs,
        out_specs=pl.BlockSpec((tm, tn), lambda j, i: (i, j)),
        out_shape=jax.ShapeDtypeStruct((M, N), out_dtype),
        compiler_params=_cp("parallel", "parallel"))(*args)


def _with_dep(in_specs, args, dep):
    if dep is None:
        return in_specs, args
    return in_specs + [ANY], args + [dep]


SUB_ROWS = 128


def _sub_blocks(tm):
    return [slice(r0, min(r0 + SUB_ROWS, tm)) for r0 in range(0, tm, SUB_ROWS)]


def _rms_hat(xv):
    r = lax.rsqrt(jnp.mean(xv * xv, axis=-1, keepdims=True) + EPS)
    return xv * r, r


def _rms_bwd_vals(dh, xh, r, g):
    dxh = dh * g
    return r * (dxh - xh * jnp.mean(dxh * xh, axis=-1, keepdims=True)), jnp.sum(dh * xh, axis=0, keepdims=True)


def _in_proj(x, g, w4, tm=512, dep=None):
    T, K = x.shape
    ng, _, wc = w4.shape
    tm = min(tm, T)

    def body(x_ref, g_ref, w4_ref, *rest):
        p_ref, h_ref, w_ref = rest[-3:]

        @pl.when(pl.program_id(0) == 0)
        def _():
            w_ref[:, ng * wc:] = jnp.zeros((K, PROJ_PAD - ng * wc), BF16)
            for j in range(ng):
                w_ref[:, j * wc:(j + 1) * wc] = w4_ref[j]
        for r in _sub_blocks(tm):
            xh, _ = _rms_hat(x_ref[r, :])
            h_ref[r, :] = (xh * g_ref[...]).astype(BF16)
        p_ref[...] = jnp.dot(h_ref[...], w_ref[...], preferred_element_type=F32)

    in_specs, args = _with_dep(
        [pl.BlockSpec((tm, K), lambda i: (i, 0)), pl.BlockSpec((1, K), lambda i: (0, 0)),
         pl.BlockSpec((ng, K, wc), lambda i: (0, 0, 0))], [x, g, w4], dep)
    return pl.pallas_call(
        body, name="in_proj", grid=(T // tm,), in_specs=in_specs,
        out_specs=[pl.BlockSpec((tm, PROJ_PAD), lambda i: (i, 0)), pl.BlockSpec((tm, K), lambda i: (i, 0)),
                   pl.BlockSpec((K, PROJ_PAD), lambda i: (0, 0))],
        out_shape=[jax.ShapeDtypeStruct((T, PROJ_PAD), F32), jax.ShapeDtypeStruct((T, K), BF16),
                   jax.ShapeDtypeStruct((K, PROJ_PAD), BF16)],
        compiler_params=_cp("arbitrary"))(*args)


def _out_proj(mix, w, x, g, tm=512):
    T, K = mix.shape
    Dm = w.shape[1]
    tm = min(tm, T)

    def body(a_ref, w_ref, x_ref, g_ref, x2_ref, h_ref):
        x2_ref[...] = _nn(a_ref[...], w_ref[...]) + x_ref[...]
        for r in _sub_blocks(tm):
            xh, _ = _rms_hat(x2_ref[r, :])
            h_ref[r, :] = (xh * g_ref[...]).astype(BF16)

    row = lambda width: pl.BlockSpec((tm, width), lambda i: (i, 0))
    return pl.pallas_call(
        body, name="out_proj", grid=(T // tm,),
        in_specs=[row(K), pl.BlockSpec((K, Dm), lambda i: (0, 0)), row(Dm), pl.BlockSpec((1, Dm), lambda i: (0, 0))],
        out_specs=[row(Dm), row(Dm)],
        out_shape=[jax.ShapeDtypeStruct((T, Dm), F32), jax.ShapeDtypeStruct((T, Dm), BF16)],
        compiler_params=_cp("parallel"))(mix, w, x, g)


def _down_proj_loss(act, w, x2, tgt, g, tm=512):
    T, K = act.shape
    Dm = w.shape[1]
    tm = min(tm, T)

    def body(a_ref, w_ref, x_ref, t_ref, g_ref, loss_ref, dx_ref, gg_ref):
        @pl.when(pl.program_id(0) == 0)
        def _():
            gg_ref[...] = jnp.zeros_like(gg_ref)
            loss_ref[...] = jnp.zeros_like(loss_ref)
        dx_ref[...] = _nn(a_ref[...], w_ref[...]) + x_ref[...]
        for r in _sub_blocks(tm):
            xh, rr = _rms_hat(dx_ref[r, :])
            e = xh * g_ref[...] - t_ref[r, :]
            loss_ref[...] += jnp.zeros_like(loss_ref) + (0.5 / Dm) * jnp.sum(e * e)
            dx, gg = _rms_bwd_vals(e * (1.0 / Dm), xh, rr, g_ref[...])
            dx_ref[r, :] = dx
            gg_ref[...] += gg

    row = lambda width: pl.BlockSpec((tm, width), lambda i: (i, 0))
    vec = pl.BlockSpec((1, Dm), lambda i: (0, 0))
    return pl.pallas_call(
        body, name="down_proj_loss", grid=(T // tm,),
        in_specs=[row(K), pl.BlockSpec((K, Dm), lambda i: (0, 0)), row(Dm), row(Dm), vec],
        out_specs=[pl.BlockSpec((1, 128), lambda i: (0, 0)), row(Dm), vec],
        out_shape=[jax.ShapeDtypeStruct((1, 128), F32), jax.ShapeDtypeStruct((T, Dm), F32),
                   jax.ShapeDtypeStruct((1, Dm), F32)],
        compiler_params=_cp("arbitrary"))(act, w, x2, tgt, g)


def _mm_nt_rms_bwd(name, a, b, x, g, dres, tm=512, dep=None):
    M, K = a.shape
    Dm = b.shape[0]
    tm = min(tm, M)

    def body(a_ref, b_ref, x_ref, g_ref, dres_ref, *rest):
        dx_ref, gg_ref = rest[-2:]

        @pl.when(pl.program_id(0) == 0)
        def _():
            gg_ref[...] = jnp.zeros_like(gg_ref)
        dx_ref[...] = _nt(a_ref[...], b_ref[...])
        for r in _sub_blocks(tm):
            xh, rr = _rms_hat(x_ref[r, :])
            dx, gg = _rms_bwd_vals(dx_ref[r, :], xh, rr, g_ref[...])
            dx_ref[r, :] = dres_ref[r, :] + dx
            gg_ref[...] += gg

    row = lambda width: pl.BlockSpec((tm, width), lambda i: (i, 0))
    vec = pl.BlockSpec((1, Dm), lambda i: (0, 0))
    in_specs, args = _with_dep([row(K), pl.BlockSpec((Dm, K), lambda i: (0, 0)), row(Dm), vec, row(Dm)],
                               [a, b, x, g, dres], dep)
    return pl.pallas_call(
        body, name=name, grid=(M // tm,), in_specs=in_specs, out_specs=[row(Dm), vec],
        out_shape=[jax.ShapeDtypeStruct((M, Dm), F32), jax.ShapeDtypeStruct((1, Dm), F32)],
        compiler_params=_cp("arbitrary"))(*args)


def _mm_nt(name, a, b, out_dtype, tm, tn, dep=None):
    M, K = a.shape
    N = b.shape[0]
    tm, tn = min(tm, M), min(tn, N)

    def body(a_ref, b_ref, *rest):
        o_ref = rest[-1]
        o_ref[...] = _nt(a_ref[...], b_ref[...]).astype(o_ref.dtype)

    in_specs, args = _with_dep(
        [pl.BlockSpec((tm, K), lambda i, j: (i, 0)), pl.BlockSpec((tn, K), lambda i, j: (j, 0))], [a, b], dep)
    return pl.pallas_call(
        body, name=name, grid=(M // tm, N // tn), in_specs=in_specs,
        out_specs=pl.BlockSpec((tm, tn), lambda i, j: (i, j)),
        out_shape=jax.ShapeDtypeStruct((M, N), out_dtype),
        compiler_params=_cp("parallel", "parallel"))(*args)


def _mm_tn(name, a, b, tm, tn, tk, col_major_tiles=False, col_groups=None):
    T, M = a.shape
    N = b.shape[1]
    tm, tn, tk = min(tm, M), min(tn, N), min(tk, T)
    nk = T // tk

    def body(a_ref, b_ref, o_ref, acc_ref):
        k = pl.program_id(2)

        @pl.when(k == 0)
        def _():
            acc_ref[...] = jnp.zeros_like(acc_ref)
        acc_ref[...] += _tn(a_ref[...], b_ref[...])

        @pl.when(k == nk - 1)
        def _():
            if col_groups:
                for j in range(col_groups[0]):
                    o_ref[j] = acc_ref[:, j * col_groups[1]:(j + 1) * col_groups[1]].astype(BF16)
            else:
                o_ref[...] = acc_ref[...].astype(BF16).reshape(o_ref.shape)

    if col_groups:
        assert tm == M and tn == N and col_groups[0] * col_groups[1] <= N
        out_spec = pl.BlockSpec((col_groups[0], M, col_groups[1]), lambda i, j, k: (0, 0, 0))
        out_shape = jax.ShapeDtypeStruct((col_groups[0], M, col_groups[1]), BF16)
    elif col_major_tiles:
        assert tm == M
        out_spec = pl.BlockSpec((1, tm, tn), lambda i, j, k: (j, 0, 0))
        out_shape = jax.ShapeDtypeStruct((N // tn, M, tn), BF16)
    else:
        out_spec = pl.BlockSpec((tm, tn), lambda i, j, k: (i, j))
        out_shape = jax.ShapeDtypeStruct((M, N), BF16)
    return pl.pallas_call(
        body, name=name, grid=(M // tm, N // tn, nk),
        in_specs=[pl.BlockSpec((tk, tm), lambda i, j, k: (k, i)), pl.BlockSpec((tk, tn), lambda i, j, k: (k, j))],
        out_specs=out_spec, out_shape=out_shape, scratch_shapes=[pltpu.VMEM((tm, tn), F32)],
        compiler_params=_cp("parallel", "parallel", "arbitrary"))(a, b)


def _halo_prev_spec(rb, width):
    return pl.BlockSpec((8, width), lambda i: (jnp.maximum(i * (rb // 8) - 1, 0), 0))


def _halo_next_spec(rb, width, T):
    return pl.BlockSpec((8, width), lambda i: (jnp.minimum((i + 1) * (rb // 8), T // 8 - 1), 0))


LANES = 128
FF_STRIPS = D_FF // LANES
ROW_CHUNK = 32


def _strip(j, base=0):
    return pl.ds(pl.multiple_of(base + j * LANES, LANES), LANES)


def _ffn_act(up, w, b, rb=256):
    T, W = up.shape
    rb = min(rb, T)

    def body(up_ref, halo_ref, w_ref, b_ref, act_ref, ext_scr):
        first = pl.program_id(0) == 0

        def strip(j, slot):
            halves = (_strip(j), _strip(j, D_FF))
            wv = [w_ref[:, cols] for cols in halves]
            bv = [b_ref[:, cols] for cols in halves]
            for h, cols in enumerate(halves):
                ext_scr[slot, h,0:8] = jnp.where(first, 0.0, halo_ref[:, cols])
                ext_scr[slot, h,8:] = up_ref[:, cols]
            for r0 in range(0, rb, ROW_CHUNK):
                n = min(ROW_CHUNK, rb - r0)
                c = [ext_scr[slot, h,6 + r0:6 + r0 + n] * wv[h][0:1] + ext_scr[slot, h,7 + r0:7 + r0 + n] * wv[h][1:2]
                     + ext_scr[slot, h,8 + r0:8 + r0 + n] * wv[h][2:3] + bv[h] for h in range(2)]
                act_ref[r0:r0 + n, halves[0]] = (_silu(c[0]) * c[1]).astype(BF16)

        def pair(jj, carry):
            strip(2 * jj, 0)
            strip(2 * jj + 1, 1)
            return carry

        lax.fori_loop(0, FF_STRIPS // 2, pair, 0)

    return pl.pallas_call(
        body, name="ffn_act", grid=(T // rb,),
        in_specs=[pl.BlockSpec((rb, W), lambda i: (i, 0)), _halo_prev_spec(rb, W),
                  pl.BlockSpec((3, W), lambda i: (0, 0)), pl.BlockSpec((1, W), lambda i: (0, 0))],
        out_specs=pl.BlockSpec((rb, D_FF), lambda i: (i, 0)),
        out_shape=jax.ShapeDtypeStruct((T, D_FF), BF16),
        scratch_shapes=[pltpu.VMEM((2, 2, rb + 8, LANES), F32)], compiler_params=_cp("parallel"))(up, up, w, b)


def _ffn_act_bwd(up, dact, w, b, rb=128, dep=None):
    T, W = up.shape
    rb = min(rb, T)
    nb = T // rb
    re = rb + 8

    def body(up_ref, prev_ref, next_ref, da_ref, danext_ref, w_ref, b_ref, *rest):
        dup_ref, gw_ref, gb_ref, ext_scr, dc_scr = rest[-5:]
        i = pl.program_id(0)

        @pl.when(i == 0)
        def _():
            gw_ref[...] = jnp.zeros_like(gw_ref)
            gb_ref[...] = jnp.zeros_like(gb_ref)
        last = i == nb - 1

        def fold8(a):
            return jnp.sum(a.reshape(a.shape[0] // 8, 8, LANES), axis=0)

        def strip(j, slot):
            halves = (_strip(j), _strip(j, D_FF))
            wv = [w_ref[:, cols] for cols in halves]
            bv = [b_ref[:, cols] for cols in halves]
            for h, cols in enumerate(halves):
                ext_scr[slot, h,0:8] = jnp.where(i > 0, prev_ref[:, cols], 0.0)
                ext_scr[slot, h,8:8 + rb] = up_ref[:, cols]
                ext_scr[slot, h,8 + rb:] = next_ref[:, cols]
            gb = [jnp.zeros((8, LANES), F32) for _ in range(2)]
            gw = [[jnp.zeros((8, LANES), F32) for _ in range(3)] for _ in range(2)]
            for r0 in range(0, re, ROW_CHUNK):
                n = min(ROW_CHUNK, re - r0)
                tp = [[ext_scr[slot, h,6 + k + r0:6 + k + r0 + n] for k in range(3)] for h in range(2)]
                c = [tp[h][0] * wv[h][0:1] + tp[h][1] * wv[h][1:2] + tp[h][2] * wv[h][2:3] + bv[h] for h in range(2)]
                if r0 < rb:
                    da = da_ref[r0:r0 + n, halves[0]]
                else:
                    da = jnp.where(last, 0.0, danext_ref[:, halves[0]])
                s = _sigmoid(c[0])
                gs = c[0] * s
                dcs = (da * c[1] * (s + gs * (1.0 - s)), da * gs)
                for h in range(2):
                    dc_scr[slot, h,r0:r0 + n] = dcs[h]
                    if r0 < rb:
                        gb[h] = gb[h] + fold8(dcs[h])
                        for k in range(3):
                            gw[h][k] = gw[h][k] + fold8(tp[h][k] * dcs[h])
            for r0 in range(0, rb, ROW_CHUNK):
                n = min(ROW_CHUNK, rb - r0)
                for h, cols in enumerate(halves):
                    dup = (dc_scr[slot, h,r0:r0 + n] * wv[h][2:3] + dc_scr[slot, h,r0 + 1:r0 + 1 + n] * wv[h][1:2]
                           + dc_scr[slot, h,r0 + 2:r0 + 2 + n] * wv[h][0:1])
                    dup_ref[r0:r0 + n, cols] = dup.astype(BF16)
            for h, cols in enumerate(halves):
                gb_ref[:, cols] += jnp.sum(gb[h], axis=0, keepdims=True)
                for k in range(3):
                    gw_ref[k:k + 1, cols] += jnp.sum(gw[h][k], axis=0, keepdims=True)

        def pair(jj, carry):
            strip(2 * jj, 0)
            strip(2 * jj + 1, 1)
            return carry

        lax.fori_loop(0, FF_STRIPS // 2, pair, 0)

    in_specs, args = _with_dep(
        [pl.BlockSpec((rb, W), lambda i: (i, 0)), _halo_prev_spec(rb, W), _halo_next_spec(rb, W, T),
         pl.BlockSpec((rb, D_FF), lambda i: (i, 0)), _halo_next_spec(rb, D_FF, T),
         pl.BlockSpec((3, W), lambda i: (0, 0)), pl.BlockSpec((1, W), lambda i: (0, 0))],
        [up, up, up, dact, dact, w, b], dep)
    return pl.pallas_call(
        body, name="ffn_act_bwd", grid=(nb,), in_specs=in_specs,
        out_specs=[pl.BlockSpec((rb, W), lambda i: (i, 0)), pl.BlockSpec((3, W), lambda i: (0, 0)),
                   pl.BlockSpec((1, W), lambda i: (0, 0))],
        out_shape=[jax.ShapeDtypeStruct((T, W), BF16), jax.ShapeDtypeStruct((3, W), F32),
                   jax.ShapeDtypeStruct((1, W), F32)],
        scratch_shapes=[pltpu.VMEM((2, 2, rb + 16, LANES), F32), pltpu.VMEM((2, 2, re, LANES), F32)],
        compiler_params=_cp("arbitrary"))(*args)


def _lane_iota(shape):
    return lax.broadcasted_iota(jnp.int32, shape, len(shape) - 1)


def _dn_act(p, conv_w, alog_row, dtb_row, rb=256):
    T = p.shape[0]
    rb = min(rb, T)
    W3 = 3 * DN_WIDTH

    def body(p_ref, halo_ref, ba_ref, w_ref, al_ref, dt_ref, q_ref, k_ref, v_ref, bg_ref, ext_scr):
        first = pl.program_id(0) == 0
        outs = (q_ref, k_ref, v_ref)
        for j in range(3 * N_HEADS):
            kind, h = divmod(j, N_HEADS)
            cols = slice(j * HEAD_DIM, (j + 1) * HEAD_DIM)
            cur = p_ref[:, cols]
            ext_scr[j, 0:8] = jnp.where(first, 0.0, halo_ref[:, cols])
            ext_scr[j, 8:] = cur
            wv = w_ref[:, cols]
            s = _silu(ext_scr[j, 5:5 + rb] * wv[0:1] + ext_scr[j, 6:6 + rb] * wv[1:2]
                      + ext_scr[j, 7:7 + rb] * wv[2:3] + cur * wv[3:4])
            if kind < 2:
                scale = HEAD_DIM ** -0.5 if kind == 0 else 1.0
                s = s * (lax.rsqrt(jnp.sum(s * s, axis=-1, keepdims=True) + EPS) * scale)
            outs[kind][:, h * HEAD_DIM:(h + 1) * HEAD_DIM] = s
        ba = ba_ref[...]
        lane = _lane_iota(ba.shape)
        beta = _sigmoid(ba)
        g = -jnp.exp(al_ref[...]) * _softplus(ba + dt_ref[...])
        bg_ref[...] = jnp.where(lane < N_HEADS, beta, jnp.where(lane < 2 * N_HEADS, g, 0.0))

    row512 = pl.BlockSpec((rb, DN_WIDTH), lambda i: (i, 0))
    row128 = pl.BlockSpec((rb, 128), lambda i: (i, 0))
    vec128 = pl.BlockSpec((1, 128), lambda i: (0, 0))
    return pl.pallas_call(
        body, name="dn_act", grid=(T // rb,),
        in_specs=[pl.BlockSpec((rb, W3), lambda i: (i, 0)), _halo_prev_spec(rb, W3),
                  pl.BlockSpec((rb, 128), lambda i: (i, BA_COL // 128)),
                  pl.BlockSpec((4, W3), lambda i: (0, 0)), vec128, vec128],
        out_specs=[row512, row512, row512, row128],
        out_shape=[jax.ShapeDtypeStruct((T, DN_WIDTH), F32)] * 3 + [jax.ShapeDtypeStruct((T, 128), F32)],
        scratch_shapes=[pltpu.VMEM((3 * N_HEADS, rb + 8, HEAD_DIM), F32)],
        compiler_params=_cp("parallel"))(p, p, p, conv_w, alog_row, dtb_row)


def _dn_act_bwd(p, conv_w, alog_row, dtb_row, dq, dk, dv, dbg, dp_mid, rb=256):
    T = p.shape[0]
    rb = min(rb, T)
    nb = T // rb
    re = rb + 8
    W3 = 3 * DN_WIDTH

    def body(p_ref, prev_ref, next_ref, ba_ref, w_ref, al_ref, dt_ref, dq_ref, dqn_ref, dk_ref, dkn_ref,
             dv_ref, dvn_ref, dbg_ref, mid_ref, draw_ref, gw_ref, gad_ref, ext_scr, dc_scr):
        i = pl.program_id(0)
        draw_ref[:, W3:2 * W3] = mid_ref[...]

        @pl.when(i == 0)
        def _():
            gw_ref[...] = jnp.zeros_like(gw_ref)
            gad_ref[...] = jnp.zeros_like(gad_ref)
        row = lax.broadcasted_iota(jnp.int32, (re, 1), 0)
        live = (row < rb) | (i < nb - 1)
        d_refs = ((dq_ref, dqn_ref), (dk_ref, dkn_ref), (dv_ref, dvn_ref))
        for j in range(3 * N_HEADS):
            kind, h = divmod(j, N_HEADS)
            cols = slice(j * HEAD_DIM, (j + 1) * HEAD_DIM)
            hcols = slice(h * HEAD_DIM, (h + 1) * HEAD_DIM)
            ext_scr[j, 0:8] = jnp.where(i > 0, prev_ref[:, cols], 0.0)
            ext_scr[j, 8:8 + rb] = p_ref[:, cols]
            ext_scr[j, 8 + rb:] = next_ref[:, cols]
            tp = [ext_scr[j, 5 + k:5 + k + re] for k in range(4)]
            wv = w_ref[:, cols]
            c = tp[0] * wv[0:1] + tp[1] * wv[1:2] + tp[2] * wv[2:3] + tp[3] * wv[3:4]
            sg = _sigmoid(c)
            s = c * sg
            d_in = jnp.where(live, jnp.concatenate([d_refs[kind][0][:, hcols], d_refs[kind][1][:, hcols]], axis=0), 0.0)
            if kind < 2:
                scale = HEAD_DIM ** -0.5 if kind == 0 else 1.0
                n = lax.rsqrt(jnp.sum(s * s, axis=-1, keepdims=True) + EPS)
                hat = s * n
                d_in = (n * scale) * (d_in - hat * jnp.sum(hat * d_in, axis=-1, keepdims=True))
            dc = d_in * (sg + s * (1.0 - sg))
            dc_scr[j] = dc
            dcc = dc[0:rb]
            draw = (dcc * wv[3:4] + dc_scr[j, 1:1 + rb] * wv[2:3] + dc_scr[j, 2:2 + rb] * wv[1:2]
                    + dc_scr[j, 3:3 + rb] * wv[0:1])
            draw_ref[:, cols] = draw.astype(BF16)
            for k in range(4):
                gw_ref[k:k + 1, cols] += jnp.sum(tp[k][0:rb] * dcc, axis=0, keepdims=True)
        ba = ba_ref[...]
        dbg = dbg_ref[...]
        lane = _lane_iota(ba.shape)
        beta = _sigmoid(ba)
        ea = jnp.exp(al_ref[...])
        z = ba + dt_ref[...]
        d_a = dbg * (-ea) * _sigmoid(z)
        dba = jnp.where(lane < N_HEADS, dbg * beta * (1.0 - beta), jnp.where(lane < 2 * N_HEADS, d_a, 0.0))
        draw_ref[:, BA_COL:] = dba.astype(BF16)
        isg = (lane >= N_HEADS) & (lane < 2 * N_HEADS)
        g = -ea * _softplus(z)
        gad_ref[0:1, :] += jnp.sum(jnp.where(isg, dbg * g, 0.0), axis=0, keepdims=True)
        gad_ref[1:2, :] += jnp.sum(jnp.where(isg, d_a, 0.0), axis=0, keepdims=True)

    row512 = pl.BlockSpec((rb, DN_WIDTH), lambda i: (i, 0))
    row128 = pl.BlockSpec((rb, 128), lambda i: (i, 0))
    vec128 = pl.BlockSpec((1, 128), lambda i: (0, 0))
    next512 = _halo_next_spec(rb, DN_WIDTH, T)
    return pl.pallas_call(
        body, name="dn_act_bwd", grid=(nb,),
        in_specs=[pl.BlockSpec((rb, W3), lambda i: (i, 0)), _halo_prev_spec(rb, W3), _halo_next_spec(rb, W3, T),
                  pl.BlockSpec((rb, 128), lambda i: (i, BA_COL // 128)),
                  pl.BlockSpec((4, W3), lambda i: (0, 0)), vec128, vec128,
                  row512, next512, row512, next512, row512, next512, row128,
                  pl.BlockSpec((rb, W3), lambda i: (i, 0))],
        out_specs=[pl.BlockSpec((rb, PROJ_PAD), lambda i: (i, 0)),
                   pl.BlockSpec((4, W3), lambda i: (0, 0)), pl.BlockSpec((2, 128), lambda i: (0, 0))],
        out_shape=[jax.ShapeDtypeStruct((T, PROJ_PAD), BF16),
                   jax.ShapeDtypeStruct((4, W3), F32), jax.ShapeDtypeStruct((2, 128), F32)],
        scratch_shapes=[pltpu.VMEM((3 * N_HEADS, rb + 16, HEAD_DIM), F32), pltpu.VMEM((3 * N_HEADS, re, HEAD_DIM), F32)],
        compiler_params=_cp("arbitrary"))(p, p, p, p, conv_w, alog_row, dtb_row, dq, dq, dk, dk, dv, dv, dbg, dp_mid)


def _tri(incl):
    ii = lax.broadcasted_iota(jnp.int32, (CHUNK, CHUNK), 0)
    jj = lax.broadcasted_iota(jnp.int32, (CHUNK, CHUNK), 1)
    return ii, jj, ((ii >= jj) if incl else (ii > jj))


def _dn_chunk(k, bg, cb=4):
    T = k.shape[0]
    N = T // CHUNK
    cb = min(cb, N)

    def body(k_ref, bg_ref, gc_ref, gct_ref, l_ref):
        ii, jj, incl = _tri(True)
        tri = incl.astype(F32)
        U = range(cb)
        bgv = [bg_ref[u * CHUNK:(u + 1) * CHUNK, :] for u in U]
        gc = [jnp.dot(tri, bgv[u], precision=lax.Precision.HIGHEST, preferred_element_type=F32) for u in U]
        gct = [gc[u].T for u in U]
        kk = [[None] * N_HEADS for _ in U]
        for u in U:
            gc_ref[u * CHUNK:(u + 1) * CHUNK, :] = gc[u]
            gct_ref[u] = gct[u][0:8]
            for h in range(N_HEADS):
                kh = k_ref[u * CHUNK:(u + 1) * CHUNK, h * HEAD_DIM:(h + 1) * HEAD_DIM]
                kk[u][h] = _nt(kh * bgv[u][:, h:h + 1], kh)
        for u in U:
            for h in range(N_HEADS):
                gcol = gc[u][:, N_HEADS + h:N_HEADS + h + 1]
                grow = gct[u][N_HEADS + h:N_HEADS + h + 1, :]
                l_ref[u, h] = kk[u][h] * jnp.exp(jnp.where(ii > jj, gcol - grow, NEG))

    rows = cb * CHUNK
    return pl.pallas_call(
        body, name="dn_chunk", grid=(N // cb,),
        in_specs=[pl.BlockSpec((rows, DN_WIDTH), lambda n: (n, 0)), pl.BlockSpec((rows, 128), lambda n: (n, 0))],
        out_specs=[pl.BlockSpec((rows, 128), lambda n: (n, 0)), pl.BlockSpec((cb, 8, CHUNK), lambda n: (n, 0, 0)),
                   pl.BlockSpec((cb, N_HEADS, CHUNK, CHUNK), lambda n: (n, 0, 0, 0))],
        out_shape=[jax.ShapeDtypeStruct((T, 128), F32), jax.ShapeDtypeStruct((N, 8, CHUNK), F32),
                   jax.ShapeDtypeStruct((N, N_HEADS, CHUNK, CHUNK), F32)],
        compiler_params=_cp("parallel"))(k, bg)


def _tri_inv(lt):
    S = lt.shape[1]

    def body(l_ref, a_ref):
        sub = lax.broadcasted_iota(jnp.int32, (8, S), 0)
        groups = CHUNK // 8
        for i in range(CHUNK):
            acc = [((sub + 8 * k) == i).astype(F32) for k in range(groups)]
            for jb in range((i + 7) // 8):
                nk = jb + 1

                def step(j, carry, nk=nk, i=i):
                    lrow = l_ref[pl.ds(i * CHUNK + j, 1), :]
                    return tuple(carry[k] - lrow * a_ref[j, 8 * k:8 * k + 8, :] for k in range(nk))

                acc[:nk] = list(lax.fori_loop(8 * jb, min(8 * jb + 8, i), step, tuple(acc[:nk])))
            for k in range(groups):
                a_ref[i, 8 * k:8 * k + 8, :] = acc[k]

    return pl.pallas_call(
        body, name="tri_inv", out_shape=jax.ShapeDtypeStruct((CHUNK, CHUNK, S), F32),
        compiler_params=pltpu.CompilerParams(vmem_limit_bytes=VMEM_LIMIT))(lt)


def _dn_head_terms(qh, kh, vh, beta, gcol, grow):
    ii, jj, incl = _tri(True)
    gam = jnp.exp(jnp.where(incl, gcol - grow, NEG))
    glast = grow[:, CHUNK - 1:CHUNK]
    cd = jnp.exp(glast)
    shape = (CHUNK, HEAD_DIM)
    E = jnp.broadcast_to(jnp.exp(gcol), shape)
    Fd = jnp.broadcast_to(jnp.exp(glast - gcol), shape)
    beta = jnp.broadcast_to(beta, shape)
    kb = kh * beta
    return dict(ii=ii, jj=jj, gam=gam, E=E, F=Fd, beta=beta, cd=cd, kb=kb, vb=vh * beta, W=kb * E, qE=qh * E,
                kt=kh * Fd)


def _apply_a(a, u):
    hi, lo = _split(a)
    ub = _bf(u)
    return jnp.dot(hi, ub, preferred_element_type=F32) + jnp.dot(lo, ub, preferred_element_type=F32)


def _dn_scan(q, k, v, bg, gc, gct, a):
    T = q.shape[0]
    N = T // CHUNK

    cb = min(SCAN_CHUNKS, N)

    def body(q_ref, k_ref, v_ref, bg_ref, gc_ref, gct_ref, a_ref, o_ref, sall_ref, s_ref):
        @pl.when(pl.program_id(0) == 0)
        def _():
            s_ref[...] = jnp.zeros_like(s_ref)
        H = range(N_HEADS)
        sl = [slice(h * HEAD_DIM, (h + 1) * HEAD_DIM) for h in H]
        pre = []
        for u in range(cb):
            r = slice(u * CHUNK, (u + 1) * CHUNK)
            bgv, gcv, gctv = bg_ref[r, :], gc_ref[r, :], gct_ref[u]
            q_, k_ = [q_ref[r, s] for s in sl], [k_ref[r, s] for s in sl]
            t = [_dn_head_terms(q_[h], k_[h], v_ref[r, sl[h]], bgv[:, h:h + 1],
                                gcv[:, N_HEADS + h:N_HEADS + h + 1], gctv[N_HEADS + h:N_HEADS + h + 1, :]) for h in H]
            P = [_nt(q_[h], k_[h]) * t[h]["gam"] for h in H]
            pre.append((r, t, P))
        S = [s_ref[h] for h in H]
        for u in range(cb):
            r, t, P = pre[u]
            for h in H:
                sall_ref[u, h] = S[h]
            WS = [_nn(t[h]["W"], S[h]) for h in H]
            qS = [_nn(t[h]["qE"], S[h]) for h in H]
            vn = [_apply_a(a_ref[u, h], t[h]["vb"] - WS[h]) for h in H]
            Pv = [_nn(P[h], vn[h]) for h in H]
            kv = [_tn(t[h]["kt"], vn[h]) for h in H]
            for h in H:
                o_ref[r, sl[h]] = qS[h] + Pv[h]
            S = [t[h]["cd"] * S[h] + kv[h] for h in H]
        for h in H:
            s_ref[h] = S[h]

    row512 = pl.BlockSpec((cb * CHUNK, DN_WIDTH), lambda n: (n, 0))
    row128 = pl.BlockSpec((cb * CHUNK, 128), lambda n: (n, 0))
    return pl.pallas_call(
        body, name="dn_scan", grid=(N // cb,),
        in_specs=[row512, row512, row512, row128, row128, pl.BlockSpec((cb, 8, CHUNK), lambda n: (n, 0, 0)),
                  pl.BlockSpec((cb, N_HEADS, CHUNK, CHUNK), lambda n: (n, 0, 0, 0))],
        out_specs=[row512, pl.BlockSpec((cb, N_HEADS, HEAD_DIM, HEAD_DIM), lambda n: (n, 0, 0, 0))],
        out_shape=[jax.ShapeDtypeStruct((T, DN_WIDTH), F32),
                   jax.ShapeDtypeStruct((N, N_HEADS, HEAD_DIM, HEAD_DIM), F32)],
        scratch_shapes=[pltpu.VMEM((N_HEADS, HEAD_DIM, HEAD_DIM), F32)],
        compiler_params=_cp("arbitrary"))(q, k, v, bg, gc, gct, a)


def _dn_scan_bwd(q, k, v, bg, gc, gct, a, a_t, sall, do, dep=None):
    T = q.shape[0]
    N = T // CHUNK

    cb = min(SCAN_CHUNKS, N)
    nb = N // cb

    def body(q_ref, k_ref, v_ref, bg_ref, gc_ref, gct_ref, a_ref, at_ref, sall_ref, do_ref, *rest):
        dq_ref, dk_ref, dv_ref, dbg_ref, ds_ref = rest[-5:]
        @pl.when(pl.program_id(0) == 0)
        def _():
            ds_ref[...] = jnp.zeros_like(ds_ref)
        lane = _lane_iota((CHUNK, 128))
        rowi = lax.broadcasted_iota(jnp.int32, (CHUNK, 1), 0)
        ii, jj, _ = _tri(True)
        rev = (jj >= ii).astype(F32)
        H = range(N_HEADS)
        sl = [slice(h * HEAD_DIM, (h + 1) * HEAD_DIM) for h in H]
        pre = {}
        for u in reversed(range(cb)):
            r = slice(u * CHUNK, (u + 1) * CHUNK)
            bgv, gcv, gctv = bg_ref[r, :], gc_ref[r, :], gct_ref[u]
            q_, k_, v_ = [q_ref[r, s] for s in sl], [k_ref[r, s] for s in sl], [v_ref[r, s] for s in sl]
            dO = [do_ref[r, s] for s in sl]
            t = [_dn_head_terms(q_[h], k_[h], v_[h], bgv[:, h:h + 1], gcv[:, N_HEADS + h:N_HEADS + h + 1],
                                gctv[N_HEADS + h:N_HEADS + h + 1, :]) for h in H]
            beta = [t[h]["beta"] for h in H]
            S = [sall_ref[u, h] for h in H]
            A = [a_ref[u, h] for h in H]
            WS = [_nn(t[h]["W"], S[h]) for h in H]
            KK = [_nt(t[h]["kb"], k_[h]) for h in H]
            QK = [_nt(q_[h], k_[h]) for h in H]
            d_qE = [_nt(dO[h], S[h]) for h in H]
            vn = [_apply_a(A[h], t[h]["vb"] - WS[h]) for h in H]
            PtdO = [_tn(QK[h] * t[h]["gam"], dO[h]) for h in H]
            qEdO = [_tn(t[h]["qE"], dO[h]) for h in H]
            dOvn = [_nt(dO[h], vn[h]) for h in H]
            dQK = [jnp.where(ii >= jj, dOvn[h], 0.0) * t[h]["gam"] for h in H]
            dQKk = [_nn(dQK[h], k_[h]) for h in H]
            dQKq = [_tn(dQK[h], q_[h]) for h in H]
            pre[u] = (r, q_, k_, v_, beta, t, S, A, KK, QK, d_qE, vn, PtdO, qEdO, dQK, dQKk, dQKq)
        dSn = [ds_ref[h] for h in H]
        for u in reversed(range(cb)):
            r, q_, k_, v_, beta, t, S, A, KK, QK, d_qE, vn, PtdO, qEdO, dQK, dQKk, dQKq = pre[u]
            gam, E, Fd, cd, kb = ([t[h][n] for h in H] for n in ("gam", "E", "F", "cd", "kb"))
            ktdS = [_nn(t[h]["kt"], dSn[h]) for h in H]
            dU = [_apply_a(at_ref[u, h], PtdO[h] + ktdS[h]) for h in H]
            d_kt = [_nt(vn[h], dSn[h]) for h in H]
            dUvn = [_nt(dU[h], vn[h]) for h in H]
            dUS = [_nt(dU[h], S[h]) for h in H]
            WdU = [_tn(t[h]["W"], dU[h]) for h in H]
            d_cd = [jnp.sum(S[h] * dSn[h]) for h in H]
            dSn = [cd[h] * dSn[h] + qEdO[h] - WdU[h] for h in H]
            dKK = [jnp.where(ii > jj, -dUvn[h], 0.0) * gam[h] for h in H]
            dKKk = [_nn(dKK[h], k_[h]) for h in H]
            dKKkb = [_tn(dKK[h], kb[h]) for h in H]
            dbeta_arr = jnp.zeros((CHUNK, 128), F32)
            dgc_arr = jnp.zeros((CHUNK, 128), F32)
            for h in H:
                dW = -dUS[h]
                dq_ref[r, sl[h]] = dQKk[h] + d_qE[h] * E[h]
                d_kb = dKKk[h] + dW * E[h]
                dk_ref[r, sl[h]] = dQKq[h] + dKKkb[h] + d_kb * beta[h] + d_kt[h] * Fd[h]
                dv_ref[r, sl[h]] = dU[h] * beta[h]
                Z = dQK[h] * QK[h] + dKK[h] * KK[h]
                dbeta = jnp.sum(dU[h] * v_[h] + d_kb * k_[h], axis=-1, keepdims=True)
                m_e = (dW * kb[h] + d_qE[h] * q_[h]) * E[h]
                m_f = d_kt[h] * k_[h] * Fd[h]
                zdiag = jnp.where(ii == jj, jnp.sum(Z, axis=0, keepdims=True), 0.0)
                dgc = (jnp.sum(m_e - m_f, axis=-1, keepdims=True) + jnp.sum(Z - zdiag, axis=-1, keepdims=True)
                       + jnp.where(rowi == CHUNK - 1, jnp.sum(m_f) + d_cd[h] * cd[h], 0.0))
                dbeta_arr = dbeta_arr + jnp.where(lane == h, dbeta, 0.0)
                dgc_arr = dgc_arr + jnp.where(lane == N_HEADS + h, dgc, 0.0)
            dbg_ref[r, :] = dbeta_arr + jnp.dot(rev, dgc_arr, precision=lax.Precision.HIGHEST,
                                                preferred_element_type=F32)
        for h in H:
            ds_ref[h] = dSn[h]

    row512 = pl.BlockSpec((cb * CHUNK, DN_WIDTH), lambda n: (nb - 1 - n, 0))
    row128 = pl.BlockSpec((cb * CHUNK, 128), lambda n: (nb - 1 - n, 0))
    in_specs, args = _with_dep(
        [row512, row512, row512, row128, row128,
         pl.BlockSpec((cb, 8, CHUNK), lambda n: (nb - 1 - n, 0, 0)),
         pl.BlockSpec((cb, N_HEADS, CHUNK, CHUNK), lambda n: (nb - 1 - n, 0, 0, 0)),
         pl.BlockSpec((cb, N_HEADS, CHUNK, CHUNK), lambda n: (nb - 1 - n, 0, 0, 0)),
         pl.BlockSpec((cb, N_HEADS, HEAD_DIM, HEAD_DIM), lambda n: (nb - 1 - n, 0, 0, 0)), row512],
        [q, k, v, bg, gc, gct, a, a_t, sall, do], dep)
    return pl.pallas_call(
        body, name="dn_scan_bwd", grid=(nb,), in_specs=in_specs,
        out_specs=[row512, row512, row512, row128],
        out_shape=[jax.ShapeDtypeStruct((T, DN_WIDTH), F32)] * 3 + [jax.ShapeDtypeStruct((T, 128), F32)],
        scratch_shapes=[pltpu.VMEM((N_HEADS, HEAD_DIM, HEAD_DIM), F32)],
        compiler_params=_cp("arbitrary"))(*args)


def _sg_mask():
    ii = lax.broadcasted_iota(jnp.int32, (SG_BLOCK, SG_BLOCK), 0) // CHUNK
    jj = lax.broadcasted_iota(jnp.int32, (SG_BLOCK, SG_BLOCK), 1) // CHUNK
    return jj <= ii


def _mix_fwd(o, p, ong, sgn, sgw, sgbt):
    T = o.shape[0]
    rb = SG_BLOCK

    def body(o_ref, gate_ref, u_ref, vg_ref, ong_ref, sgn_ref, sgw_ref, sgbt_ref, mix_ref):
        mask = _sg_mask()
        gate = gate_ref[...]
        for h in range(N_HEADS):
            sl = slice(h * HEAD_DIM, (h + 1) * HEAD_DIM)
            oh = o_ref[:, sl]
            r = lax.rsqrt(jnp.mean(oh * oh, axis=-1, keepdims=True) + EPS)
            mix_ref[:, sl] = (oh * r * ong_ref[...] * _silu(gate[:, sl])).astype(BF16)
        for gi in range(SG_GROUPS):
            sl = slice(gi * SG_BLOCK, (gi + 1) * SG_BLOCK)
            gv = _gelu(vg_ref[:, sl])
            r = lax.rsqrt(jnp.mean(gv * gv, axis=-1, keepdims=True) + EPS)
            vh = gv * r * sgn_ref[:, sl]
            s = _nn(jnp.where(mask, sgw_ref[gi], 0.0), vh) + sgbt_ref[:, gi:gi + 1]
            mix_ref[:, DN_WIDTH + gi * SG_BLOCK:DN_WIDTH + (gi + 1) * SG_BLOCK] = (_gelu(u_ref[:, sl]) * s).astype(BF16)

    def col(c):
        return pl.BlockSpec((rb, 512), lambda i: (i, c))
    return pl.pallas_call(
        body, name="mix_fwd", grid=(T // rb,),
        in_specs=[pl.BlockSpec((rb, DN_WIDTH), lambda i: (i, 0)), col(3), col(4), col(5),
                  pl.BlockSpec((1, 128), lambda i: (0, 0)), pl.BlockSpec((1, SG_WIDTH), lambda i: (0, 0)),
                  pl.BlockSpec((SG_GROUPS, SG_BLOCK, SG_BLOCK), lambda i: (0, 0, 0)),
                  pl.BlockSpec((SG_BLOCK, 128), lambda i: (0, 0))],
        out_specs=pl.BlockSpec((rb, D_MODEL), lambda i: (i, 0)),
        out_shape=jax.ShapeDtypeStruct((T, D_MODEL), BF16),
        compiler_params=_cp("parallel"))(o, p, p, p, ong, sgn, sgw, sgbt)


def _mix_bwd(o, p, ong, sgn, sgw, sgbt, dmix, dep=None):
    T = o.shape[0]
    rb = SG_BLOCK

    def body(o_ref, gate_ref, u_ref, vg_ref, ong_ref, sgn_ref, sgw_ref, sgbt_ref, dmix_ref, *rest):
        do_ref, dp_ref, gong_ref, gsgn_ref, gsgw_ref, gsgbt_ref = rest[-6:]
        @pl.when(pl.program_id(0) == 0)
        def _():
            gong_ref[...] = jnp.zeros_like(gong_ref)
            gsgn_ref[...] = jnp.zeros_like(gsgn_ref)
            gsgw_ref[...] = jnp.zeros_like(gsgw_ref)
            gsgbt_ref[...] = jnp.zeros_like(gsgbt_ref)
        mask = _sg_mask()
        gate = gate_ref[...]
        lane = _lane_iota((SG_BLOCK, 128))
        for h in range(N_HEADS):
            sl = slice(h * HEAD_DIM, (h + 1) * HEAD_DIM)
            oh = o_ref[:, sl]
            dm = dmix_ref[:, sl]
            r = lax.rsqrt(jnp.mean(oh * oh, axis=-1, keepdims=True) + EPS)
            oh_hat = oh * r
            gt = gate[:, sl]
            sg = _silu(gt)
            dp_ref[:, sl] = (dm * oh_hat * ong_ref[...] * _dsilu(gt)).astype(BF16)
            dn_ = dm * sg
            gong_ref[...] += jnp.sum(dn_ * oh_hat, axis=0, keepdims=True)
            dhat = dn_ * ong_ref[...]
            do_ref[:, sl] = r * (dhat - oh_hat * jnp.mean(dhat * oh_hat, axis=-1, keepdims=True))
        for gi in range(SG_GROUPS):
            sl = slice(gi * SG_BLOCK, (gi + 1) * SG_BLOCK)
            vraw = vg_ref[:, sl]
            gv = _gelu(vraw)
            r = lax.rsqrt(jnp.mean(gv * gv, axis=-1, keepdims=True) + EPS)
            vhat = gv * r
            vn = vhat * sgn_ref[:, sl]
            wm = jnp.where(mask, sgw_ref[gi], 0.0)
            s = _nn(wm, vn) + sgbt_ref[:, gi:gi + 1]
            uraw = u_ref[:, sl]
            dm = dmix_ref[:, DN_WIDTH + gi * SG_BLOCK:DN_WIDTH + (gi + 1) * SG_BLOCK]
            dp_ref[:, DN_WIDTH + gi * SG_BLOCK:DN_WIDTH + (gi + 1) * SG_BLOCK] = (dm * s * _dgelu(uraw)).astype(BF16)
            ds = dm * _gelu(uraw)
            gsgbt_ref[...] += jnp.where(lane == gi, jnp.sum(ds, axis=-1, keepdims=True), 0.0)
            gsgw_ref[gi] += jnp.where(mask, _nt(ds, vn), 0.0)
            dvn = _tn(wm, ds)
            gsgn_ref[:, sl] += jnp.sum(dvn * vhat, axis=0, keepdims=True)
            dhat = dvn * sgn_ref[:, sl]
            dgv = r * (dhat - vhat * jnp.mean(dhat * vhat, axis=-1, keepdims=True))
            dp_ref[:, 2 * DN_WIDTH + gi * SG_BLOCK:2 * DN_WIDTH + (gi + 1) * SG_BLOCK] = (dgv * _dgelu(vraw)).astype(BF16)

    def col(c):
        return pl.BlockSpec((rb, 512), lambda i: (i, c))
    full = lambda *s: pl.BlockSpec(s, lambda i: (0,) * len(s))
    in_specs, args = _with_dep(
        [pl.BlockSpec((rb, DN_WIDTH), lambda i: (i, 0)), col(3), col(4), col(5),
         full(1, 128), full(1, SG_WIDTH), full(SG_GROUPS, SG_BLOCK, SG_BLOCK), full(SG_BLOCK, 128),
         pl.BlockSpec((rb, D_MODEL), lambda i: (i, 0))],
        [o, p, p, p, ong, sgn, sgw, sgbt, dmix], dep)
    return pl.pallas_call(
        body, name="mix_bwd", grid=(T // rb,), in_specs=in_specs,
        out_specs=[pl.BlockSpec((rb, DN_WIDTH), lambda i: (i, 0)), pl.BlockSpec((rb, 3 * 512), lambda i: (i, 0)),
                   full(1, 128), full(1, SG_WIDTH), full(SG_GROUPS, SG_BLOCK, SG_BLOCK), full(SG_BLOCK, 128)],
        out_shape=[jax.ShapeDtypeStruct((T, DN_WIDTH), F32), jax.ShapeDtypeStruct((T, 3 * 512), BF16),
                   jax.ShapeDtypeStruct((1, 128), F32), jax.ShapeDtypeStruct((1, SG_WIDTH), F32),
                   jax.ShapeDtypeStruct((SG_GROUPS, SG_BLOCK, SG_BLOCK), F32),
                   jax.ShapeDtypeStruct((SG_BLOCK, 128), F32)],
        compiler_params=_cp("arbitrary"))(*args)


def _pad_lanes(row, offset=0):
    n = row.shape[1]
    return jnp.pad(row, ((0, 0), (offset, 128 - n - offset)))


def _local_step(x, tgt, w, dep=None, late_weights=None, on_grad=None):
    T = x.shape[0]
    N = T // CHUNK
    on_grad = on_grad or (lambda name, g: None)
    alog_row = _pad_lanes(w["dn_a_log"], N_HEADS)
    dtb_row = _pad_lanes(w["dn_dt_bias"], N_HEADS)
    sgbt = jnp.pad(w["sg_b"].T, ((0, 0), (0, 128 - SG_GROUPS)))

    p, h1, w_in_pad = _in_proj(x, w["attn_norm_g"], w["w_in"], dep=dep)
    q, k, v, bg = _dn_act(p, w["dn_conv_w"], alog_row, dtb_row)
    gc, gct, lmat = _dn_chunk(k, bg)
    lt = lmat.reshape(N * N_HEADS, CHUNK * CHUNK).T
    at = _tri_inv(lt)
    a = at.reshape(CHUNK * CHUNK, N * N_HEADS).T.reshape(N, N_HEADS, CHUNK, CHUNK)
    a_t = at.transpose(1, 0, 2).reshape(CHUNK * CHUNK, N * N_HEADS).T.reshape(N, N_HEADS, CHUNK, CHUNK)
    o, sall = _dn_scan(q, k, v, bg, gc, gct, a)
    mix = _mix_fwd(o, p, w["dn_out_norm_g"], w["sg_norm_g"], w["sg_w"], sgbt)
    if late_weights is not None:
        w = {**w, **late_weights(mix)}
    x2, h2 = _out_proj(mix, w["w_out"], x, w["ffn_norm_g"])
    up = _mm_nn("up_proj", h2, w["w_up"], F32, 512, D_FF)
    act = _ffn_act(up, w["ffn_conv_w"], w["ffn_conv_b"])
    loss, dx3, g_final = _down_proj_loss(act, w["w_down"], x2, tgt, w["final_norm_g"])

    dact = _mm_nt("d_act", dx3, w["w_down"], F32, 512, D_FF)
    g_w_down = _mm_tn("g_w_down", act, dx3, D_FF, 1024, 1024)
    tok = on_grad("w_down", g_w_down)
    dup, g_ffn_conv_w, g_ffn_conv_b = _ffn_act_bwd(up, dact, w["ffn_conv_w"], w["ffn_conv_b"], dep=tok)
    g_w_up = _mm_tn("g_w_up", h2, dup, 1024, 2 * D_FF // 4, 2048, col_major_tiles=True)
    tok = on_grad("w_up", g_w_up)
    dx2, g_ffn_norm = _mm_nt_rms_bwd("d_h2", dup, w["w_up"], x2, w["ffn_norm_g"], dx3, dep=tok)
    dmix = _mm_nt("d_mix", dx2, w["w_out"], F32, 512, 1024)
    g_w_out = _mm_tn("g_w_out", mix, dx2, 1024, 1024, 1024)
    tok = on_grad("w_out", g_w_out)
    do, dp_mid, g_ong, g_sgn, g_sgw, g_sgbt = _mix_bwd(o, p, w["dn_out_norm_g"], w["sg_norm_g"], w["sg_w"], sgbt,
                                                      dmix, dep=tok)
    early = dict(dn_out_norm_g=g_ong, sg_norm_g=g_sgn, sg_w=g_sgw, sg_b=g_sgbt[:, :SG_GROUPS].T,
                 ffn_norm_g=g_ffn_norm, ffn_conv_w=g_ffn_conv_w, ffn_conv_b=g_ffn_conv_b, final_norm_g=g_final)
    tok = on_grad("small_early", early)
    dq, dk, dv, dbg = _dn_scan_bwd(q, k, v, bg, gc, gct, a, a_t, sall, do, dep=tok)
    dp, g_dn_conv_w, g_ad = _dn_act_bwd(p, w["dn_conv_w"], alog_row, dtb_row, dq, dk, dv, dbg, dp_mid)
    g_w_in = _mm_tn("g_w_in", h1, dp, 1024, PROJ_PAD, 1024, col_groups=(4, PROJ_COLS // 4))
    tok = on_grad("w_in", g_w_in)
    grad_x, g_attn_norm = _mm_nt_rms_bwd("d_h1", dp, w_in_pad, x, w["attn_norm_g"], dx2, dep=tok)

    grads = dict(
        attn_norm_g=g_attn_norm, w_in=g_w_in, dn_conv_w=g_dn_conv_w,
        dn_a_log=g_ad[0:1, N_HEADS:2 * N_HEADS], dn_dt_bias=g_ad[1:2, N_HEADS:2 * N_HEADS],
        w_out=g_w_out, w_up=g_w_up, w_down=g_w_down, **early)
    return loss, grad_x, grads


def _me():
    return lax.axis_index("x"), lax.axis_index("y"), lax.axis_index("c")


def _peer(rel):
    x, y, c = _me()
    return {"x": (1 - x, y, c), "y": (x, 1 - y, c), "xy": (1 - x, 1 - y, c), "c": (x, y, 1 - c)}[rel]


def _chip_of(dev):
    return 2 * dev[0] + dev[1]


CHIP_RELS = ("x", "y", "xy")


def _run_copies(copies, sends, recvs):
    for cp in copies:
        cp.start()
    for cp in recvs:
        cp.wait_recv()
    for cp in sends:
        cp.wait_send()


def _gather_first(w_shard, small_shard):
    R = w_shard.shape[0]
    r2 = R // 2

    def body(w_ref, s_ref, w_out, s_out, send_sems, recv_sems):
        x, y, c = _me()
        me = _chip_of((x, y))
        sib = _peer("c")

        def half(chip, core):
            return w_out.at[chip, pl.ds(pl.multiple_of(core * r2, 8), r2), :]

        def copy(k, src, dst, to):
            return pltpu.make_async_remote_copy(src_ref=src, dst_ref=dst, send_sem=send_sems.at[k],
                                                recv_sem=recv_sems.at[k], device_id=to, device_id_type=MESH)

        own_rows = w_ref.at[pl.ds(pl.multiple_of(c * r2, 8), r2), :]
        first = [copy(r, own_rows, half(me, c), _peer(rel)) for r, rel in enumerate(CHIP_RELS)]
        first += [copy(3 + r, s_ref, s_out.at[me], _peer(rel)) for r, rel in enumerate(CHIP_RELS)]
        for cp in first:
            cp.start()
        passed = []
        for r, rel in enumerate(CHIP_RELS):
            their = _chip_of(_peer(rel))
            copy(r, own_rows, half(their, c), _peer(rel)).wait_recv()
            fwd = copy(6 + r, half(their, c), half(their, c), sib)
            fwd.start()
            passed.append(fwd)
        for r, rel in enumerate(CHIP_RELS):
            their = _chip_of(_peer(rel))
            copy(3 + r, s_ref, s_out.at[their], _peer(rel)).wait_recv()
            copy(6 + r, own_rows, half(their, 1 - c), sib).wait_recv()
        for cp in first + passed:
            cp.wait_send()

    w_all, s_all = pl.pallas_call(
        body, name="gather_first", in_specs=[ANY, ANY], out_specs=[ANY, ANY],
        out_shape=[jax.ShapeDtypeStruct((4,) + w_shard.shape, w_shard.dtype),
                   jax.ShapeDtypeStruct((4,) + small_shard.shape, small_shard.dtype)],
        scratch_shapes=[pltpu.SemaphoreType.DMA((9,)), pltpu.SemaphoreType.DMA((9,))])(w_shard, small_shard)
    me = _chip_of(_me())
    return (lax.dynamic_update_index_in_dim(w_all, w_shard, me, 0),
            lax.dynamic_update_index_in_dim(s_all, small_shard, me, 0))


OTHERS = tuple((fx, fy, fc) for fx in (0, 1) for fy in (0, 1) for fc in (0, 1) if (fx, fy, fc) != (0, 0, 0))


def _other(flip):
    x, y, c = _me()
    return (x ^ flip[0], y ^ flip[1], c ^ flip[2])


def _linear(dev):
    return 4 * dev[0] + 2 * dev[1] + dev[2]


def _exchange_small(small):
    def body(small_ref, out_ref, send_sems, recv_sems):
        my_slot = _linear(_me())
        sends, recvs = [], []
        for k, flip in enumerate(OTHERS):
            peer = _other(flip)
            sends.append(pltpu.make_async_remote_copy(
                src_ref=small_ref, dst_ref=out_ref.at[my_slot], send_sem=send_sems.at[k], recv_sem=recv_sems.at[k],
                device_id=peer, device_id_type=MESH))
            recvs.append(pltpu.make_async_remote_copy(
                src_ref=small_ref, dst_ref=out_ref.at[_linear(peer)], send_sem=send_sems.at[k],
                recv_sem=recv_sems.at[k], device_id=peer, device_id_type=MESH))
        _run_copies(sends, sends, recvs)

    out = pl.pallas_call(
        body, name="exchange_small", in_specs=[ANY], out_specs=ANY,
        out_shape=jax.ShapeDtypeStruct((8,) + small.shape, small.dtype),
        scratch_shapes=[pltpu.SemaphoreType.DMA((7,)), pltpu.SemaphoreType.DMA((7,))])(small)
    return lax.dynamic_update_index_in_dim(out, small, _linear(_me()), 0)


def _pair_swap(halves):
    n = len(halves)

    def body(*refs):
        src, out = refs[:n], refs[n:2 * n]
        send_sems, recv_sems = refs[2 * n:]
        sib = _peer("c")
        copies = [pltpu.make_async_remote_copy(
            src_ref=src[i], dst_ref=out[i], send_sem=send_sems.at[i], recv_sem=recv_sems.at[i],
            device_id=sib, device_id_type=MESH) for i in range(n)]
        _run_copies(copies, copies, copies)

    return pl.pallas_call(
        body, name="pair_swap", in_specs=[ANY] * n, out_specs=[ANY] * n,
        out_shape=[jax.ShapeDtypeStruct(h.shape, h.dtype) for h in halves],
        scratch_shapes=[pltpu.SemaphoreType.DMA((n,)), pltpu.SemaphoreType.DMA((n,))])(*halves)


HBM = pl.BlockSpec(memory_space=pltpu.HBM)
SEM = pl.BlockSpec(memory_space=pltpu.SEMAPHORE)
EFFECT = pltpu.SideEffectType.DATAFLOW_SIDE_EFFECTING


def _hbm(a):
    return pltpu.with_memory_space_constraint(a, pltpu.HBM)


def _transfer_start(name, srcs, lands, n_copies, make_copies, after=None):
    n, m = len(srcs), len(lands)

    def body(*refs):
        src, land = refs[:n], refs[n:n + m]
        outs = refs[n + m + (after is not None):]
        send_sems, recv_sems, token = outs[0], outs[1], outs[-1]
        for cp in make_copies(src, land, send_sems, recv_sems):
            cp.start()
        token[...] = jnp.zeros_like(token)

    arrs = list(srcs) + list(lands)
    in_specs, args = _with_dep([HBM] * (n + m), [_hbm(a) for a in arrs], after)
    out = pl.pallas_call(
        body, name=name,
        out_shape=(pltpu.SemaphoreType.DMA((n_copies,)), pltpu.SemaphoreType.DMA((n_copies,)),
                   *[pltpu.HBM(a.shape, a.dtype) for a in arrs], jax.ShapeDtypeStruct((8, 128), F32)),
        in_specs=in_specs,
        out_specs=(SEM, SEM, *[HBM] * (n + m), pl.BlockSpec(memory_space=pltpu.VMEM)),
        input_output_aliases={i: 2 + i for i in range(n + m)},
        compiler_params=pltpu.CompilerParams(has_side_effects=EFFECT))(*args)
    return out[0], out[1], list(out[2:2 + n]), list(out[2 + n:2 + n + m]), out[-1]


def _transfer_wait(name, send_sems, recv_sems, srcs, lands, make_copies, after):
    n, m = len(srcs), len(lands)

    def body(*refs):
        src, land = refs[:n], refs[n:n + m]
        s_sems, r_sems = refs[n + m], refs[n + m + 1]
        for cp in make_copies(src, land, s_sems, r_sems):
            cp.wait_send()
            cp.wait_recv()

    arrs = list(srcs) + list(lands)
    out = pl.pallas_call(
        body, name=name, out_shape=tuple(pltpu.HBM(a.shape, a.dtype) for a in arrs),
        in_specs=[HBM] * (n + m) + [SEM, SEM, ANY], out_specs=tuple([HBM] * (n + m)),
        input_output_aliases={i: i for i in range(n + m)},
        compiler_params=pltpu.CompilerParams(has_side_effects=EFFECT))(*arrs, send_sems, recv_sems, after)
    return list(out[:n]), list(out[n:])


def _gather_copies(src, land, send_sems, recv_sems):
    me = _chip_of(_me())
    copies = []
    for i in range(len(src)):
        for r, rel in enumerate(CHIP_RELS):
            k = 3 * i + r
            copies.append(pltpu.make_async_remote_copy(
                src_ref=src[i], dst_ref=land[i].at[me], send_sem=send_sems.at[k], recv_sem=recv_sems.at[k],
                device_id=_peer(rel), device_id_type=MESH))
    return copies


def _small_copies(src, land, send_sems, recv_sems):
    my_slot = _linear(_me())
    return [pltpu.make_async_remote_copy(
        src_ref=src[0], dst_ref=land[0].at[my_slot], send_sem=send_sems.at[k], recv_sem=recv_sems.at[k],
        device_id=_other(flip), device_id_type=MESH) for k, flip in enumerate(OTHERS)]


def _pieces_copies(src, land, send_sems, recv_sems):
    copies = []
    for k, flip in enumerate(OTHERS):
        peer = _other(flip)
        copies.append(pltpu.make_async_remote_copy(
            src_ref=src[0].at[_linear(peer)], dst_ref=land[0].at[k], send_sem=send_sems.at[k],
            recv_sem=recv_sems.at[k], device_id=peer, device_id_type=MESH))
    return copies


def _row_block(rows, cols, budget=2 * 1024 * 1024):
    rb = max(8, (budget // (4 * cols)) // 8 * 8)
    while rows % rb:
        rb -= 8
    return rb if rb > 0 else rows


def _sum_slots(name, first, rest):
    R, Cc = first.shape
    K = rest.shape[0]
    rb = _row_block(R, Cc)

    def body(f_ref, r_ref, o_ref):
        acc = f_ref[...].astype(F32)
        for j in range(K):
            acc = acc + r_ref[j].astype(F32)
        o_ref[...] = acc

    return pl.pallas_call(
        body, name=name, grid=(R // rb,),
        in_specs=[pl.BlockSpec((rb, Cc), lambda i: (i, 0)), pl.BlockSpec((K, rb, Cc), lambda i: (0, i, 0))],
        out_specs=pl.BlockSpec((rb, Cc), lambda i: (i, 0)),
        out_shape=jax.ShapeDtypeStruct((R, Cc), F32), compiler_params=_cp("parallel"))(first, rest)


def _adamw_math(w, gv, m, v):
    mn = ADAM_B1 * m + (1.0 - ADAM_B1) * gv
    vn = ADAM_B2 * v + (1.0 - ADAM_B2) * (gv * gv)
    m_hat = mn / (1.0 - ADAM_B1 ** ADAM_STEP)
    v_hat = vn / (1.0 - ADAM_B2 ** ADAM_STEP)
    return -ADAM_LR * (m_hat / (jnp.sqrt(v_hat) + ADAM_EPS) + ADAM_WD * w), mn, vn


def _adamw_halves(name, w, mine, theirs, m, v, core):
    R, Cc = w.shape
    r2 = R // 2
    rb = _row_block(r2, Cc, 1024 * 1024)
    nb2 = r2 // rb

    def body(c_ref, w_ref, mine_ref, theirs_ref, m_ref, v_ref, g_ref, d_ref, mo_ref, vo_ref):
        is_mine = (pl.program_id(0) // nb2) == c_ref[0]
        gv = jnp.where(is_mine, mine_ref[...], theirs_ref[...])
        g_ref[...] = gv
        d_ref[...], mo_ref[...], vo_ref[...] = _adamw_math(w_ref[...], gv, m_ref[...], v_ref[...])

    blk = pl.BlockSpec((rb, Cc), lambda i, c: (i, 0))
    half = lambda own: pl.BlockSpec(
        (rb, Cc), lambda i, c: (jnp.clip(i - (c[0] if own else 1 - c[0]) * nb2, 0, nb2 - 1), 0))
    return pl.pallas_call(
        body, name=name,
        grid_spec=pltpu.PrefetchScalarGridSpec(
            num_scalar_prefetch=1, grid=(2 * nb2,), in_specs=[blk, half(True), half(False), blk, blk],
            out_specs=[blk] * 4),
        out_shape=[jax.ShapeDtypeStruct((R, Cc), F32)] * 4, compiler_params=_cp("parallel"))(core, w, mine, theirs, m, v)


def _adamw_transposed(name, wt, mine, theirs, mt, vt, core):
    Cc, kh_n, _ = wt.shape
    r2 = mine.shape[0]
    per_half = kh_n // 2
    nb = -(-Cc // LANES)

    def body(c_ref, w_ref, mine_ref, theirs_ref, m_ref, v_ref, g_ref, d_ref, mo_ref, vo_ref):
        first = c_ref[0] == 0
        halves = (jnp.where(first, mine_ref[...], theirs_ref[...]).T,
                  jnp.where(first, theirs_ref[...], mine_ref[...]).T)
        for kh in range(kh_n):
            lo = (kh % per_half) * LANES
            gv = halves[kh // per_half][:, lo:lo + LANES]
            g_ref[:, kh, :] = gv
            d_ref[:, kh, :], mo_ref[:, kh, :], vo_ref[:, kh, :] = _adamw_math(
                w_ref[:, kh, :], gv, m_ref[:, kh, :], v_ref[:, kh, :])

    blk = pl.BlockSpec((LANES, kh_n, LANES), lambda i, c: (i, 0, 0))
    half = pl.BlockSpec((r2, LANES), lambda i, c: (0, i))
    return pl.pallas_call(
        body, name=name,
        grid_spec=pltpu.PrefetchScalarGridSpec(
            num_scalar_prefetch=1, grid=(nb,), in_specs=[blk, half, half, blk, blk], out_specs=[blk] * 4),
        out_shape=[jax.ShapeDtypeStruct(wt.shape, F32)] * 4, compiler_params=_cp("parallel"))(
            core, wt, mine, theirs, mt, vt)


def _adamw(name, w, g, m, v):
    R, Cc = w.shape
    rb = _row_block(R, Cc, 1024 * 1024)

    def body(w_ref, g_ref, m_ref, v_ref, d_ref, mo_ref, vo_ref):
        d_ref[...], mo_ref[...], vo_ref[...] = _adamw_math(w_ref[...], g_ref[...], m_ref[...], v_ref[...])

    blk = pl.BlockSpec((rb, Cc), lambda i: (i, 0))
    return pl.pallas_call(
        body, name=name, grid=(R // rb,), in_specs=[blk] * 4, out_specs=[blk] * 3,
        out_shape=[jax.ShapeDtypeStruct((R, Cc), F32)] * 3, compiler_params=_cp("parallel"))(w, g, m, v)


def _pack(arrs):
    rows = []
    for a in arrs:
        flat = a.reshape(-1)
        pad = (-flat.shape[0]) % 128
        rows.append(jnp.pad(flat, (0, pad)).reshape(-1, 128))
    buf = jnp.concatenate(rows, axis=0)
    return jnp.pad(buf, ((0, (-buf.shape[0]) % 8), (0, 0)))


def _unpack(buf, shapes):
    out, r = [], 0
    for s in shapes:
        n = math.prod(s)
        nr = -(-n // 128)
        out.append(buf[r:r + nr].reshape(-1)[:n].reshape(s))
        r += nr
    return out


BIG = ("w_in", "w_out", "w_up", "w_down")
CONV = ("dn_conv_w", "ffn_conv_w")
REPL = ("attn_norm_g", "dn_a_log", "dn_dt_bias", "dn_out_norm_g", "sg_norm_g", "sg_w", "sg_b",
        "ffn_norm_g", "ffn_conv_b", "final_norm_g")
ORDER = ("attn_norm_g", "w_in", "dn_conv_w", "dn_a_log", "dn_dt_bias", "dn_out_norm_g", "sg_norm_g", "sg_w",
         "sg_b", "w_out", "ffn_norm_g", "w_up", "ffn_conv_w", "ffn_conv_b", "w_down", "final_norm_g")


def kernel(x, attn_norm_g, w_in, dn_conv_w, dn_a_log, dn_dt_bias, dn_out_norm_g, sg_norm_g, sg_w, sg_b, w_out, ffn_norm_g, w_up, ffn_conv_w, ffn_conv_b, w_down, final_norm_g, loss_target, m_attn_norm_g, m_w_in, m_dn_conv_w, m_dn_a_log, m_dn_dt_bias, m_dn_out_norm_g, m_sg_norm_g, m_sg_w, m_sg_b, m_w_out, m_ffn_norm_g, m_w_up, m_ffn_conv_w, m_ffn_conv_b, m_w_down, m_final_norm_g, v_attn_norm_g, v_w_in, v_dn_conv_w, v_dn_a_log, v_dn_dt_bias, v_dn_out_norm_g, v_sg_norm_g, v_sg_w, v_sg_b, v_w_out, v_ffn_norm_g, v_w_up, v_ffn_conv_w, v_ffn_conv_b, v_w_down, v_final_norm_g):
    W = dict(attn_norm_g=attn_norm_g, w_in=w_in, dn_conv_w=dn_conv_w, dn_a_log=dn_a_log, dn_dt_bias=dn_dt_bias,
             dn_out_norm_g=dn_out_norm_g, sg_norm_g=sg_norm_g, sg_w=sg_w, sg_b=sg_b, w_out=w_out,
             ffn_norm_g=ffn_norm_g, w_up=w_up, ffn_conv_w=ffn_conv_w, ffn_conv_b=ffn_conv_b, w_down=w_down,
             final_norm_g=final_norm_g)
    Mo = dict(attn_norm_g=m_attn_norm_g, w_in=m_w_in, dn_conv_w=m_dn_conv_w, dn_a_log=m_dn_a_log,
              dn_dt_bias=m_dn_dt_bias, dn_out_norm_g=m_dn_out_norm_g, sg_norm_g=m_sg_norm_g, sg_w=m_sg_w,
              sg_b=m_sg_b, w_out=m_w_out, ffn_norm_g=m_ffn_norm_g, w_up=m_w_up, ffn_conv_w=m_ffn_conv_w,
              ffn_conv_b=m_ffn_conv_b, w_down=m_w_down, final_norm_g=m_final_norm_g)
    Vo = dict(attn_norm_g=v_attn_norm_g, w_in=v_w_in, dn_conv_w=v_dn_conv_w, dn_a_log=v_dn_a_log,
              dn_dt_bias=v_dn_dt_bias, dn_out_norm_g=v_dn_out_norm_g, sg_norm_g=v_sg_norm_g, sg_w=v_sg_w,
              sg_b=v_sg_b, w_out=v_w_out, ffn_norm_g=v_ffn_norm_g, w_up=v_w_up, ffn_conv_w=v_ffn_conv_w,
              ffn_conv_b=v_ffn_conv_b, w_down=v_w_down, final_norm_g=v_final_norm_g)
    xi, yi, ci = lax.axis_index("x"), lax.axis_index("y"), lax.axis_index("c")
    chip = 2 * xi + yi

    me_lin = 4 * xi + 2 * yi + ci

    g_in, g_dnc = _gather_first(w_in[0].astype(BF16), dn_conv_w[0])
    late = ("w_out", "w_up", "w_down", "ffn_conv_w")
    late_shards = [W[n][0].astype(BF16) for n in late[:3]] + [ffn_conv_w[0]]
    late_lands = [lax.dynamic_update_index_in_dim(lax.empty((4,) + s.shape, s.dtype), s, chip, 0) for s in late_shards]
    n_late = 3 * len(late_shards)
    ssem, rsem, late_src, late_lands, token = _transfer_start("gather_rest_start", late_shards, late_lands,
                                                              n_late, _gather_copies, after=g_in)

    def late_weights(after):
        _, (g_out, g_up, g_down, g_ffc) = _transfer_wait("gather_rest_wait", ssem, rsem, late_src, late_lands,
                                                         _gather_copies, after)
        return dict(w_out=g_out.reshape(D_MODEL, D_MODEL), w_up=g_up.transpose(1, 0, 2).reshape(D_MODEL, 2 * D_FF),
                    w_down=g_down.reshape(D_FF, D_MODEL), ffn_conv_w=g_ffc.transpose(1, 0, 2).reshape(3, 2 * D_FF))

    full = dict(
        w_in=g_in,
        dn_conv_w=g_dnc.transpose(1, 0, 2).reshape(4, 3 * DN_WIDTH),
        attn_norm_g=attn_norm_g, dn_a_log=dn_a_log, dn_dt_bias=dn_dt_bias, dn_out_norm_g=dn_out_norm_g,
        sg_norm_g=sg_norm_g, sg_w=sg_w[0], sg_b=sg_b[0], ffn_norm_g=ffn_norm_g, ffn_conv_b=ffn_conv_b,
        final_norm_g=final_norm_g[None])

    pending = {}
    early_names = ("dn_out_norm_g", "sg_norm_g", "sg_w", "sg_b", "ffn_norm_g", "ffn_conv_w", "ffn_conv_b",
                   "final_norm_g")
    late_names = ("attn_norm_g", "dn_a_log", "dn_dt_bias", "dn_conv_w")

    def on_grad(name, gw):
        if name == "small_early":
            buf = _pack([gw[n] for n in early_names])
            land = lax.dynamic_update_index_in_dim(lax.empty((8,) + buf.shape, F32), buf, me_lin, 0)
            s_sem, r_sem, src, lands, tok = _transfer_start("small_early_start", [buf], [land], 7, _small_copies)
            pending[name] = (s_sem, r_sem, src, lands)
            return tok
        g8 = gw.reshape(8, -1, gw.shape[-1])
        land = lax.empty((7,) + g8.shape[1:], BF16)
        s_sem, r_sem, src, lands, tok = _transfer_start(f"reduce_{name}_start", [g8], [land], 7, _pieces_copies)
        pending[name] = (s_sem, r_sem, src, lands)
        return tok

    loss_row, grad_x, g = _local_step(x[0], loss_target[0], full, dep=token, late_weights=late_weights,
                                      on_grad=on_grad)

    small_names = REPL + CONV
    late_all = _exchange_small(_pack([g[n] for n in late_names] + [loss_row]))
    late_sum = _sum_slots("sum_small_late", late_all[0], late_all[1:])
    s_sem, r_sem, src, lands = pending["small_early"]
    _, (early_all,) = _transfer_wait("small_early_wait", s_sem, r_sem, src, lands, _small_copies, grad_x)
    early_sum = _sum_slots("sum_small_early", early_all[0], early_all[1:])
    *late_vals, loss_sum = _unpack(late_sum, [g[n].shape for n in late_names] + [loss_row.shape])
    loss = loss_sum[0, 0]
    sg = dict(zip(late_names, late_vals))
    sg.update(zip(early_names, _unpack(early_sum, [g[n].shape for n in early_names])))
    sg["dn_conv_w"] = lax.dynamic_slice_in_dim(sg["dn_conv_w"], chip * (3 * DN_WIDTH // 4), 3 * DN_WIDTH // 4, axis=1)
    sg["ffn_conv_w"] = lax.dynamic_slice_in_dim(sg["ffn_conv_w"], chip * (2 * D_FF // 4), 2 * D_FF // 4, axis=1)

    halves = []
    for n in ("w_down", "w_up", "w_out", "w_in"):
        s_sem, r_sem, src, lands = pending[n]
        sent, got = _transfer_wait(f"reduce_{n}_wait", s_sem, r_sem, src, lands, _pieces_copies, grad_x)
        own = lax.dynamic_index_in_dim(sent[0], me_lin, axis=0, keepdims=False)
        halves.append(_sum_slots(f"sum_{n}", own, got[0]))
    theirs = _pair_swap(halves)
    core = ci.astype(jnp.int32).reshape(1)
    grads, delta, new_m, new_v = {}, {}, {}, {}
    for n, mine_h, their_h in zip(("w_down", "w_up", "w_out", "w_in"), halves, theirs):
        shp = W[n].shape
        if n == "w_in":
            to_t = lambda a: a[0].T.reshape(shp[2], shp[1] // LANES, LANES)
            from_t = lambda a: a.reshape(shp[2], shp[1]).T[None]
            outs = _adamw_transposed(f"adamw_{n}", to_t(W[n]), mine_h, their_h, to_t(Mo[n]), to_t(Vo[n]), core)
            grads[n], delta[n], new_m[n], new_v[n] = (from_t(o) for o in outs)
            continue
        gr, d, mn, vn = _adamw_halves(f"adamw_{n}", W[n][0], mine_h, their_h, Mo[n][0], Vo[n][0], core)
        grads[n], delta[n], new_m[n], new_v[n] = gr.reshape(shp), d.reshape(shp), mn.reshape(shp), vn.reshape(shp)
    shapes = [W[n].shape for n in small_names]
    for n in small_names:
        grads[n] = sg[n].reshape(W[n].shape)
    d, mn, vn = _adamw("adamw_small", _pack([W[n] for n in small_names]), _pack([grads[n] for n in small_names]),
                       _pack([Mo[n] for n in small_names]), _pack([Vo[n] for n in small_names]))
    for dst, buf in ((delta, d), (new_m, mn), (new_v, vn)):
        dst.update(zip(small_names, _unpack(buf, shapes)))

    return (loss, grad_x[None], *[grads[n] for n in ORDER], *[delta[n] for n in ORDER],
            *[new_m[n] for n in ORDER], *[new_v[n] for n in ORDER])
```

```python
import functools
import math

import jax
import jax.numpy as jnp
from jax import lax
from jax.experimental import pallas as pl
from jax.experimental.pallas import tpu as pltpu

F32 = jnp.float32
BF16 = jnp.bfloat16

D_MODEL = 1024
CHUNK = 64
SCAN_CHUNKS = 2
HEAD_DIM = 128
N_HEADS = 4
DN_WIDTH = 512
SG_WIDTH = 512
SG_GROUPS = 4
SG_BLOCK = 128
D_FF = 2816
PROJ_COLS = 3080
PROJ_PAD = 3200
BA_COL = 3072
EPS = 1e-6
NEG = -1e30
VMEM_LIMIT = 56 * 1024 * 1024

ADAM_LR = 0.001
ADAM_B1 = 0.9
ADAM_B2 = 0.999
ADAM_EPS = 1e-08
ADAM_WD = 0.01
ADAM_STEP = 10

MESH = pl.DeviceIdType.MESH
ANY = pl.BlockSpec(memory_space=pl.ANY)


def _cp(*sem):
    return pltpu.CompilerParams(dimension_semantics=sem, vmem_limit_bytes=VMEM_LIMIT)


def _bf(a):
    return a.astype(BF16)


def _nn(a, b):
    return jnp.dot(_bf(a), _bf(b), preferred_element_type=F32)


def _nt(a, b):
    return lax.dot_general(_bf(a), _bf(b), (((1,), (1,)), ((), ())), preferred_element_type=F32)


def _tn(a, b):
    return lax.dot_general(_bf(a), _bf(b), (((0,), (0,)), ((), ())), preferred_element_type=F32)


def _split(a):
    hi = _bf(a)
    return hi, _bf(a - hi.astype(F32))


def _sigmoid(x):
    return 0.5 * jnp.tanh(0.5 * x) + 0.5


def _silu(x):
    return x * _sigmoid(x)


def _dsilu(x):
    s = _sigmoid(x)
    return s * (1.0 + x * (1.0 - s))


_GELU_C = math.sqrt(2.0 / math.pi)
_GELU_A = 0.044715


def _gelu(x):
    return 0.5 * x * (1.0 + jnp.tanh(_GELU_C * (x + _GELU_A * x * x * x)))


def _dgelu(x):
    t = jnp.tanh(_GELU_C * (x + _GELU_A * x * x * x))
    return 0.5 * (1.0 + t) + 0.5 * x * (1.0 - t * t) * _GELU_C * (1.0 + 3.0 * _GELU_A * x * x)


def _softplus(x):
    return jnp.maximum(x, 0.0) + jnp.log(1.0 + jnp.exp(-jnp.abs(x)))


def _mm_nn(name, a, b, out_dtype, tm, tn, res=None):
    M, K = a.shape
    N = b.shape[1]
    tm, tn = min(tm, M), min(tn, N)

    def body(*refs):
        a_ref, b_ref = refs[0], refs[1]
        o_ref = refs[-1]
        acc = _nn(a_ref[...], b_ref[...])
        if res is not None:
            acc = acc + refs[2][...]
        o_ref[...] = acc.astype(o_ref.dtype)

    in_specs = [pl.BlockSpec((tm, K), lambda j, i: (i, 0)), pl.BlockSpec((K, tn), lambda j, i: (0, j))]
    args = [a, b]
    if res is not None:
        in_specs.append(pl.BlockSpec((tm, tn), lambda j, i: (i, j)))
        args.append(res)
    return pl.pallas_call(
        body, name=name, grid=(N // tn, M // tm), in_specs=in_specs,
        out_specs=pl.BlockSpec((tm, tn), lambda j, i: (i, j)),
        out_shape=jax.ShapeDtypeStruct((M, N), out_dtype),
        compiler_params=_cp("parallel", "parallel"))(*args)


def _with_dep(in_specs, args, dep):
    if dep is None:
        return in_specs, args
    return in_specs + [ANY], args + [dep]


SUB_ROWS = 128


def _sub_blocks(tm):
    return [slice(r0, min(r0 + SUB_ROWS, tm)) for r0 in range(0, tm, SUB_ROWS)]


def _rms_hat(xv):
    r = lax.rsqrt(jnp.mean(xv * xv, axis=-1, keepdims=True) + EPS)
    return xv * r, r


def _rms_bwd_vals(dh, xh, r, g):
    dxh = dh * g
    return r * (dxh - xh * jnp.mean(dxh * xh, axis=-1, keepdims=True)), jnp.sum(dh * xh, axis=0, keepdims=True)


def _in_proj(x, g, w4, tm=512, dep=None):
    T, K = x.shape
    ng, _, wc = w4.shape
    tm = min(tm, T)

    def body(x_ref, g_ref, w4_ref, *rest):
        p_ref, h_ref, w_ref = rest[-3:]

        @pl.when(pl.program_id(0) == 0)
        def _():
            w_ref[:, ng * wc:] = jnp.zeros((K, PROJ_PAD - ng * wc), BF16)
            for j in range(ng):
                w_ref[:, j * wc:(j + 1) * wc] = w4_ref[j]
        for r in _sub_blocks(tm):
            xh, _ = _rms_hat(x_ref[r, :])
            h_ref[r, :] = (xh * g_ref[...]).astype(BF16)
        p_ref[...] = jnp.dot(h_ref[...], w_ref[...], preferred_element_type=F32)

    in_specs, args = _with_dep(
        [pl.BlockSpec((tm, K), lambda i: (i, 0)), pl.BlockSpec((1, K), lambda i: (0, 0)),
         pl.BlockSpec((ng, K, wc), lambda i: (0, 0, 0))], [x, g, w4], dep)
    return pl.pallas_call(
        body, name="in_proj", grid=(T // tm,), in_specs=in_specs,
        out_specs=[pl.BlockSpec((tm, PROJ_PAD), lambda i: (i, 0)), pl.BlockSpec((tm, K), lambda i: (i, 0)),
                   pl.BlockSpec((K, PROJ_PAD), lambda i: (0, 0))],
        out_shape=[jax.ShapeDtypeStruct((T, PROJ_PAD), F32), jax.ShapeDtypeStruct((T, K), BF16),
                   jax.ShapeDtypeStruct((K, PROJ_PAD), BF16)],
        compiler_params=_cp("arbitrary"))(*args)


def _out_proj(mix, w, x, g, tm=512):
    T, K = mix.shape
    Dm = w.shape[1]
    tm = min(tm, T)

    def body(a_ref, w_ref, x_ref, g_ref, x2_ref, h_ref):
        x2_ref[...] = _nn(a_ref[...], w_ref[...]) + x_ref[...]
        for r in _sub_blocks(tm):
            xh, _ = _rms_hat(x2_ref[r, :])
            h_ref[r, :] = (xh * g_ref[...]).astype(BF16)

    row = lambda width: pl.BlockSpec((tm, width), lambda i: (i, 0))
    return pl.pallas_call(
        body, name="out_proj", grid=(T // tm,),
        in_specs=[row(K), pl.BlockSpec((K, Dm), lambda i: (0, 0)), row(Dm), pl.BlockSpec((1, Dm), lambda i: (0, 0))],
        out_specs=[row(Dm), row(Dm)],
        out_shape=[jax.ShapeDtypeStruct((T, Dm), F32), jax.ShapeDtypeStruct((T, Dm), BF16)],
        compiler_params=_cp("parallel"))(mix, w, x, g)


def _down_proj_loss(act, w, x2, tgt, g, tm=512):
    T, K = act.shape
    Dm = w.shape[1]
    tm = min(tm, T)

    def body(a_ref, w_ref, x_ref, t_ref, g_ref, loss_ref, dx_ref, gg_ref):
        @pl.when(pl.program_id(0) == 0)
        def _():
            gg_ref[...] = jnp.zeros_like(gg_ref)
            loss_ref[...] = jnp.zeros_like(loss_ref)
        dx_ref[...] = _nn(a_ref[...], w_ref[...]) + x_ref[...]
        for r in _sub_blocks(tm):
            xh, rr = _rms_hat(dx_ref[r, :])
            e = xh * g_ref[...] - t_ref[r, :]
            loss_ref[...] += jnp.zeros_like(loss_ref) + (0.5 / Dm) * jnp.sum(e * e)
            dx, gg = _rms_bwd_vals(e * (1.0 / Dm), xh, rr, g_ref[...])
            dx_ref[r, :] = dx
            gg_ref[...] += gg

    row = lambda width: pl.BlockSpec((tm, width), lambda i: (i, 0))
    vec = pl.BlockSpec((1, Dm), lambda i: (0, 0))
    return pl.pallas_call(
        body, name="down_proj_loss", grid=(T // tm,),
        in_specs=[row(K), pl.BlockSpec((K, Dm), lambda i: (0, 0)), row(Dm), row(Dm), vec],
        out_specs=[pl.BlockSpec((1, 128), lambda i: (0, 0)), row(Dm), vec],
        out_shape=[jax.ShapeDtypeStruct((1, 128), F32), jax.ShapeDtypeStruct((T, Dm), F32),
                   jax.ShapeDtypeStruct((1, Dm), F32)],
        compiler_params=_cp("arbitrary"))(act, w, x2, tgt, g)


def _mm_nt_rms_bwd(name, a, b, x, g, dres, tm=512, dep=None):
    M, K = a.shape
    Dm = b.shape[0]
    tm = min(tm, M)

    def body(a_ref, b_ref, x_ref, g_ref, dres_ref, *rest):
        dx_ref, gg_ref = rest[-2:]

        @pl.when(pl.program_id(0) == 0)
        def _():
            gg_ref[...] = jnp.zeros_like(gg_ref)
        dx_ref[...] = _nt(a_ref[...], b_ref[...])
        for r in _sub_blocks(tm):
            xh, rr = _rms_hat(x_ref[r, :])
            dx, gg = _rms_bwd_vals(dx_ref[r, :], xh, rr, g_ref[...])
            dx_ref[r, :] = dres_ref[r, :] + dx
            gg_ref[...] += gg

    row = lambda width: pl.BlockSpec((tm, width), lambda i: (i, 0))
    vec = pl.BlockSpec((1, Dm), lambda i: (0, 0))
    in_specs, args = _with_dep([row(K), pl.BlockSpec((Dm, K), lambda i: (0, 0)), row(Dm), vec, row(Dm)],
                               [a, b, x, g, dres], dep)
    return pl.pallas_call(
        body, name=name, grid=(M // tm,), in_specs=in_specs, out_specs=[row(Dm), vec],
        out_shape=[jax.ShapeDtypeStruct((M, Dm), F32), jax.ShapeDtypeStruct((1, Dm), F32)],
        compiler_params=_cp("arbitrary"))(*args)


def _mm_nt(name, a, b, out_dtype, tm, tn, dep=None):
    M, K = a.shape
    N = b.shape[0]
    tm, tn = min(tm, M), min(tn, N)

    def body(a_ref, b_ref, *rest):
        o_ref = rest[-1]
        o_ref[...] = _nt(a_ref[...], b_ref[...]).astype(o_ref.dtype)

    in_specs, args = _with_dep(
        [pl.BlockSpec((tm, K), lambda i, j: (i, 0)), pl.BlockSpec((tn, K), lambda i, j: (j, 0))], [a, b], dep)
    return pl.pallas_call(
        body, name=name, grid=(M // tm, N // tn), in_specs=in_specs,
        out_specs=pl.BlockSpec((tm, tn), lambda i, j: (i, j)),
        out_shape=jax.ShapeDtypeStruct((M, N), out_dtype),
        compiler_params=_cp("parallel", "parallel"))(*args)


def _mm_tn(name, a, b, tm, tn, tk, col_major_tiles=False, col_groups=None):
    T, M = a.shape
    N = b.shape[1]
    tm, tn, tk = min(tm, M), min(tn, N), min(tk, T)
    nk = T // tk

    def body(a_ref, b_ref, o_ref, acc_ref):
        k = pl.program_id(2)

        @pl.when(k == 0)
        def _():
            acc_ref[...] = jnp.zeros_like(acc_ref)
        acc_ref[...] += _tn(a_ref[...], b_ref[...])

        @pl.when(k == nk - 1)
        def _():
            if col_groups:
                for j in range(col_groups[0]):
                    o_ref[j] = acc_ref[:, j * col_groups[1]:(j + 1) * col_groups[1]].astype(BF16)
            else:
                o_ref[...] = acc_ref[...].astype(BF16).reshape(o_ref.shape)

    if col_groups:
        assert tm == M and tn == N and col_groups[0] * col_groups[1] <= N
        out_spec = pl.BlockSpec((col_groups[0], M, col_groups[1]), lambda i, j, k: (0, 0, 0))
        out_shape = jax.ShapeDtypeStruct((col_groups[0], M, col_groups[1]), BF16)
    elif col_major_tiles:
        assert tm == M
        out_spec = pl.BlockSpec((1, tm, tn), lambda i, j, k: (j, 0, 0))
        out_shape = jax.ShapeDtypeStruct((N // tn, M, tn), BF16)
    else:
        out_spec = pl.BlockSpec((tm, tn), lambda i, j, k: (i, j))
        out_shape = jax.ShapeDtypeStruct((M, N), BF16)
    return pl.pallas_call(
        body, name=name, grid=(M // tm, N // tn, nk),
        in_specs=[pl.BlockSpec((tk, tm), lambda i, j, k: (k, i)), pl.BlockSpec((tk, tn), lambda i, j, k: (k, j))],
        out_specs=out_spec, out_shape=out_shape, scratch_shapes=[pltpu.VMEM((tm, tn), F32)],
        compiler_params=_cp("parallel", "parallel", "arbitrary"))(a, b)


def _halo_prev_spec(rb, width):
    return pl.BlockSpec((8, width), lambda i: (jnp.maximum(i * (rb // 8) - 1, 0), 0))


def _halo_next_spec(rb, width, T):
    return pl.BlockSpec((8, width), lambda i: (jnp.minimum((i + 1) * (rb // 8), T // 8 - 1), 0))


LANES = 128
FF_STRIPS = D_FF // LANES
ROW_CHUNK = 32


def _strip(j, base=0):
    return pl.ds(pl.multiple_of(base + j * LANES, LANES), LANES)


def _ffn_act(up, w, b, rb=256):
    T, W = up.shape
    rb = min(rb, T)

    def body(up_ref, halo_ref, w_ref, b_ref, act_ref, ext_scr):
        first = pl.program_id(0) == 0

        def strip(j, slot):
            halves = (_strip(j), _strip(j, D_FF))
            wv = [w_ref[:, cols] for cols in halves]
            bv = [b_ref[:, cols] for cols in halves]
            for h, cols in enumerate(halves):
                ext_scr[slot, h,0:8] = jnp.where(first, 0.0, halo_ref[:, cols])
                ext_scr[slot, h,8:] = up_ref[:, cols]
            for r0 in range(0, rb, ROW_CHUNK):
                n = min(ROW_CHUNK, rb - r0)
                c = [ext_scr[slot, h,6 + r0:6 + r0 + n] * wv[h][0:1] + ext_scr[slot, h,7 + r0:7 + r0 + n] * wv[h][1:2]
                     + ext_scr[slot, h,8 + r0:8 + r0 + n] * wv[h][2:3] + bv[h] for h in range(2)]
                act_ref[r0:r0 + n, halves[0]] = (_silu(c[0]) * c[1]).astype(BF16)

        def pair(jj, carry):
            strip(2 * jj, 0)
            strip(2 * jj + 1, 1)
            return carry

        lax.fori_loop(0, FF_STRIPS // 2, pair, 0)

    return pl.pallas_call(
        body, name="ffn_act", grid=(T // rb,),
        in_specs=[pl.BlockSpec((rb, W), lambda i: (i, 0)), _halo_prev_spec(rb, W),
                  pl.BlockSpec((3, W), lambda i: (0, 0)), pl.BlockSpec((1, W), lambda i: (0, 0))],
        out_specs=pl.BlockSpec((rb, D_FF), lambda i: (i, 0)),
        out_shape=jax.ShapeDtypeStruct((T, D_FF), BF16),
        scratch_shapes=[pltpu.VMEM((2, 2, rb + 8, LANES), F32)], compiler_params=_cp("parallel"))(up, up, w, b)


def _ffn_act_bwd(up, dact, w, b, rb=128, dep=None):
    T, W = up.shape
    rb = min(rb, T)
    nb = T // rb
    re = rb + 8

    def body(up_ref, prev_ref, next_ref, da_ref, danext_ref, w_ref, b_ref, *rest):
        dup_ref, gw_ref, gb_ref, ext_scr, dc_scr = rest[-5:]
        i = pl.program_id(0)

        @pl.when(i == 0)
        def _():
            gw_ref[...] = jnp.zeros_like(gw_ref)
            gb_ref[...] = jnp.zeros_like(gb_ref)
        last = i == nb - 1

        def fold8(a):
            return jnp.sum(a.reshape(a.shape[0] // 8, 8, LANES), axis=0)

        def strip(j, slot):
            halves = (_strip(j), _strip(j, D_FF))
            wv = [w_ref[:, cols] for cols in halves]
            bv = [b_ref[:, cols] for cols in halves]
            for h, cols in enumerate(halves):
                ext_scr[slot, h,0:8] = jnp.where(i > 0, prev_ref[:, cols], 0.0)
                ext_scr[slot, h,8:8 + rb] = up_ref[:, cols]
                ext_scr[slot, h,8 + rb:] = next_ref[:, cols]
            gb = [jnp.zeros((8, LANES), F32) for _ in range(2)]
            gw = [[jnp.zeros((8, LANES), F32) for _ in range(3)] for _ in range(2)]
            for r0 in range(0, re, ROW_CHUNK):
                n = min(ROW_CHUNK, re - r0)
                tp = [[ext_scr[slot, h,6 + k + r0:6 + k + r0 + n] for k in range(3)] for h in range(2)]
                c = [tp[h][0] * wv[h][0:1] + tp[h][1] * wv[h][1:2] + tp[h][2] * wv[h][2:3] + bv[h] for h in range(2)]
                if r0 < rb:
                    da = da_ref[r0:r0 + n, halves[0]]
                else:
                    da = jnp.where(last, 0.0, danext_ref[:, halves[0]])
                s = _sigmoid(c[0])
                gs = c[0] * s
                dcs = (da * c[1] * (s + gs * (1.0 - s)), da * gs)
                for h in range(2):
                    dc_scr[slot, h,r0:r0 + n] = dcs[h]
                    if r0 < rb:
                        gb[h] = gb[h] + fold8(dcs[h])
                        for k in range(3):
                            gw[h][k] = gw[h][k] + fold8(tp[h][k] * dcs[h])
            for r0 in range(0, rb, ROW_CHUNK):
                n = min(ROW_CHUNK, rb - r0)
                for h, cols in enumerate(halves):
                    dup = (dc_scr[slot, h,r0:r0 + n] * wv[h][2:3] + dc_scr[slot, h,r0 + 1:r0 + 1 + n] * wv[h][1:2]
                           + dc_scr[slot, h,r0 + 2:r0 + 2 + n] * wv[h][0:1])
                    dup_ref[r0:r0 + n, cols] = dup.astype(BF16)
            for h, cols in enumerate(halves):
                gb_ref[:, cols] += jnp.sum(gb[h], axis=0, keepdims=True)
                for k in range(3):
                    gw_ref[k:k + 1, cols] += jnp.sum(gw[h][k], axis=0, keepdims=True)

        def pair(jj, carry):
            strip(2 * jj, 0)
            strip(2 * jj + 1, 1)
            return carry

        lax.fori_loop(0, FF_STRIPS // 2, pair, 0)

    in_specs, args = _with_dep(
        [pl.BlockSpec((rb, W), lambda i: (i, 0)), _halo_prev_spec(rb, W), _halo_next_spec(rb, W, T),
         pl.BlockSpec((rb, D_FF), lambda i: (i, 0)), _halo_next_spec(rb, D_FF, T),
         pl.BlockSpec((3, W), lambda i: (0, 0)), pl.BlockSpec((1, W), lambda i: (0, 0))],
        [up, up, up, dact, dact, w, b], dep)
    return pl.pallas_call(
        body, name="ffn_act_bwd", grid=(nb,), in_specs=in_specs,
        out_specs=[pl.BlockSpec((rb, W), lambda i: (i, 0)), pl.BlockSpec((3, W), lambda i: (0, 0)),
                   pl.BlockSpec((1, W), lambda i: (0, 0))],
        out_shape=[jax.ShapeDtypeStruct((T, W), BF16), jax.ShapeDtypeStruct((3, W), F32),
                   jax.ShapeDtypeStruct((1, W), F32)],
        scratch_shapes=[pltpu.VMEM((2, 2, rb + 16, LANES), F32), pltpu.VMEM((2, 2, re, LANES), F32)],
        compiler_params=_cp("arbitrary"))(*args)


def _lane_iota(shape):
    return lax.broadcasted_iota(jnp.int32, shape, len(shape) - 1)


def _dn_act(p, conv_w, alog_row, dtb_row, rb=256):
    T = p.shape[0]
    rb = min(rb, T)
    W3 = 3 * DN_WIDTH

    def body(p_ref, halo_ref, ba_ref, w_ref, al_ref, dt_ref, q_ref, k_ref, v_ref, bg_ref, ext_scr):
        first = pl.program_id(0) == 0
        outs = (q_ref, k_ref, v_ref)
        for j in range(3 * N_HEADS):
            kind, h = divmod(j, N_HEADS)
            cols = slice(j * HEAD_DIM, (j + 1) * HEAD_DIM)
            cur = p_ref[:, cols]
            ext_scr[j, 0:8] = jnp.where(first, 0.0, halo_ref[:, cols])
            ext_scr[j, 8:] = cur
            wv = w_ref[:, cols]
            s = _silu(ext_scr[j, 5:5 + rb] * wv[0:1] + ext_scr[j, 6:6 + rb] * wv[1:2]
                      + ext_scr[j, 7:7 + rb] * wv[2:3] + cur * wv[3:4])
            if kind < 2:
                scale = HEAD_DIM ** -0.5 if kind == 0 else 1.0
                s = s * (lax.rsqrt(jnp.sum(s * s, axis=-1, keepdims=True) + EPS) * scale)
            outs[kind][:, h * HEAD_DIM:(h + 1) * HEAD_DIM] = s
        ba = ba_ref[...]
        lane = _lane_iota(ba.shape)
        beta = _sigmoid(ba)
        g = -jnp.exp(al_ref[...]) * _softplus(ba + dt_ref[...])
        bg_ref[...] = jnp.where(lane < N_HEADS, beta, jnp.where(lane < 2 * N_HEADS, g, 0.0))

    row512 = pl.BlockSpec((rb, DN_WIDTH), lambda i: (i, 0))
    row128 = pl.BlockSpec((rb, 128), lambda i: (i, 0))
    vec128 = pl.BlockSpec((1, 128), lambda i: (0, 0))
    return pl.pallas_call(
        body, name="dn_act", grid=(T // rb,),
        in_specs=[pl.BlockSpec((rb, W3), lambda i: (i, 0)), _halo_prev_spec(rb, W3),
                  pl.BlockSpec((rb, 128), lambda i: (i, BA_COL // 128)),
                  pl.BlockSpec((4, W3), lambda i: (0, 0)), vec128, vec128],
        out_specs=[row512, row512, row512, row128],
        out_shape=[jax.ShapeDtypeStruct((T, DN_WIDTH), F32)] * 3 + [jax.ShapeDtypeStruct((T, 128), F32)],
        scratch_shapes=[pltpu.VMEM((3 * N_HEADS, rb + 8, HEAD_DIM), F32)],
        compiler_params=_cp("parallel"))(p, p, p, conv_w, alog_row, dtb_row)


def _dn_act_bwd(p, conv_w, alog_row, dtb_row, dq, dk, dv, dbg, dp_mid, rb=256):
    T = p.shape[0]
    rb = min(rb, T)
    nb = T // rb
    re = rb + 8
    W3 = 3 * DN_WIDTH

    def body(p_ref, prev_ref, next_ref, ba_ref, w_ref, al_ref, dt_ref, dq_ref, dqn_ref, dk_ref, dkn_ref,
             dv_ref, dvn_ref, dbg_ref, mid_ref, draw_ref, gw_ref, gad_ref, ext_scr, dc_scr):
        i = pl.program_id(0)
        draw_ref[:, W3:2 * W3] = mid_ref[...]

        @pl.when(i == 0)
        def _():
            gw_ref[...] = jnp.zeros_like(gw_ref)
            gad_ref[...] = jnp.zeros_like(gad_ref)
        row = lax.broadcasted_iota(jnp.int32, (re, 1), 0)
        live = (row < rb) | (i < nb - 1)
        d_refs = ((dq_ref, dqn_ref), (dk_ref, dkn_ref), (dv_ref, dvn_ref))
        for j in range(3 * N_HEADS):
            kind, h = divmod(j, N_HEADS)
            cols = slice(j * HEAD_DIM, (j + 1) * HEAD_DIM)
            hcols = slice(h * HEAD_DIM, (h + 1) * HEAD_DIM)
            ext_scr[j, 0:8] = jnp.where(i > 0, prev_ref[:, cols], 0.0)
            ext_scr[j, 8:8 + rb] = p_ref[:, cols]
            ext_scr[j, 8 + rb:] = next_ref[:, cols]
            tp = [ext_scr[j, 5 + k:5 + k + re] for k in range(4)]
            wv = w_ref[:, cols]
            c = tp[0] * wv[0:1] + tp[1] * wv[1:2] + tp[2] * wv[2:3] + tp[3] * wv[3:4]
            sg = _sigmoid(c)
            s = c * sg
            d_in = jnp.where(live, jnp.concatenate([d_refs[kind][0][:, hcols], d_refs[kind][1][:, hcols]], axis=0), 0.0)
            if kind < 2:
                scale = HEAD_DIM ** -0.5 if kind == 0 else 1.0
                n = lax.rsqrt(jnp.sum(s * s, axis=-1, keepdims=True) + EPS)
                hat = s * n
                d_in = (n * scale) * (d_in - hat * jnp.sum(hat * d_in, axis=-1, keepdims=True))
            dc = d_in * (sg + s * (1.0 - sg))
            dc_scr[j] = dc
            dcc = dc[0:rb]
            draw = (dcc * wv[3:4] + dc_scr[j, 1:1 + rb] * wv[2:3] + dc_scr[j, 2:2 + rb] * wv[1:2]
                    + dc_scr[j, 3:3 + rb] * wv[0:1])
            draw_ref[:, cols] = draw.astype(BF16)
            for k in range(4):
                gw_ref[k:k + 1, cols] += jnp.sum(tp[k][0:rb] * dcc, axis=0, keepdims=True)
        ba = ba_ref[...]
        dbg = dbg_ref[...]
        lane = _lane_iota(ba.shape)
        beta = _sigmoid(ba)
        ea = jnp.exp(al_ref[...])
        z = ba + dt_ref[...]
        d_a = dbg * (-ea) * _sigmoid(z)
        dba = jnp.where(lane < N_HEADS, dbg * beta * (1.0 - beta), jnp.where(lane < 2 * N_HEADS, d_a, 0.0))
        draw_ref[:, BA_COL:] = dba.astype(BF16)
        isg = (lane >= N_HEADS) & (lane < 2 * N_HEADS)
        g = -ea * _softplus(z)
        gad_ref[0:1, :] += jnp.sum(jnp.where(isg, dbg * g, 0.0), axis=0, keepdims=True)
        gad_ref[1:2, :] += jnp.sum(jnp.where(isg, d_a, 0.0), axis=0, keepdims=True)

    row512 = pl.BlockSpec((rb, DN_WIDTH), lambda i: (i, 0))
    row128 = pl.BlockSpec((rb, 128), lambda i: (i, 0))
    vec128 = pl.BlockSpec((1, 128), lambda i: (0, 0))
    next512 = _halo_next_spec(rb, DN_WIDTH, T)
    return pl.pallas_call(
        body, name="dn_act_bwd", grid=(nb,),
        in_specs=[pl.BlockSpec((rb, W3), lambda i: (i, 0)), _halo_prev_spec(rb, W3), _halo_next_spec(rb, W3, T),
                  pl.BlockSpec((rb, 128), lambda i: (i, BA_COL // 128)),
                  pl.BlockSpec((4, W3), lambda i: (0, 0)), vec128, vec128,
                  row512, next512, row512, next512, row512, next512, row128,
                  pl.BlockSpec((rb, W3), lambda i: (i, 0))],
        out_specs=[pl.BlockSpec((rb, PROJ_PAD), lambda i: (i, 0)),
                   pl.BlockSpec((4, W3), lambda i: (0, 0)), pl.BlockSpec((2, 128), lambda i: (0, 0))],
        out_shape=[jax.ShapeDtypeStruct((T, PROJ_PAD), BF16),
                   jax.ShapeDtypeStruct((4, W3), F32), jax.ShapeDtypeStruct((2, 128), F32)],
        scratch_shapes=[pltpu.VMEM((3 * N_HEADS, rb + 16, HEAD_DIM), F32), pltpu.VMEM((3 * N_HEADS, re, HEAD_DIM), F32)],
        compiler_params=_cp("arbitrary"))(p, p, p, p, conv_w, alog_row, dtb_row, dq, dq, dk, dk, dv, dv, dbg, dp_mid)


def _tri(incl):
    ii = lax.broadcasted_iota(jnp.int32, (CHUNK, CHUNK), 0)
    jj = lax.broadcasted_iota(jnp.int32, (CHUNK, CHUNK), 1)
    return ii, jj, ((ii >= jj) if incl else (ii > jj))


def _dn_chunk(k, bg, cb=4):
    T = k.shape[0]
    N = T // CHUNK
    cb = min(cb, N)

    def body(k_ref, bg_ref, gc_ref, gct_ref, l_ref):
        ii, jj, incl = _tri(True)
        tri = incl.astype(F32)
        U = range(cb)
        bgv = [bg_ref[u * CHUNK:(u + 1) * CHUNK, :] for u in U]
        gc = [jnp.dot(tri, bgv[u], precision=lax.Precision.HIGHEST, preferred_element_type=F32) for u in U]
        gct = [gc[u].T for u in U]
        kk = [[None] * N_HEADS for _ in U]
        for u in U:
            gc_ref[u * CHUNK:(u + 1) * CHUNK, :] = gc[u]
            gct_ref[u] = gct[u][0:8]
            for h in range(N_HEADS):
                kh = k_ref[u * CHUNK:(u + 1) * CHUNK, h * HEAD_DIM:(h + 1) * HEAD_DIM]
                kk[u][h] = _nt(kh * bgv[u][:, h:h + 1], kh)
        for u in U:
            for h in range(N_HEADS):
                gcol = gc[u][:, N_HEADS + h:N_HEADS + h + 1]
                grow = gct[u][N_HEADS + h:N_HEADS + h + 1, :]
                l_ref[u, h] = kk[u][h] * jnp.exp(jnp.where(ii > jj, gcol - grow, NEG))

    rows = cb * CHUNK
    return pl.pallas_call(
        body, name="dn_chunk", grid=(N // cb,),
        in_specs=[pl.BlockSpec((rows, DN_WIDTH), lambda n: (n, 0)), pl.BlockSpec((rows, 128), lambda n: (n, 0))],
        out_specs=[pl.BlockSpec((rows, 128), lambda n: (n, 0)), pl.BlockSpec((cb, 8, CHUNK), lambda n: (n, 0, 0)),
                   pl.BlockSpec((cb, N_HEADS, CHUNK, CHUNK), lambda n: (n, 0, 0, 0))],
        out_shape=[jax.ShapeDtypeStruct((T, 128), F32), jax.ShapeDtypeStruct((N, 8, CHUNK), F32),
                   jax.ShapeDtypeStruct((N, N_HEADS, CHUNK, CHUNK), F32)],
        compiler_params=_cp("parallel"))(k, bg)


def _tri_inv(lt):
    S = lt.shape[1]

    def body(l_ref, a_ref):
        sub = lax.broadcasted_iota(jnp.int32, (8, S), 0)
        groups = CHUNK // 8
        for i in range(CHUNK):
            acc = [((sub + 8 * k) == i).astype(F32) for k in range(groups)]
            for jb in range((i + 7) // 8):
                nk = jb + 1

                def step(j, carry, nk=nk, i=i):
                    lrow = l_ref[pl.ds(i * CHUNK + j, 1), :]
                    return tuple(carry[k] - lrow * a_ref[j, 8 * k:8 * k + 8, :] for k in range(nk))

                acc[:nk] = list(lax.fori_loop(8 * jb, min(8 * jb + 8, i), step, tuple(acc[:nk])))
            for k in range(groups):
                a_ref[i, 8 * k:8 * k + 8, :] = acc[k]

    return pl.pallas_call(
        body, name="tri_inv", out_shape=jax.ShapeDtypeStruct((CHUNK, CHUNK, S), F32),
        compiler_params=pltpu.CompilerParams(vmem_limit_bytes=VMEM_LIMIT))(lt)


def _dn_head_terms(qh, kh, vh, beta, gcol, grow):
    ii, jj, incl = _tri(True)
    gam = jnp.exp(jnp.where(incl, gcol - grow, NEG))
    glast = grow[:, CHUNK - 1:CHUNK]
    cd = jnp.exp(glast)
    shape = (CHUNK, HEAD_DIM)
    E = jnp.broadcast_to(jnp.exp(gcol), shape)
    Fd = jnp.broadcast_to(jnp.exp(glast - gcol), shape)
    beta = jnp.broadcast_to(beta, shape)
    kb = kh * beta
    return dict(ii=ii, jj=jj, gam=gam, E=E, F=Fd, beta=beta, cd=cd, kb=kb, vb=vh * beta, W=kb * E, qE=qh * E,
                kt=kh * Fd)


def _apply_a(a, u):
    hi, lo = _split(a)
    ub = _bf(u)
    return jnp.dot(hi, ub, preferred_element_type=F32) + jnp.dot(lo, ub, preferred_element_type=F32)


def _dn_scan(q, k, v, bg, gc, gct, a):
    T = q.shape[0]
    N = T // CHUNK

    cb = min(SCAN_CHUNKS, N)

    def body(q_ref, k_ref, v_ref, bg_ref, gc_ref, gct_ref, a_ref, o_ref, sall_ref, s_ref):
        @pl.when(pl.program_id(0) == 0)
        def _():
            s_ref[...] = jnp.zeros_like(s_ref)
        H = range(N_HEADS)
        sl = [slice(h * HEAD_DIM, (h + 1) * HEAD_DIM) for h in H]
        pre = []
        for u in range(cb):
            r = slice(u * CHUNK, (u + 1) * CHUNK)
            bgv, gcv, gctv = bg_ref[r, :], gc_ref[r, :], gct_ref[u]
            q_, k_ = [q_ref[r, s] for s in sl], [k_ref[r, s] for s in sl]
            t = [_dn_head_terms(q_[h], k_[h], v_ref[r, sl[h]], bgv[:, h:h + 1],
                                gcv[:, N_HEADS + h:N_HEADS + h + 1], gctv[N_HEADS + h:N_HEADS + h + 1, :]) for h in H]
            P = [_nt(q_[h], k_[h]) * t[h]["gam"] for h in H]
            pre.append((r, t, P))
        S = [s_ref[h] for h in H]
        for u in range(cb):
            r, t, P = pre[u]
            for h in H:
                sall_ref[u, h] = S[h]
            WS = [_nn(t[h]["W"], S[h]) for h in H]
            qS = [_nn(t[h]["qE"], S[h]) for h in H]
            vn = [_apply_a(a_ref[u, h], t[h]["vb"] - WS[h]) for h in H]
            Pv = [_nn(P[h], vn[h]) for h in H]
            kv = [_tn(t[h]["kt"], vn[h]) for h in H]
            for h in H:
                o_ref[r, sl[h]] = qS[h] + Pv[h]
            S = [t[h]["cd"] * S[h] + kv[h] for h in H]
        for h in H:
            s_ref[h] = S[h]

    row512 = pl.BlockSpec((cb * CHUNK, DN_WIDTH), lambda n: (n, 0))
    row128 = pl.BlockSpec((cb * CHUNK, 128), lambda n: (n, 0))
    return pl.pallas_call(
        body, name="dn_scan", grid=(N // cb,),
        in_specs=[row512, row512, row512, row128, row128, pl.BlockSpec((cb, 8, CHUNK), lambda n: (n, 0, 0)),
                  pl.BlockSpec((cb, N_HEADS, CHUNK, CHUNK), lambda n: (n, 0, 0, 0))],
        out_specs=[row512, pl.BlockSpec((cb, N_HEADS, HEAD_DIM, HEAD_DIM), lambda n: (n, 0, 0, 0))],
        out_shape=[jax.ShapeDtypeStruct((T, DN_WIDTH), F32),
                   jax.ShapeDtypeStruct((N, N_HEADS, HEAD_DIM, HEAD_DIM), F32)],
        scratch_shapes=[pltpu.VMEM((N_HEADS, HEAD_DIM, HEAD_DIM), F32)],
        compiler_params=_cp("arbitrary"))(q, k, v, bg, gc, gct, a)


def _dn_scan_bwd(q, k, v, bg, gc, gct, a, a_t, sall, do, dep=None):
    T = q.shape[0]
    N = T // CHUNK

    cb = min(SCAN_CHUNKS, N)
    nb = N // cb

    def body(q_ref, k_ref, v_ref, bg_ref, gc_ref, gct_ref, a_ref, at_ref, sall_ref, do_ref, *rest):
        dq_ref, dk_ref, dv_ref, dbg_ref, ds_ref = rest[-5:]
        @pl.when(pl.program_id(0) == 0)
        def _():
            ds_ref[...] = jnp.zeros_like(ds_ref)
        lane = _lane_iota((CHUNK, 128))
        rowi = lax.broadcasted_iota(jnp.int32, (CHUNK, 1), 0)
        ii, jj, _ = _tri(True)
        rev = (jj >= ii).astype(F32)
        H = range(N_HEADS)
        sl = [slice(h * HEAD_DIM, (h + 1) * HEAD_DIM) for h in H]
        pre = {}
        for u in reversed(range(cb)):
            r = slice(u * CHUNK, (u + 1) * CHUNK)
            bgv, gcv, gctv = bg_ref[r, :], gc_ref[r, :], gct_ref[u]
            q_, k_, v_ = [q_ref[r, s] for s in sl], [k_ref[r, s] for s in sl], [v_ref[r, s] for s in sl]
            dO = [do_ref[r, s] for s in sl]
            t = [_dn_head_terms(q_[h], k_[h], v_[h], bgv[:, h:h + 1], gcv[:, N_HEADS + h:N_HEADS + h + 1],
                                gctv[N_HEADS + h:N_HEADS + h + 1, :]) for h in H]
            beta = [t[h]["beta"] for h in H]
            S = [sall_ref[u, h] for h in H]
            A = [a_ref[u, h] for h in H]
            WS = [_nn(t[h]["W"], S[h]) for h in H]
            KK = [_nt(t[h]["kb"], k_[h]) for h in H]
            QK = [_nt(q_[h], k_[h]) for h in H]
            d_qE = [_nt(dO[h], S[h]) for h in H]
            vn = [_apply_a(A[h], t[h]["vb"] - WS[h]) for h in H]
            PtdO = [_tn(QK[h] * t[h]["gam"], dO[h]) for h in H]
            qEdO = [_tn(t[h]["qE"], dO[h]) for h in H]
            dOvn = [_nt(dO[h], vn[h]) for h in H]
            dQK = [jnp.where(ii >= jj, dOvn[h], 0.0) * t[h]["gam"] for h in H]
            dQKk = [_nn(dQK[h], k_[h]) for h in H]
            dQKq = [_tn(dQK[h], q_[h]) for h in H]
            pre[u] = (r, q_, k_, v_, beta, t, S, A, KK, QK, d_qE, vn, PtdO, qEdO, dQK, dQKk, dQKq)
        dSn = [ds_ref[h] for h in H]
        for u in reversed(range(cb)):
            r, q_, k_, v_, beta, t, S, A, KK, QK, d_qE, vn, PtdO, qEdO, dQK, dQKk, dQKq = pre[u]
            gam, E, Fd, cd, kb = ([t[h][n] for h in H] for n in ("gam", "E", "F", "cd", "kb"))
            ktdS = [_nn(t[h]["kt"], dSn[h]) for h in H]
            dU = [_apply_a(at_ref[u, h], PtdO[h] + ktdS[h]) for h in H]
            d_kt = [_nt(vn[h], dSn[h]) for h in H]
            dUvn = [_nt(dU[h], vn[h]) for h in H]
            dUS = [_nt(dU[h], S[h]) for h in H]
            WdU = [_tn(t[h]["W"], dU[h]) for h in H]
            d_cd = [jnp.sum(S[h] * dSn[h]) for h in H]
            dSn = [cd[h] * dSn[h] + qEdO[h] - WdU[h] for h in H]
            dKK = [jnp.where(ii > jj, -dUvn[h], 0.0) * gam[h] for h in H]
            dKKk = [_nn(dKK[h], k_[h]) for h in H]
            dKKkb = [_tn(dKK[h], kb[h]) for h in H]
            dbeta_arr = jnp.zeros((CHUNK, 128), F32)
            dgc_arr = jnp.zeros((CHUNK, 128), F32)
            for h in H:
                dW = -dUS[h]
                dq_ref[r, sl[h]] = dQKk[h] + d_qE[h] * E[h]
                d_kb = dKKk[h] + dW * E[h]
                dk_ref[r, sl[h]] = dQKq[h] + dKKkb[h] + d_kb * beta[h] + d_kt[h] * Fd[h]
                dv_ref[r, sl[h]] = dU[h] * beta[h]
                Z = dQK[h] * QK[h] + dKK[h] * KK[h]
                dbeta = jnp.sum(dU[h] * v_[h] + d_kb * k_[h], axis=-1, keepdims=True)
                m_e = (dW * kb[h] + d_qE[h] * q_[h]) * E[h]
                m_f = d_kt[h] * k_[h] * Fd[h]
                zdiag = jnp.where(ii == jj, jnp.sum(Z, axis=0, keepdims=True), 0.0)
                dgc = (jnp.sum(m_e - m_f, axis=-1, keepdims=True) + jnp.sum(Z - zdiag, axis=-1, keepdims=True)
                       + jnp.where(rowi == CHUNK - 1, jnp.sum(m_f) + d_cd[h] * cd[h], 0.0))
                dbeta_arr = dbeta_arr + jnp.where(lane == h, dbeta, 0.0)
                dgc_arr = dgc_arr + jnp.where(lane == N_HEADS + h, dgc, 0.0)
            dbg_ref[r, :] = dbeta_arr + jnp.dot(rev, dgc_arr, precision=lax.Precision.HIGHEST,
                                                preferred_element_type=F32)
        for h in H:
            ds_ref[h] = dSn[h]

    row512 = pl.BlockSpec((cb * CHUNK, DN_WIDTH), lambda n: (nb - 1 - n, 0))
    row128 = pl.BlockSpec((cb * CHUNK, 128), lambda n: (nb - 1 - n, 0))
    in_specs, args = _with_dep(
        [row512, row512, row512, row128, row128,
         pl.BlockSpec((cb, 8, CHUNK), lambda n: (nb - 1 - n, 0, 0)),
         pl.BlockSpec((cb, N_HEADS, CHUNK, CHUNK), lambda n: (nb - 1 - n, 0, 0, 0)),
         pl.BlockSpec((cb, N_HEADS, CHUNK, CHUNK), lambda n: (nb - 1 - n, 0, 0, 0)),
         pl.BlockSpec((cb, N_HEADS, HEAD_DIM, HEAD_DIM), lambda n: (nb - 1 - n, 0, 0, 0)), row512],
        [q, k, v, bg, gc, gct, a, a_t, sall, do], dep)
    return pl.pallas_call(
        body, name="dn_scan_bwd", grid=(nb,), in_specs=in_specs,
        out_specs=[row512, row512, row512, row128],
        out_shape=[jax.ShapeDtypeStruct((T, DN_WIDTH), F32)] * 3 + [jax.ShapeDtypeStruct((T, 128), F32)],
        scratch_shapes=[pltpu.VMEM((N_HEADS, HEAD_DIM, HEAD_DIM), F32)],
        compiler_params=_cp("arbitrary"))(*args)


def _sg_mask():
    ii = lax.broadcasted_iota(jnp.int32, (SG_BLOCK, SG_BLOCK), 0) // CHUNK
    jj = lax.broadcasted_iota(jnp.int32, (SG_BLOCK, SG_BLOCK), 1) // CHUNK
    return jj <= ii


def _mix_fwd(o, p, ong, sgn, sgw, sgbt):
    T = o.shape[0]
    rb = SG_BLOCK

    def body(o_ref, gate_ref, u_ref, vg_ref, ong_ref, sgn_ref, sgw_ref, sgbt_ref, mix_ref):
        mask = _sg_mask()
        gate = gate_ref[...]
        for h in range(N_HEADS):
            sl = slice(h * HEAD_DIM, (h + 1) * HEAD_DIM)
            oh = o_ref[:, sl]
            r = lax.rsqrt(jnp.mean(oh * oh, axis=-1, keepdims=True) + EPS)
            mix_ref[:, sl] = (oh * r * ong_ref[...] * _silu(gate[:, sl])).astype(BF16)
        for gi in range(SG_GROUPS):
            sl = slice(gi * SG_BLOCK, (gi + 1) * SG_BLOCK)
            gv = _gelu(vg_ref[:, sl])
            r = lax.rsqrt(jnp.mean(gv * gv, axis=-1, keepdims=True) + EPS)
            vh = gv * r * sgn_ref[:, sl]
            s = _nn(jnp.where(mask, sgw_ref[gi], 0.0), vh) + sgbt_ref[:, gi:gi + 1]
            mix_ref[:, DN_WIDTH + gi * SG_BLOCK:DN_WIDTH + (gi + 1) * SG_BLOCK] = (_gelu(u_ref[:, sl]) * s).astype(BF16)

    def col(c):
        return pl.BlockSpec((rb, 512), lambda i: (i, c))
    return pl.pallas_call(
        body, name="mix_fwd", grid=(T // rb,),
        in_specs=[pl.BlockSpec((rb, DN_WIDTH), lambda i: (i, 0)), col(3), col(4), col(5),
                  pl.BlockSpec((1, 128), lambda i: (0, 0)), pl.BlockSpec((1, SG_WIDTH), lambda i: (0, 0)),
                  pl.BlockSpec((SG_GROUPS, SG_BLOCK, SG_BLOCK), lambda i: (0, 0, 0)),
                  pl.BlockSpec((SG_BLOCK, 128), lambda i: (0, 0))],
        out_specs=pl.BlockSpec((rb, D_MODEL), lambda i: (i, 0)),
        out_shape=jax.ShapeDtypeStruct((T, D_MODEL), BF16),
        compiler_params=_cp("parallel"))(o, p, p, p, ong, sgn, sgw, sgbt)


def _mix_bwd(o, p, ong, sgn, sgw, sgbt, dmix, dep=None):
    T = o.shape[0]
    rb = SG_BLOCK

    def body(o_ref, gate_ref, u_ref, vg_ref, ong_ref, sgn_ref, sgw_ref, sgbt_ref, dmix_ref, *rest):
        do_ref, dp_ref, gong_ref, gsgn_ref, gsgw_ref, gsgbt_ref = rest[-6:]
        @pl.when(pl.program_id(0) == 0)
        def _():
            gong_ref[...] = jnp.zeros_like(gong_ref)
            gsgn_ref[...] = jnp.zeros_like(gsgn_ref)
            gsgw_ref[...] = jnp.zeros_like(gsgw_ref)
            gsgbt_ref[...] = jnp.zeros_like(gsgbt_ref)
        mask = _sg_mask()
        gate = gate_ref[...]
        lane = _lane_iota((SG_BLOCK, 128))
        for h in range(N_HEADS):
            sl = slice(h * HEAD_DIM, (h + 1) * HEAD_DIM)
            oh = o_ref[:, sl]
            dm = dmix_ref[:, sl]
            r = lax.rsqrt(jnp.mean(oh * oh, axis=-1, keepdims=True) + EPS)
            oh_hat = oh * r
            gt = gate[:, sl]
            sg = _silu(gt)
            dp_ref[:, sl] = (dm * oh_hat * ong_ref[...] * _dsilu(gt)).astype(BF16)
            dn_ = dm * sg
            gong_ref[...] += jnp.sum(dn_ * oh_hat, axis=0, keepdims=True)
            dhat = dn_ * ong_ref[...]
            do_ref[:, sl] = r * (dhat - oh_hat * jnp.mean(dhat * oh_hat, axis=-1, keepdims=True))
        for gi in range(SG_GROUPS):
            sl = slice(gi * SG_BLOCK, (gi + 1) * SG_BLOCK)
            vraw = vg_ref[:, sl]
            gv = _gelu(vraw)
            r = lax.rsqrt(jnp.mean(gv * gv, axis=-1, keepdims=True) + EPS)
            vhat = gv * r
            vn = vhat * sgn_ref[:, sl]
            wm = jnp.where(mask, sgw_ref[gi], 0.0)
            s = _nn(wm, vn) + sgbt_ref[:, gi:gi + 1]
            uraw = u_ref[:, sl]
            dm = dmix_ref[:, DN_WIDTH + gi * SG_BLOCK:DN_WIDTH + (gi + 1) * SG_BLOCK]
            dp_ref[:, DN_WIDTH + gi * SG_BLOCK:DN_WIDTH + (gi + 1) * SG_BLOCK] = (dm * s * _dgelu(uraw)).astype(BF16)
            ds = dm * _gelu(uraw)
            gsgbt_ref[...] += jnp.where(lane == gi, jnp.sum(ds, axis=-1, keepdims=True), 0.0)
            gsgw_ref[gi] += jnp.where(mask, _nt(ds, vn), 0.0)
            dvn = _tn(wm, ds)
            gsgn_ref[:, sl] += jnp.sum(dvn * vhat, axis=0, keepdims=True)
            dhat = dvn * sgn_ref[:, sl]
            dgv = r * (dhat - vhat * jnp.mean(dhat * vhat, axis=-1, keepdims=True))
            dp_ref[:, 2 * DN_WIDTH + gi * SG_BLOCK:2 * DN_WIDTH + (gi + 1) * SG_BLOCK] = (dgv * _dgelu(vraw)).astype(BF16)

    def col(c):
        return pl.BlockSpec((rb, 512), lambda i: (i, c))
    full = lambda *s: pl.BlockSpec(s, lambda i: (0,) * len(s))
    in_specs, args = _with_dep(
        [pl.BlockSpec((rb, DN_WIDTH), lambda i: (i, 0)), col(3), col(4), col(5),
         full(1, 128), full(1, SG_WIDTH), full(SG_GROUPS, SG_BLOCK, SG_BLOCK), full(SG_BLOCK, 128),
         pl.BlockSpec((rb, D_MODEL), lambda i: (i, 0))],
        [o, p, p, p, ong, sgn, sgw, sgbt, dmix], dep)
    return pl.pallas_call(
        body, name="mix_bwd", grid=(T // rb,), in_specs=in_specs,
        out_specs=[pl.BlockSpec((rb, DN_WIDTH), lambda i: (i, 0)), pl.BlockSpec((rb, 3 * 512), lambda i: (i, 0)),
                   full(1, 128), full(1, SG_WIDTH), full(SG_GROUPS, SG_BLOCK, SG_BLOCK), full(SG_BLOCK, 128)],
        out_shape=[jax.ShapeDtypeStruct((T, DN_WIDTH), F32), jax.ShapeDtypeStruct((T, 3 * 512), BF16),
                   jax.ShapeDtypeStruct((1, 128), F32), jax.ShapeDtypeStruct((1, SG_WIDTH), F32),
                   jax.ShapeDtypeStruct((SG_GROUPS, SG_BLOCK, SG_BLOCK), F32),
                   jax.ShapeDtypeStruct((SG_BLOCK, 128), F32)],
        compiler_params=_cp("arbitrary"))(*args)


def _pad_lanes(row, offset=0):
    n = row.shape[1]
    return jnp.pad(row, ((0, 0), (offset, 128 - n - offset)))


def _local_step(x, tgt, w, dep=None, late_weights=None, on_grad=None):
    T = x.shape[0]
    N = T // CHUNK
    on_grad = on_grad or (lambda name, g: None)
    alog_row = _pad_lanes(w["dn_a_log"], N_HEADS)
    dtb_row = _pad_lanes(w["dn_dt_bias"], N_HEADS)
    sgbt = jnp.pad(w["sg_b"].T, ((0, 0), (0, 128 - SG_GROUPS)))

    p, h1, w_in_pad = _in_proj(x, w["attn_norm_g"], w["w_in"], dep=dep)
    q, k, v, bg = _dn_act(p, w["dn_conv_w"], alog_row, dtb_row)
    gc, gct, lmat = _dn_chunk(k, bg)
    lt = lmat.reshape(N * N_HEADS, CHUNK * CHUNK).T
    at = _tri_inv(lt)
    a = at.reshape(CHUNK * CHUNK, N * N_HEADS).T.reshape(N, N_HEADS, CHUNK, CHUNK)
    a_t = at.transpose(1, 0, 2).reshape(CHUNK * CHUNK, N * N_HEADS).T.reshape(N, N_HEADS, CHUNK, CHUNK)
    o, sall = _dn_scan(q, k, v, bg, gc, gct, a)
    mix = _mix_fwd(o, p, w["dn_out_norm_g"], w["sg_norm_g"], w["sg_w"], sgbt)
    if late_weights is not None:
        w = {**w, **late_weights(mix)}
    x2, h2 = _out_proj(mix, w["w_out"], x, w["ffn_norm_g"])
    up = _mm_nn("up_proj", h2, w["w_up"], F32, 512, D_FF)
    act = _ffn_act(up, w["ffn_conv_w"], w["ffn_conv_b"])
    loss, dx3, g_final = _down_proj_loss(act, w["w_down"], x2, tgt, w["final_norm_g"])

    dact = _mm_nt("d_act", dx3, w["w_down"], F32, 512, D_FF)
    g_w_down = _mm_tn("g_w_down", act, dx3, D_FF, 1024, 1024)
    tok = on_grad("w_down", g_w_down)
    dup, g_ffn_conv_w, g_ffn_conv_b = _ffn_act_bwd(up, dact, w["ffn_conv_w"], w["ffn_conv_b"], dep=tok)
    g_w_up = _mm_tn("g_w_up", h2, dup, 1024, 2 * D_FF // 4, 2048, col_major_tiles=True)
    tok = on_grad("w_up", g_w_up)
    dx2, g_ffn_norm = _mm_nt_rms_bwd("d_h2", dup, w["w_up"], x2, w["ffn_norm_g"], dx3, dep=tok)
    dmix = _mm_nt("d_mix", dx2, w["w_out"], F32, 512, 1024)
    g_w_out = _mm_tn("g_w_out", mix, dx2, 1024, 1024, 1024)
    tok = on_grad("w_out", g_w_out)
    do, dp_mid, g_ong, g_sgn, g_sgw, g_sgbt = _mix_bwd(o, p, w["dn_out_norm_g"], w["sg_norm_g"], w["sg_w"], sgbt,
                                                      dmix, dep=tok)
    early = dict(dn_out_norm_g=g_ong, sg_norm_g=g_sgn, sg_w=g_sgw, sg_b=g_sgbt[:, :SG_GROUPS].T,
                 ffn_norm_g=g_ffn_norm, ffn_conv_w=g_ffn_conv_w, ffn_conv_b=g_ffn_conv_b, final_norm_g=g_final)
    tok = on_grad("small_early", early)
    dq, dk, dv, dbg = _dn_scan_bwd(q, k, v, bg, gc, gct, a, a_t, sall, do, dep=tok)
    dp, g_dn_conv_w, g_ad = _dn_act_bwd(p, w["dn_conv_w"], alog_row, dtb_row, dq, dk, dv, dbg, dp_mid)
    g_w_in = _mm_tn("g_w_in", h1, dp, 1024, PROJ_PAD, 1024, col_groups=(4, PROJ_COLS // 4))
    tok = on_grad("w_in", g_w_in)
    grad_x, g_attn_norm = _mm_nt_rms_bwd("d_h1", dp, w_in_pad, x, w["attn_norm_g"], dx2, dep=tok)

    grads = dict(
        attn_norm_g=g_attn_norm, w_in=g_w_in, dn_conv_w=g_dn_conv_w,
        dn_a_log=g_ad[0:1, N_HEADS:2 * N_HEADS], dn_dt_bias=g_ad[1:2, N_HEADS:2 * N_HEADS],
        w_out=g_w_out, w_up=g_w_up, w_down=g_w_down, **early)
    return loss, grad_x, grads


def _me():
    return lax.axis_index("x"), lax.axis_index("y"), lax.axis_index("c")


def _peer(rel):
    x, y, c = _me()
    return {"x": (1 - x, y, c), "y": (x, 1 - y, c), "xy": (1 - x, 1 - y, c), "c": (x, y, 1 - c)}[rel]


def _chip_of(dev):
    return 2 * dev[0] + dev[1]


CHIP_RELS = ("x", "y", "xy")


def _run_copies(copies, sends, recvs):
    for cp in copies:
        cp.start()
    for cp in recvs:
        cp.wait_recv()
    for cp in sends:
        cp.wait_send()


def _gather_first(w_shard, small_shard):
    R = w_shard.shape[0]
    r2 = R // 2

    def body(w_ref, s_ref, w_out, s_out, send_sems, recv_sems):
        x, y, c = _me()
        me = _chip_of((x, y))
        sib = _peer("c")

        def half(chip, core):
            return w_out.at[chip, pl.ds(pl.multiple_of(core * r2, 8), r2), :]

        def copy(k, src, dst, to):
            return pltpu.make_async_remote_copy(src_ref=src, dst_ref=dst, send_sem=send_sems.at[k],
                                                recv_sem=recv_sems.at[k], device_id=to, device_id_type=MESH)

        own_rows = w_ref.at[pl.ds(pl.multiple_of(c * r2, 8), r2), :]
        first = [copy(r, own_rows, half(me, c), _peer(rel)) for r, rel in enumerate(CHIP_RELS)]
        first += [copy(3 + r, s_ref, s_out.at[me], _peer(rel)) for r, rel in enumerate(CHIP_RELS)]
        for cp in first:
            cp.start()
        passed = []
        for r, rel in enumerate(CHIP_RELS):
            their = _chip_of(_peer(rel))
            copy(r, own_rows, half(their, c), _peer(rel)).wait_recv()
            fwd = copy(6 + r, half(their, c), half(their, c), sib)
            fwd.start()
            passed.append(fwd)
        for r, rel in enumerate(CHIP_RELS):
            their = _chip_of(_peer(rel))
            copy(3 + r, s_ref, s_out.at[their], _peer(rel)).wait_recv()
            copy(6 + r, own_rows, half(their, 1 - c), sib).wait_recv()
        for cp in first + passed:
            cp.wait_send()

    w_all, s_all = pl.pallas_call(
        body, name="gather_first", in_specs=[ANY, ANY], out_specs=[ANY, ANY],
        out_shape=[jax.ShapeDtypeStruct((4,) + w_shard.shape, w_shard.dtype),
                   jax.ShapeDtypeStruct((4,) + small_shard.shape, small_shard.dtype)],
        scratch_shapes=[pltpu.SemaphoreType.DMA((9,)), pltpu.SemaphoreType.DMA((9,))])(w_shard, small_shard)
    me = _chip_of(_me())
    return (lax.dynamic_update_index_in_dim(w_all, w_shard, me, 0),
            lax.dynamic_update_index_in_dim(s_all, small_shard, me, 0))


OTHERS = tuple((fx, fy, fc) for fx in (0, 1) for fy in (0, 1) for fc in (0, 1) if (fx, fy, fc) != (0, 0, 0))


def _other(flip):
    x, y, c = _me()
    return (x ^ flip[0], y ^ flip[1], c ^ flip[2])


def _linear(dev):
    return 4 * dev[0] + 2 * dev[1] + dev[2]


def _exchange_small(small):
    def body(small_ref, out_ref, send_sems, recv_sems):
        my_slot = _linear(_me())
        sends, recvs = [], []
        for k, flip in enumerate(OTHERS):
            peer = _other(flip)
            sends.append(pltpu.make_async_remote_copy(
                src_ref=small_ref, dst_ref=out_ref.at[my_slot], send_sem=send_sems.at[k], recv_sem=recv_sems.at[k],
                device_id=peer, device_id_type=MESH))
            recvs.append(pltpu.make_async_remote_copy(
                src_ref=small_ref, dst_ref=out_ref.at[_linear(peer)], send_sem=send_sems.at[k],
                recv_sem=recv_sems.at[k], device_id=peer, device_id_type=MESH))
        _run_copies(sends, sends, recvs)

    out = pl.pallas_call(
        body, name="exchange_small", in_specs=[ANY], out_specs=ANY,
        out_shape=jax.ShapeDtypeStruct((8,) + small.shape, small.dtype),
        scratch_shapes=[pltpu.SemaphoreType.DMA((7,)), pltpu.SemaphoreType.DMA((7,))])(small)
    return lax.dynamic_update_index_in_dim(out, small, _linear(_me()), 0)


def _pair_swap(halves):
    n = len(halves)

    def body(*refs):
        src, out = refs[:n], refs[n:2 * n]
        send_sems, recv_sems = refs[2 * n:]
        sib = _peer("c")
        copies = [pltpu.make_async_remote_copy(
            src_ref=src[i], dst_ref=out[i], send_sem=send_sems.at[i], recv_sem=recv_sems.at[i],
            device_id=sib, device_id_type=MESH) for i in range(n)]
        _run_copies(copies, copies, copies)

    return pl.pallas_call(
        body, name="pair_swap", in_specs=[ANY] * n, out_specs=[ANY] * n,
        out_shape=[jax.ShapeDtypeStruct(h.shape, h.dtype) for h in halves],
        scratch_shapes=[pltpu.SemaphoreType.DMA((n,)), pltpu.SemaphoreType.DMA((n,))])(*halves)


HBM = pl.BlockSpec(memory_space=pltpu.HBM)
SEM = pl.BlockSpec(memory_space=pltpu.SEMAPHORE)
EFFECT = pltpu.SideEffectType.DATAFLOW_SIDE_EFFECTING


def _hbm(a):
    return pltpu.with_memory_space_constraint(a, pltpu.HBM)


def _transfer_start(name, srcs, lands, n_copies, make_copies, after=None):
    n, m = len(srcs), len(lands)

    def body(*refs):
        src, land = refs[:n], refs[n:n + m]
        outs = refs[n + m + (after is not None):]
        send_sems, recv_sems, token = outs[0], outs[1], outs[-1]
        for cp in make_copies(src, land, send_sems, recv_sems):
            cp.start()
        token[...] = jnp.zeros_like(token)

    arrs = list(srcs) + list(lands)
    in_specs, args = _with_dep([HBM] * (n + m), [_hbm(a) for a in arrs], after)
    out = pl.pallas_call(
        body, name=name,
        out_shape=(pltpu.SemaphoreType.DMA((n_copies,)), pltpu.SemaphoreType.DMA((n_copies,)),
                   *[pltpu.HBM(a.shape, a.dtype) for a in arrs], jax.ShapeDtypeStruct((8, 128), F32)),
        in_specs=in_specs,
        out_specs=(SEM, SEM, *[HBM] * (n + m), pl.BlockSpec(memory_space=pltpu.VMEM)),
        input_output_aliases={i: 2 + i for i in range(n + m)},
        compiler_params=pltpu.CompilerParams(has_side_effects=EFFECT))(*args)
    return out[0], out[1], list(out[2:2 + n]), list(out[2 + n:2 + n + m]), out[-1]


def _transfer_wait(name, send_sems, recv_sems, srcs, lands, make_copies, after):
    n, m = len(srcs), len(lands)

    def body(*refs):
        src, land = refs[:n], refs[n:n + m]
        s_sems, r_sems = refs[n + m], refs[n + m + 1]
        for cp in make_copies(src, land, s_sems, r_sems):
            cp.wait_send()
            cp.wait_recv()

    arrs = list(srcs) + list(lands)
    out = pl.pallas_call(
        body, name=name, out_shape=tuple(pltpu.HBM(a.shape, a.dtype) for a in arrs),
        in_specs=[HBM] * (n + m) + [SEM, SEM, ANY], out_specs=tuple([HBM] * (n + m)),
        input_output_aliases={i: i for i in range(n + m)},
        compiler_params=pltpu.CompilerParams(has_side_effects=EFFECT))(*arrs, send_sems, recv_sems, after)
    return list(out[:n]), list(out[n:])


def _gather_copies(src, land, send_sems, recv_sems):
    me = _chip_of(_me())
    copies = []
    for i in range(len(src)):
        for r, rel in enumerate(CHIP_RELS):
            k = 3 * i + r
            copies.append(pltpu.make_async_remote_copy(
                src_ref=src[i], dst_ref=land[i].at[me], send_sem=send_sems.at[k], recv_sem=recv_sems.at[k],
                device_id=_peer(rel), device_id_type=MESH))
    return copies


def _small_copies(src, land, send_sems, recv_sems):
    my_slot = _linear(_me())
    return [pltpu.make_async_remote_copy(
        src_ref=src[0], dst_ref=land[0].at[my_slot], send_sem=send_sems.at[k], recv_sem=recv_sems.at[k],
        device_id=_other(flip), device_id_type=MESH) for k, flip in enumerate(OTHERS)]


def _pieces_copies(src, land, send_sems, recv_sems):
    copies = []
    for k, flip in enumerate(OTHERS):
        peer = _other(flip)
        copies.append(pltpu.make_async_remote_copy(
            src_ref=src[0].at[_linear(peer)], dst_ref=land[0].at[k], send_sem=send_sems.at[k],
            recv_sem=recv_sems.at[k], device_id=peer, device_id_type=MESH))
    return copies


def _row_block(rows, cols, budget=2 * 1024 * 1024):
    rb = max(8, (budget // (4 * cols)) // 8 * 8)
    while rows % rb:
        rb -= 8
    return rb if rb > 0 else rows


def _sum_slots(name, first, rest):
    R, Cc = first.shape
    K = rest.shape[0]
    rb = _row_block(R, Cc)

    def body(f_ref, r_ref, o_ref):
        acc = f_ref[...].astype(F32)
        for j in range(K):
            acc = acc + r_ref[j].astype(F32)
        o_ref[...] = acc

    return pl.pallas_call(
        body, name=name, grid=(R // rb,),
        in_specs=[pl.BlockSpec((rb, Cc), lambda i: (i, 0)), pl.BlockSpec((K, rb, Cc), lambda i: (0, i, 0))],
        out_specs=pl.BlockSpec((rb, Cc), lambda i: (i, 0)),
        out_shape=jax.ShapeDtypeStruct((R, Cc), F32), compiler_params=_cp("parallel"))(first, rest)


def _adamw_math(w, gv, m, v):
    mn = ADAM_B1 * m + (1.0 - ADAM_B1) * gv
    vn = ADAM_B2 * v + (1.0 - ADAM_B2) * (gv * gv)
    m_hat = mn / (1.0 - ADAM_B1 ** ADAM_STEP)
    v_hat = vn / (1.0 - ADAM_B2 ** ADAM_STEP)
    return -ADAM_LR * (m_hat / (jnp.sqrt(v_hat) + ADAM_EPS) + ADAM_WD * w), mn, vn


def _adamw_halves(name, w, mine, theirs, m, v, core):
    R, Cc = w.shape
    r2 = R // 2
    rb = _row_block(r2, Cc, 1024 * 1024)
    nb2 = r2 // rb

    def body(c_ref, w_ref, mine_ref, theirs_ref, m_ref, v_ref, g_ref, d_ref, mo_ref, vo_ref):
        is_mine = (pl.program_id(0) // nb2) == c_ref[0]
        gv = jnp.where(is_mine, mine_ref[...], theirs_ref[...])
        g_ref[...] = gv
        d_ref[...], mo_ref[...], vo_ref[...] = _adamw_math(w_ref[...], gv, m_ref[...], v_ref[...])

    blk = pl.BlockSpec((rb, Cc), lambda i, c: (i, 0))
    half = lambda own: pl.BlockSpec(
        (rb, Cc), lambda i, c: (jnp.clip(i - (c[0] if own else 1 - c[0]) * nb2, 0, nb2 - 1), 0))
    return pl.pallas_call(
        body, name=name,
        grid_spec=pltpu.PrefetchScalarGridSpec(
            num_scalar_prefetch=1, grid=(2 * nb2,), in_specs=[blk, half(True), half(False), blk, blk],
            out_specs=[blk] * 4),
        out_shape=[jax.ShapeDtypeStruct((R, Cc), F32)] * 4, compiler_params=_cp("parallel"))(core, w, mine, theirs, m, v)


def _adamw_transposed(name, wt, mine, theirs, mt, vt, core):
    Cc, kh_n, _ = wt.shape
    r2 = mine.shape[0]
    per_half = kh_n // 2
    nb = -(-Cc // LANES)

    def body(c_ref, w_ref, mine_ref, theirs_ref, m_ref, v_ref, g_ref, d_ref, mo_ref, vo_ref):
        first = c_ref[0] == 0
        halves = (jnp.where(first, mine_ref[...], theirs_ref[...]).T,
                  jnp.where(first, theirs_ref[...], mine_ref[...]).T)
        for kh in range(kh_n):
            lo = (kh % per_half) * LANES
            g_ref[:, kh, :] = halves[kh // per_half][:, lo:lo + LANES]
        d_ref[...], mo_ref[...], vo_ref[...] = _adamw_math(w_ref[...], g_ref[...], m_ref[...], v_ref[...])

    blk = pl.BlockSpec((LANES, kh_n, LANES), lambda i, c: (i, 0, 0))
    half = pl.BlockSpec((r2, LANES), lambda i, c: (0, i))
    return pl.pallas_call(
        body, name=name,
        grid_spec=pltpu.PrefetchScalarGridSpec(
            num_scalar_prefetch=1, grid=(nb,), in_specs=[blk, half, half, blk, blk], out_specs=[blk] * 4),
        out_shape=[jax.ShapeDtypeStruct(wt.shape, F32)] * 4, compiler_params=_cp("parallel"))(
            core, wt, mine, theirs, mt, vt)


def _adamw(name, w, g, m, v):
    R, Cc = w.shape
    rb = _row_block(R, Cc, 1024 * 1024)

    def body(w_ref, g_ref, m_ref, v_ref, d_ref, mo_ref, vo_ref):
        d_ref[...], mo_ref[...], vo_ref[...] = _adamw_math(w_ref[...], g_ref[...], m_ref[...], v_ref[...])

    blk = pl.BlockSpec((rb, Cc), lambda i: (i, 0))
    return pl.pallas_call(
        body, name=name, grid=(R // rb,), in_specs=[blk] * 4, out_specs=[blk] * 3,
        out_shape=[jax.ShapeDtypeStruct((R, Cc), F32)] * 3, compiler_params=_cp("parallel"))(w, g, m, v)


def _pack(arrs):
    rows = []
    for a in arrs:
        flat = a.reshape(-1)
        pad = (-flat.shape[0]) % 128
        rows.append(jnp.pad(flat, (0, pad)).reshape(-1, 128))
    buf = jnp.concatenate(rows, axis=0)
    return jnp.pad(buf, ((0, (-buf.shape[0]) % 8), (0, 0)))


def _unpack(buf, shapes):
    out, r = [], 0
    for s in shapes:
        n = math.prod(s)
        nr = -(-n // 128)
        out.append(buf[r:r + nr].reshape(-1)[:n].reshape(s))
        r += nr
    return out


BIG = ("w_in", "w_out", "w_up", "w_down")
CONV = ("dn_conv_w", "ffn_conv_w")
REPL = ("attn_norm_g", "dn_a_log", "dn_dt_bias", "dn_out_norm_g", "sg_norm_g", "sg_w", "sg_b",
        "ffn_norm_g", "ffn_conv_b", "final_norm_g")
ORDER = ("attn_norm_g", "w_in", "dn_conv_w", "dn_a_log", "dn_dt_bias", "dn_out_norm_g", "sg_norm_g", "sg_w",
         "sg_b", "w_out", "ffn_norm_g", "w_up", "ffn_conv_w", "ffn_conv_b", "w_down", "final_norm_g")


def kernel(x, attn_norm_g, w_in, dn_conv_w, dn_a_log, dn_dt_bias, dn_out_norm_g, sg_norm_g, sg_w, sg_b, w_out, ffn_norm_g, w_up, ffn_conv_w, ffn_conv_b, w_down, final_norm_g, loss_target, m_attn_norm_g, m_w_in, m_dn_conv_w, m_dn_a_log, m_dn_dt_bias, m_dn_out_norm_g, m_sg_norm_g, m_sg_w, m_sg_b, m_w_out, m_ffn_norm_g, m_w_up, m_ffn_conv_w, m_ffn_conv_b, m_w_down, m_final_norm_g, v_attn_norm_g, v_w_in, v_dn_conv_w, v_dn_a_log, v_dn_dt_bias, v_dn_out_norm_g, v_sg_norm_g, v_sg_w, v_sg_b, v_w_out, v_ffn_norm_g, v_w_up, v_ffn_conv_w, v_ffn_conv_b, v_w_down, v_final_norm_g):
    W = dict(attn_norm_g=attn_norm_g, w_in=w_in, dn_conv_w=dn_conv_w, dn_a_log=dn_a_log, dn_dt_bias=dn_dt_bias,
             dn_out_norm_g=dn_out_norm_g, sg_norm_g=sg_norm_g, sg_w=sg_w, sg_b=sg_b, w_out=w_out,
             ffn_norm_g=ffn_norm_g, w_up=w_up, ffn_conv_w=ffn_conv_w, ffn_conv_b=ffn_conv_b, w_down=w_down,
             final_norm_g=final_norm_g)
    Mo = dict(attn_norm_g=m_attn_norm_g, w_in=m_w_in, dn_conv_w=m_dn_conv_w, dn_a_log=m_dn_a_log,
              dn_dt_bias=m_dn_dt_bias, dn_out_norm_g=m_dn_out_norm_g, sg_norm_g=m_sg_norm_g, sg_w=m_sg_w,
              sg_b=m_sg_b, w_out=m_w_out, ffn_norm_g=m_ffn_norm_g, w_up=m_w_up, ffn_conv_w=m_ffn_conv_w,
              ffn_conv_b=m_ffn_conv_b, w_down=m_w_down, final_norm_g=m_final_norm_g)
    Vo = dict(attn_norm_g=v_attn_norm_g, w_in=v_w_in, dn_conv_w=v_dn_conv_w, dn_a_log=v_dn_a_log,
              dn_dt_bias=v_dn_dt_bias, dn_out_norm_g=v_dn_out_norm_g, sg_norm_g=v_sg_norm_g, sg_w=v_sg_w,
              sg_b=v_sg_b, w_out=v_w_out, ffn_norm_g=v_ffn_norm_g, w_up=v_w_up, ffn_conv_w=v_ffn_conv_w,
              ffn_conv_b=v_ffn_conv_b, w_down=v_w_down, final_norm_g=v_final_norm_g)
    xi, yi, ci = lax.axis_index("x"), lax.axis_index("y"), lax.axis_index("c")
    chip = 2 * xi + yi

    me_lin = 4 * xi + 2 * yi + ci

    g_in, g_dnc = _gather_first(w_in[0].astype(BF16), dn_conv_w[0])
    late = ("w_out", "w_up", "w_down", "ffn_conv_w")
    late_shards = [W[n][0].astype(BF16) for n in late[:3]] + [ffn_conv_w[0]]
    late_lands = [lax.dynamic_update_index_in_dim(lax.empty((4,) + s.shape, s.dtype), s, chip, 0) for s in late_shards]
    n_late = 3 * len(late_shards)
    ssem, rsem, late_src, late_lands, token = _transfer_start("gather_rest_start", late_shards, late_lands,
                                                              n_late, _gather_copies, after=g_in)

    def late_weights(after):
        _, (g_out, g_up, g_down, g_ffc) = _transfer_wait("gather_rest_wait", ssem, rsem, late_src, late_lands,
                                                         _gather_copies, after)
        return dict(w_out=g_out.reshape(D_MODEL, D_MODEL), w_up=g_up.transpose(1, 0, 2).reshape(D_MODEL, 2 * D_FF),
                    w_down=g_down.reshape(D_FF, D_MODEL), ffn_conv_w=g_ffc.transpose(1, 0, 2).reshape(3, 2 * D_FF))

    full = dict(
        w_in=g_in,
        dn_conv_w=g_dnc.transpose(1, 0, 2).reshape(4, 3 * DN_WIDTH),
        attn_norm_g=attn_norm_g, dn_a_log=dn_a_log, dn_dt_bias=dn_dt_bias, dn_out_norm_g=dn_out_norm_g,
        sg_norm_g=sg_norm_g, sg_w=sg_w[0], sg_b=sg_b[0], ffn_norm_g=ffn_norm_g, ffn_conv_b=ffn_conv_b,
        final_norm_g=final_norm_g[None])

    pending = {}
    early_names = ("dn_out_norm_g", "sg_norm_g", "sg_w", "sg_b", "ffn_norm_g", "ffn_conv_w", "ffn_conv_b",
                   "final_norm_g")
    late_names = ("attn_norm_g", "dn_a_log", "dn_dt_bias", "dn_conv_w")

    def on_grad(name, gw):
        if name == "small_early":
            buf = _pack([gw[n] for n in early_names])
            land = lax.dynamic_update_index_in_dim(lax.empty((8,) + buf.shape, F32), buf, me_lin, 0)
            s_sem, r_sem, src, lands, tok = _transfer_start("small_early_start", [buf], [land], 7, _small_copies)
            pending[name] = (s_sem, r_sem, src, lands)
            return tok
        g8 = gw.reshape(8, -1, gw.shape[-1])
        land = lax.empty((7,) + g8.shape[1:], BF16)
        s_sem, r_sem, src, lands, tok = _transfer_start(f"reduce_{name}_start", [g8], [land], 7, _pieces_copies)
        pending[name] = (s_sem, r_sem, src, lands)
        return tok

    loss_row, grad_x, g = _local_step(x[0], loss_target[0], full, dep=token, late_weights=late_weights,
                                      on_grad=on_grad)

    small_names = REPL + CONV
    late_all = _exchange_small(_pack([g[n] for n in late_names] + [loss_row]))
    late_sum = _sum_slots("sum_small_late", late_all[0], late_all[1:])
    s_sem, r_sem, src, lands = pending["small_early"]
    _, (early_all,) = _transfer_wait("small_early_wait", s_sem, r_sem, src, lands, _small_copies, grad_x)
    early_sum = _sum_slots("sum_small_early", early_all[0], early_all[1:])
    *late_vals, loss_sum = _unpack(late_sum, [g[n].shape for n in late_names] + [loss_row.shape])
    loss = loss_sum[0, 0]
    sg = dict(zip(late_names, late_vals))
    sg.update(zip(early_names, _unpack(early_sum, [g[n].shape for n in early_names])))
    sg["dn_conv_w"] = lax.dynamic_slice_in_dim(sg["dn_conv_w"], chip * (3 * DN_WIDTH // 4), 3 * DN_WIDTH // 4, axis=1)
    sg["ffn_conv_w"] = lax.dynamic_slice_in_dim(sg["ffn_conv_w"], chip * (2 * D_FF // 4), 2 * D_FF // 4, axis=1)

    halves = []
    for n in ("w_down", "w_up", "w_out", "w_in"):
        s_sem, r_sem, src, lands = pending[n]
        sent, got = _transfer_wait(f"reduce_{n}_wait", s_sem, r_sem, src, lands, _pieces_copies, grad_x)
        own = lax.dynamic_index_in_dim(sent[0], me_lin, axis=0, keepdims=False)
        halves.append(_sum_slots(f"sum_{n}", own, got[0]))
    theirs = _pair_swap(halves)
    core = ci.astype(jnp.int32).reshape(1)
    grads, delta, new_m, new_v = {}, {}, {}, {}
    for n, mine_h, their_h in zip(("w_down", "w_up", "w_out", "w_in"), halves, theirs):
        shp = W[n].shape
        if n == "w_in":
            to_t = lambda a: a.reshape(shp[1] // LANES, LANES, shp[2]).transpose(2, 0, 1)
            from_t = lambda a: a.transpose(1, 2, 0).reshape(shp)
            outs = _adamw_transposed(f"adamw_{n}", to_t(W[n]), mine_h, their_h, to_t(Mo[n]), to_t(Vo[n]), core)
            grads[n], delta[n], new_m[n], new_v[n] = (from_t(o) for o in outs)
            continue
        gr, d, mn, vn = _adamw_halves(f"adamw_{n}", W[n][0], mine_h, their_h, Mo[n][0], Vo[n][0], core)
        grads[n], delta[n], new_m[n], new_v[n] = gr.reshape(shp), d.reshape(shp), mn.reshape(shp), vn.reshape(shp)
    shapes = [W[n].shape for n in small_names]
    for n in small_names:
        grads[n] = sg[n].reshape(W[n].shape)
    d, mn, vn = _adamw("adamw_small", _pack([W[n] for n in small_names]), _pack([grads[n] for n in small_names]),
                       _pack([Mo[n] for n in small_names]), _pack([Vo[n] for n in small_names]))
    for dst, buf in ((delta, d), (new_m, mn), (new_v, vn)):
        dst.update(zip(small_names, _unpack(buf, shapes)))

    return (loss, grad_x[None], *[grads[n] for n in ORDER], *[delta[n] for n in ORDER],
            *[new_m[n] for n in ORDER], *[new_v[n] for n in ORDER])
```

```python
import functools
import math

import jax
import jax.numpy as jnp
from jax import lax
from jax.experimental import pallas as pl
from jax.experimental.pallas import tpu as pltpu

F32 = jnp.float32
BF16 = jnp.bfloat16

D_MODEL = 1024
CHUNK = 64
SCAN_CHUNKS = 4
SCAN_CHUNKS_FWD = 8
HEAD_DIM = 128
N_HEADS = 4
DN_WIDTH = 512
SG_WIDTH = 512
SG_GROUPS = 4
SG_BLOCK = 128
D_FF = 2816
PROJ_COLS = 3080
PROJ_PAD = 3200
BA_COL = 3072
EPS = 1e-6
NEG = -1e30
VMEM_LIMIT = 56 * 1024 * 1024

ADAM_LR = 0.001
ADAM_B1 = 0.9
ADAM_B2 = 0.999
ADAM_EPS = 1e-08
ADAM_WD = 0.01
ADAM_STEP = 10

MESH = pl.DeviceIdType.MESH
ANY = pl.BlockSpec(memory_space=pl.ANY)


def _cp(*sem):
    return pltpu.CompilerParams(dimension_semantics=sem, vmem_limit_bytes=VMEM_LIMIT)


def _bf(a):
    return a.astype(BF16)


def _nn(a, b):
    return jnp.dot(_bf(a), _bf(b), preferred_element_type=F32)


def _nt(a, b):
    return lax.dot_general(_bf(a), _bf(b), (((1,), (1,)), ((), ())), preferred_element_type=F32)


def _tn(a, b):
    return lax.dot_general(_bf(a), _bf(b), (((0,), (0,)), ((), ())), preferred_element_type=F32)


def _split(a):
    hi = _bf(a)
    return hi, _bf(a - hi.astype(F32))


def _sigmoid(x):
    return 0.5 * jnp.tanh(0.5 * x) + 0.5


def _silu(x):
    return x * _sigmoid(x)


def _dsilu(x):
    s = _sigmoid(x)
    return s * (1.0 + x * (1.0 - s))


_GELU_C = math.sqrt(2.0 / math.pi)
_GELU_A = 0.044715


def _gelu(x):
    return 0.5 * x * (1.0 + jnp.tanh(_GELU_C * (x + _GELU_A * x * x * x)))


def _dgelu(x):
    t = jnp.tanh(_GELU_C * (x + _GELU_A * x * x * x))
    return 0.5 * (1.0 + t) + 0.5 * x * (1.0 - t * t) * _GELU_C * (1.0 + 3.0 * _GELU_A * x * x)


def _softplus(x):
    return jnp.maximum(x, 0.0) + jnp.log(1.0 + jnp.exp(-jnp.abs(x)))


def _mm_nn(name, a, b, out_dtype, tm, tn, res=None):
    M, K = a.shape
    N = b.shape[1]
    tm, tn = min(tm, M), min(tn, N)

    def body(*refs):
        a_ref, b_ref = refs[0], refs[1]
        o_ref = refs[-1]
        acc = _nn(a_ref[...], b_ref[...])
        if res is not None:
            acc = acc + refs[2][...]
        o_ref[...] = acc.astype(o_ref.dtype)

    in_specs = [pl.BlockSpec((tm, K), lambda j, i: (i, 0)), pl.BlockSpec((K, tn), lambda j, i: (0, j))]
    args = [a, b]
    if res is not None:
        in_specs.append(pl.BlockSpec((tm, tn), lambda j, i: (i, j)))
        args.append(res)
    return pl.pallas_call(
        body, name=name, grid=(N // tn, M // tm), in_specs=in_specs,
        out_specs=pl.BlockSpec((tm, tn), lambda j, i: (i, j)),
        out_shape=jax.ShapeDtypeStruct((M, N), out_dtype),
        compiler_params=_cp("parallel", "parallel"))(*args)


def _with_dep(in_specs, args, dep):
    if dep is None:
        return in_specs, args
    return in_specs + [ANY], args + [dep]


SUB_ROWS = 128


def _sub_blocks(tm):
    return [slice(r0, min(r0 + SUB_ROWS, tm)) for r0 in range(0, tm, SUB_ROWS)]


def _rms_hat(xv):
    r = lax.rsqrt(jnp.mean(xv * xv, axis=-1, keepdims=True) + EPS)
    return xv * r, r


def _rms_bwd_vals(dh, xh, r, g):
    dxh = dh * g
    return r * (dxh - xh * jnp.mean(dxh * xh, axis=-1, keepdims=True)), jnp.sum(dh * xh, axis=0, keepdims=True)


def _in_proj(x, g, w4, tm=512, dep=None):
    T, K = x.shape
    ng, _, wc = w4.shape
    tm = min(tm, T)

    def body(x_ref, g_ref, w4_ref, *rest):
        p_ref, h_ref, w_ref = rest[-3:]

        @pl.when(pl.program_id(0) == 0)
        def _():
            w_ref[:, ng * wc:] = jnp.zeros((K, PROJ_PAD - ng * wc), BF16)
            for j in range(ng):
                w_ref[:, j * wc:(j + 1) * wc] = w4_ref[j]
        for r in _sub_blocks(tm):
            xh, _ = _rms_hat(x_ref[r, :])
            h_ref[r, :] = (xh * g_ref[...]).astype(BF16)
        p_ref[...] = jnp.dot(h_ref[...], w_ref[...], preferred_element_type=F32)

    in_specs, args = _with_dep(
        [pl.BlockSpec((tm, K), lambda i: (i, 0)), pl.BlockSpec((1, K), lambda i: (0, 0)),
         pl.BlockSpec((ng, K, wc), lambda i: (0, 0, 0))], [x, g, w4], dep)
    return pl.pallas_call(
        body, name="in_proj", grid=(T // tm,), in_specs=in_specs,
        out_specs=[pl.BlockSpec((tm, PROJ_PAD), lambda i: (i, 0)), pl.BlockSpec((tm, K), lambda i: (i, 0)),
                   pl.BlockSpec((K, PROJ_PAD), lambda i: (0, 0))],
        out_shape=[jax.ShapeDtypeStruct((T, PROJ_PAD), F32), jax.ShapeDtypeStruct((T, K), BF16),
                   jax.ShapeDtypeStruct((K, PROJ_PAD), BF16)],
        compiler_params=_cp("arbitrary"))(*args)


def _out_proj(mix, w, x, g, tm=512):
    T, K = mix.shape
    Dm = w.shape[1]
    tm = min(tm, T)

    def body(a_ref, w_ref, x_ref, g_ref, x2_ref, h_ref):
        x2_ref[...] = _nn(a_ref[...], w_ref[...]) + x_ref[...]
        for r in _sub_blocks(tm):
            xh, _ = _rms_hat(x2_ref[r, :])
            h_ref[r, :] = (xh * g_ref[...]).astype(BF16)

    row = lambda width: pl.BlockSpec((tm, width), lambda i: (i, 0))
    return pl.pallas_call(
        body, name="out_proj", grid=(T // tm,),
        in_specs=[row(K), pl.BlockSpec((K, Dm), lambda i: (0, 0)), row(Dm), pl.BlockSpec((1, Dm), lambda i: (0, 0))],
        out_specs=[row(Dm), row(Dm)],
        out_shape=[jax.ShapeDtypeStruct((T, Dm), F32), jax.ShapeDtypeStruct((T, Dm), BF16)],
        compiler_params=_cp("parallel"))(mix, w, x, g)


def _down_proj_loss(act, w, x2, tgt, g, tm=512):
    T, K = act.shape
    Dm = w.shape[1]
    tm = min(tm, T)

    def body(a_ref, w_ref, x_ref, t_ref, g_ref, loss_ref, dx_ref, gg_ref):
        @pl.when(pl.program_id(0) == 0)
        def _():
            gg_ref[...] = jnp.zeros_like(gg_ref)
            loss_ref[...] = jnp.zeros_like(loss_ref)
        dx_ref[...] = _nn(a_ref[...], w_ref[...]) + x_ref[...]
        for r in _sub_blocks(tm):
            xh, rr = _rms_hat(dx_ref[r, :])
            e = xh * g_ref[...] - t_ref[r, :]
            loss_ref[...] += jnp.zeros_like(loss_ref) + (0.5 / Dm) * jnp.sum(e * e)
            dx, gg = _rms_bwd_vals(e * (1.0 / Dm), xh, rr, g_ref[...])
            dx_ref[r, :] = dx
            gg_ref[...] += gg

    row = lambda width: pl.BlockSpec((tm, width), lambda i: (i, 0))
    vec = pl.BlockSpec((1, Dm), lambda i: (0, 0))
    return pl.pallas_call(
        body, name="down_proj_loss", grid=(T // tm,),
        in_specs=[row(K), pl.BlockSpec((K, Dm), lambda i: (0, 0)), row(Dm), row(Dm), vec],
        out_specs=[pl.BlockSpec((1, 128), lambda i: (0, 0)), row(Dm), vec],
        out_shape=[jax.ShapeDtypeStruct((1, 128), F32), jax.ShapeDtypeStruct((T, Dm), F32),
                   jax.ShapeDtypeStruct((1, Dm), F32)],
        compiler_params=_cp("arbitrary"))(act, w, x2, tgt, g)


def _mm_nt_rms_bwd(name, a, b, x, g, dres, tm=512, dep=None):
    M, K = a.shape
    Dm = b.shape[0]
    tm = min(tm, M)

    def body(a_ref, b_ref, x_ref, g_ref, dres_ref, *rest):
        dx_ref, gg_ref = rest[-2:]

        @pl.when(pl.program_id(0) == 0)
        def _():
            gg_ref[...] = jnp.zeros_like(gg_ref)
        dx_ref[...] = _nt(a_ref[...], b_ref[...])
        for r in _sub_blocks(tm):
            xh, rr = _rms_hat(x_ref[r, :])
            dx, gg = _rms_bwd_vals(dx_ref[r, :], xh, rr, g_ref[...])
            dx_ref[r, :] = dres_ref[r, :] + dx
            gg_ref[...] += gg

    row = lambda width: pl.BlockSpec((tm, width), lambda i: (i, 0))
    vec = pl.BlockSpec((1, Dm), lambda i: (0, 0))
    in_specs, args = _with_dep([row(K), pl.BlockSpec((Dm, K), lambda i: (0, 0)), row(Dm), vec, row(Dm)],
                               [a, b, x, g, dres], dep)
    return pl.pallas_call(
        body, name=name, grid=(M // tm,), in_specs=in_specs, out_specs=[row(Dm), vec],
        out_shape=[jax.ShapeDtypeStruct((M, Dm), F32), jax.ShapeDtypeStruct((1, Dm), F32)],
        compiler_params=_cp("arbitrary"))(*args)


def _mm_nt(name, a, b, out_dtype, tm, tn, dep=None):
    M, K = a.shape
    N = b.shape[0]
    tm, tn = min(tm, M), min(tn, N)

    def body(a_ref, b_ref, *rest):
        o_ref = rest[-1]
        o_ref[...] = _nt(a_ref[...], b_ref[...]).astype(o_ref.dtype)

    in_specs, args = _with_dep(
        [pl.BlockSpec((tm, K), lambda i, j: (i, 0)), pl.BlockSpec((tn, K), lambda i, j: (j, 0))], [a, b], dep)
    return pl.pallas_call(
        body, name=name, grid=(M // tm, N // tn), in_specs=in_specs,
        out_specs=pl.BlockSpec((tm, tn), lambda i, j: (i, j)),
        out_shape=jax.ShapeDtypeStruct((M, N), out_dtype),
        compiler_params=_cp("parallel", "parallel"))(*args)


def _mm_tn(name, a, b, tm, tn, tk, col_major_tiles=False, col_groups=None):
    T, M = a.shape
    N = b.shape[1]
    tm, tn, tk = min(tm, M), min(tn, N), min(tk, T)
    nk = T // tk

    def body(a_ref, b_ref, o_ref, acc_ref):
        k = pl.program_id(2)

        @pl.when(k == 0)
        def _():
            acc_ref[...] = jnp.zeros_like(acc_ref)
        acc_ref[...] += _tn(a_ref[...], b_ref[...])

        @pl.when(k == nk - 1)
        def _():
            if col_groups:
                for j in range(col_groups[0]):
                    o_ref[j] = acc_ref[:, j * col_groups[1]:(j + 1) * col_groups[1]].astype(BF16)
            else:
                o_ref[...] = acc_ref[...].astype(BF16).reshape(o_ref.shape)

    if col_groups:
        assert tm == M and tn == N and col_groups[0] * col_groups[1] <= N
        out_spec = pl.BlockSpec((col_groups[0], M, col_groups[1]), lambda i, j, k: (0, 0, 0))
        out_shape = jax.ShapeDtypeStruct((col_groups[0], M, col_groups[1]), BF16)
    elif col_major_tiles:
        assert tm == M
        out_spec = pl.BlockSpec((1, tm, tn), lambda i, j, k: (j, 0, 0))
        out_shape = jax.ShapeDtypeStruct((N // tn, M, tn), BF16)
    else:
        out_spec = pl.BlockSpec((tm, tn), lambda i, j, k: (i, j))
        out_shape = jax.ShapeDtypeStruct((M, N), BF16)
    return pl.pallas_call(
        body, name=name, grid=(M // tm, N // tn, nk),
        in_specs=[pl.BlockSpec((tk, tm), lambda i, j, k: (k, i)), pl.BlockSpec((tk, tn), lambda i, j, k: (k, j))],
        out_specs=out_spec, out_shape=out_shape, scratch_shapes=[pltpu.VMEM((tm, tn), F32)],
        compiler_params=_cp("parallel", "parallel", "arbitrary"))(a, b)


def _halo_prev_spec(rb, width):
    return pl.BlockSpec((8, width), lambda i: (jnp.maximum(i * (rb // 8) - 1, 0), 0))


def _halo_next_spec(rb, width, T):
    return pl.BlockSpec((8, width), lambda i: (jnp.minimum((i + 1) * (rb // 8), T // 8 - 1), 0))


LANES = 128
FF_STRIPS = D_FF // LANES
ROW_CHUNK = 32


def _strip(j, base=0):
    return pl.ds(pl.multiple_of(base + j * LANES, LANES), LANES)


def _ffn_act(up, w, b, rb=256):
    T, W = up.shape
    rb = min(rb, T)

    def body(up_ref, halo_ref, w_ref, b_ref, act_ref, ext_scr):
        first = pl.program_id(0) == 0

        def strip(j, slot):
            halves = (_strip(j), _strip(j, D_FF))
            wv = [w_ref[:, cols] for cols in halves]
            bv = [b_ref[:, cols] for cols in halves]
            for h, cols in enumerate(halves):
                ext_scr[slot, h,0:8] = jnp.where(first, 0.0, halo_ref[:, cols])
                ext_scr[slot, h,8:] = up_ref[:, cols]
            for r0 in range(0, rb, ROW_CHUNK):
                n = min(ROW_CHUNK, rb - r0)
                c = [ext_scr[slot, h,6 + r0:6 + r0 + n] * wv[h][0:1] + ext_scr[slot, h,7 + r0:7 + r0 + n] * wv[h][1:2]
                     + ext_scr[slot, h,8 + r0:8 + r0 + n] * wv[h][2:3] + bv[h] for h in range(2)]
                act_ref[r0:r0 + n, halves[0]] = (_silu(c[0]) * c[1]).astype(BF16)

        def pair(jj, carry):
            strip(2 * jj, 0)
            strip(2 * jj + 1, 1)
            return carry

        lax.fori_loop(0, FF_STRIPS // 2, pair, 0)

    return pl.pallas_call(
        body, name="ffn_act", grid=(T // rb,),
        in_specs=[pl.BlockSpec((rb, W), lambda i: (i, 0)), _halo_prev_spec(rb, W),
                  pl.BlockSpec((3, W), lambda i: (0, 0)), pl.BlockSpec((1, W), lambda i: (0, 0))],
        out_specs=pl.BlockSpec((rb, D_FF), lambda i: (i, 0)),
        out_shape=jax.ShapeDtypeStruct((T, D_FF), BF16),
        scratch_shapes=[pltpu.VMEM((2, 2, rb + 8, LANES), F32)], compiler_params=_cp("parallel"))(up, up, w, b)


def _ffn_act_bwd(up, dact, w, b, rb=128, dep=None):
    T, W = up.shape
    rb = min(rb, T)
    nb = T // rb
    re = rb + 8

    def body(up_ref, prev_ref, next_ref, da_ref, danext_ref, w_ref, b_ref, *rest):
        dup_ref, gw_ref, gb_ref, ext_scr, dc_scr = rest[-5:]
        i = pl.program_id(0)

        @pl.when(i == 0)
        def _():
            gw_ref[...] = jnp.zeros_like(gw_ref)
            gb_ref[...] = jnp.zeros_like(gb_ref)
        last = i == nb - 1

        def fold8(a):
            return jnp.sum(a.reshape(a.shape[0] // 8, 8, LANES), axis=0)

        def strip(j, slot):
            halves = (_strip(j), _strip(j, D_FF))
            wv = [w_ref[:, cols] for cols in halves]
            bv = [b_ref[:, cols] for cols in halves]
            for h, cols in enumerate(halves):
                ext_scr[slot, h,0:8] = jnp.where(i > 0, prev_ref[:, cols], 0.0)
                ext_scr[slot, h,8:8 + rb] = up_ref[:, cols]
                ext_scr[slot, h,8 + rb:] = next_ref[:, cols]
            gb = [jnp.zeros((8, LANES), F32) for _ in range(2)]
            gw = [[jnp.zeros((8, LANES), F32) for _ in range(3)] for _ in range(2)]
            for r0 in range(0, re, ROW_CHUNK):
                n = min(ROW_CHUNK, re - r0)
                tp = [[ext_scr[slot, h,6 + k + r0:6 + k + r0 + n] for k in range(3)] for h in range(2)]
                c = [tp[h][0] * wv[h][0:1] + tp[h][1] * wv[h][1:2] + tp[h][2] * wv[h][2:3] + bv[h] for h in range(2)]
                if r0 < rb:
                    da = da_ref[r0:r0 + n, halves[0]]
                else:
                    da = jnp.where(last, 0.0, danext_ref[:, halves[0]])
                s = _sigmoid(c[0])
                gs = c[0] * s
                dcs = (da * c[1] * (s + gs * (1.0 - s)), da * gs)
                for h in range(2):
                    dc_scr[slot, h,r0:r0 + n] = dcs[h]
                    if r0 < rb:
                        gb[h] = gb[h] + fold8(dcs[h])
                        for k in range(3):
                            gw[h][k] = gw[h][k] + fold8(tp[h][k] * dcs[h])
            for r0 in range(0, rb, ROW_CHUNK):
                n = min(ROW_CHUNK, rb - r0)
                for h, cols in enumerate(halves):
                    dup = (dc_scr[slot, h,r0:r0 + n] * wv[h][2:3] + dc_scr[slot, h,r0 + 1:r0 + 1 + n] * wv[h][1:2]
                           + dc_scr[slot, h,r0 + 2:r0 + 2 + n] * wv[h][0:1])
                    dup_ref[r0:r0 + n, cols] = dup.astype(BF16)
            for h, cols in enumerate(halves):
                gb_ref[:, cols] += jnp.sum(gb[h], axis=0, keepdims=True)
                for k in range(3):
                    gw_ref[k:k + 1, cols] += jnp.sum(gw[h][k], axis=0, keepdims=True)

        def pair(jj, carry):
            strip(2 * jj, 0)
            strip(2 * jj + 1, 1)
            return carry

        lax.fori_loop(0, FF_STRIPS // 2, pair, 0)

    in_specs, args = _with_dep(
        [pl.BlockSpec((rb, W), lambda i: (i, 0)), _halo_prev_spec(rb, W), _halo_next_spec(rb, W, T),
         pl.BlockSpec((rb, D_FF), lambda i: (i, 0)), _halo_next_spec(rb, D_FF, T),
         pl.BlockSpec((3, W), lambda i: (0, 0)), pl.BlockSpec((1, W), lambda i: (0, 0))],
        [up, up, up, dact, dact, w, b], dep)
    return pl.pallas_call(
        body, name="ffn_act_bwd", grid=(nb,), in_specs=in_specs,
        out_specs=[pl.BlockSpec((rb, W), lambda i: (i, 0)), pl.BlockSpec((3, W), lambda i: (0, 0)),
                   pl.BlockSpec((1, W), lambda i: (0, 0))],
        out_shape=[jax.ShapeDtypeStruct((T, W), BF16), jax.ShapeDtypeStruct((3, W), F32),
                   jax.ShapeDtypeStruct((1, W), F32)],
        scratch_shapes=[pltpu.VMEM((2, 2, rb + 16, LANES), F32), pltpu.VMEM((2, 2, re, LANES), F32)],
        compiler_params=_cp("arbitrary"))(*args)


def _lane_iota(shape):
    return lax.broadcasted_iota(jnp.int32, shape, len(shape) - 1)


def _dn_act(p, conv_w, alog_row, dtb_row, rb=256):
    T = p.shape[0]
    rb = min(rb, T)
    W3 = 3 * DN_WIDTH

    def body(p_ref, halo_ref, ba_ref, w_ref, al_ref, dt_ref, q_ref, k_ref, v_ref, bg_ref, ext_scr):
        first = pl.program_id(0) == 0
        outs = (q_ref, k_ref, v_ref)
        for j in range(3 * N_HEADS):
            kind, h = divmod(j, N_HEADS)
            cols = slice(j * HEAD_DIM, (j + 1) * HEAD_DIM)
            cur = p_ref[:, cols]
            ext_scr[j, 0:8] = jnp.where(first, 0.0, halo_ref[:, cols])
            ext_scr[j, 8:] = cur
            wv = w_ref[:, cols]
            s = _silu(ext_scr[j, 5:5 + rb] * wv[0:1] + ext_scr[j, 6:6 + rb] * wv[1:2]
                      + ext_scr[j, 7:7 + rb] * wv[2:3] + cur * wv[3:4])
            if kind < 2:
                scale = HEAD_DIM ** -0.5 if kind == 0 else 1.0
                s = s * (lax.rsqrt(jnp.sum(s * s, axis=-1, keepdims=True) + EPS) * scale)
            outs[kind][:, h * HEAD_DIM:(h + 1) * HEAD_DIM] = s
        ba = ba_ref[...]
        lane = _lane_iota(ba.shape)
        beta = _sigmoid(ba)
        g = -jnp.exp(al_ref[...]) * _softplus(ba + dt_ref[...])
        bg_ref[...] = jnp.where(lane < N_HEADS, beta, jnp.where(lane < 2 * N_HEADS, g, 0.0))

    row512 = pl.BlockSpec((rb, DN_WIDTH), lambda i: (i, 0))
    row128 = pl.BlockSpec((rb, 128), lambda i: (i, 0))
    vec128 = pl.BlockSpec((1, 128), lambda i: (0, 0))
    return pl.pallas_call(
        body, name="dn_act", grid=(T // rb,),
        in_specs=[pl.BlockSpec((rb, W3), lambda i: (i, 0)), _halo_prev_spec(rb, W3),
                  pl.BlockSpec((rb, 128), lambda i: (i, BA_COL // 128)),
                  pl.BlockSpec((4, W3), lambda i: (0, 0)), vec128, vec128],
        out_specs=[row512, row512, row512, row128],
        out_shape=[jax.ShapeDtypeStruct((T, DN_WIDTH), F32)] * 3 + [jax.ShapeDtypeStruct((T, 128), F32)],
        scratch_shapes=[pltpu.VMEM((3 * N_HEADS, rb + 8, HEAD_DIM), F32)],
        compiler_params=_cp("parallel"))(p, p, p, conv_w, alog_row, dtb_row)


def _dn_act_bwd(p, conv_w, alog_row, dtb_row, dq, dk, dv, dbg, dp_mid, rb=256):
    T = p.shape[0]
    rb = min(rb, T)
    nb = T // rb
    re = rb + 8
    W3 = 3 * DN_WIDTH

    def body(p_ref, prev_ref, next_ref, ba_ref, w_ref, al_ref, dt_ref, dq_ref, dqn_ref, dk_ref, dkn_ref,
             dv_ref, dvn_ref, dbg_ref, mid_ref, draw_ref, gw_ref, gad_ref, ext_scr, dc_scr):
        i = pl.program_id(0)
        draw_ref[:, W3:2 * W3] = mid_ref[...]

        @pl.when(i == 0)
        def _():
            gw_ref[...] = jnp.zeros_like(gw_ref)
            gad_ref[...] = jnp.zeros_like(gad_ref)
        row = lax.broadcasted_iota(jnp.int32, (re, 1), 0)
        live = (row < rb) | (i < nb - 1)
        d_refs = ((dq_ref, dqn_ref), (dk_ref, dkn_ref), (dv_ref, dvn_ref))
        for j in range(3 * N_HEADS):
            kind, h = divmod(j, N_HEADS)
            cols = slice(j * HEAD_DIM, (j + 1) * HEAD_DIM)
            hcols = slice(h * HEAD_DIM, (h + 1) * HEAD_DIM)
            ext_scr[j, 0:8] = jnp.where(i > 0, prev_ref[:, cols], 0.0)
            ext_scr[j, 8:8 + rb] = p_ref[:, cols]
            ext_scr[j, 8 + rb:] = next_ref[:, cols]
            tp = [ext_scr[j, 5 + k:5 + k + re] for k in range(4)]
            wv = w_ref[:, cols]
            c = tp[0] * wv[0:1] + tp[1] * wv[1:2] + tp[2] * wv[2:3] + tp[3] * wv[3:4]
            sg = _sigmoid(c)
            s = c * sg
            d_in = jnp.where(live, jnp.concatenate([d_refs[kind][0][:, hcols], d_refs[kind][1][:, hcols]], axis=0), 0.0)
            if kind < 2:
                scale = HEAD_DIM ** -0.5 if kind == 0 else 1.0
                n = lax.rsqrt(jnp.sum(s * s, axis=-1, keepdims=True) + EPS)
                hat = s * n
                d_in = (n * scale) * (d_in - hat * jnp.sum(hat * d_in, axis=-1, keepdims=True))
            dc = d_in * (sg + s * (1.0 - sg))
            dc_scr[j] = dc
            dcc = dc[0:rb]
            draw = (dcc * wv[3:4] + dc_scr[j, 1:1 + rb] * wv[2:3] + dc_scr[j, 2:2 + rb] * wv[1:2]
                    + dc_scr[j, 3:3 + rb] * wv[0:1])
            draw_ref[:, cols] = draw.astype(BF16)
            for k in range(4):
                gw_ref[k:k + 1, cols] += jnp.sum(tp[k][0:rb] * dcc, axis=0, keepdims=True)
        ba = ba_ref[...]
        dbg = dbg_ref[...]
        lane = _lane_iota(ba.shape)
        beta = _sigmoid(ba)
        ea = jnp.exp(al_ref[...])
        z = ba + dt_ref[...]
        d_a = dbg * (-ea) * _sigmoid(z)
        dba = jnp.where(lane < N_HEADS, dbg * beta * (1.0 - beta), jnp.where(lane < 2 * N_HEADS, d_a, 0.0))
        draw_ref[:, BA_COL:] = dba.astype(BF16)
        isg = (lane >= N_HEADS) & (lane < 2 * N_HEADS)
        g = -ea * _softplus(z)
        gad_ref[0:1, :] += jnp.sum(jnp.where(isg, dbg * g, 0.0), axis=0, keepdims=True)
        gad_ref[1:2, :] += jnp.sum(jnp.where(isg, d_a, 0.0), axis=0, keepdims=True)

    row512 = pl.BlockSpec((rb, DN_WIDTH), lambda i: (i, 0))
    row128 = pl.BlockSpec((rb, 128), lambda i: (i, 0))
    vec128 = pl.BlockSpec((1, 128), lambda i: (0, 0))
    next512 = _halo_next_spec(rb, DN_WIDTH, T)
    return pl.pallas_call(
        body, name="dn_act_bwd", grid=(nb,),
        in_specs=[pl.BlockSpec((rb, W3), lambda i: (i, 0)), _halo_prev_spec(rb, W3), _halo_next_spec(rb, W3, T),
                  pl.BlockSpec((rb, 128), lambda i: (i, BA_COL // 128)),
                  pl.BlockSpec((4, W3), lambda i: (0, 0)), vec128, vec128,
                  row512, next512, row512, next512, row512, next512, row128,
                  pl.BlockSpec((rb, W3), lambda i: (i, 0))],
        out_specs=[pl.BlockSpec((rb, PROJ_PAD), lambda i: (i, 0)),
                   pl.BlockSpec((4, W3), lambda i: (0, 0)), pl.BlockSpec((2, 128), lambda i: (0, 0))],
        out_shape=[jax.ShapeDtypeStruct((T, PROJ_PAD), BF16),
                   jax.ShapeDtypeStruct((4, W3), F32), jax.ShapeDtypeStruct((2, 128), F32)],
        scratch_shapes=[pltpu.VMEM((3 * N_HEADS, rb + 16, HEAD_DIM), F32), pltpu.VMEM((3 * N_HEADS, re, HEAD_DIM), F32)],
        compiler_params=_cp("arbitrary"))(p, p, p, p, conv_w, alog_row, dtb_row, dq, dq, dk, dk, dv, dv, dbg, dp_mid)


def _tri(incl):
    ii = lax.broadcasted_iota(jnp.int32, (CHUNK, CHUNK), 0)
    jj = lax.broadcasted_iota(jnp.int32, (CHUNK, CHUNK), 1)
    return ii, jj, ((ii >= jj) if incl else (ii > jj))


def _dn_chunk(k, bg, cb=4):
    T = k.shape[0]
    N = T // CHUNK
    cb = min(cb, N)

    def body(k_ref, bg_ref, gc_ref, gct_ref, l_ref):
        ii, jj, incl = _tri(True)
        tri = incl.astype(F32)
        U = range(cb)
        bgv = [bg_ref[u * CHUNK:(u + 1) * CHUNK, :] for u in U]
        gc = [jnp.dot(tri, bgv[u], precision=lax.Precision.HIGHEST, preferred_element_type=F32) for u in U]
        gct = [gc[u].T for u in U]
        kk = [[None] * N_HEADS for _ in U]
        for u in U:
            gc_ref[u * CHUNK:(u + 1) * CHUNK, :] = gc[u]
            gct_ref[u] = gct[u][0:8]
            for h in range(N_HEADS):
                kh = k_ref[u * CHUNK:(u + 1) * CHUNK, h * HEAD_DIM:(h + 1) * HEAD_DIM]
                kk[u][h] = _nt(kh * bgv[u][:, h:h + 1], kh)
        for u in U:
            for h in range(N_HEADS):
                gcol = gc[u][:, N_HEADS + h:N_HEADS + h + 1]
                grow = gct[u][N_HEADS + h:N_HEADS + h + 1, :]
                l_ref[u, h] = kk[u][h] * jnp.exp(jnp.where(ii > jj, gcol - grow, NEG))

    rows = cb * CHUNK
    return pl.pallas_call(
        body, name="dn_chunk", grid=(N // cb,),
        in_specs=[pl.BlockSpec((rows, DN_WIDTH), lambda n: (n, 0)), pl.BlockSpec((rows, 128), lambda n: (n, 0))],
        out_specs=[pl.BlockSpec((rows, 128), lambda n: (n, 0)), pl.BlockSpec((cb, 8, CHUNK), lambda n: (n, 0, 0)),
                   pl.BlockSpec((cb, N_HEADS, CHUNK, CHUNK), lambda n: (n, 0, 0, 0))],
        out_shape=[jax.ShapeDtypeStruct((T, 128), F32), jax.ShapeDtypeStruct((N, 8, CHUNK), F32),
                   jax.ShapeDtypeStruct((N, N_HEADS, CHUNK, CHUNK), F32)],
        compiler_params=_cp("parallel"))(k, bg)


def _tri_inv(lt):
    S = lt.shape[1]

    def body(l_ref, a_ref):
        sub = lax.broadcasted_iota(jnp.int32, (8, S), 0)
        groups = CHUNK // 8
        for i in range(CHUNK):
            acc = [((sub + 8 * k) == i).astype(F32) for k in range(groups)]
            for jb in range((i + 7) // 8):
                nk = jb + 1

                def step(j, carry, nk=nk, i=i):
                    lrow = l_ref[pl.ds(i * CHUNK + j, 1), :]
                    return tuple(carry[k] - lrow * a_ref[j, 8 * k:8 * k + 8, :] for k in range(nk))

                acc[:nk] = list(lax.fori_loop(8 * jb, min(8 * jb + 8, i), step, tuple(acc[:nk])))
            for k in range(groups):
                a_ref[i, 8 * k:8 * k + 8, :] = acc[k]

    return pl.pallas_call(
        body, name="tri_inv", out_shape=jax.ShapeDtypeStruct((CHUNK, CHUNK, S), F32),
        compiler_params=pltpu.CompilerParams(vmem_limit_bytes=VMEM_LIMIT))(lt)


def _dn_head_terms(qh, kh, vh, beta, gcol, grow):
    ii, jj, incl = _tri(True)
    gam = jnp.exp(jnp.where(incl, gcol - grow, NEG))
    glast = grow[:, CHUNK - 1:CHUNK]
    cd = jnp.exp(glast)
    shape = (CHUNK, HEAD_DIM)
    E = jnp.broadcast_to(jnp.exp(gcol), shape)
    Fd = jnp.broadcast_to(jnp.exp(glast - gcol), shape)
    beta = jnp.broadcast_to(beta, shape)
    kb = kh * beta
    return dict(ii=ii, jj=jj, gam=gam, E=E, F=Fd, beta=beta, cd=cd, kb=kb, vb=vh * beta, W=kb * E, qE=qh * E,
                kt=kh * Fd)


def _apply_a(a, u):
    hi, lo = _split(a)
    ub = _bf(u)
    return jnp.dot(hi, ub, preferred_element_type=F32) + jnp.dot(lo, ub, preferred_element_type=F32)


def _dn_scan(q, k, v, bg, gc, gct, a):
    T = q.shape[0]
    N = T // CHUNK
    cb = min(SCAN_CHUNKS_FWD, N)

    def body(q_ref, k_ref, v_ref, bg_ref, gc_ref, gct_ref, a_ref, o_ref, sall_ref, s_ref):
        @pl.when(pl.program_id(0) == 0)
        def _():
            s_ref[...] = jnp.zeros_like(s_ref)
        H = range(N_HEADS)
        sl = [slice(h * HEAD_DIM, (h + 1) * HEAD_DIM) for h in H]
        pre = []
        for u in range(cb):
            r = slice(u * CHUNK, (u + 1) * CHUNK)
            bgv, gcv, gctv = bg_ref[r, :], gc_ref[r, :], gct_ref[u]
            q_, k_ = [q_ref[r, s] for s in sl], [k_ref[r, s] for s in sl]
            t = [_dn_head_terms(q_[h], k_[h], v_ref[r, sl[h]], bgv[:, h:h + 1],
                                gcv[:, N_HEADS + h:N_HEADS + h + 1], gctv[N_HEADS + h:N_HEADS + h + 1, :]) for h in H]
            P = [_nt(q_[h], k_[h]) * t[h]["gam"] for h in H]
            pre.append((r, t, P))
        S = [s_ref[h] for h in H]
        for u in range(cb):
            r, t, P = pre[u]
            for h in H:
                sall_ref[u, h] = S[h]
            WS = [_nn(t[h]["W"], S[h]) for h in H]
            qS = [_nn(t[h]["qE"], S[h]) for h in H]
            vn = [_apply_a(a_ref[u, h], t[h]["vb"] - WS[h]) for h in H]
            Pv = [_nn(P[h], vn[h]) for h in H]
            kv = [_tn(t[h]["kt"], vn[h]) for h in H]
            for h in H:
                o_ref[r, sl[h]] = qS[h] + Pv[h]
            S = [t[h]["cd"] * S[h] + kv[h] for h in H]
        for h in H:
            s_ref[h] = S[h]

    row512 = pl.BlockSpec((cb * CHUNK, DN_WIDTH), lambda n: (n, 0))
    row128 = pl.BlockSpec((cb * CHUNK, 128), lambda n: (n, 0))
    return pl.pallas_call(
        body, name="dn_scan", grid=(N // cb,),
        in_specs=[row512, row512, row512, row128, row128, pl.BlockSpec((cb, 8, CHUNK), lambda n: (n, 0, 0)),
                  pl.BlockSpec((cb, N_HEADS, CHUNK, CHUNK), lambda n: (n, 0, 0, 0))],
        out_specs=[row512, pl.BlockSpec((cb, N_HEADS, HEAD_DIM, HEAD_DIM), lambda n: (n, 0, 0, 0))],
        out_shape=[jax.ShapeDtypeStruct((T, DN_WIDTH), F32),
                   jax.ShapeDtypeStruct((N, N_HEADS, HEAD_DIM, HEAD_DIM), F32)],
        scratch_shapes=[pltpu.VMEM((N_HEADS, HEAD_DIM, HEAD_DIM), F32)],
        compiler_params=_cp("arbitrary"))(q, k, v, bg, gc, gct, a)


def _dn_scan_bwd(q, k, v, bg, gc, gct, a, a_t, sall, do, dep=None):
    T = q.shape[0]
    N = T // CHUNK

    cb = min(SCAN_CHUNKS, N)
    nb = N // cb

    def body(q_ref, k_ref, v_ref, bg_ref, gc_ref, gct_ref, a_ref, at_ref, sall_ref, do_ref, *rest):
        dq_ref, dk_ref, dv_ref, dbg_ref, ds_ref = rest[-5:]
        @pl.when(pl.program_id(0) == 0)
        def _():
            ds_ref[...] = jnp.zeros_like(ds_ref)
        lane = _lane_iota((CHUNK, 128))
        rowi = lax.broadcasted_iota(jnp.int32, (CHUNK, 1), 0)
        ii, jj, _ = _tri(True)
        rev = (jj >= ii).astype(F32)
        H = range(N_HEADS)
        sl = [slice(h * HEAD_DIM, (h + 1) * HEAD_DIM) for h in H]
        pre = {}
        for u in reversed(range(cb)):
            r = slice(u * CHUNK, (u + 1) * CHUNK)
            bgv, gcv, gctv = bg_ref[r, :], gc_ref[r, :], gct_ref[u]
            q_, k_, v_ = [q_ref[r, s] for s in sl], [k_ref[r, s] for s in sl], [v_ref[r, s] for s in sl]
            dO = [do_ref[r, s] for s in sl]
            t = [_dn_head_terms(q_[h], k_[h], v_[h], bgv[:, h:h + 1], gcv[:, N_HEADS + h:N_HEADS + h + 1],
                                gctv[N_HEADS + h:N_HEADS + h + 1, :]) for h in H]
            beta = [t[h]["beta"] for h in H]
            S = [sall_ref[u, h] for h in H]
            A = [a_ref[u, h] for h in H]
            WS = [_nn(t[h]["W"], S[h]) for h in H]
            KK = [_nt(t[h]["kb"], k_[h]) for h in H]
            QK = [_nt(q_[h], k_[h]) for h in H]
            d_qE = [_nt(dO[h], S[h]) for h in H]
            vn = [_apply_a(A[h], t[h]["vb"] - WS[h]) for h in H]
            PtdO = [_tn(QK[h] * t[h]["gam"], dO[h]) for h in H]
            qEdO = [_tn(t[h]["qE"], dO[h]) for h in H]
            dOvn = [_nt(dO[h], vn[h]) for h in H]
            dQK = [jnp.where(ii >= jj, dOvn[h], 0.0) * t[h]["gam"] for h in H]
            dQKk = [_nn(dQK[h], k_[h]) for h in H]
            dQKq = [_tn(dQK[h], q_[h]) for h in H]
            pre[u] = (r, q_, k_, v_, beta, t, S, A, KK, QK, d_qE, vn, PtdO, qEdO, dQK, dQKk, dQKq)
        dSn = [ds_ref[h] for h in H]
        for u in reversed(range(cb)):
            r, q_, k_, v_, beta, t, S, A, KK, QK, d_qE, vn, PtdO, qEdO, dQK, dQKk, dQKq = pre[u]
            gam, E, Fd, cd, kb = ([t[h][n] for h in H] for n in ("gam", "E", "F", "cd", "kb"))
            ktdS = [_nn(t[h]["kt"], dSn[h]) for h in H]
            dU = [_apply_a(at_ref[u, h], PtdO[h] + ktdS[h]) for h in H]
            d_kt = [_nt(vn[h], dSn[h]) for h in H]
            dUvn = [_nt(dU[h], vn[h]) for h in H]
            dUS = [_nt(dU[h], S[h]) for h in H]
            WdU = [_tn(t[h]["W"], dU[h]) for h in H]
            d_cd = [jnp.sum(S[h] * dSn[h]) for h in H]
            dSn = [cd[h] * dSn[h] + qEdO[h] - WdU[h] for h in H]
            dKK = [jnp.where(ii > jj, -dUvn[h], 0.0) * gam[h] for h in H]
            dKKk = [_nn(dKK[h], k_[h]) for h in H]
            dKKkb = [_tn(dKK[h], kb[h]) for h in H]
            dbeta_arr = jnp.zeros((CHUNK, 128), F32)
            dgc_arr = jnp.zeros((CHUNK, 128), F32)
            for h in H:
                dW = -dUS[h]
                dq_ref[r, sl[h]] = dQKk[h] + d_qE[h] * E[h]
                d_kb = dKKk[h] + dW * E[h]
                dk_ref[r, sl[h]] = dQKq[h] + dKKkb[h] + d_kb * beta[h] + d_kt[h] * Fd[h]
                dv_ref[r, sl[h]] = dU[h] * beta[h]
                Z = dQK[h] * QK[h] + dKK[h] * KK[h]
                dbeta = jnp.sum(dU[h] * v_[h] + d_kb * k_[h], axis=-1, keepdims=True)
                m_e = (dW * kb[h] + d_qE[h] * q_[h]) * E[h]
                m_f = d_kt[h] * k_[h] * Fd[h]
                zdiag = jnp.where(ii == jj, jnp.sum(Z, axis=0, keepdims=True), 0.0)
                dgc = (jnp.sum(m_e - m_f, axis=-1, keepdims=True) + jnp.sum(Z - zdiag, axis=-1, keepdims=True)
                       + jnp.where(rowi == CHUNK - 1, jnp.sum(m_f) + d_cd[h] * cd[h], 0.0))
                dbeta_arr = dbeta_arr + jnp.where(lane == h, dbeta, 0.0)
                dgc_arr = dgc_arr + jnp.where(lane == N_HEADS + h, dgc, 0.0)
            dbg_ref[r, :] = dbeta_arr + jnp.dot(rev, dgc_arr, precision=lax.Precision.HIGHEST,
                                                preferred_element_type=F32)
        for h in H:
            ds_ref[h] = dSn[h]

    row512 = pl.BlockSpec((cb * CHUNK, DN_WIDTH), lambda n: (nb - 1 - n, 0))
    row128 = pl.BlockSpec((cb * CHUNK, 128), lambda n: (nb - 1 - n, 0))
    in_specs, args = _with_dep(
        [row512, row512, row512, row128, row128,
         pl.BlockSpec((cb, 8, CHUNK), lambda n: (nb - 1 - n, 0, 0)),
         pl.BlockSpec((cb, N_HEADS, CHUNK, CHUNK), lambda n: (nb - 1 - n, 0, 0, 0)),
         pl.BlockSpec((cb, N_HEADS, CHUNK, CHUNK), lambda n: (nb - 1 - n, 0, 0, 0)),
         pl.BlockSpec((cb, N_HEADS, HEAD_DIM, HEAD_DIM), lambda n: (nb - 1 - n, 0, 0, 0)), row512],
        [q, k, v, bg, gc, gct, a, a_t, sall, do], dep)
    return pl.pallas_call(
        body, name="dn_scan_bwd", grid=(nb,), in_specs=in_specs,
        out_specs=[row512, row512, row512, row128],
        out_shape=[jax.ShapeDtypeStruct((T, DN_WIDTH), F32)] * 3 + [jax.ShapeDtypeStruct((T, 128), F32)],
        scratch_shapes=[pltpu.VMEM((N_HEADS, HEAD_DIM, HEAD_DIM), F32)],
        compiler_params=_cp("arbitrary"))(*args)


def _sg_mask():
    ii = lax.broadcasted_iota(jnp.int32, (SG_BLOCK, SG_BLOCK), 0) // CHUNK
    jj = lax.broadcasted_iota(jnp.int32, (SG_BLOCK, SG_BLOCK), 1) // CHUNK
    return jj <= ii


def _mix_fwd(o, p, ong, sgn, sgw, sgbt):
    T = o.shape[0]
    rb = SG_BLOCK

    def body(o_ref, gate_ref, u_ref, vg_ref, ong_ref, sgn_ref, sgw_ref, sgbt_ref, mix_ref):
        mask = _sg_mask()
        gate = gate_ref[...]
        for h in range(N_HEADS):
            sl = slice(h * HEAD_DIM, (h + 1) * HEAD_DIM)
            oh = o_ref[:, sl]
            r = lax.rsqrt(jnp.mean(oh * oh, axis=-1, keepdims=True) + EPS)
            mix_ref[:, sl] = (oh * r * ong_ref[...] * _silu(gate[:, sl])).astype(BF16)
        for gi in range(SG_GROUPS):
            sl = slice(gi * SG_BLOCK, (gi + 1) * SG_BLOCK)
            gv = _gelu(vg_ref[:, sl])
            r = lax.rsqrt(jnp.mean(gv * gv, axis=-1, keepdims=True) + EPS)
            vh = gv * r * sgn_ref[:, sl]
            s = _nn(jnp.where(mask, sgw_ref[gi], 0.0), vh) + sgbt_ref[:, gi:gi + 1]
            mix_ref[:, DN_WIDTH + gi * SG_BLOCK:DN_WIDTH + (gi + 1) * SG_BLOCK] = (_gelu(u_ref[:, sl]) * s).astype(BF16)

    def col(c):
        return pl.BlockSpec((rb, 512), lambda i: (i, c))
    return pl.pallas_call(
        body, name="mix_fwd", grid=(T // rb,),
        in_specs=[pl.BlockSpec((rb, DN_WIDTH), lambda i: (i, 0)), col(3), col(4), col(5),
                  pl.BlockSpec((1, 128), lambda i: (0, 0)), pl.BlockSpec((1, SG_WIDTH), lambda i: (0, 0)),
                  pl.BlockSpec((SG_GROUPS, SG_BLOCK, SG_BLOCK), lambda i: (0, 0, 0)),
                  pl.BlockSpec((SG_BLOCK, 128), lambda i: (0, 0))],
        out_specs=pl.BlockSpec((rb, D_MODEL), lambda i: (i, 0)),
        out_shape=jax.ShapeDtypeStruct((T, D_MODEL), BF16),
        compiler_params=_cp("parallel"))(o, p, p, p, ong, sgn, sgw, sgbt)


def _mix_bwd(o, p, ong, sgn, sgw, sgbt, dmix, dep=None):
    T = o.shape[0]
    rb = SG_BLOCK

    def body(o_ref, gate_ref, u_ref, vg_ref, ong_ref, sgn_ref, sgw_ref, sgbt_ref, dmix_ref, *rest):
        do_ref, dp_ref, gong_ref, gsgn_ref, gsgw_ref, gsgbt_ref = rest[-6:]
        @pl.when(pl.program_id(0) == 0)
        def _():
            gong_ref[...] = jnp.zeros_like(gong_ref)
            gsgn_ref[...] = jnp.zeros_like(gsgn_ref)
            gsgw_ref[...] = jnp.zeros_like(gsgw_ref)
            gsgbt_ref[...] = jnp.zeros_like(gsgbt_ref)
        mask = _sg_mask()
        gate = gate_ref[...]
        lane = _lane_iota((SG_BLOCK, 128))
        for h in range(N_HEADS):
            sl = slice(h * HEAD_DIM, (h + 1) * HEAD_DIM)
            oh = o_ref[:, sl]
            dm = dmix_ref[:, sl]
            r = lax.rsqrt(jnp.mean(oh * oh, axis=-1, keepdims=True) + EPS)
            oh_hat = oh * r
            gt = gate[:, sl]
            sg = _silu(gt)
            dp_ref[:, sl] = (dm * oh_hat * ong_ref[...] * _dsilu(gt)).astype(BF16)
            dn_ = dm * sg
            gong_ref[...] += jnp.sum(dn_ * oh_hat, axis=0, keepdims=True)
            dhat = dn_ * ong_ref[...]
            do_ref[:, sl] = r * (dhat - oh_hat * jnp.mean(dhat * oh_hat, axis=-1, keepdims=True))
        for gi in range(SG_GROUPS):
            sl = slice(gi * SG_BLOCK, (gi + 1) * SG_BLOCK)
            vraw = vg_ref[:, sl]
            gv = _gelu(vraw)
            r = lax.rsqrt(jnp.mean(gv * gv, axis=-1, keepdims=True) + EPS)
            vhat = gv * r
            vn = vhat * sgn_ref[:, sl]
            wm = jnp.where(mask, sgw_ref[gi], 0.0)
            s = _nn(wm, vn) + sgbt_ref[:, gi:gi + 1]
            uraw = u_ref[:, sl]
            dm = dmix_ref[:, DN_WIDTH + gi * SG_BLOCK:DN_WIDTH + (gi + 1) * SG_BLOCK]
            dp_ref[:, DN_WIDTH + gi * SG_BLOCK:DN_WIDTH + (gi + 1) * SG_BLOCK] = (dm * s * _dgelu(uraw)).astype(BF16)
            ds = dm * _gelu(uraw)
            gsgbt_ref[...] += jnp.where(lane == gi, jnp.sum(ds, axis=-1, keepdims=True), 0.0)
            gsgw_ref[gi] += jnp.where(mask, _nt(ds, vn), 0.0)
            dvn = _tn(wm, ds)
            gsgn_ref[:, sl] += jnp.sum(dvn * vhat, axis=0, keepdims=True)
            dhat = dvn * sgn_ref[:, sl]
            dgv = r * (dhat - vhat * jnp.mean(dhat * vhat, axis=-1, keepdims=True))
            dp_ref[:, 2 * DN_WIDTH + gi * SG_BLOCK:2 * DN_WIDTH + (gi + 1) * SG_BLOCK] = (dgv * _dgelu(vraw)).astype(BF16)

    def col(c):
        return pl.BlockSpec((rb, 512), lambda i: (i, c))
    full = lambda *s: pl.BlockSpec(s, lambda i: (0,) * len(s))
    in_specs, args = _with_dep(
        [pl.BlockSpec((rb, DN_WIDTH), lambda i: (i, 0)), col(3), col(4), col(5),
         full(1, 128), full(1, SG_WIDTH), full(SG_GROUPS, SG_BLOCK, SG_BLOCK), full(SG_BLOCK, 128),
         pl.BlockSpec((rb, D_MODEL), lambda i: (i, 0))],
        [o, p, p, p, ong, sgn, sgw, sgbt, dmix], dep)
    return pl.pallas_call(
        body, name="mix_bwd", grid=(T // rb,), in_specs=in_specs,
        out_specs=[pl.BlockSpec((rb, DN_WIDTH), lambda i: (i, 0)), pl.BlockSpec((rb, 3 * 512), lambda i: (i, 0)),
                   full(1, 128), full(1, SG_WIDTH), full(SG_GROUPS, SG_BLOCK, SG_BLOCK), full(SG_BLOCK, 128)],
        out_shape=[jax.ShapeDtypeStruct((T, DN_WIDTH), F32), jax.ShapeDtypeStruct((T, 3 * 512), BF16),
                   jax.ShapeDtypeStruct((1, 128), F32), jax.ShapeDtypeStruct((1, SG_WIDTH), F32),
                   jax.ShapeDtypeStruct((SG_GROUPS, SG_BLOCK, SG_BLOCK), F32),
                   jax.ShapeDtypeStruct((SG_BLOCK, 128), F32)],
        compiler_params=_cp("arbitrary"))(*args)


def _pad_lanes(row, offset=0):
    n = row.shape[1]
    return jnp.pad(row, ((0, 0), (offset, 128 - n - offset)))


def _local_step(x, tgt, w, dep=None, late_weights=None, on_grad=None):
    T = x.shape[0]
    N = T // CHUNK
    on_grad = on_grad or (lambda name, g: None)
    alog_row = _pad_lanes(w["dn_a_log"], N_HEADS)
    dtb_row = _pad_lanes(w["dn_dt_bias"], N_HEADS)
    sgbt = jnp.pad(w["sg_b"].T, ((0, 0), (0, 128 - SG_GROUPS)))

    p, h1, w_in_pad = _in_proj(x, w["attn_norm_g"], w["w_in"], dep=dep)
    q, k, v, bg = _dn_act(p, w["dn_conv_w"], alog_row, dtb_row)
    gc, gct, lmat = _dn_chunk(k, bg)
    lt = lmat.reshape(N * N_HEADS, CHUNK * CHUNK).T
    at = _tri_inv(lt)
    a = at.reshape(CHUNK * CHUNK, N * N_HEADS).T.reshape(N, N_HEADS, CHUNK, CHUNK)
    a_t = at.transpose(1, 0, 2).reshape(CHUNK * CHUNK, N * N_HEADS).T.reshape(N, N_HEADS, CHUNK, CHUNK)
    o, sall = _dn_scan(q, k, v, bg, gc, gct, a)
    mix = _mix_fwd(o, p, w["dn_out_norm_g"], w["sg_norm_g"], w["sg_w"], sgbt)
    if late_weights is not None:
        w = {**w, **late_weights(mix)}
    x2, h2 = _out_proj(mix, w["w_out"], x, w["ffn_norm_g"])
    up = _mm_nn("up_proj", h2, w["w_up"], F32, 512, D_FF)
    act = _ffn_act(up, w["ffn_conv_w"], w["ffn_conv_b"])
    loss, dx3, g_final = _down_proj_loss(act, w["w_down"], x2, tgt, w["final_norm_g"])

    dact = _mm_nt("d_act", dx3, w["w_down"], F32, 512, D_FF)
    g_w_down = _mm_tn("g_w_down", act, dx3, D_FF, 1024, 1024)
    tok = on_grad("w_down", g_w_down)
    dup, g_ffn_conv_w, g_ffn_conv_b = _ffn_act_bwd(up, dact, w["ffn_conv_w"], w["ffn_conv_b"], dep=tok)
    g_w_up = _mm_tn("g_w_up", h2, dup, 1024, 2 * D_FF // 4, 2048, col_major_tiles=True)
    tok = on_grad("w_up", g_w_up)
    dx2, g_ffn_norm = _mm_nt_rms_bwd("d_h2", dup, w["w_up"], x2, w["ffn_norm_g"], dx3, dep=tok)
    dmix = _mm_nt("d_mix", dx2, w["w_out"], F32, 512, 1024)
    g_w_out = _mm_tn("g_w_out", mix, dx2, 1024, 1024, 1024)
    tok = on_grad("w_out", g_w_out)
    do, dp_mid, g_ong, g_sgn, g_sgw, g_sgbt = _mix_bwd(o, p, w["dn_out_norm_g"], w["sg_norm_g"], w["sg_w"], sgbt,
                                                      dmix, dep=tok)
    early = dict(dn_out_norm_g=g_ong, sg_norm_g=g_sgn, sg_w=g_sgw, sg_b=g_sgbt[:, :SG_GROUPS].T,
                 ffn_norm_g=g_ffn_norm, ffn_conv_w=g_ffn_conv_w, ffn_conv_b=g_ffn_conv_b, final_norm_g=g_final)
    tok = on_grad("small_early", early)
    dq, dk, dv, dbg = _dn_scan_bwd(q, k, v, bg, gc, gct, a, a_t, sall, do, dep=tok)
    dp, g_dn_conv_w, g_ad = _dn_act_bwd(p, w["dn_conv_w"], alog_row, dtb_row, dq, dk, dv, dbg, dp_mid)
    g_w_in = _mm_tn("g_w_in", h1, dp, 1024, PROJ_PAD, 1024, col_groups=(4, PROJ_COLS // 4))
    tok = on_grad("w_in", g_w_in)
    grad_x, g_attn_norm = _mm_nt_rms_bwd("d_h1", dp, w_in_pad, x, w["attn_norm_g"], dx2, dep=tok)

    grads = dict(
        attn_norm_g=g_attn_norm, w_in=g_w_in, dn_conv_w=g_dn_conv_w,
        dn_a_log=g_ad[0:1, N_HEADS:2 * N_HEADS], dn_dt_bias=g_ad[1:2, N_HEADS:2 * N_HEADS],
        w_out=g_w_out, w_up=g_w_up, w_down=g_w_down, **early)
    return loss, grad_x, grads


def _me():
    return lax.axis_index("x"), lax.axis_index("y"), lax.axis_index("c")


def _peer(rel):
    x, y, c = _me()
    return {"x": (1 - x, y, c), "y": (x, 1 - y, c), "xy": (1 - x, 1 - y, c), "c": (x, y, 1 - c)}[rel]


def _chip_of(dev):
    return 2 * dev[0] + dev[1]


CHIP_RELS = ("x", "y", "xy")


def _run_copies(copies, sends, recvs):
    for cp in copies:
        cp.start()
    for cp in recvs:
        cp.wait_recv()
    for cp in sends:
        cp.wait_send()


def _gather_first(w_shard, small_shard):
    R = w_shard.shape[0]
    r2 = R // 2

    def body(w_ref, s_ref, w_out, s_out, send_sems, recv_sems):
        x, y, c = _me()
        me = _chip_of((x, y))
        sib = _peer("c")

        def half(chip, core):
            return w_out.at[chip, pl.ds(pl.multiple_of(core * r2, 8), r2), :]

        def copy(k, src, dst, to):
            return pltpu.make_async_remote_copy(src_ref=src, dst_ref=dst, send_sem=send_sems.at[k],
                                                recv_sem=recv_sems.at[k], device_id=to, device_id_type=MESH)

        own_rows = w_ref.at[pl.ds(pl.multiple_of(c * r2, 8), r2), :]
        first = [copy(r, own_rows, half(me, c), _peer(rel)) for r, rel in enumerate(CHIP_RELS)]
        first += [copy(3 + r, s_ref, s_out.at[me], _peer(rel)) for r, rel in enumerate(CHIP_RELS)]
        for cp in first:
            cp.start()
        passed = []
        for r, rel in enumerate(CHIP_RELS):
            their = _chip_of(_peer(rel))
            copy(r, own_rows, half(their, c), _peer(rel)).wait_recv()
            fwd = copy(6 + r, half(their, c), half(their, c), sib)
            fwd.start()
            passed.append(fwd)
        for r, rel in enumerate(CHIP_RELS):
            their = _chip_of(_peer(rel))
            copy(3 + r, s_ref, s_out.at[their], _peer(rel)).wait_recv()
            copy(6 + r, own_rows, half(their, 1 - c), sib).wait_recv()
        for cp in first + passed:
            cp.wait_send()

    w_all, s_all = pl.pallas_call(
        body, name="gather_first", in_specs=[ANY, ANY], out_specs=[ANY, ANY],
        out_shape=[jax.ShapeDtypeStruct((4,) + w_shard.shape, w_shard.dtype),
                   jax.ShapeDtypeStruct((4,) + small_shard.shape, small_shard.dtype)],
        scratch_shapes=[pltpu.SemaphoreType.DMA((9,)), pltpu.SemaphoreType.DMA((9,))])(w_shard, small_shard)
    me = _chip_of(_me())
    return (lax.dynamic_update_index_in_dim(w_all, w_shard, me, 0),
            lax.dynamic_update_index_in_dim(s_all, small_shard, me, 0))


OTHERS = tuple((fx, fy, fc) for fx in (0, 1) for fy in (0, 1) for fc in (0, 1) if (fx, fy, fc) != (0, 0, 0))


def _other(flip):
    x, y, c = _me()
    return (x ^ flip[0], y ^ flip[1], c ^ flip[2])


def _linear(dev):
    return 4 * dev[0] + 2 * dev[1] + dev[2]


def _exchange_small(small):
    def body(small_ref, out_ref, send_sems, recv_sems):
        my_slot = _linear(_me())
        sends, recvs = [], []
        for k, flip in enumerate(OTHERS):
            peer = _other(flip)
            sends.append(pltpu.make_async_remote_copy(
                src_ref=small_ref, dst_ref=out_ref.at[my_slot], send_sem=send_sems.at[k], recv_sem=recv_sems.at[k],
                device_id=peer, device_id_type=MESH))
            recvs.append(pltpu.make_async_remote_copy(
                src_ref=small_ref, dst_ref=out_ref.at[_linear(peer)], send_sem=send_sems.at[k],
                recv_sem=recv_sems.at[k], device_id=peer, device_id_type=MESH))
        _run_copies(sends, sends, recvs)

    out = pl.pallas_call(
        body, name="exchange_small", in_specs=[ANY], out_specs=ANY,
        out_shape=jax.ShapeDtypeStruct((8,) + small.shape, small.dtype),
        scratch_shapes=[pltpu.SemaphoreType.DMA((7,)), pltpu.SemaphoreType.DMA((7,))])(small)
    return lax.dynamic_update_index_in_dim(out, small, _linear(_me()), 0)


def _pair_swap(halves):
    n = len(halves)

    def body(*refs):
        src, out = refs[:n], refs[n:2 * n]
        send_sems, recv_sems = refs[2 * n:]
        sib = _peer("c")
        copies = [pltpu.make_async_remote_copy(
            src_ref=src[i], dst_ref=out[i], send_sem=send_sems.at[i], recv_sem=recv_sems.at[i],
            device_id=sib, device_id_type=MESH) for i in range(n)]
        _run_copies(copies, copies, copies)

    return pl.pallas_call(
        body, name="pair_swap", in_specs=[ANY] * n, out_specs=[ANY] * n,
        out_shape=[jax.ShapeDtypeStruct(h.shape, h.dtype) for h in halves],
        scratch_shapes=[pltpu.SemaphoreType.DMA((n,)), pltpu.SemaphoreType.DMA((n,))])(*halves)


HBM = pl.BlockSpec(memory_space=pltpu.HBM)
SEM = pl.BlockSpec(memory_space=pltpu.SEMAPHORE)
EFFECT = pltpu.SideEffectType.DATAFLOW_SIDE_EFFECTING


def _hbm(a):
    return pltpu.with_memory_space_constraint(a, pltpu.HBM)


def _transfer_start(name, srcs, lands, n_copies, make_copies, after=None):
    n, m = len(srcs), len(lands)

    def body(*refs):
        src, land = refs[:n], refs[n:n + m]
        outs = refs[n + m + (after is not None):]
        send_sems, recv_sems, token = outs[0], outs[1], outs[-1]
        for cp in make_copies(src, land, send_sems, recv_sems):
            cp.start()
        token[...] = jnp.zeros_like(token)

    arrs = list(srcs) + list(lands)
    in_specs, args = _with_dep([HBM] * (n + m), [_hbm(a) for a in arrs], after)
    out = pl.pallas_call(
        body, name=name,
        out_shape=(pltpu.SemaphoreType.DMA((n_copies,)), pltpu.SemaphoreType.DMA((n_copies,)),
                   *[pltpu.HBM(a.shape, a.dtype) for a in arrs], jax.ShapeDtypeStruct((8, 128), F32)),
        in_specs=in_specs,
        out_specs=(SEM, SEM, *[HBM] * (n + m), pl.BlockSpec(memory_space=pltpu.VMEM)),
        input_output_aliases={i: 2 + i for i in range(n + m)},
        compiler_params=pltpu.CompilerParams(has_side_effects=EFFECT))(*args)
    return out[0], out[1], list(out[2:2 + n]), list(out[2 + n:2 + n + m]), out[-1]


def _transfer_wait(name, send_sems, recv_sems, srcs, lands, make_copies, after):
    n, m = len(srcs), len(lands)

    def body(*refs):
        src, land = refs[:n], refs[n:n + m]
        s_sems, r_sems = refs[n + m], refs[n + m + 1]
        for cp in make_copies(src, land, s_sems, r_sems):
            cp.wait_send()
            cp.wait_recv()

    arrs = list(srcs) + list(lands)
    out = pl.pallas_call(
        body, name=name, out_shape=tuple(pltpu.HBM(a.shape, a.dtype) for a in arrs),
        in_specs=[HBM] * (n + m) + [SEM, SEM, ANY], out_specs=tuple([HBM] * (n + m)),
        input_output_aliases={i: i for i in range(n + m)},
        compiler_params=pltpu.CompilerParams(has_side_effects=EFFECT))(*arrs, send_sems, recv_sems, after)
    return list(out[:n]), list(out[n:])


def _gather_copies(src, land, send_sems, recv_sems):
    me = _chip_of(_me())
    copies = []
    for i in range(len(src)):
        for r, rel in enumerate(CHIP_RELS):
            k = 3 * i + r
            copies.append(pltpu.make_async_remote_copy(
                src_ref=src[i], dst_ref=land[i].at[me], send_sem=send_sems.at[k], recv_sem=recv_sems.at[k],
                device_id=_peer(rel), device_id_type=MESH))
    return copies


def _small_copies(src, land, send_sems, recv_sems):
    my_slot = _linear(_me())
    return [pltpu.make_async_remote_copy(
        src_ref=src[0], dst_ref=land[0].at[my_slot], send_sem=send_sems.at[k], recv_sem=recv_sems.at[k],
        device_id=_other(flip), device_id_type=MESH) for k, flip in enumerate(OTHERS)]


def _pieces_copies(src, land, send_sems, recv_sems):
    copies = []
    for k, flip in enumerate(OTHERS):
        peer = _other(flip)
        copies.append(pltpu.make_async_remote_copy(
            src_ref=src[0].at[_linear(peer)], dst_ref=land[0].at[k], send_sem=send_sems.at[k],
            recv_sem=recv_sems.at[k], device_id=peer, device_id_type=MESH))
    return copies


def _row_block(rows, cols, budget=2 * 1024 * 1024):
    rb = max(8, (budget // (4 * cols)) // 8 * 8)
    while rows % rb:
        rb -= 8
    return rb if rb > 0 else rows


def _sum_slots(name, first, rest):
    R, Cc = first.shape
    K = rest.shape[0]
    rb = _row_block(R, Cc)

    def body(f_ref, r_ref, o_ref):
        acc = f_ref[...].astype(F32)
        for j in range(K):
            acc = acc + r_ref[j].astype(F32)
        o_ref[...] = acc

    return pl.pallas_call(
        body, name=name, grid=(R // rb,),
        in_specs=[pl.BlockSpec((rb, Cc), lambda i: (i, 0)), pl.BlockSpec((K, rb, Cc), lambda i: (0, i, 0))],
        out_specs=pl.BlockSpec((rb, Cc), lambda i: (i, 0)),
        out_shape=jax.ShapeDtypeStruct((R, Cc), F32), compiler_params=_cp("parallel"))(first, rest)


def _adamw_math(w, gv, m, v):
    mn = ADAM_B1 * m + (1.0 - ADAM_B1) * gv
    vn = ADAM_B2 * v + (1.0 - ADAM_B2) * (gv * gv)
    m_hat = mn / (1.0 - ADAM_B1 ** ADAM_STEP)
    v_hat = vn / (1.0 - ADAM_B2 ** ADAM_STEP)
    return -ADAM_LR * (m_hat / (jnp.sqrt(v_hat) + ADAM_EPS) + ADAM_WD * w), mn, vn


def _adamw_halves(name, w, mine, theirs, m, v, core):
    R, Cc = w.shape
    r2 = R // 2
    rb = _row_block(r2, Cc, 1024 * 1024)
    nb2 = r2 // rb

    def body(c_ref, w_ref, mine_ref, theirs_ref, m_ref, v_ref, g_ref, d_ref, mo_ref, vo_ref):
        is_mine = (pl.program_id(0) // nb2) == c_ref[0]
        gv = jnp.where(is_mine, mine_ref[...], theirs_ref[...])
        g_ref[...] = gv
        d_ref[...], mo_ref[...], vo_ref[...] = _adamw_math(w_ref[...], gv, m_ref[...], v_ref[...])

    blk = pl.BlockSpec((rb, Cc), lambda i, c: (i, 0))
    half = lambda own: pl.BlockSpec(
        (rb, Cc), lambda i, c: (jnp.clip(i - (c[0] if own else 1 - c[0]) * nb2, 0, nb2 - 1), 0))
    return pl.pallas_call(
        body, name=name,
        grid_spec=pltpu.PrefetchScalarGridSpec(
            num_scalar_prefetch=1, grid=(2 * nb2,), in_specs=[blk, half(True), half(False), blk, blk],
            out_specs=[blk] * 4),
        out_shape=[jax.ShapeDtypeStruct((R, Cc), F32)] * 4, compiler_params=_cp("parallel"))(core, w, mine, theirs, m, v)


def _adamw_transposed(name, wt, mine, theirs, mt, vt, core):
    Cc, kh_n, _ = wt.shape
    r2 = mine.shape[0]
    per_half = kh_n // 2
    nb = -(-Cc // LANES)

    def body(c_ref, w_ref, mine_ref, theirs_ref, m_ref, v_ref, g_ref, d_ref, mo_ref, vo_ref):
        first = c_ref[0] == 0
        halves = (jnp.where(first, mine_ref[...], theirs_ref[...]).T,
                  jnp.where(first, theirs_ref[...], mine_ref[...]).T)
        for kh in range(kh_n):
            lo = (kh % per_half) * LANES
            g_ref[:, kh, :] = halves[kh // per_half][:, lo:lo + LANES]
        d_ref[...], mo_ref[...], vo_ref[...] = _adamw_math(w_ref[...], g_ref[...], m_ref[...], v_ref[...])

    blk = pl.BlockSpec((LANES, kh_n, LANES), lambda i, c: (i, 0, 0))
    half = pl.BlockSpec((r2, LANES), lambda i, c: (0, i))
    return pl.pallas_call(
        body, name=name,
        grid_spec=pltpu.PrefetchScalarGridSpec(
            num_scalar_prefetch=1, grid=(nb,), in_specs=[blk, half, half, blk, blk], out_specs=[blk] * 4),
        out_shape=[jax.ShapeDtypeStruct(wt.shape, F32)] * 4, compiler_params=_cp("parallel"))(
            core, wt, mine, theirs, mt, vt)


def _adamw(name, w, g, m, v):
    R, Cc = w.shape
    rb = _row_block(R, Cc, 1024 * 1024)

    def body(w_ref, g_ref, m_ref, v_ref, d_ref, mo_ref, vo_ref):
        d_ref[...], mo_ref[...], vo_ref[...] = _adamw_math(w_ref[...], g_ref[...], m_ref[...], v_ref[...])

    blk = pl.BlockSpec((rb, Cc), lambda i: (i, 0))
    return pl.pallas_call(
        body, name=name, grid=(R // rb,), in_specs=[blk] * 4, out_specs=[blk] * 3,
        out_shape=[jax.ShapeDtypeStruct((R, Cc), F32)] * 3, compiler_params=_cp("parallel"))(w, g, m, v)


def _pack(arrs):
    rows = []
    for a in arrs:
        flat = a.reshape(-1)
        pad = (-flat.shape[0]) % 128
        rows.append(jnp.pad(flat, (0, pad)).reshape(-1, 128))
    buf = jnp.concatenate(rows, axis=0)
    return jnp.pad(buf, ((0, (-buf.shape[0]) % 8), (0, 0)))


def _unpack(buf, shapes):
    out, r = [], 0
    for s in shapes:
        n = math.prod(s)
        nr = -(-n // 128)
        out.append(buf[r:r + nr].reshape(-1)[:n].reshape(s))
        r += nr
    return out


BIG = ("w_in", "w_out", "w_up", "w_down")
CONV = ("dn_conv_w", "ffn_conv_w")
REPL = ("attn_norm_g", "dn_a_log", "dn_dt_bias", "dn_out_norm_g", "sg_norm_g", "sg_w", "sg_b",
        "ffn_norm_g", "ffn_conv_b", "final_norm_g")
ORDER = ("attn_norm_g", "w_in", "dn_conv_w", "dn_a_log", "dn_dt_bias", "dn_out_norm_g", "sg_norm_g", "sg_w",
         "sg_b", "w_out", "ffn_norm_g", "w_up", "ffn_conv_w", "ffn_conv_b", "w_down", "final_norm_g")


def kernel(x, attn_norm_g, w_in, dn_conv_w, dn_a_log, dn_dt_bias, dn_out_norm_g, sg_norm_g, sg_w, sg_b, w_out, ffn_norm_g, w_up, ffn_conv_w, ffn_conv_b, w_down, final_norm_g, loss_target, m_attn_norm_g, m_w_in, m_dn_conv_w, m_dn_a_log, m_dn_dt_bias, m_dn_out_norm_g, m_sg_norm_g, m_sg_w, m_sg_b, m_w_out, m_ffn_norm_g, m_w_up, m_ffn_conv_w, m_ffn_conv_b, m_w_down, m_final_norm_g, v_attn_norm_g, v_w_in, v_dn_conv_w, v_dn_a_log, v_dn_dt_bias, v_dn_out_norm_g, v_sg_norm_g, v_sg_w, v_sg_b, v_w_out, v_ffn_norm_g, v_w_up, v_ffn_conv_w, v_ffn_conv_b, v_w_down, v_final_norm_g):
    W = dict(attn_norm_g=attn_norm_g, w_in=w_in, dn_conv_w=dn_conv_w, dn_a_log=dn_a_log, dn_dt_bias=dn_dt_bias,
             dn_out_norm_g=dn_out_norm_g, sg_norm_g=sg_norm_g, sg_w=sg_w, sg_b=sg_b, w_out=w_out,
             ffn_norm_g=ffn_norm_g, w_up=w_up, ffn_conv_w=ffn_conv_w, ffn_conv_b=ffn_conv_b, w_down=w_down,
             final_norm_g=final_norm_g)
    Mo = dict(attn_norm_g=m_attn_norm_g, w_in=m_w_in, dn_conv_w=m_dn_conv_w, dn_a_log=m_dn_a_log,
              dn_dt_bias=m_dn_dt_bias, dn_out_norm_g=m_dn_out_norm_g, sg_norm_g=m_sg_norm_g, sg_w=m_sg_w,
              sg_b=m_sg_b, w_out=m_w_out, ffn_norm_g=m_ffn_norm_g, w_up=m_w_up, ffn_conv_w=m_ffn_conv_w,
              ffn_conv_b=m_ffn_conv_b, w_down=m_w_down, final_norm_g=m_final_norm_g)
    Vo = dict(attn_norm_g=v_attn_norm_g, w_in=v_w_in, dn_conv_w=v_dn_conv_w, dn_a_log=v_dn_a_log,
              dn_dt_bias=v_dn_dt_bias, dn_out_norm_g=v_dn_out_norm_g, sg_norm_g=v_sg_norm_g, sg_w=v_sg_w,
              sg_b=v_sg_b, w_out=v_w_out, ffn_norm_g=v_ffn_norm_g, w_up=v_w_up, ffn_conv_w=v_ffn_conv_w,
              ffn_conv_b=v_ffn_conv_b, w_down=v_w_down, final_norm_g=v_final_norm_g)
    xi, yi, ci = lax.axis_index("x"), lax.axis_index("y"), lax.axis_index("c")
    chip = 2 * xi + yi

    me_lin = 4 * xi + 2 * yi + ci

    g_in, g_dnc = _gather_first(w_in[0].astype(BF16), dn_conv_w[0])
    late = ("w_out", "w_up", "w_down", "ffn_conv_w")
    late_shards = [W[n][0].astype(BF16) for n in late[:3]] + [ffn_conv_w[0]]
    late_lands = [lax.dynamic_update_index_in_dim(lax.empty((4,) + s.shape, s.dtype), s, chip, 0) for s in late_shards]
    n_late = 3 * len(late_shards)
    ssem, rsem, late_src, late_lands, token = _transfer_start("gather_rest_start", late_shards, late_lands,
                                                              n_late, _gather_copies, after=g_in)

    def late_weights(after):
        _, (g_out, g_up, g_down, g_ffc) = _transfer_wait("gather_rest_wait", ssem, rsem, late_src, late_lands,
                                                         _gather_copies, after)
        return dict(w_out=g_out.reshape(D_MODEL, D_MODEL), w_up=g_up.transpose(1, 0, 2).reshape(D_MODEL, 2 * D_FF),
                    w_down=g_down.reshape(D_FF, D_MODEL), ffn_conv_w=g_ffc.transpose(1, 0, 2).reshape(3, 2 * D_FF))

    full = dict(
        w_in=g_in,
        dn_conv_w=g_dnc.transpose(1, 0, 2).reshape(4, 3 * DN_WIDTH),
        attn_norm_g=attn_norm_g, dn_a_log=dn_a_log, dn_dt_bias=dn_dt_bias, dn_out_norm_g=dn_out_norm_g,
        sg_norm_g=sg_norm_g, sg_w=sg_w[0], sg_b=sg_b[0], ffn_norm_g=ffn_norm_g, ffn_conv_b=ffn_conv_b,
        final_norm_g=final_norm_g[None])

    pending = {}
    early_names = ("dn_out_norm_g", "sg_norm_g", "sg_w", "sg_b", "ffn_norm_g", "ffn_conv_w", "ffn_conv_b",
                   "final_norm_g")
    late_names = ("attn_norm_g", "dn_a_log", "dn_dt_bias", "dn_conv_w")

    def on_grad(name, gw):
        if name == "small_early":
            buf = _pack([gw[n] for n in early_names])
            land = lax.dynamic_update_index_in_dim(lax.empty((8,) + buf.shape, F32), buf, me_lin, 0)
            s_sem, r_sem, src, lands, tok = _transfer_start("small_early_start", [buf], [land], 7, _small_copies)
            pending[name] = (s_sem, r_sem, src, lands)
            return tok
        g8 = gw.reshape(8, -1, gw.shape[-1])
        land = lax.empty((7,) + g8.shape[1:], BF16)
        s_sem, r_sem, src, lands, tok = _transfer_start(f"reduce_{name}_start", [g8], [land], 7, _pieces_copies)
        pending[name] = (s_sem, r_sem, src, lands)
        return tok

    loss_row, grad_x, g = _local_step(x[0], loss_target[0], full, dep=token, late_weights=late_weights,
                                      on_grad=on_grad)

    small_names = REPL + CONV
    late_all = _exchange_small(_pack([g[n] for n in late_names] + [loss_row]))
    late_sum = _sum_slots("sum_small_late", late_all[0], late_all[1:])
    s_sem, r_sem, src, lands = pending["small_early"]
    _, (early_all,) = _transfer_wait("small_early_wait", s_sem, r_sem, src, lands, _small_copies, grad_x)
    early_sum = _sum_slots("sum_small_early", early_all[0], early_all[1:])
    *late_vals, loss_sum = _unpack(late_sum, [g[n].shape for n in late_names] + [loss_row.shape])
    loss = loss_sum[0, 0]
    sg = dict(zip(late_names, late_vals))
    sg.update(zip(early_names, _unpack(early_sum, [g[n].shape for n in early_names])))
    sg["dn_conv_w"] = lax.dynamic_slice_in_dim(sg["dn_conv_w"], chip * (3 * DN_WIDTH // 4), 3 * DN_WIDTH // 4, axis=1)
    sg["ffn_conv_w"] = lax.dynamic_slice_in_dim(sg["ffn_conv_w"], chip * (2 * D_FF // 4), 2 * D_FF // 4, axis=1)

    halves = []
    for n in ("w_down", "w_up", "w_out", "w_in"):
        s_sem, r_sem, src, lands = pending[n]
        sent, got = _transfer_wait(f"reduce_{n}_wait", s_sem, r_sem, src, lands, _pieces_copies, grad_x)
        own = lax.dynamic_index_in_dim(sent[0], me_lin, axis=0, keepdims=False)
        halves.append(_sum_slots(f"sum_{n}", own, got[0]))
    theirs = _pair_swap(halves)
    core = ci.astype(jnp.int32).reshape(1)
    grads, delta, new_m, new_v = {}, {}, {}, {}
    for n, mine_h, their_h in zip(("w_down", "w_up", "w_out", "w_in"), halves, theirs):
        shp = W[n].shape
        if n == "w_in":
            to_t = lambda a: a.reshape(shp[1] // LANES, LANES, shp[2]).transpose(2, 0, 1)
            from_t = lambda a: a.transpose(1, 2, 0).reshape(shp)
            outs = _adamw_transposed(f"adamw_{n}", to_t(W[n]), mine_h, their_h, to_t(Mo[n]), to_t(Vo[n]), core)
            grads[n], delta[n], new_m[n], new_v[n] = (from_t(o) for o in outs)
            continue
        gr, d, mn, vn = _adamw_halves(f"adamw_{n}", W[n][0], mine_h, their_h, Mo[n][0], Vo[n][0], core)
        grads[n], delta[n], new_m[n], new_v[n] = gr.reshape(shp), d.reshape(shp), mn.reshape(shp), vn.reshape(shp)
    shapes = [W[n].shape for n in small_names]
    for n in small_names:
        grads[n] = sg[n].reshape(W[n].shape)
    d, mn, vn = _adamw("adamw_small", _pack([W[n] for n in small_names]), _pack([grads[n] for n in small_names]),
                       _pack([Mo[n] for n in small_names]), _pack([Vo[n] for n in small_names]))
    for dst, buf in ((delta, d), (new_m, mn), (new_v, vn)):
        dst.update(zip(small_names, _unpack(buf, shapes)))

    return (loss, grad_x[None], *[grads[n] for n in ORDER], *[delta[n] for n in ORDER],
            *[new_m[n] for n in ORDER], *[new_v[n] for n in ORDER])
```

```python
import functools
import math

import jax
import jax.numpy as jnp
from jax import lax
from jax.experimental import pallas as pl
from jax.experimental.pallas import tpu as pltpu

F32 = jnp.float32
BF16 = jnp.bfloat16

D_MODEL = 1024
CHUNK = 64
SCAN_CHUNKS = 4
SCAN_CHUNKS_FWD = 8
HEAD_DIM = 128
N_HEADS = 4
DN_WIDTH = 512
SG_WIDTH = 512
SG_GROUPS = 4
SG_BLOCK = 128
D_FF = 2816
PROJ_COLS = 3080
PROJ_PAD = 3200
BA_COL = 3072
EPS = 1e-6
NEG = -1e30
VMEM_LIMIT = 56 * 1024 * 1024

ADAM_LR = 0.001
ADAM_B1 = 0.9
ADAM_B2 = 0.999
ADAM_EPS = 1e-08
ADAM_WD = 0.01
ADAM_STEP = 10

MESH = pl.DeviceIdType.MESH
ANY = pl.BlockSpec(memory_space=pl.ANY)


def _cp(*sem):
    return pltpu.CompilerParams(dimension_semantics=sem, vmem_limit_bytes=VMEM_LIMIT)


def _bf(a):
    return a.astype(BF16)


def _nn(a, b):
    return jnp.dot(_bf(a), _bf(b), preferred_element_type=F32)


def _nt(a, b):
    return lax.dot_general(_bf(a), _bf(b), (((1,), (1,)), ((), ())), preferred_element_type=F32)


def _tn(a, b):
    return lax.dot_general(_bf(a), _bf(b), (((0,), (0,)), ((), ())), preferred_element_type=F32)


def _split(a):
    hi = _bf(a)
    return hi, _bf(a - hi.astype(F32))


def _sigmoid(x):
    return 0.5 * jnp.tanh(0.5 * x) + 0.5


def _silu(x):
    return x * _sigmoid(x)


def _dsilu(x):
    s = _sigmoid(x)
    return s * (1.0 + x * (1.0 - s))


_GELU_C = math.sqrt(2.0 / math.pi)
_GELU_A = 0.044715


def _gelu(x):
    return 0.5 * x * (1.0 + jnp.tanh(_GELU_C * (x + _GELU_A * x * x * x)))


def _dgelu(x):
    t = jnp.tanh(_GELU_C * (x + _GELU_A * x * x * x))
    return 0.5 * (1.0 + t) + 0.5 * x * (1.0 - t * t) * _GELU_C * (1.0 + 3.0 * _GELU_A * x * x)


def _softplus(x):
    return jnp.maximum(x, 0.0) + jnp.log(1.0 + jnp.exp(-jnp.abs(x)))


def _mm_nn(name, a, b, out_dtype, tm, tn, res=None):
    M, K = a.shape
    N = b.shape[1]
    tm, tn = min(tm, M), min(tn, N)

    def body(*refs):
        a_ref, b_ref = refs[0], refs[1]
        o_ref = refs[-1]
        acc = _nn(a_ref[...], b_ref[...])
        if res is not None:
            acc = acc + refs[2][...]
        o_ref[...] = acc.astype(o_ref.dtype)

    in_specs = [pl.BlockSpec((tm, K), lambda j, i: (i, 0)), pl.BlockSpec((K, tn), lambda j, i: (0, j))]
    args = [a, b]
    if res is not None:
        in_specs.append(pl.BlockSpec((tm, tn), lambda j, i: (i, j)))
        args.append(res)
    return pl.pallas_call(
        body, name=name, grid=(N // tn, M // tm), in_specs=in_specs,
        out_specs=pl.BlockSpec((tm, tn), lambda j, i: (i, j)),
        out_shape=jax.ShapeDtypeStruct((M, N), out_dtype),
        compiler_params=_cp("parallel", "parallel"))(*args)


def _with_dep(in_specs, args, dep):
    if dep is None:
        return in_specs, args
    return in_specs + [ANY], args + [dep]


SUB_ROWS = 128


def _sub_blocks(tm):
    return [slice(r0, min(r0 + SUB_ROWS, tm)) for r0 in range(0, tm, SUB_ROWS)]


def _rms_hat(xv):
    r = lax.rsqrt(jnp.mean(xv * xv, axis=-1, keepdims=True) + EPS)
    return xv * r, r


def _rms_bwd_vals(dh, xh, r, g):
    dxh = dh * g
    return r * (dxh - xh * jnp.mean(dxh * xh, axis=-1, keepdims=True)), jnp.sum(dh * xh, axis=0, keepdims=True)


def _in_proj(x, g, w4, tm=512, dep=None):
    T, K = x.shape
    ng, _, wc = w4.shape
    tm = min(tm, T)

    def body(x_ref, g_ref, w4_ref, *rest):
        p_ref, h_ref, w_ref = rest[-3:]

        @pl.when(pl.program_id(0) == 0)
        def _():
            w_ref[:, ng * wc:] = jnp.zeros((K, PROJ_PAD - ng * wc), BF16)
            for j in range(ng):
                w_ref[:, j * wc:(j + 1) * wc] = w4_ref[j]
        for r in _sub_blocks(tm):
            xh, _ = _rms_hat(x_ref[r, :])
            h_ref[r, :] = (xh * g_ref[...]).astype(BF16)
        p_ref[...] = jnp.dot(h_ref[...], w_ref[...], preferred_element_type=F32)

    in_specs, args = _with_dep(
        [pl.BlockSpec((tm, K), lambda i: (i, 0)), pl.BlockSpec((1, K), lambda i: (0, 0)),
         pl.BlockSpec((ng, K, wc), lambda i: (0, 0, 0))], [x, g, w4], dep)
    return pl.pallas_call(
        body, name="in_proj", grid=(T // tm,), in_specs=in_specs,
        out_specs=[pl.BlockSpec((tm, PROJ_PAD), lambda i: (i, 0)), pl.BlockSpec((tm, K), lambda i: (i, 0)),
                   pl.BlockSpec((K, PROJ_PAD), lambda i: (0, 0))],
        out_shape=[jax.ShapeDtypeStruct((T, PROJ_PAD), F32), jax.ShapeDtypeStruct((T, K), BF16),
                   jax.ShapeDtypeStruct((K, PROJ_PAD), BF16)],
        compiler_params=_cp("arbitrary"))(*args)


def _out_proj(mix, w, x, g, tm=512):
    T, K = mix.shape
    Dm = w.shape[1]
    tm = min(tm, T)

    def body(a_ref, w_ref, x_ref, g_ref, x2_ref, h_ref):
        x2_ref[...] = _nn(a_ref[...], w_ref[...]) + x_ref[...]
        for r in _sub_blocks(tm):
            xh, _ = _rms_hat(x2_ref[r, :])
            h_ref[r, :] = (xh * g_ref[...]).astype(BF16)

    row = lambda width: pl.BlockSpec((tm, width), lambda i: (i, 0))
    return pl.pallas_call(
        body, name="out_proj", grid=(T // tm,),
        in_specs=[row(K), pl.BlockSpec((K, Dm), lambda i: (0, 0)), row(Dm), pl.BlockSpec((1, Dm), lambda i: (0, 0))],
        out_specs=[row(Dm), row(Dm)],
        out_shape=[jax.ShapeDtypeStruct((T, Dm), F32), jax.ShapeDtypeStruct((T, Dm), BF16)],
        compiler_params=_cp("parallel"))(mix, w, x, g)


def _down_proj_loss(act, w, x2, tgt, g, tm=512):
    T, K = act.shape
    Dm = w.shape[1]
    tm = min(tm, T)

    def body(a_ref, w_ref, x_ref, t_ref, g_ref, loss_ref, dx_ref, gg_ref):
        @pl.when(pl.program_id(0) == 0)
        def _():
            gg_ref[...] = jnp.zeros_like(gg_ref)
            loss_ref[...] = jnp.zeros_like(loss_ref)
        dx_ref[...] = _nn(a_ref[...], w_ref[...]) + x_ref[...]
        for r in _sub_blocks(tm):
            xh, rr = _rms_hat(dx_ref[r, :])
            e = xh * g_ref[...] - t_ref[r, :]
            loss_ref[...] += jnp.zeros_like(loss_ref) + (0.5 / Dm) * jnp.sum(e * e)
            dx, gg = _rms_bwd_vals(e * (1.0 / Dm), xh, rr, g_ref[...])
            dx_ref[r, :] = dx
            gg_ref[...] += gg

    row = lambda width: pl.BlockSpec((tm, width), lambda i: (i, 0))
    vec = pl.BlockSpec((1, Dm), lambda i: (0, 0))
    return pl.pallas_call(
        body, name="down_proj_loss", grid=(T // tm,),
        in_specs=[row(K), pl.BlockSpec((K, Dm), lambda i: (0, 0)), row(Dm), row(Dm), vec],
        out_specs=[pl.BlockSpec((1, 128), lambda i: (0, 0)), row(Dm), vec],
        out_shape=[jax.ShapeDtypeStruct((1, 128), F32), jax.ShapeDtypeStruct((T, Dm), F32),
                   jax.ShapeDtypeStruct((1, Dm), F32)],
        compiler_params=_cp("arbitrary"))(act, w, x2, tgt, g)


def _mm_nt_rms_bwd(name, a, b, x, g, dres, tm=512, dep=None):
    M, K = a.shape
    Dm = b.shape[0]
    tm = min(tm, M)

    def body(a_ref, b_ref, x_ref, g_ref, dres_ref, *rest):
        dx_ref, gg_ref = rest[-2:]

        @pl.when(pl.program_id(0) == 0)
        def _():
            gg_ref[...] = jnp.zeros_like(gg_ref)
        dx_ref[...] = _nt(a_ref[...], b_ref[...])
        for r in _sub_blocks(tm):
            xh, rr = _rms_hat(x_ref[r, :])
            dx, gg = _rms_bwd_vals(dx_ref[r, :], xh, rr, g_ref[...])
            dx_ref[r, :] = dres_ref[r, :] + dx
            gg_ref[...] += gg

    row = lambda width: pl.BlockSpec((tm, width), lambda i: (i, 0))
    vec = pl.BlockSpec((1, Dm), lambda i: (0, 0))
    in_specs, args = _with_dep([row(K), pl.BlockSpec((Dm, K), lambda i: (0, 0)), row(Dm), vec, row(Dm)],
                               [a, b, x, g, dres], dep)
    return pl.pallas_call(
        body, name=name, grid=(M // tm,), in_specs=in_specs, out_specs=[row(Dm), vec],
        out_shape=[jax.ShapeDtypeStruct((M, Dm), F32), jax.ShapeDtypeStruct((1, Dm), F32)],
        compiler_params=_cp("arbitrary"))(*args)


def _mm_nt(name, a, b, out_dtype, tm, tn, dep=None):
    M, K = a.shape
    N = b.shape[0]
    tm, tn = min(tm, M), min(tn, N)

    def body(a_ref, b_ref, *rest):
        o_ref = rest[-1]
        o_ref[...] = _nt(a_ref[...], b_ref[...]).astype(o_ref.dtype)

    in_specs, args = _with_dep(
        [pl.BlockSpec((tm, K), lambda i, j: (i, 0)), pl.BlockSpec((tn, K), lambda i, j: (j, 0))], [a, b], dep)
    return pl.pallas_call(
        body, name=name, grid=(M // tm, N // tn), in_specs=in_specs,
        out_specs=pl.BlockSpec((tm, tn), lambda i, j: (i, j)),
        out_shape=jax.ShapeDtypeStruct((M, N), out_dtype),
        compiler_params=_cp("parallel", "parallel"))(*args)


def _mm_tn(name, a, b, tm, tn, tk, col_major_tiles=False, col_groups=None):
    T, M = a.shape
    N = b.shape[1]
    tm, tn, tk = min(tm, M), min(tn, N), min(tk, T)
    nk = T // tk

    def body(a_ref, b_ref, o_ref, acc_ref):
        k = pl.program_id(2)

        @pl.when(k == 0)
        def _():
            acc_ref[...] = jnp.zeros_like(acc_ref)
        acc_ref[...] += _tn(a_ref[...], b_ref[...])

        @pl.when(k == nk - 1)
        def _():
            if col_groups:
                for j in range(col_groups[0]):
                    o_ref[j] = acc_ref[:, j * col_groups[1]:(j + 1) * col_groups[1]].astype(BF16)
            else:
                o_ref[...] = acc_ref[...].astype(BF16).reshape(o_ref.shape)

    if col_groups:
        assert tm == M and tn == N and col_groups[0] * col_groups[1] <= N
        out_spec = pl.BlockSpec((col_groups[0], M, col_groups[1]), lambda i, j, k: (0, 0, 0))
        out_shape = jax.ShapeDtypeStruct((col_groups[0], M, col_groups[1]), BF16)
    elif col_major_tiles:
        assert tm == M
        out_spec = pl.BlockSpec((1, tm, tn), lambda i, j, k: (j, 0, 0))
        out_shape = jax.ShapeDtypeStruct((N // tn, M, tn), BF16)
    else:
        out_spec = pl.BlockSpec((tm, tn), lambda i, j, k: (i, j))
        out_shape = jax.ShapeDtypeStruct((M, N), BF16)
    return pl.pallas_call(
        body, name=name, grid=(M // tm, N // tn, nk),
        in_specs=[pl.BlockSpec((tk, tm), lambda i, j, k: (k, i)), pl.BlockSpec((tk, tn), lambda i, j, k: (k, j))],
        out_specs=out_spec, out_shape=out_shape, scratch_shapes=[pltpu.VMEM((tm, tn), F32)],
        compiler_params=_cp("parallel", "parallel", "arbitrary"))(a, b)


def _halo_prev_spec(rb, width):
    return pl.BlockSpec((8, width), lambda i: (jnp.maximum(i * (rb // 8) - 1, 0), 0))


def _halo_next_spec(rb, width, T):
    return pl.BlockSpec((8, width), lambda i: (jnp.minimum((i + 1) * (rb // 8), T // 8 - 1), 0))


LANES = 128
FF_STRIPS = D_FF // LANES
ROW_CHUNK = 32


def _strip(j, base=0):
    return pl.ds(pl.multiple_of(base + j * LANES, LANES), LANES)


def _ffn_act(up, w, b, rb=256):
    T, W = up.shape
    rb = min(rb, T)

    def body(up_ref, halo_ref, w_ref, b_ref, act_ref, ext_scr):
        first = pl.program_id(0) == 0

        def strip(j, slot):
            halves = (_strip(j), _strip(j, D_FF))
            wv = [w_ref[:, cols] for cols in halves]
            bv = [b_ref[:, cols] for cols in halves]
            for h, cols in enumerate(halves):
                ext_scr[slot, h,0:8] = jnp.where(first, 0.0, halo_ref[:, cols])
                ext_scr[slot, h,8:] = up_ref[:, cols]
            for r0 in range(0, rb, ROW_CHUNK):
                n = min(ROW_CHUNK, rb - r0)
                c = [ext_scr[slot, h,6 + r0:6 + r0 + n] * wv[h][0:1] + ext_scr[slot, h,7 + r0:7 + r0 + n] * wv[h][1:2]
                     + ext_scr[slot, h,8 + r0:8 + r0 + n] * wv[h][2:3] + bv[h] for h in range(2)]
                act_ref[r0:r0 + n, halves[0]] = (_silu(c[0]) * c[1]).astype(BF16)

        def pair(jj, carry):
            strip(2 * jj, 0)
            strip(2 * jj + 1, 1)
            return carry

        lax.fori_loop(0, FF_STRIPS // 2, pair, 0)

    return pl.pallas_call(
        body, name="ffn_act", grid=(T // rb,),
        in_specs=[pl.BlockSpec((rb, W), lambda i: (i, 0)), _halo_prev_spec(rb, W),
                  pl.BlockSpec((3, W), lambda i: (0, 0)), pl.BlockSpec((1, W), lambda i: (0, 0))],
        out_specs=pl.BlockSpec((rb, D_FF), lambda i: (i, 0)),
        out_shape=jax.ShapeDtypeStruct((T, D_FF), BF16),
        scratch_shapes=[pltpu.VMEM((2, 2, rb + 8, LANES), F32)], compiler_params=_cp("parallel"))(up, up, w, b)


def _ffn_act_bwd(up, dact, w, b, rb=128, dep=None):
    T, W = up.shape
    rb = min(rb, T)
    nb = T // rb
    re = rb + 8

    def body(up_ref, prev_ref, next_ref, da_ref, danext_ref, w_ref, b_ref, *rest):
        dup_ref, gw_ref, gb_ref, ext_scr, dc_scr = rest[-5:]
        i = pl.program_id(0)

        @pl.when(i == 0)
        def _():
            gw_ref[...] = jnp.zeros_like(gw_ref)
            gb_ref[...] = jnp.zeros_like(gb_ref)
        last = i == nb - 1

        def fold8(a):
            return jnp.sum(a.reshape(a.shape[0] // 8, 8, LANES), axis=0)

        def strip(j, slot):
            halves = (_strip(j), _strip(j, D_FF))
            wv = [w_ref[:, cols] for cols in halves]
            bv = [b_ref[:, cols] for cols in halves]
            for h, cols in enumerate(halves):
                ext_scr[slot, h,0:8] = jnp.where(i > 0, prev_ref[:, cols], 0.0)
                ext_scr[slot, h,8:8 + rb] = up_ref[:, cols]
                ext_scr[slot, h,8 + rb:] = next_ref[:, cols]
            gb = [jnp.zeros((8, LANES), F32) for _ in range(2)]
            gw = [[jnp.zeros((8, LANES), F32) for _ in range(3)] for _ in range(2)]
            for r0 in range(0, re, ROW_CHUNK):
                n = min(ROW_CHUNK, re - r0)
                tp = [[ext_scr[slot, h,6 + k + r0:6 + k + r0 + n] for k in range(3)] for h in range(2)]
                c = [tp[h][0] * wv[h][0:1] + tp[h][1] * wv[h][1:2] + tp[h][2] * wv[h][2:3] + bv[h] for h in range(2)]
                if r0 < rb:
                    da = da_ref[r0:r0 + n, halves[0]]
                else:
                    da = jnp.where(last, 0.0, danext_ref[:, halves[0]])
                s = _sigmoid(c[0])
                gs = c[0] * s
                dcs = (da * c[1] * (s + gs * (1.0 - s)), da * gs)
                for h in range(2):
                    dc_scr[slot, h,r0:r0 + n] = dcs[h]
                    if r0 < rb:
                        gb[h] = gb[h] + fold8(dcs[h])
                        for k in range(3):
                            gw[h][k] = gw[h][k] + fold8(tp[h][k] * dcs[h])
            for r0 in range(0, rb, ROW_CHUNK):
                n = min(ROW_CHUNK, rb - r0)
                for h, cols in enumerate(halves):
                    dup = (dc_scr[slot, h,r0:r0 + n] * wv[h][2:3] + dc_scr[slot, h,r0 + 1:r0 + 1 + n] * wv[h][1:2]
                           + dc_scr[slot, h,r0 + 2:r0 + 2 + n] * wv[h][0:1])
                    dup_ref[r0:r0 + n, cols] = dup.astype(BF16)
            for h, cols in enumerate(halves):
                gb_ref[:, cols] += jnp.sum(gb[h], axis=0, keepdims=True)
                for k in range(3):
                    gw_ref[k:k + 1, cols] += jnp.sum(gw[h][k], axis=0, keepdims=True)

        def pair(jj, carry):
            strip(2 * jj, 0)
            strip(2 * jj + 1, 1)
            return carry

        lax.fori_loop(0, FF_STRIPS // 2, pair, 0)

    in_specs, args = _with_dep(
        [pl.BlockSpec((rb, W), lambda i: (i, 0)), _halo_prev_spec(rb, W), _halo_next_spec(rb, W, T),
         pl.BlockSpec((rb, D_FF), lambda i: (i, 0)), _halo_next_spec(rb, D_FF, T),
         pl.BlockSpec((3, W), lambda i: (0, 0)), pl.BlockSpec((1, W), lambda i: (0, 0))],
        [up, up, up, dact, dact, w, b], dep)
    return pl.pallas_call(
        body, name="ffn_act_bwd", grid=(nb,), in_specs=in_specs,
        out_specs=[pl.BlockSpec((rb, W), lambda i: (i, 0)), pl.BlockSpec((3, W), lambda i: (0, 0)),
                   pl.BlockSpec((1, W), lambda i: (0, 0))],
        out_shape=[jax.ShapeDtypeStruct((T, W), BF16), jax.ShapeDtypeStruct((3, W), F32),
                   jax.ShapeDtypeStruct((1, W), F32)],
        scratch_shapes=[pltpu.VMEM((2, 2, rb + 16, LANES), F32), pltpu.VMEM((2, 2, re, LANES), F32)],
        compiler_params=_cp("arbitrary"))(*args)


def _lane_iota(shape):
    return lax.broadcasted_iota(jnp.int32, shape, len(shape) - 1)


def _dn_act(p, conv_w, alog_row, dtb_row, rb=256):
    T = p.shape[0]
    rb = min(rb, T)
    W3 = 3 * DN_WIDTH

    def body(p_ref, halo_ref, ba_ref, w_ref, al_ref, dt_ref, q_ref, k_ref, v_ref, bg_ref, ext_scr):
        first = pl.program_id(0) == 0
        outs = (q_ref, k_ref, v_ref)
        for j in range(3 * N_HEADS):
            kind, h = divmod(j, N_HEADS)
            cols = slice(j * HEAD_DIM, (j + 1) * HEAD_DIM)
            cur = p_ref[:, cols]
            ext_scr[j, 0:8] = jnp.where(first, 0.0, halo_ref[:, cols])
            ext_scr[j, 8:] = cur
            wv = w_ref[:, cols]
            s = _silu(ext_scr[j, 5:5 + rb] * wv[0:1] + ext_scr[j, 6:6 + rb] * wv[1:2]
                      + ext_scr[j, 7:7 + rb] * wv[2:3] + cur * wv[3:4])
            if kind < 2:
                scale = HEAD_DIM ** -0.5 if kind == 0 else 1.0
                s = s * (lax.rsqrt(jnp.sum(s * s, axis=-1, keepdims=True) + EPS) * scale)
            outs[kind][:, h * HEAD_DIM:(h + 1) * HEAD_DIM] = s
        ba = ba_ref[...]
        lane = _lane_iota(ba.shape)
        beta = _sigmoid(ba)
        g = -jnp.exp(al_ref[...]) * _softplus(ba + dt_ref[...])
        bg_ref[...] = jnp.where(lane < N_HEADS, beta, jnp.where(lane < 2 * N_HEADS, g, 0.0))

    row512 = pl.BlockSpec((rb, DN_WIDTH), lambda i: (i, 0))
    row128 = pl.BlockSpec((rb, 128), lambda i: (i, 0))
    vec128 = pl.BlockSpec((1, 128), lambda i: (0, 0))
    return pl.pallas_call(
        body, name="dn_act", grid=(T // rb,),
        in_specs=[pl.BlockSpec((rb, W3), lambda i: (i, 0)), _halo_prev_spec(rb, W3),
                  pl.BlockSpec((rb, 128), lambda i: (i, BA_COL // 128)),
                  pl.BlockSpec((4, W3), lambda i: (0, 0)), vec128, vec128],
        out_specs=[row512, row512, row512, row128],
        out_shape=[jax.ShapeDtypeStruct((T, DN_WIDTH), F32)] * 3 + [jax.ShapeDtypeStruct((T, 128), F32)],
        scratch_shapes=[pltpu.VMEM((3 * N_HEADS, rb + 8, HEAD_DIM), F32)],
        compiler_params=_cp("parallel"))(p, p, p, conv_w, alog_row, dtb_row)


def _dn_act_bwd(p, conv_w, alog_row, dtb_row, dq, dk, dv, dbg, dp_mid, rb=256):
    T = p.shape[0]
    rb = min(rb, T)
    nb = T // rb
    re = rb + 8
    W3 = 3 * DN_WIDTH

    def body(p_ref, prev_ref, next_ref, ba_ref, w_ref, al_ref, dt_ref, dq_ref, dqn_ref, dk_ref, dkn_ref,
             dv_ref, dvn_ref, dbg_ref, mid_ref, draw_ref, gw_ref, gad_ref, ext_scr, dc_scr):
        i = pl.program_id(0)
        draw_ref[:, W3:2 * W3] = mid_ref[...]

        @pl.when(i == 0)
        def _():
            gw_ref[...] = jnp.zeros_like(gw_ref)
            gad_ref[...] = jnp.zeros_like(gad_ref)
        row = lax.broadcasted_iota(jnp.int32, (re, 1), 0)
        live = (row < rb) | (i < nb - 1)
        d_refs = ((dq_ref, dqn_ref), (dk_ref, dkn_ref), (dv_ref, dvn_ref))
        for j in range(3 * N_HEADS):
            kind, h = divmod(j, N_HEADS)
            cols = slice(j * HEAD_DIM, (j + 1) * HEAD_DIM)
            hcols = slice(h * HEAD_DIM, (h + 1) * HEAD_DIM)
            ext_scr[j, 0:8] = jnp.where(i > 0, prev_ref[:, cols], 0.0)
            ext_scr[j, 8:8 + rb] = p_ref[:, cols]
            ext_scr[j, 8 + rb:] = next_ref[:, cols]
            tp = [ext_scr[j, 5 + k:5 + k + re] for k in range(4)]
            wv = w_ref[:, cols]
            c = tp[0] * wv[0:1] + tp[1] * wv[1:2] + tp[2] * wv[2:3] + tp[3] * wv[3:4]
            sg = _sigmoid(c)
            s = c * sg
            d_in = jnp.where(live, jnp.concatenate([d_refs[kind][0][:, hcols], d_refs[kind][1][:, hcols]], axis=0), 0.0)
            if kind < 2:
                scale = HEAD_DIM ** -0.5 if kind == 0 else 1.0
                n = lax.rsqrt(jnp.sum(s * s, axis=-1, keepdims=True) + EPS)
                hat = s * n
                d_in = (n * scale) * (d_in - hat * jnp.sum(hat * d_in, axis=-1, keepdims=True))
            dc = d_in * (sg + s * (1.0 - sg))
            dc_scr[j] = dc
            dcc = dc[0:rb]
            draw = (dcc * wv[3:4] + dc_scr[j, 1:1 + rb] * wv[2:3] + dc_scr[j, 2:2 + rb] * wv[1:2]
                    + dc_scr[j, 3:3 + rb] * wv[0:1])
            draw_ref[:, cols] = draw.astype(BF16)
            for k in range(4):
                gw_ref[k:k + 1, cols] += jnp.sum(tp[k][0:rb] * dcc, axis=0, keepdims=True)
        ba = ba_ref[...]
        dbg = dbg_ref[...]
        lane = _lane_iota(ba.shape)
        beta = _sigmoid(ba)
        ea = jnp.exp(al_ref[...])
        z = ba + dt_ref[...]
        d_a = dbg * (-ea) * _sigmoid(z)
        dba = jnp.where(lane < N_HEADS, dbg * beta * (1.0 - beta), jnp.where(lane < 2 * N_HEADS, d_a, 0.0))
        draw_ref[:, BA_COL:] = dba.astype(BF16)
        isg = (lane >= N_HEADS) & (lane < 2 * N_HEADS)
        g = -ea * _softplus(z)
        gad_ref[0:1, :] += jnp.sum(jnp.where(isg, dbg * g, 0.0), axis=0, keepdims=True)
        gad_ref[1:2, :] += jnp.sum(jnp.where(isg, d_a, 0.0), axis=0, keepdims=True)

    row512 = pl.BlockSpec((rb, DN_WIDTH), lambda i: (i, 0))
    row128 = pl.BlockSpec((rb, 128), lambda i: (i, 0))
    vec128 = pl.BlockSpec((1, 128), lambda i: (0, 0))
    next512 = _halo_next_spec(rb, DN_WIDTH, T)
    return pl.pallas_call(
        body, name="dn_act_bwd", grid=(nb,),
        in_specs=[pl.BlockSpec((rb, W3), lambda i: (i, 0)), _halo_prev_spec(rb, W3), _halo_next_spec(rb, W3, T),
                  pl.BlockSpec((rb, 128), lambda i: (i, BA_COL // 128)),
                  pl.BlockSpec((4, W3), lambda i: (0, 0)), vec128, vec128,
                  row512, next512, row512, next512, row512, next512, row128,
                  pl.BlockSpec((rb, W3), lambda i: (i, 0))],
        out_specs=[pl.BlockSpec((rb, PROJ_PAD), lambda i: (i, 0)),
                   pl.BlockSpec((4, W3), lambda i: (0, 0)), pl.BlockSpec((2, 128), lambda i: (0, 0))],
        out_shape=[jax.ShapeDtypeStruct((T, PROJ_PAD), BF16),
                   jax.ShapeDtypeStruct((4, W3), F32), jax.ShapeDtypeStruct((2, 128), F32)],
        scratch_shapes=[pltpu.VMEM((3 * N_HEADS, rb + 16, HEAD_DIM), F32), pltpu.VMEM((3 * N_HEADS, re, HEAD_DIM), F32)],
        compiler_params=_cp("arbitrary"))(p, p, p, p, conv_w, alog_row, dtb_row, dq, dq, dk, dk, dv, dv, dbg, dp_mid)


def _tri(incl):
    ii = lax.broadcasted_iota(jnp.int32, (CHUNK, CHUNK), 0)
    jj = lax.broadcasted_iota(jnp.int32, (CHUNK, CHUNK), 1)
    return ii, jj, ((ii >= jj) if incl else (ii > jj))


def _dn_chunk(k, bg, cb=4):
    T = k.shape[0]
    N = T // CHUNK
    cb = min(cb, N)

    def body(k_ref, bg_ref, gc_ref, gct_ref, l_ref):
        ii, jj, incl = _tri(True)
        tri = incl.astype(F32)
        U = range(cb)
        bgv = [bg_ref[u * CHUNK:(u + 1) * CHUNK, :] for u in U]
        gc = [jnp.dot(tri, bgv[u], precision=lax.Precision.HIGHEST, preferred_element_type=F32) for u in U]
        gct = [gc[u].T for u in U]
        kk = [[None] * N_HEADS for _ in U]
        for u in U:
            gc_ref[u * CHUNK:(u + 1) * CHUNK, :] = gc[u]
            gct_ref[u] = gct[u][0:8]
            for h in range(N_HEADS):
                kh = k_ref[u * CHUNK:(u + 1) * CHUNK, h * HEAD_DIM:(h + 1) * HEAD_DIM]
                kk[u][h] = _nt(kh * bgv[u][:, h:h + 1], kh)
        for u in U:
            for h in range(N_HEADS):
                gcol = gc[u][:, N_HEADS + h:N_HEADS + h + 1]
                grow = gct[u][N_HEADS + h:N_HEADS + h + 1, :]
                l_ref[u, h] = kk[u][h] * jnp.exp(jnp.where(ii > jj, gcol - grow, NEG))

    rows = cb * CHUNK
    return pl.pallas_call(
        body, name="dn_chunk", grid=(N // cb,),
        in_specs=[pl.BlockSpec((rows, DN_WIDTH), lambda n: (n, 0)), pl.BlockSpec((rows, 128), lambda n: (n, 0))],
        out_specs=[pl.BlockSpec((rows, 128), lambda n: (n, 0)), pl.BlockSpec((cb, 8, CHUNK), lambda n: (n, 0, 0)),
                   pl.BlockSpec((cb, N_HEADS, CHUNK, CHUNK), lambda n: (n, 0, 0, 0))],
        out_shape=[jax.ShapeDtypeStruct((T, 128), F32), jax.ShapeDtypeStruct((N, 8, CHUNK), F32),
                   jax.ShapeDtypeStruct((N, N_HEADS, CHUNK, CHUNK), F32)],
        compiler_params=_cp("parallel"))(k, bg)


def _tri_inv(lt):
    S = lt.shape[1]

    def body(l_ref, a_ref):
        sub = lax.broadcasted_iota(jnp.int32, (8, S), 0)
        groups = CHUNK // 8
        for i in range(CHUNK):
            acc = [((sub + 8 * k) == i).astype(F32) for k in range(groups)]
            for jb in range((i + 7) // 8):
                nk = jb + 1

                def step(j, carry, nk=nk, i=i):
                    lrow = l_ref[pl.ds(i * CHUNK + j, 1), :]
                    return tuple(carry[k] - lrow * a_ref[j, 8 * k:8 * k + 8, :] for k in range(nk))

                acc[:nk] = list(lax.fori_loop(8 * jb, min(8 * jb + 8, i), step, tuple(acc[:nk])))
            for k in range(groups):
                a_ref[i, 8 * k:8 * k + 8, :] = acc[k]

    return pl.pallas_call(
        body, name="tri_inv", out_shape=jax.ShapeDtypeStruct((CHUNK, CHUNK, S), F32),
        compiler_params=pltpu.CompilerParams(vmem_limit_bytes=VMEM_LIMIT))(lt)


def _dn_head_terms(qh, kh, vh, beta, gcol, grow):
    ii, jj, incl = _tri(True)
    gam = jnp.exp(jnp.where(incl, gcol - grow, NEG))
    glast = grow[:, CHUNK - 1:CHUNK]
    cd = jnp.exp(glast)
    shape = (CHUNK, HEAD_DIM)
    E = jnp.broadcast_to(jnp.exp(gcol), shape)
    Fd = jnp.broadcast_to(jnp.exp(glast - gcol), shape)
    beta = jnp.broadcast_to(beta, shape)
    kb = kh * beta
    return dict(ii=ii, jj=jj, gam=gam, E=E, F=Fd, beta=beta, cd=cd, kb=kb, vb=vh * beta, W=kb * E, qE=qh * E,
                kt=kh * Fd)


def _apply_a(a, u):
    hi, lo = _split(a)
    ub = _bf(u)
    return jnp.dot(hi, ub, preferred_element_type=F32) + jnp.dot(lo, ub, preferred_element_type=F32)


def _dn_scan(q, k, v, bg, gc, gct, a):
    T = q.shape[0]
    N = T // CHUNK
    cb = min(SCAN_CHUNKS_FWD, N)

    def body(q_ref, k_ref, v_ref, bg_ref, gc_ref, gct_ref, a_ref, o_ref, sall_ref, s_ref):
        @pl.when(pl.program_id(0) == 0)
        def _():
            s_ref[...] = jnp.zeros_like(s_ref)
        H = range(N_HEADS)
        sl = [slice(h * HEAD_DIM, (h + 1) * HEAD_DIM) for h in H]
        pre = []
        for u in range(cb):
            r = slice(u * CHUNK, (u + 1) * CHUNK)
            bgv, gcv, gctv = bg_ref[r, :], gc_ref[r, :], gct_ref[u]
            q_, k_ = [q_ref[r, s] for s in sl], [k_ref[r, s] for s in sl]
            t = [_dn_head_terms(q_[h], k_[h], v_ref[r, sl[h]], bgv[:, h:h + 1],
                                gcv[:, N_HEADS + h:N_HEADS + h + 1], gctv[N_HEADS + h:N_HEADS + h + 1, :]) for h in H]
            P = [_nt(q_[h], k_[h]) * t[h]["gam"] for h in H]
            pre.append((r, t, P))
        S = [s_ref[h] for h in H]
        for u in range(cb):
            r, t, P = pre[u]
            for h in H:
                sall_ref[u, h] = S[h]
            WS = [_nn(t[h]["W"], S[h]) for h in H]
            qS = [_nn(t[h]["qE"], S[h]) for h in H]
            vn = [_apply_a(a_ref[u, h], t[h]["vb"] - WS[h]) for h in H]
            Pv = [_nn(P[h], vn[h]) for h in H]
            kv = [_tn(t[h]["kt"], vn[h]) for h in H]
            for h in H:
                o_ref[r, sl[h]] = qS[h] + Pv[h]
            S = [t[h]["cd"] * S[h] + kv[h] for h in H]
        for h in H:
            s_ref[h] = S[h]

    row512 = pl.BlockSpec((cb * CHUNK, DN_WIDTH), lambda n: (n, 0))
    row128 = pl.BlockSpec((cb * CHUNK, 128), lambda n: (n, 0))
    return pl.pallas_call(
        body, name="dn_scan", grid=(N // cb,),
        in_specs=[row512, row512, row512, row128, row128, pl.BlockSpec((cb, 8, CHUNK), lambda n: (n, 0, 0)),
                  pl.BlockSpec((cb, N_HEADS, CHUNK, CHUNK), lambda n: (n, 0, 0, 0))],
        out_specs=[row512, pl.BlockSpec((cb, N_HEADS, HEAD_DIM, HEAD_DIM), lambda n: (n, 0, 0, 0))],
        out_shape=[jax.ShapeDtypeStruct((T, DN_WIDTH), F32),
                   jax.ShapeDtypeStruct((N, N_HEADS, HEAD_DIM, HEAD_DIM), F32)],
        scratch_shapes=[pltpu.VMEM((N_HEADS, HEAD_DIM, HEAD_DIM), F32)],
        compiler_params=_cp("arbitrary"))(q, k, v, bg, gc, gct, a)


def _dn_scan_bwd(q, k, v, bg, gc, gct, a, a_t, sall, do, dep=None):
    T = q.shape[0]
    N = T // CHUNK

    cb = min(SCAN_CHUNKS, N)
    nb = N // cb

    def body(q_ref, k_ref, v_ref, bg_ref, gc_ref, gct_ref, a_ref, at_ref, sall_ref, do_ref, *rest):
        dq_ref, dk_ref, dv_ref, dbg_ref, ds_ref = rest[-5:]
        @pl.when(pl.program_id(0) == 0)
        def _():
            ds_ref[...] = jnp.zeros_like(ds_ref)
        lane = _lane_iota((CHUNK, 128))
        rowi = lax.broadcasted_iota(jnp.int32, (CHUNK, 1), 0)
        ii, jj, _ = _tri(True)
        rev = (jj >= ii).astype(F32)
        H = range(N_HEADS)
        sl = [slice(h * HEAD_DIM, (h + 1) * HEAD_DIM) for h in H]
        pre = {}
        for u in reversed(range(cb)):
            r = slice(u * CHUNK, (u + 1) * CHUNK)
            bgv, gcv, gctv = bg_ref[r, :], gc_ref[r, :], gct_ref[u]
            q_, k_, v_ = [q_ref[r, s] for s in sl], [k_ref[r, s] for s in sl], [v_ref[r, s] for s in sl]
            dO = [do_ref[r, s] for s in sl]
            t = [_dn_head_terms(q_[h], k_[h], v_[h], bgv[:, h:h + 1], gcv[:, N_HEADS + h:N_HEADS + h + 1],
                                gctv[N_HEADS + h:N_HEADS + h + 1, :]) for h in H]
            beta = [t[h]["beta"] for h in H]
            S = [sall_ref[u, h] for h in H]
            A = [a_ref[u, h] for h in H]
            WS = [_nn(t[h]["W"], S[h]) for h in H]
            KK = [_nt(t[h]["kb"], k_[h]) for h in H]
            QK = [_nt(q_[h], k_[h]) for h in H]
            d_qE = [_nt(dO[h], S[h]) for h in H]
            vn = [_apply_a(A[h], t[h]["vb"] - WS[h]) for h in H]
            PtdO = [_tn(QK[h] * t[h]["gam"], dO[h]) for h in H]
            qEdO = [_tn(t[h]["qE"], dO[h]) for h in H]
            dOvn = [_nt(dO[h], vn[h]) for h in H]
            dQK = [jnp.where(ii >= jj, dOvn[h], 0.0) * t[h]["gam"] for h in H]
            dQKk = [_nn(dQK[h], k_[h]) for h in H]
            dQKq = [_tn(dQK[h], q_[h]) for h in H]
            pre[u] = (r, q_, k_, v_, beta, t, S, A, KK, QK, d_qE, vn, PtdO, qEdO, dQK, dQKk, dQKq)
        dSn = [ds_ref[h] for h in H]
        for u in reversed(range(cb)):
            r, q_, k_, v_, beta, t, S, A, KK, QK, d_qE, vn, PtdO, qEdO, dQK, dQKk, dQKq = pre[u]
            gam, E, Fd, cd, kb = ([t[h][n] for h in H] for n in ("gam", "E", "F", "cd", "kb"))
            ktdS = [_nn(t[h]["kt"], dSn[h]) for h in H]
            dU = [_apply_a(at_ref[u, h], PtdO[h] + ktdS[h]) for h in H]
            d_kt = [_nt(vn[h], dSn[h]) for h in H]
            dUvn = [_nt(dU[h], vn[h]) for h in H]
            dUS = [_nt(dU[h], S[h]) for h in H]
            WdU = [_tn(t[h]["W"], dU[h]) for h in H]
            d_cd = [jnp.sum(S[h] * dSn[h]) for h in H]
            dSn = [cd[h] * dSn[h] + qEdO[h] - WdU[h] for h in H]
            dKK = [jnp.where(ii > jj, -dUvn[h], 0.0) * gam[h] for h in H]
            dKKk = [_nn(dKK[h], k_[h]) for h in H]
            dKKkb = [_tn(dKK[h], kb[h]) for h in H]
            dbeta_arr = jnp.zeros((CHUNK, 128), F32)
            dgc_arr = jnp.zeros((CHUNK, 128), F32)
            for h in H:
                dW = -dUS[h]
                dq_ref[r, sl[h]] = dQKk[h] + d_qE[h] * E[h]
                d_kb = dKKk[h] + dW * E[h]
                dk_ref[r, sl[h]] = dQKq[h] + dKKkb[h] + d_kb * beta[h] + d_kt[h] * Fd[h]
                dv_ref[r, sl[h]] = dU[h] * beta[h]
                Z = dQK[h] * QK[h] + dKK[h] * KK[h]
                dbeta = jnp.sum(dU[h] * v_[h] + d_kb * k_[h], axis=-1, keepdims=True)
                m_e = (dW * kb[h] + d_qE[h] * q_[h]) * E[h]
                m_f = d_kt[h] * k_[h] * Fd[h]
                zdiag = jnp.where(ii == jj, jnp.sum(Z, axis=0, keepdims=True), 0.0)
                dgc = (jnp.sum(m_e - m_f, axis=-1, keepdims=True) + jnp.sum(Z - zdiag, axis=-1, keepdims=True)
                       + jnp.where(rowi == CHUNK - 1, jnp.sum(m_f) + d_cd[h] * cd[h], 0.0))
                dbeta_arr = dbeta_arr + jnp.where(lane == h, dbeta, 0.0)
                dgc_arr = dgc_arr + jnp.where(lane == N_HEADS + h, dgc, 0.0)
            dbg_ref[r, :] = dbeta_arr + jnp.dot(rev, dgc_arr, precision=lax.Precision.HIGHEST,
                                                preferred_element_type=F32)
        for h in H:
            ds_ref[h] = dSn[h]

    row512 = pl.BlockSpec((cb * CHUNK, DN_WIDTH), lambda n: (nb - 1 - n, 0))
    row128 = pl.BlockSpec((cb * CHUNK, 128), lambda n: (nb - 1 - n, 0))
    in_specs, args = _with_dep(
        [row512, row512, row512, row128, row128,
         pl.BlockSpec((cb, 8, CHUNK), lambda n: (nb - 1 - n, 0, 0)),
         pl.BlockSpec((cb, N_HEADS, CHUNK, CHUNK), lambda n: (nb - 1 - n, 0, 0, 0)),
         pl.BlockSpec((cb, N_HEADS, CHUNK, CHUNK), lambda n: (nb - 1 - n, 0, 0, 0)),
         pl.BlockSpec((cb, N_HEADS, HEAD_DIM, HEAD_DIM), lambda n: (nb - 1 - n, 0, 0, 0)), row512],
        [q, k, v, bg, gc, gct, a, a_t, sall, do], dep)
    return pl.pallas_call(
        body, name="dn_scan_bwd", grid=(nb,), in_specs=in_specs,
        out_specs=[row512, row512, row512, row128],
        out_shape=[jax.ShapeDtypeStruct((T, DN_WIDTH), F32)] * 3 + [jax.ShapeDtypeStruct((T, 128), F32)],
        scratch_shapes=[pltpu.VMEM((N_HEADS, HEAD_DIM, HEAD_DIM), F32)],
        compiler_params=_cp("arbitrary"))(*args)


def _sg_mask():
    ii = lax.broadcasted_iota(jnp.int32, (SG_BLOCK, SG_BLOCK), 0) // CHUNK
    jj = lax.broadcasted_iota(jnp.int32, (SG_BLOCK, SG_BLOCK), 1) // CHUNK
    return jj <= ii


def _mix_fwd(o, p, ong, sgn, sgw, sgbt):
    T = o.shape[0]
    rb = SG_BLOCK

    def body(o_ref, gate_ref, u_ref, vg_ref, ong_ref, sgn_ref, sgw_ref, sgbt_ref, mix_ref):
        mask = _sg_mask()
        gate = gate_ref[...]
        for h in range(N_HEADS):
            sl = slice(h * HEAD_DIM, (h + 1) * HEAD_DIM)
            oh = o_ref[:, sl]
            r = lax.rsqrt(jnp.mean(oh * oh, axis=-1, keepdims=True) + EPS)
            mix_ref[:, sl] = (oh * r * ong_ref[...] * _silu(gate[:, sl])).astype(BF16)
        for gi in range(SG_GROUPS):
            sl = slice(gi * SG_BLOCK, (gi + 1) * SG_BLOCK)
            gv = _gelu(vg_ref[:, sl])
            r = lax.rsqrt(jnp.mean(gv * gv, axis=-1, keepdims=True) + EPS)
            vh = gv * r * sgn_ref[:, sl]
            s = _nn(jnp.where(mask, sgw_ref[gi], 0.0), vh) + sgbt_ref[:, gi:gi + 1]
            mix_ref[:, DN_WIDTH + gi * SG_BLOCK:DN_WIDTH + (gi + 1) * SG_BLOCK] = (_gelu(u_ref[:, sl]) * s).astype(BF16)

    def col(c):
        return pl.BlockSpec((rb, 512), lambda i: (i, c))
    return pl.pallas_call(
        body, name="mix_fwd", grid=(T // rb,),
        in_specs=[pl.BlockSpec((rb, DN_WIDTH), lambda i: (i, 0)), col(3), col(4), col(5),
                  pl.BlockSpec((1, 128), lambda i: (0, 0)), pl.BlockSpec((1, SG_WIDTH), lambda i: (0, 0)),
                  pl.BlockSpec((SG_GROUPS, SG_BLOCK, SG_BLOCK), lambda i: (0, 0, 0)),
                  pl.BlockSpec((SG_BLOCK, 128), lambda i: (0, 0))],
        out_specs=pl.BlockSpec((rb, D_MODEL), lambda i: (i, 0)),
        out_shape=jax.ShapeDtypeStruct((T, D_MODEL), BF16),
        compiler_params=_cp("parallel"))(o, p, p, p, ong, sgn, sgw, sgbt)


def _mix_bwd(o, p, ong, sgn, sgw, sgbt, dmix, dep=None):
    T = o.shape[0]
    rb = SG_BLOCK

    def body(o_ref, gate_ref, u_ref, vg_ref, ong_ref, sgn_ref, sgw_ref, sgbt_ref, dmix_ref, *rest):
        do_ref, dp_ref, gong_ref, gsgn_ref, gsgw_ref, gsgbt_ref = rest[-6:]
        @pl.when(pl.program_id(0) == 0)
        def _():
            gong_ref[...] = jnp.zeros_like(gong_ref)
            gsgn_ref[...] = jnp.zeros_like(gsgn_ref)
            gsgw_ref[...] = jnp.zeros_like(gsgw_ref)
            gsgbt_ref[...] = jnp.zeros_like(gsgbt_ref)
        mask = _sg_mask()
        gate = gate_ref[...]
        lane = _lane_iota((SG_BLOCK, 128))
        for h in range(N_HEADS):
            sl = slice(h * HEAD_DIM, (h + 1) * HEAD_DIM)
            oh = o_ref[:, sl]
            dm = dmix_ref[:, sl]
            r = lax.rsqrt(jnp.mean(oh * oh, axis=-1, keepdims=True) + EPS)
            oh_hat = oh * r
            gt = gate[:, sl]
            sg = _silu(gt)
            dp_ref[:, sl] = (dm * oh_hat * ong_ref[...] * _dsilu(gt)).astype(BF16)
            dn_ = dm * sg
            gong_ref[...] += jnp.sum(dn_ * oh_hat, axis=0, keepdims=True)
            dhat = dn_ * ong_ref[...]
            do_ref[:, sl] = r * (dhat - oh_hat * jnp.mean(dhat * oh_hat, axis=-1, keepdims=True))
        for gi in range(SG_GROUPS):
            sl = slice(gi * SG_BLOCK, (gi + 1) * SG_BLOCK)
            vraw = vg_ref[:, sl]
            gv = _gelu(vraw)
            r = lax.rsqrt(jnp.mean(gv * gv, axis=-1, keepdims=True) + EPS)
            vhat = gv * r
            vn = vhat * sgn_ref[:, sl]
            wm = jnp.where(mask, sgw_ref[gi], 0.0)
            s = _nn(wm, vn) + sgbt_ref[:, gi:gi + 1]
            uraw = u_ref[:, sl]
            dm = dmix_ref[:, DN_WIDTH + gi * SG_BLOCK:DN_WIDTH + (gi + 1) * SG_BLOCK]
            dp_ref[:, DN_WIDTH + gi * SG_BLOCK:DN_WIDTH + (gi + 1) * SG_BLOCK] = (dm * s * _dgelu(uraw)).astype(BF16)
            ds = dm * _gelu(uraw)
            gsgbt_ref[...] += jnp.where(lane == gi, jnp.sum(ds, axis=-1, keepdims=True), 0.0)
            gsgw_ref[gi] += jnp.where(mask, _nt(ds, vn), 0.0)
            dvn = _tn(wm, ds)
            gsgn_ref[:, sl] += jnp.sum(dvn * vhat, axis=0, keepdims=True)
            dhat = dvn * sgn_ref[:, sl]
            dgv = r * (dhat - vhat * jnp.mean(dhat * vhat, axis=-1, keepdims=True))
            dp_ref[:, 2 * DN_WIDTH + gi * SG_BLOCK:2 * DN_WIDTH + (gi + 1) * SG_BLOCK] = (dgv * _dgelu(vraw)).astype(BF16)

    def col(c):
        return pl.BlockSpec((rb, 512), lambda i: (i, c))
    full = lambda *s: pl.BlockSpec(s, lambda i: (0,) * len(s))
    in_specs, args = _with_dep(
        [pl.BlockSpec((rb, DN_WIDTH), lambda i: (i, 0)), col(3), col(4), col(5),
         full(1, 128), full(1, SG_WIDTH), full(SG_GROUPS, SG_BLOCK, SG_BLOCK), full(SG_BLOCK, 128),
         pl.BlockSpec((rb, D_MODEL), lambda i: (i, 0))],
        [o, p, p, p, ong, sgn, sgw, sgbt, dmix], dep)
    return pl.pallas_call(
        body, name="mix_bwd", grid=(T // rb,), in_specs=in_specs,
        out_specs=[pl.BlockSpec((rb, DN_WIDTH), lambda i: (i, 0)), pl.BlockSpec((rb, 3 * 512), lambda i: (i, 0)),
                   full(1, 128), full(1, SG_WIDTH), full(SG_GROUPS, SG_BLOCK, SG_BLOCK), full(SG_BLOCK, 128)],
        out_shape=[jax.ShapeDtypeStruct((T, DN_WIDTH), F32), jax.ShapeDtypeStruct((T, 3 * 512), BF16),
                   jax.ShapeDtypeStruct((1, 128), F32), jax.ShapeDtypeStruct((1, SG_WIDTH), F32),
                   jax.ShapeDtypeStruct((SG_GROUPS, SG_BLOCK, SG_BLOCK), F32),
                   jax.ShapeDtypeStruct((SG_BLOCK, 128), F32)],
        compiler_params=_cp("arbitrary"))(*args)


def _pad_lanes(row, offset=0):
    n = row.shape[1]
    return jnp.pad(row, ((0, 0), (offset, 128 - n - offset)))


def _local_step(x, tgt, w, dep=None, late_weights=None, on_grad=None):
    T = x.shape[0]
    N = T // CHUNK
    on_grad = on_grad or (lambda name, g: None)
    alog_row = _pad_lanes(w["dn_a_log"], N_HEADS)
    dtb_row = _pad_lanes(w["dn_dt_bias"], N_HEADS)
    sgbt = jnp.pad(w["sg_b"].T, ((0, 0), (0, 128 - SG_GROUPS)))

    p, h1, w_in_pad = _in_proj(x, w["attn_norm_g"], w["w_in"], dep=dep)
    q, k, v, bg = _dn_act(p, w["dn_conv_w"], alog_row, dtb_row)
    gc, gct, lmat = _dn_chunk(k, bg)
    lt = lmat.reshape(N * N_HEADS, CHUNK * CHUNK).T
    at = _tri_inv(lt)
    a = at.reshape(CHUNK * CHUNK, N * N_HEADS).T.reshape(N, N_HEADS, CHUNK, CHUNK)
    a_t = at.transpose(1, 0, 2).reshape(CHUNK * CHUNK, N * N_HEADS).T.reshape(N, N_HEADS, CHUNK, CHUNK)
    o, sall = _dn_scan(q, k, v, bg, gc, gct, a)
    mix = _mix_fwd(o, p, w["dn_out_norm_g"], w["sg_norm_g"], w["sg_w"], sgbt)
    if late_weights is not None:
        w = {**w, **late_weights("out_proj", mix)}
    x2, h2 = _out_proj(mix, w["w_out"], x, w["ffn_norm_g"])
    up = _mm_nn("up_proj", h2, w["w_up"], F32, 512, D_FF)
    act = _ffn_act(up, w["ffn_conv_w"], w["ffn_conv_b"])
    if late_weights is not None:
        w = {**w, **late_weights("down_proj", act)}
    loss, dx3, g_final = _down_proj_loss(act, w["w_down"], x2, tgt, w["final_norm_g"])

    dact = _mm_nt("d_act", dx3, w["w_down"], F32, 512, D_FF)
    g_w_down = _mm_tn("g_w_down", act, dx3, D_FF, 1024, 1024)
    tok = on_grad("w_down", g_w_down)
    dup, g_ffn_conv_w, g_ffn_conv_b = _ffn_act_bwd(up, dact, w["ffn_conv_w"], w["ffn_conv_b"], dep=tok)
    g_w_up = _mm_tn("g_w_up", h2, dup, 1024, 2 * D_FF // 4, 2048, col_major_tiles=True)
    tok = on_grad("w_up", g_w_up)
    dx2, g_ffn_norm = _mm_nt_rms_bwd("d_h2", dup, w["w_up"], x2, w["ffn_norm_g"], dx3, dep=tok)
    dmix = _mm_nt("d_mix", dx2, w["w_out"], F32, 512, 1024)
    g_w_out = _mm_tn("g_w_out", mix, dx2, 1024, 1024, 1024)
    tok = on_grad("w_out", g_w_out)
    do, dp_mid, g_ong, g_sgn, g_sgw, g_sgbt = _mix_bwd(o, p, w["dn_out_norm_g"], w["sg_norm_g"], w["sg_w"], sgbt,
                                                      dmix, dep=tok)
    early = dict(dn_out_norm_g=g_ong, sg_norm_g=g_sgn, sg_w=g_sgw, sg_b=g_sgbt[:, :SG_GROUPS].T,
                 ffn_norm_g=g_ffn_norm, ffn_conv_w=g_ffn_conv_w, ffn_conv_b=g_ffn_conv_b, final_norm_g=g_final)
    tok = on_grad("small_early", early)
    dq, dk, dv, dbg = _dn_scan_bwd(q, k, v, bg, gc, gct, a, a_t, sall, do, dep=tok)
    dp, g_dn_conv_w, g_ad = _dn_act_bwd(p, w["dn_conv_w"], alog_row, dtb_row, dq, dk, dv, dbg, dp_mid)
    g_w_in = _mm_tn("g_w_in", h1, dp, 1024, PROJ_PAD, 1024, col_groups=(4, PROJ_COLS // 4))
    tok = on_grad("w_in", g_w_in)
    grad_x, g_attn_norm = _mm_nt_rms_bwd("d_h1", dp, w_in_pad, x, w["attn_norm_g"], dx2, dep=tok)

    grads = dict(
        attn_norm_g=g_attn_norm, w_in=g_w_in, dn_conv_w=g_dn_conv_w,
        dn_a_log=g_ad[0:1, N_HEADS:2 * N_HEADS], dn_dt_bias=g_ad[1:2, N_HEADS:2 * N_HEADS],
        w_out=g_w_out, w_up=g_w_up, w_down=g_w_down, **early)
    return loss, grad_x, grads


def _me():
    return lax.axis_index("x"), lax.axis_index("y"), lax.axis_index("c")


def _peer(rel):
    x, y, c = _me()
    return {"x": (1 - x, y, c), "y": (x, 1 - y, c), "xy": (1 - x, 1 - y, c), "c": (x, y, 1 - c)}[rel]


def _chip_of(dev):
    return 2 * dev[0] + dev[1]


CHIP_RELS = ("x", "y", "xy")


def _run_copies(copies, sends, recvs):
    for cp in copies:
        cp.start()
    for cp in recvs:
        cp.wait_recv()
    for cp in sends:
        cp.wait_send()


def _gather_first(w_shard, small_shard):
    R = w_shard.shape[0]
    r2 = R // 2

    def body(w_ref, s_ref, w_out, s_out, send_sems, recv_sems):
        x, y, c = _me()
        me = _chip_of((x, y))
        sib = _peer("c")

        def half(chip, core):
            return w_out.at[chip, pl.ds(pl.multiple_of(core * r2, 8), r2), :]

        def copy(k, src, dst, to):
            return pltpu.make_async_remote_copy(src_ref=src, dst_ref=dst, send_sem=send_sems.at[k],
                                                recv_sem=recv_sems.at[k], device_id=to, device_id_type=MESH)

        own_rows = w_ref.at[pl.ds(pl.multiple_of(c * r2, 8), r2), :]
        first = [copy(r, own_rows, half(me, c), _peer(rel)) for r, rel in enumerate(CHIP_RELS)]
        first += [copy(3 + r, s_ref, s_out.at[me], _peer(rel)) for r, rel in enumerate(CHIP_RELS)]
        for cp in first:
            cp.start()
        passed = []
        for r, rel in enumerate(CHIP_RELS):
            their = _chip_of(_peer(rel))
            copy(r, own_rows, half(their, c), _peer(rel)).wait_recv()
            fwd = copy(6 + r, half(their, c), half(their, c), sib)
            fwd.start()
            passed.append(fwd)
        for r, rel in enumerate(CHIP_RELS):
            their = _chip_of(_peer(rel))
            copy(3 + r, s_ref, s_out.at[their], _peer(rel)).wait_recv()
            copy(6 + r, own_rows, half(their, 1 - c), sib).wait_recv()
        for cp in first + passed:
            cp.wait_send()

    w_all, s_all = pl.pallas_call(
        body, name="gather_first", in_specs=[ANY, ANY], out_specs=[ANY, ANY],
        out_shape=[jax.ShapeDtypeStruct((4,) + w_shard.shape, w_shard.dtype),
                   jax.ShapeDtypeStruct((4,) + small_shard.shape, small_shard.dtype)],
        scratch_shapes=[pltpu.SemaphoreType.DMA((9,)), pltpu.SemaphoreType.DMA((9,))])(w_shard, small_shard)
    me = _chip_of(_me())
    return (lax.dynamic_update_index_in_dim(w_all, w_shard, me, 0),
            lax.dynamic_update_index_in_dim(s_all, small_shard, me, 0))


OTHERS = tuple((fx, fy, fc) for fx in (0, 1) for fy in (0, 1) for fc in (0, 1) if (fx, fy, fc) != (0, 0, 0))


def _other(flip):
    x, y, c = _me()
    return (x ^ flip[0], y ^ flip[1], c ^ flip[2])


def _linear(dev):
    return 4 * dev[0] + 2 * dev[1] + dev[2]


def _exchange_small(small):
    def body(small_ref, out_ref, send_sems, recv_sems):
        my_slot = _linear(_me())
        sends, recvs = [], []
        for k, flip in enumerate(OTHERS):
            peer = _other(flip)
            sends.append(pltpu.make_async_remote_copy(
                src_ref=small_ref, dst_ref=out_ref.at[my_slot], send_sem=send_sems.at[k], recv_sem=recv_sems.at[k],
                device_id=peer, device_id_type=MESH))
            recvs.append(pltpu.make_async_remote_copy(
                src_ref=small_ref, dst_ref=out_ref.at[_linear(peer)], send_sem=send_sems.at[k],
                recv_sem=recv_sems.at[k], device_id=peer, device_id_type=MESH))
        _run_copies(sends, sends, recvs)

    out = pl.pallas_call(
        body, name="exchange_small", in_specs=[ANY], out_specs=ANY,
        out_shape=jax.ShapeDtypeStruct((8,) + small.shape, small.dtype),
        scratch_shapes=[pltpu.SemaphoreType.DMA((7,)), pltpu.SemaphoreType.DMA((7,))])(small)
    return lax.dynamic_update_index_in_dim(out, small, _linear(_me()), 0)


def _pair_swap(halves):
    n = len(halves)

    def body(*refs):
        src, out = refs[:n], refs[n:2 * n]
        send_sems, recv_sems = refs[2 * n:]
        sib = _peer("c")
        copies = [pltpu.make_async_remote_copy(
            src_ref=src[i], dst_ref=out[i], send_sem=send_sems.at[i], recv_sem=recv_sems.at[i],
            device_id=sib, device_id_type=MESH) for i in range(n)]
        _run_copies(copies, copies, copies)

    return pl.pallas_call(
        body, name="pair_swap", in_specs=[ANY] * n, out_specs=[ANY] * n,
        out_shape=[jax.ShapeDtypeStruct(h.shape, h.dtype) for h in halves],
        scratch_shapes=[pltpu.SemaphoreType.DMA((n,)), pltpu.SemaphoreType.DMA((n,))])(*halves)


HBM = pl.BlockSpec(memory_space=pltpu.HBM)
SEM = pl.BlockSpec(memory_space=pltpu.SEMAPHORE)
EFFECT = pltpu.SideEffectType.DATAFLOW_SIDE_EFFECTING


def _hbm(a):
    return pltpu.with_memory_space_constraint(a, pltpu.HBM)


def _transfer_start(name, srcs, lands, n_copies, make_copies, after=None):
    n, m = len(srcs), len(lands)

    def body(*refs):
        src, land = refs[:n], refs[n:n + m]
        outs = refs[n + m + (after is not None):]
        send_sems, recv_sems, token = outs[0], outs[1], outs[-1]
        for cp in make_copies(src, land, send_sems, recv_sems):
            cp.start()
        token[...] = jnp.zeros_like(token)

    arrs = list(srcs) + list(lands)
    in_specs, args = _with_dep([HBM] * (n + m), [_hbm(a) for a in arrs], after)
    out = pl.pallas_call(
        body, name=name,
        out_shape=(pltpu.SemaphoreType.DMA((n_copies,)), pltpu.SemaphoreType.DMA((n_copies,)),
                   *[pltpu.HBM(a.shape, a.dtype) for a in arrs], jax.ShapeDtypeStruct((8, 128), F32)),
        in_specs=in_specs,
        out_specs=(SEM, SEM, *[HBM] * (n + m), pl.BlockSpec(memory_space=pltpu.VMEM)),
        input_output_aliases={i: 2 + i for i in range(n + m)},
        compiler_params=pltpu.CompilerParams(has_side_effects=EFFECT))(*args)
    return out[0], out[1], list(out[2:2 + n]), list(out[2 + n:2 + n + m]), out[-1]


def _transfer_wait(name, send_sems, recv_sems, srcs, lands, make_copies, after):
    n, m = len(srcs), len(lands)

    def body(*refs):
        src, land = refs[:n], refs[n:n + m]
        s_sems, r_sems = refs[n + m], refs[n + m + 1]
        for cp in make_copies(src, land, s_sems, r_sems):
            cp.wait_send()
            cp.wait_recv()

    arrs = list(srcs) + list(lands)
    out = pl.pallas_call(
        body, name=name, out_shape=tuple(pltpu.HBM(a.shape, a.dtype) for a in arrs),
        in_specs=[HBM] * (n + m) + [SEM, SEM, ANY], out_specs=tuple([HBM] * (n + m)),
        input_output_aliases={i: i for i in range(n + m)},
        compiler_params=pltpu.CompilerParams(has_side_effects=EFFECT))(*arrs, send_sems, recv_sems, after)
    return list(out[:n]), list(out[n:])


def _gather_copies(src, land, send_sems, recv_sems):
    me = _chip_of(_me())
    copies = []
    for i in range(len(src)):
        for r, rel in enumerate(CHIP_RELS):
            k = 3 * i + r
            copies.append(pltpu.make_async_remote_copy(
                src_ref=src[i], dst_ref=land[i].at[me], send_sem=send_sems.at[k], recv_sem=recv_sems.at[k],
                device_id=_peer(rel), device_id_type=MESH))
    return copies


def _small_copies(src, land, send_sems, recv_sems):
    my_slot = _linear(_me())
    return [pltpu.make_async_remote_copy(
        src_ref=src[0], dst_ref=land[0].at[my_slot], send_sem=send_sems.at[k], recv_sem=recv_sems.at[k],
        device_id=_other(flip), device_id_type=MESH) for k, flip in enumerate(OTHERS)]


def _pieces_copies(src, land, send_sems, recv_sems):
    copies = []
    for k, flip in enumerate(OTHERS):
        peer = _other(flip)
        copies.append(pltpu.make_async_remote_copy(
            src_ref=src[0].at[_linear(peer)], dst_ref=land[0].at[k], send_sem=send_sems.at[k],
            recv_sem=recv_sems.at[k], device_id=peer, device_id_type=MESH))
    return copies


def _row_block(rows, cols, budget=2 * 1024 * 1024):
    rb = max(8, (budget // (4 * cols)) // 8 * 8)
    while rows % rb:
        rb -= 8
    return rb if rb > 0 else rows


def _sum_slots(name, first, rest):
    R, Cc = first.shape
    K = rest.shape[0]
    rb = _row_block(R, Cc)

    def body(f_ref, r_ref, o_ref):
        acc = f_ref[...].astype(F32)
        for j in range(K):
            acc = acc + r_ref[j].astype(F32)
        o_ref[...] = acc

    return pl.pallas_call(
        body, name=name, grid=(R // rb,),
        in_specs=[pl.BlockSpec((rb, Cc), lambda i: (i, 0)), pl.BlockSpec((K, rb, Cc), lambda i: (0, i, 0))],
        out_specs=pl.BlockSpec((rb, Cc), lambda i: (i, 0)),
        out_shape=jax.ShapeDtypeStruct((R, Cc), F32), compiler_params=_cp("parallel"))(first, rest)


def _adamw_math(w, gv, m, v):
    mn = ADAM_B1 * m + (1.0 - ADAM_B1) * gv
    vn = ADAM_B2 * v + (1.0 - ADAM_B2) * (gv * gv)
    m_hat = mn / (1.0 - ADAM_B1 ** ADAM_STEP)
    v_hat = vn / (1.0 - ADAM_B2 ** ADAM_STEP)
    return -ADAM_LR * (m_hat / (jnp.sqrt(v_hat) + ADAM_EPS) + ADAM_WD * w), mn, vn


def _adamw_halves(name, w, mine, theirs, m, v, core):
    R, Cc = w.shape
    r2 = R // 2
    rb = _row_block(r2, Cc, 1024 * 1024)
    nb2 = r2 // rb

    def body(c_ref, w_ref, mine_ref, theirs_ref, m_ref, v_ref, g_ref, d_ref, mo_ref, vo_ref):
        is_mine = (pl.program_id(0) // nb2) == c_ref[0]
        gv = jnp.where(is_mine, mine_ref[...], theirs_ref[...])
        g_ref[...] = gv
        d_ref[...], mo_ref[...], vo_ref[...] = _adamw_math(w_ref[...], gv, m_ref[...], v_ref[...])

    blk = pl.BlockSpec((rb, Cc), lambda i, c: (i, 0))
    half = lambda own: pl.BlockSpec(
        (rb, Cc), lambda i, c: (jnp.clip(i - (c[0] if own else 1 - c[0]) * nb2, 0, nb2 - 1), 0))
    return pl.pallas_call(
        body, name=name,
        grid_spec=pltpu.PrefetchScalarGridSpec(
            num_scalar_prefetch=1, grid=(2 * nb2,), in_specs=[blk, half(True), half(False), blk, blk],
            out_specs=[blk] * 4),
        out_shape=[jax.ShapeDtypeStruct((R, Cc), F32)] * 4, compiler_params=_cp("parallel"))(core, w, mine, theirs, m, v)


def _adamw_transposed(name, wt, mine, theirs, mt, vt, core):
    Cc, kh_n, _ = wt.shape
    r2 = mine.shape[0]
    per_half = kh_n // 2
    nb = -(-Cc // LANES)

    def body(c_ref, w_ref, mine_ref, theirs_ref, m_ref, v_ref, g_ref, d_ref, mo_ref, vo_ref):
        first = c_ref[0] == 0
        halves = (jnp.where(first, mine_ref[...], theirs_ref[...]).T,
                  jnp.where(first, theirs_ref[...], mine_ref[...]).T)
        for kh in range(kh_n):
            lo = (kh % per_half) * LANES
            g_ref[:, kh, :] = halves[kh // per_half][:, lo:lo + LANES]
        d_ref[...], mo_ref[...], vo_ref[...] = _adamw_math(w_ref[...], g_ref[...], m_ref[...], v_ref[...])

    blk = pl.BlockSpec((LANES, kh_n, LANES), lambda i, c: (i, 0, 0))
    half = pl.BlockSpec((r2, LANES), lambda i, c: (0, i))
    return pl.pallas_call(
        body, name=name,
        grid_spec=pltpu.PrefetchScalarGridSpec(
            num_scalar_prefetch=1, grid=(nb,), in_specs=[blk, half, half, blk, blk], out_specs=[blk] * 4),
        out_shape=[jax.ShapeDtypeStruct(wt.shape, F32)] * 4, compiler_params=_cp("parallel"))(
            core, wt, mine, theirs, mt, vt)


def _adamw(name, w, g, m, v):
    R, Cc = w.shape
    rb = _row_block(R, Cc, 1024 * 1024)

    def body(w_ref, g_ref, m_ref, v_ref, d_ref, mo_ref, vo_ref):
        d_ref[...], mo_ref[...], vo_ref[...] = _adamw_math(w_ref[...], g_ref[...], m_ref[...], v_ref[...])

    blk = pl.BlockSpec((rb, Cc), lambda i: (i, 0))
    return pl.pallas_call(
        body, name=name, grid=(R // rb,), in_specs=[blk] * 4, out_specs=[blk] * 3,
        out_shape=[jax.ShapeDtypeStruct((R, Cc), F32)] * 3, compiler_params=_cp("parallel"))(w, g, m, v)


def _pack(arrs):
    rows = []
    for a in arrs:
        flat = a.reshape(-1)
        pad = (-flat.shape[0]) % 128
        rows.append(jnp.pad(flat, (0, pad)).reshape(-1, 128))
    buf = jnp.concatenate(rows, axis=0)
    return jnp.pad(buf, ((0, (-buf.shape[0]) % 8), (0, 0)))


def _unpack(buf, shapes):
    out, r = [], 0
    for s in shapes:
        n = math.prod(s)
        nr = -(-n // 128)
        out.append(buf[r:r + nr].reshape(-1)[:n].reshape(s))
        r += nr
    return out


BIG = ("w_in", "w_out", "w_up", "w_down")
CONV = ("dn_conv_w", "ffn_conv_w")
REPL = ("attn_norm_g", "dn_a_log", "dn_dt_bias", "dn_out_norm_g", "sg_norm_g", "sg_w", "sg_b",
        "ffn_norm_g", "ffn_conv_b", "final_norm_g")
ORDER = ("attn_norm_g", "w_in", "dn_conv_w", "dn_a_log", "dn_dt_bias", "dn_out_norm_g", "sg_norm_g", "sg_w",
         "sg_b", "w_out", "ffn_norm_g", "w_up", "ffn_conv_w", "ffn_conv_b", "w_down", "final_norm_g")


def kernel(x, attn_norm_g, w_in, dn_conv_w, dn_a_log, dn_dt_bias, dn_out_norm_g, sg_norm_g, sg_w, sg_b, w_out, ffn_norm_g, w_up, ffn_conv_w, ffn_conv_b, w_down, final_norm_g, loss_target, m_attn_norm_g, m_w_in, m_dn_conv_w, m_dn_a_log, m_dn_dt_bias, m_dn_out_norm_g, m_sg_norm_g, m_sg_w, m_sg_b, m_w_out, m_ffn_norm_g, m_w_up, m_ffn_conv_w, m_ffn_conv_b, m_w_down, m_final_norm_g, v_attn_norm_g, v_w_in, v_dn_conv_w, v_dn_a_log, v_dn_dt_bias, v_dn_out_norm_g, v_sg_norm_g, v_sg_w, v_sg_b, v_w_out, v_ffn_norm_g, v_w_up, v_ffn_conv_w, v_ffn_conv_b, v_w_down, v_final_norm_g):
    W = dict(attn_norm_g=attn_norm_g, w_in=w_in, dn_conv_w=dn_conv_w, dn_a_log=dn_a_log, dn_dt_bias=dn_dt_bias,
             dn_out_norm_g=dn_out_norm_g, sg_norm_g=sg_norm_g, sg_w=sg_w, sg_b=sg_b, w_out=w_out,
             ffn_norm_g=ffn_norm_g, w_up=w_up, ffn_conv_w=ffn_conv_w, ffn_conv_b=ffn_conv_b, w_down=w_down,
             final_norm_g=final_norm_g)
    Mo = dict(attn_norm_g=m_attn_norm_g, w_in=m_w_in, dn_conv_w=m_dn_conv_w, dn_a_log=m_dn_a_log,
              dn_dt_bias=m_dn_dt_bias, dn_out_norm_g=m_dn_out_norm_g, sg_norm_g=m_sg_norm_g, sg_w=m_sg_w,
              sg_b=m_sg_b, w_out=m_w_out, ffn_norm_g=m_ffn_norm_g, w_up=m_w_up, ffn_conv_w=m_ffn_conv_w,
              ffn_conv_b=m_ffn_conv_b, w_down=m_w_down, final_norm_g=m_final_norm_g)
    Vo = dict(attn_norm_g=v_attn_norm_g, w_in=v_w_in, dn_conv_w=v_dn_conv_w, dn_a_log=v_dn_a_log,
              dn_dt_bias=v_dn_dt_bias, dn_out_norm_g=v_dn_out_norm_g, sg_norm_g=v_sg_norm_g, sg_w=v_sg_w,
              sg_b=v_sg_b, w_out=v_w_out, ffn_norm_g=v_ffn_norm_g, w_up=v_w_up, ffn_conv_w=v_ffn_conv_w,
              ffn_conv_b=v_ffn_conv_b, w_down=v_w_down, final_norm_g=v_final_norm_g)
    xi, yi, ci = lax.axis_index("x"), lax.axis_index("y"), lax.axis_index("c")
    chip = 2 * xi + yi

    me_lin = 4 * xi + 2 * yi + ci

    g_in, g_dnc = _gather_first(w_in[0].astype(BF16), dn_conv_w[0])
    def start_gather(name, shards, after):
        lands = [lax.dynamic_update_index_in_dim(lax.empty((4,) + s.shape, s.dtype), s, chip, 0) for s in shards]
        return _transfer_start(name, shards, lands, 3 * len(shards), _gather_copies, after=after)

    mid = start_gather("gather_mid_start", [w_out[0].astype(BF16), w_up[0].astype(BF16), ffn_conv_w[0]], g_in)
    last = start_gather("gather_last_start", [w_down[0].astype(BF16)], mid[4])
    token = last[4]

    def late_weights(stage, after):
        if stage == "out_proj":
            _, (g_out, g_up, g_ffc) = _transfer_wait("gather_mid_wait", *mid[:4], _gather_copies, after)
            return dict(w_out=g_out.reshape(D_MODEL, D_MODEL), ffn_conv_w=g_ffc.transpose(1, 0, 2).reshape(3, 2 * D_FF),
                        w_up=g_up.transpose(1, 0, 2).reshape(D_MODEL, 2 * D_FF))
        _, (g_down,) = _transfer_wait("gather_last_wait", *last[:4], _gather_copies, after)
        return dict(w_down=g_down.reshape(D_FF, D_MODEL))

    full = dict(
        w_in=g_in,
        dn_conv_w=g_dnc.transpose(1, 0, 2).reshape(4, 3 * DN_WIDTH),
        attn_norm_g=attn_norm_g, dn_a_log=dn_a_log, dn_dt_bias=dn_dt_bias, dn_out_norm_g=dn_out_norm_g,
        sg_norm_g=sg_norm_g, sg_w=sg_w[0], sg_b=sg_b[0], ffn_norm_g=ffn_norm_g, ffn_conv_b=ffn_conv_b,
        final_norm_g=final_norm_g[None])

    pending = {}
    early_names = ("dn_out_norm_g", "sg_norm_g", "sg_w", "sg_b", "ffn_norm_g", "ffn_conv_w", "ffn_conv_b",
                   "final_norm_g")
    late_names = ("attn_norm_g", "dn_a_log", "dn_dt_bias", "dn_conv_w")

    def on_grad(name, gw):
        if name == "small_early":
            buf = _pack([gw[n] for n in early_names])
            land = lax.dynamic_update_index_in_dim(lax.empty((8,) + buf.shape, F32), buf, me_lin, 0)
            s_sem, r_sem, src, lands, tok = _transfer_start("small_early_start", [buf], [land], 7, _small_copies)
            pending[name] = (s_sem, r_sem, src, lands)
            return tok
        g8 = gw.reshape(8, -1, gw.shape[-1])
        land = lax.empty((7,) + g8.shape[1:], BF16)
        s_sem, r_sem, src, lands, tok = _transfer_start(f"reduce_{name}_start", [g8], [land], 7, _pieces_copies)
        pending[name] = (s_sem, r_sem, src, lands)
        return tok

    loss_row, grad_x, g = _local_step(x[0], loss_target[0], full, dep=token, late_weights=late_weights,
                                      on_grad=on_grad)

    small_names = REPL + CONV
    late_all = _exchange_small(_pack([g[n] for n in late_names] + [loss_row]))
    late_sum = _sum_slots("sum_small_late", late_all[0], late_all[1:])
    s_sem, r_sem, src, lands = pending["small_early"]
    _, (early_all,) = _transfer_wait("small_early_wait", s_sem, r_sem, src, lands, _small_copies, grad_x)
    early_sum = _sum_slots("sum_small_early", early_all[0], early_all[1:])
    *late_vals, loss_sum = _unpack(late_sum, [g[n].shape for n in late_names] + [loss_row.shape])
    loss = loss_sum[0, 0]
    sg = dict(zip(late_names, late_vals))
    sg.update(zip(early_names, _unpack(early_sum, [g[n].shape for n in early_names])))
    sg["dn_conv_w"] = lax.dynamic_slice_in_dim(sg["dn_conv_w"], chip * (3 * DN_WIDTH // 4), 3 * DN_WIDTH // 4, axis=1)
    sg["ffn_conv_w"] = lax.dynamic_slice_in_dim(sg["ffn_conv_w"], chip * (2 * D_FF // 4), 2 * D_FF // 4, axis=1)

    halves = []
    for n in ("w_down", "w_up", "w_out", "w_in"):
        s_sem, r_sem, src, lands = pending[n]
        sent, got = _transfer_wait(f"reduce_{n}_wait", s_sem, r_sem, src, lands, _pieces_copies, grad_x)
        own = lax.dynamic_index_in_dim(sent[0], me_lin, axis=0, keepdims=False)
        halves.append(_sum_slots(f"sum_{n}", own, got[0]))
    theirs = _pair_swap(halves)
    core = ci.astype(jnp.int32).reshape(1)
    grads, delta, new_m, new_v = {}, {}, {}, {}
    for n, mine_h, their_h in zip(("w_down", "w_up", "w_out", "w_in"), halves, theirs):
        shp = W[n].shape
        if n == "w_in":
            to_t = lambda a: a.reshape(shp[1] // LANES, LANES, shp[2]).transpose(2, 0, 1)
            from_t = lambda a: a.transpose(1, 2, 0).reshape(shp)
            outs = _adamw_transposed(f"adamw_{n}", to_t(W[n]), mine_h, their_h, to_t(Mo[n]), to_t(Vo[n]), core)
            grads[n], delta[n], new_m[n], new_v[n] = (from_t(o) for o in outs)
            continue
        gr, d, mn, vn = _adamw_halves(f"adamw_{n}", W[n][0], mine_h, their_h, Mo[n][0], Vo[n][0], core)
        grads[n], delta[n], new_m[n], new_v[n] = gr.reshape(shp), d.reshape(shp), mn.reshape(shp), vn.reshape(shp)
    shapes = [W[n].shape for n in small_names]
    for n in small_names:
        grads[n] = sg[n].reshape(W[n].shape)
    d, mn, vn = _adamw("adamw_small", _pack([W[n] for n in small_names]), _pack([grads[n] for n in small_names]),
                       _pack([Mo[n] for n in small_names]), _pack([Vo[n] for n in small_names]))
    for dst, buf in ((delta, d), (new_m, mn), (new_v, vn)):
        dst.update(zip(small_names, _unpack(buf, shapes)))

    return (loss, grad_x[None], *[grads[n] for n in ORDER], *[delta[n] for n in ORDER],
            *[new_m[n] for n in ORDER], *[new_v[n] for n in ORDER])
```

```python
import functools
import math

import jax
import jax.numpy as jnp
from jax import lax
from jax.experimental import pallas as pl
from jax.experimental.pallas import tpu as pltpu

F32 = jnp.float32
BF16 = jnp.bfloat16

D_MODEL = 1024
CHUNK = 64
SCAN_CHUNKS = 4
SCAN_CHUNKS_FWD = 8
HEAD_DIM = 128
N_HEADS = 4
DN_WIDTH = 512
SG_WIDTH = 512
SG_GROUPS = 4
SG_BLOCK = 128
D_FF = 2816
PROJ_COLS = 3080
PROJ_PAD = 3200
BA_COL = 3072
EPS = 1e-6
NEG = -1e30
VMEM_LIMIT = 56 * 1024 * 1024

ADAM_LR = 0.001
ADAM_B1 = 0.9
ADAM_B2 = 0.999
ADAM_EPS = 1e-08
ADAM_WD = 0.01
ADAM_STEP = 10

MESH = pl.DeviceIdType.MESH
ANY = pl.BlockSpec(memory_space=pl.ANY)


def _cp(*sem):
    return pltpu.CompilerParams(dimension_semantics=sem, vmem_limit_bytes=VMEM_LIMIT)


def _bf(a):
    return a.astype(BF16)


def _nn(a, b):
    return jnp.dot(_bf(a), _bf(b), preferred_element_type=F32)


def _nt(a, b):
    return lax.dot_general(_bf(a), _bf(b), (((1,), (1,)), ((), ())), preferred_element_type=F32)


def _tn(a, b):
    return lax.dot_general(_bf(a), _bf(b), (((0,), (0,)), ((), ())), preferred_element_type=F32)


def _split(a):
    hi = _bf(a)
    return hi, _bf(a - hi.astype(F32))


def _sigmoid(x):
    return 0.5 * jnp.tanh(0.5 * x) + 0.5


def _silu(x):
    return x * _sigmoid(x)


def _dsilu(x):
    s = _sigmoid(x)
    return s * (1.0 + x * (1.0 - s))


_GELU_C = math.sqrt(2.0 / math.pi)
_GELU_A = 0.044715


def _gelu(x):
    return 0.5 * x * (1.0 + jnp.tanh(_GELU_C * (x + _GELU_A * x * x * x)))


def _dgelu(x):
    t = jnp.tanh(_GELU_C * (x + _GELU_A * x * x * x))
    return 0.5 * (1.0 + t) + 0.5 * x * (1.0 - t * t) * _GELU_C * (1.0 + 3.0 * _GELU_A * x * x)


def _softplus(x):
    return jnp.maximum(x, 0.0) + jnp.log(1.0 + jnp.exp(-jnp.abs(x)))


def _mm_nn(name, a, b, out_dtype, tm, tn, res=None):
    M, K = a.shape
    N = b.shape[1]
    tm, tn = min(tm, M), min(tn, N)

    def body(*refs):
        a_ref, b_ref = refs[0], refs[1]
        o_ref = refs[-1]
        acc = _nn(a_ref[...], b_ref[...])
        if res is not None:
            acc = acc + refs[2][...]
        o_ref[...] = acc.astype(o_ref.dtype)

    in_specs = [pl.BlockSpec((tm, K), lambda j, i: (i, 0)), pl.BlockSpec((K, tn), lambda j, i: (0, j))]
    args = [a, b]
    if res is not None:
        in_specs.append(pl.BlockSpec((tm, tn), lambda j, i: (i, j)))
        args.append(res)
    return pl.pallas_call(
        body, name=name, grid=(N // tn, M // tm), in_specs=in_specs,
        out_specs=pl.BlockSpec((tm, tn), lambda j, i: (i, j)),
        out_shape=jax.ShapeDtypeStruct((M, N), out_dtype),
        compiler_params=_cp("parallel", "parallel"))(*args)


def _with_dep(in_specs, args, dep):
    if dep is None:
        return in_specs, args
    return in_specs + [ANY], args + [dep]


SUB_ROWS = 128


def _sub_blocks(tm):
    return [slice(r0, min(r0 + SUB_ROWS, tm)) for r0 in range(0, tm, SUB_ROWS)]


def _rms_hat(xv):
    r = lax.rsqrt(jnp.mean(xv * xv, axis=-1, keepdims=True) + EPS)
    return xv * r, r


def _rms_bwd_vals(dh, xh, r, g):
    dxh = dh * g
    return r * (dxh - xh * jnp.mean(dxh * xh, axis=-1, keepdims=True)), jnp.sum(dh * xh, axis=0, keepdims=True)


def _in_proj(x, g, w4, tm=512, dep=None):
    T, K = x.shape
    ng, _, wc = w4.shape
    tm = min(tm, T)

    def body(x_ref, g_ref, w4_ref, *rest):
        p_ref, h_ref, w_ref = rest[-3:]

        @pl.when(pl.program_id(0) == 0)
        def _():
            w_ref[:, ng * wc:] = jnp.zeros((K, PROJ_PAD - ng * wc), BF16)
            for j in range(ng):
                w_ref[:, j * wc:(j + 1) * wc] = w4_ref[j]
        for r in _sub_blocks(tm):
            xh, _ = _rms_hat(x_ref[r, :])
            h_ref[r, :] = (xh * g_ref[...]).astype(BF16)
        p_ref[...] = jnp.dot(h_ref[...], w_ref[...], preferred_element_type=F32)

    in_specs, args = _with_dep(
        [pl.BlockSpec((tm, K), lambda i: (i, 0)), pl.BlockSpec((1, K), lambda i: (0, 0)),
         pl.BlockSpec((ng, K, wc), lambda i: (0, 0, 0))], [x, g, w4], dep)
    return pl.pallas_call(
        body, name="in_proj", grid=(T // tm,), in_specs=in_specs,
        out_specs=[pl.BlockSpec((tm, PROJ_PAD), lambda i: (i, 0)), pl.BlockSpec((tm, K), lambda i: (i, 0)),
                   pl.BlockSpec((K, PROJ_PAD), lambda i: (0, 0))],
        out_shape=[jax.ShapeDtypeStruct((T, PROJ_PAD), F32), jax.ShapeDtypeStruct((T, K), BF16),
                   jax.ShapeDtypeStruct((K, PROJ_PAD), BF16)],
        compiler_params=_cp("arbitrary"))(*args)


def _out_proj(mix, w, x, g, tm=512):
    T, K = mix.shape
    Dm = w.shape[1]
    tm = min(tm, T)

    def body(a_ref, w_ref, x_ref, g_ref, x2_ref, h_ref):
        x2_ref[...] = _nn(a_ref[...], w_ref[...]) + x_ref[...]
        for r in _sub_blocks(tm):
            xh, _ = _rms_hat(x2_ref[r, :])
            h_ref[r, :] = (xh * g_ref[...]).astype(BF16)

    row = lambda width: pl.BlockSpec((tm, width), lambda i: (i, 0))
    return pl.pallas_call(
        body, name="out_proj", grid=(T // tm,),
        in_specs=[row(K), pl.BlockSpec((K, Dm), lambda i: (0, 0)), row(Dm), pl.BlockSpec((1, Dm), lambda i: (0, 0))],
        out_specs=[row(Dm), row(Dm)],
        out_shape=[jax.ShapeDtypeStruct((T, Dm), F32), jax.ShapeDtypeStruct((T, Dm), BF16)],
        compiler_params=_cp("parallel"))(mix, w, x, g)


def _down_proj_loss(act, w, x2, tgt, g, tm=512):
    T, K = act.shape
    Dm = w.shape[1]
    tm = min(tm, T)

    def body(a_ref, w_ref, x_ref, t_ref, g_ref, loss_ref, dx_ref, gg_ref):
        @pl.when(pl.program_id(0) == 0)
        def _():
            gg_ref[...] = jnp.zeros_like(gg_ref)
            loss_ref[...] = jnp.zeros_like(loss_ref)
        dx_ref[...] = _nn(a_ref[...], w_ref[...]) + x_ref[...]
        for r in _sub_blocks(tm):
            xh, rr = _rms_hat(dx_ref[r, :])
            e = xh * g_ref[...] - t_ref[r, :]
            loss_ref[...] += jnp.zeros_like(loss_ref) + (0.5 / Dm) * jnp.sum(e * e)
            dx, gg = _rms_bwd_vals(e * (1.0 / Dm), xh, rr, g_ref[...])
            dx_ref[r, :] = dx
            gg_ref[...] += gg

    row = lambda width: pl.BlockSpec((tm, width), lambda i: (i, 0))
    vec = pl.BlockSpec((1, Dm), lambda i: (0, 0))
    return pl.pallas_call(
        body, name="down_proj_loss", grid=(T // tm,),
        in_specs=[row(K), pl.BlockSpec((K, Dm), lambda i: (0, 0)), row(Dm), row(Dm), vec],
        out_specs=[pl.BlockSpec((1, 128), lambda i: (0, 0)), row(Dm), vec],
        out_shape=[jax.ShapeDtypeStruct((1, 128), F32), jax.ShapeDtypeStruct((T, Dm), F32),
                   jax.ShapeDtypeStruct((1, Dm), F32)],
        compiler_params=_cp("arbitrary"))(act, w, x2, tgt, g)


def _mm_nt_rms_bwd(name, a, b, x, g, dres, tm=512, dep=None):
    M, K = a.shape
    Dm = b.shape[0]
    tm = min(tm, M)

    def body(a_ref, b_ref, x_ref, g_ref, dres_ref, *rest):
        dx_ref, gg_ref = rest[-2:]

        @pl.when(pl.program_id(0) == 0)
        def _():
            gg_ref[...] = jnp.zeros_like(gg_ref)
        dx_ref[...] = _nt(a_ref[...], b_ref[...])
        for r in _sub_blocks(tm):
            xh, rr = _rms_hat(x_ref[r, :])
            dx, gg = _rms_bwd_vals(dx_ref[r, :], xh, rr, g_ref[...])
            dx_ref[r, :] = dres_ref[r, :] + dx
            gg_ref[...] += gg

    row = lambda width: pl.BlockSpec((tm, width), lambda i: (i, 0))
    vec = pl.BlockSpec((1, Dm), lambda i: (0, 0))
    in_specs, args = _with_dep([row(K), pl.BlockSpec((Dm, K), lambda i: (0, 0)), row(Dm), vec, row(Dm)],
                               [a, b, x, g, dres], dep)
    return pl.pallas_call(
        body, name=name, grid=(M // tm,), in_specs=in_specs, out_specs=[row(Dm), vec],
        out_shape=[jax.ShapeDtypeStruct((M, Dm), F32), jax.ShapeDtypeStruct((1, Dm), F32)],
        compiler_params=_cp("arbitrary"))(*args)


def _mm_nt(name, a, b, out_dtype, tm, tn, dep=None):
    M, K = a.shape
    N = b.shape[0]
    tm, tn = min(tm, M), min(tn, N)

    def body(a_ref, b_ref, *rest):
        o_ref = rest[-1]
        o_ref[...] = _nt(a_ref[...], b_ref[...]).astype(o_ref.dtype)

    in_specs, args = _with_dep(
        [pl.BlockSpec((tm, K), lambda i, j: (i, 0)), pl.BlockSpec((tn, K), lambda i, j: (j, 0))], [a, b], dep)
    return pl.pallas_call(
        body, name=name, grid=(M // tm, N // tn), in_specs=in_specs,
        out_specs=pl.BlockSpec((tm, tn), lambda i, j: (i, j)),
        out_shape=jax.ShapeDtypeStruct((M, N), out_dtype),
        compiler_params=_cp("parallel", "parallel"))(*args)


def _mm_tn(name, a, b, tm, tn, tk, col_major_tiles=False, col_groups=None):
    T, M = a.shape
    N = b.shape[1]
    tm, tn, tk = min(tm, M), min(tn, N), min(tk, T)
    nk = T // tk

    def body(a_ref, b_ref, o_ref, acc_ref):
        k = pl.program_id(2)

        @pl.when(k == 0)
        def _():
            acc_ref[...] = jnp.zeros_like(acc_ref)
        acc_ref[...] += _tn(a_ref[...], b_ref[...])

        @pl.when(k == nk - 1)
        def _():
            if col_groups:
                for j in range(col_groups[0]):
                    o_ref[j] = acc_ref[:, j * col_groups[1]:(j + 1) * col_groups[1]].astype(BF16)
            else:
                o_ref[...] = acc_ref[...].astype(BF16).reshape(o_ref.shape)

    if col_groups:
        assert tm == M and tn == N and col_groups[0] * col_groups[1] <= N
        out_spec = pl.BlockSpec((col_groups[0], M, col_groups[1]), lambda i, j, k: (0, 0, 0))
        out_shape = jax.ShapeDtypeStruct((col_groups[0], M, col_groups[1]), BF16)
    elif col_major_tiles:
        assert tm == M
        out_spec = pl.BlockSpec((1, tm, tn), lambda i, j, k: (j, 0, 0))
        out_shape = jax.ShapeDtypeStruct((N // tn, M, tn), BF16)
    else:
        out_spec = pl.BlockSpec((tm, tn), lambda i, j, k: (i, j))
        out_shape = jax.ShapeDtypeStruct((M, N), BF16)
    return pl.pallas_call(
        body, name=name, grid=(M // tm, N // tn, nk),
        in_specs=[pl.BlockSpec((tk, tm), lambda i, j, k: (k, i)), pl.BlockSpec((tk, tn), lambda i, j, k: (k, j))],
        out_specs=out_spec, out_shape=out_shape, scratch_shapes=[pltpu.VMEM((tm, tn), F32)],
        compiler_params=_cp("parallel", "parallel", "arbitrary"))(a, b)


def _halo_prev_spec(rb, width):
    return pl.BlockSpec((8, width), lambda i: (jnp.maximum(i * (rb // 8) - 1, 0), 0))


def _halo_next_spec(rb, width, T):
    return pl.BlockSpec((8, width), lambda i: (jnp.minimum((i + 1) * (rb // 8), T // 8 - 1), 0))


LANES = 128
FF_STRIPS = D_FF // LANES
ROW_CHUNK = 32


def _strip(j, base=0):
    return pl.ds(pl.multiple_of(base + j * LANES, LANES), LANES)


def _ffn_act(up, w, b, rb=256):
    T, W = up.shape
    rb = min(rb, T)

    def body(up_ref, halo_ref, w_ref, b_ref, act_ref, ext_scr):
        first = pl.program_id(0) == 0

        def strip(j, slot):
            halves = (_strip(j), _strip(j, D_FF))
            wv = [w_ref[:, cols] for cols in halves]
            bv = [b_ref[:, cols] for cols in halves]
            for h, cols in enumerate(halves):
                ext_scr[slot, h,0:8] = jnp.where(first, 0.0, halo_ref[:, cols])
                ext_scr[slot, h,8:] = up_ref[:, cols]
            for r0 in range(0, rb, ROW_CHUNK):
                n = min(ROW_CHUNK, rb - r0)
                c = [ext_scr[slot, h,6 + r0:6 + r0 + n] * wv[h][0:1] + ext_scr[slot, h,7 + r0:7 + r0 + n] * wv[h][1:2]
                     + ext_scr[slot, h,8 + r0:8 + r0 + n] * wv[h][2:3] + bv[h] for h in range(2)]
                act_ref[r0:r0 + n, halves[0]] = (_silu(c[0]) * c[1]).astype(BF16)

        def pair(jj, carry):
            strip(2 * jj, 0)
            strip(2 * jj + 1, 1)
            return carry

        lax.fori_loop(0, FF_STRIPS // 2, pair, 0)

    return pl.pallas_call(
        body, name="ffn_act", grid=(T // rb,),
        in_specs=[pl.BlockSpec((rb, W), lambda i: (i, 0)), _halo_prev_spec(rb, W),
                  pl.BlockSpec((3, W), lambda i: (0, 0)), pl.BlockSpec((1, W), lambda i: (0, 0))],
        out_specs=pl.BlockSpec((rb, D_FF), lambda i: (i, 0)),
        out_shape=jax.ShapeDtypeStruct((T, D_FF), BF16),
        scratch_shapes=[pltpu.VMEM((2, 2, rb + 8, LANES), F32)], compiler_params=_cp("parallel"))(up, up, w, b)


def _ffn_act_bwd(up, dact, w, b, rb=128, dep=None):
    T, W = up.shape
    rb = min(rb, T)
    nb = T // rb
    re = rb + 8

    def body(up_ref, prev_ref, next_ref, da_ref, danext_ref, w_ref, b_ref, *rest):
        dup_ref, gw_ref, gb_ref, ext_scr, dc_scr = rest[-5:]
        i = pl.program_id(0)

        @pl.when(i == 0)
        def _():
            gw_ref[...] = jnp.zeros_like(gw_ref)
            gb_ref[...] = jnp.zeros_like(gb_ref)
        last = i == nb - 1

        def fold8(a):
            return jnp.sum(a.reshape(a.shape[0] // 8, 8, LANES), axis=0)

        def strip(j, slot):
            halves = (_strip(j), _strip(j, D_FF))
            wv = [w_ref[:, cols] for cols in halves]
            bv = [b_ref[:, cols] for cols in halves]
            for h, cols in enumerate(halves):
                ext_scr[slot, h,0:8] = jnp.where(i > 0, prev_ref[:, cols], 0.0)
                ext_scr[slot, h,8:8 + rb] = up_ref[:, cols]
                ext_scr[slot, h,8 + rb:] = next_ref[:, cols]
            gb = [jnp.zeros((8, LANES), F32) for _ in range(2)]
            gw = [[jnp.zeros((8, LANES), F32) for _ in range(3)] for _ in range(2)]
            for r0 in range(0, re, ROW_CHUNK):
                n = min(ROW_CHUNK, re - r0)
                tp = [[ext_scr[slot, h,6 + k + r0:6 + k + r0 + n] for k in range(3)] for h in range(2)]
                c = [tp[h][0] * wv[h][0:1] + tp[h][1] * wv[h][1:2] + tp[h][2] * wv[h][2:3] + bv[h] for h in range(2)]
                if r0 < rb:
                    da = da_ref[r0:r0 + n, halves[0]]
                else:
                    da = jnp.where(last, 0.0, danext_ref[:, halves[0]])
                s = _sigmoid(c[0])
                gs = c[0] * s
                dcs = (da * c[1] * (s + gs * (1.0 - s)), da * gs)
                for h in range(2):
                    dc_scr[slot, h,r0:r0 + n] = dcs[h]
                    if r0 < rb:
                        gb[h] = gb[h] + fold8(dcs[h])
                        for k in range(3):
                            gw[h][k] = gw[h][k] + fold8(tp[h][k] * dcs[h])
            for r0 in range(0, rb, ROW_CHUNK):
                n = min(ROW_CHUNK, rb - r0)
                for h, cols in enumerate(halves):
                    dup = (dc_scr[slot, h,r0:r0 + n] * wv[h][2:3] + dc_scr[slot, h,r0 + 1:r0 + 1 + n] * wv[h][1:2]
                           + dc_scr[slot, h,r0 + 2:r0 + 2 + n] * wv[h][0:1])
                    dup_ref[r0:r0 + n, cols] = dup.astype(BF16)
            for h, cols in enumerate(halves):
                gb_ref[:, cols] += jnp.sum(gb[h], axis=0, keepdims=True)
                for k in range(3):
                    gw_ref[k:k + 1, cols] += jnp.sum(gw[h][k], axis=0, keepdims=True)

        def pair(jj, carry):
            strip(2 * jj, 0)
            strip(2 * jj + 1, 1)
            return carry

        lax.fori_loop(0, FF_STRIPS // 2, pair, 0)

    in_specs, args = _with_dep(
        [pl.BlockSpec((rb, W), lambda i: (i, 0)), _halo_prev_spec(rb, W), _halo_next_spec(rb, W, T),
         pl.BlockSpec((rb, D_FF), lambda i: (i, 0)), _halo_next_spec(rb, D_FF, T),
         pl.BlockSpec((3, W), lambda i: (0, 0)), pl.BlockSpec((1, W), lambda i: (0, 0))],
        [up, up, up, dact, dact, w, b], dep)
    return pl.pallas_call(
        body, name="ffn_act_bwd", grid=(nb,), in_specs=in_specs,
        out_specs=[pl.BlockSpec((rb, W), lambda i: (i, 0)), pl.BlockSpec((3, W), lambda i: (0, 0)),
                   pl.BlockSpec((1, W), lambda i: (0, 0))],
        out_shape=[jax.ShapeDtypeStruct((T, W), BF16), jax.ShapeDtypeStruct((3, W), F32),
                   jax.ShapeDtypeStruct((1, W), F32)],
        scratch_shapes=[pltpu.VMEM((2, 2, rb + 16, LANES), F32), pltpu.VMEM((2, 2, re, LANES), F32)],
        compiler_params=_cp("arbitrary"))(*args)


def _lane_iota(shape):
    return lax.broadcasted_iota(jnp.int32, shape, len(shape) - 1)


def _dn_act(p, conv_w, alog_row, dtb_row, rb=256):
    T = p.shape[0]
    rb = min(rb, T)
    W3 = 3 * DN_WIDTH

    def body(p_ref, halo_ref, ba_ref, w_ref, al_ref, dt_ref, q_ref, k_ref, v_ref, bg_ref, ext_scr):
        first = pl.program_id(0) == 0
        outs = (q_ref, k_ref, v_ref)
        for j in range(3 * N_HEADS):
            kind, h = divmod(j, N_HEADS)
            cols = slice(j * HEAD_DIM, (j + 1) * HEAD_DIM)
            cur = p_ref[:, cols]
            ext_scr[j, 0:8] = jnp.where(first, 0.0, halo_ref[:, cols])
            ext_scr[j, 8:] = cur
            wv = w_ref[:, cols]
            s = _silu(ext_scr[j, 5:5 + rb] * wv[0:1] + ext_scr[j, 6:6 + rb] * wv[1:2]
                      + ext_scr[j, 7:7 + rb] * wv[2:3] + cur * wv[3:4])
            if kind < 2:
                scale = HEAD_DIM ** -0.5 if kind == 0 else 1.0
                s = s * (lax.rsqrt(jnp.sum(s * s, axis=-1, keepdims=True) + EPS) * scale)
            outs[kind][:, h * HEAD_DIM:(h + 1) * HEAD_DIM] = s
        ba = ba_ref[...]
        lane = _lane_iota(ba.shape)
        beta = _sigmoid(ba)
        g = -jnp.exp(al_ref[...]) * _softplus(ba + dt_ref[...])
        bg_ref[...] = jnp.where(lane < N_HEADS, beta, jnp.where(lane < 2 * N_HEADS, g, 0.0))

    row512 = pl.BlockSpec((rb, DN_WIDTH), lambda i: (i, 0))
    row128 = pl.BlockSpec((rb, 128), lambda i: (i, 0))
    vec128 = pl.BlockSpec((1, 128), lambda i: (0, 0))
    return pl.pallas_call(
        body, name="dn_act", grid=(T // rb,),
        in_specs=[pl.BlockSpec((rb, W3), lambda i: (i, 0)), _halo_prev_spec(rb, W3),
                  pl.BlockSpec((rb, 128), lambda i: (i, BA_COL // 128)),
                  pl.BlockSpec((4, W3), lambda i: (0, 0)), vec128, vec128],
        out_specs=[row512, row512, row512, row128],
        out_shape=[jax.ShapeDtypeStruct((T, DN_WIDTH), F32)] * 3 + [jax.ShapeDtypeStruct((T, 128), F32)],
        scratch_shapes=[pltpu.VMEM((3 * N_HEADS, rb + 8, HEAD_DIM), F32)],
        compiler_params=_cp("parallel"))(p, p, p, conv_w, alog_row, dtb_row)


def _dn_act_bwd(p, conv_w, alog_row, dtb_row, dq, dk, dv, dbg, dp_mid, rb=256):
    T = p.shape[0]
    rb = min(rb, T)
    nb = T // rb
    re = rb + 8
    W3 = 3 * DN_WIDTH

    def body(p_ref, prev_ref, next_ref, ba_ref, w_ref, al_ref, dt_ref, dq_ref, dqn_ref, dk_ref, dkn_ref,
             dv_ref, dvn_ref, dbg_ref, mid_ref, draw_ref, gw_ref, gad_ref, ext_scr, dc_scr):
        i = pl.program_id(0)
        draw_ref[:, W3:2 * W3] = mid_ref[...]

        @pl.when(i == 0)
        def _():
            gw_ref[...] = jnp.zeros_like(gw_ref)
            gad_ref[...] = jnp.zeros_like(gad_ref)
        row = lax.broadcasted_iota(jnp.int32, (re, 1), 0)
        live = (row < rb) | (i < nb - 1)
        d_refs = ((dq_ref, dqn_ref), (dk_ref, dkn_ref), (dv_ref, dvn_ref))
        for j in range(3 * N_HEADS):
            kind, h = divmod(j, N_HEADS)
            cols = slice(j * HEAD_DIM, (j + 1) * HEAD_DIM)
            hcols = slice(h * HEAD_DIM, (h + 1) * HEAD_DIM)
            ext_scr[j, 0:8] = jnp.where(i > 0, prev_ref[:, cols], 0.0)
            ext_scr[j, 8:8 + rb] = p_ref[:, cols]
            ext_scr[j, 8 + rb:] = next_ref[:, cols]
            tp = [ext_scr[j, 5 + k:5 + k + re] for k in range(4)]
            wv = w_ref[:, cols]
            c = tp[0] * wv[0:1] + tp[1] * wv[1:2] + tp[2] * wv[2:3] + tp[3] * wv[3:4]
            sg = _sigmoid(c)
            s = c * sg
            d_in = jnp.where(live, jnp.concatenate([d_refs[kind][0][:, hcols], d_refs[kind][1][:, hcols]], axis=0), 0.0)
            if kind < 2:
                scale = HEAD_DIM ** -0.5 if kind == 0 else 1.0
                n = lax.rsqrt(jnp.sum(s * s, axis=-1, keepdims=True) + EPS)
                hat = s * n
                d_in = (n * scale) * (d_in - hat * jnp.sum(hat * d_in, axis=-1, keepdims=True))
            dc = d_in * (sg + s * (1.0 - sg))
            dc_scr[j] = dc
            dcc = dc[0:rb]
            draw = (dcc * wv[3:4] + dc_scr[j, 1:1 + rb] * wv[2:3] + dc_scr[j, 2:2 + rb] * wv[1:2]
                    + dc_scr[j, 3:3 + rb] * wv[0:1])
            draw_ref[:, cols] = draw.astype(BF16)
            for k in range(4):
                gw_ref[k:k + 1, cols] += jnp.sum(tp[k][0:rb] * dcc, axis=0, keepdims=True)
        ba = ba_ref[...]
        dbg = dbg_ref[...]
        lane = _lane_iota(ba.shape)
        beta = _sigmoid(ba)
        ea = jnp.exp(al_ref[...])
        z = ba + dt_ref[...]
        d_a = dbg * (-ea) * _sigmoid(z)
        dba = jnp.where(lane < N_HEADS, dbg * beta * (1.0 - beta), jnp.where(lane < 2 * N_HEADS, d_a, 0.0))
        draw_ref[:, BA_COL:] = dba.astype(BF16)
        isg = (lane >= N_HEADS) & (lane < 2 * N_HEADS)
        g = -ea * _softplus(z)
        gad_ref[0:1, :] += jnp.sum(jnp.where(isg, dbg * g, 0.0), axis=0, keepdims=True)
        gad_ref[1:2, :] += jnp.sum(jnp.where(isg, d_a, 0.0), axis=0, keepdims=True)

    row512 = pl.BlockSpec((rb, DN_WIDTH), lambda i: (i, 0))
    row128 = pl.BlockSpec((rb, 128), lambda i: (i, 0))
    vec128 = pl.BlockSpec((1, 128), lambda i: (0, 0))
    next512 = _halo_next_spec(rb, DN_WIDTH, T)
    return pl.pallas_call(
        body, name="dn_act_bwd", grid=(nb,),
        in_specs=[pl.BlockSpec((rb, W3), lambda i: (i, 0)), _halo_prev_spec(rb, W3), _halo_next_spec(rb, W3, T),
                  pl.BlockSpec((rb, 128), lambda i: (i, BA_COL // 128)),
                  pl.BlockSpec((4, W3), lambda i: (0, 0)), vec128, vec128,
                  row512, next512, row512, next512, row512, next512, row128,
                  pl.BlockSpec((rb, W3), lambda i: (i, 0))],
        out_specs=[pl.BlockSpec((rb, PROJ_PAD), lambda i: (i, 0)),
                   pl.BlockSpec((4, W3), lambda i: (0, 0)), pl.BlockSpec((2, 128), lambda i: (0, 0))],
        out_shape=[jax.ShapeDtypeStruct((T, PROJ_PAD), BF16),
                   jax.ShapeDtypeStruct((4, W3), F32), jax.ShapeDtypeStruct((2, 128), F32)],
        scratch_shapes=[pltpu.VMEM((3 * N_HEADS, rb + 16, HEAD_DIM), F32), pltpu.VMEM((3 * N_HEADS, re, HEAD_DIM), F32)],
        compiler_params=_cp("arbitrary"))(p, p, p, p, conv_w, alog_row, dtb_row, dq, dq, dk, dk, dv, dv, dbg, dp_mid)


def _tri(incl):
    ii = lax.broadcasted_iota(jnp.int32, (CHUNK, CHUNK), 0)
    jj = lax.broadcasted_iota(jnp.int32, (CHUNK, CHUNK), 1)
    return ii, jj, ((ii >= jj) if incl else (ii > jj))


def _dn_chunk(k, bg, cb=4):
    T = k.shape[0]
    N = T // CHUNK
    cb = min(cb, N)

    def body(k_ref, bg_ref, gc_ref, gct_ref, l_ref):
        ii, jj, incl = _tri(True)
        tri = incl.astype(F32)
        U = range(cb)
        bgv = [bg_ref[u * CHUNK:(u + 1) * CHUNK, :] for u in U]
        gc = [jnp.dot(tri, bgv[u], precision=lax.Precision.HIGHEST, preferred_element_type=F32) for u in U]
        gct = [gc[u].T for u in U]
        kk = [[None] * N_HEADS for _ in U]
        for u in U:
            gc_ref[u * CHUNK:(u + 1) * CHUNK, :] = gc[u]
            gct_ref[u] = gct[u][0:8]
            for h in range(N_HEADS):
                kh = k_ref[u * CHUNK:(u + 1) * CHUNK, h * HEAD_DIM:(h + 1) * HEAD_DIM]
                kk[u][h] = _nt(kh * bgv[u][:, h:h + 1], kh)
        for u in U:
            for h in range(N_HEADS):
                gcol = gc[u][:, N_HEADS + h:N_HEADS + h + 1]
                grow = gct[u][N_HEADS + h:N_HEADS + h + 1, :]
                l_ref[u, h] = kk[u][h] * jnp.exp(jnp.where(ii > jj, gcol - grow, NEG))

    rows = cb * CHUNK
    return pl.pallas_call(
        body, name="dn_chunk", grid=(N // cb,),
        in_specs=[pl.BlockSpec((rows, DN_WIDTH), lambda n: (n, 0)), pl.BlockSpec((rows, 128), lambda n: (n, 0))],
        out_specs=[pl.BlockSpec((rows, 128), lambda n: (n, 0)), pl.BlockSpec((cb, 8, CHUNK), lambda n: (n, 0, 0)),
                   pl.BlockSpec((cb, N_HEADS, CHUNK, CHUNK), lambda n: (n, 0, 0, 0))],
        out_shape=[jax.ShapeDtypeStruct((T, 128), F32), jax.ShapeDtypeStruct((N, 8, CHUNK), F32),
                   jax.ShapeDtypeStruct((N, N_HEADS, CHUNK, CHUNK), F32)],
        compiler_params=_cp("parallel"))(k, bg)


def _tri_inv(lt):
    S = lt.shape[1]

    def body(l_ref, a_ref):
        sub = lax.broadcasted_iota(jnp.int32, (8, S), 0)
        groups = CHUNK // 8
        for i in range(CHUNK):
            acc = [((sub + 8 * k) == i).astype(F32) for k in range(groups)]
            for jb in range((i + 7) // 8):
                nk = jb + 1

                def step(j, carry, nk=nk, i=i):
                    lrow = l_ref[pl.ds(i * CHUNK + j, 1), :]
                    return tuple(carry[k] - lrow * a_ref[j, 8 * k:8 * k + 8, :] for k in range(nk))

                acc[:nk] = list(lax.fori_loop(8 * jb, min(8 * jb + 8, i), step, tuple(acc[:nk])))
            for k in range(groups):
                a_ref[i, 8 * k:8 * k + 8, :] = acc[k]

    return pl.pallas_call(
        body, name="tri_inv", out_shape=jax.ShapeDtypeStruct((CHUNK, CHUNK, S), F32),
        compiler_params=pltpu.CompilerParams(vmem_limit_bytes=VMEM_LIMIT))(lt)


def _dn_head_terms(qh, kh, vh, beta, gcol, grow):
    ii, jj, incl = _tri(True)
    gam = jnp.exp(jnp.where(incl, gcol - grow, NEG))
    glast = grow[:, CHUNK - 1:CHUNK]
    cd = jnp.exp(glast)
    shape = (CHUNK, HEAD_DIM)
    E = jnp.broadcast_to(jnp.exp(gcol), shape)
    Fd = jnp.broadcast_to(jnp.exp(glast - gcol), shape)
    beta = jnp.broadcast_to(beta, shape)
    kb = kh * beta
    return dict(ii=ii, jj=jj, gam=gam, E=E, F=Fd, beta=beta, cd=cd, kb=kb, vb=vh * beta, W=kb * E, qE=qh * E,
                kt=kh * Fd)


def _apply_a(a, u):
    hi, lo = _split(a)
    ub = _bf(u)
    return jnp.dot(hi, ub, preferred_element_type=F32) + jnp.dot(lo, ub, preferred_element_type=F32)


def _dn_scan(q, k, v, bg, gc, gct, a):
    T = q.shape[0]
    N = T // CHUNK
    cb = min(SCAN_CHUNKS_FWD, N)

    def body(q_ref, k_ref, v_ref, bg_ref, gc_ref, gct_ref, a_ref, o_ref, sall_ref, s_ref):
        @pl.when(pl.program_id(0) == 0)
        def _():
            s_ref[...] = jnp.zeros_like(s_ref)
        H = range(N_HEADS)
        sl = [slice(h * HEAD_DIM, (h + 1) * HEAD_DIM) for h in H]
        pre = []
        for u in range(cb):
            r = slice(u * CHUNK, (u + 1) * CHUNK)
            bgv, gcv, gctv = bg_ref[r, :], gc_ref[r, :], gct_ref[u]
            q_, k_ = [q_ref[r, s] for s in sl], [k_ref[r, s] for s in sl]
            t = [_dn_head_terms(q_[h], k_[h], v_ref[r, sl[h]], bgv[:, h:h + 1],
                                gcv[:, N_HEADS + h:N_HEADS + h + 1], gctv[N_HEADS + h:N_HEADS + h + 1, :]) for h in H]
            P = [_nt(q_[h], k_[h]) * t[h]["gam"] for h in H]
            pre.append((r, t, P))
        S = [s_ref[h] for h in H]
        for u in range(cb):
            r, t, P = pre[u]
            for h in H:
                sall_ref[u, h] = S[h]
            WS = [_nn(t[h]["W"], S[h]) for h in H]
            qS = [_nn(t[h]["qE"], S[h]) for h in H]
            vn = [_apply_a(a_ref[u, h], t[h]["vb"] - WS[h]) for h in H]
            Pv = [_nn(P[h], vn[h]) for h in H]
            kv = [_tn(t[h]["kt"], vn[h]) for h in H]
            for h in H:
                o_ref[r, sl[h]] = qS[h] + Pv[h]
            S = [t[h]["cd"] * S[h] + kv[h] for h in H]
        for h in H:
            s_ref[h] = S[h]

    row512 = pl.BlockSpec((cb * CHUNK, DN_WIDTH), lambda n: (n, 0))
    row128 = pl.BlockSpec((cb * CHUNK, 128), lambda n: (n, 0))
    return pl.pallas_call(
        body, name="dn_scan", grid=(N // cb,),
        in_specs=[row512, row512, row512, row128, row128, pl.BlockSpec((cb, 8, CHUNK), lambda n: (n, 0, 0)),
                  pl.BlockSpec((cb, N_HEADS, CHUNK, CHUNK), lambda n: (n, 0, 0, 0))],
        out_specs=[row512, pl.BlockSpec((cb, N_HEADS, HEAD_DIM, HEAD_DIM), lambda n: (n, 0, 0, 0))],
        out_shape=[jax.ShapeDtypeStruct((T, DN_WIDTH), F32),
                   jax.ShapeDtypeStruct((N, N_HEADS, HEAD_DIM, HEAD_DIM), F32)],
        scratch_shapes=[pltpu.VMEM((N_HEADS, HEAD_DIM, HEAD_DIM), F32)],
        compiler_params=_cp("arbitrary"))(q, k, v, bg, gc, gct, a)


def _dn_scan_bwd(q, k, v, bg, gc, gct, a, a_t, sall, do, dep=None):
    T = q.shape[0]
    N = T // CHUNK

    cb = min(SCAN_CHUNKS, N)
    nb = N // cb

    def body(q_ref, k_ref, v_ref, bg_ref, gc_ref, gct_ref, a_ref, at_ref, sall_ref, do_ref, *rest):
        dq_ref, dk_ref, dv_ref, dbg_ref, ds_ref = rest[-5:]
        @pl.when(pl.program_id(0) == 0)
        def _():
            ds_ref[...] = jnp.zeros_like(ds_ref)
        lane = _lane_iota((CHUNK, 128))
        rowi = lax.broadcasted_iota(jnp.int32, (CHUNK, 1), 0)
        ii, jj, _ = _tri(True)
        rev = (jj >= ii).astype(F32)
        H = range(N_HEADS)
        sl = [slice(h * HEAD_DIM, (h + 1) * HEAD_DIM) for h in H]
        pre = {}
        for u in reversed(range(cb)):
            r = slice(u * CHUNK, (u + 1) * CHUNK)
            bgv, gcv, gctv = bg_ref[r, :], gc_ref[r, :], gct_ref[u]
            q_, k_, v_ = [q_ref[r, s] for s in sl], [k_ref[r, s] for s in sl], [v_ref[r, s] for s in sl]
            dO = [do_ref[r, s] for s in sl]
            t = [_dn_head_terms(q_[h], k_[h], v_[h], bgv[:, h:h + 1], gcv[:, N_HEADS + h:N_HEADS + h + 1],
                                gctv[N_HEADS + h:N_HEADS + h + 1, :]) for h in H]
            beta = [t[h]["beta"] for h in H]
            S = [sall_ref[u, h] for h in H]
            A = [a_ref[u, h] for h in H]
            WS = [_nn(t[h]["W"], S[h]) for h in H]
            KK = [_nt(t[h]["kb"], k_[h]) for h in H]
            QK = [_nt(q_[h], k_[h]) for h in H]
            d_qE = [_nt(dO[h], S[h]) for h in H]
            vn = [_apply_a(A[h], t[h]["vb"] - WS[h]) for h in H]
            PtdO = [_tn(QK[h] * t[h]["gam"], dO[h]) for h in H]
            qEdO = [_tn(t[h]["qE"], dO[h]) for h in H]
            dOvn = [_nt(dO[h], vn[h]) for h in H]
            dQK = [jnp.where(ii >= jj, dOvn[h], 0.0) * t[h]["gam"] for h in H]
            dQKk = [_nn(dQK[h], k_[h]) for h in H]
            dQKq = [_tn(dQK[h], q_[h]) for h in H]
            pre[u] = (r, q_, k_, v_, beta, t, S, A, KK, QK, d_qE, vn, PtdO, qEdO, dQK, dQKk, dQKq)
        dSn = [ds_ref[h] for h in H]
        for u in reversed(range(cb)):
            r, q_, k_, v_, beta, t, S, A, KK, QK, d_qE, vn, PtdO, qEdO, dQK, dQKk, dQKq = pre[u]
            gam, E, Fd, cd, kb = ([t[h][n] for h in H] for n in ("gam", "E", "F", "cd", "kb"))
            ktdS = [_nn(t[h]["kt"], dSn[h]) for h in H]
            dU = [_apply_a(at_ref[u, h], PtdO[h] + ktdS[h]) for h in H]
            d_kt = [_nt(vn[h], dSn[h]) for h in H]
            dUvn = [_nt(dU[h], vn[h]) for h in H]
            dUS = [_nt(dU[h], S[h]) for h in H]
            WdU = [_tn(t[h]["W"], dU[h]) for h in H]
            d_cd = [jnp.sum(S[h] * dSn[h]) for h in H]
            dSn = [cd[h] * dSn[h] + qEdO[h] - WdU[h] for h in H]
            dKK = [jnp.where(ii > jj, -dUvn[h], 0.0) * gam[h] for h in H]
            dKKk = [_nn(dKK[h], k_[h]) for h in H]
            dKKkb = [_tn(dKK[h], kb[h]) for h in H]
            dbeta_arr = jnp.zeros((CHUNK, 128), F32)
            dgc_arr = jnp.zeros((CHUNK, 128), F32)
            for h in H:
                dW = -dUS[h]
                dq_ref[r, sl[h]] = dQKk[h] + d_qE[h] * E[h]
                d_kb = dKKk[h] + dW * E[h]
                dk_ref[r, sl[h]] = dQKq[h] + dKKkb[h] + d_kb * beta[h] + d_kt[h] * Fd[h]
                dv_ref[r, sl[h]] = dU[h] * beta[h]
                Z = dQK[h] * QK[h] + dKK[h] * KK[h]
                dbeta = jnp.sum(dU[h] * v_[h] + d_kb * k_[h], axis=-1, keepdims=True)
                m_e = (dW * kb[h] + d_qE[h] * q_[h]) * E[h]
                m_f = d_kt[h] * k_[h] * Fd[h]
                zdiag = jnp.where(ii == jj, jnp.sum(Z, axis=0, keepdims=True), 0.0)
                dgc = (jnp.sum(m_e - m_f, axis=-1, keepdims=True) + jnp.sum(Z - zdiag, axis=-1, keepdims=True)
                       + jnp.where(rowi == CHUNK - 1, jnp.sum(m_f) + d_cd[h] * cd[h], 0.0))
                dbeta_arr = dbeta_arr + jnp.where(lane == h, dbeta, 0.0)
                dgc_arr = dgc_arr + jnp.where(lane == N_HEADS + h, dgc, 0.0)
            dbg_ref[r, :] = dbeta_arr + jnp.dot(rev, dgc_arr, precision=lax.Precision.HIGHEST,
                                                preferred_element_type=F32)
        for h in H:
            ds_ref[h] = dSn[h]

    row512 = pl.BlockSpec((cb * CHUNK, DN_WIDTH), lambda n: (nb - 1 - n, 0))
    row128 = pl.BlockSpec((cb * CHUNK, 128), lambda n: (nb - 1 - n, 0))
    in_specs, args = _with_dep(
        [row512, row512, row512, row128, row128,
         pl.BlockSpec((cb, 8, CHUNK), lambda n: (nb - 1 - n, 0, 0)),
         pl.BlockSpec((cb, N_HEADS, CHUNK, CHUNK), lambda n: (nb - 1 - n, 0, 0, 0)),
         pl.BlockSpec((cb, N_HEADS, CHUNK, CHUNK), lambda n: (nb - 1 - n, 0, 0, 0)),
         pl.BlockSpec((cb, N_HEADS, HEAD_DIM, HEAD_DIM), lambda n: (nb - 1 - n, 0, 0, 0)), row512],
        [q, k, v, bg, gc, gct, a, a_t, sall, do], dep)
    return pl.pallas_call(
        body, name="dn_scan_bwd", grid=(nb,), in_specs=in_specs,
        out_specs=[row512, row512, row512, row128],
        out_shape=[jax.ShapeDtypeStruct((T, DN_WIDTH), F32)] * 3 + [jax.ShapeDtypeStruct((T, 128), F32)],
        scratch_shapes=[pltpu.VMEM((N_HEADS, HEAD_DIM, HEAD_DIM), F32)],
        compiler_params=_cp("arbitrary"))(*args)


def _sg_mask():
    ii = lax.broadcasted_iota(jnp.int32, (SG_BLOCK, SG_BLOCK), 0) // CHUNK
    jj = lax.broadcasted_iota(jnp.int32, (SG_BLOCK, SG_BLOCK), 1) // CHUNK
    return jj <= ii


def _mix_fwd(o, p, ong, sgn, sgw, sgbt):
    T = o.shape[0]
    rb = SG_BLOCK

    def body(o_ref, gate_ref, u_ref, vg_ref, ong_ref, sgn_ref, sgw_ref, sgbt_ref, mix_ref):
        mask = _sg_mask()
        gate = gate_ref[...]
        for h in range(N_HEADS):
            sl = slice(h * HEAD_DIM, (h + 1) * HEAD_DIM)
            oh = o_ref[:, sl]
            r = lax.rsqrt(jnp.mean(oh * oh, axis=-1, keepdims=True) + EPS)
            mix_ref[:, sl] = (oh * r * ong_ref[...] * _silu(gate[:, sl])).astype(BF16)
        for gi in range(SG_GROUPS):
            sl = slice(gi * SG_BLOCK, (gi + 1) * SG_BLOCK)
            gv = _gelu(vg_ref[:, sl])
            r = lax.rsqrt(jnp.mean(gv * gv, axis=-1, keepdims=True) + EPS)
            vh = gv * r * sgn_ref[:, sl]
            s = _nn(jnp.where(mask, sgw_ref[gi], 0.0), vh) + sgbt_ref[:, gi:gi + 1]
            mix_ref[:, DN_WIDTH + gi * SG_BLOCK:DN_WIDTH + (gi + 1) * SG_BLOCK] = (_gelu(u_ref[:, sl]) * s).astype(BF16)

    def col(c):
        return pl.BlockSpec((rb, 512), lambda i: (i, c))
    return pl.pallas_call(
        body, name="mix_fwd", grid=(T // rb,),
        in_specs=[pl.BlockSpec((rb, DN_WIDTH), lambda i: (i, 0)), col(3), col(4), col(5),
                  pl.BlockSpec((1, 128), lambda i: (0, 0)), pl.BlockSpec((1, SG_WIDTH), lambda i: (0, 0)),
                  pl.BlockSpec((SG_GROUPS, SG_BLOCK, SG_BLOCK), lambda i: (0, 0, 0)),
                  pl.BlockSpec((SG_BLOCK, 128), lambda i: (0, 0))],
        out_specs=pl.BlockSpec((rb, D_MODEL), lambda i: (i, 0)),
        out_shape=jax.ShapeDtypeStruct((T, D_MODEL), BF16),
        compiler_params=_cp("parallel"))(o, p, p, p, ong, sgn, sgw, sgbt)


def _mix_bwd(o, p, ong, sgn, sgw, sgbt, dmix, dep=None):
    T = o.shape[0]
    rb = SG_BLOCK

    def body(o_ref, gate_ref, u_ref, vg_ref, ong_ref, sgn_ref, sgw_ref, sgbt_ref, dmix_ref, *rest):
        do_ref, dp_ref, gong_ref, gsgn_ref, gsgw_ref, gsgbt_ref = rest[-6:]
        @pl.when(pl.program_id(0) == 0)
        def _():
            gong_ref[...] = jnp.zeros_like(gong_ref)
            gsgn_ref[...] = jnp.zeros_like(gsgn_ref)
            gsgw_ref[...] = jnp.zeros_like(gsgw_ref)
            gsgbt_ref[...] = jnp.zeros_like(gsgbt_ref)
        mask = _sg_mask()
        gate = gate_ref[...]
        lane = _lane_iota((SG_BLOCK, 128))
        for h in range(N_HEADS):
            sl = slice(h * HEAD_DIM, (h + 1) * HEAD_DIM)
            oh = o_ref[:, sl]
            dm = dmix_ref[:, sl]
            r = lax.rsqrt(jnp.mean(oh * oh, axis=-1, keepdims=True) + EPS)
            oh_hat = oh * r
            gt = gate[:, sl]
            sg = _silu(gt)
            dp_ref[:, sl] = (dm * oh_hat * ong_ref[...] * _dsilu(gt)).astype(BF16)
            dn_ = dm * sg
            gong_ref[...] += jnp.sum(dn_ * oh_hat, axis=0, keepdims=True)
            dhat = dn_ * ong_ref[...]
            do_ref[:, sl] = r * (dhat - oh_hat * jnp.mean(dhat * oh_hat, axis=-1, keepdims=True))
        for gi in range(SG_GROUPS):
            sl = slice(gi * SG_BLOCK, (gi + 1) * SG_BLOCK)
            vraw = vg_ref[:, sl]
            gv = _gelu(vraw)
            r = lax.rsqrt(jnp.mean(gv * gv, axis=-1, keepdims=True) + EPS)
            vhat = gv * r
            vn = vhat * sgn_ref[:, sl]
            wm = jnp.where(mask, sgw_ref[gi], 0.0)
            s = _nn(wm, vn) + sgbt_ref[:, gi:gi + 1]
            uraw = u_ref[:, sl]
            dm = dmix_ref[:, DN_WIDTH + gi * SG_BLOCK:DN_WIDTH + (gi + 1) * SG_BLOCK]
            dp_ref[:, DN_WIDTH + gi * SG_BLOCK:DN_WIDTH + (gi + 1) * SG_BLOCK] = (dm * s * _dgelu(uraw)).astype(BF16)
            ds = dm * _gelu(uraw)
            gsgbt_ref[...] += jnp.where(lane == gi, jnp.sum(ds, axis=-1, keepdims=True), 0.0)
            gsgw_ref[gi] += jnp.where(mask, _nt(ds, vn), 0.0)
            dvn = _tn(wm, ds)
            gsgn_ref[:, sl] += jnp.sum(dvn * vhat, axis=0, keepdims=True)
            dhat = dvn * sgn_ref[:, sl]
            dgv = r * (dhat - vhat * jnp.mean(dhat * vhat, axis=-1, keepdims=True))
            dp_ref[:, 2 * DN_WIDTH + gi * SG_BLOCK:2 * DN_WIDTH + (gi + 1) * SG_BLOCK] = (dgv * _dgelu(vraw)).astype(BF16)

    def col(c):
        return pl.BlockSpec((rb, 512), lambda i: (i, c))
    full = lambda *s: pl.BlockSpec(s, lambda i: (0,) * len(s))
    in_specs, args = _with_dep(
        [pl.BlockSpec((rb, DN_WIDTH), lambda i: (i, 0)), col(3), col(4), col(5),
         full(1, 128), full(1, SG_WIDTH), full(SG_GROUPS, SG_BLOCK, SG_BLOCK), full(SG_BLOCK, 128),
         pl.BlockSpec((rb, D_MODEL), lambda i: (i, 0))],
        [o, p, p, p, ong, sgn, sgw, sgbt, dmix], dep)
    return pl.pallas_call(
        body, name="mix_bwd", grid=(T // rb,), in_specs=in_specs,
        out_specs=[pl.BlockSpec((rb, DN_WIDTH), lambda i: (i, 0)), pl.BlockSpec((rb, 3 * 512), lambda i: (i, 0)),
                   full(1, 128), full(1, SG_WIDTH), full(SG_GROUPS, SG_BLOCK, SG_BLOCK), full(SG_BLOCK, 128)],
        out_shape=[jax.ShapeDtypeStruct((T, DN_WIDTH), F32), jax.ShapeDtypeStruct((T, 3 * 512), BF16),
                   jax.ShapeDtypeStruct((1, 128), F32), jax.ShapeDtypeStruct((1, SG_WIDTH), F32),
                   jax.ShapeDtypeStruct((SG_GROUPS, SG_BLOCK, SG_BLOCK), F32),
                   jax.ShapeDtypeStruct((SG_BLOCK, 128), F32)],
        compiler_params=_cp("arbitrary"))(*args)


def _pad_lanes(row, offset=0):
    n = row.shape[1]
    return jnp.pad(row, ((0, 0), (offset, 128 - n - offset)))


def _local_step(x, tgt, w, dep=None, late_weights=None, on_grad=None):
    T = x.shape[0]
    N = T // CHUNK
    on_grad = on_grad or (lambda name, g: None)
    alog_row = _pad_lanes(w["dn_a_log"], N_HEADS)
    dtb_row = _pad_lanes(w["dn_dt_bias"], N_HEADS)
    sgbt = jnp.pad(w["sg_b"].T, ((0, 0), (0, 128 - SG_GROUPS)))

    p, h1, w_in_pad = _in_proj(x, w["attn_norm_g"], w["w_in"], dep=dep)
    q, k, v, bg = _dn_act(p, w["dn_conv_w"], alog_row, dtb_row)
    gc, gct, lmat = _dn_chunk(k, bg)
    lt = lmat.reshape(N * N_HEADS, CHUNK * CHUNK).T
    at = _tri_inv(lt)
    a = at.reshape(CHUNK * CHUNK, N * N_HEADS).T.reshape(N, N_HEADS, CHUNK, CHUNK)
    a_t = at.transpose(1, 0, 2).reshape(CHUNK * CHUNK, N * N_HEADS).T.reshape(N, N_HEADS, CHUNK, CHUNK)
    o, sall = _dn_scan(q, k, v, bg, gc, gct, a)
    mix = _mix_fwd(o, p, w["dn_out_norm_g"], w["sg_norm_g"], w["sg_w"], sgbt)
    if late_weights is not None:
        w = {**w, **late_weights("out_proj", mix)}
    x2, h2 = _out_proj(mix, w["w_out"], x, w["ffn_norm_g"])
    up = _mm_nn("up_proj", h2, w["w_up"], F32, 512, D_FF)
    act = _ffn_act(up, w["ffn_conv_w"], w["ffn_conv_b"])
    if late_weights is not None:
        w = {**w, **late_weights("down_proj", act)}
    loss, dx3, g_final = _down_proj_loss(act, w["w_down"], x2, tgt, w["final_norm_g"])

    dact = _mm_nt("d_act", dx3, w["w_down"], F32, 512, D_FF)
    g_w_down = _mm_tn("g_w_down", act, dx3, D_FF, 1024, 1024)
    tok = on_grad("w_down", g_w_down)
    dup, g_ffn_conv_w, g_ffn_conv_b = _ffn_act_bwd(up, dact, w["ffn_conv_w"], w["ffn_conv_b"], dep=tok)
    g_w_up = _mm_tn("g_w_up", h2, dup, 1024, 2 * D_FF // 4, 2048, col_major_tiles=True)
    tok = on_grad("w_up", g_w_up)
    dx2, g_ffn_norm = _mm_nt_rms_bwd("d_h2", dup, w["w_up"], x2, w["ffn_norm_g"], dx3, dep=tok)
    dmix = _mm_nt("d_mix", dx2, w["w_out"], F32, 512, 1024)
    g_w_out = _mm_tn("g_w_out", mix, dx2, 1024, 1024, 1024)
    tok = on_grad("w_out", g_w_out)
    do, dp_mid, g_ong, g_sgn, g_sgw, g_sgbt = _mix_bwd(o, p, w["dn_out_norm_g"], w["sg_norm_g"], w["sg_w"], sgbt,
                                                      dmix, dep=tok)
    early = dict(dn_out_norm_g=g_ong, sg_norm_g=g_sgn, sg_w=g_sgw, sg_bt=g_sgbt,
                 ffn_norm_g=g_ffn_norm, ffn_conv_w=g_ffn_conv_w, ffn_conv_b=g_ffn_conv_b, final_norm_g=g_final)
    tok = on_grad("small_early", early)
    dq, dk, dv, dbg = _dn_scan_bwd(q, k, v, bg, gc, gct, a, a_t, sall, do, dep=tok)
    dp, g_dn_conv_w, g_ad = _dn_act_bwd(p, w["dn_conv_w"], alog_row, dtb_row, dq, dk, dv, dbg, dp_mid)
    g_w_in = _mm_tn("g_w_in", h1, dp, 1024, PROJ_PAD, 1024, col_groups=(4, PROJ_COLS // 4))
    tok = on_grad("w_in", g_w_in)
    grad_x, g_attn_norm = _mm_nt_rms_bwd("d_h1", dp, w_in_pad, x, w["attn_norm_g"], dx2, dep=tok)

    grads = dict(attn_norm_g=g_attn_norm, w_in=g_w_in, dn_conv_w=g_dn_conv_w, a_dt=g_ad,
                 w_out=g_w_out, w_up=g_w_up, w_down=g_w_down, **early)
    return loss, grad_x, grads


def _me():
    return lax.axis_index("x"), lax.axis_index("y"), lax.axis_index("c")


def _peer(rel):
    x, y, c = _me()
    return {"x": (1 - x, y, c), "y": (x, 1 - y, c), "xy": (1 - x, 1 - y, c), "c": (x, y, 1 - c)}[rel]


def _chip_of(dev):
    return 2 * dev[0] + dev[1]


CHIP_RELS = ("x", "y", "xy")


def _run_copies(copies, sends, recvs):
    for cp in copies:
        cp.start()
    for cp in recvs:
        cp.wait_recv()
    for cp in sends:
        cp.wait_send()


def _gather_first(w_shard, small_shard):
    R = w_shard.shape[0]
    r2 = R // 2

    def body(w_ref, s_ref, w_out, s_out, send_sems, recv_sems):
        x, y, c = _me()
        me = _chip_of((x, y))
        sib = _peer("c")

        def half(chip, core):
            return w_out.at[chip, pl.ds(pl.multiple_of(core * r2, 8), r2), :]

        def copy(k, src, dst, to):
            return pltpu.make_async_remote_copy(src_ref=src, dst_ref=dst, send_sem=send_sems.at[k],
                                                recv_sem=recv_sems.at[k], device_id=to, device_id_type=MESH)

        own_rows = w_ref.at[pl.ds(pl.multiple_of(c * r2, 8), r2), :]
        first = [copy(r, own_rows, half(me, c), _peer(rel)) for r, rel in enumerate(CHIP_RELS)]
        first += [copy(3 + r, s_ref, s_out.at[me], _peer(rel)) for r, rel in enumerate(CHIP_RELS)]
        for cp in first:
            cp.start()
        passed = []
        for r, rel in enumerate(CHIP_RELS):
            their = _chip_of(_peer(rel))
            copy(r, own_rows, half(their, c), _peer(rel)).wait_recv()
            fwd = copy(6 + r, half(their, c), half(their, c), sib)
            fwd.start()
            passed.append(fwd)
        for r, rel in enumerate(CHIP_RELS):
            their = _chip_of(_peer(rel))
            copy(3 + r, s_ref, s_out.at[their], _peer(rel)).wait_recv()
            copy(6 + r, own_rows, half(their, 1 - c), sib).wait_recv()
        for cp in first + passed:
            cp.wait_send()

    w_all, s_all = pl.pallas_call(
        body, name="gather_first", in_specs=[ANY, ANY], out_specs=[ANY, ANY],
        out_shape=[jax.ShapeDtypeStruct((4,) + w_shard.shape, w_shard.dtype),
                   jax.ShapeDtypeStruct((4,) + small_shard.shape, small_shard.dtype)],
        scratch_shapes=[pltpu.SemaphoreType.DMA((9,)), pltpu.SemaphoreType.DMA((9,))])(w_shard, small_shard)
    me = _chip_of(_me())
    return (lax.dynamic_update_index_in_dim(w_all, w_shard, me, 0),
            lax.dynamic_update_index_in_dim(s_all, small_shard, me, 0))


OTHERS = tuple((fx, fy, fc) for fx in (0, 1) for fy in (0, 1) for fc in (0, 1) if (fx, fy, fc) != (0, 0, 0))


def _other(flip):
    x, y, c = _me()
    return (x ^ flip[0], y ^ flip[1], c ^ flip[2])


def _linear(dev):
    return 4 * dev[0] + 2 * dev[1] + dev[2]


def _exchange_small(small):
    def body(small_ref, out_ref, send_sems, recv_sems):
        my_slot = _linear(_me())
        sends, recvs = [], []
        for k, flip in enumerate(OTHERS):
            peer = _other(flip)
            sends.append(pltpu.make_async_remote_copy(
                src_ref=small_ref, dst_ref=out_ref.at[my_slot], send_sem=send_sems.at[k], recv_sem=recv_sems.at[k],
                device_id=peer, device_id_type=MESH))
            recvs.append(pltpu.make_async_remote_copy(
                src_ref=small_ref, dst_ref=out_ref.at[_linear(peer)], send_sem=send_sems.at[k],
                recv_sem=recv_sems.at[k], device_id=peer, device_id_type=MESH))
        _run_copies(sends, sends, recvs)

    out = pl.pallas_call(
        body, name="exchange_small", in_specs=[ANY], out_specs=ANY,
        out_shape=jax.ShapeDtypeStruct((8,) + small.shape, small.dtype),
        scratch_shapes=[pltpu.SemaphoreType.DMA((7,)), pltpu.SemaphoreType.DMA((7,))])(small)
    return lax.dynamic_update_index_in_dim(out, small, _linear(_me()), 0)


def _pair_swap(halves):
    n = len(halves)

    def body(*refs):
        src, out = refs[:n], refs[n:2 * n]
        send_sems, recv_sems = refs[2 * n:]
        sib = _peer("c")
        copies = [pltpu.make_async_remote_copy(
            src_ref=src[i], dst_ref=out[i], send_sem=send_sems.at[i], recv_sem=recv_sems.at[i],
            device_id=sib, device_id_type=MESH) for i in range(n)]
        _run_copies(copies, copies, copies)

    return pl.pallas_call(
        body, name="pair_swap", in_specs=[ANY] * n, out_specs=[ANY] * n,
        out_shape=[jax.ShapeDtypeStruct(h.shape, h.dtype) for h in halves],
        scratch_shapes=[pltpu.SemaphoreType.DMA((n,)), pltpu.SemaphoreType.DMA((n,))])(*halves)


HBM = pl.BlockSpec(memory_space=pltpu.HBM)
SEM = pl.BlockSpec(memory_space=pltpu.SEMAPHORE)
EFFECT = pltpu.SideEffectType.DATAFLOW_SIDE_EFFECTING


def _hbm(a):
    return pltpu.with_memory_space_constraint(a, pltpu.HBM)


def _transfer_start(name, srcs, lands, n_copies, make_copies, after=None):
    n, m = len(srcs), len(lands)

    def body(*refs):
        src, land = refs[:n], refs[n:n + m]
        outs = refs[n + m + (after is not None):]
        send_sems, recv_sems, token = outs[0], outs[1], outs[-1]
        for cp in make_copies(src, land, send_sems, recv_sems):
            cp.start()
        token[...] = jnp.zeros_like(token)

    arrs = list(srcs) + list(lands)
    in_specs, args = _with_dep([HBM] * (n + m), [_hbm(a) for a in arrs], after)
    out = pl.pallas_call(
        body, name=name,
        out_shape=(pltpu.SemaphoreType.DMA((n_copies,)), pltpu.SemaphoreType.DMA((n_copies,)),
                   *[pltpu.HBM(a.shape, a.dtype) for a in arrs], jax.ShapeDtypeStruct((8, 128), F32)),
        in_specs=in_specs,
        out_specs=(SEM, SEM, *[HBM] * (n + m), pl.BlockSpec(memory_space=pltpu.VMEM)),
        input_output_aliases={i: 2 + i for i in range(n + m)},
        compiler_params=pltpu.CompilerParams(has_side_effects=EFFECT))(*args)
    return out[0], out[1], list(out[2:2 + n]), list(out[2 + n:2 + n + m]), out[-1]


def _transfer_wait(name, send_sems, recv_sems, srcs, lands, make_copies, after):
    n, m = len(srcs), len(lands)

    def body(*refs):
        src, land = refs[:n], refs[n:n + m]
        s_sems, r_sems = refs[n + m], refs[n + m + 1]
        for cp in make_copies(src, land, s_sems, r_sems):
            cp.wait_send()
            cp.wait_recv()

    arrs = list(srcs) + list(lands)
    out = pl.pallas_call(
        body, name=name, out_shape=tuple(pltpu.HBM(a.shape, a.dtype) for a in arrs),
        in_specs=[HBM] * (n + m) + [SEM, SEM, ANY], out_specs=tuple([HBM] * (n + m)),
        input_output_aliases={i: i for i in range(n + m)},
        compiler_params=pltpu.CompilerParams(has_side_effects=EFFECT))(*arrs, send_sems, recv_sems, after)
    return list(out[:n]), list(out[n:])


def _gather_copies(src, land, send_sems, recv_sems):
    me = _chip_of(_me())
    copies = []
    for i in range(len(src)):
        for r, rel in enumerate(CHIP_RELS):
            k = 3 * i + r
            copies.append(pltpu.make_async_remote_copy(
                src_ref=src[i], dst_ref=land[i].at[me], send_sem=send_sems.at[k], recv_sem=recv_sems.at[k],
                device_id=_peer(rel), device_id_type=MESH))
    return copies


def _small_copies(src, land, send_sems, recv_sems):
    my_slot = _linear(_me())
    return [pltpu.make_async_remote_copy(
        src_ref=src[0], dst_ref=land[0].at[my_slot], send_sem=send_sems.at[k], recv_sem=recv_sems.at[k],
        device_id=_other(flip), device_id_type=MESH) for k, flip in enumerate(OTHERS)]


def _pieces_copies(src, land, send_sems, recv_sems):
    copies = []
    for k, flip in enumerate(OTHERS):
        peer = _other(flip)
        copies.append(pltpu.make_async_remote_copy(
            src_ref=src[0].at[_linear(peer)], dst_ref=land[0].at[k], send_sem=send_sems.at[k],
            recv_sem=recv_sems.at[k], device_id=peer, device_id_type=MESH))
    return copies


def _row_block(rows, cols, budget=2 * 1024 * 1024):
    rb = max(8, (budget // (4 * cols)) // 8 * 8)
    while rows % rb:
        rb -= 8
    return rb if rb > 0 else rows


def _sum_slots(name, first, rest):
    R, Cc = first.shape
    K = rest.shape[0]
    rb = _row_block(R, Cc)

    def body(f_ref, r_ref, o_ref):
        acc = f_ref[...].astype(F32)
        for j in range(K):
            acc = acc + r_ref[j].astype(F32)
        o_ref[...] = acc

    return pl.pallas_call(
        body, name=name, grid=(R // rb,),
        in_specs=[pl.BlockSpec((rb, Cc), lambda i: (i, 0)), pl.BlockSpec((K, rb, Cc), lambda i: (0, i, 0))],
        out_specs=pl.BlockSpec((rb, Cc), lambda i: (i, 0)),
        out_shape=jax.ShapeDtypeStruct((R, Cc), F32), compiler_params=_cp("parallel"))(first, rest)


def _adamw_math(w, gv, m, v):
    mn = ADAM_B1 * m + (1.0 - ADAM_B1) * gv
    vn = ADAM_B2 * v + (1.0 - ADAM_B2) * (gv * gv)
    m_hat = mn / (1.0 - ADAM_B1 ** ADAM_STEP)
    v_hat = vn / (1.0 - ADAM_B2 ** ADAM_STEP)
    return -ADAM_LR * (m_hat / (jnp.sqrt(v_hat) + ADAM_EPS) + ADAM_WD * w), mn, vn


def _adamw_halves(name, w, mine, theirs, m, v, core):
    R, Cc = w.shape
    r2 = R // 2
    rb = _row_block(r2, Cc, 1024 * 1024)
    nb2 = r2 // rb

    def body(c_ref, w_ref, mine_ref, theirs_ref, m_ref, v_ref, g_ref, d_ref, mo_ref, vo_ref):
        is_mine = (pl.program_id(0) // nb2) == c_ref[0]
        gv = jnp.where(is_mine, mine_ref[...], theirs_ref[...])
        g_ref[...] = gv
        d_ref[...], mo_ref[...], vo_ref[...] = _adamw_math(w_ref[...], gv, m_ref[...], v_ref[...])

    blk = pl.BlockSpec((rb, Cc), lambda i, c: (i, 0))
    half = lambda own: pl.BlockSpec(
        (rb, Cc), lambda i, c: (jnp.clip(i - (c[0] if own else 1 - c[0]) * nb2, 0, nb2 - 1), 0))
    return pl.pallas_call(
        body, name=name,
        grid_spec=pltpu.PrefetchScalarGridSpec(
            num_scalar_prefetch=1, grid=(2 * nb2,), in_specs=[blk, half(True), half(False), blk, blk],
            out_specs=[blk] * 4),
        out_shape=[jax.ShapeDtypeStruct((R, Cc), F32)] * 4, compiler_params=_cp("parallel"))(core, w, mine, theirs, m, v)


def _adamw_transposed(name, wt, mine, theirs, mt, vt, core):
    Cc, kh_n, _ = wt.shape
    r2 = mine.shape[0]
    per_half = kh_n // 2
    nb = -(-Cc // LANES)

    def body(c_ref, w_ref, mine_ref, theirs_ref, m_ref, v_ref, g_ref, d_ref, mo_ref, vo_ref):
        first = c_ref[0] == 0
        halves = (jnp.where(first, mine_ref[...], theirs_ref[...]).T,
                  jnp.where(first, theirs_ref[...], mine_ref[...]).T)
        for kh in range(kh_n):
            lo = (kh % per_half) * LANES
            g_ref[:, kh, :] = halves[kh // per_half][:, lo:lo + LANES]
        d_ref[...], mo_ref[...], vo_ref[...] = _adamw_math(w_ref[...], g_ref[...], m_ref[...], v_ref[...])

    blk = pl.BlockSpec((LANES, kh_n, LANES), lambda i, c: (i, 0, 0))
    half = pl.BlockSpec((r2, LANES), lambda i, c: (0, i))
    return pl.pallas_call(
        body, name=name,
        grid_spec=pltpu.PrefetchScalarGridSpec(
            num_scalar_prefetch=1, grid=(nb,), in_specs=[blk, half, half, blk, blk], out_specs=[blk] * 4),
        out_shape=[jax.ShapeDtypeStruct(wt.shape, F32)] * 4, compiler_params=_cp("parallel"))(
            core, wt, mine, theirs, mt, vt)


FF_W = 2 * D_FF
FF_CH = FF_W // LANES
DNC_W = 3 * DN_WIDTH
DNC_CH = DNC_W // LANES
E_ONG, E_SGN, E_SGW, E_SGBT = 0, 1, 8, 8 + SG_GROUPS * SG_BLOCK
E_FFN = E_SGBT + SG_BLOCK
E_FCW = E_FFN + D_MODEL // LANES
E_FCB = E_FCW + 3 * FF_CH
E_FIN = E_FCB + FF_CH
EARLY_ROWS = E_FIN + D_MODEL // LANES
L_ATTN, L_DNC = 0, D_MODEL // LANES
L_AD = L_DNC + 4 * DNC_CH
L_LOSS = L_AD + 2
LATE_ROWS = -(-(L_LOSS + 1) // 8) * 8


def _put_rows(out, r0, x):
    k, width = x.shape
    n = width // LANES
    for t in range(k):
        for j in range(n):
            out[r0 + t * n + j:r0 + t * n + j + 1, :] = x[t:t + 1, j * LANES:(j + 1) * LANES]


def _pack_early(ong, sgn, sgw, sgbt, ffn, fcw, fcb, fin):
    def body(ong_ref, sgn_ref, sgw_ref, sgbt_ref, ffn_ref, fcw_ref, fcb_ref, fin_ref, out):
        out[...] = jnp.zeros_like(out)
        _put_rows(out, E_ONG, ong_ref)
        _put_rows(out, E_SGN, sgn_ref)
        for gi in range(SG_GROUPS):
            out[E_SGW + gi * SG_BLOCK:E_SGW + (gi + 1) * SG_BLOCK, :] = sgw_ref[gi]
        out[E_SGBT:E_SGBT + SG_BLOCK, :] = sgbt_ref[...]
        _put_rows(out, E_FFN, ffn_ref)
        _put_rows(out, E_FCW, fcw_ref)
        _put_rows(out, E_FCB, fcb_ref)
        _put_rows(out, E_FIN, fin_ref)

    return pl.pallas_call(body, name="pack_small_early", out_shape=jax.ShapeDtypeStruct((EARLY_ROWS, LANES), F32))(
        ong, sgn, sgw, sgbt, ffn, fcw, fcb, fin)


def _pack_late(attn, dnc, ad, loss_row):
    def body(attn_ref, dnc_ref, ad_ref, loss_ref, out):
        out[...] = jnp.zeros_like(out)
        _put_rows(out, L_ATTN, attn_ref)
        _put_rows(out, L_DNC, dnc_ref)
        out[L_AD:L_AD + 2, :] = ad_ref[...]
        out[L_LOSS:L_LOSS + 1, :] = loss_ref[...]

    return pl.pallas_call(body, name="pack_small_late", out_shape=jax.ShapeDtypeStruct((LATE_ROWS, LANES), F32))(
        attn, dnc, ad, loss_row)


SMALL = ("attn_norm_g", "dn_a_log", "dn_dt_bias", "dn_out_norm_g", "sg_norm_g", "sg_w", "sg_b", "ffn_norm_g",
         "ffn_conv_b", "final_norm_g", "dn_conv_w", "ffn_conv_w")


def _small_update(early_all, late_all, chip, W, M, V):
    n = len(SMALL)
    arrs = [d[k] for d in (W, M, V) for k in SMALL]

    def body(c_ref, e_ref, l_ref, *refs):
        w_, m_, v_ = refs[:n], refs[n:2 * n], refs[2 * n:3 * n]
        loss_ref = refs[3 * n]
        outs = refs[3 * n + 1:]
        g_, d_, mo_, vo_ = outs[:n], outs[n:2 * n], outs[2 * n:3 * n], outs[3 * n:4 * n]
        chip_i = c_ref[0]

        def total(ref, r0, rows=1):
            acc = ref[0, pl.ds(r0, rows), :]
            for s in range(1, 8):
                acc = acc + ref[s, pl.ds(r0, rows), :]
            return acc

        def update(i, idx, g):
            g_[i][idx] = g
            d_[i][idx], mo_[i][idx], vo_[i][idx] = _adamw_math(w_[i][idx], g, m_[i][idx], v_[i][idx])

        def rows_param(name, ref, r0, width):
            i = SMALL.index(name)
            for j in range(width // LANES):
                update(i, (slice(None), slice(j * LANES, (j + 1) * LANES)), total(ref, r0 + j))

        rows_param("attn_norm_g", l_ref, L_ATTN, D_MODEL)
        ad = (total(l_ref, L_AD), total(l_ref, L_AD + 1))
        update(SMALL.index("dn_a_log"), (slice(None), slice(None)), ad[0][:, N_HEADS:2 * N_HEADS])
        update(SMALL.index("dn_dt_bias"), (slice(None), slice(None)), ad[1][:, N_HEADS:2 * N_HEADS])
        rows_param("dn_out_norm_g", e_ref, E_ONG, HEAD_DIM)
        rows_param("sg_norm_g", e_ref, E_SGN, SG_WIDTH)
        sgbt = total(e_ref, E_SGBT, SG_BLOCK).T
        for gi in range(SG_GROUPS):
            update(SMALL.index("sg_w"), (0, gi), total(e_ref, E_SGW + gi * SG_BLOCK, SG_BLOCK))
            update(SMALL.index("sg_b"), (0, slice(gi, gi + 1), slice(None)), sgbt[gi:gi + 1, :])
        rows_param("ffn_norm_g", e_ref, E_FFN, D_MODEL)
        rows_param("ffn_conv_b", e_ref, E_FCB, FF_W)
        rows_param("final_norm_g", e_ref, E_FIN, D_MODEL)
        for name, ref, r0, taps, chunks in (("dn_conv_w", l_ref, L_DNC, 4, DNC_CH), ("ffn_conv_w", e_ref, E_FCW, 3, FF_CH)):
            mine = chunks // 4
            for t in range(taps):
                for j in range(mine):
                    update(SMALL.index(name), (0, slice(t, t + 1), slice(j * LANES, (j + 1) * LANES)),
                           total(ref, r0 + t * chunks + chip_i * mine + j))
        loss_ref[...] = total(l_ref, L_LOSS)

    full = lambda a: pl.BlockSpec(a.shape, lambda i, c, nd=a.ndim: (0,) * nd)
    shapes = [jax.ShapeDtypeStruct(W[k].shape, F32) for k in SMALL]
    outs = pl.pallas_call(
        body, name="small_update",
        grid_spec=pltpu.PrefetchScalarGridSpec(
            num_scalar_prefetch=1, grid=(1,), in_specs=[full(early_all), full(late_all)] + [full(a) for a in arrs],
            out_specs=[pl.BlockSpec((1, LANES), lambda i, c: (0, 0))] + [full(s) for s in shapes] * 4),
        out_shape=[jax.ShapeDtypeStruct((1, LANES), F32)] + shapes * 4,
        compiler_params=pltpu.CompilerParams(vmem_limit_bytes=VMEM_LIMIT))(chip, early_all, late_all, *arrs)
    loss, outs = outs[0], outs[1:]
    return (loss,) + tuple(dict(zip(SMALL, outs[k * n:(k + 1) * n])) for k in range(4))


ORDER =("attn_norm_g", "w_in", "dn_conv_w", "dn_a_log", "dn_dt_bias", "dn_out_norm_g", "sg_norm_g", "sg_w",
         "sg_b", "w_out", "ffn_norm_g", "w_up", "ffn_conv_w", "ffn_conv_b", "w_down", "final_norm_g")


def kernel(x, attn_norm_g, w_in, dn_conv_w, dn_a_log, dn_dt_bias, dn_out_norm_g, sg_norm_g, sg_w, sg_b, w_out, ffn_norm_g, w_up, ffn_conv_w, ffn_conv_b, w_down, final_norm_g, loss_target, m_attn_norm_g, m_w_in, m_dn_conv_w, m_dn_a_log, m_dn_dt_bias, m_dn_out_norm_g, m_sg_norm_g, m_sg_w, m_sg_b, m_w_out, m_ffn_norm_g, m_w_up, m_ffn_conv_w, m_ffn_conv_b, m_w_down, m_final_norm_g, v_attn_norm_g, v_w_in, v_dn_conv_w, v_dn_a_log, v_dn_dt_bias, v_dn_out_norm_g, v_sg_norm_g, v_sg_w, v_sg_b, v_w_out, v_ffn_norm_g, v_w_up, v_ffn_conv_w, v_ffn_conv_b, v_w_down, v_final_norm_g):
    W = dict(attn_norm_g=attn_norm_g, w_in=w_in, dn_conv_w=dn_conv_w, dn_a_log=dn_a_log, dn_dt_bias=dn_dt_bias,
             dn_out_norm_g=dn_out_norm_g, sg_norm_g=sg_norm_g, sg_w=sg_w, sg_b=sg_b, w_out=w_out,
             ffn_norm_g=ffn_norm_g, w_up=w_up, ffn_conv_w=ffn_conv_w, ffn_conv_b=ffn_conv_b, w_down=w_down,
             final_norm_g=final_norm_g)
    Mo = dict(attn_norm_g=m_attn_norm_g, w_in=m_w_in, dn_conv_w=m_dn_conv_w, dn_a_log=m_dn_a_log,
              dn_dt_bias=m_dn_dt_bias, dn_out_norm_g=m_dn_out_norm_g, sg_norm_g=m_sg_norm_g, sg_w=m_sg_w,
              sg_b=m_sg_b, w_out=m_w_out, ffn_norm_g=m_ffn_norm_g, w_up=m_w_up, ffn_conv_w=m_ffn_conv_w,
              ffn_conv_b=m_ffn_conv_b, w_down=m_w_down, final_norm_g=m_final_norm_g)
    Vo = dict(attn_norm_g=v_attn_norm_g, w_in=v_w_in, dn_conv_w=v_dn_conv_w, dn_a_log=v_dn_a_log,
              dn_dt_bias=v_dn_dt_bias, dn_out_norm_g=v_dn_out_norm_g, sg_norm_g=v_sg_norm_g, sg_w=v_sg_w,
              sg_b=v_sg_b, w_out=v_w_out, ffn_norm_g=v_ffn_norm_g, w_up=v_w_up, ffn_conv_w=v_ffn_conv_w,
              ffn_conv_b=v_ffn_conv_b, w_down=v_w_down, final_norm_g=v_final_norm_g)
    xi, yi, ci = lax.axis_index("x"), lax.axis_index("y"), lax.axis_index("c")
    chip = 2 * xi + yi

    me_lin = 4 * xi + 2 * yi + ci

    g_in, g_dnc = _gather_first(w_in[0].astype(BF16), dn_conv_w[0])
    def start_gather(name, shards, after):
        lands = [lax.dynamic_update_index_in_dim(lax.empty((4,) + s.shape, s.dtype), s, chip, 0) for s in shards]
        return _transfer_start(name, shards, lands, 3 * len(shards), _gather_copies, after=after)

    mid = start_gather("gather_mid_start", [w_out[0].astype(BF16), w_up[0].astype(BF16), ffn_conv_w[0]], g_in)
    last = start_gather("gather_last_start", [w_down[0].astype(BF16)], mid[4])
    token = last[4]

    def late_weights(stage, after):
        if stage == "out_proj":
            _, (g_out, g_up, g_ffc) = _transfer_wait("gather_mid_wait", *mid[:4], _gather_copies, after)
            return dict(w_out=g_out.reshape(D_MODEL, D_MODEL), ffn_conv_w=g_ffc.transpose(1, 0, 2).reshape(3, 2 * D_FF),
                        w_up=g_up.transpose(1, 0, 2).reshape(D_MODEL, 2 * D_FF))
        _, (g_down,) = _transfer_wait("gather_last_wait", *last[:4], _gather_copies, after)
        return dict(w_down=g_down.reshape(D_FF, D_MODEL))

    full = dict(
        w_in=g_in,
        dn_conv_w=g_dnc.transpose(1, 0, 2).reshape(4, 3 * DN_WIDTH),
        attn_norm_g=attn_norm_g, dn_a_log=dn_a_log, dn_dt_bias=dn_dt_bias, dn_out_norm_g=dn_out_norm_g,
        sg_norm_g=sg_norm_g, sg_w=sg_w[0], sg_b=sg_b[0], ffn_norm_g=ffn_norm_g, ffn_conv_b=ffn_conv_b,
        final_norm_g=final_norm_g[None])

    pending = {}

    def on_grad(name, gw):
        if name == "small_early":
            buf = _pack_early(gw["dn_out_norm_g"], gw["sg_norm_g"], gw["sg_w"], gw["sg_bt"], gw["ffn_norm_g"],
                              gw["ffn_conv_w"], gw["ffn_conv_b"], gw["final_norm_g"])
            land = lax.dynamic_update_index_in_dim(lax.empty((8,) + buf.shape, F32), buf, me_lin, 0)
            s_sem, r_sem, src, lands, tok = _transfer_start("small_early_start", [buf], [land], 7, _small_copies)
            pending[name] = (s_sem, r_sem, src, lands)
            return tok
        g8 = gw.reshape(8, -1, gw.shape[-1])
        land = lax.empty((7,) + g8.shape[1:], BF16)
        s_sem, r_sem, src, lands, tok = _transfer_start(f"reduce_{name}_start", [g8], [land], 7, _pieces_copies)
        pending[name] = (s_sem, r_sem, src, lands)
        return tok

    loss_row, grad_x, g = _local_step(x[0], loss_target[0], full, dep=token, late_weights=late_weights,
                                      on_grad=on_grad)

    late_all = _exchange_small(_pack_late(g["attn_norm_g"], g["dn_conv_w"], g["a_dt"], loss_row))
    s_sem, r_sem, src, lands = pending["small_early"]
    _, (early_all,) = _transfer_wait("small_early_wait", s_sem, r_sem, src, lands, _small_copies, grad_x)
    row = lambda d: {k: (d[k].reshape(1, -1) if k == "final_norm_g" else d[k]) for k in SMALL}
    loss_sum, *small_out = _small_update(early_all, late_all, chip.astype(jnp.int32).reshape(1), row(W), row(Mo), row(Vo))
    loss = loss_sum[0, 0]

    halves = []
    for n in ("w_down", "w_up", "w_out", "w_in"):
        s_sem, r_sem, src, lands = pending[n]
        sent, got = _transfer_wait(f"reduce_{n}_wait", s_sem, r_sem, src, lands, _pieces_copies, grad_x)
        own = lax.dynamic_index_in_dim(sent[0], me_lin, axis=0, keepdims=False)
        halves.append(_sum_slots(f"sum_{n}", own, got[0]))
    theirs = _pair_swap(halves)
    core = ci.astype(jnp.int32).reshape(1)
    grads, delta, new_m, new_v = {}, {}, {}, {}
    for n, mine_h, their_h in zip(("w_down", "w_up", "w_out", "w_in"), halves, theirs):
        shp = W[n].shape
        if n == "w_in":
            to_t = lambda a: a.reshape(shp[1] // LANES, LANES, shp[2]).transpose(2, 0, 1)
            from_t = lambda a: a.transpose(1, 2, 0).reshape(shp)
            outs = _adamw_transposed(f"adamw_{n}", to_t(W[n]), mine_h, their_h, to_t(Mo[n]), to_t(Vo[n]), core)
            grads[n], delta[n], new_m[n], new_v[n] = (from_t(o) for o in outs)
            continue
        gr, d, mn, vn = _adamw_halves(f"adamw_{n}", W[n][0], mine_h, their_h, Mo[n][0], Vo[n][0], core)
        grads[n], delta[n], new_m[n], new_v[n] = gr.reshape(shp), d.reshape(shp), mn.reshape(shp), vn.reshape(shp)
    for dst, src_d in zip((grads, delta, new_m, new_v), small_out):
        dst.update({k: (a.reshape(W[k].shape) if k == "final_norm_g" else a) for k, a in src_d.items()})

    return (loss, grad_x[None], *[grads[n] for n in ORDER], *[delta[n] for n in ORDER],
            *[new_m[n] for n in ORDER], *[new_v[n] for n in ORDER])
```

```python
import functools
import math

import jax
import jax.numpy as jnp
from jax import lax
from jax.experimental import pallas as pl
from jax.experimental.pallas import tpu as pltpu

F32 = jnp.float32
BF16 = jnp.bfloat16

D_MODEL = 1024
CHUNK = 64
SCAN_CHUNKS = 4
SCAN_CHUNKS_FWD = 8
HEAD_DIM = 128
N_HEADS = 4
DN_WIDTH = 512
SG_WIDTH = 512
SG_GROUPS = 4
SG_BLOCK = 128
D_FF = 2816
PROJ_COLS = 3080
PROJ_PAD = 3200
BA_COL = 3072
EPS = 1e-6
NEG = -1e30
VMEM_LIMIT = 56 * 1024 * 1024

ADAM_LR = 0.001
ADAM_B1 = 0.9
ADAM_B2 = 0.999
ADAM_EPS = 1e-08
ADAM_WD = 0.01
ADAM_STEP = 10

MESH = pl.DeviceIdType.MESH
ANY = pl.BlockSpec(memory_space=pl.ANY)


def _cp(*sem):
    return pltpu.CompilerParams(dimension_semantics=sem, vmem_limit_bytes=VMEM_LIMIT)


def _bf(a):
    return a.astype(BF16)


def _nn(a, b):
    return jnp.dot(_bf(a), _bf(b), preferred_element_type=F32)


def _nt(a, b):
    return lax.dot_general(_bf(a), _bf(b), (((1,), (1,)), ((), ())), preferred_element_type=F32)


def _tn(a, b):
    return lax.dot_general(_bf(a), _bf(b), (((0,), (0,)), ((), ())), preferred_element_type=F32)


def _split(a):
    hi = _bf(a)
    return hi, _bf(a - hi.astype(F32))


def _sigmoid(x):
    return 0.5 * jnp.tanh(0.5 * x) + 0.5


def _silu(x):
    return x * _sigmoid(x)


def _dsilu(x):
    s = _sigmoid(x)
    return s * (1.0 + x * (1.0 - s))


_GELU_C = math.sqrt(2.0 / math.pi)
_GELU_A = 0.044715


def _gelu(x):
    return 0.5 * x * (1.0 + jnp.tanh(_GELU_C * (x + _GELU_A * x * x * x)))


def _dgelu(x):
    t = jnp.tanh(_GELU_C * (x + _GELU_A * x * x * x))
    return 0.5 * (1.0 + t) + 0.5 * x * (1.0 - t * t) * _GELU_C * (1.0 + 3.0 * _GELU_A * x * x)


def _softplus(x):
    return jnp.maximum(x, 0.0) + jnp.log(1.0 + jnp.exp(-jnp.abs(x)))


def _with_dep(in_specs, args, dep):
    if dep is None:
        return in_specs, args
    return in_specs + [ANY], args + [dep]


SUB_ROWS = 128


def _sub_blocks(tm):
    return [slice(r0, min(r0 + SUB_ROWS, tm)) for r0 in range(0, tm, SUB_ROWS)]


def _rms_hat(xv):
    r = lax.rsqrt(jnp.mean(xv * xv, axis=-1, keepdims=True) + EPS)
    return xv * r, r


def _rms_bwd_vals(dh, xh, r, g):
    dxh = dh * g
    return r * (dxh - xh * jnp.mean(dxh * xh, axis=-1, keepdims=True)), jnp.sum(dh * xh, axis=0, keepdims=True)


def _in_proj(x, g, w4, tm=512, dep=None):
    T, K = x.shape
    ng, _, wc = w4.shape
    tm = min(tm, T)

    def body(x_ref, g_ref, w4_ref, *rest):
        p_ref, h_ref, w_ref = rest[-3:]

        @pl.when(pl.program_id(0) == 0)
        def _():
            w_ref[:, ng * wc:] = jnp.zeros((K, PROJ_PAD - ng * wc), BF16)
            for j in range(ng):
                w_ref[:, j * wc:(j + 1) * wc] = w4_ref[j]
        for r in _sub_blocks(tm):
            xh, _ = _rms_hat(x_ref[r, :])
            h_ref[r, :] = (xh * g_ref[...]).astype(BF16)
        p_ref[...] = jnp.dot(h_ref[...], w_ref[...], preferred_element_type=F32)

    in_specs, args = _with_dep(
        [pl.BlockSpec((tm, K), lambda i: (i, 0)), pl.BlockSpec((1, K), lambda i: (0, 0)),
         pl.BlockSpec((ng, K, wc), lambda i: (0, 0, 0))], [x, g, w4], dep)
    return pl.pallas_call(
        body, name="in_proj", grid=(T // tm,), in_specs=in_specs,
        out_specs=[pl.BlockSpec((tm, PROJ_PAD), lambda i: (i, 0)), pl.BlockSpec((tm, K), lambda i: (i, 0)),
                   pl.BlockSpec((K, PROJ_PAD), lambda i: (0, 0))],
        out_shape=[jax.ShapeDtypeStruct((T, PROJ_PAD), F32), jax.ShapeDtypeStruct((T, K), BF16),
                   jax.ShapeDtypeStruct((K, PROJ_PAD), BF16)],
        compiler_params=_cp("arbitrary"))(*args)


def _out_proj(mix, w, x, g, tm=512):
    T, K = mix.shape
    Dm = w.shape[1]
    tm = min(tm, T)

    def body(a_ref, w_ref, x_ref, g_ref, x2_ref, h_ref):
        x2_ref[...] = _nn(a_ref[...], w_ref[...]) + x_ref[...]
        for r in _sub_blocks(tm):
            xh, _ = _rms_hat(x2_ref[r, :])
            h_ref[r, :] = (xh * g_ref[...]).astype(BF16)

    row = lambda width: pl.BlockSpec((tm, width), lambda i: (i, 0))
    return pl.pallas_call(
        body, name="out_proj", grid=(T // tm,),
        in_specs=[row(K), pl.BlockSpec((K, Dm), lambda i: (0, 0)), row(Dm), pl.BlockSpec((1, Dm), lambda i: (0, 0))],
        out_specs=[row(Dm), row(Dm)],
        out_shape=[jax.ShapeDtypeStruct((T, Dm), F32), jax.ShapeDtypeStruct((T, Dm), BF16)],
        compiler_params=_cp("parallel"))(mix, w, x, g)


def _down_proj_loss(act, w, x2, tgt, g, tm=512):
    T, K = act.shape
    Dm = w.shape[1]
    tm = min(tm, T)

    def body(a_ref, w_ref, x_ref, t_ref, g_ref, loss_ref, dx_ref, gg_ref):
        @pl.when(pl.program_id(0) == 0)
        def _():
            gg_ref[...] = jnp.zeros_like(gg_ref)
            loss_ref[...] = jnp.zeros_like(loss_ref)
        dx_ref[...] = _nn(a_ref[...], w_ref[...]) + x_ref[...]
        for r in _sub_blocks(tm):
            xh, rr = _rms_hat(dx_ref[r, :])
            e = xh * g_ref[...] - t_ref[r, :]
            loss_ref[...] += jnp.zeros_like(loss_ref) + (0.5 / Dm) * jnp.sum(e * e)
            dx, gg = _rms_bwd_vals(e * (1.0 / Dm), xh, rr, g_ref[...])
            dx_ref[r, :] = dx
            gg_ref[...] += gg

    row = lambda width: pl.BlockSpec((tm, width), lambda i: (i, 0))
    vec = pl.BlockSpec((1, Dm), lambda i: (0, 0))
    return pl.pallas_call(
        body, name="down_proj_loss", grid=(T // tm,),
        in_specs=[row(K), pl.BlockSpec((K, Dm), lambda i: (0, 0)), row(Dm), row(Dm), vec],
        out_specs=[pl.BlockSpec((1, 128), lambda i: (0, 0)), row(Dm), vec],
        out_shape=[jax.ShapeDtypeStruct((1, 128), F32), jax.ShapeDtypeStruct((T, Dm), F32),
                   jax.ShapeDtypeStruct((1, Dm), F32)],
        compiler_params=_cp("arbitrary"))(act, w, x2, tgt, g)


def _mm_nt_rms_bwd(name, a, b, x, g, dres, tm=512, dep=None):
    M, K = a.shape
    Dm = b.shape[0]
    tm = min(tm, M)

    def body(a_ref, b_ref, x_ref, g_ref, dres_ref, *rest):
        dx_ref, gg_ref = rest[-2:]

        @pl.when(pl.program_id(0) == 0)
        def _():
            gg_ref[...] = jnp.zeros_like(gg_ref)
        dx_ref[...] = _nt(a_ref[...], b_ref[...])
        for r in _sub_blocks(tm):
            xh, rr = _rms_hat(x_ref[r, :])
            dx, gg = _rms_bwd_vals(dx_ref[r, :], xh, rr, g_ref[...])
            dx_ref[r, :] = dres_ref[r, :] + dx
            gg_ref[...] += gg

    row = lambda width: pl.BlockSpec((tm, width), lambda i: (i, 0))
    vec = pl.BlockSpec((1, Dm), lambda i: (0, 0))
    in_specs, args = _with_dep([row(K), pl.BlockSpec((Dm, K), lambda i: (0, 0)), row(Dm), vec, row(Dm)],
                               [a, b, x, g, dres], dep)
    return pl.pallas_call(
        body, name=name, grid=(M // tm,), in_specs=in_specs, out_specs=[row(Dm), vec],
        out_shape=[jax.ShapeDtypeStruct((M, Dm), F32), jax.ShapeDtypeStruct((1, Dm), F32)],
        compiler_params=_cp("arbitrary"))(*args)


def _mm_nt(name, a, b, out_dtype, tm, tn, dep=None):
    M, K = a.shape
    N = b.shape[0]
    tm, tn = min(tm, M), min(tn, N)

    def body(a_ref, b_ref, *rest):
        o_ref = rest[-1]
        o_ref[...] = _nt(a_ref[...], b_ref[...]).astype(o_ref.dtype)

    in_specs, args = _with_dep(
        [pl.BlockSpec((tm, K), lambda i, j: (i, 0)), pl.BlockSpec((tn, K), lambda i, j: (j, 0))], [a, b], dep)
    return pl.pallas_call(
        body, name=name, grid=(M // tm, N // tn), in_specs=in_specs,
        out_specs=pl.BlockSpec((tm, tn), lambda i, j: (i, j)),
        out_shape=jax.ShapeDtypeStruct((M, N), out_dtype),
        compiler_params=_cp("parallel", "parallel"))(*args)


def _mm_tn(name, a, b, tm, tn, tk, col_major_tiles=False, col_groups=None):
    T, M = a.shape
    N = b.shape[1]
    tm, tn, tk = min(tm, M), min(tn, N), min(tk, T)
    nk = T // tk

    def body(a_ref, b_ref, o_ref, acc_ref):
        k = pl.program_id(2)

        @pl.when(k == 0)
        def _():
            acc_ref[...] = jnp.zeros_like(acc_ref)
        acc_ref[...] += _tn(a_ref[...], b_ref[...])

        @pl.when(k == nk - 1)
        def _():
            if col_groups:
                for j in range(col_groups[0]):
                    o_ref[j] = acc_ref[:, j * col_groups[1]:(j + 1) * col_groups[1]].astype(BF16)
            else:
                o_ref[...] = acc_ref[...].astype(BF16).reshape(o_ref.shape)

    if col_groups:
        assert tm == M and tn == N and col_groups[0] * col_groups[1] <= N
        out_spec = pl.BlockSpec((col_groups[0], M, col_groups[1]), lambda i, j, k: (0, 0, 0))
        out_shape = jax.ShapeDtypeStruct((col_groups[0], M, col_groups[1]), BF16)
    elif col_major_tiles:
        assert tm == M
        out_spec = pl.BlockSpec((1, tm, tn), lambda i, j, k: (j, 0, 0))
        out_shape = jax.ShapeDtypeStruct((N // tn, M, tn), BF16)
    else:
        out_spec = pl.BlockSpec((tm, tn), lambda i, j, k: (i, j))
        out_shape = jax.ShapeDtypeStruct((M, N), BF16)
    return pl.pallas_call(
        body, name=name, grid=(M // tm, N // tn, nk),
        in_specs=[pl.BlockSpec((tk, tm), lambda i, j, k: (k, i)), pl.BlockSpec((tk, tn), lambda i, j, k: (k, j))],
        out_specs=out_spec, out_shape=out_shape, scratch_shapes=[pltpu.VMEM((tm, tn), F32)],
        compiler_params=_cp("parallel", "parallel", "arbitrary"))(a, b)


def _halo_prev_spec(rb, width):
    return pl.BlockSpec((8, width), lambda i: (jnp.maximum(i * (rb // 8) - 1, 0), 0))


def _halo_next_spec(rb, width, T):
    return pl.BlockSpec((8, width), lambda i: (jnp.minimum((i + 1) * (rb // 8), T // 8 - 1), 0))


LANES = 128
FF_STRIPS = D_FF // LANES
ROW_CHUNK = 32


def _strip(j, base=0):
    return pl.ds(pl.multiple_of(base + j * LANES, LANES), LANES)


def _up_proj_act(h, w_up, w, b, rb=256):
    T, K = h.shape
    W = w_up.shape[1]
    rb = min(rb, T)

    def body(h_ref, wup_ref, w_ref, b_ref, up_ref, act_ref, ext_scr, tail_scr):
        @pl.when(pl.program_id(0) == 0)
        def _():
            tail_scr[...] = jnp.zeros_like(tail_scr)
        up_ref[...] = jnp.dot(h_ref[...], wup_ref[...], preferred_element_type=F32)

        def strip(j, slot):
            halves = (_strip(j), _strip(j, D_FF))
            wv = [w_ref[:, cols] for cols in halves]
            bv = [b_ref[:, cols] for cols in halves]
            for h, cols in enumerate(halves):
                ext_scr[slot, h,0:8] = tail_scr[:, cols]
                ext_scr[slot, h,8:] = up_ref[:, cols]
            for r0 in range(0, rb, ROW_CHUNK):
                n = min(ROW_CHUNK, rb - r0)
                c = [ext_scr[slot, h,6 + r0:6 + r0 + n] * wv[h][0:1] + ext_scr[slot, h,7 + r0:7 + r0 + n] * wv[h][1:2]
                     + ext_scr[slot, h,8 + r0:8 + r0 + n] * wv[h][2:3] + bv[h] for h in range(2)]
                act_ref[r0:r0 + n, halves[0]] = (_silu(c[0]) * c[1]).astype(BF16)

        def pair(jj, carry):
            strip(2 * jj, 0)
            strip(2 * jj + 1, 1)
            return carry

        lax.fori_loop(0, FF_STRIPS // 2, pair, 0)
        tail_scr[...] = up_ref[rb - 8:rb, :]

    return pl.pallas_call(
        body, name="up_proj_act", grid=(T // rb,),
        in_specs=[pl.BlockSpec((rb, K), lambda i: (i, 0)), pl.BlockSpec((K, W), lambda i: (0, 0)),
                  pl.BlockSpec((3, W), lambda i: (0, 0)), pl.BlockSpec((1, W), lambda i: (0, 0))],
        out_specs=[pl.BlockSpec((rb, W), lambda i: (i, 0)), pl.BlockSpec((rb, D_FF), lambda i: (i, 0))],
        out_shape=[jax.ShapeDtypeStruct((T, W), F32), jax.ShapeDtypeStruct((T, D_FF), BF16)],
        scratch_shapes=[pltpu.VMEM((2, 2, rb + 8, LANES), F32), pltpu.VMEM((8, W), F32)],
        compiler_params=_cp("arbitrary"))(h, w_up, w, b)


def _ffn_act_bwd(up, dact, w, b, rb=128, dep=None):
    T, W = up.shape
    rb = min(rb, T)
    nb = T // rb
    re = rb + 8

    def body(up_ref, prev_ref, next_ref, da_ref, danext_ref, w_ref, b_ref, *rest):
        dup_ref, gw_ref, gb_ref, ext_scr, dc_scr = rest[-5:]
        i = pl.program_id(0)

        @pl.when(i == 0)
        def _():
            gw_ref[...] = jnp.zeros_like(gw_ref)
            gb_ref[...] = jnp.zeros_like(gb_ref)
        last = i == nb - 1

        def fold8(a):
            return jnp.sum(a.reshape(a.shape[0] // 8, 8, LANES), axis=0)

        def strip(j, slot):
            halves = (_strip(j), _strip(j, D_FF))
            wv = [w_ref[:, cols] for cols in halves]
            bv = [b_ref[:, cols] for cols in halves]
            for h, cols in enumerate(halves):
                ext_scr[slot, h,0:8] = jnp.where(i > 0, prev_ref[:, cols], 0.0)
                ext_scr[slot, h,8:8 + rb] = up_ref[:, cols]
                ext_scr[slot, h,8 + rb:] = next_ref[:, cols]
            gb = [jnp.zeros((8, LANES), F32) for _ in range(2)]
            gw = [[jnp.zeros((8, LANES), F32) for _ in range(3)] for _ in range(2)]
            for r0 in range(0, re, ROW_CHUNK):
                n = min(ROW_CHUNK, re - r0)
                tp = [[ext_scr[slot, h,6 + k + r0:6 + k + r0 + n] for k in range(3)] for h in range(2)]
                c = [tp[h][0] * wv[h][0:1] + tp[h][1] * wv[h][1:2] + tp[h][2] * wv[h][2:3] + bv[h] for h in range(2)]
                if r0 < rb:
                    da = da_ref[r0:r0 + n, halves[0]]
                else:
                    da = jnp.where(last, 0.0, danext_ref[:, halves[0]])
                s = _sigmoid(c[0])
                gs = c[0] * s
                dcs = (da * c[1] * (s + gs * (1.0 - s)), da * gs)
                for h in range(2):
                    dc_scr[slot, h,r0:r0 + n] = dcs[h]
                    if r0 < rb:
                        gb[h] = gb[h] + fold8(dcs[h])
                        for k in range(3):
                            gw[h][k] = gw[h][k] + fold8(tp[h][k] * dcs[h])
            for r0 in range(0, rb, ROW_CHUNK):
                n = min(ROW_CHUNK, rb - r0)
                for h, cols in enumerate(halves):
                    dup = (dc_scr[slot, h,r0:r0 + n] * wv[h][2:3] + dc_scr[slot, h,r0 + 1:r0 + 1 + n] * wv[h][1:2]
                           + dc_scr[slot, h,r0 + 2:r0 + 2 + n] * wv[h][0:1])
                    dup_ref[r0:r0 + n, cols] = dup.astype(BF16)
            for h, cols in enumerate(halves):
                gb_ref[:, cols] += jnp.sum(gb[h], axis=0, keepdims=True)
                for k in range(3):
                    gw_ref[k:k + 1, cols] += jnp.sum(gw[h][k], axis=0, keepdims=True)

        def pair(jj, carry):
            strip(2 * jj, 0)
            strip(2 * jj + 1, 1)
            return carry

        lax.fori_loop(0, FF_STRIPS // 2, pair, 0)

    in_specs, args = _with_dep(
        [pl.BlockSpec((rb, W), lambda i: (i, 0)), _halo_prev_spec(rb, W), _halo_next_spec(rb, W, T),
         pl.BlockSpec((rb, D_FF), lambda i: (i, 0)), _halo_next_spec(rb, D_FF, T),
         pl.BlockSpec((3, W), lambda i: (0, 0)), pl.BlockSpec((1, W), lambda i: (0, 0))],
        [up, up, up, dact, dact, w, b], dep)
    return pl.pallas_call(
        body, name="ffn_act_bwd", grid=(nb,), in_specs=in_specs,
        out_specs=[pl.BlockSpec((rb, W), lambda i: (i, 0)), pl.BlockSpec((3, W), lambda i: (0, 0)),
                   pl.BlockSpec((1, W), lambda i: (0, 0))],
        out_shape=[jax.ShapeDtypeStruct((T, W), BF16), jax.ShapeDtypeStruct((3, W), F32),
                   jax.ShapeDtypeStruct((1, W), F32)],
        scratch_shapes=[pltpu.VMEM((2, 2, rb + 16, LANES), F32), pltpu.VMEM((2, 2, re, LANES), F32)],
        compiler_params=_cp("arbitrary"))(*args)


def _lane_iota(shape):
    return lax.broadcasted_iota(jnp.int32, shape, len(shape) - 1)


def _dn_act(p, conv_w, alog_row, dtb_row, rb=256):
    T = p.shape[0]
    rb = min(rb, T)
    W3 = 3 * DN_WIDTH

    def body(p_ref, halo_ref, ba_ref, w_ref, al_ref, dt_ref, q_ref, k_ref, v_ref, bg_ref, ext_scr):
        first = pl.program_id(0) == 0
        outs = (q_ref, k_ref, v_ref)
        for j in range(3 * N_HEADS):
            kind, h = divmod(j, N_HEADS)
            cols = slice(j * HEAD_DIM, (j + 1) * HEAD_DIM)
            cur = p_ref[:, cols]
            ext_scr[j, 0:8] = jnp.where(first, 0.0, halo_ref[:, cols])
            ext_scr[j, 8:] = cur
            wv = w_ref[:, cols]
            s = _silu(ext_scr[j, 5:5 + rb] * wv[0:1] + ext_scr[j, 6:6 + rb] * wv[1:2]
                      + ext_scr[j, 7:7 + rb] * wv[2:3] + cur * wv[3:4])
            if kind < 2:
                scale = HEAD_DIM ** -0.5 if kind == 0 else 1.0
                s = s * (lax.rsqrt(jnp.sum(s * s, axis=-1, keepdims=True) + EPS) * scale)
            outs[kind][:, h * HEAD_DIM:(h + 1) * HEAD_DIM] = s
        ba = ba_ref[...]
        lane = _lane_iota(ba.shape)
        beta = _sigmoid(ba)
        g = -jnp.exp(al_ref[...]) * _softplus(ba + dt_ref[...])
        bg_ref[...] = jnp.where(lane < N_HEADS, beta, jnp.where(lane < 2 * N_HEADS, g, 0.0))

    row512 = pl.BlockSpec((rb, DN_WIDTH), lambda i: (i, 0))
    row128 = pl.BlockSpec((rb, 128), lambda i: (i, 0))
    vec128 = pl.BlockSpec((1, 128), lambda i: (0, 0))
    return pl.pallas_call(
        body, name="dn_act", grid=(T // rb,),
        in_specs=[pl.BlockSpec((rb, W3), lambda i: (i, 0)), _halo_prev_spec(rb, W3),
                  pl.BlockSpec((rb, 128), lambda i: (i, BA_COL // 128)),
                  pl.BlockSpec((4, W3), lambda i: (0, 0)), vec128, vec128],
        out_specs=[row512, row512, row512, row128],
        out_shape=[jax.ShapeDtypeStruct((T, DN_WIDTH), F32)] * 3 + [jax.ShapeDtypeStruct((T, 128), F32)],
        scratch_shapes=[pltpu.VMEM((3 * N_HEADS, rb + 8, HEAD_DIM), F32)],
        compiler_params=_cp("parallel"))(p, p, p, conv_w, alog_row, dtb_row)


def _dn_act_bwd(p, conv_w, alog_row, dtb_row, dq, dk, dv, dbg, dp_mid, rb=256):
    T = p.shape[0]
    rb = min(rb, T)
    nb = T // rb
    re = rb + 8
    W3 = 3 * DN_WIDTH

    def body(p_ref, prev_ref, next_ref, ba_ref, w_ref, al_ref, dt_ref, dq_ref, dqn_ref, dk_ref, dkn_ref,
             dv_ref, dvn_ref, dbg_ref, mid_ref, draw_ref, gw_ref, gad_ref, ext_scr, dc_scr):
        i = pl.program_id(0)
        draw_ref[:, W3:2 * W3] = mid_ref[...]

        @pl.when(i == 0)
        def _():
            gw_ref[...] = jnp.zeros_like(gw_ref)
            gad_ref[...] = jnp.zeros_like(gad_ref)
        row = lax.broadcasted_iota(jnp.int32, (re, 1), 0)
        live = (row < rb) | (i < nb - 1)
        d_refs = ((dq_ref, dqn_ref), (dk_ref, dkn_ref), (dv_ref, dvn_ref))
        for j in range(3 * N_HEADS):
            kind, h = divmod(j, N_HEADS)
            cols = slice(j * HEAD_DIM, (j + 1) * HEAD_DIM)
            hcols = slice(h * HEAD_DIM, (h + 1) * HEAD_DIM)
            ext_scr[j, 0:8] = jnp.where(i > 0, prev_ref[:, cols], 0.0)
            ext_scr[j, 8:8 + rb] = p_ref[:, cols]
            ext_scr[j, 8 + rb:] = next_ref[:, cols]
            tp = [ext_scr[j, 5 + k:5 + k + re] for k in range(4)]
            wv = w_ref[:, cols]
            c = tp[0] * wv[0:1] + tp[1] * wv[1:2] + tp[2] * wv[2:3] + tp[3] * wv[3:4]
            sg = _sigmoid(c)
            s = c * sg
            d_in = jnp.where(live, jnp.concatenate([d_refs[kind][0][:, hcols], d_refs[kind][1][:, hcols]], axis=0), 0.0)
            if kind < 2:
                scale = HEAD_DIM ** -0.5 if kind == 0 else 1.0
                n = lax.rsqrt(jnp.sum(s * s, axis=-1, keepdims=True) + EPS)
                hat = s * n
                d_in = (n * scale) * (d_in - hat * jnp.sum(hat * d_in, axis=-1, keepdims=True))
            dc = d_in * (sg + s * (1.0 - sg))
            dc_scr[j] = dc
            dcc = dc[0:rb]
            draw = (dcc * wv[3:4] + dc_scr[j, 1:1 + rb] * wv[2:3] + dc_scr[j, 2:2 + rb] * wv[1:2]
                    + dc_scr[j, 3:3 + rb] * wv[0:1])
            draw_ref[:, cols] = draw.astype(BF16)
            for k in range(4):
                gw_ref[k:k + 1, cols] += jnp.sum(tp[k][0:rb] * dcc, axis=0, keepdims=True)
        ba = ba_ref[...]
        dbg = dbg_ref[...]
        lane = _lane_iota(ba.shape)
        beta = _sigmoid(ba)
        ea = jnp.exp(al_ref[...])
        z = ba + dt_ref[...]
        d_a = dbg * (-ea) * _sigmoid(z)
        dba = jnp.where(lane < N_HEADS, dbg * beta * (1.0 - beta), jnp.where(lane < 2 * N_HEADS, d_a, 0.0))
        draw_ref[:, BA_COL:] = dba.astype(BF16)
        isg = (lane >= N_HEADS) & (lane < 2 * N_HEADS)
        g = -ea * _softplus(z)
        gad_ref[0:1, :] += jnp.sum(jnp.where(isg, dbg * g, 0.0), axis=0, keepdims=True)
        gad_ref[1:2, :] += jnp.sum(jnp.where(isg, d_a, 0.0), axis=0, keepdims=True)

    row512 = pl.BlockSpec((rb, DN_WIDTH), lambda i: (i, 0))
    row128 = pl.BlockSpec((rb, 128), lambda i: (i, 0))
    vec128 = pl.BlockSpec((1, 128), lambda i: (0, 0))
    next512 = _halo_next_spec(rb, DN_WIDTH, T)
    return pl.pallas_call(
        body, name="dn_act_bwd", grid=(nb,),
        in_specs=[pl.BlockSpec((rb, W3), lambda i: (i, 0)), _halo_prev_spec(rb, W3), _halo_next_spec(rb, W3, T),
                  pl.BlockSpec((rb, 128), lambda i: (i, BA_COL // 128)),
                  pl.BlockSpec((4, W3), lambda i: (0, 0)), vec128, vec128,
                  row512, next512, row512, next512, row512, next512, row128,
                  pl.BlockSpec((rb, W3), lambda i: (i, 0))],
        out_specs=[pl.BlockSpec((rb, PROJ_PAD), lambda i: (i, 0)),
                   pl.BlockSpec((4, W3), lambda i: (0, 0)), pl.BlockSpec((2, 128), lambda i: (0, 0))],
        out_shape=[jax.ShapeDtypeStruct((T, PROJ_PAD), BF16),
                   jax.ShapeDtypeStruct((4, W3), F32), jax.ShapeDtypeStruct((2, 128), F32)],
        scratch_shapes=[pltpu.VMEM((3 * N_HEADS, rb + 16, HEAD_DIM), F32), pltpu.VMEM((3 * N_HEADS, re, HEAD_DIM), F32)],
        compiler_params=_cp("arbitrary"))(p, p, p, p, conv_w, alog_row, dtb_row, dq, dq, dk, dk, dv, dv, dbg, dp_mid)


def _tri(incl):
    ii = lax.broadcasted_iota(jnp.int32, (CHUNK, CHUNK), 0)
    jj = lax.broadcasted_iota(jnp.int32, (CHUNK, CHUNK), 1)
    return ii, jj, ((ii >= jj) if incl else (ii > jj))


def _dn_chunk(k, bg, cb=4):
    T = k.shape[0]
    N = T // CHUNK
    cb = min(cb, N)

    def body(k_ref, bg_ref, gc_ref, gct_ref, l_ref):
        ii, jj, incl = _tri(True)
        tri = incl.astype(F32)
        U = range(cb)
        bgv = [bg_ref[u * CHUNK:(u + 1) * CHUNK, :] for u in U]
        gc = [jnp.dot(tri, bgv[u], precision=lax.Precision.HIGHEST, preferred_element_type=F32) for u in U]
        gct = [gc[u].T for u in U]
        kk = [[None] * N_HEADS for _ in U]
        for u in U:
            gc_ref[u * CHUNK:(u + 1) * CHUNK, :] = gc[u]
            gct_ref[u] = gct[u][0:8]
            for h in range(N_HEADS):
                kh = k_ref[u * CHUNK:(u + 1) * CHUNK, h * HEAD_DIM:(h + 1) * HEAD_DIM]
                kk[u][h] = _nt(kh * bgv[u][:, h:h + 1], kh)
        for u in U:
            for h in range(N_HEADS):
                gcol = gc[u][:, N_HEADS + h:N_HEADS + h + 1]
                grow = gct[u][N_HEADS + h:N_HEADS + h + 1, :]
                l_ref[u, h] = kk[u][h] * jnp.exp(jnp.where(ii > jj, gcol - grow, NEG))

    rows = cb * CHUNK
    return pl.pallas_call(
        body, name="dn_chunk", grid=(N // cb,),
        in_specs=[pl.BlockSpec((rows, DN_WIDTH), lambda n: (n, 0)), pl.BlockSpec((rows, 128), lambda n: (n, 0))],
        out_specs=[pl.BlockSpec((rows, 128), lambda n: (n, 0)), pl.BlockSpec((cb, 8, CHUNK), lambda n: (n, 0, 0)),
                   pl.BlockSpec((cb, N_HEADS, CHUNK, CHUNK), lambda n: (n, 0, 0, 0))],
        out_shape=[jax.ShapeDtypeStruct((T, 128), F32), jax.ShapeDtypeStruct((N, 8, CHUNK), F32),
                   jax.ShapeDtypeStruct((N, N_HEADS, CHUNK, CHUNK), F32)],
        compiler_params=_cp("parallel"))(k, bg)


def _tri_inv(lt):
    S = lt.shape[1]

    def body(l_ref, a_ref):
        sub = lax.broadcasted_iota(jnp.int32, (8, S), 0)
        groups = CHUNK // 8
        for i in range(CHUNK):
            acc = [((sub + 8 * k) == i).astype(F32) for k in range(groups)]
            for jb in range((i + 7) // 8):
                nk = jb + 1

                def step(j, carry, nk=nk, i=i):
                    lrow = l_ref[pl.ds(i * CHUNK + j, 1), :]
                    return tuple(carry[k] - lrow * a_ref[j, 8 * k:8 * k + 8, :] for k in range(nk))

                acc[:nk] = list(lax.fori_loop(8 * jb, min(8 * jb + 8, i), step, tuple(acc[:nk])))
            for k in range(groups):
                a_ref[i, 8 * k:8 * k + 8, :] = acc[k]

    return pl.pallas_call(
        body, name="tri_inv", out_shape=jax.ShapeDtypeStruct((CHUNK, CHUNK, S), F32),
        compiler_params=pltpu.CompilerParams(vmem_limit_bytes=VMEM_LIMIT))(lt)


def _dn_head_terms(qh, kh, vh, beta, gcol, grow):
    ii, jj, incl = _tri(True)
    gam = jnp.exp(jnp.where(incl, gcol - grow, NEG))
    glast = grow[:, CHUNK - 1:CHUNK]
    cd = jnp.exp(glast)
    shape = (CHUNK, HEAD_DIM)
    E = jnp.broadcast_to(jnp.exp(gcol), shape)
    Fd = jnp.broadcast_to(jnp.exp(glast - gcol), shape)
    beta = jnp.broadcast_to(beta, shape)
    kb = kh * beta
    return dict(ii=ii, jj=jj, gam=gam, E=E, F=Fd, beta=beta, cd=cd, kb=kb, vb=vh * beta, W=kb * E, qE=qh * E,
                kt=kh * Fd)


def _apply_a(a, u):
    hi, lo = _split(a)
    ub = _bf(u)
    return jnp.dot(hi, ub, preferred_element_type=F32) + jnp.dot(lo, ub, preferred_element_type=F32)


def _dn_scan(q, k, v, bg, gc, gct, a):
    T = q.shape[0]
    N = T // CHUNK
    cb = min(SCAN_CHUNKS_FWD, N)

    def body(q_ref, k_ref, v_ref, bg_ref, gc_ref, gct_ref, a_ref, o_ref, sall_ref, s_ref):
        @pl.when(pl.program_id(0) == 0)
        def _():
            s_ref[...] = jnp.zeros_like(s_ref)
        H = range(N_HEADS)
        sl = [slice(h * HEAD_DIM, (h + 1) * HEAD_DIM) for h in H]
        pre = []
        for u in range(cb):
            r = slice(u * CHUNK, (u + 1) * CHUNK)
            bgv, gcv, gctv = bg_ref[r, :], gc_ref[r, :], gct_ref[u]
            q_, k_ = [q_ref[r, s] for s in sl], [k_ref[r, s] for s in sl]
            t = [_dn_head_terms(q_[h], k_[h], v_ref[r, sl[h]], bgv[:, h:h + 1],
                                gcv[:, N_HEADS + h:N_HEADS + h + 1], gctv[N_HEADS + h:N_HEADS + h + 1, :]) for h in H]
            P = [_nt(q_[h], k_[h]) * t[h]["gam"] for h in H]
            pre.append((r, t, P))
        S = [s_ref[h] for h in H]
        for u in range(cb):
            r, t, P = pre[u]
            for h in H:
                sall_ref[u, h] = S[h]
            WS = [_nn(t[h]["W"], S[h]) for h in H]
            qS = [_nn(t[h]["qE"], S[h]) for h in H]
            vn = [_apply_a(a_ref[u, h], t[h]["vb"] - WS[h]) for h in H]
            Pv = [_nn(P[h], vn[h]) for h in H]
            kv = [_tn(t[h]["kt"], vn[h]) for h in H]
            for h in H:
                o_ref[r, sl[h]] = qS[h] + Pv[h]
            S = [t[h]["cd"] * S[h] + kv[h] for h in H]
        for h in H:
            s_ref[h] = S[h]

    row512 = pl.BlockSpec((cb * CHUNK, DN_WIDTH), lambda n: (n, 0))
    row128 = pl.BlockSpec((cb * CHUNK, 128), lambda n: (n, 0))
    return pl.pallas_call(
        body, name="dn_scan", grid=(N // cb,),
        in_specs=[row512, row512, row512, row128, row128, pl.BlockSpec((cb, 8, CHUNK), lambda n: (n, 0, 0)),
                  pl.BlockSpec((cb, N_HEADS, CHUNK, CHUNK), lambda n: (n, 0, 0, 0))],
        out_specs=[row512, pl.BlockSpec((cb, N_HEADS, HEAD_DIM, HEAD_DIM), lambda n: (n, 0, 0, 0))],
        out_shape=[jax.ShapeDtypeStruct((T, DN_WIDTH), F32),
                   jax.ShapeDtypeStruct((N, N_HEADS, HEAD_DIM, HEAD_DIM), F32)],
        scratch_shapes=[pltpu.VMEM((N_HEADS, HEAD_DIM, HEAD_DIM), F32)],
        compiler_params=_cp("arbitrary"))(q, k, v, bg, gc, gct, a)


def _dn_scan_bwd(q, k, v, bg, gc, gct, a, a_t, sall, do, dep=None):
    T = q.shape[0]
    N = T // CHUNK

    cb = min(SCAN_CHUNKS, N)
    nb = N // cb

    def body(q_ref, k_ref, v_ref, bg_ref, gc_ref, gct_ref, a_ref, at_ref, sall_ref, do_ref, *rest):
        dq_ref, dk_ref, dv_ref, dbg_ref, ds_ref = rest[-5:]
        @pl.when(pl.program_id(0) == 0)
        def _():
            ds_ref[...] = jnp.zeros_like(ds_ref)
        lane = _lane_iota((CHUNK, 128))
        rowi = lax.broadcasted_iota(jnp.int32, (CHUNK, 1), 0)
        ii, jj, _ = _tri(True)
        rev = (jj >= ii).astype(F32)
        H = range(N_HEADS)
        sl = [slice(h * HEAD_DIM, (h + 1) * HEAD_DIM) for h in H]
        pre = {}
        for u in reversed(range(cb)):
            r = slice(u * CHUNK, (u + 1) * CHUNK)
            bgv, gcv, gctv = bg_ref[r, :], gc_ref[r, :], gct_ref[u]
            q_, k_, v_ = [q_ref[r, s] for s in sl], [k_ref[r, s] for s in sl], [v_ref[r, s] for s in sl]
            dO = [do_ref[r, s] for s in sl]
            t = [_dn_head_terms(q_[h], k_[h], v_[h], bgv[:, h:h + 1], gcv[:, N_HEADS + h:N_HEADS + h + 1],
                                gctv[N_HEADS + h:N_HEADS + h + 1, :]) for h in H]
            beta = [t[h]["beta"] for h in H]
            S = [sall_ref[u, h] for h in H]
            A = [a_ref[u, h] for h in H]
            WS = [_nn(t[h]["W"], S[h]) for h in H]
            KK = [_nt(t[h]["kb"], k_[h]) for h in H]
            QK = [_nt(q_[h], k_[h]) for h in H]
            d_qE = [_nt(dO[h], S[h]) for h in H]
            vn = [_apply_a(A[h], t[h]["vb"] - WS[h]) for h in H]
            PtdO = [_tn(QK[h] * t[h]["gam"], dO[h]) for h in H]
            qEdO = [_tn(t[h]["qE"], dO[h]) for h in H]
            dOvn = [_nt(dO[h], vn[h]) for h in H]
            dQK = [jnp.where(ii >= jj, dOvn[h], 0.0) * t[h]["gam"] for h in H]
            dQKk = [_nn(dQK[h], k_[h]) for h in H]
            dQKq = [_tn(dQK[h], q_[h]) for h in H]
            pre[u] = (r, q_, k_, v_, beta, t, S, A, KK, QK, d_qE, vn, PtdO, qEdO, dQK, dQKk, dQKq)
        dSn = [ds_ref[h] for h in H]
        for u in reversed(range(cb)):
            r, q_, k_, v_, beta, t, S, A, KK, QK, d_qE, vn, PtdO, qEdO, dQK, dQKk, dQKq = pre[u]
            gam, E, Fd, cd, kb = ([t[h][n] for h in H] for n in ("gam", "E", "F", "cd", "kb"))
            ktdS = [_nn(t[h]["kt"], dSn[h]) for h in H]
            dU = [_apply_a(at_ref[u, h], PtdO[h] + ktdS[h]) for h in H]
            d_kt = [_nt(vn[h], dSn[h]) for h in H]
            dUvn = [_nt(dU[h], vn[h]) for h in H]
            dUS = [_nt(dU[h], S[h]) for h in H]
            WdU = [_tn(t[h]["W"], dU[h]) for h in H]
            d_cd = [jnp.sum(S[h] * dSn[h]) for h in H]
            dSn = [cd[h] * dSn[h] + qEdO[h] - WdU[h] for h in H]
            dKK = [jnp.where(ii > jj, -dUvn[h], 0.0) * gam[h] for h in H]
            dKKk = [_nn(dKK[h], k_[h]) for h in H]
            dKKkb = [_tn(dKK[h], kb[h]) for h in H]
            dbeta_arr = jnp.zeros((CHUNK, 128), F32)
            dgc_arr = jnp.zeros((CHUNK, 128), F32)
            for h in H:
                dW = -dUS[h]
                dq_ref[r, sl[h]] = dQKk[h] + d_qE[h] * E[h]
                d_kb = dKKk[h] + dW * E[h]
                dk_ref[r, sl[h]] = dQKq[h] + dKKkb[h] + d_kb * beta[h] + d_kt[h] * Fd[h]
                dv_ref[r, sl[h]] = dU[h] * beta[h]
                Z = dQK[h] * QK[h] + dKK[h] * KK[h]
                dbeta = jnp.sum(dU[h] * v_[h] + d_kb * k_[h], axis=-1, keepdims=True)
                m_e = (dW * kb[h] + d_qE[h] * q_[h]) * E[h]
                m_f = d_kt[h] * k_[h] * Fd[h]
                zdiag = jnp.where(ii == jj, jnp.sum(Z, axis=0, keepdims=True), 0.0)
                dgc = (jnp.sum(m_e - m_f, axis=-1, keepdims=True) + jnp.sum(Z - zdiag, axis=-1, keepdims=True)
                       + jnp.where(rowi == CHUNK - 1, jnp.sum(m_f) + d_cd[h] * cd[h], 0.0))
                dbeta_arr = dbeta_arr + jnp.where(lane == h, dbeta, 0.0)
                dgc_arr = dgc_arr + jnp.where(lane == N_HEADS + h, dgc, 0.0)
            dbg_ref[r, :] = dbeta_arr + jnp.dot(rev, dgc_arr, precision=lax.Precision.HIGHEST,
                                                preferred_element_type=F32)
        for h in H:
            ds_ref[h] = dSn[h]

    row512 = pl.BlockSpec((cb * CHUNK, DN_WIDTH), lambda n: (nb - 1 - n, 0))
    row128 = pl.BlockSpec((cb * CHUNK, 128), lambda n: (nb - 1 - n, 0))
    in_specs, args = _with_dep(
        [row512, row512, row512, row128, row128,
         pl.BlockSpec((cb, 8, CHUNK), lambda n: (nb - 1 - n, 0, 0)),
         pl.BlockSpec((cb, N_HEADS, CHUNK, CHUNK), lambda n: (nb - 1 - n, 0, 0, 0)),
         pl.BlockSpec((cb, N_HEADS, CHUNK, CHUNK), lambda n: (nb - 1 - n, 0, 0, 0)),
         pl.BlockSpec((cb, N_HEADS, HEAD_DIM, HEAD_DIM), lambda n: (nb - 1 - n, 0, 0, 0)), row512],
        [q, k, v, bg, gc, gct, a, a_t, sall, do], dep)
    return pl.pallas_call(
        body, name="dn_scan_bwd", grid=(nb,), in_specs=in_specs,
        out_specs=[row512, row512, row512, row128],
        out_shape=[jax.ShapeDtypeStruct((T, DN_WIDTH), F32)] * 3 + [jax.ShapeDtypeStruct((T, 128), F32)],
        scratch_shapes=[pltpu.VMEM((N_HEADS, HEAD_DIM, HEAD_DIM), F32)],
        compiler_params=_cp("arbitrary"))(*args)


def _sg_mask():
    ii = lax.broadcasted_iota(jnp.int32, (SG_BLOCK, SG_BLOCK), 0) // CHUNK
    jj = lax.broadcasted_iota(jnp.int32, (SG_BLOCK, SG_BLOCK), 1) // CHUNK
    return jj <= ii


def _mix_fwd(o, p, ong, sgn, sgw, sgbt):
    T = o.shape[0]
    rb = SG_BLOCK

    def body(o_ref, gate_ref, u_ref, vg_ref, ong_ref, sgn_ref, sgw_ref, sgbt_ref, mix_ref):
        mask = _sg_mask()
        gate = gate_ref[...]
        for h in range(N_HEADS):
            sl = slice(h * HEAD_DIM, (h + 1) * HEAD_DIM)
            oh = o_ref[:, sl]
            r = lax.rsqrt(jnp.mean(oh * oh, axis=-1, keepdims=True) + EPS)
            mix_ref[:, sl] = (oh * r * ong_ref[...] * _silu(gate[:, sl])).astype(BF16)
        for gi in range(SG_GROUPS):
            sl = slice(gi * SG_BLOCK, (gi + 1) * SG_BLOCK)
            gv = _gelu(vg_ref[:, sl])
            r = lax.rsqrt(jnp.mean(gv * gv, axis=-1, keepdims=True) + EPS)
            vh = gv * r * sgn_ref[:, sl]
            s = _nn(jnp.where(mask, sgw_ref[gi], 0.0), vh) + sgbt_ref[:, gi:gi + 1]
            mix_ref[:, DN_WIDTH + gi * SG_BLOCK:DN_WIDTH + (gi + 1) * SG_BLOCK] = (_gelu(u_ref[:, sl]) * s).astype(BF16)

    def col(c):
        return pl.BlockSpec((rb, 512), lambda i: (i, c))
    return pl.pallas_call(
        body, name="mix_fwd", grid=(T // rb,),
        in_specs=[pl.BlockSpec((rb, DN_WIDTH), lambda i: (i, 0)), col(3), col(4), col(5),
                  pl.BlockSpec((1, 128), lambda i: (0, 0)), pl.BlockSpec((1, SG_WIDTH), lambda i: (0, 0)),
                  pl.BlockSpec((SG_GROUPS, SG_BLOCK, SG_BLOCK), lambda i: (0, 0, 0)),
                  pl.BlockSpec((SG_BLOCK, 128), lambda i: (0, 0))],
        out_specs=pl.BlockSpec((rb, D_MODEL), lambda i: (i, 0)),
        out_shape=jax.ShapeDtypeStruct((T, D_MODEL), BF16),
        compiler_params=_cp("parallel"))(o, p, p, p, ong, sgn, sgw, sgbt)


def _mix_bwd(o, p, ong, sgn, sgw, sgbt, dmix, dep=None):
    T = o.shape[0]
    rb = SG_BLOCK

    def body(o_ref, gate_ref, u_ref, vg_ref, ong_ref, sgn_ref, sgw_ref, sgbt_ref, dmix_ref, *rest):
        do_ref, dp_ref, gong_ref, gsgn_ref, gsgw_ref, gsgbt_ref = rest[-6:]
        @pl.when(pl.program_id(0) == 0)
        def _():
            gong_ref[...] = jnp.zeros_like(gong_ref)
            gsgn_ref[...] = jnp.zeros_like(gsgn_ref)
            gsgw_ref[...] = jnp.zeros_like(gsgw_ref)
            gsgbt_ref[...] = jnp.zeros_like(gsgbt_ref)
        mask = _sg_mask()
        gate = gate_ref[...]
        lane = _lane_iota((SG_BLOCK, 128))
        for h in range(N_HEADS):
            sl = slice(h * HEAD_DIM, (h + 1) * HEAD_DIM)
            oh = o_ref[:, sl]
            dm = dmix_ref[:, sl]
            r = lax.rsqrt(jnp.mean(oh * oh, axis=-1, keepdims=True) + EPS)
            oh_hat = oh * r
            gt = gate[:, sl]
            sg = _silu(gt)
            dp_ref[:, sl] = (dm * oh_hat * ong_ref[...] * _dsilu(gt)).astype(BF16)
            dn_ = dm * sg
            gong_ref[...] += jnp.sum(dn_ * oh_hat, axis=0, keepdims=True)
            dhat = dn_ * ong_ref[...]
            do_ref[:, sl] = r * (dhat - oh_hat * jnp.mean(dhat * oh_hat, axis=-1, keepdims=True))
        for gi in range(SG_GROUPS):
            sl = slice(gi * SG_BLOCK, (gi + 1) * SG_BLOCK)
            vraw = vg_ref[:, sl]
            gv = _gelu(vraw)
            r = lax.rsqrt(jnp.mean(gv * gv, axis=-1, keepdims=True) + EPS)
            vhat = gv * r
            vn = vhat * sgn_ref[:, sl]
            wm = jnp.where(mask, sgw_ref[gi], 0.0)
            s = _nn(wm, vn) + sgbt_ref[:, gi:gi + 1]
            uraw = u_ref[:, sl]
            dm = dmix_ref[:, DN_WIDTH + gi * SG_BLOCK:DN_WIDTH + (gi + 1) * SG_BLOCK]
            dp_ref[:, DN_WIDTH + gi * SG_BLOCK:DN_WIDTH + (gi + 1) * SG_BLOCK] = (dm * s * _dgelu(uraw)).astype(BF16)
            ds = dm * _gelu(uraw)
            gsgbt_ref[...] += jnp.where(lane == gi, jnp.sum(ds, axis=-1, keepdims=True), 0.0)
            gsgw_ref[gi] += jnp.where(mask, _nt(ds, vn), 0.0)
            dvn = _tn(wm, ds)
            gsgn_ref[:, sl] += jnp.sum(dvn * vhat, axis=0, keepdims=True)
            dhat = dvn * sgn_ref[:, sl]
            dgv = r * (dhat - vhat * jnp.mean(dhat * vhat, axis=-1, keepdims=True))
            dp_ref[:, 2 * DN_WIDTH + gi * SG_BLOCK:2 * DN_WIDTH + (gi + 1) * SG_BLOCK] = (dgv * _dgelu(vraw)).astype(BF16)

    def col(c):
        return pl.BlockSpec((rb, 512), lambda i: (i, c))
    full = lambda *s: pl.BlockSpec(s, lambda i: (0,) * len(s))
    in_specs, args = _with_dep(
        [pl.BlockSpec((rb, DN_WIDTH), lambda i: (i, 0)), col(3), col(4), col(5),
         full(1, 128), full(1, SG_WIDTH), full(SG_GROUPS, SG_BLOCK, SG_BLOCK), full(SG_BLOCK, 128),
         pl.BlockSpec((rb, D_MODEL), lambda i: (i, 0))],
        [o, p, p, p, ong, sgn, sgw, sgbt, dmix], dep)
    return pl.pallas_call(
        body, name="mix_bwd", grid=(T // rb,), in_specs=in_specs,
        out_specs=[pl.BlockSpec((rb, DN_WIDTH), lambda i: (i, 0)), pl.BlockSpec((rb, 3 * 512), lambda i: (i, 0)),
                   full(1, 128), full(1, SG_WIDTH), full(SG_GROUPS, SG_BLOCK, SG_BLOCK), full(SG_BLOCK, 128)],
        out_shape=[jax.ShapeDtypeStruct((T, DN_WIDTH), F32), jax.ShapeDtypeStruct((T, 3 * 512), BF16),
                   jax.ShapeDtypeStruct((1, 128), F32), jax.ShapeDtypeStruct((1, SG_WIDTH), F32),
                   jax.ShapeDtypeStruct((SG_GROUPS, SG_BLOCK, SG_BLOCK), F32),
                   jax.ShapeDtypeStruct((SG_BLOCK, 128), F32)],
        compiler_params=_cp("arbitrary"))(*args)


def _pad_lanes(row, offset=0):
    n = row.shape[1]
    return jnp.pad(row, ((0, 0), (offset, 128 - n - offset)))


def _local_step(x, tgt, w, dep=None, late_weights=None, on_grad=None):
    T = x.shape[0]
    N = T // CHUNK
    on_grad = on_grad or (lambda name, g: None)
    alog_row = _pad_lanes(w["dn_a_log"], N_HEADS)
    dtb_row = _pad_lanes(w["dn_dt_bias"], N_HEADS)
    sgbt = jnp.pad(w["sg_b"].T, ((0, 0), (0, 128 - SG_GROUPS)))

    p, h1, w_in_pad = _in_proj(x, w["attn_norm_g"], w["w_in"], dep=dep)
    q, k, v, bg = _dn_act(p, w["dn_conv_w"], alog_row, dtb_row)
    gc, gct, lmat = _dn_chunk(k, bg)
    lt = lmat.reshape(N * N_HEADS, CHUNK * CHUNK).T
    at = _tri_inv(lt)
    a = at.reshape(CHUNK * CHUNK, N * N_HEADS).T.reshape(N, N_HEADS, CHUNK, CHUNK)
    a_t = at.transpose(1, 0, 2).reshape(CHUNK * CHUNK, N * N_HEADS).T.reshape(N, N_HEADS, CHUNK, CHUNK)
    o, sall = _dn_scan(q, k, v, bg, gc, gct, a)
    mix = _mix_fwd(o, p, w["dn_out_norm_g"], w["sg_norm_g"], w["sg_w"], sgbt)
    if late_weights is not None:
        w = {**w, **late_weights("out_proj", mix)}
    x2, h2 = _out_proj(mix, w["w_out"], x, w["ffn_norm_g"])
    up, act = _up_proj_act(h2, w["w_up"], w["ffn_conv_w"], w["ffn_conv_b"])
    if late_weights is not None:
        w = {**w, **late_weights("down_proj", act)}
    loss, dx3, g_final = _down_proj_loss(act, w["w_down"], x2, tgt, w["final_norm_g"])

    dact = _mm_nt("d_act", dx3, w["w_down"], F32, 512, D_FF)
    g_w_down = _mm_tn("g_w_down", act, dx3, D_FF, 1024, 1024)
    tok = on_grad("w_down", g_w_down)
    dup, g_ffn_conv_w, g_ffn_conv_b = _ffn_act_bwd(up, dact, w["ffn_conv_w"], w["ffn_conv_b"], dep=tok)
    g_w_up = _mm_tn("g_w_up", h2, dup, 1024, 2 * D_FF // 4, 2048, col_major_tiles=True)
    tok = on_grad("w_up", g_w_up)
    dx2, g_ffn_norm = _mm_nt_rms_bwd("d_h2", dup, w["w_up"], x2, w["ffn_norm_g"], dx3, dep=tok)
    dmix = _mm_nt("d_mix", dx2, w["w_out"], F32, 512, 1024)
    g_w_out = _mm_tn("g_w_out", mix, dx2, 1024, 1024, 1024)
    tok = on_grad("w_out", g_w_out)
    do, dp_mid, g_ong, g_sgn, g_sgw, g_sgbt = _mix_bwd(o, p, w["dn_out_norm_g"], w["sg_norm_g"], w["sg_w"], sgbt,
                                                      dmix, dep=tok)
    early = dict(dn_out_norm_g=g_ong, sg_norm_g=g_sgn, sg_w=g_sgw, sg_bt=g_sgbt,
                 ffn_norm_g=g_ffn_norm, ffn_conv_w=g_ffn_conv_w, ffn_conv_b=g_ffn_conv_b, final_norm_g=g_final)
    tok = on_grad("small_early", early)
    dq, dk, dv, dbg = _dn_scan_bwd(q, k, v, bg, gc, gct, a, a_t, sall, do, dep=tok)
    dp, g_dn_conv_w, g_ad = _dn_act_bwd(p, w["dn_conv_w"], alog_row, dtb_row, dq, dk, dv, dbg, dp_mid)
    g_w_in = _mm_tn("g_w_in", h1, dp, 1024, PROJ_PAD, 1024, col_groups=(4, PROJ_COLS // 4))
    tok = on_grad("w_in", g_w_in)
    grad_x, g_attn_norm = _mm_nt_rms_bwd("d_h1", dp, w_in_pad, x, w["attn_norm_g"], dx2, dep=tok)

    grads = dict(attn_norm_g=g_attn_norm, w_in=g_w_in, dn_conv_w=g_dn_conv_w, a_dt=g_ad,
                 w_out=g_w_out, w_up=g_w_up, w_down=g_w_down, **early)
    return loss, grad_x, grads


def _me():
    return lax.axis_index("x"), lax.axis_index("y"), lax.axis_index("c")


def _peer(rel):
    x, y, c = _me()
    return {"x": (1 - x, y, c), "y": (x, 1 - y, c), "xy": (1 - x, 1 - y, c), "c": (x, y, 1 - c)}[rel]


def _chip_of(dev):
    return 2 * dev[0] + dev[1]


CHIP_RELS = ("x", "y", "xy")


def _run_copies(copies, sends, recvs):
    for cp in copies:
        cp.start()
    for cp in recvs:
        cp.wait_recv()
    for cp in sends:
        cp.wait_send()


def _gather_first(w_shard, small_shard):
    R = w_shard.shape[0]
    r2 = R // 2

    def body(w_ref, s_ref, w_out, s_out, send_sems, recv_sems):
        x, y, c = _me()
        me = _chip_of((x, y))
        sib = _peer("c")

        def half(chip, core):
            return w_out.at[chip, pl.ds(pl.multiple_of(core * r2, 8), r2), :]

        def copy(k, src, dst, to):
            return pltpu.make_async_remote_copy(src_ref=src, dst_ref=dst, send_sem=send_sems.at[k],
                                                recv_sem=recv_sems.at[k], device_id=to, device_id_type=MESH)

        own_rows = w_ref.at[pl.ds(pl.multiple_of(c * r2, 8), r2), :]
        first = [copy(r, own_rows, half(me, c), _peer(rel)) for r, rel in enumerate(CHIP_RELS)]
        first += [copy(3 + r, s_ref, s_out.at[me], _peer(rel)) for r, rel in enumerate(CHIP_RELS)]
        for cp in first:
            cp.start()
        passed = []
        for r, rel in enumerate(CHIP_RELS):
            their = _chip_of(_peer(rel))
            copy(r, own_rows, half(their, c), _peer(rel)).wait_recv()
            fwd = copy(6 + r, half(their, c), half(their, c), sib)
            fwd.start()
            passed.append(fwd)
        for r, rel in enumerate(CHIP_RELS):
            their = _chip_of(_peer(rel))
            copy(3 + r, s_ref, s_out.at[their], _peer(rel)).wait_recv()
            copy(6 + r, own_rows, half(their, 1 - c), sib).wait_recv()
        for cp in first + passed:
            cp.wait_send()

    w_all, s_all = pl.pallas_call(
        body, name="gather_first", in_specs=[ANY, ANY], out_specs=[ANY, ANY],
        out_shape=[jax.ShapeDtypeStruct((4,) + w_shard.shape, w_shard.dtype),
                   jax.ShapeDtypeStruct((4,) + small_shard.shape, small_shard.dtype)],
        scratch_shapes=[pltpu.SemaphoreType.DMA((9,)), pltpu.SemaphoreType.DMA((9,))])(w_shard, small_shard)
    me = _chip_of(_me())
    return (lax.dynamic_update_index_in_dim(w_all, w_shard, me, 0),
            lax.dynamic_update_index_in_dim(s_all, small_shard, me, 0))


OTHERS = tuple((fx, fy, fc) for fx in (0, 1) for fy in (0, 1) for fc in (0, 1) if (fx, fy, fc) != (0, 0, 0))


def _other(flip):
    x, y, c = _me()
    return (x ^ flip[0], y ^ flip[1], c ^ flip[2])


def _linear(dev):
    return 4 * dev[0] + 2 * dev[1] + dev[2]


def _exchange_small(small):
    def body(small_ref, out_ref, send_sems, recv_sems):
        my_slot = _linear(_me())
        sends, recvs = [], []
        for k, flip in enumerate(OTHERS):
            peer = _other(flip)
            sends.append(pltpu.make_async_remote_copy(
                src_ref=small_ref, dst_ref=out_ref.at[my_slot], send_sem=send_sems.at[k], recv_sem=recv_sems.at[k],
                device_id=peer, device_id_type=MESH))
            recvs.append(pltpu.make_async_remote_copy(
                src_ref=small_ref, dst_ref=out_ref.at[_linear(peer)], send_sem=send_sems.at[k],
                recv_sem=recv_sems.at[k], device_id=peer, device_id_type=MESH))
        _run_copies(sends, sends, recvs)

    out = pl.pallas_call(
        body, name="exchange_small", in_specs=[ANY], out_specs=ANY,
        out_shape=jax.ShapeDtypeStruct((8,) + small.shape, small.dtype),
        scratch_shapes=[pltpu.SemaphoreType.DMA((7,)), pltpu.SemaphoreType.DMA((7,))])(small)
    return lax.dynamic_update_index_in_dim(out, small, _linear(_me()), 0)


def _pair_swap(halves):
    n = len(halves)

    def body(*refs):
        src, out = refs[:n], refs[n:2 * n]
        send_sems, recv_sems = refs[2 * n:]
        sib = _peer("c")
        copies = [pltpu.make_async_remote_copy(
            src_ref=src[i], dst_ref=out[i], send_sem=send_sems.at[i], recv_sem=recv_sems.at[i],
            device_id=sib, device_id_type=MESH) for i in range(n)]
        _run_copies(copies, copies, copies)

    return pl.pallas_call(
        body, name="pair_swap", in_specs=[ANY] * n, out_specs=[ANY] * n,
        out_shape=[jax.ShapeDtypeStruct(h.shape, h.dtype) for h in halves],
        scratch_shapes=[pltpu.SemaphoreType.DMA((n,)), pltpu.SemaphoreType.DMA((n,))])(*halves)


HBM = pl.BlockSpec(memory_space=pltpu.HBM)
SEM = pl.BlockSpec(memory_space=pltpu.SEMAPHORE)
EFFECT = pltpu.SideEffectType.DATAFLOW_SIDE_EFFECTING


def _hbm(a):
    return pltpu.with_memory_space_constraint(a, pltpu.HBM)


def _transfer_start(name, srcs, lands, n_copies, make_copies, after=None):
    n, m = len(srcs), len(lands)

    def body(*refs):
        src, land = refs[:n], refs[n:n + m]
        outs = refs[n + m + (after is not None):]
        send_sems, recv_sems, token = outs[0], outs[1], outs[-1]
        for cp in make_copies(src, land, send_sems, recv_sems):
            cp.start()
        token[...] = jnp.zeros_like(token)

    arrs = list(srcs) + list(lands)
    in_specs, args = _with_dep([HBM] * (n + m), [_hbm(a) for a in arrs], after)
    out = pl.pallas_call(
        body, name=name,
        out_shape=(pltpu.SemaphoreType.DMA((n_copies,)), pltpu.SemaphoreType.DMA((n_copies,)),
                   *[pltpu.HBM(a.shape, a.dtype) for a in arrs], jax.ShapeDtypeStruct((8, 128), F32)),
        in_specs=in_specs,
        out_specs=(SEM, SEM, *[HBM] * (n + m), pl.BlockSpec(memory_space=pltpu.VMEM)),
        input_output_aliases={i: 2 + i for i in range(n + m)},
        compiler_params=pltpu.CompilerParams(has_side_effects=EFFECT))(*args)
    return out[0], out[1], list(out[2:2 + n]), list(out[2 + n:2 + n + m]), out[-1]


def _transfer_wait(name, send_sems, recv_sems, srcs, lands, make_copies, after):
    n, m = len(srcs), len(lands)

    def body(*refs):
        src, land = refs[:n], refs[n:n + m]
        s_sems, r_sems = refs[n + m], refs[n + m + 1]
        for cp in make_copies(src, land, s_sems, r_sems):
            cp.wait_send()
            cp.wait_recv()

    arrs = list(srcs) + list(lands)
    out = pl.pallas_call(
        body, name=name, out_shape=tuple(pltpu.HBM(a.shape, a.dtype) for a in arrs),
        in_specs=[HBM] * (n + m) + [SEM, SEM, ANY], out_specs=tuple([HBM] * (n + m)),
        input_output_aliases={i: i for i in range(n + m)},
        compiler_params=pltpu.CompilerParams(has_side_effects=EFFECT))(*arrs, send_sems, recv_sems, after)
    return list(out[:n]), list(out[n:])


def _gather_copies(src, land, send_sems, recv_sems):
    me = _chip_of(_me())
    copies = []
    for i in range(len(src)):
        for r, rel in enumerate(CHIP_RELS):
            k = 3 * i + r
            copies.append(pltpu.make_async_remote_copy(
                src_ref=src[i], dst_ref=land[i].at[me], send_sem=send_sems.at[k], recv_sem=recv_sems.at[k],
                device_id=_peer(rel), device_id_type=MESH))
    return copies


def _small_copies(src, land, send_sems, recv_sems):
    my_slot = _linear(_me())
    return [pltpu.make_async_remote_copy(
        src_ref=src[0], dst_ref=land[0].at[my_slot], send_sem=send_sems.at[k], recv_sem=recv_sems.at[k],
        device_id=_other(flip), device_id_type=MESH) for k, flip in enumerate(OTHERS)]


def _pieces_copies(src, land, send_sems, recv_sems):
    copies = []
    for k, flip in enumerate(OTHERS):
        peer = _other(flip)
        copies.append(pltpu.make_async_remote_copy(
            src_ref=src[0].at[_linear(peer)], dst_ref=land[0].at[k], send_sem=send_sems.at[k],
            recv_sem=recv_sems.at[k], device_id=peer, device_id_type=MESH))
    return copies


def _row_block(rows, cols, budget=2 * 1024 * 1024):
    rb = max(8, (budget // (4 * cols)) // 8 * 8)
    while rows % rb:
        rb -= 8
    return rb if rb > 0 else rows


def _sum_slots(name, first, rest):
    R, Cc = first.shape
    K = rest.shape[0]
    rb = _row_block(R, Cc)

    def body(f_ref, r_ref, o_ref):
        acc = f_ref[...].astype(F32)
        for j in range(K):
            acc = acc + r_ref[j].astype(F32)
        o_ref[...] = acc

    return pl.pallas_call(
        body, name=name, grid=(R // rb,),
        in_specs=[pl.BlockSpec((rb, Cc), lambda i: (i, 0)), pl.BlockSpec((K, rb, Cc), lambda i: (0, i, 0))],
        out_specs=pl.BlockSpec((rb, Cc), lambda i: (i, 0)),
        out_shape=jax.ShapeDtypeStruct((R, Cc), F32), compiler_params=_cp("parallel"))(first, rest)


def _adamw_math(w, gv, m, v):
    mn = ADAM_B1 * m + (1.0 - ADAM_B1) * gv
    vn = ADAM_B2 * v + (1.0 - ADAM_B2) * (gv * gv)
    m_hat = mn / (1.0 - ADAM_B1 ** ADAM_STEP)
    v_hat = vn / (1.0 - ADAM_B2 ** ADAM_STEP)
    return -ADAM_LR * (m_hat / (jnp.sqrt(v_hat) + ADAM_EPS) + ADAM_WD * w), mn, vn


def _adamw_halves(name, w, mine, theirs, m, v, core):
    R, Cc = w.shape
    r2 = R // 2
    rb = _row_block(r2, Cc, 1024 * 1024)
    nb2 = r2 // rb

    def body(c_ref, w_ref, mine_ref, theirs_ref, m_ref, v_ref, g_ref, d_ref, mo_ref, vo_ref):
        is_mine = (pl.program_id(0) // nb2) == c_ref[0]
        gv = jnp.where(is_mine, mine_ref[...], theirs_ref[...])
        g_ref[...] = gv
        d_ref[...], mo_ref[...], vo_ref[...] = _adamw_math(w_ref[...], gv, m_ref[...], v_ref[...])

    blk = pl.BlockSpec((rb, Cc), lambda i, c: (i, 0))
    half = lambda own: pl.BlockSpec(
        (rb, Cc), lambda i, c: (jnp.clip(i - (c[0] if own else 1 - c[0]) * nb2, 0, nb2 - 1), 0))
    return pl.pallas_call(
        body, name=name,
        grid_spec=pltpu.PrefetchScalarGridSpec(
            num_scalar_prefetch=1, grid=(2 * nb2,), in_specs=[blk, half(True), half(False), blk, blk],
            out_specs=[blk] * 4),
        out_shape=[jax.ShapeDtypeStruct((R, Cc), F32)] * 4, compiler_params=_cp("parallel"))(core, w, mine, theirs, m, v)


def _adamw_transposed(name, wt, mine, theirs, mt, vt, core):
    Cc, kh_n, _ = wt.shape
    r2 = mine.shape[0]
    per_half = kh_n // 2
    nb = -(-Cc // LANES)

    def body(c_ref, w_ref, mine_ref, theirs_ref, m_ref, v_ref, g_ref, d_ref, mo_ref, vo_ref):
        first = c_ref[0] == 0
        halves = (jnp.where(first, mine_ref[...], theirs_ref[...]).T,
                  jnp.where(first, theirs_ref[...], mine_ref[...]).T)
        for kh in range(kh_n):
            lo = (kh % per_half) * LANES
            g_ref[:, kh, :] = halves[kh // per_half][:, lo:lo + LANES]
        d_ref[...], mo_ref[...], vo_ref[...] = _adamw_math(w_ref[...], g_ref[...], m_ref[...], v_ref[...])

    blk = pl.BlockSpec((LANES, kh_n, LANES), lambda i, c: (i, 0, 0))
    half = pl.BlockSpec((r2, LANES), lambda i, c: (0, i))
    return pl.pallas_call(
        body, name=name,
        grid_spec=pltpu.PrefetchScalarGridSpec(
            num_scalar_prefetch=1, grid=(nb,), in_specs=[blk, half, half, blk, blk], out_specs=[blk] * 4),
        out_shape=[jax.ShapeDtypeStruct(wt.shape, F32)] * 4, compiler_params=_cp("parallel"))(
            core, wt, mine, theirs, mt, vt)


FF_W = 2 * D_FF
FF_CH = FF_W // LANES
DNC_W = 3 * DN_WIDTH
DNC_CH = DNC_W // LANES
E_ONG, E_SGN, E_SGW, E_SGBT = 0, 1, 8, 8 + SG_GROUPS * SG_BLOCK
E_FFN = E_SGBT + SG_BLOCK
E_FCW = E_FFN + D_MODEL // LANES
E_FCB = E_FCW + 3 * FF_CH
E_FIN = E_FCB + FF_CH
EARLY_ROWS = E_FIN + D_MODEL // LANES
L_ATTN, L_DNC = 0, D_MODEL // LANES
L_AD = L_DNC + 4 * DNC_CH
L_LOSS = L_AD + 2
LATE_ROWS = -(-(L_LOSS + 1) // 8) * 8


def _put_rows(out, r0, x):
    k, width = x.shape
    n = width // LANES
    for t in range(k):
        for j in range(n):
            out[r0 + t * n + j:r0 + t * n + j + 1, :] = x[t:t + 1, j * LANES:(j + 1) * LANES]


def _pack_early(ong, sgn, sgw, sgbt, ffn, fcw, fcb, fin):
    def body(ong_ref, sgn_ref, sgw_ref, sgbt_ref, ffn_ref, fcw_ref, fcb_ref, fin_ref, out):
        out[...] = jnp.zeros_like(out)
        _put_rows(out, E_ONG, ong_ref)
        _put_rows(out, E_SGN, sgn_ref)
        for gi in range(SG_GROUPS):
            out[E_SGW + gi * SG_BLOCK:E_SGW + (gi + 1) * SG_BLOCK, :] = sgw_ref[gi]
        out[E_SGBT:E_SGBT + SG_BLOCK, :] = sgbt_ref[...]
        _put_rows(out, E_FFN, ffn_ref)
        _put_rows(out, E_FCW, fcw_ref)
        _put_rows(out, E_FCB, fcb_ref)
        _put_rows(out, E_FIN, fin_ref)

    return pl.pallas_call(body, name="pack_small_early", out_shape=jax.ShapeDtypeStruct((EARLY_ROWS, LANES), F32))(
        ong, sgn, sgw, sgbt, ffn, fcw, fcb, fin)


def _pack_late(attn, dnc, ad, loss_row):
    def body(attn_ref, dnc_ref, ad_ref, loss_ref, out):
        out[...] = jnp.zeros_like(out)
        _put_rows(out, L_ATTN, attn_ref)
        _put_rows(out, L_DNC, dnc_ref)
        out[L_AD:L_AD + 2, :] = ad_ref[...]
        out[L_LOSS:L_LOSS + 1, :] = loss_ref[...]

    return pl.pallas_call(body, name="pack_small_late", out_shape=jax.ShapeDtypeStruct((LATE_ROWS, LANES), F32))(
        attn, dnc, ad, loss_row)


SMALL = ("attn_norm_g", "dn_a_log", "dn_dt_bias", "dn_out_norm_g", "sg_norm_g", "sg_w", "sg_b", "ffn_norm_g",
         "ffn_conv_b", "final_norm_g", "dn_conv_w", "ffn_conv_w")


def _small_update(early_all, late_all, chip, W, M, V):
    n = len(SMALL)
    arrs = [d[k] for d in (W, M, V) for k in SMALL]

    def body(c_ref, e_ref, l_ref, *refs):
        w_, m_, v_ = refs[:n], refs[n:2 * n], refs[2 * n:3 * n]
        loss_ref = refs[3 * n]
        outs = refs[3 * n + 1:]
        g_, d_, mo_, vo_ = outs[:n], outs[n:2 * n], outs[2 * n:3 * n], outs[3 * n:4 * n]
        chip_i = c_ref[0]

        def total(ref, r0, rows=1):
            acc = ref[0, pl.ds(r0, rows), :]
            for s in range(1, 8):
                acc = acc + ref[s, pl.ds(r0, rows), :]
            return acc

        def update(i, idx, g):
            g_[i][idx] = g
            d_[i][idx], mo_[i][idx], vo_[i][idx] = _adamw_math(w_[i][idx], g, m_[i][idx], v_[i][idx])

        def rows_param(name, ref, r0, width):
            i = SMALL.index(name)
            for j in range(width // LANES):
                update(i, (slice(None), slice(j * LANES, (j + 1) * LANES)), total(ref, r0 + j))

        rows_param("attn_norm_g", l_ref, L_ATTN, D_MODEL)
        ad = (total(l_ref, L_AD), total(l_ref, L_AD + 1))
        update(SMALL.index("dn_a_log"), (slice(None), slice(None)), ad[0][:, N_HEADS:2 * N_HEADS])
        update(SMALL.index("dn_dt_bias"), (slice(None), slice(None)), ad[1][:, N_HEADS:2 * N_HEADS])
        rows_param("dn_out_norm_g", e_ref, E_ONG, HEAD_DIM)
        rows_param("sg_norm_g", e_ref, E_SGN, SG_WIDTH)
        sgbt = total(e_ref, E_SGBT, SG_BLOCK).T
        for gi in range(SG_GROUPS):
            update(SMALL.index("sg_w"), (0, gi), total(e_ref, E_SGW + gi * SG_BLOCK, SG_BLOCK))
            update(SMALL.index("sg_b"), (0, slice(gi, gi + 1), slice(None)), sgbt[gi:gi + 1, :])
        rows_param("ffn_norm_g", e_ref, E_FFN, D_MODEL)
        rows_param("ffn_conv_b", e_ref, E_FCB, FF_W)
        rows_param("final_norm_g", e_ref, E_FIN, D_MODEL)
        for name, ref, r0, taps, chunks in (("dn_conv_w", l_ref, L_DNC, 4, DNC_CH), ("ffn_conv_w", e_ref, E_FCW, 3, FF_CH)):
            mine = chunks // 4
            for t in range(taps):
                for j in range(mine):
                    update(SMALL.index(name), (0, slice(t, t + 1), slice(j * LANES, (j + 1) * LANES)),
                           total(ref, r0 + t * chunks + chip_i * mine + j))
        loss_ref[...] = total(l_ref, L_LOSS)

    full = lambda a: pl.BlockSpec(a.shape, lambda i, c, nd=a.ndim: (0,) * nd)
    shapes = [jax.ShapeDtypeStruct(W[k].shape, F32) for k in SMALL]
    outs = pl.pallas_call(
        body, name="small_update",
        grid_spec=pltpu.PrefetchScalarGridSpec(
            num_scalar_prefetch=1, grid=(1,), in_specs=[full(early_all), full(late_all)] + [full(a) for a in arrs],
            out_specs=[pl.BlockSpec((1, LANES), lambda i, c: (0, 0))] + [full(s) for s in shapes] * 4),
        out_shape=[jax.ShapeDtypeStruct((1, LANES), F32)] + shapes * 4,
        compiler_params=pltpu.CompilerParams(vmem_limit_bytes=VMEM_LIMIT))(chip, early_all, late_all, *arrs)
    loss, outs = outs[0], outs[1:]
    return (loss,) + tuple(dict(zip(SMALL, outs[k * n:(k + 1) * n])) for k in range(4))


ORDER =("attn_norm_g", "w_in", "dn_conv_w", "dn_a_log", "dn_dt_bias", "dn_out_norm_g", "sg_norm_g", "sg_w",
         "sg_b", "w_out", "ffn_norm_g", "w_up", "ffn_conv_w", "ffn_conv_b", "w_down", "final_norm_g")


def kernel(x, attn_norm_g, w_in, dn_conv_w, dn_a_log, dn_dt_bias, dn_out_norm_g, sg_norm_g, sg_w, sg_b, w_out, ffn_norm_g, w_up, ffn_conv_w, ffn_conv_b, w_down, final_norm_g, loss_target, m_attn_norm_g, m_w_in, m_dn_conv_w, m_dn_a_log, m_dn_dt_bias, m_dn_out_norm_g, m_sg_norm_g, m_sg_w, m_sg_b, m_w_out, m_ffn_norm_g, m_w_up, m_ffn_conv_w, m_ffn_conv_b, m_w_down, m_final_norm_g, v_attn_norm_g, v_w_in, v_dn_conv_w, v_dn_a_log, v_dn_dt_bias, v_dn_out_norm_g, v_sg_norm_g, v_sg_w, v_sg_b, v_w_out, v_ffn_norm_g, v_w_up, v_ffn_conv_w, v_ffn_conv_b, v_w_down, v_final_norm_g):
    W = dict(attn_norm_g=attn_norm_g, w_in=w_in, dn_conv_w=dn_conv_w, dn_a_log=dn_a_log, dn_dt_bias=dn_dt_bias,
             dn_out_norm_g=dn_out_norm_g, sg_norm_g=sg_norm_g, sg_w=sg_w, sg_b=sg_b, w_out=w_out,
             ffn_norm_g=ffn_norm_g, w_up=w_up, ffn_conv_w=ffn_conv_w, ffn_conv_b=ffn_conv_b, w_down=w_down,
             final_norm_g=final_norm_g)
    Mo = dict(attn_norm_g=m_attn_norm_g, w_in=m_w_in, dn_conv_w=m_dn_conv_w, dn_a_log=m_dn_a_log,
              dn_dt_bias=m_dn_dt_bias, dn_out_norm_g=m_dn_out_norm_g, sg_norm_g=m_sg_norm_g, sg_w=m_sg_w,
              sg_b=m_sg_b, w_out=m_w_out, ffn_norm_g=m_ffn_norm_g, w_up=m_w_up, ffn_conv_w=m_ffn_conv_w,
              ffn_conv_b=m_ffn_conv_b, w_down=m_w_down, final_norm_g=m_final_norm_g)
    Vo = dict(attn_norm_g=v_attn_norm_g, w_in=v_w_in, dn_conv_w=v_dn_conv_w, dn_a_log=v_dn_a_log,
              dn_dt_bias=v_dn_dt_bias, dn_out_norm_g=v_dn_out_norm_g, sg_norm_g=v_sg_norm_g, sg_w=v_sg_w,
              sg_b=v_sg_b, w_out=v_w_out, ffn_norm_g=v_ffn_norm_g, w_up=v_w_up, ffn_conv_w=v_ffn_conv_w,
              ffn_conv_b=v_ffn_conv_b, w_down=v_w_down, final_norm_g=v_final_norm_g)
    xi, yi, ci = lax.axis_index("x"), lax.axis_index("y"), lax.axis_index("c")
    chip = 2 * xi + yi

    me_lin = 4 * xi + 2 * yi + ci

    g_in, g_dnc = _gather_first(w_in[0].astype(BF16), dn_conv_w[0])
    def start_gather(name, shards, after):
        lands = [lax.dynamic_update_index_in_dim(lax.empty((4,) + s.shape, s.dtype), s, chip, 0) for s in shards]
        return _transfer_start(name, shards, lands, 3 * len(shards), _gather_copies, after=after)

    mid = start_gather("gather_mid_start", [w_out[0].astype(BF16), w_up[0].astype(BF16), ffn_conv_w[0]], g_in)
    last = start_gather("gather_last_start", [w_down[0].astype(BF16)], mid[4])
    token = last[4]

    def late_weights(stage, after):
        if stage == "out_proj":
            _, (g_out, g_up, g_ffc) = _transfer_wait("gather_mid_wait", *mid[:4], _gather_copies, after)
            return dict(w_out=g_out.reshape(D_MODEL, D_MODEL), ffn_conv_w=g_ffc.transpose(1, 0, 2).reshape(3, 2 * D_FF),
                        w_up=g_up.transpose(1, 0, 2).reshape(D_MODEL, 2 * D_FF))
        _, (g_down,) = _transfer_wait("gather_last_wait", *last[:4], _gather_copies, after)
        return dict(w_down=g_down.reshape(D_FF, D_MODEL))

    full = dict(
        w_in=g_in,
        dn_conv_w=g_dnc.transpose(1, 0, 2).reshape(4, 3 * DN_WIDTH),
        attn_norm_g=attn_norm_g, dn_a_log=dn_a_log, dn_dt_bias=dn_dt_bias, dn_out_norm_g=dn_out_norm_g,
        sg_norm_g=sg_norm_g, sg_w=sg_w[0], sg_b=sg_b[0], ffn_norm_g=ffn_norm_g, ffn_conv_b=ffn_conv_b,
        final_norm_g=final_norm_g[None])

    pending = {}

    def on_grad(name, gw):
        if name == "small_early":
            buf = _pack_early(gw["dn_out_norm_g"], gw["sg_norm_g"], gw["sg_w"], gw["sg_bt"], gw["ffn_norm_g"],
                              gw["ffn_conv_w"], gw["ffn_conv_b"], gw["final_norm_g"])
            land = lax.dynamic_update_index_in_dim(lax.empty((8,) + buf.shape, F32), buf, me_lin, 0)
            s_sem, r_sem, src, lands, tok = _transfer_start("small_early_start", [buf], [land], 7, _small_copies)
            pending[name] = (s_sem, r_sem, src, lands)
            return tok
        g8 = gw.reshape(8, -1, gw.shape[-1])
        land = lax.empty((7,) + g8.shape[1:], BF16)
        s_sem, r_sem, src, lands, tok = _transfer_start(f"reduce_{name}_start", [g8], [land], 7, _pieces_copies)
        pending[name] = (s_sem, r_sem, src, lands)
        return tok

    loss_row, grad_x, g = _local_step(x[0], loss_target[0], full, dep=token, late_weights=late_weights,
                                      on_grad=on_grad)

    late_all = _exchange_small(_pack_late(g["attn_norm_g"], g["dn_conv_w"], g["a_dt"], loss_row))
    s_sem, r_sem, src, lands = pending["small_early"]
    _, (early_all,) = _transfer_wait("small_early_wait", s_sem, r_sem, src, lands, _small_copies, grad_x)
    row = lambda d: {k: (d[k].reshape(1, -1) if k == "final_norm_g" else d[k]) for k in SMALL}
    loss_sum, *small_out = _small_update(early_all, late_all, chip.astype(jnp.int32).reshape(1), row(W), row(Mo), row(Vo))
    loss = loss_sum[0, 0]

    halves = []
    for n in ("w_down", "w_up", "w_out", "w_in"):
        s_sem, r_sem, src, lands = pending[n]
        sent, got = _transfer_wait(f"reduce_{n}_wait", s_sem, r_sem, src, lands, _pieces_copies, grad_x)
        own = lax.dynamic_index_in_dim(sent[0], me_lin, axis=0, keepdims=False)
        halves.append(_sum_slots(f"sum_{n}", own, got[0]))
    theirs = _pair_swap(halves)
    core = ci.astype(jnp.int32).reshape(1)
    grads, delta, new_m, new_v = {}, {}, {}, {}
    for n, mine_h, their_h in zip(("w_down", "w_up", "w_out", "w_in"), halves, theirs):
        shp = W[n].shape
        if n == "w_in":
            to_t = lambda a: a.reshape(shp[1] // LANES, LANES, shp[2]).transpose(2, 0, 1)
            from_t = lambda a: a.transpose(1, 2, 0).reshape(shp)
            outs = _adamw_transposed(f"adamw_{n}", to_t(W[n]), mine_h, their_h, to_t(Mo[n]), to_t(Vo[n]), core)
            grads[n], delta[n], new_m[n], new_v[n] = (from_t(o) for o in outs)
            continue
        gr, d, mn, vn = _adamw_halves(f"adamw_{n}", W[n][0], mine_h, their_h, Mo[n][0], Vo[n][0], core)
        grads[n], delta[n], new_m[n], new_v[n] = gr.reshape(shp), d.reshape(shp), mn.reshape(shp), vn.reshape(shp)
    for dst, src_d in zip((grads, delta, new_m, new_v), small_out):
        dst.update({k: (a.reshape(W[k].shape) if k == "final_norm_g" else a) for k, a in src_d.items()})

    return (loss, grad_x[None], *[grads[n] for n in ORDER], *[delta[n] for n in ORDER],
            *[new_m[n] for n in ORDER], *[new_v[n] for n in ORDER])
```

```python
import functools
import math

import jax
import jax.numpy as jnp
from jax import lax
from jax.experimental import pallas as pl
from jax.experimental.pallas import tpu as pltpu

F32 = jnp.float32
BF16 = jnp.bfloat16

D_MODEL = 1024
CHUNK = 64
SCAN_CHUNKS = 4
SCAN_CHUNKS_FWD = 8
HEAD_DIM = 128
N_HEADS = 4
DN_WIDTH = 512
SG_WIDTH = 512
SG_GROUPS = 4
SG_BLOCK = 128
D_FF = 2816
PROJ_COLS = 3080
PROJ_PAD = 3200
BA_COL = 3072
EPS = 1e-6
NEG = -1e30
VMEM_LIMIT = 56 * 1024 * 1024

ADAM_LR = 0.001
ADAM_B1 = 0.9
ADAM_B2 = 0.999
ADAM_EPS = 1e-08
ADAM_WD = 0.01
ADAM_STEP = 10

MESH = pl.DeviceIdType.MESH
ANY = pl.BlockSpec(memory_space=pl.ANY)


def _cp(*sem):
    return pltpu.CompilerParams(dimension_semantics=sem, vmem_limit_bytes=VMEM_LIMIT)


def _bf(a):
    return a.astype(BF16)


def _nn(a, b):
    return jnp.dot(_bf(a), _bf(b), preferred_element_type=F32)


def _nt(a, b):
    return lax.dot_general(_bf(a), _bf(b), (((1,), (1,)), ((), ())), preferred_element_type=F32)


def _tn(a, b):
    return lax.dot_general(_bf(a), _bf(b), (((0,), (0,)), ((), ())), preferred_element_type=F32)


def _split(a):
    hi = _bf(a)
    return hi, _bf(a - hi.astype(F32))


def _sigmoid(x):
    return 0.5 * jnp.tanh(0.5 * x) + 0.5


def _silu(x):
    return x * _sigmoid(x)


def _dsilu(x):
    s = _sigmoid(x)
    return s * (1.0 + x * (1.0 - s))


_GELU_C = math.sqrt(2.0 / math.pi)
_GELU_A = 0.044715


def _gelu(x):
    return 0.5 * x * (1.0 + jnp.tanh(_GELU_C * (x + _GELU_A * x * x * x)))


def _dgelu(x):
    t = jnp.tanh(_GELU_C * (x + _GELU_A * x * x * x))
    return 0.5 * (1.0 + t) + 0.5 * x * (1.0 - t * t) * _GELU_C * (1.0 + 3.0 * _GELU_A * x * x)


def _softplus(x):
    return jnp.maximum(x, 0.0) + jnp.log(1.0 + jnp.exp(-jnp.abs(x)))


def _with_dep(in_specs, args, dep):
    if dep is None:
        return in_specs, args
    return in_specs + [ANY], args + [dep]


SUB_ROWS = 128


def _sub_blocks(tm):
    return [slice(r0, min(r0 + SUB_ROWS, tm)) for r0 in range(0, tm, SUB_ROWS)]


def _rms_hat(xv):
    r = lax.rsqrt(jnp.mean(xv * xv, axis=-1, keepdims=True) + EPS)
    return xv * r, r


def _rms_bwd_vals(dh, xh, r, g):
    dxh = dh * g
    return r * (dxh - xh * jnp.mean(dxh * xh, axis=-1, keepdims=True)), jnp.sum(dh * xh, axis=0, keepdims=True)


def _in_proj(x, g, w4, tm=512, dep=None):
    T, K = x.shape
    ng, _, wc = w4.shape
    tm = min(tm, T)

    def body(x_ref, g_ref, w4_ref, *rest):
        p_ref, h_ref, w_ref = rest[-3:]

        @pl.when(pl.program_id(0) == 0)
        def _():
            w_ref[:, ng * wc:] = jnp.zeros((K, PROJ_PAD - ng * wc), BF16)
            for j in range(ng):
                w_ref[:, j * wc:(j + 1) * wc] = w4_ref[j]
        for r in _sub_blocks(tm):
            xh, _ = _rms_hat(x_ref[r, :])
            h_ref[r, :] = (xh * g_ref[...]).astype(BF16)
        p_ref[...] = jnp.dot(h_ref[...], w_ref[...], preferred_element_type=F32)

    in_specs, args = _with_dep(
        [pl.BlockSpec((tm, K), lambda i: (i, 0)), pl.BlockSpec((1, K), lambda i: (0, 0)),
         pl.BlockSpec((ng, K, wc), lambda i: (0, 0, 0))], [x, g, w4], dep)
    return pl.pallas_call(
        body, name="in_proj", grid=(T // tm,), in_specs=in_specs,
        out_specs=[pl.BlockSpec((tm, PROJ_PAD), lambda i: (i, 0)), pl.BlockSpec((tm, K), lambda i: (i, 0)),
                   pl.BlockSpec((K, PROJ_PAD), lambda i: (0, 0))],
        out_shape=[jax.ShapeDtypeStruct((T, PROJ_PAD), F32), jax.ShapeDtypeStruct((T, K), BF16),
                   jax.ShapeDtypeStruct((K, PROJ_PAD), BF16)],
        compiler_params=_cp("arbitrary"))(*args)


def _out_proj(mix, w, x, g, tm=512):
    T, K = mix.shape
    Dm = w.shape[1]
    tm = min(tm, T)

    def body(a_ref, w_ref, x_ref, g_ref, x2_ref, h_ref):
        x2_ref[...] = _nn(a_ref[...], w_ref[...]) + x_ref[...]
        for r in _sub_blocks(tm):
            xh, _ = _rms_hat(x2_ref[r, :])
            h_ref[r, :] = (xh * g_ref[...]).astype(BF16)

    row = lambda width: pl.BlockSpec((tm, width), lambda i: (i, 0))
    return pl.pallas_call(
        body, name="out_proj", grid=(T // tm,),
        in_specs=[row(K), pl.BlockSpec((K, Dm), lambda i: (0, 0)), row(Dm), pl.BlockSpec((1, Dm), lambda i: (0, 0))],
        out_specs=[row(Dm), row(Dm)],
        out_shape=[jax.ShapeDtypeStruct((T, Dm), F32), jax.ShapeDtypeStruct((T, Dm), BF16)],
        compiler_params=_cp("parallel"))(mix, w, x, g)


def _down_proj_loss(act, w, x2, tgt, g, tm=512):
    T, K = act.shape
    Dm = w.shape[1]
    tm = min(tm, T)

    def body(a_ref, w_ref, x_ref, t_ref, g_ref, loss_ref, dx_ref, gg_ref):
        @pl.when(pl.program_id(0) == 0)
        def _():
            gg_ref[...] = jnp.zeros_like(gg_ref)
            loss_ref[...] = jnp.zeros_like(loss_ref)
        dx_ref[...] = _nn(a_ref[...], w_ref[...]) + x_ref[...]
        for r in _sub_blocks(tm):
            xh, rr = _rms_hat(dx_ref[r, :])
            e = xh * g_ref[...] - t_ref[r, :]
            loss_ref[...] += jnp.zeros_like(loss_ref) + (0.5 / Dm) * jnp.sum(e * e)
            dx, gg = _rms_bwd_vals(e * (1.0 / Dm), xh, rr, g_ref[...])
            dx_ref[r, :] = dx
            gg_ref[...] += gg

    row = lambda width: pl.BlockSpec((tm, width), lambda i: (i, 0))
    vec = pl.BlockSpec((1, Dm), lambda i: (0, 0))
    return pl.pallas_call(
        body, name="down_proj_loss", grid=(T // tm,),
        in_specs=[row(K), pl.BlockSpec((K, Dm), lambda i: (0, 0)), row(Dm), row(Dm), vec],
        out_specs=[pl.BlockSpec((1, 128), lambda i: (0, 0)), row(Dm), vec],
        out_shape=[jax.ShapeDtypeStruct((1, 128), F32), jax.ShapeDtypeStruct((T, Dm), F32),
                   jax.ShapeDtypeStruct((1, Dm), F32)],
        compiler_params=_cp("arbitrary"))(act, w, x2, tgt, g)


def _mm_nt_rms_bwd(name, a, b, x, g, dres, tm=512, dep=None):
    M, K = a.shape
    Dm = b.shape[0]
    tm = min(tm, M)

    def body(a_ref, b_ref, x_ref, g_ref, dres_ref, *rest):
        dx_ref, gg_ref = rest[-2:]

        @pl.when(pl.program_id(0) == 0)
        def _():
            gg_ref[...] = jnp.zeros_like(gg_ref)
        dx_ref[...] = _nt(a_ref[...], b_ref[...])
        for r in _sub_blocks(tm):
            xh, rr = _rms_hat(x_ref[r, :])
            dx, gg = _rms_bwd_vals(dx_ref[r, :], xh, rr, g_ref[...])
            dx_ref[r, :] = dres_ref[r, :] + dx
            gg_ref[...] += gg

    row = lambda width: pl.BlockSpec((tm, width), lambda i: (i, 0))
    vec = pl.BlockSpec((1, Dm), lambda i: (0, 0))
    in_specs, args = _with_dep([row(K), pl.BlockSpec((Dm, K), lambda i: (0, 0)), row(Dm), vec, row(Dm)],
                               [a, b, x, g, dres], dep)
    return pl.pallas_call(
        body, name=name, grid=(M // tm,), in_specs=in_specs, out_specs=[row(Dm), vec],
        out_shape=[jax.ShapeDtypeStruct((M, Dm), F32), jax.ShapeDtypeStruct((1, Dm), F32)],
        compiler_params=_cp("arbitrary"))(*args)


def _mm_nt(name, a, b, out_dtype, tm, tn, dep=None):
    M, K = a.shape
    N = b.shape[0]
    tm, tn = min(tm, M), min(tn, N)

    def body(a_ref, b_ref, *rest):
        o_ref = rest[-1]
        o_ref[...] = _nt(a_ref[...], b_ref[...]).astype(o_ref.dtype)

    in_specs, args = _with_dep(
        [pl.BlockSpec((tm, K), lambda i, j: (i, 0)), pl.BlockSpec((tn, K), lambda i, j: (j, 0))], [a, b], dep)
    return pl.pallas_call(
        body, name=name, grid=(M // tm, N // tn), in_specs=in_specs,
        out_specs=pl.BlockSpec((tm, tn), lambda i, j: (i, j)),
        out_shape=jax.ShapeDtypeStruct((M, N), out_dtype),
        compiler_params=_cp("parallel", "parallel"))(*args)


def _mm_tn(name, a, b, tm, tn, tk, col_major_tiles=False, col_groups=None):
    T, M = a.shape
    N = b.shape[1]
    tm, tn, tk = min(tm, M), min(tn, N), min(tk, T)
    nk = T // tk

    def body(a_ref, b_ref, o_ref, acc_ref):
        k = pl.program_id(2)

        @pl.when(k == 0)
        def _():
            acc_ref[...] = jnp.zeros_like(acc_ref)
        acc_ref[...] += _tn(a_ref[...], b_ref[...])

        @pl.when(k == nk - 1)
        def _():
            if col_groups:
                for j in range(col_groups[0]):
                    o_ref[j] = acc_ref[:, j * col_groups[1]:(j + 1) * col_groups[1]].astype(BF16)
            else:
                o_ref[...] = acc_ref[...].astype(BF16).reshape(o_ref.shape)

    if col_groups:
        assert tm == M and tn == N and col_groups[0] * col_groups[1] <= N
        out_spec = pl.BlockSpec((col_groups[0], M, col_groups[1]), lambda i, j, k: (0, 0, 0))
        out_shape = jax.ShapeDtypeStruct((col_groups[0], M, col_groups[1]), BF16)
    elif col_major_tiles:
        assert tm == M
        out_spec = pl.BlockSpec((1, tm, tn), lambda i, j, k: (j, 0, 0))
        out_shape = jax.ShapeDtypeStruct((N // tn, M, tn), BF16)
    else:
        out_spec = pl.BlockSpec((tm, tn), lambda i, j, k: (i, j))
        out_shape = jax.ShapeDtypeStruct((M, N), BF16)
    return pl.pallas_call(
        body, name=name, grid=(M // tm, N // tn, nk),
        in_specs=[pl.BlockSpec((tk, tm), lambda i, j, k: (k, i)), pl.BlockSpec((tk, tn), lambda i, j, k: (k, j))],
        out_specs=out_spec, out_shape=out_shape, scratch_shapes=[pltpu.VMEM((tm, tn), F32)],
        compiler_params=_cp("parallel", "parallel", "arbitrary"))(a, b)


def _halo_prev_spec(rb, width):
    return pl.BlockSpec((8, width), lambda i: (jnp.maximum(i * (rb // 8) - 1, 0), 0))


def _halo_next_spec(rb, width, T):
    return pl.BlockSpec((8, width), lambda i: (jnp.minimum((i + 1) * (rb // 8), T // 8 - 1), 0))


LANES = 128
FF_STRIPS = D_FF // LANES
ROW_CHUNK = 32


def _strip(j, base=0):
    return pl.ds(pl.multiple_of(base + j * LANES, LANES), LANES)


def _up_proj_act(h, w_up, w, b, rb=256):
    T, K = h.shape
    W = w_up.shape[1]
    rb = min(rb, T)

    def body(h_ref, wup_ref, w_ref, b_ref, up_ref, act_ref, ext_scr, tail_scr):
        @pl.when(pl.program_id(0) == 0)
        def _():
            tail_scr[...] = jnp.zeros_like(tail_scr)
        up_ref[...] = jnp.dot(h_ref[...], wup_ref[...], preferred_element_type=F32)

        def strip(j, slot):
            halves = (_strip(j), _strip(j, D_FF))
            wv = [w_ref[:, cols] for cols in halves]
            bv = [b_ref[:, cols] for cols in halves]
            for h, cols in enumerate(halves):
                ext_scr[slot, h,0:8] = tail_scr[:, cols]
                ext_scr[slot, h,8:] = up_ref[:, cols]
            for r0 in range(0, rb, ROW_CHUNK):
                n = min(ROW_CHUNK, rb - r0)
                c = [ext_scr[slot, h,6 + r0:6 + r0 + n] * wv[h][0:1] + ext_scr[slot, h,7 + r0:7 + r0 + n] * wv[h][1:2]
                     + ext_scr[slot, h,8 + r0:8 + r0 + n] * wv[h][2:3] + bv[h] for h in range(2)]
                act_ref[r0:r0 + n, halves[0]] = (_silu(c[0]) * c[1]).astype(BF16)

        def pair(jj, carry):
            strip(2 * jj, 0)
            strip(2 * jj + 1, 1)
            return carry

        lax.fori_loop(0, FF_STRIPS // 2, pair, 0)
        tail_scr[...] = up_ref[rb - 8:rb, :]

    return pl.pallas_call(
        body, name="up_proj_act", grid=(T // rb,),
        in_specs=[pl.BlockSpec((rb, K), lambda i: (i, 0)), pl.BlockSpec((K, W), lambda i: (0, 0)),
                  pl.BlockSpec((3, W), lambda i: (0, 0)), pl.BlockSpec((1, W), lambda i: (0, 0))],
        out_specs=[pl.BlockSpec((rb, W), lambda i: (i, 0)), pl.BlockSpec((rb, D_FF), lambda i: (i, 0))],
        out_shape=[jax.ShapeDtypeStruct((T, W), F32), jax.ShapeDtypeStruct((T, D_FF), BF16)],
        scratch_shapes=[pltpu.VMEM((2, 2, rb + 8, LANES), F32), pltpu.VMEM((8, W), F32)],
        compiler_params=_cp("arbitrary"))(h, w_up, w, b)


def _ffn_act_bwd(up, dact, w, b, rb=128, dep=None):
    T, W = up.shape
    rb = min(rb, T)
    nb = T // rb
    re = rb + 8

    def body(up_ref, prev_ref, next_ref, da_ref, danext_ref, w_ref, b_ref, *rest):
        dup_ref, gw_ref, gb_ref, ext_scr, dc_scr = rest[-5:]
        i = pl.program_id(0)

        @pl.when(i == 0)
        def _():
            gw_ref[...] = jnp.zeros_like(gw_ref)
            gb_ref[...] = jnp.zeros_like(gb_ref)
        last = i == nb - 1

        def fold8(a):
            return jnp.sum(a.reshape(a.shape[0] // 8, 8, LANES), axis=0)

        def strip(j, slot):
            halves = (_strip(j), _strip(j, D_FF))
            wv = [w_ref[:, cols] for cols in halves]
            bv = [b_ref[:, cols] for cols in halves]
            for h, cols in enumerate(halves):
                ext_scr[slot, h,0:8] = jnp.where(i > 0, prev_ref[:, cols], 0.0)
                ext_scr[slot, h,8:8 + rb] = up_ref[:, cols]
                ext_scr[slot, h,8 + rb:] = next_ref[:, cols]
            gb = [jnp.zeros((8, LANES), F32) for _ in range(2)]
            gw = [[jnp.zeros((8, LANES), F32) for _ in range(3)] for _ in range(2)]
            for r0 in range(0, re, ROW_CHUNK):
                n = min(ROW_CHUNK, re - r0)
                tp = [[ext_scr[slot, h,6 + k + r0:6 + k + r0 + n] for k in range(3)] for h in range(2)]
                c = [tp[h][0] * wv[h][0:1] + tp[h][1] * wv[h][1:2] + tp[h][2] * wv[h][2:3] + bv[h] for h in range(2)]
                if r0 < rb:
                    da = da_ref[r0:r0 + n, halves[0]]
                else:
                    da = jnp.where(last, 0.0, danext_ref[:, halves[0]])
                s = _sigmoid(c[0])
                gs = c[0] * s
                dcs = (da * c[1] * (s + gs * (1.0 - s)), da * gs)
                for h in range(2):
                    dc_scr[slot, h,r0:r0 + n] = dcs[h]
                    if r0 < rb:
                        gb[h] = gb[h] + fold8(dcs[h])
                        for k in range(3):
                            gw[h][k] = gw[h][k] + fold8(tp[h][k] * dcs[h])
            for r0 in range(0, rb, ROW_CHUNK):
                n = min(ROW_CHUNK, rb - r0)
                for h, cols in enumerate(halves):
                    dup = (dc_scr[slot, h,r0:r0 + n] * wv[h][2:3] + dc_scr[slot, h,r0 + 1:r0 + 1 + n] * wv[h][1:2]
                           + dc_scr[slot, h,r0 + 2:r0 + 2 + n] * wv[h][0:1])
                    dup_ref[r0:r0 + n, cols] = dup.astype(BF16)
            for h, cols in enumerate(halves):
                gb_ref[:, cols] += jnp.sum(gb[h], axis=0, keepdims=True)
                for k in range(3):
                    gw_ref[k:k + 1, cols] += jnp.sum(gw[h][k], axis=0, keepdims=True)

        def pair(jj, carry):
            strip(2 * jj, 0)
            strip(2 * jj + 1, 1)
            return carry

        lax.fori_loop(0, FF_STRIPS // 2, pair, 0)

    in_specs, args = _with_dep(
        [pl.BlockSpec((rb, W), lambda i: (i, 0)), _halo_prev_spec(rb, W), _halo_next_spec(rb, W, T),
         pl.BlockSpec((rb, D_FF), lambda i: (i, 0)), _halo_next_spec(rb, D_FF, T),
         pl.BlockSpec((3, W), lambda i: (0, 0)), pl.BlockSpec((1, W), lambda i: (0, 0))],
        [up, up, up, dact, dact, w, b], dep)
    return pl.pallas_call(
        body, name="ffn_act_bwd", grid=(nb,), in_specs=in_specs,
        out_specs=[pl.BlockSpec((rb, W), lambda i: (i, 0)), pl.BlockSpec((3, W), lambda i: (0, 0)),
                   pl.BlockSpec((1, W), lambda i: (0, 0))],
        out_shape=[jax.ShapeDtypeStruct((T, W), BF16), jax.ShapeDtypeStruct((3, W), F32),
                   jax.ShapeDtypeStruct((1, W), F32)],
        scratch_shapes=[pltpu.VMEM((2, 2, rb + 16, LANES), F32), pltpu.VMEM((2, 2, re, LANES), F32)],
        compiler_params=_cp("arbitrary"))(*args)


def _lane_iota(shape):
    return lax.broadcasted_iota(jnp.int32, shape, len(shape) - 1)


def _dn_act(p, conv_w, alog_row, dtb_row, rb=256):
    T = p.shape[0]
    rb = min(rb, T)
    W3 = 3 * DN_WIDTH

    def body(p_ref, halo_ref, ba_ref, w_ref, al_ref, dt_ref, q_ref, k_ref, v_ref, bg_ref, ext_scr):
        first = pl.program_id(0) == 0
        outs = (q_ref, k_ref, v_ref)
        for j in range(3 * N_HEADS):
            kind, h = divmod(j, N_HEADS)
            cols = slice(j * HEAD_DIM, (j + 1) * HEAD_DIM)
            cur = p_ref[:, cols]
            ext_scr[j, 0:8] = jnp.where(first, 0.0, halo_ref[:, cols])
            ext_scr[j, 8:] = cur
            wv = w_ref[:, cols]
            s = _silu(ext_scr[j, 5:5 + rb] * wv[0:1] + ext_scr[j, 6:6 + rb] * wv[1:2]
                      + ext_scr[j, 7:7 + rb] * wv[2:3] + cur * wv[3:4])
            if kind < 2:
                scale = HEAD_DIM ** -0.5 if kind == 0 else 1.0
                s = s * (lax.rsqrt(jnp.sum(s * s, axis=-1, keepdims=True) + EPS) * scale)
            outs[kind][:, h * HEAD_DIM:(h + 1) * HEAD_DIM] = s
        ba = ba_ref[...]
        lane = _lane_iota(ba.shape)
        beta = _sigmoid(ba)
        g = -jnp.exp(al_ref[...]) * _softplus(ba + dt_ref[...])
        bg_ref[...] = jnp.where(lane < N_HEADS, beta, jnp.where(lane < 2 * N_HEADS, g, 0.0))

    row512 = pl.BlockSpec((rb, DN_WIDTH), lambda i: (i, 0))
    row128 = pl.BlockSpec((rb, 128), lambda i: (i, 0))
    vec128 = pl.BlockSpec((1, 128), lambda i: (0, 0))
    return pl.pallas_call(
        body, name="dn_act", grid=(T // rb,),
        in_specs=[pl.BlockSpec((rb, W3), lambda i: (i, 0)), _halo_prev_spec(rb, W3),
                  pl.BlockSpec((rb, 128), lambda i: (i, BA_COL // 128)),
                  pl.BlockSpec((4, W3), lambda i: (0, 0)), vec128, vec128],
        out_specs=[row512, row512, row512, row128],
        out_shape=[jax.ShapeDtypeStruct((T, DN_WIDTH), F32)] * 3 + [jax.ShapeDtypeStruct((T, 128), F32)],
        scratch_shapes=[pltpu.VMEM((3 * N_HEADS, rb + 8, HEAD_DIM), F32)],
        compiler_params=_cp("parallel"))(p, p, p, conv_w, alog_row, dtb_row)


def _dn_act_bwd(p, conv_w, alog_row, dtb_row, dq, dk, dv, dbg, dp_mid, rb=256):
    T = p.shape[0]
    rb = min(rb, T)
    nb = T // rb
    re = rb + 8
    W3 = 3 * DN_WIDTH

    def body(p_ref, prev_ref, next_ref, ba_ref, w_ref, al_ref, dt_ref, dq_ref, dqn_ref, dk_ref, dkn_ref,
             dv_ref, dvn_ref, dbg_ref, mid_ref, draw_ref, gw_ref, gad_ref, ext_scr, dc_scr):
        i = pl.program_id(0)
        draw_ref[:, W3:2 * W3] = mid_ref[...]

        @pl.when(i == 0)
        def _():
            gw_ref[...] = jnp.zeros_like(gw_ref)
            gad_ref[...] = jnp.zeros_like(gad_ref)
        row = lax.broadcasted_iota(jnp.int32, (re, 1), 0)
        live = (row < rb) | (i < nb - 1)
        d_refs = ((dq_ref, dqn_ref), (dk_ref, dkn_ref), (dv_ref, dvn_ref))
        for j in range(3 * N_HEADS):
            kind, h = divmod(j, N_HEADS)
            cols = slice(j * HEAD_DIM, (j + 1) * HEAD_DIM)
            hcols = slice(h * HEAD_DIM, (h + 1) * HEAD_DIM)
            ext_scr[j, 0:8] = jnp.where(i > 0, prev_ref[:, cols], 0.0)
            ext_scr[j, 8:8 + rb] = p_ref[:, cols]
            ext_scr[j, 8 + rb:] = next_ref[:, cols]
            tp = [ext_scr[j, 5 + k:5 + k + re] for k in range(4)]
            wv = w_ref[:, cols]
            c = tp[0] * wv[0:1] + tp[1] * wv[1:2] + tp[2] * wv[2:3] + tp[3] * wv[3:4]
            sg = _sigmoid(c)
            s = c * sg
            d_in = jnp.where(live, jnp.concatenate([d_refs[kind][0][:, hcols], d_refs[kind][1][:, hcols]], axis=0), 0.0)
            if kind < 2:
                scale = HEAD_DIM ** -0.5 if kind == 0 else 1.0
                n = lax.rsqrt(jnp.sum(s * s, axis=-1, keepdims=True) + EPS)
                hat = s * n
                d_in = (n * scale) * (d_in - hat * jnp.sum(hat * d_in, axis=-1, keepdims=True))
            dc = d_in * (sg + s * (1.0 - sg))
            dc_scr[j] = dc
            dcc = dc[0:rb]
            draw = (dcc * wv[3:4] + dc_scr[j, 1:1 + rb] * wv[2:3] + dc_scr[j, 2:2 + rb] * wv[1:2]
                    + dc_scr[j, 3:3 + rb] * wv[0:1])
            draw_ref[:, cols] = draw.astype(BF16)
            for k in range(4):
                gw_ref[k:k + 1, cols] += jnp.sum(tp[k][0:rb] * dcc, axis=0, keepdims=True)
        ba = ba_ref[...]
        dbg = dbg_ref[...]
        lane = _lane_iota(ba.shape)
        beta = _sigmoid(ba)
        ea = jnp.exp(al_ref[...])
        z = ba + dt_ref[...]
        d_a = dbg * (-ea) * _sigmoid(z)
        dba = jnp.where(lane < N_HEADS, dbg * beta * (1.0 - beta), jnp.where(lane < 2 * N_HEADS, d_a, 0.0))
        draw_ref[:, BA_COL:] = dba.astype(BF16)
        isg = (lane >= N_HEADS) & (lane < 2 * N_HEADS)
        g = -ea * _softplus(z)
        gad_ref[0:1, :] += jnp.sum(jnp.where(isg, dbg * g, 0.0), axis=0, keepdims=True)
        gad_ref[1:2, :] += jnp.sum(jnp.where(isg, d_a, 0.0), axis=0, keepdims=True)

    row512 = pl.BlockSpec((rb, DN_WIDTH), lambda i: (i, 0))
    row128 = pl.BlockSpec((rb, 128), lambda i: (i, 0))
    vec128 = pl.BlockSpec((1, 128), lambda i: (0, 0))
    next512 = _halo_next_spec(rb, DN_WIDTH, T)
    return pl.pallas_call(
        body, name="dn_act_bwd", grid=(nb,),
        in_specs=[pl.BlockSpec((rb, W3), lambda i: (i, 0)), _halo_prev_spec(rb, W3), _halo_next_spec(rb, W3, T),
                  pl.BlockSpec((rb, 128), lambda i: (i, BA_COL // 128)),
                  pl.BlockSpec((4, W3), lambda i: (0, 0)), vec128, vec128,
                  row512, next512, row512, next512, row512, next512, row128,
                  pl.BlockSpec((rb, W3), lambda i: (i, 0))],
        out_specs=[pl.BlockSpec((rb, PROJ_PAD), lambda i: (i, 0)),
                   pl.BlockSpec((4, W3), lambda i: (0, 0)), pl.BlockSpec((2, 128), lambda i: (0, 0))],
        out_shape=[jax.ShapeDtypeStruct((T, PROJ_PAD), BF16),
                   jax.ShapeDtypeStruct((4, W3), F32), jax.ShapeDtypeStruct((2, 128), F32)],
        scratch_shapes=[pltpu.VMEM((3 * N_HEADS, rb + 16, HEAD_DIM), F32), pltpu.VMEM((3 * N_HEADS, re, HEAD_DIM), F32)],
        compiler_params=_cp("arbitrary"))(p, p, p, p, conv_w, alog_row, dtb_row, dq, dq, dk, dk, dv, dv, dbg, dp_mid)


def _tri(incl):
    ii = lax.broadcasted_iota(jnp.int32, (CHUNK, CHUNK), 0)
    jj = lax.broadcasted_iota(jnp.int32, (CHUNK, CHUNK), 1)
    return ii, jj, ((ii >= jj) if incl else (ii > jj))


def _dn_chunk(k, bg, cb=4):
    T = k.shape[0]
    N = T // CHUNK
    cb = min(cb, N)

    def body(k_ref, bg_ref, gc_ref, gct_ref, l_ref):
        ii, jj, incl = _tri(True)
        tri = incl.astype(F32)
        U = range(cb)
        bgv = [bg_ref[u * CHUNK:(u + 1) * CHUNK, :] for u in U]
        gc = [jnp.dot(tri, bgv[u], precision=lax.Precision.HIGHEST, preferred_element_type=F32) for u in U]
        gct = [gc[u].T for u in U]
        kk = [[None] * N_HEADS for _ in U]
        for u in U:
            gc_ref[u * CHUNK:(u + 1) * CHUNK, :] = gc[u]
            gct_ref[u] = gct[u][0:8]
            for h in range(N_HEADS):
                kh = k_ref[u * CHUNK:(u + 1) * CHUNK, h * HEAD_DIM:(h + 1) * HEAD_DIM]
                kk[u][h] = _nt(kh * bgv[u][:, h:h + 1], kh)
        for u in U:
            for h in range(N_HEADS):
                gcol = gc[u][:, N_HEADS + h:N_HEADS + h + 1]
                grow = gct[u][N_HEADS + h:N_HEADS + h + 1, :]
                l_ref[u, h] = kk[u][h] * jnp.exp(jnp.where(ii > jj, gcol - grow, NEG))

    rows = cb * CHUNK
    return pl.pallas_call(
        body, name="dn_chunk", grid=(N // cb,),
        in_specs=[pl.BlockSpec((rows, DN_WIDTH), lambda n: (n, 0)), pl.BlockSpec((rows, 128), lambda n: (n, 0))],
        out_specs=[pl.BlockSpec((rows, 128), lambda n: (n, 0)), pl.BlockSpec((cb, 8, CHUNK), lambda n: (n, 0, 0)),
                   pl.BlockSpec((cb, N_HEADS, CHUNK, CHUNK), lambda n: (n, 0, 0, 0))],
        out_shape=[jax.ShapeDtypeStruct((T, 128), F32), jax.ShapeDtypeStruct((N, 8, CHUNK), F32),
                   jax.ShapeDtypeStruct((N, N_HEADS, CHUNK, CHUNK), F32)],
        compiler_params=_cp("parallel"))(k, bg)


def _tri_inv(lt):
    S = lt.shape[1]

    def body(l_ref, a_ref):
        sub = lax.broadcasted_iota(jnp.int32, (8, S), 0)
        groups = CHUNK // 8
        for i in range(CHUNK):
            acc = [((sub + 8 * k) == i).astype(F32) for k in range(groups)]
            for jb in range((i + 7) // 8):
                nk = jb + 1

                def step(j, carry, nk=nk, i=i):
                    lrow = l_ref[pl.ds(i * CHUNK + j, 1), :]
                    return tuple(carry[k] - lrow * a_ref[j, 8 * k:8 * k + 8, :] for k in range(nk))

                acc[:nk] = list(lax.fori_loop(8 * jb, min(8 * jb + 8, i), step, tuple(acc[:nk])))
            for k in range(groups):
                a_ref[i, 8 * k:8 * k + 8, :] = acc[k]

    return pl.pallas_call(
        body, name="tri_inv", out_shape=jax.ShapeDtypeStruct((CHUNK, CHUNK, S), F32),
        compiler_params=pltpu.CompilerParams(vmem_limit_bytes=VMEM_LIMIT))(lt)


def _dn_head_terms(qh, kh, vh, beta, gcol, grow):
    ii, jj, incl = _tri(True)
    gam = jnp.exp(jnp.where(incl, gcol - grow, NEG))
    glast = grow[:, CHUNK - 1:CHUNK]
    cd = jnp.exp(glast)
    shape = (CHUNK, HEAD_DIM)
    E = jnp.broadcast_to(jnp.exp(gcol), shape)
    Fd = jnp.broadcast_to(jnp.exp(glast - gcol), shape)
    beta = jnp.broadcast_to(beta, shape)
    kb = kh * beta
    return dict(ii=ii, jj=jj, gam=gam, E=E, F=Fd, beta=beta, cd=cd, kb=kb, vb=vh * beta, W=kb * E, qE=qh * E,
                kt=kh * Fd)


def _apply_a(a, u):
    hi, lo = _split(a)
    ub = _bf(u)
    return jnp.dot(hi, ub, preferred_element_type=F32) + jnp.dot(lo, ub, preferred_element_type=F32)


def _dn_scan(q, k, v, bg, gc, gct, a):
    T = q.shape[0]
    N = T // CHUNK
    cb = min(SCAN_CHUNKS_FWD, N)

    def body(q_ref, k_ref, v_ref, bg_ref, gc_ref, gct_ref, a_ref, o_ref, sall_ref, s_ref):
        @pl.when(pl.program_id(0) == 0)
        def _():
            s_ref[...] = jnp.zeros_like(s_ref)
        H = range(N_HEADS)
        sl = [slice(h * HEAD_DIM, (h + 1) * HEAD_DIM) for h in H]
        pre = []
        for u in range(cb):
            r = slice(u * CHUNK, (u + 1) * CHUNK)
            bgv, gcv, gctv = bg_ref[r, :], gc_ref[r, :], gct_ref[u]
            q_, k_ = [q_ref[r, s] for s in sl], [k_ref[r, s] for s in sl]
            t = [_dn_head_terms(q_[h], k_[h], v_ref[r, sl[h]], bgv[:, h:h + 1],
                                gcv[:, N_HEADS + h:N_HEADS + h + 1], gctv[N_HEADS + h:N_HEADS + h + 1, :]) for h in H]
            P = [_nt(q_[h], k_[h]) * t[h]["gam"] for h in H]
            pre.append((r, t, P))
        S = [s_ref[h] for h in H]
        for u in range(cb):
            r, t, P = pre[u]
            for h in H:
                sall_ref[u, h] = S[h]
            WS = [_nn(t[h]["W"], S[h]) for h in H]
            qS = [_nn(t[h]["qE"], S[h]) for h in H]
            vn = [_apply_a(a_ref[u, h], t[h]["vb"] - WS[h]) for h in H]
            Pv = [_nn(P[h], vn[h]) for h in H]
            kv = [_tn(t[h]["kt"], vn[h]) for h in H]
            for h in H:
                o_ref[r, sl[h]] = qS[h] + Pv[h]
            S = [t[h]["cd"] * S[h] + kv[h] for h in H]
        for h in H:
            s_ref[h] = S[h]

    row512 = pl.BlockSpec((cb * CHUNK, DN_WIDTH), lambda n: (n, 0))
    row128 = pl.BlockSpec((cb * CHUNK, 128), lambda n: (n, 0))
    return pl.pallas_call(
        body, name="dn_scan", grid=(N // cb,),
        in_specs=[row512, row512, row512, row128, row128, pl.BlockSpec((cb, 8, CHUNK), lambda n: (n, 0, 0)),
                  pl.BlockSpec((cb, N_HEADS, CHUNK, CHUNK), lambda n: (n, 0, 0, 0))],
        out_specs=[row512, pl.BlockSpec((cb, N_HEADS, HEAD_DIM, HEAD_DIM), lambda n: (n, 0, 0, 0))],
        out_shape=[jax.ShapeDtypeStruct((T, DN_WIDTH), F32),
                   jax.ShapeDtypeStruct((N, N_HEADS, HEAD_DIM, HEAD_DIM), F32)],
        scratch_shapes=[pltpu.VMEM((N_HEADS, HEAD_DIM, HEAD_DIM), F32)],
        compiler_params=_cp("arbitrary"))(q, k, v, bg, gc, gct, a)


def _dn_scan_bwd(q, k, v, bg, gc, gct, a, a_t, sall, do, dep=None):
    T = q.shape[0]
    N = T // CHUNK

    cb = min(SCAN_CHUNKS, N)
    nb = N // cb

    def body(q_ref, k_ref, v_ref, bg_ref, gc_ref, gct_ref, a_ref, at_ref, sall_ref, do_ref, *rest):
        dq_ref, dk_ref, dv_ref, dbg_ref, ds_ref = rest[-5:]
        @pl.when(pl.program_id(0) == 0)
        def _():
            ds_ref[...] = jnp.zeros_like(ds_ref)
        lane = _lane_iota((CHUNK, 128))
        rowi = lax.broadcasted_iota(jnp.int32, (CHUNK, 1), 0)
        ii, jj, _ = _tri(True)
        rev = (jj >= ii).astype(F32)
        H = range(N_HEADS)
        sl = [slice(h * HEAD_DIM, (h + 1) * HEAD_DIM) for h in H]
        pre = {}
        for u in reversed(range(cb)):
            r = slice(u * CHUNK, (u + 1) * CHUNK)
            bgv, gcv, gctv = bg_ref[r, :], gc_ref[r, :], gct_ref[u]
            q_, k_, v_ = [q_ref[r, s] for s in sl], [k_ref[r, s] for s in sl], [v_ref[r, s] for s in sl]
            dO = [do_ref[r, s] for s in sl]
            t = [_dn_head_terms(q_[h], k_[h], v_[h], bgv[:, h:h + 1], gcv[:, N_HEADS + h:N_HEADS + h + 1],
                                gctv[N_HEADS + h:N_HEADS + h + 1, :]) for h in H]
            beta = [t[h]["beta"] for h in H]
            S = [sall_ref[u, h] for h in H]
            A = [a_ref[u, h] for h in H]
            WS = [_nn(t[h]["W"], S[h]) for h in H]
            KK = [_nt(t[h]["kb"], k_[h]) for h in H]
            QK = [_nt(q_[h], k_[h]) for h in H]
            d_qE = [_nt(dO[h], S[h]) for h in H]
            vn = [_apply_a(A[h], t[h]["vb"] - WS[h]) for h in H]
            PtdO = [_tn(QK[h] * t[h]["gam"], dO[h]) for h in H]
            qEdO = [_tn(t[h]["qE"], dO[h]) for h in H]
            dOvn = [_nt(dO[h], vn[h]) for h in H]
            dQK = [jnp.where(ii >= jj, dOvn[h], 0.0) * t[h]["gam"] for h in H]
            dQKk = [_nn(dQK[h], k_[h]) for h in H]
            dQKq = [_tn(dQK[h], q_[h]) for h in H]
            pre[u] = (r, q_, k_, v_, beta, t, S, A, KK, QK, d_qE, vn, PtdO, qEdO, dQK, dQKk, dQKq)
        dSn = [ds_ref[h] for h in H]
        for u in reversed(range(cb)):
            r, q_, k_, v_, beta, t, S, A, KK, QK, d_qE, vn, PtdO, qEdO, dQK, dQKk, dQKq = pre[u]
            gam, E, Fd, cd, kb = ([t[h][n] for h in H] for n in ("gam", "E", "F", "cd", "kb"))
            ktdS = [_nn(t[h]["kt"], dSn[h]) for h in H]
            dU = [_apply_a(at_ref[u, h], PtdO[h] + ktdS[h]) for h in H]
            d_kt = [_nt(vn[h], dSn[h]) for h in H]
            dUvn = [_nt(dU[h], vn[h]) for h in H]
            dUS = [_nt(dU[h], S[h]) for h in H]
            WdU = [_tn(t[h]["W"], dU[h]) for h in H]
            d_cd = [jnp.sum(S[h] * dSn[h]) for h in H]
            dSn = [cd[h] * dSn[h] + qEdO[h] - WdU[h] for h in H]
            dKK = [jnp.where(ii > jj, -dUvn[h], 0.0) * gam[h] for h in H]
            dKKk = [_nn(dKK[h], k_[h]) for h in H]
            dKKkb = [_tn(dKK[h], kb[h]) for h in H]
            dbeta_arr = jnp.zeros((CHUNK, 128), F32)
            dgc_arr = jnp.zeros((CHUNK, 128), F32)
            for h in H:
                dW = -dUS[h]
                dq_ref[r, sl[h]] = dQKk[h] + d_qE[h] * E[h]
                d_kb = dKKk[h] + dW * E[h]
                dk_ref[r, sl[h]] = dQKq[h] + dKKkb[h] + d_kb * beta[h] + d_kt[h] * Fd[h]
                dv_ref[r, sl[h]] = dU[h] * beta[h]
                Z = dQK[h] * QK[h] + dKK[h] * KK[h]
                dbeta = jnp.sum(dU[h] * v_[h] + d_kb * k_[h], axis=-1, keepdims=True)
                m_e = (dW * kb[h] + d_qE[h] * q_[h]) * E[h]
                m_f = d_kt[h] * k_[h] * Fd[h]
                zdiag = jnp.where(ii == jj, jnp.sum(Z, axis=0, keepdims=True), 0.0)
                dgc = (jnp.sum(m_e - m_f, axis=-1, keepdims=True) + jnp.sum(Z - zdiag, axis=-1, keepdims=True)
                       + jnp.where(rowi == CHUNK - 1, jnp.sum(m_f) + d_cd[h] * cd[h], 0.0))
                dbeta_arr = dbeta_arr + jnp.where(lane == h, dbeta, 0.0)
                dgc_arr = dgc_arr + jnp.where(lane == N_HEADS + h, dgc, 0.0)
            dbg_ref[r, :] = dbeta_arr + jnp.dot(rev, dgc_arr, precision=lax.Precision.HIGHEST,
                                                preferred_element_type=F32)
        for h in H:
            ds_ref[h] = dSn[h]

    row512 = pl.BlockSpec((cb * CHUNK, DN_WIDTH), lambda n: (nb - 1 - n, 0))
    row128 = pl.BlockSpec((cb * CHUNK, 128), lambda n: (nb - 1 - n, 0))
    in_specs, args = _with_dep(
        [row512, row512, row512, row128, row128,
         pl.BlockSpec((cb, 8, CHUNK), lambda n: (nb - 1 - n, 0, 0)),
         pl.BlockSpec((cb, N_HEADS, CHUNK, CHUNK), lambda n: (nb - 1 - n, 0, 0, 0)),
         pl.BlockSpec((cb, N_HEADS, CHUNK, CHUNK), lambda n: (nb - 1 - n, 0, 0, 0)),
         pl.BlockSpec((cb, N_HEADS, HEAD_DIM, HEAD_DIM), lambda n: (nb - 1 - n, 0, 0, 0)), row512],
        [q, k, v, bg, gc, gct, a, a_t, sall, do], dep)
    return pl.pallas_call(
        body, name="dn_scan_bwd", grid=(nb,), in_specs=in_specs,
        out_specs=[row512, row512, row512, row128],
        out_shape=[jax.ShapeDtypeStruct((T, DN_WIDTH), F32)] * 3 + [jax.ShapeDtypeStruct((T, 128), F32)],
        scratch_shapes=[pltpu.VMEM((N_HEADS, HEAD_DIM, HEAD_DIM), F32)],
        compiler_params=_cp("arbitrary"))(*args)


def _sg_mask():
    ii = lax.broadcasted_iota(jnp.int32, (SG_BLOCK, SG_BLOCK), 0) // CHUNK
    jj = lax.broadcasted_iota(jnp.int32, (SG_BLOCK, SG_BLOCK), 1) // CHUNK
    return jj <= ii


def _mix_fwd(o, p, ong, sgn, sgw, sgbt):
    T = o.shape[0]
    rb = SG_BLOCK

    def body(o_ref, gate_ref, u_ref, vg_ref, ong_ref, sgn_ref, sgw_ref, sgbt_ref, mix_ref):
        mask = _sg_mask()
        gate = gate_ref[...]
        for h in range(N_HEADS):
            sl = slice(h * HEAD_DIM, (h + 1) * HEAD_DIM)
            oh = o_ref[:, sl]
            r = lax.rsqrt(jnp.mean(oh * oh, axis=-1, keepdims=True) + EPS)
            mix_ref[:, sl] = (oh * r * ong_ref[...] * _silu(gate[:, sl])).astype(BF16)
        for gi in range(SG_GROUPS):
            sl = slice(gi * SG_BLOCK, (gi + 1) * SG_BLOCK)
            gv = _gelu(vg_ref[:, sl])
            r = lax.rsqrt(jnp.mean(gv * gv, axis=-1, keepdims=True) + EPS)
            vh = gv * r * sgn_ref[:, sl]
            s = _nn(jnp.where(mask, sgw_ref[gi], 0.0), vh) + sgbt_ref[:, gi:gi + 1]
            mix_ref[:, DN_WIDTH + gi * SG_BLOCK:DN_WIDTH + (gi + 1) * SG_BLOCK] = (_gelu(u_ref[:, sl]) * s).astype(BF16)

    def col(c):
        return pl.BlockSpec((rb, 512), lambda i: (i, c))
    return pl.pallas_call(
        body, name="mix_fwd", grid=(T // rb,),
        in_specs=[pl.BlockSpec((rb, DN_WIDTH), lambda i: (i, 0)), col(3), col(4), col(5),
                  pl.BlockSpec((1, 128), lambda i: (0, 0)), pl.BlockSpec((1, SG_WIDTH), lambda i: (0, 0)),
                  pl.BlockSpec((SG_GROUPS, SG_BLOCK, SG_BLOCK), lambda i: (0, 0, 0)),
                  pl.BlockSpec((SG_BLOCK, 128), lambda i: (0, 0))],
        out_specs=pl.BlockSpec((rb, D_MODEL), lambda i: (i, 0)),
        out_shape=jax.ShapeDtypeStruct((T, D_MODEL), BF16),
        compiler_params=_cp("parallel"))(o, p, p, p, ong, sgn, sgw, sgbt)


def _mix_bwd(o, p, ong, sgn, sgw, sgbt, dmix, dep=None):
    T = o.shape[0]
    rb = SG_BLOCK

    def body(o_ref, gate_ref, u_ref, vg_ref, ong_ref, sgn_ref, sgw_ref, sgbt_ref, dmix_ref, *rest):
        do_ref, dp_ref, gong_ref, gsgn_ref, gsgw_ref, gsgbt_ref = rest[-6:]
        @pl.when(pl.program_id(0) == 0)
        def _():
            gong_ref[...] = jnp.zeros_like(gong_ref)
            gsgn_ref[...] = jnp.zeros_like(gsgn_ref)
            gsgw_ref[...] = jnp.zeros_like(gsgw_ref)
            gsgbt_ref[...] = jnp.zeros_like(gsgbt_ref)
        mask = _sg_mask()
        gate = gate_ref[...]
        lane = _lane_iota((SG_BLOCK, 128))
        for h in range(N_HEADS):
            sl = slice(h * HEAD_DIM, (h + 1) * HEAD_DIM)
            oh = o_ref[:, sl]
            dm = dmix_ref[:, sl]
            r = lax.rsqrt(jnp.mean(oh * oh, axis=-1, keepdims=True) + EPS)
            oh_hat = oh * r
            gt = gate[:, sl]
            sg = _silu(gt)
            dp_ref[:, sl] = (dm * oh_hat * ong_ref[...] * _dsilu(gt)).astype(BF16)
            dn_ = dm * sg
            gong_ref[...] += jnp.sum(dn_ * oh_hat, axis=0, keepdims=True)
            dhat = dn_ * ong_ref[...]
            do_ref[:, sl] = r * (dhat - oh_hat * jnp.mean(dhat * oh_hat, axis=-1, keepdims=True))
        for gi in range(SG_GROUPS):
            sl = slice(gi * SG_BLOCK, (gi + 1) * SG_BLOCK)
            vraw = vg_ref[:, sl]
            gv = _gelu(vraw)
            r = lax.rsqrt(jnp.mean(gv * gv, axis=-1, keepdims=True) + EPS)
            vhat = gv * r
            vn = vhat * sgn_ref[:, sl]
            wm = jnp.where(mask, sgw_ref[gi], 0.0)
            s = _nn(wm, vn) + sgbt_ref[:, gi:gi + 1]
            uraw = u_ref[:, sl]
            dm = dmix_ref[:, DN_WIDTH + gi * SG_BLOCK:DN_WIDTH + (gi + 1) * SG_BLOCK]
            dp_ref[:, DN_WIDTH + gi * SG_BLOCK:DN_WIDTH + (gi + 1) * SG_BLOCK] = (dm * s * _dgelu(uraw)).astype(BF16)
            ds = dm * _gelu(uraw)
            gsgbt_ref[...] += jnp.where(lane == gi, jnp.sum(ds, axis=-1, keepdims=True), 0.0)
            gsgw_ref[gi] += jnp.where(mask, _nt(ds, vn), 0.0)
            dvn = _tn(wm, ds)
            gsgn_ref[:, sl] += jnp.sum(dvn * vhat, axis=0, keepdims=True)
            dhat = dvn * sgn_ref[:, sl]
            dgv = r * (dhat - vhat * jnp.mean(dhat * vhat, axis=-1, keepdims=True))
            dp_ref[:, 2 * DN_WIDTH + gi * SG_BLOCK:2 * DN_WIDTH + (gi + 1) * SG_BLOCK] = (dgv * _dgelu(vraw)).astype(BF16)

    def col(c):
        return pl.BlockSpec((rb, 512), lambda i: (i, c))
    full = lambda *s: pl.BlockSpec(s, lambda i: (0,) * len(s))
    in_specs, args = _with_dep(
        [pl.BlockSpec((rb, DN_WIDTH), lambda i: (i, 0)), col(3), col(4), col(5),
         full(1, 128), full(1, SG_WIDTH), full(SG_GROUPS, SG_BLOCK, SG_BLOCK), full(SG_BLOCK, 128),
         pl.BlockSpec((rb, D_MODEL), lambda i: (i, 0))],
        [o, p, p, p, ong, sgn, sgw, sgbt, dmix], dep)
    return pl.pallas_call(
        body, name="mix_bwd", grid=(T // rb,), in_specs=in_specs,
        out_specs=[pl.BlockSpec((rb, DN_WIDTH), lambda i: (i, 0)), pl.BlockSpec((rb, 3 * 512), lambda i: (i, 0)),
                   full(1, 128), full(1, SG_WIDTH), full(SG_GROUPS, SG_BLOCK, SG_BLOCK), full(SG_BLOCK, 128)],
        out_shape=[jax.ShapeDtypeStruct((T, DN_WIDTH), F32), jax.ShapeDtypeStruct((T, 3 * 512), BF16),
                   jax.ShapeDtypeStruct((1, 128), F32), jax.ShapeDtypeStruct((1, SG_WIDTH), F32),
                   jax.ShapeDtypeStruct((SG_GROUPS, SG_BLOCK, SG_BLOCK), F32),
                   jax.ShapeDtypeStruct((SG_BLOCK, 128), F32)],
        compiler_params=_cp("arbitrary"))(*args)


def _pad_lanes(row, offset=0):
    n = row.shape[1]
    return jnp.pad(row, ((0, 0), (offset, 128 - n - offset)))


def _local_step(x, tgt, w, dep=None, late_weights=None, on_grad=None):
    T = x.shape[0]
    N = T // CHUNK
    on_grad = on_grad or (lambda name, g: None)
    alog_row = _pad_lanes(w["dn_a_log"], N_HEADS)
    dtb_row = _pad_lanes(w["dn_dt_bias"], N_HEADS)
    sgbt = jnp.pad(w["sg_b"].T, ((0, 0), (0, 128 - SG_GROUPS)))

    p, h1, w_in_pad = _in_proj(x, w["attn_norm_g"], w["w_in"], dep=dep)
    q, k, v, bg = _dn_act(p, w["dn_conv_w"], alog_row, dtb_row)
    gc, gct, lmat = _dn_chunk(k, bg)
    lt = lmat.reshape(N * N_HEADS, CHUNK * CHUNK).T
    at = _tri_inv(lt)
    a = at.reshape(CHUNK * CHUNK, N * N_HEADS).T.reshape(N, N_HEADS, CHUNK, CHUNK)
    a_t = at.transpose(1, 0, 2).reshape(CHUNK * CHUNK, N * N_HEADS).T.reshape(N, N_HEADS, CHUNK, CHUNK)
    o, sall = _dn_scan(q, k, v, bg, gc, gct, a)
    mix = _mix_fwd(o, p, w["dn_out_norm_g"], w["sg_norm_g"], w["sg_w"], sgbt)
    if late_weights is not None:
        w = {**w, **late_weights("out_proj", mix)}
    x2, h2 = _out_proj(mix, w["w_out"], x, w["ffn_norm_g"])
    up, act = _up_proj_act(h2, w["w_up"], w["ffn_conv_w"], w["ffn_conv_b"])
    if late_weights is not None:
        w = {**w, **late_weights("down_proj", act)}
    loss, dx3, g_final = _down_proj_loss(act, w["w_down"], x2, tgt, w["final_norm_g"])

    dact = _mm_nt("d_act", dx3, w["w_down"], F32, 512, D_FF)
    g_w_down = _mm_tn("g_w_down", act, dx3, D_FF, 1024, 1024)
    tok = on_grad("w_down", g_w_down)
    dup, g_ffn_conv_w, g_ffn_conv_b = _ffn_act_bwd(up, dact, w["ffn_conv_w"], w["ffn_conv_b"], dep=tok)
    g_w_up = _mm_tn("g_w_up", h2, dup, 1024, 2 * D_FF // 4, 2048, col_major_tiles=True)
    tok = on_grad("w_up", g_w_up)
    dx2, g_ffn_norm = _mm_nt_rms_bwd("d_h2", dup, w["w_up"], x2, w["ffn_norm_g"], dx3, dep=tok)
    dmix = _mm_nt("d_mix", dx2, w["w_out"], F32, 512, 1024)
    g_w_out = _mm_tn("g_w_out", mix, dx2, 1024, 1024, 1024)
    tok = on_grad("w_out", g_w_out)
    do, dp_mid, g_ong, g_sgn, g_sgw, g_sgbt = _mix_bwd(o, p, w["dn_out_norm_g"], w["sg_norm_g"], w["sg_w"], sgbt,
                                                      dmix, dep=tok)
    early = dict(dn_out_norm_g=g_ong, sg_norm_g=g_sgn, sg_w=g_sgw, sg_bt=g_sgbt,
                 ffn_norm_g=g_ffn_norm, ffn_conv_w=g_ffn_conv_w, ffn_conv_b=g_ffn_conv_b, final_norm_g=g_final)
    tok = on_grad("small_early", early)
    dq, dk, dv, dbg = _dn_scan_bwd(q, k, v, bg, gc, gct, a, a_t, sall, do, dep=tok)
    dp, g_dn_conv_w, g_ad = _dn_act_bwd(p, w["dn_conv_w"], alog_row, dtb_row, dq, dk, dv, dbg, dp_mid)
    g_w_in = _mm_tn("g_w_in", h1, dp, 1024, PROJ_PAD, 1024, col_groups=(4, PROJ_COLS // 4))
    tok = on_grad("w_in", g_w_in)
    grad_x, g_attn_norm = _mm_nt_rms_bwd("d_h1", dp, w_in_pad, x, w["attn_norm_g"], dx2, dep=tok)

    grads = dict(attn_norm_g=g_attn_norm, w_in=g_w_in, dn_conv_w=g_dn_conv_w, a_dt=g_ad,
                 w_out=g_w_out, w_up=g_w_up, w_down=g_w_down, **early)
    return loss, grad_x, grads


def _me():
    return lax.axis_index("x"), lax.axis_index("y"), lax.axis_index("c")


def _peer(rel):
    x, y, c = _me()
    return {"x": (1 - x, y, c), "y": (x, 1 - y, c), "xy": (1 - x, 1 - y, c), "c": (x, y, 1 - c)}[rel]


def _chip_of(dev):
    return 2 * dev[0] + dev[1]


CHIP_RELS = ("x", "y", "xy")


def _run_copies(copies, sends, recvs):
    for cp in copies:
        cp.start()
    for cp in recvs:
        cp.wait_recv()
    for cp in sends:
        cp.wait_send()


def _gather_first(w_shard, small_shard):
    R = w_shard.shape[0]
    r2 = R // 2

    def body(w_ref, s_ref, w_out, s_out, send_sems, recv_sems):
        x, y, c = _me()
        me = _chip_of((x, y))
        sib = _peer("c")

        def half(chip, core):
            return w_out.at[chip, pl.ds(pl.multiple_of(core * r2, 8), r2), :]

        def copy(k, src, dst, to):
            return pltpu.make_async_remote_copy(src_ref=src, dst_ref=dst, send_sem=send_sems.at[k],
                                                recv_sem=recv_sems.at[k], device_id=to, device_id_type=MESH)

        own_rows = w_ref.at[pl.ds(pl.multiple_of(c * r2, 8), r2), :]
        first = [copy(r, own_rows, half(me, c), _peer(rel)) for r, rel in enumerate(CHIP_RELS)]
        first += [copy(3 + r, s_ref, s_out.at[me], _peer(rel)) for r, rel in enumerate(CHIP_RELS)]
        for cp in first:
            cp.start()
        passed = []
        for r, rel in enumerate(CHIP_RELS):
            their = _chip_of(_peer(rel))
            copy(r, own_rows, half(their, c), _peer(rel)).wait_recv()
            fwd = copy(6 + r, half(their, c), half(their, c), sib)
            fwd.start()
            passed.append(fwd)
        for r, rel in enumerate(CHIP_RELS):
            their = _chip_of(_peer(rel))
            copy(3 + r, s_ref, s_out.at[their], _peer(rel)).wait_recv()
            copy(6 + r, own_rows, half(their, 1 - c), sib).wait_recv()
        for cp in first + passed:
            cp.wait_send()

    w_all, s_all = pl.pallas_call(
        body, name="gather_first", in_specs=[ANY, ANY], out_specs=[ANY, ANY],
        out_shape=[jax.ShapeDtypeStruct((4,) + w_shard.shape, w_shard.dtype),
                   jax.ShapeDtypeStruct((4,) + small_shard.shape, small_shard.dtype)],
        scratch_shapes=[pltpu.SemaphoreType.DMA((9,)), pltpu.SemaphoreType.DMA((9,))])(w_shard, small_shard)
    me = _chip_of(_me())
    return (lax.dynamic_update_index_in_dim(w_all, w_shard, me, 0),
            lax.dynamic_update_index_in_dim(s_all, small_shard, me, 0))


OTHERS = tuple((fx, fy, fc) for fx in (0, 1) for fy in (0, 1) for fc in (0, 1) if (fx, fy, fc) != (0, 0, 0))


def _other(flip):
    x, y, c = _me()
    return (x ^ flip[0], y ^ flip[1], c ^ flip[2])


def _linear(dev):
    return 4 * dev[0] + 2 * dev[1] + dev[2]


def _exchange_small(small):
    def body(small_ref, out_ref, send_sems, recv_sems):
        my_slot = _linear(_me())
        sends, recvs = [], []
        for k, flip in enumerate(OTHERS):
            peer = _other(flip)
            sends.append(pltpu.make_async_remote_copy(
                src_ref=small_ref, dst_ref=out_ref.at[my_slot], send_sem=send_sems.at[k], recv_sem=recv_sems.at[k],
                device_id=peer, device_id_type=MESH))
            recvs.append(pltpu.make_async_remote_copy(
                src_ref=small_ref, dst_ref=out_ref.at[_linear(peer)], send_sem=send_sems.at[k],
                recv_sem=recv_sems.at[k], device_id=peer, device_id_type=MESH))
        _run_copies(sends, sends, recvs)

    out = pl.pallas_call(
        body, name="exchange_small", in_specs=[ANY], out_specs=ANY,
        out_shape=jax.ShapeDtypeStruct((8,) + small.shape, small.dtype),
        scratch_shapes=[pltpu.SemaphoreType.DMA((7,)), pltpu.SemaphoreType.DMA((7,))])(small)
    return lax.dynamic_update_index_in_dim(out, small, _linear(_me()), 0)


def _pair_swap(name, halves):
    n = len(halves)

    def body(*refs):
        src, out = refs[:n], refs[n:2 * n]
        send_sems, recv_sems = refs[2 * n:]
        sib = _peer("c")
        copies = [pltpu.make_async_remote_copy(
            src_ref=src[i], dst_ref=out[i], send_sem=send_sems.at[i], recv_sem=recv_sems.at[i],
            device_id=sib, device_id_type=MESH) for i in range(n)]
        _run_copies(copies, copies, copies)

    return pl.pallas_call(
        body, name=name, in_specs=[ANY] * n, out_specs=[ANY] * n,
        out_shape=[jax.ShapeDtypeStruct(h.shape, h.dtype) for h in halves],
        scratch_shapes=[pltpu.SemaphoreType.DMA((n,)), pltpu.SemaphoreType.DMA((n,))])(*halves)


HBM = pl.BlockSpec(memory_space=pltpu.HBM)
SEM = pl.BlockSpec(memory_space=pltpu.SEMAPHORE)
EFFECT = pltpu.SideEffectType.DATAFLOW_SIDE_EFFECTING


def _hbm(a):
    return pltpu.with_memory_space_constraint(a, pltpu.HBM)


def _transfer_start(name, srcs, lands, n_copies, make_copies, after=None):
    n, m = len(srcs), len(lands)

    def body(*refs):
        src, land = refs[:n], refs[n:n + m]
        outs = refs[n + m + (after is not None):]
        send_sems, recv_sems, token = outs[0], outs[1], outs[-1]
        for cp in make_copies(src, land, send_sems, recv_sems):
            cp.start()
        token[...] = jnp.zeros_like(token)

    arrs = list(srcs) + list(lands)
    in_specs, args = _with_dep([HBM] * (n + m), [_hbm(a) for a in arrs], after)
    out = pl.pallas_call(
        body, name=name,
        out_shape=(pltpu.SemaphoreType.DMA((n_copies,)), pltpu.SemaphoreType.DMA((n_copies,)),
                   *[pltpu.HBM(a.shape, a.dtype) for a in arrs], jax.ShapeDtypeStruct((8, 128), F32)),
        in_specs=in_specs,
        out_specs=(SEM, SEM, *[HBM] * (n + m), pl.BlockSpec(memory_space=pltpu.VMEM)),
        input_output_aliases={i: 2 + i for i in range(n + m)},
        compiler_params=pltpu.CompilerParams(has_side_effects=EFFECT))(*args)
    return out[0], out[1], list(out[2:2 + n]), list(out[2 + n:2 + n + m]), out[-1]


def _transfer_wait(name, send_sems, recv_sems, srcs, lands, make_copies, after):
    n, m = len(srcs), len(lands)

    def body(*refs):
        src, land = refs[:n], refs[n:n + m]
        s_sems, r_sems = refs[n + m], refs[n + m + 1]
        for cp in make_copies(src, land, s_sems, r_sems):
            cp.wait_send()
            cp.wait_recv()

    arrs = list(srcs) + list(lands)
    out = pl.pallas_call(
        body, name=name, out_shape=tuple(pltpu.HBM(a.shape, a.dtype) for a in arrs),
        in_specs=[HBM] * (n + m) + [SEM, SEM, ANY], out_specs=tuple([HBM] * (n + m)),
        input_output_aliases={i: i for i in range(n + m)},
        compiler_params=pltpu.CompilerParams(has_side_effects=EFFECT))(*arrs, send_sems, recv_sems, after)
    return list(out[:n]), list(out[n:])


def _gather_copies(src, land, send_sems, recv_sems):
    me = _chip_of(_me())
    copies = []
    for i in range(len(src)):
        for r, rel in enumerate(CHIP_RELS):
            k = 3 * i + r
            copies.append(pltpu.make_async_remote_copy(
                src_ref=src[i], dst_ref=land[i].at[me], send_sem=send_sems.at[k], recv_sem=recv_sems.at[k],
                device_id=_peer(rel), device_id_type=MESH))
    return copies


def _small_copies(src, land, send_sems, recv_sems):
    my_slot = _linear(_me())
    return [pltpu.make_async_remote_copy(
        src_ref=src[0], dst_ref=land[0].at[my_slot], send_sem=send_sems.at[k], recv_sem=recv_sems.at[k],
        device_id=_other(flip), device_id_type=MESH) for k, flip in enumerate(OTHERS)]


def _pieces_copies(src, land, send_sems, recv_sems):
    copies = []
    for k, flip in enumerate(OTHERS):
        peer = _other(flip)
        copies.append(pltpu.make_async_remote_copy(
            src_ref=src[0].at[_linear(peer)], dst_ref=land[0].at[k], send_sem=send_sems.at[k],
            recv_sem=recv_sems.at[k], device_id=peer, device_id_type=MESH))
    return copies


def _row_block(rows, cols, budget=2 * 1024 * 1024):
    rb = max(8, (budget // (4 * cols)) // 8 * 8)
    while rows % rb:
        rb -= 8
    return rb if rb > 0 else rows


def _sum_pieces(name, mine, slot, rest):
    _, R, Cc = mine.shape
    K = rest.shape[0]
    rb = _row_block(R, Cc)

    def body(s_ref, f_ref, r_ref, o_ref):
        acc = f_ref[0].astype(F32)
        for j in range(K):
            acc = acc + r_ref[j].astype(F32)
        o_ref[...] = acc

    return pl.pallas_call(
        body, name=name,
        grid_spec=pltpu.PrefetchScalarGridSpec(
            num_scalar_prefetch=1, grid=(R // rb,),
            in_specs=[pl.BlockSpec((1, rb, Cc), lambda i, s: (s[0], i, 0)),
                      pl.BlockSpec((K, rb, Cc), lambda i, s: (0, i, 0))],
            out_specs=pl.BlockSpec((rb, Cc), lambda i, s: (i, 0))),
        out_shape=jax.ShapeDtypeStruct((R, Cc), F32), compiler_params=_cp("parallel"))(slot, mine, rest)


def _adamw_math(w, gv, m, v):
    mn = ADAM_B1 * m + (1.0 - ADAM_B1) * gv
    vn = ADAM_B2 * v + (1.0 - ADAM_B2) * (gv * gv)
    m_hat = mn / (1.0 - ADAM_B1 ** ADAM_STEP)
    v_hat = vn / (1.0 - ADAM_B2 ** ADAM_STEP)
    return -ADAM_LR * (m_hat / (jnp.sqrt(v_hat) + ADAM_EPS) + ADAM_WD * w), mn, vn


def _adamw_halves(name, w, mine, theirs, m, v, core):
    R, Cc = w.shape
    r2 = R // 2
    rb = _row_block(r2, Cc, 1024 * 1024)
    nb2 = r2 // rb

    def body(c_ref, w_ref, mine_ref, theirs_ref, m_ref, v_ref, g_ref, d_ref, mo_ref, vo_ref):
        is_mine = (pl.program_id(0) // nb2) == c_ref[0]
        gv = jnp.where(is_mine, mine_ref[...], theirs_ref[...])
        g_ref[...] = gv
        d_ref[...], mo_ref[...], vo_ref[...] = _adamw_math(w_ref[...], gv, m_ref[...], v_ref[...])

    blk = pl.BlockSpec((rb, Cc), lambda i, c: (i, 0))
    half = lambda own: pl.BlockSpec(
        (rb, Cc), lambda i, c: (jnp.clip(i - (c[0] if own else 1 - c[0]) * nb2, 0, nb2 - 1), 0))
    return pl.pallas_call(
        body, name=name,
        grid_spec=pltpu.PrefetchScalarGridSpec(
            num_scalar_prefetch=1, grid=(2 * nb2,), in_specs=[blk, half(True), half(False), blk, blk],
            out_specs=[blk] * 4),
        out_shape=[jax.ShapeDtypeStruct((R, Cc), F32)] * 4, compiler_params=_cp("parallel"))(core, w, mine, theirs, m, v)


def _adamw_transposed(name, wt, mine, theirs, mt, vt, core):
    Cc, kh_n, _ = wt.shape
    r2 = mine.shape[0]
    per_half = kh_n // 2
    nb = -(-Cc // LANES)

    def body(c_ref, w_ref, mine_ref, theirs_ref, m_ref, v_ref, g_ref, d_ref, mo_ref, vo_ref):
        first = c_ref[0] == 0
        halves = (jnp.where(first, mine_ref[...], theirs_ref[...]).T,
                  jnp.where(first, theirs_ref[...], mine_ref[...]).T)
        for kh in range(kh_n):
            lo = (kh % per_half) * LANES
            g_ref[:, kh, :] = halves[kh // per_half][:, lo:lo + LANES]
        d_ref[...], mo_ref[...], vo_ref[...] = _adamw_math(w_ref[...], g_ref[...], m_ref[...], v_ref[...])

    blk = pl.BlockSpec((LANES, kh_n, LANES), lambda i, c: (i, 0, 0))
    half = pl.BlockSpec((r2, LANES), lambda i, c: (0, i))
    return pl.pallas_call(
        body, name=name,
        grid_spec=pltpu.PrefetchScalarGridSpec(
            num_scalar_prefetch=1, grid=(nb,), in_specs=[blk, half, half, blk, blk], out_specs=[blk] * 4),
        out_shape=[jax.ShapeDtypeStruct(wt.shape, F32)] * 4, compiler_params=_cp("parallel"))(
            core, wt, mine, theirs, mt, vt)


FF_W = 2 * D_FF
FF_CH = FF_W // LANES
DNC_W = 3 * DN_WIDTH
DNC_CH = DNC_W // LANES
E_ONG, E_SGN, E_SGW, E_SGBT = 0, 1, 8, 8 + SG_GROUPS * SG_BLOCK
E_FFN = E_SGBT + SG_BLOCK
E_FCW = E_FFN + D_MODEL // LANES
E_FCB = E_FCW + 3 * FF_CH
E_FIN = E_FCB + FF_CH
EARLY_ROWS = E_FIN + D_MODEL // LANES
L_ATTN, L_DNC = 0, D_MODEL // LANES
L_AD = L_DNC + 4 * DNC_CH
L_LOSS = L_AD + 2
LATE_ROWS = -(-(L_LOSS + 1) // 8) * 8


def _put_rows(out, r0, x):
    k, width = x.shape
    n = width // LANES
    for t in range(k):
        for j in range(n):
            out[r0 + t * n + j:r0 + t * n + j + 1, :] = x[t:t + 1, j * LANES:(j + 1) * LANES]


def _pack_early(ong, sgn, sgw, sgbt, ffn, fcw, fcb, fin):
    def body(ong_ref, sgn_ref, sgw_ref, sgbt_ref, ffn_ref, fcw_ref, fcb_ref, fin_ref, out):
        out[...] = jnp.zeros_like(out)
        _put_rows(out, E_ONG, ong_ref)
        _put_rows(out, E_SGN, sgn_ref)
        for gi in range(SG_GROUPS):
            out[E_SGW + gi * SG_BLOCK:E_SGW + (gi + 1) * SG_BLOCK, :] = sgw_ref[gi]
        out[E_SGBT:E_SGBT + SG_BLOCK, :] = sgbt_ref[...]
        _put_rows(out, E_FFN, ffn_ref)
        _put_rows(out, E_FCW, fcw_ref)
        _put_rows(out, E_FCB, fcb_ref)
        _put_rows(out, E_FIN, fin_ref)

    return pl.pallas_call(body, name="pack_small_early", out_shape=jax.ShapeDtypeStruct((EARLY_ROWS, LANES), F32))(
        ong, sgn, sgw, sgbt, ffn, fcw, fcb, fin)


def _pack_late(attn, dnc, ad, loss_row):
    def body(attn_ref, dnc_ref, ad_ref, loss_ref, out):
        out[...] = jnp.zeros_like(out)
        _put_rows(out, L_ATTN, attn_ref)
        _put_rows(out, L_DNC, dnc_ref)
        out[L_AD:L_AD + 2, :] = ad_ref[...]
        out[L_LOSS:L_LOSS + 1, :] = loss_ref[...]

    return pl.pallas_call(body, name="pack_small_late", out_shape=jax.ShapeDtypeStruct((LATE_ROWS, LANES), F32))(
        attn, dnc, ad, loss_row)


SMALL = ("attn_norm_g", "dn_a_log", "dn_dt_bias", "dn_out_norm_g", "sg_norm_g", "sg_w", "sg_b", "ffn_norm_g",
         "ffn_conv_b", "final_norm_g", "dn_conv_w", "ffn_conv_w")


def _small_update(early_all, late_all, chip, W, M, V):
    n = len(SMALL)
    arrs = [d[k] for d in (W, M, V) for k in SMALL]

    def body(c_ref, e_ref, l_ref, *refs):
        w_, m_, v_ = refs[:n], refs[n:2 * n], refs[2 * n:3 * n]
        loss_ref = refs[3 * n]
        outs = refs[3 * n + 1:]
        g_, d_, mo_, vo_ = outs[:n], outs[n:2 * n], outs[2 * n:3 * n], outs[3 * n:4 * n]
        chip_i = c_ref[0]

        def total(ref, r0, rows=1):
            acc = ref[0, pl.ds(r0, rows), :]
            for s in range(1, 8):
                acc = acc + ref[s, pl.ds(r0, rows), :]
            return acc

        def update(i, idx, g):
            g_[i][idx] = g
            d_[i][idx], mo_[i][idx], vo_[i][idx] = _adamw_math(w_[i][idx], g, m_[i][idx], v_[i][idx])

        def rows_param(name, ref, r0, width):
            i = SMALL.index(name)
            for j in range(width // LANES):
                update(i, (slice(None), slice(j * LANES, (j + 1) * LANES)), total(ref, r0 + j))

        rows_param("attn_norm_g", l_ref, L_ATTN, D_MODEL)
        ad = (total(l_ref, L_AD), total(l_ref, L_AD + 1))
        update(SMALL.index("dn_a_log"), (slice(None), slice(None)), ad[0][:, N_HEADS:2 * N_HEADS])
        update(SMALL.index("dn_dt_bias"), (slice(None), slice(None)), ad[1][:, N_HEADS:2 * N_HEADS])
        rows_param("dn_out_norm_g", e_ref, E_ONG, HEAD_DIM)
        rows_param("sg_norm_g", e_ref, E_SGN, SG_WIDTH)
        sgbt = total(e_ref, E_SGBT, SG_BLOCK).T
        for gi in range(SG_GROUPS):
            update(SMALL.index("sg_w"), (0, gi), total(e_ref, E_SGW + gi * SG_BLOCK, SG_BLOCK))
            update(SMALL.index("sg_b"), (0, slice(gi, gi + 1), slice(None)), sgbt[gi:gi + 1, :])
        rows_param("ffn_norm_g", e_ref, E_FFN, D_MODEL)
        rows_param("ffn_conv_b", e_ref, E_FCB, FF_W)
        rows_param("final_norm_g", e_ref, E_FIN, D_MODEL)
        for name, ref, r0, taps, chunks in (("dn_conv_w", l_ref, L_DNC, 4, DNC_CH), ("ffn_conv_w", e_ref, E_FCW, 3, FF_CH)):
            mine = chunks // 4
            for t in range(taps):
                for j in range(mine):
                    update(SMALL.index(name), (0, slice(t, t + 1), slice(j * LANES, (j + 1) * LANES)),
                           total(ref, r0 + t * chunks + chip_i * mine + j))
        loss_ref[...] = total(l_ref, L_LOSS)

    full = lambda a: pl.BlockSpec(a.shape, lambda i, c, nd=a.ndim: (0,) * nd)
    shapes = [jax.ShapeDtypeStruct(W[k].shape, F32) for k in SMALL]
    outs = pl.pallas_call(
        body, name="small_update",
        grid_spec=pltpu.PrefetchScalarGridSpec(
            num_scalar_prefetch=1, grid=(1,), in_specs=[full(early_all), full(late_all)] + [full(a) for a in arrs],
            out_specs=[pl.BlockSpec((1, LANES), lambda i, c: (0, 0))] + [full(s) for s in shapes] * 4),
        out_shape=[jax.ShapeDtypeStruct((1, LANES), F32)] + shapes * 4,
        compiler_params=pltpu.CompilerParams(vmem_limit_bytes=VMEM_LIMIT))(chip, early_all, late_all, *arrs)
    loss, outs = outs[0], outs[1:]
    return (loss,) + tuple(dict(zip(SMALL, outs[k * n:(k + 1) * n])) for k in range(4))


ORDER =("attn_norm_g", "w_in", "dn_conv_w", "dn_a_log", "dn_dt_bias", "dn_out_norm_g", "sg_norm_g", "sg_w",
         "sg_b", "w_out", "ffn_norm_g", "w_up", "ffn_conv_w", "ffn_conv_b", "w_down", "final_norm_g")


def kernel(x, attn_norm_g, w_in, dn_conv_w, dn_a_log, dn_dt_bias, dn_out_norm_g, sg_norm_g, sg_w, sg_b, w_out, ffn_norm_g, w_up, ffn_conv_w, ffn_conv_b, w_down, final_norm_g, loss_target, m_attn_norm_g, m_w_in, m_dn_conv_w, m_dn_a_log, m_dn_dt_bias, m_dn_out_norm_g, m_sg_norm_g, m_sg_w, m_sg_b, m_w_out, m_ffn_norm_g, m_w_up, m_ffn_conv_w, m_ffn_conv_b, m_w_down, m_final_norm_g, v_attn_norm_g, v_w_in, v_dn_conv_w, v_dn_a_log, v_dn_dt_bias, v_dn_out_norm_g, v_sg_norm_g, v_sg_w, v_sg_b, v_w_out, v_ffn_norm_g, v_w_up, v_ffn_conv_w, v_ffn_conv_b, v_w_down, v_final_norm_g):
    W = dict(attn_norm_g=attn_norm_g, w_in=w_in, dn_conv_w=dn_conv_w, dn_a_log=dn_a_log, dn_dt_bias=dn_dt_bias,
             dn_out_norm_g=dn_out_norm_g, sg_norm_g=sg_norm_g, sg_w=sg_w, sg_b=sg_b, w_out=w_out,
             ffn_norm_g=ffn_norm_g, w_up=w_up, ffn_conv_w=ffn_conv_w, ffn_conv_b=ffn_conv_b, w_down=w_down,
             final_norm_g=final_norm_g)
    Mo = dict(attn_norm_g=m_attn_norm_g, w_in=m_w_in, dn_conv_w=m_dn_conv_w, dn_a_log=m_dn_a_log,
              dn_dt_bias=m_dn_dt_bias, dn_out_norm_g=m_dn_out_norm_g, sg_norm_g=m_sg_norm_g, sg_w=m_sg_w,
              sg_b=m_sg_b, w_out=m_w_out, ffn_norm_g=m_ffn_norm_g, w_up=m_w_up, ffn_conv_w=m_ffn_conv_w,
              ffn_conv_b=m_ffn_conv_b, w_down=m_w_down, final_norm_g=m_final_norm_g)
    Vo = dict(attn_norm_g=v_attn_norm_g, w_in=v_w_in, dn_conv_w=v_dn_conv_w, dn_a_log=v_dn_a_log,
              dn_dt_bias=v_dn_dt_bias, dn_out_norm_g=v_dn_out_norm_g, sg_norm_g=v_sg_norm_g, sg_w=v_sg_w,
              sg_b=v_sg_b, w_out=v_w_out, ffn_norm_g=v_ffn_norm_g, w_up=v_w_up, ffn_conv_w=v_ffn_conv_w,
              ffn_conv_b=v_ffn_conv_b, w_down=v_w_down, final_norm_g=v_final_norm_g)
    xi, yi, ci = lax.axis_index("x"), lax.axis_index("y"), lax.axis_index("c")
    chip = 2 * xi + yi

    me_lin = 4 * xi + 2 * yi + ci

    g_in, g_dnc = _gather_first(w_in[0].astype(BF16), dn_conv_w[0])
    def start_gather(name, shards, after):
        lands = [lax.dynamic_update_index_in_dim(lax.empty((4,) + s.shape, s.dtype), s, chip, 0) for s in shards]
        return _transfer_start(name, shards, lands, 3 * len(shards), _gather_copies, after=after)

    mid = start_gather("gather_mid_start", [w_out[0].astype(BF16), w_up[0].astype(BF16), ffn_conv_w[0]], g_in)
    last = start_gather("gather_last_start", [w_down[0].astype(BF16)], mid[4])
    token = last[4]

    def late_weights(stage, after):
        if stage == "out_proj":
            _, (g_out, g_up, g_ffc) = _transfer_wait("gather_mid_wait", *mid[:4], _gather_copies, after)
            return dict(w_out=g_out.reshape(D_MODEL, D_MODEL), ffn_conv_w=g_ffc.transpose(1, 0, 2).reshape(3, 2 * D_FF),
                        w_up=g_up.transpose(1, 0, 2).reshape(D_MODEL, 2 * D_FF))
        _, (g_down,) = _transfer_wait("gather_last_wait", *last[:4], _gather_copies, after)
        return dict(w_down=g_down.reshape(D_FF, D_MODEL))

    full = dict(
        w_in=g_in,
        dn_conv_w=g_dnc.transpose(1, 0, 2).reshape(4, 3 * DN_WIDTH),
        attn_norm_g=attn_norm_g, dn_a_log=dn_a_log, dn_dt_bias=dn_dt_bias, dn_out_norm_g=dn_out_norm_g,
        sg_norm_g=sg_norm_g, sg_w=sg_w[0], sg_b=sg_b[0], ffn_norm_g=ffn_norm_g, ffn_conv_b=ffn_conv_b,
        final_norm_g=final_norm_g[None])

    pending = {}

    def on_grad(name, gw):
        if name == "small_early":
            buf = _pack_early(gw["dn_out_norm_g"], gw["sg_norm_g"], gw["sg_w"], gw["sg_bt"], gw["ffn_norm_g"],
                              gw["ffn_conv_w"], gw["ffn_conv_b"], gw["final_norm_g"])
            land = lax.dynamic_update_index_in_dim(lax.empty((8,) + buf.shape, F32), buf, me_lin, 0)
            s_sem, r_sem, src, lands, tok = _transfer_start("small_early_start", [buf], [land], 7, _small_copies)
            pending[name] = (s_sem, r_sem, src, lands)
            return tok
        g8 = gw.reshape(8, -1, gw.shape[-1])
        land = lax.empty((7,) + g8.shape[1:], BF16)
        s_sem, r_sem, src, lands, tok = _transfer_start(f"reduce_{name}_start", [g8], [land], 7, _pieces_copies)
        pending[name] = (s_sem, r_sem, src, lands)
        return tok

    loss_row, grad_x, g = _local_step(x[0], loss_target[0], full, dep=token, late_weights=late_weights,
                                      on_grad=on_grad)

    late_all = _exchange_small(_pack_late(g["attn_norm_g"], g["dn_conv_w"], g["a_dt"], loss_row))
    s_sem, r_sem, src, lands = pending["small_early"]
    _, (early_all,) = _transfer_wait("small_early_wait", s_sem, r_sem, src, lands, _small_copies, grad_x)
    row = lambda d: {k: (d[k].reshape(1, -1) if k == "final_norm_g" else d[k]) for k in SMALL}
    loss_sum, *small_out = _small_update(early_all, late_all, chip.astype(jnp.int32).reshape(1), row(W), row(Mo), row(Vo))
    loss = loss_sum[0, 0]

    def summed_half(n, after):
        s_sem, r_sem, src, lands = pending[n]
        sent, got = _transfer_wait(f"reduce_{n}_wait", s_sem, r_sem, src, lands, _pieces_copies, after)
        return _sum_pieces(f"sum_{n}", sent[0], me_lin.astype(jnp.int32).reshape(1), got[0])

    first3 = ("w_down", "w_up", "w_out")
    halves = [summed_half(n, grad_x) for n in first3]
    theirs = _pair_swap("pair_swap", halves)
    core = ci.astype(jnp.int32).reshape(1)
    grads, delta, new_m, new_v = {}, {}, {}, {}
    for n, mine_h, their_h in zip(first3, halves, theirs):
        shp = W[n].shape
        gr, d, mn, vn = _adamw_halves(f"adamw_{n}", W[n][0], mine_h, their_h, Mo[n][0], Vo[n][0], core)
        grads[n], delta[n], new_m[n], new_v[n] = gr.reshape(shp), d.reshape(shp), mn.reshape(shp), vn.reshape(shp)
    mine_h = summed_half("w_in", delta["w_out"])
    (their_h,) = _pair_swap("pair_swap_w_in", [mine_h])
    shp = w_in.shape
    to_t = lambda a: a.reshape(shp[1] // LANES, LANES, shp[2]).transpose(2, 0, 1)
    from_t = lambda a: a.transpose(1, 2, 0).reshape(shp)
    outs = _adamw_transposed("adamw_w_in", to_t(w_in), mine_h, their_h, to_t(m_w_in), to_t(v_w_in), core)
    grads["w_in"], delta["w_in"], new_m["w_in"], new_v["w_in"] = (from_t(o) for o in outs)
    for dst, src_d in zip((grads, delta, new_m, new_v), small_out):
        dst.update({k: (a.reshape(W[k].shape) if k == "final_norm_g" else a) for k, a in src_d.items()})

    return (loss, grad_x[None], *[grads[n] for n in ORDER], *[delta[n] for n in ORDER],
            *[new_m[n] for n in ORDER], *[new_v[n] for n in ORDER])
```

```python
import functools
import math

import jax
import jax.numpy as jnp
from jax import lax
from jax.experimental import pallas as pl
from jax.experimental.pallas import tpu as pltpu

F32 = jnp.float32
BF16 = jnp.bfloat16

D_MODEL = 1024
CHUNK = 64
SCAN_CHUNKS = 4
SCAN_CHUNKS_FWD = 8
HEAD_DIM = 128
N_HEADS = 4
DN_WIDTH = 512
SG_WIDTH = 512
SG_GROUPS = 4
SG_BLOCK = 128
D_FF = 2816
PROJ_COLS = 3080
PROJ_PAD = 3200
BA_COL = 3072
EPS = 1e-6
NEG = -1e30
VMEM_LIMIT = 56 * 1024 * 1024

ADAM_LR = 0.001
ADAM_B1 = 0.9
ADAM_B2 = 0.999
ADAM_EPS = 1e-08
ADAM_WD = 0.01
ADAM_STEP = 10

MESH = pl.DeviceIdType.MESH
ANY = pl.BlockSpec(memory_space=pl.ANY)


def _cp(*sem):
    return pltpu.CompilerParams(dimension_semantics=sem, vmem_limit_bytes=VMEM_LIMIT)


def _bf(a):
    return a.astype(BF16)


def _nn(a, b):
    return jnp.dot(_bf(a), _bf(b), preferred_element_type=F32)


def _nt(a, b):
    return lax.dot_general(_bf(a), _bf(b), (((1,), (1,)), ((), ())), preferred_element_type=F32)


def _tn(a, b):
    return lax.dot_general(_bf(a), _bf(b), (((0,), (0,)), ((), ())), preferred_element_type=F32)


def _split(a):
    hi = _bf(a)
    return hi, _bf(a - hi.astype(F32))


def _sigmoid(x):
    return 0.5 * jnp.tanh(0.5 * x) + 0.5


def _silu(x):
    return x * _sigmoid(x)


def _dsilu(x):
    s = _sigmoid(x)
    return s * (1.0 + x * (1.0 - s))


_GELU_C = math.sqrt(2.0 / math.pi)
_GELU_A = 0.044715


def _gelu(x):
    return 0.5 * x * (1.0 + jnp.tanh(_GELU_C * (x + _GELU_A * x * x * x)))


def _dgelu(x):
    t = jnp.tanh(_GELU_C * (x + _GELU_A * x * x * x))
    return 0.5 * (1.0 + t) + 0.5 * x * (1.0 - t * t) * _GELU_C * (1.0 + 3.0 * _GELU_A * x * x)


def _softplus(x):
    return jnp.maximum(x, 0.0) + jnp.log(1.0 + jnp.exp(-jnp.abs(x)))


def _with_dep(in_specs, args, dep):
    if dep is None:
        return in_specs, args
    return in_specs + [ANY], args + [dep]


SUB_ROWS = 128


def _sub_blocks(tm):
    return [slice(r0, min(r0 + SUB_ROWS, tm)) for r0 in range(0, tm, SUB_ROWS)]


def _rms_hat(xv):
    r = lax.rsqrt(jnp.mean(xv * xv, axis=-1, keepdims=True) + EPS)
    return xv * r, r


def _rms_bwd_vals(dh, xh, r, g):
    dxh = dh * g
    return r * (dxh - xh * jnp.mean(dxh * xh, axis=-1, keepdims=True)), jnp.sum(dh * xh, axis=0, keepdims=True)


def _in_proj(x, g, w4, tm=512, dep=None):
    T, K = x.shape
    ng, _, wc = w4.shape
    tm = min(tm, T)

    def body(x_ref, g_ref, w4_ref, *rest):
        p_ref, h_ref, w_ref = rest[-3:]

        @pl.when(pl.program_id(0) == 0)
        def _():
            w_ref[:, ng * wc:] = jnp.zeros((K, PROJ_PAD - ng * wc), BF16)
            for j in range(ng):
                w_ref[:, j * wc:(j + 1) * wc] = w4_ref[j]
        for r in _sub_blocks(tm):
            xh, _ = _rms_hat(x_ref[r, :])
            h_ref[r, :] = (xh * g_ref[...]).astype(BF16)
        p_ref[...] = jnp.dot(h_ref[...], w_ref[...], preferred_element_type=F32)

    in_specs, args = _with_dep(
        [pl.BlockSpec((tm, K), lambda i: (i, 0)), pl.BlockSpec((1, K), lambda i: (0, 0)),
         pl.BlockSpec((ng, K, wc), lambda i: (0, 0, 0))], [x, g, w4], dep)
    return pl.pallas_call(
        body, name="in_proj", grid=(T // tm,), in_specs=in_specs,
        out_specs=[pl.BlockSpec((tm, PROJ_PAD), lambda i: (i, 0)), pl.BlockSpec((tm, K), lambda i: (i, 0)),
                   pl.BlockSpec((K, PROJ_PAD), lambda i: (0, 0))],
        out_shape=[jax.ShapeDtypeStruct((T, PROJ_PAD), F32), jax.ShapeDtypeStruct((T, K), BF16),
                   jax.ShapeDtypeStruct((K, PROJ_PAD), BF16)],
        compiler_params=_cp("arbitrary"))(*args)


def _out_proj(mix, w, x, g, tm=512):
    T, K = mix.shape
    Dm = w.shape[1]
    tm = min(tm, T)

    def body(a_ref, w_ref, x_ref, g_ref, x2_ref, h_ref):
        x2_ref[...] = _nn(a_ref[...], w_ref[...]) + x_ref[...]
        for r in _sub_blocks(tm):
            xh, _ = _rms_hat(x2_ref[r, :])
            h_ref[r, :] = (xh * g_ref[...]).astype(BF16)

    row = lambda width: pl.BlockSpec((tm, width), lambda i: (i, 0))
    return pl.pallas_call(
        body, name="out_proj", grid=(T // tm,),
        in_specs=[row(K), pl.BlockSpec((K, Dm), lambda i: (0, 0)), row(Dm), pl.BlockSpec((1, Dm), lambda i: (0, 0))],
        out_specs=[row(Dm), row(Dm)],
        out_shape=[jax.ShapeDtypeStruct((T, Dm), F32), jax.ShapeDtypeStruct((T, Dm), BF16)],
        compiler_params=_cp("parallel"))(mix, w, x, g)


def _down_proj_loss(act, w, x2, tgt, g, tm=512):
    T, K = act.shape
    Dm = w.shape[1]
    tm = min(tm, T)

    def body(a_ref, w_ref, x_ref, t_ref, g_ref, loss_ref, dx_ref, gg_ref):
        @pl.when(pl.program_id(0) == 0)
        def _():
            gg_ref[...] = jnp.zeros_like(gg_ref)
            loss_ref[...] = jnp.zeros_like(loss_ref)
        dx_ref[...] = _nn(a_ref[...], w_ref[...]) + x_ref[...]
        for r in _sub_blocks(tm):
            xh, rr = _rms_hat(dx_ref[r, :])
            e = xh * g_ref[...] - t_ref[r, :]
            loss_ref[...] += jnp.zeros_like(loss_ref) + (0.5 / Dm) * jnp.sum(e * e)
            dx, gg = _rms_bwd_vals(e * (1.0 / Dm), xh, rr, g_ref[...])
            dx_ref[r, :] = dx
            gg_ref[...] += gg

    row = lambda width: pl.BlockSpec((tm, width), lambda i: (i, 0))
    vec = pl.BlockSpec((1, Dm), lambda i: (0, 0))
    return pl.pallas_call(
        body, name="down_proj_loss", grid=(T // tm,),
        in_specs=[row(K), pl.BlockSpec((K, Dm), lambda i: (0, 0)), row(Dm), row(Dm), vec],
        out_specs=[pl.BlockSpec((1, 128), lambda i: (0, 0)), row(Dm), vec],
        out_shape=[jax.ShapeDtypeStruct((1, 128), F32), jax.ShapeDtypeStruct((T, Dm), F32),
                   jax.ShapeDtypeStruct((1, Dm), F32)],
        compiler_params=_cp("arbitrary"))(act, w, x2, tgt, g)


def _mm_nt_rms_bwd(name, a, b, x, g, dres, tm=512, dep=None):
    M, K = a.shape
    Dm = b.shape[0]
    tm = min(tm, M)

    def body(a_ref, b_ref, x_ref, g_ref, dres_ref, *rest):
        dx_ref, gg_ref = rest[-2:]

        @pl.when(pl.program_id(0) == 0)
        def _():
            gg_ref[...] = jnp.zeros_like(gg_ref)
        dx_ref[...] = _nt(a_ref[...], b_ref[...])
        for r in _sub_blocks(tm):
            xh, rr = _rms_hat(x_ref[r, :])
            dx, gg = _rms_bwd_vals(dx_ref[r, :], xh, rr, g_ref[...])
            dx_ref[r, :] = dres_ref[r, :] + dx
            gg_ref[...] += gg

    row = lambda width: pl.BlockSpec((tm, width), lambda i: (i, 0))
    vec = pl.BlockSpec((1, Dm), lambda i: (0, 0))
    in_specs, args = _with_dep([row(K), pl.BlockSpec((Dm, K), lambda i: (0, 0)), row(Dm), vec, row(Dm)],
                               [a, b, x, g, dres], dep)
    return pl.pallas_call(
        body, name=name, grid=(M // tm,), in_specs=in_specs, out_specs=[row(Dm), vec],
        out_shape=[jax.ShapeDtypeStruct((M, Dm), F32), jax.ShapeDtypeStruct((1, Dm), F32)],
        compiler_params=_cp("arbitrary"))(*args)


def _mm_nt(name, a, b, out_dtype, tm, tn, dep=None):
    M, K = a.shape
    N = b.shape[0]
    tm, tn = min(tm, M), min(tn, N)

    def body(a_ref, b_ref, *rest):
        o_ref = rest[-1]
        o_ref[...] = _nt(a_ref[...], b_ref[...]).astype(o_ref.dtype)

    in_specs, args = _with_dep(
        [pl.BlockSpec((tm, K), lambda i, j: (i, 0)), pl.BlockSpec((tn, K), lambda i, j: (j, 0))], [a, b], dep)
    return pl.pallas_call(
        body, name=name, grid=(M // tm, N // tn), in_specs=in_specs,
        out_specs=pl.BlockSpec((tm, tn), lambda i, j: (i, j)),
        out_shape=jax.ShapeDtypeStruct((M, N), out_dtype),
        compiler_params=_cp("parallel", "parallel"))(*args)


def _mm_tn(name, a, b, tm, tn, tk, col_major_tiles=False, col_groups=None):
    T, M = a.shape
    N = b.shape[1]
    tm, tn, tk = min(tm, M), min(tn, N), min(tk, T)
    nk = T // tk

    def body(a_ref, b_ref, o_ref, acc_ref):
        k = pl.program_id(2)

        @pl.when(k == 0)
        def _():
            acc_ref[...] = jnp.zeros_like(acc_ref)
        acc_ref[...] += _tn(a_ref[...], b_ref[...])

        @pl.when(k == nk - 1)
        def _():
            if col_groups:
                for j in range(col_groups[0]):
                    o_ref[j] = acc_ref[:, j * col_groups[1]:(j + 1) * col_groups[1]].astype(BF16)
            else:
                o_ref[...] = acc_ref[...].astype(BF16).reshape(o_ref.shape)

    if col_groups:
        assert tm == M and tn == N and col_groups[0] * col_groups[1] <= N
        out_spec = pl.BlockSpec((col_groups[0], M, col_groups[1]), lambda i, j, k: (0, 0, 0))
        out_shape = jax.ShapeDtypeStruct((col_groups[0], M, col_groups[1]), BF16)
    elif col_major_tiles:
        assert tm == M
        out_spec = pl.BlockSpec((1, tm, tn), lambda i, j, k: (j, 0, 0))
        out_shape = jax.ShapeDtypeStruct((N // tn, M, tn), BF16)
    else:
        out_spec = pl.BlockSpec((tm, tn), lambda i, j, k: (i, j))
        out_shape = jax.ShapeDtypeStruct((M, N), BF16)
    return pl.pallas_call(
        body, name=name, grid=(M // tm, N // tn, nk),
        in_specs=[pl.BlockSpec((tk, tm), lambda i, j, k: (k, i)), pl.BlockSpec((tk, tn), lambda i, j, k: (k, j))],
        out_specs=out_spec, out_shape=out_shape, scratch_shapes=[pltpu.VMEM((tm, tn), F32)],
        compiler_params=_cp("parallel", "parallel", "arbitrary"))(a, b)


def _halo_prev_spec(rb, width):
    return pl.BlockSpec((8, width), lambda i: (jnp.maximum(i * (rb // 8) - 1, 0), 0))


def _halo_next_spec(rb, width, T):
    return pl.BlockSpec((8, width), lambda i: (jnp.minimum((i + 1) * (rb // 8), T // 8 - 1), 0))


LANES = 128
FF_STRIPS = D_FF // LANES
ROW_CHUNK = 32


def _strip(j, base=0):
    return pl.ds(pl.multiple_of(base + j * LANES, LANES), LANES)


def _up_proj_act(h, w_up, w, b, rb=256):
    T, K = h.shape
    W = w_up.shape[1]
    rb = min(rb, T)
    nb = T // rb

    def body(h_ref, wup_ref, w_ref, b_ref, up_ref, act_ref, prev_scr, ext_scr, tail_scr):
        @pl.when(pl.program_id(0) == 0)
        def _():
            tail_scr[...] = jnp.zeros_like(tail_scr)
            prev_scr[...] = jnp.zeros_like(prev_scr)
        up_ref[...] = jnp.dot(h_ref[...], wup_ref[...], preferred_element_type=F32)
        for j in range(FF_STRIPS):
            slot = j % 2
            halves = (slice(j * LANES, (j + 1) * LANES), slice(D_FF + j * LANES, D_FF + (j + 1) * LANES))
            wv = [w_ref[:, cols] for cols in halves]
            bv = [b_ref[:, cols] for cols in halves]
            for hh, cols in enumerate(halves):
                ext_scr[slot, hh, 0:8] = tail_scr[:, cols]
                ext_scr[slot, hh, 8:] = prev_scr[:, cols]
            for r0 in range(0, rb, ROW_CHUNK):
                n = min(ROW_CHUNK, rb - r0)
                c = [ext_scr[slot, hh, 6 + r0:6 + r0 + n] * wv[hh][0:1] + ext_scr[slot, hh, 7 + r0:7 + r0 + n] * wv[hh][1:2]
                     + ext_scr[slot, hh, 8 + r0:8 + r0 + n] * wv[hh][2:3] + bv[hh] for hh in range(2)]
                act_ref[r0:r0 + n, halves[0]] = (_silu(c[0]) * c[1]).astype(BF16)
        tail_scr[...] = prev_scr[rb - 8:rb, :]
        prev_scr[...] = up_ref[...]

    cur = lambda i: (jnp.minimum(i, nb - 1), 0)
    return pl.pallas_call(
        body, name="up_proj_act", grid=(nb + 1,),
        in_specs=[pl.BlockSpec((rb, K), cur), pl.BlockSpec((K, W), lambda i: (0, 0)),
                  pl.BlockSpec((3, W), lambda i: (0, 0)), pl.BlockSpec((1, W), lambda i: (0, 0))],
        out_specs=[pl.BlockSpec((rb, W), cur), pl.BlockSpec((rb, D_FF), lambda i: (jnp.maximum(i - 1, 0), 0))],
        out_shape=[jax.ShapeDtypeStruct((T, W), F32), jax.ShapeDtypeStruct((T, D_FF), BF16)],
        scratch_shapes=[pltpu.VMEM((rb, W), F32), pltpu.VMEM((2, 2, rb + 8, LANES), F32), pltpu.VMEM((8, W), F32)],
        compiler_params=_cp("arbitrary"))(h, w_up, w, b)


def _ffn_act_bwd(up, dact, w, b, rb=128, dep=None):
    T, W = up.shape
    rb = min(rb, T)
    nb = T // rb
    re = rb + 8

    def body(up_ref, prev_ref, next_ref, da_ref, danext_ref, w_ref, b_ref, *rest):
        dup_ref, gw_ref, gb_ref, ext_scr, dc_scr = rest[-5:]
        i = pl.program_id(0)

        @pl.when(i == 0)
        def _():
            gw_ref[...] = jnp.zeros_like(gw_ref)
            gb_ref[...] = jnp.zeros_like(gb_ref)
        last = i == nb - 1

        def fold8(a):
            return jnp.sum(a.reshape(a.shape[0] // 8, 8, LANES), axis=0)

        def strip(j, slot):
            halves = (_strip(j), _strip(j, D_FF))
            wv = [w_ref[:, cols] for cols in halves]
            bv = [b_ref[:, cols] for cols in halves]
            for h, cols in enumerate(halves):
                ext_scr[slot, h,0:8] = jnp.where(i > 0, prev_ref[:, cols], 0.0)
                ext_scr[slot, h,8:8 + rb] = up_ref[:, cols]
                ext_scr[slot, h,8 + rb:] = next_ref[:, cols]
            gb = [jnp.zeros((8, LANES), F32) for _ in range(2)]
            gw = [[jnp.zeros((8, LANES), F32) for _ in range(3)] for _ in range(2)]
            for r0 in range(0, re, ROW_CHUNK):
                n = min(ROW_CHUNK, re - r0)
                tp = [[ext_scr[slot, h,6 + k + r0:6 + k + r0 + n] for k in range(3)] for h in range(2)]
                c = [tp[h][0] * wv[h][0:1] + tp[h][1] * wv[h][1:2] + tp[h][2] * wv[h][2:3] + bv[h] for h in range(2)]
                if r0 < rb:
                    da = da_ref[r0:r0 + n, halves[0]]
                else:
                    da = jnp.where(last, 0.0, danext_ref[:, halves[0]])
                s = _sigmoid(c[0])
                gs = c[0] * s
                dcs = (da * c[1] * (s + gs * (1.0 - s)), da * gs)
                for h in range(2):
                    dc_scr[slot, h,r0:r0 + n] = dcs[h]
                    if r0 < rb:
                        gb[h] = gb[h] + fold8(dcs[h])
                        for k in range(3):
                            gw[h][k] = gw[h][k] + fold8(tp[h][k] * dcs[h])
            for r0 in range(0, rb, ROW_CHUNK):
                n = min(ROW_CHUNK, rb - r0)
                for h, cols in enumerate(halves):
                    dup = (dc_scr[slot, h,r0:r0 + n] * wv[h][2:3] + dc_scr[slot, h,r0 + 1:r0 + 1 + n] * wv[h][1:2]
                           + dc_scr[slot, h,r0 + 2:r0 + 2 + n] * wv[h][0:1])
                    dup_ref[r0:r0 + n, cols] = dup.astype(BF16)
            for h, cols in enumerate(halves):
                gb_ref[:, cols] += jnp.sum(gb[h], axis=0, keepdims=True)
                for k in range(3):
                    gw_ref[k:k + 1, cols] += jnp.sum(gw[h][k], axis=0, keepdims=True)

        def pair(jj, carry):
            strip(2 * jj, 0)
            strip(2 * jj + 1, 1)
            return carry

        lax.fori_loop(0, FF_STRIPS // 2, pair, 0)

    in_specs, args = _with_dep(
        [pl.BlockSpec((rb, W), lambda i: (i, 0)), _halo_prev_spec(rb, W), _halo_next_spec(rb, W, T),
         pl.BlockSpec((rb, D_FF), lambda i: (i, 0)), _halo_next_spec(rb, D_FF, T),
         pl.BlockSpec((3, W), lambda i: (0, 0)), pl.BlockSpec((1, W), lambda i: (0, 0))],
        [up, up, up, dact, dact, w, b], dep)
    return pl.pallas_call(
        body, name="ffn_act_bwd", grid=(nb,), in_specs=in_specs,
        out_specs=[pl.BlockSpec((rb, W), lambda i: (i, 0)), pl.BlockSpec((3, W), lambda i: (0, 0)),
                   pl.BlockSpec((1, W), lambda i: (0, 0))],
        out_shape=[jax.ShapeDtypeStruct((T, W), BF16), jax.ShapeDtypeStruct((3, W), F32),
                   jax.ShapeDtypeStruct((1, W), F32)],
        scratch_shapes=[pltpu.VMEM((2, 2, rb + 16, LANES), F32), pltpu.VMEM((2, 2, re, LANES), F32)],
        compiler_params=_cp("arbitrary"))(*args)


def _lane_iota(shape):
    return lax.broadcasted_iota(jnp.int32, shape, len(shape) - 1)


def _dn_act(p, conv_w, alog_row, dtb_row, rb=256):
    T = p.shape[0]
    rb = min(rb, T)
    W3 = 3 * DN_WIDTH

    def body(p_ref, halo_ref, ba_ref, w_ref, al_ref, dt_ref, q_ref, k_ref, v_ref, bg_ref, ext_scr):
        first = pl.program_id(0) == 0
        outs = (q_ref, k_ref, v_ref)
        for j in range(3 * N_HEADS):
            kind, h = divmod(j, N_HEADS)
            cols = slice(j * HEAD_DIM, (j + 1) * HEAD_DIM)
            cur = p_ref[:, cols]
            ext_scr[j, 0:8] = jnp.where(first, 0.0, halo_ref[:, cols])
            ext_scr[j, 8:] = cur
            wv = w_ref[:, cols]
            s = _silu(ext_scr[j, 5:5 + rb] * wv[0:1] + ext_scr[j, 6:6 + rb] * wv[1:2]
                      + ext_scr[j, 7:7 + rb] * wv[2:3] + cur * wv[3:4])
            if kind < 2:
                scale = HEAD_DIM ** -0.5 if kind == 0 else 1.0
                s = s * (lax.rsqrt(jnp.sum(s * s, axis=-1, keepdims=True) + EPS) * scale)
            outs[kind][:, h * HEAD_DIM:(h + 1) * HEAD_DIM] = s
        ba = ba_ref[...]
        lane = _lane_iota(ba.shape)
        beta = _sigmoid(ba)
        g = -jnp.exp(al_ref[...]) * _softplus(ba + dt_ref[...])
        bg_ref[...] = jnp.where(lane < N_HEADS, beta, jnp.where(lane < 2 * N_HEADS, g, 0.0))

    row512 = pl.BlockSpec((rb, DN_WIDTH), lambda i: (i, 0))
    row128 = pl.BlockSpec((rb, 128), lambda i: (i, 0))
    vec128 = pl.BlockSpec((1, 128), lambda i: (0, 0))
    return pl.pallas_call(
        body, name="dn_act", grid=(T // rb,),
        in_specs=[pl.BlockSpec((rb, W3), lambda i: (i, 0)), _halo_prev_spec(rb, W3),
                  pl.BlockSpec((rb, 128), lambda i: (i, BA_COL // 128)),
                  pl.BlockSpec((4, W3), lambda i: (0, 0)), vec128, vec128],
        out_specs=[row512, row512, row512, row128],
        out_shape=[jax.ShapeDtypeStruct((T, DN_WIDTH), F32)] * 3 + [jax.ShapeDtypeStruct((T, 128), F32)],
        scratch_shapes=[pltpu.VMEM((3 * N_HEADS, rb + 8, HEAD_DIM), F32)],
        compiler_params=_cp("parallel"))(p, p, p, conv_w, alog_row, dtb_row)


def _dn_act_bwd(p, conv_w, alog_row, dtb_row, dq, dk, dv, dbg, dp_mid, rb=256):
    T = p.shape[0]
    rb = min(rb, T)
    nb = T // rb
    re = rb + 8
    W3 = 3 * DN_WIDTH

    def body(p_ref, prev_ref, next_ref, ba_ref, w_ref, al_ref, dt_ref, dq_ref, dqn_ref, dk_ref, dkn_ref,
             dv_ref, dvn_ref, dbg_ref, mid_ref, draw_ref, gw_ref, gad_ref, ext_scr, dc_scr):
        i = pl.program_id(0)
        draw_ref[:, W3:2 * W3] = mid_ref[...]

        @pl.when(i == 0)
        def _():
            gw_ref[...] = jnp.zeros_like(gw_ref)
            gad_ref[...] = jnp.zeros_like(gad_ref)
        row = lax.broadcasted_iota(jnp.int32, (re, 1), 0)
        live = (row < rb) | (i < nb - 1)
        d_refs = ((dq_ref, dqn_ref), (dk_ref, dkn_ref), (dv_ref, dvn_ref))
        for j in range(3 * N_HEADS):
            kind, h = divmod(j, N_HEADS)
            cols = slice(j * HEAD_DIM, (j + 1) * HEAD_DIM)
            hcols = slice(h * HEAD_DIM, (h + 1) * HEAD_DIM)
            ext_scr[j, 0:8] = jnp.where(i > 0, prev_ref[:, cols], 0.0)
            ext_scr[j, 8:8 + rb] = p_ref[:, cols]
            ext_scr[j, 8 + rb:] = next_ref[:, cols]
            tp = [ext_scr[j, 5 + k:5 + k + re] for k in range(4)]
            wv = w_ref[:, cols]
            c = tp[0] * wv[0:1] + tp[1] * wv[1:2] + tp[2] * wv[2:3] + tp[3] * wv[3:4]
            sg = _sigmoid(c)
            s = c * sg
            d_in = jnp.where(live, jnp.concatenate([d_refs[kind][0][:, hcols], d_refs[kind][1][:, hcols]], axis=0), 0.0)
            if kind < 2:
                scale = HEAD_DIM ** -0.5 if kind == 0 else 1.0
                n = lax.rsqrt(jnp.sum(s * s, axis=-1, keepdims=True) + EPS)
                hat = s * n
                d_in = (n * scale) * (d_in - hat * jnp.sum(hat * d_in, axis=-1, keepdims=True))
            dc = d_in * (sg + s * (1.0 - sg))
            dc_scr[j] = dc
            dcc = dc[0:rb]
            draw = (dcc * wv[3:4] + dc_scr[j, 1:1 + rb] * wv[2:3] + dc_scr[j, 2:2 + rb] * wv[1:2]
                    + dc_scr[j, 3:3 + rb] * wv[0:1])
            draw_ref[:, cols] = draw.astype(BF16)
            for k in range(4):
                gw_ref[k:k + 1, cols] += jnp.sum(tp[k][0:rb] * dcc, axis=0, keepdims=True)
        ba = ba_ref[...]
        dbg = dbg_ref[...]
        lane = _lane_iota(ba.shape)
        beta = _sigmoid(ba)
        ea = jnp.exp(al_ref[...])
        z = ba + dt_ref[...]
        d_a = dbg * (-ea) * _sigmoid(z)
        dba = jnp.where(lane < N_HEADS, dbg * beta * (1.0 - beta), jnp.where(lane < 2 * N_HEADS, d_a, 0.0))
        draw_ref[:, BA_COL:] = dba.astype(BF16)
        isg = (lane >= N_HEADS) & (lane < 2 * N_HEADS)
        g = -ea * _softplus(z)
        gad_ref[0:1, :] += jnp.sum(jnp.where(isg, dbg * g, 0.0), axis=0, keepdims=True)
        gad_ref[1:2, :] += jnp.sum(jnp.where(isg, d_a, 0.0), axis=0, keepdims=True)

    row512 = pl.BlockSpec((rb, DN_WIDTH), lambda i: (i, 0))
    row128 = pl.BlockSpec((rb, 128), lambda i: (i, 0))
    vec128 = pl.BlockSpec((1, 128), lambda i: (0, 0))
    next512 = _halo_next_spec(rb, DN_WIDTH, T)
    return pl.pallas_call(
        body, name="dn_act_bwd", grid=(nb,),
        in_specs=[pl.BlockSpec((rb, W3), lambda i: (i, 0)), _halo_prev_spec(rb, W3), _halo_next_spec(rb, W3, T),
                  pl.BlockSpec((rb, 128), lambda i: (i, BA_COL // 128)),
                  pl.BlockSpec((4, W3), lambda i: (0, 0)), vec128, vec128,
                  row512, next512, row512, next512, row512, next512, row128,
                  pl.BlockSpec((rb, W3), lambda i: (i, 0))],
        out_specs=[pl.BlockSpec((rb, PROJ_PAD), lambda i: (i, 0)),
                   pl.BlockSpec((4, W3), lambda i: (0, 0)), pl.BlockSpec((2, 128), lambda i: (0, 0))],
        out_shape=[jax.ShapeDtypeStruct((T, PROJ_PAD), BF16),
                   jax.ShapeDtypeStruct((4, W3), F32), jax.ShapeDtypeStruct((2, 128), F32)],
        scratch_shapes=[pltpu.VMEM((3 * N_HEADS, rb + 16, HEAD_DIM), F32), pltpu.VMEM((3 * N_HEADS, re, HEAD_DIM), F32)],
        compiler_params=_cp("arbitrary"))(p, p, p, p, conv_w, alog_row, dtb_row, dq, dq, dk, dk, dv, dv, dbg, dp_mid)


def _tri(incl):
    ii = lax.broadcasted_iota(jnp.int32, (CHUNK, CHUNK), 0)
    jj = lax.broadcasted_iota(jnp.int32, (CHUNK, CHUNK), 1)
    return ii, jj, ((ii >= jj) if incl else (ii > jj))


def _dn_chunk(k, bg, cb=4):
    T = k.shape[0]
    N = T // CHUNK
    cb = min(cb, N)

    def body(k_ref, bg_ref, gc_ref, gct_ref, l_ref):
        ii, jj, incl = _tri(True)
        tri = incl.astype(F32)
        U = range(cb)
        bgv = [bg_ref[u * CHUNK:(u + 1) * CHUNK, :] for u in U]
        gc = [jnp.dot(tri, bgv[u], precision=lax.Precision.HIGHEST, preferred_element_type=F32) for u in U]
        gct = [gc[u].T for u in U]
        kk = [[None] * N_HEADS for _ in U]
        for u in U:
            gc_ref[u * CHUNK:(u + 1) * CHUNK, :] = gc[u]
            gct_ref[u] = gct[u][0:8]
            for h in range(N_HEADS):
                kh = k_ref[u * CHUNK:(u + 1) * CHUNK, h * HEAD_DIM:(h + 1) * HEAD_DIM]
                kk[u][h] = _nt(kh * bgv[u][:, h:h + 1], kh)
        for u in U:
            for h in range(N_HEADS):
                gcol = gc[u][:, N_HEADS + h:N_HEADS + h + 1]
                grow = gct[u][N_HEADS + h:N_HEADS + h + 1, :]
                l_ref[u, h] = kk[u][h] * jnp.exp(jnp.where(ii > jj, gcol - grow, NEG))

    rows = cb * CHUNK
    return pl.pallas_call(
        body, name="dn_chunk", grid=(N // cb,),
        in_specs=[pl.BlockSpec((rows, DN_WIDTH), lambda n: (n, 0)), pl.BlockSpec((rows, 128), lambda n: (n, 0))],
        out_specs=[pl.BlockSpec((rows, 128), lambda n: (n, 0)), pl.BlockSpec((cb, 8, CHUNK), lambda n: (n, 0, 0)),
                   pl.BlockSpec((cb, N_HEADS, CHUNK, CHUNK), lambda n: (n, 0, 0, 0))],
        out_shape=[jax.ShapeDtypeStruct((T, 128), F32), jax.ShapeDtypeStruct((N, 8, CHUNK), F32),
                   jax.ShapeDtypeStruct((N, N_HEADS, CHUNK, CHUNK), F32)],
        compiler_params=_cp("parallel"))(k, bg)


def _tri_inv(lt):
    S = lt.shape[1]

    def body(l_ref, a_ref):
        sub = lax.broadcasted_iota(jnp.int32, (8, S), 0)
        groups = CHUNK // 8
        for i in range(CHUNK):
            acc = [((sub + 8 * k) == i).astype(F32) for k in range(groups)]
            for jb in range((i + 7) // 8):
                nk = jb + 1

                def step(j, carry, nk=nk, i=i):
                    lrow = l_ref[pl.ds(i * CHUNK + j, 1), :]
                    return tuple(carry[k] - lrow * a_ref[j, 8 * k:8 * k + 8, :] for k in range(nk))

                acc[:nk] = list(lax.fori_loop(8 * jb, min(8 * jb + 8, i), step, tuple(acc[:nk])))
            for k in range(groups):
                a_ref[i, 8 * k:8 * k + 8, :] = acc[k]

    return pl.pallas_call(
        body, name="tri_inv", out_shape=jax.ShapeDtypeStruct((CHUNK, CHUNK, S), F32),
        compiler_params=pltpu.CompilerParams(vmem_limit_bytes=VMEM_LIMIT))(lt)


def _dn_head_terms(qh, kh, vh, beta, gcol, grow):
    ii, jj, incl = _tri(True)
    gam = jnp.exp(jnp.where(incl, gcol - grow, NEG))
    glast = grow[:, CHUNK - 1:CHUNK]
    cd = jnp.exp(glast)
    shape = (CHUNK, HEAD_DIM)
    E = jnp.broadcast_to(jnp.exp(gcol), shape)
    Fd = jnp.broadcast_to(jnp.exp(glast - gcol), shape)
    beta = jnp.broadcast_to(beta, shape)
    kb = kh * beta
    return dict(ii=ii, jj=jj, gam=gam, E=E, F=Fd, beta=beta, cd=cd, kb=kb, vb=vh * beta, W=kb * E, qE=qh * E,
                kt=kh * Fd)


def _apply_a(a, u):
    hi, lo = _split(a)
    ub = _bf(u)
    return jnp.dot(hi, ub, preferred_element_type=F32) + jnp.dot(lo, ub, preferred_element_type=F32)


def _dn_scan(q, k, v, bg, gc, gct, a):
    T = q.shape[0]
    N = T // CHUNK
    cb = min(SCAN_CHUNKS_FWD, N)

    def body(q_ref, k_ref, v_ref, bg_ref, gc_ref, gct_ref, a_ref, o_ref, sall_ref, s_ref):
        @pl.when(pl.program_id(0) == 0)
        def _():
            s_ref[...] = jnp.zeros_like(s_ref)
        H = range(N_HEADS)
        sl = [slice(h * HEAD_DIM, (h + 1) * HEAD_DIM) for h in H]
        pre = []
        for u in range(cb):
            r = slice(u * CHUNK, (u + 1) * CHUNK)
            bgv, gcv, gctv = bg_ref[r, :], gc_ref[r, :], gct_ref[u]
            q_, k_ = [q_ref[r, s] for s in sl], [k_ref[r, s] for s in sl]
            t = [_dn_head_terms(q_[h], k_[h], v_ref[r, sl[h]], bgv[:, h:h + 1],
                                gcv[:, N_HEADS + h:N_HEADS + h + 1], gctv[N_HEADS + h:N_HEADS + h + 1, :]) for h in H]
            P = [_nt(q_[h], k_[h]) * t[h]["gam"] for h in H]
            pre.append((r, t, P))
        S = [s_ref[h] for h in H]
        for u in range(cb):
            r, t, P = pre[u]
            for h in H:
                sall_ref[u, h] = S[h]
            WS = [_nn(t[h]["W"], S[h]) for h in H]
            qS = [_nn(t[h]["qE"], S[h]) for h in H]
            vn = [_apply_a(a_ref[u, h], t[h]["vb"] - WS[h]) for h in H]
            Pv = [_nn(P[h], vn[h]) for h in H]
            kv = [_tn(t[h]["kt"], vn[h]) for h in H]
            for h in H:
                o_ref[r, sl[h]] = qS[h] + Pv[h]
            S = [t[h]["cd"] * S[h] + kv[h] for h in H]
        for h in H:
            s_ref[h] = S[h]

    row512 = pl.BlockSpec((cb * CHUNK, DN_WIDTH), lambda n: (n, 0))
    row128 = pl.BlockSpec((cb * CHUNK, 128), lambda n: (n, 0))
    return pl.pallas_call(
        body, name="dn_scan", grid=(N // cb,),
        in_specs=[row512, row512, row512, row128, row128, pl.BlockSpec((cb, 8, CHUNK), lambda n: (n, 0, 0)),
                  pl.BlockSpec((cb, N_HEADS, CHUNK, CHUNK), lambda n: (n, 0, 0, 0))],
        out_specs=[row512, pl.BlockSpec((cb, N_HEADS, HEAD_DIM, HEAD_DIM), lambda n: (n, 0, 0, 0))],
        out_shape=[jax.ShapeDtypeStruct((T, DN_WIDTH), F32),
                   jax.ShapeDtypeStruct((N, N_HEADS, HEAD_DIM, HEAD_DIM), F32)],
        scratch_shapes=[pltpu.VMEM((N_HEADS, HEAD_DIM, HEAD_DIM), F32)],
        compiler_params=_cp("arbitrary"))(q, k, v, bg, gc, gct, a)


def _dn_scan_bwd(q, k, v, bg, gc, gct, a, a_t, sall, do, dep=None):
    T = q.shape[0]
    N = T // CHUNK

    cb = min(SCAN_CHUNKS, N)
    nb = N // cb

    def body(q_ref, k_ref, v_ref, bg_ref, gc_ref, gct_ref, a_ref, at_ref, sall_ref, do_ref, *rest):
        dq_ref, dk_ref, dv_ref, dbg_ref, ds_ref = rest[-5:]
        @pl.when(pl.program_id(0) == 0)
        def _():
            ds_ref[...] = jnp.zeros_like(ds_ref)
        lane = _lane_iota((CHUNK, 128))
        rowi = lax.broadcasted_iota(jnp.int32, (CHUNK, 1), 0)
        ii, jj, _ = _tri(True)
        rev = (jj >= ii).astype(F32)
        H = range(N_HEADS)
        sl = [slice(h * HEAD_DIM, (h + 1) * HEAD_DIM) for h in H]
        pre = {}
        for u in reversed(range(cb)):
            r = slice(u * CHUNK, (u + 1) * CHUNK)
            bgv, gcv, gctv = bg_ref[r, :], gc_ref[r, :], gct_ref[u]
            q_, k_, v_ = [q_ref[r, s] for s in sl], [k_ref[r, s] for s in sl], [v_ref[r, s] for s in sl]
            dO = [do_ref[r, s] for s in sl]
            t = [_dn_head_terms(q_[h], k_[h], v_[h], bgv[:, h:h + 1], gcv[:, N_HEADS + h:N_HEADS + h + 1],
                                gctv[N_HEADS + h:N_HEADS + h + 1, :]) for h in H]
            beta = [t[h]["beta"] for h in H]
            S = [sall_ref[u, h] for h in H]
            A = [a_ref[u, h] for h in H]
            WS = [_nn(t[h]["W"], S[h]) for h in H]
            KK = [_nt(t[h]["kb"], k_[h]) for h in H]
            QK = [_nt(q_[h], k_[h]) for h in H]
            d_qE = [_nt(dO[h], S[h]) for h in H]
            vn = [_apply_a(A[h], t[h]["vb"] - WS[h]) for h in H]
            PtdO = [_tn(QK[h] * t[h]["gam"], dO[h]) for h in H]
            qEdO = [_tn(t[h]["qE"], dO[h]) for h in H]
            dOvn = [_nt(dO[h], vn[h]) for h in H]
            dQK = [jnp.where(ii >= jj, dOvn[h], 0.0) * t[h]["gam"] for h in H]
            dQKk = [_nn(dQK[h], k_[h]) for h in H]
            dQKq = [_tn(dQK[h], q_[h]) for h in H]
            pre[u] = (r, q_, k_, v_, beta, t, S, A, KK, QK, d_qE, vn, PtdO, qEdO, dQK, dQKk, dQKq)
        dSn = [ds_ref[h] for h in H]
        for u in reversed(range(cb)):
            r, q_, k_, v_, beta, t, S, A, KK, QK, d_qE, vn, PtdO, qEdO, dQK, dQKk, dQKq = pre[u]
            gam, E, Fd, cd, kb = ([t[h][n] for h in H] for n in ("gam", "E", "F", "cd", "kb"))
            ktdS = [_nn(t[h]["kt"], dSn[h]) for h in H]
            dU = [_apply_a(at_ref[u, h], PtdO[h] + ktdS[h]) for h in H]
            d_kt = [_nt(vn[h], dSn[h]) for h in H]
            dUvn = [_nt(dU[h], vn[h]) for h in H]
            dUS = [_nt(dU[h], S[h]) for h in H]
            WdU = [_tn(t[h]["W"], dU[h]) for h in H]
            d_cd = [jnp.sum(S[h] * dSn[h]) for h in H]
            dSn = [cd[h] * dSn[h] + qEdO[h] - WdU[h] for h in H]
            dKK = [jnp.where(ii > jj, -dUvn[h], 0.0) * gam[h] for h in H]
            dKKk = [_nn(dKK[h], k_[h]) for h in H]
            dKKkb = [_tn(dKK[h], kb[h]) for h in H]
            dbeta_arr = jnp.zeros((CHUNK, 128), F32)
            dgc_arr = jnp.zeros((CHUNK, 128), F32)
            for h in H:
                dW = -dUS[h]
                dq_ref[r, sl[h]] = dQKk[h] + d_qE[h] * E[h]
                d_kb = dKKk[h] + dW * E[h]
                dk_ref[r, sl[h]] = dQKq[h] + dKKkb[h] + d_kb * beta[h] + d_kt[h] * Fd[h]
                dv_ref[r, sl[h]] = dU[h] * beta[h]
                Z = dQK[h] * QK[h] + dKK[h] * KK[h]
                dbeta = jnp.sum(dU[h] * v_[h] + d_kb * k_[h], axis=-1, keepdims=True)
                m_e = (dW * kb[h] + d_qE[h] * q_[h]) * E[h]
                m_f = d_kt[h] * k_[h] * Fd[h]
                zdiag = jnp.where(ii == jj, jnp.sum(Z, axis=0, keepdims=True), 0.0)
                dgc = (jnp.sum(m_e - m_f, axis=-1, keepdims=True) + jnp.sum(Z - zdiag, axis=-1, keepdims=True)
                       + jnp.where(rowi == CHUNK - 1, jnp.sum(m_f) + d_cd[h] * cd[h], 0.0))
                dbeta_arr = dbeta_arr + jnp.where(lane == h, dbeta, 0.0)
                dgc_arr = dgc_arr + jnp.where(lane == N_HEADS + h, dgc, 0.0)
            dbg_ref[r, :] = dbeta_arr + jnp.dot(rev, dgc_arr, precision=lax.Precision.HIGHEST,
                                                preferred_element_type=F32)
        for h in H:
            ds_ref[h] = dSn[h]

    row512 = pl.BlockSpec((cb * CHUNK, DN_WIDTH), lambda n: (nb - 1 - n, 0))
    row128 = pl.BlockSpec((cb * CHUNK, 128), lambda n: (nb - 1 - n, 0))
    in_specs, args = _with_dep(
        [row512, row512, row512, row128, row128,
         pl.BlockSpec((cb, 8, CHUNK), lambda n: (nb - 1 - n, 0, 0)),
         pl.BlockSpec((cb, N_HEADS, CHUNK, CHUNK), lambda n: (nb - 1 - n, 0, 0, 0)),
         pl.BlockSpec((cb, N_HEADS, CHUNK, CHUNK), lambda n: (nb - 1 - n, 0, 0, 0)),
         pl.BlockSpec((cb, N_HEADS, HEAD_DIM, HEAD_DIM), lambda n: (nb - 1 - n, 0, 0, 0)), row512],
        [q, k, v, bg, gc, gct, a, a_t, sall, do], dep)
    return pl.pallas_call(
        body, name="dn_scan_bwd", grid=(nb,), in_specs=in_specs,
        out_specs=[row512, row512, row512, row128],
        out_shape=[jax.ShapeDtypeStruct((T, DN_WIDTH), F32)] * 3 + [jax.ShapeDtypeStruct((T, 128), F32)],
        scratch_shapes=[pltpu.VMEM((N_HEADS, HEAD_DIM, HEAD_DIM), F32)],
        compiler_params=_cp("arbitrary"))(*args)


def _sg_mask():
    ii = lax.broadcasted_iota(jnp.int32, (SG_BLOCK, SG_BLOCK), 0) // CHUNK
    jj = lax.broadcasted_iota(jnp.int32, (SG_BLOCK, SG_BLOCK), 1) // CHUNK
    return jj <= ii


def _mix_fwd(o, p, ong, sgn, sgw, sgbt):
    T = o.shape[0]
    rb = SG_BLOCK

    def body(o_ref, gate_ref, u_ref, vg_ref, ong_ref, sgn_ref, sgw_ref, sgbt_ref, mix_ref):
        mask = _sg_mask()
        gate = gate_ref[...]
        for h in range(N_HEADS):
            sl = slice(h * HEAD_DIM, (h + 1) * HEAD_DIM)
            oh = o_ref[:, sl]
            r = lax.rsqrt(jnp.mean(oh * oh, axis=-1, keepdims=True) + EPS)
            mix_ref[:, sl] = (oh * r * ong_ref[...] * _silu(gate[:, sl])).astype(BF16)
        for gi in range(SG_GROUPS):
            sl = slice(gi * SG_BLOCK, (gi + 1) * SG_BLOCK)
            gv = _gelu(vg_ref[:, sl])
            r = lax.rsqrt(jnp.mean(gv * gv, axis=-1, keepdims=True) + EPS)
            vh = gv * r * sgn_ref[:, sl]
            s = _nn(jnp.where(mask, sgw_ref[gi], 0.0), vh) + sgbt_ref[:, gi:gi + 1]
            mix_ref[:, DN_WIDTH + gi * SG_BLOCK:DN_WIDTH + (gi + 1) * SG_BLOCK] = (_gelu(u_ref[:, sl]) * s).astype(BF16)

    def col(c):
        return pl.BlockSpec((rb, 512), lambda i: (i, c))
    return pl.pallas_call(
        body, name="mix_fwd", grid=(T // rb,),
        in_specs=[pl.BlockSpec((rb, DN_WIDTH), lambda i: (i, 0)), col(3), col(4), col(5),
                  pl.BlockSpec((1, 128), lambda i: (0, 0)), pl.BlockSpec((1, SG_WIDTH), lambda i: (0, 0)),
                  pl.BlockSpec((SG_GROUPS, SG_BLOCK, SG_BLOCK), lambda i: (0, 0, 0)),
                  pl.BlockSpec((SG_BLOCK, 128), lambda i: (0, 0))],
        out_specs=pl.BlockSpec((rb, D_MODEL), lambda i: (i, 0)),
        out_shape=jax.ShapeDtypeStruct((T, D_MODEL), BF16),
        compiler_params=_cp("parallel"))(o, p, p, p, ong, sgn, sgw, sgbt)


def _mix_bwd(o, p, ong, sgn, sgw, sgbt, dmix, dep=None):
    T = o.shape[0]
    rb = SG_BLOCK

    def body(o_ref, gate_ref, u_ref, vg_ref, ong_ref, sgn_ref, sgw_ref, sgbt_ref, dmix_ref, *rest):
        do_ref, dp_ref, gong_ref, gsgn_ref, gsgw_ref, gsgbt_ref = rest[-6:]
        @pl.when(pl.program_id(0) == 0)
        def _():
            gong_ref[...] = jnp.zeros_like(gong_ref)
            gsgn_ref[...] = jnp.zeros_like(gsgn_ref)
            gsgw_ref[...] = jnp.zeros_like(gsgw_ref)
            gsgbt_ref[...] = jnp.zeros_like(gsgbt_ref)
        mask = _sg_mask()
        gate = gate_ref[...]
        lane = _lane_iota((SG_BLOCK, 128))
        for h in range(N_HEADS):
            sl = slice(h * HEAD_DIM, (h + 1) * HEAD_DIM)
            oh = o_ref[:, sl]
            dm = dmix_ref[:, sl]
            r = lax.rsqrt(jnp.mean(oh * oh, axis=-1, keepdims=True) + EPS)
            oh_hat = oh * r
            gt = gate[:, sl]
            sg = _silu(gt)
            dp_ref[:, sl] = (dm * oh_hat * ong_ref[...] * _dsilu(gt)).astype(BF16)
            dn_ = dm * sg
            gong_ref[...] += jnp.sum(dn_ * oh_hat, axis=0, keepdims=True)
            dhat = dn_ * ong_ref[...]
            do_ref[:, sl] = r * (dhat - oh_hat * jnp.mean(dhat * oh_hat, axis=-1, keepdims=True))
        for gi in range(SG_GROUPS):
            sl = slice(gi * SG_BLOCK, (gi + 1) * SG_BLOCK)
            vraw = vg_ref[:, sl]
            gv = _gelu(vraw)
            r = lax.rsqrt(jnp.mean(gv * gv, axis=-1, keepdims=True) + EPS)
            vhat = gv * r
            vn = vhat * sgn_ref[:, sl]
            wm = jnp.where(mask, sgw_ref[gi], 0.0)
            s = _nn(wm, vn) + sgbt_ref[:, gi:gi + 1]
            uraw = u_ref[:, sl]
            dm = dmix_ref[:, DN_WIDTH + gi * SG_BLOCK:DN_WIDTH + (gi + 1) * SG_BLOCK]
            dp_ref[:, DN_WIDTH + gi * SG_BLOCK:DN_WIDTH + (gi + 1) * SG_BLOCK] = (dm * s * _dgelu(uraw)).astype(BF16)
            ds = dm * _gelu(uraw)
            gsgbt_ref[...] += jnp.where(lane == gi, jnp.sum(ds, axis=-1, keepdims=True), 0.0)
            gsgw_ref[gi] += jnp.where(mask, _nt(ds, vn), 0.0)
            dvn = _tn(wm, ds)
            gsgn_ref[:, sl] += jnp.sum(dvn * vhat, axis=0, keepdims=True)
            dhat = dvn * sgn_ref[:, sl]
            dgv = r * (dhat - vhat * jnp.mean(dhat * vhat, axis=-1, keepdims=True))
            dp_ref[:, 2 * DN_WIDTH + gi * SG_BLOCK:2 * DN_WIDTH + (gi + 1) * SG_BLOCK] = (dgv * _dgelu(vraw)).astype(BF16)

    def col(c):
        return pl.BlockSpec((rb, 512), lambda i: (i, c))
    full = lambda *s: pl.BlockSpec(s, lambda i: (0,) * len(s))
    in_specs, args = _with_dep(
        [pl.BlockSpec((rb, DN_WIDTH), lambda i: (i, 0)), col(3), col(4), col(5),
         full(1, 128), full(1, SG_WIDTH), full(SG_GROUPS, SG_BLOCK, SG_BLOCK), full(SG_BLOCK, 128),
         pl.BlockSpec((rb, D_MODEL), lambda i: (i, 0))],
        [o, p, p, p, ong, sgn, sgw, sgbt, dmix], dep)
    return pl.pallas_call(
        body, name="mix_bwd", grid=(T // rb,), in_specs=in_specs,
        out_specs=[pl.BlockSpec((rb, DN_WIDTH), lambda i: (i, 0)), pl.BlockSpec((rb, 3 * 512), lambda i: (i, 0)),
                   full(1, 128), full(1, SG_WIDTH), full(SG_GROUPS, SG_BLOCK, SG_BLOCK), full(SG_BLOCK, 128)],
        out_shape=[jax.ShapeDtypeStruct((T, DN_WIDTH), F32), jax.ShapeDtypeStruct((T, 3 * 512), BF16),
                   jax.ShapeDtypeStruct((1, 128), F32), jax.ShapeDtypeStruct((1, SG_WIDTH), F32),
                   jax.ShapeDtypeStruct((SG_GROUPS, SG_BLOCK, SG_BLOCK), F32),
                   jax.ShapeDtypeStruct((SG_BLOCK, 128), F32)],
        compiler_params=_cp("arbitrary"))(*args)


def _pad_lanes(row, offset=0):
    n = row.shape[1]
    return jnp.pad(row, ((0, 0), (offset, 128 - n - offset)))


def _local_step(x, tgt, w, dep=None, late_weights=None, on_grad=None):
    T = x.shape[0]
    N = T // CHUNK
    on_grad = on_grad or (lambda name, g: None)
    alog_row = _pad_lanes(w["dn_a_log"], N_HEADS)
    dtb_row = _pad_lanes(w["dn_dt_bias"], N_HEADS)
    sgbt = jnp.pad(w["sg_b"].T, ((0, 0), (0, 128 - SG_GROUPS)))

    p, h1, w_in_pad = _in_proj(x, w["attn_norm_g"], w["w_in"], dep=dep)
    q, k, v, bg = _dn_act(p, w["dn_conv_w"], alog_row, dtb_row)
    gc, gct, lmat = _dn_chunk(k, bg)
    lt = lmat.reshape(N * N_HEADS, CHUNK * CHUNK).T
    at = _tri_inv(lt)
    a = at.reshape(CHUNK * CHUNK, N * N_HEADS).T.reshape(N, N_HEADS, CHUNK, CHUNK)
    a_t = at.transpose(1, 0, 2).reshape(CHUNK * CHUNK, N * N_HEADS).T.reshape(N, N_HEADS, CHUNK, CHUNK)
    o, sall = _dn_scan(q, k, v, bg, gc, gct, a)
    mix = _mix_fwd(o, p, w["dn_out_norm_g"], w["sg_norm_g"], w["sg_w"], sgbt)
    if late_weights is not None:
        w = {**w, **late_weights("out_proj", mix)}
    x2, h2 = _out_proj(mix, w["w_out"], x, w["ffn_norm_g"])
    up, act = _up_proj_act(h2, w["w_up"], w["ffn_conv_w"], w["ffn_conv_b"])
    if late_weights is not None:
        w = {**w, **late_weights("down_proj", act)}
    loss, dx3, g_final = _down_proj_loss(act, w["w_down"], x2, tgt, w["final_norm_g"])

    dact = _mm_nt("d_act", dx3, w["w_down"], F32, 512, D_FF)
    g_w_down = _mm_tn("g_w_down", act, dx3, D_FF, 1024, 1024)
    tok = on_grad("w_down", g_w_down)
    dup, g_ffn_conv_w, g_ffn_conv_b = _ffn_act_bwd(up, dact, w["ffn_conv_w"], w["ffn_conv_b"], dep=tok)
    g_w_up = _mm_tn("g_w_up", h2, dup, 1024, 2 * D_FF // 4, 2048, col_major_tiles=True)
    tok = on_grad("w_up", g_w_up)
    dx2, g_ffn_norm = _mm_nt_rms_bwd("d_h2", dup, w["w_up"], x2, w["ffn_norm_g"], dx3, dep=tok)
    dmix = _mm_nt("d_mix", dx2, w["w_out"], F32, 512, 1024)
    g_w_out = _mm_tn("g_w_out", mix, dx2, 1024, 1024, 1024)
    tok = on_grad("w_out", g_w_out)
    do, dp_mid, g_ong, g_sgn, g_sgw, g_sgbt = _mix_bwd(o, p, w["dn_out_norm_g"], w["sg_norm_g"], w["sg_w"], sgbt,
                                                      dmix, dep=tok)
    early = dict(dn_out_norm_g=g_ong, sg_norm_g=g_sgn, sg_w=g_sgw, sg_bt=g_sgbt,
                 ffn_norm_g=g_ffn_norm, ffn_conv_w=g_ffn_conv_w, ffn_conv_b=g_ffn_conv_b, final_norm_g=g_final)
    tok = on_grad("small_early", early)
    dq, dk, dv, dbg = _dn_scan_bwd(q, k, v, bg, gc, gct, a, a_t, sall, do, dep=tok)
    dp, g_dn_conv_w, g_ad = _dn_act_bwd(p, w["dn_conv_w"], alog_row, dtb_row, dq, dk, dv, dbg, dp_mid)
    g_w_in = _mm_tn("g_w_in", h1, dp, 1024, PROJ_PAD, 1024, col_groups=(4, PROJ_COLS // 4))
    tok = on_grad("w_in", g_w_in)
    grad_x, g_attn_norm = _mm_nt_rms_bwd("d_h1", dp, w_in_pad, x, w["attn_norm_g"], dx2, dep=tok)

    grads = dict(attn_norm_g=g_attn_norm, w_in=g_w_in, dn_conv_w=g_dn_conv_w, a_dt=g_ad,
                 w_out=g_w_out, w_up=g_w_up, w_down=g_w_down, **early)
    return loss, grad_x, grads


def _me():
    return lax.axis_index("x"), lax.axis_index("y"), lax.axis_index("c")


def _peer(rel):
    x, y, c = _me()
    return {"x": (1 - x, y, c), "y": (x, 1 - y, c), "xy": (1 - x, 1 - y, c), "c": (x, y, 1 - c)}[rel]


def _chip_of(dev):
    return 2 * dev[0] + dev[1]


CHIP_RELS = ("x", "y", "xy")


def _run_copies(copies, sends, recvs):
    for cp in copies:
        cp.start()
    for cp in recvs:
        cp.wait_recv()
    for cp in sends:
        cp.wait_send()


def _gather_first(w_shard, small_shard):
    R = w_shard.shape[0]
    r2 = R // 2

    def body(w_ref, s_ref, w_out, s_out, send_sems, recv_sems):
        x, y, c = _me()
        me = _chip_of((x, y))
        sib = _peer("c")

        def half(chip, core):
            return w_out.at[chip, pl.ds(pl.multiple_of(core * r2, 8), r2), :]

        def copy(k, src, dst, to):
            return pltpu.make_async_remote_copy(src_ref=src, dst_ref=dst, send_sem=send_sems.at[k],
                                                recv_sem=recv_sems.at[k], device_id=to, device_id_type=MESH)

        own_rows = w_ref.at[pl.ds(pl.multiple_of(c * r2, 8), r2), :]
        first = [copy(r, own_rows, half(me, c), _peer(rel)) for r, rel in enumerate(CHIP_RELS)]
        first += [copy(3 + r, s_ref, s_out.at[me], _peer(rel)) for r, rel in enumerate(CHIP_RELS)]
        for cp in first:
            cp.start()
        passed = []
        for r, rel in enumerate(CHIP_RELS):
            their = _chip_of(_peer(rel))
            copy(r, own_rows, half(their, c), _peer(rel)).wait_recv()
            fwd = copy(6 + r, half(their, c), half(their, c), sib)
            fwd.start()
            passed.append(fwd)
        for r, rel in enumerate(CHIP_RELS):
            their = _chip_of(_peer(rel))
            copy(3 + r, s_ref, s_out.at[their], _peer(rel)).wait_recv()
            copy(6 + r, own_rows, half(their, 1 - c), sib).wait_recv()
        for cp in first + passed:
            cp.wait_send()

    w_all, s_all = pl.pallas_call(
        body, name="gather_first", in_specs=[ANY, ANY], out_specs=[ANY, ANY],
        out_shape=[jax.ShapeDtypeStruct((4,) + w_shard.shape, w_shard.dtype),
                   jax.ShapeDtypeStruct((4,) + small_shard.shape, small_shard.dtype)],
        scratch_shapes=[pltpu.SemaphoreType.DMA((9,)), pltpu.SemaphoreType.DMA((9,))])(w_shard, small_shard)
    me = _chip_of(_me())
    return (lax.dynamic_update_index_in_dim(w_all, w_shard, me, 0),
            lax.dynamic_update_index_in_dim(s_all, small_shard, me, 0))


OTHERS = tuple((fx, fy, fc) for fx in (0, 1) for fy in (0, 1) for fc in (0, 1) if (fx, fy, fc) != (0, 0, 0))


def _other(flip):
    x, y, c = _me()
    return (x ^ flip[0], y ^ flip[1], c ^ flip[2])


def _linear(dev):
    return 4 * dev[0] + 2 * dev[1] + dev[2]


def _exchange_small(small):
    def body(small_ref, out_ref, send_sems, recv_sems):
        my_slot = _linear(_me())
        sends, recvs = [], []
        for k, flip in enumerate(OTHERS):
            peer = _other(flip)
            sends.append(pltpu.make_async_remote_copy(
                src_ref=small_ref, dst_ref=out_ref.at[my_slot], send_sem=send_sems.at[k], recv_sem=recv_sems.at[k],
                device_id=peer, device_id_type=MESH))
            recvs.append(pltpu.make_async_remote_copy(
                src_ref=small_ref, dst_ref=out_ref.at[_linear(peer)], send_sem=send_sems.at[k],
                recv_sem=recv_sems.at[k], device_id=peer, device_id_type=MESH))
        _run_copies(sends, sends, recvs)

    out = pl.pallas_call(
        body, name="exchange_small", in_specs=[ANY], out_specs=ANY,
        out_shape=jax.ShapeDtypeStruct((8,) + small.shape, small.dtype),
        scratch_shapes=[pltpu.SemaphoreType.DMA((7,)), pltpu.SemaphoreType.DMA((7,))])(small)
    return lax.dynamic_update_index_in_dim(out, small, _linear(_me()), 0)


def _pair_swap(name, halves):
    n = len(halves)

    def body(*refs):
        src, out = refs[:n], refs[n:2 * n]
        send_sems, recv_sems = refs[2 * n:]
        sib = _peer("c")
        copies = [pltpu.make_async_remote_copy(
            src_ref=src[i], dst_ref=out[i], send_sem=send_sems.at[i], recv_sem=recv_sems.at[i],
            device_id=sib, device_id_type=MESH) for i in range(n)]
        _run_copies(copies, copies, copies)

    return pl.pallas_call(
        body, name=name, in_specs=[ANY] * n, out_specs=[ANY] * n,
        out_shape=[jax.ShapeDtypeStruct(h.shape, h.dtype) for h in halves],
        scratch_shapes=[pltpu.SemaphoreType.DMA((n,)), pltpu.SemaphoreType.DMA((n,))])(*halves)


HBM = pl.BlockSpec(memory_space=pltpu.HBM)
SEM = pl.BlockSpec(memory_space=pltpu.SEMAPHORE)
EFFECT = pltpu.SideEffectType.DATAFLOW_SIDE_EFFECTING


def _hbm(a):
    return pltpu.with_memory_space_constraint(a, pltpu.HBM)


def _transfer_start(name, srcs, lands, n_copies, make_copies, after=None):
    n, m = len(srcs), len(lands)

    def body(*refs):
        src, land = refs[:n], refs[n:n + m]
        outs = refs[n + m + (after is not None):]
        send_sems, recv_sems, token = outs[0], outs[1], outs[-1]
        for cp in make_copies(src, land, send_sems, recv_sems):
            cp.start()
        token[...] = jnp.zeros_like(token)

    arrs = list(srcs) + list(lands)
    in_specs, args = _with_dep([HBM] * (n + m), [_hbm(a) for a in arrs], after)
    out = pl.pallas_call(
        body, name=name,
        out_shape=(pltpu.SemaphoreType.DMA((n_copies,)), pltpu.SemaphoreType.DMA((n_copies,)),
                   *[pltpu.HBM(a.shape, a.dtype) for a in arrs], jax.ShapeDtypeStruct((8, 128), F32)),
        in_specs=in_specs,
        out_specs=(SEM, SEM, *[HBM] * (n + m), pl.BlockSpec(memory_space=pltpu.VMEM)),
        input_output_aliases={i: 2 + i for i in range(n + m)},
        compiler_params=pltpu.CompilerParams(has_side_effects=EFFECT))(*args)
    return out[0], out[1], list(out[2:2 + n]), list(out[2 + n:2 + n + m]), out[-1]


def _transfer_wait(name, send_sems, recv_sems, srcs, lands, make_copies, after):
    n, m = len(srcs), len(lands)

    def body(*refs):
        src, land = refs[:n], refs[n:n + m]
        s_sems, r_sems = refs[n + m], refs[n + m + 1]
        for cp in make_copies(src, land, s_sems, r_sems):
            cp.wait_send()
            cp.wait_recv()

    arrs = list(srcs) + list(lands)
    out = pl.pallas_call(
        body, name=name, out_shape=tuple(pltpu.HBM(a.shape, a.dtype) for a in arrs),
        in_specs=[HBM] * (n + m) + [SEM, SEM, ANY], out_specs=tuple([HBM] * (n + m)),
        input_output_aliases={i: i for i in range(n + m)},
        compiler_params=pltpu.CompilerParams(has_side_effects=EFFECT))(*arrs, send_sems, recv_sems, after)
    return list(out[:n]), list(out[n:])


def _gather_copies(src, land, send_sems, recv_sems):
    me = _chip_of(_me())
    copies = []
    for i in range(len(src)):
        for r, rel in enumerate(CHIP_RELS):
            k = 3 * i + r
            copies.append(pltpu.make_async_remote_copy(
                src_ref=src[i], dst_ref=land[i].at[me], send_sem=send_sems.at[k], recv_sem=recv_sems.at[k],
                device_id=_peer(rel), device_id_type=MESH))
    return copies


def _small_copies(src, land, send_sems, recv_sems):
    my_slot = _linear(_me())
    return [pltpu.make_async_remote_copy(
        src_ref=src[0], dst_ref=land[0].at[my_slot], send_sem=send_sems.at[k], recv_sem=recv_sems.at[k],
        device_id=_other(flip), device_id_type=MESH) for k, flip in enumerate(OTHERS)]


def _pieces_copies(src, land, send_sems, recv_sems):
    copies = []
    for k, flip in enumerate(OTHERS):
        peer = _other(flip)
        copies.append(pltpu.make_async_remote_copy(
            src_ref=src[0].at[_linear(peer)], dst_ref=land[0].at[k], send_sem=send_sems.at[k],
            recv_sem=recv_sems.at[k], device_id=peer, device_id_type=MESH))
    return copies


def _row_block(rows, cols, budget=2 * 1024 * 1024):
    rb = max(8, (budget // (4 * cols)) // 8 * 8)
    while rows % rb:
        rb -= 8
    return rb if rb > 0 else rows


def _sum_pieces(name, mine, slot, rest):
    _, R, Cc = mine.shape
    K = rest.shape[0]
    rb = _row_block(R, Cc)

    def body(s_ref, f_ref, r_ref, o_ref):
        acc = f_ref[0].astype(F32)
        for j in range(K):
            acc = acc + r_ref[j].astype(F32)
        o_ref[...] = acc

    return pl.pallas_call(
        body, name=name,
        grid_spec=pltpu.PrefetchScalarGridSpec(
            num_scalar_prefetch=1, grid=(R // rb,),
            in_specs=[pl.BlockSpec((1, rb, Cc), lambda i, s: (s[0], i, 0)),
                      pl.BlockSpec((K, rb, Cc), lambda i, s: (0, i, 0))],
            out_specs=pl.BlockSpec((rb, Cc), lambda i, s: (i, 0))),
        out_shape=jax.ShapeDtypeStruct((R, Cc), F32), compiler_params=_cp("parallel"))(slot, mine, rest)


def _adamw_math(w, gv, m, v):
    mn = ADAM_B1 * m + (1.0 - ADAM_B1) * gv
    vn = ADAM_B2 * v + (1.0 - ADAM_B2) * (gv * gv)
    m_hat = mn / (1.0 - ADAM_B1 ** ADAM_STEP)
    v_hat = vn / (1.0 - ADAM_B2 ** ADAM_STEP)
    return -ADAM_LR * (m_hat / (jnp.sqrt(v_hat) + ADAM_EPS) + ADAM_WD * w), mn, vn


def _adamw_halves(name, w, mine, theirs, m, v, core):
    R, Cc = w.shape
    r2 = R // 2
    rb = _row_block(r2, Cc, 1024 * 1024)
    nb2 = r2 // rb

    def body(c_ref, w_ref, mine_ref, theirs_ref, m_ref, v_ref, g_ref, d_ref, mo_ref, vo_ref):
        is_mine = (pl.program_id(0) // nb2) == c_ref[0]
        gv = jnp.where(is_mine, mine_ref[...], theirs_ref[...])
        g_ref[...] = gv
        d_ref[...], mo_ref[...], vo_ref[...] = _adamw_math(w_ref[...], gv, m_ref[...], v_ref[...])

    blk = pl.BlockSpec((rb, Cc), lambda i, c: (i, 0))
    half = lambda own: pl.BlockSpec(
        (rb, Cc), lambda i, c: (jnp.clip(i - (c[0] if own else 1 - c[0]) * nb2, 0, nb2 - 1), 0))
    return pl.pallas_call(
        body, name=name,
        grid_spec=pltpu.PrefetchScalarGridSpec(
            num_scalar_prefetch=1, grid=(2 * nb2,), in_specs=[blk, half(True), half(False), blk, blk],
            out_specs=[blk] * 4),
        out_shape=[jax.ShapeDtypeStruct((R, Cc), F32)] * 4, compiler_params=_cp("parallel"))(core, w, mine, theirs, m, v)


def _adamw_transposed(name, wt, mine, theirs, mt, vt, core):
    Cc, kh_n, _ = wt.shape
    r2 = mine.shape[0]
    per_half = kh_n // 2
    nb = -(-Cc // LANES)

    def body(c_ref, w_ref, mine_ref, theirs_ref, m_ref, v_ref, g_ref, d_ref, mo_ref, vo_ref):
        first = c_ref[0] == 0
        halves = (jnp.where(first, mine_ref[...], theirs_ref[...]).T,
                  jnp.where(first, theirs_ref[...], mine_ref[...]).T)
        for kh in range(kh_n):
            lo = (kh % per_half) * LANES
            g_ref[:, kh, :] = halves[kh // per_half][:, lo:lo + LANES]
        d_ref[...], mo_ref[...], vo_ref[...] = _adamw_math(w_ref[...], g_ref[...], m_ref[...], v_ref[...])

    blk = pl.BlockSpec((LANES, kh_n, LANES), lambda i, c: (i, 0, 0))
    half = pl.BlockSpec((r2, LANES), lambda i, c: (0, i))
    return pl.pallas_call(
        body, name=name,
        grid_spec=pltpu.PrefetchScalarGridSpec(
            num_scalar_prefetch=1, grid=(nb,), in_specs=[blk, half, half, blk, blk], out_specs=[blk] * 4),
        out_shape=[jax.ShapeDtypeStruct(wt.shape, F32)] * 4, compiler_params=_cp("parallel"))(
            core, wt, mine, theirs, mt, vt)


FF_W = 2 * D_FF
FF_CH = FF_W // LANES
DNC_W = 3 * DN_WIDTH
DNC_CH = DNC_W // LANES
E_ONG, E_SGN, E_SGW, E_SGBT = 0, 1, 8, 8 + SG_GROUPS * SG_BLOCK
E_FFN = E_SGBT + SG_BLOCK
E_FCW = E_FFN + D_MODEL // LANES
E_FCB = E_FCW + 3 * FF_CH
E_FIN = E_FCB + FF_CH
EARLY_ROWS = E_FIN + D_MODEL // LANES
L_ATTN, L_DNC = 0, D_MODEL // LANES
L_AD = L_DNC + 4 * DNC_CH
L_LOSS = L_AD + 2
LATE_ROWS = -(-(L_LOSS + 1) // 8) * 8


def _put_rows(out, r0, x):
    k, width = x.shape
    n = width // LANES
    for t in range(k):
        for j in range(n):
            out[r0 + t * n + j:r0 + t * n + j + 1, :] = x[t:t + 1, j * LANES:(j + 1) * LANES]


def _pack_early(ong, sgn, sgw, sgbt, ffn, fcw, fcb, fin):
    def body(ong_ref, sgn_ref, sgw_ref, sgbt_ref, ffn_ref, fcw_ref, fcb_ref, fin_ref, out):
        out[...] = jnp.zeros_like(out)
        _put_rows(out, E_ONG, ong_ref)
        _put_rows(out, E_SGN, sgn_ref)
        for gi in range(SG_GROUPS):
            out[E_SGW + gi * SG_BLOCK:E_SGW + (gi + 1) * SG_BLOCK, :] = sgw_ref[gi]
        out[E_SGBT:E_SGBT + SG_BLOCK, :] = sgbt_ref[...]
        _put_rows(out, E_FFN, ffn_ref)
        _put_rows(out, E_FCW, fcw_ref)
        _put_rows(out, E_FCB, fcb_ref)
        _put_rows(out, E_FIN, fin_ref)

    return pl.pallas_call(body, name="pack_small_early", out_shape=jax.ShapeDtypeStruct((EARLY_ROWS, LANES), F32))(
        ong, sgn, sgw, sgbt, ffn, fcw, fcb, fin)


def _pack_late(attn, dnc, ad, loss_row):
    def body(attn_ref, dnc_ref, ad_ref, loss_ref, out):
        out[...] = jnp.zeros_like(out)
        _put_rows(out, L_ATTN, attn_ref)
        _put_rows(out, L_DNC, dnc_ref)
        out[L_AD:L_AD + 2, :] = ad_ref[...]
        out[L_LOSS:L_LOSS + 1, :] = loss_ref[...]

    return pl.pallas_call(body, name="pack_small_late", out_shape=jax.ShapeDtypeStruct((LATE_ROWS, LANES), F32))(
        attn, dnc, ad, loss_row)


SMALL = ("attn_norm_g", "dn_a_log", "dn_dt_bias", "dn_out_norm_g", "sg_norm_g", "sg_w", "sg_b", "ffn_norm_g",
         "ffn_conv_b", "final_norm_g", "dn_conv_w", "ffn_conv_w")


def _small_update(early_all, late_all, chip, W, M, V):
    n = len(SMALL)
    arrs = [d[k] for d in (W, M, V) for k in SMALL]

    def body(c_ref, e_ref, l_ref, *refs):
        w_, m_, v_ = refs[:n], refs[n:2 * n], refs[2 * n:3 * n]
        loss_ref = refs[3 * n]
        outs = refs[3 * n + 1:]
        g_, d_, mo_, vo_ = outs[:n], outs[n:2 * n], outs[2 * n:3 * n], outs[3 * n:4 * n]
        chip_i = c_ref[0]

        def total(ref, r0, rows=1):
            acc = ref[0, pl.ds(r0, rows), :]
            for s in range(1, 8):
                acc = acc + ref[s, pl.ds(r0, rows), :]
            return acc

        def update(i, idx, g):
            g_[i][idx] = g
            d_[i][idx], mo_[i][idx], vo_[i][idx] = _adamw_math(w_[i][idx], g, m_[i][idx], v_[i][idx])

        def rows_param(name, ref, r0, width):
            i = SMALL.index(name)
            for j in range(width // LANES):
                update(i, (slice(None), slice(j * LANES, (j + 1) * LANES)), total(ref, r0 + j))

        rows_param("attn_norm_g", l_ref, L_ATTN, D_MODEL)
        ad = (total(l_ref, L_AD), total(l_ref, L_AD + 1))
        update(SMALL.index("dn_a_log"), (slice(None), slice(None)), ad[0][:, N_HEADS:2 * N_HEADS])
        update(SMALL.index("dn_dt_bias"), (slice(None), slice(None)), ad[1][:, N_HEADS:2 * N_HEADS])
        rows_param("dn_out_norm_g", e_ref, E_ONG, HEAD_DIM)
        rows_param("sg_norm_g", e_ref, E_SGN, SG_WIDTH)
        sgbt = total(e_ref, E_SGBT, SG_BLOCK).T
        for gi in range(SG_GROUPS):
            update(SMALL.index("sg_w"), (0, gi), total(e_ref, E_SGW + gi * SG_BLOCK, SG_BLOCK))
            update(SMALL.index("sg_b"), (0, slice(gi, gi + 1), slice(None)), sgbt[gi:gi + 1, :])
        rows_param("ffn_norm_g", e_ref, E_FFN, D_MODEL)
        rows_param("ffn_conv_b", e_ref, E_FCB, FF_W)
        rows_param("final_norm_g", e_ref, E_FIN, D_MODEL)
        for name, ref, r0, taps, chunks in (("dn_conv_w", l_ref, L_DNC, 4, DNC_CH), ("ffn_conv_w", e_ref, E_FCW, 3, FF_CH)):
            mine = chunks // 4
            for t in range(taps):
                for j in range(mine):
                    update(SMALL.index(name), (0, slice(t, t + 1), slice(j * LANES, (j + 1) * LANES)),
                           total(ref, r0 + t * chunks + chip_i * mine + j))
        loss_ref[...] = total(l_ref, L_LOSS)

    full = lambda a: pl.BlockSpec(a.shape, lambda i, c, nd=a.ndim: (0,) * nd)
    shapes = [jax.ShapeDtypeStruct(W[k].shape, F32) for k in SMALL]
    outs = pl.pallas_call(
        body, name="small_update",
        grid_spec=pltpu.PrefetchScalarGridSpec(
            num_scalar_prefetch=1, grid=(1,), in_specs=[full(early_all), full(late_all)] + [full(a) for a in arrs],
            out_specs=[pl.BlockSpec((1, LANES), lambda i, c: (0, 0))] + [full(s) for s in shapes] * 4),
        out_shape=[jax.ShapeDtypeStruct((1, LANES), F32)] + shapes * 4,
        compiler_params=pltpu.CompilerParams(vmem_limit_bytes=VMEM_LIMIT))(chip, early_all, late_all, *arrs)
    loss, outs = outs[0], outs[1:]
    return (loss,) + tuple(dict(zip(SMALL, outs[k * n:(k + 1) * n])) for k in range(4))


ORDER =("attn_norm_g", "w_in", "dn_conv_w", "dn_a_log", "dn_dt_bias", "dn_out_norm_g", "sg_norm_g", "sg_w",
         "sg_b", "w_out", "ffn_norm_g", "w_up", "ffn_conv_w", "ffn_conv_b", "w_down", "final_norm_g")


def kernel(x, attn_norm_g, w_in, dn_conv_w, dn_a_log, dn_dt_bias, dn_out_norm_g, sg_norm_g, sg_w, sg_b, w_out, ffn_norm_g, w_up, ffn_conv_w, ffn_conv_b, w_down, final_norm_g, loss_target, m_attn_norm_g, m_w_in, m_dn_conv_w, m_dn_a_log, m_dn_dt_bias, m_dn_out_norm_g, m_sg_norm_g, m_sg_w, m_sg_b, m_w_out, m_ffn_norm_g, m_w_up, m_ffn_conv_w, m_ffn_conv_b, m_w_down, m_final_norm_g, v_attn_norm_g, v_w_in, v_dn_conv_w, v_dn_a_log, v_dn_dt_bias, v_dn_out_norm_g, v_sg_norm_g, v_sg_w, v_sg_b, v_w_out, v_ffn_norm_g, v_w_up, v_ffn_conv_w, v_ffn_conv_b, v_w_down, v_final_norm_g):
    W = dict(attn_norm_g=attn_norm_g, w_in=w_in, dn_conv_w=dn_conv_w, dn_a_log=dn_a_log, dn_dt_bias=dn_dt_bias,
             dn_out_norm_g=dn_out_norm_g, sg_norm_g=sg_norm_g, sg_w=sg_w, sg_b=sg_b, w_out=w_out,
             ffn_norm_g=ffn_norm_g, w_up=w_up, ffn_conv_w=ffn_conv_w, ffn_conv_b=ffn_conv_b, w_down=w_down,
             final_norm_g=final_norm_g)
    Mo = dict(attn_norm_g=m_attn_norm_g, w_in=m_w_in, dn_conv_w=m_dn_conv_w, dn_a_log=m_dn_a_log,
              dn_dt_bias=m_dn_dt_bias, dn_out_norm_g=m_dn_out_norm_g, sg_norm_g=m_sg_norm_g, sg_w=m_sg_w,
              sg_b=m_sg_b, w_out=m_w_out, ffn_norm_g=m_ffn_norm_g, w_up=m_w_up, ffn_conv_w=m_ffn_conv_w,
              ffn_conv_b=m_ffn_conv_b, w_down=m_w_down, final_norm_g=m_final_norm_g)
    Vo = dict(attn_norm_g=v_attn_norm_g, w_in=v_w_in, dn_conv_w=v_dn_conv_w, dn_a_log=v_dn_a_log,
              dn_dt_bias=v_dn_dt_bias, dn_out_norm_g=v_dn_out_norm_g, sg_norm_g=v_sg_norm_g, sg_w=v_sg_w,
              sg_b=v_sg_b, w_out=v_w_out, ffn_norm_g=v_ffn_norm_g, w_up=v_w_up, ffn_conv_w=v_ffn_conv_w,
              ffn_conv_b=v_ffn_conv_b, w_down=v_w_down, final_norm_g=v_final_norm_g)
    xi, yi, ci = lax.axis_index("x"), lax.axis_index("y"), lax.axis_index("c")
    chip = 2 * xi + yi

    me_lin = 4 * xi + 2 * yi + ci

    g_in, g_dnc = _gather_first(w_in[0].astype(BF16), dn_conv_w[0])
    def start_gather(name, shards, after):
        lands = [lax.dynamic_update_index_in_dim(lax.empty((4,) + s.shape, s.dtype), s, chip, 0) for s in shards]
        return _transfer_start(name, shards, lands, 3 * len(shards), _gather_copies, after=after)

    mid = start_gather("gather_mid_start", [w_out[0].astype(BF16), w_up[0].astype(BF16), ffn_conv_w[0]], g_in)
    last = start_gather("gather_last_start", [w_down[0].astype(BF16)], mid[4])
    token = last[4]

    def late_weights(stage, after):
        if stage == "out_proj":
            _, (g_out, g_up, g_ffc) = _transfer_wait("gather_mid_wait", *mid[:4], _gather_copies, after)
            return dict(w_out=g_out.reshape(D_MODEL, D_MODEL), ffn_conv_w=g_ffc.transpose(1, 0, 2).reshape(3, 2 * D_FF),
                        w_up=g_up.transpose(1, 0, 2).reshape(D_MODEL, 2 * D_FF))
        _, (g_down,) = _transfer_wait("gather_last_wait", *last[:4], _gather_copies, after)
        return dict(w_down=g_down.reshape(D_FF, D_MODEL))

    full = dict(
        w_in=g_in,
        dn_conv_w=g_dnc.transpose(1, 0, 2).reshape(4, 3 * DN_WIDTH),
        attn_norm_g=attn_norm_g, dn_a_log=dn_a_log, dn_dt_bias=dn_dt_bias, dn_out_norm_g=dn_out_norm_g,
        sg_norm_g=sg_norm_g, sg_w=sg_w[0], sg_b=sg_b[0], ffn_norm_g=ffn_norm_g, ffn_conv_b=ffn_conv_b,
        final_norm_g=final_norm_g[None])

    pending = {}

    def on_grad(name, gw):
        if name == "small_early":
            buf = _pack_early(gw["dn_out_norm_g"], gw["sg_norm_g"], gw["sg_w"], gw["sg_bt"], gw["ffn_norm_g"],
                              gw["ffn_conv_w"], gw["ffn_conv_b"], gw["final_norm_g"])
            land = lax.dynamic_update_index_in_dim(lax.empty((8,) + buf.shape, F32), buf, me_lin, 0)
            s_sem, r_sem, src, lands, tok = _transfer_start("small_early_start", [buf], [land], 7, _small_copies)
            pending[name] = (s_sem, r_sem, src, lands)
            return tok
        g8 = gw.reshape(8, -1, gw.shape[-1])
        land = lax.empty((7,) + g8.shape[1:], BF16)
        s_sem, r_sem, src, lands, tok = _transfer_start(f"reduce_{name}_start", [g8], [land], 7, _pieces_copies)
        pending[name] = (s_sem, r_sem, src, lands)
        return tok

    loss_row, grad_x, g = _local_step(x[0], loss_target[0], full, dep=token, late_weights=late_weights,
                                      on_grad=on_grad)

    late_all = _exchange_small(_pack_late(g["attn_norm_g"], g["dn_conv_w"], g["a_dt"], loss_row))
    s_sem, r_sem, src, lands = pending["small_early"]
    _, (early_all,) = _transfer_wait("small_early_wait", s_sem, r_sem, src, lands, _small_copies, grad_x)
    row = lambda d: {k: (d[k].reshape(1, -1) if k == "final_norm_g" else d[k]) for k in SMALL}
    loss_sum, *small_out = _small_update(early_all, late_all, chip.astype(jnp.int32).reshape(1), row(W), row(Mo), row(Vo))
    loss = loss_sum[0, 0]

    def summed_half(n, after):
        s_sem, r_sem, src, lands = pending[n]
        sent, got = _transfer_wait(f"reduce_{n}_wait", s_sem, r_sem, src, lands, _pieces_copies, after)
        return _sum_pieces(f"sum_{n}", sent[0], me_lin.astype(jnp.int32).reshape(1), got[0])

    first3 = ("w_down", "w_up", "w_out")
    halves = [summed_half(n, grad_x) for n in first3]
    theirs = _pair_swap("pair_swap", halves)
    core = ci.astype(jnp.int32).reshape(1)
    grads, delta, new_m, new_v = {}, {}, {}, {}
    for n, mine_h, their_h in zip(first3, halves, theirs):
        shp = W[n].shape
        gr, d, mn, vn = _adamw_halves(f"adamw_{n}", W[n][0], mine_h, their_h, Mo[n][0], Vo[n][0], core)
        grads[n], delta[n], new_m[n], new_v[n] = gr.reshape(shp), d.reshape(shp), mn.reshape(shp), vn.reshape(shp)
    mine_h = summed_half("w_in", delta["w_out"])
    (their_h,) = _pair_swap("pair_swap_w_in", [mine_h])
    shp = w_in.shape
    to_t = lambda a: a.reshape(shp[1] // LANES, LANES, shp[2]).transpose(2, 0, 1)
    from_t = lambda a: a.transpose(1, 2, 0).reshape(shp)
    outs = _adamw_transposed("adamw_w_in", to_t(w_in), mine_h, their_h, to_t(m_w_in), to_t(v_w_in), core)
    grads["w_in"], delta["w_in"], new_m["w_in"], new_v["w_in"] = (from_t(o) for o in outs)
    for dst, src_d in zip((grads, delta, new_m, new_v), small_out):
        dst.update({k: (a.reshape(W[k].shape) if k == "final_norm_g" else a) for k, a in src_d.items()})

    return (loss, grad_x[None], *[grads[n] for n in ORDER], *[delta[n] for n in ORDER],
            *[new_m[n] for n in ORDER], *[new_v[n] for n in ORDER])
```

```python
import functools
import math

import jax
import jax.numpy as jnp
from jax import lax
from jax.experimental import pallas as pl
from jax.experimental.pallas import tpu as pltpu

F32 = jnp.float32
BF16 = jnp.bfloat16

D_MODEL = 1024
CHUNK = 64
SCAN_CHUNKS = 4
SCAN_CHUNKS_FWD = 8
HEAD_DIM = 128
N_HEADS = 4
DN_WIDTH = 512
SG_WIDTH = 512
SG_GROUPS = 4
SG_BLOCK = 128
D_FF = 2816
PROJ_COLS = 3080
PROJ_PAD = 3200
BA_COL = 3072
EPS = 1e-6
NEG = -1e30
VMEM_LIMIT = 56 * 1024 * 1024

ADAM_LR = 0.001
ADAM_B1 = 0.9
ADAM_B2 = 0.999
ADAM_EPS = 1e-08
ADAM_WD = 0.01
ADAM_STEP = 10

MESH = pl.DeviceIdType.MESH
ANY = pl.BlockSpec(memory_space=pl.ANY)


def _cp(*sem):
    return pltpu.CompilerParams(dimension_semantics=sem, vmem_limit_bytes=VMEM_LIMIT)


def _bf(a):
    return a.astype(BF16)


def _nn(a, b):
    return jnp.dot(_bf(a), _bf(b), preferred_element_type=F32)


def _nt(a, b):
    return lax.dot_general(_bf(a), _bf(b), (((1,), (1,)), ((), ())), preferred_element_type=F32)


def _tn(a, b):
    return lax.dot_general(_bf(a), _bf(b), (((0,), (0,)), ((), ())), preferred_element_type=F32)


def _split(a):
    hi = _bf(a)
    return hi, _bf(a - hi.astype(F32))


def _sigmoid(x):
    return 0.5 * jnp.tanh(0.5 * x) + 0.5


def _silu(x):
    return x * _sigmoid(x)


def _dsilu(x):
    s = _sigmoid(x)
    return s * (1.0 + x * (1.0 - s))


_GELU_C = math.sqrt(2.0 / math.pi)
_GELU_A = 0.044715


def _gelu(x):
    return 0.5 * x * (1.0 + jnp.tanh(_GELU_C * (x + _GELU_A * x * x * x)))


def _dgelu(x):
    t = jnp.tanh(_GELU_C * (x + _GELU_A * x * x * x))
    return 0.5 * (1.0 + t) + 0.5 * x * (1.0 - t * t) * _GELU_C * (1.0 + 3.0 * _GELU_A * x * x)


def _softplus(x):
    return jnp.maximum(x, 0.0) + jnp.log(1.0 + jnp.exp(-jnp.abs(x)))


def _with_dep(in_specs, args, dep):
    if dep is None:
        return in_specs, args
    return in_specs + [ANY], args + [dep]


SUB_ROWS = 128


def _sub_blocks(tm):
    return [slice(r0, min(r0 + SUB_ROWS, tm)) for r0 in range(0, tm, SUB_ROWS)]


def _rms_hat(xv):
    r = lax.rsqrt(jnp.mean(xv * xv, axis=-1, keepdims=True) + EPS)
    return xv * r, r


def _rms_bwd_vals(dh, xh, r, g):
    dxh = dh * g
    return r * (dxh - xh * jnp.mean(dxh * xh, axis=-1, keepdims=True)), jnp.sum(dh * xh, axis=0, keepdims=True)


def _in_proj(x, g, w4, tm=512, dep=None):
    T, K = x.shape
    ng, _, wc = w4.shape
    tm = min(tm, T)

    def body(x_ref, g_ref, w4_ref, *rest):
        p_ref, h_ref, w_ref = rest[-3:]

        @pl.when(pl.program_id(0) == 0)
        def _():
            w_ref[:, ng * wc:] = jnp.zeros((K, PROJ_PAD - ng * wc), BF16)
            for j in range(ng):
                w_ref[:, j * wc:(j + 1) * wc] = w4_ref[j]
        for r in _sub_blocks(tm):
            xh, _ = _rms_hat(x_ref[r, :])
            h_ref[r, :] = (xh * g_ref[...]).astype(BF16)
        p_ref[...] = jnp.dot(h_ref[...], w_ref[...], preferred_element_type=F32)

    in_specs, args = _with_dep(
        [pl.BlockSpec((tm, K), lambda i: (i, 0)), pl.BlockSpec((1, K), lambda i: (0, 0)),
         pl.BlockSpec((ng, K, wc), lambda i: (0, 0, 0))], [x, g, w4], dep)
    return pl.pallas_call(
        body, name="in_proj", grid=(T // tm,), in_specs=in_specs,
        out_specs=[pl.BlockSpec((tm, PROJ_PAD), lambda i: (i, 0)), pl.BlockSpec((tm, K), lambda i: (i, 0)),
                   pl.BlockSpec((K, PROJ_PAD), lambda i: (0, 0))],
        out_shape=[jax.ShapeDtypeStruct((T, PROJ_PAD), F32), jax.ShapeDtypeStruct((T, K), BF16),
                   jax.ShapeDtypeStruct((K, PROJ_PAD), BF16)],
        compiler_params=_cp("arbitrary"))(*args)


def _out_proj(mix, w, x, g, tm=512):
    T, K = mix.shape
    Dm = w.shape[1]
    tm = min(tm, T)

    def body(a_ref, w_ref, x_ref, g_ref, x2_ref, h_ref):
        x2_ref[...] = _nn(a_ref[...], w_ref[...]) + x_ref[...]
        for r in _sub_blocks(tm):
            xh, _ = _rms_hat(x2_ref[r, :])
            h_ref[r, :] = (xh * g_ref[...]).astype(BF16)

    row = lambda width: pl.BlockSpec((tm, width), lambda i: (i, 0))
    return pl.pallas_call(
        body, name="out_proj", grid=(T // tm,),
        in_specs=[row(K), pl.BlockSpec((K, Dm), lambda i: (0, 0)), row(Dm), pl.BlockSpec((1, Dm), lambda i: (0, 0))],
        out_specs=[row(Dm), row(Dm)],
        out_shape=[jax.ShapeDtypeStruct((T, Dm), F32), jax.ShapeDtypeStruct((T, Dm), BF16)],
        compiler_params=_cp("parallel"))(mix, w, x, g)


def _down_proj_loss(act, w, x2, tgt, g, tm=512):
    T, K = act.shape
    Dm = w.shape[1]
    tm = min(tm, T)

    def body(a_ref, w_ref, x_ref, t_ref, g_ref, loss_ref, dx_ref, gg_ref):
        @pl.when(pl.program_id(0) == 0)
        def _():
            gg_ref[...] = jnp.zeros_like(gg_ref)
            loss_ref[...] = jnp.zeros_like(loss_ref)
        dx_ref[...] = _nn(a_ref[...], w_ref[...]) + x_ref[...]
        for r in _sub_blocks(tm):
            xh, rr = _rms_hat(dx_ref[r, :])
            e = xh * g_ref[...] - t_ref[r, :]
            loss_ref[...] += jnp.zeros_like(loss_ref) + (0.5 / Dm) * jnp.sum(e * e)
            dx, gg = _rms_bwd_vals(e * (1.0 / Dm), xh, rr, g_ref[...])
            dx_ref[r, :] = dx
            gg_ref[...] += gg

    row = lambda width: pl.BlockSpec((tm, width), lambda i: (i, 0))
    vec = pl.BlockSpec((1, Dm), lambda i: (0, 0))
    return pl.pallas_call(
        body, name="down_proj_loss", grid=(T // tm,),
        in_specs=[row(K), pl.BlockSpec((K, Dm), lambda i: (0, 0)), row(Dm), row(Dm), vec],
        out_specs=[pl.BlockSpec((1, 128), lambda i: (0, 0)), row(Dm), vec],
        out_shape=[jax.ShapeDtypeStruct((1, 128), F32), jax.ShapeDtypeStruct((T, Dm), F32),
                   jax.ShapeDtypeStruct((1, Dm), F32)],
        compiler_params=_cp("arbitrary"))(act, w, x2, tgt, g)


def _mm_nt_rms_bwd(name, a, b, x, g, dres, tm=512, dep=None):
    M, K = a.shape
    Dm = b.shape[0]
    tm = min(tm, M)

    def body(a_ref, b_ref, x_ref, g_ref, dres_ref, *rest):
        dx_ref, gg_ref = rest[-2:]

        @pl.when(pl.program_id(0) == 0)
        def _():
            gg_ref[...] = jnp.zeros_like(gg_ref)
        dx_ref[...] = _nt(a_ref[...], b_ref[...])
        for r in _sub_blocks(tm):
            xh, rr = _rms_hat(x_ref[r, :])
            dx, gg = _rms_bwd_vals(dx_ref[r, :], xh, rr, g_ref[...])
            dx_ref[r, :] = dres_ref[r, :] + dx
            gg_ref[...] += gg

    row = lambda width: pl.BlockSpec((tm, width), lambda i: (i, 0))
    vec = pl.BlockSpec((1, Dm), lambda i: (0, 0))
    in_specs, args = _with_dep([row(K), pl.BlockSpec((Dm, K), lambda i: (0, 0)), row(Dm), vec, row(Dm)],
                               [a, b, x, g, dres], dep)
    return pl.pallas_call(
        body, name=name, grid=(M // tm,), in_specs=in_specs, out_specs=[row(Dm), vec],
        out_shape=[jax.ShapeDtypeStruct((M, Dm), F32), jax.ShapeDtypeStruct((1, Dm), F32)],
        compiler_params=_cp("arbitrary"))(*args)


def _mm_nt(name, a, b, out_dtype, tm, tn, dep=None):
    M, K = a.shape
    N = b.shape[0]
    tm, tn = min(tm, M), min(tn, N)

    def body(a_ref, b_ref, *rest):
        o_ref = rest[-1]
        o_ref[...] = _nt(a_ref[...], b_ref[...]).astype(o_ref.dtype)

    in_specs, args = _with_dep(
        [pl.BlockSpec((tm, K), lambda i, j: (i, 0)), pl.BlockSpec((tn, K), lambda i, j: (j, 0))], [a, b], dep)
    return pl.pallas_call(
        body, name=name, grid=(M // tm, N // tn), in_specs=in_specs,
        out_specs=pl.BlockSpec((tm, tn), lambda i, j: (i, j)),
        out_shape=jax.ShapeDtypeStruct((M, N), out_dtype),
        compiler_params=_cp("parallel", "parallel"))(*args)


def _mm_tn(name, a, b, tm, tn, tk, col_major_tiles=False, col_groups=None):
    T, M = a.shape
    N = b.shape[1]
    tm, tn, tk = min(tm, M), min(tn, N), min(tk, T)
    nk = T // tk

    def body(a_ref, b_ref, o_ref, acc_ref):
        k = pl.program_id(2)

        @pl.when(k == 0)
        def _():
            acc_ref[...] = jnp.zeros_like(acc_ref)
        acc_ref[...] += _tn(a_ref[...], b_ref[...])

        @pl.when(k == nk - 1)
        def _():
            if col_groups:
                for j in range(col_groups[0]):
                    o_ref[j] = acc_ref[:, j * col_groups[1]:(j + 1) * col_groups[1]].astype(BF16)
            else:
                o_ref[...] = acc_ref[...].astype(BF16).reshape(o_ref.shape)

    if col_groups:
        assert tm == M and tn == N and col_groups[0] * col_groups[1] <= N
        out_spec = pl.BlockSpec((col_groups[0], M, col_groups[1]), lambda i, j, k: (0, 0, 0))
        out_shape = jax.ShapeDtypeStruct((col_groups[0], M, col_groups[1]), BF16)
    elif col_major_tiles:
        assert tm == M
        out_spec = pl.BlockSpec((1, tm, tn), lambda i, j, k: (j, 0, 0))
        out_shape = jax.ShapeDtypeStruct((N // tn, M, tn), BF16)
    else:
        out_spec = pl.BlockSpec((tm, tn), lambda i, j, k: (i, j))
        out_shape = jax.ShapeDtypeStruct((M, N), BF16)
    return pl.pallas_call(
        body, name=name, grid=(M // tm, N // tn, nk),
        in_specs=[pl.BlockSpec((tk, tm), lambda i, j, k: (k, i)), pl.BlockSpec((tk, tn), lambda i, j, k: (k, j))],
        out_specs=out_spec, out_shape=out_shape, scratch_shapes=[pltpu.VMEM((tm, tn), F32)],
        compiler_params=_cp("parallel", "parallel", "arbitrary"))(a, b)


def _halo_prev_spec(rb, width):
    return pl.BlockSpec((8, width), lambda i: (jnp.maximum(i * (rb // 8) - 1, 0), 0))


def _halo_next_spec(rb, width, T):
    return pl.BlockSpec((8, width), lambda i: (jnp.minimum((i + 1) * (rb // 8), T // 8 - 1), 0))


LANES = 128
FF_STRIPS = D_FF // LANES
ROW_CHUNK = 32


def _strip(j, base=0):
    return pl.ds(pl.multiple_of(base + j * LANES, LANES), LANES)


def _up_proj_act(h, w_up, w, b, rb=256):
    T, K = h.shape
    W = w_up.shape[1]
    rb = min(rb, T)
    nb = T // rb

    def body(h_ref, wup_ref, w_ref, b_ref, up_ref, act_ref, prev_scr, ext_scr, tail_scr):
        @pl.when(pl.program_id(0) == 0)
        def _():
            tail_scr[...] = jnp.zeros_like(tail_scr)
            prev_scr[...] = jnp.zeros_like(prev_scr)
        up_ref[...] = jnp.dot(h_ref[...], wup_ref[...], preferred_element_type=F32)
        for j in range(FF_STRIPS):
            slot = j % 2
            halves = (slice(j * LANES, (j + 1) * LANES), slice(D_FF + j * LANES, D_FF + (j + 1) * LANES))
            wv = [w_ref[:, cols] for cols in halves]
            bv = [b_ref[:, cols] for cols in halves]
            for hh, cols in enumerate(halves):
                ext_scr[slot, hh, 0:8] = tail_scr[:, cols]
                ext_scr[slot, hh, 8:] = prev_scr[:, cols]
            for r0 in range(0, rb, ROW_CHUNK):
                n = min(ROW_CHUNK, rb - r0)
                c = [ext_scr[slot, hh, 6 + r0:6 + r0 + n] * wv[hh][0:1] + ext_scr[slot, hh, 7 + r0:7 + r0 + n] * wv[hh][1:2]
                     + ext_scr[slot, hh, 8 + r0:8 + r0 + n] * wv[hh][2:3] + bv[hh] for hh in range(2)]
                act_ref[r0:r0 + n, halves[0]] = (_silu(c[0]) * c[1]).astype(BF16)
        tail_scr[...] = prev_scr[rb - 8:rb, :]
        prev_scr[...] = up_ref[...]

    cur = lambda i: (jnp.minimum(i, nb - 1), 0)
    return pl.pallas_call(
        body, name="up_proj_act", grid=(nb + 1,),
        in_specs=[pl.BlockSpec((rb, K), cur), pl.BlockSpec((K, W), lambda i: (0, 0)),
                  pl.BlockSpec((3, W), lambda i: (0, 0)), pl.BlockSpec((1, W), lambda i: (0, 0))],
        out_specs=[pl.BlockSpec((rb, W), cur), pl.BlockSpec((rb, D_FF), lambda i: (jnp.maximum(i - 1, 0), 0))],
        out_shape=[jax.ShapeDtypeStruct((T, W), F32), jax.ShapeDtypeStruct((T, D_FF), BF16)],
        scratch_shapes=[pltpu.VMEM((rb, W), F32), pltpu.VMEM((2, 2, rb + 8, LANES), F32), pltpu.VMEM((8, W), F32)],
        compiler_params=_cp("arbitrary"))(h, w_up, w, b)


def _ffn_act_bwd(up, dact, w, b, rb=256, dep=None):
    T, W = up.shape
    rb = min(rb, T)
    nb = T // rb
    re = rb + 8

    def body(up_ref, prev_ref, next_ref, da_ref, danext_ref, w_ref, b_ref, *rest):
        dup_ref, gw_ref, gb_ref, ext_scr, dc_scr = rest[-5:]
        i = pl.program_id(0)

        @pl.when(i == 0)
        def _():
            gw_ref[...] = jnp.zeros_like(gw_ref)
            gb_ref[...] = jnp.zeros_like(gb_ref)
        last = i == nb - 1

        def fold8(a):
            return jnp.sum(a.reshape(a.shape[0] // 8, 8, LANES), axis=0)

        def strip(j, slot):
            halves = (_strip(j), _strip(j, D_FF))
            wv = [w_ref[:, cols] for cols in halves]
            bv = [b_ref[:, cols] for cols in halves]
            for h, cols in enumerate(halves):
                ext_scr[slot, h,0:8] = jnp.where(i > 0, prev_ref[:, cols], 0.0)
                ext_scr[slot, h,8:8 + rb] = up_ref[:, cols]
                ext_scr[slot, h,8 + rb:] = next_ref[:, cols]
            gb = [jnp.zeros((8, LANES), F32) for _ in range(2)]
            gw = [[jnp.zeros((8, LANES), F32) for _ in range(3)] for _ in range(2)]
            for r0 in range(0, re, ROW_CHUNK):
                n = min(ROW_CHUNK, re - r0)
                tp = [[ext_scr[slot, h,6 + k + r0:6 + k + r0 + n] for k in range(3)] for h in range(2)]
                c = [tp[h][0] * wv[h][0:1] + tp[h][1] * wv[h][1:2] + tp[h][2] * wv[h][2:3] + bv[h] for h in range(2)]
                if r0 < rb:
                    da = da_ref[r0:r0 + n, halves[0]]
                else:
                    da = jnp.where(last, 0.0, danext_ref[:, halves[0]])
                s = _sigmoid(c[0])
                gs = c[0] * s
                dcs = (da * c[1] * (s + gs * (1.0 - s)), da * gs)
                for h in range(2):
                    dc_scr[slot, h,r0:r0 + n] = dcs[h]
                    if r0 < rb:
                        gb[h] = gb[h] + fold8(dcs[h])
                        for k in range(3):
                            gw[h][k] = gw[h][k] + fold8(tp[h][k] * dcs[h])
            for r0 in range(0, rb, ROW_CHUNK):
                n = min(ROW_CHUNK, rb - r0)
                for h, cols in enumerate(halves):
                    dup = (dc_scr[slot, h,r0:r0 + n] * wv[h][2:3] + dc_scr[slot, h,r0 + 1:r0 + 1 + n] * wv[h][1:2]
                           + dc_scr[slot, h,r0 + 2:r0 + 2 + n] * wv[h][0:1])
                    dup_ref[r0:r0 + n, cols] = dup.astype(BF16)
            for h, cols in enumerate(halves):
                gb_ref[:, cols] += jnp.sum(gb[h], axis=0, keepdims=True)
                for k in range(3):
                    gw_ref[k:k + 1, cols] += jnp.sum(gw[h][k], axis=0, keepdims=True)

        def pair(jj, carry):
            strip(2 * jj, 0)
            strip(2 * jj + 1, 1)
            return carry

        lax.fori_loop(0, FF_STRIPS // 2, pair, 0)

    in_specs, args = _with_dep(
        [pl.BlockSpec((rb, W), lambda i: (i, 0)), _halo_prev_spec(rb, W), _halo_next_spec(rb, W, T),
         pl.BlockSpec((rb, D_FF), lambda i: (i, 0)), _halo_next_spec(rb, D_FF, T),
         pl.BlockSpec((3, W), lambda i: (0, 0)), pl.BlockSpec((1, W), lambda i: (0, 0))],
        [up, up, up, dact, dact, w, b], dep)
    return pl.pallas_call(
        body, name="ffn_act_bwd", grid=(nb,), in_specs=in_specs,
        out_specs=[pl.BlockSpec((rb, W), lambda i: (i, 0)), pl.BlockSpec((3, W), lambda i: (0, 0)),
                   pl.BlockSpec((1, W), lambda i: (0, 0))],
        out_shape=[jax.ShapeDtypeStruct((T, W), BF16), jax.ShapeDtypeStruct((3, W), F32),
                   jax.ShapeDtypeStruct((1, W), F32)],
        scratch_shapes=[pltpu.VMEM((2, 2, rb + 16, LANES), F32), pltpu.VMEM((2, 2, re, LANES), F32)],
        compiler_params=_cp("arbitrary"))(*args)


def _lane_iota(shape):
    return lax.broadcasted_iota(jnp.int32, shape, len(shape) - 1)


def _dn_act(p, conv_w, alog_row, dtb_row, rb=256):
    T = p.shape[0]
    rb = min(rb, T)
    W3 = 3 * DN_WIDTH

    def body(p_ref, halo_ref, ba_ref, w_ref, al_ref, dt_ref, q_ref, k_ref, v_ref, bg_ref, ext_scr):
        first = pl.program_id(0) == 0
        outs = (q_ref, k_ref, v_ref)
        for j in range(3 * N_HEADS):
            kind, h = divmod(j, N_HEADS)
            cols = slice(j * HEAD_DIM, (j + 1) * HEAD_DIM)
            cur = p_ref[:, cols]
            ext_scr[j, 0:8] = jnp.where(first, 0.0, halo_ref[:, cols])
            ext_scr[j, 8:] = cur
            wv = w_ref[:, cols]
            s = _silu(ext_scr[j, 5:5 + rb] * wv[0:1] + ext_scr[j, 6:6 + rb] * wv[1:2]
                      + ext_scr[j, 7:7 + rb] * wv[2:3] + cur * wv[3:4])
            if kind < 2:
                scale = HEAD_DIM ** -0.5 if kind == 0 else 1.0
                s = s * (lax.rsqrt(jnp.sum(s * s, axis=-1, keepdims=True) + EPS) * scale)
            outs[kind][:, h * HEAD_DIM:(h + 1) * HEAD_DIM] = s
        ba = ba_ref[...]
        lane = _lane_iota(ba.shape)
        beta = _sigmoid(ba)
        g = -jnp.exp(al_ref[...]) * _softplus(ba + dt_ref[...])
        bg_ref[...] = jnp.where(lane < N_HEADS, beta, jnp.where(lane < 2 * N_HEADS, g, 0.0))

    row512 = pl.BlockSpec((rb, DN_WIDTH), lambda i: (i, 0))
    row128 = pl.BlockSpec((rb, 128), lambda i: (i, 0))
    vec128 = pl.BlockSpec((1, 128), lambda i: (0, 0))
    return pl.pallas_call(
        body, name="dn_act", grid=(T // rb,),
        in_specs=[pl.BlockSpec((rb, W3), lambda i: (i, 0)), _halo_prev_spec(rb, W3),
                  pl.BlockSpec((rb, 128), lambda i: (i, BA_COL // 128)),
                  pl.BlockSpec((4, W3), lambda i: (0, 0)), vec128, vec128],
        out_specs=[row512, row512, row512, row128],
        out_shape=[jax.ShapeDtypeStruct((T, DN_WIDTH), F32)] * 3 + [jax.ShapeDtypeStruct((T, 128), F32)],
        scratch_shapes=[pltpu.VMEM((3 * N_HEADS, rb + 8, HEAD_DIM), F32)],
        compiler_params=_cp("parallel"))(p, p, p, conv_w, alog_row, dtb_row)


def _dn_act_bwd(p, conv_w, alog_row, dtb_row, dq, dk, dv, dbg, dp_mid, rb=256):
    T = p.shape[0]
    rb = min(rb, T)
    nb = T // rb
    re = rb + 8
    W3 = 3 * DN_WIDTH

    def body(p_ref, prev_ref, next_ref, ba_ref, w_ref, al_ref, dt_ref, dq_ref, dqn_ref, dk_ref, dkn_ref,
             dv_ref, dvn_ref, dbg_ref, mid_ref, draw_ref, gw_ref, gad_ref, ext_scr, dc_scr):
        i = pl.program_id(0)
        draw_ref[:, W3:2 * W3] = mid_ref[...]

        @pl.when(i == 0)
        def _():
            gw_ref[...] = jnp.zeros_like(gw_ref)
            gad_ref[...] = jnp.zeros_like(gad_ref)
        row = lax.broadcasted_iota(jnp.int32, (re, 1), 0)
        live = (row < rb) | (i < nb - 1)
        d_refs = ((dq_ref, dqn_ref), (dk_ref, dkn_ref), (dv_ref, dvn_ref))
        for j in range(3 * N_HEADS):
            kind, h = divmod(j, N_HEADS)
            cols = slice(j * HEAD_DIM, (j + 1) * HEAD_DIM)
            hcols = slice(h * HEAD_DIM, (h + 1) * HEAD_DIM)
            ext_scr[j, 0:8] = jnp.where(i > 0, prev_ref[:, cols], 0.0)
            ext_scr[j, 8:8 + rb] = p_ref[:, cols]
            ext_scr[j, 8 + rb:] = next_ref[:, cols]
            tp = [ext_scr[j, 5 + k:5 + k + re] for k in range(4)]
            wv = w_ref[:, cols]
            c = tp[0] * wv[0:1] + tp[1] * wv[1:2] + tp[2] * wv[2:3] + tp[3] * wv[3:4]
            sg = _sigmoid(c)
            s = c * sg
            d_in = jnp.where(live, jnp.concatenate([d_refs[kind][0][:, hcols], d_refs[kind][1][:, hcols]], axis=0), 0.0)
            if kind < 2:
                scale = HEAD_DIM ** -0.5 if kind == 0 else 1.0
                n = lax.rsqrt(jnp.sum(s * s, axis=-1, keepdims=True) + EPS)
                hat = s * n
                d_in = (n * scale) * (d_in - hat * jnp.sum(hat * d_in, axis=-1, keepdims=True))
            dc = d_in * (sg + s * (1.0 - sg))
            dc_scr[j] = dc
            dcc = dc[0:rb]
            draw = (dcc * wv[3:4] + dc_scr[j, 1:1 + rb] * wv[2:3] + dc_scr[j, 2:2 + rb] * wv[1:2]
                    + dc_scr[j, 3:3 + rb] * wv[0:1])
            draw_ref[:, cols] = draw.astype(BF16)
            for k in range(4):
                gw_ref[k:k + 1, cols] += jnp.sum(tp[k][0:rb] * dcc, axis=0, keepdims=True)
        ba = ba_ref[...]
        dbg = dbg_ref[...]
        lane = _lane_iota(ba.shape)
        beta = _sigmoid(ba)
        ea = jnp.exp(al_ref[...])
        z = ba + dt_ref[...]
        d_a = dbg * (-ea) * _sigmoid(z)
        dba = jnp.where(lane < N_HEADS, dbg * beta * (1.0 - beta), jnp.where(lane < 2 * N_HEADS, d_a, 0.0))
        draw_ref[:, BA_COL:] = dba.astype(BF16)
        isg = (lane >= N_HEADS) & (lane < 2 * N_HEADS)
        g = -ea * _softplus(z)
        gad_ref[0:1, :] += jnp.sum(jnp.where(isg, dbg * g, 0.0), axis=0, keepdims=True)
        gad_ref[1:2, :] += jnp.sum(jnp.where(isg, d_a, 0.0), axis=0, keepdims=True)

    row512 = pl.BlockSpec((rb, DN_WIDTH), lambda i: (i, 0))
    row128 = pl.BlockSpec((rb, 128), lambda i: (i, 0))
    vec128 = pl.BlockSpec((1, 128), lambda i: (0, 0))
    next512 = _halo_next_spec(rb, DN_WIDTH, T)
    return pl.pallas_call(
        body, name="dn_act_bwd", grid=(nb,),
        in_specs=[pl.BlockSpec((rb, W3), lambda i: (i, 0)), _halo_prev_spec(rb, W3), _halo_next_spec(rb, W3, T),
                  pl.BlockSpec((rb, 128), lambda i: (i, BA_COL // 128)),
                  pl.BlockSpec((4, W3), lambda i: (0, 0)), vec128, vec128,
                  row512, next512, row512, next512, row512, next512, row128,
                  pl.BlockSpec((rb, W3), lambda i: (i, 0))],
        out_specs=[pl.BlockSpec((rb, PROJ_PAD), lambda i: (i, 0)),
                   pl.BlockSpec((4, W3), lambda i: (0, 0)), pl.BlockSpec((2, 128), lambda i: (0, 0))],
        out_shape=[jax.ShapeDtypeStruct((T, PROJ_PAD), BF16),
                   jax.ShapeDtypeStruct((4, W3), F32), jax.ShapeDtypeStruct((2, 128), F32)],
        scratch_shapes=[pltpu.VMEM((3 * N_HEADS, rb + 16, HEAD_DIM), F32), pltpu.VMEM((3 * N_HEADS, re, HEAD_DIM), F32)],
        compiler_params=_cp("arbitrary"))(p, p, p, p, conv_w, alog_row, dtb_row, dq, dq, dk, dk, dv, dv, dbg, dp_mid)


def _tri(incl):
    ii = lax.broadcasted_iota(jnp.int32, (CHUNK, CHUNK), 0)
    jj = lax.broadcasted_iota(jnp.int32, (CHUNK, CHUNK), 1)
    return ii, jj, ((ii >= jj) if incl else (ii > jj))


def _dn_chunk(k, bg, cb=4):
    T = k.shape[0]
    N = T // CHUNK
    cb = min(cb, N)

    def body(k_ref, bg_ref, gc_ref, gct_ref, l_ref):
        ii, jj, incl = _tri(True)
        tri = incl.astype(F32)
        U = range(cb)
        bgv = [bg_ref[u * CHUNK:(u + 1) * CHUNK, :] for u in U]
        gc = [jnp.dot(tri, bgv[u], precision=lax.Precision.HIGHEST, preferred_element_type=F32) for u in U]
        gct = [gc[u].T for u in U]
        kk = [[None] * N_HEADS for _ in U]
        for u in U:
            gc_ref[u * CHUNK:(u + 1) * CHUNK, :] = gc[u]
            gct_ref[u] = gct[u][0:8]
            for h in range(N_HEADS):
                kh = k_ref[u * CHUNK:(u + 1) * CHUNK, h * HEAD_DIM:(h + 1) * HEAD_DIM]
                kk[u][h] = _nt(kh * bgv[u][:, h:h + 1], kh)
        for u in U:
            for h in range(N_HEADS):
                gcol = gc[u][:, N_HEADS + h:N_HEADS + h + 1]
                grow = gct[u][N_HEADS + h:N_HEADS + h + 1, :]
                l_ref[u, h] = kk[u][h] * jnp.exp(jnp.where(ii > jj, gcol - grow, NEG))

    rows = cb * CHUNK
    return pl.pallas_call(
        body, name="dn_chunk", grid=(N // cb,),
        in_specs=[pl.BlockSpec((rows, DN_WIDTH), lambda n: (n, 0)), pl.BlockSpec((rows, 128), lambda n: (n, 0))],
        out_specs=[pl.BlockSpec((rows, 128), lambda n: (n, 0)), pl.BlockSpec((cb, 8, CHUNK), lambda n: (n, 0, 0)),
                   pl.BlockSpec((cb, N_HEADS, CHUNK, CHUNK), lambda n: (n, 0, 0, 0))],
        out_shape=[jax.ShapeDtypeStruct((T, 128), F32), jax.ShapeDtypeStruct((N, 8, CHUNK), F32),
                   jax.ShapeDtypeStruct((N, N_HEADS, CHUNK, CHUNK), F32)],
        compiler_params=_cp("parallel"))(k, bg)


def _tri_inv(lt):
    S = lt.shape[1]

    def body(l_ref, a_ref):
        sub = lax.broadcasted_iota(jnp.int32, (8, S), 0)
        groups = CHUNK // 8
        for i in range(CHUNK):
            acc = [((sub + 8 * k) == i).astype(F32) for k in range(groups)]
            for jb in range((i + 7) // 8):
                nk = jb + 1

                def step(j, carry, nk=nk, i=i):
                    lrow = l_ref[pl.ds(i * CHUNK + j, 1), :]
                    return tuple(carry[k] - lrow * a_ref[j, 8 * k:8 * k + 8, :] for k in range(nk))

                acc[:nk] = list(lax.fori_loop(8 * jb, min(8 * jb + 8, i), step, tuple(acc[:nk])))
            for k in range(groups):
                a_ref[i, 8 * k:8 * k + 8, :] = acc[k]

    return pl.pallas_call(
        body, name="tri_inv", out_shape=jax.ShapeDtypeStruct((CHUNK, CHUNK, S), F32),
        compiler_params=pltpu.CompilerParams(vmem_limit_bytes=VMEM_LIMIT))(lt)


def _dn_head_terms(qh, kh, vh, beta, gcol, grow):
    ii, jj, incl = _tri(True)
    gam = jnp.exp(jnp.where(incl, gcol - grow, NEG))
    glast = grow[:, CHUNK - 1:CHUNK]
    cd = jnp.exp(glast)
    shape = (CHUNK, HEAD_DIM)
    E = jnp.broadcast_to(jnp.exp(gcol), shape)
    Fd = jnp.broadcast_to(jnp.exp(glast - gcol), shape)
    beta = jnp.broadcast_to(beta, shape)
    kb = kh * beta
    return dict(ii=ii, jj=jj, gam=gam, E=E, F=Fd, beta=beta, cd=cd, kb=kb, vb=vh * beta, W=kb * E, qE=qh * E,
                kt=kh * Fd)


def _apply_a(a, u):
    hi, lo = _split(a)
    ub = _bf(u)
    return jnp.dot(hi, ub, preferred_element_type=F32) + jnp.dot(lo, ub, preferred_element_type=F32)


def _dn_scan(q, k, v, bg, gc, gct, a):
    T = q.shape[0]
    N = T // CHUNK
    cb = min(SCAN_CHUNKS_FWD, N)

    def body(q_ref, k_ref, v_ref, bg_ref, gc_ref, gct_ref, a_ref, o_ref, sall_ref, s_ref):
        @pl.when(pl.program_id(0) == 0)
        def _():
            s_ref[...] = jnp.zeros_like(s_ref)
        H = range(N_HEADS)
        sl = [slice(h * HEAD_DIM, (h + 1) * HEAD_DIM) for h in H]
        pre = []
        for u in range(cb):
            r = slice(u * CHUNK, (u + 1) * CHUNK)
            bgv, gcv, gctv = bg_ref[r, :], gc_ref[r, :], gct_ref[u]
            q_, k_ = [q_ref[r, s] for s in sl], [k_ref[r, s] for s in sl]
            t = [_dn_head_terms(q_[h], k_[h], v_ref[r, sl[h]], bgv[:, h:h + 1],
                                gcv[:, N_HEADS + h:N_HEADS + h + 1], gctv[N_HEADS + h:N_HEADS + h + 1, :]) for h in H]
            P = [_nt(q_[h], k_[h]) * t[h]["gam"] for h in H]
            pre.append((r, t, P))
        S = [s_ref[h] for h in H]
        for u in range(cb):
            r, t, P = pre[u]
            for h in H:
                sall_ref[u, h] = S[h]
            WS = [_nn(t[h]["W"], S[h]) for h in H]
            qS = [_nn(t[h]["qE"], S[h]) for h in H]
            vn = [_apply_a(a_ref[u, h], t[h]["vb"] - WS[h]) for h in H]
            Pv = [_nn(P[h], vn[h]) for h in H]
            kv = [_tn(t[h]["kt"], vn[h]) for h in H]
            for h in H:
                o_ref[r, sl[h]] = qS[h] + Pv[h]
            S = [t[h]["cd"] * S[h] + kv[h] for h in H]
        for h in H:
            s_ref[h] = S[h]

    row512 = pl.BlockSpec((cb * CHUNK, DN_WIDTH), lambda n: (n, 0))
    row128 = pl.BlockSpec((cb * CHUNK, 128), lambda n: (n, 0))
    return pl.pallas_call(
        body, name="dn_scan", grid=(N // cb,),
        in_specs=[row512, row512, row512, row128, row128, pl.BlockSpec((cb, 8, CHUNK), lambda n: (n, 0, 0)),
                  pl.BlockSpec((cb, N_HEADS, CHUNK, CHUNK), lambda n: (n, 0, 0, 0))],
        out_specs=[row512, pl.BlockSpec((cb, N_HEADS, HEAD_DIM, HEAD_DIM), lambda n: (n, 0, 0, 0))],
        out_shape=[jax.ShapeDtypeStruct((T, DN_WIDTH), F32),
                   jax.ShapeDtypeStruct((N, N_HEADS, HEAD_DIM, HEAD_DIM), F32)],
        scratch_shapes=[pltpu.VMEM((N_HEADS, HEAD_DIM, HEAD_DIM), F32)],
        compiler_params=_cp("arbitrary"))(q, k, v, bg, gc, gct, a)


def _dn_scan_bwd(q, k, v, bg, gc, gct, a, a_t, sall, do, dep=None):
    T = q.shape[0]
    N = T // CHUNK

    cb = min(SCAN_CHUNKS, N)
    nb = N // cb

    def body(q_ref, k_ref, v_ref, bg_ref, gc_ref, gct_ref, a_ref, at_ref, sall_ref, do_ref, *rest):
        dq_ref, dk_ref, dv_ref, dbg_ref, ds_ref = rest[-5:]
        @pl.when(pl.program_id(0) == 0)
        def _():
            ds_ref[...] = jnp.zeros_like(ds_ref)
        lane = _lane_iota((CHUNK, 128))
        rowi = lax.broadcasted_iota(jnp.int32, (CHUNK, 1), 0)
        ii, jj, _ = _tri(True)
        rev = (jj >= ii).astype(F32)
        H = range(N_HEADS)
        sl = [slice(h * HEAD_DIM, (h + 1) * HEAD_DIM) for h in H]
        pre = {}
        for u in reversed(range(cb)):
            r = slice(u * CHUNK, (u + 1) * CHUNK)
            bgv, gcv, gctv = bg_ref[r, :], gc_ref[r, :], gct_ref[u]
            q_, k_, v_ = [q_ref[r, s] for s in sl], [k_ref[r, s] for s in sl], [v_ref[r, s] for s in sl]
            dO = [do_ref[r, s] for s in sl]
            t = [_dn_head_terms(q_[h], k_[h], v_[h], bgv[:, h:h + 1], gcv[:, N_HEADS + h:N_HEADS + h + 1],
                                gctv[N_HEADS + h:N_HEADS + h + 1, :]) for h in H]
            beta = [t[h]["beta"] for h in H]
            S = [sall_ref[u, h] for h in H]
            A = [a_ref[u, h] for h in H]
            WS = [_nn(t[h]["W"], S[h]) for h in H]
            KK = [_nt(t[h]["kb"], k_[h]) for h in H]
            QK = [_nt(q_[h], k_[h]) for h in H]
            d_qE = [_nt(dO[h], S[h]) for h in H]
            vn = [_apply_a(A[h], t[h]["vb"] - WS[h]) for h in H]
            PtdO = [_tn(QK[h] * t[h]["gam"], dO[h]) for h in H]
            qEdO = [_tn(t[h]["qE"], dO[h]) for h in H]
            dOvn = [_nt(dO[h], vn[h]) for h in H]
            dQK = [jnp.where(ii >= jj, dOvn[h], 0.0) * t[h]["gam"] for h in H]
            dQKk = [_nn(dQK[h], k_[h]) for h in H]
            dQKq = [_tn(dQK[h], q_[h]) for h in H]
            pre[u] = (r, q_, k_, v_, beta, t, S, A, KK, QK, d_qE, vn, PtdO, qEdO, dQK, dQKk, dQKq)
        dSn = [ds_ref[h] for h in H]
        for u in reversed(range(cb)):
            r, q_, k_, v_, beta, t, S, A, KK, QK, d_qE, vn, PtdO, qEdO, dQK, dQKk, dQKq = pre[u]
            gam, E, Fd, cd, kb = ([t[h][n] for h in H] for n in ("gam", "E", "F", "cd", "kb"))
            ktdS = [_nn(t[h]["kt"], dSn[h]) for h in H]
            dU = [_apply_a(at_ref[u, h], PtdO[h] + ktdS[h]) for h in H]
            d_kt = [_nt(vn[h], dSn[h]) for h in H]
            dUvn = [_nt(dU[h], vn[h]) for h in H]
            dUS = [_nt(dU[h], S[h]) for h in H]
            WdU = [_tn(t[h]["W"], dU[h]) for h in H]
            d_cd = [jnp.sum(S[h] * dSn[h]) for h in H]
            dSn = [cd[h] * dSn[h] + qEdO[h] - WdU[h] for h in H]
            dKK = [jnp.where(ii > jj, -dUvn[h], 0.0) * gam[h] for h in H]
            dKKk = [_nn(dKK[h], k_[h]) for h in H]
            dKKkb = [_tn(dKK[h], kb[h]) for h in H]
            dbeta_arr = jnp.zeros((CHUNK, 128), F32)
            dgc_arr = jnp.zeros((CHUNK, 128), F32)
            for h in H:
                dW = -dUS[h]
                dq_ref[r, sl[h]] = dQKk[h] + d_qE[h] * E[h]
                d_kb = dKKk[h] + dW * E[h]
                dk_ref[r, sl[h]] = dQKq[h] + dKKkb[h] + d_kb * beta[h] + d_kt[h] * Fd[h]
                dv_ref[r, sl[h]] = dU[h] * beta[h]
                Z = dQK[h] * QK[h] + dKK[h] * KK[h]
                dbeta = jnp.sum(dU[h] * v_[h] + d_kb * k_[h], axis=-1, keepdims=True)
                m_e = (dW * kb[h] + d_qE[h] * q_[h]) * E[h]
                m_f = d_kt[h] * k_[h] * Fd[h]
                zdiag = jnp.where(ii == jj, jnp.sum(Z, axis=0, keepdims=True), 0.0)
                dgc = (jnp.sum(m_e - m_f, axis=-1, keepdims=True) + jnp.sum(Z - zdiag, axis=-1, keepdims=True)
                       + jnp.where(rowi == CHUNK - 1, jnp.sum(m_f) + d_cd[h] * cd[h], 0.0))
                dbeta_arr = dbeta_arr + jnp.where(lane == h, dbeta, 0.0)
                dgc_arr = dgc_arr + jnp.where(lane == N_HEADS + h, dgc, 0.0)
            dbg_ref[r, :] = dbeta_arr + jnp.dot(rev, dgc_arr, precision=lax.Precision.HIGHEST,
                                                preferred_element_type=F32)
        for h in H:
            ds_ref[h] = dSn[h]

    row512 = pl.BlockSpec((cb * CHUNK, DN_WIDTH), lambda n: (nb - 1 - n, 0))
    row128 = pl.BlockSpec((cb * CHUNK, 128), lambda n: (nb - 1 - n, 0))
    in_specs, args = _with_dep(
        [row512, row512, row512, row128, row128,
         pl.BlockSpec((cb, 8, CHUNK), lambda n: (nb - 1 - n, 0, 0)),
         pl.BlockSpec((cb, N_HEADS, CHUNK, CHUNK), lambda n: (nb - 1 - n, 0, 0, 0)),
         pl.BlockSpec((cb, N_HEADS, CHUNK, CHUNK), lambda n: (nb - 1 - n, 0, 0, 0)),
         pl.BlockSpec((cb, N_HEADS, HEAD_DIM, HEAD_DIM), lambda n: (nb - 1 - n, 0, 0, 0)), row512],
        [q, k, v, bg, gc, gct, a, a_t, sall, do], dep)
    return pl.pallas_call(
        body, name="dn_scan_bwd", grid=(nb,), in_specs=in_specs,
        out_specs=[row512, row512, row512, row128],
        out_shape=[jax.ShapeDtypeStruct((T, DN_WIDTH), F32)] * 3 + [jax.ShapeDtypeStruct((T, 128), F32)],
        scratch_shapes=[pltpu.VMEM((N_HEADS, HEAD_DIM, HEAD_DIM), F32)],
        compiler_params=_cp("arbitrary"))(*args)


MIX_BLOCKS = 2


def _sg_mask():
    ii = lax.broadcasted_iota(jnp.int32, (SG_BLOCK, SG_BLOCK), 0) // CHUNK
    jj = lax.broadcasted_iota(jnp.int32, (SG_BLOCK, SG_BLOCK), 1) // CHUNK
    return jj <= ii


def _mix_fwd(o, p, ong, sgn, sgw, sgbt):
    T = o.shape[0]
    rb = min(MIX_BLOCKS * SG_BLOCK, T)

    def body(o_ref, gate_ref, u_ref, vg_ref, ong_ref, sgn_ref, sgw_ref, sgbt_ref, mix_ref):
        mask = _sg_mask()
        for u0 in range(0, rb, SG_BLOCK):
            rows = slice(u0, u0 + SG_BLOCK)
            for h in range(N_HEADS):
                sl = slice(h * HEAD_DIM, (h + 1) * HEAD_DIM)
                oh = o_ref[rows, sl]
                r = lax.rsqrt(jnp.mean(oh * oh, axis=-1, keepdims=True) + EPS)
                mix_ref[rows, sl] = (oh * r * ong_ref[...] * _silu(gate_ref[rows, sl])).astype(BF16)
            for gi in range(SG_GROUPS):
                sl = slice(gi * SG_BLOCK, (gi + 1) * SG_BLOCK)
                gv = _gelu(vg_ref[rows, sl])
                r = lax.rsqrt(jnp.mean(gv * gv, axis=-1, keepdims=True) + EPS)
                vh = gv * r * sgn_ref[:, sl]
                s = _nn(jnp.where(mask, sgw_ref[gi], 0.0), vh) + sgbt_ref[:, gi:gi + 1]
                mix_ref[rows, DN_WIDTH + gi * SG_BLOCK:DN_WIDTH + (gi + 1) * SG_BLOCK] = (
                    _gelu(u_ref[rows, sl]) * s).astype(BF16)

    def col(c):
        return pl.BlockSpec((rb, 512), lambda i: (i, c))
    return pl.pallas_call(
        body, name="mix_fwd", grid=(T // rb,),
        in_specs=[pl.BlockSpec((rb, DN_WIDTH), lambda i: (i, 0)), col(3), col(4), col(5),
                  pl.BlockSpec((1, 128), lambda i: (0, 0)), pl.BlockSpec((1, SG_WIDTH), lambda i: (0, 0)),
                  pl.BlockSpec((SG_GROUPS, SG_BLOCK, SG_BLOCK), lambda i: (0, 0, 0)),
                  pl.BlockSpec((SG_BLOCK, 128), lambda i: (0, 0))],
        out_specs=pl.BlockSpec((rb, D_MODEL), lambda i: (i, 0)),
        out_shape=jax.ShapeDtypeStruct((T, D_MODEL), BF16),
        compiler_params=_cp("parallel"))(o, p, p, p, ong, sgn, sgw, sgbt)


def _mix_bwd(o, p, ong, sgn, sgw, sgbt, dmix, dep=None):
    T = o.shape[0]
    rb = min(MIX_BLOCKS * SG_BLOCK, T)

    def body(o_ref, gate_ref, u_ref, vg_ref, ong_ref, sgn_ref, sgw_ref, sgbt_ref, dmix_ref, *rest):
        do_ref, dp_ref, gong_ref, gsgn_ref, gsgw_ref, gsgbt_ref = rest[-6:]
        @pl.when(pl.program_id(0) == 0)
        def _():
            gong_ref[...] = jnp.zeros_like(gong_ref)
            gsgn_ref[...] = jnp.zeros_like(gsgn_ref)
            gsgw_ref[...] = jnp.zeros_like(gsgw_ref)
            gsgbt_ref[...] = jnp.zeros_like(gsgbt_ref)
        mask = _sg_mask()
        lane = _lane_iota((SG_BLOCK, 128))
        for u0 in range(0, rb, SG_BLOCK):
            rows = slice(u0, u0 + SG_BLOCK)
            for h in range(N_HEADS):
                sl = slice(h * HEAD_DIM, (h + 1) * HEAD_DIM)
                oh = o_ref[rows, sl]
                dm = dmix_ref[rows, sl]
                r = lax.rsqrt(jnp.mean(oh * oh, axis=-1, keepdims=True) + EPS)
                oh_hat = oh * r
                gt = gate_ref[rows, sl]
                sg = _silu(gt)
                dp_ref[rows, sl] = (dm * oh_hat * ong_ref[...] * _dsilu(gt)).astype(BF16)
                dn_ = dm * sg
                gong_ref[...] += jnp.sum(dn_ * oh_hat, axis=0, keepdims=True)
                dhat = dn_ * ong_ref[...]
                do_ref[rows, sl] = r * (dhat - oh_hat * jnp.mean(dhat * oh_hat, axis=-1, keepdims=True))
            for gi in range(SG_GROUPS):
                sl = slice(gi * SG_BLOCK, (gi + 1) * SG_BLOCK)
                vraw = vg_ref[rows, sl]
                gv = _gelu(vraw)
                r = lax.rsqrt(jnp.mean(gv * gv, axis=-1, keepdims=True) + EPS)
                vhat = gv * r
                vn = vhat * sgn_ref[:, sl]
                wm = jnp.where(mask, sgw_ref[gi], 0.0)
                s = _nn(wm, vn) + sgbt_ref[:, gi:gi + 1]
                uraw = u_ref[rows, sl]
                dm = dmix_ref[rows, DN_WIDTH + gi * SG_BLOCK:DN_WIDTH + (gi + 1) * SG_BLOCK]
                dp_ref[rows, DN_WIDTH + gi * SG_BLOCK:DN_WIDTH + (gi + 1) * SG_BLOCK] = (
                    dm * s * _dgelu(uraw)).astype(BF16)
                ds = dm * _gelu(uraw)
                gsgbt_ref[...] += jnp.where(lane == gi, jnp.sum(ds, axis=-1, keepdims=True), 0.0)
                gsgw_ref[gi] += jnp.where(mask, _nt(ds, vn), 0.0)
                dvn = _tn(wm, ds)
                gsgn_ref[:, sl] += jnp.sum(dvn * vhat, axis=0, keepdims=True)
                dhat = dvn * sgn_ref[:, sl]
                dgv = r * (dhat - vhat * jnp.mean(dhat * vhat, axis=-1, keepdims=True))
                dp_ref[rows, 2 * DN_WIDTH + gi * SG_BLOCK:2 * DN_WIDTH + (gi + 1) * SG_BLOCK] = (
                    dgv * _dgelu(vraw)).astype(BF16)

    def col(c):
        return pl.BlockSpec((rb, 512), lambda i: (i, c))
    full = lambda *s: pl.BlockSpec(s, lambda i: (0,) * len(s))
    in_specs, args = _with_dep(
        [pl.BlockSpec((rb, DN_WIDTH), lambda i: (i, 0)), col(3), col(4), col(5),
         full(1, 128), full(1, SG_WIDTH), full(SG_GROUPS, SG_BLOCK, SG_BLOCK), full(SG_BLOCK, 128),
         pl.BlockSpec((rb, D_MODEL), lambda i: (i, 0))],
        [o, p, p, p, ong, sgn, sgw, sgbt, dmix], dep)
    return pl.pallas_call(
        body, name="mix_bwd", grid=(T // rb,), in_specs=in_specs,
        out_specs=[pl.BlockSpec((rb, DN_WIDTH), lambda i: (i, 0)), pl.BlockSpec((rb, 3 * 512), lambda i: (i, 0)),
                   full(1, 128), full(1, SG_WIDTH), full(SG_GROUPS, SG_BLOCK, SG_BLOCK), full(SG_BLOCK, 128)],
        out_shape=[jax.ShapeDtypeStruct((T, DN_WIDTH), F32), jax.ShapeDtypeStruct((T, 3 * 512), BF16),
                   jax.ShapeDtypeStruct((1, 128), F32), jax.ShapeDtypeStruct((1, SG_WIDTH), F32),
                   jax.ShapeDtypeStruct((SG_GROUPS, SG_BLOCK, SG_BLOCK), F32),
                   jax.ShapeDtypeStruct((SG_BLOCK, 128), F32)],
        compiler_params=_cp("arbitrary"))(*args)


def _pad_lanes(row, offset=0):
    n = row.shape[1]
    return jnp.pad(row, ((0, 0), (offset, 128 - n - offset)))


def _local_step(x, tgt, w, dep=None, late_weights=None, on_grad=None):
    T = x.shape[0]
    N = T // CHUNK
    on_grad = on_grad or (lambda name, g: None)
    alog_row = _pad_lanes(w["dn_a_log"], N_HEADS)
    dtb_row = _pad_lanes(w["dn_dt_bias"], N_HEADS)
    sgbt = jnp.pad(w["sg_b"].T, ((0, 0), (0, 128 - SG_GROUPS)))

    p, h1, w_in_pad = _in_proj(x, w["attn_norm_g"], w["w_in"], dep=dep)
    q, k, v, bg = _dn_act(p, w["dn_conv_w"], alog_row, dtb_row)
    gc, gct, lmat = _dn_chunk(k, bg)
    lt = lmat.reshape(N * N_HEADS, CHUNK * CHUNK).T
    at = _tri_inv(lt)
    a = at.reshape(CHUNK * CHUNK, N * N_HEADS).T.reshape(N, N_HEADS, CHUNK, CHUNK)
    a_t = at.transpose(1, 0, 2).reshape(CHUNK * CHUNK, N * N_HEADS).T.reshape(N, N_HEADS, CHUNK, CHUNK)
    o, sall = _dn_scan(q, k, v, bg, gc, gct, a)
    mix = _mix_fwd(o, p, w["dn_out_norm_g"], w["sg_norm_g"], w["sg_w"], sgbt)
    if late_weights is not None:
        w = {**w, **late_weights("out_proj", mix)}
    x2, h2 = _out_proj(mix, w["w_out"], x, w["ffn_norm_g"])
    up, act = _up_proj_act(h2, w["w_up"], w["ffn_conv_w"], w["ffn_conv_b"])
    if late_weights is not None:
        w = {**w, **late_weights("down_proj", act)}
    loss, dx3, g_final = _down_proj_loss(act, w["w_down"], x2, tgt, w["final_norm_g"])

    dact = _mm_nt("d_act", dx3, w["w_down"], F32, 512, D_FF)
    g_w_down = _mm_tn("g_w_down", act, dx3, D_FF, 1024, 1024)
    tok = on_grad("w_down", g_w_down)
    dup, g_ffn_conv_w, g_ffn_conv_b = _ffn_act_bwd(up, dact, w["ffn_conv_w"], w["ffn_conv_b"], dep=tok)
    g_w_up = _mm_tn("g_w_up", h2, dup, 1024, 2 * D_FF // 4, 2048, col_major_tiles=True)
    tok = on_grad("w_up", g_w_up)
    dx2, g_ffn_norm = _mm_nt_rms_bwd("d_h2", dup, w["w_up"], x2, w["ffn_norm_g"], dx3, dep=tok)
    dmix = _mm_nt("d_mix", dx2, w["w_out"], F32, 512, 1024)
    g_w_out = _mm_tn("g_w_out", mix, dx2, 1024, 1024, 1024)
    tok = on_grad("w_out", g_w_out)
    do, dp_mid, g_ong, g_sgn, g_sgw, g_sgbt = _mix_bwd(o, p, w["dn_out_norm_g"], w["sg_norm_g"], w["sg_w"], sgbt,
                                                      dmix, dep=tok)
    early = dict(dn_out_norm_g=g_ong, sg_norm_g=g_sgn, sg_w=g_sgw, sg_bt=g_sgbt,
                 ffn_norm_g=g_ffn_norm, ffn_conv_w=g_ffn_conv_w, ffn_conv_b=g_ffn_conv_b, final_norm_g=g_final)
    tok = on_grad("small_early", early)
    dq, dk, dv, dbg = _dn_scan_bwd(q, k, v, bg, gc, gct, a, a_t, sall, do, dep=tok)
    dp, g_dn_conv_w, g_ad = _dn_act_bwd(p, w["dn_conv_w"], alog_row, dtb_row, dq, dk, dv, dbg, dp_mid)
    g_w_in = _mm_tn("g_w_in", h1, dp, 1024, PROJ_PAD, 1024, col_groups=(4, PROJ_COLS // 4))
    tok = on_grad("w_in", g_w_in)
    grad_x, g_attn_norm = _mm_nt_rms_bwd("d_h1", dp, w_in_pad, x, w["attn_norm_g"], dx2, dep=tok)

    grads = dict(attn_norm_g=g_attn_norm, w_in=g_w_in, dn_conv_w=g_dn_conv_w, a_dt=g_ad,
                 w_out=g_w_out, w_up=g_w_up, w_down=g_w_down, **early)
    return loss, grad_x, grads


def _me():
    return lax.axis_index("x"), lax.axis_index("y"), lax.axis_index("c")


def _peer(rel):
    x, y, c = _me()
    return {"x": (1 - x, y, c), "y": (x, 1 - y, c), "xy": (1 - x, 1 - y, c), "c": (x, y, 1 - c)}[rel]


def _chip_of(dev):
    return 2 * dev[0] + dev[1]


CHIP_RELS = ("x", "y", "xy")


def _run_copies(copies, sends, recvs):
    for cp in copies:
        cp.start()
    for cp in recvs:
        cp.wait_recv()
    for cp in sends:
        cp.wait_send()


def _gather_first(w_shard, small_shard):
    R = w_shard.shape[0]
    r2 = R // 2

    def body(w_ref, s_ref, w_out, s_out, send_sems, recv_sems):
        x, y, c = _me()
        me = _chip_of((x, y))
        sib = _peer("c")

        def half(chip, core):
            return w_out.at[chip, pl.ds(pl.multiple_of(core * r2, 8), r2), :]

        def copy(k, src, dst, to):
            return pltpu.make_async_remote_copy(src_ref=src, dst_ref=dst, send_sem=send_sems.at[k],
                                                recv_sem=recv_sems.at[k], device_id=to, device_id_type=MESH)

        own_rows = w_ref.at[pl.ds(pl.multiple_of(c * r2, 8), r2), :]
        first = [copy(r, own_rows, half(me, c), _peer(rel)) for r, rel in enumerate(CHIP_RELS)]
        first += [copy(3 + r, s_ref, s_out.at[me], _peer(rel)) for r, rel in enumerate(CHIP_RELS)]
        for cp in first:
            cp.start()
        passed = []
        for r, rel in enumerate(CHIP_RELS):
            their = _chip_of(_peer(rel))
            copy(r, own_rows, half(their, c), _peer(rel)).wait_recv()
            fwd = copy(6 + r, half(their, c), half(their, c), sib)
            fwd.start()
            passed.append(fwd)
        for r, rel in enumerate(CHIP_RELS):
            their = _chip_of(_peer(rel))
            copy(3 + r, s_ref, s_out.at[their], _peer(rel)).wait_recv()
            copy(6 + r, own_rows, half(their, 1 - c), sib).wait_recv()
        for cp in first + passed:
            cp.wait_send()

    w_all, s_all = pl.pallas_call(
        body, name="gather_first", in_specs=[ANY, ANY], out_specs=[ANY, ANY],
        out_shape=[jax.ShapeDtypeStruct((4,) + w_shard.shape, w_shard.dtype),
                   jax.ShapeDtypeStruct((4,) + small_shard.shape, small_shard.dtype)],
        scratch_shapes=[pltpu.SemaphoreType.DMA((9,)), pltpu.SemaphoreType.DMA((9,))])(w_shard, small_shard)
    me = _chip_of(_me())
    return (lax.dynamic_update_index_in_dim(w_all, w_shard, me, 0),
            lax.dynamic_update_index_in_dim(s_all, small_shard, me, 0))


OTHERS = tuple((fx, fy, fc) for fx in (0, 1) for fy in (0, 1) for fc in (0, 1) if (fx, fy, fc) != (0, 0, 0))


def _other(flip):
    x, y, c = _me()
    return (x ^ flip[0], y ^ flip[1], c ^ flip[2])


def _linear(dev):
    return 4 * dev[0] + 2 * dev[1] + dev[2]


def _exchange_small(small):
    def body(small_ref, out_ref, send_sems, recv_sems):
        my_slot = _linear(_me())
        sends, recvs = [], []
        for k, flip in enumerate(OTHERS):
            peer = _other(flip)
            sends.append(pltpu.make_async_remote_copy(
                src_ref=small_ref, dst_ref=out_ref.at[my_slot], send_sem=send_sems.at[k], recv_sem=recv_sems.at[k],
                device_id=peer, device_id_type=MESH))
            recvs.append(pltpu.make_async_remote_copy(
                src_ref=small_ref, dst_ref=out_ref.at[_linear(peer)], send_sem=send_sems.at[k],
                recv_sem=recv_sems.at[k], device_id=peer, device_id_type=MESH))
        _run_copies(sends, sends, recvs)

    out = pl.pallas_call(
        body, name="exchange_small", in_specs=[ANY], out_specs=ANY,
        out_shape=jax.ShapeDtypeStruct((8,) + small.shape, small.dtype),
        scratch_shapes=[pltpu.SemaphoreType.DMA((7,)), pltpu.SemaphoreType.DMA((7,))])(small)
    return lax.dynamic_update_index_in_dim(out, small, _linear(_me()), 0)


def _pair_swap(name, halves):
    n = len(halves)

    def body(*refs):
        src, out = refs[:n], refs[n:2 * n]
        send_sems, recv_sems = refs[2 * n:]
        sib = _peer("c")
        copies = [pltpu.make_async_remote_copy(
            src_ref=src[i], dst_ref=out[i], send_sem=send_sems.at[i], recv_sem=recv_sems.at[i],
            device_id=sib, device_id_type=MESH) for i in range(n)]
        _run_copies(copies, copies, copies)

    return pl.pallas_call(
        body, name=name, in_specs=[ANY] * n, out_specs=[ANY] * n,
        out_shape=[jax.ShapeDtypeStruct(h.shape, h.dtype) for h in halves],
        scratch_shapes=[pltpu.SemaphoreType.DMA((n,)), pltpu.SemaphoreType.DMA((n,))])(*halves)


HBM = pl.BlockSpec(memory_space=pltpu.HBM)
SEM = pl.BlockSpec(memory_space=pltpu.SEMAPHORE)
EFFECT = pltpu.SideEffectType.DATAFLOW_SIDE_EFFECTING


def _hbm(a):
    return pltpu.with_memory_space_constraint(a, pltpu.HBM)


def _transfer_start(name, srcs, lands, n_copies, make_copies, after=None):
    n, m = len(srcs), len(lands)

    def body(*refs):
        src, land = refs[:n], refs[n:n + m]
        outs = refs[n + m + (after is not None):]
        send_sems, recv_sems, token = outs[0], outs[1], outs[-1]
        for cp in make_copies(src, land, send_sems, recv_sems):
            cp.start()
        token[...] = jnp.zeros_like(token)

    arrs = list(srcs) + list(lands)
    in_specs, args = _with_dep([HBM] * (n + m), [_hbm(a) for a in arrs], after)
    out = pl.pallas_call(
        body, name=name,
        out_shape=(pltpu.SemaphoreType.DMA((n_copies,)), pltpu.SemaphoreType.DMA((n_copies,)),
                   *[pltpu.HBM(a.shape, a.dtype) for a in arrs], jax.ShapeDtypeStruct((8, 128), F32)),
        in_specs=in_specs,
        out_specs=(SEM, SEM, *[HBM] * (n + m), pl.BlockSpec(memory_space=pltpu.VMEM)),
        input_output_aliases={i: 2 + i for i in range(n + m)},
        compiler_params=pltpu.CompilerParams(has_side_effects=EFFECT))(*args)
    return out[0], out[1], list(out[2:2 + n]), list(out[2 + n:2 + n + m]), out[-1]


def _transfer_wait(name, send_sems, recv_sems, srcs, lands, make_copies, after):
    n, m = len(srcs), len(lands)

    def body(*refs):
        src, land = refs[:n], refs[n:n + m]
        s_sems, r_sems = refs[n + m], refs[n + m + 1]
        for cp in make_copies(src, land, s_sems, r_sems):
            cp.wait_send()
            cp.wait_recv()

    arrs = list(srcs) + list(lands)
    out = pl.pallas_call(
        body, name=name, out_shape=tuple(pltpu.HBM(a.shape, a.dtype) for a in arrs),
        in_specs=[HBM] * (n + m) + [SEM, SEM, ANY], out_specs=tuple([HBM] * (n + m)),
        input_output_aliases={i: i for i in range(n + m)},
        compiler_params=pltpu.CompilerParams(has_side_effects=EFFECT))(*arrs, send_sems, recv_sems, after)
    return list(out[:n]), list(out[n:])


def _gather_copies(src, land, send_sems, recv_sems):
    me = _chip_of(_me())
    copies = []
    for i in range(len(src)):
        for r, rel in enumerate(CHIP_RELS):
            k = 3 * i + r
            copies.append(pltpu.make_async_remote_copy(
                src_ref=src[i], dst_ref=land[i].at[me], send_sem=send_sems.at[k], recv_sem=recv_sems.at[k],
                device_id=_peer(rel), device_id_type=MESH))
    return copies


def _small_copies(src, land, send_sems, recv_sems):
    my_slot = _linear(_me())
    return [pltpu.make_async_remote_copy(
        src_ref=src[0], dst_ref=land[0].at[my_slot], send_sem=send_sems.at[k], recv_sem=recv_sems.at[k],
        device_id=_other(flip), device_id_type=MESH) for k, flip in enumerate(OTHERS)]


def _pieces_copies(src, land, send_sems, recv_sems):
    copies = []
    for k, flip in enumerate(OTHERS):
        peer = _other(flip)
        copies.append(pltpu.make_async_remote_copy(
            src_ref=src[0].at[_linear(peer)], dst_ref=land[0].at[k], send_sem=send_sems.at[k],
            recv_sem=recv_sems.at[k], device_id=peer, device_id_type=MESH))
    return copies


def _row_block(rows, cols, budget=2 * 1024 * 1024):
    rb = max(8, (budget // (4 * cols)) // 8 * 8)
    while rows % rb:
        rb -= 8
    return rb if rb > 0 else rows


def _sum_pieces(name, mine, slot, rest):
    _, R, Cc = mine.shape
    K = rest.shape[0]
    rb = _row_block(R, Cc)

    def body(s_ref, f_ref, r_ref, o_ref):
        acc = f_ref[0].astype(F32)
        for j in range(K):
            acc = acc + r_ref[j].astype(F32)
        o_ref[...] = acc

    return pl.pallas_call(
        body, name=name,
        grid_spec=pltpu.PrefetchScalarGridSpec(
            num_scalar_prefetch=1, grid=(R // rb,),
            in_specs=[pl.BlockSpec((1, rb, Cc), lambda i, s: (s[0], i, 0)),
                      pl.BlockSpec((K, rb, Cc), lambda i, s: (0, i, 0))],
            out_specs=pl.BlockSpec((rb, Cc), lambda i, s: (i, 0))),
        out_shape=jax.ShapeDtypeStruct((R, Cc), F32), compiler_params=_cp("parallel"))(slot, mine, rest)


def _adamw_math(w, gv, m, v):
    mn = ADAM_B1 * m + (1.0 - ADAM_B1) * gv
    vn = ADAM_B2 * v + (1.0 - ADAM_B2) * (gv * gv)
    m_hat = mn / (1.0 - ADAM_B1 ** ADAM_STEP)
    v_hat = vn / (1.0 - ADAM_B2 ** ADAM_STEP)
    return -ADAM_LR * (m_hat / (jnp.sqrt(v_hat) + ADAM_EPS) + ADAM_WD * w), mn, vn


def _adamw_halves(name, w, mine, theirs, m, v, core):
    R, Cc = w.shape
    r2 = R // 2
    rb = _row_block(r2, Cc, 1024 * 1024)
    nb2 = r2 // rb

    def body(c_ref, w_ref, mine_ref, theirs_ref, m_ref, v_ref, g_ref, d_ref, mo_ref, vo_ref):
        is_mine = (pl.program_id(0) // nb2) == c_ref[0]
        gv = jnp.where(is_mine, mine_ref[...], theirs_ref[...])
        g_ref[...] = gv
        d_ref[...], mo_ref[...], vo_ref[...] = _adamw_math(w_ref[...], gv, m_ref[...], v_ref[...])

    blk = pl.BlockSpec((rb, Cc), lambda i, c: (i, 0))
    half = lambda own: pl.BlockSpec(
        (rb, Cc), lambda i, c: (jnp.clip(i - (c[0] if own else 1 - c[0]) * nb2, 0, nb2 - 1), 0))
    return pl.pallas_call(
        body, name=name,
        grid_spec=pltpu.PrefetchScalarGridSpec(
            num_scalar_prefetch=1, grid=(2 * nb2,), in_specs=[blk, half(True), half(False), blk, blk],
            out_specs=[blk] * 4),
        out_shape=[jax.ShapeDtypeStruct((R, Cc), F32)] * 4, compiler_params=_cp("parallel"))(core, w, mine, theirs, m, v)


def _adamw_transposed(name, wt, mine, theirs, mt, vt, core):
    Cc, kh_n, _ = wt.shape
    r2 = mine.shape[0]
    per_half = kh_n // 2
    nb = -(-Cc // LANES)

    def body(c_ref, w_ref, mine_ref, theirs_ref, m_ref, v_ref, g_ref, d_ref, mo_ref, vo_ref):
        first = c_ref[0] == 0
        halves = (jnp.where(first, mine_ref[...], theirs_ref[...]).T,
                  jnp.where(first, theirs_ref[...], mine_ref[...]).T)
        for kh in range(kh_n):
            lo = (kh % per_half) * LANES
            g_ref[:, kh, :] = halves[kh // per_half][:, lo:lo + LANES]
        d_ref[...], mo_ref[...], vo_ref[...] = _adamw_math(w_ref[...], g_ref[...], m_ref[...], v_ref[...])

    blk = pl.BlockSpec((LANES, kh_n, LANES), lambda i, c: (i, 0, 0))
    half = pl.BlockSpec((r2, LANES), lambda i, c: (0, i))
    return pl.pallas_call(
        body, name=name,
        grid_spec=pltpu.PrefetchScalarGridSpec(
            num_scalar_prefetch=1, grid=(nb,), in_specs=[blk, half, half, blk, blk], out_specs=[blk] * 4),
        out_shape=[jax.ShapeDtypeStruct(wt.shape, F32)] * 4, compiler_params=_cp("parallel"))(
            core, wt, mine, theirs, mt, vt)


FF_W = 2 * D_FF
FF_CH = FF_W // LANES
DNC_W = 3 * DN_WIDTH
DNC_CH = DNC_W // LANES
E_ONG, E_SGN, E_SGW, E_SGBT = 0, 1, 8, 8 + SG_GROUPS * SG_BLOCK
E_FFN = E_SGBT + SG_BLOCK
E_FCW = E_FFN + D_MODEL // LANES
E_FCB = E_FCW + 3 * FF_CH
E_FIN = E_FCB + FF_CH
EARLY_ROWS = E_FIN + D_MODEL // LANES
L_ATTN, L_DNC = 0, D_MODEL // LANES
L_AD = L_DNC + 4 * DNC_CH
L_LOSS = L_AD + 2
LATE_ROWS = -(-(L_LOSS + 1) // 8) * 8


def _put_rows(out, r0, x):
    k, width = x.shape
    n = width // LANES
    for t in range(k):
        for j in range(n):
            out[r0 + t * n + j:r0 + t * n + j + 1, :] = x[t:t + 1, j * LANES:(j + 1) * LANES]


def _pack_early(ong, sgn, sgw, sgbt, ffn, fcw, fcb, fin):
    def body(ong_ref, sgn_ref, sgw_ref, sgbt_ref, ffn_ref, fcw_ref, fcb_ref, fin_ref, out):
        out[...] = jnp.zeros_like(out)
        _put_rows(out, E_ONG, ong_ref)
        _put_rows(out, E_SGN, sgn_ref)
        for gi in range(SG_GROUPS):
            out[E_SGW + gi * SG_BLOCK:E_SGW + (gi + 1) * SG_BLOCK, :] = sgw_ref[gi]
        out[E_SGBT:E_SGBT + SG_BLOCK, :] = sgbt_ref[...]
        _put_rows(out, E_FFN, ffn_ref)
        _put_rows(out, E_FCW, fcw_ref)
        _put_rows(out, E_FCB, fcb_ref)
        _put_rows(out, E_FIN, fin_ref)

    return pl.pallas_call(body, name="pack_small_early", out_shape=jax.ShapeDtypeStruct((EARLY_ROWS, LANES), F32))(
        ong, sgn, sgw, sgbt, ffn, fcw, fcb, fin)


def _pack_late(attn, dnc, ad, loss_row):
    def body(attn_ref, dnc_ref, ad_ref, loss_ref, out):
        out[...] = jnp.zeros_like(out)
        _put_rows(out, L_ATTN, attn_ref)
        _put_rows(out, L_DNC, dnc_ref)
        out[L_AD:L_AD + 2, :] = ad_ref[...]
        out[L_LOSS:L_LOSS + 1, :] = loss_ref[...]

    return pl.pallas_call(body, name="pack_small_late", out_shape=jax.ShapeDtypeStruct((LATE_ROWS, LANES), F32))(
        attn, dnc, ad, loss_row)


SMALL = ("attn_norm_g", "dn_a_log", "dn_dt_bias", "dn_out_norm_g", "sg_norm_g", "sg_w", "sg_b", "ffn_norm_g",
         "ffn_conv_b", "final_norm_g", "dn_conv_w", "ffn_conv_w")


def _small_update(early_all, late_all, chip, W, M, V):
    n = len(SMALL)
    arrs = [d[k] for d in (W, M, V) for k in SMALL]

    def body(c_ref, e_ref, l_ref, *refs):
        w_, m_, v_ = refs[:n], refs[n:2 * n], refs[2 * n:3 * n]
        loss_ref = refs[3 * n]
        outs = refs[3 * n + 1:]
        g_, d_, mo_, vo_ = outs[:n], outs[n:2 * n], outs[2 * n:3 * n], outs[3 * n:4 * n]
        chip_i = c_ref[0]

        def total(ref, r0, rows=1):
            acc = ref[0, pl.ds(r0, rows), :]
            for s in range(1, 8):
                acc = acc + ref[s, pl.ds(r0, rows), :]
            return acc

        def update(i, idx, g):
            g_[i][idx] = g
            d_[i][idx], mo_[i][idx], vo_[i][idx] = _adamw_math(w_[i][idx], g, m_[i][idx], v_[i][idx])

        def rows_param(name, ref, r0, width):
            i = SMALL.index(name)
            for j in range(width // LANES):
                update(i, (slice(None), slice(j * LANES, (j + 1) * LANES)), total(ref, r0 + j))

        rows_param("attn_norm_g", l_ref, L_ATTN, D_MODEL)
        ad = (total(l_ref, L_AD), total(l_ref, L_AD + 1))
        update(SMALL.index("dn_a_log"), (slice(None), slice(None)), ad[0][:, N_HEADS:2 * N_HEADS])
        update(SMALL.index("dn_dt_bias"), (slice(None), slice(None)), ad[1][:, N_HEADS:2 * N_HEADS])
        rows_param("dn_out_norm_g", e_ref, E_ONG, HEAD_DIM)
        rows_param("sg_norm_g", e_ref, E_SGN, SG_WIDTH)
        sgbt = total(e_ref, E_SGBT, SG_BLOCK).T
        for gi in range(SG_GROUPS):
            update(SMALL.index("sg_w"), (0, gi), total(e_ref, E_SGW + gi * SG_BLOCK, SG_BLOCK))
            update(SMALL.index("sg_b"), (0, slice(gi, gi + 1), slice(None)), sgbt[gi:gi + 1, :])
        rows_param("ffn_norm_g", e_ref, E_FFN, D_MODEL)
        rows_param("ffn_conv_b", e_ref, E_FCB, FF_W)
        rows_param("final_norm_g", e_ref, E_FIN, D_MODEL)
        for name, ref, r0, taps, chunks in (("dn_conv_w", l_ref, L_DNC, 4, DNC_CH), ("ffn_conv_w", e_ref, E_FCW, 3, FF_CH)):
            mine = chunks // 4
            for t in range(taps):
                for j in range(mine):
                    update(SMALL.index(name), (0, slice(t, t + 1), slice(j * LANES, (j + 1) * LANES)),
                           total(ref, r0 + t * chunks + chip_i * mine + j))
        loss_ref[...] = total(l_ref, L_LOSS)

    full = lambda a: pl.BlockSpec(a.shape, lambda i, c, nd=a.ndim: (0,) * nd)
    shapes = [jax.ShapeDtypeStruct(W[k].shape, F32) for k in SMALL]
    outs = pl.pallas_call(
        body, name="small_update",
        grid_spec=pltpu.PrefetchScalarGridSpec(
            num_scalar_prefetch=1, grid=(1,), in_specs=[full(early_all), full(late_all)] + [full(a) for a in arrs],
            out_specs=[pl.BlockSpec((1, LANES), lambda i, c: (0, 0))] + [full(s) for s in shapes] * 4),
        out_shape=[jax.ShapeDtypeStruct((1, LANES), F32)] + shapes * 4,
        compiler_params=pltpu.CompilerParams(vmem_limit_bytes=VMEM_LIMIT))(chip, early_all, late_all, *arrs)
    loss, outs = outs[0], outs[1:]
    return (loss,) + tuple(dict(zip(SMALL, outs[k * n:(k + 1) * n])) for k in range(4))


ORDER =("attn_norm_g", "w_in", "dn_conv_w", "dn_a_log", "dn_dt_bias", "dn_out_norm_g", "sg_norm_g", "sg_w",
         "sg_b", "w_out", "ffn_norm_g", "w_up", "ffn_conv_w", "ffn_conv_b", "w_down", "final_norm_g")


def kernel(x, attn_norm_g, w_in, dn_conv_w, dn_a_log, dn_dt_bias, dn_out_norm_g, sg_norm_g, sg_w, sg_b, w_out, ffn_norm_g, w_up, ffn_conv_w, ffn_conv_b, w_down, final_norm_g, loss_target, m_attn_norm_g, m_w_in, m_dn_conv_w, m_dn_a_log, m_dn_dt_bias, m_dn_out_norm_g, m_sg_norm_g, m_sg_w, m_sg_b, m_w_out, m_ffn_norm_g, m_w_up, m_ffn_conv_w, m_ffn_conv_b, m_w_down, m_final_norm_g, v_attn_norm_g, v_w_in, v_dn_conv_w, v_dn_a_log, v_dn_dt_bias, v_dn_out_norm_g, v_sg_norm_g, v_sg_w, v_sg_b, v_w_out, v_ffn_norm_g, v_w_up, v_ffn_conv_w, v_ffn_conv_b, v_w_down, v_final_norm_g):
    W = dict(attn_norm_g=attn_norm_g, w_in=w_in, dn_conv_w=dn_conv_w, dn_a_log=dn_a_log, dn_dt_bias=dn_dt_bias,
             dn_out_norm_g=dn_out_norm_g, sg_norm_g=sg_norm_g, sg_w=sg_w, sg_b=sg_b, w_out=w_out,
             ffn_norm_g=ffn_norm_g, w_up=w_up, ffn_conv_w=ffn_conv_w, ffn_conv_b=ffn_conv_b, w_down=w_down,
             final_norm_g=final_norm_g)
    Mo = dict(attn_norm_g=m_attn_norm_g, w_in=m_w_in, dn_conv_w=m_dn_conv_w, dn_a_log=m_dn_a_log,
              dn_dt_bias=m_dn_dt_bias, dn_out_norm_g=m_dn_out_norm_g, sg_norm_g=m_sg_norm_g, sg_w=m_sg_w,
              sg_b=m_sg_b, w_out=m_w_out, ffn_norm_g=m_ffn_norm_g, w_up=m_w_up, ffn_conv_w=m_ffn_conv_w,
              ffn_conv_b=m_ffn_conv_b, w_down=m_w_down, final_norm_g=m_final_norm_g)
    Vo = dict(attn_norm_g=v_attn_norm_g, w_in=v_w_in, dn_conv_w=v_dn_conv_w, dn_a_log=v_dn_a_log,
              dn_dt_bias=v_dn_dt_bias, dn_out_norm_g=v_dn_out_norm_g, sg_norm_g=v_sg_norm_g, sg_w=v_sg_w,
              sg_b=v_sg_b, w_out=v_w_out, ffn_norm_g=v_ffn_norm_g, w_up=v_w_up, ffn_conv_w=v_ffn_conv_w,
              ffn_conv_b=v_ffn_conv_b, w_down=v_w_down, final_norm_g=v_final_norm_g)
    xi, yi, ci = lax.axis_index("x"), lax.axis_index("y"), lax.axis_index("c")
    chip = 2 * xi + yi

    me_lin = 4 * xi + 2 * yi + ci

    g_in, g_dnc = _gather_first(w_in[0].astype(BF16), dn_conv_w[0])
    def start_gather(name, shards, after):
        lands = [lax.dynamic_update_index_in_dim(lax.empty((4,) + s.shape, s.dtype), s, chip, 0) for s in shards]
        return _transfer_start(name, shards, lands, 3 * len(shards), _gather_copies, after=after)

    mid = start_gather("gather_mid_start", [w_out[0].astype(BF16), w_up[0].astype(BF16), ffn_conv_w[0]], g_in)
    last = start_gather("gather_last_start", [w_down[0].astype(BF16)], mid[4])
    token = last[4]

    def late_weights(stage, after):
        if stage == "out_proj":
            _, (g_out, g_up, g_ffc) = _transfer_wait("gather_mid_wait", *mid[:4], _gather_copies, after)
            return dict(w_out=g_out.reshape(D_MODEL, D_MODEL), ffn_conv_w=g_ffc.transpose(1, 0, 2).reshape(3, 2 * D_FF),
                        w_up=g_up.transpose(1, 0, 2).reshape(D_MODEL, 2 * D_FF))
        _, (g_down,) = _transfer_wait("gather_last_wait", *last[:4], _gather_copies, after)
        return dict(w_down=g_down.reshape(D_FF, D_MODEL))

    full = dict(
        w_in=g_in,
        dn_conv_w=g_dnc.transpose(1, 0, 2).reshape(4, 3 * DN_WIDTH),
        attn_norm_g=attn_norm_g, dn_a_log=dn_a_log, dn_dt_bias=dn_dt_bias, dn_out_norm_g=dn_out_norm_g,
        sg_norm_g=sg_norm_g, sg_w=sg_w[0], sg_b=sg_b[0], ffn_norm_g=ffn_norm_g, ffn_conv_b=ffn_conv_b,
        final_norm_g=final_norm_g[None])

    pending = {}

    def on_grad(name, gw):
        if name == "small_early":
            buf = _pack_early(gw["dn_out_norm_g"], gw["sg_norm_g"], gw["sg_w"], gw["sg_bt"], gw["ffn_norm_g"],
                              gw["ffn_conv_w"], gw["ffn_conv_b"], gw["final_norm_g"])
            land = lax.dynamic_update_index_in_dim(lax.empty((8,) + buf.shape, F32), buf, me_lin, 0)
            s_sem, r_sem, src, lands, tok = _transfer_start("small_early_start", [buf], [land], 7, _small_copies)
            pending[name] = (s_sem, r_sem, src, lands)
            return tok
        g8 = gw.reshape(8, -1, gw.shape[-1])
        land = lax.empty((7,) + g8.shape[1:], BF16)
        s_sem, r_sem, src, lands, tok = _transfer_start(f"reduce_{name}_start", [g8], [land], 7, _pieces_copies)
        pending[name] = (s_sem, r_sem, src, lands)
        return tok

    loss_row, grad_x, g = _local_step(x[0], loss_target[0], full, dep=token, late_weights=late_weights,
                                      on_grad=on_grad)

    late_all = _exchange_small(_pack_late(g["attn_norm_g"], g["dn_conv_w"], g["a_dt"], loss_row))
    s_sem, r_sem, src, lands = pending["small_early"]
    _, (early_all,) = _transfer_wait("small_early_wait", s_sem, r_sem, src, lands, _small_copies, grad_x)
    row = lambda d: {k: (d[k].reshape(1, -1) if k == "final_norm_g" else d[k]) for k in SMALL}
    loss_sum, *small_out = _small_update(early_all, late_all, chip.astype(jnp.int32).reshape(1), row(W), row(Mo), row(Vo))
    loss = loss_sum[0, 0]

    def summed_half(n, after):
        s_sem, r_sem, src, lands = pending[n]
        sent, got = _transfer_wait(f"reduce_{n}_wait", s_sem, r_sem, src, lands, _pieces_copies, after)
        return _sum_pieces(f"sum_{n}", sent[0], me_lin.astype(jnp.int32).reshape(1), got[0])

    first3 = ("w_down", "w_up", "w_out")
    halves = [summed_half(n, grad_x) for n in first3]
    theirs = _pair_swap("pair_swap", halves)
    core = ci.astype(jnp.int32).reshape(1)
    grads, delta, new_m, new_v = {}, {}, {}, {}
    for n, mine_h, their_h in zip(first3, halves, theirs):
        shp = W[n].shape
        gr, d, mn, vn = _adamw_halves(f"adamw_{n}", W[n][0], mine_h, their_h, Mo[n][0], Vo[n][0], core)
        grads[n], delta[n], new_m[n], new_v[n] = gr.reshape(shp), d.reshape(shp), mn.reshape(shp), vn.reshape(shp)
    mine_h = summed_half("w_in", delta["w_out"])
    (their_h,) = _pair_swap("pair_swap_w_in", [mine_h])
    shp = w_in.shape
    to_t = lambda a: a.reshape(shp[1] // LANES, LANES, shp[2]).transpose(2, 0, 1)
    from_t = lambda a: a.transpose(1, 2, 0).reshape(shp)
    outs = _adamw_transposed("adamw_w_in", to_t(w_in), mine_h, their_h, to_t(m_w_in), to_t(v_w_in), core)
    grads["w_in"], delta["w_in"], new_m["w_in"], new_v["w_in"] = (from_t(o) for o in outs)
    for dst, src_d in zip((grads, delta, new_m, new_v), small_out):
        dst.update({k: (a.reshape(W[k].shape) if k == "final_norm_g" else a) for k, a in src_d.items()})

    return (loss, grad_x[None], *[grads[n] for n in ORDER], *[delta[n] for n in ORDER],
            *[new_m[n] for n in ORDER], *[new_v[n] for n in ORDER])
```

```python
import functools
import math

import jax
import jax.numpy as jnp
from jax import lax
from jax.experimental import pallas as pl
from jax.experimental.pallas import tpu as pltpu

F32 = jnp.float32
BF16 = jnp.bfloat16

D_MODEL = 1024
CHUNK = 64
SCAN_CHUNKS = 8
SCAN_CHUNKS_FWD = 8
HEAD_DIM = 128
N_HEADS = 4
DN_WIDTH = 512
SG_WIDTH = 512
SG_GROUPS = 4
SG_BLOCK = 128
D_FF = 2816
PROJ_COLS = 3080
PROJ_PAD = 3200
BA_COL = 3072
EPS = 1e-6
NEG = -1e30
VMEM_LIMIT = 56 * 1024 * 1024

ADAM_LR = 0.001
ADAM_B1 = 0.9
ADAM_B2 = 0.999
ADAM_EPS = 1e-08
ADAM_WD = 0.01
ADAM_STEP = 10

MESH = pl.DeviceIdType.MESH
ANY = pl.BlockSpec(memory_space=pl.ANY)


def _cp(*sem):
    return pltpu.CompilerParams(dimension_semantics=sem, vmem_limit_bytes=VMEM_LIMIT)


def _bf(a):
    return a.astype(BF16)


def _nn(a, b):
    return jnp.dot(_bf(a), _bf(b), preferred_element_type=F32)


def _nt(a, b):
    return lax.dot_general(_bf(a), _bf(b), (((1,), (1,)), ((), ())), preferred_element_type=F32)


def _tn(a, b):
    return lax.dot_general(_bf(a), _bf(b), (((0,), (0,)), ((), ())), preferred_element_type=F32)


def _split(a):
    hi = _bf(a)
    return hi, _bf(a - hi.astype(F32))


def _sigmoid(x):
    return 0.5 * jnp.tanh(0.5 * x) + 0.5


def _silu(x):
    return x * _sigmoid(x)


def _dsilu(x):
    s = _sigmoid(x)
    return s * (1.0 + x * (1.0 - s))


_GELU_C = math.sqrt(2.0 / math.pi)
_GELU_A = 0.044715


def _gelu(x):
    return 0.5 * x * (1.0 + jnp.tanh(_GELU_C * (x + _GELU_A * x * x * x)))


def _dgelu(x):
    t = jnp.tanh(_GELU_C * (x + _GELU_A * x * x * x))
    return 0.5 * (1.0 + t) + 0.5 * x * (1.0 - t * t) * _GELU_C * (1.0 + 3.0 * _GELU_A * x * x)


def _softplus(x):
    return jnp.maximum(x, 0.0) + jnp.log(1.0 + jnp.exp(-jnp.abs(x)))


def _with_dep(in_specs, args, dep):
    if dep is None:
        return in_specs, args
    return in_specs + [ANY], args + [dep]


SUB_ROWS = 128


def _sub_blocks(tm):
    return [slice(r0, min(r0 + SUB_ROWS, tm)) for r0 in range(0, tm, SUB_ROWS)]


def _rms_hat(xv):
    r = lax.rsqrt(jnp.mean(xv * xv, axis=-1, keepdims=True) + EPS)
    return xv * r, r


def _rms_bwd_vals(dh, xh, r, g):
    dxh = dh * g
    return r * (dxh - xh * jnp.mean(dxh * xh, axis=-1, keepdims=True)), jnp.sum(dh * xh, axis=0, keepdims=True)


def _in_proj(x, g, w4, tm=512, dep=None):
    T, K = x.shape
    ng, _, wc = w4.shape
    tm = min(tm, T)

    def body(x_ref, g_ref, w4_ref, *rest):
        p_ref, h_ref, w_ref = rest[-3:]

        @pl.when(pl.program_id(0) == 0)
        def _():
            w_ref[:, ng * wc:] = jnp.zeros((K, PROJ_PAD - ng * wc), BF16)
            for j in range(ng):
                w_ref[:, j * wc:(j + 1) * wc] = w4_ref[j]
        for r in _sub_blocks(tm):
            xh, _ = _rms_hat(x_ref[r, :])
            h_ref[r, :] = (xh * g_ref[...]).astype(BF16)
        p_ref[...] = jnp.dot(h_ref[...], w_ref[...], preferred_element_type=F32)

    in_specs, args = _with_dep(
        [pl.BlockSpec((tm, K), lambda i: (i, 0)), pl.BlockSpec((1, K), lambda i: (0, 0)),
         pl.BlockSpec((ng, K, wc), lambda i: (0, 0, 0))], [x, g, w4], dep)
    return pl.pallas_call(
        body, name="in_proj", grid=(T // tm,), in_specs=in_specs,
        out_specs=[pl.BlockSpec((tm, PROJ_PAD), lambda i: (i, 0)), pl.BlockSpec((tm, K), lambda i: (i, 0)),
                   pl.BlockSpec((K, PROJ_PAD), lambda i: (0, 0))],
        out_shape=[jax.ShapeDtypeStruct((T, PROJ_PAD), F32), jax.ShapeDtypeStruct((T, K), BF16),
                   jax.ShapeDtypeStruct((K, PROJ_PAD), BF16)],
        compiler_params=_cp("arbitrary"))(*args)


def _out_proj(mix, w, x, g, tm=512):
    T, K = mix.shape
    Dm = w.shape[1]
    tm = min(tm, T)

    def body(a_ref, w_ref, x_ref, g_ref, x2_ref, h_ref):
        x2_ref[...] = _nn(a_ref[...], w_ref[...]) + x_ref[...]
        for r in _sub_blocks(tm):
            xh, _ = _rms_hat(x2_ref[r, :])
            h_ref[r, :] = (xh * g_ref[...]).astype(BF16)

    row = lambda width: pl.BlockSpec((tm, width), lambda i: (i, 0))
    return pl.pallas_call(
        body, name="out_proj", grid=(T // tm,),
        in_specs=[row(K), pl.BlockSpec((K, Dm), lambda i: (0, 0)), row(Dm), pl.BlockSpec((1, Dm), lambda i: (0, 0))],
        out_specs=[row(Dm), row(Dm)],
        out_shape=[jax.ShapeDtypeStruct((T, Dm), F32), jax.ShapeDtypeStruct((T, Dm), BF16)],
        compiler_params=_cp("parallel"))(mix, w, x, g)


def _down_proj_loss(act, w, x2, tgt, g, tm=512):
    T, K = act.shape
    Dm = w.shape[1]
    tm = min(tm, T)

    def body(a_ref, w_ref, x_ref, t_ref, g_ref, loss_ref, dx_ref, gg_ref):
        @pl.when(pl.program_id(0) == 0)
        def _():
            gg_ref[...] = jnp.zeros_like(gg_ref)
            loss_ref[...] = jnp.zeros_like(loss_ref)
        dx_ref[...] = _nn(a_ref[...], w_ref[...]) + x_ref[...]
        for r in _sub_blocks(tm):
            xh, rr = _rms_hat(dx_ref[r, :])
            e = xh * g_ref[...] - t_ref[r, :]
            loss_ref[...] += jnp.zeros_like(loss_ref) + (0.5 / Dm) * jnp.sum(e * e)
            dx, gg = _rms_bwd_vals(e * (1.0 / Dm), xh, rr, g_ref[...])
            dx_ref[r, :] = dx
            gg_ref[...] += gg

    row = lambda width: pl.BlockSpec((tm, width), lambda i: (i, 0))
    vec = pl.BlockSpec((1, Dm), lambda i: (0, 0))
    return pl.pallas_call(
        body, name="down_proj_loss", grid=(T // tm,),
        in_specs=[row(K), pl.BlockSpec((K, Dm), lambda i: (0, 0)), row(Dm), row(Dm), vec],
        out_specs=[pl.BlockSpec((1, 128), lambda i: (0, 0)), row(Dm), vec],
        out_shape=[jax.ShapeDtypeStruct((1, 128), F32), jax.ShapeDtypeStruct((T, Dm), F32),
                   jax.ShapeDtypeStruct((1, Dm), F32)],
        compiler_params=_cp("arbitrary"))(act, w, x2, tgt, g)


def _mm_nt_rms_bwd(name, a, b, x, g, dres, tm=512, dep=None):
    M, K = a.shape
    Dm = b.shape[0]
    tm = min(tm, M)

    def body(a_ref, b_ref, x_ref, g_ref, dres_ref, *rest):
        dx_ref, gg_ref = rest[-2:]

        @pl.when(pl.program_id(0) == 0)
        def _():
            gg_ref[...] = jnp.zeros_like(gg_ref)
        dx_ref[...] = _nt(a_ref[...], b_ref[...])
        for r in _sub_blocks(tm):
            xh, rr = _rms_hat(x_ref[r, :])
            dx, gg = _rms_bwd_vals(dx_ref[r, :], xh, rr, g_ref[...])
            dx_ref[r, :] = dres_ref[r, :] + dx
            gg_ref[...] += gg

    row = lambda width: pl.BlockSpec((tm, width), lambda i: (i, 0))
    vec = pl.BlockSpec((1, Dm), lambda i: (0, 0))
    in_specs, args = _with_dep([row(K), pl.BlockSpec((Dm, K), lambda i: (0, 0)), row(Dm), vec, row(Dm)],
                               [a, b, x, g, dres], dep)
    return pl.pallas_call(
        body, name=name, grid=(M // tm,), in_specs=in_specs, out_specs=[row(Dm), vec],
        out_shape=[jax.ShapeDtypeStruct((M, Dm), F32), jax.ShapeDtypeStruct((1, Dm), F32)],
        compiler_params=_cp("arbitrary"))(*args)


def _mm_nt(name, a, b, out_dtype, tm, tn, dep=None):
    M, K = a.shape
    N = b.shape[0]
    tm, tn = min(tm, M), min(tn, N)

    def body(a_ref, b_ref, *rest):
        o_ref = rest[-1]
        o_ref[...] = _nt(a_ref[...], b_ref[...]).astype(o_ref.dtype)

    in_specs, args = _with_dep(
        [pl.BlockSpec((tm, K), lambda i, j: (i, 0)), pl.BlockSpec((tn, K), lambda i, j: (j, 0))], [a, b], dep)
    return pl.pallas_call(
        body, name=name, grid=(M // tm, N // tn), in_specs=in_specs,
        out_specs=pl.BlockSpec((tm, tn), lambda i, j: (i, j)),
        out_shape=jax.ShapeDtypeStruct((M, N), out_dtype),
        compiler_params=_cp("parallel", "parallel"))(*args)


def _mm_tn(name, a, b, tm, tn, tk, col_major_tiles=False, col_groups=None):
    T, M = a.shape
    N = b.shape[1]
    tm, tn, tk = min(tm, M), min(tn, N), min(tk, T)
    nk = T // tk

    def body(a_ref, b_ref, o_ref, acc_ref):
        k = pl.program_id(2)

        @pl.when(k == 0)
        def _():
            acc_ref[...] = jnp.zeros_like(acc_ref)
        acc_ref[...] += _tn(a_ref[...], b_ref[...])

        @pl.when(k == nk - 1)
        def _():
            if col_groups:
                for j in range(col_groups[0]):
                    o_ref[j] = acc_ref[:, j * col_groups[1]:(j + 1) * col_groups[1]].astype(BF16)
            else:
                o_ref[...] = acc_ref[...].astype(BF16).reshape(o_ref.shape)

    if col_groups:
        assert tm == M and tn == N and col_groups[0] * col_groups[1] <= N
        out_spec = pl.BlockSpec((col_groups[0], M, col_groups[1]), lambda i, j, k: (0, 0, 0))
        out_shape = jax.ShapeDtypeStruct((col_groups[0], M, col_groups[1]), BF16)
    elif col_major_tiles:
        assert tm == M
        out_spec = pl.BlockSpec((1, tm, tn), lambda i, j, k: (j, 0, 0))
        out_shape = jax.ShapeDtypeStruct((N // tn, M, tn), BF16)
    else:
        out_spec = pl.BlockSpec((tm, tn), lambda i, j, k: (i, j))
        out_shape = jax.ShapeDtypeStruct((M, N), BF16)
    return pl.pallas_call(
        body, name=name, grid=(M // tm, N // tn, nk),
        in_specs=[pl.BlockSpec((tk, tm), lambda i, j, k: (k, i)), pl.BlockSpec((tk, tn), lambda i, j, k: (k, j))],
        out_specs=out_spec, out_shape=out_shape, scratch_shapes=[pltpu.VMEM((tm, tn), F32)],
        compiler_params=_cp("parallel", "parallel", "arbitrary"))(a, b)


def _halo_prev_spec(rb, width):
    return pl.BlockSpec((8, width), lambda i: (jnp.maximum(i * (rb // 8) - 1, 0), 0))


def _halo_next_spec(rb, width, T):
    return pl.BlockSpec((8, width), lambda i: (jnp.minimum((i + 1) * (rb // 8), T // 8 - 1), 0))


LANES = 128
FF_STRIPS = D_FF // LANES
ROW_CHUNK = 32


def _strip(j, base=0):
    return pl.ds(pl.multiple_of(base + j * LANES, LANES), LANES)


def _up_proj_act(h, w_up, w, b, rb=256):
    T, K = h.shape
    W = w_up.shape[1]
    rb = min(rb, T)
    nb = T // rb

    def body(h_ref, wup_ref, w_ref, b_ref, up_ref, act_ref, prev_scr, ext_scr, tail_scr):
        @pl.when(pl.program_id(0) == 0)
        def _():
            tail_scr[...] = jnp.zeros_like(tail_scr)
            prev_scr[...] = jnp.zeros_like(prev_scr)
        up_ref[...] = jnp.dot(h_ref[...], wup_ref[...], preferred_element_type=F32)
        for j in range(FF_STRIPS):
            slot = j % 2
            halves = (slice(j * LANES, (j + 1) * LANES), slice(D_FF + j * LANES, D_FF + (j + 1) * LANES))
            wv = [w_ref[:, cols] for cols in halves]
            bv = [b_ref[:, cols] for cols in halves]
            for hh, cols in enumerate(halves):
                ext_scr[slot, hh, 0:8] = tail_scr[:, cols]
                ext_scr[slot, hh, 8:] = prev_scr[:, cols]
            for r0 in range(0, rb, ROW_CHUNK):
                n = min(ROW_CHUNK, rb - r0)
                c = [ext_scr[slot, hh, 6 + r0:6 + r0 + n] * wv[hh][0:1] + ext_scr[slot, hh, 7 + r0:7 + r0 + n] * wv[hh][1:2]
                     + ext_scr[slot, hh, 8 + r0:8 + r0 + n] * wv[hh][2:3] + bv[hh] for hh in range(2)]
                act_ref[r0:r0 + n, halves[0]] = (_silu(c[0]) * c[1]).astype(BF16)
        tail_scr[...] = prev_scr[rb - 8:rb, :]
        prev_scr[...] = up_ref[...]

    cur = lambda i: (jnp.minimum(i, nb - 1), 0)
    return pl.pallas_call(
        body, name="up_proj_act", grid=(nb + 1,),
        in_specs=[pl.BlockSpec((rb, K), cur), pl.BlockSpec((K, W), lambda i: (0, 0)),
                  pl.BlockSpec((3, W), lambda i: (0, 0)), pl.BlockSpec((1, W), lambda i: (0, 0))],
        out_specs=[pl.BlockSpec((rb, W), cur), pl.BlockSpec((rb, D_FF), lambda i: (jnp.maximum(i - 1, 0), 0))],
        out_shape=[jax.ShapeDtypeStruct((T, W), F32), jax.ShapeDtypeStruct((T, D_FF), BF16)],
        scratch_shapes=[pltpu.VMEM((rb, W), F32), pltpu.VMEM((2, 2, rb + 8, LANES), F32), pltpu.VMEM((8, W), F32)],
        compiler_params=_cp("arbitrary"))(h, w_up, w, b)


def _ffn_act_bwd(up, dact, w, b, rb=256, dep=None):
    T, W = up.shape
    rb = min(rb, T)
    nb = T // rb
    re = rb + 8

    def body(up_ref, prev_ref, next_ref, da_ref, danext_ref, w_ref, b_ref, *rest):
        dup_ref, gw_ref, gb_ref, ext_scr, dc_scr = rest[-5:]
        i = pl.program_id(0)

        @pl.when(i == 0)
        def _():
            gw_ref[...] = jnp.zeros_like(gw_ref)
            gb_ref[...] = jnp.zeros_like(gb_ref)
        last = i == nb - 1

        def fold8(a):
            return jnp.sum(a.reshape(a.shape[0] // 8, 8, LANES), axis=0)

        def strip(j, slot):
            halves = (_strip(j), _strip(j, D_FF))
            wv = [w_ref[:, cols] for cols in halves]
            bv = [b_ref[:, cols] for cols in halves]
            for h, cols in enumerate(halves):
                ext_scr[slot, h,0:8] = jnp.where(i > 0, prev_ref[:, cols], 0.0)
                ext_scr[slot, h,8:8 + rb] = up_ref[:, cols]
                ext_scr[slot, h,8 + rb:] = next_ref[:, cols]
            gb = [jnp.zeros((8, LANES), F32) for _ in range(2)]
            gw = [[jnp.zeros((8, LANES), F32) for _ in range(3)] for _ in range(2)]
            for r0 in range(0, re, ROW_CHUNK):
                n = min(ROW_CHUNK, re - r0)
                tp = [[ext_scr[slot, h,6 + k + r0:6 + k + r0 + n] for k in range(3)] for h in range(2)]
                c = [tp[h][0] * wv[h][0:1] + tp[h][1] * wv[h][1:2] + tp[h][2] * wv[h][2:3] + bv[h] for h in range(2)]
                if r0 < rb:
                    da = da_ref[r0:r0 + n, halves[0]]
                else:
                    da = jnp.where(last, 0.0, danext_ref[:, halves[0]])
                s = _sigmoid(c[0])
                gs = c[0] * s
                dcs = (da * c[1] * (s + gs * (1.0 - s)), da * gs)
                for h in range(2):
                    dc_scr[slot, h,r0:r0 + n] = dcs[h]
                    if r0 < rb:
                        gb[h] = gb[h] + fold8(dcs[h])
                        for k in range(3):
                            gw[h][k] = gw[h][k] + fold8(tp[h][k] * dcs[h])
            for r0 in range(0, rb, ROW_CHUNK):
                n = min(ROW_CHUNK, rb - r0)
                for h, cols in enumerate(halves):
                    dup = (dc_scr[slot, h,r0:r0 + n] * wv[h][2:3] + dc_scr[slot, h,r0 + 1:r0 + 1 + n] * wv[h][1:2]
                           + dc_scr[slot, h,r0 + 2:r0 + 2 + n] * wv[h][0:1])
                    dup_ref[r0:r0 + n, cols] = dup.astype(BF16)
            for h, cols in enumerate(halves):
                gb_ref[:, cols] += jnp.sum(gb[h], axis=0, keepdims=True)
                for k in range(3):
                    gw_ref[k:k + 1, cols] += jnp.sum(gw[h][k], axis=0, keepdims=True)

        def pair(jj, carry):
            strip(2 * jj, 0)
            strip(2 * jj + 1, 1)
            return carry

        lax.fori_loop(0, FF_STRIPS // 2, pair, 0)

    in_specs, args = _with_dep(
        [pl.BlockSpec((rb, W), lambda i: (i, 0)), _halo_prev_spec(rb, W), _halo_next_spec(rb, W, T),
         pl.BlockSpec((rb, D_FF), lambda i: (i, 0)), _halo_next_spec(rb, D_FF, T),
         pl.BlockSpec((3, W), lambda i: (0, 0)), pl.BlockSpec((1, W), lambda i: (0, 0))],
        [up, up, up, dact, dact, w, b], dep)
    return pl.pallas_call(
        body, name="ffn_act_bwd", grid=(nb,), in_specs=in_specs,
        out_specs=[pl.BlockSpec((rb, W), lambda i: (i, 0)), pl.BlockSpec((3, W), lambda i: (0, 0)),
                   pl.BlockSpec((1, W), lambda i: (0, 0))],
        out_shape=[jax.ShapeDtypeStruct((T, W), BF16), jax.ShapeDtypeStruct((3, W), F32),
                   jax.ShapeDtypeStruct((1, W), F32)],
        scratch_shapes=[pltpu.VMEM((2, 2, rb + 16, LANES), F32), pltpu.VMEM((2, 2, re, LANES), F32)],
        compiler_params=_cp("arbitrary"))(*args)


def _lane_iota(shape):
    return lax.broadcasted_iota(jnp.int32, shape, len(shape) - 1)


def _dn_act(p, conv_w, alog_row, dtb_row, rb=256):
    T = p.shape[0]
    rb = min(rb, T)
    W3 = 3 * DN_WIDTH

    def body(p_ref, halo_ref, ba_ref, w_ref, al_ref, dt_ref, q_ref, k_ref, v_ref, bg_ref, ext_scr):
        first = pl.program_id(0) == 0
        outs = (q_ref, k_ref, v_ref)
        for j in range(3 * N_HEADS):
            kind, h = divmod(j, N_HEADS)
            cols = slice(j * HEAD_DIM, (j + 1) * HEAD_DIM)
            cur = p_ref[:, cols]
            ext_scr[j, 0:8] = jnp.where(first, 0.0, halo_ref[:, cols])
            ext_scr[j, 8:] = cur
            wv = w_ref[:, cols]
            s = _silu(ext_scr[j, 5:5 + rb] * wv[0:1] + ext_scr[j, 6:6 + rb] * wv[1:2]
                      + ext_scr[j, 7:7 + rb] * wv[2:3] + cur * wv[3:4])
            if kind < 2:
                scale = HEAD_DIM ** -0.5 if kind == 0 else 1.0
                s = s * (lax.rsqrt(jnp.sum(s * s, axis=-1, keepdims=True) + EPS) * scale)
            outs[kind][:, h * HEAD_DIM:(h + 1) * HEAD_DIM] = s
        ba = ba_ref[...]
        lane = _lane_iota(ba.shape)
        beta = _sigmoid(ba)
        g = -jnp.exp(al_ref[...]) * _softplus(ba + dt_ref[...])
        bg_ref[...] = jnp.where(lane < N_HEADS, beta, jnp.where(lane < 2 * N_HEADS, g, 0.0))

    row512 = pl.BlockSpec((rb, DN_WIDTH), lambda i: (i, 0))
    row128 = pl.BlockSpec((rb, 128), lambda i: (i, 0))
    vec128 = pl.BlockSpec((1, 128), lambda i: (0, 0))
    return pl.pallas_call(
        body, name="dn_act", grid=(T // rb,),
        in_specs=[pl.BlockSpec((rb, W3), lambda i: (i, 0)), _halo_prev_spec(rb, W3),
                  pl.BlockSpec((rb, 128), lambda i: (i, BA_COL // 128)),
                  pl.BlockSpec((4, W3), lambda i: (0, 0)), vec128, vec128],
        out_specs=[row512, row512, row512, row128],
        out_shape=[jax.ShapeDtypeStruct((T, DN_WIDTH), F32)] * 3 + [jax.ShapeDtypeStruct((T, 128), F32)],
        scratch_shapes=[pltpu.VMEM((3 * N_HEADS, rb + 8, HEAD_DIM), F32)],
        compiler_params=_cp("parallel"))(p, p, p, conv_w, alog_row, dtb_row)


def _dn_act_bwd(p, conv_w, alog_row, dtb_row, dq, dk, dv, dbg, dp_mid, rb=256):
    T = p.shape[0]
    rb = min(rb, T)
    nb = T // rb
    re = rb + 8
    W3 = 3 * DN_WIDTH

    def body(p_ref, prev_ref, next_ref, ba_ref, w_ref, al_ref, dt_ref, dq_ref, dqn_ref, dk_ref, dkn_ref,
             dv_ref, dvn_ref, dbg_ref, mid_ref, draw_ref, gw_ref, gad_ref, ext_scr, dc_scr):
        i = pl.program_id(0)
        draw_ref[:, W3:2 * W3] = mid_ref[...]

        @pl.when(i == 0)
        def _():
            gw_ref[...] = jnp.zeros_like(gw_ref)
            gad_ref[...] = jnp.zeros_like(gad_ref)
        row = lax.broadcasted_iota(jnp.int32, (re, 1), 0)
        live = (row < rb) | (i < nb - 1)
        d_refs = ((dq_ref, dqn_ref), (dk_ref, dkn_ref), (dv_ref, dvn_ref))
        for j in range(3 * N_HEADS):
            kind, h = divmod(j, N_HEADS)
            cols = slice(j * HEAD_DIM, (j + 1) * HEAD_DIM)
            hcols = slice(h * HEAD_DIM, (h + 1) * HEAD_DIM)
            ext_scr[j, 0:8] = jnp.where(i > 0, prev_ref[:, cols], 0.0)
            ext_scr[j, 8:8 + rb] = p_ref[:, cols]
            ext_scr[j, 8 + rb:] = next_ref[:, cols]
            tp = [ext_scr[j, 5 + k:5 + k + re] for k in range(4)]
            wv = w_ref[:, cols]
            c = tp[0] * wv[0:1] + tp[1] * wv[1:2] + tp[2] * wv[2:3] + tp[3] * wv[3:4]
            sg = _sigmoid(c)
            s = c * sg
            d_in = jnp.where(live, jnp.concatenate([d_refs[kind][0][:, hcols], d_refs[kind][1][:, hcols]], axis=0), 0.0)
            if kind < 2:
                scale = HEAD_DIM ** -0.5 if kind == 0 else 1.0
                n = lax.rsqrt(jnp.sum(s * s, axis=-1, keepdims=True) + EPS)
                hat = s * n
                d_in = (n * scale) * (d_in - hat * jnp.sum(hat * d_in, axis=-1, keepdims=True))
            dc = d_in * (sg + s * (1.0 - sg))
            dc_scr[j] = dc
            dcc = dc[0:rb]
            draw = (dcc * wv[3:4] + dc_scr[j, 1:1 + rb] * wv[2:3] + dc_scr[j, 2:2 + rb] * wv[1:2]
                    + dc_scr[j, 3:3 + rb] * wv[0:1])
            draw_ref[:, cols] = draw.astype(BF16)
            for k in range(4):
                gw_ref[k:k + 1, cols] += jnp.sum(tp[k][0:rb] * dcc, axis=0, keepdims=True)
        ba = ba_ref[...]
        dbg = dbg_ref[...]
        lane = _lane_iota(ba.shape)
        beta = _sigmoid(ba)
        ea = jnp.exp(al_ref[...])
        z = ba + dt_ref[...]
        d_a = dbg * (-ea) * _sigmoid(z)
        dba = jnp.where(lane < N_HEADS, dbg * beta * (1.0 - beta), jnp.where(lane < 2 * N_HEADS, d_a, 0.0))
        draw_ref[:, BA_COL:] = dba.astype(BF16)
        isg = (lane >= N_HEADS) & (lane < 2 * N_HEADS)
        g = -ea * _softplus(z)
        gad_ref[0:1, :] += jnp.sum(jnp.where(isg, dbg * g, 0.0), axis=0, keepdims=True)
        gad_ref[1:2, :] += jnp.sum(jnp.where(isg, d_a, 0.0), axis=0, keepdims=True)

    row512 = pl.BlockSpec((rb, DN_WIDTH), lambda i: (i, 0))
    row128 = pl.BlockSpec((rb, 128), lambda i: (i, 0))
    vec128 = pl.BlockSpec((1, 128), lambda i: (0, 0))
    next512 = _halo_next_spec(rb, DN_WIDTH, T)
    return pl.pallas_call(
        body, name="dn_act_bwd", grid=(nb,),
        in_specs=[pl.BlockSpec((rb, W3), lambda i: (i, 0)), _halo_prev_spec(rb, W3), _halo_next_spec(rb, W3, T),
                  pl.BlockSpec((rb, 128), lambda i: (i, BA_COL // 128)),
                  pl.BlockSpec((4, W3), lambda i: (0, 0)), vec128, vec128,
                  row512, next512, row512, next512, row512, next512, row128,
                  pl.BlockSpec((rb, W3), lambda i: (i, 0))],
        out_specs=[pl.BlockSpec((rb, PROJ_PAD), lambda i: (i, 0)),
                   pl.BlockSpec((4, W3), lambda i: (0, 0)), pl.BlockSpec((2, 128), lambda i: (0, 0))],
        out_shape=[jax.ShapeDtypeStruct((T, PROJ_PAD), BF16),
                   jax.ShapeDtypeStruct((4, W3), F32), jax.ShapeDtypeStruct((2, 128), F32)],
        scratch_shapes=[pltpu.VMEM((3 * N_HEADS, rb + 16, HEAD_DIM), F32), pltpu.VMEM((3 * N_HEADS, re, HEAD_DIM), F32)],
        compiler_params=_cp("arbitrary"))(p, p, p, p, conv_w, alog_row, dtb_row, dq, dq, dk, dk, dv, dv, dbg, dp_mid)


def _tri(incl):
    ii = lax.broadcasted_iota(jnp.int32, (CHUNK, CHUNK), 0)
    jj = lax.broadcasted_iota(jnp.int32, (CHUNK, CHUNK), 1)
    return ii, jj, ((ii >= jj) if incl else (ii > jj))


def _dn_chunk(k, bg, cb=4):
    T = k.shape[0]
    N = T // CHUNK
    cb = min(cb, N)

    def body(k_ref, bg_ref, gc_ref, gct_ref, l_ref):
        ii, jj, incl = _tri(True)
        tri = incl.astype(F32)
        U = range(cb)
        bgv = [bg_ref[u * CHUNK:(u + 1) * CHUNK, :] for u in U]
        gc = [jnp.dot(tri, bgv[u], precision=lax.Precision.HIGHEST, preferred_element_type=F32) for u in U]
        gct = [gc[u].T for u in U]
        kk = [[None] * N_HEADS for _ in U]
        for u in U:
            gc_ref[u * CHUNK:(u + 1) * CHUNK, :] = gc[u]
            gct_ref[u] = gct[u][0:8]
            for h in range(N_HEADS):
                kh = k_ref[u * CHUNK:(u + 1) * CHUNK, h * HEAD_DIM:(h + 1) * HEAD_DIM]
                kk[u][h] = _nt(kh * bgv[u][:, h:h + 1], kh)
        for u in U:
            for h in range(N_HEADS):
                gcol = gc[u][:, N_HEADS + h:N_HEADS + h + 1]
                grow = gct[u][N_HEADS + h:N_HEADS + h + 1, :]
                l_ref[u, h] = kk[u][h] * jnp.exp(jnp.where(ii > jj, gcol - grow, NEG))

    rows = cb * CHUNK
    return pl.pallas_call(
        body, name="dn_chunk", grid=(N // cb,),
        in_specs=[pl.BlockSpec((rows, DN_WIDTH), lambda n: (n, 0)), pl.BlockSpec((rows, 128), lambda n: (n, 0))],
        out_specs=[pl.BlockSpec((rows, 128), lambda n: (n, 0)), pl.BlockSpec((cb, 8, CHUNK), lambda n: (n, 0, 0)),
                   pl.BlockSpec((cb, N_HEADS, CHUNK, CHUNK), lambda n: (n, 0, 0, 0))],
        out_shape=[jax.ShapeDtypeStruct((T, 128), F32), jax.ShapeDtypeStruct((N, 8, CHUNK), F32),
                   jax.ShapeDtypeStruct((N, N_HEADS, CHUNK, CHUNK), F32)],
        compiler_params=_cp("parallel"))(k, bg)


def _tri_inv(lt):
    S = lt.shape[1]

    def body(l_ref, a_ref):
        sub = lax.broadcasted_iota(jnp.int32, (8, S), 0)
        groups = CHUNK // 8
        for i in range(CHUNK):
            acc = [((sub + 8 * k) == i).astype(F32) for k in range(groups)]
            for jb in range((i + 7) // 8):
                nk = jb + 1

                def step(j, carry, nk=nk, i=i):
                    lrow = l_ref[pl.ds(i * CHUNK + j, 1), :]
                    return tuple(carry[k] - lrow * a_ref[j, 8 * k:8 * k + 8, :] for k in range(nk))

                acc[:nk] = list(lax.fori_loop(8 * jb, min(8 * jb + 8, i), step, tuple(acc[:nk])))
            for k in range(groups):
                a_ref[i, 8 * k:8 * k + 8, :] = acc[k]

    return pl.pallas_call(
        body, name="tri_inv", out_shape=jax.ShapeDtypeStruct((CHUNK, CHUNK, S), F32),
        compiler_params=pltpu.CompilerParams(vmem_limit_bytes=VMEM_LIMIT))(lt)


def _dn_head_terms(qh, kh, vh, beta, gcol, grow):
    ii, jj, incl = _tri(True)
    gam = jnp.exp(jnp.where(incl, gcol - grow, NEG))
    glast = grow[:, CHUNK - 1:CHUNK]
    cd = jnp.exp(glast)
    shape = (CHUNK, HEAD_DIM)
    E = jnp.broadcast_to(jnp.exp(gcol), shape)
    Fd = jnp.broadcast_to(jnp.exp(glast - gcol), shape)
    beta = jnp.broadcast_to(beta, shape)
    kb = kh * beta
    return dict(ii=ii, jj=jj, gam=gam, E=E, F=Fd, beta=beta, cd=cd, kb=kb, vb=vh * beta, W=kb * E, qE=qh * E,
                kt=kh * Fd)


def _apply_a(a, u):
    hi, lo = _split(a)
    ub = _bf(u)
    return jnp.dot(hi, ub, preferred_element_type=F32) + jnp.dot(lo, ub, preferred_element_type=F32)


def _dn_scan(q, k, v, bg, gc, gct, a):
    T = q.shape[0]
    N = T // CHUNK
    cb = min(SCAN_CHUNKS_FWD, N)

    def body(q_ref, k_ref, v_ref, bg_ref, gc_ref, gct_ref, a_ref, o_ref, sall_ref, s_ref):
        @pl.when(pl.program_id(0) == 0)
        def _():
            s_ref[...] = jnp.zeros_like(s_ref)
        H = range(N_HEADS)
        sl = [slice(h * HEAD_DIM, (h + 1) * HEAD_DIM) for h in H]
        pre = []
        for u in range(cb):
            r = slice(u * CHUNK, (u + 1) * CHUNK)
            bgv, gcv, gctv = bg_ref[r, :], gc_ref[r, :], gct_ref[u]
            q_, k_ = [q_ref[r, s] for s in sl], [k_ref[r, s] for s in sl]
            t = [_dn_head_terms(q_[h], k_[h], v_ref[r, sl[h]], bgv[:, h:h + 1],
                                gcv[:, N_HEADS + h:N_HEADS + h + 1], gctv[N_HEADS + h:N_HEADS + h + 1, :]) for h in H]
            P = [_nt(q_[h], k_[h]) * t[h]["gam"] for h in H]
            pre.append((r, t, P))
        S = [s_ref[h] for h in H]
        for u in range(cb):
            r, t, P = pre[u]
            for h in H:
                sall_ref[u, h] = S[h]
            WS = [_nn(t[h]["W"], S[h]) for h in H]
            qS = [_nn(t[h]["qE"], S[h]) for h in H]
            vn = [_apply_a(a_ref[u, h], t[h]["vb"] - WS[h]) for h in H]
            Pv = [_nn(P[h], vn[h]) for h in H]
            kv = [_tn(t[h]["kt"], vn[h]) for h in H]
            for h in H:
                o_ref[r, sl[h]] = qS[h] + Pv[h]
            S = [t[h]["cd"] * S[h] + kv[h] for h in H]
        for h in H:
            s_ref[h] = S[h]

    row512 = pl.BlockSpec((cb * CHUNK, DN_WIDTH), lambda n: (n, 0))
    row128 = pl.BlockSpec((cb * CHUNK, 128), lambda n: (n, 0))
    return pl.pallas_call(
        body, name="dn_scan", grid=(N // cb,),
        in_specs=[row512, row512, row512, row128, row128, pl.BlockSpec((cb, 8, CHUNK), lambda n: (n, 0, 0)),
                  pl.BlockSpec((cb, N_HEADS, CHUNK, CHUNK), lambda n: (n, 0, 0, 0))],
        out_specs=[row512, pl.BlockSpec((cb, N_HEADS, HEAD_DIM, HEAD_DIM), lambda n: (n, 0, 0, 0))],
        out_shape=[jax.ShapeDtypeStruct((T, DN_WIDTH), F32),
                   jax.ShapeDtypeStruct((N, N_HEADS, HEAD_DIM, HEAD_DIM), F32)],
        scratch_shapes=[pltpu.VMEM((N_HEADS, HEAD_DIM, HEAD_DIM), F32)],
        compiler_params=_cp("arbitrary"))(q, k, v, bg, gc, gct, a)


def _dn_scan_bwd(q, k, v, bg, gc, gct, a, a_t, sall, do, dep=None):
    T = q.shape[0]
    N = T // CHUNK

    cb = min(SCAN_CHUNKS, N)
    nb = N // cb

    def body(q_ref, k_ref, v_ref, bg_ref, gc_ref, gct_ref, a_ref, at_ref, sall_ref, do_ref, *rest):
        dq_ref, dk_ref, dv_ref, dbg_ref, ds_ref = rest[-5:]
        @pl.when(pl.program_id(0) == 0)
        def _():
            ds_ref[...] = jnp.zeros_like(ds_ref)
        lane = _lane_iota((CHUNK, 128))
        rowi = lax.broadcasted_iota(jnp.int32, (CHUNK, 1), 0)
        ii, jj, _ = _tri(True)
        rev = (jj >= ii).astype(F32)
        H = range(N_HEADS)
        sl = [slice(h * HEAD_DIM, (h + 1) * HEAD_DIM) for h in H]
        pre = {}
        for u in reversed(range(cb)):
            r = slice(u * CHUNK, (u + 1) * CHUNK)
            bgv, gcv, gctv = bg_ref[r, :], gc_ref[r, :], gct_ref[u]
            q_, k_, v_ = [q_ref[r, s] for s in sl], [k_ref[r, s] for s in sl], [v_ref[r, s] for s in sl]
            dO = [do_ref[r, s] for s in sl]
            t = [_dn_head_terms(q_[h], k_[h], v_[h], bgv[:, h:h + 1], gcv[:, N_HEADS + h:N_HEADS + h + 1],
                                gctv[N_HEADS + h:N_HEADS + h + 1, :]) for h in H]
            beta = [t[h]["beta"] for h in H]
            S = [sall_ref[u, h] for h in H]
            A = [a_ref[u, h] for h in H]
            WS = [_nn(t[h]["W"], S[h]) for h in H]
            KK = [_nt(t[h]["kb"], k_[h]) for h in H]
            QK = [_nt(q_[h], k_[h]) for h in H]
            d_qE = [_nt(dO[h], S[h]) for h in H]
            vn = [_apply_a(A[h], t[h]["vb"] - WS[h]) for h in H]
            PtdO = [_tn(QK[h] * t[h]["gam"], dO[h]) for h in H]
            qEdO = [_tn(t[h]["qE"], dO[h]) for h in H]
            dOvn = [_nt(dO[h], vn[h]) for h in H]
            dQK = [jnp.where(ii >= jj, dOvn[h], 0.0) * t[h]["gam"] for h in H]
            dQKk = [_nn(dQK[h], k_[h]) for h in H]
            dQKq = [_tn(dQK[h], q_[h]) for h in H]
            pre[u] = (r, q_, k_, v_, beta, t, S, A, KK, QK, d_qE, vn, PtdO, qEdO, dQK, dQKk, dQKq)
        dSn = [ds_ref[h] for h in H]
        for u in reversed(range(cb)):
            r, q_, k_, v_, beta, t, S, A, KK, QK, d_qE, vn, PtdO, qEdO, dQK, dQKk, dQKq = pre[u]
            gam, E, Fd, cd, kb = ([t[h][n] for h in H] for n in ("gam", "E", "F", "cd", "kb"))
            ktdS = [_nn(t[h]["kt"], dSn[h]) for h in H]
            dU = [_apply_a(at_ref[u, h], PtdO[h] + ktdS[h]) for h in H]
            d_kt = [_nt(vn[h], dSn[h]) for h in H]
            dUvn = [_nt(dU[h], vn[h]) for h in H]
            dUS = [_nt(dU[h], S[h]) for h in H]
            WdU = [_tn(t[h]["W"], dU[h]) for h in H]
            d_cd = [jnp.sum(S[h] * dSn[h]) for h in H]
            dSn = [cd[h] * dSn[h] + qEdO[h] - WdU[h] for h in H]
            dKK = [jnp.where(ii > jj, -dUvn[h], 0.0) * gam[h] for h in H]
            dKKk = [_nn(dKK[h], k_[h]) for h in H]
            dKKkb = [_tn(dKK[h], kb[h]) for h in H]
            dbeta_arr = jnp.zeros((CHUNK, 128), F32)
            dgc_arr = jnp.zeros((CHUNK, 128), F32)
            for h in H:
                dW = -dUS[h]
                dq_ref[r, sl[h]] = dQKk[h] + d_qE[h] * E[h]
                d_kb = dKKk[h] + dW * E[h]
                dk_ref[r, sl[h]] = dQKq[h] + dKKkb[h] + d_kb * beta[h] + d_kt[h] * Fd[h]
                dv_ref[r, sl[h]] = dU[h] * beta[h]
                Z = dQK[h] * QK[h] + dKK[h] * KK[h]
                dbeta = jnp.sum(dU[h] * v_[h] + d_kb * k_[h], axis=-1, keepdims=True)
                m_e = (dW * kb[h] + d_qE[h] * q_[h]) * E[h]
                m_f = d_kt[h] * k_[h] * Fd[h]
                zdiag = jnp.where(ii == jj, jnp.sum(Z, axis=0, keepdims=True), 0.0)
                dgc = (jnp.sum(m_e - m_f, axis=-1, keepdims=True) + jnp.sum(Z - zdiag, axis=-1, keepdims=True)
                       + jnp.where(rowi == CHUNK - 1, jnp.sum(m_f) + d_cd[h] * cd[h], 0.0))
                dbeta_arr = dbeta_arr + jnp.where(lane == h, dbeta, 0.0)
                dgc_arr = dgc_arr + jnp.where(lane == N_HEADS + h, dgc, 0.0)
            dbg_ref[r, :] = dbeta_arr + jnp.dot(rev, dgc_arr, precision=lax.Precision.HIGHEST,
                                                preferred_element_type=F32)
        for h in H:
            ds_ref[h] = dSn[h]

    row512 = pl.BlockSpec((cb * CHUNK, DN_WIDTH), lambda n: (nb - 1 - n, 0))
    row128 = pl.BlockSpec((cb * CHUNK, 128), lambda n: (nb - 1 - n, 0))
    in_specs, args = _with_dep(
        [row512, row512, row512, row128, row128,
         pl.BlockSpec((cb, 8, CHUNK), lambda n: (nb - 1 - n, 0, 0)),
         pl.BlockSpec((cb, N_HEADS, CHUNK, CHUNK), lambda n: (nb - 1 - n, 0, 0, 0)),
         pl.BlockSpec((cb, N_HEADS, CHUNK, CHUNK), lambda n: (nb - 1 - n, 0, 0, 0)),
         pl.BlockSpec((cb, N_HEADS, HEAD_DIM, HEAD_DIM), lambda n: (nb - 1 - n, 0, 0, 0)), row512],
        [q, k, v, bg, gc, gct, a, a_t, sall, do], dep)
    return pl.pallas_call(
        body, name="dn_scan_bwd", grid=(nb,), in_specs=in_specs,
        out_specs=[row512, row512, row512, row128],
        out_shape=[jax.ShapeDtypeStruct((T, DN_WIDTH), F32)] * 3 + [jax.ShapeDtypeStruct((T, 128), F32)],
        scratch_shapes=[pltpu.VMEM((N_HEADS, HEAD_DIM, HEAD_DIM), F32)],
        compiler_params=_cp("arbitrary"))(*args)


MIX_BLOCKS = 4


def _sg_mask():
    ii = lax.broadcasted_iota(jnp.int32, (SG_BLOCK, SG_BLOCK), 0) // CHUNK
    jj = lax.broadcasted_iota(jnp.int32, (SG_BLOCK, SG_BLOCK), 1) // CHUNK
    return jj <= ii


def _mix_fwd(o, p, ong, sgn, sgw, sgbt):
    T = o.shape[0]
    rb = min(MIX_BLOCKS * SG_BLOCK, T)

    def body(o_ref, gate_ref, u_ref, vg_ref, ong_ref, sgn_ref, sgw_ref, sgbt_ref, mix_ref):
        mask = _sg_mask()
        for u0 in range(0, rb, SG_BLOCK):
            rows = slice(u0, u0 + SG_BLOCK)
            for h in range(N_HEADS):
                sl = slice(h * HEAD_DIM, (h + 1) * HEAD_DIM)
                oh = o_ref[rows, sl]
                r = lax.rsqrt(jnp.mean(oh * oh, axis=-1, keepdims=True) + EPS)
                mix_ref[rows, sl] = (oh * r * ong_ref[...] * _silu(gate_ref[rows, sl])).astype(BF16)
            for gi in range(SG_GROUPS):
                sl = slice(gi * SG_BLOCK, (gi + 1) * SG_BLOCK)
                gv = _gelu(vg_ref[rows, sl])
                r = lax.rsqrt(jnp.mean(gv * gv, axis=-1, keepdims=True) + EPS)
                vh = gv * r * sgn_ref[:, sl]
                s = _nn(jnp.where(mask, sgw_ref[gi], 0.0), vh) + sgbt_ref[:, gi:gi + 1]
                mix_ref[rows, DN_WIDTH + gi * SG_BLOCK:DN_WIDTH + (gi + 1) * SG_BLOCK] = (
                    _gelu(u_ref[rows, sl]) * s).astype(BF16)

    def col(c):
        return pl.BlockSpec((rb, 512), lambda i: (i, c))
    return pl.pallas_call(
        body, name="mix_fwd", grid=(T // rb,),
        in_specs=[pl.BlockSpec((rb, DN_WIDTH), lambda i: (i, 0)), col(3), col(4), col(5),
                  pl.BlockSpec((1, 128), lambda i: (0, 0)), pl.BlockSpec((1, SG_WIDTH), lambda i: (0, 0)),
                  pl.BlockSpec((SG_GROUPS, SG_BLOCK, SG_BLOCK), lambda i: (0, 0, 0)),
                  pl.BlockSpec((SG_BLOCK, 128), lambda i: (0, 0))],
        out_specs=pl.BlockSpec((rb, D_MODEL), lambda i: (i, 0)),
        out_shape=jax.ShapeDtypeStruct((T, D_MODEL), BF16),
        compiler_params=_cp("parallel"))(o, p, p, p, ong, sgn, sgw, sgbt)


def _mix_bwd(o, p, ong, sgn, sgw, sgbt, dmix, dep=None):
    T = o.shape[0]
    rb = min(MIX_BLOCKS * SG_BLOCK, T)

    def body(o_ref, gate_ref, u_ref, vg_ref, ong_ref, sgn_ref, sgw_ref, sgbt_ref, dmix_ref, *rest):
        do_ref, dp_ref, gong_ref, gsgn_ref, gsgw_ref, gsgbt_ref = rest[-6:]
        @pl.when(pl.program_id(0) == 0)
        def _():
            gong_ref[...] = jnp.zeros_like(gong_ref)
            gsgn_ref[...] = jnp.zeros_like(gsgn_ref)
            gsgw_ref[...] = jnp.zeros_like(gsgw_ref)
            gsgbt_ref[...] = jnp.zeros_like(gsgbt_ref)
        mask = _sg_mask()
        lane = _lane_iota((SG_BLOCK, 128))
        for u0 in range(0, rb, SG_BLOCK):
            rows = slice(u0, u0 + SG_BLOCK)
            for h in range(N_HEADS):
                sl = slice(h * HEAD_DIM, (h + 1) * HEAD_DIM)
                oh = o_ref[rows, sl]
                dm = dmix_ref[rows, sl]
                r = lax.rsqrt(jnp.mean(oh * oh, axis=-1, keepdims=True) + EPS)
                oh_hat = oh * r
                gt = gate_ref[rows, sl]
                sg = _silu(gt)
                dp_ref[rows, sl] = (dm * oh_hat * ong_ref[...] * _dsilu(gt)).astype(BF16)
                dn_ = dm * sg
                gong_ref[...] += jnp.sum(dn_ * oh_hat, axis=0, keepdims=True)
                dhat = dn_ * ong_ref[...]
                do_ref[rows, sl] = r * (dhat - oh_hat * jnp.mean(dhat * oh_hat, axis=-1, keepdims=True))
            for gi in range(SG_GROUPS):
                sl = slice(gi * SG_BLOCK, (gi + 1) * SG_BLOCK)
                vraw = vg_ref[rows, sl]
                gv = _gelu(vraw)
                r = lax.rsqrt(jnp.mean(gv * gv, axis=-1, keepdims=True) + EPS)
                vhat = gv * r
                vn = vhat * sgn_ref[:, sl]
                wm = jnp.where(mask, sgw_ref[gi], 0.0)
                s = _nn(wm, vn) + sgbt_ref[:, gi:gi + 1]
                uraw = u_ref[rows, sl]
                dm = dmix_ref[rows, DN_WIDTH + gi * SG_BLOCK:DN_WIDTH + (gi + 1) * SG_BLOCK]
                dp_ref[rows, DN_WIDTH + gi * SG_BLOCK:DN_WIDTH + (gi + 1) * SG_BLOCK] = (
                    dm * s * _dgelu(uraw)).astype(BF16)
                ds = dm * _gelu(uraw)
                gsgbt_ref[...] += jnp.where(lane == gi, jnp.sum(ds, axis=-1, keepdims=True), 0.0)
                gsgw_ref[gi] += jnp.where(mask, _nt(ds, vn), 0.0)
                dvn = _tn(wm, ds)
                gsgn_ref[:, sl] += jnp.sum(dvn * vhat, axis=0, keepdims=True)
                dhat = dvn * sgn_ref[:, sl]
                dgv = r * (dhat - vhat * jnp.mean(dhat * vhat, axis=-1, keepdims=True))
                dp_ref[rows, 2 * DN_WIDTH + gi * SG_BLOCK:2 * DN_WIDTH + (gi + 1) * SG_BLOCK] = (
                    dgv * _dgelu(vraw)).astype(BF16)

    def col(c):
        return pl.BlockSpec((rb, 512), lambda i: (i, c))
    full = lambda *s: pl.BlockSpec(s, lambda i: (0,) * len(s))
    in_specs, args = _with_dep(
        [pl.BlockSpec((rb, DN_WIDTH), lambda i: (i, 0)), col(3), col(4), col(5),
         full(1, 128), full(1, SG_WIDTH), full(SG_GROUPS, SG_BLOCK, SG_BLOCK), full(SG_BLOCK, 128),
         pl.BlockSpec((rb, D_MODEL), lambda i: (i, 0))],
        [o, p, p, p, ong, sgn, sgw, sgbt, dmix], dep)
    return pl.pallas_call(
        body, name="mix_bwd", grid=(T // rb,), in_specs=in_specs,
        out_specs=[pl.BlockSpec((rb, DN_WIDTH), lambda i: (i, 0)), pl.BlockSpec((rb, 3 * 512), lambda i: (i, 0)),
                   full(1, 128), full(1, SG_WIDTH), full(SG_GROUPS, SG_BLOCK, SG_BLOCK), full(SG_BLOCK, 128)],
        out_shape=[jax.ShapeDtypeStruct((T, DN_WIDTH), F32), jax.ShapeDtypeStruct((T, 3 * 512), BF16),
                   jax.ShapeDtypeStruct((1, 128), F32), jax.ShapeDtypeStruct((1, SG_WIDTH), F32),
                   jax.ShapeDtypeStruct((SG_GROUPS, SG_BLOCK, SG_BLOCK), F32),
                   jax.ShapeDtypeStruct((SG_BLOCK, 128), F32)],
        compiler_params=_cp("arbitrary"))(*args)


def _pad_lanes(row, offset=0):
    n = row.shape[1]
    return jnp.pad(row, ((0, 0), (offset, 128 - n - offset)))


def _local_step(x, tgt, w, dep=None, late_weights=None, on_grad=None):
    T = x.shape[0]
    N = T // CHUNK
    on_grad = on_grad or (lambda name, g: None)
    alog_row = _pad_lanes(w["dn_a_log"], N_HEADS)
    dtb_row = _pad_lanes(w["dn_dt_bias"], N_HEADS)
    sgbt = jnp.pad(w["sg_b"].T, ((0, 0), (0, 128 - SG_GROUPS)))

    p, h1, w_in_pad = _in_proj(x, w["attn_norm_g"], w["w_in"], dep=dep)
    q, k, v, bg = _dn_act(p, w["dn_conv_w"], alog_row, dtb_row)
    gc, gct, lmat = _dn_chunk(k, bg)
    lt = lmat.reshape(N * N_HEADS, CHUNK * CHUNK).T
    at = _tri_inv(lt)
    a = at.reshape(CHUNK * CHUNK, N * N_HEADS).T.reshape(N, N_HEADS, CHUNK, CHUNK)
    a_t = at.transpose(1, 0, 2).reshape(CHUNK * CHUNK, N * N_HEADS).T.reshape(N, N_HEADS, CHUNK, CHUNK)
    o, sall = _dn_scan(q, k, v, bg, gc, gct, a)
    mix = _mix_fwd(o, p, w["dn_out_norm_g"], w["sg_norm_g"], w["sg_w"], sgbt)
    if late_weights is not None:
        w = {**w, **late_weights("out_proj", mix)}
    x2, h2 = _out_proj(mix, w["w_out"], x, w["ffn_norm_g"])
    up, act = _up_proj_act(h2, w["w_up"], w["ffn_conv_w"], w["ffn_conv_b"])
    if late_weights is not None:
        w = {**w, **late_weights("down_proj", act)}
    loss, dx3, g_final = _down_proj_loss(act, w["w_down"], x2, tgt, w["final_norm_g"])

    dact = _mm_nt("d_act", dx3, w["w_down"], F32, 512, D_FF)
    g_w_down = _mm_tn("g_w_down", act, dx3, D_FF, 1024, 1024)
    tok = on_grad("w_down", g_w_down)
    dup, g_ffn_conv_w, g_ffn_conv_b = _ffn_act_bwd(up, dact, w["ffn_conv_w"], w["ffn_conv_b"], dep=tok)
    g_w_up = _mm_tn("g_w_up", h2, dup, 1024, 2 * D_FF // 4, 2048, col_major_tiles=True)
    tok = on_grad("w_up", g_w_up)
    dx2, g_ffn_norm = _mm_nt_rms_bwd("d_h2", dup, w["w_up"], x2, w["ffn_norm_g"], dx3, dep=tok)
    dmix = _mm_nt("d_mix", dx2, w["w_out"], F32, 512, 1024)
    g_w_out = _mm_tn("g_w_out", mix, dx2, 1024, 1024, 1024)
    tok = on_grad("w_out", g_w_out)
    do, dp_mid, g_ong, g_sgn, g_sgw, g_sgbt = _mix_bwd(o, p, w["dn_out_norm_g"], w["sg_norm_g"], w["sg_w"], sgbt,
                                                      dmix, dep=tok)
    early = dict(dn_out_norm_g=g_ong, sg_norm_g=g_sgn, sg_w=g_sgw, sg_bt=g_sgbt,
                 ffn_norm_g=g_ffn_norm, ffn_conv_w=g_ffn_conv_w, ffn_conv_b=g_ffn_conv_b, final_norm_g=g_final)
    tok = on_grad("small_early", early)
    dq, dk, dv, dbg = _dn_scan_bwd(q, k, v, bg, gc, gct, a, a_t, sall, do, dep=tok)
    dp, g_dn_conv_w, g_ad = _dn_act_bwd(p, w["dn_conv_w"], alog_row, dtb_row, dq, dk, dv, dbg, dp_mid)
    g_w_in = _mm_tn("g_w_in", h1, dp, 1024, PROJ_PAD, 1024, col_groups=(4, PROJ_COLS // 4))
    tok = on_grad("w_in", g_w_in)
    grad_x, g_attn_norm = _mm_nt_rms_bwd("d_h1", dp, w_in_pad, x, w["attn_norm_g"], dx2, dep=tok)

    grads = dict(attn_norm_g=g_attn_norm, w_in=g_w_in, dn_conv_w=g_dn_conv_w, a_dt=g_ad,
                 w_out=g_w_out, w_up=g_w_up, w_down=g_w_down, **early)
    return loss, grad_x, grads


def _me():
    return lax.axis_index("x"), lax.axis_index("y"), lax.axis_index("c")


def _peer(rel):
    x, y, c = _me()
    return {"x": (1 - x, y, c), "y": (x, 1 - y, c), "xy": (1 - x, 1 - y, c), "c": (x, y, 1 - c)}[rel]


def _chip_of(dev):
    return 2 * dev[0] + dev[1]


CHIP_RELS = ("x", "y", "xy")


def _run_copies(copies, sends, recvs):
    for cp in copies:
        cp.start()
    for cp in recvs:
        cp.wait_recv()
    for cp in sends:
        cp.wait_send()


def _gather_first(w_shard, small_shard):
    R = w_shard.shape[0]
    r2 = R // 2

    def body(w_ref, s_ref, w_out, s_out, send_sems, recv_sems):
        x, y, c = _me()
        me = _chip_of((x, y))
        sib = _peer("c")

        def half(chip, core):
            return w_out.at[chip, pl.ds(pl.multiple_of(core * r2, 8), r2), :]

        def copy(k, src, dst, to):
            return pltpu.make_async_remote_copy(src_ref=src, dst_ref=dst, send_sem=send_sems.at[k],
                                                recv_sem=recv_sems.at[k], device_id=to, device_id_type=MESH)

        own_rows = w_ref.at[pl.ds(pl.multiple_of(c * r2, 8), r2), :]
        first = [copy(r, own_rows, half(me, c), _peer(rel)) for r, rel in enumerate(CHIP_RELS)]
        first += [copy(3 + r, s_ref, s_out.at[me], _peer(rel)) for r, rel in enumerate(CHIP_RELS)]
        for cp in first:
            cp.start()
        passed = []
        for r, rel in enumerate(CHIP_RELS):
            their = _chip_of(_peer(rel))
            copy(r, own_rows, half(their, c), _peer(rel)).wait_recv()
            fwd = copy(6 + r, half(their, c), half(their, c), sib)
            fwd.start()
            passed.append(fwd)
        for r, rel in enumerate(CHIP_RELS):
            their = _chip_of(_peer(rel))
            copy(3 + r, s_ref, s_out.at[their], _peer(rel)).wait_recv()
            copy(6 + r, own_rows, half(their, 1 - c), sib).wait_recv()
        for cp in first + passed:
            cp.wait_send()

    w_all, s_all = pl.pallas_call(
        body, name="gather_first", in_specs=[ANY, ANY], out_specs=[ANY, ANY],
        out_shape=[jax.ShapeDtypeStruct((4,) + w_shard.shape, w_shard.dtype),
                   jax.ShapeDtypeStruct((4,) + small_shard.shape, small_shard.dtype)],
        scratch_shapes=[pltpu.SemaphoreType.DMA((9,)), pltpu.SemaphoreType.DMA((9,))])(w_shard, small_shard)
    me = _chip_of(_me())
    return (lax.dynamic_update_index_in_dim(w_all, w_shard, me, 0),
            lax.dynamic_update_index_in_dim(s_all, small_shard, me, 0))


OTHERS = tuple((fx, fy, fc) for fx in (0, 1) for fy in (0, 1) for fc in (0, 1) if (fx, fy, fc) != (0, 0, 0))


def _other(flip):
    x, y, c = _me()
    return (x ^ flip[0], y ^ flip[1], c ^ flip[2])


def _linear(dev):
    return 4 * dev[0] + 2 * dev[1] + dev[2]


def _exchange_small(small):
    def body(small_ref, out_ref, send_sems, recv_sems):
        my_slot = _linear(_me())
        sends, recvs = [], []
        for k, flip in enumerate(OTHERS):
            peer = _other(flip)
            sends.append(pltpu.make_async_remote_copy(
                src_ref=small_ref, dst_ref=out_ref.at[my_slot], send_sem=send_sems.at[k], recv_sem=recv_sems.at[k],
                device_id=peer, device_id_type=MESH))
            recvs.append(pltpu.make_async_remote_copy(
                src_ref=small_ref, dst_ref=out_ref.at[_linear(peer)], send_sem=send_sems.at[k],
                recv_sem=recv_sems.at[k], device_id=peer, device_id_type=MESH))
        _run_copies(sends, sends, recvs)

    out = pl.pallas_call(
        body, name="exchange_small", in_specs=[ANY], out_specs=ANY,
        out_shape=jax.ShapeDtypeStruct((8,) + small.shape, small.dtype),
        scratch_shapes=[pltpu.SemaphoreType.DMA((7,)), pltpu.SemaphoreType.DMA((7,))])(small)
    return lax.dynamic_update_index_in_dim(out, small, _linear(_me()), 0)


def _pair_swap(name, halves):
    n = len(halves)

    def body(*refs):
        src, out = refs[:n], refs[n:2 * n]
        send_sems, recv_sems = refs[2 * n:]
        sib = _peer("c")
        copies = [pltpu.make_async_remote_copy(
            src_ref=src[i], dst_ref=out[i], send_sem=send_sems.at[i], recv_sem=recv_sems.at[i],
            device_id=sib, device_id_type=MESH) for i in range(n)]
        _run_copies(copies, copies, copies)

    return pl.pallas_call(
        body, name=name, in_specs=[ANY] * n, out_specs=[ANY] * n,
        out_shape=[jax.ShapeDtypeStruct(h.shape, h.dtype) for h in halves],
        scratch_shapes=[pltpu.SemaphoreType.DMA((n,)), pltpu.SemaphoreType.DMA((n,))])(*halves)


HBM = pl.BlockSpec(memory_space=pltpu.HBM)
SEM = pl.BlockSpec(memory_space=pltpu.SEMAPHORE)
EFFECT = pltpu.SideEffectType.DATAFLOW_SIDE_EFFECTING


def _hbm(a):
    return pltpu.with_memory_space_constraint(a, pltpu.HBM)


def _transfer_start(name, srcs, lands, n_copies, make_copies, after=None):
    n, m = len(srcs), len(lands)

    def body(*refs):
        src, land = refs[:n], refs[n:n + m]
        outs = refs[n + m + (after is not None):]
        send_sems, recv_sems, token = outs[0], outs[1], outs[-1]
        for cp in make_copies(src, land, send_sems, recv_sems):
            cp.start()
        token[...] = jnp.zeros_like(token)

    arrs = list(srcs) + list(lands)
    in_specs, args = _with_dep([HBM] * (n + m), [_hbm(a) for a in arrs], after)
    out = pl.pallas_call(
        body, name=name,
        out_shape=(pltpu.SemaphoreType.DMA((n_copies,)), pltpu.SemaphoreType.DMA((n_copies,)),
                   *[pltpu.HBM(a.shape, a.dtype) for a in arrs], jax.ShapeDtypeStruct((8, 128), F32)),
        in_specs=in_specs,
        out_specs=(SEM, SEM, *[HBM] * (n + m), pl.BlockSpec(memory_space=pltpu.VMEM)),
        input_output_aliases={i: 2 + i for i in range(n + m)},
        compiler_params=pltpu.CompilerParams(has_side_effects=EFFECT))(*args)
    return out[0], out[1], list(out[2:2 + n]), list(out[2 + n:2 + n + m]), out[-1]


def _transfer_wait(name, send_sems, recv_sems, srcs, lands, make_copies, after):
    n, m = len(srcs), len(lands)

    def body(*refs):
        src, land = refs[:n], refs[n:n + m]
        s_sems, r_sems = refs[n + m], refs[n + m + 1]
        for cp in make_copies(src, land, s_sems, r_sems):
            cp.wait_send()
            cp.wait_recv()

    arrs = list(srcs) + list(lands)
    out = pl.pallas_call(
        body, name=name, out_shape=tuple(pltpu.HBM(a.shape, a.dtype) for a in arrs),
        in_specs=[HBM] * (n + m) + [SEM, SEM, ANY], out_specs=tuple([HBM] * (n + m)),
        input_output_aliases={i: i for i in range(n + m)},
        compiler_params=pltpu.CompilerParams(has_side_effects=EFFECT))(*arrs, send_sems, recv_sems, after)
    return list(out[:n]), list(out[n:])


def _gather_copies(src, land, send_sems, recv_sems):
    me = _chip_of(_me())
    copies = []
    for i in range(len(src)):
        for r, rel in enumerate(CHIP_RELS):
            k = 3 * i + r
            copies.append(pltpu.make_async_remote_copy(
                src_ref=src[i], dst_ref=land[i].at[me], send_sem=send_sems.at[k], recv_sem=recv_sems.at[k],
                device_id=_peer(rel), device_id_type=MESH))
    return copies


def _small_copies(src, land, send_sems, recv_sems):
    my_slot = _linear(_me())
    return [pltpu.make_async_remote_copy(
        src_ref=src[0], dst_ref=land[0].at[my_slot], send_sem=send_sems.at[k], recv_sem=recv_sems.at[k],
        device_id=_other(flip), device_id_type=MESH) for k, flip in enumerate(OTHERS)]


def _pieces_copies(src, land, send_sems, recv_sems):
    copies = []
    for k, flip in enumerate(OTHERS):
        peer = _other(flip)
        copies.append(pltpu.make_async_remote_copy(
            src_ref=src[0].at[_linear(peer)], dst_ref=land[0].at[k], send_sem=send_sems.at[k],
            recv_sem=recv_sems.at[k], device_id=peer, device_id_type=MESH))
    return copies


def _row_block(rows, cols, budget=2 * 1024 * 1024):
    rb = max(8, (budget // (4 * cols)) // 8 * 8)
    while rows % rb:
        rb -= 8
    return rb if rb > 0 else rows


def _sum_pieces(name, mine, slot, rest):
    _, R, Cc = mine.shape
    K = rest.shape[0]
    rb = _row_block(R, Cc)

    def body(s_ref, f_ref, r_ref, o_ref):
        acc = f_ref[0].astype(F32)
        for j in range(K):
            acc = acc + r_ref[j].astype(F32)
        o_ref[...] = acc

    return pl.pallas_call(
        body, name=name,
        grid_spec=pltpu.PrefetchScalarGridSpec(
            num_scalar_prefetch=1, grid=(R // rb,),
            in_specs=[pl.BlockSpec((1, rb, Cc), lambda i, s: (s[0], i, 0)),
                      pl.BlockSpec((K, rb, Cc), lambda i, s: (0, i, 0))],
            out_specs=pl.BlockSpec((rb, Cc), lambda i, s: (i, 0))),
        out_shape=jax.ShapeDtypeStruct((R, Cc), F32), compiler_params=_cp("parallel"))(slot, mine, rest)


def _adamw_math(w, gv, m, v):
    mn = ADAM_B1 * m + (1.0 - ADAM_B1) * gv
    vn = ADAM_B2 * v + (1.0 - ADAM_B2) * (gv * gv)
    m_hat = mn / (1.0 - ADAM_B1 ** ADAM_STEP)
    v_hat = vn / (1.0 - ADAM_B2 ** ADAM_STEP)
    return -ADAM_LR * (m_hat / (jnp.sqrt(v_hat) + ADAM_EPS) + ADAM_WD * w), mn, vn


def _adamw_halves(name, w, mine, theirs, m, v, core):
    R, Cc = w.shape
    r2 = R // 2
    rb = _row_block(r2, Cc, 1024 * 1024)
    nb2 = r2 // rb

    def body(c_ref, w_ref, mine_ref, theirs_ref, m_ref, v_ref, g_ref, d_ref, mo_ref, vo_ref):
        is_mine = (pl.program_id(0) // nb2) == c_ref[0]
        gv = jnp.where(is_mine, mine_ref[...], theirs_ref[...])
        g_ref[...] = gv
        d_ref[...], mo_ref[...], vo_ref[...] = _adamw_math(w_ref[...], gv, m_ref[...], v_ref[...])

    blk = pl.BlockSpec((rb, Cc), lambda i, c: (i, 0))
    half = lambda own: pl.BlockSpec(
        (rb, Cc), lambda i, c: (jnp.clip(i - (c[0] if own else 1 - c[0]) * nb2, 0, nb2 - 1), 0))
    return pl.pallas_call(
        body, name=name,
        grid_spec=pltpu.PrefetchScalarGridSpec(
            num_scalar_prefetch=1, grid=(2 * nb2,), in_specs=[blk, half(True), half(False), blk, blk],
            out_specs=[blk] * 4),
        out_shape=[jax.ShapeDtypeStruct((R, Cc), F32)] * 4, compiler_params=_cp("parallel"))(core, w, mine, theirs, m, v)


def _adamw_transposed(name, wt, mine, theirs, mt, vt, core):
    Cc, kh_n, _ = wt.shape
    r2 = mine.shape[0]
    per_half = kh_n // 2
    nb = -(-Cc // LANES)

    def body(c_ref, w_ref, mine_ref, theirs_ref, m_ref, v_ref, g_ref, d_ref, mo_ref, vo_ref):
        first = c_ref[0] == 0
        halves = (jnp.where(first, mine_ref[...], theirs_ref[...]).T,
                  jnp.where(first, theirs_ref[...], mine_ref[...]).T)
        for kh in range(kh_n):
            lo = (kh % per_half) * LANES
            g_ref[:, kh, :] = halves[kh // per_half][:, lo:lo + LANES]
        d_ref[...], mo_ref[...], vo_ref[...] = _adamw_math(w_ref[...], g_ref[...], m_ref[...], v_ref[...])

    blk = pl.BlockSpec((LANES, kh_n, LANES), lambda i, c: (i, 0, 0))
    half = pl.BlockSpec((r2, LANES), lambda i, c: (0, i))
    return pl.pallas_call(
        body, name=name,
        grid_spec=pltpu.PrefetchScalarGridSpec(
            num_scalar_prefetch=1, grid=(nb,), in_specs=[blk, half, half, blk, blk], out_specs=[blk] * 4),
        out_shape=[jax.ShapeDtypeStruct(wt.shape, F32)] * 4, compiler_params=_cp("parallel"))(
            core, wt, mine, theirs, mt, vt)


FF_W = 2 * D_FF
FF_CH = FF_W // LANES
DNC_W = 3 * DN_WIDTH
DNC_CH = DNC_W // LANES
E_ONG, E_SGN, E_SGW, E_SGBT = 0, 1, 8, 8 + SG_GROUPS * SG_BLOCK
E_FFN = E_SGBT + SG_BLOCK
E_FCW = E_FFN + D_MODEL // LANES
E_FCB = E_FCW + 3 * FF_CH
E_FIN = E_FCB + FF_CH
EARLY_ROWS = E_FIN + D_MODEL // LANES
L_ATTN, L_DNC = 0, D_MODEL // LANES
L_AD = L_DNC + 4 * DNC_CH
L_LOSS = L_AD + 2
LATE_ROWS = -(-(L_LOSS + 1) // 8) * 8


def _put_rows(out, r0, x):
    k, width = x.shape
    n = width // LANES
    for t in range(k):
        for j in range(n):
            out[r0 + t * n + j:r0 + t * n + j + 1, :] = x[t:t + 1, j * LANES:(j + 1) * LANES]


def _pack_early(ong, sgn, sgw, sgbt, ffn, fcw, fcb, fin):
    def body(ong_ref, sgn_ref, sgw_ref, sgbt_ref, ffn_ref, fcw_ref, fcb_ref, fin_ref, out):
        out[...] = jnp.zeros_like(out)
        _put_rows(out, E_ONG, ong_ref)
        _put_rows(out, E_SGN, sgn_ref)
        for gi in range(SG_GROUPS):
            out[E_SGW + gi * SG_BLOCK:E_SGW + (gi + 1) * SG_BLOCK, :] = sgw_ref[gi]
        out[E_SGBT:E_SGBT + SG_BLOCK, :] = sgbt_ref[...]
        _put_rows(out, E_FFN, ffn_ref)
        _put_rows(out, E_FCW, fcw_ref)
        _put_rows(out, E_FCB, fcb_ref)
        _put_rows(out, E_FIN, fin_ref)

    return pl.pallas_call(body, name="pack_small_early", out_shape=jax.ShapeDtypeStruct((EARLY_ROWS, LANES), F32))(
        ong, sgn, sgw, sgbt, ffn, fcw, fcb, fin)


def _pack_late(attn, dnc, ad, loss_row):
    def body(attn_ref, dnc_ref, ad_ref, loss_ref, out):
        out[...] = jnp.zeros_like(out)
        _put_rows(out, L_ATTN, attn_ref)
        _put_rows(out, L_DNC, dnc_ref)
        out[L_AD:L_AD + 2, :] = ad_ref[...]
        out[L_LOSS:L_LOSS + 1, :] = loss_ref[...]

    return pl.pallas_call(body, name="pack_small_late", out_shape=jax.ShapeDtypeStruct((LATE_ROWS, LANES), F32))(
        attn, dnc, ad, loss_row)


SMALL = ("attn_norm_g", "dn_a_log", "dn_dt_bias", "dn_out_norm_g", "sg_norm_g", "sg_w", "sg_b", "ffn_norm_g",
         "ffn_conv_b", "final_norm_g", "dn_conv_w", "ffn_conv_w")


def _small_update(early_all, late_all, chip, W, M, V):
    n = len(SMALL)
    arrs = [d[k] for d in (W, M, V) for k in SMALL]

    def body(c_ref, e_ref, l_ref, *refs):
        w_, m_, v_ = refs[:n], refs[n:2 * n], refs[2 * n:3 * n]
        loss_ref = refs[3 * n]
        outs = refs[3 * n + 1:]
        g_, d_, mo_, vo_ = outs[:n], outs[n:2 * n], outs[2 * n:3 * n], outs[3 * n:4 * n]
        chip_i = c_ref[0]

        def total(ref, r0, rows=1):
            acc = ref[0, pl.ds(r0, rows), :]
            for s in range(1, 8):
                acc = acc + ref[s, pl.ds(r0, rows), :]
            return acc

        def update(i, idx, g):
            g_[i][idx] = g
            d_[i][idx], mo_[i][idx], vo_[i][idx] = _adamw_math(w_[i][idx], g, m_[i][idx], v_[i][idx])

        def rows_param(name, ref, r0, width):
            i = SMALL.index(name)
            for j in range(width // LANES):
                update(i, (slice(None), slice(j * LANES, (j + 1) * LANES)), total(ref, r0 + j))

        rows_param("attn_norm_g", l_ref, L_ATTN, D_MODEL)
        ad = (total(l_ref, L_AD), total(l_ref, L_AD + 1))
        update(SMALL.index("dn_a_log"), (slice(None), slice(None)), ad[0][:, N_HEADS:2 * N_HEADS])
        update(SMALL.index("dn_dt_bias"), (slice(None), slice(None)), ad[1][:, N_HEADS:2 * N_HEADS])
        rows_param("dn_out_norm_g", e_ref, E_ONG, HEAD_DIM)
        rows_param("sg_norm_g", e_ref, E_SGN, SG_WIDTH)
        sgbt = total(e_ref, E_SGBT, SG_BLOCK).T
        for gi in range(SG_GROUPS):
            update(SMALL.index("sg_w"), (0, gi), total(e_ref, E_SGW + gi * SG_BLOCK, SG_BLOCK))
            update(SMALL.index("sg_b"), (0, slice(gi, gi + 1), slice(None)), sgbt[gi:gi + 1, :])
        rows_param("ffn_norm_g", e_ref, E_FFN, D_MODEL)
        rows_param("ffn_conv_b", e_ref, E_FCB, FF_W)
        rows_param("final_norm_g", e_ref, E_FIN, D_MODEL)
        for name, ref, r0, taps, chunks in (("dn_conv_w", l_ref, L_DNC, 4, DNC_CH), ("ffn_conv_w", e_ref, E_FCW, 3, FF_CH)):
            mine = chunks // 4
            for t in range(taps):
                for j in range(mine):
                    update(SMALL.index(name), (0, slice(t, t + 1), slice(j * LANES, (j + 1) * LANES)),
                           total(ref, r0 + t * chunks + chip_i * mine + j))
        loss_ref[...] = total(l_ref, L_LOSS)

    full = lambda a: pl.BlockSpec(a.shape, lambda i, c, nd=a.ndim: (0,) * nd)
    shapes = [jax.ShapeDtypeStruct(W[k].shape, F32) for k in SMALL]
    outs = pl.pallas_call(
        body, name="small_update",
        grid_spec=pltpu.PrefetchScalarGridSpec(
            num_scalar_prefetch=1, grid=(1,), in_specs=[full(early_all), full(late_all)] + [full(a) for a in arrs],
            out_specs=[pl.BlockSpec((1, LANES), lambda i, c: (0, 0))] + [full(s) for s in shapes] * 4),
        out_shape=[jax.ShapeDtypeStruct((1, LANES), F32)] + shapes * 4,
        compiler_params=pltpu.CompilerParams(vmem_limit_bytes=VMEM_LIMIT))(chip, early_all, late_all, *arrs)
    loss, outs = outs[0], outs[1:]
    return (loss,) + tuple(dict(zip(SMALL, outs[k * n:(k + 1) * n])) for k in range(4))


ORDER =("attn_norm_g", "w_in", "dn_conv_w", "dn_a_log", "dn_dt_bias", "dn_out_norm_g", "sg_norm_g", "sg_w",
         "sg_b", "w_out", "ffn_norm_g", "w_up", "ffn_conv_w", "ffn_conv_b", "w_down", "final_norm_g")


def kernel(x, attn_norm_g, w_in, dn_conv_w, dn_a_log, dn_dt_bias, dn_out_norm_g, sg_norm_g, sg_w, sg_b, w_out, ffn_norm_g, w_up, ffn_conv_w, ffn_conv_b, w_down, final_norm_g, loss_target, m_attn_norm_g, m_w_in, m_dn_conv_w, m_dn_a_log, m_dn_dt_bias, m_dn_out_norm_g, m_sg_norm_g, m_sg_w, m_sg_b, m_w_out, m_ffn_norm_g, m_w_up, m_ffn_conv_w, m_ffn_conv_b, m_w_down, m_final_norm_g, v_attn_norm_g, v_w_in, v_dn_conv_w, v_dn_a_log, v_dn_dt_bias, v_dn_out_norm_g, v_sg_norm_g, v_sg_w, v_sg_b, v_w_out, v_ffn_norm_g, v_w_up, v_ffn_conv_w, v_ffn_conv_b, v_w_down, v_final_norm_g):
    W = dict(attn_norm_g=attn_norm_g, w_in=w_in, dn_conv_w=dn_conv_w, dn_a_log=dn_a_log, dn_dt_bias=dn_dt_bias,
             dn_out_norm_g=dn_out_norm_g, sg_norm_g=sg_norm_g, sg_w=sg_w, sg_b=sg_b, w_out=w_out,
             ffn_norm_g=ffn_norm_g, w_up=w_up, ffn_conv_w=ffn_conv_w, ffn_conv_b=ffn_conv_b, w_down=w_down,
             final_norm_g=final_norm_g)
    Mo = dict(attn_norm_g=m_attn_norm_g, w_in=m_w_in, dn_conv_w=m_dn_conv_w, dn_a_log=m_dn_a_log,
              dn_dt_bias=m_dn_dt_bias, dn_out_norm_g=m_dn_out_norm_g, sg_norm_g=m_sg_norm_g, sg_w=m_sg_w,
              sg_b=m_sg_b, w_out=m_w_out, ffn_norm_g=m_ffn_norm_g, w_up=m_w_up, ffn_conv_w=m_ffn_conv_w,
              ffn_conv_b=m_ffn_conv_b, w_down=m_w_down, final_norm_g=m_final_norm_g)
    Vo = dict(attn_norm_g=v_attn_norm_g, w_in=v_w_in, dn_conv_w=v_dn_conv_w, dn_a_log=v_dn_a_log,
              dn_dt_bias=v_dn_dt_bias, dn_out_norm_g=v_dn_out_norm_g, sg_norm_g=v_sg_norm_g, sg_w=v_sg_w,
              sg_b=v_sg_b, w_out=v_w_out, ffn_norm_g=v_ffn_norm_g, w_up=v_w_up, ffn_conv_w=v_ffn_conv_w,
              ffn_conv_b=v_ffn_conv_b, w_down=v_w_down, final_norm_g=v_final_norm_g)
    xi, yi, ci = lax.axis_index("x"), lax.axis_index("y"), lax.axis_index("c")
    chip = 2 * xi + yi

    me_lin = 4 * xi + 2 * yi + ci

    g_in, g_dnc = _gather_first(w_in[0].astype(BF16), dn_conv_w[0])
    def start_gather(name, shards, after):
        lands = [lax.dynamic_update_index_in_dim(lax.empty((4,) + s.shape, s.dtype), s, chip, 0) for s in shards]
        return _transfer_start(name, shards, lands, 3 * len(shards), _gather_copies, after=after)

    mid = start_gather("gather_mid_start", [w_out[0].astype(BF16), w_up[0].astype(BF16), ffn_conv_w[0]], g_in)
    last = start_gather("gather_last_start", [w_down[0].astype(BF16)], mid[4])
    token = last[4]

    def late_weights(stage, after):
        if stage == "out_proj":
            _, (g_out, g_up, g_ffc) = _transfer_wait("gather_mid_wait", *mid[:4], _gather_copies, after)
            return dict(w_out=g_out.reshape(D_MODEL, D_MODEL), ffn_conv_w=g_ffc.transpose(1, 0, 2).reshape(3, 2 * D_FF),
                        w_up=g_up.transpose(1, 0, 2).reshape(D_MODEL, 2 * D_FF))
        _, (g_down,) = _transfer_wait("gather_last_wait", *last[:4], _gather_copies, after)
        return dict(w_down=g_down.reshape(D_FF, D_MODEL))

    full = dict(
        w_in=g_in,
        dn_conv_w=g_dnc.transpose(1, 0, 2).reshape(4, 3 * DN_WIDTH),
        attn_norm_g=attn_norm_g, dn_a_log=dn_a_log, dn_dt_bias=dn_dt_bias, dn_out_norm_g=dn_out_norm_g,
        sg_norm_g=sg_norm_g, sg_w=sg_w[0], sg_b=sg_b[0], ffn_norm_g=ffn_norm_g, ffn_conv_b=ffn_conv_b,
        final_norm_g=final_norm_g[None])

    pending = {}

    def on_grad(name, gw):
        if name == "small_early":
            buf = _pack_early(gw["dn_out_norm_g"], gw["sg_norm_g"], gw["sg_w"], gw["sg_bt"], gw["ffn_norm_g"],
                              gw["ffn_conv_w"], gw["ffn_conv_b"], gw["final_norm_g"])
            land = lax.dynamic_update_index_in_dim(lax.empty((8,) + buf.shape, F32), buf, me_lin, 0)
            s_sem, r_sem, src, lands, tok = _transfer_start("small_early_start", [buf], [land], 7, _small_copies)
            pending[name] = (s_sem, r_sem, src, lands)
            return tok
        g8 = gw.reshape(8, -1, gw.shape[-1])
        land = lax.empty((7,) + g8.shape[1:], BF16)
        s_sem, r_sem, src, lands, tok = _transfer_start(f"reduce_{name}_start", [g8], [land], 7, _pieces_copies)
        pending[name] = (s_sem, r_sem, src, lands)
        return tok

    loss_row, grad_x, g = _local_step(x[0], loss_target[0], full, dep=token, late_weights=late_weights,
                                      on_grad=on_grad)

    late_all = _exchange_small(_pack_late(g["attn_norm_g"], g["dn_conv_w"], g["a_dt"], loss_row))
    s_sem, r_sem, src, lands = pending["small_early"]
    _, (early_all,) = _transfer_wait("small_early_wait", s_sem, r_sem, src, lands, _small_copies, grad_x)
    row = lambda d: {k: (d[k].reshape(1, -1) if k == "final_norm_g" else d[k]) for k in SMALL}
    loss_sum, *small_out = _small_update(early_all, late_all, chip.astype(jnp.int32).reshape(1), row(W), row(Mo), row(Vo))
    loss = loss_sum[0, 0]

    def summed_half(n, after):
        s_sem, r_sem, src, lands = pending[n]
        sent, got = _transfer_wait(f"reduce_{n}_wait", s_sem, r_sem, src, lands, _pieces_copies, after)
        return _sum_pieces(f"sum_{n}", sent[0], me_lin.astype(jnp.int32).reshape(1), got[0])

    first3 = ("w_down", "w_up", "w_out")
    halves = [summed_half(n, grad_x) for n in first3]
    theirs = _pair_swap("pair_swap", halves)
    core = ci.astype(jnp.int32).reshape(1)
    grads, delta, new_m, new_v = {}, {}, {}, {}
    for n, mine_h, their_h in zip(first3, halves, theirs):
        shp = W[n].shape
        gr, d, mn, vn = _adamw_halves(f"adamw_{n}", W[n][0], mine_h, their_h, Mo[n][0], Vo[n][0], core)
        grads[n], delta[n], new_m[n], new_v[n] = gr.reshape(shp), d.reshape(shp), mn.reshape(shp), vn.reshape(shp)
    mine_h = summed_half("w_in", delta["w_out"])
    (their_h,) = _pair_swap("pair_swap_w_in", [mine_h])
    shp = w_in.shape
    to_t = lambda a: a.reshape(shp[1] // LANES, LANES, shp[2]).transpose(2, 0, 1)
    from_t = lambda a: a.transpose(1, 2, 0).reshape(shp)
    outs = _adamw_transposed("adamw_w_in", to_t(w_in), mine_h, their_h, to_t(m_w_in), to_t(v_w_in), core)
    grads["w_in"], delta["w_in"], new_m["w_in"], new_v["w_in"] = (from_t(o) for o in outs)
    for dst, src_d in zip((grads, delta, new_m, new_v), small_out):
        dst.update({k: (a.reshape(W[k].shape) if k == "final_norm_g" else a) for k, a in src_d.items()})

    return (loss, grad_x[None], *[grads[n] for n in ORDER], *[delta[n] for n in ORDER],
            *[new_m[n] for n in ORDER], *[new_v[n] for n in ORDER])
```

```python
import functools
import math

import jax
import jax.numpy as jnp
from jax import lax
from jax.experimental import pallas as pl
from jax.experimental.pallas import tpu as pltpu

F32 = jnp.float32
BF16 = jnp.bfloat16

D_MODEL = 1024
CHUNK = 64
SCAN_CHUNKS = 8
SCAN_CHUNKS_FWD = 8
HEAD_DIM = 128
N_HEADS = 4
DN_WIDTH = 512
SG_WIDTH = 512
SG_GROUPS = 4
SG_BLOCK = 128
D_FF = 2816
PROJ_COLS = 3080
PROJ_PAD = 3200
BA_COL = 3072
EPS = 1e-6
NEG = -1e30
VMEM_LIMIT = 56 * 1024 * 1024

ADAM_LR = 0.001
ADAM_B1 = 0.9
ADAM_B2 = 0.999
ADAM_EPS = 1e-08
ADAM_WD = 0.01
ADAM_STEP = 10

MESH = pl.DeviceIdType.MESH
ANY = pl.BlockSpec(memory_space=pl.ANY)


def _cp(*sem):
    return pltpu.CompilerParams(dimension_semantics=sem, vmem_limit_bytes=VMEM_LIMIT)


def _bf(a):
    return a.astype(BF16)


def _nn(a, b):
    return jnp.dot(_bf(a), _bf(b), preferred_element_type=F32)


def _nt(a, b):
    return lax.dot_general(_bf(a), _bf(b), (((1,), (1,)), ((), ())), preferred_element_type=F32)


def _tn(a, b):
    return lax.dot_general(_bf(a), _bf(b), (((0,), (0,)), ((), ())), preferred_element_type=F32)


def _split(a):
    hi = _bf(a)
    return hi, _bf(a - hi.astype(F32))


def _sigmoid(x):
    return 0.5 * jnp.tanh(0.5 * x) + 0.5


def _silu(x):
    return x * _sigmoid(x)


def _dsilu(x):
    s = _sigmoid(x)
    return s * (1.0 + x * (1.0 - s))


_GELU_C = math.sqrt(2.0 / math.pi)
_GELU_A = 0.044715


def _gelu(x):
    return 0.5 * x * (1.0 + jnp.tanh(_GELU_C * (x + _GELU_A * x * x * x)))


def _dgelu(x):
    t = jnp.tanh(_GELU_C * (x + _GELU_A * x * x * x))
    return 0.5 * (1.0 + t) + 0.5 * x * (1.0 - t * t) * _GELU_C * (1.0 + 3.0 * _GELU_A * x * x)


def _softplus(x):
    return jnp.maximum(x, 0.0) + jnp.log(1.0 + jnp.exp(-jnp.abs(x)))


def _with_dep(in_specs, args, dep):
    if dep is None:
        return in_specs, args
    return in_specs + [ANY], args + [dep]


SUB_ROWS = 128


def _sub_blocks(tm):
    return [slice(r0, min(r0 + SUB_ROWS, tm)) for r0 in range(0, tm, SUB_ROWS)]


def _rms_hat(xv):
    r = lax.rsqrt(jnp.mean(xv * xv, axis=-1, keepdims=True) + EPS)
    return xv * r, r


def _rms_bwd_vals(dh, xh, r, g):
    dxh = dh * g
    return r * (dxh - xh * jnp.mean(dxh * xh, axis=-1, keepdims=True)), jnp.sum(dh * xh, axis=0, keepdims=True)


def _in_proj(x, g, w4, tm=512, dep=None):
    T, K = x.shape
    ng, _, wc = w4.shape
    tm = min(tm, T)

    def body(x_ref, g_ref, w4_ref, *rest):
        p_ref, h_ref, w_ref = rest[-3:]

        @pl.when(pl.program_id(0) == 0)
        def _():
            w_ref[:, ng * wc:] = jnp.zeros((K, PROJ_PAD - ng * wc), BF16)
            for j in range(ng):
                w_ref[:, j * wc:(j + 1) * wc] = w4_ref[j]
        for r in _sub_blocks(tm):
            xh, _ = _rms_hat(x_ref[r, :])
            h_ref[r, :] = (xh * g_ref[...]).astype(BF16)
        p_ref[...] = jnp.dot(h_ref[...], w_ref[...], preferred_element_type=F32)

    in_specs, args = _with_dep(
        [pl.BlockSpec((tm, K), lambda i: (i, 0)), pl.BlockSpec((1, K), lambda i: (0, 0)),
         pl.BlockSpec((ng, K, wc), lambda i: (0, 0, 0))], [x, g, w4], dep)
    return pl.pallas_call(
        body, name="in_proj", grid=(T // tm,), in_specs=in_specs,
        out_specs=[pl.BlockSpec((tm, PROJ_PAD), lambda i: (i, 0)), pl.BlockSpec((tm, K), lambda i: (i, 0)),
                   pl.BlockSpec((K, PROJ_PAD), lambda i: (0, 0))],
        out_shape=[jax.ShapeDtypeStruct((T, PROJ_PAD), F32), jax.ShapeDtypeStruct((T, K), BF16),
                   jax.ShapeDtypeStruct((K, PROJ_PAD), BF16)],
        compiler_params=_cp("arbitrary"))(*args)


def _out_proj(mix, w, x, g, tm=512):
    T, K = mix.shape
    Dm = w.shape[1]
    tm = min(tm, T)

    def body(a_ref, w_ref, x_ref, g_ref, x2_ref, h_ref):
        x2_ref[...] = _nn(a_ref[...], w_ref[...]) + x_ref[...]
        for r in _sub_blocks(tm):
            xh, _ = _rms_hat(x2_ref[r, :])
            h_ref[r, :] = (xh * g_ref[...]).astype(BF16)

    row = lambda width: pl.BlockSpec((tm, width), lambda i: (i, 0))
    return pl.pallas_call(
        body, name="out_proj", grid=(T // tm,),
        in_specs=[row(K), pl.BlockSpec((K, Dm), lambda i: (0, 0)), row(Dm), pl.BlockSpec((1, Dm), lambda i: (0, 0))],
        out_specs=[row(Dm), row(Dm)],
        out_shape=[jax.ShapeDtypeStruct((T, Dm), F32), jax.ShapeDtypeStruct((T, Dm), BF16)],
        compiler_params=_cp("parallel"))(mix, w, x, g)


def _down_proj_loss(act, w, x2, tgt, g, tm=512):
    T, K = act.shape
    Dm = w.shape[1]
    tm = min(tm, T)

    def body(a_ref, w_ref, x_ref, t_ref, g_ref, loss_ref, dx_ref, gg_ref):
        @pl.when(pl.program_id(0) == 0)
        def _():
            gg_ref[...] = jnp.zeros_like(gg_ref)
            loss_ref[...] = jnp.zeros_like(loss_ref)
        dx_ref[...] = _nn(a_ref[...], w_ref[...]) + x_ref[...]
        for r in _sub_blocks(tm):
            xh, rr = _rms_hat(dx_ref[r, :])
            e = xh * g_ref[...] - t_ref[r, :]
            loss_ref[...] += jnp.zeros_like(loss_ref) + (0.5 / Dm) * jnp.sum(e * e)
            dx, gg = _rms_bwd_vals(e * (1.0 / Dm), xh, rr, g_ref[...])
            dx_ref[r, :] = dx
            gg_ref[...] += gg

    row = lambda width: pl.BlockSpec((tm, width), lambda i: (i, 0))
    vec = pl.BlockSpec((1, Dm), lambda i: (0, 0))
    return pl.pallas_call(
        body, name="down_proj_loss", grid=(T // tm,),
        in_specs=[row(K), pl.BlockSpec((K, Dm), lambda i: (0, 0)), row(Dm), row(Dm), vec],
        out_specs=[pl.BlockSpec((1, 128), lambda i: (0, 0)), row(Dm), vec],
        out_shape=[jax.ShapeDtypeStruct((1, 128), F32), jax.ShapeDtypeStruct((T, Dm), F32),
                   jax.ShapeDtypeStruct((1, Dm), F32)],
        compiler_params=_cp("arbitrary"))(act, w, x2, tgt, g)


def _mm_nt_rms_bwd(name, a, b, x, g, dres, tm=512, dep=None):
    M, K = a.shape
    Dm = b.shape[0]
    tm = min(tm, M)

    def body(a_ref, b_ref, x_ref, g_ref, dres_ref, *rest):
        dx_ref, gg_ref = rest[-2:]

        @pl.when(pl.program_id(0) == 0)
        def _():
            gg_ref[...] = jnp.zeros_like(gg_ref)
        dx_ref[...] = _nt(a_ref[...], b_ref[...])
        for r in _sub_blocks(tm):
            xh, rr = _rms_hat(x_ref[r, :])
            dx, gg = _rms_bwd_vals(dx_ref[r, :], xh, rr, g_ref[...])
            dx_ref[r, :] = dres_ref[r, :] + dx
            gg_ref[...] += gg

    row = lambda width: pl.BlockSpec((tm, width), lambda i: (i, 0))
    vec = pl.BlockSpec((1, Dm), lambda i: (0, 0))
    in_specs, args = _with_dep([row(K), pl.BlockSpec((Dm, K), lambda i: (0, 0)), row(Dm), vec, row(Dm)],
                               [a, b, x, g, dres], dep)
    return pl.pallas_call(
        body, name=name, grid=(M // tm,), in_specs=in_specs, out_specs=[row(Dm), vec],
        out_shape=[jax.ShapeDtypeStruct((M, Dm), F32), jax.ShapeDtypeStruct((1, Dm), F32)],
        compiler_params=_cp("arbitrary"))(*args)


def _mm_nt(name, a, b, out_dtype, tm, tn, dep=None):
    M, K = a.shape
    N = b.shape[0]
    tm, tn = min(tm, M), min(tn, N)

    def body(a_ref, b_ref, *rest):
        o_ref = rest[-1]
        o_ref[...] = _nt(a_ref[...], b_ref[...]).astype(o_ref.dtype)

    in_specs, args = _with_dep(
        [pl.BlockSpec((tm, K), lambda i, j: (i, 0)), pl.BlockSpec((tn, K), lambda i, j: (j, 0))], [a, b], dep)
    return pl.pallas_call(
        body, name=name, grid=(M // tm, N // tn), in_specs=in_specs,
        out_specs=pl.BlockSpec((tm, tn), lambda i, j: (i, j)),
        out_shape=jax.ShapeDtypeStruct((M, N), out_dtype),
        compiler_params=_cp("parallel", "parallel"))(*args)


def _mm_tn(name, a, b, tm, tn, tk, col_major_tiles=False, col_groups=None):
    T, M = a.shape
    N = b.shape[1]
    tm, tn, tk = min(tm, M), min(tn, N), min(tk, T)
    nk = T // tk

    def body(a_ref, b_ref, o_ref, acc_ref):
        k = pl.program_id(2)

        @pl.when(k == 0)
        def _():
            acc_ref[...] = jnp.zeros_like(acc_ref)
        acc_ref[...] += _tn(a_ref[...], b_ref[...])

        @pl.when(k == nk - 1)
        def _():
            if col_groups:
                for j in range(col_groups[0]):
                    o_ref[j] = acc_ref[:, j * col_groups[1]:(j + 1) * col_groups[1]].astype(BF16)
            else:
                o_ref[...] = acc_ref[...].astype(BF16).reshape(o_ref.shape)

    if col_groups:
        assert tm == M and tn == N and col_groups[0] * col_groups[1] <= N
        out_spec = pl.BlockSpec((col_groups[0], M, col_groups[1]), lambda i, j, k: (0, 0, 0))
        out_shape = jax.ShapeDtypeStruct((col_groups[0], M, col_groups[1]), BF16)
    elif col_major_tiles:
        assert tm == M
        out_spec = pl.BlockSpec((1, tm, tn), lambda i, j, k: (j, 0, 0))
        out_shape = jax.ShapeDtypeStruct((N // tn, M, tn), BF16)
    else:
        out_spec = pl.BlockSpec((tm, tn), lambda i, j, k: (i, j))
        out_shape = jax.ShapeDtypeStruct((M, N), BF16)
    return pl.pallas_call(
        body, name=name, grid=(M // tm, N // tn, nk),
        in_specs=[pl.BlockSpec((tk, tm), lambda i, j, k: (k, i)), pl.BlockSpec((tk, tn), lambda i, j, k: (k, j))],
        out_specs=out_spec, out_shape=out_shape, scratch_shapes=[pltpu.VMEM((tm, tn), F32)],
        compiler_params=_cp("parallel", "parallel", "arbitrary"))(a, b)


def _halo_prev_spec(rb, width):
    return pl.BlockSpec((8, width), lambda i: (jnp.maximum(i * (rb // 8) - 1, 0), 0))


def _halo_next_spec(rb, width, T):
    return pl.BlockSpec((8, width), lambda i: (jnp.minimum((i + 1) * (rb // 8), T // 8 - 1), 0))


LANES = 128
FF_STRIPS = D_FF // LANES
ROW_CHUNK = 32


def _strip(j, base=0):
    return pl.ds(pl.multiple_of(base + j * LANES, LANES), LANES)


def _up_proj_act(h, w_up, w, b, rb=256):
    T, K = h.shape
    W = w_up.shape[1]
    rb = min(rb, T)
    nb = T // rb

    def body(h_ref, wup_ref, w_ref, b_ref, up_ref, act_ref, prev_scr, ext_scr, tail_scr):
        @pl.when(pl.program_id(0) == 0)
        def _():
            tail_scr[...] = jnp.zeros_like(tail_scr)
            prev_scr[...] = jnp.zeros_like(prev_scr)
        up_ref[...] = jnp.dot(h_ref[...], wup_ref[...], preferred_element_type=F32)
        for j in range(FF_STRIPS):
            slot = j % 2
            halves = (slice(j * LANES, (j + 1) * LANES), slice(D_FF + j * LANES, D_FF + (j + 1) * LANES))
            wv = [w_ref[:, cols] for cols in halves]
            bv = [b_ref[:, cols] for cols in halves]
            for hh, cols in enumerate(halves):
                ext_scr[slot, hh, 0:8] = tail_scr[:, cols]
                ext_scr[slot, hh, 8:] = prev_scr[:, cols]
            for r0 in range(0, rb, ROW_CHUNK):
                n = min(ROW_CHUNK, rb - r0)
                c = [ext_scr[slot, hh, 6 + r0:6 + r0 + n] * wv[hh][0:1] + ext_scr[slot, hh, 7 + r0:7 + r0 + n] * wv[hh][1:2]
                     + ext_scr[slot, hh, 8 + r0:8 + r0 + n] * wv[hh][2:3] + bv[hh] for hh in range(2)]
                act_ref[r0:r0 + n, halves[0]] = (_silu(c[0]) * c[1]).astype(BF16)
        tail_scr[...] = prev_scr[rb - 8:rb, :]
        prev_scr[...] = up_ref[...]

    cur = lambda i: (jnp.minimum(i, nb - 1), 0)
    return pl.pallas_call(
        body, name="up_proj_act", grid=(nb + 1,),
        in_specs=[pl.BlockSpec((rb, K), cur), pl.BlockSpec((K, W), lambda i: (0, 0)),
                  pl.BlockSpec((3, W), lambda i: (0, 0)), pl.BlockSpec((1, W), lambda i: (0, 0))],
        out_specs=[pl.BlockSpec((rb, W), cur), pl.BlockSpec((rb, D_FF), lambda i: (jnp.maximum(i - 1, 0), 0))],
        out_shape=[jax.ShapeDtypeStruct((T, W), F32), jax.ShapeDtypeStruct((T, D_FF), BF16)],
        scratch_shapes=[pltpu.VMEM((rb, W), F32), pltpu.VMEM((2, 2, rb + 8, LANES), F32), pltpu.VMEM((8, W), F32)],
        compiler_params=_cp("arbitrary"))(h, w_up, w, b)


def _ffn_act_bwd(up, dact, w, b, rb=256, dep=None):
    T, W = up.shape
    rb = min(rb, T)
    nb = T // rb
    re = rb + 8

    def body(up_ref, prev_ref, next_ref, da_ref, danext_ref, w_ref, b_ref, *rest):
        dup_ref, gw_ref, gb_ref, ext_scr, dc_scr = rest[-5:]
        i = pl.program_id(0)

        @pl.when(i == 0)
        def _():
            gw_ref[...] = jnp.zeros_like(gw_ref)
            gb_ref[...] = jnp.zeros_like(gb_ref)
        last = i == nb - 1

        def fold8(a):
            return jnp.sum(a.reshape(a.shape[0] // 8, 8, LANES), axis=0)

        def strip(j, slot):
            halves = (_strip(j), _strip(j, D_FF))
            wv = [w_ref[:, cols] for cols in halves]
            bv = [b_ref[:, cols] for cols in halves]
            for h, cols in enumerate(halves):
                ext_scr[slot, h,0:8] = jnp.where(i > 0, prev_ref[:, cols], 0.0)
                ext_scr[slot, h,8:8 + rb] = up_ref[:, cols]
                ext_scr[slot, h,8 + rb:] = next_ref[:, cols]
            gb = [jnp.zeros((8, LANES), F32) for _ in range(2)]
            gw = [[jnp.zeros((8, LANES), F32) for _ in range(3)] for _ in range(2)]
            for r0 in range(0, re, ROW_CHUNK):
                n = min(ROW_CHUNK, re - r0)
                tp = [[ext_scr[slot, h,6 + k + r0:6 + k + r0 + n] for k in range(3)] for h in range(2)]
                c = [tp[h][0] * wv[h][0:1] + tp[h][1] * wv[h][1:2] + tp[h][2] * wv[h][2:3] + bv[h] for h in range(2)]
                if r0 < rb:
                    da = da_ref[r0:r0 + n, halves[0]]
                else:
                    da = jnp.where(last, 0.0, danext_ref[:, halves[0]])
                s = _sigmoid(c[0])
                gs = c[0] * s
                dcs = (da * c[1] * (s + gs * (1.0 - s)), da * gs)
                for h in range(2):
                    dc_scr[slot, h,r0:r0 + n] = dcs[h]
                    if r0 < rb:
                        gb[h] = gb[h] + fold8(dcs[h])
                        for k in range(3):
                            gw[h][k] = gw[h][k] + fold8(tp[h][k] * dcs[h])
            for r0 in range(0, rb, ROW_CHUNK):
                n = min(ROW_CHUNK, rb - r0)
                for h, cols in enumerate(halves):
                    dup = (dc_scr[slot, h,r0:r0 + n] * wv[h][2:3] + dc_scr[slot, h,r0 + 1:r0 + 1 + n] * wv[h][1:2]
                           + dc_scr[slot, h,r0 + 2:r0 + 2 + n] * wv[h][0:1])
                    dup_ref[r0:r0 + n, cols] = dup.astype(BF16)
            for h, cols in enumerate(halves):
                gb_ref[:, cols] += jnp.sum(gb[h], axis=0, keepdims=True)
                for k in range(3):
                    gw_ref[k:k + 1, cols] += jnp.sum(gw[h][k], axis=0, keepdims=True)

        def pair(jj, carry):
            strip(2 * jj, 0)
            strip(2 * jj + 1, 1)
            return carry

        lax.fori_loop(0, FF_STRIPS // 2, pair, 0)

    in_specs, args = _with_dep(
        [pl.BlockSpec((rb, W), lambda i: (i, 0)), _halo_prev_spec(rb, W), _halo_next_spec(rb, W, T),
         pl.BlockSpec((rb, D_FF), lambda i: (i, 0)), _halo_next_spec(rb, D_FF, T),
         pl.BlockSpec((3, W), lambda i: (0, 0)), pl.BlockSpec((1, W), lambda i: (0, 0))],
        [up, up, up, dact, dact, w, b], dep)
    return pl.pallas_call(
        body, name="ffn_act_bwd", grid=(nb,), in_specs=in_specs,
        out_specs=[pl.BlockSpec((rb, W), lambda i: (i, 0)), pl.BlockSpec((3, W), lambda i: (0, 0)),
                   pl.BlockSpec((1, W), lambda i: (0, 0))],
        out_shape=[jax.ShapeDtypeStruct((T, W), BF16), jax.ShapeDtypeStruct((3, W), F32),
                   jax.ShapeDtypeStruct((1, W), F32)],
        scratch_shapes=[pltpu.VMEM((2, 2, rb + 16, LANES), F32), pltpu.VMEM((2, 2, re, LANES), F32)],
        compiler_params=_cp("arbitrary"))(*args)


def _lane_iota(shape):
    return lax.broadcasted_iota(jnp.int32, shape, len(shape) - 1)


def _dn_act(p, conv_w, alog_row, dtb_row, rb=256):
    T = p.shape[0]
    rb = min(rb, T)
    W3 = 3 * DN_WIDTH

    def body(p_ref, halo_ref, ba_ref, w_ref, al_ref, dt_ref, q_ref, k_ref, v_ref, bg_ref, ext_scr):
        first = pl.program_id(0) == 0
        outs = (q_ref, k_ref, v_ref)
        for j in range(3 * N_HEADS):
            kind, h = divmod(j, N_HEADS)
            cols = slice(j * HEAD_DIM, (j + 1) * HEAD_DIM)
            cur = p_ref[:, cols]
            ext_scr[j, 0:8] = jnp.where(first, 0.0, halo_ref[:, cols])
            ext_scr[j, 8:] = cur
            wv = w_ref[:, cols]
            s = _silu(ext_scr[j, 5:5 + rb] * wv[0:1] + ext_scr[j, 6:6 + rb] * wv[1:2]
                      + ext_scr[j, 7:7 + rb] * wv[2:3] + cur * wv[3:4])
            if kind < 2:
                scale = HEAD_DIM ** -0.5 if kind == 0 else 1.0
                s = s * (lax.rsqrt(jnp.sum(s * s, axis=-1, keepdims=True) + EPS) * scale)
            outs[kind][:, h * HEAD_DIM:(h + 1) * HEAD_DIM] = s
        ba = ba_ref[...]
        lane = _lane_iota(ba.shape)
        beta = _sigmoid(ba)
        g = -jnp.exp(al_ref[...]) * _softplus(ba + dt_ref[...])
        bg_ref[...] = jnp.where(lane < N_HEADS, beta, jnp.where(lane < 2 * N_HEADS, g, 0.0))

    row512 = pl.BlockSpec((rb, DN_WIDTH), lambda i: (i, 0))
    row128 = pl.BlockSpec((rb, 128), lambda i: (i, 0))
    vec128 = pl.BlockSpec((1, 128), lambda i: (0, 0))
    return pl.pallas_call(
        body, name="dn_act", grid=(T // rb,),
        in_specs=[pl.BlockSpec((rb, W3), lambda i: (i, 0)), _halo_prev_spec(rb, W3),
                  pl.BlockSpec((rb, 128), lambda i: (i, BA_COL // 128)),
                  pl.BlockSpec((4, W3), lambda i: (0, 0)), vec128, vec128],
        out_specs=[row512, row512, row512, row128],
        out_shape=[jax.ShapeDtypeStruct((T, DN_WIDTH), F32)] * 3 + [jax.ShapeDtypeStruct((T, 128), F32)],
        scratch_shapes=[pltpu.VMEM((3 * N_HEADS, rb + 8, HEAD_DIM), F32)],
        compiler_params=_cp("parallel"))(p, p, p, conv_w, alog_row, dtb_row)


def _dn_act_bwd(p, conv_w, alog_row, dtb_row, dq, dk, dv, dbg, dp_mid, rb=256):
    T = p.shape[0]
    rb = min(rb, T)
    nb = T // rb
    re = rb + 8
    W3 = 3 * DN_WIDTH

    def body(p_ref, prev_ref, next_ref, ba_ref, w_ref, al_ref, dt_ref, dq_ref, dqn_ref, dk_ref, dkn_ref,
             dv_ref, dvn_ref, dbg_ref, mid_ref, draw_ref, gw_ref, gad_ref, ext_scr, dc_scr):
        i = pl.program_id(0)
        draw_ref[:, W3:2 * W3] = mid_ref[...]

        @pl.when(i == 0)
        def _():
            gw_ref[...] = jnp.zeros_like(gw_ref)
            gad_ref[...] = jnp.zeros_like(gad_ref)
        row = lax.broadcasted_iota(jnp.int32, (re, 1), 0)
        live = (row < rb) | (i < nb - 1)
        d_refs = ((dq_ref, dqn_ref), (dk_ref, dkn_ref), (dv_ref, dvn_ref))
        for j in range(3 * N_HEADS):
            kind, h = divmod(j, N_HEADS)
            cols = slice(j * HEAD_DIM, (j + 1) * HEAD_DIM)
            hcols = slice(h * HEAD_DIM, (h + 1) * HEAD_DIM)
            ext_scr[j, 0:8] = jnp.where(i > 0, prev_ref[:, cols], 0.0)
            ext_scr[j, 8:8 + rb] = p_ref[:, cols]
            ext_scr[j, 8 + rb:] = next_ref[:, cols]
            tp = [ext_scr[j, 5 + k:5 + k + re] for k in range(4)]
            wv = w_ref[:, cols]
            c = tp[0] * wv[0:1] + tp[1] * wv[1:2] + tp[2] * wv[2:3] + tp[3] * wv[3:4]
            sg = _sigmoid(c)
            s = c * sg
            d_in = jnp.where(live, jnp.concatenate([d_refs[kind][0][:, hcols], d_refs[kind][1][:, hcols]], axis=0), 0.0)
            if kind < 2:
                scale = HEAD_DIM ** -0.5 if kind == 0 else 1.0
                n = lax.rsqrt(jnp.sum(s * s, axis=-1, keepdims=True) + EPS)
                hat = s * n
                d_in = (n * scale) * (d_in - hat * jnp.sum(hat * d_in, axis=-1, keepdims=True))
            dc = d_in * (sg + s * (1.0 - sg))
            dc_scr[j] = dc
            dcc = dc[0:rb]
            draw = (dcc * wv[3:4] + dc_scr[j, 1:1 + rb] * wv[2:3] + dc_scr[j, 2:2 + rb] * wv[1:2]
                    + dc_scr[j, 3:3 + rb] * wv[0:1])
            draw_ref[:, cols] = draw.astype(BF16)
            for k in range(4):
                gw_ref[k:k + 1, cols] += jnp.sum(tp[k][0:rb] * dcc, axis=0, keepdims=True)
        ba = ba_ref[...]
        dbg = dbg_ref[...]
        lane = _lane_iota(ba.shape)
        beta = _sigmoid(ba)
        ea = jnp.exp(al_ref[...])
        z = ba + dt_ref[...]
        d_a = dbg * (-ea) * _sigmoid(z)
        dba = jnp.where(lane < N_HEADS, dbg * beta * (1.0 - beta), jnp.where(lane < 2 * N_HEADS, d_a, 0.0))
        draw_ref[:, BA_COL:] = dba.astype(BF16)
        isg = (lane >= N_HEADS) & (lane < 2 * N_HEADS)
        g = -ea * _softplus(z)
        gad_ref[0:1, :] += jnp.sum(jnp.where(isg, dbg * g, 0.0), axis=0, keepdims=True)
        gad_ref[1:2, :] += jnp.sum(jnp.where(isg, d_a, 0.0), axis=0, keepdims=True)

    row512 = pl.BlockSpec((rb, DN_WIDTH), lambda i: (i, 0))
    row128 = pl.BlockSpec((rb, 128), lambda i: (i, 0))
    vec128 = pl.BlockSpec((1, 128), lambda i: (0, 0))
    next512 = _halo_next_spec(rb, DN_WIDTH, T)
    return pl.pallas_call(
        body, name="dn_act_bwd", grid=(nb,),
        in_specs=[pl.BlockSpec((rb, W3), lambda i: (i, 0)), _halo_prev_spec(rb, W3), _halo_next_spec(rb, W3, T),
                  pl.BlockSpec((rb, 128), lambda i: (i, BA_COL // 128)),
                  pl.BlockSpec((4, W3), lambda i: (0, 0)), vec128, vec128,
                  row512, next512, row512, next512, row512, next512, row128,
                  pl.BlockSpec((rb, W3), lambda i: (i, 0))],
        out_specs=[pl.BlockSpec((rb, PROJ_PAD), lambda i: (i, 0)),
                   pl.BlockSpec((4, W3), lambda i: (0, 0)), pl.BlockSpec((2, 128), lambda i: (0, 0))],
        out_shape=[jax.ShapeDtypeStruct((T, PROJ_PAD), BF16),
                   jax.ShapeDtypeStruct((4, W3), F32), jax.ShapeDtypeStruct((2, 128), F32)],
        scratch_shapes=[pltpu.VMEM((3 * N_HEADS, rb + 16, HEAD_DIM), F32), pltpu.VMEM((3 * N_HEADS, re, HEAD_DIM), F32)],
        compiler_params=_cp("arbitrary"))(p, p, p, p, conv_w, alog_row, dtb_row, dq, dq, dk, dk, dv, dv, dbg, dp_mid)


def _tri(incl):
    ii = lax.broadcasted_iota(jnp.int32, (CHUNK, CHUNK), 0)
    jj = lax.broadcasted_iota(jnp.int32, (CHUNK, CHUNK), 1)
    return ii, jj, ((ii >= jj) if incl else (ii > jj))


def _dn_chunk(k, bg, cb=4):
    T = k.shape[0]
    N = T // CHUNK
    cb = min(cb, N)

    def body(k_ref, bg_ref, gc_ref, gct_ref, l_ref):
        ii, jj, incl = _tri(True)
        tri = incl.astype(F32)
        U = range(cb)
        bgv = [bg_ref[u * CHUNK:(u + 1) * CHUNK, :] for u in U]
        gc = [jnp.dot(tri, bgv[u], precision=lax.Precision.HIGHEST, preferred_element_type=F32) for u in U]
        gct = [gc[u].T for u in U]
        kk = [[None] * N_HEADS for _ in U]
        for u in U:
            gc_ref[u * CHUNK:(u + 1) * CHUNK, :] = gc[u]
            gct_ref[u] = gct[u][0:8]
            for h in range(N_HEADS):
                kh = k_ref[u * CHUNK:(u + 1) * CHUNK, h * HEAD_DIM:(h + 1) * HEAD_DIM]
                kk[u][h] = _nt(kh * bgv[u][:, h:h + 1], kh)
        for u in U:
            for h in range(N_HEADS):
                gcol = gc[u][:, N_HEADS + h:N_HEADS + h + 1]
                grow = gct[u][N_HEADS + h:N_HEADS + h + 1, :]
                l_ref[u, h] = kk[u][h] * jnp.exp(jnp.where(ii > jj, gcol - grow, NEG))

    rows = cb * CHUNK
    return pl.pallas_call(
        body, name="dn_chunk", grid=(N // cb,),
        in_specs=[pl.BlockSpec((rows, DN_WIDTH), lambda n: (n, 0)), pl.BlockSpec((rows, 128), lambda n: (n, 0))],
        out_specs=[pl.BlockSpec((rows, 128), lambda n: (n, 0)), pl.BlockSpec((cb, 8, CHUNK), lambda n: (n, 0, 0)),
                   pl.BlockSpec((cb, N_HEADS, CHUNK, CHUNK), lambda n: (n, 0, 0, 0))],
        out_shape=[jax.ShapeDtypeStruct((T, 128), F32), jax.ShapeDtypeStruct((N, 8, CHUNK), F32),
                   jax.ShapeDtypeStruct((N, N_HEADS, CHUNK, CHUNK), F32)],
        compiler_params=_cp("parallel"))(k, bg)


def _tri_inv(lt):
    S = lt.shape[1]

    def body(l_ref, a_ref):
        sub = lax.broadcasted_iota(jnp.int32, (8, S), 0)
        groups = CHUNK // 8
        for i in range(CHUNK):
            acc = [((sub + 8 * k) == i).astype(F32) for k in range(groups)]
            for jb in range((i + 7) // 8):
                nk = jb + 1

                def step(j, carry, nk=nk, i=i):
                    lrow = l_ref[pl.ds(i * CHUNK + j, 1), :]
                    return tuple(carry[k] - lrow * a_ref[j, 8 * k:8 * k + 8, :] for k in range(nk))

                acc[:nk] = list(lax.fori_loop(8 * jb, min(8 * jb + 8, i), step, tuple(acc[:nk])))
            for k in range(groups):
                a_ref[i, 8 * k:8 * k + 8, :] = acc[k]

    return pl.pallas_call(
        body, name="tri_inv", out_shape=jax.ShapeDtypeStruct((CHUNK, CHUNK, S), F32),
        compiler_params=pltpu.CompilerParams(vmem_limit_bytes=VMEM_LIMIT))(lt)


def _dn_head_terms(qh, kh, vh, beta, gcol, grow):
    ii, jj, incl = _tri(True)
    gam = jnp.exp(jnp.where(incl, gcol - grow, NEG))
    glast = grow[:, CHUNK - 1:CHUNK]
    cd = jnp.exp(glast)
    shape = (CHUNK, HEAD_DIM)
    E = jnp.broadcast_to(jnp.exp(gcol), shape)
    Fd = jnp.broadcast_to(jnp.exp(glast - gcol), shape)
    beta = jnp.broadcast_to(beta, shape)
    kb = kh * beta
    return dict(ii=ii, jj=jj, gam=gam, E=E, F=Fd, beta=beta, cd=cd, kb=kb, vb=vh * beta, W=kb * E, qE=qh * E,
                kt=kh * Fd)


def _apply_a(a, u):
    hi, lo = _split(a)
    ub = _bf(u)
    return jnp.dot(hi, ub, preferred_element_type=F32) + jnp.dot(lo, ub, preferred_element_type=F32)


def _dn_scan(q, k, v, bg, gc, gct, a):
    T = q.shape[0]
    N = T // CHUNK
    cb = min(SCAN_CHUNKS_FWD, N)

    def body(q_ref, k_ref, v_ref, bg_ref, gc_ref, gct_ref, a_ref, o_ref, sall_ref, s_ref):
        @pl.when(pl.program_id(0) == 0)
        def _():
            s_ref[...] = jnp.zeros_like(s_ref)
        H = range(N_HEADS)
        sl = [slice(h * HEAD_DIM, (h + 1) * HEAD_DIM) for h in H]
        pre = []
        for u in range(cb):
            r = slice(u * CHUNK, (u + 1) * CHUNK)
            bgv, gcv, gctv = bg_ref[r, :], gc_ref[r, :], gct_ref[u]
            q_, k_ = [q_ref[r, s] for s in sl], [k_ref[r, s] for s in sl]
            t = [_dn_head_terms(q_[h], k_[h], v_ref[r, sl[h]], bgv[:, h:h + 1],
                                gcv[:, N_HEADS + h:N_HEADS + h + 1], gctv[N_HEADS + h:N_HEADS + h + 1, :]) for h in H]
            P = [_nt(q_[h], k_[h]) * t[h]["gam"] for h in H]
            pre.append((r, t, P))
        S = [s_ref[h] for h in H]
        for u in range(cb):
            r, t, P = pre[u]
            for h in H:
                sall_ref[u, h] = S[h]
            WS = [_nn(t[h]["W"], S[h]) for h in H]
            qS = [_nn(t[h]["qE"], S[h]) for h in H]
            vn = [_apply_a(a_ref[u, h], t[h]["vb"] - WS[h]) for h in H]
            Pv = [_nn(P[h], vn[h]) for h in H]
            kv = [_tn(t[h]["kt"], vn[h]) for h in H]
            for h in H:
                o_ref[r, sl[h]] = qS[h] + Pv[h]
            S = [t[h]["cd"] * S[h] + kv[h] for h in H]
        for h in H:
            s_ref[h] = S[h]

    row512 = pl.BlockSpec((cb * CHUNK, DN_WIDTH), lambda n: (n, 0))
    row128 = pl.BlockSpec((cb * CHUNK, 128), lambda n: (n, 0))
    return pl.pallas_call(
        body, name="dn_scan", grid=(N // cb,),
        in_specs=[row512, row512, row512, row128, row128, pl.BlockSpec((cb, 8, CHUNK), lambda n: (n, 0, 0)),
                  pl.BlockSpec((cb, N_HEADS, CHUNK, CHUNK), lambda n: (n, 0, 0, 0))],
        out_specs=[row512, pl.BlockSpec((cb, N_HEADS, HEAD_DIM, HEAD_DIM), lambda n: (n, 0, 0, 0))],
        out_shape=[jax.ShapeDtypeStruct((T, DN_WIDTH), F32),
                   jax.ShapeDtypeStruct((N, N_HEADS, HEAD_DIM, HEAD_DIM), F32)],
        scratch_shapes=[pltpu.VMEM((N_HEADS, HEAD_DIM, HEAD_DIM), F32)],
        compiler_params=_cp("arbitrary"))(q, k, v, bg, gc, gct, a)


def _dn_scan_bwd(q, k, v, bg, gc, gct, a, a_t, sall, do, dep=None):
    T = q.shape[0]
    N = T // CHUNK

    cb = min(SCAN_CHUNKS, N)
    nb = N // cb

    def body(q_ref, k_ref, v_ref, bg_ref, gc_ref, gct_ref, a_ref, at_ref, sall_ref, do_ref, *rest):
        dq_ref, dk_ref, dv_ref, dbg_ref, ds_ref = rest[-5:]
        @pl.when(pl.program_id(0) == 0)
        def _():
            ds_ref[...] = jnp.zeros_like(ds_ref)
        lane = _lane_iota((CHUNK, 128))
        rowi = lax.broadcasted_iota(jnp.int32, (CHUNK, 1), 0)
        ii, jj, _ = _tri(True)
        rev = (jj >= ii).astype(F32)
        H = range(N_HEADS)
        sl = [slice(h * HEAD_DIM, (h + 1) * HEAD_DIM) for h in H]
        pre = {}
        for u in reversed(range(cb)):
            r = slice(u * CHUNK, (u + 1) * CHUNK)
            bgv, gcv, gctv = bg_ref[r, :], gc_ref[r, :], gct_ref[u]
            q_, k_, v_ = [q_ref[r, s] for s in sl], [k_ref[r, s] for s in sl], [v_ref[r, s] for s in sl]
            dO = [do_ref[r, s] for s in sl]
            t = [_dn_head_terms(q_[h], k_[h], v_[h], bgv[:, h:h + 1], gcv[:, N_HEADS + h:N_HEADS + h + 1],
                                gctv[N_HEADS + h:N_HEADS + h + 1, :]) for h in H]
            beta = [t[h]["beta"] for h in H]
            S = [sall_ref[u, h] for h in H]
            A = [a_ref[u, h] for h in H]
            WS = [_nn(t[h]["W"], S[h]) for h in H]
            KK = [_nt(t[h]["kb"], k_[h]) for h in H]
            QK = [_nt(q_[h], k_[h]) for h in H]
            d_qE = [_nt(dO[h], S[h]) for h in H]
            vn = [_apply_a(A[h], t[h]["vb"] - WS[h]) for h in H]
            PtdO = [_tn(QK[h] * t[h]["gam"], dO[h]) for h in H]
            qEdO = [_tn(t[h]["qE"], dO[h]) for h in H]
            dOvn = [_nt(dO[h], vn[h]) for h in H]
            dQK = [jnp.where(ii >= jj, dOvn[h], 0.0) * t[h]["gam"] for h in H]
            dQKk = [_nn(dQK[h], k_[h]) for h in H]
            dQKq = [_tn(dQK[h], q_[h]) for h in H]
            pre[u] = (r, q_, k_, v_, beta, t, S, A, KK, QK, d_qE, vn, PtdO, qEdO, dQK, dQKk, dQKq)
        dSn = [ds_ref[h] for h in H]
        for u in reversed(range(cb)):
            r, q_, k_, v_, beta, t, S, A, KK, QK, d_qE, vn, PtdO, qEdO, dQK, dQKk, dQKq = pre[u]
            gam, E, Fd, cd, kb = ([t[h][n] for h in H] for n in ("gam", "E", "F", "cd", "kb"))
            ktdS = [_nn(t[h]["kt"], dSn[h]) for h in H]
            dU = [_apply_a(at_ref[u, h], PtdO[h] + ktdS[h]) for h in H]
            d_kt = [_nt(vn[h], dSn[h]) for h in H]
            dUvn = [_nt(dU[h], vn[h]) for h in H]
            dUS = [_nt(dU[h], S[h]) for h in H]
            WdU = [_tn(t[h]["W"], dU[h]) for h in H]
            d_cd = [jnp.sum(S[h] * dSn[h]) for h in H]
            dSn = [cd[h] * dSn[h] + qEdO[h] - WdU[h] for h in H]
            dKK = [jnp.where(ii > jj, -dUvn[h], 0.0) * gam[h] for h in H]
            dKKk = [_nn(dKK[h], k_[h]) for h in H]
            dKKkb = [_tn(dKK[h], kb[h]) for h in H]
            dbeta_arr = jnp.zeros((CHUNK, 128), F32)
            dgc_arr = jnp.zeros((CHUNK, 128), F32)
            for h in H:
                dW = -dUS[h]
                dq_ref[r, sl[h]] = dQKk[h] + d_qE[h] * E[h]
                d_kb = dKKk[h] + dW * E[h]
                dk_ref[r, sl[h]] = dQKq[h] + dKKkb[h] + d_kb * beta[h] + d_kt[h] * Fd[h]
                dv_ref[r, sl[h]] = dU[h] * beta[h]
                Z = dQK[h] * QK[h] + dKK[h] * KK[h]
                dbeta = jnp.sum(dU[h] * v_[h] + d_kb * k_[h], axis=-1, keepdims=True)
                m_e = (dW * kb[h] + d_qE[h] * q_[h]) * E[h]
                m_f = d_kt[h] * k_[h] * Fd[h]
                zdiag = jnp.where(ii == jj, jnp.sum(Z, axis=0, keepdims=True), 0.0)
                dgc = (jnp.sum(m_e - m_f, axis=-1, keepdims=True) + jnp.sum(Z - zdiag, axis=-1, keepdims=True)
                       + jnp.where(rowi == CHUNK - 1, jnp.sum(m_f) + d_cd[h] * cd[h], 0.0))
                dbeta_arr = dbeta_arr + jnp.where(lane == h, dbeta, 0.0)
                dgc_arr = dgc_arr + jnp.where(lane == N_HEADS + h, dgc, 0.0)
            dbg_ref[r, :] = dbeta_arr + jnp.dot(rev, dgc_arr, precision=lax.Precision.HIGHEST,
                                                preferred_element_type=F32)
        for h in H:
            ds_ref[h] = dSn[h]

    row512 = pl.BlockSpec((cb * CHUNK, DN_WIDTH), lambda n: (nb - 1 - n, 0))
    row128 = pl.BlockSpec((cb * CHUNK, 128), lambda n: (nb - 1 - n, 0))
    in_specs, args = _with_dep(
        [row512, row512, row512, row128, row128,
         pl.BlockSpec((cb, 8, CHUNK), lambda n: (nb - 1 - n, 0, 0)),
         pl.BlockSpec((cb, N_HEADS, CHUNK, CHUNK), lambda n: (nb - 1 - n, 0, 0, 0)),
         pl.BlockSpec((cb, N_HEADS, CHUNK, CHUNK), lambda n: (nb - 1 - n, 0, 0, 0)),
         pl.BlockSpec((cb, N_HEADS, HEAD_DIM, HEAD_DIM), lambda n: (nb - 1 - n, 0, 0, 0)), row512],
        [q, k, v, bg, gc, gct, a, a_t, sall, do], dep)
    return pl.pallas_call(
        body, name="dn_scan_bwd", grid=(nb,), in_specs=in_specs,
        out_specs=[row512, row512, row512, row128],
        out_shape=[jax.ShapeDtypeStruct((T, DN_WIDTH), F32)] * 3 + [jax.ShapeDtypeStruct((T, 128), F32)],
        scratch_shapes=[pltpu.VMEM((N_HEADS, HEAD_DIM, HEAD_DIM), F32)],
        compiler_params=_cp("arbitrary"))(*args)


MIX_BLOCKS_FWD = 4
MIX_BLOCKS_BWD = 2


def _sg_mask():
    ii = lax.broadcasted_iota(jnp.int32, (SG_BLOCK, SG_BLOCK), 0) // CHUNK
    jj = lax.broadcasted_iota(jnp.int32, (SG_BLOCK, SG_BLOCK), 1) // CHUNK
    return jj <= ii


def _mix_fwd(o, p, ong, sgn, sgw, sgbt):
    T = o.shape[0]
    rb = min(MIX_BLOCKS_FWD * SG_BLOCK, T)

    def body(o_ref, gate_ref, u_ref, vg_ref, ong_ref, sgn_ref, sgw_ref, sgbt_ref, mix_ref):
        mask = _sg_mask()
        for u0 in range(0, rb, SG_BLOCK):
            rows = slice(u0, u0 + SG_BLOCK)
            for h in range(N_HEADS):
                sl = slice(h * HEAD_DIM, (h + 1) * HEAD_DIM)
                oh = o_ref[rows, sl]
                r = lax.rsqrt(jnp.mean(oh * oh, axis=-1, keepdims=True) + EPS)
                mix_ref[rows, sl] = (oh * r * ong_ref[...] * _silu(gate_ref[rows, sl])).astype(BF16)
            for gi in range(SG_GROUPS):
                sl = slice(gi * SG_BLOCK, (gi + 1) * SG_BLOCK)
                gv = _gelu(vg_ref[rows, sl])
                r = lax.rsqrt(jnp.mean(gv * gv, axis=-1, keepdims=True) + EPS)
                vh = gv * r * sgn_ref[:, sl]
                s = _nn(jnp.where(mask, sgw_ref[gi], 0.0), vh) + sgbt_ref[:, gi:gi + 1]
                mix_ref[rows, DN_WIDTH + gi * SG_BLOCK:DN_WIDTH + (gi + 1) * SG_BLOCK] = (
                    _gelu(u_ref[rows, sl]) * s).astype(BF16)

    def col(c):
        return pl.BlockSpec((rb, 512), lambda i: (i, c))
    return pl.pallas_call(
        body, name="mix_fwd", grid=(T // rb,),
        in_specs=[pl.BlockSpec((rb, DN_WIDTH), lambda i: (i, 0)), col(3), col(4), col(5),
                  pl.BlockSpec((1, 128), lambda i: (0, 0)), pl.BlockSpec((1, SG_WIDTH), lambda i: (0, 0)),
                  pl.BlockSpec((SG_GROUPS, SG_BLOCK, SG_BLOCK), lambda i: (0, 0, 0)),
                  pl.BlockSpec((SG_BLOCK, 128), lambda i: (0, 0))],
        out_specs=pl.BlockSpec((rb, D_MODEL), lambda i: (i, 0)),
        out_shape=jax.ShapeDtypeStruct((T, D_MODEL), BF16),
        compiler_params=_cp("parallel"))(o, p, p, p, ong, sgn, sgw, sgbt)


def _mix_bwd(o, p, ong, sgn, sgw, sgbt, dmix, dep=None):
    T = o.shape[0]
    rb = min(MIX_BLOCKS_BWD * SG_BLOCK, T)

    def body(o_ref, gate_ref, u_ref, vg_ref, ong_ref, sgn_ref, sgw_ref, sgbt_ref, dmix_ref, *rest):
        do_ref, dp_ref, gong_ref, gsgn_ref, gsgw_ref, gsgbt_ref = rest[-6:]
        @pl.when(pl.program_id(0) == 0)
        def _():
            gong_ref[...] = jnp.zeros_like(gong_ref)
            gsgn_ref[...] = jnp.zeros_like(gsgn_ref)
            gsgw_ref[...] = jnp.zeros_like(gsgw_ref)
            gsgbt_ref[...] = jnp.zeros_like(gsgbt_ref)
        mask = _sg_mask()
        lane = _lane_iota((SG_BLOCK, 128))
        for u0 in range(0, rb, SG_BLOCK):
            rows = slice(u0, u0 + SG_BLOCK)
            for h in range(N_HEADS):
                sl = slice(h * HEAD_DIM, (h + 1) * HEAD_DIM)
                oh = o_ref[rows, sl]
                dm = dmix_ref[rows, sl]
                r = lax.rsqrt(jnp.mean(oh * oh, axis=-1, keepdims=True) + EPS)
                oh_hat = oh * r
                gt = gate_ref[rows, sl]
                sg = _silu(gt)
                dp_ref[rows, sl] = (dm * oh_hat * ong_ref[...] * _dsilu(gt)).astype(BF16)
                dn_ = dm * sg
                gong_ref[...] += jnp.sum(dn_ * oh_hat, axis=0, keepdims=True)
                dhat = dn_ * ong_ref[...]
                do_ref[rows, sl] = r * (dhat - oh_hat * jnp.mean(dhat * oh_hat, axis=-1, keepdims=True))
            for gi in range(SG_GROUPS):
                sl = slice(gi * SG_BLOCK, (gi + 1) * SG_BLOCK)
                vraw = vg_ref[rows, sl]
                gv = _gelu(vraw)
                r = lax.rsqrt(jnp.mean(gv * gv, axis=-1, keepdims=True) + EPS)
                vhat = gv * r
                vn = vhat * sgn_ref[:, sl]
                wm = jnp.where(mask, sgw_ref[gi], 0.0)
                s = _nn(wm, vn) + sgbt_ref[:, gi:gi + 1]
                uraw = u_ref[rows, sl]
                dm = dmix_ref[rows, DN_WIDTH + gi * SG_BLOCK:DN_WIDTH + (gi + 1) * SG_BLOCK]
                dp_ref[rows, DN_WIDTH + gi * SG_BLOCK:DN_WIDTH + (gi + 1) * SG_BLOCK] = (
                    dm * s * _dgelu(uraw)).astype(BF16)
                ds = dm * _gelu(uraw)
                gsgbt_ref[...] += jnp.where(lane == gi, jnp.sum(ds, axis=-1, keepdims=True), 0.0)
                gsgw_ref[gi] += jnp.where(mask, _nt(ds, vn), 0.0)
                dvn = _tn(wm, ds)
                gsgn_ref[:, sl] += jnp.sum(dvn * vhat, axis=0, keepdims=True)
                dhat = dvn * sgn_ref[:, sl]
                dgv = r * (dhat - vhat * jnp.mean(dhat * vhat, axis=-1, keepdims=True))
                dp_ref[rows, 2 * DN_WIDTH + gi * SG_BLOCK:2 * DN_WIDTH + (gi + 1) * SG_BLOCK] = (
                    dgv * _dgelu(vraw)).astype(BF16)

    def col(c):
        return pl.BlockSpec((rb, 512), lambda i: (i, c))
    full = lambda *s: pl.BlockSpec(s, lambda i: (0,) * len(s))
    in_specs, args = _with_dep(
        [pl.BlockSpec((rb, DN_WIDTH), lambda i: (i, 0)), col(3), col(4), col(5),
         full(1, 128), full(1, SG_WIDTH), full(SG_GROUPS, SG_BLOCK, SG_BLOCK), full(SG_BLOCK, 128),
         pl.BlockSpec((rb, D_MODEL), lambda i: (i, 0))],
        [o, p, p, p, ong, sgn, sgw, sgbt, dmix], dep)
    return pl.pallas_call(
        body, name="mix_bwd", grid=(T // rb,), in_specs=in_specs,
        out_specs=[pl.BlockSpec((rb, DN_WIDTH), lambda i: (i, 0)), pl.BlockSpec((rb, 3 * 512), lambda i: (i, 0)),
                   full(1, 128), full(1, SG_WIDTH), full(SG_GROUPS, SG_BLOCK, SG_BLOCK), full(SG_BLOCK, 128)],
        out_shape=[jax.ShapeDtypeStruct((T, DN_WIDTH), F32), jax.ShapeDtypeStruct((T, 3 * 512), BF16),
                   jax.ShapeDtypeStruct((1, 128), F32), jax.ShapeDtypeStruct((1, SG_WIDTH), F32),
                   jax.ShapeDtypeStruct((SG_GROUPS, SG_BLOCK, SG_BLOCK), F32),
                   jax.ShapeDtypeStruct((SG_BLOCK, 128), F32)],
        compiler_params=_cp("arbitrary"))(*args)


def _pad_lanes(row, offset=0):
    n = row.shape[1]
    return jnp.pad(row, ((0, 0), (offset, 128 - n - offset)))


def _local_step(x, tgt, w, dep=None, late_weights=None, on_grad=None):
    T = x.shape[0]
    N = T // CHUNK
    on_grad = on_grad or (lambda name, g: None)
    alog_row = _pad_lanes(w["dn_a_log"], N_HEADS)
    dtb_row = _pad_lanes(w["dn_dt_bias"], N_HEADS)
    sgbt = jnp.pad(w["sg_b"].T, ((0, 0), (0, 128 - SG_GROUPS)))

    p, h1, w_in_pad = _in_proj(x, w["attn_norm_g"], w["w_in"], dep=dep)
    q, k, v, bg = _dn_act(p, w["dn_conv_w"], alog_row, dtb_row)
    gc, gct, lmat = _dn_chunk(k, bg)
    lt = lmat.reshape(N * N_HEADS, CHUNK * CHUNK).T
    at = _tri_inv(lt)
    a = at.reshape(CHUNK * CHUNK, N * N_HEADS).T.reshape(N, N_HEADS, CHUNK, CHUNK)
    a_t = at.transpose(1, 0, 2).reshape(CHUNK * CHUNK, N * N_HEADS).T.reshape(N, N_HEADS, CHUNK, CHUNK)
    o, sall = _dn_scan(q, k, v, bg, gc, gct, a)
    mix = _mix_fwd(o, p, w["dn_out_norm_g"], w["sg_norm_g"], w["sg_w"], sgbt)
    if late_weights is not None:
        w = {**w, **late_weights("out_proj", mix)}
    x2, h2 = _out_proj(mix, w["w_out"], x, w["ffn_norm_g"])
    up, act = _up_proj_act(h2, w["w_up"], w["ffn_conv_w"], w["ffn_conv_b"])
    if late_weights is not None:
        w = {**w, **late_weights("down_proj", act)}
    loss, dx3, g_final = _down_proj_loss(act, w["w_down"], x2, tgt, w["final_norm_g"])

    dact = _mm_nt("d_act", dx3, w["w_down"], F32, 512, D_FF)
    g_w_down = _mm_tn("g_w_down", act, dx3, D_FF, 1024, 1024)
    tok = on_grad("w_down", g_w_down)
    dup, g_ffn_conv_w, g_ffn_conv_b = _ffn_act_bwd(up, dact, w["ffn_conv_w"], w["ffn_conv_b"], dep=tok)
    g_w_up = _mm_tn("g_w_up", h2, dup, 1024, 2 * D_FF // 4, 2048, col_major_tiles=True)
    tok = on_grad("w_up", g_w_up)
    dx2, g_ffn_norm = _mm_nt_rms_bwd("d_h2", dup, w["w_up"], x2, w["ffn_norm_g"], dx3, dep=tok)
    dmix = _mm_nt("d_mix", dx2, w["w_out"], F32, 512, 1024)
    g_w_out = _mm_tn("g_w_out", mix, dx2, 1024, 1024, 1024)
    tok = on_grad("w_out", g_w_out)
    do, dp_mid, g_ong, g_sgn, g_sgw, g_sgbt = _mix_bwd(o, p, w["dn_out_norm_g"], w["sg_norm_g"], w["sg_w"], sgbt,
                                                      dmix, dep=tok)
    early = dict(dn_out_norm_g=g_ong, sg_norm_g=g_sgn, sg_w=g_sgw, sg_bt=g_sgbt,
                 ffn_norm_g=g_ffn_norm, ffn_conv_w=g_ffn_conv_w, ffn_conv_b=g_ffn_conv_b, final_norm_g=g_final)
    tok = on_grad("small_early", early)
    dq, dk, dv, dbg = _dn_scan_bwd(q, k, v, bg, gc, gct, a, a_t, sall, do, dep=tok)
    dp, g_dn_conv_w, g_ad = _dn_act_bwd(p, w["dn_conv_w"], alog_row, dtb_row, dq, dk, dv, dbg, dp_mid)
    g_w_in = _mm_tn("g_w_in", h1, dp, 1024, PROJ_PAD, 1024, col_groups=(4, PROJ_COLS // 4))
    tok = on_grad("w_in", g_w_in)
    grad_x, g_attn_norm = _mm_nt_rms_bwd("d_h1", dp, w_in_pad, x, w["attn_norm_g"], dx2, dep=tok)

    grads = dict(attn_norm_g=g_attn_norm, w_in=g_w_in, dn_conv_w=g_dn_conv_w, a_dt=g_ad,
                 w_out=g_w_out, w_up=g_w_up, w_down=g_w_down, **early)
    return loss, grad_x, grads


def _me():
    return lax.axis_index("x"), lax.axis_index("y"), lax.axis_index("c")


def _peer(rel):
    x, y, c = _me()
    return {"x": (1 - x, y, c), "y": (x, 1 - y, c), "xy": (1 - x, 1 - y, c), "c": (x, y, 1 - c)}[rel]


def _chip_of(dev):
    return 2 * dev[0] + dev[1]


CHIP_RELS = ("x", "y", "xy")


def _run_copies(copies, sends, recvs):
    for cp in copies:
        cp.start()
    for cp in recvs:
        cp.wait_recv()
    for cp in sends:
        cp.wait_send()


def _gather_first(w_shard, small_shard):
    R = w_shard.shape[0]
    r2 = R // 2

    def body(w_ref, s_ref, w_out, s_out, send_sems, recv_sems):
        x, y, c = _me()
        me = _chip_of((x, y))
        sib = _peer("c")

        def half(chip, core):
            return w_out.at[chip, pl.ds(pl.multiple_of(core * r2, 8), r2), :]

        def copy(k, src, dst, to):
            return pltpu.make_async_remote_copy(src_ref=src, dst_ref=dst, send_sem=send_sems.at[k],
                                                recv_sem=recv_sems.at[k], device_id=to, device_id_type=MESH)

        own_rows = w_ref.at[pl.ds(pl.multiple_of(c * r2, 8), r2), :]
        first = [copy(r, own_rows, half(me, c), _peer(rel)) for r, rel in enumerate(CHIP_RELS)]
        first += [copy(3 + r, s_ref, s_out.at[me], _peer(rel)) for r, rel in enumerate(CHIP_RELS)]
        for cp in first:
            cp.start()
        passed = []
        for r, rel in enumerate(CHIP_RELS):
            their = _chip_of(_peer(rel))
            copy(r, own_rows, half(their, c), _peer(rel)).wait_recv()
            fwd = copy(6 + r, half(their, c), half(their, c), sib)
            fwd.start()
            passed.append(fwd)
        for r, rel in enumerate(CHIP_RELS):
            their = _chip_of(_peer(rel))
            copy(3 + r, s_ref, s_out.at[their], _peer(rel)).wait_recv()
            copy(6 + r, own_rows, half(their, 1 - c), sib).wait_recv()
        for cp in first + passed:
            cp.wait_send()

    w_all, s_all = pl.pallas_call(
        body, name="gather_first", in_specs=[ANY, ANY], out_specs=[ANY, ANY],
        out_shape=[jax.ShapeDtypeStruct((4,) + w_shard.shape, w_shard.dtype),
                   jax.ShapeDtypeStruct((4,) + small_shard.shape, small_shard.dtype)],
        scratch_shapes=[pltpu.SemaphoreType.DMA((9,)), pltpu.SemaphoreType.DMA((9,))])(w_shard, small_shard)
    me = _chip_of(_me())
    return (lax.dynamic_update_index_in_dim(w_all, w_shard, me, 0),
            lax.dynamic_update_index_in_dim(s_all, small_shard, me, 0))


OTHERS = tuple((fx, fy, fc) for fx in (0, 1) for fy in (0, 1) for fc in (0, 1) if (fx, fy, fc) != (0, 0, 0))


def _other(flip):
    x, y, c = _me()
    return (x ^ flip[0], y ^ flip[1], c ^ flip[2])


def _linear(dev):
    return 4 * dev[0] + 2 * dev[1] + dev[2]


def _exchange_small(small):
    def body(small_ref, out_ref, send_sems, recv_sems):
        my_slot = _linear(_me())
        sends, recvs = [], []
        for k, flip in enumerate(OTHERS):
            peer = _other(flip)
            sends.append(pltpu.make_async_remote_copy(
                src_ref=small_ref, dst_ref=out_ref.at[my_slot], send_sem=send_sems.at[k], recv_sem=recv_sems.at[k],
                device_id=peer, device_id_type=MESH))
            recvs.append(pltpu.make_async_remote_copy(
                src_ref=small_ref, dst_ref=out_ref.at[_linear(peer)], send_sem=send_sems.at[k],
                recv_sem=recv_sems.at[k], device_id=peer, device_id_type=MESH))
        _run_copies(sends, sends, recvs)

    out = pl.pallas_call(
        body, name="exchange_small", in_specs=[ANY], out_specs=ANY,
        out_shape=jax.ShapeDtypeStruct((8,) + small.shape, small.dtype),
        scratch_shapes=[pltpu.SemaphoreType.DMA((7,)), pltpu.SemaphoreType.DMA((7,))])(small)
    return lax.dynamic_update_index_in_dim(out, small, _linear(_me()), 0)


def _pair_swap(name, halves):
    n = len(halves)

    def body(*refs):
        src, out = refs[:n], refs[n:2 * n]
        send_sems, recv_sems = refs[2 * n:]
        sib = _peer("c")
        copies = [pltpu.make_async_remote_copy(
            src_ref=src[i], dst_ref=out[i], send_sem=send_sems.at[i], recv_sem=recv_sems.at[i],
            device_id=sib, device_id_type=MESH) for i in range(n)]
        _run_copies(copies, copies, copies)

    return pl.pallas_call(
        body, name=name, in_specs=[ANY] * n, out_specs=[ANY] * n,
        out_shape=[jax.ShapeDtypeStruct(h.shape, h.dtype) for h in halves],
        scratch_shapes=[pltpu.SemaphoreType.DMA((n,)), pltpu.SemaphoreType.DMA((n,))])(*halves)


HBM = pl.BlockSpec(memory_space=pltpu.HBM)
SEM = pl.BlockSpec(memory_space=pltpu.SEMAPHORE)
EFFECT = pltpu.SideEffectType.DATAFLOW_SIDE_EFFECTING


def _hbm(a):
    return pltpu.with_memory_space_constraint(a, pltpu.HBM)


def _transfer_start(name, srcs, lands, n_copies, make_copies, after=None):
    n, m = len(srcs), len(lands)

    def body(*refs):
        src, land = refs[:n], refs[n:n + m]
        outs = refs[n + m + (after is not None):]
        send_sems, recv_sems, token = outs[0], outs[1], outs[-1]
        for cp in make_copies(src, land, send_sems, recv_sems):
            cp.start()
        token[...] = jnp.zeros_like(token)

    arrs = list(srcs) + list(lands)
    in_specs, args = _with_dep([HBM] * (n + m), [_hbm(a) for a in arrs], after)
    out = pl.pallas_call(
        body, name=name,
        out_shape=(pltpu.SemaphoreType.DMA((n_copies,)), pltpu.SemaphoreType.DMA((n_copies,)),
                   *[pltpu.HBM(a.shape, a.dtype) for a in arrs], jax.ShapeDtypeStruct((8, 128), F32)),
        in_specs=in_specs,
        out_specs=(SEM, SEM, *[HBM] * (n + m), pl.BlockSpec(memory_space=pltpu.VMEM)),
        input_output_aliases={i: 2 + i for i in range(n + m)},
        compiler_params=pltpu.CompilerParams(has_side_effects=EFFECT))(*args)
    return out[0], out[1], list(out[2:2 + n]), list(out[2 + n:2 + n + m]), out[-1]


def _transfer_wait(name, send_sems, recv_sems, srcs, lands, make_copies, after):
    n, m = len(srcs), len(lands)

    def body(*refs):
        src, land = refs[:n], refs[n:n + m]
        s_sems, r_sems = refs[n + m], refs[n + m + 1]
        for cp in make_copies(src, land, s_sems, r_sems):
            cp.wait_send()
            cp.wait_recv()

    arrs = list(srcs) + list(lands)
    out = pl.pallas_call(
        body, name=name, out_shape=tuple(pltpu.HBM(a.shape, a.dtype) for a in arrs),
        in_specs=[HBM] * (n + m) + [SEM, SEM, ANY], out_specs=tuple([HBM] * (n + m)),
        input_output_aliases={i: i for i in range(n + m)},
        compiler_params=pltpu.CompilerParams(has_side_effects=EFFECT))(*arrs, send_sems, recv_sems, after)
    return list(out[:n]), list(out[n:])


def _gather_copies(src, land, send_sems, recv_sems):
    me = _chip_of(_me())
    copies = []
    for i in range(len(src)):
        for r, rel in enumerate(CHIP_RELS):
            k = 3 * i + r
            copies.append(pltpu.make_async_remote_copy(
                src_ref=src[i], dst_ref=land[i].at[me], send_sem=send_sems.at[k], recv_sem=recv_sems.at[k],
                device_id=_peer(rel), device_id_type=MESH))
    return copies


def _small_copies(src, land, send_sems, recv_sems):
    my_slot = _linear(_me())
    return [pltpu.make_async_remote_copy(
        src_ref=src[0], dst_ref=land[0].at[my_slot], send_sem=send_sems.at[k], recv_sem=recv_sems.at[k],
        device_id=_other(flip), device_id_type=MESH) for k, flip in enumerate(OTHERS)]


def _pieces_copies(src, land, send_sems, recv_sems):
    copies = []
    for k, flip in enumerate(OTHERS):
        peer = _other(flip)
        copies.append(pltpu.make_async_remote_copy(
            src_ref=src[0].at[_linear(peer)], dst_ref=land[0].at[k], send_sem=send_sems.at[k],
            recv_sem=recv_sems.at[k], device_id=peer, device_id_type=MESH))
    return copies


def _row_block(rows, cols, budget=2 * 1024 * 1024):
    rb = max(8, (budget // (4 * cols)) // 8 * 8)
    while rows % rb:
        rb -= 8
    return rb if rb > 0 else rows


def _sum_pieces(name, mine, slot, rest):
    _, R, Cc = mine.shape
    K = rest.shape[0]
    rb = _row_block(R, Cc)

    def body(s_ref, f_ref, r_ref, o_ref):
        acc = f_ref[0].astype(F32)
        for j in range(K):
            acc = acc + r_ref[j].astype(F32)
        o_ref[...] = acc

    return pl.pallas_call(
        body, name=name,
        grid_spec=pltpu.PrefetchScalarGridSpec(
            num_scalar_prefetch=1, grid=(R // rb,),
            in_specs=[pl.BlockSpec((1, rb, Cc), lambda i, s: (s[0], i, 0)),
                      pl.BlockSpec((K, rb, Cc), lambda i, s: (0, i, 0))],
            out_specs=pl.BlockSpec((rb, Cc), lambda i, s: (i, 0))),
        out_shape=jax.ShapeDtypeStruct((R, Cc), F32), compiler_params=_cp("parallel"))(slot, mine, rest)


def _adamw_math(w, gv, m, v):
    mn = ADAM_B1 * m + (1.0 - ADAM_B1) * gv
    vn = ADAM_B2 * v + (1.0 - ADAM_B2) * (gv * gv)
    m_hat = mn / (1.0 - ADAM_B1 ** ADAM_STEP)
    v_hat = vn / (1.0 - ADAM_B2 ** ADAM_STEP)
    return -ADAM_LR * (m_hat / (jnp.sqrt(v_hat) + ADAM_EPS) + ADAM_WD * w), mn, vn


def _adamw_halves(name, w, mine, theirs, m, v, core):
    R, Cc = w.shape
    r2 = R // 2
    rb = _row_block(r2, Cc, 1024 * 1024)
    nb2 = r2 // rb

    def body(c_ref, w_ref, mine_ref, theirs_ref, m_ref, v_ref, g_ref, d_ref, mo_ref, vo_ref):
        is_mine = (pl.program_id(0) // nb2) == c_ref[0]
        gv = jnp.where(is_mine, mine_ref[...], theirs_ref[...])
        g_ref[...] = gv
        d_ref[...], mo_ref[...], vo_ref[...] = _adamw_math(w_ref[...], gv, m_ref[...], v_ref[...])

    blk = pl.BlockSpec((rb, Cc), lambda i, c: (i, 0))
    half = lambda own: pl.BlockSpec(
        (rb, Cc), lambda i, c: (jnp.clip(i - (c[0] if own else 1 - c[0]) * nb2, 0, nb2 - 1), 0))
    return pl.pallas_call(
        body, name=name,
        grid_spec=pltpu.PrefetchScalarGridSpec(
            num_scalar_prefetch=1, grid=(2 * nb2,), in_specs=[blk, half(True), half(False), blk, blk],
            out_specs=[blk] * 4),
        out_shape=[jax.ShapeDtypeStruct((R, Cc), F32)] * 4, compiler_params=_cp("parallel"))(core, w, mine, theirs, m, v)


def _adamw_transposed(name, wt, mine, theirs, mt, vt, core):
    Cc, kh_n, _ = wt.shape
    r2 = mine.shape[0]
    per_half = kh_n // 2
    nb = -(-Cc // LANES)

    def body(c_ref, w_ref, mine_ref, theirs_ref, m_ref, v_ref, g_ref, d_ref, mo_ref, vo_ref):
        first = c_ref[0] == 0
        halves = (jnp.where(first, mine_ref[...], theirs_ref[...]).T,
                  jnp.where(first, theirs_ref[...], mine_ref[...]).T)
        for kh in range(kh_n):
            lo = (kh % per_half) * LANES
            g_ref[:, kh, :] = halves[kh // per_half][:, lo:lo + LANES]
        d_ref[...], mo_ref[...], vo_ref[...] = _adamw_math(w_ref[...], g_ref[...], m_ref[...], v_ref[...])

    blk = pl.BlockSpec((LANES, kh_n, LANES), lambda i, c: (i, 0, 0))
    half = pl.BlockSpec((r2, LANES), lambda i, c: (0, i))
    return pl.pallas_call(
        body, name=name,
        grid_spec=pltpu.PrefetchScalarGridSpec(
            num_scalar_prefetch=1, grid=(nb,), in_specs=[blk, half, half, blk, blk], out_specs=[blk] * 4),
        out_shape=[jax.ShapeDtypeStruct(wt.shape, F32)] * 4, compiler_params=_cp("parallel"))(
            core, wt, mine, theirs, mt, vt)


FF_W = 2 * D_FF
FF_CH = FF_W // LANES
DNC_W = 3 * DN_WIDTH
DNC_CH = DNC_W // LANES
E_ONG, E_SGN, E_SGW, E_SGBT = 0, 1, 8, 8 + SG_GROUPS * SG_BLOCK
E_FFN = E_SGBT + SG_BLOCK
E_FCW = E_FFN + D_MODEL // LANES
E_FCB = E_FCW + 3 * FF_CH
E_FIN = E_FCB + FF_CH
EARLY_ROWS = E_FIN + D_MODEL // LANES
L_ATTN, L_DNC = 0, D_MODEL // LANES
L_AD = L_DNC + 4 * DNC_CH
L_LOSS = L_AD + 2
LATE_ROWS = -(-(L_LOSS + 1) // 8) * 8


def _put_rows(out, r0, x):
    k, width = x.shape
    n = width // LANES
    for t in range(k):
        for j in range(n):
            out[r0 + t * n + j:r0 + t * n + j + 1, :] = x[t:t + 1, j * LANES:(j + 1) * LANES]


def _pack_early(ong, sgn, sgw, sgbt, ffn, fcw, fcb, fin):
    def body(ong_ref, sgn_ref, sgw_ref, sgbt_ref, ffn_ref, fcw_ref, fcb_ref, fin_ref, out):
        out[...] = jnp.zeros_like(out)
        _put_rows(out, E_ONG, ong_ref)
        _put_rows(out, E_SGN, sgn_ref)
        for gi in range(SG_GROUPS):
            out[E_SGW + gi * SG_BLOCK:E_SGW + (gi + 1) * SG_BLOCK, :] = sgw_ref[gi]
        out[E_SGBT:E_SGBT + SG_BLOCK, :] = sgbt_ref[...]
        _put_rows(out, E_FFN, ffn_ref)
        _put_rows(out, E_FCW, fcw_ref)
        _put_rows(out, E_FCB, fcb_ref)
        _put_rows(out, E_FIN, fin_ref)

    return pl.pallas_call(body, name="pack_small_early", out_shape=jax.ShapeDtypeStruct((EARLY_ROWS, LANES), F32))(
        ong, sgn, sgw, sgbt, ffn, fcw, fcb, fin)


def _pack_late(attn, dnc, ad, loss_row):
    def body(attn_ref, dnc_ref, ad_ref, loss_ref, out):
        out[...] = jnp.zeros_like(out)
        _put_rows(out, L_ATTN, attn_ref)
        _put_rows(out, L_DNC, dnc_ref)
        out[L_AD:L_AD + 2, :] = ad_ref[...]
        out[L_LOSS:L_LOSS + 1, :] = loss_ref[...]

    return pl.pallas_call(body, name="pack_small_late", out_shape=jax.ShapeDtypeStruct((LATE_ROWS, LANES), F32))(
        attn, dnc, ad, loss_row)


SMALL = ("attn_norm_g", "dn_a_log", "dn_dt_bias", "dn_out_norm_g", "sg_norm_g", "sg_w", "sg_b", "ffn_norm_g",
         "ffn_conv_b", "final_norm_g", "dn_conv_w", "ffn_conv_w")


def _small_update(early_all, late_all, chip, W, M, V):
    n = len(SMALL)
    arrs = [d[k] for d in (W, M, V) for k in SMALL]

    def body(c_ref, e_ref, l_ref, *refs):
        w_, m_, v_ = refs[:n], refs[n:2 * n], refs[2 * n:3 * n]
        loss_ref = refs[3 * n]
        outs = refs[3 * n + 1:]
        g_, d_, mo_, vo_ = outs[:n], outs[n:2 * n], outs[2 * n:3 * n], outs[3 * n:4 * n]
        chip_i = c_ref[0]

        def total(ref, r0, rows=1):
            acc = ref[0, pl.ds(r0, rows), :]
            for s in range(1, 8):
                acc = acc + ref[s, pl.ds(r0, rows), :]
            return acc

        def update(i, idx, g):
            g_[i][idx] = g
            d_[i][idx], mo_[i][idx], vo_[i][idx] = _adamw_math(w_[i][idx], g, m_[i][idx], v_[i][idx])

        def rows_param(name, ref, r0, width):
            i = SMALL.index(name)
            for j in range(width // LANES):
                update(i, (slice(None), slice(j * LANES, (j + 1) * LANES)), total(ref, r0 + j))

        rows_param("attn_norm_g", l_ref, L_ATTN, D_MODEL)
        ad = (total(l_ref, L_AD), total(l_ref, L_AD + 1))
        update(SMALL.index("dn_a_log"), (slice(None), slice(None)), ad[0][:, N_HEADS:2 * N_HEADS])
        update(SMALL.index("dn_dt_bias"), (slice(None), slice(None)), ad[1][:, N_HEADS:2 * N_HEADS])
        rows_param("dn_out_norm_g", e_ref, E_ONG, HEAD_DIM)
        rows_param("sg_norm_g", e_ref, E_SGN, SG_WIDTH)
        sgbt = total(e_ref, E_SGBT, SG_BLOCK).T
        for gi in range(SG_GROUPS):
            update(SMALL.index("sg_w"), (0, gi), total(e_ref, E_SGW + gi * SG_BLOCK, SG_BLOCK))
            update(SMALL.index("sg_b"), (0, slice(gi, gi + 1), slice(None)), sgbt[gi:gi + 1, :])
        rows_param("ffn_norm_g", e_ref, E_FFN, D_MODEL)
        rows_param("ffn_conv_b", e_ref, E_FCB, FF_W)
        rows_param("final_norm_g", e_ref, E_FIN, D_MODEL)
        for name, ref, r0, taps, chunks in (("dn_conv_w", l_ref, L_DNC, 4, DNC_CH), ("ffn_conv_w", e_ref, E_FCW, 3, FF_CH)):
            mine = chunks // 4
            for t in range(taps):
                for j in range(mine):
                    update(SMALL.index(name), (0, slice(t, t + 1), slice(j * LANES, (j + 1) * LANES)),
                           total(ref, r0 + t * chunks + chip_i * mine + j))
        loss_ref[...] = total(l_ref, L_LOSS)

    full = lambda a: pl.BlockSpec(a.shape, lambda i, c, nd=a.ndim: (0,) * nd)
    shapes = [jax.ShapeDtypeStruct(W[k].shape, F32) for k in SMALL]
    outs = pl.pallas_call(
        body, name="small_update",
        grid_spec=pltpu.PrefetchScalarGridSpec(
            num_scalar_prefetch=1, grid=(1,), in_specs=[full(early_all), full(late_all)] + [full(a) for a in arrs],
            out_specs=[pl.BlockSpec((1, LANES), lambda i, c: (0, 0))] + [full(s) for s in shapes] * 4),
        out_shape=[jax.ShapeDtypeStruct((1, LANES), F32)] + shapes * 4,
        compiler_params=pltpu.CompilerParams(vmem_limit_bytes=VMEM_LIMIT))(chip, early_all, late_all, *arrs)
    loss, outs = outs[0], outs[1:]
    return (loss,) + tuple(dict(zip(SMALL, outs[k * n:(k + 1) * n])) for k in range(4))


ORDER =("attn_norm_g", "w_in", "dn_conv_w", "dn_a_log", "dn_dt_bias", "dn_out_norm_g", "sg_norm_g", "sg_w",
         "sg_b", "w_out", "ffn_norm_g", "w_up", "ffn_conv_w", "ffn_conv_b", "w_down", "final_norm_g")


def kernel(x, attn_norm_g, w_in, dn_conv_w, dn_a_log, dn_dt_bias, dn_out_norm_g, sg_norm_g, sg_w, sg_b, w_out, ffn_norm_g, w_up, ffn_conv_w, ffn_conv_b, w_down, final_norm_g, loss_target, m_attn_norm_g, m_w_in, m_dn_conv_w, m_dn_a_log, m_dn_dt_bias, m_dn_out_norm_g, m_sg_norm_g, m_sg_w, m_sg_b, m_w_out, m_ffn_norm_g, m_w_up, m_ffn_conv_w, m_ffn_conv_b, m_w_down, m_final_norm_g, v_attn_norm_g, v_w_in, v_dn_conv_w, v_dn_a_log, v_dn_dt_bias, v_dn_out_norm_g, v_sg_norm_g, v_sg_w, v_sg_b, v_w_out, v_ffn_norm_g, v_w_up, v_ffn_conv_w, v_ffn_conv_b, v_w_down, v_final_norm_g):
    W = dict(attn_norm_g=attn_norm_g, w_in=w_in, dn_conv_w=dn_conv_w, dn_a_log=dn_a_log, dn_dt_bias=dn_dt_bias,
             dn_out_norm_g=dn_out_norm_g, sg_norm_g=sg_norm_g, sg_w=sg_w, sg_b=sg_b, w_out=w_out,
             ffn_norm_g=ffn_norm_g, w_up=w_up, ffn_conv_w=ffn_conv_w, ffn_conv_b=ffn_conv_b, w_down=w_down,
             final_norm_g=final_norm_g)
    Mo = dict(attn_norm_g=m_attn_norm_g, w_in=m_w_in, dn_conv_w=m_dn_conv_w, dn_a_log=m_dn_a_log,
              dn_dt_bias=m_dn_dt_bias, dn_out_norm_g=m_dn_out_norm_g, sg_norm_g=m_sg_norm_g, sg_w=m_sg_w,
              sg_b=m_sg_b, w_out=m_w_out, ffn_norm_g=m_ffn_norm_g, w_up=m_w_up, ffn_conv_w=m_ffn_conv_w,
              ffn_conv_b=m_ffn_conv_b, w_down=m_w_down, final_norm_g=m_final_norm_g)
    Vo = dict(attn_norm_g=v_attn_norm_g, w_in=v_w_in, dn_conv_w=v_dn_conv_w, dn_a_log=v_dn_a_log,
              dn_dt_bias=v_dn_dt_bias, dn_out_norm_g=v_dn_out_norm_g, sg_norm_g=v_sg_norm_g, sg_w=v_sg_w,
              sg_b=v_sg_b, w_out=v_w_out, ffn_norm_g=v_ffn_norm_g, w_up=v_w_up, ffn_conv_w=v_ffn_conv_w,
              ffn_conv_b=v_ffn_conv_b, w_down=v_w_down, final_norm_g=v_final_norm_g)
    xi, yi, ci = lax.axis_index("x"), lax.axis_index("y"), lax.axis_index("c")
    chip = 2 * xi + yi

    me_lin = 4 * xi + 2 * yi + ci

    g_in, g_dnc = _gather_first(w_in[0].astype(BF16), dn_conv_w[0])
    def start_gather(name, shards, after):
        lands = [lax.dynamic_update_index_in_dim(lax.empty((4,) + s.shape, s.dtype), s, chip, 0) for s in shards]
        return _transfer_start(name, shards, lands, 3 * len(shards), _gather_copies, after=after)

    mid = start_gather("gather_mid_start", [w_out[0].astype(BF16), w_up[0].astype(BF16), ffn_conv_w[0]], g_in)
    last = start_gather("gather_last_start", [w_down[0].astype(BF16)], mid[4])
    token = last[4]

    def late_weights(stage, after):
        if stage == "out_proj":
            _, (g_out, g_up, g_ffc) = _transfer_wait("gather_mid_wait", *mid[:4], _gather_copies, after)
            return dict(w_out=g_out.reshape(D_MODEL, D_MODEL), ffn_conv_w=g_ffc.transpose(1, 0, 2).reshape(3, 2 * D_FF),
                        w_up=g_up.transpose(1, 0, 2).reshape(D_MODEL, 2 * D_FF))
        _, (g_down,) = _transfer_wait("gather_last_wait", *last[:4], _gather_copies, after)
        return dict(w_down=g_down.reshape(D_FF, D_MODEL))

    full = dict(
        w_in=g_in,
        dn_conv_w=g_dnc.transpose(1, 0, 2).reshape(4, 3 * DN_WIDTH),
        attn_norm_g=attn_norm_g, dn_a_log=dn_a_log, dn_dt_bias=dn_dt_bias, dn_out_norm_g=dn_out_norm_g,
        sg_norm_g=sg_norm_g, sg_w=sg_w[0], sg_b=sg_b[0], ffn_norm_g=ffn_norm_g, ffn_conv_b=ffn_conv_b,
        final_norm_g=final_norm_g[None])

    pending = {}

    def on_grad(name, gw):
        if name == "small_early":
            buf = _pack_early(gw["dn_out_norm_g"], gw["sg_norm_g"], gw["sg_w"], gw["sg_bt"], gw["ffn_norm_g"],
                              gw["ffn_conv_w"], gw["ffn_conv_b"], gw["final_norm_g"])
            land = lax.dynamic_update_index_in_dim(lax.empty((8,) + buf.shape, F32), buf, me_lin, 0)
            s_sem, r_sem, src, lands, tok = _transfer_start("small_early_start", [buf], [land], 7, _small_copies)
            pending[name] = (s_sem, r_sem, src, lands)
            return tok
        g8 = gw.reshape(8, -1, gw.shape[-1])
        land = lax.empty((7,) + g8.shape[1:], BF16)
        s_sem, r_sem, src, lands, tok = _transfer_start(f"reduce_{name}_start", [g8], [land], 7, _pieces_copies)
        pending[name] = (s_sem, r_sem, src, lands)
        return tok

    loss_row, grad_x, g = _local_step(x[0], loss_target[0], full, dep=token, late_weights=late_weights,
                                      on_grad=on_grad)

    late_all = _exchange_small(_pack_late(g["attn_norm_g"], g["dn_conv_w"], g["a_dt"], loss_row))
    s_sem, r_sem, src, lands = pending["small_early"]
    _, (early_all,) = _transfer_wait("small_early_wait", s_sem, r_sem, src, lands, _small_copies, grad_x)
    row = lambda d: {k: (d[k].reshape(1, -1) if k == "final_norm_g" else d[k]) for k in SMALL}
    loss_sum, *small_out = _small_update(early_all, late_all, chip.astype(jnp.int32).reshape(1), row(W), row(Mo), row(Vo))
    loss = loss_sum[0, 0]

    def summed_half(n, after):
        s_sem, r_sem, src, lands = pending[n]
        sent, got = _transfer_wait(f"reduce_{n}_wait", s_sem, r_sem, src, lands, _pieces_copies, after)
        return _sum_pieces(f"sum_{n}", sent[0], me_lin.astype(jnp.int32).reshape(1), got[0])

    first3 = ("w_down", "w_up", "w_out")
    halves = [summed_half(n, grad_x) for n in first3]
    theirs = _pair_swap("pair_swap", halves)
    core = ci.astype(jnp.int32).reshape(1)
    grads, delta, new_m, new_v = {}, {}, {}, {}
    for n, mine_h, their_h in zip(first3, halves, theirs):
        shp = W[n].shape
        gr, d, mn, vn = _adamw_halves(f"adamw_{n}", W[n][0], mine_h, their_h, Mo[n][0], Vo[n][0], core)
        grads[n], delta[n], new_m[n], new_v[n] = gr.reshape(shp), d.reshape(shp), mn.reshape(shp), vn.reshape(shp)
    mine_h = summed_half("w_in", delta["w_out"])
    (their_h,) = _pair_swap("pair_swap_w_in", [mine_h])
    shp = w_in.shape
    to_t = lambda a: a.reshape(shp[1] // LANES, LANES, shp[2]).transpose(2, 0, 1)
    from_t = lambda a: a.transpose(1, 2, 0).reshape(shp)
    outs = _adamw_transposed("adamw_w_in", to_t(w_in), mine_h, their_h, to_t(m_w_in), to_t(v_w_in), core)
    grads["w_in"], delta["w_in"], new_m["w_in"], new_v["w_in"] = (from_t(o) for o in outs)
    for dst, src_d in zip((grads, delta, new_m, new_v), small_out):
        dst.update({k: (a.reshape(W[k].shape) if k == "final_norm_g" else a) for k, a in src_d.items()})

    return (loss, grad_x[None], *[grads[n] for n in ORDER], *[delta[n] for n in ORDER],
            *[new_m[n] for n in ORDER], *[new_v[n] for n in ORDER])
```

```python
import math

import jax
import jax.numpy as jnp
from jax import lax
from jax.experimental import pallas as pl
from jax.experimental.pallas import tpu as pltpu

F32 = jnp.float32
BF16 = jnp.bfloat16

D_MODEL = 1024
CHUNK = 64
SCAN_CHUNKS = 8
HEAD_DIM = 128
N_HEADS = 4
DN_WIDTH = 512
SG_WIDTH = 512
SG_GROUPS = 4
SG_BLOCK = 128
D_FF = 2816
PROJ_COLS = 3080
PROJ_PAD = 3200
BA_COL = 3072
EPS = 1e-6
NEG = -1e30
VMEM_LIMIT = 56 * 1024 * 1024

ADAM_LR = 0.001
ADAM_B1 = 0.9
ADAM_B2 = 0.999
ADAM_EPS = 1e-08
ADAM_WD = 0.01
ADAM_STEP = 10

MESH = pl.DeviceIdType.MESH
ANY = pl.BlockSpec(memory_space=pl.ANY)


def _cp(*sem):
    return pltpu.CompilerParams(dimension_semantics=sem, vmem_limit_bytes=VMEM_LIMIT)


def _bf(a):
    return a.astype(BF16)


def _nn(a, b):
    return jnp.dot(_bf(a), _bf(b), preferred_element_type=F32)


def _nt(a, b):
    return lax.dot_general(_bf(a), _bf(b), (((1,), (1,)), ((), ())), preferred_element_type=F32)


def _tn(a, b):
    return lax.dot_general(_bf(a), _bf(b), (((0,), (0,)), ((), ())), preferred_element_type=F32)


def _split(a):
    hi = _bf(a)
    return hi, _bf(a - hi.astype(F32))


def _sigmoid(x):
    return 0.5 * jnp.tanh(0.5 * x) + 0.5


def _silu(x):
    return x * _sigmoid(x)


def _dsilu(x):
    s = _sigmoid(x)
    return s * (1.0 + x * (1.0 - s))


_GELU_C = math.sqrt(2.0 / math.pi)
_GELU_A = 0.044715


def _gelu(x):
    return 0.5 * x * (1.0 + jnp.tanh(_GELU_C * (x + _GELU_A * x * x * x)))


def _dgelu(x):
    t = jnp.tanh(_GELU_C * (x + _GELU_A * x * x * x))
    return 0.5 * (1.0 + t) + 0.5 * x * (1.0 - t * t) * _GELU_C * (1.0 + 3.0 * _GELU_A * x * x)


def _softplus(x):
    return jnp.maximum(x, 0.0) + jnp.log(1.0 + jnp.exp(-jnp.abs(x)))


def _with_dep(in_specs, args, dep):
    if dep is None:
        return in_specs, args
    return in_specs + [ANY], args + [dep]


SUB_ROWS = 128


def _sub_blocks(tm):
    return [slice(r0, min(r0 + SUB_ROWS, tm)) for r0 in range(0, tm, SUB_ROWS)]


def _rms_hat(xv):
    r = lax.rsqrt(jnp.mean(xv * xv, axis=-1, keepdims=True) + EPS)
    return xv * r, r


def _rms_bwd_vals(dh, xh, r, g):
    dxh = dh * g
    return r * (dxh - xh * jnp.mean(dxh * xh, axis=-1, keepdims=True)), jnp.sum(dh * xh, axis=0, keepdims=True)


def _in_proj_act(x, g, w4, conv_w, alog_row, dtb_row, tm=256, dep=None):
    T, K = x.shape
    ng, _, wc = w4.shape
    tm = min(tm, T)
    nb = T // tm
    W3 = 3 * DN_WIDTH

    def body(x_ref, g_ref, w4_ref, cw_ref, al_ref, dt_ref, *rest):
        p_ref, h_ref, w_ref, q_ref, k_ref, v_ref, bg_ref, prev_scr, ext_scr, tail_scr = rest[-10:]

        @pl.when(pl.program_id(0) == 0)
        def _():
            w_ref[:, ng * wc:] = jnp.zeros((K, PROJ_PAD - ng * wc), BF16)
            for j in range(ng):
                w_ref[:, j * wc:(j + 1) * wc] = w4_ref[j]
            tail_scr[...] = jnp.zeros_like(tail_scr)
            prev_scr[...] = jnp.zeros_like(prev_scr)
        for r in _sub_blocks(tm):
            xh, _ = _rms_hat(x_ref[r, :])
            h_ref[r, :] = (xh * g_ref[...]).astype(BF16)
        p_ref[...] = jnp.dot(h_ref[...], w_ref[...], preferred_element_type=F32)
        outs = (q_ref, k_ref, v_ref)
        for j in range(3 * N_HEADS):
            kind, hd = divmod(j, N_HEADS)
            cols = slice(j * HEAD_DIM, (j + 1) * HEAD_DIM)
            cur = prev_scr[:, cols]
            ext_scr[j, 0:8] = tail_scr[:, cols]
            ext_scr[j, 8:] = cur
            wv = cw_ref[:, cols]
            s = _silu(ext_scr[j, 5:5 + tm] * wv[0:1] + ext_scr[j, 6:6 + tm] * wv[1:2]
                      + ext_scr[j, 7:7 + tm] * wv[2:3] + cur * wv[3:4])
            if kind < 2:
                scale = HEAD_DIM ** -0.5 if kind == 0 else 1.0
                s = s * (lax.rsqrt(jnp.sum(s * s, axis=-1, keepdims=True) + EPS) * scale)
            outs[kind][:, hd * HEAD_DIM:(hd + 1) * HEAD_DIM] = s
        ba = prev_scr[:, W3:]
        lane = _lane_iota(ba.shape)
        beta = _sigmoid(ba)
        gl = -jnp.exp(al_ref[...]) * _softplus(ba + dt_ref[...])
        bg_ref[...] = jnp.where(lane < N_HEADS, beta, jnp.where(lane < 2 * N_HEADS, gl, 0.0))
        tail_scr[...] = prev_scr[tm - 8:tm, 0:W3]
        prev_scr[:, 0:W3] = p_ref[:, 0:W3]
        prev_scr[:, W3:] = p_ref[:, BA_COL:]

    cur_blk = lambda i: (jnp.minimum(i, nb - 1), 0)
    prev_blk = lambda i: (jnp.maximum(i - 1, 0), 0)
    vec128 = pl.BlockSpec((1, 128), lambda i: (0, 0))
    in_specs, args = _with_dep(
        [pl.BlockSpec((tm, K), cur_blk), pl.BlockSpec((1, K), lambda i: (0, 0)),
         pl.BlockSpec((ng, K, wc), lambda i: (0, 0, 0)), pl.BlockSpec((4, W3), lambda i: (0, 0)), vec128, vec128],
        [x, g, w4, conv_w, alog_row, dtb_row], dep)
    row512 = pl.BlockSpec((tm, DN_WIDTH), prev_blk)
    return pl.pallas_call(
        body, name="in_proj_act", grid=(nb + 1,), in_specs=in_specs,
        out_specs=[pl.BlockSpec((tm, PROJ_PAD), cur_blk), pl.BlockSpec((tm, K), cur_blk),
                   pl.BlockSpec((K, PROJ_PAD), lambda i: (0, 0)), row512, row512, row512,
                   pl.BlockSpec((tm, 128), prev_blk)],
        out_shape=[jax.ShapeDtypeStruct((T, PROJ_PAD), F32), jax.ShapeDtypeStruct((T, K), BF16),
                   jax.ShapeDtypeStruct((K, PROJ_PAD), BF16)] + [jax.ShapeDtypeStruct((T, DN_WIDTH), F32)] * 3
        + [jax.ShapeDtypeStruct((T, 128), F32)],
        scratch_shapes=[pltpu.VMEM((tm, W3 + 128), F32), pltpu.VMEM((3 * N_HEADS, tm + 8, HEAD_DIM), F32),
                        pltpu.VMEM((8, W3), F32)],
        compiler_params=_cp("arbitrary"))(*args)


def _out_proj(mix, w, x, g, tm=512):
    T, K = mix.shape
    Dm = w.shape[1]
    tm = min(tm, T)

    def body(a_ref, w_ref, x_ref, g_ref, x2_ref, h_ref):
        x2_ref[...] = _nn(a_ref[...], w_ref[...]) + x_ref[...]
        for r in _sub_blocks(tm):
            xh, _ = _rms_hat(x2_ref[r, :])
            h_ref[r, :] = (xh * g_ref[...]).astype(BF16)

    row = lambda width: pl.BlockSpec((tm, width), lambda i: (i, 0))
    return pl.pallas_call(
        body, name="out_proj", grid=(T // tm,),
        in_specs=[row(K), pl.BlockSpec((K, Dm), lambda i: (0, 0)), row(Dm), pl.BlockSpec((1, Dm), lambda i: (0, 0))],
        out_specs=[row(Dm), row(Dm)],
        out_shape=[jax.ShapeDtypeStruct((T, Dm), F32), jax.ShapeDtypeStruct((T, Dm), BF16)],
        compiler_params=_cp("parallel"))(mix, w, x, g)


def _down_proj_loss(act, w, x2, tgt, g, tm=512):
    T, K = act.shape
    Dm = w.shape[1]
    tm = min(tm, T)

    def body(a_ref, w_ref, x_ref, t_ref, g_ref, loss_ref, dx_ref, gg_ref):
        @pl.when(pl.program_id(0) == 0)
        def _():
            gg_ref[...] = jnp.zeros_like(gg_ref)
            loss_ref[...] = jnp.zeros_like(loss_ref)
        dx_ref[...] = _nn(a_ref[...], w_ref[...]) + x_ref[...]
        for r in _sub_blocks(tm):
            xh, rr = _rms_hat(dx_ref[r, :])
            e = xh * g_ref[...] - t_ref[r, :]
            loss_ref[...] += jnp.zeros_like(loss_ref) + (0.5 / Dm) * jnp.sum(e * e)
            dx, gg = _rms_bwd_vals(e * (1.0 / Dm), xh, rr, g_ref[...])
            dx_ref[r, :] = dx
            gg_ref[...] += gg

    row = lambda width: pl.BlockSpec((tm, width), lambda i: (i, 0))
    vec = pl.BlockSpec((1, Dm), lambda i: (0, 0))
    return pl.pallas_call(
        body, name="down_proj_loss", grid=(T // tm,),
        in_specs=[row(K), pl.BlockSpec((K, Dm), lambda i: (0, 0)), row(Dm), row(Dm), vec],
        out_specs=[pl.BlockSpec((1, 128), lambda i: (0, 0)), row(Dm), vec],
        out_shape=[jax.ShapeDtypeStruct((1, 128), F32), jax.ShapeDtypeStruct((T, Dm), F32),
                   jax.ShapeDtypeStruct((1, Dm), F32)],
        compiler_params=_cp("arbitrary"))(act, w, x2, tgt, g)


def _mm_nt_rms_bwd(name, a, b, x, g, dres, tm=512, dep=None):
    M, K = a.shape
    Dm = b.shape[0]
    tm = min(tm, M)

    def body(a_ref, b_ref, x_ref, g_ref, dres_ref, *rest):
        dx_ref, gg_ref = rest[-2:]

        @pl.when(pl.program_id(0) == 0)
        def _():
            gg_ref[...] = jnp.zeros_like(gg_ref)
        dx_ref[...] = _nt(a_ref[...], b_ref[...])
        for r in _sub_blocks(tm):
            xh, rr = _rms_hat(x_ref[r, :])
            dx, gg = _rms_bwd_vals(dx_ref[r, :], xh, rr, g_ref[...])
            dx_ref[r, :] = dres_ref[r, :] + dx
            gg_ref[...] += gg

    row = lambda width: pl.BlockSpec((tm, width), lambda i: (i, 0))
    vec = pl.BlockSpec((1, Dm), lambda i: (0, 0))
    in_specs, args = _with_dep([row(K), pl.BlockSpec((Dm, K), lambda i: (0, 0)), row(Dm), vec, row(Dm)],
                               [a, b, x, g, dres], dep)
    return pl.pallas_call(
        body, name=name, grid=(M // tm,), in_specs=in_specs, out_specs=[row(Dm), vec],
        out_shape=[jax.ShapeDtypeStruct((M, Dm), F32), jax.ShapeDtypeStruct((1, Dm), F32)],
        compiler_params=_cp("arbitrary"))(*args)


def _mm_nt(name, a, b, out_dtype, tm, tn, dep=None):
    M, K = a.shape
    N = b.shape[0]
    tm, tn = min(tm, M), min(tn, N)

    def body(a_ref, b_ref, *rest):
        o_ref = rest[-1]
        o_ref[...] = _nt(a_ref[...], b_ref[...]).astype(o_ref.dtype)

    in_specs, args = _with_dep(
        [pl.BlockSpec((tm, K), lambda i, j: (i, 0)), pl.BlockSpec((tn, K), lambda i, j: (j, 0))], [a, b], dep)
    return pl.pallas_call(
        body, name=name, grid=(M // tm, N // tn), in_specs=in_specs,
        out_specs=pl.BlockSpec((tm, tn), lambda i, j: (i, j)),
        out_shape=jax.ShapeDtypeStruct((M, N), out_dtype),
        compiler_params=_cp("parallel", "parallel"))(*args)


def _mm_tn(name, a, b, tm, tn, tk, col_major_tiles=False, col_groups=None):
    T, M = a.shape
    N = b.shape[1]
    tm, tn, tk = min(tm, M), min(tn, N), min(tk, T)
    nk = T // tk

    def body(a_ref, b_ref, o_ref, acc_ref):
        k = pl.program_id(2)

        @pl.when(k == 0)
        def _():
            acc_ref[...] = jnp.zeros_like(acc_ref)
        acc_ref[...] += _tn(a_ref[...], b_ref[...])

        @pl.when(k == nk - 1)
        def _():
            if col_groups:
                for j in range(col_groups[0]):
                    o_ref[j] = acc_ref[:, j * col_groups[1]:(j + 1) * col_groups[1]].astype(BF16)
            else:
                o_ref[...] = acc_ref[...].astype(BF16).reshape(o_ref.shape)

    if col_groups:
        assert tm == M and tn == N and col_groups[0] * col_groups[1] <= N
        out_spec = pl.BlockSpec((col_groups[0], M, col_groups[1]), lambda i, j, k: (0, 0, 0))
        out_shape = jax.ShapeDtypeStruct((col_groups[0], M, col_groups[1]), BF16)
    elif col_major_tiles:
        assert tm == M
        out_spec = pl.BlockSpec((1, tm, tn), lambda i, j, k: (j, 0, 0))
        out_shape = jax.ShapeDtypeStruct((N // tn, M, tn), BF16)
    else:
        out_spec = pl.BlockSpec((tm, tn), lambda i, j, k: (i, j))
        out_shape = jax.ShapeDtypeStruct((M, N), BF16)
    return pl.pallas_call(
        body, name=name, grid=(M // tm, N // tn, nk),
        in_specs=[pl.BlockSpec((tk, tm), lambda i, j, k: (k, i)), pl.BlockSpec((tk, tn), lambda i, j, k: (k, j))],
        out_specs=out_spec, out_shape=out_shape, scratch_shapes=[pltpu.VMEM((tm, tn), F32)],
        compiler_params=_cp("parallel", "parallel", "arbitrary"))(a, b)


def _halo_prev_spec(rb, width):
    return pl.BlockSpec((8, width), lambda i: (jnp.maximum(i * (rb // 8) - 1, 0), 0))


def _halo_next_spec(rb, width, T):
    return pl.BlockSpec((8, width), lambda i: (jnp.minimum((i + 1) * (rb // 8), T // 8 - 1), 0))


LANES = 128
FF_STRIPS = D_FF // LANES
ROW_CHUNK = 32


def _strip(j, base=0):
    return pl.ds(pl.multiple_of(base + j * LANES, LANES), LANES)


def _up_proj_act(h, w_up, w, b, rb=256):
    T, K = h.shape
    W = w_up.shape[1]
    rb = min(rb, T)
    nb = T // rb

    def body(h_ref, wup_ref, w_ref, b_ref, up_ref, act_ref, prev_scr, ext_scr, tail_scr):
        @pl.when(pl.program_id(0) == 0)
        def _():
            tail_scr[...] = jnp.zeros_like(tail_scr)
            prev_scr[...] = jnp.zeros_like(prev_scr)
        up_ref[...] = jnp.dot(h_ref[...], wup_ref[...], preferred_element_type=F32)
        for j in range(FF_STRIPS):
            slot = j % 2
            halves = (slice(j * LANES, (j + 1) * LANES), slice(D_FF + j * LANES, D_FF + (j + 1) * LANES))
            wv = [w_ref[:, cols] for cols in halves]
            bv = [b_ref[:, cols] for cols in halves]
            for hh, cols in enumerate(halves):
                ext_scr[slot, hh, 0:8] = tail_scr[:, cols]
                ext_scr[slot, hh, 8:] = prev_scr[:, cols]
            for r0 in range(0, rb, ROW_CHUNK):
                n = min(ROW_CHUNK, rb - r0)
                c = [ext_scr[slot, hh, 6 + r0:6 + r0 + n] * wv[hh][0:1] + ext_scr[slot, hh, 7 + r0:7 + r0 + n] * wv[hh][1:2]
                     + ext_scr[slot, hh, 8 + r0:8 + r0 + n] * wv[hh][2:3] + bv[hh] for hh in range(2)]
                act_ref[r0:r0 + n, halves[0]] = (_silu(c[0]) * c[1]).astype(BF16)
        tail_scr[...] = prev_scr[rb - 8:rb, :]
        prev_scr[...] = up_ref[...]

    cur = lambda i: (jnp.minimum(i, nb - 1), 0)
    return pl.pallas_call(
        body, name="up_proj_act", grid=(nb + 1,),
        in_specs=[pl.BlockSpec((rb, K), cur), pl.BlockSpec((K, W), lambda i: (0, 0)),
                  pl.BlockSpec((3, W), lambda i: (0, 0)), pl.BlockSpec((1, W), lambda i: (0, 0))],
        out_specs=[pl.BlockSpec((rb, W), cur), pl.BlockSpec((rb, D_FF), lambda i: (jnp.maximum(i - 1, 0), 0))],
        out_shape=[jax.ShapeDtypeStruct((T, W), F32), jax.ShapeDtypeStruct((T, D_FF), BF16)],
        scratch_shapes=[pltpu.VMEM((rb, W), F32), pltpu.VMEM((2, 2, rb + 8, LANES), F32), pltpu.VMEM((8, W), F32)],
        compiler_params=_cp("arbitrary"))(h, w_up, w, b)


def _ffn_act_bwd(up, dact, w, b, rb=256, dep=None):
    T, W = up.shape
    rb = min(rb, T)
    nb = T // rb
    re = rb + 8

    def body(up_ref, prev_ref, next_ref, da_ref, danext_ref, w_ref, b_ref, *rest):
        dup_ref, gw_ref, gb_ref, ext_scr, dc_scr = rest[-5:]
        i = pl.program_id(0)

        @pl.when(i == 0)
        def _():
            gw_ref[...] = jnp.zeros_like(gw_ref)
            gb_ref[...] = jnp.zeros_like(gb_ref)
        last = i == nb - 1

        def fold8(a):
            return jnp.sum(a.reshape(a.shape[0] // 8, 8, LANES), axis=0)

        def strip(j, slot):
            halves = (_strip(j), _strip(j, D_FF))
            wv = [w_ref[:, cols] for cols in halves]
            bv = [b_ref[:, cols] for cols in halves]
            for h, cols in enumerate(halves):
                ext_scr[slot, h,0:8] = jnp.where(i > 0, prev_ref[:, cols], 0.0)
                ext_scr[slot, h,8:8 + rb] = up_ref[:, cols]
                ext_scr[slot, h,8 + rb:] = next_ref[:, cols]
            gb = [jnp.zeros((8, LANES), F32) for _ in range(2)]
            gw = [[jnp.zeros((8, LANES), F32) for _ in range(3)] for _ in range(2)]
            for r0 in range(0, re, ROW_CHUNK):
                n = min(ROW_CHUNK, re - r0)
                tp = [[ext_scr[slot, h,6 + k + r0:6 + k + r0 + n] for k in range(3)] for h in range(2)]
                c = [tp[h][0] * wv[h][0:1] + tp[h][1] * wv[h][1:2] + tp[h][2] * wv[h][2:3] + bv[h] for h in range(2)]
                if r0 < rb:
                    da = da_ref[r0:r0 + n, halves[0]]
                else:
                    da = jnp.where(last, 0.0, danext_ref[:, halves[0]])
                s = _sigmoid(c[0])
                gs = c[0] * s
                dcs = (da * c[1] * (s + gs * (1.0 - s)), da * gs)
                for h in range(2):
                    dc_scr[slot, h,r0:r0 + n] = dcs[h]
                    if r0 < rb:
                        gb[h] = gb[h] + fold8(dcs[h])
                        for k in range(3):
                            gw[h][k] = gw[h][k] + fold8(tp[h][k] * dcs[h])
            for r0 in range(0, rb, ROW_CHUNK):
                n = min(ROW_CHUNK, rb - r0)
                for h, cols in enumerate(halves):
                    dup = (dc_scr[slot, h,r0:r0 + n] * wv[h][2:3] + dc_scr[slot, h,r0 + 1:r0 + 1 + n] * wv[h][1:2]
                           + dc_scr[slot, h,r0 + 2:r0 + 2 + n] * wv[h][0:1])
                    dup_ref[r0:r0 + n, cols] = dup.astype(BF16)
            for h, cols in enumerate(halves):
                gb_ref[:, cols] += jnp.sum(gb[h], axis=0, keepdims=True)
                for k in range(3):
                    gw_ref[k:k + 1, cols] += jnp.sum(gw[h][k], axis=0, keepdims=True)

        def pair(jj, carry):
            strip(2 * jj, 0)
            strip(2 * jj + 1, 1)
            return carry

        lax.fori_loop(0, FF_STRIPS // 2, pair, 0)

    in_specs, args = _with_dep(
        [pl.BlockSpec((rb, W), lambda i: (i, 0)), _halo_prev_spec(rb, W), _halo_next_spec(rb, W, T),
         pl.BlockSpec((rb, D_FF), lambda i: (i, 0)), _halo_next_spec(rb, D_FF, T),
         pl.BlockSpec((3, W), lambda i: (0, 0)), pl.BlockSpec((1, W), lambda i: (0, 0))],
        [up, up, up, dact, dact, w, b], dep)
    return pl.pallas_call(
        body, name="ffn_act_bwd", grid=(nb,), in_specs=in_specs,
        out_specs=[pl.BlockSpec((rb, W), lambda i: (i, 0)), pl.BlockSpec((3, W), lambda i: (0, 0)),
                   pl.BlockSpec((1, W), lambda i: (0, 0))],
        out_shape=[jax.ShapeDtypeStruct((T, W), BF16), jax.ShapeDtypeStruct((3, W), F32),
                   jax.ShapeDtypeStruct((1, W), F32)],
        scratch_shapes=[pltpu.VMEM((2, 2, rb + 16, LANES), F32), pltpu.VMEM((2, 2, re, LANES), F32)],
        compiler_params=_cp("arbitrary"))(*args)


def _lane_iota(shape):
    return lax.broadcasted_iota(jnp.int32, shape, len(shape) - 1)


def _dn_act_bwd(p, conv_w, alog_row, dtb_row, dq, dk, dv, dbg, dp_mid, rb=256):
    T = p.shape[0]
    rb = min(rb, T)
    nb = T // rb
    re = rb + 8
    W3 = 3 * DN_WIDTH

    def body(p_ref, prev_ref, next_ref, ba_ref, w_ref, al_ref, dt_ref, dq_ref, dqn_ref, dk_ref, dkn_ref,
             dv_ref, dvn_ref, dbg_ref, mid_ref, draw_ref, gw_ref, gad_ref, ext_scr, dc_scr):
        i = pl.program_id(0)
        draw_ref[:, W3:2 * W3] = mid_ref[...]

        @pl.when(i == 0)
        def _():
            gw_ref[...] = jnp.zeros_like(gw_ref)
            gad_ref[...] = jnp.zeros_like(gad_ref)
        row = lax.broadcasted_iota(jnp.int32, (re, 1), 0)
        live = (row < rb) | (i < nb - 1)
        d_refs = ((dq_ref, dqn_ref), (dk_ref, dkn_ref), (dv_ref, dvn_ref))
        for j in range(3 * N_HEADS):
            kind, h = divmod(j, N_HEADS)
            cols = slice(j * HEAD_DIM, (j + 1) * HEAD_DIM)
            hcols = slice(h * HEAD_DIM, (h + 1) * HEAD_DIM)
            ext_scr[j, 0:8] = jnp.where(i > 0, prev_ref[:, cols], 0.0)
            ext_scr[j, 8:8 + rb] = p_ref[:, cols]
            ext_scr[j, 8 + rb:] = next_ref[:, cols]
            tp = [ext_scr[j, 5 + k:5 + k + re] for k in range(4)]
            wv = w_ref[:, cols]
            c = tp[0] * wv[0:1] + tp[1] * wv[1:2] + tp[2] * wv[2:3] + tp[3] * wv[3:4]
            sg = _sigmoid(c)
            s = c * sg
            d_in = jnp.where(live, jnp.concatenate([d_refs[kind][0][:, hcols], d_refs[kind][1][:, hcols]], axis=0), 0.0)
            if kind < 2:
                scale = HEAD_DIM ** -0.5 if kind == 0 else 1.0
                n = lax.rsqrt(jnp.sum(s * s, axis=-1, keepdims=True) + EPS)
                hat = s * n
                d_in = (n * scale) * (d_in - hat * jnp.sum(hat * d_in, axis=-1, keepdims=True))
            dc = d_in * (sg + s * (1.0 - sg))
            dc_scr[j] = dc
            dcc = dc[0:rb]
            draw = (dcc * wv[3:4] + dc_scr[j, 1:1 + rb] * wv[2:3] + dc_scr[j, 2:2 + rb] * wv[1:2]
                    + dc_scr[j, 3:3 + rb] * wv[0:1])
            draw_ref[:, cols] = draw.astype(BF16)
            for k in range(4):
                gw_ref[k:k + 1, cols] += jnp.sum(tp[k][0:rb] * dcc, axis=0, keepdims=True)
        ba = ba_ref[...]
        dbg = dbg_ref[...]
        lane = _lane_iota(ba.shape)
        beta = _sigmoid(ba)
        ea = jnp.exp(al_ref[...])
        z = ba + dt_ref[...]
        d_a = dbg * (-ea) * _sigmoid(z)
        dba = jnp.where(lane < N_HEADS, dbg * beta * (1.0 - beta), jnp.where(lane < 2 * N_HEADS, d_a, 0.0))
        draw_ref[:, BA_COL:] = dba.astype(BF16)
        isg = (lane >= N_HEADS) & (lane < 2 * N_HEADS)
        g = -ea * _softplus(z)
        gad_ref[0:1, :] += jnp.sum(jnp.where(isg, dbg * g, 0.0), axis=0, keepdims=True)
        gad_ref[1:2, :] += jnp.sum(jnp.where(isg, d_a, 0.0), axis=0, keepdims=True)

    row512 = pl.BlockSpec((rb, DN_WIDTH), lambda i: (i, 0))
    row128 = pl.BlockSpec((rb, 128), lambda i: (i, 0))
    vec128 = pl.BlockSpec((1, 128), lambda i: (0, 0))
    next512 = _halo_next_spec(rb, DN_WIDTH, T)
    return pl.pallas_call(
        body, name="dn_act_bwd", grid=(nb,),
        in_specs=[pl.BlockSpec((rb, W3), lambda i: (i, 0)), _halo_prev_spec(rb, W3), _halo_next_spec(rb, W3, T),
                  pl.BlockSpec((rb, 128), lambda i: (i, BA_COL // 128)),
                  pl.BlockSpec((4, W3), lambda i: (0, 0)), vec128, vec128,
                  row512, next512, row512, next512, row512, next512, row128,
                  pl.BlockSpec((rb, W3), lambda i: (i, 0))],
        out_specs=[pl.BlockSpec((rb, PROJ_PAD), lambda i: (i, 0)),
                   pl.BlockSpec((4, W3), lambda i: (0, 0)), pl.BlockSpec((2, 128), lambda i: (0, 0))],
        out_shape=[jax.ShapeDtypeStruct((T, PROJ_PAD), BF16),
                   jax.ShapeDtypeStruct((4, W3), F32), jax.ShapeDtypeStruct((2, 128), F32)],
        scratch_shapes=[pltpu.VMEM((3 * N_HEADS, rb + 16, HEAD_DIM), F32), pltpu.VMEM((3 * N_HEADS, re, HEAD_DIM), F32)],
        compiler_params=_cp("arbitrary"))(p, p, p, p, conv_w, alog_row, dtb_row, dq, dq, dk, dk, dv, dv, dbg, dp_mid)


def _tri(incl):
    ii = lax.broadcasted_iota(jnp.int32, (CHUNK, CHUNK), 0)
    jj = lax.broadcasted_iota(jnp.int32, (CHUNK, CHUNK), 1)
    return ii, jj, ((ii >= jj) if incl else (ii > jj))


def _dn_chunk(k, bg, cb=4):
    T = k.shape[0]
    N = T // CHUNK
    cb = min(cb, N)

    def body(k_ref, bg_ref, gc_ref, gct_ref, l_ref):
        ii, jj, incl = _tri(True)
        tri = incl.astype(F32)
        U = range(cb)
        bgv = [bg_ref[u * CHUNK:(u + 1) * CHUNK, :] for u in U]
        gc = [jnp.dot(tri, bgv[u], precision=lax.Precision.HIGHEST, preferred_element_type=F32) for u in U]
        gct = [gc[u].T for u in U]
        kk = [[None] * N_HEADS for _ in U]
        for u in U:
            gc_ref[u * CHUNK:(u + 1) * CHUNK, :] = gc[u]
            gct_ref[u] = gct[u][0:8]
            for h in range(N_HEADS):
                kh = k_ref[u * CHUNK:(u + 1) * CHUNK, h * HEAD_DIM:(h + 1) * HEAD_DIM]
                kk[u][h] = _nt(kh * bgv[u][:, h:h + 1], kh)
        for u in U:
            for h in range(N_HEADS):
                gcol = gc[u][:, N_HEADS + h:N_HEADS + h + 1]
                grow = gct[u][N_HEADS + h:N_HEADS + h + 1, :]
                l_ref[u, h] = kk[u][h] * jnp.exp(jnp.where(ii > jj, gcol - grow, NEG))

    rows = cb * CHUNK
    return pl.pallas_call(
        body, name="dn_chunk", grid=(N // cb,),
        in_specs=[pl.BlockSpec((rows, DN_WIDTH), lambda n: (n, 0)), pl.BlockSpec((rows, 128), lambda n: (n, 0))],
        out_specs=[pl.BlockSpec((rows, 128), lambda n: (n, 0)), pl.BlockSpec((cb, 8, CHUNK), lambda n: (n, 0, 0)),
                   pl.BlockSpec((cb, N_HEADS, CHUNK, CHUNK), lambda n: (n, 0, 0, 0))],
        out_shape=[jax.ShapeDtypeStruct((T, 128), F32), jax.ShapeDtypeStruct((N, 8, CHUNK), F32),
                   jax.ShapeDtypeStruct((N, N_HEADS, CHUNK, CHUNK), F32)],
        compiler_params=_cp("parallel"))(k, bg)


def _tri_inv(lt):
    S = lt.shape[1]

    def body(l_ref, a_ref):
        sub = lax.broadcasted_iota(jnp.int32, (8, S), 0)
        groups = CHUNK // 8
        for i in range(CHUNK):
            acc = [((sub + 8 * k) == i).astype(F32) for k in range(groups)]
            for jb in range((i + 7) // 8):
                nk = jb + 1

                def step(j, carry, nk=nk, i=i):
                    lrow = l_ref[pl.ds(i * CHUNK + j, 1), :]
                    return tuple(carry[k] - lrow * a_ref[j, 8 * k:8 * k + 8, :] for k in range(nk))

                acc[:nk] = list(lax.fori_loop(8 * jb, min(8 * jb + 8, i), step, tuple(acc[:nk])))
            for k in range(groups):
                a_ref[i, 8 * k:8 * k + 8, :] = acc[k]

    return pl.pallas_call(
        body, name="tri_inv", out_shape=jax.ShapeDtypeStruct((CHUNK, CHUNK, S), F32),
        compiler_params=pltpu.CompilerParams(vmem_limit_bytes=VMEM_LIMIT))(lt)


def _dn_head_terms(qh, kh, vh, beta, gcol, grow):
    ii, jj, incl = _tri(True)
    gam = jnp.exp(jnp.where(incl, gcol - grow, NEG))
    glast = grow[:, CHUNK - 1:CHUNK]
    cd = jnp.exp(glast)
    shape = (CHUNK, HEAD_DIM)
    E = jnp.broadcast_to(jnp.exp(gcol), shape)
    Fd = jnp.broadcast_to(jnp.exp(glast - gcol), shape)
    beta = jnp.broadcast_to(beta, shape)
    kb = kh * beta
    return dict(ii=ii, jj=jj, gam=gam, E=E, F=Fd, beta=beta, cd=cd, kb=kb, vb=vh * beta, W=kb * E, qE=qh * E,
                kt=kh * Fd)


def _apply_a(a, u):
    hi, lo = _split(a)
    ub = _bf(u)
    return jnp.dot(hi, ub, preferred_element_type=F32) + jnp.dot(lo, ub, preferred_element_type=F32)


def _dn_scan(q, k, v, bg, gc, gct, a):
    T = q.shape[0]
    N = T // CHUNK
    cb = min(SCAN_CHUNKS, N)

    def body(q_ref, k_ref, v_ref, bg_ref, gc_ref, gct_ref, a_ref, o_ref, sall_ref, s_ref):
        @pl.when(pl.program_id(0) == 0)
        def _():
            s_ref[...] = jnp.zeros_like(s_ref)
        H = range(N_HEADS)
        sl = [slice(h * HEAD_DIM, (h + 1) * HEAD_DIM) for h in H]
        pre = []
        for u in range(cb):
            r = slice(u * CHUNK, (u + 1) * CHUNK)
            bgv, gcv, gctv = bg_ref[r, :], gc_ref[r, :], gct_ref[u]
            q_, k_ = [q_ref[r, s] for s in sl], [k_ref[r, s] for s in sl]
            t = [_dn_head_terms(q_[h], k_[h], v_ref[r, sl[h]], bgv[:, h:h + 1],
                                gcv[:, N_HEADS + h:N_HEADS + h + 1], gctv[N_HEADS + h:N_HEADS + h + 1, :]) for h in H]
            P = [_nt(q_[h], k_[h]) * t[h]["gam"] for h in H]
            pre.append((r, t, P))
        S = [s_ref[h] for h in H]
        for u in range(cb):
            r, t, P = pre[u]
            for h in H:
                sall_ref[u, h] = S[h]
            WS = [_nn(t[h]["W"], S[h]) for h in H]
            qS = [_nn(t[h]["qE"], S[h]) for h in H]
            vn = [_apply_a(a_ref[u, h], t[h]["vb"] - WS[h]) for h in H]
            Pv = [_nn(P[h], vn[h]) for h in H]
            kv = [_tn(t[h]["kt"], vn[h]) for h in H]
            for h in H:
                o_ref[r, sl[h]] = qS[h] + Pv[h]
            S = [t[h]["cd"] * S[h] + kv[h] for h in H]
        for h in H:
            s_ref[h] = S[h]

    row512 = pl.BlockSpec((cb * CHUNK, DN_WIDTH), lambda n: (n, 0))
    row128 = pl.BlockSpec((cb * CHUNK, 128), lambda n: (n, 0))
    return pl.pallas_call(
        body, name="dn_scan", grid=(N // cb,),
        in_specs=[row512, row512, row512, row128, row128, pl.BlockSpec((cb, 8, CHUNK), lambda n: (n, 0, 0)),
                  pl.BlockSpec((cb, N_HEADS, CHUNK, CHUNK), lambda n: (n, 0, 0, 0))],
        out_specs=[row512, pl.BlockSpec((cb, N_HEADS, HEAD_DIM, HEAD_DIM), lambda n: (n, 0, 0, 0))],
        out_shape=[jax.ShapeDtypeStruct((T, DN_WIDTH), F32),
                   jax.ShapeDtypeStruct((N, N_HEADS, HEAD_DIM, HEAD_DIM), F32)],
        scratch_shapes=[pltpu.VMEM((N_HEADS, HEAD_DIM, HEAD_DIM), F32)],
        compiler_params=_cp("arbitrary"))(q, k, v, bg, gc, gct, a)


def _dn_scan_bwd(q, k, v, bg, gc, gct, a, a_t, sall, do, dep=None):
    T = q.shape[0]
    N = T // CHUNK

    cb = min(SCAN_CHUNKS, N)
    nb = N // cb

    def body(q_ref, k_ref, v_ref, bg_ref, gc_ref, gct_ref, a_ref, at_ref, sall_ref, do_ref, *rest):
        dq_ref, dk_ref, dv_ref, dbg_ref, ds_ref = rest[-5:]
        @pl.when(pl.program_id(0) == 0)
        def _():
            ds_ref[...] = jnp.zeros_like(ds_ref)
        lane = _lane_iota((CHUNK, 128))
        rowi = lax.broadcasted_iota(jnp.int32, (CHUNK, 1), 0)
        ii, jj, _ = _tri(True)
        rev = (jj >= ii).astype(F32)
        H = range(N_HEADS)
        sl = [slice(h * HEAD_DIM, (h + 1) * HEAD_DIM) for h in H]
        pre = {}
        for u in reversed(range(cb)):
            r = slice(u * CHUNK, (u + 1) * CHUNK)
            bgv, gcv, gctv = bg_ref[r, :], gc_ref[r, :], gct_ref[u]
            q_, k_, v_ = [q_ref[r, s] for s in sl], [k_ref[r, s] for s in sl], [v_ref[r, s] for s in sl]
            dO = [do_ref[r, s] for s in sl]
            t = [_dn_head_terms(q_[h], k_[h], v_[h], bgv[:, h:h + 1], gcv[:, N_HEADS + h:N_HEADS + h + 1],
                                gctv[N_HEADS + h:N_HEADS + h + 1, :]) for h in H]
            beta = [t[h]["beta"] for h in H]
            S = [sall_ref[u, h] for h in H]
            A = [a_ref[u, h] for h in H]
            WS = [_nn(t[h]["W"], S[h]) for h in H]
            KK = [_nt(t[h]["kb"], k_[h]) for h in H]
            QK = [_nt(q_[h], k_[h]) for h in H]
            d_qE = [_nt(dO[h], S[h]) for h in H]
            vn = [_apply_a(A[h], t[h]["vb"] - WS[h]) for h in H]
            PtdO = [_tn(QK[h] * t[h]["gam"], dO[h]) for h in H]
            qEdO = [_tn(t[h]["qE"], dO[h]) for h in H]
            dOvn = [_nt(dO[h], vn[h]) for h in H]
            dQK = [jnp.where(ii >= jj, dOvn[h], 0.0) * t[h]["gam"] for h in H]
            dQKk = [_nn(dQK[h], k_[h]) for h in H]
            dQKq = [_tn(dQK[h], q_[h]) for h in H]
            pre[u] = (r, q_, k_, v_, beta, t, S, A, KK, QK, d_qE, vn, PtdO, qEdO, dQK, dQKk, dQKq)
        dSn = [ds_ref[h] for h in H]
        for u in reversed(range(cb)):
            r, q_, k_, v_, beta, t, S, A, KK, QK, d_qE, vn, PtdO, qEdO, dQK, dQKk, dQKq = pre[u]
            gam, E, Fd, cd, kb = ([t[h][n] for h in H] for n in ("gam", "E", "F", "cd", "kb"))
            ktdS = [_nn(t[h]["kt"], dSn[h]) for h in H]
            dU = [_apply_a(at_ref[u, h], PtdO[h] + ktdS[h]) for h in H]
            d_kt = [_nt(vn[h], dSn[h]) for h in H]
            dUvn = [_nt(dU[h], vn[h]) for h in H]
            dUS = [_nt(dU[h], S[h]) for h in H]
            WdU = [_tn(t[h]["W"], dU[h]) for h in H]
            d_cd = [jnp.sum(S[h] * dSn[h]) for h in H]
            dSn = [cd[h] * dSn[h] + qEdO[h] - WdU[h] for h in H]
            dKK = [jnp.where(ii > jj, -dUvn[h], 0.0) * gam[h] for h in H]
            dKKk = [_nn(dKK[h], k_[h]) for h in H]
            dKKkb = [_tn(dKK[h], kb[h]) for h in H]
            dbeta_arr = jnp.zeros((CHUNK, 128), F32)
            dgc_arr = jnp.zeros((CHUNK, 128), F32)
            for h in H:
                dW = -dUS[h]
                dq_ref[r, sl[h]] = dQKk[h] + d_qE[h] * E[h]
                d_kb = dKKk[h] + dW * E[h]
                dk_ref[r, sl[h]] = dQKq[h] + dKKkb[h] + d_kb * beta[h] + d_kt[h] * Fd[h]
                dv_ref[r, sl[h]] = dU[h] * beta[h]
                Z = dQK[h] * QK[h] + dKK[h] * KK[h]
                dbeta = jnp.sum(dU[h] * v_[h] + d_kb * k_[h], axis=-1, keepdims=True)
                m_e = (dW * kb[h] + d_qE[h] * q_[h]) * E[h]
                m_f = d_kt[h] * k_[h] * Fd[h]
                zdiag = jnp.where(ii == jj, jnp.sum(Z, axis=0, keepdims=True), 0.0)
                dgc = (jnp.sum(m_e - m_f, axis=-1, keepdims=True) + jnp.sum(Z - zdiag, axis=-1, keepdims=True)
                       + jnp.where(rowi == CHUNK - 1, jnp.sum(m_f) + d_cd[h] * cd[h], 0.0))
                dbeta_arr = dbeta_arr + jnp.where(lane == h, dbeta, 0.0)
                dgc_arr = dgc_arr + jnp.where(lane == N_HEADS + h, dgc, 0.0)
            dbg_ref[r, :] = dbeta_arr + jnp.dot(rev, dgc_arr, precision=lax.Precision.HIGHEST,
                                                preferred_element_type=F32)
        for h in H:
            ds_ref[h] = dSn[h]

    row512 = pl.BlockSpec((cb * CHUNK, DN_WIDTH), lambda n: (nb - 1 - n, 0))
    row128 = pl.BlockSpec((cb * CHUNK, 128), lambda n: (nb - 1 - n, 0))
    in_specs, args = _with_dep(
        [row512, row512, row512, row128, row128,
         pl.BlockSpec((cb, 8, CHUNK), lambda n: (nb - 1 - n, 0, 0)),
         pl.BlockSpec((cb, N_HEADS, CHUNK, CHUNK), lambda n: (nb - 1 - n, 0, 0, 0)),
         pl.BlockSpec((cb, N_HEADS, CHUNK, CHUNK), lambda n: (nb - 1 - n, 0, 0, 0)),
         pl.BlockSpec((cb, N_HEADS, HEAD_DIM, HEAD_DIM), lambda n: (nb - 1 - n, 0, 0, 0)), row512],
        [q, k, v, bg, gc, gct, a, a_t, sall, do], dep)
    return pl.pallas_call(
        body, name="dn_scan_bwd", grid=(nb,), in_specs=in_specs,
        out_specs=[row512, row512, row512, row128],
        out_shape=[jax.ShapeDtypeStruct((T, DN_WIDTH), F32)] * 3 + [jax.ShapeDtypeStruct((T, 128), F32)],
        scratch_shapes=[pltpu.VMEM((N_HEADS, HEAD_DIM, HEAD_DIM), F32)],
        compiler_params=_cp("arbitrary"))(*args)


MIX_BLOCKS_FWD = 4
MIX_BLOCKS_BWD = 2


def _sg_mask():
    ii = lax.broadcasted_iota(jnp.int32, (SG_BLOCK, SG_BLOCK), 0) // CHUNK
    jj = lax.broadcasted_iota(jnp.int32, (SG_BLOCK, SG_BLOCK), 1) // CHUNK
    return jj <= ii


def _mix_fwd(o, p, ong, sgn, sgw, sgbt):
    T = o.shape[0]
    rb = min(MIX_BLOCKS_FWD * SG_BLOCK, T)

    def body(o_ref, gate_ref, u_ref, vg_ref, ong_ref, sgn_ref, sgw_ref, sgbt_ref, mix_ref):
        mask = _sg_mask()
        for u0 in range(0, rb, SG_BLOCK):
            rows = slice(u0, u0 + SG_BLOCK)
            for h in range(N_HEADS):
                sl = slice(h * HEAD_DIM, (h + 1) * HEAD_DIM)
                oh = o_ref[rows, sl]
                r = lax.rsqrt(jnp.mean(oh * oh, axis=-1, keepdims=True) + EPS)
                mix_ref[rows, sl] = (oh * r * ong_ref[...] * _silu(gate_ref[rows, sl])).astype(BF16)
            for gi in range(SG_GROUPS):
                sl = slice(gi * SG_BLOCK, (gi + 1) * SG_BLOCK)
                gv = _gelu(vg_ref[rows, sl])
                r = lax.rsqrt(jnp.mean(gv * gv, axis=-1, keepdims=True) + EPS)
                vh = gv * r * sgn_ref[:, sl]
                s = _nn(jnp.where(mask, sgw_ref[gi], 0.0), vh) + sgbt_ref[:, gi:gi + 1]
                mix_ref[rows, DN_WIDTH + gi * SG_BLOCK:DN_WIDTH + (gi + 1) * SG_BLOCK] = (
                    _gelu(u_ref[rows, sl]) * s).astype(BF16)

    def col(c):
        return pl.BlockSpec((rb, 512), lambda i: (i, c))
    return pl.pallas_call(
        body, name="mix_fwd", grid=(T // rb,),
        in_specs=[pl.BlockSpec((rb, DN_WIDTH), lambda i: (i, 0)), col(3), col(4), col(5),
                  pl.BlockSpec((1, 128), lambda i: (0, 0)), pl.BlockSpec((1, SG_WIDTH), lambda i: (0, 0)),
                  pl.BlockSpec((SG_GROUPS, SG_BLOCK, SG_BLOCK), lambda i: (0, 0, 0)),
                  pl.BlockSpec((SG_BLOCK, 128), lambda i: (0, 0))],
        out_specs=pl.BlockSpec((rb, D_MODEL), lambda i: (i, 0)),
        out_shape=jax.ShapeDtypeStruct((T, D_MODEL), BF16),
        compiler_params=_cp("parallel"))(o, p, p, p, ong, sgn, sgw, sgbt)


def _mix_bwd(o, p, ong, sgn, sgw, sgbt, dmix, dep=None):
    T = o.shape[0]
    rb = min(MIX_BLOCKS_BWD * SG_BLOCK, T)

    def body(o_ref, gate_ref, u_ref, vg_ref, ong_ref, sgn_ref, sgw_ref, sgbt_ref, dmix_ref, *rest):
        do_ref, dp_ref, gong_ref, gsgn_ref, gsgw_ref, gsgbt_ref = rest[-6:]
        @pl.when(pl.program_id(0) == 0)
        def _():
            gong_ref[...] = jnp.zeros_like(gong_ref)
            gsgn_ref[...] = jnp.zeros_like(gsgn_ref)
            gsgw_ref[...] = jnp.zeros_like(gsgw_ref)
            gsgbt_ref[...] = jnp.zeros_like(gsgbt_ref)
        mask = _sg_mask()
        lane = _lane_iota((SG_BLOCK, 128))
        for u0 in range(0, rb, SG_BLOCK):
            rows = slice(u0, u0 + SG_BLOCK)
            for h in range(N_HEADS):
                sl = slice(h * HEAD_DIM, (h + 1) * HEAD_DIM)
                oh = o_ref[rows, sl]
                dm = dmix_ref[rows, sl]
                r = lax.rsqrt(jnp.mean(oh * oh, axis=-1, keepdims=True) + EPS)
                oh_hat = oh * r
                gt = gate_ref[rows, sl]
                sg = _silu(gt)
                dp_ref[rows, sl] = (dm * oh_hat * ong_ref[...] * _dsilu(gt)).astype(BF16)
                dn_ = dm * sg
                gong_ref[...] += jnp.sum(dn_ * oh_hat, axis=0, keepdims=True)
                dhat = dn_ * ong_ref[...]
                do_ref[rows, sl] = r * (dhat - oh_hat * jnp.mean(dhat * oh_hat, axis=-1, keepdims=True))
            for gi in range(SG_GROUPS):
                sl = slice(gi * SG_BLOCK, (gi + 1) * SG_BLOCK)
                vraw = vg_ref[rows, sl]
                gv = _gelu(vraw)
                r = lax.rsqrt(jnp.mean(gv * gv, axis=-1, keepdims=True) + EPS)
                vhat = gv * r
                vn = vhat * sgn_ref[:, sl]
                wm = jnp.where(mask, sgw_ref[gi], 0.0)
                s = _nn(wm, vn) + sgbt_ref[:, gi:gi + 1]
                uraw = u_ref[rows, sl]
                dm = dmix_ref[rows, DN_WIDTH + gi * SG_BLOCK:DN_WIDTH + (gi + 1) * SG_BLOCK]
                dp_ref[rows, DN_WIDTH + gi * SG_BLOCK:DN_WIDTH + (gi + 1) * SG_BLOCK] = (
                    dm * s * _dgelu(uraw)).astype(BF16)
                ds = dm * _gelu(uraw)
                gsgbt_ref[...] += jnp.where(lane == gi, jnp.sum(ds, axis=-1, keepdims=True), 0.0)
                gsgw_ref[gi] += jnp.where(mask, _nt(ds, vn), 0.0)
                dvn = _tn(wm, ds)
                gsgn_ref[:, sl] += jnp.sum(dvn * vhat, axis=0, keepdims=True)
                dhat = dvn * sgn_ref[:, sl]
                dgv = r * (dhat - vhat * jnp.mean(dhat * vhat, axis=-1, keepdims=True))
                dp_ref[rows, 2 * DN_WIDTH + gi * SG_BLOCK:2 * DN_WIDTH + (gi + 1) * SG_BLOCK] = (
                    dgv * _dgelu(vraw)).astype(BF16)

    def col(c):
        return pl.BlockSpec((rb, 512), lambda i: (i, c))
    full = lambda *s: pl.BlockSpec(s, lambda i: (0,) * len(s))
    in_specs, args = _with_dep(
        [pl.BlockSpec((rb, DN_WIDTH), lambda i: (i, 0)), col(3), col(4), col(5),
         full(1, 128), full(1, SG_WIDTH), full(SG_GROUPS, SG_BLOCK, SG_BLOCK), full(SG_BLOCK, 128),
         pl.BlockSpec((rb, D_MODEL), lambda i: (i, 0))],
        [o, p, p, p, ong, sgn, sgw, sgbt, dmix], dep)
    return pl.pallas_call(
        body, name="mix_bwd", grid=(T // rb,), in_specs=in_specs,
        out_specs=[pl.BlockSpec((rb, DN_WIDTH), lambda i: (i, 0)), pl.BlockSpec((rb, 3 * 512), lambda i: (i, 0)),
                   full(1, 128), full(1, SG_WIDTH), full(SG_GROUPS, SG_BLOCK, SG_BLOCK), full(SG_BLOCK, 128)],
        out_shape=[jax.ShapeDtypeStruct((T, DN_WIDTH), F32), jax.ShapeDtypeStruct((T, 3 * 512), BF16),
                   jax.ShapeDtypeStruct((1, 128), F32), jax.ShapeDtypeStruct((1, SG_WIDTH), F32),
                   jax.ShapeDtypeStruct((SG_GROUPS, SG_BLOCK, SG_BLOCK), F32),
                   jax.ShapeDtypeStruct((SG_BLOCK, 128), F32)],
        compiler_params=_cp("arbitrary"))(*args)


def _pad_lanes(row, offset=0):
    n = row.shape[1]
    return jnp.pad(row, ((0, 0), (offset, 128 - n - offset)))


def _local_step(x, tgt, w, dep=None, late_weights=None, on_grad=None):
    T = x.shape[0]
    N = T // CHUNK
    on_grad = on_grad or (lambda name, g: None)
    alog_row = _pad_lanes(w["dn_a_log"], N_HEADS)
    dtb_row = _pad_lanes(w["dn_dt_bias"], N_HEADS)
    sgbt = jnp.pad(w["sg_b"].T, ((0, 0), (0, 128 - SG_GROUPS)))

    p, h1, w_in_pad, q, k, v, bg = _in_proj_act(x, w["attn_norm_g"], w["w_in"], w["dn_conv_w"], alog_row, dtb_row,
                                                dep=dep)
    gc, gct, lmat = _dn_chunk(k, bg)
    lt = lmat.reshape(N * N_HEADS, CHUNK * CHUNK).T
    at = _tri_inv(lt)
    a = at.reshape(CHUNK * CHUNK, N * N_HEADS).T.reshape(N, N_HEADS, CHUNK, CHUNK)
    a_t = at.transpose(1, 0, 2).reshape(CHUNK * CHUNK, N * N_HEADS).T.reshape(N, N_HEADS, CHUNK, CHUNK)
    o, sall = _dn_scan(q, k, v, bg, gc, gct, a)
    mix = _mix_fwd(o, p, w["dn_out_norm_g"], w["sg_norm_g"], w["sg_w"], sgbt)
    if late_weights is not None:
        w = {**w, **late_weights("out_proj", mix)}
    x2, h2 = _out_proj(mix, w["w_out"], x, w["ffn_norm_g"])
    up, act = _up_proj_act(h2, w["w_up"], w["ffn_conv_w"], w["ffn_conv_b"])
    if late_weights is not None:
        w = {**w, **late_weights("down_proj", act)}
    loss, dx3, g_final = _down_proj_loss(act, w["w_down"], x2, tgt, w["final_norm_g"])

    dact = _mm_nt("d_act", dx3, w["w_down"], F32, 512, D_FF)
    g_w_down = _mm_tn("g_w_down", act, dx3, D_FF, 1024, 1024)
    tok = on_grad("w_down", g_w_down)
    dup, g_ffn_conv_w, g_ffn_conv_b = _ffn_act_bwd(up, dact, w["ffn_conv_w"], w["ffn_conv_b"], dep=tok)
    g_w_up = _mm_tn("g_w_up", h2, dup, 1024, 2 * D_FF // 4, 2048, col_major_tiles=True)
    tok = on_grad("w_up", g_w_up)
    dx2, g_ffn_norm = _mm_nt_rms_bwd("d_h2", dup, w["w_up"], x2, w["ffn_norm_g"], dx3, dep=tok)
    dmix = _mm_nt("d_mix", dx2, w["w_out"], F32, 512, 1024)
    g_w_out = _mm_tn("g_w_out", mix, dx2, 1024, 1024, 1024)
    tok = on_grad("w_out", g_w_out)
    do, dp_mid, g_ong, g_sgn, g_sgw, g_sgbt = _mix_bwd(o, p, w["dn_out_norm_g"], w["sg_norm_g"], w["sg_w"], sgbt,
                                                      dmix, dep=tok)
    early = dict(dn_out_norm_g=g_ong, sg_norm_g=g_sgn, sg_w=g_sgw, sg_bt=g_sgbt,
                 ffn_norm_g=g_ffn_norm, ffn_conv_w=g_ffn_conv_w, ffn_conv_b=g_ffn_conv_b, final_norm_g=g_final)
    tok = on_grad("small_early", early)
    dq, dk, dv, dbg = _dn_scan_bwd(q, k, v, bg, gc, gct, a, a_t, sall, do, dep=tok)
    dp, g_dn_conv_w, g_ad = _dn_act_bwd(p, w["dn_conv_w"], alog_row, dtb_row, dq, dk, dv, dbg, dp_mid)
    g_w_in = _mm_tn("g_w_in", h1, dp, 1024, PROJ_PAD, 1024, col_groups=(4, PROJ_COLS // 4))
    tok = on_grad("w_in", g_w_in)
    grad_x, g_attn_norm = _mm_nt_rms_bwd("d_h1", dp, w_in_pad, x, w["attn_norm_g"], dx2, dep=tok)

    grads = dict(attn_norm_g=g_attn_norm, w_in=g_w_in, dn_conv_w=g_dn_conv_w, a_dt=g_ad,
                 w_out=g_w_out, w_up=g_w_up, w_down=g_w_down, **early)
    return loss, grad_x, grads


def _me():
    return lax.axis_index("x"), lax.axis_index("y"), lax.axis_index("c")


def _peer(rel):
    x, y, c = _me()
    return {"x": (1 - x, y, c), "y": (x, 1 - y, c), "xy": (1 - x, 1 - y, c), "c": (x, y, 1 - c)}[rel]


def _chip_of(dev):
    return 2 * dev[0] + dev[1]


CHIP_RELS = ("x", "y", "xy")


def _run_copies(copies, sends, recvs):
    for cp in copies:
        cp.start()
    for cp in recvs:
        cp.wait_recv()
    for cp in sends:
        cp.wait_send()


def _gather_first(w_shard, small_shard):
    R = w_shard.shape[0]
    r2 = R // 2

    def body(w_ref, s_ref, w_out, s_out, send_sems, recv_sems):
        x, y, c = _me()
        me = _chip_of((x, y))
        sib = _peer("c")

        def half(chip, core):
            return w_out.at[chip, pl.ds(pl.multiple_of(core * r2, 8), r2), :]

        def copy(k, src, dst, to):
            return pltpu.make_async_remote_copy(src_ref=src, dst_ref=dst, send_sem=send_sems.at[k],
                                                recv_sem=recv_sems.at[k], device_id=to, device_id_type=MESH)

        own_rows = w_ref.at[pl.ds(pl.multiple_of(c * r2, 8), r2), :]
        first = [copy(r, own_rows, half(me, c), _peer(rel)) for r, rel in enumerate(CHIP_RELS)]
        first += [copy(3 + r, s_ref, s_out.at[me], _peer(rel)) for r, rel in enumerate(CHIP_RELS)]
        for cp in first:
            cp.start()
        passed = []
        for r, rel in enumerate(CHIP_RELS):
            their = _chip_of(_peer(rel))
            copy(r, own_rows, half(their, c), _peer(rel)).wait_recv()
            fwd = copy(6 + r, half(their, c), half(their, c), sib)
            fwd.start()
            passed.append(fwd)
        for r, rel in enumerate(CHIP_RELS):
            their = _chip_of(_peer(rel))
            copy(3 + r, s_ref, s_out.at[their], _peer(rel)).wait_recv()
            copy(6 + r, own_rows, half(their, 1 - c), sib).wait_recv()
        for cp in first + passed:
            cp.wait_send()

    w_all, s_all = pl.pallas_call(
        body, name="gather_first", in_specs=[ANY, ANY], out_specs=[ANY, ANY],
        out_shape=[jax.ShapeDtypeStruct((4,) + w_shard.shape, w_shard.dtype),
                   jax.ShapeDtypeStruct((4,) + small_shard.shape, small_shard.dtype)],
        scratch_shapes=[pltpu.SemaphoreType.DMA((9,)), pltpu.SemaphoreType.DMA((9,))])(w_shard, small_shard)
    me = _chip_of(_me())
    return (lax.dynamic_update_index_in_dim(w_all, w_shard, me, 0),
            lax.dynamic_update_index_in_dim(s_all, small_shard, me, 0))


OTHERS = tuple((fx, fy, fc) for fx in (0, 1) for fy in (0, 1) for fc in (0, 1) if (fx, fy, fc) != (0, 0, 0))


def _other(flip):
    x, y, c = _me()
    return (x ^ flip[0], y ^ flip[1], c ^ flip[2])


def _linear(dev):
    return 4 * dev[0] + 2 * dev[1] + dev[2]


def _exchange_small(small):
    def body(small_ref, out_ref, send_sems, recv_sems):
        my_slot = _linear(_me())
        sends, recvs = [], []
        for k, flip in enumerate(OTHERS):
            peer = _other(flip)
            sends.append(pltpu.make_async_remote_copy(
                src_ref=small_ref, dst_ref=out_ref.at[my_slot], send_sem=send_sems.at[k], recv_sem=recv_sems.at[k],
                device_id=peer, device_id_type=MESH))
            recvs.append(pltpu.make_async_remote_copy(
                src_ref=small_ref, dst_ref=out_ref.at[_linear(peer)], send_sem=send_sems.at[k],
                recv_sem=recv_sems.at[k], device_id=peer, device_id_type=MESH))
        _run_copies(sends, sends, recvs)

    out = pl.pallas_call(
        body, name="exchange_small", in_specs=[ANY], out_specs=ANY,
        out_shape=jax.ShapeDtypeStruct((8,) + small.shape, small.dtype),
        scratch_shapes=[pltpu.SemaphoreType.DMA((7,)), pltpu.SemaphoreType.DMA((7,))])(small)
    return lax.dynamic_update_index_in_dim(out, small, _linear(_me()), 0)


def _pair_swap(name, halves):
    n = len(halves)

    def body(*refs):
        src, out = refs[:n], refs[n:2 * n]
        send_sems, recv_sems = refs[2 * n:]
        sib = _peer("c")
        copies = [pltpu.make_async_remote_copy(
            src_ref=src[i], dst_ref=out[i], send_sem=send_sems.at[i], recv_sem=recv_sems.at[i],
            device_id=sib, device_id_type=MESH) for i in range(n)]
        _run_copies(copies, copies, copies)

    return pl.pallas_call(
        body, name=name, in_specs=[ANY] * n, out_specs=[ANY] * n,
        out_shape=[jax.ShapeDtypeStruct(h.shape, h.dtype) for h in halves],
        scratch_shapes=[pltpu.SemaphoreType.DMA((n,)), pltpu.SemaphoreType.DMA((n,))])(*halves)


HBM = pl.BlockSpec(memory_space=pltpu.HBM)
SEM = pl.BlockSpec(memory_space=pltpu.SEMAPHORE)
EFFECT = pltpu.SideEffectType.DATAFLOW_SIDE_EFFECTING


def _hbm(a):
    return pltpu.with_memory_space_constraint(a, pltpu.HBM)


def _transfer_start(name, srcs, lands, n_copies, make_copies, after=None):
    n, m = len(srcs), len(lands)

    def body(*refs):
        src, land = refs[:n], refs[n:n + m]
        outs = refs[n + m + (after is not None):]
        send_sems, recv_sems, token = outs[0], outs[1], outs[-1]
        for cp in make_copies(src, land, send_sems, recv_sems):
            cp.start()
        token[...] = jnp.zeros_like(token)

    arrs = list(srcs) + list(lands)
    in_specs, args = _with_dep([HBM] * (n + m), [_hbm(a) for a in arrs], after)
    out = pl.pallas_call(
        body, name=name,
        out_shape=(pltpu.SemaphoreType.DMA((n_copies,)), pltpu.SemaphoreType.DMA((n_copies,)),
                   *[pltpu.HBM(a.shape, a.dtype) for a in arrs], jax.ShapeDtypeStruct((8, 128), F32)),
        in_specs=in_specs,
        out_specs=(SEM, SEM, *[HBM] * (n + m), pl.BlockSpec(memory_space=pltpu.VMEM)),
        input_output_aliases={i: 2 + i for i in range(n + m)},
        compiler_params=pltpu.CompilerParams(has_side_effects=EFFECT))(*args)
    return out[0], out[1], list(out[2:2 + n]), list(out[2 + n:2 + n + m]), out[-1]


def _transfer_wait(name, send_sems, recv_sems, srcs, lands, make_copies, after):
    n, m = len(srcs), len(lands)

    def body(*refs):
        src, land = refs[:n], refs[n:n + m]
        s_sems, r_sems = refs[n + m], refs[n + m + 1]
        for cp in make_copies(src, land, s_sems, r_sems):
            cp.wait_send()
            cp.wait_recv()

    arrs = list(srcs) + list(lands)
    out = pl.pallas_call(
        body, name=name, out_shape=tuple(pltpu.HBM(a.shape, a.dtype) for a in arrs),
        in_specs=[HBM] * (n + m) + [SEM, SEM, ANY], out_specs=tuple([HBM] * (n + m)),
        input_output_aliases={i: i for i in range(n + m)},
        compiler_params=pltpu.CompilerParams(has_side_effects=EFFECT))(*arrs, send_sems, recv_sems, after)
    return list(out[:n]), list(out[n:])


def _gather_copies(src, land, send_sems, recv_sems):
    me = _chip_of(_me())
    copies = []
    for i in range(len(src)):
        for r, rel in enumerate(CHIP_RELS):
            k = 3 * i + r
            copies.append(pltpu.make_async_remote_copy(
                src_ref=src[i], dst_ref=land[i].at[me], send_sem=send_sems.at[k], recv_sem=recv_sems.at[k],
                device_id=_peer(rel), device_id_type=MESH))
    return copies


def _small_copies(src, land, send_sems, recv_sems):
    my_slot = _linear(_me())
    return [pltpu.make_async_remote_copy(
        src_ref=src[0], dst_ref=land[0].at[my_slot], send_sem=send_sems.at[k], recv_sem=recv_sems.at[k],
        device_id=_other(flip), device_id_type=MESH) for k, flip in enumerate(OTHERS)]


def _pieces_copies(src, land, send_sems, recv_sems):
    copies = []
    for k, flip in enumerate(OTHERS):
        peer = _other(flip)
        copies.append(pltpu.make_async_remote_copy(
            src_ref=src[0].at[_linear(peer)], dst_ref=land[0].at[k], send_sem=send_sems.at[k],
            recv_sem=recv_sems.at[k], device_id=peer, device_id_type=MESH))
    return copies


def _row_block(rows, cols, budget=2 * 1024 * 1024):
    rb = max(8, (budget // (4 * cols)) // 8 * 8)
    while rows % rb:
        rb -= 8
    return rb if rb > 0 else rows


def _sum_pieces(name, mine, slot, rest):
    _, R, Cc = mine.shape
    K = rest.shape[0]
    rb = _row_block(R, Cc)

    def body(s_ref, f_ref, r_ref, o_ref):
        acc = f_ref[0].astype(F32)
        for j in range(K):
            acc = acc + r_ref[j].astype(F32)
        o_ref[...] = acc

    return pl.pallas_call(
        body, name=name,
        grid_spec=pltpu.PrefetchScalarGridSpec(
            num_scalar_prefetch=1, grid=(R // rb,),
            in_specs=[pl.BlockSpec((1, rb, Cc), lambda i, s: (s[0], i, 0)),
                      pl.BlockSpec((K, rb, Cc), lambda i, s: (0, i, 0))],
            out_specs=pl.BlockSpec((rb, Cc), lambda i, s: (i, 0))),
        out_shape=jax.ShapeDtypeStruct((R, Cc), F32), compiler_params=_cp("parallel"))(slot, mine, rest)


def _adamw_math(w, gv, m, v):
    mn = ADAM_B1 * m + (1.0 - ADAM_B1) * gv
    vn = ADAM_B2 * v + (1.0 - ADAM_B2) * (gv * gv)
    m_hat = mn / (1.0 - ADAM_B1 ** ADAM_STEP)
    v_hat = vn / (1.0 - ADAM_B2 ** ADAM_STEP)
    return -ADAM_LR * (m_hat / (jnp.sqrt(v_hat) + ADAM_EPS) + ADAM_WD * w), mn, vn


def _adamw_halves(name, w, mine, theirs, m, v, core):
    R, Cc = w.shape
    r2 = R // 2
    rb = _row_block(r2, Cc, 1024 * 1024)
    nb2 = r2 // rb

    def body(c_ref, w_ref, mine_ref, theirs_ref, m_ref, v_ref, g_ref, d_ref, mo_ref, vo_ref):
        is_mine = (pl.program_id(0) // nb2) == c_ref[0]
        gv = jnp.where(is_mine, mine_ref[...], theirs_ref[...])
        g_ref[...] = gv
        d_ref[...], mo_ref[...], vo_ref[...] = _adamw_math(w_ref[...], gv, m_ref[...], v_ref[...])

    blk = pl.BlockSpec((rb, Cc), lambda i, c: (i, 0))
    half = lambda own: pl.BlockSpec(
        (rb, Cc), lambda i, c: (jnp.clip(i - (c[0] if own else 1 - c[0]) * nb2, 0, nb2 - 1), 0))
    return pl.pallas_call(
        body, name=name,
        grid_spec=pltpu.PrefetchScalarGridSpec(
            num_scalar_prefetch=1, grid=(2 * nb2,), in_specs=[blk, half(True), half(False), blk, blk],
            out_specs=[blk] * 4),
        out_shape=[jax.ShapeDtypeStruct((R, Cc), F32)] * 4, compiler_params=_cp("parallel"))(core, w, mine, theirs, m, v)


def _adamw_transposed(name, wt, mine, theirs, mt, vt, core):
    Cc, kh_n, _ = wt.shape
    r2 = mine.shape[0]
    per_half = kh_n // 2
    nb = -(-Cc // LANES)

    def body(c_ref, w_ref, mine_ref, theirs_ref, m_ref, v_ref, g_ref, d_ref, mo_ref, vo_ref):
        first = c_ref[0] == 0
        halves = (jnp.where(first, mine_ref[...], theirs_ref[...]).T,
                  jnp.where(first, theirs_ref[...], mine_ref[...]).T)
        for kh in range(kh_n):
            lo = (kh % per_half) * LANES
            g_ref[:, kh, :] = halves[kh // per_half][:, lo:lo + LANES]
        d_ref[...], mo_ref[...], vo_ref[...] = _adamw_math(w_ref[...], g_ref[...], m_ref[...], v_ref[...])

    blk = pl.BlockSpec((LANES, kh_n, LANES), lambda i, c: (i, 0, 0))
    half = pl.BlockSpec((r2, LANES), lambda i, c: (0, i))
    return pl.pallas_call(
        body, name=name,
        grid_spec=pltpu.PrefetchScalarGridSpec(
            num_scalar_prefetch=1, grid=(nb,), in_specs=[blk, half, half, blk, blk], out_specs=[blk] * 4),
        out_shape=[jax.ShapeDtypeStruct(wt.shape, F32)] * 4, compiler_params=_cp("parallel"))(
            core, wt, mine, theirs, mt, vt)


FF_W = 2 * D_FF
FF_CH = FF_W // LANES
DNC_W = 3 * DN_WIDTH
DNC_CH = DNC_W // LANES
E_ONG, E_SGN, E_SGW, E_SGBT = 0, 1, 8, 8 + SG_GROUPS * SG_BLOCK
E_FFN = E_SGBT + SG_BLOCK
E_FCW = E_FFN + D_MODEL // LANES
E_FCB = E_FCW + 3 * FF_CH
E_FIN = E_FCB + FF_CH
EARLY_ROWS = E_FIN + D_MODEL // LANES
L_ATTN, L_DNC = 0, D_MODEL // LANES
L_AD = L_DNC + 4 * DNC_CH
L_LOSS = L_AD + 2
LATE_ROWS = -(-(L_LOSS + 1) // 8) * 8


def _put_rows(out, r0, x):
    k, width = x.shape
    n = width // LANES
    for t in range(k):
        for j in range(n):
            out[r0 + t * n + j:r0 + t * n + j + 1, :] = x[t:t + 1, j * LANES:(j + 1) * LANES]


def _pack_early(ong, sgn, sgw, sgbt, ffn, fcw, fcb, fin):
    def body(ong_ref, sgn_ref, sgw_ref, sgbt_ref, ffn_ref, fcw_ref, fcb_ref, fin_ref, out):
        out[...] = jnp.zeros_like(out)
        _put_rows(out, E_ONG, ong_ref)
        _put_rows(out, E_SGN, sgn_ref)
        for gi in range(SG_GROUPS):
            out[E_SGW + gi * SG_BLOCK:E_SGW + (gi + 1) * SG_BLOCK, :] = sgw_ref[gi]
        out[E_SGBT:E_SGBT + SG_BLOCK, :] = sgbt_ref[...]
        _put_rows(out, E_FFN, ffn_ref)
        _put_rows(out, E_FCW, fcw_ref)
        _put_rows(out, E_FCB, fcb_ref)
        _put_rows(out, E_FIN, fin_ref)

    return pl.pallas_call(body, name="pack_small_early", out_shape=jax.ShapeDtypeStruct((EARLY_ROWS, LANES), F32))(
        ong, sgn, sgw, sgbt, ffn, fcw, fcb, fin)


def _pack_late(attn, dnc, ad, loss_row):
    def body(attn_ref, dnc_ref, ad_ref, loss_ref, out):
        out[...] = jnp.zeros_like(out)
        _put_rows(out, L_ATTN, attn_ref)
        _put_rows(out, L_DNC, dnc_ref)
        out[L_AD:L_AD + 2, :] = ad_ref[...]
        out[L_LOSS:L_LOSS + 1, :] = loss_ref[...]

    return pl.pallas_call(body, name="pack_small_late", out_shape=jax.ShapeDtypeStruct((LATE_ROWS, LANES), F32))(
        attn, dnc, ad, loss_row)


SMALL = ("attn_norm_g", "dn_a_log", "dn_dt_bias", "dn_out_norm_g", "sg_norm_g", "sg_w", "sg_b", "ffn_norm_g",
         "ffn_conv_b", "final_norm_g", "dn_conv_w", "ffn_conv_w")


def _small_update(early_all, late_all, chip, W, M, V):
    n = len(SMALL)
    arrs = [d[k] for d in (W, M, V) for k in SMALL]

    def body(c_ref, e_ref, l_ref, *refs):
        w_, m_, v_ = refs[:n], refs[n:2 * n], refs[2 * n:3 * n]
        loss_ref = refs[3 * n]
        outs = refs[3 * n + 1:]
        g_, d_, mo_, vo_ = outs[:n], outs[n:2 * n], outs[2 * n:3 * n], outs[3 * n:4 * n]
        chip_i = c_ref[0]

        def total(ref, r0, rows=1):
            acc = ref[0, pl.ds(r0, rows), :]
            for s in range(1, 8):
                acc = acc + ref[s, pl.ds(r0, rows), :]
            return acc

        def update(i, idx, g):
            g_[i][idx] = g
            d_[i][idx], mo_[i][idx], vo_[i][idx] = _adamw_math(w_[i][idx], g, m_[i][idx], v_[i][idx])

        def rows_param(name, ref, r0, width):
            i = SMALL.index(name)
            for j in range(width // LANES):
                update(i, (slice(None), slice(j * LANES, (j + 1) * LANES)), total(ref, r0 + j))

        rows_param("attn_norm_g", l_ref, L_ATTN, D_MODEL)
        ad = (total(l_ref, L_AD), total(l_ref, L_AD + 1))
        update(SMALL.index("dn_a_log"), (slice(None), slice(None)), ad[0][:, N_HEADS:2 * N_HEADS])
        update(SMALL.index("dn_dt_bias"), (slice(None), slice(None)), ad[1][:, N_HEADS:2 * N_HEADS])
        rows_param("dn_out_norm_g", e_ref, E_ONG, HEAD_DIM)
        rows_param("sg_norm_g", e_ref, E_SGN, SG_WIDTH)
        sgbt = total(e_ref, E_SGBT, SG_BLOCK).T
        for gi in range(SG_GROUPS):
            update(SMALL.index("sg_w"), (0, gi), total(e_ref, E_SGW + gi * SG_BLOCK, SG_BLOCK))
            update(SMALL.index("sg_b"), (0, slice(gi, gi + 1), slice(None)), sgbt[gi:gi + 1, :])
        rows_param("ffn_norm_g", e_ref, E_FFN, D_MODEL)
        rows_param("ffn_conv_b", e_ref, E_FCB, FF_W)
        rows_param("final_norm_g", e_ref, E_FIN, D_MODEL)
        for name, ref, r0, taps, chunks in (("dn_conv_w", l_ref, L_DNC, 4, DNC_CH), ("ffn_conv_w", e_ref, E_FCW, 3, FF_CH)):
            mine = chunks // 4
            for t in range(taps):
                for j in range(mine):
                    update(SMALL.index(name), (0, slice(t, t + 1), slice(j * LANES, (j + 1) * LANES)),
                           total(ref, r0 + t * chunks + chip_i * mine + j))
        loss_ref[...] = total(l_ref, L_LOSS)

    full = lambda a: pl.BlockSpec(a.shape, lambda i, c, nd=a.ndim: (0,) * nd)
    shapes = [jax.ShapeDtypeStruct(W[k].shape, F32) for k in SMALL]
    outs = pl.pallas_call(
        body, name="small_update",
        grid_spec=pltpu.PrefetchScalarGridSpec(
            num_scalar_prefetch=1, grid=(1,), in_specs=[full(early_all), full(late_all)] + [full(a) for a in arrs],
            out_specs=[pl.BlockSpec((1, LANES), lambda i, c: (0, 0))] + [full(s) for s in shapes] * 4),
        out_shape=[jax.ShapeDtypeStruct((1, LANES), F32)] + shapes * 4,
        compiler_params=pltpu.CompilerParams(vmem_limit_bytes=VMEM_LIMIT))(chip, early_all, late_all, *arrs)
    loss, outs = outs[0], outs[1:]
    return (loss,) + tuple(dict(zip(SMALL, outs[k * n:(k + 1) * n])) for k in range(4))


ORDER =("attn_norm_g", "w_in", "dn_conv_w", "dn_a_log", "dn_dt_bias", "dn_out_norm_g", "sg_norm_g", "sg_w",
         "sg_b", "w_out", "ffn_norm_g", "w_up", "ffn_conv_w", "ffn_conv_b", "w_down", "final_norm_g")


def kernel(x, attn_norm_g, w_in, dn_conv_w, dn_a_log, dn_dt_bias, dn_out_norm_g, sg_norm_g, sg_w, sg_b, w_out, ffn_norm_g, w_up, ffn_conv_w, ffn_conv_b, w_down, final_norm_g, loss_target, m_attn_norm_g, m_w_in, m_dn_conv_w, m_dn_a_log, m_dn_dt_bias, m_dn_out_norm_g, m_sg_norm_g, m_sg_w, m_sg_b, m_w_out, m_ffn_norm_g, m_w_up, m_ffn_conv_w, m_ffn_conv_b, m_w_down, m_final_norm_g, v_attn_norm_g, v_w_in, v_dn_conv_w, v_dn_a_log, v_dn_dt_bias, v_dn_out_norm_g, v_sg_norm_g, v_sg_w, v_sg_b, v_w_out, v_ffn_norm_g, v_w_up, v_ffn_conv_w, v_ffn_conv_b, v_w_down, v_final_norm_g):
    W = dict(attn_norm_g=attn_norm_g, w_in=w_in, dn_conv_w=dn_conv_w, dn_a_log=dn_a_log, dn_dt_bias=dn_dt_bias,
             dn_out_norm_g=dn_out_norm_g, sg_norm_g=sg_norm_g, sg_w=sg_w, sg_b=sg_b, w_out=w_out,
             ffn_norm_g=ffn_norm_g, w_up=w_up, ffn_conv_w=ffn_conv_w, ffn_conv_b=ffn_conv_b, w_down=w_down,
             final_norm_g=final_norm_g)
    Mo = dict(attn_norm_g=m_attn_norm_g, w_in=m_w_in, dn_conv_w=m_dn_conv_w, dn_a_log=m_dn_a_log,
              dn_dt_bias=m_dn_dt_bias, dn_out_norm_g=m_dn_out_norm_g, sg_norm_g=m_sg_norm_g, sg_w=m_sg_w,
              sg_b=m_sg_b, w_out=m_w_out, ffn_norm_g=m_ffn_norm_g, w_up=m_w_up, ffn_conv_w=m_ffn_conv_w,
              ffn_conv_b=m_ffn_conv_b, w_down=m_w_down, final_norm_g=m_final_norm_g)
    Vo = dict(attn_norm_g=v_attn_norm_g, w_in=v_w_in, dn_conv_w=v_dn_conv_w, dn_a_log=v_dn_a_log,
              dn_dt_bias=v_dn_dt_bias, dn_out_norm_g=v_dn_out_norm_g, sg_norm_g=v_sg_norm_g, sg_w=v_sg_w,
              sg_b=v_sg_b, w_out=v_w_out, ffn_norm_g=v_ffn_norm_g, w_up=v_w_up, ffn_conv_w=v_ffn_conv_w,
              ffn_conv_b=v_ffn_conv_b, w_down=v_w_down, final_norm_g=v_final_norm_g)
    xi, yi, ci = lax.axis_index("x"), lax.axis_index("y"), lax.axis_index("c")
    chip = 2 * xi + yi

    me_lin = 4 * xi + 2 * yi + ci

    g_in, g_dnc = _gather_first(w_in[0].astype(BF16), dn_conv_w[0])
    def start_gather(name, shards, after):
        lands = [lax.dynamic_update_index_in_dim(lax.empty((4,) + s.shape, s.dtype), s, chip, 0) for s in shards]
        return _transfer_start(name, shards, lands, 3 * len(shards), _gather_copies, after=after)

    mid = start_gather("gather_mid_start", [w_out[0].astype(BF16), w_up[0].astype(BF16), ffn_conv_w[0]], g_in)
    last = start_gather("gather_last_start", [w_down[0].astype(BF16)], mid[4])
    token = last[4]

    def late_weights(stage, after):
        if stage == "out_proj":
            _, (g_out, g_up, g_ffc) = _transfer_wait("gather_mid_wait", *mid[:4], _gather_copies, after)
            return dict(w_out=g_out.reshape(D_MODEL, D_MODEL), ffn_conv_w=g_ffc.transpose(1, 0, 2).reshape(3, 2 * D_FF),
                        w_up=g_up.transpose(1, 0, 2).reshape(D_MODEL, 2 * D_FF))
        _, (g_down,) = _transfer_wait("gather_last_wait", *last[:4], _gather_copies, after)
        return dict(w_down=g_down.reshape(D_FF, D_MODEL))

    full = dict(
        w_in=g_in,
        dn_conv_w=g_dnc.transpose(1, 0, 2).reshape(4, 3 * DN_WIDTH),
        attn_norm_g=attn_norm_g, dn_a_log=dn_a_log, dn_dt_bias=dn_dt_bias, dn_out_norm_g=dn_out_norm_g,
        sg_norm_g=sg_norm_g, sg_w=sg_w[0], sg_b=sg_b[0], ffn_norm_g=ffn_norm_g, ffn_conv_b=ffn_conv_b,
        final_norm_g=final_norm_g[None])

    pending = {}

    def on_grad(name, gw):
        if name == "small_early":
            buf = _pack_early(gw["dn_out_norm_g"], gw["sg_norm_g"], gw["sg_w"], gw["sg_bt"], gw["ffn_norm_g"],
                              gw["ffn_conv_w"], gw["ffn_conv_b"], gw["final_norm_g"])
            land = lax.dynamic_update_index_in_dim(lax.empty((8,) + buf.shape, F32), buf, me_lin, 0)
            s_sem, r_sem, src, lands, tok = _transfer_start("small_early_start", [buf], [land], 7, _small_copies)
            pending[name] = (s_sem, r_sem, src, lands)
            return tok
        g8 = gw.reshape(8, -1, gw.shape[-1])
        land = lax.empty((7,) + g8.shape[1:], BF16)
        s_sem, r_sem, src, lands, tok = _transfer_start(f"reduce_{name}_start", [g8], [land], 7, _pieces_copies)
        pending[name] = (s_sem, r_sem, src, lands)
        return tok

    loss_row, grad_x, g = _local_step(x[0], loss_target[0], full, dep=token, late_weights=late_weights,
                                      on_grad=on_grad)

    late_all = _exchange_small(_pack_late(g["attn_norm_g"], g["dn_conv_w"], g["a_dt"], loss_row))
    s_sem, r_sem, src, lands = pending["small_early"]
    _, (early_all,) = _transfer_wait("small_early_wait", s_sem, r_sem, src, lands, _small_copies, grad_x)
    row = lambda d: {k: (d[k].reshape(1, -1) if k == "final_norm_g" else d[k]) for k in SMALL}
    loss_sum, *small_out = _small_update(early_all, late_all, chip.astype(jnp.int32).reshape(1), row(W), row(Mo), row(Vo))
    loss = loss_sum[0, 0]

    def summed_half(n, after):
        s_sem, r_sem, src, lands = pending[n]
        sent, got = _transfer_wait(f"reduce_{n}_wait", s_sem, r_sem, src, lands, _pieces_copies, after)
        return _sum_pieces(f"sum_{n}", sent[0], me_lin.astype(jnp.int32).reshape(1), got[0])

    first3 = ("w_down", "w_up", "w_out")
    halves = [summed_half(n, grad_x) for n in first3]
    theirs = _pair_swap("pair_swap", halves)
    core = ci.astype(jnp.int32).reshape(1)
    grads, delta, new_m, new_v = {}, {}, {}, {}
    for n, mine_h, their_h in zip(first3, halves, theirs):
        shp = W[n].shape
        gr, d, mn, vn = _adamw_halves(f"adamw_{n}", W[n][0], mine_h, their_h, Mo[n][0], Vo[n][0], core)
        grads[n], delta[n], new_m[n], new_v[n] = gr.reshape(shp), d.reshape(shp), mn.reshape(shp), vn.reshape(shp)
    mine_h = summed_half("w_in", delta["w_out"])
    (their_h,) = _pair_swap("pair_swap_w_in", [mine_h])
    shp = w_in.shape
    to_t = lambda a: a.reshape(shp[1] // LANES, LANES, shp[2]).transpose(2, 0, 1)
    from_t = lambda a: a.transpose(1, 2, 0).reshape(shp)
    outs = _adamw_transposed("adamw_w_in", to_t(w_in), mine_h, their_h, to_t(m_w_in), to_t(v_w_in), core)
    grads["w_in"], delta["w_in"], new_m["w_in"], new_v["w_in"] = (from_t(o) for o in outs)
    for dst, src_d in zip((grads, delta, new_m, new_v), small_out):
        dst.update({k: (a.reshape(W[k].shape) if k == "final_norm_g" else a) for k, a in src_d.items()})

    return (loss, grad_x[None], *[grads[n] for n in ORDER], *[delta[n] for n in ORDER],
            *[new_m[n] for n in ORDER], *[new_v[n] for n in ORDER])
```

```python
import math

import jax
import jax.numpy as jnp
from jax import lax
from jax.experimental import pallas as pl
from jax.experimental.pallas import tpu as pltpu

F32 = jnp.float32
BF16 = jnp.bfloat16

D_MODEL = 1024
CHUNK = 64
SCAN_CHUNKS = 8
HEAD_DIM = 128
N_HEADS = 4
DN_WIDTH = 512
SG_WIDTH = 512
SG_GROUPS = 4
SG_BLOCK = 128
D_FF = 2816
PROJ_COLS = 3080
PROJ_PAD = 3200
BA_COL = 3072
EPS = 1e-6
NEG = -1e30
VMEM_LIMIT = 56 * 1024 * 1024

ADAM_LR = 0.001
ADAM_B1 = 0.9
ADAM_B2 = 0.999
ADAM_EPS = 1e-08
ADAM_WD = 0.01
ADAM_STEP = 10

MESH = pl.DeviceIdType.MESH
ANY = pl.BlockSpec(memory_space=pl.ANY)


def _cp(*sem):
    return pltpu.CompilerParams(dimension_semantics=sem, vmem_limit_bytes=VMEM_LIMIT)


def _bf(a):
    return a.astype(BF16)


def _nn(a, b):
    return jnp.dot(_bf(a), _bf(b), preferred_element_type=F32)


def _nt(a, b):
    return lax.dot_general(_bf(a), _bf(b), (((1,), (1,)), ((), ())), preferred_element_type=F32)


def _tn(a, b):
    return lax.dot_general(_bf(a), _bf(b), (((0,), (0,)), ((), ())), preferred_element_type=F32)


def _split(a):
    hi = _bf(a)
    return hi, _bf(a - hi.astype(F32))


def _sigmoid(x):
    return 0.5 * jnp.tanh(0.5 * x) + 0.5


def _silu(x):
    return x * _sigmoid(x)


def _dsilu(x):
    s = _sigmoid(x)
    return s * (1.0 + x * (1.0 - s))


_GELU_C = math.sqrt(2.0 / math.pi)
_GELU_A = 0.044715


def _gelu(x):
    return 0.5 * x * (1.0 + jnp.tanh(_GELU_C * (x + _GELU_A * x * x * x)))


def _dgelu(x):
    t = jnp.tanh(_GELU_C * (x + _GELU_A * x * x * x))
    return 0.5 * (1.0 + t) + 0.5 * x * (1.0 - t * t) * _GELU_C * (1.0 + 3.0 * _GELU_A * x * x)


def _softplus(x):
    return jnp.maximum(x, 0.0) + jnp.log(1.0 + jnp.exp(-jnp.abs(x)))


def _with_dep(in_specs, args, dep):
    if dep is None:
        return in_specs, args
    return in_specs + [ANY], args + [dep]


SUB_ROWS = 128


def _sub_blocks(tm):
    return [slice(r0, min(r0 + SUB_ROWS, tm)) for r0 in range(0, tm, SUB_ROWS)]


def _rms_hat(xv):
    r = lax.rsqrt(jnp.mean(xv * xv, axis=-1, keepdims=True) + EPS)
    return xv * r, r


def _rms_bwd_vals(dh, xh, r, g):
    dxh = dh * g
    return r * (dxh - xh * jnp.mean(dxh * xh, axis=-1, keepdims=True)), jnp.sum(dh * xh, axis=0, keepdims=True)


def _in_proj_act(x, g, w4, conv_w, alog_row, dtb_row, tm=256, dep=None):
    T, K = x.shape
    ng, _, wc = w4.shape
    tm = min(tm, T)
    nb = T // tm
    W3 = 3 * DN_WIDTH

    def body(x_ref, g_ref, w4_ref, cw_ref, al_ref, dt_ref, *rest):
        p_ref, h_ref, w_ref, q_ref, k_ref, v_ref, bg_ref, prev_scr, ext_scr, tail_scr = rest[-10:]

        @pl.when(pl.program_id(0) == 0)
        def _():
            w_ref[:, ng * wc:] = jnp.zeros((K, PROJ_PAD - ng * wc), BF16)
            for j in range(ng):
                w_ref[:, j * wc:(j + 1) * wc] = w4_ref[j]
            tail_scr[...] = jnp.zeros_like(tail_scr)
            prev_scr[...] = jnp.zeros_like(prev_scr)
        for r in _sub_blocks(tm):
            xh, _ = _rms_hat(x_ref[r, :])
            h_ref[r, :] = (xh * g_ref[...]).astype(BF16)
        p_ref[...] = jnp.dot(h_ref[...], w_ref[...], preferred_element_type=F32)
        outs = (q_ref, k_ref, v_ref)
        for j in range(3 * N_HEADS):
            kind, hd = divmod(j, N_HEADS)
            cols = slice(j * HEAD_DIM, (j + 1) * HEAD_DIM)
            cur = prev_scr[:, cols]
            ext_scr[j, 0:8] = tail_scr[:, cols]
            ext_scr[j, 8:] = cur
            wv = cw_ref[:, cols]
            s = _silu(ext_scr[j, 5:5 + tm] * wv[0:1] + ext_scr[j, 6:6 + tm] * wv[1:2]
                      + ext_scr[j, 7:7 + tm] * wv[2:3] + cur * wv[3:4])
            if kind < 2:
                scale = HEAD_DIM ** -0.5 if kind == 0 else 1.0
                s = s * (lax.rsqrt(jnp.sum(s * s, axis=-1, keepdims=True) + EPS) * scale)
            outs[kind][:, hd * HEAD_DIM:(hd + 1) * HEAD_DIM] = s
        ba = prev_scr[:, W3:]
        lane = _lane_iota(ba.shape)
        beta = _sigmoid(ba)
        gl = -jnp.exp(al_ref[...]) * _softplus(ba + dt_ref[...])
        bg_ref[...] = jnp.where(lane < N_HEADS, beta, jnp.where(lane < 2 * N_HEADS, gl, 0.0))
        tail_scr[...] = prev_scr[tm - 8:tm, 0:W3]
        prev_scr[:, 0:W3] = p_ref[:, 0:W3]
        prev_scr[:, W3:] = p_ref[:, BA_COL:]

    cur_blk = lambda i: (jnp.minimum(i, nb - 1), 0)
    prev_blk = lambda i: (jnp.maximum(i - 1, 0), 0)
    vec128 = pl.BlockSpec((1, 128), lambda i: (0, 0))
    in_specs, args = _with_dep(
        [pl.BlockSpec((tm, K), cur_blk), pl.BlockSpec((1, K), lambda i: (0, 0)),
         pl.BlockSpec((ng, K, wc), lambda i: (0, 0, 0)), pl.BlockSpec((4, W3), lambda i: (0, 0)), vec128, vec128],
        [x, g, w4, conv_w, alog_row, dtb_row], dep)
    row512 = pl.BlockSpec((tm, DN_WIDTH), prev_blk)
    return pl.pallas_call(
        body, name="in_proj_act", grid=(nb + 1,), in_specs=in_specs,
        out_specs=[pl.BlockSpec((tm, PROJ_PAD), cur_blk), pl.BlockSpec((tm, K), cur_blk),
                   pl.BlockSpec((K, PROJ_PAD), lambda i: (0, 0)), row512, row512, row512,
                   pl.BlockSpec((tm, 128), prev_blk)],
        out_shape=[jax.ShapeDtypeStruct((T, PROJ_PAD), F32), jax.ShapeDtypeStruct((T, K), BF16),
                   jax.ShapeDtypeStruct((K, PROJ_PAD), BF16)] + [jax.ShapeDtypeStruct((T, DN_WIDTH), F32)] * 3
        + [jax.ShapeDtypeStruct((T, 128), F32)],
        scratch_shapes=[pltpu.VMEM((tm, W3 + 128), F32), pltpu.VMEM((3 * N_HEADS, tm + 8, HEAD_DIM), F32),
                        pltpu.VMEM((8, W3), F32)],
        compiler_params=_cp("arbitrary"))(*args)


def _out_proj(mix, w, x, g, tm=512):
    T, K = mix.shape
    Dm = w.shape[1]
    tm = min(tm, T)

    def body(a_ref, w_ref, x_ref, g_ref, x2_ref, h_ref):
        x2_ref[...] = _nn(a_ref[...], w_ref[...]) + x_ref[...]
        for r in _sub_blocks(tm):
            xh, _ = _rms_hat(x2_ref[r, :])
            h_ref[r, :] = (xh * g_ref[...]).astype(BF16)

    row = lambda width: pl.BlockSpec((tm, width), lambda i: (i, 0))
    return pl.pallas_call(
        body, name="out_proj", grid=(T // tm,),
        in_specs=[row(K), pl.BlockSpec((K, Dm), lambda i: (0, 0)), row(Dm), pl.BlockSpec((1, Dm), lambda i: (0, 0))],
        out_specs=[row(Dm), row(Dm)],
        out_shape=[jax.ShapeDtypeStruct((T, Dm), F32), jax.ShapeDtypeStruct((T, Dm), BF16)],
        compiler_params=_cp("parallel"))(mix, w, x, g)


def _down_proj_loss(act, w, x2, tgt, g, tm=512):
    T, K = act.shape
    Dm = w.shape[1]
    tm = min(tm, T)

    def body(a_ref, w_ref, x_ref, t_ref, g_ref, loss_ref, dx_ref, gg_ref):
        @pl.when(pl.program_id(0) == 0)
        def _():
            gg_ref[...] = jnp.zeros_like(gg_ref)
            loss_ref[...] = jnp.zeros_like(loss_ref)
        dx_ref[...] = _nn(a_ref[...], w_ref[...]) + x_ref[...]
        for r in _sub_blocks(tm):
            xh, rr = _rms_hat(dx_ref[r, :])
            e = xh * g_ref[...] - t_ref[r, :]
            loss_ref[...] += jnp.zeros_like(loss_ref) + (0.5 / Dm) * jnp.sum(e * e)
            dx, gg = _rms_bwd_vals(e * (1.0 / Dm), xh, rr, g_ref[...])
            dx_ref[r, :] = dx
            gg_ref[...] += gg

    row = lambda width: pl.BlockSpec((tm, width), lambda i: (i, 0))
    vec = pl.BlockSpec((1, Dm), lambda i: (0, 0))
    return pl.pallas_call(
        body, name="down_proj_loss", grid=(T // tm,),
        in_specs=[row(K), pl.BlockSpec((K, Dm), lambda i: (0, 0)), row(Dm), row(Dm), vec],
        out_specs=[pl.BlockSpec((1, 128), lambda i: (0, 0)), row(Dm), vec],
        out_shape=[jax.ShapeDtypeStruct((1, 128), F32), jax.ShapeDtypeStruct((T, Dm), F32),
                   jax.ShapeDtypeStruct((1, Dm), F32)],
        compiler_params=_cp("arbitrary"))(act, w, x2, tgt, g)


def _mm_nt_rms_bwd(name, a, b, x, g, dres, tm=512, dep=None):
    M, K = a.shape
    Dm = b.shape[0]
    tm = min(tm, M)

    def body(a_ref, b_ref, x_ref, g_ref, dres_ref, *rest):
        dx_ref, gg_ref = rest[-2:]

        @pl.when(pl.program_id(0) == 0)
        def _():
            gg_ref[...] = jnp.zeros_like(gg_ref)
        dx_ref[...] = _nt(a_ref[...], b_ref[...])
        for r in _sub_blocks(tm):
            xh, rr = _rms_hat(x_ref[r, :])
            dx, gg = _rms_bwd_vals(dx_ref[r, :], xh, rr, g_ref[...])
            dx_ref[r, :] = dres_ref[r, :] + dx
            gg_ref[...] += gg

    row = lambda width: pl.BlockSpec((tm, width), lambda i: (i, 0))
    vec = pl.BlockSpec((1, Dm), lambda i: (0, 0))
    in_specs, args = _with_dep([row(K), pl.BlockSpec((Dm, K), lambda i: (0, 0)), row(Dm), vec, row(Dm)],
                               [a, b, x, g, dres], dep)
    return pl.pallas_call(
        body, name=name, grid=(M // tm,), in_specs=in_specs, out_specs=[row(Dm), vec],
        out_shape=[jax.ShapeDtypeStruct((M, Dm), F32), jax.ShapeDtypeStruct((1, Dm), F32)],
        compiler_params=_cp("arbitrary"))(*args)


def _mm_nt(name, a, b, out_dtype, tm, tn, dep=None):
    M, K = a.shape
    N = b.shape[0]
    tm, tn = min(tm, M), min(tn, N)

    def body(a_ref, b_ref, *rest):
        o_ref = rest[-1]
        o_ref[...] = _nt(a_ref[...], b_ref[...]).astype(o_ref.dtype)

    in_specs, args = _with_dep(
        [pl.BlockSpec((tm, K), lambda i, j: (i, 0)), pl.BlockSpec((tn, K), lambda i, j: (j, 0))], [a, b], dep)
    return pl.pallas_call(
        body, name=name, grid=(M // tm, N // tn), in_specs=in_specs,
        out_specs=pl.BlockSpec((tm, tn), lambda i, j: (i, j)),
        out_shape=jax.ShapeDtypeStruct((M, N), out_dtype),
        compiler_params=_cp("parallel", "parallel"))(*args)


def _mm_tn(name, a, b, tm, tn, tk, col_major_tiles=False, col_groups=None):
    T, M = a.shape
    N = b.shape[1]
    tm, tn, tk = min(tm, M), min(tn, N), min(tk, T)
    nk = T // tk

    def body(a_ref, b_ref, o_ref, acc_ref):
        k = pl.program_id(2)

        @pl.when(k == 0)
        def _():
            acc_ref[...] = jnp.zeros_like(acc_ref)
        acc_ref[...] += _tn(a_ref[...], b_ref[...])

        @pl.when(k == nk - 1)
        def _():
            if col_groups:
                for j in range(col_groups[0]):
                    o_ref[j] = acc_ref[:, j * col_groups[1]:(j + 1) * col_groups[1]].astype(BF16)
            else:
                o_ref[...] = acc_ref[...].astype(BF16).reshape(o_ref.shape)

    if col_groups:
        assert tm == M and tn == N and col_groups[0] * col_groups[1] <= N
        out_spec = pl.BlockSpec((col_groups[0], M, col_groups[1]), lambda i, j, k: (0, 0, 0))
        out_shape = jax.ShapeDtypeStruct((col_groups[0], M, col_groups[1]), BF16)
    elif col_major_tiles:
        assert tm == M
        out_spec = pl.BlockSpec((1, tm, tn), lambda i, j, k: (j, 0, 0))
        out_shape = jax.ShapeDtypeStruct((N // tn, M, tn), BF16)
    else:
        out_spec = pl.BlockSpec((tm, tn), lambda i, j, k: (i, j))
        out_shape = jax.ShapeDtypeStruct((M, N), BF16)
    return pl.pallas_call(
        body, name=name, grid=(M // tm, N // tn, nk),
        in_specs=[pl.BlockSpec((tk, tm), lambda i, j, k: (k, i)), pl.BlockSpec((tk, tn), lambda i, j, k: (k, j))],
        out_specs=out_spec, out_shape=out_shape, scratch_shapes=[pltpu.VMEM((tm, tn), F32)],
        compiler_params=_cp("parallel", "parallel", "arbitrary"))(a, b)


def _halo_prev_spec(rb, width):
    return pl.BlockSpec((8, width), lambda i: (jnp.maximum(i * (rb // 8) - 1, 0), 0))


def _halo_next_spec(rb, width, T):
    return pl.BlockSpec((8, width), lambda i: (jnp.minimum((i + 1) * (rb // 8), T // 8 - 1), 0))


LANES = 128
FF_STRIPS = D_FF // LANES
ROW_CHUNK = 32


def _strip(j, base=0):
    return pl.ds(pl.multiple_of(base + j * LANES, LANES), LANES)


def _up_proj_act(h, w_up, w, b, rb=256):
    T, K = h.shape
    W = w_up.shape[1]
    rb = min(rb, T)
    nb = T // rb

    def body(h_ref, wup_ref, w_ref, b_ref, up_ref, act_ref, prev_scr, ext_scr, tail_scr):
        @pl.when(pl.program_id(0) == 0)
        def _():
            tail_scr[...] = jnp.zeros_like(tail_scr)
            prev_scr[...] = jnp.zeros_like(prev_scr)
        up_ref[...] = jnp.dot(h_ref[...], wup_ref[...], preferred_element_type=F32)
        for j in range(FF_STRIPS):
            slot = j % 2
            halves = (slice(j * LANES, (j + 1) * LANES), slice(D_FF + j * LANES, D_FF + (j + 1) * LANES))
            wv = [w_ref[:, cols] for cols in halves]
            bv = [b_ref[:, cols] for cols in halves]
            for hh, cols in enumerate(halves):
                ext_scr[slot, hh, 0:8] = tail_scr[:, cols]
                ext_scr[slot, hh, 8:] = prev_scr[:, cols]
            for r0 in range(0, rb, ROW_CHUNK):
                n = min(ROW_CHUNK, rb - r0)
                c = [ext_scr[slot, hh, 6 + r0:6 + r0 + n] * wv[hh][0:1] + ext_scr[slot, hh, 7 + r0:7 + r0 + n] * wv[hh][1:2]
                     + ext_scr[slot, hh, 8 + r0:8 + r0 + n] * wv[hh][2:3] + bv[hh] for hh in range(2)]
                act_ref[r0:r0 + n, halves[0]] = (_silu(c[0]) * c[1]).astype(BF16)
        tail_scr[...] = prev_scr[rb - 8:rb, :]
        prev_scr[...] = up_ref[...]

    cur = lambda i: (jnp.minimum(i, nb - 1), 0)
    return pl.pallas_call(
        body, name="up_proj_act", grid=(nb + 1,),
        in_specs=[pl.BlockSpec((rb, K), cur), pl.BlockSpec((K, W), lambda i: (0, 0)),
                  pl.BlockSpec((3, W), lambda i: (0, 0)), pl.BlockSpec((1, W), lambda i: (0, 0))],
        out_specs=[pl.BlockSpec((rb, W), cur), pl.BlockSpec((rb, D_FF), lambda i: (jnp.maximum(i - 1, 0), 0))],
        out_shape=[jax.ShapeDtypeStruct((T, W), F32), jax.ShapeDtypeStruct((T, D_FF), BF16)],
        scratch_shapes=[pltpu.VMEM((rb, W), F32), pltpu.VMEM((2, 2, rb + 8, LANES), F32), pltpu.VMEM((8, W), F32)],
        compiler_params=_cp("arbitrary"))(h, w_up, w, b)


def _ffn_act_bwd(up, dact, w, b, rb=256, dep=None):
    T, W = up.shape
    rb = min(rb, T)
    nb = T // rb
    re = rb + 8

    def body(up_ref, prev_ref, next_ref, da_ref, danext_ref, w_ref, b_ref, *rest):
        dup_ref, gw_ref, gb_ref, ext_scr, dc_scr = rest[-5:]
        i = pl.program_id(0)

        @pl.when(i == 0)
        def _():
            gw_ref[...] = jnp.zeros_like(gw_ref)
            gb_ref[...] = jnp.zeros_like(gb_ref)
        last = i == nb - 1

        def fold8(a):
            return jnp.sum(a.reshape(a.shape[0] // 8, 8, LANES), axis=0)

        def strip(j, slot):
            halves = (_strip(j), _strip(j, D_FF))
            wv = [w_ref[:, cols] for cols in halves]
            bv = [b_ref[:, cols] for cols in halves]
            for h, cols in enumerate(halves):
                ext_scr[slot, h,0:8] = jnp.where(i > 0, prev_ref[:, cols], 0.0)
                ext_scr[slot, h,8:8 + rb] = up_ref[:, cols]
                ext_scr[slot, h,8 + rb:] = next_ref[:, cols]
            gb = [jnp.zeros((8, LANES), F32) for _ in range(2)]
            gw = [[jnp.zeros((8, LANES), F32) for _ in range(3)] for _ in range(2)]
            for r0 in range(0, re, ROW_CHUNK):
                n = min(ROW_CHUNK, re - r0)
                tp = [[ext_scr[slot, h,6 + k + r0:6 + k + r0 + n] for k in range(3)] for h in range(2)]
                c = [tp[h][0] * wv[h][0:1] + tp[h][1] * wv[h][1:2] + tp[h][2] * wv[h][2:3] + bv[h] for h in range(2)]
                if r0 < rb:
                    da = da_ref[r0:r0 + n, halves[0]]
                else:
                    da = jnp.where(last, 0.0, danext_ref[:, halves[0]])
                s = _sigmoid(c[0])
                gs = c[0] * s
                dcs = (da * c[1] * (s + gs * (1.0 - s)), da * gs)
                for h in range(2):
                    dc_scr[slot, h,r0:r0 + n] = dcs[h]
                    if r0 < rb:
                        gb[h] = gb[h] + fold8(dcs[h])
                        for k in range(3):
                            gw[h][k] = gw[h][k] + fold8(tp[h][k] * dcs[h])
            for r0 in range(0, rb, ROW_CHUNK):
                n = min(ROW_CHUNK, rb - r0)
                for h, cols in enumerate(halves):
                    dup = (dc_scr[slot, h,r0:r0 + n] * wv[h][2:3] + dc_scr[slot, h,r0 + 1:r0 + 1 + n] * wv[h][1:2]
                           + dc_scr[slot, h,r0 + 2:r0 + 2 + n] * wv[h][0:1])
                    dup_ref[r0:r0 + n, cols] = dup.astype(BF16)
            for h, cols in enumerate(halves):
                gb_ref[:, cols] += jnp.sum(gb[h], axis=0, keepdims=True)
                for k in range(3):
                    gw_ref[k:k + 1, cols] += jnp.sum(gw[h][k], axis=0, keepdims=True)

        def pair(jj, carry):
            strip(2 * jj, 0)
            strip(2 * jj + 1, 1)
            return carry

        lax.fori_loop(0, FF_STRIPS // 2, pair, 0)

    in_specs, args = _with_dep(
        [pl.BlockSpec((rb, W), lambda i: (i, 0)), _halo_prev_spec(rb, W), _halo_next_spec(rb, W, T),
         pl.BlockSpec((rb, D_FF), lambda i: (i, 0)), _halo_next_spec(rb, D_FF, T),
         pl.BlockSpec((3, W), lambda i: (0, 0)), pl.BlockSpec((1, W), lambda i: (0, 0))],
        [up, up, up, dact, dact, w, b], dep)
    return pl.pallas_call(
        body, name="ffn_act_bwd", grid=(nb,), in_specs=in_specs,
        out_specs=[pl.BlockSpec((rb, W), lambda i: (i, 0)), pl.BlockSpec((3, W), lambda i: (0, 0)),
                   pl.BlockSpec((1, W), lambda i: (0, 0))],
        out_shape=[jax.ShapeDtypeStruct((T, W), BF16), jax.ShapeDtypeStruct((3, W), F32),
                   jax.ShapeDtypeStruct((1, W), F32)],
        scratch_shapes=[pltpu.VMEM((2, 2, rb + 16, LANES), F32), pltpu.VMEM((2, 2, re, LANES), F32)],
        compiler_params=_cp("arbitrary"))(*args)


def _lane_iota(shape):
    return lax.broadcasted_iota(jnp.int32, shape, len(shape) - 1)


def _dn_act_bwd(p, conv_w, alog_row, dtb_row, dq, dk, dv, dbg, dp_mid, rb=256):
    T = p.shape[0]
    rb = min(rb, T)
    nb = T // rb
    re = rb + 8
    W3 = 3 * DN_WIDTH

    def body(p_ref, prev_ref, next_ref, ba_ref, w_ref, al_ref, dt_ref, dq_ref, dqn_ref, dk_ref, dkn_ref,
             dv_ref, dvn_ref, dbg_ref, mid_ref, draw_ref, gw_ref, gad_ref, ext_scr, dc_scr):
        i = pl.program_id(0)
        draw_ref[:, W3:2 * W3] = mid_ref[...]

        @pl.when(i == 0)
        def _():
            gw_ref[...] = jnp.zeros_like(gw_ref)
            gad_ref[...] = jnp.zeros_like(gad_ref)
        row = lax.broadcasted_iota(jnp.int32, (re, 1), 0)
        live = (row < rb) | (i < nb - 1)
        d_refs = ((dq_ref, dqn_ref), (dk_ref, dkn_ref), (dv_ref, dvn_ref))
        for j in range(3 * N_HEADS):
            kind, h = divmod(j, N_HEADS)
            cols = slice(j * HEAD_DIM, (j + 1) * HEAD_DIM)
            hcols = slice(h * HEAD_DIM, (h + 1) * HEAD_DIM)
            ext_scr[j, 0:8] = jnp.where(i > 0, prev_ref[:, cols], 0.0)
            ext_scr[j, 8:8 + rb] = p_ref[:, cols]
            ext_scr[j, 8 + rb:] = next_ref[:, cols]
            tp = [ext_scr[j, 5 + k:5 + k + re] for k in range(4)]
            wv = w_ref[:, cols]
            c = tp[0] * wv[0:1] + tp[1] * wv[1:2] + tp[2] * wv[2:3] + tp[3] * wv[3:4]
            sg = _sigmoid(c)
            s = c * sg
            d_in = jnp.where(live, jnp.concatenate([d_refs[kind][0][:, hcols], d_refs[kind][1][:, hcols]], axis=0), 0.0)
            if kind < 2:
                scale = HEAD_DIM ** -0.5 if kind == 0 else 1.0
                n = lax.rsqrt(jnp.sum(s * s, axis=-1, keepdims=True) + EPS)
                hat = s * n
                d_in = (n * scale) * (d_in - hat * jnp.sum(hat * d_in, axis=-1, keepdims=True))
            dc = d_in * (sg + s * (1.0 - sg))
            dc_scr[j] = dc
            dcc = dc[0:rb]
            draw = (dcc * wv[3:4] + dc_scr[j, 1:1 + rb] * wv[2:3] + dc_scr[j, 2:2 + rb] * wv[1:2]
                    + dc_scr[j, 3:3 + rb] * wv[0:1])
            draw_ref[:, cols] = draw.astype(BF16)
            for k in range(4):
                gw_ref[k:k + 1, cols] += jnp.sum(tp[k][0:rb] * dcc, axis=0, keepdims=True)
        ba = ba_ref[...]
        dbg = dbg_ref[...]
        lane = _lane_iota(ba.shape)
        beta = _sigmoid(ba)
        ea = jnp.exp(al_ref[...])
        z = ba + dt_ref[...]
        d_a = dbg * (-ea) * _sigmoid(z)
        dba = jnp.where(lane < N_HEADS, dbg * beta * (1.0 - beta), jnp.where(lane < 2 * N_HEADS, d_a, 0.0))
        draw_ref[:, BA_COL:] = dba.astype(BF16)
        isg = (lane >= N_HEADS) & (lane < 2 * N_HEADS)
        g = -ea * _softplus(z)
        gad_ref[0:1, :] += jnp.sum(jnp.where(isg, dbg * g, 0.0), axis=0, keepdims=True)
        gad_ref[1:2, :] += jnp.sum(jnp.where(isg, d_a, 0.0), axis=0, keepdims=True)

    row512 = pl.BlockSpec((rb, DN_WIDTH), lambda i: (i, 0))
    row128 = pl.BlockSpec((rb, 128), lambda i: (i, 0))
    vec128 = pl.BlockSpec((1, 128), lambda i: (0, 0))
    next512 = _halo_next_spec(rb, DN_WIDTH, T)
    return pl.pallas_call(
        body, name="dn_act_bwd", grid=(nb,),
        in_specs=[pl.BlockSpec((rb, W3), lambda i: (i, 0)), _halo_prev_spec(rb, W3), _halo_next_spec(rb, W3, T),
                  pl.BlockSpec((rb, 128), lambda i: (i, BA_COL // 128)),
                  pl.BlockSpec((4, W3), lambda i: (0, 0)), vec128, vec128,
                  row512, next512, row512, next512, row512, next512, row128,
                  pl.BlockSpec((rb, W3), lambda i: (i, 0))],
        out_specs=[pl.BlockSpec((rb, PROJ_PAD), lambda i: (i, 0)),
                   pl.BlockSpec((4, W3), lambda i: (0, 0)), pl.BlockSpec((2, 128), lambda i: (0, 0))],
        out_shape=[jax.ShapeDtypeStruct((T, PROJ_PAD), BF16),
                   jax.ShapeDtypeStruct((4, W3), F32), jax.ShapeDtypeStruct((2, 128), F32)],
        scratch_shapes=[pltpu.VMEM((3 * N_HEADS, rb + 16, HEAD_DIM), F32), pltpu.VMEM((3 * N_HEADS, re, HEAD_DIM), F32)],
        compiler_params=_cp("arbitrary"))(p, p, p, p, conv_w, alog_row, dtb_row, dq, dq, dk, dk, dv, dv, dbg, dp_mid)


def _tri(incl):
    ii = lax.broadcasted_iota(jnp.int32, (CHUNK, CHUNK), 0)
    jj = lax.broadcasted_iota(jnp.int32, (CHUNK, CHUNK), 1)
    return ii, jj, ((ii >= jj) if incl else (ii > jj))


def _dn_chunk(k, bg, cb=4):
    T = k.shape[0]
    N = T // CHUNK
    cb = min(cb, N)

    def body(k_ref, bg_ref, gc_ref, gct_ref, l_ref):
        ii, jj, incl = _tri(True)
        tri = incl.astype(F32)
        U = range(cb)
        bgv = [bg_ref[u * CHUNK:(u + 1) * CHUNK, :] for u in U]
        gc = [jnp.dot(tri, bgv[u], precision=lax.Precision.HIGHEST, preferred_element_type=F32) for u in U]
        gct = [gc[u].T for u in U]
        kk = [[None] * N_HEADS for _ in U]
        for u in U:
            gc_ref[u * CHUNK:(u + 1) * CHUNK, :] = gc[u]
            gct_ref[u] = gct[u][0:8]
            for h in range(N_HEADS):
                kh = k_ref[u * CHUNK:(u + 1) * CHUNK, h * HEAD_DIM:(h + 1) * HEAD_DIM]
                kk[u][h] = _nt(kh * bgv[u][:, h:h + 1], kh)
        for u in U:
            for h in range(N_HEADS):
                gcol = gc[u][:, N_HEADS + h:N_HEADS + h + 1]
                grow = gct[u][N_HEADS + h:N_HEADS + h + 1, :]
                l_ref[u, h] = kk[u][h] * jnp.exp(jnp.where(ii > jj, gcol - grow, NEG))

    rows = cb * CHUNK
    return pl.pallas_call(
        body, name="dn_chunk", grid=(N // cb,),
        in_specs=[pl.BlockSpec((rows, DN_WIDTH), lambda n: (n, 0)), pl.BlockSpec((rows, 128), lambda n: (n, 0))],
        out_specs=[pl.BlockSpec((rows, 128), lambda n: (n, 0)), pl.BlockSpec((cb, 8, CHUNK), lambda n: (n, 0, 0)),
                   pl.BlockSpec((cb, N_HEADS, CHUNK, CHUNK), lambda n: (n, 0, 0, 0))],
        out_shape=[jax.ShapeDtypeStruct((T, 128), F32), jax.ShapeDtypeStruct((N, 8, CHUNK), F32),
                   jax.ShapeDtypeStruct((N, N_HEADS, CHUNK, CHUNK), F32)],
        compiler_params=_cp("parallel"))(k, bg)


def _tri_inv(lt):
    S = lt.shape[1]

    def body(l_ref, a_ref):
        sub = lax.broadcasted_iota(jnp.int32, (8, S), 0)
        groups = CHUNK // 8
        for i in range(CHUNK):
            acc = [((sub + 8 * k) == i).astype(F32) for k in range(groups)]
            for jb in range((i + 7) // 8):
                nk = jb + 1

                def step(j, carry, nk=nk, i=i):
                    lrow = l_ref[pl.ds(i * CHUNK + j, 1), :]
                    return tuple(carry[k] - lrow * a_ref[j, 8 * k:8 * k + 8, :] for k in range(nk))

                acc[:nk] = list(lax.fori_loop(8 * jb, min(8 * jb + 8, i), step, tuple(acc[:nk])))
            for k in range(groups):
                a_ref[i, 8 * k:8 * k + 8, :] = acc[k]

    return pl.pallas_call(
        body, name="tri_inv", out_shape=jax.ShapeDtypeStruct((CHUNK, CHUNK, S), F32),
        compiler_params=pltpu.CompilerParams(vmem_limit_bytes=VMEM_LIMIT))(lt)


def _dn_head_terms(qh, kh, vh, beta, gcol, grow):
    ii, jj, incl = _tri(True)
    gam = jnp.exp(jnp.where(incl, gcol - grow, NEG))
    glast = grow[:, CHUNK - 1:CHUNK]
    cd = jnp.exp(glast)
    shape = (CHUNK, HEAD_DIM)
    E = jnp.broadcast_to(jnp.exp(gcol), shape)
    Fd = jnp.broadcast_to(jnp.exp(glast - gcol), shape)
    beta = jnp.broadcast_to(beta, shape)
    kb = kh * beta
    return dict(ii=ii, jj=jj, gam=gam, E=E, F=Fd, beta=beta, cd=cd, kb=kb, vb=vh * beta, W=kb * E, qE=qh * E,
                kt=kh * Fd)


def _apply_a(a, u):
    hi, lo = _split(a)
    ub = _bf(u)
    return jnp.dot(hi, ub, preferred_element_type=F32) + jnp.dot(lo, ub, preferred_element_type=F32)


def _dn_scan(q, k, v, bg, gc, gct, a):
    T = q.shape[0]
    N = T // CHUNK
    cb = min(SCAN_CHUNKS, N)

    def body(q_ref, k_ref, v_ref, bg_ref, gc_ref, gct_ref, a_ref, o_ref, sall_ref, s_ref):
        @pl.when(pl.program_id(0) == 0)
        def _():
            s_ref[...] = jnp.zeros_like(s_ref)
        H = range(N_HEADS)
        sl = [slice(h * HEAD_DIM, (h + 1) * HEAD_DIM) for h in H]
        pre = []
        for u in range(cb):
            r = slice(u * CHUNK, (u + 1) * CHUNK)
            bgv, gcv, gctv = bg_ref[r, :], gc_ref[r, :], gct_ref[u]
            q_, k_ = [q_ref[r, s] for s in sl], [k_ref[r, s] for s in sl]
            t = [_dn_head_terms(q_[h], k_[h], v_ref[r, sl[h]], bgv[:, h:h + 1],
                                gcv[:, N_HEADS + h:N_HEADS + h + 1], gctv[N_HEADS + h:N_HEADS + h + 1, :]) for h in H]
            P = [_nt(q_[h], k_[h]) * t[h]["gam"] for h in H]
            pre.append((r, t, P))
        S = [s_ref[h] for h in H]
        for u in range(cb):
            r, t, P = pre[u]
            for h in H:
                sall_ref[u, h] = S[h]
            WS = [_nn(t[h]["W"], S[h]) for h in H]
            qS = [_nn(t[h]["qE"], S[h]) for h in H]
            vn = [_apply_a(a_ref[u, h], t[h]["vb"] - WS[h]) for h in H]
            Pv = [_nn(P[h], vn[h]) for h in H]
            kv = [_tn(t[h]["kt"], vn[h]) for h in H]
            for h in H:
                o_ref[r, sl[h]] = qS[h] + Pv[h]
            S = [t[h]["cd"] * S[h] + kv[h] for h in H]
        for h in H:
            s_ref[h] = S[h]

    row512 = pl.BlockSpec((cb * CHUNK, DN_WIDTH), lambda n: (n, 0))
    row128 = pl.BlockSpec((cb * CHUNK, 128), lambda n: (n, 0))
    return pl.pallas_call(
        body, name="dn_scan", grid=(N // cb,),
        in_specs=[row512, row512, row512, row128, row128, pl.BlockSpec((cb, 8, CHUNK), lambda n: (n, 0, 0)),
                  pl.BlockSpec((cb, N_HEADS, CHUNK, CHUNK), lambda n: (n, 0, 0, 0))],
        out_specs=[row512, pl.BlockSpec((cb, N_HEADS, HEAD_DIM, HEAD_DIM), lambda n: (n, 0, 0, 0))],
        out_shape=[jax.ShapeDtypeStruct((T, DN_WIDTH), F32),
                   jax.ShapeDtypeStruct((N, N_HEADS, HEAD_DIM, HEAD_DIM), F32)],
        scratch_shapes=[pltpu.VMEM((N_HEADS, HEAD_DIM, HEAD_DIM), F32)],
        compiler_params=_cp("arbitrary"))(q, k, v, bg, gc, gct, a)


def _dn_scan_bwd(q, k, v, bg, gc, gct, a, a_t, sall, do, dep=None):
    T = q.shape[0]
    N = T // CHUNK

    cb = min(SCAN_CHUNKS, N)
    nb = N // cb

    def body(q_ref, k_ref, v_ref, bg_ref, gc_ref, gct_ref, a_ref, at_ref, sall_ref, do_ref, *rest):
        dq_ref, dk_ref, dv_ref, dbg_ref, ds_ref = rest[-5:]
        @pl.when(pl.program_id(0) == 0)
        def _():
            ds_ref[...] = jnp.zeros_like(ds_ref)
        lane = _lane_iota((CHUNK, 128))
        rowi = lax.broadcasted_iota(jnp.int32, (CHUNK, 1), 0)
        ii, jj, _ = _tri(True)
        rev = (jj >= ii).astype(F32)
        H = range(N_HEADS)
        sl = [slice(h * HEAD_DIM, (h + 1) * HEAD_DIM) for h in H]
        pre = {}
        for u in reversed(range(cb)):
            r = slice(u * CHUNK, (u + 1) * CHUNK)
            bgv, gcv, gctv = bg_ref[r, :], gc_ref[r, :], gct_ref[u]
            q_, k_, v_ = [q_ref[r, s] for s in sl], [k_ref[r, s] for s in sl], [v_ref[r, s] for s in sl]
            dO = [do_ref[r, s] for s in sl]
            t = [_dn_head_terms(q_[h], k_[h], v_[h], bgv[:, h:h + 1], gcv[:, N_HEADS + h:N_HEADS + h + 1],
                                gctv[N_HEADS + h:N_HEADS + h + 1, :]) for h in H]
            beta = [t[h]["beta"] for h in H]
            S = [sall_ref[u, h] for h in H]
            A = [a_ref[u, h] for h in H]
            WS = [_nn(t[h]["W"], S[h]) for h in H]
            KK = [_nt(t[h]["kb"], k_[h]) for h in H]
            QK = [_nt(q_[h], k_[h]) for h in H]
            d_qE = [_nt(dO[h], S[h]) for h in H]
            vn = [_apply_a(A[h], t[h]["vb"] - WS[h]) for h in H]
            PtdO = [_tn(QK[h] * t[h]["gam"], dO[h]) for h in H]
            qEdO = [_tn(t[h]["qE"], dO[h]) for h in H]
            dOvn = [_nt(dO[h], vn[h]) for h in H]
            dQK = [jnp.where(ii >= jj, dOvn[h], 0.0) * t[h]["gam"] for h in H]
            dQKk = [_nn(dQK[h], k_[h]) for h in H]
            dQKq = [_tn(dQK[h], q_[h]) for h in H]
            pre[u] = (r, q_, k_, v_, beta, t, S, A, KK, QK, d_qE, vn, PtdO, qEdO, dQK, dQKk, dQKq)
        dSn = [ds_ref[h] for h in H]
        for u in reversed(range(cb)):
            r, q_, k_, v_, beta, t, S, A, KK, QK, d_qE, vn, PtdO, qEdO, dQK, dQKk, dQKq = pre[u]
            gam, E, Fd, cd, kb = ([t[h][n] for h in H] for n in ("gam", "E", "F", "cd", "kb"))
            ktdS = [_nn(t[h]["kt"], dSn[h]) for h in H]
            dU = [_apply_a(at_ref[u, h], PtdO[h] + ktdS[h]) for h in H]
            d_kt = [_nt(vn[h], dSn[h]) for h in H]
            dUvn = [_nt(dU[h], vn[h]) for h in H]
            dUS = [_nt(dU[h], S[h]) for h in H]
            WdU = [_tn(t[h]["W"], dU[h]) for h in H]
            d_cd = [jnp.sum(S[h] * dSn[h]) for h in H]
            dSn = [cd[h] * dSn[h] + qEdO[h] - WdU[h] for h in H]
            dKK = [jnp.where(ii > jj, -dUvn[h], 0.0) * gam[h] for h in H]
            dKKk = [_nn(dKK[h], k_[h]) for h in H]
            dKKkb = [_tn(dKK[h], kb[h]) for h in H]
            dbeta_arr = jnp.zeros((CHUNK, 128), F32)
            dgc_arr = jnp.zeros((CHUNK, 128), F32)
            for h in H:
                dW = -dUS[h]
                dq_ref[r, sl[h]] = dQKk[h] + d_qE[h] * E[h]
                d_kb = dKKk[h] + dW * E[h]
                dk_ref[r, sl[h]] = dQKq[h] + dKKkb[h] + d_kb * beta[h] + d_kt[h] * Fd[h]
                dv_ref[r, sl[h]] = dU[h] * beta[h]
                Z = dQK[h] * QK[h] + dKK[h] * KK[h]
                dbeta = jnp.sum(dU[h] * v_[h] + d_kb * k_[h], axis=-1, keepdims=True)
                m_e = (dW * kb[h] + d_qE[h] * q_[h]) * E[h]
                m_f = d_kt[h] * k_[h] * Fd[h]
                zdiag = jnp.where(ii == jj, jnp.sum(Z, axis=0, keepdims=True), 0.0)
                dgc = (jnp.sum(m_e - m_f, axis=-1, keepdims=True) + jnp.sum(Z - zdiag, axis=-1, keepdims=True)
                       + jnp.where(rowi == CHUNK - 1, jnp.sum(m_f) + d_cd[h] * cd[h], 0.0))
                dbeta_arr = dbeta_arr + jnp.where(lane == h, dbeta, 0.0)
                dgc_arr = dgc_arr + jnp.where(lane == N_HEADS + h, dgc, 0.0)
            dbg_ref[r, :] = dbeta_arr + jnp.dot(rev, dgc_arr, precision=lax.Precision.HIGHEST,
                                                preferred_element_type=F32)
        for h in H:
            ds_ref[h] = dSn[h]

    row512 = pl.BlockSpec((cb * CHUNK, DN_WIDTH), lambda n: (nb - 1 - n, 0))
    row128 = pl.BlockSpec((cb * CHUNK, 128), lambda n: (nb - 1 - n, 0))
    in_specs, args = _with_dep(
        [row512, row512, row512, row128, row128,
         pl.BlockSpec((cb, 8, CHUNK), lambda n: (nb - 1 - n, 0, 0)),
         pl.BlockSpec((cb, N_HEADS, CHUNK, CHUNK), lambda n: (nb - 1 - n, 0, 0, 0)),
         pl.BlockSpec((cb, N_HEADS, CHUNK, CHUNK), lambda n: (nb - 1 - n, 0, 0, 0)),
         pl.BlockSpec((cb, N_HEADS, HEAD_DIM, HEAD_DIM), lambda n: (nb - 1 - n, 0, 0, 0)), row512],
        [q, k, v, bg, gc, gct, a, a_t, sall, do], dep)
    return pl.pallas_call(
        body, name="dn_scan_bwd", grid=(nb,), in_specs=in_specs,
        out_specs=[row512, row512, row512, row128],
        out_shape=[jax.ShapeDtypeStruct((T, DN_WIDTH), F32)] * 3 + [jax.ShapeDtypeStruct((T, 128), F32)],
        scratch_shapes=[pltpu.VMEM((N_HEADS, HEAD_DIM, HEAD_DIM), F32)],
        compiler_params=_cp("arbitrary"))(*args)


MIX_BLOCKS_FWD = 4
MIX_BLOCKS_BWD = 2


def _sg_mask():
    ii = lax.broadcasted_iota(jnp.int32, (SG_BLOCK, SG_BLOCK), 0) // CHUNK
    jj = lax.broadcasted_iota(jnp.int32, (SG_BLOCK, SG_BLOCK), 1) // CHUNK
    return jj <= ii


def _mix_fwd(o, p, ong, sgn, sgw, sgbt):
    T = o.shape[0]
    rb = min(MIX_BLOCKS_FWD * SG_BLOCK, T)

    def body(o_ref, gate_ref, u_ref, vg_ref, ong_ref, sgn_ref, sgw_ref, sgbt_ref, mix_ref):
        mask = _sg_mask()
        for u0 in range(0, rb, SG_BLOCK):
            rows = slice(u0, u0 + SG_BLOCK)
            for h in range(N_HEADS):
                sl = slice(h * HEAD_DIM, (h + 1) * HEAD_DIM)
                oh = o_ref[rows, sl]
                r = lax.rsqrt(jnp.mean(oh * oh, axis=-1, keepdims=True) + EPS)
                mix_ref[rows, sl] = (oh * r * ong_ref[...] * _silu(gate_ref[rows, sl])).astype(BF16)
            for gi in range(SG_GROUPS):
                sl = slice(gi * SG_BLOCK, (gi + 1) * SG_BLOCK)
                gv = _gelu(vg_ref[rows, sl])
                r = lax.rsqrt(jnp.mean(gv * gv, axis=-1, keepdims=True) + EPS)
                vh = gv * r * sgn_ref[:, sl]
                s = _nn(jnp.where(mask, sgw_ref[gi], 0.0), vh) + sgbt_ref[:, gi:gi + 1]
                mix_ref[rows, DN_WIDTH + gi * SG_BLOCK:DN_WIDTH + (gi + 1) * SG_BLOCK] = (
                    _gelu(u_ref[rows, sl]) * s).astype(BF16)

    def col(c):
        return pl.BlockSpec((rb, 512), lambda i: (i, c))
    return pl.pallas_call(
        body, name="mix_fwd", grid=(T // rb,),
        in_specs=[pl.BlockSpec((rb, DN_WIDTH), lambda i: (i, 0)), col(3), col(4), col(5),
                  pl.BlockSpec((1, 128), lambda i: (0, 0)), pl.BlockSpec((1, SG_WIDTH), lambda i: (0, 0)),
                  pl.BlockSpec((SG_GROUPS, SG_BLOCK, SG_BLOCK), lambda i: (0, 0, 0)),
                  pl.BlockSpec((SG_BLOCK, 128), lambda i: (0, 0))],
        out_specs=pl.BlockSpec((rb, D_MODEL), lambda i: (i, 0)),
        out_shape=jax.ShapeDtypeStruct((T, D_MODEL), BF16),
        compiler_params=_cp("parallel"))(o, p, p, p, ong, sgn, sgw, sgbt)


def _mix_bwd(o, p, ong, sgn, sgw, sgbt, dx2, w_out, dep=None):
    T = o.shape[0]
    rb = min(MIX_BLOCKS_BWD * SG_BLOCK, T)
    nb = T // rb

    def body(o_ref, gate_ref, u_ref, vg_ref, ong_ref, sgn_ref, sgw_ref, sgbt_ref, dx2_ref, wout_ref, *rest):
        do_ref, dp_ref, gong_ref, gsgn_ref, gsgw_ref, gsgbt_ref, dmix_ref = rest[-7:]
        @pl.when(pl.program_id(0) == 0)
        def _():
            gong_ref[...] = jnp.zeros_like(gong_ref)
            gsgn_ref[...] = jnp.zeros_like(gsgn_ref)
            gsgw_ref[...] = jnp.zeros_like(gsgw_ref)
            gsgbt_ref[...] = jnp.zeros_like(gsgbt_ref)
            dmix_ref[...] = jnp.zeros_like(dmix_ref)
        dmix_next = _nt(dx2_ref[...], wout_ref[...])
        mask = _sg_mask()
        lane = _lane_iota((SG_BLOCK, 128))
        for u0 in range(0, rb, SG_BLOCK):
            rows = slice(u0, u0 + SG_BLOCK)
            for h in range(N_HEADS):
                sl = slice(h * HEAD_DIM, (h + 1) * HEAD_DIM)
                oh = o_ref[rows, sl]
                dm = dmix_ref[rows, sl]
                r = lax.rsqrt(jnp.mean(oh * oh, axis=-1, keepdims=True) + EPS)
                oh_hat = oh * r
                gt = gate_ref[rows, sl]
                sg = _silu(gt)
                dp_ref[rows, sl] = (dm * oh_hat * ong_ref[...] * _dsilu(gt)).astype(BF16)
                dn_ = dm * sg
                gong_ref[...] += jnp.sum(dn_ * oh_hat, axis=0, keepdims=True)
                dhat = dn_ * ong_ref[...]
                do_ref[rows, sl] = r * (dhat - oh_hat * jnp.mean(dhat * oh_hat, axis=-1, keepdims=True))
            for gi in range(SG_GROUPS):
                sl = slice(gi * SG_BLOCK, (gi + 1) * SG_BLOCK)
                vraw = vg_ref[rows, sl]
                gv = _gelu(vraw)
                r = lax.rsqrt(jnp.mean(gv * gv, axis=-1, keepdims=True) + EPS)
                vhat = gv * r
                vn = vhat * sgn_ref[:, sl]
                wm = jnp.where(mask, sgw_ref[gi], 0.0)
                s = _nn(wm, vn) + sgbt_ref[:, gi:gi + 1]
                uraw = u_ref[rows, sl]
                dm = dmix_ref[rows, DN_WIDTH + gi * SG_BLOCK:DN_WIDTH + (gi + 1) * SG_BLOCK]
                dp_ref[rows, DN_WIDTH + gi * SG_BLOCK:DN_WIDTH + (gi + 1) * SG_BLOCK] = (
                    dm * s * _dgelu(uraw)).astype(BF16)
                ds = dm * _gelu(uraw)
                gsgbt_ref[...] += jnp.where(lane == gi, jnp.sum(ds, axis=-1, keepdims=True), 0.0)
                gsgw_ref[gi] += jnp.where(mask, _nt(ds, vn), 0.0)
                dvn = _tn(wm, ds)
                gsgn_ref[:, sl] += jnp.sum(dvn * vhat, axis=0, keepdims=True)
                dhat = dvn * sgn_ref[:, sl]
                dgv = r * (dhat - vhat * jnp.mean(dhat * vhat, axis=-1, keepdims=True))
                dp_ref[rows, 2 * DN_WIDTH + gi * SG_BLOCK:2 * DN_WIDTH + (gi + 1) * SG_BLOCK] = (
                    dgv * _dgelu(vraw)).astype(BF16)
        dmix_ref[...] = dmix_next

    prev = lambda i: jnp.maximum(i - 1, 0)

    def col(c):
        return pl.BlockSpec((rb, 512), lambda i: (prev(i), c))
    full = lambda *s: pl.BlockSpec(s, lambda i: (0,) * len(s))
    in_specs, args = _with_dep(
        [pl.BlockSpec((rb, DN_WIDTH), lambda i: (prev(i), 0)), col(3), col(4), col(5),
         full(1, 128), full(1, SG_WIDTH), full(SG_GROUPS, SG_BLOCK, SG_BLOCK), full(SG_BLOCK, 128),
         pl.BlockSpec((rb, D_MODEL), lambda i: (jnp.minimum(i, nb - 1), 0)), full(D_MODEL, D_MODEL)],
        [o, p, p, p, ong, sgn, sgw, sgbt, dx2, w_out], dep)
    return pl.pallas_call(
        body, name="mix_bwd", grid=(nb + 1,), in_specs=in_specs,
        out_specs=[pl.BlockSpec((rb, DN_WIDTH), lambda i: (prev(i), 0)),
                   pl.BlockSpec((rb, 3 * 512), lambda i: (prev(i), 0)),
                   full(1, 128), full(1, SG_WIDTH), full(SG_GROUPS, SG_BLOCK, SG_BLOCK), full(SG_BLOCK, 128)],
        out_shape=[jax.ShapeDtypeStruct((T, DN_WIDTH), F32), jax.ShapeDtypeStruct((T, 3 * 512), BF16),
                   jax.ShapeDtypeStruct((1, 128), F32), jax.ShapeDtypeStruct((1, SG_WIDTH), F32),
                   jax.ShapeDtypeStruct((SG_GROUPS, SG_BLOCK, SG_BLOCK), F32),
                   jax.ShapeDtypeStruct((SG_BLOCK, 128), F32)],
        scratch_shapes=[pltpu.VMEM((rb, D_MODEL), F32)],
        compiler_params=_cp("arbitrary"))(*args)


def _pad_lanes(row, offset=0):
    n = row.shape[1]
    return jnp.pad(row, ((0, 0), (offset, 128 - n - offset)))


def _local_step(x, tgt, w, dep=None, late_weights=None, on_grad=None):
    T = x.shape[0]
    N = T // CHUNK
    on_grad = on_grad or (lambda name, g: None)
    alog_row = _pad_lanes(w["dn_a_log"], N_HEADS)
    dtb_row = _pad_lanes(w["dn_dt_bias"], N_HEADS)
    sgbt = jnp.pad(w["sg_b"].T, ((0, 0), (0, 128 - SG_GROUPS)))

    p, h1, w_in_pad, q, k, v, bg = _in_proj_act(x, w["attn_norm_g"], w["w_in"], w["dn_conv_w"], alog_row, dtb_row,
                                                dep=dep)
    gc, gct, lmat = _dn_chunk(k, bg)
    lt = lmat.reshape(N * N_HEADS, CHUNK * CHUNK).T
    at = _tri_inv(lt)
    a = at.reshape(CHUNK * CHUNK, N * N_HEADS).T.reshape(N, N_HEADS, CHUNK, CHUNK)
    a_t = at.transpose(1, 0, 2).reshape(CHUNK * CHUNK, N * N_HEADS).T.reshape(N, N_HEADS, CHUNK, CHUNK)
    o, sall = _dn_scan(q, k, v, bg, gc, gct, a)
    mix = _mix_fwd(o, p, w["dn_out_norm_g"], w["sg_norm_g"], w["sg_w"], sgbt)
    if late_weights is not None:
        w = {**w, **late_weights("out_proj", mix)}
    x2, h2 = _out_proj(mix, w["w_out"], x, w["ffn_norm_g"])
    up, act = _up_proj_act(h2, w["w_up"], w["ffn_conv_w"], w["ffn_conv_b"])
    if late_weights is not None:
        w = {**w, **late_weights("down_proj", act)}
    loss, dx3, g_final = _down_proj_loss(act, w["w_down"], x2, tgt, w["final_norm_g"])

    dact = _mm_nt("d_act", dx3, w["w_down"], F32, 512, D_FF)
    g_w_down = _mm_tn("g_w_down", act, dx3, D_FF, 1024, 1024)
    tok = on_grad("w_down", g_w_down)
    dup, g_ffn_conv_w, g_ffn_conv_b = _ffn_act_bwd(up, dact, w["ffn_conv_w"], w["ffn_conv_b"], dep=tok)
    g_w_up = _mm_tn("g_w_up", h2, dup, 1024, 2 * D_FF // 4, 2048, col_major_tiles=True)
    tok = on_grad("w_up", g_w_up)
    dx2, g_ffn_norm = _mm_nt_rms_bwd("d_h2", dup, w["w_up"], x2, w["ffn_norm_g"], dx3, dep=tok)
    g_w_out = _mm_tn("g_w_out", mix, dx2, 1024, 1024, 1024)
    tok = on_grad("w_out", g_w_out)
    do, dp_mid, g_ong, g_sgn, g_sgw, g_sgbt = _mix_bwd(o, p, w["dn_out_norm_g"], w["sg_norm_g"], w["sg_w"], sgbt,
                                                      dx2, w["w_out"], dep=tok)
    early = dict(dn_out_norm_g=g_ong, sg_norm_g=g_sgn, sg_w=g_sgw, sg_bt=g_sgbt,
                 ffn_norm_g=g_ffn_norm, ffn_conv_w=g_ffn_conv_w, ffn_conv_b=g_ffn_conv_b, final_norm_g=g_final)
    tok = on_grad("small_early", early)
    dq, dk, dv, dbg = _dn_scan_bwd(q, k, v, bg, gc, gct, a, a_t, sall, do, dep=tok)
    dp, g_dn_conv_w, g_ad = _dn_act_bwd(p, w["dn_conv_w"], alog_row, dtb_row, dq, dk, dv, dbg, dp_mid)
    g_w_in = _mm_tn("g_w_in", h1, dp, 1024, PROJ_PAD, 1024, col_groups=(4, PROJ_COLS // 4))
    tok = on_grad("w_in", g_w_in)
    grad_x, g_attn_norm = _mm_nt_rms_bwd("d_h1", dp, w_in_pad, x, w["attn_norm_g"], dx2, dep=tok)

    grads = dict(attn_norm_g=g_attn_norm, w_in=g_w_in, dn_conv_w=g_dn_conv_w, a_dt=g_ad,
                 w_out=g_w_out, w_up=g_w_up, w_down=g_w_down, **early)
    return loss, grad_x, grads


def _me():
    return lax.axis_index("x"), lax.axis_index("y"), lax.axis_index("c")


def _peer(rel):
    x, y, c = _me()
    return {"x": (1 - x, y, c), "y": (x, 1 - y, c), "xy": (1 - x, 1 - y, c), "c": (x, y, 1 - c)}[rel]


def _chip_of(dev):
    return 2 * dev[0] + dev[1]


CHIP_RELS = ("x", "y", "xy")


def _run_copies(copies, sends, recvs):
    for cp in copies:
        cp.start()
    for cp in recvs:
        cp.wait_recv()
    for cp in sends:
        cp.wait_send()


def _gather_first(w_shard, small_shard):
    R = w_shard.shape[0]
    r2 = R // 2

    def body(w_ref, s_ref, w_out, s_out, send_sems, recv_sems):
        x, y, c = _me()
        me = _chip_of((x, y))
        sib = _peer("c")

        def half(chip, core):
            return w_out.at[chip, pl.ds(pl.multiple_of(core * r2, 8), r2), :]

        def copy(k, src, dst, to):
            return pltpu.make_async_remote_copy(src_ref=src, dst_ref=dst, send_sem=send_sems.at[k],
                                                recv_sem=recv_sems.at[k], device_id=to, device_id_type=MESH)

        own_rows = w_ref.at[pl.ds(pl.multiple_of(c * r2, 8), r2), :]
        first = [copy(r, own_rows, half(me, c), _peer(rel)) for r, rel in enumerate(CHIP_RELS)]
        first += [copy(3 + r, s_ref, s_out.at[me], _peer(rel)) for r, rel in enumerate(CHIP_RELS)]
        for cp in first:
            cp.start()
        passed = []
        for r, rel in enumerate(CHIP_RELS):
            their = _chip_of(_peer(rel))
            copy(r, own_rows, half(their, c), _peer(rel)).wait_recv()
            fwd = copy(6 + r, half(their, c), half(their, c), sib)
            fwd.start()
            passed.append(fwd)
        for r, rel in enumerate(CHIP_RELS):
            their = _chip_of(_peer(rel))
            copy(3 + r, s_ref, s_out.at[their], _peer(rel)).wait_recv()
            copy(6 + r, own_rows, half(their, 1 - c), sib).wait_recv()
        for cp in first + passed:
            cp.wait_send()

    w_all, s_all = pl.pallas_call(
        body, name="gather_first", in_specs=[ANY, ANY], out_specs=[ANY, ANY],
        out_shape=[jax.ShapeDtypeStruct((4,) + w_shard.shape, w_shard.dtype),
                   jax.ShapeDtypeStruct((4,) + small_shard.shape, small_shard.dtype)],
        scratch_shapes=[pltpu.SemaphoreType.DMA((9,)), pltpu.SemaphoreType.DMA((9,))])(w_shard, small_shard)
    me = _chip_of(_me())
    return (lax.dynamic_update_index_in_dim(w_all, w_shard, me, 0),
            lax.dynamic_update_index_in_dim(s_all, small_shard, me, 0))


OTHERS = tuple((fx, fy, fc) for fx in (0, 1) for fy in (0, 1) for fc in (0, 1) if (fx, fy, fc) != (0, 0, 0))


def _other(flip):
    x, y, c = _me()
    return (x ^ flip[0], y ^ flip[1], c ^ flip[2])


def _linear(dev):
    return 4 * dev[0] + 2 * dev[1] + dev[2]


def _exchange_small(small):
    def body(small_ref, out_ref, send_sems, recv_sems):
        my_slot = _linear(_me())
        sends, recvs = [], []
        for k, flip in enumerate(OTHERS):
            peer = _other(flip)
            sends.append(pltpu.make_async_remote_copy(
                src_ref=small_ref, dst_ref=out_ref.at[my_slot], send_sem=send_sems.at[k], recv_sem=recv_sems.at[k],
                device_id=peer, device_id_type=MESH))
            recvs.append(pltpu.make_async_remote_copy(
                src_ref=small_ref, dst_ref=out_ref.at[_linear(peer)], send_sem=send_sems.at[k],
                recv_sem=recv_sems.at[k], device_id=peer, device_id_type=MESH))
        _run_copies(sends, sends, recvs)

    out = pl.pallas_call(
        body, name="exchange_small", in_specs=[ANY], out_specs=ANY,
        out_shape=jax.ShapeDtypeStruct((8,) + small.shape, small.dtype),
        scratch_shapes=[pltpu.SemaphoreType.DMA((7,)), pltpu.SemaphoreType.DMA((7,))])(small)
    return lax.dynamic_update_index_in_dim(out, small, _linear(_me()), 0)


def _pair_swap(name, halves):
    n = len(halves)

    def body(*refs):
        src, out = refs[:n], refs[n:2 * n]
        send_sems, recv_sems = refs[2 * n:]
        sib = _peer("c")
        copies = [pltpu.make_async_remote_copy(
            src_ref=src[i], dst_ref=out[i], send_sem=send_sems.at[i], recv_sem=recv_sems.at[i],
            device_id=sib, device_id_type=MESH) for i in range(n)]
        _run_copies(copies, copies, copies)

    return pl.pallas_call(
        body, name=name, in_specs=[ANY] * n, out_specs=[ANY] * n,
        out_shape=[jax.ShapeDtypeStruct(h.shape, h.dtype) for h in halves],
        scratch_shapes=[pltpu.SemaphoreType.DMA((n,)), pltpu.SemaphoreType.DMA((n,))])(*halves)


HBM = pl.BlockSpec(memory_space=pltpu.HBM)
SEM = pl.BlockSpec(memory_space=pltpu.SEMAPHORE)
EFFECT = pltpu.SideEffectType.DATAFLOW_SIDE_EFFECTING


def _hbm(a):
    return pltpu.with_memory_space_constraint(a, pltpu.HBM)


def _transfer_start(name, srcs, lands, n_copies, make_copies, after=None):
    n, m = len(srcs), len(lands)

    def body(*refs):
        src, land = refs[:n], refs[n:n + m]
        outs = refs[n + m + (after is not None):]
        send_sems, recv_sems, token = outs[0], outs[1], outs[-1]
        for cp in make_copies(src, land, send_sems, recv_sems):
            cp.start()
        token[...] = jnp.zeros_like(token)

    arrs = list(srcs) + list(lands)
    in_specs, args = _with_dep([HBM] * (n + m), [_hbm(a) for a in arrs], after)
    out = pl.pallas_call(
        body, name=name,
        out_shape=(pltpu.SemaphoreType.DMA((n_copies,)), pltpu.SemaphoreType.DMA((n_copies,)),
                   *[pltpu.HBM(a.shape, a.dtype) for a in arrs], jax.ShapeDtypeStruct((8, 128), F32)),
        in_specs=in_specs,
        out_specs=(SEM, SEM, *[HBM] * (n + m), pl.BlockSpec(memory_space=pltpu.VMEM)),
        input_output_aliases={i: 2 + i for i in range(n + m)},
        compiler_params=pltpu.CompilerParams(has_side_effects=EFFECT))(*args)
    return out[0], out[1], list(out[2:2 + n]), list(out[2 + n:2 + n + m]), out[-1]


def _transfer_wait(name, send_sems, recv_sems, srcs, lands, make_copies, after):
    n, m = len(srcs), len(lands)

    def body(*refs):
        src, land = refs[:n], refs[n:n + m]
        s_sems, r_sems = refs[n + m], refs[n + m + 1]
        for cp in make_copies(src, land, s_sems, r_sems):
            cp.wait_send()
            cp.wait_recv()

    arrs = list(srcs) + list(lands)
    out = pl.pallas_call(
        body, name=name, out_shape=tuple(pltpu.HBM(a.shape, a.dtype) for a in arrs),
        in_specs=[HBM] * (n + m) + [SEM, SEM, ANY], out_specs=tuple([HBM] * (n + m)),
        input_output_aliases={i: i for i in range(n + m)},
        compiler_params=pltpu.CompilerParams(has_side_effects=EFFECT))(*arrs, send_sems, recv_sems, after)
    return list(out[:n]), list(out[n:])


def _gather_copies(src, land, send_sems, recv_sems):
    me = _chip_of(_me())
    copies = []
    for i in range(len(src)):
        for r, rel in enumerate(CHIP_RELS):
            k = 3 * i + r
            copies.append(pltpu.make_async_remote_copy(
                src_ref=src[i], dst_ref=land[i].at[me], send_sem=send_sems.at[k], recv_sem=recv_sems.at[k],
                device_id=_peer(rel), device_id_type=MESH))
    return copies


def _small_copies(src, land, send_sems, recv_sems):
    my_slot = _linear(_me())
    return [pltpu.make_async_remote_copy(
        src_ref=src[0], dst_ref=land[0].at[my_slot], send_sem=send_sems.at[k], recv_sem=recv_sems.at[k],
        device_id=_other(flip), device_id_type=MESH) for k, flip in enumerate(OTHERS)]


def _pieces_copies(src, land, send_sems, recv_sems):
    copies = []
    for k, flip in enumerate(OTHERS):
        peer = _other(flip)
        copies.append(pltpu.make_async_remote_copy(
            src_ref=src[0].at[_linear(peer)], dst_ref=land[0].at[k], send_sem=send_sems.at[k],
            recv_sem=recv_sems.at[k], device_id=peer, device_id_type=MESH))
    return copies


def _row_block(rows, cols, budget=2 * 1024 * 1024):
    rb = max(8, (budget // (4 * cols)) // 8 * 8)
    while rows % rb:
        rb -= 8
    return rb if rb > 0 else rows


def _sum_pieces(name, mine, slot, rest):
    _, R, Cc = mine.shape
    K = rest.shape[0]
    rb = _row_block(R, Cc)

    def body(s_ref, f_ref, r_ref, o_ref):
        acc = f_ref[0].astype(F32)
        for j in range(K):
            acc = acc + r_ref[j].astype(F32)
        o_ref[...] = acc

    return pl.pallas_call(
        body, name=name,
        grid_spec=pltpu.PrefetchScalarGridSpec(
            num_scalar_prefetch=1, grid=(R // rb,),
            in_specs=[pl.BlockSpec((1, rb, Cc), lambda i, s: (s[0], i, 0)),
                      pl.BlockSpec((K, rb, Cc), lambda i, s: (0, i, 0))],
            out_specs=pl.BlockSpec((rb, Cc), lambda i, s: (i, 0))),
        out_shape=jax.ShapeDtypeStruct((R, Cc), F32), compiler_params=_cp("parallel"))(slot, mine, rest)


def _adamw_math(w, gv, m, v):
    mn = ADAM_B1 * m + (1.0 - ADAM_B1) * gv
    vn = ADAM_B2 * v + (1.0 - ADAM_B2) * (gv * gv)
    m_hat = mn / (1.0 - ADAM_B1 ** ADAM_STEP)
    v_hat = vn / (1.0 - ADAM_B2 ** ADAM_STEP)
    return -ADAM_LR * (m_hat / (jnp.sqrt(v_hat) + ADAM_EPS) + ADAM_WD * w), mn, vn


def _adamw_halves(name, w, mine, theirs, m, v, core):
    R, Cc = w.shape
    r2 = R // 2
    rb = _row_block(r2, Cc, 1024 * 1024)
    nb2 = r2 // rb

    def body(c_ref, w_ref, mine_ref, theirs_ref, m_ref, v_ref, g_ref, d_ref, mo_ref, vo_ref):
        is_mine = (pl.program_id(0) // nb2) == c_ref[0]
        gv = jnp.where(is_mine, mine_ref[...], theirs_ref[...])
        g_ref[...] = gv
        d_ref[...], mo_ref[...], vo_ref[...] = _adamw_math(w_ref[...], gv, m_ref[...], v_ref[...])

    blk = pl.BlockSpec((rb, Cc), lambda i, c: (i, 0))
    half = lambda own: pl.BlockSpec(
        (rb, Cc), lambda i, c: (jnp.clip(i - (c[0] if own else 1 - c[0]) * nb2, 0, nb2 - 1), 0))
    return pl.pallas_call(
        body, name=name,
        grid_spec=pltpu.PrefetchScalarGridSpec(
            num_scalar_prefetch=1, grid=(2 * nb2,), in_specs=[blk, half(True), half(False), blk, blk],
            out_specs=[blk] * 4),
        out_shape=[jax.ShapeDtypeStruct((R, Cc), F32)] * 4, compiler_params=_cp("parallel"))(core, w, mine, theirs, m, v)


def _adamw_transposed(name, wt, mine, theirs, mt, vt, core):
    Cc, kh_n, _ = wt.shape
    r2 = mine.shape[0]
    per_half = kh_n // 2
    nb = -(-Cc // LANES)

    def body(c_ref, w_ref, mine_ref, theirs_ref, m_ref, v_ref, g_ref, d_ref, mo_ref, vo_ref):
        first = c_ref[0] == 0
        halves = (jnp.where(first, mine_ref[...], theirs_ref[...]).T,
                  jnp.where(first, theirs_ref[...], mine_ref[...]).T)
        for kh in range(kh_n):
            lo = (kh % per_half) * LANES
            g_ref[:, kh, :] = halves[kh // per_half][:, lo:lo + LANES]
        d_ref[...], mo_ref[...], vo_ref[...] = _adamw_math(w_ref[...], g_ref[...], m_ref[...], v_ref[...])

    blk = pl.BlockSpec((LANES, kh_n, LANES), lambda i, c: (i, 0, 0))
    half = pl.BlockSpec((r2, LANES), lambda i, c: (0, i))
    return pl.pallas_call(
        body, name=name,
        grid_spec=pltpu.PrefetchScalarGridSpec(
            num_scalar_prefetch=1, grid=(nb,), in_specs=[blk, half, half, blk, blk], out_specs=[blk] * 4),
        out_shape=[jax.ShapeDtypeStruct(wt.shape, F32)] * 4, compiler_params=_cp("parallel"))(
            core, wt, mine, theirs, mt, vt)


FF_W = 2 * D_FF
FF_CH = FF_W // LANES
DNC_W = 3 * DN_WIDTH
DNC_CH = DNC_W // LANES
E_ONG, E_SGN, E_SGW, E_SGBT = 0, 1, 8, 8 + SG_GROUPS * SG_BLOCK
E_FFN = E_SGBT + SG_BLOCK
E_FCW = E_FFN + D_MODEL // LANES
E_FCB = E_FCW + 3 * FF_CH
E_FIN = E_FCB + FF_CH
EARLY_ROWS = E_FIN + D_MODEL // LANES
L_ATTN, L_DNC = 0, D_MODEL // LANES
L_AD = L_DNC + 4 * DNC_CH
L_LOSS = L_AD + 2
LATE_ROWS = -(-(L_LOSS + 1) // 8) * 8


def _put_rows(out, r0, x):
    k, width = x.shape
    n = width // LANES
    for t in range(k):
        for j in range(n):
            out[r0 + t * n + j:r0 + t * n + j + 1, :] = x[t:t + 1, j * LANES:(j + 1) * LANES]


def _pack_early(ong, sgn, sgw, sgbt, ffn, fcw, fcb, fin):
    def body(ong_ref, sgn_ref, sgw_ref, sgbt_ref, ffn_ref, fcw_ref, fcb_ref, fin_ref, out):
        out[...] = jnp.zeros_like(out)
        _put_rows(out, E_ONG, ong_ref)
        _put_rows(out, E_SGN, sgn_ref)
        for gi in range(SG_GROUPS):
            out[E_SGW + gi * SG_BLOCK:E_SGW + (gi + 1) * SG_BLOCK, :] = sgw_ref[gi]
        out[E_SGBT:E_SGBT + SG_BLOCK, :] = sgbt_ref[...]
        _put_rows(out, E_FFN, ffn_ref)
        _put_rows(out, E_FCW, fcw_ref)
        _put_rows(out, E_FCB, fcb_ref)
        _put_rows(out, E_FIN, fin_ref)

    return pl.pallas_call(body, name="pack_small_early", out_shape=jax.ShapeDtypeStruct((EARLY_ROWS, LANES), F32))(
        ong, sgn, sgw, sgbt, ffn, fcw, fcb, fin)


def _pack_late(attn, dnc, ad, loss_row):
    def body(attn_ref, dnc_ref, ad_ref, loss_ref, out):
        out[...] = jnp.zeros_like(out)
        _put_rows(out, L_ATTN, attn_ref)
        _put_rows(out, L_DNC, dnc_ref)
        out[L_AD:L_AD + 2, :] = ad_ref[...]
        out[L_LOSS:L_LOSS + 1, :] = loss_ref[...]

    return pl.pallas_call(body, name="pack_small_late", out_shape=jax.ShapeDtypeStruct((LATE_ROWS, LANES), F32))(
        attn, dnc, ad, loss_row)


SMALL = ("attn_norm_g", "dn_a_log", "dn_dt_bias", "dn_out_norm_g", "sg_norm_g", "sg_w", "sg_b", "ffn_norm_g",
         "ffn_conv_b", "final_norm_g", "dn_conv_w", "ffn_conv_w")


def _small_update(early_all, late_all, chip, W, M, V):
    n = len(SMALL)
    arrs = [d[k] for d in (W, M, V) for k in SMALL]

    def body(c_ref, e_ref, l_ref, *refs):
        w_, m_, v_ = refs[:n], refs[n:2 * n], refs[2 * n:3 * n]
        loss_ref = refs[3 * n]
        outs = refs[3 * n + 1:]
        g_, d_, mo_, vo_ = outs[:n], outs[n:2 * n], outs[2 * n:3 * n], outs[3 * n:4 * n]
        chip_i = c_ref[0]

        def total(ref, r0, rows=1):
            acc = ref[0, pl.ds(r0, rows), :]
            for s in range(1, 8):
                acc = acc + ref[s, pl.ds(r0, rows), :]
            return acc

        def update(i, idx, g):
            g_[i][idx] = g
            d_[i][idx], mo_[i][idx], vo_[i][idx] = _adamw_math(w_[i][idx], g, m_[i][idx], v_[i][idx])

        def rows_param(name, ref, r0, width):
            i = SMALL.index(name)
            for j in range(width // LANES):
                update(i, (slice(None), slice(j * LANES, (j + 1) * LANES)), total(ref, r0 + j))

        rows_param("attn_norm_g", l_ref, L_ATTN, D_MODEL)
        ad = (total(l_ref, L_AD), total(l_ref, L_AD + 1))
        update(SMALL.index("dn_a_log"), (slice(None), slice(None)), ad[0][:, N_HEADS:2 * N_HEADS])
        update(SMALL.index("dn_dt_bias"), (slice(None), slice(None)), ad[1][:, N_HEADS:2 * N_HEADS])
        rows_param("dn_out_norm_g", e_ref, E_ONG, HEAD_DIM)
        rows_param("sg_norm_g", e_ref, E_SGN, SG_WIDTH)
        sgbt = total(e_ref, E_SGBT, SG_BLOCK).T
        for gi in range(SG_GROUPS):
            update(SMALL.index("sg_w"), (0, gi), total(e_ref, E_SGW + gi * SG_BLOCK, SG_BLOCK))
            update(SMALL.index("sg_b"), (0, slice(gi, gi + 1), slice(None)), sgbt[gi:gi + 1, :])
        rows_param("ffn_norm_g", e_ref, E_FFN, D_MODEL)
        rows_param("ffn_conv_b", e_ref, E_FCB, FF_W)
        rows_param("final_norm_g", e_ref, E_FIN, D_MODEL)
        for name, ref, r0, taps, chunks in (("dn_conv_w", l_ref, L_DNC, 4, DNC_CH), ("ffn_conv_w", e_ref, E_FCW, 3, FF_CH)):
            mine = chunks // 4
            for t in range(taps):
                for j in range(mine):
                    update(SMALL.index(name), (0, slice(t, t + 1), slice(j * LANES, (j + 1) * LANES)),
                           total(ref, r0 + t * chunks + chip_i * mine + j))
        loss_ref[...] = total(l_ref, L_LOSS)

    full = lambda a: pl.BlockSpec(a.shape, lambda i, c, nd=a.ndim: (0,) * nd)
    shapes = [jax.ShapeDtypeStruct(W[k].shape, F32) for k in SMALL]
    outs = pl.pallas_call(
        body, name="small_update",
        grid_spec=pltpu.PrefetchScalarGridSpec(
            num_scalar_prefetch=1, grid=(1,), in_specs=[full(early_all), full(late_all)] + [full(a) for a in arrs],
            out_specs=[pl.BlockSpec((1, LANES), lambda i, c: (0, 0))] + [full(s) for s in shapes] * 4),
        out_shape=[jax.ShapeDtypeStruct((1, LANES), F32)] + shapes * 4,
        compiler_params=pltpu.CompilerParams(vmem_limit_bytes=VMEM_LIMIT))(chip, early_all, late_all, *arrs)
    loss, outs = outs[0], outs[1:]
    return (loss,) + tuple(dict(zip(SMALL, outs[k * n:(k + 1) * n])) for k in range(4))


ORDER =("attn_norm_g", "w_in", "dn_conv_w", "dn_a_log", "dn_dt_bias", "dn_out_norm_g", "sg_norm_g", "sg_w",
         "sg_b", "w_out", "ffn_norm_g", "w_up", "ffn_conv_w", "ffn_conv_b", "w_down", "final_norm_g")


def kernel(x, attn_norm_g, w_in, dn_conv_w, dn_a_log, dn_dt_bias, dn_out_norm_g, sg_norm_g, sg_w, sg_b, w_out, ffn_norm_g, w_up, ffn_conv_w, ffn_conv_b, w_down, final_norm_g, loss_target, m_attn_norm_g, m_w_in, m_dn_conv_w, m_dn_a_log, m_dn_dt_bias, m_dn_out_norm_g, m_sg_norm_g, m_sg_w, m_sg_b, m_w_out, m_ffn_norm_g, m_w_up, m_ffn_conv_w, m_ffn_conv_b, m_w_down, m_final_norm_g, v_attn_norm_g, v_w_in, v_dn_conv_w, v_dn_a_log, v_dn_dt_bias, v_dn_out_norm_g, v_sg_norm_g, v_sg_w, v_sg_b, v_w_out, v_ffn_norm_g, v_w_up, v_ffn_conv_w, v_ffn_conv_b, v_w_down, v_final_norm_g):
    W = dict(attn_norm_g=attn_norm_g, w_in=w_in, dn_conv_w=dn_conv_w, dn_a_log=dn_a_log, dn_dt_bias=dn_dt_bias,
             dn_out_norm_g=dn_out_norm_g, sg_norm_g=sg_norm_g, sg_w=sg_w, sg_b=sg_b, w_out=w_out,
             ffn_norm_g=ffn_norm_g, w_up=w_up, ffn_conv_w=ffn_conv_w, ffn_conv_b=ffn_conv_b, w_down=w_down,
             final_norm_g=final_norm_g)
    Mo = dict(attn_norm_g=m_attn_norm_g, w_in=m_w_in, dn_conv_w=m_dn_conv_w, dn_a_log=m_dn_a_log,
              dn_dt_bias=m_dn_dt_bias, dn_out_norm_g=m_dn_out_norm_g, sg_norm_g=m_sg_norm_g, sg_w=m_sg_w,
              sg_b=m_sg_b, w_out=m_w_out, ffn_norm_g=m_ffn_norm_g, w_up=m_w_up, ffn_conv_w=m_ffn_conv_w,
              ffn_conv_b=m_ffn_conv_b, w_down=m_w_down, final_norm_g=m_final_norm_g)
    Vo = dict(attn_norm_g=v_attn_norm_g, w_in=v_w_in, dn_conv_w=v_dn_conv_w, dn_a_log=v_dn_a_log,
              dn_dt_bias=v_dn_dt_bias, dn_out_norm_g=v_dn_out_norm_g, sg_norm_g=v_sg_norm_g, sg_w=v_sg_w,
              sg_b=v_sg_b, w_out=v_w_out, ffn_norm_g=v_ffn_norm_g, w_up=v_w_up, ffn_conv_w=v_ffn_conv_w,
              ffn_conv_b=v_ffn_conv_b, w_down=v_w_down, final_norm_g=v_final_norm_g)
    xi, yi, ci = lax.axis_index("x"), lax.axis_index("y"), lax.axis_index("c")
    chip = 2 * xi + yi

    me_lin = 4 * xi + 2 * yi + ci

    g_in, g_dnc = _gather_first(w_in[0].astype(BF16), dn_conv_w[0])
    def start_gather(name, shards, after):
        lands = [lax.dynamic_update_index_in_dim(lax.empty((4,) + s.shape, s.dtype), s, chip, 0) for s in shards]
        return _transfer_start(name, shards, lands, 3 * len(shards), _gather_copies, after=after)

    mid = start_gather("gather_mid_start", [w_out[0].astype(BF16), w_up[0].astype(BF16), ffn_conv_w[0]], g_in)
    last = start_gather("gather_last_start", [w_down[0].astype(BF16)], mid[4])
    token = last[4]

    def late_weights(stage, after):
        if stage == "out_proj":
            _, (g_out, g_up, g_ffc) = _transfer_wait("gather_mid_wait", *mid[:4], _gather_copies, after)
            return dict(w_out=g_out.reshape(D_MODEL, D_MODEL), ffn_conv_w=g_ffc.transpose(1, 0, 2).reshape(3, 2 * D_FF),
                        w_up=g_up.transpose(1, 0, 2).reshape(D_MODEL, 2 * D_FF))
        _, (g_down,) = _transfer_wait("gather_last_wait", *last[:4], _gather_copies, after)
        return dict(w_down=g_down.reshape(D_FF, D_MODEL))

    full = dict(
        w_in=g_in,
        dn_conv_w=g_dnc.transpose(1, 0, 2).reshape(4, 3 * DN_WIDTH),
        attn_norm_g=attn_norm_g, dn_a_log=dn_a_log, dn_dt_bias=dn_dt_bias, dn_out_norm_g=dn_out_norm_g,
        sg_norm_g=sg_norm_g, sg_w=sg_w[0], sg_b=sg_b[0], ffn_norm_g=ffn_norm_g, ffn_conv_b=ffn_conv_b,
        final_norm_g=final_norm_g[None])

    pending = {}

    def on_grad(name, gw):
        if name == "small_early":
            buf = _pack_early(gw["dn_out_norm_g"], gw["sg_norm_g"], gw["sg_w"], gw["sg_bt"], gw["ffn_norm_g"],
                              gw["ffn_conv_w"], gw["ffn_conv_b"], gw["final_norm_g"])
            land = lax.dynamic_update_index_in_dim(lax.empty((8,) + buf.shape, F32), buf, me_lin, 0)
            s_sem, r_sem, src, lands, tok = _transfer_start("small_early_start", [buf], [land], 7, _small_copies)
            pending[name] = (s_sem, r_sem, src, lands)
            return tok
        g8 = gw.reshape(8, -1, gw.shape[-1])
        land = lax.empty((7,) + g8.shape[1:], BF16)
        s_sem, r_sem, src, lands, tok = _transfer_start(f"reduce_{name}_start", [g8], [land], 7, _pieces_copies)
        pending[name] = (s_sem, r_sem, src, lands)
        return tok

    loss_row, grad_x, g = _local_step(x[0], loss_target[0], full, dep=token, late_weights=late_weights,
                                      on_grad=on_grad)

    late_all = _exchange_small(_pack_late(g["attn_norm_g"], g["dn_conv_w"], g["a_dt"], loss_row))
    s_sem, r_sem, src, lands = pending["small_early"]
    _, (early_all,) = _transfer_wait("small_early_wait", s_sem, r_sem, src, lands, _small_copies, grad_x)
    row = lambda d: {k: (d[k].reshape(1, -1) if k == "final_norm_g" else d[k]) for k in SMALL}
    loss_sum, *small_out = _small_update(early_all, late_all, chip.astype(jnp.int32).reshape(1), row(W), row(Mo), row(Vo))
    loss = loss_sum[0, 0]

    def summed_half(n, after):
        s_sem, r_sem, src, lands = pending[n]
        sent, got = _transfer_wait(f"reduce_{n}_wait", s_sem, r_sem, src, lands, _pieces_copies, after)
        return _sum_pieces(f"sum_{n}", sent[0], me_lin.astype(jnp.int32).reshape(1), got[0])

    first3 = ("w_down", "w_up", "w_out")
    halves = [summed_half(n, grad_x) for n in first3]
    theirs = _pair_swap("pair_swap", halves)
    core = ci.astype(jnp.int32).reshape(1)
    grads, delta, new_m, new_v = {}, {}, {}, {}
    for n, mine_h, their_h in zip(first3, halves, theirs):
        shp = W[n].shape
        gr, d, mn, vn = _adamw_halves(f"adamw_{n}", W[n][0], mine_h, their_h, Mo[n][0], Vo[n][0], core)
        grads[n], delta[n], new_m[n], new_v[n] = gr.reshape(shp), d.reshape(shp), mn.reshape(shp), vn.reshape(shp)
    mine_h = summed_half("w_in", delta["w_out"])
    (their_h,) = _pair_swap("pair_swap_w_in", [mine_h])
    shp = w_in.shape
    to_t = lambda a: a.reshape(shp[1] // LANES, LANES, shp[2]).transpose(2, 0, 1)
    from_t = lambda a: a.transpose(1, 2, 0).reshape(shp)
    outs = _adamw_transposed("adamw_w_in", to_t(w_in), mine_h, their_h, to_t(m_w_in), to_t(v_w_in), core)
    grads["w_in"], delta["w_in"], new_m["w_in"], new_v["w_in"] = (from_t(o) for o in outs)
    for dst, src_d in zip((grads, delta, new_m, new_v), small_out):
        dst.update({k: (a.reshape(W[k].shape) if k == "final_norm_g" else a) for k, a in src_d.items()})

    return (loss, grad_x[None], *[grads[n] for n in ORDER], *[delta[n] for n in ORDER],
            *[new_m[n] for n in ORDER], *[new_v[n] for n in ORDER])
```

```python
import math

import jax
import jax.numpy as jnp
from jax import lax
from jax.experimental import pallas as pl
from jax.experimental.pallas import tpu as pltpu

F32 = jnp.float32
BF16 = jnp.bfloat16

D_MODEL = 1024
CHUNK = 64
SCAN_CHUNKS = 8
HEAD_DIM = 128
N_HEADS = 4
DN_WIDTH = 512
SG_WIDTH = 512
SG_GROUPS = 4
SG_BLOCK = 128
D_FF = 2816
PROJ_COLS = 3080
PROJ_PAD = 3200
BA_COL = 3072
EPS = 1e-6
NEG = -1e30
VMEM_LIMIT = 56 * 1024 * 1024

ADAM_LR = 0.001
ADAM_B1 = 0.9
ADAM_B2 = 0.999
ADAM_EPS = 1e-08
ADAM_WD = 0.01
ADAM_STEP = 10

MESH = pl.DeviceIdType.MESH
ANY = pl.BlockSpec(memory_space=pl.ANY)


def _cp(*sem):
    return pltpu.CompilerParams(dimension_semantics=sem, vmem_limit_bytes=VMEM_LIMIT)


def _bf(a):
    return a.astype(BF16)


def _nn(a, b):
    return jnp.dot(_bf(a), _bf(b), preferred_element_type=F32)


def _nt(a, b):
    return lax.dot_general(_bf(a), _bf(b), (((1,), (1,)), ((), ())), preferred_element_type=F32)


def _tn(a, b):
    return lax.dot_general(_bf(a), _bf(b), (((0,), (0,)), ((), ())), preferred_element_type=F32)


def _split(a):
    hi = _bf(a)
    return hi, _bf(a - hi.astype(F32))


def _sigmoid(x):
    return 0.5 * jnp.tanh(0.5 * x) + 0.5


def _silu(x):
    return x * _sigmoid(x)


def _dsilu(x):
    s = _sigmoid(x)
    return s * (1.0 + x * (1.0 - s))


_GELU_C = math.sqrt(2.0 / math.pi)
_GELU_A = 0.044715


def _gelu(x):
    return 0.5 * x * (1.0 + jnp.tanh(_GELU_C * (x + _GELU_A * x * x * x)))


def _dgelu(x):
    t = jnp.tanh(_GELU_C * (x + _GELU_A * x * x * x))
    return 0.5 * (1.0 + t) + 0.5 * x * (1.0 - t * t) * _GELU_C * (1.0 + 3.0 * _GELU_A * x * x)


def _softplus(x):
    return jnp.maximum(x, 0.0) + jnp.log(1.0 + jnp.exp(-jnp.abs(x)))


def _with_dep(in_specs, args, dep):
    if dep is None:
        return in_specs, args
    return in_specs + [ANY], args + [dep]


SUB_ROWS = 128


def _sub_blocks(tm):
    return [slice(r0, min(r0 + SUB_ROWS, tm)) for r0 in range(0, tm, SUB_ROWS)]


def _rms_hat(xv):
    r = lax.rsqrt(jnp.mean(xv * xv, axis=-1, keepdims=True) + EPS)
    return xv * r, r


def _rms_bwd_vals(dh, xh, r, g):
    dxh = dh * g
    return r * (dxh - xh * jnp.mean(dxh * xh, axis=-1, keepdims=True)), jnp.sum(dh * xh, axis=0, keepdims=True)


def _in_proj_act(x, g, w4, conv_w, alog_row, dtb_row, tm=256, dep=None):
    T, K = x.shape
    ng, _, wc = w4.shape
    tm = min(tm, T)
    nb = T // tm
    W3 = 3 * DN_WIDTH

    def body(x_ref, g_ref, w4_ref, cw_ref, al_ref, dt_ref, *rest):
        p_ref, h_ref, w_ref, q_ref, k_ref, v_ref, bg_ref, prev_scr, ext_scr, tail_scr = rest[-10:]

        @pl.when(pl.program_id(0) == 0)
        def _():
            w_ref[:, ng * wc:] = jnp.zeros((K, PROJ_PAD - ng * wc), BF16)
            for j in range(ng):
                w_ref[:, j * wc:(j + 1) * wc] = w4_ref[j]
            tail_scr[...] = jnp.zeros_like(tail_scr)
            prev_scr[...] = jnp.zeros_like(prev_scr)
        for r in _sub_blocks(tm):
            xh, _ = _rms_hat(x_ref[r, :])
            h_ref[r, :] = (xh * g_ref[...]).astype(BF16)
        p_ref[...] = jnp.dot(h_ref[...], w_ref[...], preferred_element_type=F32)
        outs = (q_ref, k_ref, v_ref)
        for j in range(3 * N_HEADS):
            kind, hd = divmod(j, N_HEADS)
            cols = slice(j * HEAD_DIM, (j + 1) * HEAD_DIM)
            cur = prev_scr[:, cols]
            ext_scr[j, 0:8] = tail_scr[:, cols]
            ext_scr[j, 8:] = cur
            wv = cw_ref[:, cols]
            s = _silu(ext_scr[j, 5:5 + tm] * wv[0:1] + ext_scr[j, 6:6 + tm] * wv[1:2]
                      + ext_scr[j, 7:7 + tm] * wv[2:3] + cur * wv[3:4])
            if kind < 2:
                scale = HEAD_DIM ** -0.5 if kind == 0 else 1.0
                s = s * (lax.rsqrt(jnp.sum(s * s, axis=-1, keepdims=True) + EPS) * scale)
            outs[kind][:, hd * HEAD_DIM:(hd + 1) * HEAD_DIM] = s
        ba = prev_scr[:, W3:]
        lane = _lane_iota(ba.shape)
        beta = _sigmoid(ba)
        gl = -jnp.exp(al_ref[...]) * _softplus(ba + dt_ref[...])
        bg_ref[...] = jnp.where(lane < N_HEADS, beta, jnp.where(lane < 2 * N_HEADS, gl, 0.0))
        tail_scr[...] = prev_scr[tm - 8:tm, 0:W3]
        prev_scr[:, 0:W3] = p_ref[:, 0:W3]
        prev_scr[:, W3:] = p_ref[:, BA_COL:]

    cur_blk = lambda i: (jnp.minimum(i, nb - 1), 0)
    prev_blk = lambda i: (jnp.maximum(i - 1, 0), 0)
    vec128 = pl.BlockSpec((1, 128), lambda i: (0, 0))
    in_specs, args = _with_dep(
        [pl.BlockSpec((tm, K), cur_blk), pl.BlockSpec((1, K), lambda i: (0, 0)),
         pl.BlockSpec((ng, K, wc), lambda i: (0, 0, 0)), pl.BlockSpec((4, W3), lambda i: (0, 0)), vec128, vec128],
        [x, g, w4, conv_w, alog_row, dtb_row], dep)
    row512 = pl.BlockSpec((tm, DN_WIDTH), prev_blk)
    return pl.pallas_call(
        body, name="in_proj_act", grid=(nb + 1,), in_specs=in_specs,
        out_specs=[pl.BlockSpec((tm, PROJ_PAD), cur_blk), pl.BlockSpec((tm, K), cur_blk),
                   pl.BlockSpec((K, PROJ_PAD), lambda i: (0, 0)), row512, row512, row512,
                   pl.BlockSpec((tm, 128), prev_blk)],
        out_shape=[jax.ShapeDtypeStruct((T, PROJ_PAD), F32), jax.ShapeDtypeStruct((T, K), BF16),
                   jax.ShapeDtypeStruct((K, PROJ_PAD), BF16)] + [jax.ShapeDtypeStruct((T, DN_WIDTH), F32)] * 3
        + [jax.ShapeDtypeStruct((T, 128), F32)],
        scratch_shapes=[pltpu.VMEM((tm, W3 + 128), F32), pltpu.VMEM((3 * N_HEADS, tm + 8, HEAD_DIM), F32),
                        pltpu.VMEM((8, W3), F32)],
        compiler_params=_cp("arbitrary"))(*args)


def _down_proj_loss(act, w, x2, tgt, g, tm=512):
    T, K = act.shape
    Dm = w.shape[1]
    tm = min(tm, T)

    def body(a_ref, w_ref, x_ref, t_ref, g_ref, loss_ref, dx_ref, gg_ref):
        @pl.when(pl.program_id(0) == 0)
        def _():
            gg_ref[...] = jnp.zeros_like(gg_ref)
            loss_ref[...] = jnp.zeros_like(loss_ref)
        dx_ref[...] = _nn(a_ref[...], w_ref[...]) + x_ref[...]
        for r in _sub_blocks(tm):
            xh, rr = _rms_hat(dx_ref[r, :])
            e = xh * g_ref[...] - t_ref[r, :]
            loss_ref[...] += jnp.zeros_like(loss_ref) + (0.5 / Dm) * jnp.sum(e * e)
            dx, gg = _rms_bwd_vals(e * (1.0 / Dm), xh, rr, g_ref[...])
            dx_ref[r, :] = dx
            gg_ref[...] += gg

    row = lambda width: pl.BlockSpec((tm, width), lambda i: (i, 0))
    vec = pl.BlockSpec((1, Dm), lambda i: (0, 0))
    return pl.pallas_call(
        body, name="down_proj_loss", grid=(T // tm,),
        in_specs=[row(K), pl.BlockSpec((K, Dm), lambda i: (0, 0)), row(Dm), row(Dm), vec],
        out_specs=[pl.BlockSpec((1, 128), lambda i: (0, 0)), row(Dm), vec],
        out_shape=[jax.ShapeDtypeStruct((1, 128), F32), jax.ShapeDtypeStruct((T, Dm), F32),
                   jax.ShapeDtypeStruct((1, Dm), F32)],
        compiler_params=_cp("arbitrary"))(act, w, x2, tgt, g)


def _mm_nt_rms_bwd(name, a, b, x, g, dres, tm=512, dep=None):
    M, K = a.shape
    Dm = b.shape[0]
    tm = min(tm, M)

    def body(a_ref, b_ref, x_ref, g_ref, dres_ref, *rest):
        dx_ref, gg_ref = rest[-2:]

        @pl.when(pl.program_id(0) == 0)
        def _():
            gg_ref[...] = jnp.zeros_like(gg_ref)
        dx_ref[...] = _nt(a_ref[...], b_ref[...])
        for r in _sub_blocks(tm):
            xh, rr = _rms_hat(x_ref[r, :])
            dx, gg = _rms_bwd_vals(dx_ref[r, :], xh, rr, g_ref[...])
            dx_ref[r, :] = dres_ref[r, :] + dx
            gg_ref[...] += gg

    row = lambda width: pl.BlockSpec((tm, width), lambda i: (i, 0))
    vec = pl.BlockSpec((1, Dm), lambda i: (0, 0))
    in_specs, args = _with_dep([row(K), pl.BlockSpec((Dm, K), lambda i: (0, 0)), row(Dm), vec, row(Dm)],
                               [a, b, x, g, dres], dep)
    return pl.pallas_call(
        body, name=name, grid=(M // tm,), in_specs=in_specs, out_specs=[row(Dm), vec],
        out_shape=[jax.ShapeDtypeStruct((M, Dm), F32), jax.ShapeDtypeStruct((1, Dm), F32)],
        compiler_params=_cp("arbitrary"))(*args)


def _mm_nt(name, a, b, out_dtype, tm, tn, dep=None):
    M, K = a.shape
    N = b.shape[0]
    tm, tn = min(tm, M), min(tn, N)

    def body(a_ref, b_ref, *rest):
        o_ref = rest[-1]
        o_ref[...] = _nt(a_ref[...], b_ref[...]).astype(o_ref.dtype)

    in_specs, args = _with_dep(
        [pl.BlockSpec((tm, K), lambda i, j: (i, 0)), pl.BlockSpec((tn, K), lambda i, j: (j, 0))], [a, b], dep)
    return pl.pallas_call(
        body, name=name, grid=(M // tm, N // tn), in_specs=in_specs,
        out_specs=pl.BlockSpec((tm, tn), lambda i, j: (i, j)),
        out_shape=jax.ShapeDtypeStruct((M, N), out_dtype),
        compiler_params=_cp("parallel", "parallel"))(*args)


def _mm_tn(name, a, b, tm, tn, tk, col_major_tiles=False, col_groups=None):
    T, M = a.shape
    N = b.shape[1]
    tm, tn, tk = min(tm, M), min(tn, N), min(tk, T)
    nk = T // tk

    def body(a_ref, b_ref, o_ref, acc_ref):
        k = pl.program_id(2)

        @pl.when(k == 0)
        def _():
            acc_ref[...] = jnp.zeros_like(acc_ref)
        acc_ref[...] += _tn(a_ref[...], b_ref[...])

        @pl.when(k == nk - 1)
        def _():
            if col_groups:
                for j in range(col_groups[0]):
                    o_ref[j] = acc_ref[:, j * col_groups[1]:(j + 1) * col_groups[1]].astype(BF16)
            else:
                o_ref[...] = acc_ref[...].astype(BF16).reshape(o_ref.shape)

    if col_groups:
        assert tm == M and tn == N and col_groups[0] * col_groups[1] <= N
        out_spec = pl.BlockSpec((col_groups[0], M, col_groups[1]), lambda i, j, k: (0, 0, 0))
        out_shape = jax.ShapeDtypeStruct((col_groups[0], M, col_groups[1]), BF16)
    elif col_major_tiles:
        assert tm == M
        out_spec = pl.BlockSpec((1, tm, tn), lambda i, j, k: (j, 0, 0))
        out_shape = jax.ShapeDtypeStruct((N // tn, M, tn), BF16)
    else:
        out_spec = pl.BlockSpec((tm, tn), lambda i, j, k: (i, j))
        out_shape = jax.ShapeDtypeStruct((M, N), BF16)
    return pl.pallas_call(
        body, name=name, grid=(M // tm, N // tn, nk),
        in_specs=[pl.BlockSpec((tk, tm), lambda i, j, k: (k, i)), pl.BlockSpec((tk, tn), lambda i, j, k: (k, j))],
        out_specs=out_spec, out_shape=out_shape, scratch_shapes=[pltpu.VMEM((tm, tn), F32)],
        compiler_params=_cp("parallel", "parallel", "arbitrary"))(a, b)


def _halo_prev_spec(rb, width):
    return pl.BlockSpec((8, width), lambda i: (jnp.maximum(i * (rb // 8) - 1, 0), 0))


def _halo_next_spec(rb, width, T):
    return pl.BlockSpec((8, width), lambda i: (jnp.minimum((i + 1) * (rb // 8), T // 8 - 1), 0))


LANES = 128
FF_STRIPS = D_FF // LANES
ROW_CHUNK = 32


def _strip(j, base=0):
    return pl.ds(pl.multiple_of(base + j * LANES, LANES), LANES)


def _up_proj_act(h, w_up, w, b, rb=256):
    T, K = h.shape
    W = w_up.shape[1]
    rb = min(rb, T)
    nb = T // rb

    def body(h_ref, wup_ref, w_ref, b_ref, up_ref, act_ref, prev_scr, ext_scr, tail_scr):
        @pl.when(pl.program_id(0) == 0)
        def _():
            tail_scr[...] = jnp.zeros_like(tail_scr)
            prev_scr[...] = jnp.zeros_like(prev_scr)
        up_ref[...] = jnp.dot(h_ref[...], wup_ref[...], preferred_element_type=F32)
        for j in range(FF_STRIPS):
            slot = j % 2
            halves = (slice(j * LANES, (j + 1) * LANES), slice(D_FF + j * LANES, D_FF + (j + 1) * LANES))
            wv = [w_ref[:, cols] for cols in halves]
            bv = [b_ref[:, cols] for cols in halves]
            for hh, cols in enumerate(halves):
                ext_scr[slot, hh, 0:8] = tail_scr[:, cols]
                ext_scr[slot, hh, 8:] = prev_scr[:, cols]
            for r0 in range(0, rb, ROW_CHUNK):
                n = min(ROW_CHUNK, rb - r0)
                c = [ext_scr[slot, hh, 6 + r0:6 + r0 + n] * wv[hh][0:1] + ext_scr[slot, hh, 7 + r0:7 + r0 + n] * wv[hh][1:2]
                     + ext_scr[slot, hh, 8 + r0:8 + r0 + n] * wv[hh][2:3] + bv[hh] for hh in range(2)]
                act_ref[r0:r0 + n, halves[0]] = (_silu(c[0]) * c[1]).astype(BF16)
        tail_scr[...] = prev_scr[rb - 8:rb, :]
        prev_scr[...] = up_ref[...]

    cur = lambda i: (jnp.minimum(i, nb - 1), 0)
    return pl.pallas_call(
        body, name="up_proj_act", grid=(nb + 1,),
        in_specs=[pl.BlockSpec((rb, K), cur), pl.BlockSpec((K, W), lambda i: (0, 0)),
                  pl.BlockSpec((3, W), lambda i: (0, 0)), pl.BlockSpec((1, W), lambda i: (0, 0))],
        out_specs=[pl.BlockSpec((rb, W), cur), pl.BlockSpec((rb, D_FF), lambda i: (jnp.maximum(i - 1, 0), 0))],
        out_shape=[jax.ShapeDtypeStruct((T, W), F32), jax.ShapeDtypeStruct((T, D_FF), BF16)],
        scratch_shapes=[pltpu.VMEM((rb, W), F32), pltpu.VMEM((2, 2, rb + 8, LANES), F32), pltpu.VMEM((8, W), F32)],
        compiler_params=_cp("arbitrary"))(h, w_up, w, b)


def _ffn_act_bwd(up, dact, w, b, rb=256, dep=None):
    T, W = up.shape
    rb = min(rb, T)
    nb = T // rb
    re = rb + 8

    def body(up_ref, prev_ref, next_ref, da_ref, danext_ref, w_ref, b_ref, *rest):
        dup_ref, gw_ref, gb_ref, ext_scr, dc_scr = rest[-5:]
        i = pl.program_id(0)

        @pl.when(i == 0)
        def _():
            gw_ref[...] = jnp.zeros_like(gw_ref)
            gb_ref[...] = jnp.zeros_like(gb_ref)
        last = i == nb - 1

        def fold8(a):
            return jnp.sum(a.reshape(a.shape[0] // 8, 8, LANES), axis=0)

        def strip(j, slot):
            halves = (_strip(j), _strip(j, D_FF))
            wv = [w_ref[:, cols] for cols in halves]
            bv = [b_ref[:, cols] for cols in halves]
            for h, cols in enumerate(halves):
                ext_scr[slot, h,0:8] = jnp.where(i > 0, prev_ref[:, cols], 0.0)
                ext_scr[slot, h,8:8 + rb] = up_ref[:, cols]
                ext_scr[slot, h,8 + rb:] = next_ref[:, cols]
            gb = [jnp.zeros((8, LANES), F32) for _ in range(2)]
            gw = [[jnp.zeros((8, LANES), F32) for _ in range(3)] for _ in range(2)]
            for r0 in range(0, re, ROW_CHUNK):
                n = min(ROW_CHUNK, re - r0)
                tp = [[ext_scr[slot, h,6 + k + r0:6 + k + r0 + n] for k in range(3)] for h in range(2)]
                c = [tp[h][0] * wv[h][0:1] + tp[h][1] * wv[h][1:2] + tp[h][2] * wv[h][2:3] + bv[h] for h in range(2)]
                if r0 < rb:
                    da = da_ref[r0:r0 + n, halves[0]]
                else:
                    da = jnp.where(last, 0.0, danext_ref[:, halves[0]])
                s = _sigmoid(c[0])
                gs = c[0] * s
                dcs = (da * c[1] * (s + gs * (1.0 - s)), da * gs)
                for h in range(2):
                    dc_scr[slot, h,r0:r0 + n] = dcs[h]
                    if r0 < rb:
                        gb[h] = gb[h] + fold8(dcs[h])
                        for k in range(3):
                            gw[h][k] = gw[h][k] + fold8(tp[h][k] * dcs[h])
            for r0 in range(0, rb, ROW_CHUNK):
                n = min(ROW_CHUNK, rb - r0)
                for h, cols in enumerate(halves):
                    dup = (dc_scr[slot, h,r0:r0 + n] * wv[h][2:3] + dc_scr[slot, h,r0 + 1:r0 + 1 + n] * wv[h][1:2]
                           + dc_scr[slot, h,r0 + 2:r0 + 2 + n] * wv[h][0:1])
                    dup_ref[r0:r0 + n, cols] = dup.astype(BF16)
            for h, cols in enumerate(halves):
                gb_ref[:, cols] += jnp.sum(gb[h], axis=0, keepdims=True)
                for k in range(3):
                    gw_ref[k:k + 1, cols] += jnp.sum(gw[h][k], axis=0, keepdims=True)

        def pair(jj, carry):
            strip(2 * jj, 0)
            strip(2 * jj + 1, 1)
            return carry

        lax.fori_loop(0, FF_STRIPS // 2, pair, 0)

    in_specs, args = _with_dep(
        [pl.BlockSpec((rb, W), lambda i: (i, 0)), _halo_prev_spec(rb, W), _halo_next_spec(rb, W, T),
         pl.BlockSpec((rb, D_FF), lambda i: (i, 0)), _halo_next_spec(rb, D_FF, T),
         pl.BlockSpec((3, W), lambda i: (0, 0)), pl.BlockSpec((1, W), lambda i: (0, 0))],
        [up, up, up, dact, dact, w, b], dep)
    return pl.pallas_call(
        body, name="ffn_act_bwd", grid=(nb,), in_specs=in_specs,
        out_specs=[pl.BlockSpec((rb, W), lambda i: (i, 0)), pl.BlockSpec((3, W), lambda i: (0, 0)),
                   pl.BlockSpec((1, W), lambda i: (0, 0))],
        out_shape=[jax.ShapeDtypeStruct((T, W), BF16), jax.ShapeDtypeStruct((3, W), F32),
                   jax.ShapeDtypeStruct((1, W), F32)],
        scratch_shapes=[pltpu.VMEM((2, 2, rb + 16, LANES), F32), pltpu.VMEM((2, 2, re, LANES), F32)],
        compiler_params=_cp("arbitrary"))(*args)


def _lane_iota(shape):
    return lax.broadcasted_iota(jnp.int32, shape, len(shape) - 1)


def _dn_act_bwd(p, conv_w, alog_row, dtb_row, dq, dk, dv, dbg, dp_mid, rb=256):
    T = p.shape[0]
    rb = min(rb, T)
    nb = T // rb
    re = rb + 8
    W3 = 3 * DN_WIDTH

    def body(p_ref, prev_ref, next_ref, ba_ref, w_ref, al_ref, dt_ref, dq_ref, dqn_ref, dk_ref, dkn_ref,
             dv_ref, dvn_ref, dbg_ref, mid_ref, draw_ref, gw_ref, gad_ref, ext_scr, dc_scr):
        i = pl.program_id(0)
        draw_ref[:, W3:2 * W3] = mid_ref[...]

        @pl.when(i == 0)
        def _():
            gw_ref[...] = jnp.zeros_like(gw_ref)
            gad_ref[...] = jnp.zeros_like(gad_ref)
        row = lax.broadcasted_iota(jnp.int32, (re, 1), 0)
        live = (row < rb) | (i < nb - 1)
        d_refs = ((dq_ref, dqn_ref), (dk_ref, dkn_ref), (dv_ref, dvn_ref))
        for j in range(3 * N_HEADS):
            kind, h = divmod(j, N_HEADS)
            cols = slice(j * HEAD_DIM, (j + 1) * HEAD_DIM)
            hcols = slice(h * HEAD_DIM, (h + 1) * HEAD_DIM)
            ext_scr[j, 0:8] = jnp.where(i > 0, prev_ref[:, cols], 0.0)
            ext_scr[j, 8:8 + rb] = p_ref[:, cols]
            ext_scr[j, 8 + rb:] = next_ref[:, cols]
            tp = [ext_scr[j, 5 + k:5 + k + re] for k in range(4)]
            wv = w_ref[:, cols]
            c = tp[0] * wv[0:1] + tp[1] * wv[1:2] + tp[2] * wv[2:3] + tp[3] * wv[3:4]
            sg = _sigmoid(c)
            s = c * sg
            d_in = jnp.where(live, jnp.concatenate([d_refs[kind][0][:, hcols], d_refs[kind][1][:, hcols]], axis=0), 0.0)
            if kind < 2:
                scale = HEAD_DIM ** -0.5 if kind == 0 else 1.0
                n = lax.rsqrt(jnp.sum(s * s, axis=-1, keepdims=True) + EPS)
                hat = s * n
                d_in = (n * scale) * (d_in - hat * jnp.sum(hat * d_in, axis=-1, keepdims=True))
            dc = d_in * (sg + s * (1.0 - sg))
            dc_scr[j] = dc
            dcc = dc[0:rb]
            draw = (dcc * wv[3:4] + dc_scr[j, 1:1 + rb] * wv[2:3] + dc_scr[j, 2:2 + rb] * wv[1:2]
                    + dc_scr[j, 3:3 + rb] * wv[0:1])
            draw_ref[:, cols] = draw.astype(BF16)
            for k in range(4):
                gw_ref[k:k + 1, cols] += jnp.sum(tp[k][0:rb] * dcc, axis=0, keepdims=True)
        ba = ba_ref[...]
        dbg = dbg_ref[...]
        lane = _lane_iota(ba.shape)
        beta = _sigmoid(ba)
        ea = jnp.exp(al_ref[...])
        z = ba + dt_ref[...]
        d_a = dbg * (-ea) * _sigmoid(z)
        dba = jnp.where(lane < N_HEADS, dbg * beta * (1.0 - beta), jnp.where(lane < 2 * N_HEADS, d_a, 0.0))
        draw_ref[:, BA_COL:] = dba.astype(BF16)
        isg = (lane >= N_HEADS) & (lane < 2 * N_HEADS)
        g = -ea * _softplus(z)
        gad_ref[0:1, :] += jnp.sum(jnp.where(isg, dbg * g, 0.0), axis=0, keepdims=True)
        gad_ref[1:2, :] += jnp.sum(jnp.where(isg, d_a, 0.0), axis=0, keepdims=True)

    row512 = pl.BlockSpec((rb, DN_WIDTH), lambda i: (i, 0))
    row128 = pl.BlockSpec((rb, 128), lambda i: (i, 0))
    vec128 = pl.BlockSpec((1, 128), lambda i: (0, 0))
    next512 = _halo_next_spec(rb, DN_WIDTH, T)
    return pl.pallas_call(
        body, name="dn_act_bwd", grid=(nb,),
        in_specs=[pl.BlockSpec((rb, W3), lambda i: (i, 0)), _halo_prev_spec(rb, W3), _halo_next_spec(rb, W3, T),
                  pl.BlockSpec((rb, 128), lambda i: (i, BA_COL // 128)),
                  pl.BlockSpec((4, W3), lambda i: (0, 0)), vec128, vec128,
                  row512, next512, row512, next512, row512, next512, row128,
                  pl.BlockSpec((rb, W3), lambda i: (i, 0))],
        out_specs=[pl.BlockSpec((rb, PROJ_PAD), lambda i: (i, 0)),
                   pl.BlockSpec((4, W3), lambda i: (0, 0)), pl.BlockSpec((2, 128), lambda i: (0, 0))],
        out_shape=[jax.ShapeDtypeStruct((T, PROJ_PAD), BF16),
                   jax.ShapeDtypeStruct((4, W3), F32), jax.ShapeDtypeStruct((2, 128), F32)],
        scratch_shapes=[pltpu.VMEM((3 * N_HEADS, rb + 16, HEAD_DIM), F32), pltpu.VMEM((3 * N_HEADS, re, HEAD_DIM), F32)],
        compiler_params=_cp("arbitrary"))(p, p, p, p, conv_w, alog_row, dtb_row, dq, dq, dk, dk, dv, dv, dbg, dp_mid)


def _tri(incl):
    ii = lax.broadcasted_iota(jnp.int32, (CHUNK, CHUNK), 0)
    jj = lax.broadcasted_iota(jnp.int32, (CHUNK, CHUNK), 1)
    return ii, jj, ((ii >= jj) if incl else (ii > jj))


def _dn_chunk(k, bg, cb=4):
    T = k.shape[0]
    N = T // CHUNK
    cb = min(cb, N)

    def body(k_ref, bg_ref, gc_ref, gct_ref, l_ref):
        ii, jj, incl = _tri(True)
        tri = incl.astype(F32)
        U = range(cb)
        bgv = [bg_ref[u * CHUNK:(u + 1) * CHUNK, :] for u in U]
        gc = [jnp.dot(tri, bgv[u], precision=lax.Precision.HIGHEST, preferred_element_type=F32) for u in U]
        gct = [gc[u].T for u in U]
        kk = [[None] * N_HEADS for _ in U]
        for u in U:
            gc_ref[u * CHUNK:(u + 1) * CHUNK, :] = gc[u]
            gct_ref[u] = gct[u][0:8]
            for h in range(N_HEADS):
                kh = k_ref[u * CHUNK:(u + 1) * CHUNK, h * HEAD_DIM:(h + 1) * HEAD_DIM]
                kk[u][h] = _nt(kh * bgv[u][:, h:h + 1], kh)
        for u in U:
            for h in range(N_HEADS):
                gcol = gc[u][:, N_HEADS + h:N_HEADS + h + 1]
                grow = gct[u][N_HEADS + h:N_HEADS + h + 1, :]
                l_ref[u, h] = kk[u][h] * jnp.exp(jnp.where(ii > jj, gcol - grow, NEG))

    rows = cb * CHUNK
    return pl.pallas_call(
        body, name="dn_chunk", grid=(N // cb,),
        in_specs=[pl.BlockSpec((rows, DN_WIDTH), lambda n: (n, 0)), pl.BlockSpec((rows, 128), lambda n: (n, 0))],
        out_specs=[pl.BlockSpec((rows, 128), lambda n: (n, 0)), pl.BlockSpec((cb, 8, CHUNK), lambda n: (n, 0, 0)),
                   pl.BlockSpec((cb, N_HEADS, CHUNK, CHUNK), lambda n: (n, 0, 0, 0))],
        out_shape=[jax.ShapeDtypeStruct((T, 128), F32), jax.ShapeDtypeStruct((N, 8, CHUNK), F32),
                   jax.ShapeDtypeStruct((N, N_HEADS, CHUNK, CHUNK), F32)],
        compiler_params=_cp("parallel"))(k, bg)


def _tri_inv(lt):
    S = lt.shape[1]

    def body(l_ref, a_ref):
        sub = lax.broadcasted_iota(jnp.int32, (8, S), 0)
        groups = CHUNK // 8
        for i in range(CHUNK):
            acc = [((sub + 8 * k) == i).astype(F32) for k in range(groups)]
            for jb in range((i + 7) // 8):
                nk = jb + 1

                def step(j, carry, nk=nk, i=i):
                    lrow = l_ref[pl.ds(i * CHUNK + j, 1), :]
                    return tuple(carry[k] - lrow * a_ref[j, 8 * k:8 * k + 8, :] for k in range(nk))

                acc[:nk] = list(lax.fori_loop(8 * jb, min(8 * jb + 8, i), step, tuple(acc[:nk])))
            for k in range(groups):
                a_ref[i, 8 * k:8 * k + 8, :] = acc[k]

    return pl.pallas_call(
        body, name="tri_inv", out_shape=jax.ShapeDtypeStruct((CHUNK, CHUNK, S), F32),
        compiler_params=pltpu.CompilerParams(vmem_limit_bytes=VMEM_LIMIT))(lt)


def _dn_head_terms(qh, kh, vh, beta, gcol, grow):
    ii, jj, incl = _tri(True)
    gam = jnp.exp(jnp.where(incl, gcol - grow, NEG))
    glast = grow[:, CHUNK - 1:CHUNK]
    cd = jnp.exp(glast)
    shape = (CHUNK, HEAD_DIM)
    E = jnp.broadcast_to(jnp.exp(gcol), shape)
    Fd = jnp.broadcast_to(jnp.exp(glast - gcol), shape)
    beta = jnp.broadcast_to(beta, shape)
    kb = kh * beta
    return dict(ii=ii, jj=jj, gam=gam, E=E, F=Fd, beta=beta, cd=cd, kb=kb, vb=vh * beta, W=kb * E, qE=qh * E,
                kt=kh * Fd)


def _apply_a(a, u):
    hi, lo = _split(a)
    ub = _bf(u)
    return jnp.dot(hi, ub, preferred_element_type=F32) + jnp.dot(lo, ub, preferred_element_type=F32)


def _dn_scan(q, k, v, bg, gc, gct, a):
    T = q.shape[0]
    N = T // CHUNK
    cb = min(SCAN_CHUNKS, N)

    def body(q_ref, k_ref, v_ref, bg_ref, gc_ref, gct_ref, a_ref, o_ref, sall_ref, s_ref):
        @pl.when(pl.program_id(0) == 0)
        def _():
            s_ref[...] = jnp.zeros_like(s_ref)
        H = range(N_HEADS)
        sl = [slice(h * HEAD_DIM, (h + 1) * HEAD_DIM) for h in H]
        pre = []
        for u in range(cb):
            r = slice(u * CHUNK, (u + 1) * CHUNK)
            bgv, gcv, gctv = bg_ref[r, :], gc_ref[r, :], gct_ref[u]
            q_, k_ = [q_ref[r, s] for s in sl], [k_ref[r, s] for s in sl]
            t = [_dn_head_terms(q_[h], k_[h], v_ref[r, sl[h]], bgv[:, h:h + 1],
                                gcv[:, N_HEADS + h:N_HEADS + h + 1], gctv[N_HEADS + h:N_HEADS + h + 1, :]) for h in H]
            P = [_nt(q_[h], k_[h]) * t[h]["gam"] for h in H]
            pre.append((r, t, P))
        S = [s_ref[h] for h in H]
        for u in range(cb):
            r, t, P = pre[u]
            for h in H:
                sall_ref[u, h] = S[h]
            WS = [_nn(t[h]["W"], S[h]) for h in H]
            qS = [_nn(t[h]["qE"], S[h]) for h in H]
            vn = [_apply_a(a_ref[u, h], t[h]["vb"] - WS[h]) for h in H]
            Pv = [_nn(P[h], vn[h]) for h in H]
            kv = [_tn(t[h]["kt"], vn[h]) for h in H]
            for h in H:
                o_ref[r, sl[h]] = qS[h] + Pv[h]
            S = [t[h]["cd"] * S[h] + kv[h] for h in H]
        for h in H:
            s_ref[h] = S[h]

    row512 = pl.BlockSpec((cb * CHUNK, DN_WIDTH), lambda n: (n, 0))
    row128 = pl.BlockSpec((cb * CHUNK, 128), lambda n: (n, 0))
    return pl.pallas_call(
        body, name="dn_scan", grid=(N // cb,),
        in_specs=[row512, row512, row512, row128, row128, pl.BlockSpec((cb, 8, CHUNK), lambda n: (n, 0, 0)),
                  pl.BlockSpec((cb, N_HEADS, CHUNK, CHUNK), lambda n: (n, 0, 0, 0))],
        out_specs=[row512, pl.BlockSpec((cb, N_HEADS, HEAD_DIM, HEAD_DIM), lambda n: (n, 0, 0, 0))],
        out_shape=[jax.ShapeDtypeStruct((T, DN_WIDTH), F32),
                   jax.ShapeDtypeStruct((N, N_HEADS, HEAD_DIM, HEAD_DIM), F32)],
        scratch_shapes=[pltpu.VMEM((N_HEADS, HEAD_DIM, HEAD_DIM), F32)],
        compiler_params=_cp("arbitrary"))(q, k, v, bg, gc, gct, a)


def _dn_scan_bwd(q, k, v, bg, gc, gct, a, a_t, sall, do, dep=None):
    T = q.shape[0]
    N = T // CHUNK

    cb = min(SCAN_CHUNKS, N)
    nb = N // cb

    def body(q_ref, k_ref, v_ref, bg_ref, gc_ref, gct_ref, a_ref, at_ref, sall_ref, do_ref, *rest):
        dq_ref, dk_ref, dv_ref, dbg_ref, ds_ref = rest[-5:]
        @pl.when(pl.program_id(0) == 0)
        def _():
            ds_ref[...] = jnp.zeros_like(ds_ref)
        lane = _lane_iota((CHUNK, 128))
        rowi = lax.broadcasted_iota(jnp.int32, (CHUNK, 1), 0)
        ii, jj, _ = _tri(True)
        rev = (jj >= ii).astype(F32)
        H = range(N_HEADS)
        sl = [slice(h * HEAD_DIM, (h + 1) * HEAD_DIM) for h in H]
        pre = {}
        for u in reversed(range(cb)):
            r = slice(u * CHUNK, (u + 1) * CHUNK)
            bgv, gcv, gctv = bg_ref[r, :], gc_ref[r, :], gct_ref[u]
            q_, k_, v_ = [q_ref[r, s] for s in sl], [k_ref[r, s] for s in sl], [v_ref[r, s] for s in sl]
            dO = [do_ref[r, s] for s in sl]
            t = [_dn_head_terms(q_[h], k_[h], v_[h], bgv[:, h:h + 1], gcv[:, N_HEADS + h:N_HEADS + h + 1],
                                gctv[N_HEADS + h:N_HEADS + h + 1, :]) for h in H]
            beta = [t[h]["beta"] for h in H]
            S = [sall_ref[u, h] for h in H]
            A = [a_ref[u, h] for h in H]
            WS = [_nn(t[h]["W"], S[h]) for h in H]
            KK = [_nt(t[h]["kb"], k_[h]) for h in H]
            QK = [_nt(q_[h], k_[h]) for h in H]
            d_qE = [_nt(dO[h], S[h]) for h in H]
            vn = [_apply_a(A[h], t[h]["vb"] - WS[h]) for h in H]
            PtdO = [_tn(QK[h] * t[h]["gam"], dO[h]) for h in H]
            qEdO = [_tn(t[h]["qE"], dO[h]) for h in H]
            dOvn = [_nt(dO[h], vn[h]) for h in H]
            dQK = [jnp.where(ii >= jj, dOvn[h], 0.0) * t[h]["gam"] for h in H]
            dQKk = [_nn(dQK[h], k_[h]) for h in H]
            dQKq = [_tn(dQK[h], q_[h]) for h in H]
            pre[u] = (r, q_, k_, v_, beta, t, S, A, KK, QK, d_qE, vn, PtdO, qEdO, dQK, dQKk, dQKq)
        dSn = [ds_ref[h] for h in H]
        for u in reversed(range(cb)):
            r, q_, k_, v_, beta, t, S, A, KK, QK, d_qE, vn, PtdO, qEdO, dQK, dQKk, dQKq = pre[u]
            gam, E, Fd, cd, kb = ([t[h][n] for h in H] for n in ("gam", "E", "F", "cd", "kb"))
            ktdS = [_nn(t[h]["kt"], dSn[h]) for h in H]
            dU = [_apply_a(at_ref[u, h], PtdO[h] + ktdS[h]) for h in H]
            d_kt = [_nt(vn[h], dSn[h]) for h in H]
            dUvn = [_nt(dU[h], vn[h]) for h in H]
            dUS = [_nt(dU[h], S[h]) for h in H]
            WdU = [_tn(t[h]["W"], dU[h]) for h in H]
            d_cd = [jnp.sum(S[h] * dSn[h]) for h in H]
            dSn = [cd[h] * dSn[h] + qEdO[h] - WdU[h] for h in H]
            dKK = [jnp.where(ii > jj, -dUvn[h], 0.0) * gam[h] for h in H]
            dKKk = [_nn(dKK[h], k_[h]) for h in H]
            dKKkb = [_tn(dKK[h], kb[h]) for h in H]
            dbeta_arr = jnp.zeros((CHUNK, 128), F32)
            dgc_arr = jnp.zeros((CHUNK, 128), F32)
            for h in H:
                dW = -dUS[h]
                dq_ref[r, sl[h]] = dQKk[h] + d_qE[h] * E[h]
                d_kb = dKKk[h] + dW * E[h]
                dk_ref[r, sl[h]] = dQKq[h] + dKKkb[h] + d_kb * beta[h] + d_kt[h] * Fd[h]
                dv_ref[r, sl[h]] = dU[h] * beta[h]
                Z = dQK[h] * QK[h] + dKK[h] * KK[h]
                dbeta = jnp.sum(dU[h] * v_[h] + d_kb * k_[h], axis=-1, keepdims=True)
                m_e = (dW * kb[h] + d_qE[h] * q_[h]) * E[h]
                m_f = d_kt[h] * k_[h] * Fd[h]
                zdiag = jnp.where(ii == jj, jnp.sum(Z, axis=0, keepdims=True), 0.0)
                dgc = (jnp.sum(m_e - m_f, axis=-1, keepdims=True) + jnp.sum(Z - zdiag, axis=-1, keepdims=True)
                       + jnp.where(rowi == CHUNK - 1, jnp.sum(m_f) + d_cd[h] * cd[h], 0.0))
                dbeta_arr = dbeta_arr + jnp.where(lane == h, dbeta, 0.0)
                dgc_arr = dgc_arr + jnp.where(lane == N_HEADS + h, dgc, 0.0)
            dbg_ref[r, :] = dbeta_arr + jnp.dot(rev, dgc_arr, precision=lax.Precision.HIGHEST,
                                                preferred_element_type=F32)
        for h in H:
            ds_ref[h] = dSn[h]

    row512 = pl.BlockSpec((cb * CHUNK, DN_WIDTH), lambda n: (nb - 1 - n, 0))
    row128 = pl.BlockSpec((cb * CHUNK, 128), lambda n: (nb - 1 - n, 0))
    in_specs, args = _with_dep(
        [row512, row512, row512, row128, row128,
         pl.BlockSpec((cb, 8, CHUNK), lambda n: (nb - 1 - n, 0, 0)),
         pl.BlockSpec((cb, N_HEADS, CHUNK, CHUNK), lambda n: (nb - 1 - n, 0, 0, 0)),
         pl.BlockSpec((cb, N_HEADS, CHUNK, CHUNK), lambda n: (nb - 1 - n, 0, 0, 0)),
         pl.BlockSpec((cb, N_HEADS, HEAD_DIM, HEAD_DIM), lambda n: (nb - 1 - n, 0, 0, 0)), row512],
        [q, k, v, bg, gc, gct, a, a_t, sall, do], dep)
    return pl.pallas_call(
        body, name="dn_scan_bwd", grid=(nb,), in_specs=in_specs,
        out_specs=[row512, row512, row512, row128],
        out_shape=[jax.ShapeDtypeStruct((T, DN_WIDTH), F32)] * 3 + [jax.ShapeDtypeStruct((T, 128), F32)],
        scratch_shapes=[pltpu.VMEM((N_HEADS, HEAD_DIM, HEAD_DIM), F32)],
        compiler_params=_cp("arbitrary"))(*args)


MIX_BLOCKS_FWD = 4
MIX_BLOCKS_BWD = 2


def _sg_mask():
    ii = lax.broadcasted_iota(jnp.int32, (SG_BLOCK, SG_BLOCK), 0) // CHUNK
    jj = lax.broadcasted_iota(jnp.int32, (SG_BLOCK, SG_BLOCK), 1) // CHUNK
    return jj <= ii


def _mix_out_proj(o, p, ong, sgn, sgw, sgbt, w_out, x, g):
    T = o.shape[0]
    rb = min(MIX_BLOCKS_FWD * SG_BLOCK, T)
    nb = T // rb

    def body(o_ref, gate_ref, u_ref, vg_ref, ong_ref, sgn_ref, sgw_ref, sgbt_ref, w_ref, x_ref, g_ref,
             mix_ref, x2_ref, h_ref, prev_scr):
        @pl.when(pl.program_id(0) == 0)
        def _():
            prev_scr[...] = jnp.zeros_like(prev_scr)
        x2_ref[...] = jnp.dot(prev_scr[...], w_ref[...], preferred_element_type=F32) + x_ref[...]
        for r in _sub_blocks(rb):
            xh, _ = _rms_hat(x2_ref[r, :])
            h_ref[r, :] = (xh * g_ref[...]).astype(BF16)
        mask = _sg_mask()
        for u0 in range(0, rb, SG_BLOCK):
            rows = slice(u0, u0 + SG_BLOCK)
            for h in range(N_HEADS):
                sl = slice(h * HEAD_DIM, (h + 1) * HEAD_DIM)
                oh = o_ref[rows, sl]
                r = lax.rsqrt(jnp.mean(oh * oh, axis=-1, keepdims=True) + EPS)
                mix_ref[rows, sl] = (oh * r * ong_ref[...] * _silu(gate_ref[rows, sl])).astype(BF16)
            for gi in range(SG_GROUPS):
                sl = slice(gi * SG_BLOCK, (gi + 1) * SG_BLOCK)
                gv = _gelu(vg_ref[rows, sl])
                r = lax.rsqrt(jnp.mean(gv * gv, axis=-1, keepdims=True) + EPS)
                vh = gv * r * sgn_ref[:, sl]
                s = _nn(jnp.where(mask, sgw_ref[gi], 0.0), vh) + sgbt_ref[:, gi:gi + 1]
                mix_ref[rows, DN_WIDTH + gi * SG_BLOCK:DN_WIDTH + (gi + 1) * SG_BLOCK] = (
                    _gelu(u_ref[rows, sl]) * s).astype(BF16)
        prev_scr[...] = mix_ref[...]

    cur = lambda i: jnp.minimum(i, nb - 1)
    prev = lambda i: jnp.maximum(i - 1, 0)

    def col(c):
        return pl.BlockSpec((rb, 512), lambda i: (cur(i), c))
    row_prev = lambda dt: pl.BlockSpec((rb, D_MODEL), lambda i: (prev(i), 0))
    return pl.pallas_call(
        body, name="mix_out_proj", grid=(nb + 1,),
        in_specs=[pl.BlockSpec((rb, DN_WIDTH), lambda i: (cur(i), 0)), col(3), col(4), col(5),
                  pl.BlockSpec((1, 128), lambda i: (0, 0)), pl.BlockSpec((1, SG_WIDTH), lambda i: (0, 0)),
                  pl.BlockSpec((SG_GROUPS, SG_BLOCK, SG_BLOCK), lambda i: (0, 0, 0)),
                  pl.BlockSpec((SG_BLOCK, 128), lambda i: (0, 0)),
                  pl.BlockSpec((D_MODEL, D_MODEL), lambda i: (0, 0)), row_prev(F32),
                  pl.BlockSpec((1, D_MODEL), lambda i: (0, 0))],
        out_specs=[pl.BlockSpec((rb, D_MODEL), lambda i: (cur(i), 0)), row_prev(F32), row_prev(BF16)],
        out_shape=[jax.ShapeDtypeStruct((T, D_MODEL), BF16), jax.ShapeDtypeStruct((T, D_MODEL), F32),
                   jax.ShapeDtypeStruct((T, D_MODEL), BF16)],
        scratch_shapes=[pltpu.VMEM((rb, D_MODEL), BF16)],
        compiler_params=_cp("arbitrary"))(o, p, p, p, ong, sgn, sgw, sgbt, w_out, x, g)


def _mix_bwd(o, p, ong, sgn, sgw, sgbt, dx2, w_out, dep=None):
    T = o.shape[0]
    rb = min(MIX_BLOCKS_BWD * SG_BLOCK, T)
    nb = T // rb

    def body(o_ref, gate_ref, u_ref, vg_ref, ong_ref, sgn_ref, sgw_ref, sgbt_ref, dx2_ref, wout_ref, *rest):
        do_ref, dp_ref, gong_ref, gsgn_ref, gsgw_ref, gsgbt_ref, dmix_ref = rest[-7:]
        @pl.when(pl.program_id(0) == 0)
        def _():
            gong_ref[...] = jnp.zeros_like(gong_ref)
            gsgn_ref[...] = jnp.zeros_like(gsgn_ref)
            gsgw_ref[...] = jnp.zeros_like(gsgw_ref)
            gsgbt_ref[...] = jnp.zeros_like(gsgbt_ref)
            dmix_ref[...] = jnp.zeros_like(dmix_ref)
        dmix_next = _nt(dx2_ref[...], wout_ref[...])
        mask = _sg_mask()
        lane = _lane_iota((SG_BLOCK, 128))
        for u0 in range(0, rb, SG_BLOCK):
            rows = slice(u0, u0 + SG_BLOCK)
            for h in range(N_HEADS):
                sl = slice(h * HEAD_DIM, (h + 1) * HEAD_DIM)
                oh = o_ref[rows, sl]
                dm = dmix_ref[rows, sl]
                r = lax.rsqrt(jnp.mean(oh * oh, axis=-1, keepdims=True) + EPS)
                oh_hat = oh * r
                gt = gate_ref[rows, sl]
                sg = _silu(gt)
                dp_ref[rows, sl] = (dm * oh_hat * ong_ref[...] * _dsilu(gt)).astype(BF16)
                dn_ = dm * sg
                gong_ref[...] += jnp.sum(dn_ * oh_hat, axis=0, keepdims=True)
                dhat = dn_ * ong_ref[...]
                do_ref[rows, sl] = r * (dhat - oh_hat * jnp.mean(dhat * oh_hat, axis=-1, keepdims=True))
            for gi in range(SG_GROUPS):
                sl = slice(gi * SG_BLOCK, (gi + 1) * SG_BLOCK)
                vraw = vg_ref[rows, sl]
                gv = _gelu(vraw)
                r = lax.rsqrt(jnp.mean(gv * gv, axis=-1, keepdims=True) + EPS)
                vhat = gv * r
                vn = vhat * sgn_ref[:, sl]
                wm = jnp.where(mask, sgw_ref[gi], 0.0)
                s = _nn(wm, vn) + sgbt_ref[:, gi:gi + 1]
                uraw = u_ref[rows, sl]
                dm = dmix_ref[rows, DN_WIDTH + gi * SG_BLOCK:DN_WIDTH + (gi + 1) * SG_BLOCK]
                dp_ref[rows, DN_WIDTH + gi * SG_BLOCK:DN_WIDTH + (gi + 1) * SG_BLOCK] = (
                    dm * s * _dgelu(uraw)).astype(BF16)
                ds = dm * _gelu(uraw)
                gsgbt_ref[...] += jnp.where(lane == gi, jnp.sum(ds, axis=-1, keepdims=True), 0.0)
                gsgw_ref[gi] += jnp.where(mask, _nt(ds, vn), 0.0)
                dvn = _tn(wm, ds)
                gsgn_ref[:, sl] += jnp.sum(dvn * vhat, axis=0, keepdims=True)
                dhat = dvn * sgn_ref[:, sl]
                dgv = r * (dhat - vhat * jnp.mean(dhat * vhat, axis=-1, keepdims=True))
                dp_ref[rows, 2 * DN_WIDTH + gi * SG_BLOCK:2 * DN_WIDTH + (gi + 1) * SG_BLOCK] = (
                    dgv * _dgelu(vraw)).astype(BF16)
        dmix_ref[...] = dmix_next

    prev = lambda i: jnp.maximum(i - 1, 0)

    def col(c):
        return pl.BlockSpec((rb, 512), lambda i: (prev(i), c))
    full = lambda *s: pl.BlockSpec(s, lambda i: (0,) * len(s))
    in_specs, args = _with_dep(
        [pl.BlockSpec((rb, DN_WIDTH), lambda i: (prev(i), 0)), col(3), col(4), col(5),
         full(1, 128), full(1, SG_WIDTH), full(SG_GROUPS, SG_BLOCK, SG_BLOCK), full(SG_BLOCK, 128),
         pl.BlockSpec((rb, D_MODEL), lambda i: (jnp.minimum(i, nb - 1), 0)), full(D_MODEL, D_MODEL)],
        [o, p, p, p, ong, sgn, sgw, sgbt, dx2, w_out], dep)
    return pl.pallas_call(
        body, name="mix_bwd", grid=(nb + 1,), in_specs=in_specs,
        out_specs=[pl.BlockSpec((rb, DN_WIDTH), lambda i: (prev(i), 0)),
                   pl.BlockSpec((rb, 3 * 512), lambda i: (prev(i), 0)),
                   full(1, 128), full(1, SG_WIDTH), full(SG_GROUPS, SG_BLOCK, SG_BLOCK), full(SG_BLOCK, 128)],
        out_shape=[jax.ShapeDtypeStruct((T, DN_WIDTH), F32), jax.ShapeDtypeStruct((T, 3 * 512), BF16),
                   jax.ShapeDtypeStruct((1, 128), F32), jax.ShapeDtypeStruct((1, SG_WIDTH), F32),
                   jax.ShapeDtypeStruct((SG_GROUPS, SG_BLOCK, SG_BLOCK), F32),
                   jax.ShapeDtypeStruct((SG_BLOCK, 128), F32)],
        scratch_shapes=[pltpu.VMEM((rb, D_MODEL), F32)],
        compiler_params=_cp("arbitrary"))(*args)


def _pad_lanes(row, offset=0):
    n = row.shape[1]
    return jnp.pad(row, ((0, 0), (offset, 128 - n - offset)))


def _local_step(x, tgt, w, dep=None, late_weights=None, on_grad=None):
    T = x.shape[0]
    N = T // CHUNK
    on_grad = on_grad or (lambda name, g: None)
    alog_row = _pad_lanes(w["dn_a_log"], N_HEADS)
    dtb_row = _pad_lanes(w["dn_dt_bias"], N_HEADS)
    sgbt = jnp.pad(w["sg_b"].T, ((0, 0), (0, 128 - SG_GROUPS)))

    p, h1, w_in_pad, q, k, v, bg = _in_proj_act(x, w["attn_norm_g"], w["w_in"], w["dn_conv_w"], alog_row, dtb_row,
                                                dep=dep)
    gc, gct, lmat = _dn_chunk(k, bg)
    lt = lmat.reshape(N * N_HEADS, CHUNK * CHUNK).T
    at = _tri_inv(lt)
    a = at.reshape(CHUNK * CHUNK, N * N_HEADS).T.reshape(N, N_HEADS, CHUNK, CHUNK)
    a_t = at.transpose(1, 0, 2).reshape(CHUNK * CHUNK, N * N_HEADS).T.reshape(N, N_HEADS, CHUNK, CHUNK)
    o, sall = _dn_scan(q, k, v, bg, gc, gct, a)
    if late_weights is not None:
        w = {**w, **late_weights("out_proj", o)}
    mix, x2, h2 = _mix_out_proj(o, p, w["dn_out_norm_g"], w["sg_norm_g"], w["sg_w"], sgbt, w["w_out"], x,
                                w["ffn_norm_g"])
    up, act = _up_proj_act(h2, w["w_up"], w["ffn_conv_w"], w["ffn_conv_b"])
    if late_weights is not None:
        w = {**w, **late_weights("down_proj", act)}
    loss, dx3, g_final = _down_proj_loss(act, w["w_down"], x2, tgt, w["final_norm_g"])

    dact = _mm_nt("d_act", dx3, w["w_down"], F32, 512, D_FF)
    g_w_down = _mm_tn("g_w_down", act, dx3, D_FF, 1024, 1024)
    tok = on_grad("w_down", g_w_down)
    dup, g_ffn_conv_w, g_ffn_conv_b = _ffn_act_bwd(up, dact, w["ffn_conv_w"], w["ffn_conv_b"], dep=tok)
    g_w_up = _mm_tn("g_w_up", h2, dup, 1024, 2 * D_FF // 4, 2048, col_major_tiles=True)
    tok = on_grad("w_up", g_w_up)
    dx2, g_ffn_norm = _mm_nt_rms_bwd("d_h2", dup, w["w_up"], x2, w["ffn_norm_g"], dx3, dep=tok)
    g_w_out = _mm_tn("g_w_out", mix, dx2, 1024, 1024, 1024)
    tok = on_grad("w_out", g_w_out)
    do, dp_mid, g_ong, g_sgn, g_sgw, g_sgbt = _mix_bwd(o, p, w["dn_out_norm_g"], w["sg_norm_g"], w["sg_w"], sgbt,
                                                      dx2, w["w_out"], dep=tok)
    early = dict(dn_out_norm_g=g_ong, sg_norm_g=g_sgn, sg_w=g_sgw, sg_bt=g_sgbt,
                 ffn_norm_g=g_ffn_norm, ffn_conv_w=g_ffn_conv_w, ffn_conv_b=g_ffn_conv_b, final_norm_g=g_final)
    tok = on_grad("small_early", early)
    dq, dk, dv, dbg = _dn_scan_bwd(q, k, v, bg, gc, gct, a, a_t, sall, do, dep=tok)
    dp, g_dn_conv_w, g_ad = _dn_act_bwd(p, w["dn_conv_w"], alog_row, dtb_row, dq, dk, dv, dbg, dp_mid)
    g_w_in = _mm_tn("g_w_in", h1, dp, 1024, PROJ_PAD, 1024, col_groups=(4, PROJ_COLS // 4))
    tok = on_grad("w_in", g_w_in)
    grad_x, g_attn_norm = _mm_nt_rms_bwd("d_h1", dp, w_in_pad, x, w["attn_norm_g"], dx2, dep=tok)

    grads = dict(attn_norm_g=g_attn_norm, w_in=g_w_in, dn_conv_w=g_dn_conv_w, a_dt=g_ad,
                 w_out=g_w_out, w_up=g_w_up, w_down=g_w_down, **early)
    return loss, grad_x, grads


def _me():
    return lax.axis_index("x"), lax.axis_index("y"), lax.axis_index("c")


def _peer(rel):
    x, y, c = _me()
    return {"x": (1 - x, y, c), "y": (x, 1 - y, c), "xy": (1 - x, 1 - y, c), "c": (x, y, 1 - c)}[rel]


def _chip_of(dev):
    return 2 * dev[0] + dev[1]


CHIP_RELS = ("x", "y", "xy")


def _run_copies(copies, sends, recvs):
    for cp in copies:
        cp.start()
    for cp in recvs:
        cp.wait_recv()
    for cp in sends:
        cp.wait_send()


def _gather_first(w_shard, small_shard):
    R = w_shard.shape[0]
    r2 = R // 2

    def body(w_ref, s_ref, w_out, s_out, send_sems, recv_sems):
        x, y, c = _me()
        me = _chip_of((x, y))
        sib = _peer("c")

        def half(chip, core):
            return w_out.at[chip, pl.ds(pl.multiple_of(core * r2, 8), r2), :]

        def copy(k, src, dst, to):
            return pltpu.make_async_remote_copy(src_ref=src, dst_ref=dst, send_sem=send_sems.at[k],
                                                recv_sem=recv_sems.at[k], device_id=to, device_id_type=MESH)

        own_rows = w_ref.at[pl.ds(pl.multiple_of(c * r2, 8), r2), :]
        first = [copy(r, own_rows, half(me, c), _peer(rel)) for r, rel in enumerate(CHIP_RELS)]
        first += [copy(3 + r, s_ref, s_out.at[me], _peer(rel)) for r, rel in enumerate(CHIP_RELS)]
        for cp in first:
            cp.start()
        passed = []
        for r, rel in enumerate(CHIP_RELS):
            their = _chip_of(_peer(rel))
            copy(r, own_rows, half(their, c), _peer(rel)).wait_recv()
            fwd = copy(6 + r, half(their, c), half(their, c), sib)
            fwd.start()
            passed.append(fwd)
        for r, rel in enumerate(CHIP_RELS):
            their = _chip_of(_peer(rel))
            copy(3 + r, s_ref, s_out.at[their], _peer(rel)).wait_recv()
            copy(6 + r, own_rows, half(their, 1 - c), sib).wait_recv()
        for cp in first + passed:
            cp.wait_send()

    w_all, s_all = pl.pallas_call(
        body, name="gather_first", in_specs=[ANY, ANY], out_specs=[ANY, ANY],
        out_shape=[jax.ShapeDtypeStruct((4,) + w_shard.shape, w_shard.dtype),
                   jax.ShapeDtypeStruct((4,) + small_shard.shape, small_shard.dtype)],
        scratch_shapes=[pltpu.SemaphoreType.DMA((9,)), pltpu.SemaphoreType.DMA((9,))])(w_shard, small_shard)
    me = _chip_of(_me())
    return (lax.dynamic_update_index_in_dim(w_all, w_shard, me, 0),
            lax.dynamic_update_index_in_dim(s_all, small_shard, me, 0))


OTHERS = tuple((fx, fy, fc) for fx in (0, 1) for fy in (0, 1) for fc in (0, 1) if (fx, fy, fc) != (0, 0, 0))


def _other(flip):
    x, y, c = _me()
    return (x ^ flip[0], y ^ flip[1], c ^ flip[2])


def _linear(dev):
    return 4 * dev[0] + 2 * dev[1] + dev[2]


def _exchange_small(small):
    def body(small_ref, out_ref, send_sems, recv_sems):
        my_slot = _linear(_me())
        sends, recvs = [], []
        for k, flip in enumerate(OTHERS):
            peer = _other(flip)
            sends.append(pltpu.make_async_remote_copy(
                src_ref=small_ref, dst_ref=out_ref.at[my_slot], send_sem=send_sems.at[k], recv_sem=recv_sems.at[k],
                device_id=peer, device_id_type=MESH))
            recvs.append(pltpu.make_async_remote_copy(
                src_ref=small_ref, dst_ref=out_ref.at[_linear(peer)], send_sem=send_sems.at[k],
                recv_sem=recv_sems.at[k], device_id=peer, device_id_type=MESH))
        _run_copies(sends, sends, recvs)

    out = pl.pallas_call(
        body, name="exchange_small", in_specs=[ANY], out_specs=ANY,
        out_shape=jax.ShapeDtypeStruct((8,) + small.shape, small.dtype),
        scratch_shapes=[pltpu.SemaphoreType.DMA((7,)), pltpu.SemaphoreType.DMA((7,))])(small)
    return lax.dynamic_update_index_in_dim(out, small, _linear(_me()), 0)


def _pair_swap(name, halves):
    n = len(halves)

    def body(*refs):
        src, out = refs[:n], refs[n:2 * n]
        send_sems, recv_sems = refs[2 * n:]
        sib = _peer("c")
        copies = [pltpu.make_async_remote_copy(
            src_ref=src[i], dst_ref=out[i], send_sem=send_sems.at[i], recv_sem=recv_sems.at[i],
            device_id=sib, device_id_type=MESH) for i in range(n)]
        _run_copies(copies, copies, copies)

    return pl.pallas_call(
        body, name=name, in_specs=[ANY] * n, out_specs=[ANY] * n,
        out_shape=[jax.ShapeDtypeStruct(h.shape, h.dtype) for h in halves],
        scratch_shapes=[pltpu.SemaphoreType.DMA((n,)), pltpu.SemaphoreType.DMA((n,))])(*halves)


HBM = pl.BlockSpec(memory_space=pltpu.HBM)
SEM = pl.BlockSpec(memory_space=pltpu.SEMAPHORE)
EFFECT = pltpu.SideEffectType.DATAFLOW_SIDE_EFFECTING


def _hbm(a):
    return pltpu.with_memory_space_constraint(a, pltpu.HBM)


def _transfer_start(name, srcs, lands, n_copies, make_copies, after=None):
    n, m = len(srcs), len(lands)

    def body(*refs):
        src, land = refs[:n], refs[n:n + m]
        outs = refs[n + m + (after is not None):]
        send_sems, recv_sems, token = outs[0], outs[1], outs[-1]
        for cp in make_copies(src, land, send_sems, recv_sems):
            cp.start()
        token[...] = jnp.zeros_like(token)

    arrs = list(srcs) + list(lands)
    in_specs, args = _with_dep([HBM] * (n + m), [_hbm(a) for a in arrs], after)
    out = pl.pallas_call(
        body, name=name,
        out_shape=(pltpu.SemaphoreType.DMA((n_copies,)), pltpu.SemaphoreType.DMA((n_copies,)),
                   *[pltpu.HBM(a.shape, a.dtype) for a in arrs], jax.ShapeDtypeStruct((8, 128), F32)),
        in_specs=in_specs,
        out_specs=(SEM, SEM, *[HBM] * (n + m), pl.BlockSpec(memory_space=pltpu.VMEM)),
        input_output_aliases={i: 2 + i for i in range(n + m)},
        compiler_params=pltpu.CompilerParams(has_side_effects=EFFECT))(*args)
    return out[0], out[1], list(out[2:2 + n]), list(out[2 + n:2 + n + m]), out[-1]


def _transfer_wait(name, send_sems, recv_sems, srcs, lands, make_copies, after):
    n, m = len(srcs), len(lands)

    def body(*refs):
        src, land = refs[:n], refs[n:n + m]
        s_sems, r_sems = refs[n + m], refs[n + m + 1]
        for cp in make_copies(src, land, s_sems, r_sems):
            cp.wait_send()
            cp.wait_recv()

    arrs = list(srcs) + list(lands)
    out = pl.pallas_call(
        body, name=name, out_shape=tuple(pltpu.HBM(a.shape, a.dtype) for a in arrs),
        in_specs=[HBM] * (n + m) + [SEM, SEM, ANY], out_specs=tuple([HBM] * (n + m)),
        input_output_aliases={i: i for i in range(n + m)},
        compiler_params=pltpu.CompilerParams(has_side_effects=EFFECT))(*arrs, send_sems, recv_sems, after)
    return list(out[:n]), list(out[n:])


def _gather_copies(src, land, send_sems, recv_sems):
    me = _chip_of(_me())
    copies = []
    for i in range(len(src)):
        for r, rel in enumerate(CHIP_RELS):
            k = 3 * i + r
            copies.append(pltpu.make_async_remote_copy(
                src_ref=src[i], dst_ref=land[i].at[me], send_sem=send_sems.at[k], recv_sem=recv_sems.at[k],
                device_id=_peer(rel), device_id_type=MESH))
    return copies


def _small_copies(src, land, send_sems, recv_sems):
    my_slot = _linear(_me())
    return [pltpu.make_async_remote_copy(
        src_ref=src[0], dst_ref=land[0].at[my_slot], send_sem=send_sems.at[k], recv_sem=recv_sems.at[k],
        device_id=_other(flip), device_id_type=MESH) for k, flip in enumerate(OTHERS)]


def _pieces_copies(src, land, send_sems, recv_sems):
    copies = []
    for k, flip in enumerate(OTHERS):
        peer = _other(flip)
        copies.append(pltpu.make_async_remote_copy(
            src_ref=src[0].at[_linear(peer)], dst_ref=land[0].at[k], send_sem=send_sems.at[k],
            recv_sem=recv_sems.at[k], device_id=peer, device_id_type=MESH))
    return copies


def _row_block(rows, cols, budget=2 * 1024 * 1024):
    rb = max(8, (budget // (4 * cols)) // 8 * 8)
    while rows % rb:
        rb -= 8
    return rb if rb > 0 else rows


def _sum_pieces(name, mine, slot, rest):
    _, R, Cc = mine.shape
    K = rest.shape[0]
    rb = _row_block(R, Cc)

    def body(s_ref, f_ref, r_ref, o_ref):
        acc = f_ref[0].astype(F32)
        for j in range(K):
            acc = acc + r_ref[j].astype(F32)
        o_ref[...] = acc

    return pl.pallas_call(
        body, name=name,
        grid_spec=pltpu.PrefetchScalarGridSpec(
            num_scalar_prefetch=1, grid=(R // rb,),
            in_specs=[pl.BlockSpec((1, rb, Cc), lambda i, s: (s[0], i, 0)),
                      pl.BlockSpec((K, rb, Cc), lambda i, s: (0, i, 0))],
            out_specs=pl.BlockSpec((rb, Cc), lambda i, s: (i, 0))),
        out_shape=jax.ShapeDtypeStruct((R, Cc), F32), compiler_params=_cp("parallel"))(slot, mine, rest)


def _adamw_math(w, gv, m, v):
    mn = ADAM_B1 * m + (1.0 - ADAM_B1) * gv
    vn = ADAM_B2 * v + (1.0 - ADAM_B2) * (gv * gv)
    m_hat = mn / (1.0 - ADAM_B1 ** ADAM_STEP)
    v_hat = vn / (1.0 - ADAM_B2 ** ADAM_STEP)
    return -ADAM_LR * (m_hat / (jnp.sqrt(v_hat) + ADAM_EPS) + ADAM_WD * w), mn, vn


def _adamw_halves(name, w, mine, theirs, m, v, core):
    R, Cc = w.shape
    r2 = R // 2
    rb = _row_block(r2, Cc, 1024 * 1024)
    nb2 = r2 // rb

    def body(c_ref, w_ref, mine_ref, theirs_ref, m_ref, v_ref, g_ref, d_ref, mo_ref, vo_ref):
        is_mine = (pl.program_id(0) // nb2) == c_ref[0]
        gv = jnp.where(is_mine, mine_ref[...], theirs_ref[...])
        g_ref[...] = gv
        d_ref[...], mo_ref[...], vo_ref[...] = _adamw_math(w_ref[...], gv, m_ref[...], v_ref[...])

    blk = pl.BlockSpec((rb, Cc), lambda i, c: (i, 0))
    half = lambda own: pl.BlockSpec(
        (rb, Cc), lambda i, c: (jnp.clip(i - (c[0] if own else 1 - c[0]) * nb2, 0, nb2 - 1), 0))
    return pl.pallas_call(
        body, name=name,
        grid_spec=pltpu.PrefetchScalarGridSpec(
            num_scalar_prefetch=1, grid=(2 * nb2,), in_specs=[blk, half(True), half(False), blk, blk],
            out_specs=[blk] * 4),
        out_shape=[jax.ShapeDtypeStruct((R, Cc), F32)] * 4, compiler_params=_cp("parallel"))(core, w, mine, theirs, m, v)


def _adamw_transposed(name, wt, mine, theirs, mt, vt, core):
    Cc, kh_n, _ = wt.shape
    r2 = mine.shape[0]
    per_half = kh_n // 2
    nb = -(-Cc // LANES)

    def body(c_ref, w_ref, mine_ref, theirs_ref, m_ref, v_ref, g_ref, d_ref, mo_ref, vo_ref):
        first = c_ref[0] == 0
        halves = (jnp.where(first, mine_ref[...], theirs_ref[...]).T,
                  jnp.where(first, theirs_ref[...], mine_ref[...]).T)
        for kh in range(kh_n):
            lo = (kh % per_half) * LANES
            g_ref[:, kh, :] = halves[kh // per_half][:, lo:lo + LANES]
        d_ref[...], mo_ref[...], vo_ref[...] = _adamw_math(w_ref[...], g_ref[...], m_ref[...], v_ref[...])

    blk = pl.BlockSpec((LANES, kh_n, LANES), lambda i, c: (i, 0, 0))
    half = pl.BlockSpec((r2, LANES), lambda i, c: (0, i))
    return pl.pallas_call(
        body, name=name,
        grid_spec=pltpu.PrefetchScalarGridSpec(
            num_scalar_prefetch=1, grid=(nb,), in_specs=[blk, half, half, blk, blk], out_specs=[blk] * 4),
        out_shape=[jax.ShapeDtypeStruct(wt.shape, F32)] * 4, compiler_params=_cp("parallel"))(
            core, wt, mine, theirs, mt, vt)


FF_W = 2 * D_FF
FF_CH = FF_W // LANES
DNC_W = 3 * DN_WIDTH
DNC_CH = DNC_W // LANES
E_ONG, E_SGN, E_SGW, E_SGBT = 0, 1, 8, 8 + SG_GROUPS * SG_BLOCK
E_FFN = E_SGBT + SG_BLOCK
E_FCW = E_FFN + D_MODEL // LANES
E_FCB = E_FCW + 3 * FF_CH
E_FIN = E_FCB + FF_CH
EARLY_ROWS = E_FIN + D_MODEL // LANES
L_ATTN, L_DNC = 0, D_MODEL // LANES
L_AD = L_DNC + 4 * DNC_CH
L_LOSS = L_AD + 2
LATE_ROWS = -(-(L_LOSS + 1) // 8) * 8


def _put_rows(out, r0, x):
    k, width = x.shape
    n = width // LANES
    for t in range(k):
        for j in range(n):
            out[r0 + t * n + j:r0 + t * n + j + 1, :] = x[t:t + 1, j * LANES:(j + 1) * LANES]


def _pack_early(ong, sgn, sgw, sgbt, ffn, fcw, fcb, fin):
    def body(ong_ref, sgn_ref, sgw_ref, sgbt_ref, ffn_ref, fcw_ref, fcb_ref, fin_ref, out):
        out[...] = jnp.zeros_like(out)
        _put_rows(out, E_ONG, ong_ref)
        _put_rows(out, E_SGN, sgn_ref)
        for gi in range(SG_GROUPS):
            out[E_SGW + gi * SG_BLOCK:E_SGW + (gi + 1) * SG_BLOCK, :] = sgw_ref[gi]
        out[E_SGBT:E_SGBT + SG_BLOCK, :] = sgbt_ref[...]
        _put_rows(out, E_FFN, ffn_ref)
        _put_rows(out, E_FCW, fcw_ref)
        _put_rows(out, E_FCB, fcb_ref)
        _put_rows(out, E_FIN, fin_ref)

    return pl.pallas_call(body, name="pack_small_early", out_shape=jax.ShapeDtypeStruct((EARLY_ROWS, LANES), F32))(
        ong, sgn, sgw, sgbt, ffn, fcw, fcb, fin)


def _pack_late(attn, dnc, ad, loss_row):
    def body(attn_ref, dnc_ref, ad_ref, loss_ref, out):
        out[...] = jnp.zeros_like(out)
        _put_rows(out, L_ATTN, attn_ref)
        _put_rows(out, L_DNC, dnc_ref)
        out[L_AD:L_AD + 2, :] = ad_ref[...]
        out[L_LOSS:L_LOSS + 1, :] = loss_ref[...]

    return pl.pallas_call(body, name="pack_small_late", out_shape=jax.ShapeDtypeStruct((LATE_ROWS, LANES), F32))(
        attn, dnc, ad, loss_row)


SMALL = ("attn_norm_g", "dn_a_log", "dn_dt_bias", "dn_out_norm_g", "sg_norm_g", "sg_w", "sg_b", "ffn_norm_g",
         "ffn_conv_b", "final_norm_g", "dn_conv_w", "ffn_conv_w")


def _small_update(early_all, late_all, chip, W, M, V):
    n = len(SMALL)
    arrs = [d[k] for d in (W, M, V) for k in SMALL]

    def body(c_ref, e_ref, l_ref, *refs):
        w_, m_, v_ = refs[:n], refs[n:2 * n], refs[2 * n:3 * n]
        loss_ref = refs[3 * n]
        outs = refs[3 * n + 1:]
        g_, d_, mo_, vo_ = outs[:n], outs[n:2 * n], outs[2 * n:3 * n], outs[3 * n:4 * n]
        chip_i = c_ref[0]

        def total(ref, r0, rows=1):
            acc = ref[0, pl.ds(r0, rows), :]
            for s in range(1, 8):
                acc = acc + ref[s, pl.ds(r0, rows), :]
            return acc

        def update(i, idx, g):
            g_[i][idx] = g
            d_[i][idx], mo_[i][idx], vo_[i][idx] = _adamw_math(w_[i][idx], g, m_[i][idx], v_[i][idx])

        def rows_param(name, ref, r0, width):
            i = SMALL.index(name)
            for j in range(width // LANES):
                update(i, (slice(None), slice(j * LANES, (j + 1) * LANES)), total(ref, r0 + j))

        rows_param("attn_norm_g", l_ref, L_ATTN, D_MODEL)
        ad = (total(l_ref, L_AD), total(l_ref, L_AD + 1))
        update(SMALL.index("dn_a_log"), (slice(None), slice(None)), ad[0][:, N_HEADS:2 * N_HEADS])
        update(SMALL.index("dn_dt_bias"), (slice(None), slice(None)), ad[1][:, N_HEADS:2 * N_HEADS])
        rows_param("dn_out_norm_g", e_ref, E_ONG, HEAD_DIM)
        rows_param("sg_norm_g", e_ref, E_SGN, SG_WIDTH)
        sgbt = total(e_ref, E_SGBT, SG_BLOCK).T
        for gi in range(SG_GROUPS):
            update(SMALL.index("sg_w"), (0, gi), total(e_ref, E_SGW + gi * SG_BLOCK, SG_BLOCK))
            update(SMALL.index("sg_b"), (0, slice(gi, gi + 1), slice(None)), sgbt[gi:gi + 1, :])
        rows_param("ffn_norm_g", e_ref, E_FFN, D_MODEL)
        rows_param("ffn_conv_b", e_ref, E_FCB, FF_W)
        rows_param("final_norm_g", e_ref, E_FIN, D_MODEL)
        for name, ref, r0, taps, chunks in (("dn_conv_w", l_ref, L_DNC, 4, DNC_CH), ("ffn_conv_w", e_ref, E_FCW, 3, FF_CH)):
            mine = chunks // 4
            for t in range(taps):
                for j in range(mine):
                    update(SMALL.index(name), (0, slice(t, t + 1), slice(j * LANES, (j + 1) * LANES)),
                           total(ref, r0 + t * chunks + chip_i * mine + j))
        loss_ref[...] = total(l_ref, L_LOSS)

    full = lambda a: pl.BlockSpec(a.shape, lambda i, c, nd=a.ndim: (0,) * nd)
    shapes = [jax.ShapeDtypeStruct(W[k].shape, F32) for k in SMALL]
    outs = pl.pallas_call(
        body, name="small_update",
        grid_spec=pltpu.PrefetchScalarGridSpec(
            num_scalar_prefetch=1, grid=(1,), in_specs=[full(early_all), full(late_all)] + [full(a) for a in arrs],
            out_specs=[pl.BlockSpec((1, LANES), lambda i, c: (0, 0))] + [full(s) for s in shapes] * 4),
        out_shape=[jax.ShapeDtypeStruct((1, LANES), F32)] + shapes * 4,
        compiler_params=pltpu.CompilerParams(vmem_limit_bytes=VMEM_LIMIT))(chip, early_all, late_all, *arrs)
    loss, outs = outs[0], outs[1:]
    return (loss,) + tuple(dict(zip(SMALL, outs[k * n:(k + 1) * n])) for k in range(4))


ORDER =("attn_norm_g", "w_in", "dn_conv_w", "dn_a_log", "dn_dt_bias", "dn_out_norm_g", "sg_norm_g", "sg_w",
         "sg_b", "w_out", "ffn_norm_g", "w_up", "ffn_conv_w", "ffn_conv_b", "w_down", "final_norm_g")


def kernel(x, attn_norm_g, w_in, dn_conv_w, dn_a_log, dn_dt_bias, dn_out_norm_g, sg_norm_g, sg_w, sg_b, w_out, ffn_norm_g, w_up, ffn_conv_w, ffn_conv_b, w_down, final_norm_g, loss_target, m_attn_norm_g, m_w_in, m_dn_conv_w, m_dn_a_log, m_dn_dt_bias, m_dn_out_norm_g, m_sg_norm_g, m_sg_w, m_sg_b, m_w_out, m_ffn_norm_g, m_w_up, m_ffn_conv_w, m_ffn_conv_b, m_w_down, m_final_norm_g, v_attn_norm_g, v_w_in, v_dn_conv_w, v_dn_a_log, v_dn_dt_bias, v_dn_out_norm_g, v_sg_norm_g, v_sg_w, v_sg_b, v_w_out, v_ffn_norm_g, v_w_up, v_ffn_conv_w, v_ffn_conv_b, v_w_down, v_final_norm_g):
    W = dict(attn_norm_g=attn_norm_g, w_in=w_in, dn_conv_w=dn_conv_w, dn_a_log=dn_a_log, dn_dt_bias=dn_dt_bias,
             dn_out_norm_g=dn_out_norm_g, sg_norm_g=sg_norm_g, sg_w=sg_w, sg_b=sg_b, w_out=w_out,
             ffn_norm_g=ffn_norm_g, w_up=w_up, ffn_conv_w=ffn_conv_w, ffn_conv_b=ffn_conv_b, w_down=w_down,
             final_norm_g=final_norm_g)
    Mo = dict(attn_norm_g=m_attn_norm_g, w_in=m_w_in, dn_conv_w=m_dn_conv_w, dn_a_log=m_dn_a_log,
              dn_dt_bias=m_dn_dt_bias, dn_out_norm_g=m_dn_out_norm_g, sg_norm_g=m_sg_norm_g, sg_w=m_sg_w,
              sg_b=m_sg_b, w_out=m_w_out, ffn_norm_g=m_ffn_norm_g, w_up=m_w_up, ffn_conv_w=m_ffn_conv_w,
              ffn_conv_b=m_ffn_conv_b, w_down=m_w_down, final_norm_g=m_final_norm_g)
    Vo = dict(attn_norm_g=v_attn_norm_g, w_in=v_w_in, dn_conv_w=v_dn_conv_w, dn_a_log=v_dn_a_log,
              dn_dt_bias=v_dn_dt_bias, dn_out_norm_g=v_dn_out_norm_g, sg_norm_g=v_sg_norm_g, sg_w=v_sg_w,
              sg_b=v_sg_b, w_out=v_w_out, ffn_norm_g=v_ffn_norm_g, w_up=v_w_up, ffn_conv_w=v_ffn_conv_w,
              ffn_conv_b=v_ffn_conv_b, w_down=v_w_down, final_norm_g=v_final_norm_g)
    xi, yi, ci = lax.axis_index("x"), lax.axis_index("y"), lax.axis_index("c")
    chip = 2 * xi + yi

    me_lin = 4 * xi + 2 * yi + ci

    g_in, g_dnc = _gather_first(w_in[0].astype(BF16), dn_conv_w[0])
    def start_gather(name, shards, after):
        lands = [lax.dynamic_update_index_in_dim(lax.empty((4,) + s.shape, s.dtype), s, chip, 0) for s in shards]
        return _transfer_start(name, shards, lands, 3 * len(shards), _gather_copies, after=after)

    mid = start_gather("gather_mid_start", [w_out[0].astype(BF16), w_up[0].astype(BF16), ffn_conv_w[0]], g_in)
    last = start_gather("gather_last_start", [w_down[0].astype(BF16)], mid[4])
    token = last[4]

    def late_weights(stage, after):
        if stage == "out_proj":
            _, (g_out, g_up, g_ffc) = _transfer_wait("gather_mid_wait", *mid[:4], _gather_copies, after)
            return dict(w_out=g_out.reshape(D_MODEL, D_MODEL), ffn_conv_w=g_ffc.transpose(1, 0, 2).reshape(3, 2 * D_FF),
                        w_up=g_up.transpose(1, 0, 2).reshape(D_MODEL, 2 * D_FF))
        _, (g_down,) = _transfer_wait("gather_last_wait", *last[:4], _gather_copies, after)
        return dict(w_down=g_down.reshape(D_FF, D_MODEL))

    full = dict(
        w_in=g_in,
        dn_conv_w=g_dnc.transpose(1, 0, 2).reshape(4, 3 * DN_WIDTH),
        attn_norm_g=attn_norm_g, dn_a_log=dn_a_log, dn_dt_bias=dn_dt_bias, dn_out_norm_g=dn_out_norm_g,
        sg_norm_g=sg_norm_g, sg_w=sg_w[0], sg_b=sg_b[0], ffn_norm_g=ffn_norm_g, ffn_conv_b=ffn_conv_b,
        final_norm_g=final_norm_g[None])

    pending = {}

    def on_grad(name, gw):
        if name == "small_early":
            buf = _pack_early(gw["dn_out_norm_g"], gw["sg_norm_g"], gw["sg_w"], gw["sg_bt"], gw["ffn_norm_g"],
                              gw["ffn_conv_w"], gw["ffn_conv_b"], gw["final_norm_g"])
            land = lax.dynamic_update_index_in_dim(lax.empty((8,) + buf.shape, F32), buf, me_lin, 0)
            s_sem, r_sem, src, lands, tok = _transfer_start("small_early_start", [buf], [land], 7, _small_copies)
            pending[name] = (s_sem, r_sem, src, lands)
            return tok
        g8 = gw.reshape(8, -1, gw.shape[-1])
        land = lax.empty((7,) + g8.shape[1:], BF16)
        s_sem, r_sem, src, lands, tok = _transfer_start(f"reduce_{name}_start", [g8], [land], 7, _pieces_copies)
        pending[name] = (s_sem, r_sem, src, lands)
        return tok

    loss_row, grad_x, g = _local_step(x[0], loss_target[0], full, dep=token, late_weights=late_weights,
                                      on_grad=on_grad)

    late_all = _exchange_small(_pack_late(g["attn_norm_g"], g["dn_conv_w"], g["a_dt"], loss_row))
    s_sem, r_sem, src, lands = pending["small_early"]
    _, (early_all,) = _transfer_wait("small_early_wait", s_sem, r_sem, src, lands, _small_copies, grad_x)
    row = lambda d: {k: (d[k].reshape(1, -1) if k == "final_norm_g" else d[k]) for k in SMALL}
    loss_sum, *small_out = _small_update(early_all, late_all, chip.astype(jnp.int32).reshape(1), row(W), row(Mo), row(Vo))
    loss = loss_sum[0, 0]

    def summed_half(n, after):
        s_sem, r_sem, src, lands = pending[n]
        sent, got = _transfer_wait(f"reduce_{n}_wait", s_sem, r_sem, src, lands, _pieces_copies, after)
        return _sum_pieces(f"sum_{n}", sent[0], me_lin.astype(jnp.int32).reshape(1), got[0])

    first3 = ("w_down", "w_up", "w_out")
    halves = [summed_half(n, grad_x) for n in first3]
    theirs = _pair_swap("pair_swap", halves)
    core = ci.astype(jnp.int32).reshape(1)
    grads, delta, new_m, new_v = {}, {}, {}, {}
    for n, mine_h, their_h in zip(first3, halves, theirs):
        shp = W[n].shape
        gr, d, mn, vn = _adamw_halves(f"adamw_{n}", W[n][0], mine_h, their_h, Mo[n][0], Vo[n][0], core)
        grads[n], delta[n], new_m[n], new_v[n] = gr.reshape(shp), d.reshape(shp), mn.reshape(shp), vn.reshape(shp)
    mine_h = summed_half("w_in", delta["w_out"])
    (their_h,) = _pair_swap("pair_swap_w_in", [mine_h])
    shp = w_in.shape
    to_t = lambda a: a.reshape(shp[1] // LANES, LANES, shp[2]).transpose(2, 0, 1)
    from_t = lambda a: a.transpose(1, 2, 0).reshape(shp)
    outs = _adamw_transposed("adamw_w_in", to_t(w_in), mine_h, their_h, to_t(m_w_in), to_t(v_w_in), core)
    grads["w_in"], delta["w_in"], new_m["w_in"], new_v["w_in"] = (from_t(o) for o in outs)
    for dst, src_d in zip((grads, delta, new_m, new_v), small_out):
        dst.update({k: (a.reshape(W[k].shape) if k == "final_norm_g" else a) for k, a in src_d.items()})

    return (loss, grad_x[None], *[grads[n] for n in ORDER], *[delta[n] for n in ORDER],
            *[new_m[n] for n in ORDER], *[new_v[n] for n in ORDER])
```
